```python
import math
import jax, jax.numpy as jnp
from jax import lax
import numpy as np

D_MODEL = 1024
BATCH = 32
SEQ = 2048
DEPTH = 1

FOX_HEADS = 8
FOX_HEAD_DIM = 128
FOX_WIDTH = FOX_HEADS * FOX_HEAD_DIM
FOX_BLOCK = 128
GDN_HEADS = 8
GDN_HEAD_K = 128
GDN_HEAD_V = 128
GDN_K_WIDTH = GDN_HEADS * GDN_HEAD_K
GDN_V_WIDTH = GDN_HEADS * GDN_HEAD_V
GDN_CONV = 4
GDN_CONV_CH = 2 * GDN_K_WIDTH + GDN_V_WIDTH
GDN_CHUNK = 64
D_FF = 2816
FFN_CONV = 3
EPS = 1e-6

IN_SIZES = (FOX_WIDTH, FOX_WIDTH, FOX_WIDTH, FOX_HEADS,
            GDN_K_WIDTH, GDN_K_WIDTH, GDN_V_WIDTH, GDN_HEADS, GDN_HEADS, GDN_V_WIDTH,
            D_MODEL, D_MODEL)
D_IN = sum(IN_SIZES)

kernel_name = "fox_gdn_parallel_hybrid_convffn"


def rmsnorm(x, g):
    xf = x.astype(jnp.float32)
    xf = xf * lax.rsqrt(jnp.mean(xf * xf, axis=-1, keepdims=True) + EPS)
    return (xf * g.astype(jnp.float32)).astype(x.dtype)


def l2norm(x):
    xf = x.astype(jnp.float32)
    return xf * lax.rsqrt(jnp.sum(xf * xf, axis=-1, keepdims=True) + EPS)


def causal_dwconv(x, w):
    K = w.shape[0]
    S = x.shape[1]
    xp = jnp.pad(x, ((0, 0), (K - 1, 0), (0, 0)))
    return sum(xp[:, i:i + S, :] * w[i] for i in range(K))


def split_in(h):
    idx = np.cumsum(np.array(IN_SIZES))[:-1].tolist()
    return jnp.split(h, idx, axis=-1)


def fox_attention(q, k, v, log_f):
    B, S, H, Dh = q.shape
    nb = S // FOX_BLOCK
    scale = Dh ** -0.5
    c = jnp.cumsum(log_f.astype(jnp.float32), axis=1).transpose(0, 2, 1)
    kh = k.transpose(0, 2, 1, 3)
    vh = v.transpose(0, 2, 1, 3)
    qb = q.reshape(B, nb, FOX_BLOCK, H, Dh).transpose(1, 0, 3, 2, 4)
    cb = c.reshape(B, H, nb, FOX_BLOCK).transpose(2, 0, 1, 3)
    key_pos = jnp.arange(S)

    def block(args):
        qi, ci, i = args
        s = jnp.einsum('bhqd,bhkd->bhqk', qi, kh).astype(jnp.float32) * scale
        s = s + (ci[..., :, None] - c[..., None, :])
        q_pos = i * FOX_BLOCK + jnp.arange(FOX_BLOCK)
        causal = key_pos[None, :] <= q_pos[:, None]
        p = jax.nn.softmax(jnp.where(causal, s, -jnp.inf), axis=-1)
        return jnp.einsum('bhqk,bhkd->bhqd', p.astype(vh.dtype), vh)

    o = lax.map(block, (qb, cb, jnp.arange(nb)))
    return o.transpose(1, 0, 3, 2, 4).reshape(B, S, H * Dh)


def gated_delta_rule(q, k, v, g, beta):
    B, S, H, dk = q.shape
    dv = v.shape[-1]
    C = GDN_CHUNK
    N = S // C
    f32 = jnp.float32

    def chunk4(t):
        return t.astype(f32).reshape(B, N, C, H, t.shape[-1]).transpose(0, 3, 1, 2, 4)

    def chunk3(t):
        return t.astype(f32).reshape(B, N, C, H).transpose(0, 3, 1, 2)

    qc = chunk4(q) * (dk ** -0.5)
    kc = chunk4(k)
    vc = chunk4(v)
    bc = chunk3(beta)
    gc = jnp.cumsum(chunk3(g), axis=-1)

    tri_incl = jnp.tril(jnp.ones((C, C), dtype=bool))
    tri_strict = jnp.tril(jnp.ones((C, C), dtype=bool), k=-1)
    diff = gc[..., :, None] - gc[..., None, :]
    decay = jnp.exp(jnp.where(tri_incl, diff, -jnp.inf))

    kb = kc * bc[..., None]
    A = jnp.where(tri_strict, jnp.einsum('bhncd,bhnmd->bhncm', kb, kc) * decay, 0.0)
    lhs = A + jnp.eye(C, dtype=f32)
    rhs = jnp.concatenate([vc * bc[..., None], kb * jnp.exp(gc)[..., None]], axis=-1)
    sol = lax.linalg.triangular_solve(lhs, rhs, left_side=True, lower=True, unit_diagonal=True)
    u_hat, w = sol[..., :dv], sol[..., dv:]

    attn = jnp.einsum('bhncd,bhnmd->bhncm', qc, kc) * decay
    q_dec = qc * jnp.exp(gc)[..., None]
    k_dec = kc * jnp.exp(gc[..., -1:] - gc)[..., None]
    g_last = jnp.exp(gc[..., -1])

    xs = tuple(jnp.moveaxis(t, 2, 0) for t in (u_hat, w, attn, q_dec, k_dec, g_last))

    def step(state, inp):
        u_hat_i, w_i, attn_i, q_dec_i, k_dec_i, gl_i = inp
        u = u_hat_i - jnp.einsum('bhcd,bhdv->bhcv', w_i, state)
        o = jnp.einsum('bhcd,bhdv->bhcv', q_dec_i, state) + jnp.einsum('bhcm,bhmv->bhcv', attn_i, u)
        state = state * gl_i[..., None, None] + jnp.einsum('bhcd,bhcv->bhdv', k_dec_i, u)
        return state, o

    s0 = jnp.zeros((B, H, dk, dv), f32)
    _, o = lax.scan(step, s0, xs)
    return o.transpose(1, 0, 3, 2, 4).reshape(B, S, H, dv).astype(v.dtype)


def _fwd_setup_inputs(seed: int = 0) -> dict:
    key = jax.random.key(seed)
    ks = jax.random.split(key, 20)
    L, D = DEPTH, D_MODEL
    nrm = jax.random.normal
    x = nrm(ks[0], (BATCH, SEQ, D), jnp.float32)
    norm_mix = 1.0 + 0.05 * nrm(ks[1], (L, D), jnp.float32)
    w_in = nrm(ks[2], (L, D, D_IN), jnp.float32) * D ** -0.5
    fox_f_bias = 3.0 + 0.5 * nrm(ks[3], (L, FOX_HEADS), jnp.float32)
    gdn_conv_w = nrm(ks[4], (L, GDN_CONV, GDN_CONV_CH), jnp.float32) * GDN_CONV ** -0.5
    gdn_a_log = jnp.log(jax.random.uniform(ks[5], (L, GDN_HEADS), jnp.float32, 1.0, 16.0))
    dt = jnp.exp(jax.random.uniform(ks[6], (L, GDN_HEADS), jnp.float32, math.log(1e-3), math.log(1e-1)))
    gdn_dt_bias = dt + jnp.log(-jnp.expm1(-dt))
    gdn_norm = 1.0 + 0.05 * nrm(ks[7], (L, GDN_HEAD_V), jnp.float32)
    w_branch_fox = nrm(ks[8], (L, FOX_WIDTH, D), jnp.float32) * FOX_WIDTH ** -0.5
    w_branch_gdn = nrm(ks[9], (L, GDN_V_WIDTH, D), jnp.float32) * GDN_V_WIDTH ** -0.5
    w_out = nrm(ks[10], (L, D, D), jnp.float32) * D ** -0.5
    norm_ffn = 1.0 + 0.05 * nrm(ks[11], (L, D), jnp.float32)
    w_up = nrm(ks[12], (L, D, 2 * D_FF), jnp.float32) * D ** -0.5
    ffn_conv_w = nrm(ks[13], (L, FFN_CONV, 2 * D_FF), jnp.float32) * FFN_CONV ** -0.5
    w_down = nrm(ks[14], (L, D_FF, D), jnp.float32) * D_FF ** -0.5
    norm_final = 1.0 + 0.05 * nrm(ks[15], (D,), jnp.float32)
    return {"x": x, "norm_mix": norm_mix, "w_in": w_in, "fox_f_bias": fox_f_bias,
            "gdn_conv_w": gdn_conv_w, "gdn_a_log": gdn_a_log, "gdn_dt_bias": gdn_dt_bias,
            "gdn_norm": gdn_norm, "w_branch_fox": w_branch_fox, "w_branch_gdn": w_branch_gdn,
            "w_out": w_out, "norm_ffn": norm_ffn, "w_up": w_up, "ffn_conv_w": ffn_conv_w,
            "w_down": w_down, "norm_final": norm_final}


def _fwd_reference(x, norm_mix, w_in, fox_f_bias, gdn_conv_w, gdn_a_log, gdn_dt_bias, gdn_norm,
              w_branch_fox, w_branch_gdn, w_out, norm_ffn, w_up, ffn_conv_w, w_down, norm_final):
    B, S, D = x.shape
    h = x
    for l in range(DEPTH):
        hn = rmsnorm(h, norm_mix[l])
        proj = jnp.einsum('bsd,de->bse', hn, w_in[l])
        (fq, fk, fv, ff, gq, gk, gv, ga, gb, gz, gate_fox, gate_gdn) = split_in(proj)

        log_f = jax.nn.log_sigmoid(ff.astype(jnp.float32) + fox_f_bias[l].astype(jnp.float32))
        y_fox = fox_attention(fq.reshape(B, S, FOX_HEADS, FOX_HEAD_DIM),
                              fk.reshape(B, S, FOX_HEADS, FOX_HEAD_DIM),
                              fv.reshape(B, S, FOX_HEADS, FOX_HEAD_DIM), log_f)

        qkv = jax.nn.silu(causal_dwconv(jnp.concatenate([gq, gk, gv], axis=-1), gdn_conv_w[l]))
        cq, ck, cv = jnp.split(qkv, [GDN_K_WIDTH, 2 * GDN_K_WIDTH], axis=-1)
        q = l2norm(cq.reshape(B, S, GDN_HEADS, GDN_HEAD_K))
        k = l2norm(ck.reshape(B, S, GDN_HEADS, GDN_HEAD_K))
        v = cv.reshape(B, S, GDN_HEADS, GDN_HEAD_V)
        g = -jnp.exp(gdn_a_log[l].astype(jnp.float32)) * jax.nn.softplus(
            ga.astype(jnp.float32) + gdn_dt_bias[l].astype(jnp.float32))
        beta = jax.nn.sigmoid(gb.astype(jnp.float32))
        o = gated_delta_rule(q, k, v, g, beta)
        o = rmsnorm(o, gdn_norm[l]) * jax.nn.silu(gz.reshape(B, S, GDN_HEADS, GDN_HEAD_V))
        y_gdn = o.reshape(B, S, GDN_V_WIDTH)

        y = (jax.nn.sigmoid(gate_fox) * jnp.einsum('bse,ed->bsd', y_fox, w_branch_fox[l])
             + jax.nn.sigmoid(gate_gdn) * jnp.einsum('bse,ed->bsd', y_gdn, w_branch_gdn[l]))
        h = h + jnp.einsum('bsd,de->bse', y, w_out[l])

        hn = rmsnorm(h, norm_ffn[l])
        up = causal_dwconv(jnp.einsum('bsd,df->bsf', hn, w_up[l]), ffn_conv_w[l])
        u_gate, u_val = jnp.split(up, 2, axis=-1)
        h = h + jnp.einsum('bsf,fd->bsd', jax.nn.silu(u_gate) * u_val, w_down[l])
    return rmsnorm(h, norm_final)


import jax as _jax
import jax.numpy as _jnp

TWIN_FORMAT = 'train_step'
FWD_PARAMS = ['x', 'norm_mix', 'w_in', 'fox_f_bias', 'gdn_conv_w', 'gdn_a_log', 'gdn_dt_bias', 'gdn_norm', 'w_branch_fox', 'w_branch_gdn', 'w_out', 'norm_ffn', 'w_up', 'ffn_conv_w', 'w_down', 'norm_final']
TWIN_WEIGHTS = ['norm_mix', 'w_in', 'fox_f_bias', 'gdn_conv_w', 'gdn_a_log', 'gdn_dt_bias', 'gdn_norm', 'w_branch_fox', 'w_branch_gdn', 'w_out', 'norm_ffn', 'w_up', 'ffn_conv_w', 'w_down', 'norm_final']
TWIN_DIFF_INPUT = 'x'
TWIN_INPUTS = ['x', 'norm_mix', 'w_in', 'fox_f_bias', 'gdn_conv_w', 'gdn_a_log', 'gdn_dt_bias', 'gdn_norm', 'w_branch_fox', 'w_branch_gdn', 'w_out', 'norm_ffn', 'w_up', 'ffn_conv_w', 'w_down', 'norm_final', 'loss_target', 'm_norm_mix', 'm_w_in', 'm_fox_f_bias', 'm_gdn_conv_w', 'm_gdn_a_log', 'm_gdn_dt_bias', 'm_gdn_norm', 'm_w_branch_fox', 'm_w_branch_gdn', 'm_w_out', 'm_norm_ffn', 'm_w_up', 'm_ffn_conv_w', 'm_w_down', 'm_norm_final', 'v_norm_mix', 'v_w_in', 'v_fox_f_bias', 'v_gdn_conv_w', 'v_gdn_a_log', 'v_gdn_dt_bias', 'v_gdn_norm', 'v_w_branch_fox', 'v_w_branch_gdn', 'v_w_out', 'v_norm_ffn', 'v_w_up', 'v_ffn_conv_w', 'v_w_down', 'v_norm_final']
TWIN_OUTPUTS = ['loss', 'grad_x', 'grad_norm_mix', 'grad_w_in', 'grad_fox_f_bias', 'grad_gdn_conv_w', 'grad_gdn_a_log', 'grad_gdn_dt_bias', 'grad_gdn_norm', 'grad_w_branch_fox', 'grad_w_branch_gdn', 'grad_w_out', 'grad_norm_ffn', 'grad_w_up', 'grad_ffn_conv_w', 'grad_w_down', 'grad_norm_final', 'delta_norm_mix', 'delta_w_in', 'delta_fox_f_bias', 'delta_gdn_conv_w', 'delta_gdn_a_log', 'delta_gdn_dt_bias', 'delta_gdn_norm', 'delta_w_branch_fox', 'delta_w_branch_gdn', 'delta_w_out', 'delta_norm_ffn', 'delta_w_up', 'delta_ffn_conv_w', 'delta_w_down', 'delta_norm_final', 'new_m_norm_mix', 'new_m_w_in', 'new_m_fox_f_bias', 'new_m_gdn_conv_w', 'new_m_gdn_a_log', 'new_m_gdn_dt_bias', 'new_m_gdn_norm', 'new_m_w_branch_fox', 'new_m_w_branch_gdn', 'new_m_w_out', 'new_m_norm_ffn', 'new_m_w_up', 'new_m_ffn_conv_w', 'new_m_w_down', 'new_m_norm_final', 'new_v_norm_mix', 'new_v_w_in', 'new_v_fox_f_bias', 'new_v_gdn_conv_w', 'new_v_gdn_a_log', 'new_v_gdn_dt_bias', 'new_v_gdn_norm', 'new_v_w_branch_fox', 'new_v_w_branch_gdn', 'new_v_w_out', 'new_v_norm_ffn', 'new_v_w_up', 'new_v_ffn_conv_w', 'new_v_w_down', 'new_v_norm_final']
TWIN_LEAF_KINDS = {'loss': 'loss', 'grad_x': 'grad_x', 'grad_norm_mix': 'grad_w', 'grad_w_in': 'grad_w', 'grad_fox_f_bias': 'grad_w', 'grad_gdn_conv_w': 'grad_w', 'grad_gdn_a_log': 'grad_w', 'grad_gdn_dt_bias': 'grad_w', 'grad_gdn_norm': 'grad_w', 'grad_w_branch_fox': 'grad_w', 'grad_w_branch_gdn': 'grad_w', 'grad_w_out': 'grad_w', 'grad_norm_ffn': 'grad_w', 'grad_w_up': 'grad_w', 'grad_ffn_conv_w': 'grad_w', 'grad_w_down': 'grad_w', 'grad_norm_final': 'grad_w', 'delta_norm_mix': 'delta_w', 'delta_w_in': 'delta_w', 'delta_fox_f_bias': 'delta_w', 'delta_gdn_conv_w': 'delta_w', 'delta_gdn_a_log': 'delta_w', 'delta_gdn_dt_bias': 'delta_w', 'delta_gdn_norm': 'delta_w', 'delta_w_branch_fox': 'delta_w', 'delta_w_branch_gdn': 'delta_w', 'delta_w_out': 'delta_w', 'delta_norm_ffn': 'delta_w', 'delta_w_up': 'delta_w', 'delta_ffn_conv_w': 'delta_w', 'delta_w_down': 'delta_w', 'delta_norm_final': 'delta_w', 'new_m_norm_mix': 'new_m', 'new_m_w_in': 'new_m', 'new_m_fox_f_bias': 'new_m', 'new_m_gdn_conv_w': 'new_m', 'new_m_gdn_a_log': 'new_m', 'new_m_gdn_dt_bias': 'new_m', 'new_m_gdn_norm': 'new_m', 'new_m_w_branch_fox': 'new_m', 'new_m_w_branch_gdn': 'new_m', 'new_m_w_out': 'new_m', 'new_m_norm_ffn': 'new_m', 'new_m_w_up': 'new_m', 'new_m_ffn_conv_w': 'new_m', 'new_m_w_down': 'new_m', 'new_m_norm_final': 'new_m', 'new_v_norm_mix': 'new_v', 'new_v_w_in': 'new_v', 'new_v_fox_f_bias': 'new_v', 'new_v_gdn_conv_w': 'new_v', 'new_v_gdn_a_log': 'new_v', 'new_v_gdn_dt_bias': 'new_v', 'new_v_gdn_norm': 'new_v', 'new_v_w_branch_fox': 'new_v', 'new_v_w_branch_gdn': 'new_v', 'new_v_w_out': 'new_v', 'new_v_norm_ffn': 'new_v', 'new_v_w_up': 'new_v', 'new_v_ffn_conv_w': 'new_v', 'new_v_w_down': 'new_v', 'new_v_norm_final': 'new_v'}


def _forward(args):
    return _fwd_reference(*[args[k] for k in FWD_PARAMS])


def _output_shape():
    out = _jax.eval_shape(lambda: _forward(_fwd_setup_inputs(0)))
    return out.shape, out.dtype

N_MICROBATCH = 1
ADAM_LR = 0.001
ADAM_B1 = 0.9
ADAM_B2 = 0.999
ADAM_EPS = 1e-08
ADAM_WD = 0.01
ADAM_STEP = 10
PER_EXAMPLE_BATCH_AXIS = {'x': 0, 'loss_target': 0}
SHARED_INPUTS = []
_WEIGHT_DTYPES = {'norm_mix': _jnp.float32, 'w_in': _jnp.float32, 'fox_f_bias': _jnp.float32, 'gdn_conv_w': _jnp.float32, 'gdn_a_log': _jnp.float32, 'gdn_dt_bias': _jnp.float32, 'gdn_norm': _jnp.float32, 'w_branch_fox': _jnp.float32, 'w_branch_gdn': _jnp.float32, 'w_out': _jnp.float32, 'norm_ffn': _jnp.float32, 'w_up': _jnp.float32, 'ffn_conv_w': _jnp.float32, 'w_down': _jnp.float32, 'norm_final': _jnp.float32}
MOMENT_SCALE = {'norm_mix': 1.754875e-01, 'w_in': 5.415985e-02, 'fox_f_bias': 2.875068e-01, 'gdn_conv_w': 6.802284e-02, 'gdn_a_log': 5.520462e-01, 'gdn_dt_bias': 5.591670e-01, 'gdn_norm': 3.019148e-01, 'w_branch_fox': 4.743481e-02, 'w_branch_gdn': 8.462290e-02, 'w_out': 9.513271e-02, 'norm_ffn': 2.072781e-01, 'w_up': 7.929772e-02, 'ffn_conv_w': 7.785631e-02, 'w_down': 1.321088e-01, 'norm_final': 6.393263e+01}


def _to_microbatches(a, axis):
    t = _jnp.moveaxis(a, axis, 0)
    t = t.reshape((N_MICROBATCH, t.shape[0] // N_MICROBATCH) + t.shape[1:])
    return _jnp.moveaxis(t, 1, axis + 1)


def setup_inputs(seed: int = 0) -> dict:
    inp = _fwd_setup_inputs(seed)
    key = _jax.random.fold_in(_jax.random.key(seed), 7919)
    shape, _ = _output_shape()
    out = dict(inp)
    out["loss_target"] = _jax.random.normal(_jax.random.fold_in(key, 0), shape, _jnp.float32)
    for i, name in enumerate(TWIN_WEIGHTS):
        w = inp[name].astype(_jnp.float32)
        if MOMENT_SCALE is None:
            s = _jnp.sqrt(_jnp.mean(_jnp.square(w)) + 1e-30)
        else:
            s = MOMENT_SCALE[name]
        km, kv = _jax.random.split(_jax.random.fold_in(key, i + 1))
        out[name] = w
        out["m_" + name] = s * _jax.random.normal(km, w.shape, _jnp.float32)
        out["v_" + name] = (s * s) * _jax.random.uniform(kv, w.shape, _jnp.float32, 0.5, 1.5)
    if N_MICROBATCH > 1:
        for name, axis in PER_EXAMPLE_BATCH_AXIS.items():
            out[name] = _to_microbatches(out[name], axis)
    return {'x': out['x'], 'norm_mix': out['norm_mix'], 'w_in': out['w_in'], 'fox_f_bias': out['fox_f_bias'], 'gdn_conv_w': out['gdn_conv_w'], 'gdn_a_log': out['gdn_a_log'], 'gdn_dt_bias': out['gdn_dt_bias'], 'gdn_norm': out['gdn_norm'], 'w_branch_fox': out['w_branch_fox'], 'w_branch_gdn': out['w_branch_gdn'], 'w_out': out['w_out'], 'norm_ffn': out['norm_ffn'], 'w_up': out['w_up'], 'ffn_conv_w': out['ffn_conv_w'], 'w_down': out['w_down'], 'norm_final': out['norm_final'], 'loss_target': out['loss_target'], 'm_norm_mix': out['m_norm_mix'], 'm_w_in': out['m_w_in'], 'm_fox_f_bias': out['m_fox_f_bias'], 'm_gdn_conv_w': out['m_gdn_conv_w'], 'm_gdn_a_log': out['m_gdn_a_log'], 'm_gdn_dt_bias': out['m_gdn_dt_bias'], 'm_gdn_norm': out['m_gdn_norm'], 'm_w_branch_fox': out['m_w_branch_fox'], 'm_w_branch_gdn': out['m_w_branch_gdn'], 'm_w_out': out['m_w_out'], 'm_norm_ffn': out['m_norm_ffn'], 'm_w_up': out['m_w_up'], 'm_ffn_conv_w': out['m_ffn_conv_w'], 'm_w_down': out['m_w_down'], 'm_norm_final': out['m_norm_final'], 'v_norm_mix': out['v_norm_mix'], 'v_w_in': out['v_w_in'], 'v_fox_f_bias': out['v_fox_f_bias'], 'v_gdn_conv_w': out['v_gdn_conv_w'], 'v_gdn_a_log': out['v_gdn_a_log'], 'v_gdn_dt_bias': out['v_gdn_dt_bias'], 'v_gdn_norm': out['v_gdn_norm'], 'v_w_branch_fox': out['v_w_branch_fox'], 'v_w_branch_gdn': out['v_w_branch_gdn'], 'v_w_out': out['v_w_out'], 'v_norm_ffn': out['v_norm_ffn'], 'v_w_up': out['v_w_up'], 'v_ffn_conv_w': out['v_ffn_conv_w'], 'v_w_down': out['v_w_down'], 'v_norm_final': out['v_norm_final']}


def _loss(weights, diff, rest, loss_target):
    with _jax.named_scope("forward"):
        args = {**rest, TWIN_DIFF_INPUT: diff, **{k: w.astype(_WEIGHT_DTYPES[k]) for k, w in weights.items()}}
        y = _forward(args)
    with _jax.named_scope("loss_head"):
        err = _jnp.square(y.astype(_jnp.float32) - loss_target)
        return 0.5 * _jnp.sum(_jnp.mean(err, axis=-1)) if err.ndim else 0.5 * err


def _adamw(w, g, m, v):
    m = ADAM_B1 * m + (1.0 - ADAM_B1) * g
    v = ADAM_B2 * v + (1.0 - ADAM_B2) * _jnp.square(g)
    m_hat = m / (1.0 - ADAM_B1 ** ADAM_STEP)
    v_hat = v / (1.0 - ADAM_B2 ** ADAM_STEP)
    delta = -ADAM_LR * (m_hat / (_jnp.sqrt(v_hat) + ADAM_EPS) + ADAM_WD * w)
    return delta, m, v


def reference(x, norm_mix, w_in, fox_f_bias, gdn_conv_w, gdn_a_log, gdn_dt_bias, gdn_norm, w_branch_fox, w_branch_gdn, w_out, norm_ffn, w_up, ffn_conv_w, w_down, norm_final, loss_target, m_norm_mix, m_w_in, m_fox_f_bias, m_gdn_conv_w, m_gdn_a_log, m_gdn_dt_bias, m_gdn_norm, m_w_branch_fox, m_w_branch_gdn, m_w_out, m_norm_ffn, m_w_up, m_ffn_conv_w, m_w_down, m_norm_final, v_norm_mix, v_w_in, v_fox_f_bias, v_gdn_conv_w, v_gdn_a_log, v_gdn_dt_bias, v_gdn_norm, v_w_branch_fox, v_w_branch_gdn, v_w_out, v_norm_ffn, v_w_up, v_ffn_conv_w, v_w_down, v_norm_final):
    given = dict(x=x, norm_mix=norm_mix, w_in=w_in, fox_f_bias=fox_f_bias, gdn_conv_w=gdn_conv_w, gdn_a_log=gdn_a_log, gdn_dt_bias=gdn_dt_bias, gdn_norm=gdn_norm, w_branch_fox=w_branch_fox, w_branch_gdn=w_branch_gdn, w_out=w_out, norm_ffn=norm_ffn, w_up=w_up, ffn_conv_w=ffn_conv_w, w_down=w_down, norm_final=norm_final, loss_target=loss_target, m_norm_mix=m_norm_mix, m_w_in=m_w_in, m_fox_f_bias=m_fox_f_bias, m_gdn_conv_w=m_gdn_conv_w, m_gdn_a_log=m_gdn_a_log, m_gdn_dt_bias=m_gdn_dt_bias, m_gdn_norm=m_gdn_norm, m_w_branch_fox=m_w_branch_fox, m_w_branch_gdn=m_w_branch_gdn, m_w_out=m_w_out, m_norm_ffn=m_norm_ffn, m_w_up=m_w_up, m_ffn_conv_w=m_ffn_conv_w, m_w_down=m_w_down, m_norm_final=m_norm_final, v_norm_mix=v_norm_mix, v_w_in=v_w_in, v_fox_f_bias=v_fox_f_bias, v_gdn_conv_w=v_gdn_conv_w, v_gdn_a_log=v_gdn_a_log, v_gdn_dt_bias=v_gdn_dt_bias, v_gdn_norm=v_gdn_norm, v_w_branch_fox=v_w_branch_fox, v_w_branch_gdn=v_w_branch_gdn, v_w_out=v_w_out, v_norm_ffn=v_norm_ffn, v_w_up=v_w_up, v_ffn_conv_w=v_ffn_conv_w, v_w_down=v_w_down, v_norm_final=v_norm_final)
    weights = {n: given[n] for n in TWIN_WEIGHTS}
    shared = {n: given[n] for n in SHARED_INPUTS}
    per_example = {n: given[n] for n in ['x']}
    grad_fn = _jax.value_and_grad(_loss, argnums=(0, 1))

    def one_microbatch(ex, loss_target):
        ex = dict(ex)
        diff = ex.pop(TWIN_DIFF_INPUT)
        return grad_fn(weights, diff, {**shared, **ex}, loss_target)

    if N_MICROBATCH == 1:
        loss, (grad_w, grad_x) = one_microbatch(per_example, given["loss_target"])
    else:
        def body(carry, xs):
            loss_sum, grad_sum = carry
            l_k, (gw_k, gx_k) = one_microbatch(xs[0], xs[1])
            with _jax.named_scope("update"):
                return (loss_sum + l_k, _jax.tree.map(_jnp.add, grad_sum, gw_k)), gx_k

        init = (_jnp.zeros((), _jnp.float32), _jax.tree.map(_jnp.zeros_like, weights))
        (loss, grad_w), grad_x = _jax.lax.scan(body, init, (per_example, given["loss_target"]))
    with _jax.named_scope("update"):
        delta_w, new_m, new_v = {}, {}, {}
        for n in TWIN_WEIGHTS:
            delta_w[n], new_m[n], new_v[n] = _adamw(weights[n], grad_w[n], given["m_" + n], given["v_" + n])
    return (loss, grad_x, *[grad_w[n] for n in TWIN_WEIGHTS], *[delta_w[n] for n in TWIN_WEIGHTS],
            *[new_m[n] for n in TWIN_WEIGHTS], *[new_v[n] for n in TWIN_WEIGHTS])
```

```python
import functools
import math

import jax
import jax.numpy as jnp
from jax import lax
from jax.experimental import pallas as pl
from jax.experimental.pallas import tpu as pltpu

F32 = jnp.float32
BF16 = jnp.bfloat16
HEAD = 128
CHUNK = 64
GDN_CONV = 4
FFN_CONV = 3
EPS = 1e-6
NEG = -1e30
ROW = 1024
ATT_TILE = 256
N_CHIP = 4
N_DEV = 8
MESH = pl.DeviceIdType.MESH
HI = lax.Precision.HIGHEST

ADAM_LR, ADAM_B1, ADAM_B2, ADAM_EPS, ADAM_WD, ADAM_STEP = 0.001, 0.9, 0.999, 1e-08, 0.01, 10


def _tile(n, cap, unit=128):
    best = None
    t = unit
    while t <= min(n, cap):
        if n % t == 0:
            best = t
        t += unit
    return best if best is not None else n


def _params(*sem):
    return pltpu.CompilerParams(dimension_semantics=sem)


_NN = (((1,), (0,)), ((), ()))
_NT = (((1,), (1,)), ((), ()))
_TN = (((0,), (0,)), ((), ()))


def _dg(a, b, dims, hi):
    if hi:
        return lax.dot_general(a, b, dims, precision=HI, preferred_element_type=F32)
    return lax.dot_general(a.astype(BF16), b.astype(BF16), dims, preferred_element_type=F32)


class _RawOps:
    @staticmethod
    def nn(a, b, hi=False):
        return _dg(a, b, _NN, hi)

    @staticmethod
    def nt(a, b, hi=False):
        return _dg(a, b, _NT, hi)

    @staticmethod
    def tn(a, b, hi=False):
        return _dg(a, b, _TN, hi)


def _make_diff_ops():
    def build(hi):
        @jax.custom_vjp
        def nn(a, b):
            return _dg(a, b, _NN, hi)

        nn.defvjp(lambda a, b: (_dg(a, b, _NN, hi), (a, b)),
                  lambda r, g: (_dg(g, r[1], _NT, hi), _dg(r[0], g, _TN, hi)))

        @jax.custom_vjp
        def nt(a, b):
            return _dg(a, b, _NT, hi)

        nt.defvjp(lambda a, b: (_dg(a, b, _NT, hi), (a, b)),
                  lambda r, g: (_dg(g, r[1], _NN, hi), _dg(g, r[0], _TN, hi)))

        @jax.custom_vjp
        def tn(a, b):
            return _dg(a, b, _TN, hi)

        tn.defvjp(lambda a, b: (_dg(a, b, _TN, hi), (a, b)),
                  lambda r, g: (_dg(r[1], g, _NT, hi), _dg(r[0], g, _NN, hi)))
        return nn, nt, tn

    lo, hi_ = build(False), build(True)

    class _DiffOps:
        @staticmethod
        def nn(a, b, hi=False):
            return (hi_ if hi else lo)[0](a, b)

        @staticmethod
        def nt(a, b, hi=False):
            return (hi_ if hi else lo)[1](a, b)

        @staticmethod
        def tn(a, b, hi=False):
            return (hi_ if hi else lo)[2](a, b)

    return _DiffOps


_DiffOps = _make_diff_ops()


def _sigmoid(x):
    return 1.0 / (1.0 + jnp.exp(-x))


def matmul(a, b, mode, name, add=None, out_dtype=F32, tm_cap=512, tn_cap=1024, tk_cap=1024):
    if mode == "nn":
        (M, K), (K2, N) = a.shape, b.shape
    elif mode == "nt":
        (M, K), (N, K2) = a.shape, b.shape
    else:
        (K, M), (K2, N) = a.shape, b.shape
    assert K == K2, (name, a.shape, b.shape)
    tm, tn, tk = _tile(M, tm_cap), _tile(N, tn_cap), _tile(K, tk_cap)
    nk = K // tk
    dims = {"nn": _NN, "nt": _NT, "tn": _TN}[mode]
    if mode == "tn":
        a_spec = pl.BlockSpec((tk, tm), lambda i, j, k: (k, i))
    else:
        a_spec = pl.BlockSpec((tm, tk), lambda i, j, k: (i, k))
    if mode == "nt":
        b_spec = pl.BlockSpec((tn, tk), lambda i, j, k: (j, k))
    else:
        b_spec = pl.BlockSpec((tk, tn), lambda i, j, k: (k, j))
    o_spec = pl.BlockSpec((tm, tn), lambda i, j, k: (i, j))
    has_add = add is not None

    def body(*refs):
        if has_add:
            a_ref, b_ref, add_ref, o_ref, acc_ref = refs
        else:
            a_ref, b_ref, o_ref, acc_ref = refs
        k = pl.program_id(2)

        @pl.when(k == 0)
        def _():
            acc_ref[...] = jnp.zeros_like(acc_ref)

        acc_ref[...] += lax.dot_general(a_ref[...].astype(BF16), b_ref[...].astype(BF16), dims,
                                        preferred_element_type=F32)

        @pl.when(k == nk - 1)
        def _():
            r = acc_ref[...]
            if has_add:
                r = r + add_ref[...]
            o_ref[...] = r.astype(out_dtype)

    in_specs = [a_spec, b_spec] + ([o_spec] if has_add else [])
    args = (a, b) + ((add,) if has_add else ())
    return pl.pallas_call(
        body, name=name, grid=(M // tm, N // tn, nk), in_specs=in_specs, out_specs=o_spec,
        out_shape=jax.ShapeDtypeStruct((M, N), out_dtype),
        scratch_shapes=[pltpu.VMEM((tm, tn), F32)],
        compiler_params=_params("parallel", "parallel", "arbitrary"),
    )(*args)


def rmsnorm_fwd(x, g, name):
    T, D = x.shape
    tm = _tile(T, 512, 8)

    def body(x_ref, g_ref, o_ref):
        xv = x_ref[...]
        r = lax.rsqrt(jnp.mean(xv * xv, axis=-1, keepdims=True) + EPS)
        o_ref[...] = (xv * r * g_ref[...]).astype(BF16)

    return pl.pallas_call(
        body, name=name, grid=(T // tm,),
        in_specs=[pl.BlockSpec((tm, D), lambda i: (i, 0)), pl.BlockSpec((1, D), lambda i: (0, 0))],
        out_specs=pl.BlockSpec((tm, D), lambda i: (i, 0)),
        out_shape=jax.ShapeDtypeStruct((T, D), BF16),
        compiler_params=_params("parallel"),
    )(x, g)


def rmsnorm_bwd(x, g, dy, dres, name):
    T, D = x.shape
    tm = _tile(T, 512, 8)

    def body(x_ref, g_ref, dy_ref, dres_ref, dx_ref, dg_ref):
        @pl.when(pl.program_id(0) == 0)
        def _():
            dg_ref[...] = jnp.zeros_like(dg_ref)

        xv = x_ref[...]
        r = lax.rsqrt(jnp.mean(xv * xv, axis=-1, keepdims=True) + EPS)
        xh = xv * r
        dyv = dy_ref[...]
        dg_ref[...] += jnp.sum(dyv * xh, axis=0, keepdims=True)
        dxh = dyv * g_ref[...]
        dx_ref[...] = dres_ref[...] + r * (dxh - xh * jnp.mean(dxh * xh, axis=-1, keepdims=True))

    row = pl.BlockSpec((tm, D), lambda i: (i, 0))
    vec = pl.BlockSpec((1, D), lambda i: (0, 0))
    return pl.pallas_call(
        body, name=name, grid=(T // tm,), in_specs=[row, vec, row, row], out_specs=[row, vec],
        out_shape=[jax.ShapeDtypeStruct((T, D), F32), jax.ShapeDtypeStruct((1, D), F32)],
        compiler_params=_params("arbitrary"),
    )(x, g, dy, dres)


def final_loss(h, g, target):
    T, D = h.shape
    tm = _tile(T, 512, 8)

    def body(h_ref, g_ref, t_ref, loss_ref, dh_ref, dg_ref):
        @pl.when(pl.program_id(0) == 0)
        def _():
            loss_ref[...] = jnp.zeros_like(loss_ref)
            dg_ref[...] = jnp.zeros_like(dg_ref)

        hv = h_ref[...]
        r = lax.rsqrt(jnp.mean(hv * hv, axis=-1, keepdims=True) + EPS)
        xh = hv * r
        err = xh * g_ref[...] - t_ref[...]
        loss_ref[...] += jnp.sum(err * err, axis=0, keepdims=True)
        dy = err * (1.0 / D)
        dg_ref[...] += jnp.sum(dy * xh, axis=0, keepdims=True)
        dxh = dy * g_ref[...]
        dh_ref[...] = r * (dxh - xh * jnp.mean(dxh * xh, axis=-1, keepdims=True))

    row = pl.BlockSpec((tm, D), lambda i: (i, 0))
    vec = pl.BlockSpec((1, D), lambda i: (0, 0))
    return pl.pallas_call(
        body, name="final_loss", grid=(T // tm,), in_specs=[row, vec, row], out_specs=[vec, row, vec],
        out_shape=[jax.ShapeDtypeStruct((1, D), F32), jax.ShapeDtypeStruct((T, D), F32),
                   jax.ShapeDtypeStruct((1, D), F32)],
        compiler_params=_params("arbitrary"),
    )(h, g, target)


def _shift_down(x, k):
    if k == 0:
        return x
    rows = lax.broadcasted_iota(jnp.int32, x.shape, 0)
    return jnp.where(rows >= k, pltpu.roll(x, k, 0), 0.0)


def _shift_up(x, k):
    if k == 0:
        return x
    s = x.shape[0]
    rows = lax.broadcasted_iota(jnp.int32, x.shape, 0)
    return jnp.where(rows < s - k, pltpu.roll(x, s - k, 0), 0.0)


def _conv_fwd(x, w_ref, kw):
    y = x * w_ref[kw - 1:kw, :]
    for i in range(kw - 1):
        y = y + _shift_down(x, kw - 1 - i) * w_ref[i:i + 1, :]
    return y


def _conv_bwd(x, dy, w_ref, kw):
    dx = dy * w_ref[kw - 1:kw, :]
    dws = []
    for i in range(kw - 1):
        dx = dx + _shift_up(dy, kw - 1 - i) * w_ref[i:i + 1, :]
        dws.append(jnp.sum(dy * _shift_down(x, kw - 1 - i), axis=0, keepdims=True))
    dws.append(jnp.sum(dy * x, axis=0, keepdims=True))
    return dx, dws


def ffn_gate_fwd(up_g, up_v, cw_g, cw_v, B, S):
    T, Fd = up_g.shape
    tc = _tile(Fd, 256)

    def body(g_ref, v_ref, wg_ref, wv_ref, o_ref):
        ug = _conv_fwd(g_ref[...], wg_ref, FFN_CONV)
        uv = _conv_fwd(v_ref[...], wv_ref, FFN_CONV)
        o_ref[...] = (ug * _sigmoid(ug) * uv).astype(BF16)

    blk = pl.BlockSpec((S, tc), lambda b, j: (b, j))
    wblk = pl.BlockSpec((FFN_CONV, tc), lambda b, j: (0, j))
    return pl.pallas_call(
        body, name="ffn_gate_fwd", grid=(B, Fd // tc), in_specs=[blk, blk, wblk, wblk], out_specs=blk,
        out_shape=jax.ShapeDtypeStruct((T, Fd), BF16), compiler_params=_params("parallel", "parallel"),
    )(up_g, up_v, cw_g, cw_v)


def ffn_gate_bwd(up_g, up_v, cw_g, cw_v, d_act, B, S):
    T, Fd = up_g.shape
    tc = _tile(Fd, 256)

    def body(g_ref, v_ref, wg_ref, wv_ref, da_ref, dg_ref, dv_ref, dwg_ref, dwv_ref):
        @pl.when(pl.program_id(1) == 0)
        def _():
            dwg_ref[...] = jnp.zeros_like(dwg_ref)
            dwv_ref[...] = jnp.zeros_like(dwv_ref)

        xg, xv = g_ref[...], v_ref[...]
        ug = _conv_fwd(xg, wg_ref, FFN_CONV)
        uv = _conv_fwd(xv, wv_ref, FFN_CONV)
        da = da_ref[...]
        sg = _sigmoid(ug)
        d_ug = da * uv * (sg + ug * sg * (1.0 - sg))
        d_uv = da * ug * sg
        dxg, dwg = _conv_bwd(xg, d_ug, wg_ref, FFN_CONV)
        dxv, dwv = _conv_bwd(xv, d_uv, wv_ref, FFN_CONV)
        dg_ref[...] = dxg.astype(BF16)
        dv_ref[...] = dxv.astype(BF16)
        for i in range(FFN_CONV):
            dwg_ref[i:i + 1, :] += dwg[i]
            dwv_ref[i:i + 1, :] += dwv[i]

    blk = pl.BlockSpec((S, tc), lambda j, b: (b, j))
    wblk = pl.BlockSpec((FFN_CONV, tc), lambda j, b: (0, j))
    return pl.pallas_call(
        body, name="ffn_gate_bwd", grid=(Fd // tc, B), in_specs=[blk, blk, wblk, wblk, blk],
        out_specs=[blk, blk, wblk, wblk],
        out_shape=[jax.ShapeDtypeStruct((T, Fd), BF16), jax.ShapeDtypeStruct((T, Fd), BF16),
                   jax.ShapeDtypeStruct((FFN_CONV, Fd), F32), jax.ShapeDtypeStruct((FFN_CONV, Fd), F32)],
        compiler_params=_params("parallel", "arbitrary"),
    )(up_g, up_v, cw_g, cw_v, d_act)


def merge_fwd(p_gates, bf_, bg_):
    T, D = bf_.shape
    tm = _tile(T, 512, 8)

    def body(gf_ref, gg_ref, bf_ref, bg_ref, o_ref):
        o_ref[...] = (_sigmoid(gf_ref[...]) * bf_ref[...] + _sigmoid(gg_ref[...]) * bg_ref[...]).astype(BF16)

    lo = pl.BlockSpec((tm, D), lambda i: (i, 0))
    hi = pl.BlockSpec((tm, D), lambda i: (i, 1))
    return pl.pallas_call(
        body, name="merge_fwd", grid=(T // tm,), in_specs=[lo, hi, lo, lo], out_specs=lo,
        out_shape=jax.ShapeDtypeStruct((T, D), BF16), compiler_params=_params("parallel"),
    )(p_gates, p_gates, bf_, bg_)


def merge_bwd(p_gates, bf_, bg_, dy):
    T, D = bf_.shape
    tm = _tile(T, 512, 8)

    def body(gf_ref, gg_ref, bf_ref, bg_ref, dy_ref, dbf_ref, dbg_ref, dgate_ref):
        d = dy_ref[...]
        sf, sg = _sigmoid(gf_ref[...]), _sigmoid(gg_ref[...])
        dbf_ref[...] = (d * sf).astype(BF16)
        dbg_ref[...] = (d * sg).astype(BF16)
        dgate_ref[:, 0:D] = (d * bf_ref[...] * sf * (1.0 - sf)).astype(BF16)
        dgate_ref[:, D:2 * D] = (d * bg_ref[...] * sg * (1.0 - sg)).astype(BF16)

    lo = pl.BlockSpec((tm, D), lambda i: (i, 0))
    hi = pl.BlockSpec((tm, D), lambda i: (i, 1))
    both = pl.BlockSpec((tm, 2 * D), lambda i: (i, 0))
    return pl.pallas_call(
        body, name="merge_bwd", grid=(T // tm,), in_specs=[lo, hi, lo, lo, lo], out_specs=[lo, lo, both],
        out_shape=[jax.ShapeDtypeStruct((T, D), BF16), jax.ShapeDtypeStruct((T, D), BF16),
                   jax.ShapeDtypeStruct((T, 2 * D), BF16)],
        compiler_params=_params("parallel"),
    )(p_gates, p_gates, bf_, bg_, dy)


def fox_fwd(p_fox, c_col, c_row, B, S, H):
    T = B * S
    t = _tile(S, ATT_TILE)
    nq = S // t
    scale = HEAD ** -0.5

    def body(q_ref, k_ref, v_ref, cq_ref, cr_ref, o_ref, lse_ref):
        i = pl.program_id(2)
        q = q_ref[...]
        cq = cq_ref[...]
        row = lax.broadcasted_iota(jnp.int32, (t, t), 0)
        col = lax.broadcasted_iota(jnp.int32, (t, t), 1)

        def step(j, carry):
            m, l, acc = carry
            off = pl.multiple_of(j * t, t)
            k = k_ref[pl.ds(off, t), :]
            v = v_ref[pl.ds(off, t), :]
            s = lax.dot_general(q, k, _NT, preferred_element_type=F32) * scale + (cq - cr_ref[:, pl.ds(off, t)])
            s = jnp.where((j < i) | (col <= row), s, NEG)
            m_new = jnp.maximum(m, jnp.max(s, axis=-1, keepdims=True))
            alpha = jnp.exp(m - m_new)
            p = jnp.exp(s - m_new)
            l = alpha * l + jnp.sum(p, axis=-1, keepdims=True)
            acc = alpha * acc + lax.dot_general(p.astype(BF16), v, _NN, preferred_element_type=F32)
            return m_new, l, acc

        m0 = jnp.full((t, 1), NEG, F32)
        m, l, acc = lax.fori_loop(0, i + 1, step, (m0, jnp.zeros((t, 1), F32), jnp.zeros((t, HEAD), F32)))
        o_ref[...] = acc / l
        lse_ref[...] = m + jnp.log(l)

    return pl.pallas_call(
        body, name="fox_fwd", grid=(B, H, nq),
        in_specs=[pl.BlockSpec((t, HEAD), lambda b, h, i: (b * nq + i, 3 * h)),
                  pl.BlockSpec((S, HEAD), lambda b, h, i: (b, 3 * h + 1)),
                  pl.BlockSpec((S, HEAD), lambda b, h, i: (b, 3 * h + 2)),
                  pl.BlockSpec((None, None, t, 1), lambda b, h, i: (b, h, i, 0)),
                  pl.BlockSpec((None, None, 1, S), lambda b, h, i: (b, h, 0, 0))],
        out_specs=[pl.BlockSpec((t, HEAD), lambda b, h, i: (b * nq + i, h)),
                   pl.BlockSpec((None, None, t, 1), lambda b, h, i: (b, h, i, 0))],
        out_shape=[jax.ShapeDtypeStruct((T, H * HEAD), F32), jax.ShapeDtypeStruct((B, H, S, 1), F32)],
        compiler_params=_params("parallel", "parallel", "arbitrary"),
    )(p_fox, p_fox, p_fox, c_col, c_row)


def fox_bwd(p_fox, c_col, c_row, o, lse, do, B, S, H):
    T = B * S
    t = _tile(S, ATT_TILE)
    n = S // t
    scale = HEAD ** -0.5

    def body(q_ref, k_ref, v_ref, cq_ref, cr_ref, o_ref, lse_ref, do_ref, dqkv_ref, dcq_ref, dcr_ref, dq_acc, delta_s):
        row = lax.broadcasted_iota(jnp.int32, (t, t), 0)
        col = lax.broadcasted_iota(jnp.int32, (t, t), 1)

        def prep(i, c):
            rows = pl.ds(pl.multiple_of(i * t, t), t)
            delta_s[rows, :] = jnp.sum(do_ref[rows, :] * o_ref[rows, :], axis=-1, keepdims=True)
            dq_acc[rows, :] = jnp.zeros((t, HEAD), F32)
            dcq_ref[rows, :] = jnp.zeros((t, 1), F32)
            return c

        lax.fori_loop(0, n, prep, 0)

        def kv_step(j, c):
            joff = pl.multiple_of(j * t, t)
            k = k_ref[pl.ds(joff, t), :]
            v = v_ref[pl.ds(joff, t), :]
            crj = cr_ref[:, pl.ds(joff, t)]

            def q_step(i, carry):
                dk, dv, dc = carry
                rows = pl.ds(pl.multiple_of(i * t, t), t)
                q = q_ref[rows, :]
                dob = do_ref[rows, :].astype(BF16)
                s = lax.dot_general(q, k, _NT, preferred_element_type=F32) * scale + (cq_ref[rows, :] - crj)
                s = jnp.where((i > j) | (col <= row), s, NEG)
                p = jnp.exp(s - lse_ref[rows, :])
                dp = lax.dot_general(dob, v, _NT, preferred_element_type=F32)
                ds = p * (dp - delta_s[rows, :])
                dsb = ds.astype(BF16)
                dv = dv + lax.dot_general(p.astype(BF16), dob, _TN, preferred_element_type=F32)
                dk = dk + lax.dot_general(dsb, q, _TN, preferred_element_type=F32)
                dq_acc[rows, :] += lax.dot_general(dsb, k, _NN, preferred_element_type=F32) * scale
                dc = dc + jnp.sum(ds, axis=0, keepdims=True)
                dcq_ref[rows, :] += jnp.sum(ds, axis=-1, keepdims=True)
                return dk, dv, dc

            z = jnp.zeros((t, HEAD), F32)
            dk, dv, dc = lax.fori_loop(j, n, q_step, (z, z, jnp.zeros((1, t), F32)))
            dqkv_ref[pl.ds(joff, t), HEAD:2 * HEAD] = (dk * scale).astype(BF16)
            dqkv_ref[pl.ds(joff, t), 2 * HEAD:3 * HEAD] = dv.astype(BF16)
            dcr_ref[:, pl.ds(joff, t)] = -dc
            return c

        lax.fori_loop(0, n, kv_step, 0)
        dqkv_ref[:, 0:HEAD] = dq_acc[...].astype(BF16)

    col_spec = pl.BlockSpec((None, None, S, 1), lambda b, h: (b, h, 0, 0))
    row_spec = pl.BlockSpec((None, None, 1, S), lambda b, h: (b, h, 0, 0))
    head = pl.BlockSpec((S, HEAD), lambda b, h: (b, h))
    return pl.pallas_call(
        body, name="fox_bwd", grid=(B, H),
        in_specs=[pl.BlockSpec((S, HEAD), lambda b, h: (b, 3 * h)),
                  pl.BlockSpec((S, HEAD), lambda b, h: (b, 3 * h + 1)),
                  pl.BlockSpec((S, HEAD), lambda b, h: (b, 3 * h + 2)),
                  col_spec, row_spec, head, col_spec, head],
        out_specs=[pl.BlockSpec((S, 3 * HEAD), lambda b, h: (b, h)), col_spec, row_spec],
        out_shape=[jax.ShapeDtypeStruct((T, 3 * H * HEAD), BF16), jax.ShapeDtypeStruct((B, H, S, 1), F32),
                   jax.ShapeDtypeStruct((B, H, 1, S), F32)],
        scratch_shapes=[pltpu.VMEM((S, HEAD), F32), pltpu.VMEM((S, 1), F32)],
        compiler_params=_params("parallel", "parallel"),
    )(p_fox, p_fox, p_fox, c_col, c_row, o, lse, do)


def _small_fn(x, b0, b1, H):
    S = x.shape[0]
    lane = lax.broadcasted_iota(jnp.int32, x.shape, 1)
    z = x + b0
    tail = jnp.log1p(jnp.exp(-jnp.abs(z)))
    softplus = jnp.maximum(z, 0.0) + tail
    logsig = -(jnp.maximum(-z, 0.0) + tail)
    g = -jnp.exp(b1) * softplus
    pre = jnp.where(lane < H, logsig, jnp.where(lane < 2 * H, g, 0.0))
    bl = _tile(S, 256, CHUNK)
    r = lax.broadcasted_iota(jnp.int32, (bl, bl), 0)
    c = lax.broadcasted_iota(jnp.int32, (bl, bl), 1)
    tri = (r >= c).astype(F32)
    tri_chunk = jnp.where((r >= c) & (jnp.right_shift(r, 6) == jnp.right_shift(c, 6)), 1.0, 0.0)
    carry = jnp.zeros((1, x.shape[1]), F32)
    parts = []
    for i in range(S // bl):
        blk = pre[i * bl:(i + 1) * bl, :]
        full = lax.dot_general(tri, blk, _NN, precision=HI, preferred_element_type=F32) + carry
        chunked = lax.dot_general(tri_chunk, blk, _NN, precision=HI, preferred_element_type=F32)
        parts.append(jnp.where(lane[:bl] < H, full, chunked))
        carry = carry + jnp.sum(blk, axis=0, keepdims=True)
    cum = parts[0] if len(parts) == 1 else jnp.concatenate(parts, axis=0)
    return jnp.where(lane < 2 * H, cum, jnp.where(lane < 3 * H, _sigmoid(x), 0.0))


def small_fwd(p_small, prm, B, S, H):
    T = B * S

    def body(x_ref, p_ref, o_ref):
        o_ref[...] = _small_fn(x_ref[...], p_ref[0:1, :], p_ref[1:2, :], H)

    blk = pl.BlockSpec((S, 128), lambda b: (b, 0))
    return pl.pallas_call(
        body, name="small_fwd", grid=(B,), in_specs=[blk, pl.BlockSpec((8, 128), lambda b: (0, 0))], out_specs=blk,
        out_shape=jax.ShapeDtypeStruct((T, 128), F32), compiler_params=_params("parallel"),
    )(p_small, prm)


def small_bwd(p_small, prm, d_out, B, S, H):
    T = B * S

    def body(x_ref, p_ref, d_ref, dx_ref, dp_ref):
        @pl.when(pl.program_id(0) == 0)
        def _():
            dp_ref[...] = jnp.zeros_like(dp_ref)

        _, vjp = jax.vjp(functools.partial(_small_fn, H=H), x_ref[...], p_ref[0:1, :], p_ref[1:2, :])
        dx, db0, db1 = vjp(d_ref[...])
        dx_ref[...] = dx.astype(BF16)
        dp_ref[0:1, :] += db0
        dp_ref[1:2, :] += db1

    blk = pl.BlockSpec((S, 128), lambda b: (b, 0))
    pblk = pl.BlockSpec((8, 128), lambda b: (0, 0))
    return pl.pallas_call(
        body, name="small_bwd", grid=(B,), in_specs=[blk, pblk, blk], out_specs=[blk, pblk],
        out_shape=[jax.ShapeDtypeStruct((T, 128), BF16), jax.ShapeDtypeStruct((8, 128), F32)],
        compiler_params=_params("arbitrary"),
    )(p_small, prm, d_out)


def gdn_prep_fwd(p_gqkv, cw, B, S, H):
    T = B * S

    def body(x_ref, w_ref, o_ref):
        y = _conv_fwd(x_ref[...], w_ref, GDN_CONV)
        a = y * _sigmoid(y)
        rs = lax.rsqrt(jnp.sum(a * a, axis=-1, keepdims=True) + EPS)
        is_qk = (pl.program_id(1) % 3) < 2
        o_ref[...] = a * jnp.where(is_qk, rs, 1.0)

    blk = pl.BlockSpec((S, HEAD), lambda b, n: (b, n))
    wblk = pl.BlockSpec((GDN_CONV, HEAD), lambda b, n: (0, n))
    return pl.pallas_call(
        body, name="gdn_prep_fwd", grid=(B, 3 * H), in_specs=[blk, wblk], out_specs=blk,
        out_shape=jax.ShapeDtypeStruct((T, 3 * H * HEAD), F32), compiler_params=_params("parallel", "parallel"),
    )(p_gqkv, cw)


def gdn_prep_bwd(p_gqkv, cw, d_out, B, S, H):
    T = B * S

    def body(x_ref, w_ref, d_ref, dx_ref, dw_ref):
        @pl.when(pl.program_id(1) == 0)
        def _():
            dw_ref[...] = jnp.zeros_like(dw_ref)

        x = x_ref[...]
        y = _conv_fwd(x, w_ref, GDN_CONV)
        sg = _sigmoid(y)
        a = y * sg
        rs = lax.rsqrt(jnp.sum(a * a, axis=-1, keepdims=True) + EPS)
        d = d_ref[...]
        out = a * rs
        da_qk = rs * (d - out * jnp.sum(d * out, axis=-1, keepdims=True))
        is_qk = (pl.program_id(0) % 3) < 2
        da = jnp.where(is_qk, da_qk, d)
        dy = da * (sg + y * sg * (1.0 - sg))
        dx, dws = _conv_bwd(x, dy, w_ref, GDN_CONV)
        dx_ref[...] = dx.astype(BF16)
        for i in range(GDN_CONV):
            dw_ref[i:i + 1, :] += dws[i]

    blk = pl.BlockSpec((S, HEAD), lambda n, b: (b, n))
    wblk = pl.BlockSpec((GDN_CONV, HEAD), lambda n, b: (0, n))
    return pl.pallas_call(
        body, name="gdn_prep_bwd", grid=(3 * H, B), in_specs=[blk, wblk, blk], out_specs=[blk, wblk],
        out_shape=[jax.ShapeDtypeStruct((T, 3 * H * HEAD), BF16), jax.ShapeDtypeStruct((GDN_CONV, 3 * H * HEAD), F32)],
        compiler_params=_params("parallel", "arbitrary"),
    )(p_gqkv, cw, d_out)


def _intra_fn(k, v, beta, gcc, gcr, ops):
    r = lax.broadcasted_iota(jnp.int32, (CHUNK, CHUNK), 0)
    c = lax.broadcasted_iota(jnp.int32, (CHUNK, CHUNK), 1)
    decay = jnp.exp(jnp.where(r > c, gcc - gcr, NEG))
    kb = k * beta
    a = ops.nt(kb, k) * decay
    p = -a
    tm = jnp.where(r == c, 1.0, 0.0) + p
    for _ in range(5):
        p = ops.nn(p, p, hi=True)
        tm = tm + ops.nn(tm, p, hi=True)
    u_hat = ops.nn(tm, v * beta, hi=True)
    w = ops.nn(tm, kb * jnp.exp(gcc), hi=True)
    return u_hat, w


INTRA_NB = 4


def gdn_intra_fwd(qkvn, beta5, gcc5, gcr5, B, S, H):
    T = B * S
    N = S // CHUNK
    nb = min(INTRA_NB, N)
    rows = nb * CHUNK
    ns = N // nb

    def body(k_ref, v_ref, b_ref, gc_ref, gr_ref, uh_ref, w_ref):
        for ci in range(nb):
            sl = slice(ci * CHUNK, (ci + 1) * CHUNK)
            u_hat, w = _intra_fn(k_ref[sl, :], v_ref[sl, :], b_ref[ci], gc_ref[ci], gr_ref[ci], _RawOps)
            uh_ref[sl, :] = u_hat
            w_ref[sl, :] = w

    colspec = pl.BlockSpec((None, None, nb, CHUNK, 1), lambda b, h, i: (b, h, i, 0, 0))
    rowspec = pl.BlockSpec((None, None, nb, 1, CHUNK), lambda b, h, i: (b, h, i, 0, 0))
    out = pl.BlockSpec((rows, HEAD), lambda b, h, i: (b * ns + i, h))
    return pl.pallas_call(
        body, name="gdn_intra_fwd", grid=(B, H, ns),
        in_specs=[pl.BlockSpec((rows, HEAD), lambda b, h, i: (b * ns + i, 3 * h + 1)),
                  pl.BlockSpec((rows, HEAD), lambda b, h, i: (b * ns + i, 3 * h + 2)),
                  colspec, colspec, rowspec],
        out_specs=[out, out],
        out_shape=[jax.ShapeDtypeStruct((T, H * HEAD), F32), jax.ShapeDtypeStruct((T, H * HEAD), F32)],
        compiler_params=_params("parallel", "parallel", "parallel"),
    )(qkvn, qkvn, beta5, gcc5, gcr5)


def gdn_intra_bwd(qkvn, beta5, gcc5, gcr5, d_uh, d_w, dq_in, dk_in, B, S, H):
    T = B * S
    N = S // CHUNK
    nb = min(INTRA_NB, N)
    rows = nb * CHUNK
    ns = N // nb

    def body(k_ref, v_ref, b_ref, gc_ref, gr_ref, duh_ref, dw_ref, dq_ref, dk_ref, o_ref, db_ref, dgc_ref, dgr_ref):
        for ci in range(nb):
            sl = slice(ci * CHUNK, (ci + 1) * CHUNK)
            _, vjp = jax.vjp(functools.partial(_intra_fn, ops=_DiffOps),
                             k_ref[sl, :], v_ref[sl, :], b_ref[ci], gc_ref[ci], gr_ref[ci])
            dk, dv, db, dgc, dgr = vjp((duh_ref[sl, :], dw_ref[sl, :]))
            o_ref[sl, 0:HEAD] = dq_ref[sl, :]
            o_ref[sl, HEAD:2 * HEAD] = dk + dk_ref[sl, :]
            o_ref[sl, 2 * HEAD:3 * HEAD] = dv
            db_ref[ci] = db
            dgc_ref[ci] = dgc
            dgr_ref[ci] = dgr

    colspec = pl.BlockSpec((None, None, nb, CHUNK, 1), lambda b, h, i: (b, h, i, 0, 0))
    rowspec = pl.BlockSpec((None, None, nb, 1, CHUNK), lambda b, h, i: (b, h, i, 0, 0))
    head = pl.BlockSpec((rows, HEAD), lambda b, h, i: (b * ns + i, h))
    return pl.pallas_call(
        body, name="gdn_intra_bwd", grid=(B, H, ns),
        in_specs=[pl.BlockSpec((rows, HEAD), lambda b, h, i: (b * ns + i, 3 * h + 1)),
                  pl.BlockSpec((rows, HEAD), lambda b, h, i: (b * ns + i, 3 * h + 2)),
                  colspec, colspec, rowspec, head, head, head, head],
        out_specs=[pl.BlockSpec((rows, 3 * HEAD), lambda b, h, i: (b * ns + i, h)), colspec, colspec, rowspec],
        out_shape=[jax.ShapeDtypeStruct((T, 3 * H * HEAD), F32),
                   jax.ShapeDtypeStruct((B, H, N, CHUNK, 1), F32), jax.ShapeDtypeStruct((B, H, N, CHUNK, 1), F32),
                   jax.ShapeDtypeStruct((B, H, N, 1, CHUNK), F32)],
        compiler_params=_params("parallel", "parallel", "parallel"),
    )(qkvn, qkvn, beta5, gcc5, gcr5, d_uh, d_w, dq_in, dk_in)


def _inter_fn(q, k, u_hat, w, gcc, gcr, state, ops):
    r = lax.broadcasted_iota(jnp.int32, (CHUNK, CHUNK), 0)
    c = lax.broadcasted_iota(jnp.int32, (CHUNK, CHUNK), 1)
    decay = jnp.exp(jnp.where(r >= c, gcc - gcr, NEG))
    qs = q * (HEAD ** -0.5)
    attn = ops.nt(qs, k) * decay
    last = lax.broadcasted_iota(jnp.int32, (CHUNK, 1), 0) == CHUNK - 1
    gl = jnp.sum(jnp.where(last, gcc, 0.0), axis=0, keepdims=True)
    u = u_hat - ops.nn(w, state)
    o = ops.nn(qs * jnp.exp(gcc), state) + ops.nn(attn, u)
    k_dec = k * jnp.exp(gl - gcc)
    new_state = state * jnp.exp(gl) + ops.tn(k_dec, u)
    return o, new_state


def gdn_inter_fwd(qkvn, u_hat, w, gcc5, gcr5, B, S, H):
    T = B * S
    N = S // CHUNK

    def body(q_ref, k_ref, uh_ref, w_ref, gc_ref, gr_ref, o_ref, st_ref, s_scr):
        s_scr[...] = jnp.zeros_like(s_scr)

        def step(n, c):
            rows = pl.ds(pl.multiple_of(n * CHUNK, CHUNK), CHUNK)
            st = s_scr[...]
            st_ref[n] = st
            o, new = _inter_fn(q_ref[rows, :], k_ref[rows, :], uh_ref[rows, :], w_ref[rows, :], gc_ref[n], gr_ref[n],
                               st, _RawOps)
            o_ref[rows, :] = o
            s_scr[...] = new
            return c

        lax.fori_loop(0, N, step, 0)

    colspec = pl.BlockSpec((None, None, N, CHUNK, 1), lambda b, h: (b, h, 0, 0, 0))
    rowspec = pl.BlockSpec((None, None, N, 1, CHUNK), lambda b, h: (b, h, 0, 0, 0))
    head = pl.BlockSpec((S, HEAD), lambda b, h: (b, h))
    return pl.pallas_call(
        body, name="gdn_inter_fwd", grid=(B, H),
        in_specs=[pl.BlockSpec((S, HEAD), lambda b, h: (b, 3 * h)), pl.BlockSpec((S, HEAD), lambda b, h: (b, 3 * h + 1)),
                  head, head, colspec, rowspec],
        out_specs=[head, pl.BlockSpec((None, None, N, HEAD, HEAD), lambda b, h: (b, h, 0, 0, 0))],
        out_shape=[jax.ShapeDtypeStruct((T, H * HEAD), F32), jax.ShapeDtypeStruct((B, H, N, HEAD, HEAD), F32)],
        scratch_shapes=[pltpu.VMEM((HEAD, HEAD), F32)],
        compiler_params=_params("parallel", "parallel"),
    )(qkvn, qkvn, u_hat, w, gcc5, gcr5)


def gdn_inter_bwd(qkvn, u_hat, w, gcc5, gcr5, states, d_o, B, S, H):
    T = B * S
    N = S // CHUNK

    def body(q_ref, k_ref, uh_ref, w_ref, gc_ref, gr_ref, st_ref, do_ref,
             dq_ref, dk_ref, duh_ref, dw_ref, dgc_ref, dgr_ref, ds_scr):
        ds_scr[...] = jnp.zeros_like(ds_scr)

        def step(i, c):
            n = N - 1 - i
            rows = pl.ds(pl.multiple_of(n * CHUNK, CHUNK), CHUNK)
            _, vjp = jax.vjp(functools.partial(_inter_fn, ops=_DiffOps), q_ref[rows, :], k_ref[rows, :],
                             uh_ref[rows, :], w_ref[rows, :], gc_ref[n], gr_ref[n], st_ref[n])
            dq, dk, duh, dw, dgc, dgr, ds = vjp((do_ref[rows, :], ds_scr[...]))
            dq_ref[rows, :] = dq
            dk_ref[rows, :] = dk
            duh_ref[rows, :] = duh
            dw_ref[rows, :] = dw
            dgc_ref[n] = dgc
            dgr_ref[n] = dgr
            ds_scr[...] = ds
            return c

        lax.fori_loop(0, N, step, 0)

    colspec = pl.BlockSpec((None, None, N, CHUNK, 1), lambda b, h: (b, h, 0, 0, 0))
    rowspec = pl.BlockSpec((None, None, N, 1, CHUNK), lambda b, h: (b, h, 0, 0, 0))
    head = pl.BlockSpec((S, HEAD), lambda b, h: (b, h))
    hshape = jax.ShapeDtypeStruct((T, H * HEAD), F32)
    return pl.pallas_call(
        body, name="gdn_inter_bwd", grid=(B, H),
        in_specs=[pl.BlockSpec((S, HEAD), lambda b, h: (b, 3 * h)), pl.BlockSpec((S, HEAD), lambda b, h: (b, 3 * h + 1)),
                  head, head, colspec, rowspec,
                  pl.BlockSpec((None, None, N, HEAD, HEAD), lambda b, h: (b, h, 0, 0, 0)), head],
        out_specs=[head, head, head, head, colspec, rowspec],
        out_shape=[hshape, hshape, hshape, hshape,
                   jax.ShapeDtypeStruct((B, H, N, CHUNK, 1), F32), jax.ShapeDtypeStruct((B, H, N, 1, CHUNK), F32)],
        scratch_shapes=[pltpu.VMEM((HEAD, HEAD), F32)],
        compiler_params=_params("parallel", "parallel"),
    )(qkvn, qkvn, u_hat, w, gcc5, gcr5, states, d_o)


def gdn_post_fwd(o, p_gz, g, H):
    T = o.shape[0]
    tm = _tile(T, 1024, 8)

    def body(o_ref, z_ref, g_ref, y_ref):
        ov, z = o_ref[...], z_ref[...]
        r = lax.rsqrt(jnp.mean(ov * ov, axis=-1, keepdims=True) + EPS)
        y_ref[...] = (ov * r * g_ref[...] * z * _sigmoid(z)).astype(BF16)

    blk = pl.BlockSpec((tm, HEAD), lambda i, h: (i, h))
    return pl.pallas_call(
        body, name="gdn_post_fwd", grid=(T // tm, H), in_specs=[blk, blk, pl.BlockSpec((1, HEAD), lambda i, h: (0, 0))],
        out_specs=blk, out_shape=jax.ShapeDtypeStruct((T, H * HEAD), BF16),
        compiler_params=_params("parallel", "parallel"),
    )(o, p_gz, g)


def gdn_post_bwd(o, p_gz, g, dy, H):
    T = o.shape[0]
    tm = _tile(T, 1024, 8)

    def body(o_ref, z_ref, g_ref, dy_ref, do_ref, dz_ref, dg_ref):
        @pl.when((pl.program_id(0) == 0) & (pl.program_id(1) == 0))
        def _():
            dg_ref[...] = jnp.zeros_like(dg_ref)

        ov, z, d = o_ref[...], z_ref[...], dy_ref[...]
        r = lax.rsqrt(jnp.mean(ov * ov, axis=-1, keepdims=True) + EPS)
        xh = ov * r
        sg = _sigmoid(z)
        sz = z * sg
        d_n = d * sz
        dz_ref[...] = (d * xh * g_ref[...] * (sg + z * sg * (1.0 - sg))).astype(BF16)
        dg_ref[...] += jnp.sum(d_n * xh, axis=0, keepdims=True)
        dxh = d_n * g_ref[...]
        do_ref[...] = r * (dxh - xh * jnp.mean(dxh * xh, axis=-1, keepdims=True))

    blk = pl.BlockSpec((tm, HEAD), lambda i, h: (i, h))
    vec = pl.BlockSpec((1, HEAD), lambda i, h: (0, 0))
    return pl.pallas_call(
        body, name="gdn_post_bwd", grid=(T // tm, H), in_specs=[blk, blk, vec, blk], out_specs=[blk, blk, vec],
        out_shape=[jax.ShapeDtypeStruct((T, H * HEAD), F32), jax.ShapeDtypeStruct((T, H * HEAD), BF16),
                   jax.ShapeDtypeStruct((1, HEAD), F32)],
        compiler_params=_params("arbitrary", "arbitrary"),
    )(o, p_gz, g, dy)


def adamw(w, g, m, v, name):
    shape = w.shape
    w2, g2, m2, v2 = (a.reshape(-1, shape[-1]) for a in (w, g, m, v))
    R, C = w2.shape
    tr = _tile(R, 128, 8)

    def body(w_ref, g_ref, m_ref, v_ref, d_ref, nm_ref, nv_ref):
        gv = g_ref[...]
        nm = ADAM_B1 * m_ref[...] + (1.0 - ADAM_B1) * gv
        nv = ADAM_B2 * v_ref[...] + (1.0 - ADAM_B2) * (gv * gv)
        m_hat = nm / (1.0 - ADAM_B1 ** ADAM_STEP)
        v_hat = nv / (1.0 - ADAM_B2 ** ADAM_STEP)
        d_ref[...] = -ADAM_LR * (m_hat / (jnp.sqrt(v_hat) + ADAM_EPS) + ADAM_WD * w_ref[...])
        nm_ref[...] = nm
        nv_ref[...] = nv

    blk = pl.BlockSpec((tr, C), lambda i: (i, 0))
    sh = jax.ShapeDtypeStruct((R, C), F32)
    d, nm, nv = pl.pallas_call(
        body, name=name, grid=(R // tr,), in_specs=[blk] * 4, out_specs=[blk] * 3, out_shape=[sh] * 3,
        compiler_params=_params("parallel"),
    )(w2, g2, m2, v2)
    return d.reshape(shape), nm.reshape(shape), nv.reshape(shape)


def _place():
    x, y, c = lax.axis_index("x"), lax.axis_index("y"), lax.axis_index("c")
    chips = [(1 - x, y), (x, 1 - y), (1 - x, 1 - y)]
    return x, y, c, chips


_HBM = pl.BlockSpec(memory_space=pltpu.HBM)


def allgather_weights(packed):
    _, Rh, _ = packed.shape

    def body(in_ref, out_ref, send_sems, recv_sems, local_sem):
        x, y, c, chips = _place()
        me_s = 2 * x + y
        sibling = (x, y, 1 - c)
        mine = pltpu.make_async_copy(in_ref, out_ref.at[me_s], local_sem)
        mine.start()

        def copy(k, shard, half, to, src=None):
            dst = out_ref.at[shard, half]
            return pltpu.make_async_remote_copy(src_ref=dst if src is None else src, dst_ref=dst,
                                                send_sem=send_sems.at[k], recv_sem=recv_sems.at[k],
                                                device_id=to, device_id_type=MESH)

        first = [copy(j, me_s, c, (*chip, c), src=in_ref.at[c]) for j, chip in enumerate(chips)]
        for cp in first:
            cp.start()
        passed = [copy(3 + j, 2 * chip[0] + chip[1], c, sibling) for j, chip in enumerate(chips)]
        for j, chip in enumerate(chips):
            copy(j, 2 * chip[0] + chip[1], c, (x, y, c)).wait_recv()
            passed[j].start()
        for j, chip in enumerate(chips):
            copy(3 + j, 2 * chip[0] + chip[1], 1 - c, (x, y, c)).wait_recv()
        for cp in first + passed:
            cp.wait_send()
        mine.wait()

    return pl.pallas_call(
        body, name="allgather_weights", in_specs=[_HBM], out_specs=_HBM,
        out_shape=jax.ShapeDtypeStruct((N_CHIP, 2, Rh, ROW), packed.dtype),
        scratch_shapes=[pltpu.SemaphoreType.DMA((6,)), pltpu.SemaphoreType.DMA((6,)), pltpu.SemaphoreType.DMA],
    )(packed)


def exchange_halves(g):
    _, ns, Rh, _ = g.shape

    def body(g_ref, out_ref, send_sem, recv_sem):
        x, y, c, _ = _place()
        cp = pltpu.make_async_remote_copy(src_ref=g_ref.at[1 - c], dst_ref=out_ref, send_sem=send_sem, recv_sem=recv_sem,
                                          device_id=(x, y, 1 - c), device_id_type=MESH)
        cp.start()
        cp.wait()

    return pl.pallas_call(
        body, name="exchange_halves", in_specs=[_HBM], out_specs=_HBM,
        out_shape=jax.ShapeDtypeStruct((ns, Rh, ROW), F32),
        scratch_shapes=[pltpu.SemaphoreType.DMA, pltpu.SemaphoreType.DMA],
    )(g)


def add_halves(g, got, idx):
    _, ns, Rh, _ = g.shape
    tr = _tile(Rh, 256, 16)

    def body(idx_ref, a_ref, b_ref, o32_ref, o16_ref):
        s = a_ref[...] + b_ref[...]
        o32_ref[...] = s
        o16_ref[...] = s.astype(BF16)

    blk = pl.BlockSpec((None, tr, ROW), lambda s, i, idx_ref: (s, i, 0))
    return pl.pallas_call(
        body, name="add_halves",
        grid_spec=pltpu.PrefetchScalarGridSpec(
            num_scalar_prefetch=1, grid=(ns, Rh // tr),
            in_specs=[pl.BlockSpec((None, None, tr, ROW), lambda s, i, idx_ref: (idx_ref[0], s, i, 0)), blk],
            out_specs=[blk, blk]),
        out_shape=[jax.ShapeDtypeStruct((ns, Rh, ROW), F32), jax.ShapeDtypeStruct((ns, Rh, ROW), BF16)],
        compiler_params=_params("parallel", "parallel"),
    )(idx, g, got)


def scatter_chips(b16):
    ns, Rh, _ = b16.shape

    def body(in_ref, out_ref, send_sems, recv_sems):
        x, y, c, chips = _place()
        me_s = 2 * x + y
        cps = []
        for j, chip in enumerate(chips):
            cps.append(pltpu.make_async_remote_copy(
                src_ref=in_ref.at[2 * chip[0] + chip[1]], dst_ref=out_ref.at[me_s],
                send_sem=send_sems.at[j], recv_sem=recv_sems.at[j], device_id=(*chip, c), device_id_type=MESH))
        for cp in cps:
            cp.start()
        for j, chip in enumerate(chips):
            pltpu.make_async_remote_copy(
                src_ref=in_ref.at[me_s], dst_ref=out_ref.at[2 * chip[0] + chip[1]],
                send_sem=send_sems.at[j], recv_sem=recv_sems.at[j], device_id=(x, y, c), device_id_type=MESH).wait_recv()
        for cp in cps:
            cp.wait_send()

    return pl.pallas_call(
        body, name="scatter_chips", in_specs=[_HBM], out_specs=_HBM,
        out_shape=jax.ShapeDtypeStruct((ns, Rh, ROW), BF16),
        scratch_shapes=[pltpu.SemaphoreType.DMA((3,)), pltpu.SemaphoreType.DMA((3,))],
    )(b16)


def add_chips(a32, got16, idx):
    ns, Rh, _ = a32.shape
    tr = _tile(Rh, 256, 16)

    def body(idx_ref, a_ref, r1_ref, r2_ref, r3_ref, o_ref):
        o_ref[...] = ((a_ref[...] + r1_ref[...].astype(F32)) + r2_ref[...].astype(F32)) + r3_ref[...].astype(F32)

    def slab(k):
        return pl.BlockSpec((None, tr, ROW), lambda i, idx_ref: ((idx_ref[1] + k) % ns, i, 0))

    return pl.pallas_call(
        body, name="add_chips",
        grid_spec=pltpu.PrefetchScalarGridSpec(
            num_scalar_prefetch=1, grid=(Rh // tr,), in_specs=[slab(0), slab(1), slab(2), slab(3)],
            out_specs=pl.BlockSpec((tr, ROW), lambda i, idx_ref: (i, 0))),
        out_shape=jax.ShapeDtypeStruct((Rh, ROW), F32),
        compiler_params=_params("parallel"),
    )(idx, a32, got16, got16, got16)


def share_halves(half):
    Rh, _ = half.shape

    def body(in_ref, out_ref, send_sem, recv_sem, local_sem):
        x, y, c, _ = _place()
        mine = pltpu.make_async_copy(in_ref, out_ref.at[c], local_sem)
        mine.start()
        cp = pltpu.make_async_remote_copy(src_ref=in_ref, dst_ref=out_ref.at[c], send_sem=send_sem, recv_sem=recv_sem,
                                          device_id=(x, y, 1 - c), device_id_type=MESH)
        cp.start()
        pltpu.make_async_remote_copy(src_ref=in_ref, dst_ref=out_ref.at[1 - c], send_sem=send_sem, recv_sem=recv_sem,
                                     device_id=(x, y, c), device_id_type=MESH).wait_recv()
        cp.wait_send()
        mine.wait()

    return pl.pallas_call(
        body, name="share_halves", in_specs=[_HBM], out_specs=_HBM,
        out_shape=jax.ShapeDtypeStruct((2, Rh, ROW), F32),
        scratch_shapes=[pltpu.SemaphoreType.DMA, pltpu.SemaphoreType.DMA, pltpu.SemaphoreType.DMA],
    )(half)


def allreduce_small(v):
    R, _ = v.shape

    def body(in_ref, out_ref, slots, send_sems, recv_sems):
        x, y, c, _ = _place()
        me = 4 * x + 2 * y + c
        slots[me] = in_ref[...]
        cps = []
        for k in range(1, N_DEV):
            to = (x ^ (k >> 2), y ^ ((k >> 1) & 1), c ^ (k & 1))
            cps.append(pltpu.make_async_remote_copy(src_ref=in_ref, dst_ref=slots.at[me], send_sem=send_sems.at[k - 1],
                                                    recv_sem=recv_sems.at[k - 1], device_id=to, device_id_type=MESH))
        for cp in cps:
            cp.start()
        for k in range(1, N_DEV):
            frm = 4 * (x ^ (k >> 2)) + 2 * (y ^ ((k >> 1) & 1)) + (c ^ (k & 1))
            pltpu.make_async_remote_copy(src_ref=in_ref, dst_ref=slots.at[frm], send_sem=send_sems.at[k - 1],
                                         recv_sem=recv_sems.at[k - 1], device_id=(x, y, c), device_id_type=MESH).wait_recv()
        for cp in cps:
            cp.wait_send()
        acc = slots[0]
        for d in range(1, N_DEV):
            acc = acc + slots[d]
        out_ref[...] = acc

    vm = pl.BlockSpec(memory_space=pltpu.VMEM)
    return pl.pallas_call(
        body, name="allreduce_small", in_specs=[vm], out_specs=vm, out_shape=jax.ShapeDtypeStruct((R, ROW), F32),
        scratch_shapes=[pltpu.VMEM((N_DEV, R, ROW), F32), pltpu.SemaphoreType.DMA((N_DEV - 1,)),
                        pltpu.SemaphoreType.DMA((N_DEV - 1,))],
    )(v)


def _rows_of(n):
    return -(-n // ROW)


def _pack_rows(items, total_rows, dtype):
    parts = []
    used = 0
    for a in items:
        flat = a.reshape(-1)
        r = _rows_of(flat.shape[0])
        flat = jnp.pad(flat, (0, r * ROW - flat.shape[0]))
        parts.append(flat.reshape(r, ROW))
        used += r
    if total_rows > used:
        parts.append(jnp.zeros((total_rows - used, ROW), dtype))
    return jnp.concatenate(parts, axis=0)


def _unpack_rows(buf, shapes):
    lead = buf.shape[:-2]
    out = []
    off = 0
    for shp in shapes:
        n = math.prod(shp)
        r = _rows_of(n)
        piece = buf[..., off:off + r, :].reshape(*lead, r * ROW)[..., :n].reshape(*lead, *shp)
        out.append(piece)
        off += r
    return out


def _interleave_heads(w, H):
    lead = w.shape[:-1]
    return w.reshape(*lead, 3, H, HEAD).swapaxes(-3, -2).reshape(*lead, 3 * H * HEAD)


def _deinterleave_heads(w, H):
    lead = w.shape[:-1]
    return w.reshape(*lead, H, 3, HEAD).swapaxes(-3, -2).reshape(*lead, 3 * H * HEAD)


def kernel(x, norm_mix, w_in, fox_f_bias, gdn_conv_w, gdn_a_log, gdn_dt_bias, gdn_norm, w_branch_fox, w_branch_gdn, w_out, norm_ffn, w_up, ffn_conv_w, w_down, norm_final, loss_target, m_norm_mix, m_w_in, m_fox_f_bias, m_gdn_conv_w, m_gdn_a_log, m_gdn_dt_bias, m_gdn_norm, m_w_branch_fox, m_w_branch_gdn, m_w_out, m_norm_ffn, m_w_up, m_ffn_conv_w, m_w_down, m_norm_final, v_norm_mix, v_w_in, v_fox_f_bias, v_gdn_conv_w, v_gdn_a_log, v_gdn_dt_bias, v_gdn_norm, v_w_branch_fox, v_w_branch_gdn, v_w_out, v_norm_ffn, v_w_up, v_ffn_conv_w, v_w_down, v_norm_final):
    B, S, D = x.shape
    T = B * S
    H = D // HEAD
    N = S // CHUNK
    FF = w_down.shape[1] * N_CHIP
    d_in = 9 * D + 3 * H
    assert w_in.shape[2] * N_CHIP == d_in and 3 * H <= 128

    cidx = lax.axis_index("c").astype(jnp.int32)
    sidx = (2 * lax.axis_index("x") + lax.axis_index("y")).astype(jnp.int32)
    idx = jnp.stack([cidx, sidx])

    big = [w_in[0], w_branch_fox[0], w_branch_gdn[0], w_out[0], w_up[0], w_down[0]]
    convs = [gdn_conv_w[0], ffn_conv_w[0]]
    big_shapes = [a.shape for a in big]
    conv_bits_shapes = [a.shape + (2,) for a in convs]
    conv_bits = [lax.bitcast_convert_type(a, BF16) for a in convs]
    n_rows = sum(_rows_of(math.prod(s)) for s in big_shapes + conv_bits_shapes)
    Rh = -(-n_rows // 32) * 16
    packed = _pack_rows([a.astype(BF16) for a in big] + conv_bits, 2 * Rh, BF16).reshape(2, Rh, ROW)
    gathered = allgather_weights(packed).reshape(N_CHIP, 2 * Rh, ROW)
    parts = _unpack_rows(gathered, big_shapes + conv_bits_shapes)
    cat_cols = lambda p: jnp.concatenate([p[i] for i in range(N_CHIP)], axis=-1)
    cat_rows = lambda p: p.reshape(-1, p.shape[-1])
    W_in = cat_cols(parts[0])
    W_bf, W_bg, W_out = cat_rows(parts[1]), cat_rows(parts[2]), cat_rows(parts[3])
    W_up, W_down = cat_cols(parts[4]), cat_rows(parts[5])
    gconv = cat_cols(lax.bitcast_convert_type(parts[6], F32))
    fconv = cat_cols(lax.bitcast_convert_type(parts[7], F32))

    o1, o2 = 3 * D, 3 * D + H
    o3, o4, o5, o6 = o2 + 3 * D, o2 + 3 * D + H, o2 + 3 * D + 2 * H, o2 + 4 * D + 2 * H
    W_fox = _interleave_heads(W_in[:, :o1], H)
    W_gqkv = _interleave_heads(W_in[:, o2:o3], H)
    W_gz = W_in[:, o5:o6]
    W_gates = W_in[:, o6:]
    W_small = jnp.concatenate([W_in[:, o1:o2], W_in[:, o3:o5], jnp.zeros((D, 128 - 3 * H), BF16)], axis=1)
    gconv_i = _interleave_heads(gconv, H)
    W_up_g, W_up_v = W_up[:, :FF], W_up[:, FF:]
    fconv_g, fconv_v = fconv[:, :FF], fconv[:, FF:]
    prm = jnp.zeros((8, 128), F32)
    prm = prm.at[0, 0:H].set(fox_f_bias[0]).at[0, H:2 * H].set(gdn_dt_bias[0]).at[1, H:2 * H].set(gdn_a_log[0])

    x2 = x.reshape(T, D)
    tgt = loss_target.reshape(T, D)

    hn1 = rmsnorm_fwd(x2, norm_mix, "rmsnorm_mix")
    p_fox = matmul(hn1, W_fox, "nn", "proj_fox", out_dtype=BF16)
    p_gqkv = matmul(hn1, W_gqkv, "nn", "proj_gqkv")
    p_gz = matmul(hn1, W_gz, "nn", "proj_gz")
    p_gates = matmul(hn1, W_gates, "nn", "proj_gates")
    p_small = matmul(hn1, W_small, "nn", "proj_small")

    sm = small_fwd(p_small, prm, B, S, H)
    heads = lambda a: a.reshape(B, S, H).transpose(0, 2, 1)
    c_bhs, gc_bhs, beta_bhs = heads(sm[:, 0:H]), heads(sm[:, H:2 * H]), heads(sm[:, 2 * H:3 * H])
    c_col, c_row = c_bhs[..., None], c_bhs[:, :, None, :]
    gcc5, gcr5 = gc_bhs.reshape(B, H, N, CHUNK, 1), gc_bhs.reshape(B, H, N, 1, CHUNK)
    beta5 = beta_bhs.reshape(B, H, N, CHUNK, 1)

    o_fox, lse = fox_fwd(p_fox, c_col, c_row, B, S, H)
    qkvn = gdn_prep_fwd(p_gqkv, gconv_i, B, S, H)
    u_hat, w_t = gdn_intra_fwd(qkvn, beta5, gcc5, gcr5, B, S, H)
    o_gdn, states = gdn_inter_fwd(qkvn, u_hat, w_t, gcc5, gcr5, B, S, H)
    y_gdn = gdn_post_fwd(o_gdn, p_gz, gdn_norm, H)
    bf_ = matmul(o_fox, W_bf, "nn", "branch_fox")
    bg_ = matmul(y_gdn, W_bg, "nn", "branch_gdn")
    y = merge_fwd(p_gates, bf_, bg_)
    h1 = matmul(y, W_out, "nn", "out_proj", add=x2)
    hn2 = rmsnorm_fwd(h1, norm_ffn, "rmsnorm_ffn")
    up_g = matmul(hn2, W_up_g, "nn", "up_gate")
    up_v = matmul(hn2, W_up_v, "nn", "up_val")
    act = ffn_gate_fwd(up_g, up_v, fconv_g, fconv_v, B, S)
    h2 = matmul(act, W_down, "nn", "down_proj", add=h1)
    loss_cols, dh2, d_norm_final = final_loss(h2, norm_final.reshape(1, D), tgt)
    loss = lax.psum(0.5 * jnp.sum(loss_cols) / D, ("x", "y", "c"))

    d_act = matmul(dh2, W_down, "nt", "d_act")
    dW_down = matmul(act, dh2, "tn", "dw_down")
    d_upg, d_upv, d_fconv_g, d_fconv_v = ffn_gate_bwd(up_g, up_v, fconv_g, fconv_v, d_act, B, S)
    d_hn2 = matmul(d_upg, W_up_g, "nt", "d_hn2_g")
    d_hn2 = matmul(d_upv, W_up_v, "nt", "d_hn2_v", add=d_hn2)
    dW_up = jnp.concatenate([matmul(hn2, d_upg, "tn", "dw_up_g"), matmul(hn2, d_upv, "tn", "dw_up_v")], axis=1)
    dh1, d_norm_ffn = rmsnorm_bwd(h1, norm_ffn, d_hn2, dh2, "rmsnorm_ffn_bwd")
    d_y = matmul(dh1, W_out, "nt", "d_y")
    dW_out = matmul(y, dh1, "tn", "dw_out")
    d_bf, d_bg, d_gates = merge_bwd(p_gates, bf_, bg_, d_y)
    d_ofox = matmul(d_bf, W_bf, "nt", "d_ofox")
    dW_bf = matmul(o_fox, d_bf, "tn", "dw_bf")
    d_ygdn = matmul(d_bg, W_bg, "nt", "d_ygdn")
    dW_bg = matmul(y_gdn, d_bg, "tn", "dw_bg")

    d_pfox, d_ccol, d_crow = fox_bwd(p_fox, c_col, c_row, o_fox, lse, d_ofox, B, S, H)

    d_ogdn, d_gz, d_gdn_norm = gdn_post_bwd(o_gdn, p_gz, gdn_norm, d_ygdn, H)
    dq_i, dk_i, d_uh, d_wt, dgcc_a, dgcr_a = gdn_inter_bwd(qkvn, u_hat, w_t, gcc5, gcr5, states, d_ogdn, B, S, H)
    d_qkvn, d_beta5, dgcc_b, dgcr_b = gdn_intra_bwd(qkvn, beta5, gcc5, gcr5, d_uh, d_wt, dq_i, dk_i, B, S, H)
    d_pgqkv, d_gconv_i = gdn_prep_bwd(p_gqkv, gconv_i, d_qkvn, B, S, H)

    tokens = lambda a: a.reshape(B, H, S).transpose(0, 2, 1).reshape(T, H)
    d_gc = (dgcc_a + dgcc_b).reshape(B, H, S) + (dgcr_a + dgcr_b).reshape(B, H, S)
    d_sm = jnp.concatenate([tokens(d_ccol.reshape(B, H, S) + d_crow.reshape(B, H, S)), tokens(d_gc), tokens(d_beta5.reshape(B, H, S)),
                            jnp.zeros((T, 128 - 3 * H), F32)], axis=1)
    d_psmall, d_prm = small_bwd(p_small, prm, d_sm, B, S, H)

    d_hn1 = matmul(d_pfox, W_fox, "nt", "d_hn1_fox")
    d_hn1 = matmul(d_pgqkv, W_gqkv, "nt", "d_hn1_gqkv", add=d_hn1)
    d_hn1 = matmul(d_gz, W_gz, "nt", "d_hn1_gz", add=d_hn1)
    d_hn1 = matmul(d_gates, W_gates, "nt", "d_hn1_gates", add=d_hn1)
    d_hn1 = matmul(d_psmall, W_small, "nt", "d_hn1_small", add=d_hn1)
    dW_fox = matmul(hn1, d_pfox, "tn", "dw_fox")
    dW_gqkv = matmul(hn1, d_pgqkv, "tn", "dw_gqkv")
    dW_gz = matmul(hn1, d_gz, "tn", "dw_gz")
    dW_gates = matmul(hn1, d_gates, "tn", "dw_gates")
    dW_small = matmul(hn1, d_psmall, "tn", "dw_small")
    grad_x, d_norm_mix = rmsnorm_bwd(x2, norm_mix, d_hn1, dh1, "rmsnorm_mix_bwd")

    dW_in = jnp.concatenate([_deinterleave_heads(dW_fox, H), dW_small[:, 0:H], _deinterleave_heads(dW_gqkv, H),
                             dW_small[:, H:3 * H], dW_gz, dW_gates], axis=1)
    d_gconv = _deinterleave_heads(d_gconv_i, H)
    d_fconv = jnp.concatenate([d_fconv_g, d_fconv_v], axis=1)

    col_shard = lambda g, s: g[:, s * (g.shape[1] // N_CHIP):(s + 1) * (g.shape[1] // N_CHIP)]
    row_shard = lambda g, s: g[s * (g.shape[0] // N_CHIP):(s + 1) * (g.shape[0] // N_CHIP)]
    shard_items = lambda s: [col_shard(dW_in, s), row_shard(dW_bf, s), row_shard(dW_bg, s), row_shard(dW_out, s),
                             col_shard(dW_up, s), row_shard(dW_down, s), col_shard(d_gconv, s), col_shard(d_fconv, s)]
    g_shapes = [a.shape for a in shard_items(0)]
    assert sum(_rows_of(math.prod(s)) for s in g_shapes) <= 2 * Rh
    gpack = jnp.stack([_pack_rows(shard_items(s), 2 * Rh, F32).reshape(2, Rh, ROW) for s in range(N_CHIP)], axis=1)
    got = exchange_halves(gpack)
    sum32, sum16 = add_halves(gpack, got, idx)
    got16 = scatter_chips(sum16)
    half = add_chips(sum32, got16, idx)
    full = share_halves(half).reshape(2 * Rh, ROW)
    g_w_in, g_bf, g_bg, g_out, g_up, g_down, g_gconv, g_fconv = _unpack_rows(full, g_shapes)

    small_items = [d_norm_mix, d_norm_ffn, d_norm_final, d_gdn_norm, d_prm]
    small_shapes = [a.shape for a in small_items]
    sv = allreduce_small(_pack_rows(small_items, 8, F32))
    g_norm_mix, g_norm_ffn, g_norm_final, g_gdn_norm, g_prm = _unpack_rows(sv, small_shapes)
    g_norm_final = g_norm_final.reshape(D)
    g_fbias, g_dtb, g_alog = g_prm[0:1, 0:H], g_prm[0:1, H:2 * H], g_prm[1:2, H:2 * H]

    names = ["norm_mix", "w_in", "fox_f_bias", "gdn_conv_w", "gdn_a_log", "gdn_dt_bias", "gdn_norm", "w_branch_fox",
             "w_branch_gdn", "w_out", "norm_ffn", "w_up", "ffn_conv_w", "w_down", "norm_final"]
    ws = [norm_mix, w_in, fox_f_bias, gdn_conv_w, gdn_a_log, gdn_dt_bias, gdn_norm, w_branch_fox, w_branch_gdn, w_out,
          norm_ffn, w_up, ffn_conv_w, w_down, norm_final]
    ms = [m_norm_mix, m_w_in, m_fox_f_bias, m_gdn_conv_w, m_gdn_a_log, m_gdn_dt_bias, m_gdn_norm, m_w_branch_fox,
          m_w_branch_gdn, m_w_out, m_norm_ffn, m_w_up, m_ffn_conv_w, m_w_down, m_norm_final]
    vs = [v_norm_mix, v_w_in, v_fox_f_bias, v_gdn_conv_w, v_gdn_a_log, v_gdn_dt_bias, v_gdn_norm, v_w_branch_fox,
          v_w_branch_gdn, v_w_out, v_norm_ffn, v_w_up, v_ffn_conv_w, v_w_down, v_norm_final]
    gs = [g_norm_mix, g_w_in, g_fbias, g_gconv, g_alog, g_dtb, g_gdn_norm, g_bf, g_bg, g_out, g_norm_ffn, g_up,
          g_fconv, g_down, g_norm_final]
    gs = [g.reshape(w.shape) for g, w in zip(gs, ws)]
    deltas, new_ms, new_vs = [], [], []
    for nm, w, g, m, v in zip(names, ws, gs, ms, vs):
        if w.ndim == 1:
            d, a, b = adamw(w.reshape(1, -1), g.reshape(1, -1), m.reshape(1, -1), v.reshape(1, -1), "adamw_" + nm)
            d, a, b = d.reshape(w.shape), a.reshape(w.shape), b.reshape(w.shape)
        else:
            d, a, b = adamw(w, g, m, v, "adamw_" + nm)
        deltas.append(d)
        new_ms.append(a)
        new_vs.append(b)

    return (loss, grad_x.reshape(B, S, D), *gs, *deltas, *new_ms, *new_vs)
```

```python
import functools
import math

import jax
import jax.numpy as jnp
from jax import lax
from jax.experimental import pallas as pl
from jax.experimental.pallas import tpu as pltpu

F32 = jnp.float32
BF16 = jnp.bfloat16
HEAD = 128
CHUNK = 64
GDN_CONV = 4
FFN_CONV = 3
EPS = 1e-6
NEG = -1e30
ROW = 1024
ATT_TILE = 512
N_CHIP = 4
N_DEV = 8
MESH = pl.DeviceIdType.MESH
HI = lax.Precision.HIGH
EXACT = lax.Precision.HIGHEST

ADAM_LR, ADAM_B1, ADAM_B2, ADAM_EPS, ADAM_WD, ADAM_STEP = 0.001, 0.9, 0.999, 1e-08, 0.01, 10


def _tile(n, cap, unit=128):
    best = None
    t = unit
    while t <= min(n, cap):
        if n % t == 0:
            best = t
        t += unit
    return best if best is not None else n


def _params(*sem):
    return pltpu.CompilerParams(dimension_semantics=sem)


_NN = (((1,), (0,)), ((), ()))
_NT = (((1,), (1,)), ((), ()))
_TN = (((0,), (0,)), ((), ()))


def _dg(a, b, dims, hi):
    if hi:
        return lax.dot_general(a, b, dims, precision=HI, preferred_element_type=F32)
    return lax.dot_general(a.astype(BF16), b.astype(BF16), dims, preferred_element_type=F32)


class _RawOps:
    @staticmethod
    def nn(a, b, hi=False):
        return _dg(a, b, _NN, hi)

    @staticmethod
    def nt(a, b, hi=False):
        return _dg(a, b, _NT, hi)

    @staticmethod
    def tn(a, b, hi=False):
        return _dg(a, b, _TN, hi)


def _make_diff_ops():
    def build(hi):
        @jax.custom_vjp
        def nn(a, b):
            return _dg(a, b, _NN, hi)

        nn.defvjp(lambda a, b: (_dg(a, b, _NN, hi), (a, b)),
                  lambda r, g: (_dg(g, r[1], _NT, hi), _dg(r[0], g, _TN, hi)))

        @jax.custom_vjp
        def nt(a, b):
            return _dg(a, b, _NT, hi)

        nt.defvjp(lambda a, b: (_dg(a, b, _NT, hi), (a, b)),
                  lambda r, g: (_dg(g, r[1], _NN, hi), _dg(g, r[0], _TN, hi)))

        @jax.custom_vjp
        def tn(a, b):
            return _dg(a, b, _TN, hi)

        tn.defvjp(lambda a, b: (_dg(a, b, _TN, hi), (a, b)),
                  lambda r, g: (_dg(r[1], g, _NT, hi), _dg(r[0], g, _NN, hi)))
        return nn, nt, tn

    lo, hi_ = build(False), build(True)

    class _DiffOps:
        @staticmethod
        def nn(a, b, hi=False):
            return (hi_ if hi else lo)[0](a, b)

        @staticmethod
        def nt(a, b, hi=False):
            return (hi_ if hi else lo)[1](a, b)

        @staticmethod
        def tn(a, b, hi=False):
            return (hi_ if hi else lo)[2](a, b)

    return _DiffOps


_DiffOps = _make_diff_ops()


def _sigmoid(x):
    return 1.0 / (1.0 + jnp.exp(-x))


def _mm_tile(n, pref):
    if n % pref == 0:
        return pref
    if n % 1408 == 0:
        return 1408
    return _tile(n, pref)


def matmul(a, b, mode, name, add=None, out_dtype=F32):
    if mode == "nn":
        (M, K), (K2, N) = a.shape, b.shape
    elif mode == "nt":
        (M, K), (N, K2) = a.shape, b.shape
    else:
        (K, M), (K2, N) = a.shape, b.shape
    assert K == K2, (name, a.shape, b.shape)
    tm, tn, tk = _mm_tile(M, 512), _mm_tile(N, 1024), _mm_tile(K, 1024)
    nk = K // tk
    dims = {"nn": _NN, "nt": _NT, "tn": _TN}[mode]
    if mode == "tn":
        a_spec = pl.BlockSpec((tk, tm), lambda i, j, k: (k, i))
    else:
        a_spec = pl.BlockSpec((tm, tk), lambda i, j, k: (i, k))
    if mode == "nt":
        b_spec = pl.BlockSpec((tn, tk), lambda i, j, k: (j, k))
    else:
        b_spec = pl.BlockSpec((tk, tn), lambda i, j, k: (k, j))
    o_spec = pl.BlockSpec((tm, tn), lambda i, j, k: (i, j))
    has_add = add is not None

    def body(*refs):
        if has_add:
            a_ref, b_ref, add_ref, o_ref, acc_ref = refs
        else:
            a_ref, b_ref, o_ref, acc_ref = refs
        k = pl.program_id(2)

        @pl.when(k == 0)
        def _():
            acc_ref[...] = jnp.zeros_like(acc_ref)

        acc_ref[...] += lax.dot_general(a_ref[...].astype(BF16), b_ref[...].astype(BF16), dims,
                                        preferred_element_type=F32)

        @pl.when(k == nk - 1)
        def _():
            r = acc_ref[...]
            if has_add:
                r = r + add_ref[...]
            o_ref[...] = r.astype(out_dtype)

    in_specs = [a_spec, b_spec] + ([o_spec] if has_add else [])
    args = (a, b) + ((add,) if has_add else ())
    return pl.pallas_call(
        body, name=name, grid=(M // tm, N // tn, nk), in_specs=in_specs, out_specs=o_spec,
        out_shape=jax.ShapeDtypeStruct((M, N), out_dtype),
        scratch_shapes=[pltpu.VMEM((tm, tn), F32)],
        compiler_params=_params("parallel", "parallel", "arbitrary"),
    )(*args)


def rmsnorm_fwd(x, g, name):
    T, D = x.shape
    tm = _tile(T, 512, 8)

    def body(x_ref, g_ref, o_ref):
        xv = x_ref[...]
        r = lax.rsqrt(jnp.mean(xv * xv, axis=-1, keepdims=True) + EPS)
        o_ref[...] = (xv * r * g_ref[...]).astype(BF16)

    return pl.pallas_call(
        body, name=name, grid=(T // tm,),
        in_specs=[pl.BlockSpec((tm, D), lambda i: (i, 0)), pl.BlockSpec((1, D), lambda i: (0, 0))],
        out_specs=pl.BlockSpec((tm, D), lambda i: (i, 0)),
        out_shape=jax.ShapeDtypeStruct((T, D), BF16),
        compiler_params=_params("parallel"),
    )(x, g)


def rmsnorm_bwd(x, g, dy, dres, name):
    T, D = x.shape
    tm = _tile(T, 512, 8)

    def body(x_ref, g_ref, dy_ref, dres_ref, dx_ref, dx16_ref, dg_ref):
        @pl.when(pl.program_id(0) == 0)
        def _():
            dg_ref[...] = jnp.zeros_like(dg_ref)

        xv = x_ref[...]
        r = lax.rsqrt(jnp.mean(xv * xv, axis=-1, keepdims=True) + EPS)
        xh = xv * r
        dyv = dy_ref[...]
        dg_ref[...] += jnp.sum(dyv * xh, axis=0, keepdims=True)
        dxh = dyv * g_ref[...]
        dx = dres_ref[...] + r * (dxh - xh * jnp.mean(dxh * xh, axis=-1, keepdims=True))
        dx_ref[...] = dx
        dx16_ref[...] = dx.astype(BF16)

    row = pl.BlockSpec((tm, D), lambda i: (i, 0))
    vec = pl.BlockSpec((1, D), lambda i: (0, 0))
    return pl.pallas_call(
        body, name=name, grid=(T // tm,), in_specs=[row, vec, row, row], out_specs=[row, row, vec],
        out_shape=[jax.ShapeDtypeStruct((T, D), F32), jax.ShapeDtypeStruct((T, D), BF16),
                   jax.ShapeDtypeStruct((1, D), F32)],
        compiler_params=_params("arbitrary"),
    )(x, g, dy, dres)


def final_loss(h, g, target):
    T, D = h.shape
    tm = _tile(T, 512, 8)

    def body(h_ref, g_ref, t_ref, loss_ref, dh_ref, dh16_ref, dg_ref):
        @pl.when(pl.program_id(0) == 0)
        def _():
            loss_ref[...] = jnp.zeros_like(loss_ref)
            dg_ref[...] = jnp.zeros_like(dg_ref)

        hv = h_ref[...]
        r = lax.rsqrt(jnp.mean(hv * hv, axis=-1, keepdims=True) + EPS)
        xh = hv * r
        err = xh * g_ref[...] - t_ref[...]
        loss_ref[...] += jnp.sum(err * err, axis=0, keepdims=True)
        dy = err * (1.0 / D)
        dg_ref[...] += jnp.sum(dy * xh, axis=0, keepdims=True)
        dxh = dy * g_ref[...]
        dh = r * (dxh - xh * jnp.mean(dxh * xh, axis=-1, keepdims=True))
        dh_ref[...] = dh
        dh16_ref[...] = dh.astype(BF16)

    row = pl.BlockSpec((tm, D), lambda i: (i, 0))
    vec = pl.BlockSpec((1, D), lambda i: (0, 0))
    return pl.pallas_call(
        body, name="final_loss", grid=(T // tm,), in_specs=[row, vec, row], out_specs=[vec, row, row, vec],
        out_shape=[jax.ShapeDtypeStruct((1, D), F32), jax.ShapeDtypeStruct((T, D), F32),
                   jax.ShapeDtypeStruct((T, D), BF16), jax.ShapeDtypeStruct((1, D), F32)],
        compiler_params=_params("arbitrary"),
    )(h, g, target)


def _shift_down(x, k):
    if k == 0:
        return x
    rows = lax.broadcasted_iota(jnp.int32, x.shape, 0)
    return jnp.where(rows >= k, pltpu.roll(x, k, 0), 0.0)


def _shift_up(x, k):
    if k == 0:
        return x
    s = x.shape[0]
    rows = lax.broadcasted_iota(jnp.int32, x.shape, 0)
    return jnp.where(rows < s - k, pltpu.roll(x, s - k, 0), 0.0)


def _conv_fwd(x, w_ref, kw):
    y = x * w_ref[kw - 1:kw, :]
    for i in range(kw - 1):
        y = y + _shift_down(x, kw - 1 - i) * w_ref[i:i + 1, :]
    return y


def _conv_bwd(x, dy, w_ref, kw):
    dx = dy * w_ref[kw - 1:kw, :]
    dws = []
    for i in range(kw - 1):
        dx = dx + _shift_up(dy, kw - 1 - i) * w_ref[i:i + 1, :]
        dws.append(jnp.sum(dy * _shift_down(x, kw - 1 - i), axis=0, keepdims=True))
    dws.append(jnp.sum(dy * x, axis=0, keepdims=True))
    return dx, dws


def ffn_gate_fwd(up_g, up_v, cw_g, cw_v, B, S):
    T, Fd = up_g.shape
    tc = _tile(Fd, 256)

    def body(g_ref, v_ref, wg_ref, wv_ref, o_ref):
        ug = _conv_fwd(g_ref[...], wg_ref, FFN_CONV)
        uv = _conv_fwd(v_ref[...], wv_ref, FFN_CONV)
        o_ref[...] = (ug * _sigmoid(ug) * uv).astype(BF16)

    blk = pl.BlockSpec((S, tc), lambda b, j: (b, j))
    wblk = pl.BlockSpec((FFN_CONV, tc), lambda b, j: (0, j))
    return pl.pallas_call(
        body, name="ffn_gate_fwd", grid=(B, Fd // tc), in_specs=[blk, blk, wblk, wblk], out_specs=blk,
        out_shape=jax.ShapeDtypeStruct((T, Fd), BF16), compiler_params=_params("parallel", "parallel"),
    )(up_g, up_v, cw_g, cw_v)


def ffn_gate_bwd(up_g, up_v, cw_g, cw_v, d_act, B, S):
    T, Fd = up_g.shape
    tc = _tile(Fd, 256)

    def body(g_ref, v_ref, wg_ref, wv_ref, da_ref, dg_ref, dv_ref, dwg_ref, dwv_ref):
        @pl.when(pl.program_id(1) == 0)
        def _():
            dwg_ref[...] = jnp.zeros_like(dwg_ref)
            dwv_ref[...] = jnp.zeros_like(dwv_ref)

        xg, xv = g_ref[...], v_ref[...]
        ug = _conv_fwd(xg, wg_ref, FFN_CONV)
        uv = _conv_fwd(xv, wv_ref, FFN_CONV)
        da = da_ref[...]
        sg = _sigmoid(ug)
        d_ug = da * uv * (sg + ug * sg * (1.0 - sg))
        d_uv = da * ug * sg
        dxg, dwg = _conv_bwd(xg, d_ug, wg_ref, FFN_CONV)
        dxv, dwv = _conv_bwd(xv, d_uv, wv_ref, FFN_CONV)
        dg_ref[...] = dxg.astype(BF16)
        dv_ref[...] = dxv.astype(BF16)
        for i in range(FFN_CONV):
            dwg_ref[i:i + 1, :] += dwg[i]
            dwv_ref[i:i + 1, :] += dwv[i]

    blk = pl.BlockSpec((S, tc), lambda j, b: (b, j))
    wblk = pl.BlockSpec((FFN_CONV, tc), lambda j, b: (0, j))
    return pl.pallas_call(
        body, name="ffn_gate_bwd", grid=(Fd // tc, B), in_specs=[blk, blk, wblk, wblk, blk],
        out_specs=[blk, blk, wblk, wblk],
        out_shape=[jax.ShapeDtypeStruct((T, Fd), BF16), jax.ShapeDtypeStruct((T, Fd), BF16),
                   jax.ShapeDtypeStruct((FFN_CONV, Fd), F32), jax.ShapeDtypeStruct((FFN_CONV, Fd), F32)],
        compiler_params=_params("parallel", "arbitrary"),
    )(up_g, up_v, cw_g, cw_v, d_act)


def merge_fwd(p_gates, bf_, bg_):
    T, D = bf_.shape
    tm = _tile(T, 512, 8)

    def body(gf_ref, gg_ref, bf_ref, bg_ref, o_ref):
        o_ref[...] = (_sigmoid(gf_ref[...]) * bf_ref[...] + _sigmoid(gg_ref[...]) * bg_ref[...]).astype(BF16)

    lo = pl.BlockSpec((tm, D), lambda i: (i, 0))
    hi = pl.BlockSpec((tm, D), lambda i: (i, 1))
    return pl.pallas_call(
        body, name="merge_fwd", grid=(T // tm,), in_specs=[lo, hi, lo, lo], out_specs=lo,
        out_shape=jax.ShapeDtypeStruct((T, D), BF16), compiler_params=_params("parallel"),
    )(p_gates, p_gates, bf_, bg_)


def merge_bwd(p_gates, bf_, bg_, dy):
    T, D = bf_.shape
    tm = _tile(T, 512, 8)

    def body(gf_ref, gg_ref, bf_ref, bg_ref, dy_ref, dbf_ref, dbg_ref, dgate_ref):
        d = dy_ref[...]
        sf, sg = _sigmoid(gf_ref[...]), _sigmoid(gg_ref[...])
        dbf_ref[...] = (d * sf).astype(BF16)
        dbg_ref[...] = (d * sg).astype(BF16)
        dgate_ref[:, 0:D] = (d * bf_ref[...] * sf * (1.0 - sf)).astype(BF16)
        dgate_ref[:, D:2 * D] = (d * bg_ref[...] * sg * (1.0 - sg)).astype(BF16)

    lo = pl.BlockSpec((tm, D), lambda i: (i, 0))
    hi = pl.BlockSpec((tm, D), lambda i: (i, 1))
    both = pl.BlockSpec((tm, 2 * D), lambda i: (i, 0))
    return pl.pallas_call(
        body, name="merge_bwd", grid=(T // tm,), in_specs=[lo, hi, lo, lo, lo], out_specs=[lo, lo, both],
        out_shape=[jax.ShapeDtypeStruct((T, D), BF16), jax.ShapeDtypeStruct((T, D), BF16),
                   jax.ShapeDtypeStruct((T, 2 * D), BF16)],
        compiler_params=_params("parallel"),
    )(p_gates, p_gates, bf_, bg_, dy)


def fox_fwd(p_fox, c_col, c_row, B, S, H):
    T = B * S
    t = _tile(S, ATT_TILE)
    nq = S // t
    scale = HEAD ** -0.5

    def body(q_ref, k_ref, v_ref, cq_ref, cr_ref, o_ref, o16_ref, lse_ref):
        i = pl.program_id(2)
        q = q_ref[...]
        cq = cq_ref[...]
        row = lax.broadcasted_iota(jnp.int32, (t, t), 0)
        col = lax.broadcasted_iota(jnp.int32, (t, t), 1)

        def step(j, carry):
            m, l, acc = carry
            off = pl.multiple_of(j * t, t)
            k = k_ref[pl.ds(off, t), :]
            v = v_ref[pl.ds(off, t), :]
            s = lax.dot_general(q, k, _NT, preferred_element_type=F32) * scale + (cq - cr_ref[:, pl.ds(off, t)])
            s = jnp.where((j < i) | (col <= row), s, NEG)
            m_new = jnp.maximum(m, jnp.max(s, axis=-1, keepdims=True))
            alpha = jnp.exp(m - m_new)
            p = jnp.exp(s - m_new)
            l = alpha * l + jnp.sum(p, axis=-1, keepdims=True)
            acc = alpha * acc + lax.dot_general(p.astype(BF16), v, _NN, preferred_element_type=F32)
            return m_new, l, acc

        m0 = jnp.full((t, 1), NEG, F32)
        m, l, acc = lax.fori_loop(0, i + 1, step, (m0, jnp.zeros((t, 1), F32), jnp.zeros((t, HEAD), F32)))
        o = acc / l
        o_ref[...] = o
        o16_ref[...] = o.astype(BF16)
        lse_ref[...] = m + jnp.log(l)

    return pl.pallas_call(
        body, name="fox_fwd", grid=(B, H, nq),
        in_specs=[pl.BlockSpec((t, HEAD), lambda b, h, i: (b * nq + i, 3 * h)),
                  pl.BlockSpec((S, HEAD), lambda b, h, i: (b, 3 * h + 1)),
                  pl.BlockSpec((S, HEAD), lambda b, h, i: (b, 3 * h + 2)),
                  pl.BlockSpec((None, None, t, 1), lambda b, h, i: (b, h, i, 0)),
                  pl.BlockSpec((None, None, 1, S), lambda b, h, i: (b, h, 0, 0))],
        out_specs=[pl.BlockSpec((t, HEAD), lambda b, h, i: (b * nq + i, h)),
                   pl.BlockSpec((t, HEAD), lambda b, h, i: (b * nq + i, h)),
                   pl.BlockSpec((None, None, t, 1), lambda b, h, i: (b, h, i, 0))],
        out_shape=[jax.ShapeDtypeStruct((T, H * HEAD), F32), jax.ShapeDtypeStruct((T, H * HEAD), BF16),
                   jax.ShapeDtypeStruct((B, H, S, 1), F32)],
        compiler_params=_params("parallel", "parallel", "arbitrary"),
    )(p_fox, p_fox, p_fox, c_col, c_row)


def fox_bwd(p_fox, c_col, c_row, o, lse, do, B, S, H):
    T = B * S
    t = _tile(S, ATT_TILE)
    n = S // t
    scale = HEAD ** -0.5

    def body(q_ref, k_ref, v_ref, cq_ref, cr_ref, o_ref, lse_ref, do_ref, dqkv_ref, dcq_ref, dcr_ref, dq_acc, delta_s):
        row = lax.broadcasted_iota(jnp.int32, (t, t), 0)
        col = lax.broadcasted_iota(jnp.int32, (t, t), 1)

        def prep(i, c):
            rows = pl.ds(pl.multiple_of(i * t, t), t)
            delta_s[rows, :] = jnp.sum(do_ref[rows, :] * o_ref[rows, :], axis=-1, keepdims=True)
            dq_acc[rows, :] = jnp.zeros((t, HEAD), F32)
            dcq_ref[rows, :] = jnp.zeros((t, 1), F32)
            return c

        lax.fori_loop(0, n, prep, 0)

        def kv_step(j, c):
            joff = pl.multiple_of(j * t, t)
            k = k_ref[pl.ds(joff, t), :]
            v = v_ref[pl.ds(joff, t), :]
            crj = cr_ref[:, pl.ds(joff, t)]

            def q_step(i, carry):
                dk, dv, dc = carry
                rows = pl.ds(pl.multiple_of(i * t, t), t)
                q = q_ref[rows, :]
                dob = do_ref[rows, :].astype(BF16)
                s = lax.dot_general(q, k, _NT, preferred_element_type=F32) * scale + (cq_ref[rows, :] - crj)
                s = jnp.where((i > j) | (col <= row), s, NEG)
                p = jnp.exp(s - lse_ref[rows, :])
                dp = lax.dot_general(dob, v, _NT, preferred_element_type=F32)
                ds = p * (dp - delta_s[rows, :])
                dsb = ds.astype(BF16)
                dv = dv + lax.dot_general(p.astype(BF16), dob, _TN, preferred_element_type=F32)
                dk = dk + lax.dot_general(dsb, q, _TN, preferred_element_type=F32)
                dq_acc[rows, :] += lax.dot_general(dsb, k, _NN, preferred_element_type=F32) * scale
                dc = dc + jnp.sum(ds, axis=0, keepdims=True)
                dcq_ref[rows, :] += jnp.sum(ds, axis=-1, keepdims=True)
                return dk, dv, dc

            z = jnp.zeros((t, HEAD), F32)
            dk, dv, dc = lax.fori_loop(j, n, q_step, (z, z, jnp.zeros((1, t), F32)))
            dqkv_ref[pl.ds(joff, t), HEAD:2 * HEAD] = (dk * scale).astype(BF16)
            dqkv_ref[pl.ds(joff, t), 2 * HEAD:3 * HEAD] = dv.astype(BF16)
            dcr_ref[:, pl.ds(joff, t)] = -dc
            return c

        lax.fori_loop(0, n, kv_step, 0)
        dqkv_ref[:, 0:HEAD] = dq_acc[...].astype(BF16)

    col_spec = pl.BlockSpec((None, None, S, 1), lambda b, h: (b, h, 0, 0))
    row_spec = pl.BlockSpec((None, None, 1, S), lambda b, h: (b, h, 0, 0))
    head = pl.BlockSpec((S, HEAD), lambda b, h: (b, h))
    return pl.pallas_call(
        body, name="fox_bwd", grid=(B, H),
        in_specs=[pl.BlockSpec((S, HEAD), lambda b, h: (b, 3 * h)),
                  pl.BlockSpec((S, HEAD), lambda b, h: (b, 3 * h + 1)),
                  pl.BlockSpec((S, HEAD), lambda b, h: (b, 3 * h + 2)),
                  col_spec, row_spec, head, col_spec, head],
        out_specs=[pl.BlockSpec((S, 3 * HEAD), lambda b, h: (b, h)), col_spec, row_spec],
        out_shape=[jax.ShapeDtypeStruct((T, 3 * H * HEAD), BF16), jax.ShapeDtypeStruct((B, H, S, 1), F32),
                   jax.ShapeDtypeStruct((B, H, 1, S), F32)],
        scratch_shapes=[pltpu.VMEM((S, HEAD), F32), pltpu.VMEM((S, 1), F32)],
        compiler_params=_params("parallel", "parallel"),
    )(p_fox, p_fox, p_fox, c_col, c_row, o, lse, do)


def _small_fn(x, b0, b1, H):
    S = x.shape[0]
    lane = lax.broadcasted_iota(jnp.int32, x.shape, 1)
    z = x + b0
    tail = jnp.log1p(jnp.exp(-jnp.abs(z)))
    softplus = jnp.maximum(z, 0.0) + tail
    logsig = -(jnp.maximum(-z, 0.0) + tail)
    g = -jnp.exp(b1) * softplus
    pre = jnp.where(lane < H, logsig, jnp.where(lane < 2 * H, g, 0.0))
    bl = _tile(S, 256, CHUNK)
    r = lax.broadcasted_iota(jnp.int32, (bl, bl), 0)
    c = lax.broadcasted_iota(jnp.int32, (bl, bl), 1)
    tri = (r >= c).astype(F32)
    tri_chunk = jnp.where((r >= c) & (jnp.right_shift(r, 6) == jnp.right_shift(c, 6)), 1.0, 0.0)
    carry = jnp.zeros((1, x.shape[1]), F32)
    parts = []
    for i in range(S // bl):
        blk = pre[i * bl:(i + 1) * bl, :]
        full = lax.dot_general(tri, blk, _NN, precision=EXACT, preferred_element_type=F32) + carry
        chunked = lax.dot_general(tri_chunk, blk, _NN, precision=EXACT, preferred_element_type=F32)
        parts.append(jnp.where(lane[:bl] < H, full, chunked))
        carry = carry + jnp.sum(blk, axis=0, keepdims=True)
    cum = parts[0] if len(parts) == 1 else jnp.concatenate(parts, axis=0)
    return jnp.where(lane < 2 * H, cum, jnp.where(lane < 3 * H, _sigmoid(x), 0.0))


def small_fwd(p_small, prm, B, S, H):
    T = B * S

    def body(x_ref, p_ref, o_ref):
        o_ref[...] = _small_fn(x_ref[...], p_ref[0:1, :], p_ref[1:2, :], H)

    blk = pl.BlockSpec((S, 128), lambda b: (b, 0))
    return pl.pallas_call(
        body, name="small_fwd", grid=(B,), in_specs=[blk, pl.BlockSpec((8, 128), lambda b: (0, 0))], out_specs=blk,
        out_shape=jax.ShapeDtypeStruct((T, 128), F32), compiler_params=_params("parallel"),
    )(p_small, prm)


def small_bwd(p_small, prm, d_out, B, S, H):
    T = B * S

    def body(x_ref, p_ref, d_ref, dx_ref, dp_ref):
        @pl.when(pl.program_id(0) == 0)
        def _():
            dp_ref[...] = jnp.zeros_like(dp_ref)

        _, vjp = jax.vjp(functools.partial(_small_fn, H=H), x_ref[...], p_ref[0:1, :], p_ref[1:2, :])
        dx, db0, db1 = vjp(d_ref[...])
        dx_ref[...] = dx.astype(BF16)
        dp_ref[0:1, :] += db0
        dp_ref[1:2, :] += db1

    blk = pl.BlockSpec((S, 128), lambda b: (b, 0))
    pblk = pl.BlockSpec((8, 128), lambda b: (0, 0))
    return pl.pallas_call(
        body, name="small_bwd", grid=(B,), in_specs=[blk, pblk, blk], out_specs=[blk, pblk],
        out_shape=[jax.ShapeDtypeStruct((T, 128), BF16), jax.ShapeDtypeStruct((8, 128), F32)],
        compiler_params=_params("arbitrary"),
    )(p_small, prm, d_out)


def gdn_prep_fwd(p_gqkv, cw, B, S, H):
    T = B * S

    def body(x_ref, w_ref, o_ref):
        y = _conv_fwd(x_ref[...], w_ref, GDN_CONV)
        a = y * _sigmoid(y)
        rs = lax.rsqrt(jnp.sum(a * a, axis=-1, keepdims=True) + EPS)
        is_qk = (pl.program_id(1) % 3) < 2
        o_ref[...] = a * jnp.where(is_qk, rs, 1.0)

    blk = pl.BlockSpec((S, HEAD), lambda b, n: (b, n))
    wblk = pl.BlockSpec((GDN_CONV, HEAD), lambda b, n: (0, n))
    return pl.pallas_call(
        body, name="gdn_prep_fwd", grid=(B, 3 * H), in_specs=[blk, wblk], out_specs=blk,
        out_shape=jax.ShapeDtypeStruct((T, 3 * H * HEAD), F32), compiler_params=_params("parallel", "parallel"),
    )(p_gqkv, cw)


def gdn_prep_bwd(p_gqkv, cw, d_out, B, S, H):
    T = B * S

    def body(x_ref, w_ref, d_ref, dx_ref, dw_ref):
        @pl.when(pl.program_id(1) == 0)
        def _():
            dw_ref[...] = jnp.zeros_like(dw_ref)

        x = x_ref[...]
        y = _conv_fwd(x, w_ref, GDN_CONV)
        sg = _sigmoid(y)
        a = y * sg
        rs = lax.rsqrt(jnp.sum(a * a, axis=-1, keepdims=True) + EPS)
        d = d_ref[...]
        out = a * rs
        da_qk = rs * (d - out * jnp.sum(d * out, axis=-1, keepdims=True))
        is_qk = (pl.program_id(0) % 3) < 2
        da = jnp.where(is_qk, da_qk, d)
        dy = da * (sg + y * sg * (1.0 - sg))
        dx, dws = _conv_bwd(x, dy, w_ref, GDN_CONV)
        dx_ref[...] = dx.astype(BF16)
        for i in range(GDN_CONV):
            dw_ref[i:i + 1, :] += dws[i]

    blk = pl.BlockSpec((S, HEAD), lambda n, b: (b, n))
    wblk = pl.BlockSpec((GDN_CONV, HEAD), lambda n, b: (0, n))
    return pl.pallas_call(
        body, name="gdn_prep_bwd", grid=(3 * H, B), in_specs=[blk, wblk, blk], out_specs=[blk, wblk],
        out_shape=[jax.ShapeDtypeStruct((T, 3 * H * HEAD), BF16), jax.ShapeDtypeStruct((GDN_CONV, 3 * H * HEAD), F32)],
        compiler_params=_params("parallel", "arbitrary"),
    )(p_gqkv, cw, d_out)


@jax.custom_vjp
def _given_inverse(a, t):
    return t


def _given_inverse_fwd(a, t):
    return t, t


def _given_inverse_bwd(t, g):
    x = _dg(t, g, _TN, True)
    return -_dg(x, t, _NT, True), jnp.zeros_like(t)


_given_inverse.defvjp(_given_inverse_fwd, _given_inverse_bwd)


def _intra_fn(k, v, beta, gcc, gcr, ops, t_known=None):
    r = lax.broadcasted_iota(jnp.int32, (CHUNK, CHUNK), 0)
    c = lax.broadcasted_iota(jnp.int32, (CHUNK, CHUNK), 1)
    decay = jnp.exp(jnp.where(r > c, gcc - gcr, NEG))
    kb = k * beta
    a = ops.nt(kb, k) * decay
    if t_known is None:
        p = -a
        tm = jnp.where(r == c, 1.0, 0.0) + p
        for _ in range(5):
            p = ops.nn(p, p, hi=True)
            tm = tm + ops.nn(tm, p, hi=True)
    else:
        tm = _given_inverse(a, t_known)
    u_hat = ops.nn(tm, v * beta, hi=True)
    w = ops.nn(tm, kb * jnp.exp(gcc), hi=True)
    return u_hat, w, tm


INTRA_NB = 8


def gdn_intra_fwd(qkvn, beta5, gcc5, gcr5, B, S, H):
    T = B * S
    N = S // CHUNK
    nb = min(INTRA_NB, N)
    rows = nb * CHUNK
    ns = N // nb

    def body(k_ref, v_ref, b_ref, gc_ref, gr_ref, uh_ref, w_ref, t_ref):
        for ci in range(nb):
            sl = slice(ci * CHUNK, (ci + 1) * CHUNK)
            u_hat, w, tm = _intra_fn(k_ref[sl, :], v_ref[sl, :], b_ref[ci], gc_ref[ci], gr_ref[ci], _RawOps)
            uh_ref[sl, :] = u_hat
            w_ref[sl, :] = w
            t_ref[ci] = tm

    colspec = pl.BlockSpec((None, None, nb, CHUNK, 1), lambda b, h, i: (b, h, i, 0, 0))
    rowspec = pl.BlockSpec((None, None, nb, 1, CHUNK), lambda b, h, i: (b, h, i, 0, 0))
    sqspec = pl.BlockSpec((None, None, nb, CHUNK, CHUNK), lambda b, h, i: (b, h, i, 0, 0))
    out = pl.BlockSpec((rows, HEAD), lambda b, h, i: (b * ns + i, h))
    return pl.pallas_call(
        body, name="gdn_intra_fwd", grid=(B, H, ns),
        in_specs=[pl.BlockSpec((rows, HEAD), lambda b, h, i: (b * ns + i, 3 * h + 1)),
                  pl.BlockSpec((rows, HEAD), lambda b, h, i: (b * ns + i, 3 * h + 2)),
                  colspec, colspec, rowspec],
        out_specs=[out, out, sqspec],
        out_shape=[jax.ShapeDtypeStruct((T, H * HEAD), F32), jax.ShapeDtypeStruct((T, H * HEAD), F32),
                   jax.ShapeDtypeStruct((B, H, N, CHUNK, CHUNK), F32)],
        compiler_params=_params("parallel", "parallel", "parallel"),
    )(qkvn, qkvn, beta5, gcc5, gcr5)


def gdn_intra_bwd(qkvn, beta5, gcc5, gcr5, t_inv, d_uh, d_w, dq_in, dk_in, B, S, H):
    T = B * S
    N = S // CHUNK
    nb = min(INTRA_NB, N)
    rows = nb * CHUNK
    ns = N // nb

    def body(k_ref, v_ref, b_ref, gc_ref, gr_ref, t_ref, duh_ref, dw_ref, dq_ref, dk_ref,
             o_ref, db_ref, dgc_ref, dgr_ref):
        for ci in range(nb):
            sl = slice(ci * CHUNK, (ci + 1) * CHUNK)
            _, vjp = jax.vjp(functools.partial(_intra_fn, ops=_DiffOps, t_known=t_ref[ci]),
                             k_ref[sl, :], v_ref[sl, :], b_ref[ci], gc_ref[ci], gr_ref[ci])
            dk, dv, db, dgc, dgr = vjp((duh_ref[sl, :], dw_ref[sl, :], jnp.zeros((CHUNK, CHUNK), F32)))
            o_ref[sl, 0:HEAD] = dq_ref[sl, :]
            o_ref[sl, HEAD:2 * HEAD] = dk + dk_ref[sl, :]
            o_ref[sl, 2 * HEAD:3 * HEAD] = dv
            db_ref[ci] = db
            dgc_ref[ci] = dgc
            dgr_ref[ci] = dgr

    colspec = pl.BlockSpec((None, None, nb, CHUNK, 1), lambda b, h, i: (b, h, i, 0, 0))
    rowspec = pl.BlockSpec((None, None, nb, 1, CHUNK), lambda b, h, i: (b, h, i, 0, 0))
    sqspec = pl.BlockSpec((None, None, nb, CHUNK, CHUNK), lambda b, h, i: (b, h, i, 0, 0))
    head = pl.BlockSpec((rows, HEAD), lambda b, h, i: (b * ns + i, h))
    return pl.pallas_call(
        body, name="gdn_intra_bwd", grid=(B, H, ns),
        in_specs=[pl.BlockSpec((rows, HEAD), lambda b, h, i: (b * ns + i, 3 * h + 1)),
                  pl.BlockSpec((rows, HEAD), lambda b, h, i: (b * ns + i, 3 * h + 2)),
                  colspec, colspec, rowspec, sqspec, head, head, head, head],
        out_specs=[pl.BlockSpec((rows, 3 * HEAD), lambda b, h, i: (b * ns + i, h)), colspec, colspec, rowspec],
        out_shape=[jax.ShapeDtypeStruct((T, 3 * H * HEAD), F32),
                   jax.ShapeDtypeStruct((B, H, N, CHUNK, 1), F32), jax.ShapeDtypeStruct((B, H, N, CHUNK, 1), F32),
                   jax.ShapeDtypeStruct((B, H, N, 1, CHUNK), F32)],
        compiler_params=_params("parallel", "parallel", "parallel"),
    )(qkvn, qkvn, beta5, gcc5, gcr5, t_inv, d_uh, d_w, dq_in, dk_in)


def _inter_fn(q, k, u_hat, w, gcc, gcr, state, ops):
    r = lax.broadcasted_iota(jnp.int32, (CHUNK, CHUNK), 0)
    c = lax.broadcasted_iota(jnp.int32, (CHUNK, CHUNK), 1)
    decay = jnp.exp(jnp.where(r >= c, gcc - gcr, NEG))
    qs = q * (HEAD ** -0.5)
    attn = ops.nt(qs, k) * decay
    last = lax.broadcasted_iota(jnp.int32, (CHUNK, 1), 0) == CHUNK - 1
    gl = jnp.sum(jnp.where(last, gcc, 0.0), axis=0, keepdims=True)
    u = u_hat - ops.nn(w, state)
    o = ops.nn(qs * jnp.exp(gcc), state) + ops.nn(attn, u)
    k_dec = k * jnp.exp(gl - gcc)
    new_state = state * jnp.exp(gl) + ops.tn(k_dec, u)
    return o, new_state


def gdn_inter_fwd(qkvn, u_hat, w, gcc5, gcr5, B, S, H):
    T = B * S
    N = S // CHUNK

    def body(q_ref, k_ref, uh_ref, w_ref, gc_ref, gr_ref, o_ref, st_ref, s_scr):
        s_scr[...] = jnp.zeros_like(s_scr)

        def step(n, c):
            rows = pl.ds(pl.multiple_of(n * CHUNK, CHUNK), CHUNK)
            st = s_scr[...]
            st_ref[n] = st
            o, new = _inter_fn(q_ref[rows, :], k_ref[rows, :], uh_ref[rows, :], w_ref[rows, :], gc_ref[n], gr_ref[n],
                               st, _RawOps)
            o_ref[rows, :] = o
            s_scr[...] = new
            return c

        lax.fori_loop(0, N, step, 0)

    colspec = pl.BlockSpec((None, None, N, CHUNK, 1), lambda b, h: (b, h, 0, 0, 0))
    rowspec = pl.BlockSpec((None, None, N, 1, CHUNK), lambda b, h: (b, h, 0, 0, 0))
    head = pl.BlockSpec((S, HEAD), lambda b, h: (b, h))
    return pl.pallas_call(
        body, name="gdn_inter_fwd", grid=(B, H),
        in_specs=[pl.BlockSpec((S, HEAD), lambda b, h: (b, 3 * h)), pl.BlockSpec((S, HEAD), lambda b, h: (b, 3 * h + 1)),
                  head, head, colspec, rowspec],
        out_specs=[head, pl.BlockSpec((None, None, N, HEAD, HEAD), lambda b, h: (b, h, 0, 0, 0))],
        out_shape=[jax.ShapeDtypeStruct((T, H * HEAD), F32), jax.ShapeDtypeStruct((B, H, N, HEAD, HEAD), F32)],
        scratch_shapes=[pltpu.VMEM((HEAD, HEAD), F32)],
        compiler_params=_params("parallel", "parallel"),
    )(qkvn, qkvn, u_hat, w, gcc5, gcr5)


def gdn_inter_bwd(qkvn, u_hat, w, gcc5, gcr5, states, d_o, B, S, H):
    T = B * S
    N = S // CHUNK

    def body(q_ref, k_ref, uh_ref, w_ref, gc_ref, gr_ref, st_ref, do_ref,
             dq_ref, dk_ref, duh_ref, dw_ref, dgc_ref, dgr_ref, ds_scr):
        ds_scr[...] = jnp.zeros_like(ds_scr)

        def step(i, c):
            n = N - 1 - i
            rows = pl.ds(pl.multiple_of(n * CHUNK, CHUNK), CHUNK)
            _, vjp = jax.vjp(functools.partial(_inter_fn, ops=_DiffOps), q_ref[rows, :], k_ref[rows, :],
                             uh_ref[rows, :], w_ref[rows, :], gc_ref[n], gr_ref[n], st_ref[n])
            dq, dk, duh, dw, dgc, dgr, ds = vjp((do_ref[rows, :], ds_scr[...]))
            dq_ref[rows, :] = dq
            dk_ref[rows, :] = dk
            duh_ref[rows, :] = duh
            dw_ref[rows, :] = dw
            dgc_ref[n] = dgc
            dgr_ref[n] = dgr
            ds_scr[...] = ds
            return c

        lax.fori_loop(0, N, step, 0)

    colspec = pl.BlockSpec((None, None, N, CHUNK, 1), lambda b, h: (b, h, 0, 0, 0))
    rowspec = pl.BlockSpec((None, None, N, 1, CHUNK), lambda b, h: (b, h, 0, 0, 0))
    head = pl.BlockSpec((S, HEAD), lambda b, h: (b, h))
    hshape = jax.ShapeDtypeStruct((T, H * HEAD), F32)
    return pl.pallas_call(
        body, name="gdn_inter_bwd", grid=(B, H),
        in_specs=[pl.BlockSpec((S, HEAD), lambda b, h: (b, 3 * h)), pl.BlockSpec((S, HEAD), lambda b, h: (b, 3 * h + 1)),
                  head, head, colspec, rowspec,
                  pl.BlockSpec((None, None, N, HEAD, HEAD), lambda b, h: (b, h, 0, 0, 0)), head],
        out_specs=[head, head, head, head, colspec, rowspec],
        out_shape=[hshape, hshape, hshape, hshape,
                   jax.ShapeDtypeStruct((B, H, N, CHUNK, 1), F32), jax.ShapeDtypeStruct((B, H, N, 1, CHUNK), F32)],
        scratch_shapes=[pltpu.VMEM((HEAD, HEAD), F32)],
        compiler_params=_params("parallel", "parallel"),
    )(qkvn, qkvn, u_hat, w, gcc5, gcr5, states, d_o)


def gdn_post_fwd(o, p_gz, g, H):
    T = o.shape[0]
    tm = _tile(T, 1024, 8)

    def body(o_ref, z_ref, g_ref, y_ref):
        ov, z = o_ref[...], z_ref[...]
        r = lax.rsqrt(jnp.mean(ov * ov, axis=-1, keepdims=True) + EPS)
        y_ref[...] = (ov * r * g_ref[...] * z * _sigmoid(z)).astype(BF16)

    blk = pl.BlockSpec((tm, HEAD), lambda i, h: (i, h))
    return pl.pallas_call(
        body, name="gdn_post_fwd", grid=(T // tm, H), in_specs=[blk, blk, pl.BlockSpec((1, HEAD), lambda i, h: (0, 0))],
        out_specs=blk, out_shape=jax.ShapeDtypeStruct((T, H * HEAD), BF16),
        compiler_params=_params("parallel", "parallel"),
    )(o, p_gz, g)


def gdn_post_bwd(o, p_gz, g, dy, H):
    T = o.shape[0]
    tm = _tile(T, 1024, 8)

    def body(o_ref, z_ref, g_ref, dy_ref, do_ref, dz_ref, dg_ref):
        @pl.when((pl.program_id(0) == 0) & (pl.program_id(1) == 0))
        def _():
            dg_ref[...] = jnp.zeros_like(dg_ref)

        ov, z, d = o_ref[...], z_ref[...], dy_ref[...]
        r = lax.rsqrt(jnp.mean(ov * ov, axis=-1, keepdims=True) + EPS)
        xh = ov * r
        sg = _sigmoid(z)
        sz = z * sg
        d_n = d * sz
        dz_ref[...] = (d * xh * g_ref[...] * (sg + z * sg * (1.0 - sg))).astype(BF16)
        dg_ref[...] += jnp.sum(d_n * xh, axis=0, keepdims=True)
        dxh = d_n * g_ref[...]
        do_ref[...] = r * (dxh - xh * jnp.mean(dxh * xh, axis=-1, keepdims=True))

    blk = pl.BlockSpec((tm, HEAD), lambda i, h: (i, h))
    vec = pl.BlockSpec((1, HEAD), lambda i, h: (0, 0))
    return pl.pallas_call(
        body, name="gdn_post_bwd", grid=(T // tm, H), in_specs=[blk, blk, vec, blk], out_specs=[blk, blk, vec],
        out_shape=[jax.ShapeDtypeStruct((T, H * HEAD), F32), jax.ShapeDtypeStruct((T, H * HEAD), BF16),
                   jax.ShapeDtypeStruct((1, HEAD), F32)],
        compiler_params=_params("arbitrary", "arbitrary"),
    )(o, p_gz, g, dy)


def adamw(w, g, m, v, name):
    shape = w.shape
    w2, g2, m2, v2 = (a.reshape(-1, shape[-1]) for a in (w, g, m, v))
    R, C = w2.shape
    tr = _tile(R, 128, 8)

    def body(w_ref, g_ref, m_ref, v_ref, d_ref, nm_ref, nv_ref):
        gv = g_ref[...]
        nm = ADAM_B1 * m_ref[...] + (1.0 - ADAM_B1) * gv
        nv = ADAM_B2 * v_ref[...] + (1.0 - ADAM_B2) * (gv * gv)
        m_hat = nm / (1.0 - ADAM_B1 ** ADAM_STEP)
        v_hat = nv / (1.0 - ADAM_B2 ** ADAM_STEP)
        d_ref[...] = -ADAM_LR * (m_hat / (jnp.sqrt(v_hat) + ADAM_EPS) + ADAM_WD * w_ref[...])
        nm_ref[...] = nm
        nv_ref[...] = nv

    blk = pl.BlockSpec((tr, C), lambda i: (i, 0))
    sh = jax.ShapeDtypeStruct((R, C), F32)
    d, nm, nv = pl.pallas_call(
        body, name=name, grid=(R // tr,), in_specs=[blk] * 4, out_specs=[blk] * 3, out_shape=[sh] * 3,
        compiler_params=_params("parallel"),
    )(w2, g2, m2, v2)
    return d.reshape(shape), nm.reshape(shape), nv.reshape(shape)


def _place():
    x, y, c = lax.axis_index("x"), lax.axis_index("y"), lax.axis_index("c")
    chips = [(1 - x, y), (x, 1 - y), (1 - x, 1 - y)]
    return x, y, c, chips


_HBM = pl.BlockSpec(memory_space=pltpu.HBM)


def allgather_weights(packed):
    _, Rh, _ = packed.shape

    def body(in_ref, out_ref, send_sems, recv_sems, local_sem):
        x, y, c, chips = _place()
        me_s = 2 * x + y
        sibling = (x, y, 1 - c)
        mine = pltpu.make_async_copy(in_ref, out_ref.at[me_s], local_sem)
        mine.start()

        def copy(k, shard, half, to, src=None):
            dst = out_ref.at[shard, half]
            return pltpu.make_async_remote_copy(src_ref=dst if src is None else src, dst_ref=dst,
                                                send_sem=send_sems.at[k], recv_sem=recv_sems.at[k],
                                                device_id=to, device_id_type=MESH)

        first = [copy(j, me_s, c, (*chip, c), src=in_ref.at[c]) for j, chip in enumerate(chips)]
        for cp in first:
            cp.start()
        passed = [copy(3 + j, 2 * chip[0] + chip[1], c, sibling) for j, chip in enumerate(chips)]
        for j, chip in enumerate(chips):
            copy(j, 2 * chip[0] + chip[1], c, (x, y, c)).wait_recv()
            passed[j].start()
        for j, chip in enumerate(chips):
            copy(3 + j, 2 * chip[0] + chip[1], 1 - c, (x, y, c)).wait_recv()
        for cp in first + passed:
            cp.wait_send()
        mine.wait()

    return pl.pallas_call(
        body, name="allgather_weights", in_specs=[_HBM], out_specs=_HBM,
        out_shape=jax.ShapeDtypeStruct((N_CHIP, 2, Rh, ROW), packed.dtype),
        scratch_shapes=[pltpu.SemaphoreType.DMA((6,)), pltpu.SemaphoreType.DMA((6,)), pltpu.SemaphoreType.DMA],
    )(packed)


def exchange_halves(g):
    _, ns, Rh, _ = g.shape

    def body(g_ref, out_ref, send_sem, recv_sem):
        x, y, c, _ = _place()
        cp = pltpu.make_async_remote_copy(src_ref=g_ref.at[1 - c], dst_ref=out_ref, send_sem=send_sem, recv_sem=recv_sem,
                                          device_id=(x, y, 1 - c), device_id_type=MESH)
        cp.start()
        cp.wait()

    return pl.pallas_call(
        body, name="exchange_halves", in_specs=[_HBM], out_specs=_HBM,
        out_shape=jax.ShapeDtypeStruct((ns, Rh, ROW), F32),
        scratch_shapes=[pltpu.SemaphoreType.DMA, pltpu.SemaphoreType.DMA],
    )(g)


def add_halves(g, got, idx):
    _, ns, Rh, _ = g.shape
    tr = _tile(Rh, 512, 128)

    def body(idx_ref, a_ref, b_ref, o32_ref, o16_ref):
        s = a_ref[...] + b_ref[...]
        o32_ref[...] = s
        o16_ref[...] = s.astype(BF16)

    blk = pl.BlockSpec((None, tr, ROW), lambda s, i, idx_ref: (s, i, 0))
    return pl.pallas_call(
        body, name="add_halves",
        grid_spec=pltpu.PrefetchScalarGridSpec(
            num_scalar_prefetch=1, grid=(ns, Rh // tr),
            in_specs=[pl.BlockSpec((None, None, tr, ROW), lambda s, i, idx_ref: (idx_ref[0], s, i, 0)), blk],
            out_specs=[blk, blk]),
        out_shape=[jax.ShapeDtypeStruct((ns, Rh, ROW), F32), jax.ShapeDtypeStruct((ns, Rh, ROW), BF16)],
        compiler_params=_params("parallel", "parallel"),
    )(idx, g, got)


def scatter_chips(b16):
    ns, Rh, _ = b16.shape

    def body(in_ref, out_ref, send_sems, recv_sems):
        x, y, c, chips = _place()
        me_s = 2 * x + y
        cps = []
        for j, chip in enumerate(chips):
            cps.append(pltpu.make_async_remote_copy(
                src_ref=in_ref.at[2 * chip[0] + chip[1]], dst_ref=out_ref.at[me_s],
                send_sem=send_sems.at[j], recv_sem=recv_sems.at[j], device_id=(*chip, c), device_id_type=MESH))
        for cp in cps:
            cp.start()
        for j, chip in enumerate(chips):
            pltpu.make_async_remote_copy(
                src_ref=in_ref.at[me_s], dst_ref=out_ref.at[2 * chip[0] + chip[1]],
                send_sem=send_sems.at[j], recv_sem=recv_sems.at[j], device_id=(x, y, c), device_id_type=MESH).wait_recv()
        for cp in cps:
            cp.wait_send()

    return pl.pallas_call(
        body, name="scatter_chips", in_specs=[_HBM], out_specs=_HBM,
        out_shape=jax.ShapeDtypeStruct((ns, Rh, ROW), BF16),
        scratch_shapes=[pltpu.SemaphoreType.DMA((3,)), pltpu.SemaphoreType.DMA((3,))],
    )(b16)


def add_chips(a32, got16, idx):
    ns, Rh, _ = a32.shape
    tr = _tile(Rh, 512, 128)

    def body(idx_ref, a_ref, r1_ref, r2_ref, r3_ref, o_ref):
        o_ref[...] = ((a_ref[...] + r1_ref[...].astype(F32)) + r2_ref[...].astype(F32)) + r3_ref[...].astype(F32)

    def slab(k):
        return pl.BlockSpec((None, tr, ROW), lambda i, idx_ref: ((idx_ref[1] + k) % ns, i, 0))

    return pl.pallas_call(
        body, name="add_chips",
        grid_spec=pltpu.PrefetchScalarGridSpec(
            num_scalar_prefetch=1, grid=(Rh // tr,), in_specs=[slab(0), slab(1), slab(2), slab(3)],
            out_specs=pl.BlockSpec((tr, ROW), lambda i, idx_ref: (i, 0))),
        out_shape=jax.ShapeDtypeStruct((Rh, ROW), F32),
        compiler_params=_params("parallel"),
    )(idx, a32, got16, got16, got16)


def share_halves(half):
    Rh, _ = half.shape

    def body(in_ref, out_ref, send_sem, recv_sem, local_sem):
        x, y, c, _ = _place()
        mine = pltpu.make_async_copy(in_ref, out_ref.at[c], local_sem)
        mine.start()
        cp = pltpu.make_async_remote_copy(src_ref=in_ref, dst_ref=out_ref.at[c], send_sem=send_sem, recv_sem=recv_sem,
                                          device_id=(x, y, 1 - c), device_id_type=MESH)
        cp.start()
        pltpu.make_async_remote_copy(src_ref=in_ref, dst_ref=out_ref.at[1 - c], send_sem=send_sem, recv_sem=recv_sem,
                                     device_id=(x, y, c), device_id_type=MESH).wait_recv()
        cp.wait_send()
        mine.wait()

    return pl.pallas_call(
        body, name="share_halves", in_specs=[_HBM], out_specs=_HBM,
        out_shape=jax.ShapeDtypeStruct((2, Rh, ROW), F32),
        scratch_shapes=[pltpu.SemaphoreType.DMA, pltpu.SemaphoreType.DMA, pltpu.SemaphoreType.DMA],
    )(half)


def allreduce_small(v):
    R, _ = v.shape

    def body(in_ref, out_ref, slots, send_sems, recv_sems):
        x, y, c, _ = _place()
        me = 4 * x + 2 * y + c
        slots[me] = in_ref[...]
        cps = []
        for k in range(1, N_DEV):
            to = (x ^ (k >> 2), y ^ ((k >> 1) & 1), c ^ (k & 1))
            cps.append(pltpu.make_async_remote_copy(src_ref=in_ref, dst_ref=slots.at[me], send_sem=send_sems.at[k - 1],
                                                    recv_sem=recv_sems.at[k - 1], device_id=to, device_id_type=MESH))
        for cp in cps:
            cp.start()
        for k in range(1, N_DEV):
            frm = 4 * (x ^ (k >> 2)) + 2 * (y ^ ((k >> 1) & 1)) + (c ^ (k & 1))
            pltpu.make_async_remote_copy(src_ref=in_ref, dst_ref=slots.at[frm], send_sem=send_sems.at[k - 1],
                                         recv_sem=recv_sems.at[k - 1], device_id=(x, y, c), device_id_type=MESH).wait_recv()
        for cp in cps:
            cp.wait_send()
        acc = slots[0]
        for d in range(1, N_DEV):
            acc = acc + slots[d]
        out_ref[...] = acc

    vm = pl.BlockSpec(memory_space=pltpu.VMEM)
    return pl.pallas_call(
        body, name="allreduce_small", in_specs=[vm], out_specs=vm, out_shape=jax.ShapeDtypeStruct((R, ROW), F32),
        scratch_shapes=[pltpu.VMEM((N_DEV, R, ROW), F32), pltpu.SemaphoreType.DMA((N_DEV - 1,)),
                        pltpu.SemaphoreType.DMA((N_DEV - 1,))],
    )(v)


def _rows_of(n):
    return -(-n // (16 * ROW)) * 16


def _pack_rows(items, total_rows, dtype):
    parts = []
    used = 0
    for a in items:
        flat = a.reshape(-1)
        r = _rows_of(flat.shape[0])
        flat = jnp.pad(flat, (0, r * ROW - flat.shape[0]))
        parts.append(flat.reshape(r, ROW))
        used += r
    if total_rows > used:
        parts.append(jnp.zeros((total_rows - used, ROW), dtype))
    return jnp.concatenate(parts, axis=0)


def _unpack_rows(buf, shapes):
    lead = buf.shape[:-2]
    out = []
    off = 0
    for shp in shapes:
        n = math.prod(shp)
        r = _rows_of(n)
        piece = buf[..., off:off + r, :].reshape(*lead, r * ROW)[..., :n].reshape(*lead, *shp)
        out.append(piece)
        off += r
    return out


def _interleave_heads(w, H):
    lead = w.shape[:-1]
    return w.reshape(*lead, 3, H, HEAD).swapaxes(-3, -2).reshape(*lead, 3 * H * HEAD)


def _deinterleave_heads(w, H):
    lead = w.shape[:-1]
    return w.reshape(*lead, H, 3, HEAD).swapaxes(-3, -2).reshape(*lead, 3 * H * HEAD)


def kernel(x, norm_mix, w_in, fox_f_bias, gdn_conv_w, gdn_a_log, gdn_dt_bias, gdn_norm, w_branch_fox, w_branch_gdn, w_out, norm_ffn, w_up, ffn_conv_w, w_down, norm_final, loss_target, m_norm_mix, m_w_in, m_fox_f_bias, m_gdn_conv_w, m_gdn_a_log, m_gdn_dt_bias, m_gdn_norm, m_w_branch_fox, m_w_branch_gdn, m_w_out, m_norm_ffn, m_w_up, m_ffn_conv_w, m_w_down, m_norm_final, v_norm_mix, v_w_in, v_fox_f_bias, v_gdn_conv_w, v_gdn_a_log, v_gdn_dt_bias, v_gdn_norm, v_w_branch_fox, v_w_branch_gdn, v_w_out, v_norm_ffn, v_w_up, v_ffn_conv_w, v_w_down, v_norm_final):
    B, S, D = x.shape
    T = B * S
    H = D // HEAD
    N = S // CHUNK
    FF = w_down.shape[1] * N_CHIP
    d_in = 9 * D + 3 * H
    assert w_in.shape[2] * N_CHIP == d_in and 3 * H <= 128

    cidx = lax.axis_index("c").astype(jnp.int32)
    sidx = (2 * lax.axis_index("x") + lax.axis_index("y")).astype(jnp.int32)
    idx = jnp.stack([cidx, sidx])

    big = [w_in[0], w_branch_fox[0], w_branch_gdn[0], w_out[0], w_up[0], w_down[0]]
    convs = [gdn_conv_w[0], ffn_conv_w[0]]
    big_shapes = [a.shape for a in big]
    conv_bits_shapes = [a.shape + (2,) for a in convs]
    conv_bits = [lax.bitcast_convert_type(a, BF16) for a in convs]
    n_rows = sum(_rows_of(math.prod(s)) for s in big_shapes + conv_bits_shapes)
    Rh = -(-n_rows // 256) * 128
    packed = _pack_rows([a.astype(BF16) for a in big] + conv_bits, 2 * Rh, BF16).reshape(2, Rh, ROW)
    gathered = allgather_weights(packed).reshape(N_CHIP, 2 * Rh, ROW)
    parts = _unpack_rows(gathered, big_shapes + conv_bits_shapes)
    cat_cols = lambda p: jnp.concatenate([p[i] for i in range(N_CHIP)], axis=-1)
    cat_rows = lambda p: p.reshape(-1, p.shape[-1])
    W_in = cat_cols(parts[0])
    W_bf, W_bg, W_out = cat_rows(parts[1]), cat_rows(parts[2]), cat_rows(parts[3])
    W_up, W_down = cat_cols(parts[4]), cat_rows(parts[5])
    gconv = cat_cols(lax.bitcast_convert_type(parts[6], F32))
    fconv = cat_cols(lax.bitcast_convert_type(parts[7], F32))

    o1, o2 = 3 * D, 3 * D + H
    o3, o4, o5, o6 = o2 + 3 * D, o2 + 3 * D + H, o2 + 3 * D + 2 * H, o2 + 4 * D + 2 * H
    W_fox = _interleave_heads(W_in[:, :o1], H)
    W_gqkv = _interleave_heads(W_in[:, o2:o3], H)
    W_gz = W_in[:, o5:o6]
    W_gates = W_in[:, o6:]
    W_small = jnp.concatenate([W_in[:, o1:o2], W_in[:, o3:o5], jnp.zeros((D, 128 - 3 * H), BF16)], axis=1)
    gconv_i = _interleave_heads(gconv, H)
    W_up_g, W_up_v = W_up[:, :FF], W_up[:, FF:]
    fconv_g, fconv_v = fconv[:, :FF], fconv[:, FF:]
    prm = jnp.zeros((8, 128), F32)
    prm = prm.at[0, 0:H].set(fox_f_bias[0]).at[0, H:2 * H].set(gdn_dt_bias[0]).at[1, H:2 * H].set(gdn_a_log[0])

    x2 = x.reshape(T, D)
    tgt = loss_target.reshape(T, D)

    hn1 = rmsnorm_fwd(x2, norm_mix, "rmsnorm_mix")
    p_fox = matmul(hn1, W_fox, "nn", "proj_fox", out_dtype=BF16)
    p_gqkv = matmul(hn1, W_gqkv, "nn", "proj_gqkv")
    p_gz = matmul(hn1, W_gz, "nn", "proj_gz")
    p_gates = matmul(hn1, W_gates, "nn", "proj_gates")
    p_small = matmul(hn1, W_small, "nn", "proj_small")

    sm = small_fwd(p_small, prm, B, S, H)
    heads = lambda a: a.reshape(B, S, H).transpose(0, 2, 1)
    c_bhs, gc_bhs, beta_bhs = heads(sm[:, 0:H]), heads(sm[:, H:2 * H]), heads(sm[:, 2 * H:3 * H])
    c_col, c_row = c_bhs[..., None], c_bhs[:, :, None, :]
    gcc5, gcr5 = gc_bhs.reshape(B, H, N, CHUNK, 1), gc_bhs.reshape(B, H, N, 1, CHUNK)
    beta5 = beta_bhs.reshape(B, H, N, CHUNK, 1)

    o_fox, o_fox16, lse = fox_fwd(p_fox, c_col, c_row, B, S, H)
    qkvn = gdn_prep_fwd(p_gqkv, gconv_i, B, S, H)
    u_hat, w_t, t_inv = gdn_intra_fwd(qkvn, beta5, gcc5, gcr5, B, S, H)
    o_gdn, states = gdn_inter_fwd(qkvn, u_hat, w_t, gcc5, gcr5, B, S, H)
    y_gdn = gdn_post_fwd(o_gdn, p_gz, gdn_norm, H)
    bf_ = matmul(o_fox16, W_bf, "nn", "branch_fox")
    bg_ = matmul(y_gdn, W_bg, "nn", "branch_gdn")
    y = merge_fwd(p_gates, bf_, bg_)
    h1 = matmul(y, W_out, "nn", "out_proj", add=x2)
    hn2 = rmsnorm_fwd(h1, norm_ffn, "rmsnorm_ffn")
    up_g = matmul(hn2, W_up_g, "nn", "up_gate")
    up_v = matmul(hn2, W_up_v, "nn", "up_val")
    act = ffn_gate_fwd(up_g, up_v, fconv_g, fconv_v, B, S)
    h2 = matmul(act, W_down, "nn", "down_proj", add=h1)
    loss_cols, dh2, dh2_16, d_norm_final = final_loss(h2, norm_final.reshape(1, D), tgt)
    loss = lax.psum(0.5 * jnp.sum(loss_cols) / D, ("x", "y", "c"))

    d_act = matmul(dh2_16, W_down, "nt", "d_act")
    dW_down = matmul(act, dh2_16, "tn", "dw_down")
    d_upg, d_upv, d_fconv_g, d_fconv_v = ffn_gate_bwd(up_g, up_v, fconv_g, fconv_v, d_act, B, S)
    d_hn2 = matmul(d_upg, W_up_g, "nt", "d_hn2_g")
    d_hn2 = matmul(d_upv, W_up_v, "nt", "d_hn2_v", add=d_hn2)
    dW_up = jnp.concatenate([matmul(hn2, d_upg, "tn", "dw_up_g"), matmul(hn2, d_upv, "tn", "dw_up_v")], axis=1)
    dh1, dh1_16, d_norm_ffn = rmsnorm_bwd(h1, norm_ffn, d_hn2, dh2, "rmsnorm_ffn_bwd")
    d_y = matmul(dh1_16, W_out, "nt", "d_y")
    dW_out = matmul(y, dh1_16, "tn", "dw_out")
    d_bf, d_bg, d_gates = merge_bwd(p_gates, bf_, bg_, d_y)
    d_ofox = matmul(d_bf, W_bf, "nt", "d_ofox")
    dW_bf = matmul(o_fox16, d_bf, "tn", "dw_bf")
    d_ygdn = matmul(d_bg, W_bg, "nt", "d_ygdn")
    dW_bg = matmul(y_gdn, d_bg, "tn", "dw_bg")

    d_pfox, d_ccol, d_crow = fox_bwd(p_fox, c_col, c_row, o_fox, lse, d_ofox, B, S, H)

    d_ogdn, d_gz, d_gdn_norm = gdn_post_bwd(o_gdn, p_gz, gdn_norm, d_ygdn, H)
    dq_i, dk_i, d_uh, d_wt, dgcc_a, dgcr_a = gdn_inter_bwd(qkvn, u_hat, w_t, gcc5, gcr5, states, d_ogdn, B, S, H)
    d_qkvn, d_beta5, dgcc_b, dgcr_b = gdn_intra_bwd(qkvn, beta5, gcc5, gcr5, t_inv, d_uh, d_wt, dq_i, dk_i, B, S, H)
    d_pgqkv, d_gconv_i = gdn_prep_bwd(p_gqkv, gconv_i, d_qkvn, B, S, H)

    tokens = lambda a: a.reshape(B, H, S).transpose(0, 2, 1).reshape(T, H)
    d_gc = (dgcc_a + dgcc_b).reshape(B, H, S) + (dgcr_a + dgcr_b).reshape(B, H, S)
    d_sm = jnp.concatenate([tokens(d_ccol.reshape(B, H, S) + d_crow.reshape(B, H, S)), tokens(d_gc), tokens(d_beta5.reshape(B, H, S)),
                            jnp.zeros((T, 128 - 3 * H), F32)], axis=1)
    d_psmall, d_prm = small_bwd(p_small, prm, d_sm, B, S, H)

    d_hn1 = matmul(d_pfox, W_fox, "nt", "d_hn1_fox")
    d_hn1 = matmul(d_pgqkv, W_gqkv, "nt", "d_hn1_gqkv", add=d_hn1)
    d_hn1 = matmul(d_gz, W_gz, "nt", "d_hn1_gz", add=d_hn1)
    d_hn1 = matmul(d_gates, W_gates, "nt", "d_hn1_gates", add=d_hn1)
    d_hn1 = matmul(d_psmall, W_small, "nt", "d_hn1_small", add=d_hn1)
    dW_fox = matmul(hn1, d_pfox, "tn", "dw_fox")
    dW_gqkv = matmul(hn1, d_pgqkv, "tn", "dw_gqkv")
    dW_gz = matmul(hn1, d_gz, "tn", "dw_gz")
    dW_gates = matmul(hn1, d_gates, "tn", "dw_gates")
    dW_small = matmul(hn1, d_psmall, "tn", "dw_small")
    grad_x, _, d_norm_mix = rmsnorm_bwd(x2, norm_mix, d_hn1, dh1, "rmsnorm_mix_bwd")

    dW_in = jnp.concatenate([_deinterleave_heads(dW_fox, H), dW_small[:, 0:H], _deinterleave_heads(dW_gqkv, H),
                             dW_small[:, H:3 * H], dW_gz, dW_gates], axis=1)
    d_gconv = _deinterleave_heads(d_gconv_i, H)
    d_fconv = jnp.concatenate([d_fconv_g, d_fconv_v], axis=1)

    col_shard = lambda g, s: g[:, s * (g.shape[1] // N_CHIP):(s + 1) * (g.shape[1] // N_CHIP)]
    row_shard = lambda g, s: g[s * (g.shape[0] // N_CHIP):(s + 1) * (g.shape[0] // N_CHIP)]
    shard_items = lambda s: [col_shard(dW_in, s), row_shard(dW_bf, s), row_shard(dW_bg, s), row_shard(dW_out, s),
                             col_shard(dW_up, s), row_shard(dW_down, s), col_shard(d_gconv, s), col_shard(d_fconv, s)]
    g_shapes = [a.shape for a in shard_items(0)]
    assert sum(_rows_of(math.prod(s)) for s in g_shapes) <= 2 * Rh
    gpack = jnp.stack([_pack_rows(shard_items(s), 2 * Rh, F32).reshape(2, Rh, ROW) for s in range(N_CHIP)], axis=1)
    got = exchange_halves(gpack)
    sum32, sum16 = add_halves(gpack, got, idx)
    got16 = scatter_chips(sum16)
    half = add_chips(sum32, got16, idx)
    full = share_halves(half).reshape(2 * Rh, ROW)
    g_w_in, g_bf, g_bg, g_out, g_up, g_down, g_gconv, g_fconv = _unpack_rows(full, g_shapes)

    small_items = [d_norm_mix, d_norm_ffn, d_norm_final, d_gdn_norm, d_prm]
    small_shapes = [a.shape for a in small_items]
    sv = allreduce_small(_pack_rows(small_items, 8, F32))
    g_norm_mix, g_norm_ffn, g_norm_final, g_gdn_norm, g_prm = _unpack_rows(sv, small_shapes)
    g_norm_final = g_norm_final.reshape(D)
    g_fbias, g_dtb, g_alog = g_prm[0:1, 0:H], g_prm[0:1, H:2 * H], g_prm[1:2, H:2 * H]

    names = ["norm_mix", "w_in", "fox_f_bias", "gdn_conv_w", "gdn_a_log", "gdn_dt_bias", "gdn_norm", "w_branch_fox",
             "w_branch_gdn", "w_out", "norm_ffn", "w_up", "ffn_conv_w", "w_down", "norm_final"]
    ws = [norm_mix, w_in, fox_f_bias, gdn_conv_w, gdn_a_log, gdn_dt_bias, gdn_norm, w_branch_fox, w_branch_gdn, w_out,
          norm_ffn, w_up, ffn_conv_w, w_down, norm_final]
    ms = [m_norm_mix, m_w_in, m_fox_f_bias, m_gdn_conv_w, m_gdn_a_log, m_gdn_dt_bias, m_gdn_norm, m_w_branch_fox,
          m_w_branch_gdn, m_w_out, m_norm_ffn, m_w_up, m_ffn_conv_w, m_w_down, m_norm_final]
    vs = [v_norm_mix, v_w_in, v_fox_f_bias, v_gdn_conv_w, v_gdn_a_log, v_gdn_dt_bias, v_gdn_norm, v_w_branch_fox,
          v_w_branch_gdn, v_w_out, v_norm_ffn, v_w_up, v_ffn_conv_w, v_w_down, v_norm_final]
    gs = [g_norm_mix, g_w_in, g_fbias, g_gconv, g_alog, g_dtb, g_gdn_norm, g_bf, g_bg, g_out, g_norm_ffn, g_up,
          g_fconv, g_down, g_norm_final]
    gs = [g.reshape(w.shape) for g, w in zip(gs, ws)]
    deltas, new_ms, new_vs = [], [], []
    for nm, w, g, m, v in zip(names, ws, gs, ms, vs):
        if w.ndim == 1:
            d, a, b = adamw(w.reshape(1, -1), g.reshape(1, -1), m.reshape(1, -1), v.reshape(1, -1), "adamw_" + nm)
            d, a, b = d.reshape(w.shape), a.reshape(w.shape), b.reshape(w.shape)
        else:
            d, a, b = adamw(w, g, m, v, "adamw_" + nm)
        deltas.append(d)
        new_ms.append(a)
        new_vs.append(b)

    return (loss, grad_x.reshape(B, S, D), *gs, *deltas, *new_ms, *new_vs)
```

```python
import functools
import math

import jax
import jax.numpy as jnp
from jax import lax
from jax.experimental import pallas as pl
from jax.experimental.pallas import tpu as pltpu

F32 = jnp.float32
BF16 = jnp.bfloat16
HEAD = 128
CHUNK = 64
GDN_CONV = 4
FFN_CONV = 3
EPS = 1e-6
NEG = -1e30
ROW = 1024
ATT_TILE = 512
N_CHIP = 4
N_DEV = 8
MESH = pl.DeviceIdType.MESH
HI = lax.Precision.HIGH
EXACT = lax.Precision.HIGHEST

ADAM_LR, ADAM_B1, ADAM_B2, ADAM_EPS, ADAM_WD, ADAM_STEP = 0.001, 0.9, 0.999, 1e-08, 0.01, 10


def _tile(n, cap, unit=128):
    best = None
    t = unit
    while t <= min(n, cap):
        if n % t == 0:
            best = t
        t += unit
    return best if best is not None else n


def _params(*sem):
    return pltpu.CompilerParams(dimension_semantics=sem)


_NN = (((1,), (0,)), ((), ()))
_NT = (((1,), (1,)), ((), ()))
_TN = (((0,), (0,)), ((), ()))


def _dg(a, b, dims, hi):
    if hi:
        return lax.dot_general(a, b, dims, precision=HI, preferred_element_type=F32)
    return lax.dot_general(a.astype(BF16), b.astype(BF16), dims, preferred_element_type=F32)


class _RawOps:
    @staticmethod
    def nn(a, b, hi=False):
        return _dg(a, b, _NN, hi)

    @staticmethod
    def nt(a, b, hi=False):
        return _dg(a, b, _NT, hi)

    @staticmethod
    def tn(a, b, hi=False):
        return _dg(a, b, _TN, hi)


def _make_diff_ops():
    def build(hi):
        @jax.custom_vjp
        def nn(a, b):
            return _dg(a, b, _NN, hi)

        nn.defvjp(lambda a, b: (_dg(a, b, _NN, hi), (a, b)),
                  lambda r, g: (_dg(g, r[1], _NT, hi), _dg(r[0], g, _TN, hi)))

        @jax.custom_vjp
        def nt(a, b):
            return _dg(a, b, _NT, hi)

        nt.defvjp(lambda a, b: (_dg(a, b, _NT, hi), (a, b)),
                  lambda r, g: (_dg(g, r[1], _NN, hi), _dg(g, r[0], _TN, hi)))

        @jax.custom_vjp
        def tn(a, b):
            return _dg(a, b, _TN, hi)

        tn.defvjp(lambda a, b: (_dg(a, b, _TN, hi), (a, b)),
                  lambda r, g: (_dg(r[1], g, _NT, hi), _dg(r[0], g, _NN, hi)))
        return nn, nt, tn

    lo, hi_ = build(False), build(True)

    class _DiffOps:
        @staticmethod
        def nn(a, b, hi=False):
            return (hi_ if hi else lo)[0](a, b)

        @staticmethod
        def nt(a, b, hi=False):
            return (hi_ if hi else lo)[1](a, b)

        @staticmethod
        def tn(a, b, hi=False):
            return (hi_ if hi else lo)[2](a, b)

    return _DiffOps


_DiffOps = _make_diff_ops()


def _sigmoid(x):
    return 1.0 / (1.0 + jnp.exp(-x))


def _mm_tile(n, pref):
    if n % pref == 0:
        return pref
    if n % 1408 == 0:
        return 1408
    return _tile(n, pref)


def matmul(a, b, mode, name, add=None, out_dtype=F32):
    if mode == "nn":
        (M, K), (K2, N) = a.shape, b.shape
    elif mode == "nt":
        (M, K), (N, K2) = a.shape, b.shape
    else:
        (K, M), (K2, N) = a.shape, b.shape
    assert K == K2, (name, a.shape, b.shape)
    tm, tn, tk = _mm_tile(M, 512), _mm_tile(N, 1024), _mm_tile(K, 1024)
    nk = K // tk
    dims = {"nn": _NN, "nt": _NT, "tn": _TN}[mode]
    if mode == "tn":
        a_spec = pl.BlockSpec((tk, tm), lambda i, j, k: (k, i))
    else:
        a_spec = pl.BlockSpec((tm, tk), lambda i, j, k: (i, k))
    if mode == "nt":
        b_spec = pl.BlockSpec((tn, tk), lambda i, j, k: (j, k))
    else:
        b_spec = pl.BlockSpec((tk, tn), lambda i, j, k: (k, j))
    o_spec = pl.BlockSpec((tm, tn), lambda i, j, k: (i, j))
    has_add = add is not None

    def body(*refs):
        if has_add:
            a_ref, b_ref, add_ref, o_ref, acc_ref = refs
        else:
            a_ref, b_ref, o_ref, acc_ref = refs
        k = pl.program_id(2)

        @pl.when(k == 0)
        def _():
            acc_ref[...] = jnp.zeros_like(acc_ref)

        acc_ref[...] += lax.dot_general(a_ref[...].astype(BF16), b_ref[...].astype(BF16), dims,
                                        preferred_element_type=F32)

        @pl.when(k == nk - 1)
        def _():
            r = acc_ref[...]
            if has_add:
                r = r + add_ref[...]
            o_ref[...] = r.astype(out_dtype)

    in_specs = [a_spec, b_spec] + ([o_spec] if has_add else [])
    args = (a, b) + ((add,) if has_add else ())
    return pl.pallas_call(
        body, name=name, grid=(M // tm, N // tn, nk), in_specs=in_specs, out_specs=o_spec,
        out_shape=jax.ShapeDtypeStruct((M, N), out_dtype),
        scratch_shapes=[pltpu.VMEM((tm, tn), F32)],
        compiler_params=_params("parallel", "parallel", "arbitrary"),
    )(*args)


def rmsnorm_fwd(x, g, name):
    T, D = x.shape
    tm = _tile(T, 512, 8)

    def body(x_ref, g_ref, o_ref):
        xv = x_ref[...]
        r = lax.rsqrt(jnp.mean(xv * xv, axis=-1, keepdims=True) + EPS)
        o_ref[...] = (xv * r * g_ref[...]).astype(BF16)

    return pl.pallas_call(
        body, name=name, grid=(T // tm,),
        in_specs=[pl.BlockSpec((tm, D), lambda i: (i, 0)), pl.BlockSpec((1, D), lambda i: (0, 0))],
        out_specs=pl.BlockSpec((tm, D), lambda i: (i, 0)),
        out_shape=jax.ShapeDtypeStruct((T, D), BF16),
        compiler_params=_params("parallel"),
    )(x, g)


def rmsnorm_bwd(x, g, dy, dres, name):
    T, D = x.shape
    tm = _tile(T, 512, 8)

    def body(x_ref, g_ref, dy_ref, dres_ref, dx_ref, dx16_ref, dg_ref):
        @pl.when(pl.program_id(0) == 0)
        def _():
            dg_ref[...] = jnp.zeros_like(dg_ref)

        xv = x_ref[...]
        r = lax.rsqrt(jnp.mean(xv * xv, axis=-1, keepdims=True) + EPS)
        xh = xv * r
        dyv = dy_ref[...]
        dg_ref[...] += jnp.sum(dyv * xh, axis=0, keepdims=True)
        dxh = dyv * g_ref[...]
        dx = dres_ref[...] + r * (dxh - xh * jnp.mean(dxh * xh, axis=-1, keepdims=True))
        dx_ref[...] = dx
        dx16_ref[...] = dx.astype(BF16)

    row = pl.BlockSpec((tm, D), lambda i: (i, 0))
    vec = pl.BlockSpec((1, D), lambda i: (0, 0))
    return pl.pallas_call(
        body, name=name, grid=(T // tm,), in_specs=[row, vec, row, row], out_specs=[row, row, vec],
        out_shape=[jax.ShapeDtypeStruct((T, D), F32), jax.ShapeDtypeStruct((T, D), BF16),
                   jax.ShapeDtypeStruct((1, D), F32)],
        compiler_params=_params("arbitrary"),
    )(x, g, dy, dres)


def final_loss(h, g, target):
    T, D = h.shape
    tm = _tile(T, 512, 8)

    def body(h_ref, g_ref, t_ref, loss_ref, dh_ref, dh16_ref, dg_ref):
        @pl.when(pl.program_id(0) == 0)
        def _():
            loss_ref[...] = jnp.zeros_like(loss_ref)
            dg_ref[...] = jnp.zeros_like(dg_ref)

        hv = h_ref[...]
        r = lax.rsqrt(jnp.mean(hv * hv, axis=-1, keepdims=True) + EPS)
        xh = hv * r
        err = xh * g_ref[...] - t_ref[...]
        loss_ref[...] += jnp.sum(err * err, axis=0, keepdims=True)
        dy = err * (1.0 / D)
        dg_ref[...] += jnp.sum(dy * xh, axis=0, keepdims=True)
        dxh = dy * g_ref[...]
        dh = r * (dxh - xh * jnp.mean(dxh * xh, axis=-1, keepdims=True))
        dh_ref[...] = dh
        dh16_ref[...] = dh.astype(BF16)

    row = pl.BlockSpec((tm, D), lambda i: (i, 0))
    vec = pl.BlockSpec((1, D), lambda i: (0, 0))
    return pl.pallas_call(
        body, name="final_loss", grid=(T // tm,), in_specs=[row, vec, row], out_specs=[vec, row, row, vec],
        out_shape=[jax.ShapeDtypeStruct((1, D), F32), jax.ShapeDtypeStruct((T, D), F32),
                   jax.ShapeDtypeStruct((T, D), BF16), jax.ShapeDtypeStruct((1, D), F32)],
        compiler_params=_params("arbitrary"),
    )(h, g, target)


def _shift_down(x, k):
    if k == 0:
        return x
    rows = lax.broadcasted_iota(jnp.int32, x.shape, 0)
    return jnp.where(rows >= k, pltpu.roll(x, k, 0), 0.0)


def _shift_up(x, k):
    if k == 0:
        return x
    s = x.shape[0]
    rows = lax.broadcasted_iota(jnp.int32, x.shape, 0)
    return jnp.where(rows < s - k, pltpu.roll(x, s - k, 0), 0.0)


def _conv_fwd(x, w_ref, kw):
    y = x * w_ref[kw - 1:kw, :]
    for i in range(kw - 1):
        y = y + _shift_down(x, kw - 1 - i) * w_ref[i:i + 1, :]
    return y


def _conv_bwd(x, dy, w_ref, kw):
    dx = dy * w_ref[kw - 1:kw, :]
    dws = []
    for i in range(kw - 1):
        dx = dx + _shift_up(dy, kw - 1 - i) * w_ref[i:i + 1, :]
        dws.append(jnp.sum(dy * _shift_down(x, kw - 1 - i), axis=0, keepdims=True))
    dws.append(jnp.sum(dy * x, axis=0, keepdims=True))
    return dx, dws


def ffn_gate_fwd(up_g, up_v, cw_g, cw_v, B, S):
    T, Fd = up_g.shape
    tc = _tile(Fd, 256)

    def body(g_ref, v_ref, wg_ref, wv_ref, o_ref):
        ug = _conv_fwd(g_ref[...], wg_ref, FFN_CONV)
        uv = _conv_fwd(v_ref[...], wv_ref, FFN_CONV)
        o_ref[...] = (ug * _sigmoid(ug) * uv).astype(BF16)

    blk = pl.BlockSpec((S, tc), lambda b, j: (b, j))
    wblk = pl.BlockSpec((FFN_CONV, tc), lambda b, j: (0, j))
    return pl.pallas_call(
        body, name="ffn_gate_fwd", grid=(B, Fd // tc), in_specs=[blk, blk, wblk, wblk], out_specs=blk,
        out_shape=jax.ShapeDtypeStruct((T, Fd), BF16), compiler_params=_params("parallel", "parallel"),
    )(up_g, up_v, cw_g, cw_v)


def ffn_gate_bwd(up_g, up_v, cw_g, cw_v, d_act, B, S):
    T, Fd = up_g.shape
    tc = _tile(Fd, 256)

    def body(g_ref, v_ref, wg_ref, wv_ref, da_ref, dg_ref, dv_ref, dwg_ref, dwv_ref):
        @pl.when(pl.program_id(1) == 0)
        def _():
            dwg_ref[...] = jnp.zeros_like(dwg_ref)
            dwv_ref[...] = jnp.zeros_like(dwv_ref)

        xg, xv = g_ref[...], v_ref[...]
        ug = _conv_fwd(xg, wg_ref, FFN_CONV)
        uv = _conv_fwd(xv, wv_ref, FFN_CONV)
        da = da_ref[...]
        sg = _sigmoid(ug)
        d_ug = da * uv * (sg + ug * sg * (1.0 - sg))
        d_uv = da * ug * sg
        dxg, dwg = _conv_bwd(xg, d_ug, wg_ref, FFN_CONV)
        dxv, dwv = _conv_bwd(xv, d_uv, wv_ref, FFN_CONV)
        dg_ref[...] = dxg.astype(BF16)
        dv_ref[...] = dxv.astype(BF16)
        for i in range(FFN_CONV):
            dwg_ref[i:i + 1, :] += dwg[i]
            dwv_ref[i:i + 1, :] += dwv[i]

    blk = pl.BlockSpec((S, tc), lambda j, b: (b, j))
    wblk = pl.BlockSpec((FFN_CONV, tc), lambda j, b: (0, j))
    return pl.pallas_call(
        body, name="ffn_gate_bwd", grid=(Fd // tc, B), in_specs=[blk, blk, wblk, wblk, blk],
        out_specs=[blk, blk, wblk, wblk],
        out_shape=[jax.ShapeDtypeStruct((T, Fd), BF16), jax.ShapeDtypeStruct((T, Fd), BF16),
                   jax.ShapeDtypeStruct((FFN_CONV, Fd), F32), jax.ShapeDtypeStruct((FFN_CONV, Fd), F32)],
        compiler_params=_params("parallel", "arbitrary"),
    )(up_g, up_v, cw_g, cw_v, d_act)


def merge_fwd(p_gates, bf_, bg_):
    T, D = bf_.shape
    tm = _tile(T, 512, 8)

    def body(gf_ref, gg_ref, bf_ref, bg_ref, o_ref):
        o_ref[...] = (_sigmoid(gf_ref[...]) * bf_ref[...] + _sigmoid(gg_ref[...]) * bg_ref[...]).astype(BF16)

    lo = pl.BlockSpec((tm, D), lambda i: (i, 0))
    hi = pl.BlockSpec((tm, D), lambda i: (i, 1))
    return pl.pallas_call(
        body, name="merge_fwd", grid=(T // tm,), in_specs=[lo, hi, lo, lo], out_specs=lo,
        out_shape=jax.ShapeDtypeStruct((T, D), BF16), compiler_params=_params("parallel"),
    )(p_gates, p_gates, bf_, bg_)


def merge_bwd(p_gates, bf_, bg_, dy):
    T, D = bf_.shape
    tm = _tile(T, 512, 8)

    def body(gf_ref, gg_ref, bf_ref, bg_ref, dy_ref, dbf_ref, dbg_ref, dgate_ref):
        d = dy_ref[...]
        sf, sg = _sigmoid(gf_ref[...]), _sigmoid(gg_ref[...])
        dbf_ref[...] = (d * sf).astype(BF16)
        dbg_ref[...] = (d * sg).astype(BF16)
        dgate_ref[:, 0:D] = (d * bf_ref[...] * sf * (1.0 - sf)).astype(BF16)
        dgate_ref[:, D:2 * D] = (d * bg_ref[...] * sg * (1.0 - sg)).astype(BF16)

    lo = pl.BlockSpec((tm, D), lambda i: (i, 0))
    hi = pl.BlockSpec((tm, D), lambda i: (i, 1))
    both = pl.BlockSpec((tm, 2 * D), lambda i: (i, 0))
    return pl.pallas_call(
        body, name="merge_bwd", grid=(T // tm,), in_specs=[lo, hi, lo, lo, lo], out_specs=[lo, lo, both],
        out_shape=[jax.ShapeDtypeStruct((T, D), BF16), jax.ShapeDtypeStruct((T, D), BF16),
                   jax.ShapeDtypeStruct((T, 2 * D), BF16)],
        compiler_params=_params("parallel"),
    )(p_gates, p_gates, bf_, bg_, dy)


def fox_fwd(p_fox, c_col, c_row, B, S, H):
    T = B * S
    t = _tile(S, ATT_TILE)
    nq = S // t
    scale = HEAD ** -0.5

    def body(q_ref, k_ref, v_ref, cq_ref, cr_ref, o_ref, o16_ref, lse_ref):
        i = pl.program_id(2)
        q = q_ref[...]
        cq = cq_ref[...]
        row = lax.broadcasted_iota(jnp.int32, (t, t), 0)
        col = lax.broadcasted_iota(jnp.int32, (t, t), 1)

        def step(j, carry):
            m, l, acc = carry
            off = pl.multiple_of(j * t, t)
            k = k_ref[pl.ds(off, t), :]
            v = v_ref[pl.ds(off, t), :]
            s = lax.dot_general(q, k, _NT, preferred_element_type=F32) * scale + (cq - cr_ref[:, pl.ds(off, t)])
            s = jnp.where((j < i) | (col <= row), s, NEG)
            m_new = jnp.maximum(m, jnp.max(s, axis=-1, keepdims=True))
            alpha = jnp.exp(m - m_new)
            p = jnp.exp(s - m_new)
            l = alpha * l + jnp.sum(p, axis=-1, keepdims=True)
            acc = alpha * acc + lax.dot_general(p.astype(BF16), v, _NN, preferred_element_type=F32)
            return m_new, l, acc

        m0 = jnp.full((t, 1), NEG, F32)
        m, l, acc = lax.fori_loop(0, i + 1, step, (m0, jnp.zeros((t, 1), F32), jnp.zeros((t, HEAD), F32)))
        o = acc / l
        o_ref[...] = o
        o16_ref[...] = o.astype(BF16)
        lse_ref[...] = m + jnp.log(l)

    return pl.pallas_call(
        body, name="fox_fwd", grid=(B, H, nq),
        in_specs=[pl.BlockSpec((t, HEAD), lambda b, h, i: (b * nq + i, 3 * h)),
                  pl.BlockSpec((S, HEAD), lambda b, h, i: (b, 3 * h + 1)),
                  pl.BlockSpec((S, HEAD), lambda b, h, i: (b, 3 * h + 2)),
                  pl.BlockSpec((None, None, t, 1), lambda b, h, i: (b, h, i, 0)),
                  pl.BlockSpec((None, None, 1, S), lambda b, h, i: (b, h, 0, 0))],
        out_specs=[pl.BlockSpec((t, HEAD), lambda b, h, i: (b * nq + i, h)),
                   pl.BlockSpec((t, HEAD), lambda b, h, i: (b * nq + i, h)),
                   pl.BlockSpec((None, None, t, 1), lambda b, h, i: (b, h, i, 0))],
        out_shape=[jax.ShapeDtypeStruct((T, H * HEAD), F32), jax.ShapeDtypeStruct((T, H * HEAD), BF16),
                   jax.ShapeDtypeStruct((B, H, S, 1), F32)],
        compiler_params=_params("parallel", "parallel", "arbitrary"),
    )(p_fox, p_fox, p_fox, c_col, c_row)


def fox_bwd(p_fox, c_col, c_row, o, lse, do, B, S, H):
    T = B * S
    t = _tile(S, ATT_TILE)
    n = S // t
    scale = HEAD ** -0.5

    def body(q_ref, k_ref, v_ref, cq_ref, cr_ref, o_ref, lse_ref, do_ref, dqkv_ref, dcq_ref, dcr_ref, dq_acc, delta_s):
        row = lax.broadcasted_iota(jnp.int32, (t, t), 0)
        col = lax.broadcasted_iota(jnp.int32, (t, t), 1)

        def prep(i, c):
            rows = pl.ds(pl.multiple_of(i * t, t), t)
            delta_s[rows, :] = jnp.sum(do_ref[rows, :] * o_ref[rows, :], axis=-1, keepdims=True)
            dq_acc[rows, :] = jnp.zeros((t, HEAD), F32)
            dcq_ref[rows, :] = jnp.zeros((t, 1), F32)
            return c

        lax.fori_loop(0, n, prep, 0)

        def kv_step(j, c):
            joff = pl.multiple_of(j * t, t)
            k = k_ref[pl.ds(joff, t), :]
            v = v_ref[pl.ds(joff, t), :]
            crj = cr_ref[:, pl.ds(joff, t)]

            def q_step(i, carry):
                dk, dv, dc = carry
                rows = pl.ds(pl.multiple_of(i * t, t), t)
                q = q_ref[rows, :]
                dob = do_ref[rows, :].astype(BF16)
                s = lax.dot_general(q, k, _NT, preferred_element_type=F32) * scale + (cq_ref[rows, :] - crj)
                s = jnp.where((i > j) | (col <= row), s, NEG)
                p = jnp.exp(s - lse_ref[rows, :])
                dp = lax.dot_general(dob, v, _NT, preferred_element_type=F32)
                ds = p * (dp - delta_s[rows, :])
                dsb = ds.astype(BF16)
                dv = dv + lax.dot_general(p.astype(BF16), dob, _TN, preferred_element_type=F32)
                dk = dk + lax.dot_general(dsb, q, _TN, preferred_element_type=F32)
                dq_acc[rows, :] += lax.dot_general(dsb, k, _NN, preferred_element_type=F32) * scale
                dc = dc + jnp.sum(ds, axis=0, keepdims=True)
                dcq_ref[rows, :] += jnp.sum(ds, axis=-1, keepdims=True)
                return dk, dv, dc

            z = jnp.zeros((t, HEAD), F32)
            dk, dv, dc = lax.fori_loop(j, n, q_step, (z, z, jnp.zeros((1, t), F32)))
            dqkv_ref[pl.ds(joff, t), HEAD:2 * HEAD] = (dk * scale).astype(BF16)
            dqkv_ref[pl.ds(joff, t), 2 * HEAD:3 * HEAD] = dv.astype(BF16)
            dcr_ref[:, pl.ds(joff, t)] = -dc
            return c

        lax.fori_loop(0, n, kv_step, 0)
        dqkv_ref[:, 0:HEAD] = dq_acc[...].astype(BF16)

    col_spec = pl.BlockSpec((None, None, S, 1), lambda b, h: (b, h, 0, 0))
    row_spec = pl.BlockSpec((None, None, 1, S), lambda b, h: (b, h, 0, 0))
    head = pl.BlockSpec((S, HEAD), lambda b, h: (b, h))
    return pl.pallas_call(
        body, name="fox_bwd", grid=(B, H),
        in_specs=[pl.BlockSpec((S, HEAD), lambda b, h: (b, 3 * h)),
                  pl.BlockSpec((S, HEAD), lambda b, h: (b, 3 * h + 1)),
                  pl.BlockSpec((S, HEAD), lambda b, h: (b, 3 * h + 2)),
                  col_spec, row_spec, head, col_spec, head],
        out_specs=[pl.BlockSpec((S, 3 * HEAD), lambda b, h: (b, h)), col_spec, row_spec],
        out_shape=[jax.ShapeDtypeStruct((T, 3 * H * HEAD), BF16), jax.ShapeDtypeStruct((B, H, S, 1), F32),
                   jax.ShapeDtypeStruct((B, H, 1, S), F32)],
        scratch_shapes=[pltpu.VMEM((S, HEAD), F32), pltpu.VMEM((S, 1), F32)],
        compiler_params=_params("parallel", "parallel"),
    )(p_fox, p_fox, p_fox, c_col, c_row, o, lse, do)


def _small_fn(x, b0, b1, H):
    S = x.shape[0]
    lane = lax.broadcasted_iota(jnp.int32, x.shape, 1)
    z = x + b0
    tail = jnp.log1p(jnp.exp(-jnp.abs(z)))
    softplus = jnp.maximum(z, 0.0) + tail
    logsig = -(jnp.maximum(-z, 0.0) + tail)
    g = -jnp.exp(b1) * softplus
    pre = jnp.where(lane < H, logsig, jnp.where(lane < 2 * H, g, 0.0))
    bl = _tile(S, 256, CHUNK)
    r = lax.broadcasted_iota(jnp.int32, (bl, bl), 0)
    c = lax.broadcasted_iota(jnp.int32, (bl, bl), 1)
    tri = (r >= c).astype(F32)
    tri_chunk = jnp.where((r >= c) & (jnp.right_shift(r, 6) == jnp.right_shift(c, 6)), 1.0, 0.0)
    carry = jnp.zeros((1, x.shape[1]), F32)
    parts = []
    for i in range(S // bl):
        blk = pre[i * bl:(i + 1) * bl, :]
        full = lax.dot_general(tri, blk, _NN, precision=EXACT, preferred_element_type=F32) + carry
        chunked = lax.dot_general(tri_chunk, blk, _NN, precision=EXACT, preferred_element_type=F32)
        parts.append(jnp.where(lane[:bl] < H, full, chunked))
        carry = carry + jnp.sum(blk, axis=0, keepdims=True)
    cum = parts[0] if len(parts) == 1 else jnp.concatenate(parts, axis=0)
    return jnp.where(lane < 2 * H, cum, jnp.where(lane < 3 * H, _sigmoid(x), 0.0))


def small_fwd(p_small, prm, B, S, H):
    T = B * S

    def body(x_ref, p_ref, o_ref):
        o_ref[...] = _small_fn(x_ref[...], p_ref[0:1, :], p_ref[1:2, :], H)

    blk = pl.BlockSpec((S, 128), lambda b: (b, 0))
    return pl.pallas_call(
        body, name="small_fwd", grid=(B,), in_specs=[blk, pl.BlockSpec((8, 128), lambda b: (0, 0))], out_specs=blk,
        out_shape=jax.ShapeDtypeStruct((T, 128), F32), compiler_params=_params("parallel"),
    )(p_small, prm)


def small_bwd(p_small, prm, d_out, B, S, H):
    T = B * S

    def body(x_ref, p_ref, d_ref, dx_ref, dp_ref):
        @pl.when(pl.program_id(0) == 0)
        def _():
            dp_ref[...] = jnp.zeros_like(dp_ref)

        _, vjp = jax.vjp(functools.partial(_small_fn, H=H), x_ref[...], p_ref[0:1, :], p_ref[1:2, :])
        dx, db0, db1 = vjp(d_ref[...])
        dx_ref[...] = dx.astype(BF16)
        dp_ref[0:1, :] += db0
        dp_ref[1:2, :] += db1

    blk = pl.BlockSpec((S, 128), lambda b: (b, 0))
    pblk = pl.BlockSpec((8, 128), lambda b: (0, 0))
    return pl.pallas_call(
        body, name="small_bwd", grid=(B,), in_specs=[blk, pblk, blk], out_specs=[blk, pblk],
        out_shape=[jax.ShapeDtypeStruct((T, 128), BF16), jax.ShapeDtypeStruct((8, 128), F32)],
        compiler_params=_params("arbitrary"),
    )(p_small, prm, d_out)


def gdn_prep_fwd(p_gqkv, cw, B, S, H):
    T = B * S

    def body(x_ref, w_ref, o_ref):
        y = _conv_fwd(x_ref[...], w_ref, GDN_CONV)
        a = y * _sigmoid(y)
        rs = lax.rsqrt(jnp.sum(a * a, axis=-1, keepdims=True) + EPS)
        is_qk = (pl.program_id(1) % 3) < 2
        o_ref[...] = a * jnp.where(is_qk, rs, 1.0)

    blk = pl.BlockSpec((S, HEAD), lambda b, n: (b, n))
    wblk = pl.BlockSpec((GDN_CONV, HEAD), lambda b, n: (0, n))
    return pl.pallas_call(
        body, name="gdn_prep_fwd", grid=(B, 3 * H), in_specs=[blk, wblk], out_specs=blk,
        out_shape=jax.ShapeDtypeStruct((T, 3 * H * HEAD), F32), compiler_params=_params("parallel", "parallel"),
    )(p_gqkv, cw)


def gdn_prep_bwd(p_gqkv, cw, d_out, B, S, H):
    T = B * S

    def body(x_ref, w_ref, d_ref, dx_ref, dw_ref):
        @pl.when(pl.program_id(1) == 0)
        def _():
            dw_ref[...] = jnp.zeros_like(dw_ref)

        x = x_ref[...]
        y = _conv_fwd(x, w_ref, GDN_CONV)
        sg = _sigmoid(y)
        a = y * sg
        rs = lax.rsqrt(jnp.sum(a * a, axis=-1, keepdims=True) + EPS)
        d = d_ref[...]
        out = a * rs
        da_qk = rs * (d - out * jnp.sum(d * out, axis=-1, keepdims=True))
        is_qk = (pl.program_id(0) % 3) < 2
        da = jnp.where(is_qk, da_qk, d)
        dy = da * (sg + y * sg * (1.0 - sg))
        dx, dws = _conv_bwd(x, dy, w_ref, GDN_CONV)
        dx_ref[...] = dx.astype(BF16)
        for i in range(GDN_CONV):
            dw_ref[i:i + 1, :] += dws[i]

    blk = pl.BlockSpec((S, HEAD), lambda n, b: (b, n))
    wblk = pl.BlockSpec((GDN_CONV, HEAD), lambda n, b: (0, n))
    return pl.pallas_call(
        body, name="gdn_prep_bwd", grid=(3 * H, B), in_specs=[blk, wblk, blk], out_specs=[blk, wblk],
        out_shape=[jax.ShapeDtypeStruct((T, 3 * H * HEAD), BF16), jax.ShapeDtypeStruct((GDN_CONV, 3 * H * HEAD), F32)],
        compiler_params=_params("parallel", "arbitrary"),
    )(p_gqkv, cw, d_out)


@jax.custom_vjp
def _given_inverse(a, t):
    return t


def _given_inverse_fwd(a, t):
    return t, t


def _given_inverse_bwd(t, g):
    x = _dg(t, g, _TN, True)
    return -_dg(x, t, _NT, True), jnp.zeros_like(t)


_given_inverse.defvjp(_given_inverse_fwd, _given_inverse_bwd)


def _to_col(row):
    r = lax.broadcasted_iota(jnp.int32, (CHUNK, CHUNK), 0)
    c = lax.broadcasted_iota(jnp.int32, (CHUNK, CHUNK), 1)
    return jnp.sum(jnp.where(r == c, row, 0.0), axis=1, keepdims=True)


def _intra_fn(k, v, beta_r, gcr, ops, t_known=None):
    n = len(k)
    r = lax.broadcasted_iota(jnp.int32, (CHUNK, CHUNK), 0)
    c = lax.broadcasted_iota(jnp.int32, (CHUNK, CHUNK), 1)
    beta = [_to_col(beta_r[i]) for i in range(n)]
    gcc = [_to_col(gcr[i]) for i in range(n)]
    decay = [jnp.exp(jnp.where(r > c, gcc[i] - gcr[i], NEG)) for i in range(n)]
    kb = [k[i] * beta[i] for i in range(n)]
    a = [ops.nt(kb[i], k[i]) * decay[i] for i in range(n)]
    if t_known is None:
        p = [-a[i] for i in range(n)]
        tm = [jnp.where(r == c, 1.0, 0.0) + p[i] for i in range(n)]
        for _ in range(5):
            p = [ops.nn(p[i], p[i], hi=True) for i in range(n)]
            tm = [tm[i] + ops.nn(tm[i], p[i], hi=True) for i in range(n)]
    else:
        tm = [_given_inverse(a[i], t_known[i]) for i in range(n)]
    u_hat = [ops.nn(tm[i], v[i] * beta[i], hi=True) for i in range(n)]
    w = [ops.nn(tm[i], kb[i] * jnp.exp(gcc[i]), hi=True) for i in range(n)]
    return tuple(u_hat), tuple(w), tuple(tm)


INTRA_NB = 8


def gdn_intra_fwd(qkvn, betar5, gcr5, B, S, H):
    T = B * S
    N = S // CHUNK
    nb = min(INTRA_NB, N)
    rows = nb * CHUNK
    ns = N // nb

    def body(k_ref, v_ref, b_ref, gr_ref, uh_ref, w_ref, t_ref):
        sls = [slice(ci * CHUNK, (ci + 1) * CHUNK) for ci in range(nb)]
        u_hat, w, tm = _intra_fn(tuple(k_ref[sl, :] for sl in sls), tuple(v_ref[sl, :] for sl in sls),
                                 tuple(b_ref[ci] for ci in range(nb)), tuple(gr_ref[ci] for ci in range(nb)), _RawOps)
        for ci, sl in enumerate(sls):
            uh_ref[sl, :] = u_hat[ci]
            w_ref[sl, :] = w[ci]
            t_ref[ci] = tm[ci]

    rowspec = pl.BlockSpec((None, None, nb, 1, CHUNK), lambda b, h, i: (b, h, i, 0, 0))
    sqspec = pl.BlockSpec((None, None, nb, CHUNK, CHUNK), lambda b, h, i: (b, h, i, 0, 0))
    out = pl.BlockSpec((rows, HEAD), lambda b, h, i: (b * ns + i, h))
    return pl.pallas_call(
        body, name="gdn_intra_fwd", grid=(B, H, ns),
        in_specs=[pl.BlockSpec((rows, HEAD), lambda b, h, i: (b * ns + i, 3 * h + 1)),
                  pl.BlockSpec((rows, HEAD), lambda b, h, i: (b * ns + i, 3 * h + 2)),
                  rowspec, rowspec],
        out_specs=[out, out, sqspec],
        out_shape=[jax.ShapeDtypeStruct((T, H * HEAD), F32), jax.ShapeDtypeStruct((T, H * HEAD), F32),
                   jax.ShapeDtypeStruct((B, H, N, CHUNK, CHUNK), F32)],
        compiler_params=_params("parallel", "parallel", "parallel"),
    )(qkvn, qkvn, betar5, gcr5)


def gdn_intra_bwd(qkvn, betar5, gcr5, t_inv, d_uh, d_w, dq_in, dk_in, B, S, H):
    T = B * S
    N = S // CHUNK
    nb = min(INTRA_NB, N)
    rows = nb * CHUNK
    ns = N // nb

    def body(k_ref, v_ref, b_ref, gr_ref, t_ref, duh_ref, dw_ref, dq_ref, dk_ref, o_ref, db_ref, dgr_ref):
        sls = [slice(ci * CHUNK, (ci + 1) * CHUNK) for ci in range(nb)]
        chunks = range(nb)
        _, vjp = jax.vjp(
            functools.partial(_intra_fn, ops=_DiffOps, t_known=tuple(t_ref[ci] for ci in chunks)),
            tuple(k_ref[sl, :] for sl in sls), tuple(v_ref[sl, :] for sl in sls), tuple(b_ref[ci] for ci in chunks),
            tuple(gr_ref[ci] for ci in chunks))
        zero = jnp.zeros((CHUNK, CHUNK), F32)
        dk, dv, db, dgr = vjp((tuple(duh_ref[sl, :] for sl in sls), tuple(dw_ref[sl, :] for sl in sls),
                               tuple(zero for _ in chunks)))
        for ci, sl in enumerate(sls):
            o_ref[sl, 0:HEAD] = dq_ref[sl, :]
            o_ref[sl, HEAD:2 * HEAD] = dk[ci] + dk_ref[sl, :]
            o_ref[sl, 2 * HEAD:3 * HEAD] = dv[ci]
            db_ref[ci] = db[ci]
            dgr_ref[ci] = dgr[ci]

    rowspec = pl.BlockSpec((None, None, nb, 1, CHUNK), lambda b, h, i: (b, h, i, 0, 0))
    sqspec = pl.BlockSpec((None, None, nb, CHUNK, CHUNK), lambda b, h, i: (b, h, i, 0, 0))
    head = pl.BlockSpec((rows, HEAD), lambda b, h, i: (b * ns + i, h))
    return pl.pallas_call(
        body, name="gdn_intra_bwd", grid=(B, H, ns),
        in_specs=[pl.BlockSpec((rows, HEAD), lambda b, h, i: (b * ns + i, 3 * h + 1)),
                  pl.BlockSpec((rows, HEAD), lambda b, h, i: (b * ns + i, 3 * h + 2)),
                  rowspec, rowspec, sqspec, head, head, head, head],
        out_specs=[pl.BlockSpec((rows, 3 * HEAD), lambda b, h, i: (b * ns + i, h)), rowspec, rowspec],
        out_shape=[jax.ShapeDtypeStruct((T, 3 * H * HEAD), F32),
                   jax.ShapeDtypeStruct((B, H, N, 1, CHUNK), F32), jax.ShapeDtypeStruct((B, H, N, 1, CHUNK), F32)],
        compiler_params=_params("parallel", "parallel", "parallel"),
    )(qkvn, qkvn, betar5, gcr5, t_inv, d_uh, d_w, dq_in, dk_in)


def _inter_fn(q, k, u_hat, w, gcr, state, ops):
    n = len(q)
    r = lax.broadcasted_iota(jnp.int32, (CHUNK, CHUNK), 0)
    c = lax.broadcasted_iota(jnp.int32, (CHUNK, CHUNK), 1)
    last = lax.broadcasted_iota(jnp.int32, (1, CHUNK), 1) == CHUNK - 1
    gcc = [_to_col(gcr[i]) for i in range(n)]
    gl = [jnp.sum(jnp.where(last, gcr[i], 0.0), axis=1, keepdims=True) for i in range(n)]
    decay = [jnp.exp(jnp.where(r >= c, gcc[i] - gcr[i], NEG)) for i in range(n)]
    qs = [q[i] * (HEAD ** -0.5) for i in range(n)]
    ws = [ops.nn(w[i], state[i]) for i in range(n)]
    qst = [ops.nn(qs[i] * jnp.exp(gcc[i]), state[i]) for i in range(n)]
    attn = [ops.nt(qs[i], k[i]) * decay[i] for i in range(n)]
    u = [u_hat[i] - ws[i] for i in range(n)]
    o = [qst[i] + ops.nn(attn[i], u[i]) for i in range(n)]
    kdu = [ops.tn(k[i] * jnp.exp(gl[i] - gcc[i]), u[i]) for i in range(n)]
    new_state = [state[i] * jnp.exp(gl[i]) + kdu[i] for i in range(n)]
    return tuple(o), tuple(new_state)


INTER_HEADS = 2


def _inter_specs(S, N, hp):
    qk = []
    for hh in range(hp):
        qk.append(pl.BlockSpec((S, HEAD), lambda b, g, hh=hh: (b, 3 * (hp * g + hh))))
        qk.append(pl.BlockSpec((S, HEAD), lambda b, g, hh=hh: (b, 3 * (hp * g + hh) + 1)))
    heads = pl.BlockSpec((S, hp * HEAD), lambda b, g: (b, g))
    rowspec = pl.BlockSpec((None, hp, N, 1, CHUNK), lambda b, g: (b, g, 0, 0, 0))
    stspec = pl.BlockSpec((None, hp, N, HEAD, HEAD), lambda b, g: (b, g, 0, 0, 0))
    return qk, heads, rowspec, stspec


def gdn_inter_fwd(qkvn, u_hat, w, gcr5, B, S, H):
    T = B * S
    N = S // CHUNK
    hp = INTER_HEADS if H % INTER_HEADS == 0 else 1
    hs = range(hp)

    def body(*refs):
        qk_refs, (uh_ref, w_ref, gr_ref, o_ref, st_ref, s_scr) = refs[:2 * hp], refs[2 * hp:]
        s_scr[...] = jnp.zeros_like(s_scr)

        def step(n, c):
            rows = pl.ds(pl.multiple_of(n * CHUNK, CHUNK), CHUNK)
            st = tuple(s_scr[hh] for hh in hs)
            for hh in hs:
                st_ref[hh, n] = st[hh]
            o, new = _inter_fn(tuple(qk_refs[2 * hh][rows, :] for hh in hs), tuple(qk_refs[2 * hh + 1][rows, :] for hh in hs),
                               tuple(uh_ref[rows, hh * HEAD:(hh + 1) * HEAD] for hh in hs),
                               tuple(w_ref[rows, hh * HEAD:(hh + 1) * HEAD] for hh in hs),
                               tuple(gr_ref[hh, n] for hh in hs), st, _RawOps)
            for hh in hs:
                o_ref[rows, hh * HEAD:(hh + 1) * HEAD] = o[hh]
                s_scr[hh] = new[hh]
            return c

        lax.fori_loop(0, N, step, 0)

    qk, heads, rowspec, stspec = _inter_specs(S, N, hp)
    return pl.pallas_call(
        body, name="gdn_inter_fwd", grid=(B, H // hp),
        in_specs=qk + [heads, heads, rowspec], out_specs=[heads, stspec],
        out_shape=[jax.ShapeDtypeStruct((T, H * HEAD), F32), jax.ShapeDtypeStruct((B, H, N, HEAD, HEAD), F32)],
        scratch_shapes=[pltpu.VMEM((hp, HEAD, HEAD), F32)],
        compiler_params=_params("parallel", "parallel"),
    )(*([qkvn] * (2 * hp)), u_hat, w, gcr5)


def gdn_inter_bwd(qkvn, u_hat, w, gcr5, states, d_o, B, S, H):
    T = B * S
    N = S // CHUNK
    hp = INTER_HEADS if H % INTER_HEADS == 0 else 1
    hs = range(hp)

    def body(*refs):
        qk_refs = refs[:2 * hp]
        uh_ref, w_ref, gr_ref, st_ref, do_ref, dq_ref, dk_ref, duh_ref, dw_ref, dgr_ref, ds_scr = refs[2 * hp:]
        ds_scr[...] = jnp.zeros_like(ds_scr)
        cols = [slice(hh * HEAD, (hh + 1) * HEAD) for hh in hs]

        def step(i, c):
            n = N - 1 - i
            rows = pl.ds(pl.multiple_of(n * CHUNK, CHUNK), CHUNK)
            _, vjp = jax.vjp(functools.partial(_inter_fn, ops=_DiffOps),
                             tuple(qk_refs[2 * hh][rows, :] for hh in hs), tuple(qk_refs[2 * hh + 1][rows, :] for hh in hs),
                             tuple(uh_ref[rows, cols[hh]] for hh in hs), tuple(w_ref[rows, cols[hh]] for hh in hs),
                             tuple(gr_ref[hh, n] for hh in hs), tuple(st_ref[hh, n] for hh in hs))
            dq, dk, duh, dw, dgr, ds = vjp((tuple(do_ref[rows, cols[hh]] for hh in hs), tuple(ds_scr[hh] for hh in hs)))
            for hh in hs:
                dq_ref[rows, cols[hh]] = dq[hh]
                dk_ref[rows, cols[hh]] = dk[hh]
                duh_ref[rows, cols[hh]] = duh[hh]
                dw_ref[rows, cols[hh]] = dw[hh]
                dgr_ref[hh, n] = dgr[hh]
                ds_scr[hh] = ds[hh]
            return c

        lax.fori_loop(0, N, step, 0)

    qk, heads, rowspec, stspec = _inter_specs(S, N, hp)
    hshape = jax.ShapeDtypeStruct((T, H * HEAD), F32)
    return pl.pallas_call(
        body, name="gdn_inter_bwd", grid=(B, H // hp),
        in_specs=qk + [heads, heads, rowspec, stspec, heads],
        out_specs=[heads, heads, heads, heads, rowspec],
        out_shape=[hshape, hshape, hshape, hshape, jax.ShapeDtypeStruct((B, H, N, 1, CHUNK), F32)],
        scratch_shapes=[pltpu.VMEM((hp, HEAD, HEAD), F32)],
        compiler_params=_params("parallel", "parallel"),
    )(*([qkvn] * (2 * hp)), u_hat, w, gcr5, states, d_o)


def gdn_post_fwd(o, p_gz, g, H):
    T = o.shape[0]
    tm = _tile(T, 1024, 8)

    def body(o_ref, z_ref, g_ref, y_ref):
        ov, z = o_ref[...], z_ref[...]
        r = lax.rsqrt(jnp.mean(ov * ov, axis=-1, keepdims=True) + EPS)
        y_ref[...] = (ov * r * g_ref[...] * z * _sigmoid(z)).astype(BF16)

    blk = pl.BlockSpec((tm, HEAD), lambda i, h: (i, h))
    return pl.pallas_call(
        body, name="gdn_post_fwd", grid=(T // tm, H), in_specs=[blk, blk, pl.BlockSpec((1, HEAD), lambda i, h: (0, 0))],
        out_specs=blk, out_shape=jax.ShapeDtypeStruct((T, H * HEAD), BF16),
        compiler_params=_params("parallel", "parallel"),
    )(o, p_gz, g)


def gdn_post_bwd(o, p_gz, g, dy, H):
    T = o.shape[0]
    tm = _tile(T, 1024, 8)

    def body(o_ref, z_ref, g_ref, dy_ref, do_ref, dz_ref, dg_ref):
        @pl.when((pl.program_id(0) == 0) & (pl.program_id(1) == 0))
        def _():
            dg_ref[...] = jnp.zeros_like(dg_ref)

        ov, z, d = o_ref[...], z_ref[...], dy_ref[...]
        r = lax.rsqrt(jnp.mean(ov * ov, axis=-1, keepdims=True) + EPS)
        xh = ov * r
        sg = _sigmoid(z)
        sz = z * sg
        d_n = d * sz
        dz_ref[...] = (d * xh * g_ref[...] * (sg + z * sg * (1.0 - sg))).astype(BF16)
        dg_ref[...] += jnp.sum(d_n * xh, axis=0, keepdims=True)
        dxh = d_n * g_ref[...]
        do_ref[...] = r * (dxh - xh * jnp.mean(dxh * xh, axis=-1, keepdims=True))

    blk = pl.BlockSpec((tm, HEAD), lambda i, h: (i, h))
    vec = pl.BlockSpec((1, HEAD), lambda i, h: (0, 0))
    return pl.pallas_call(
        body, name="gdn_post_bwd", grid=(T // tm, H), in_specs=[blk, blk, vec, blk], out_specs=[blk, blk, vec],
        out_shape=[jax.ShapeDtypeStruct((T, H * HEAD), F32), jax.ShapeDtypeStruct((T, H * HEAD), BF16),
                   jax.ShapeDtypeStruct((1, HEAD), F32)],
        compiler_params=_params("arbitrary", "arbitrary"),
    )(o, p_gz, g, dy)


def adamw(w, g, m, v, name):
    shape = w.shape
    w2, g2, m2, v2 = (a.reshape(-1, shape[-1]) for a in (w, g, m, v))
    R, C = w2.shape
    tr = _tile(R, 128, 8)

    def body(w_ref, g_ref, m_ref, v_ref, d_ref, nm_ref, nv_ref):
        gv = g_ref[...]
        nm = ADAM_B1 * m_ref[...] + (1.0 - ADAM_B1) * gv
        nv = ADAM_B2 * v_ref[...] + (1.0 - ADAM_B2) * (gv * gv)
        m_hat = nm / (1.0 - ADAM_B1 ** ADAM_STEP)
        v_hat = nv / (1.0 - ADAM_B2 ** ADAM_STEP)
        d_ref[...] = -ADAM_LR * (m_hat / (jnp.sqrt(v_hat) + ADAM_EPS) + ADAM_WD * w_ref[...])
        nm_ref[...] = nm
        nv_ref[...] = nv

    blk = pl.BlockSpec((tr, C), lambda i: (i, 0))
    sh = jax.ShapeDtypeStruct((R, C), F32)
    d, nm, nv = pl.pallas_call(
        body, name=name, grid=(R // tr,), in_specs=[blk] * 4, out_specs=[blk] * 3, out_shape=[sh] * 3,
        compiler_params=_params("parallel"),
    )(w2, g2, m2, v2)
    return d.reshape(shape), nm.reshape(shape), nv.reshape(shape)


def _place():
    x, y, c = lax.axis_index("x"), lax.axis_index("y"), lax.axis_index("c")
    chips = [(1 - x, y), (x, 1 - y), (1 - x, 1 - y)]
    return x, y, c, chips


_HBM = pl.BlockSpec(memory_space=pltpu.HBM)


def allgather_weights(packed):
    _, Rh, _ = packed.shape

    def body(in_ref, out_ref, send_sems, recv_sems):
        x, y, c, chips = _place()
        me_s = 2 * x + y
        sibling = (x, y, 1 - c)

        def copy(k, shard, half, to, src=None):
            dst = out_ref.at[shard, half]
            return pltpu.make_async_remote_copy(src_ref=dst if src is None else src, dst_ref=dst,
                                                send_sem=send_sems.at[k], recv_sem=recv_sems.at[k],
                                                device_id=to, device_id_type=MESH)

        first = [copy(j, me_s, c, (*chip, c), src=in_ref.at[c]) for j, chip in enumerate(chips)]
        for cp in first:
            cp.start()
        passed = [copy(3 + j, 2 * chip[0] + chip[1], c, sibling) for j, chip in enumerate(chips)]
        for j, chip in enumerate(chips):
            copy(j, 2 * chip[0] + chip[1], c, (x, y, c)).wait_recv()
            passed[j].start()
        for j, chip in enumerate(chips):
            copy(3 + j, 2 * chip[0] + chip[1], 1 - c, (x, y, c)).wait_recv()
        for cp in first + passed:
            cp.wait_send()

    return pl.pallas_call(
        body, name="allgather_weights", in_specs=[_HBM], out_specs=_HBM,
        out_shape=jax.ShapeDtypeStruct((N_CHIP, 2, Rh, ROW), packed.dtype),
        scratch_shapes=[pltpu.SemaphoreType.DMA((6,)), pltpu.SemaphoreType.DMA((6,))],
    )(packed)


def exchange_halves(g):
    _, ns, Rh, _ = g.shape

    def body(g_ref, out_ref, send_sem, recv_sem):
        x, y, c, _ = _place()
        cp = pltpu.make_async_remote_copy(src_ref=g_ref.at[1 - c], dst_ref=out_ref, send_sem=send_sem, recv_sem=recv_sem,
                                          device_id=(x, y, 1 - c), device_id_type=MESH)
        cp.start()
        cp.wait()

    return pl.pallas_call(
        body, name="exchange_halves", in_specs=[_HBM], out_specs=_HBM,
        out_shape=jax.ShapeDtypeStruct((ns, Rh, ROW), F32),
        scratch_shapes=[pltpu.SemaphoreType.DMA, pltpu.SemaphoreType.DMA],
    )(g)


def add_halves(g, got, idx):
    _, ns, Rh, _ = g.shape
    tr = _tile(Rh, 512, 128)

    def body(idx_ref, a_ref, b_ref, o32_ref, o16_ref):
        s = a_ref[...] + b_ref[...]
        o32_ref[...] = s
        o16_ref[...] = s.astype(BF16)

    blk = pl.BlockSpec((None, tr, ROW), lambda s, i, idx_ref: (s, i, 0))
    return pl.pallas_call(
        body, name="add_halves",
        grid_spec=pltpu.PrefetchScalarGridSpec(
            num_scalar_prefetch=1, grid=(ns, Rh // tr),
            in_specs=[pl.BlockSpec((None, None, tr, ROW), lambda s, i, idx_ref: (idx_ref[0], s, i, 0)), blk],
            out_specs=[blk, blk]),
        out_shape=[jax.ShapeDtypeStruct((ns, Rh, ROW), F32), jax.ShapeDtypeStruct((ns, Rh, ROW), BF16)],
        compiler_params=_params("parallel", "parallel"),
    )(idx, g, got)


def scatter_chips(b16):
    ns, Rh, _ = b16.shape

    def body(in_ref, out_ref, send_sems, recv_sems):
        x, y, c, chips = _place()
        me_s = 2 * x + y
        cps = []
        for j, chip in enumerate(chips):
            cps.append(pltpu.make_async_remote_copy(
                src_ref=in_ref.at[2 * chip[0] + chip[1]], dst_ref=out_ref.at[me_s],
                send_sem=send_sems.at[j], recv_sem=recv_sems.at[j], device_id=(*chip, c), device_id_type=MESH))
        for cp in cps:
            cp.start()
        for j, chip in enumerate(chips):
            pltpu.make_async_remote_copy(
                src_ref=in_ref.at[me_s], dst_ref=out_ref.at[2 * chip[0] + chip[1]],
                send_sem=send_sems.at[j], recv_sem=recv_sems.at[j], device_id=(x, y, c), device_id_type=MESH).wait_recv()
        for cp in cps:
            cp.wait_send()

    return pl.pallas_call(
        body, name="scatter_chips", in_specs=[_HBM], out_specs=_HBM,
        out_shape=jax.ShapeDtypeStruct((ns, Rh, ROW), BF16),
        scratch_shapes=[pltpu.SemaphoreType.DMA((3,)), pltpu.SemaphoreType.DMA((3,))],
    )(b16)


def add_chips(a32, got16, idx):
    ns, Rh, _ = a32.shape
    tr = _tile(Rh, 512, 128)

    def body(idx_ref, a_ref, r1_ref, r2_ref, r3_ref, o_ref):
        o_ref[...] = ((a_ref[...] + r1_ref[...].astype(F32)) + r2_ref[...].astype(F32)) + r3_ref[...].astype(F32)

    def slab(k):
        return pl.BlockSpec((None, tr, ROW), lambda i, idx_ref: ((idx_ref[1] + k) % ns, i, 0))

    return pl.pallas_call(
        body, name="add_chips",
        grid_spec=pltpu.PrefetchScalarGridSpec(
            num_scalar_prefetch=1, grid=(Rh // tr,), in_specs=[slab(0), slab(1), slab(2), slab(3)],
            out_specs=pl.BlockSpec((tr, ROW), lambda i, idx_ref: (i, 0))),
        out_shape=jax.ShapeDtypeStruct((Rh, ROW), F32),
        compiler_params=_params("parallel"),
    )(idx, a32, got16, got16, got16)


def share_halves(half):
    Rh, _ = half.shape

    def body(in_ref, out_ref, send_sem, recv_sem):
        x, y, c, _ = _place()
        cp = pltpu.make_async_remote_copy(src_ref=in_ref, dst_ref=out_ref, send_sem=send_sem, recv_sem=recv_sem,
                                          device_id=(x, y, 1 - c), device_id_type=MESH)
        cp.start()
        cp.wait()

    return pl.pallas_call(
        body, name="share_halves", in_specs=[_HBM], out_specs=_HBM,
        out_shape=jax.ShapeDtypeStruct((Rh, ROW), F32),
        scratch_shapes=[pltpu.SemaphoreType.DMA, pltpu.SemaphoreType.DMA],
    )(half)


def allreduce_small(v):
    R, _ = v.shape

    def body(in_ref, out_ref, slots, send_sems, recv_sems):
        x, y, c, _ = _place()
        me = 4 * x + 2 * y + c
        slots[me] = in_ref[...]
        cps = []
        for k in range(1, N_DEV):
            to = (x ^ (k >> 2), y ^ ((k >> 1) & 1), c ^ (k & 1))
            cps.append(pltpu.make_async_remote_copy(src_ref=in_ref, dst_ref=slots.at[me], send_sem=send_sems.at[k - 1],
                                                    recv_sem=recv_sems.at[k - 1], device_id=to, device_id_type=MESH))
        for cp in cps:
            cp.start()
        for k in range(1, N_DEV):
            frm = 4 * (x ^ (k >> 2)) + 2 * (y ^ ((k >> 1) & 1)) + (c ^ (k & 1))
            pltpu.make_async_remote_copy(src_ref=in_ref, dst_ref=slots.at[frm], send_sem=send_sems.at[k - 1],
                                         recv_sem=recv_sems.at[k - 1], device_id=(x, y, c), device_id_type=MESH).wait_recv()
        for cp in cps:
            cp.wait_send()
        acc = slots[0]
        for d in range(1, N_DEV):
            acc = acc + slots[d]
        out_ref[...] = acc

    vm = pl.BlockSpec(memory_space=pltpu.VMEM)
    return pl.pallas_call(
        body, name="allreduce_small", in_specs=[vm], out_specs=vm, out_shape=jax.ShapeDtypeStruct((R, ROW), F32),
        scratch_shapes=[pltpu.VMEM((N_DEV, R, ROW), F32), pltpu.SemaphoreType.DMA((N_DEV - 1,)),
                        pltpu.SemaphoreType.DMA((N_DEV - 1,))],
    )(v)


def _rows_of(n):
    return -(-n // (16 * ROW)) * 16


def _pack_rows(items, total_rows, dtype):
    parts = []
    used = 0
    for a in items:
        flat = a.reshape(-1)
        r = _rows_of(flat.shape[0])
        flat = jnp.pad(flat, (0, r * ROW - flat.shape[0]))
        parts.append(flat.reshape(r, ROW))
        used += r
    if total_rows > used:
        parts.append(jnp.zeros((total_rows - used, ROW), dtype))
    return jnp.concatenate(parts, axis=0)


def _unpack_rows(buf, shapes):
    lead = buf.shape[:-2]
    out = []
    off = 0
    for shp in shapes:
        n = math.prod(shp)
        r = _rows_of(n)
        piece = buf[..., off:off + r, :].reshape(*lead, r * ROW)[..., :n].reshape(*lead, *shp)
        out.append(piece)
        off += r
    return out


def _interleave_heads(w, H):
    lead = w.shape[:-1]
    return w.reshape(*lead, 3, H, HEAD).swapaxes(-3, -2).reshape(*lead, 3 * H * HEAD)


def _deinterleave_heads(w, H):
    lead = w.shape[:-1]
    return w.reshape(*lead, H, 3, HEAD).swapaxes(-3, -2).reshape(*lead, 3 * H * HEAD)


def kernel(x, norm_mix, w_in, fox_f_bias, gdn_conv_w, gdn_a_log, gdn_dt_bias, gdn_norm, w_branch_fox, w_branch_gdn, w_out, norm_ffn, w_up, ffn_conv_w, w_down, norm_final, loss_target, m_norm_mix, m_w_in, m_fox_f_bias, m_gdn_conv_w, m_gdn_a_log, m_gdn_dt_bias, m_gdn_norm, m_w_branch_fox, m_w_branch_gdn, m_w_out, m_norm_ffn, m_w_up, m_ffn_conv_w, m_w_down, m_norm_final, v_norm_mix, v_w_in, v_fox_f_bias, v_gdn_conv_w, v_gdn_a_log, v_gdn_dt_bias, v_gdn_norm, v_w_branch_fox, v_w_branch_gdn, v_w_out, v_norm_ffn, v_w_up, v_ffn_conv_w, v_w_down, v_norm_final):
    B, S, D = x.shape
    T = B * S
    H = D // HEAD
    N = S // CHUNK
    FF = w_down.shape[1] * N_CHIP
    d_in = 9 * D + 3 * H
    assert w_in.shape[2] * N_CHIP == d_in and 3 * H <= 128

    cidx = lax.axis_index("c").astype(jnp.int32)
    sidx = (2 * lax.axis_index("x") + lax.axis_index("y")).astype(jnp.int32)
    idx = jnp.stack([cidx, sidx])

    big = [w_in[0], w_branch_fox[0], w_branch_gdn[0], w_out[0], w_up[0], w_down[0]]
    convs = [gdn_conv_w[0], ffn_conv_w[0]]
    big_shapes = [a.shape for a in big]
    conv_bits_shapes = [a.shape + (2,) for a in convs]
    conv_bits = [lax.bitcast_convert_type(a, BF16) for a in convs]
    n_rows = sum(_rows_of(math.prod(s)) for s in big_shapes + conv_bits_shapes)
    Rh = -(-n_rows // 256) * 128
    packed = _pack_rows([a.astype(BF16) for a in big] + conv_bits, 2 * Rh, BF16).reshape(2, Rh, ROW)
    gathered = lax.dynamic_update_slice(allgather_weights(packed), packed[None], (sidx, 0, 0, 0))
    gathered = gathered.reshape(N_CHIP, 2 * Rh, ROW)
    parts = _unpack_rows(gathered, big_shapes + conv_bits_shapes)
    cat_cols = lambda p: jnp.concatenate([p[i] for i in range(N_CHIP)], axis=-1)
    cat_rows = lambda p: p.reshape(-1, p.shape[-1])
    W_in = cat_cols(parts[0])
    W_bf, W_bg, W_out = cat_rows(parts[1]), cat_rows(parts[2]), cat_rows(parts[3])
    W_up, W_down = cat_cols(parts[4]), cat_rows(parts[5])
    gconv = cat_cols(lax.bitcast_convert_type(parts[6], F32))
    fconv = cat_cols(lax.bitcast_convert_type(parts[7], F32))

    o1, o2 = 3 * D, 3 * D + H
    o3, o4, o5, o6 = o2 + 3 * D, o2 + 3 * D + H, o2 + 3 * D + 2 * H, o2 + 4 * D + 2 * H
    W_fox = _interleave_heads(W_in[:, :o1], H)
    W_gqkv = _interleave_heads(W_in[:, o2:o3], H)
    W_gz = W_in[:, o5:o6]
    W_gates = W_in[:, o6:]
    W_small = jnp.concatenate([W_in[:, o1:o2], W_in[:, o3:o5], jnp.zeros((D, 128 - 3 * H), BF16)], axis=1)
    gconv_i = _interleave_heads(gconv, H)
    W_up_g, W_up_v = W_up[:, :FF], W_up[:, FF:]
    fconv_g, fconv_v = fconv[:, :FF], fconv[:, FF:]
    prm = jnp.zeros((8, 128), F32)
    prm = prm.at[0, 0:H].set(fox_f_bias[0]).at[0, H:2 * H].set(gdn_dt_bias[0]).at[1, H:2 * H].set(gdn_a_log[0])

    x2 = x.reshape(T, D)
    tgt = loss_target.reshape(T, D)

    hn1 = rmsnorm_fwd(x2, norm_mix, "rmsnorm_mix")
    p_fox = matmul(hn1, W_fox, "nn", "proj_fox", out_dtype=BF16)
    p_gqkv = matmul(hn1, W_gqkv, "nn", "proj_gqkv")
    p_gz = matmul(hn1, W_gz, "nn", "proj_gz")
    p_gates = matmul(hn1, W_gates, "nn", "proj_gates")
    p_small = matmul(hn1, W_small, "nn", "proj_small")

    sm = small_fwd(p_small, prm, B, S, H)
    heads = lambda a: a.reshape(B, S, H).transpose(0, 2, 1)
    c_bhs, gc_bhs, beta_bhs = heads(sm[:, 0:H]), heads(sm[:, H:2 * H]), heads(sm[:, 2 * H:3 * H])
    c_col, c_row = c_bhs[..., None], c_bhs[:, :, None, :]
    gcr5 = gc_bhs.reshape(B, H, N, 1, CHUNK)
    betar5 = beta_bhs.reshape(B, H, N, 1, CHUNK)

    o_fox, o_fox16, lse = fox_fwd(p_fox, c_col, c_row, B, S, H)
    qkvn = gdn_prep_fwd(p_gqkv, gconv_i, B, S, H)
    u_hat, w_t, t_inv = gdn_intra_fwd(qkvn, betar5, gcr5, B, S, H)
    o_gdn, states = gdn_inter_fwd(qkvn, u_hat, w_t, gcr5, B, S, H)
    y_gdn = gdn_post_fwd(o_gdn, p_gz, gdn_norm, H)
    bf_ = matmul(o_fox16, W_bf, "nn", "branch_fox")
    bg_ = matmul(y_gdn, W_bg, "nn", "branch_gdn")
    y = merge_fwd(p_gates, bf_, bg_)
    h1 = matmul(y, W_out, "nn", "out_proj", add=x2)
    hn2 = rmsnorm_fwd(h1, norm_ffn, "rmsnorm_ffn")
    up_g = matmul(hn2, W_up_g, "nn", "up_gate")
    up_v = matmul(hn2, W_up_v, "nn", "up_val")
    act = ffn_gate_fwd(up_g, up_v, fconv_g, fconv_v, B, S)
    h2 = matmul(act, W_down, "nn", "down_proj", add=h1)
    loss_cols, dh2, dh2_16, d_norm_final = final_loss(h2, norm_final.reshape(1, D), tgt)
    loss = lax.psum(0.5 * jnp.sum(loss_cols) / D, ("x", "y", "c"))

    d_act = matmul(dh2_16, W_down, "nt", "d_act")
    dW_down = matmul(act, dh2_16, "tn", "dw_down")
    d_upg, d_upv, d_fconv_g, d_fconv_v = ffn_gate_bwd(up_g, up_v, fconv_g, fconv_v, d_act, B, S)
    d_hn2 = matmul(d_upg, W_up_g, "nt", "d_hn2_g")
    d_hn2 = matmul(d_upv, W_up_v, "nt", "d_hn2_v", add=d_hn2)
    dW_up = jnp.concatenate([matmul(hn2, d_upg, "tn", "dw_up_g"), matmul(hn2, d_upv, "tn", "dw_up_v")], axis=1)
    dh1, dh1_16, d_norm_ffn = rmsnorm_bwd(h1, norm_ffn, d_hn2, dh2, "rmsnorm_ffn_bwd")
    d_y = matmul(dh1_16, W_out, "nt", "d_y")
    dW_out = matmul(y, dh1_16, "tn", "dw_out")
    d_bf, d_bg, d_gates = merge_bwd(p_gates, bf_, bg_, d_y)
    d_ofox = matmul(d_bf, W_bf, "nt", "d_ofox")
    dW_bf = matmul(o_fox16, d_bf, "tn", "dw_bf")
    d_ygdn = matmul(d_bg, W_bg, "nt", "d_ygdn")
    dW_bg = matmul(y_gdn, d_bg, "tn", "dw_bg")

    d_pfox, d_ccol, d_crow = fox_bwd(p_fox, c_col, c_row, o_fox, lse, d_ofox, B, S, H)

    d_ogdn, d_gz, d_gdn_norm = gdn_post_bwd(o_gdn, p_gz, gdn_norm, d_ygdn, H)
    dq_i, dk_i, d_uh, d_wt, dgcr_a = gdn_inter_bwd(qkvn, u_hat, w_t, gcr5, states, d_ogdn, B, S, H)
    d_qkvn, d_betar5, dgcr_b = gdn_intra_bwd(qkvn, betar5, gcr5, t_inv, d_uh, d_wt, dq_i, dk_i, B, S, H)
    d_pgqkv, d_gconv_i = gdn_prep_bwd(p_gqkv, gconv_i, d_qkvn, B, S, H)

    tokens = lambda a: a.reshape(B, H, S).transpose(0, 2, 1).reshape(T, H)
    d_gc = (dgcr_a + dgcr_b).reshape(B, H, S)
    d_sm = jnp.concatenate([tokens(d_ccol.reshape(B, H, S) + d_crow.reshape(B, H, S)), tokens(d_gc), tokens(d_betar5.reshape(B, H, S)),
                            jnp.zeros((T, 128 - 3 * H), F32)], axis=1)
    d_psmall, d_prm = small_bwd(p_small, prm, d_sm, B, S, H)

    d_hn1 = matmul(d_pfox, W_fox, "nt", "d_hn1_fox")
    d_hn1 = matmul(d_pgqkv, W_gqkv, "nt", "d_hn1_gqkv", add=d_hn1)
    d_hn1 = matmul(d_gz, W_gz, "nt", "d_hn1_gz", add=d_hn1)
    d_hn1 = matmul(d_gates, W_gates, "nt", "d_hn1_gates", add=d_hn1)
    d_hn1 = matmul(d_psmall, W_small, "nt", "d_hn1_small", add=d_hn1)
    dW_fox = matmul(hn1, d_pfox, "tn", "dw_fox")
    dW_gqkv = matmul(hn1, d_pgqkv, "tn", "dw_gqkv")
    dW_gz = matmul(hn1, d_gz, "tn", "dw_gz")
    dW_gates = matmul(hn1, d_gates, "tn", "dw_gates")
    dW_small = matmul(hn1, d_psmall, "tn", "dw_small")
    grad_x, _, d_norm_mix = rmsnorm_bwd(x2, norm_mix, d_hn1, dh1, "rmsnorm_mix_bwd")

    dW_in = jnp.concatenate([_deinterleave_heads(dW_fox, H), dW_small[:, 0:H], _deinterleave_heads(dW_gqkv, H),
                             dW_small[:, H:3 * H], dW_gz, dW_gates], axis=1)
    d_gconv = _deinterleave_heads(d_gconv_i, H)
    d_fconv = jnp.concatenate([d_fconv_g, d_fconv_v], axis=1)

    col_shard = lambda g, s: g[:, s * (g.shape[1] // N_CHIP):(s + 1) * (g.shape[1] // N_CHIP)]
    row_shard = lambda g, s: g[s * (g.shape[0] // N_CHIP):(s + 1) * (g.shape[0] // N_CHIP)]
    shard_items = lambda s: [col_shard(dW_in, s), row_shard(dW_bf, s), row_shard(dW_bg, s), row_shard(dW_out, s),
                             col_shard(dW_up, s), row_shard(dW_down, s), col_shard(d_gconv, s), col_shard(d_fconv, s)]
    g_shapes = [a.shape for a in shard_items(0)]
    assert sum(_rows_of(math.prod(s)) for s in g_shapes) <= 2 * Rh
    gpack = jnp.stack([_pack_rows(shard_items(s), 2 * Rh, F32).reshape(2, Rh, ROW) for s in range(N_CHIP)], axis=1)
    got = exchange_halves(gpack)
    sum32, sum16 = add_halves(gpack, got, idx)
    got16 = scatter_chips(sum16)
    half = add_chips(sum32, got16, idx)
    other = share_halves(half)
    full = jnp.concatenate([jnp.where(cidx == 0, half, other), jnp.where(cidx == 0, other, half)], axis=0)
    g_w_in, g_bf, g_bg, g_out, g_up, g_down, g_gconv, g_fconv = _unpack_rows(full, g_shapes)

    small_items = [d_norm_mix, d_norm_ffn, d_norm_final, d_gdn_norm, d_prm]
    small_shapes = [a.shape for a in small_items]
    sv = allreduce_small(_pack_rows(small_items, 8, F32))
    g_norm_mix, g_norm_ffn, g_norm_final, g_gdn_norm, g_prm = _unpack_rows(sv, small_shapes)
    g_norm_final = g_norm_final.reshape(D)
    g_fbias, g_dtb, g_alog = g_prm[0:1, 0:H], g_prm[0:1, H:2 * H], g_prm[1:2, H:2 * H]

    names = ["norm_mix", "w_in", "fox_f_bias", "gdn_conv_w", "gdn_a_log", "gdn_dt_bias", "gdn_norm", "w_branch_fox",
             "w_branch_gdn", "w_out", "norm_ffn", "w_up", "ffn_conv_w", "w_down", "norm_final"]
    ws = [norm_mix, w_in, fox_f_bias, gdn_conv_w, gdn_a_log, gdn_dt_bias, gdn_norm, w_branch_fox, w_branch_gdn, w_out,
          norm_ffn, w_up, ffn_conv_w, w_down, norm_final]
    ms = [m_norm_mix, m_w_in, m_fox_f_bias, m_gdn_conv_w, m_gdn_a_log, m_gdn_dt_bias, m_gdn_norm, m_w_branch_fox,
          m_w_branch_gdn, m_w_out, m_norm_ffn, m_w_up, m_ffn_conv_w, m_w_down, m_norm_final]
    vs = [v_norm_mix, v_w_in, v_fox_f_bias, v_gdn_conv_w, v_gdn_a_log, v_gdn_dt_bias, v_gdn_norm, v_w_branch_fox,
          v_w_branch_gdn, v_w_out, v_norm_ffn, v_w_up, v_ffn_conv_w, v_w_down, v_norm_final]
    gs = [g_norm_mix, g_w_in, g_fbias, g_gconv, g_alog, g_dtb, g_gdn_norm, g_bf, g_bg, g_out, g_norm_ffn, g_up,
          g_fconv, g_down, g_norm_final]
    gs = [g.reshape(w.shape) for g, w in zip(gs, ws)]
    deltas, new_ms, new_vs = [], [], []
    for nm, w, g, m, v in zip(names, ws, gs, ms, vs):
        if w.ndim == 1:
            d, a, b = adamw(w.reshape(1, -1), g.reshape(1, -1), m.reshape(1, -1), v.reshape(1, -1), "adamw_" + nm)
            d, a, b = d.reshape(w.shape), a.reshape(w.shape), b.reshape(w.shape)
        else:
            d, a, b = adamw(w, g, m, v, "adamw_" + nm)
        deltas.append(d)
        new_ms.append(a)
        new_vs.append(b)

    return (loss, grad_x.reshape(B, S, D), *gs, *deltas, *new_ms, *new_vs)
```

```python
import functools
import math

import jax
import jax.numpy as jnp
from jax import lax
from jax.experimental import pallas as pl
from jax.experimental.pallas import tpu as pltpu

F32 = jnp.float32
BF16 = jnp.bfloat16
HEAD = 128
CHUNK = 64
GDN_CONV = 4
FFN_CONV = 3
EPS = 1e-6
NEG = -1e30
ROW = 1024
ATT_TILE = 512
MM_WEIGHT_TILE_BYTES = 8 << 20
N_CHIP = 4
N_DEV = 8
MESH = pl.DeviceIdType.MESH
HI = lax.Precision.HIGH
EXACT = lax.Precision.HIGHEST

ADAM_LR, ADAM_B1, ADAM_B2, ADAM_EPS, ADAM_WD, ADAM_STEP = 0.001, 0.9, 0.999, 1e-08, 0.01, 10


def _tile(n, cap, unit=128):
    best = None
    t = unit
    while t <= min(n, cap):
        if n % t == 0:
            best = t
        t += unit
    return best if best is not None else n


def _params(*sem):
    return pltpu.CompilerParams(dimension_semantics=sem)


_NN = (((1,), (0,)), ((), ()))
_NT = (((1,), (1,)), ((), ()))
_TN = (((0,), (0,)), ((), ()))


def _dg(a, b, dims, hi):
    if hi:
        return lax.dot_general(a, b, dims, precision=HI, preferred_element_type=F32)
    return lax.dot_general(a.astype(BF16), b.astype(BF16), dims, preferred_element_type=F32)


class _RawOps:
    @staticmethod
    def nn(a, b, hi=False):
        return _dg(a, b, _NN, hi)

    @staticmethod
    def nt(a, b, hi=False):
        return _dg(a, b, _NT, hi)

    @staticmethod
    def tn(a, b, hi=False):
        return _dg(a, b, _TN, hi)


def _make_diff_ops():
    def build(hi):
        @jax.custom_vjp
        def nn(a, b):
            return _dg(a, b, _NN, hi)

        nn.defvjp(lambda a, b: (_dg(a, b, _NN, hi), (a, b)),
                  lambda r, g: (_dg(g, r[1], _NT, hi), _dg(r[0], g, _TN, hi)))

        @jax.custom_vjp
        def nt(a, b):
            return _dg(a, b, _NT, hi)

        nt.defvjp(lambda a, b: (_dg(a, b, _NT, hi), (a, b)),
                  lambda r, g: (_dg(g, r[1], _NN, hi), _dg(g, r[0], _TN, hi)))

        @jax.custom_vjp
        def tn(a, b):
            return _dg(a, b, _TN, hi)

        tn.defvjp(lambda a, b: (_dg(a, b, _TN, hi), (a, b)),
                  lambda r, g: (_dg(r[1], g, _NT, hi), _dg(r[0], g, _NN, hi)))
        return nn, nt, tn

    lo, hi_ = build(False), build(True)

    class _DiffOps:
        @staticmethod
        def nn(a, b, hi=False):
            return (hi_ if hi else lo)[0](a, b)

        @staticmethod
        def nt(a, b, hi=False):
            return (hi_ if hi else lo)[1](a, b)

        @staticmethod
        def tn(a, b, hi=False):
            return (hi_ if hi else lo)[2](a, b)

    return _DiffOps


_DiffOps = _make_diff_ops()


def _sigmoid(x):
    return 1.0 / (1.0 + jnp.exp(-x))


def _mm_tile(n, pref):
    if n % pref == 0:
        return pref
    if n % 1408 == 0:
        return 1408
    return _tile(n, pref)


def matmul(a, b, mode, name, add=None, out_dtype=F32):
    if mode == "nn":
        (M, K), (K2, N) = a.shape, b.shape
    elif mode == "nt":
        (M, K), (N, K2) = a.shape, b.shape
    else:
        (K, M), (K2, N) = a.shape, b.shape
    assert K == K2, (name, a.shape, b.shape)
    tn = _mm_tile(N, 1024)
    if mode == "tn":
        tm = M if M <= 1408 else _mm_tile(M, 1408)
        tk = _mm_tile(K, 1024)
    else:
        tm = _mm_tile(M, 512)
        tk = K if K * tn * 2 <= MM_WEIGHT_TILE_BYTES else _mm_tile(K, 1024)
    nk = K // tk
    dims = {"nn": _NN, "nt": _NT, "tn": _TN}[mode]
    if mode == "tn":
        a_spec = pl.BlockSpec((tk, tm), lambda j, i, k: (k, i))
    else:
        a_spec = pl.BlockSpec((tm, tk), lambda j, i, k: (i, k))
    if mode == "nt":
        b_spec = pl.BlockSpec((tn, tk), lambda j, i, k: (j, k))
    else:
        b_spec = pl.BlockSpec((tk, tn), lambda j, i, k: (k, j))
    o_spec = pl.BlockSpec((tm, tn), lambda j, i, k: (i, j))
    has_add = add is not None

    def body(*refs):
        if has_add:
            a_ref, b_ref, add_ref, o_ref, acc_ref = refs
        else:
            a_ref, b_ref, o_ref, acc_ref = refs
        k = pl.program_id(2)

        @pl.when(k == 0)
        def _():
            acc_ref[...] = jnp.zeros_like(acc_ref)

        acc_ref[...] += lax.dot_general(a_ref[...].astype(BF16), b_ref[...].astype(BF16), dims,
                                        preferred_element_type=F32)

        @pl.when(k == nk - 1)
        def _():
            r = acc_ref[...]
            if has_add:
                r = r + add_ref[...]
            o_ref[...] = r.astype(out_dtype)

    in_specs = [a_spec, b_spec] + ([o_spec] if has_add else [])
    args = (a, b) + ((add,) if has_add else ())
    return pl.pallas_call(
        body, name=name, grid=(N // tn, M // tm, nk), in_specs=in_specs, out_specs=o_spec,
        out_shape=jax.ShapeDtypeStruct((M, N), out_dtype),
        scratch_shapes=[pltpu.VMEM((tm, tn), F32)],
        compiler_params=_params("parallel", "parallel", "arbitrary"),
    )(*args)


def rmsnorm_fwd(x, g, name):
    T, D = x.shape
    tm = _tile(T, 512, 8)

    def body(x_ref, g_ref, o_ref):
        xv = x_ref[...]
        r = lax.rsqrt(jnp.mean(xv * xv, axis=-1, keepdims=True) + EPS)
        o_ref[...] = (xv * r * g_ref[...]).astype(BF16)

    return pl.pallas_call(
        body, name=name, grid=(T // tm,),
        in_specs=[pl.BlockSpec((tm, D), lambda i: (i, 0)), pl.BlockSpec((1, D), lambda i: (0, 0))],
        out_specs=pl.BlockSpec((tm, D), lambda i: (i, 0)),
        out_shape=jax.ShapeDtypeStruct((T, D), BF16),
        compiler_params=_params("parallel"),
    )(x, g)


def rmsnorm_bwd(x, g, dy, dres, name):
    T, D = x.shape
    tm = _tile(T, 512, 8)

    def body(x_ref, g_ref, dy_ref, dres_ref, dx_ref, dx16_ref, dg_ref):
        @pl.when(pl.program_id(0) == 0)
        def _():
            dg_ref[...] = jnp.zeros_like(dg_ref)

        xv = x_ref[...]
        r = lax.rsqrt(jnp.mean(xv * xv, axis=-1, keepdims=True) + EPS)
        xh = xv * r
        dyv = dy_ref[...]
        dg_ref[...] += jnp.sum(dyv * xh, axis=0, keepdims=True)
        dxh = dyv * g_ref[...]
        dx = dres_ref[...] + r * (dxh - xh * jnp.mean(dxh * xh, axis=-1, keepdims=True))
        dx_ref[...] = dx
        dx16_ref[...] = dx.astype(BF16)

    row = pl.BlockSpec((tm, D), lambda i: (i, 0))
    vec = pl.BlockSpec((1, D), lambda i: (0, 0))
    return pl.pallas_call(
        body, name=name, grid=(T // tm,), in_specs=[row, vec, row, row], out_specs=[row, row, vec],
        out_shape=[jax.ShapeDtypeStruct((T, D), F32), jax.ShapeDtypeStruct((T, D), BF16),
                   jax.ShapeDtypeStruct((1, D), F32)],
        compiler_params=_params("arbitrary"),
    )(x, g, dy, dres)


def final_loss(h, g, target):
    T, D = h.shape
    tm = _tile(T, 512, 8)

    def body(h_ref, g_ref, t_ref, loss_ref, dh_ref, dh16_ref, dg_ref):
        @pl.when(pl.program_id(0) == 0)
        def _():
            loss_ref[...] = jnp.zeros_like(loss_ref)
            dg_ref[...] = jnp.zeros_like(dg_ref)

        hv = h_ref[...]
        r = lax.rsqrt(jnp.mean(hv * hv, axis=-1, keepdims=True) + EPS)
        xh = hv * r
        err = xh * g_ref[...] - t_ref[...]
        loss_ref[...] += jnp.sum(err * err, axis=0, keepdims=True)
        dy = err * (1.0 / D)
        dg_ref[...] += jnp.sum(dy * xh, axis=0, keepdims=True)
        dxh = dy * g_ref[...]
        dh = r * (dxh - xh * jnp.mean(dxh * xh, axis=-1, keepdims=True))
        dh_ref[...] = dh
        dh16_ref[...] = dh.astype(BF16)

    row = pl.BlockSpec((tm, D), lambda i: (i, 0))
    vec = pl.BlockSpec((1, D), lambda i: (0, 0))
    return pl.pallas_call(
        body, name="final_loss", grid=(T // tm,), in_specs=[row, vec, row], out_specs=[vec, row, row, vec],
        out_shape=[jax.ShapeDtypeStruct((1, D), F32), jax.ShapeDtypeStruct((T, D), F32),
                   jax.ShapeDtypeStruct((T, D), BF16), jax.ShapeDtypeStruct((1, D), F32)],
        compiler_params=_params("arbitrary"),
    )(h, g, target)


def _shift_down(x, k):
    if k == 0:
        return x
    rows = lax.broadcasted_iota(jnp.int32, x.shape, 0)
    return jnp.where(rows >= k, pltpu.roll(x, k, 0), 0.0)


def _shift_up(x, k):
    if k == 0:
        return x
    s = x.shape[0]
    rows = lax.broadcasted_iota(jnp.int32, x.shape, 0)
    return jnp.where(rows < s - k, pltpu.roll(x, s - k, 0), 0.0)


def _conv_fwd(x, w_ref, kw):
    y = x * w_ref[kw - 1:kw, :]
    for i in range(kw - 1):
        y = y + _shift_down(x, kw - 1 - i) * w_ref[i:i + 1, :]
    return y


def _conv_bwd(x, dy, w_ref, kw):
    dx = dy * w_ref[kw - 1:kw, :]
    dws = []
    for i in range(kw - 1):
        dx = dx + _shift_up(dy, kw - 1 - i) * w_ref[i:i + 1, :]
        dws.append(jnp.sum(dy * _shift_down(x, kw - 1 - i), axis=0, keepdims=True))
    dws.append(jnp.sum(dy * x, axis=0, keepdims=True))
    return dx, dws


def ffn_gate_fwd(up_g, up_v, cw_g, cw_v, B, S):
    T, Fd = up_g.shape
    tc = _tile(Fd, 256)

    def body(g_ref, v_ref, wg_ref, wv_ref, o_ref):
        ug = _conv_fwd(g_ref[...], wg_ref, FFN_CONV)
        uv = _conv_fwd(v_ref[...], wv_ref, FFN_CONV)
        o_ref[...] = (ug * _sigmoid(ug) * uv).astype(BF16)

    blk = pl.BlockSpec((S, tc), lambda b, j: (b, j))
    wblk = pl.BlockSpec((FFN_CONV, tc), lambda b, j: (0, j))
    return pl.pallas_call(
        body, name="ffn_gate_fwd", grid=(B, Fd // tc), in_specs=[blk, blk, wblk, wblk], out_specs=blk,
        out_shape=jax.ShapeDtypeStruct((T, Fd), BF16), compiler_params=_params("parallel", "parallel"),
    )(up_g, up_v, cw_g, cw_v)


def ffn_gate_bwd(up_g, up_v, cw_g, cw_v, d_act, B, S):
    T, Fd = up_g.shape
    tc = _tile(Fd, 256)

    def body(g_ref, v_ref, wg_ref, wv_ref, da_ref, dg_ref, dv_ref, dwg_ref, dwv_ref):
        @pl.when(pl.program_id(1) == 0)
        def _():
            dwg_ref[...] = jnp.zeros_like(dwg_ref)
            dwv_ref[...] = jnp.zeros_like(dwv_ref)

        xg, xv = g_ref[...], v_ref[...]
        ug = _conv_fwd(xg, wg_ref, FFN_CONV)
        uv = _conv_fwd(xv, wv_ref, FFN_CONV)
        da = da_ref[...]
        sg = _sigmoid(ug)
        d_ug = da * uv * (sg + ug * sg * (1.0 - sg))
        d_uv = da * ug * sg
        dxg, dwg = _conv_bwd(xg, d_ug, wg_ref, FFN_CONV)
        dxv, dwv = _conv_bwd(xv, d_uv, wv_ref, FFN_CONV)
        dg_ref[...] = dxg.astype(BF16)
        dv_ref[...] = dxv.astype(BF16)
        for i in range(FFN_CONV):
            dwg_ref[i:i + 1, :] += dwg[i]
            dwv_ref[i:i + 1, :] += dwv[i]

    blk = pl.BlockSpec((S, tc), lambda j, b: (b, j))
    wblk = pl.BlockSpec((FFN_CONV, tc), lambda j, b: (0, j))
    return pl.pallas_call(
        body, name="ffn_gate_bwd", grid=(Fd // tc, B), in_specs=[blk, blk, wblk, wblk, blk],
        out_specs=[blk, blk, wblk, wblk],
        out_shape=[jax.ShapeDtypeStruct((T, Fd), BF16), jax.ShapeDtypeStruct((T, Fd), BF16),
                   jax.ShapeDtypeStruct((FFN_CONV, Fd), F32), jax.ShapeDtypeStruct((FFN_CONV, Fd), F32)],
        compiler_params=_params("parallel", "arbitrary"),
    )(up_g, up_v, cw_g, cw_v, d_act)


def merge_fwd(p_gates, bf_, bg_):
    T, D = bf_.shape
    tm = _tile(T, 512, 8)

    def body(gf_ref, gg_ref, bf_ref, bg_ref, o_ref):
        o_ref[...] = (_sigmoid(gf_ref[...]) * bf_ref[...] + _sigmoid(gg_ref[...]) * bg_ref[...]).astype(BF16)

    lo = pl.BlockSpec((tm, D), lambda i: (i, 0))
    hi = pl.BlockSpec((tm, D), lambda i: (i, 1))
    return pl.pallas_call(
        body, name="merge_fwd", grid=(T // tm,), in_specs=[lo, hi, lo, lo], out_specs=lo,
        out_shape=jax.ShapeDtypeStruct((T, D), BF16), compiler_params=_params("parallel"),
    )(p_gates, p_gates, bf_, bg_)


def merge_bwd(p_gates, bf_, bg_, dy):
    T, D = bf_.shape
    tm = _tile(T, 512, 8)

    def body(gf_ref, gg_ref, bf_ref, bg_ref, dy_ref, dbf_ref, dbg_ref, dgate_ref):
        d = dy_ref[...]
        sf, sg = _sigmoid(gf_ref[...]), _sigmoid(gg_ref[...])
        dbf_ref[...] = (d * sf).astype(BF16)
        dbg_ref[...] = (d * sg).astype(BF16)
        dgate_ref[:, 0:D] = (d * bf_ref[...] * sf * (1.0 - sf)).astype(BF16)
        dgate_ref[:, D:2 * D] = (d * bg_ref[...] * sg * (1.0 - sg)).astype(BF16)

    lo = pl.BlockSpec((tm, D), lambda i: (i, 0))
    hi = pl.BlockSpec((tm, D), lambda i: (i, 1))
    both = pl.BlockSpec((tm, 2 * D), lambda i: (i, 0))
    return pl.pallas_call(
        body, name="merge_bwd", grid=(T // tm,), in_specs=[lo, hi, lo, lo, lo], out_specs=[lo, lo, both],
        out_shape=[jax.ShapeDtypeStruct((T, D), BF16), jax.ShapeDtypeStruct((T, D), BF16),
                   jax.ShapeDtypeStruct((T, 2 * D), BF16)],
        compiler_params=_params("parallel"),
    )(p_gates, p_gates, bf_, bg_, dy)


def fox_fwd(p_fox, c_col, c_row, B, S, H):
    T = B * S
    t = _tile(S, ATT_TILE)
    nq = S // t
    scale = HEAD ** -0.5

    def body(q_ref, k_ref, v_ref, cq_ref, cr_ref, o_ref, o16_ref, lse_ref):
        i = pl.program_id(2)
        q = q_ref[...]
        cq = cq_ref[...]
        row = lax.broadcasted_iota(jnp.int32, (t, t), 0)
        col = lax.broadcasted_iota(jnp.int32, (t, t), 1)

        def step(j, carry):
            m, l, acc = carry
            off = pl.multiple_of(j * t, t)
            k = k_ref[pl.ds(off, t), :]
            v = v_ref[pl.ds(off, t), :]
            s = lax.dot_general(q, k, _NT, preferred_element_type=F32) * scale + (cq - cr_ref[:, pl.ds(off, t)])
            s = jnp.where((j < i) | (col <= row), s, NEG)
            m_new = jnp.maximum(m, jnp.max(s, axis=-1, keepdims=True))
            alpha = jnp.exp(m - m_new)
            p = jnp.exp(s - m_new)
            l = alpha * l + jnp.sum(p, axis=-1, keepdims=True)
            acc = alpha * acc + lax.dot_general(p.astype(BF16), v, _NN, preferred_element_type=F32)
            return m_new, l, acc

        m0 = jnp.full((t, 1), NEG, F32)
        m, l, acc = lax.fori_loop(0, i + 1, step, (m0, jnp.zeros((t, 1), F32), jnp.zeros((t, HEAD), F32)))
        o = acc / l
        o_ref[...] = o
        o16_ref[...] = o.astype(BF16)
        lse_ref[...] = m + jnp.log(l)

    return pl.pallas_call(
        body, name="fox_fwd", grid=(B, H, nq),
        in_specs=[pl.BlockSpec((t, HEAD), lambda b, h, i: (b * nq + i, 3 * h)),
                  pl.BlockSpec((S, HEAD), lambda b, h, i: (b, 3 * h + 1)),
                  pl.BlockSpec((S, HEAD), lambda b, h, i: (b, 3 * h + 2)),
                  pl.BlockSpec((None, None, t, 1), lambda b, h, i: (b, h, i, 0)),
                  pl.BlockSpec((None, None, 1, S), lambda b, h, i: (b, h, 0, 0))],
        out_specs=[pl.BlockSpec((t, HEAD), lambda b, h, i: (b * nq + i, h)),
                   pl.BlockSpec((t, HEAD), lambda b, h, i: (b * nq + i, h)),
                   pl.BlockSpec((None, None, t, 1), lambda b, h, i: (b, h, i, 0))],
        out_shape=[jax.ShapeDtypeStruct((T, H * HEAD), F32), jax.ShapeDtypeStruct((T, H * HEAD), BF16),
                   jax.ShapeDtypeStruct((B, H, S, 1), F32)],
        compiler_params=_params("parallel", "parallel", "arbitrary"),
    )(p_fox, p_fox, p_fox, c_col, c_row)


def fox_bwd(p_fox, c_col, c_row, o, lse, do, B, S, H):
    T = B * S
    t = _tile(S, ATT_TILE)
    n = S // t
    scale = HEAD ** -0.5

    def body(q_ref, k_ref, v_ref, cq_ref, cr_ref, o_ref, lse_ref, do_ref, dqkv_ref, dcq_ref, dcr_ref, dq_acc, delta_s):
        row = lax.broadcasted_iota(jnp.int32, (t, t), 0)
        col = lax.broadcasted_iota(jnp.int32, (t, t), 1)

        def prep(i, c):
            rows = pl.ds(pl.multiple_of(i * t, t), t)
            delta_s[rows, :] = jnp.sum(do_ref[rows, :] * o_ref[rows, :], axis=-1, keepdims=True)
            dq_acc[rows, :] = jnp.zeros((t, HEAD), F32)
            dcq_ref[rows, :] = jnp.zeros((t, 1), F32)
            return c

        lax.fori_loop(0, n, prep, 0)

        def kv_step(j, c):
            joff = pl.multiple_of(j * t, t)
            k = k_ref[pl.ds(joff, t), :]
            v = v_ref[pl.ds(joff, t), :]
            crj = cr_ref[:, pl.ds(joff, t)]

            def q_step(i, carry):
                dk, dv, dc = carry
                rows = pl.ds(pl.multiple_of(i * t, t), t)
                q = q_ref[rows, :]
                dob = do_ref[rows, :].astype(BF16)
                s = lax.dot_general(q, k, _NT, preferred_element_type=F32) * scale + (cq_ref[rows, :] - crj)
                s = jnp.where((i > j) | (col <= row), s, NEG)
                p = jnp.exp(s - lse_ref[rows, :])
                dp = lax.dot_general(dob, v, _NT, preferred_element_type=F32)
                ds = p * (dp - delta_s[rows, :])
                dsb = ds.astype(BF16)
                dv = dv + lax.dot_general(p.astype(BF16), dob, _TN, preferred_element_type=F32)
                dk = dk + lax.dot_general(dsb, q, _TN, preferred_element_type=F32)
                dq_acc[rows, :] += lax.dot_general(dsb, k, _NN, preferred_element_type=F32) * scale
                dc = dc + jnp.sum(ds, axis=0, keepdims=True)
                dcq_ref[rows, :] += jnp.sum(ds, axis=-1, keepdims=True)
                return dk, dv, dc

            z = jnp.zeros((t, HEAD), F32)
            dk, dv, dc = lax.fori_loop(j, n, q_step, (z, z, jnp.zeros((1, t), F32)))
            dqkv_ref[pl.ds(joff, t), HEAD:2 * HEAD] = (dk * scale).astype(BF16)
            dqkv_ref[pl.ds(joff, t), 2 * HEAD:3 * HEAD] = dv.astype(BF16)
            dcr_ref[:, pl.ds(joff, t)] = -dc
            return c

        lax.fori_loop(0, n, kv_step, 0)
        dqkv_ref[:, 0:HEAD] = dq_acc[...].astype(BF16)

    col_spec = pl.BlockSpec((None, None, S, 1), lambda b, h: (b, h, 0, 0))
    row_spec = pl.BlockSpec((None, None, 1, S), lambda b, h: (b, h, 0, 0))
    head = pl.BlockSpec((S, HEAD), lambda b, h: (b, h))
    return pl.pallas_call(
        body, name="fox_bwd", grid=(B, H),
        in_specs=[pl.BlockSpec((S, HEAD), lambda b, h: (b, 3 * h)),
                  pl.BlockSpec((S, HEAD), lambda b, h: (b, 3 * h + 1)),
                  pl.BlockSpec((S, HEAD), lambda b, h: (b, 3 * h + 2)),
                  col_spec, row_spec, head, col_spec, head],
        out_specs=[pl.BlockSpec((S, 3 * HEAD), lambda b, h: (b, h)), col_spec, row_spec],
        out_shape=[jax.ShapeDtypeStruct((T, 3 * H * HEAD), BF16), jax.ShapeDtypeStruct((B, H, S, 1), F32),
                   jax.ShapeDtypeStruct((B, H, 1, S), F32)],
        scratch_shapes=[pltpu.VMEM((S, HEAD), F32), pltpu.VMEM((S, 1), F32)],
        compiler_params=_params("parallel", "parallel"),
    )(p_fox, p_fox, p_fox, c_col, c_row, o, lse, do)


def _small_fn(x, b0, b1, H):
    S = x.shape[0]
    lane = lax.broadcasted_iota(jnp.int32, x.shape, 1)
    z = x + b0
    tail = jnp.log1p(jnp.exp(-jnp.abs(z)))
    softplus = jnp.maximum(z, 0.0) + tail
    logsig = -(jnp.maximum(-z, 0.0) + tail)
    g = -jnp.exp(b1) * softplus
    pre = jnp.where(lane < H, logsig, jnp.where(lane < 2 * H, g, 0.0))
    bl = _tile(S, 256, CHUNK)
    r = lax.broadcasted_iota(jnp.int32, (bl, bl), 0)
    c = lax.broadcasted_iota(jnp.int32, (bl, bl), 1)
    tri = (r >= c).astype(F32)
    tri_chunk = jnp.where((r >= c) & (jnp.right_shift(r, 6) == jnp.right_shift(c, 6)), 1.0, 0.0)
    carry = jnp.zeros((1, x.shape[1]), F32)
    parts = []
    for i in range(S // bl):
        blk = pre[i * bl:(i + 1) * bl, :]
        full = lax.dot_general(tri, blk, _NN, precision=EXACT, preferred_element_type=F32) + carry
        chunked = lax.dot_general(tri_chunk, blk, _NN, precision=EXACT, preferred_element_type=F32)
        parts.append(jnp.where(lane[:bl] < H, full, chunked))
        carry = carry + jnp.sum(blk, axis=0, keepdims=True)
    cum = parts[0] if len(parts) == 1 else jnp.concatenate(parts, axis=0)
    return jnp.where(lane < 2 * H, cum, jnp.where(lane < 3 * H, _sigmoid(x), 0.0))


def small_fwd(p_small, prm, B, S, H):
    T = B * S

    def body(x_ref, p_ref, o_ref):
        o_ref[...] = _small_fn(x_ref[...], p_ref[0:1, :], p_ref[1:2, :], H)

    blk = pl.BlockSpec((S, 128), lambda b: (b, 0))
    return pl.pallas_call(
        body, name="small_fwd", grid=(B,), in_specs=[blk, pl.BlockSpec((8, 128), lambda b: (0, 0))], out_specs=blk,
        out_shape=jax.ShapeDtypeStruct((T, 128), F32), compiler_params=_params("parallel"),
    )(p_small, prm)


def small_bwd(p_small, prm, d_out, B, S, H):
    T = B * S

    def body(x_ref, p_ref, d_ref, dx_ref, dp_ref):
        @pl.when(pl.program_id(0) == 0)
        def _():
            dp_ref[...] = jnp.zeros_like(dp_ref)

        _, vjp = jax.vjp(functools.partial(_small_fn, H=H), x_ref[...], p_ref[0:1, :], p_ref[1:2, :])
        dx, db0, db1 = vjp(d_ref[...])
        dx_ref[...] = dx.astype(BF16)
        dp_ref[0:1, :] += db0
        dp_ref[1:2, :] += db1

    blk = pl.BlockSpec((S, 128), lambda b: (b, 0))
    pblk = pl.BlockSpec((8, 128), lambda b: (0, 0))
    return pl.pallas_call(
        body, name="small_bwd", grid=(B,), in_specs=[blk, pblk, blk], out_specs=[blk, pblk],
        out_shape=[jax.ShapeDtypeStruct((T, 128), BF16), jax.ShapeDtypeStruct((8, 128), F32)],
        compiler_params=_params("arbitrary"),
    )(p_small, prm, d_out)


def gdn_prep_fwd(p_gqkv, cw, B, S, H):
    T = B * S

    def body(x_ref, w_ref, o_ref):
        y = _conv_fwd(x_ref[...], w_ref, GDN_CONV)
        a = y * _sigmoid(y)
        rs = lax.rsqrt(jnp.sum(a * a, axis=-1, keepdims=True) + EPS)
        is_qk = (pl.program_id(1) % 3) < 2
        o_ref[...] = a * jnp.where(is_qk, rs, 1.0)

    blk = pl.BlockSpec((S, HEAD), lambda b, n: (b, n))
    wblk = pl.BlockSpec((GDN_CONV, HEAD), lambda b, n: (0, n))
    return pl.pallas_call(
        body, name="gdn_prep_fwd", grid=(B, 3 * H), in_specs=[blk, wblk], out_specs=blk,
        out_shape=jax.ShapeDtypeStruct((T, 3 * H * HEAD), F32), compiler_params=_params("parallel", "parallel"),
    )(p_gqkv, cw)


def gdn_prep_bwd(p_gqkv, cw, d_out, B, S, H):
    T = B * S

    def body(x_ref, w_ref, d_ref, dx_ref, dw_ref):
        @pl.when(pl.program_id(1) == 0)
        def _():
            dw_ref[...] = jnp.zeros_like(dw_ref)

        x = x_ref[...]
        y = _conv_fwd(x, w_ref, GDN_CONV)
        sg = _sigmoid(y)
        a = y * sg
        rs = lax.rsqrt(jnp.sum(a * a, axis=-1, keepdims=True) + EPS)
        d = d_ref[...]
        out = a * rs
        da_qk = rs * (d - out * jnp.sum(d * out, axis=-1, keepdims=True))
        is_qk = (pl.program_id(0) % 3) < 2
        da = jnp.where(is_qk, da_qk, d)
        dy = da * (sg + y * sg * (1.0 - sg))
        dx, dws = _conv_bwd(x, dy, w_ref, GDN_CONV)
        dx_ref[...] = dx.astype(BF16)
        for i in range(GDN_CONV):
            dw_ref[i:i + 1, :] += dws[i]

    blk = pl.BlockSpec((S, HEAD), lambda n, b: (b, n))
    wblk = pl.BlockSpec((GDN_CONV, HEAD), lambda n, b: (0, n))
    return pl.pallas_call(
        body, name="gdn_prep_bwd", grid=(3 * H, B), in_specs=[blk, wblk, blk], out_specs=[blk, wblk],
        out_shape=[jax.ShapeDtypeStruct((T, 3 * H * HEAD), BF16), jax.ShapeDtypeStruct((GDN_CONV, 3 * H * HEAD), F32)],
        compiler_params=_params("parallel", "arbitrary"),
    )(p_gqkv, cw, d_out)


@jax.custom_vjp
def _given_inverse(a, t):
    return t


def _given_inverse_fwd(a, t):
    return t, t


def _given_inverse_bwd(t, g):
    x = _dg(t, g, _TN, True)
    return -_dg(x, t, _NT, True), jnp.zeros_like(t)


_given_inverse.defvjp(_given_inverse_fwd, _given_inverse_bwd)


def _to_col(row):
    r = lax.broadcasted_iota(jnp.int32, (CHUNK, CHUNK), 0)
    c = lax.broadcasted_iota(jnp.int32, (CHUNK, CHUNK), 1)
    return jnp.sum(jnp.where(r == c, row, 0.0), axis=1, keepdims=True)


def _intra_fn(k, v, beta_r, gcr, ops, t_known=None):
    n = len(k)
    r = lax.broadcasted_iota(jnp.int32, (CHUNK, CHUNK), 0)
    c = lax.broadcasted_iota(jnp.int32, (CHUNK, CHUNK), 1)
    beta = [_to_col(beta_r[i]) for i in range(n)]
    gcc = [_to_col(gcr[i]) for i in range(n)]
    decay = [jnp.exp(jnp.where(r > c, gcc[i] - gcr[i], NEG)) for i in range(n)]
    kb = [k[i] * beta[i] for i in range(n)]
    a = [ops.nt(kb[i], k[i]) * decay[i] for i in range(n)]
    if t_known is None:
        p = [-a[i] for i in range(n)]
        tm = [jnp.where(r == c, 1.0, 0.0) + p[i] for i in range(n)]
        for _ in range(5):
            p = [ops.nn(p[i], p[i], hi=True) for i in range(n)]
            tm = [tm[i] + ops.nn(tm[i], p[i], hi=True) for i in range(n)]
    else:
        tm = [_given_inverse(a[i], t_known[i]) for i in range(n)]
    u_hat = [ops.nn(tm[i], v[i] * beta[i], hi=True) for i in range(n)]
    w = [ops.nn(tm[i], kb[i] * jnp.exp(gcc[i]), hi=True) for i in range(n)]
    return tuple(u_hat), tuple(w), tuple(tm)


INTRA_NB = 8


def gdn_intra_fwd(qkvn, betar5, gcr5, B, S, H):
    T = B * S
    N = S // CHUNK
    nb = min(INTRA_NB, N)
    rows = nb * CHUNK
    ns = N // nb

    def body(k_ref, v_ref, b_ref, gr_ref, uh_ref, w_ref, t_ref):
        sls = [slice(ci * CHUNK, (ci + 1) * CHUNK) for ci in range(nb)]
        u_hat, w, tm = _intra_fn(tuple(k_ref[sl, :] for sl in sls), tuple(v_ref[sl, :] for sl in sls),
                                 tuple(b_ref[ci] for ci in range(nb)), tuple(gr_ref[ci] for ci in range(nb)), _RawOps)
        for ci, sl in enumerate(sls):
            uh_ref[sl, :] = u_hat[ci]
            w_ref[sl, :] = w[ci]
            t_ref[ci] = tm[ci]

    rowspec = pl.BlockSpec((None, None, nb, 1, CHUNK), lambda b, h, i: (b, h, i, 0, 0))
    sqspec = pl.BlockSpec((None, None, nb, CHUNK, CHUNK), lambda b, h, i: (b, h, i, 0, 0))
    out = pl.BlockSpec((rows, HEAD), lambda b, h, i: (b * ns + i, h))
    return pl.pallas_call(
        body, name="gdn_intra_fwd", grid=(B, H, ns),
        in_specs=[pl.BlockSpec((rows, HEAD), lambda b, h, i: (b * ns + i, 3 * h + 1)),
                  pl.BlockSpec((rows, HEAD), lambda b, h, i: (b * ns + i, 3 * h + 2)),
                  rowspec, rowspec],
        out_specs=[out, out, sqspec],
        out_shape=[jax.ShapeDtypeStruct((T, H * HEAD), F32), jax.ShapeDtypeStruct((T, H * HEAD), F32),
                   jax.ShapeDtypeStruct((B, H, N, CHUNK, CHUNK), F32)],
        compiler_params=_params("parallel", "parallel", "parallel"),
    )(qkvn, qkvn, betar5, gcr5)


def gdn_intra_bwd(qkvn, betar5, gcr5, t_inv, d_uh, d_w, dq_in, dk_in, B, S, H):
    T = B * S
    N = S // CHUNK
    nb = min(INTRA_NB, N)
    rows = nb * CHUNK
    ns = N // nb

    def body(k_ref, v_ref, b_ref, gr_ref, t_ref, duh_ref, dw_ref, dq_ref, dk_ref, o_ref, db_ref, dgr_ref):
        sls = [slice(ci * CHUNK, (ci + 1) * CHUNK) for ci in range(nb)]
        chunks = range(nb)
        _, vjp = jax.vjp(
            functools.partial(_intra_fn, ops=_DiffOps, t_known=tuple(t_ref[ci] for ci in chunks)),
            tuple(k_ref[sl, :] for sl in sls), tuple(v_ref[sl, :] for sl in sls), tuple(b_ref[ci] for ci in chunks),
            tuple(gr_ref[ci] for ci in chunks))
        zero = jnp.zeros((CHUNK, CHUNK), F32)
        dk, dv, db, dgr = vjp((tuple(duh_ref[sl, :] for sl in sls), tuple(dw_ref[sl, :] for sl in sls),
                               tuple(zero for _ in chunks)))
        for ci, sl in enumerate(sls):
            o_ref[sl, 0:HEAD] = dq_ref[sl, :]
            o_ref[sl, HEAD:2 * HEAD] = dk[ci] + dk_ref[sl, :]
            o_ref[sl, 2 * HEAD:3 * HEAD] = dv[ci]
            db_ref[ci] = db[ci]
            dgr_ref[ci] = dgr[ci]

    rowspec = pl.BlockSpec((None, None, nb, 1, CHUNK), lambda b, h, i: (b, h, i, 0, 0))
    sqspec = pl.BlockSpec((None, None, nb, CHUNK, CHUNK), lambda b, h, i: (b, h, i, 0, 0))
    head = pl.BlockSpec((rows, HEAD), lambda b, h, i: (b * ns + i, h))
    return pl.pallas_call(
        body, name="gdn_intra_bwd", grid=(B, H, ns),
        in_specs=[pl.BlockSpec((rows, HEAD), lambda b, h, i: (b * ns + i, 3 * h + 1)),
                  pl.BlockSpec((rows, HEAD), lambda b, h, i: (b * ns + i, 3 * h + 2)),
                  rowspec, rowspec, sqspec, head, head, head, head],
        out_specs=[pl.BlockSpec((rows, 3 * HEAD), lambda b, h, i: (b * ns + i, h)), rowspec, rowspec],
        out_shape=[jax.ShapeDtypeStruct((T, 3 * H * HEAD), F32),
                   jax.ShapeDtypeStruct((B, H, N, 1, CHUNK), F32), jax.ShapeDtypeStruct((B, H, N, 1, CHUNK), F32)],
        compiler_params=_params("parallel", "parallel", "parallel"),
    )(qkvn, qkvn, betar5, gcr5, t_inv, d_uh, d_w, dq_in, dk_in)


def _inter_fn(q, k, u_hat, w, gcr, state, ops):
    n = len(q)
    r = lax.broadcasted_iota(jnp.int32, (CHUNK, CHUNK), 0)
    c = lax.broadcasted_iota(jnp.int32, (CHUNK, CHUNK), 1)
    last = lax.broadcasted_iota(jnp.int32, (1, CHUNK), 1) == CHUNK - 1
    gcc = [_to_col(gcr[i]) for i in range(n)]
    gl = [jnp.sum(jnp.where(last, gcr[i], 0.0), axis=1, keepdims=True) for i in range(n)]
    decay = [jnp.exp(jnp.where(r >= c, gcc[i] - gcr[i], NEG)) for i in range(n)]
    qs = [q[i] * (HEAD ** -0.5) for i in range(n)]
    ws = [ops.nn(w[i], state[i]) for i in range(n)]
    qst = [ops.nn(qs[i] * jnp.exp(gcc[i]), state[i]) for i in range(n)]
    attn = [ops.nt(qs[i], k[i]) * decay[i] for i in range(n)]
    u = [u_hat[i] - ws[i] for i in range(n)]
    o = [qst[i] + ops.nn(attn[i], u[i]) for i in range(n)]
    kdu = [ops.tn(k[i] * jnp.exp(gl[i] - gcc[i]), u[i]) for i in range(n)]
    new_state = [state[i] * jnp.exp(gl[i]) + kdu[i] for i in range(n)]
    return tuple(o), tuple(new_state)


INTER_HEADS = 2


def _inter_specs(S, N, hp):
    qk = []
    for hh in range(hp):
        qk.append(pl.BlockSpec((S, HEAD), lambda b, g, hh=hh: (b, 3 * (hp * g + hh))))
        qk.append(pl.BlockSpec((S, HEAD), lambda b, g, hh=hh: (b, 3 * (hp * g + hh) + 1)))
    heads = pl.BlockSpec((S, hp * HEAD), lambda b, g: (b, g))
    rowspec = pl.BlockSpec((None, hp, N, 1, CHUNK), lambda b, g: (b, g, 0, 0, 0))
    stspec = pl.BlockSpec((None, hp, N, HEAD, HEAD), lambda b, g: (b, g, 0, 0, 0))
    return qk, heads, rowspec, stspec


def gdn_inter_fwd(qkvn, u_hat, w, gcr5, B, S, H):
    T = B * S
    N = S // CHUNK
    hp = INTER_HEADS if H % INTER_HEADS == 0 else 1
    hs = range(hp)

    def body(*refs):
        qk_refs, (uh_ref, w_ref, gr_ref, o_ref, st_ref, s_scr) = refs[:2 * hp], refs[2 * hp:]
        s_scr[...] = jnp.zeros_like(s_scr)

        def step(n, c):
            rows = pl.ds(pl.multiple_of(n * CHUNK, CHUNK), CHUNK)
            st = tuple(s_scr[hh] for hh in hs)
            for hh in hs:
                st_ref[hh, n] = st[hh]
            o, new = _inter_fn(tuple(qk_refs[2 * hh][rows, :] for hh in hs), tuple(qk_refs[2 * hh + 1][rows, :] for hh in hs),
                               tuple(uh_ref[rows, hh * HEAD:(hh + 1) * HEAD] for hh in hs),
                               tuple(w_ref[rows, hh * HEAD:(hh + 1) * HEAD] for hh in hs),
                               tuple(gr_ref[hh, n] for hh in hs), st, _RawOps)
            for hh in hs:
                o_ref[rows, hh * HEAD:(hh + 1) * HEAD] = o[hh]
                s_scr[hh] = new[hh]
            return c

        lax.fori_loop(0, N, step, 0)

    qk, heads, rowspec, stspec = _inter_specs(S, N, hp)
    return pl.pallas_call(
        body, name="gdn_inter_fwd", grid=(B, H // hp),
        in_specs=qk + [heads, heads, rowspec], out_specs=[heads, stspec],
        out_shape=[jax.ShapeDtypeStruct((T, H * HEAD), F32), jax.ShapeDtypeStruct((B, H, N, HEAD, HEAD), F32)],
        scratch_shapes=[pltpu.VMEM((hp, HEAD, HEAD), F32)],
        compiler_params=_params("parallel", "parallel"),
    )(*([qkvn] * (2 * hp)), u_hat, w, gcr5)


def gdn_inter_bwd(qkvn, u_hat, w, gcr5, states, d_o, B, S, H):
    T = B * S
    N = S // CHUNK
    hp = INTER_HEADS if H % INTER_HEADS == 0 else 1
    hs = range(hp)

    def body(*refs):
        qk_refs = refs[:2 * hp]
        uh_ref, w_ref, gr_ref, st_ref, do_ref, dq_ref, dk_ref, duh_ref, dw_ref, dgr_ref, ds_scr = refs[2 * hp:]
        ds_scr[...] = jnp.zeros_like(ds_scr)
        cols = [slice(hh * HEAD, (hh + 1) * HEAD) for hh in hs]

        def step(i, c):
            n = N - 1 - i
            rows = pl.ds(pl.multiple_of(n * CHUNK, CHUNK), CHUNK)
            _, vjp = jax.vjp(functools.partial(_inter_fn, ops=_DiffOps),
                             tuple(qk_refs[2 * hh][rows, :] for hh in hs), tuple(qk_refs[2 * hh + 1][rows, :] for hh in hs),
                             tuple(uh_ref[rows, cols[hh]] for hh in hs), tuple(w_ref[rows, cols[hh]] for hh in hs),
                             tuple(gr_ref[hh, n] for hh in hs), tuple(st_ref[hh, n] for hh in hs))
            dq, dk, duh, dw, dgr, ds = vjp((tuple(do_ref[rows, cols[hh]] for hh in hs), tuple(ds_scr[hh] for hh in hs)))
            for hh in hs:
                dq_ref[rows, cols[hh]] = dq[hh]
                dk_ref[rows, cols[hh]] = dk[hh]
                duh_ref[rows, cols[hh]] = duh[hh]
                dw_ref[rows, cols[hh]] = dw[hh]
                dgr_ref[hh, n] = dgr[hh]
                ds_scr[hh] = ds[hh]
            return c

        lax.fori_loop(0, N, step, 0)

    qk, heads, rowspec, stspec = _inter_specs(S, N, hp)
    hshape = jax.ShapeDtypeStruct((T, H * HEAD), F32)
    return pl.pallas_call(
        body, name="gdn_inter_bwd", grid=(B, H // hp),
        in_specs=qk + [heads, heads, rowspec, stspec, heads],
        out_specs=[heads, heads, heads, heads, rowspec],
        out_shape=[hshape, hshape, hshape, hshape, jax.ShapeDtypeStruct((B, H, N, 1, CHUNK), F32)],
        scratch_shapes=[pltpu.VMEM((hp, HEAD, HEAD), F32)],
        compiler_params=_params("parallel", "parallel"),
    )(*([qkvn] * (2 * hp)), u_hat, w, gcr5, states, d_o)


def gdn_post_fwd(o, p_gz, g, H):
    T = o.shape[0]
    tm = _tile(T, 1024, 8)

    def body(o_ref, z_ref, g_ref, y_ref):
        ov, z = o_ref[...], z_ref[...]
        r = lax.rsqrt(jnp.mean(ov * ov, axis=-1, keepdims=True) + EPS)
        y_ref[...] = (ov * r * g_ref[...] * z * _sigmoid(z)).astype(BF16)

    blk = pl.BlockSpec((tm, HEAD), lambda i, h: (i, h))
    return pl.pallas_call(
        body, name="gdn_post_fwd", grid=(T // tm, H), in_specs=[blk, blk, pl.BlockSpec((1, HEAD), lambda i, h: (0, 0))],
        out_specs=blk, out_shape=jax.ShapeDtypeStruct((T, H * HEAD), BF16),
        compiler_params=_params("parallel", "parallel"),
    )(o, p_gz, g)


def gdn_post_bwd(o, p_gz, g, dy, H):
    T = o.shape[0]
    tm = _tile(T, 1024, 8)

    def body(o_ref, z_ref, g_ref, dy_ref, do_ref, dz_ref, dg_ref):
        @pl.when((pl.program_id(0) == 0) & (pl.program_id(1) == 0))
        def _():
            dg_ref[...] = jnp.zeros_like(dg_ref)

        ov, z, d = o_ref[...], z_ref[...], dy_ref[...]
        r = lax.rsqrt(jnp.mean(ov * ov, axis=-1, keepdims=True) + EPS)
        xh = ov * r
        sg = _sigmoid(z)
        sz = z * sg
        d_n = d * sz
        dz_ref[...] = (d * xh * g_ref[...] * (sg + z * sg * (1.0 - sg))).astype(BF16)
        dg_ref[...] += jnp.sum(d_n * xh, axis=0, keepdims=True)
        dxh = d_n * g_ref[...]
        do_ref[...] = r * (dxh - xh * jnp.mean(dxh * xh, axis=-1, keepdims=True))

    blk = pl.BlockSpec((tm, HEAD), lambda i, h: (i, h))
    vec = pl.BlockSpec((1, HEAD), lambda i, h: (0, 0))
    return pl.pallas_call(
        body, name="gdn_post_bwd", grid=(T // tm, H), in_specs=[blk, blk, vec, blk], out_specs=[blk, blk, vec],
        out_shape=[jax.ShapeDtypeStruct((T, H * HEAD), F32), jax.ShapeDtypeStruct((T, H * HEAD), BF16),
                   jax.ShapeDtypeStruct((1, HEAD), F32)],
        compiler_params=_params("arbitrary", "arbitrary"),
    )(o, p_gz, g, dy)


def adamw(w, g, m, v, name):
    shape = w.shape
    w2, g2, m2, v2 = (a.reshape(-1, shape[-1]) for a in (w, g, m, v))
    R, C = w2.shape
    tr = _tile(R, 128, 8)

    def body(w_ref, g_ref, m_ref, v_ref, d_ref, nm_ref, nv_ref):
        gv = g_ref[...]
        nm = ADAM_B1 * m_ref[...] + (1.0 - ADAM_B1) * gv
        nv = ADAM_B2 * v_ref[...] + (1.0 - ADAM_B2) * (gv * gv)
        m_hat = nm / (1.0 - ADAM_B1 ** ADAM_STEP)
        v_hat = nv / (1.0 - ADAM_B2 ** ADAM_STEP)
        d_ref[...] = -ADAM_LR * (m_hat / (jnp.sqrt(v_hat) + ADAM_EPS) + ADAM_WD * w_ref[...])
        nm_ref[...] = nm
        nv_ref[...] = nv

    blk = pl.BlockSpec((tr, C), lambda i: (i, 0))
    sh = jax.ShapeDtypeStruct((R, C), F32)
    d, nm, nv = pl.pallas_call(
        body, name=name, grid=(R // tr,), in_specs=[blk] * 4, out_specs=[blk] * 3, out_shape=[sh] * 3,
        compiler_params=_params("parallel"),
    )(w2, g2, m2, v2)
    return d.reshape(shape), nm.reshape(shape), nv.reshape(shape)


def _place():
    x, y, c = lax.axis_index("x"), lax.axis_index("y"), lax.axis_index("c")
    chips = [(1 - x, y), (x, 1 - y), (1 - x, 1 - y)]
    return x, y, c, chips


_HBM = pl.BlockSpec(memory_space=pltpu.HBM)


def allgather_weights(packs):
    n = len(packs)

    def body(*refs):
        in_refs, out_refs, (send_sems, recv_sems) = refs[:n], refs[n:2 * n], refs[2 * n:]
        x, y, c, chips = _place()
        me_s = 2 * x + y
        me, sibling = (x, y, c), (x, y, 1 - c)
        shards = [2 * chip[0] + chip[1] for chip in chips]

        def copy(a, k, shard, half, to, src=None):
            dst = out_refs[a].at[shard, half]
            return pltpu.make_async_remote_copy(src_ref=dst if src is None else src, dst_ref=dst,
                                                send_sem=send_sems.at[6 * a + k], recv_sem=recv_sems.at[6 * a + k],
                                                device_id=to, device_id_type=MESH)

        first = [copy(a, j, me_s, c, (*chip, c), src=in_refs[a].at[c]) for a in range(n) for j, chip in enumerate(chips)]
        for cp in first:
            cp.start()
        passed = []
        for a in range(n):
            for j in range(3):
                copy(a, j, shards[j], c, me).wait_recv()
                passed.append(copy(a, 3 + j, shards[j], c, sibling))
                passed[-1].start()
        for a in range(n):
            for j in range(3):
                copy(a, 3 + j, shards[j], 1 - c, me).wait_recv()
        for cp in first + passed:
            cp.wait_send()

    return pl.pallas_call(
        body, name="allgather_weights", in_specs=[_HBM] * n, out_specs=[_HBM] * n,
        out_shape=[jax.ShapeDtypeStruct((N_CHIP,) + p.shape, p.dtype) for p in packs],
        scratch_shapes=[pltpu.SemaphoreType.DMA((6 * n,)), pltpu.SemaphoreType.DMA((6 * n,))],
    )(*packs)


def exchange_halves(gs):
    n = len(gs)

    def body(*refs):
        g_refs, out_refs, (send_sems, recv_sems) = refs[:n], refs[n:2 * n], refs[2 * n:]
        x, y, c, _ = _place()
        cps = [pltpu.make_async_remote_copy(src_ref=g_refs[a].at[1 - c], dst_ref=out_refs[a], send_sem=send_sems.at[a],
                                            recv_sem=recv_sems.at[a], device_id=(x, y, 1 - c), device_id_type=MESH)
               for a in range(n)]
        for cp in cps:
            cp.start()
        for cp in cps:
            cp.wait()

    return pl.pallas_call(
        body, name="exchange_halves", in_specs=[_HBM] * n, out_specs=[_HBM] * n,
        out_shape=[jax.ShapeDtypeStruct(g.shape[1:], F32) for g in gs],
        scratch_shapes=[pltpu.SemaphoreType.DMA((n,)), pltpu.SemaphoreType.DMA((n,))],
    )(*gs)


def add_halves(g, got, idx, name):
    _, ns, r, cols = g.shape
    tr = _tile(r, 256, 16)

    def body(idx_ref, a_ref, b_ref, o32_ref, o16_ref):
        s = a_ref[...] + b_ref[...]
        o32_ref[...] = s
        o16_ref[...] = s.astype(BF16)

    blk = pl.BlockSpec((None, tr, cols), lambda s, i, idx_ref: (s, i, 0))
    return pl.pallas_call(
        body, name=name,
        grid_spec=pltpu.PrefetchScalarGridSpec(
            num_scalar_prefetch=1, grid=(ns, r // tr),
            in_specs=[pl.BlockSpec((None, None, tr, cols), lambda s, i, idx_ref: (idx_ref[0], s, i, 0)), blk],
            out_specs=[blk, blk]),
        out_shape=[jax.ShapeDtypeStruct((ns, r, cols), F32), jax.ShapeDtypeStruct((ns, r, cols), BF16)],
        compiler_params=_params("parallel", "parallel"),
    )(idx, g, got)


def scatter_chips(b16s):
    n = len(b16s)

    def body(*refs):
        in_refs, out_refs, (send_sems, recv_sems) = refs[:n], refs[n:2 * n], refs[2 * n:]
        x, y, c, chips = _place()
        me_s = 2 * x + y
        cps = []
        for a in range(n):
            for j, chip in enumerate(chips):
                cps.append(pltpu.make_async_remote_copy(
                    src_ref=in_refs[a].at[2 * chip[0] + chip[1]], dst_ref=out_refs[a].at[me_s],
                    send_sem=send_sems.at[3 * a + j], recv_sem=recv_sems.at[3 * a + j], device_id=(*chip, c),
                    device_id_type=MESH))
        for cp in cps:
            cp.start()
        for a in range(n):
            for j, chip in enumerate(chips):
                pltpu.make_async_remote_copy(
                    src_ref=in_refs[a].at[me_s], dst_ref=out_refs[a].at[2 * chip[0] + chip[1]],
                    send_sem=send_sems.at[3 * a + j], recv_sem=recv_sems.at[3 * a + j], device_id=(x, y, c),
                    device_id_type=MESH).wait_recv()
        for cp in cps:
            cp.wait_send()

    return pl.pallas_call(
        body, name="scatter_chips", in_specs=[_HBM] * n, out_specs=[_HBM] * n,
        out_shape=[jax.ShapeDtypeStruct(b.shape, BF16) for b in b16s],
        scratch_shapes=[pltpu.SemaphoreType.DMA((3 * n,)), pltpu.SemaphoreType.DMA((3 * n,))],
    )(*b16s)


def add_chips(a32, got16, idx, name):
    ns, r, cols = a32.shape
    tr = _tile(r, 256, 16)

    def body(idx_ref, a_ref, r1_ref, r2_ref, r3_ref, o_ref):
        o_ref[...] = ((a_ref[...] + r1_ref[...].astype(F32)) + r2_ref[...].astype(F32)) + r3_ref[...].astype(F32)

    def slab(k):
        return pl.BlockSpec((None, tr, cols), lambda i, idx_ref: ((idx_ref[1] + k) % ns, i, 0))

    return pl.pallas_call(
        body, name=name,
        grid_spec=pltpu.PrefetchScalarGridSpec(
            num_scalar_prefetch=1, grid=(r // tr,), in_specs=[slab(0), slab(1), slab(2), slab(3)],
            out_specs=pl.BlockSpec((tr, cols), lambda i, idx_ref: (i, 0))),
        out_shape=jax.ShapeDtypeStruct((r, cols), F32),
        compiler_params=_params("parallel"),
    )(idx, a32, got16, got16, got16)


def share_halves(halves):
    n = len(halves)

    def body(*refs):
        in_refs, out_refs, (send_sems, recv_sems) = refs[:n], refs[n:2 * n], refs[2 * n:]
        x, y, c, _ = _place()
        cps = [pltpu.make_async_remote_copy(src_ref=in_refs[a], dst_ref=out_refs[a], send_sem=send_sems.at[a],
                                            recv_sem=recv_sems.at[a], device_id=(x, y, 1 - c), device_id_type=MESH)
               for a in range(n)]
        for cp in cps:
            cp.start()
        for cp in cps:
            cp.wait()

    return pl.pallas_call(
        body, name="share_halves", in_specs=[_HBM] * n, out_specs=[_HBM] * n,
        out_shape=[jax.ShapeDtypeStruct(h.shape, F32) for h in halves],
        scratch_shapes=[pltpu.SemaphoreType.DMA((n,)), pltpu.SemaphoreType.DMA((n,))],
    )(*halves)


def allreduce_small(v):
    R, _ = v.shape

    def body(in_ref, out_ref, slots, send_sems, recv_sems):
        x, y, c, _ = _place()
        me = 4 * x + 2 * y + c
        slots[me] = in_ref[...]
        cps = []
        for k in range(1, N_DEV):
            to = (x ^ (k >> 2), y ^ ((k >> 1) & 1), c ^ (k & 1))
            cps.append(pltpu.make_async_remote_copy(src_ref=in_ref, dst_ref=slots.at[me], send_sem=send_sems.at[k - 1],
                                                    recv_sem=recv_sems.at[k - 1], device_id=to, device_id_type=MESH))
        for cp in cps:
            cp.start()
        for k in range(1, N_DEV):
            frm = 4 * (x ^ (k >> 2)) + 2 * (y ^ ((k >> 1) & 1)) + (c ^ (k & 1))
            pltpu.make_async_remote_copy(src_ref=in_ref, dst_ref=slots.at[frm], send_sem=send_sems.at[k - 1],
                                         recv_sem=recv_sems.at[k - 1], device_id=(x, y, c), device_id_type=MESH).wait_recv()
        for cp in cps:
            cp.wait_send()
        acc = slots[0]
        for d in range(1, N_DEV):
            acc = acc + slots[d]
        out_ref[...] = acc

    vm = pl.BlockSpec(memory_space=pltpu.VMEM)
    return pl.pallas_call(
        body, name="allreduce_small", in_specs=[vm], out_specs=vm, out_shape=jax.ShapeDtypeStruct((R, ROW), F32),
        scratch_shapes=[pltpu.VMEM((N_DEV, R, ROW), F32), pltpu.SemaphoreType.DMA((N_DEV - 1,)),
                        pltpu.SemaphoreType.DMA((N_DEV - 1,))],
    )(v)


def _rows_of(n):
    return -(-n // (16 * ROW)) * 16


def _pack_rows(items, total_rows, dtype):
    parts = []
    used = 0
    for a in items:
        flat = a.reshape(-1)
        r = _rows_of(flat.shape[0])
        flat = jnp.pad(flat, (0, r * ROW - flat.shape[0]))
        parts.append(flat.reshape(r, ROW))
        used += r
    if total_rows > used:
        parts.append(jnp.zeros((total_rows - used, ROW), dtype))
    return jnp.concatenate(parts, axis=0)


def _unpack_rows(buf, shapes):
    lead = buf.shape[:-2]
    out = []
    off = 0
    for shp in shapes:
        n = math.prod(shp)
        r = _rows_of(n)
        piece = buf[..., off:off + r, :].reshape(*lead, r * ROW)[..., :n].reshape(*lead, *shp)
        out.append(piece)
        off += r
    return out


def _interleave_heads(w, H):
    lead = w.shape[:-1]
    return w.reshape(*lead, 3, H, HEAD).swapaxes(-3, -2).reshape(*lead, 3 * H * HEAD)


def _deinterleave_heads(w, H):
    lead = w.shape[:-1]
    return w.reshape(*lead, H, 3, HEAD).swapaxes(-3, -2).reshape(*lead, 3 * H * HEAD)


def kernel(x, norm_mix, w_in, fox_f_bias, gdn_conv_w, gdn_a_log, gdn_dt_bias, gdn_norm, w_branch_fox, w_branch_gdn, w_out, norm_ffn, w_up, ffn_conv_w, w_down, norm_final, loss_target, m_norm_mix, m_w_in, m_fox_f_bias, m_gdn_conv_w, m_gdn_a_log, m_gdn_dt_bias, m_gdn_norm, m_w_branch_fox, m_w_branch_gdn, m_w_out, m_norm_ffn, m_w_up, m_ffn_conv_w, m_w_down, m_norm_final, v_norm_mix, v_w_in, v_fox_f_bias, v_gdn_conv_w, v_gdn_a_log, v_gdn_dt_bias, v_gdn_norm, v_w_branch_fox, v_w_branch_gdn, v_w_out, v_norm_ffn, v_w_up, v_ffn_conv_w, v_w_down, v_norm_final):
    B, S, D = x.shape
    T = B * S
    H = D // HEAD
    N = S // CHUNK
    FF = w_down.shape[1] * N_CHIP
    d_in = 9 * D + 3 * H
    assert w_in.shape[2] * N_CHIP == d_in and 3 * H <= 128

    cidx = lax.axis_index("c").astype(jnp.int32)
    sidx = (2 * lax.axis_index("x") + lax.axis_index("y")).astype(jnp.int32)
    idx = jnp.stack([cidx, sidx])

    rowed = [w_branch_fox[0], w_branch_gdn[0], w_out[0], w_down[0]]
    convs = [gdn_conv_w[0], ffn_conv_w[0]]
    rowed_shapes = [a.shape for a in rowed] + [a.shape + (2,) for a in convs]
    n_rows = sum(_rows_of(math.prod(s)) for s in rowed_shapes)
    Rh = -(-n_rows // 256) * 128
    rows16 = _pack_rows([a.astype(BF16) for a in rowed] + [lax.bitcast_convert_type(a, BF16) for a in convs], 2 * Rh, BF16)
    halves = lambda a: a.reshape(2, a.shape[0] // 2, a.shape[1])
    packs = [halves(w_in[0].astype(BF16)), halves(w_up[0].astype(BF16)), halves(rows16)]
    gathered = [lax.dynamic_update_slice(g, p[None], (sidx, 0, 0, 0)) for g, p in zip(allgather_weights(packs), packs)]
    by_cols = lambda g: g.transpose(1, 2, 0, 3).reshape(2 * g.shape[2], N_CHIP * g.shape[3])
    cat_cols = lambda p: jnp.concatenate([p[i] for i in range(N_CHIP)], axis=-1)
    cat_rows = lambda p: p.reshape(-1, p.shape[-1])
    W_in = by_cols(gathered[0])
    W_up = by_cols(gathered[1])
    parts = _unpack_rows(gathered[2].reshape(N_CHIP, 2 * Rh, ROW), rowed_shapes)
    W_bf, W_bg, W_out, W_down = (cat_rows(p) for p in parts[:4])
    gconv = cat_cols(lax.bitcast_convert_type(parts[4], F32))
    fconv = cat_cols(lax.bitcast_convert_type(parts[5], F32))

    o1, o2 = 3 * D, 3 * D + H
    o3, o4, o5, o6 = o2 + 3 * D, o2 + 3 * D + H, o2 + 3 * D + 2 * H, o2 + 4 * D + 2 * H
    W_fox = _interleave_heads(W_in[:, :o1], H)
    W_gqkv = _interleave_heads(W_in[:, o2:o3], H)
    W_gz = W_in[:, o5:o6]
    W_gates = W_in[:, o6:]
    W_small = jnp.concatenate([W_in[:, o1:o2], W_in[:, o3:o5], jnp.zeros((D, 128 - 3 * H), BF16)], axis=1)
    gconv_i = _interleave_heads(gconv, H)
    W_up_g, W_up_v = W_up[:, :FF], W_up[:, FF:]
    fconv_g, fconv_v = fconv[:, :FF], fconv[:, FF:]
    prm = jnp.zeros((8, 128), F32)
    prm = prm.at[0, 0:H].set(fox_f_bias[0]).at[0, H:2 * H].set(gdn_dt_bias[0]).at[1, H:2 * H].set(gdn_a_log[0])

    x2 = x.reshape(T, D)
    tgt = loss_target.reshape(T, D)

    hn1 = rmsnorm_fwd(x2, norm_mix, "rmsnorm_mix")
    p_fox = matmul(hn1, W_fox, "nn", "proj_fox", out_dtype=BF16)
    p_gqkv = matmul(hn1, W_gqkv, "nn", "proj_gqkv")
    p_gz = matmul(hn1, W_gz, "nn", "proj_gz")
    p_gates = matmul(hn1, W_gates, "nn", "proj_gates")
    p_small = matmul(hn1, W_small, "nn", "proj_small")

    sm = small_fwd(p_small, prm, B, S, H)
    heads = lambda a: a.reshape(B, S, H).transpose(0, 2, 1)
    c_bhs, gc_bhs, beta_bhs = heads(sm[:, 0:H]), heads(sm[:, H:2 * H]), heads(sm[:, 2 * H:3 * H])
    c_col, c_row = c_bhs[..., None], c_bhs[:, :, None, :]
    gcr5 = gc_bhs.reshape(B, H, N, 1, CHUNK)
    betar5 = beta_bhs.reshape(B, H, N, 1, CHUNK)

    o_fox, o_fox16, lse = fox_fwd(p_fox, c_col, c_row, B, S, H)
    qkvn = gdn_prep_fwd(p_gqkv, gconv_i, B, S, H)
    u_hat, w_t, t_inv = gdn_intra_fwd(qkvn, betar5, gcr5, B, S, H)
    o_gdn, states = gdn_inter_fwd(qkvn, u_hat, w_t, gcr5, B, S, H)
    y_gdn = gdn_post_fwd(o_gdn, p_gz, gdn_norm, H)
    bf_ = matmul(o_fox16, W_bf, "nn", "branch_fox")
    bg_ = matmul(y_gdn, W_bg, "nn", "branch_gdn")
    y = merge_fwd(p_gates, bf_, bg_)
    h1 = matmul(y, W_out, "nn", "out_proj", add=x2)
    hn2 = rmsnorm_fwd(h1, norm_ffn, "rmsnorm_ffn")
    up_g = matmul(hn2, W_up_g, "nn", "up_gate")
    up_v = matmul(hn2, W_up_v, "nn", "up_val")
    act = ffn_gate_fwd(up_g, up_v, fconv_g, fconv_v, B, S)
    h2 = matmul(act, W_down, "nn", "down_proj", add=h1)
    loss_cols, dh2, dh2_16, d_norm_final = final_loss(h2, norm_final.reshape(1, D), tgt)
    loss = lax.psum(0.5 * jnp.sum(loss_cols) / D, ("x", "y", "c"))

    d_act = matmul(dh2_16, W_down, "nt", "d_act")
    dW_down = matmul(act, dh2_16, "tn", "dw_down")
    d_upg, d_upv, d_fconv_g, d_fconv_v = ffn_gate_bwd(up_g, up_v, fconv_g, fconv_v, d_act, B, S)
    d_hn2 = matmul(d_upg, W_up_g, "nt", "d_hn2_g")
    d_hn2 = matmul(d_upv, W_up_v, "nt", "d_hn2_v", add=d_hn2)
    dW_up = jnp.concatenate([matmul(hn2, d_upg, "tn", "dw_up_g"), matmul(hn2, d_upv, "tn", "dw_up_v")], axis=1)
    dh1, dh1_16, d_norm_ffn = rmsnorm_bwd(h1, norm_ffn, d_hn2, dh2, "rmsnorm_ffn_bwd")
    d_y = matmul(dh1_16, W_out, "nt", "d_y")
    dW_out = matmul(y, dh1_16, "tn", "dw_out")
    d_bf, d_bg, d_gates = merge_bwd(p_gates, bf_, bg_, d_y)
    d_ofox = matmul(d_bf, W_bf, "nt", "d_ofox")
    dW_bf = matmul(o_fox16, d_bf, "tn", "dw_bf")
    d_ygdn = matmul(d_bg, W_bg, "nt", "d_ygdn")
    dW_bg = matmul(y_gdn, d_bg, "tn", "dw_bg")

    d_pfox, d_ccol, d_crow = fox_bwd(p_fox, c_col, c_row, o_fox, lse, d_ofox, B, S, H)

    d_ogdn, d_gz, d_gdn_norm = gdn_post_bwd(o_gdn, p_gz, gdn_norm, d_ygdn, H)
    dq_i, dk_i, d_uh, d_wt, dgcr_a = gdn_inter_bwd(qkvn, u_hat, w_t, gcr5, states, d_ogdn, B, S, H)
    d_qkvn, d_betar5, dgcr_b = gdn_intra_bwd(qkvn, betar5, gcr5, t_inv, d_uh, d_wt, dq_i, dk_i, B, S, H)
    d_pgqkv, d_gconv_i = gdn_prep_bwd(p_gqkv, gconv_i, d_qkvn, B, S, H)

    tokens = lambda a: a.reshape(B, H, S).transpose(0, 2, 1).reshape(T, H)
    d_gc = (dgcr_a + dgcr_b).reshape(B, H, S)
    d_sm = jnp.concatenate([tokens(d_ccol.reshape(B, H, S) + d_crow.reshape(B, H, S)), tokens(d_gc), tokens(d_betar5.reshape(B, H, S)),
                            jnp.zeros((T, 128 - 3 * H), F32)], axis=1)
    d_psmall, d_prm = small_bwd(p_small, prm, d_sm, B, S, H)

    d_hn1 = matmul(d_pfox, W_fox, "nt", "d_hn1_fox")
    d_hn1 = matmul(d_pgqkv, W_gqkv, "nt", "d_hn1_gqkv", add=d_hn1)
    d_hn1 = matmul(d_gz, W_gz, "nt", "d_hn1_gz", add=d_hn1)
    d_hn1 = matmul(d_gates, W_gates, "nt", "d_hn1_gates", add=d_hn1)
    d_hn1 = matmul(d_psmall, W_small, "nt", "d_hn1_small", add=d_hn1)
    dW_fox = matmul(hn1, d_pfox, "tn", "dw_fox")
    dW_gqkv = matmul(hn1, d_pgqkv, "tn", "dw_gqkv")
    dW_gz = matmul(hn1, d_gz, "tn", "dw_gz")
    dW_gates = matmul(hn1, d_gates, "tn", "dw_gates")
    dW_small = matmul(hn1, d_psmall, "tn", "dw_small")
    grad_x, _, d_norm_mix = rmsnorm_bwd(x2, norm_mix, d_hn1, dh1, "rmsnorm_mix_bwd")

    dW_in = jnp.concatenate([_deinterleave_heads(dW_fox, H), dW_small[:, 0:H], _deinterleave_heads(dW_gqkv, H),
                             dW_small[:, H:3 * H], dW_gz, dW_gates], axis=1)
    d_gconv = _deinterleave_heads(d_gconv_i, H)
    d_fconv = jnp.concatenate([d_fconv_g, d_fconv_v], axis=1)

    col_shard = lambda g, s: g[:, s * (g.shape[1] // N_CHIP):(s + 1) * (g.shape[1] // N_CHIP)]
    row_shard = lambda g, s: g[s * (g.shape[0] // N_CHIP):(s + 1) * (g.shape[0] // N_CHIP)]
    shard_items = lambda s: [row_shard(dW_bf, s), row_shard(dW_bg, s), row_shard(dW_out, s), row_shard(dW_down, s),
                             col_shard(d_gconv, s), col_shard(d_fconv, s)]
    g_shapes = [a.shape for a in shard_items(0)]
    assert sum(_rows_of(math.prod(s)) for s in g_shapes) <= 2 * Rh
    to_slabs = lambda g: g.reshape(2, g.shape[0] // 2, N_CHIP, g.shape[1] // N_CHIP).transpose(0, 2, 1, 3)
    gpacks = [to_slabs(dW_in), to_slabs(dW_up),
              jnp.stack([_pack_rows(shard_items(s), 2 * Rh, F32).reshape(2, Rh, ROW) for s in range(N_CHIP)], axis=1)]
    gots = exchange_halves(gpacks)
    sums = [add_halves(g, got, idx, "add_halves_%d" % i) for i, (g, got) in enumerate(zip(gpacks, gots))]
    got16s = scatter_chips([s16 for _, s16 in sums])
    mine = [add_chips(s32, g16, idx, "add_chips_%d" % i) for i, ((s32, _), g16) in enumerate(zip(sums, got16s))]
    others = share_halves(mine)
    g_w_in, g_up, g_rows = (jnp.concatenate([jnp.where(cidx == 0, h, o), jnp.where(cidx == 0, o, h)], axis=0)
                            for h, o in zip(mine, others))
    g_bf, g_bg, g_out, g_down, g_gconv, g_fconv = _unpack_rows(g_rows, g_shapes)

    small_items = [d_norm_mix, d_norm_ffn, d_norm_final, d_gdn_norm, d_prm]
    small_shapes = [a.shape for a in small_items]
    sv = allreduce_small(_pack_rows(small_items, 8, F32))
    g_norm_mix, g_norm_ffn, g_norm_final, g_gdn_norm, g_prm = _unpack_rows(sv, small_shapes)
    g_norm_final = g_norm_final.reshape(D)
    g_fbias, g_dtb, g_alog = g_prm[0:1, 0:H], g_prm[0:1, H:2 * H], g_prm[1:2, H:2 * H]

    names = ["norm_mix", "w_in", "fox_f_bias", "gdn_conv_w", "gdn_a_log", "gdn_dt_bias", "gdn_norm", "w_branch_fox",
             "w_branch_gdn", "w_out", "norm_ffn", "w_up", "ffn_conv_w", "w_down", "norm_final"]
    ws = [norm_mix, w_in, fox_f_bias, gdn_conv_w, gdn_a_log, gdn_dt_bias, gdn_norm, w_branch_fox, w_branch_gdn, w_out,
          norm_ffn, w_up, ffn_conv_w, w_down, norm_final]
    ms = [m_norm_mix, m_w_in, m_fox_f_bias, m_gdn_conv_w, m_gdn_a_log, m_gdn_dt_bias, m_gdn_norm, m_w_branch_fox,
          m_w_branch_gdn, m_w_out, m_norm_ffn, m_w_up, m_ffn_conv_w, m_w_down, m_norm_final]
    vs = [v_norm_mix, v_w_in, v_fox_f_bias, v_gdn_conv_w, v_gdn_a_log, v_gdn_dt_bias, v_gdn_norm, v_w_branch_fox,
          v_w_branch_gdn, v_w_out, v_norm_ffn, v_w_up, v_ffn_conv_w, v_w_down, v_norm_final]
    gs = [g_norm_mix, g_w_in, g_fbias, g_gconv, g_alog, g_dtb, g_gdn_norm, g_bf, g_bg, g_out, g_norm_ffn, g_up,
          g_fconv, g_down, g_norm_final]
    gs = [g.reshape(w.shape) for g, w in zip(gs, ws)]
    deltas, new_ms, new_vs = [], [], []
    for nm, w, g, m, v in zip(names, ws, gs, ms, vs):
        if w.ndim == 1:
            d, a, b = adamw(w.reshape(1, -1), g.reshape(1, -1), m.reshape(1, -1), v.reshape(1, -1), "adamw_" + nm)
            d, a, b = d.reshape(w.shape), a.reshape(w.shape), b.reshape(w.shape)
        else:
            d, a, b = adamw(w, g, m, v, "adamw_" + nm)
        deltas.append(d)
        new_ms.append(a)
        new_vs.append(b)

    return (loss, grad_x.reshape(B, S, D), *gs, *deltas, *new_ms, *new_vs)
```

```python
import functools
import math

import jax
import jax.numpy as jnp
from jax import lax
from jax.experimental import pallas as pl
from jax.experimental.pallas import tpu as pltpu

F32 = jnp.float32
BF16 = jnp.bfloat16
HEAD = 128
CHUNK = 64
GDN_CONV = 4
FFN_CONV = 3
EPS = 1e-6
NEG = -1e30
ROW = 1024
ATT_TILE = 512
MM_WEIGHT_TILE_BYTES = 8 << 20
N_CHIP = 4
N_DEV = 8
MESH = pl.DeviceIdType.MESH
HI = lax.Precision.HIGH
EXACT = lax.Precision.HIGHEST

ADAM_LR, ADAM_B1, ADAM_B2, ADAM_EPS, ADAM_WD, ADAM_STEP = 0.001, 0.9, 0.999, 1e-08, 0.01, 10


def _tile(n, cap, unit=128):
    best = None
    t = unit
    while t <= min(n, cap):
        if n % t == 0:
            best = t
        t += unit
    return best if best is not None else n


def _params(*sem):
    return pltpu.CompilerParams(dimension_semantics=sem)


_NN = (((1,), (0,)), ((), ()))
_NT = (((1,), (1,)), ((), ()))
_TN = (((0,), (0,)), ((), ()))


def _dg(a, b, dims, hi):
    if hi:
        return lax.dot_general(a, b, dims, precision=HI, preferred_element_type=F32)
    return lax.dot_general(a.astype(BF16), b.astype(BF16), dims, preferred_element_type=F32)


class _RawOps:
    @staticmethod
    def nn(a, b, hi=False):
        return _dg(a, b, _NN, hi)

    @staticmethod
    def nt(a, b, hi=False):
        return _dg(a, b, _NT, hi)

    @staticmethod
    def tn(a, b, hi=False):
        return _dg(a, b, _TN, hi)


def _make_diff_ops():
    def build(hi):
        @jax.custom_vjp
        def nn(a, b):
            return _dg(a, b, _NN, hi)

        nn.defvjp(lambda a, b: (_dg(a, b, _NN, hi), (a, b)),
                  lambda r, g: (_dg(g, r[1], _NT, hi), _dg(r[0], g, _TN, hi)))

        @jax.custom_vjp
        def nt(a, b):
            return _dg(a, b, _NT, hi)

        nt.defvjp(lambda a, b: (_dg(a, b, _NT, hi), (a, b)),
                  lambda r, g: (_dg(g, r[1], _NN, hi), _dg(g, r[0], _TN, hi)))

        @jax.custom_vjp
        def tn(a, b):
            return _dg(a, b, _TN, hi)

        tn.defvjp(lambda a, b: (_dg(a, b, _TN, hi), (a, b)),
                  lambda r, g: (_dg(r[1], g, _NT, hi), _dg(r[0], g, _NN, hi)))
        return nn, nt, tn

    lo, hi_ = build(False), build(True)

    class _DiffOps:
        @staticmethod
        def nn(a, b, hi=False):
            return (hi_ if hi else lo)[0](a, b)

        @staticmethod
        def nt(a, b, hi=False):
            return (hi_ if hi else lo)[1](a, b)

        @staticmethod
        def tn(a, b, hi=False):
            return (hi_ if hi else lo)[2](a, b)

    return _DiffOps


_DiffOps = _make_diff_ops()


def _sigmoid(x):
    return 1.0 / (1.0 + jnp.exp(-x))


def _mm_tile(n, pref):
    if n % pref == 0:
        return pref
    if n % 1408 == 0:
        return 1408
    return _tile(n, pref)


def matmul(a, b, mode, name, add=None, out_dtype=F32):
    if mode == "nn":
        (M, K), (K2, N) = a.shape, b.shape
    elif mode == "nt":
        (M, K), (N, K2) = a.shape, b.shape
    else:
        (K, M), (K2, N) = a.shape, b.shape
    assert K == K2, (name, a.shape, b.shape)
    tn = _mm_tile(N, 1024)
    if mode == "tn":
        tm = M if M <= 1408 else _mm_tile(M, 1408)
        tk = _mm_tile(K, 1024)
    else:
        tm = _mm_tile(M, 512)
        tk = K if K * tn * 2 <= MM_WEIGHT_TILE_BYTES else _mm_tile(K, 1024)
    nk = K // tk
    dims = {"nn": _NN, "nt": _NT, "tn": _TN}[mode]
    if mode == "tn":
        a_spec = pl.BlockSpec((tk, tm), lambda j, i, k: (k, i))
    else:
        a_spec = pl.BlockSpec((tm, tk), lambda j, i, k: (i, k))
    if mode == "nt":
        b_spec = pl.BlockSpec((tn, tk), lambda j, i, k: (j, k))
    else:
        b_spec = pl.BlockSpec((tk, tn), lambda j, i, k: (k, j))
    o_spec = pl.BlockSpec((tm, tn), lambda j, i, k: (i, j))
    has_add = add is not None

    def body(*refs):
        if has_add:
            a_ref, b_ref, add_ref, o_ref, acc_ref = refs
        else:
            a_ref, b_ref, o_ref, acc_ref = refs
        k = pl.program_id(2)

        @pl.when(k == 0)
        def _():
            acc_ref[...] = jnp.zeros_like(acc_ref)

        acc_ref[...] += lax.dot_general(a_ref[...].astype(BF16), b_ref[...].astype(BF16), dims,
                                        preferred_element_type=F32)

        @pl.when(k == nk - 1)
        def _():
            r = acc_ref[...]
            if has_add:
                r = r + add_ref[...]
            o_ref[...] = r.astype(out_dtype)

    in_specs = [a_spec, b_spec] + ([o_spec] if has_add else [])
    args = (a, b) + ((add,) if has_add else ())
    return pl.pallas_call(
        body, name=name, grid=(N // tn, M // tm, nk), in_specs=in_specs, out_specs=o_spec,
        out_shape=jax.ShapeDtypeStruct((M, N), out_dtype),
        scratch_shapes=[pltpu.VMEM((tm, tn), F32)],
        compiler_params=_params("parallel", "parallel", "arbitrary"),
    )(*args)


def rmsnorm_fwd(x, g, name):
    T, D = x.shape
    tm = _tile(T, 512, 8)

    def body(x_ref, g_ref, o_ref):
        xv = x_ref[...]
        r = lax.rsqrt(jnp.mean(xv * xv, axis=-1, keepdims=True) + EPS)
        o_ref[...] = (xv * r * g_ref[...]).astype(BF16)

    return pl.pallas_call(
        body, name=name, grid=(T // tm,),
        in_specs=[pl.BlockSpec((tm, D), lambda i: (i, 0)), pl.BlockSpec((1, D), lambda i: (0, 0))],
        out_specs=pl.BlockSpec((tm, D), lambda i: (i, 0)),
        out_shape=jax.ShapeDtypeStruct((T, D), BF16),
        compiler_params=_params("parallel"),
    )(x, g)


def rmsnorm_bwd(x, g, dy, dres, name):
    T, D = x.shape
    tm = _tile(T, 512, 8)

    def body(x_ref, g_ref, dy_ref, dres_ref, dx_ref, dx16_ref, dg_ref):
        @pl.when(pl.program_id(0) == 0)
        def _():
            dg_ref[...] = jnp.zeros_like(dg_ref)

        xv = x_ref[...]
        r = lax.rsqrt(jnp.mean(xv * xv, axis=-1, keepdims=True) + EPS)
        xh = xv * r
        dyv = dy_ref[...]
        dg_ref[...] += jnp.sum(dyv * xh, axis=0, keepdims=True)
        dxh = dyv * g_ref[...]
        dx = dres_ref[...] + r * (dxh - xh * jnp.mean(dxh * xh, axis=-1, keepdims=True))
        dx_ref[...] = dx
        dx16_ref[...] = dx.astype(BF16)

    row = pl.BlockSpec((tm, D), lambda i: (i, 0))
    vec = pl.BlockSpec((1, D), lambda i: (0, 0))
    return pl.pallas_call(
        body, name=name, grid=(T // tm,), in_specs=[row, vec, row, row], out_specs=[row, row, vec],
        out_shape=[jax.ShapeDtypeStruct((T, D), F32), jax.ShapeDtypeStruct((T, D), BF16),
                   jax.ShapeDtypeStruct((1, D), F32)],
        compiler_params=_params("arbitrary"),
    )(x, g, dy, dres)


def final_loss(h, g, target):
    T, D = h.shape
    tm = _tile(T, 512, 8)

    def body(h_ref, g_ref, t_ref, loss_ref, dh_ref, dh16_ref, dg_ref):
        @pl.when(pl.program_id(0) == 0)
        def _():
            loss_ref[...] = jnp.zeros_like(loss_ref)
            dg_ref[...] = jnp.zeros_like(dg_ref)

        hv = h_ref[...]
        r = lax.rsqrt(jnp.mean(hv * hv, axis=-1, keepdims=True) + EPS)
        xh = hv * r
        err = xh * g_ref[...] - t_ref[...]
        loss_ref[...] += jnp.sum(err * err, axis=0, keepdims=True)
        dy = err * (1.0 / D)
        dg_ref[...] += jnp.sum(dy * xh, axis=0, keepdims=True)
        dxh = dy * g_ref[...]
        dh = r * (dxh - xh * jnp.mean(dxh * xh, axis=-1, keepdims=True))
        dh_ref[...] = dh
        dh16_ref[...] = dh.astype(BF16)

    row = pl.BlockSpec((tm, D), lambda i: (i, 0))
    vec = pl.BlockSpec((1, D), lambda i: (0, 0))
    return pl.pallas_call(
        body, name="final_loss", grid=(T // tm,), in_specs=[row, vec, row], out_specs=[vec, row, row, vec],
        out_shape=[jax.ShapeDtypeStruct((1, D), F32), jax.ShapeDtypeStruct((T, D), F32),
                   jax.ShapeDtypeStruct((T, D), BF16), jax.ShapeDtypeStruct((1, D), F32)],
        compiler_params=_params("arbitrary"),
    )(h, g, target)


def _shift_down(x, k):
    if k == 0:
        return x
    rows = lax.broadcasted_iota(jnp.int32, x.shape, 0)
    return jnp.where(rows >= k, pltpu.roll(x, k, 0), 0.0)


def _shift_up(x, k):
    if k == 0:
        return x
    s = x.shape[0]
    rows = lax.broadcasted_iota(jnp.int32, x.shape, 0)
    return jnp.where(rows < s - k, pltpu.roll(x, s - k, 0), 0.0)


def _conv_fwd(x, w_ref, kw):
    y = x * w_ref[kw - 1:kw, :]
    for i in range(kw - 1):
        y = y + _shift_down(x, kw - 1 - i) * w_ref[i:i + 1, :]
    return y


def _conv_bwd(x, dy, w_ref, kw):
    dx = dy * w_ref[kw - 1:kw, :]
    dws = []
    for i in range(kw - 1):
        dx = dx + _shift_up(dy, kw - 1 - i) * w_ref[i:i + 1, :]
        dws.append(jnp.sum(dy * _shift_down(x, kw - 1 - i), axis=0, keepdims=True))
    dws.append(jnp.sum(dy * x, axis=0, keepdims=True))
    return dx, dws


def ffn_gate_fwd(up_g, up_v, cw_g, cw_v, B, S):
    T, Fd = up_g.shape
    tc = _tile(Fd, 256)

    def body(g_ref, v_ref, wg_ref, wv_ref, o_ref):
        ug = _conv_fwd(g_ref[...], wg_ref, FFN_CONV)
        uv = _conv_fwd(v_ref[...], wv_ref, FFN_CONV)
        o_ref[...] = (ug * _sigmoid(ug) * uv).astype(BF16)

    blk = pl.BlockSpec((S, tc), lambda b, j: (b, j))
    wblk = pl.BlockSpec((FFN_CONV, tc), lambda b, j: (0, j))
    return pl.pallas_call(
        body, name="ffn_gate_fwd", grid=(B, Fd // tc), in_specs=[blk, blk, wblk, wblk], out_specs=blk,
        out_shape=jax.ShapeDtypeStruct((T, Fd), BF16), compiler_params=_params("parallel", "parallel"),
    )(up_g, up_v, cw_g, cw_v)


def ffn_gate_bwd(up_g, up_v, cw_g, cw_v, d_act, B, S):
    T, Fd = up_g.shape
    tc = _tile(Fd, 256)

    def body(g_ref, v_ref, wg_ref, wv_ref, da_ref, dg_ref, dv_ref, dwg_ref, dwv_ref):
        @pl.when(pl.program_id(1) == 0)
        def _():
            dwg_ref[...] = jnp.zeros_like(dwg_ref)
            dwv_ref[...] = jnp.zeros_like(dwv_ref)

        xg, xv = g_ref[...], v_ref[...]
        ug = _conv_fwd(xg, wg_ref, FFN_CONV)
        uv = _conv_fwd(xv, wv_ref, FFN_CONV)
        da = da_ref[...]
        sg = _sigmoid(ug)
        d_ug = da * uv * (sg + ug * sg * (1.0 - sg))
        d_uv = da * ug * sg
        dxg, dwg = _conv_bwd(xg, d_ug, wg_ref, FFN_CONV)
        dxv, dwv = _conv_bwd(xv, d_uv, wv_ref, FFN_CONV)
        dg_ref[...] = dxg.astype(BF16)
        dv_ref[...] = dxv.astype(BF16)
        for i in range(FFN_CONV):
            dwg_ref[i:i + 1, :] += dwg[i]
            dwv_ref[i:i + 1, :] += dwv[i]

    blk = pl.BlockSpec((S, tc), lambda j, b: (b, j))
    wblk = pl.BlockSpec((FFN_CONV, tc), lambda j, b: (0, j))
    return pl.pallas_call(
        body, name="ffn_gate_bwd", grid=(Fd // tc, B), in_specs=[blk, blk, wblk, wblk, blk],
        out_specs=[blk, blk, wblk, wblk],
        out_shape=[jax.ShapeDtypeStruct((T, Fd), BF16), jax.ShapeDtypeStruct((T, Fd), BF16),
                   jax.ShapeDtypeStruct((FFN_CONV, Fd), F32), jax.ShapeDtypeStruct((FFN_CONV, Fd), F32)],
        compiler_params=_params("parallel", "arbitrary"),
    )(up_g, up_v, cw_g, cw_v, d_act)


def merge_fwd(p_gates, bf_, bg_):
    T, D = bf_.shape
    tm = _tile(T, 512, 8)

    def body(gf_ref, gg_ref, bf_ref, bg_ref, o_ref):
        o_ref[...] = (_sigmoid(gf_ref[...]) * bf_ref[...] + _sigmoid(gg_ref[...]) * bg_ref[...]).astype(BF16)

    lo = pl.BlockSpec((tm, D), lambda i: (i, 0))
    hi = pl.BlockSpec((tm, D), lambda i: (i, 1))
    return pl.pallas_call(
        body, name="merge_fwd", grid=(T // tm,), in_specs=[lo, hi, lo, lo], out_specs=lo,
        out_shape=jax.ShapeDtypeStruct((T, D), BF16), compiler_params=_params("parallel"),
    )(p_gates, p_gates, bf_, bg_)


def merge_bwd(p_gates, bf_, bg_, dy):
    T, D = bf_.shape
    tm = _tile(T, 512, 8)

    def body(gf_ref, gg_ref, bf_ref, bg_ref, dy_ref, dbf_ref, dbg_ref, dgate_ref):
        d = dy_ref[...]
        sf, sg = _sigmoid(gf_ref[...]), _sigmoid(gg_ref[...])
        dbf_ref[...] = (d * sf).astype(BF16)
        dbg_ref[...] = (d * sg).astype(BF16)
        dgate_ref[:, 0:D] = (d * bf_ref[...] * sf * (1.0 - sf)).astype(BF16)
        dgate_ref[:, D:2 * D] = (d * bg_ref[...] * sg * (1.0 - sg)).astype(BF16)

    lo = pl.BlockSpec((tm, D), lambda i: (i, 0))
    hi = pl.BlockSpec((tm, D), lambda i: (i, 1))
    both = pl.BlockSpec((tm, 2 * D), lambda i: (i, 0))
    return pl.pallas_call(
        body, name="merge_bwd", grid=(T // tm,), in_specs=[lo, hi, lo, lo, lo], out_specs=[lo, lo, both],
        out_shape=[jax.ShapeDtypeStruct((T, D), BF16), jax.ShapeDtypeStruct((T, D), BF16),
                   jax.ShapeDtypeStruct((T, 2 * D), BF16)],
        compiler_params=_params("parallel"),
    )(p_gates, p_gates, bf_, bg_, dy)


def fox_fwd(p_fox, c_col, c_row, B, S, H, rider=None):
    T = B * S
    t = _tile(S, ATT_TILE)
    nq = S // t
    scale = HEAD ** -0.5

    def body(q_ref, k_ref, v_ref, cq_ref, cr_ref, o_ref, o16_ref, lse_ref):
        i = pl.program_id(2)
        q = q_ref[...]
        cq = cq_ref[...]
        row = lax.broadcasted_iota(jnp.int32, (t, t), 0)
        col = lax.broadcasted_iota(jnp.int32, (t, t), 1)

        def step(j, carry, diagonal):
            m, l, acc = carry
            off = pl.multiple_of(j * t, t)
            k = k_ref[pl.ds(off, t), :]
            v = v_ref[pl.ds(off, t), :]
            s = lax.dot_general(q, k, _NT, preferred_element_type=F32) * scale + (cq - cr_ref[:, pl.ds(off, t)])
            if diagonal:
                s = jnp.where(col <= row, s, NEG)
            m_new = jnp.maximum(m, jnp.max(s, axis=-1, keepdims=True))
            alpha = jnp.exp(m - m_new)
            p = jnp.exp(s - m_new)
            l = alpha * l + jnp.sum(p, axis=-1, keepdims=True)
            acc = alpha * acc + lax.dot_general(p.astype(BF16), v, _NN, preferred_element_type=F32)
            return m_new, l, acc

        m0 = jnp.full((t, 1), NEG, F32)
        below = lax.fori_loop(0, i, functools.partial(step, diagonal=False),
                              (m0, jnp.zeros((t, 1), F32), jnp.zeros((t, HEAD), F32)))
        m, l, acc = step(i, below, diagonal=True)
        o = acc / l
        o_ref[...] = o
        o16_ref[...] = o.astype(BF16)
        lse_ref[...] = m + jnp.log(l)

    return _hosted_call(
        body, rider, name="fox_fwd", grid=(B, H, nq),
        in_specs=[pl.BlockSpec((t, HEAD), lambda b, h, i: (b * nq + i, 3 * h)),
                  pl.BlockSpec((S, HEAD), lambda b, h, i: (b, 3 * h + 1)),
                  pl.BlockSpec((S, HEAD), lambda b, h, i: (b, 3 * h + 2)),
                  pl.BlockSpec((None, None, t, 1), lambda b, h, i: (b, h, i, 0)),
                  pl.BlockSpec((None, None, 1, S), lambda b, h, i: (b, h, 0, 0))],
        out_specs=[pl.BlockSpec((t, HEAD), lambda b, h, i: (b * nq + i, h)),
                   pl.BlockSpec((t, HEAD), lambda b, h, i: (b * nq + i, h)),
                   pl.BlockSpec((None, None, t, 1), lambda b, h, i: (b, h, i, 0))],
        out_shape=[jax.ShapeDtypeStruct((T, H * HEAD), F32), jax.ShapeDtypeStruct((T, H * HEAD), BF16),
                   jax.ShapeDtypeStruct((B, H, S, 1), F32)],
        scratch_shapes=[], semantics=("parallel", "parallel", "arbitrary"),
    )(p_fox, p_fox, p_fox, c_col, c_row)


def fox_bwd(p_fox, c_col, c_row, o, lse, do, B, S, H, rider=None):
    T = B * S
    t = _tile(S, ATT_TILE)
    n = S // t
    scale = HEAD ** -0.5

    def body(q_ref, k_ref, v_ref, cq_ref, cr_ref, o_ref, lse_ref, do_ref, dqkv_ref, dcq_ref, dcr_ref, dq_acc, delta_s):
        row = lax.broadcasted_iota(jnp.int32, (t, t), 0)
        col = lax.broadcasted_iota(jnp.int32, (t, t), 1)

        def prep(i, c):
            rows = pl.ds(pl.multiple_of(i * t, t), t)
            delta_s[rows, :] = jnp.sum(do_ref[rows, :] * o_ref[rows, :], axis=-1, keepdims=True)
            dq_acc[rows, :] = jnp.zeros((t, HEAD), F32)
            dcq_ref[rows, :] = jnp.zeros((t, 1), F32)
            return c

        lax.fori_loop(0, n, prep, 0)

        def kv_step(j, c):
            joff = pl.multiple_of(j * t, t)
            k = k_ref[pl.ds(joff, t), :]
            v = v_ref[pl.ds(joff, t), :]
            crj = cr_ref[:, pl.ds(joff, t)]

            def q_step(i, carry, diagonal):
                dk, dv, dc = carry
                rows = pl.ds(pl.multiple_of(i * t, t), t)
                q = q_ref[rows, :]
                dob = do_ref[rows, :].astype(BF16)
                s = lax.dot_general(q, k, _NT, preferred_element_type=F32) * scale + (cq_ref[rows, :] - crj)
                if diagonal:
                    s = jnp.where(col <= row, s, NEG)
                p = jnp.exp(s - lse_ref[rows, :])
                dp = lax.dot_general(dob, v, _NT, preferred_element_type=F32)
                ds = p * (dp - delta_s[rows, :])
                dsb = ds.astype(BF16)
                dv = dv + lax.dot_general(p.astype(BF16), dob, _TN, preferred_element_type=F32)
                dk = dk + lax.dot_general(dsb, q, _TN, preferred_element_type=F32)
                dq_acc[rows, :] += lax.dot_general(dsb, k, _NN, preferred_element_type=F32) * scale
                dc = dc + jnp.sum(ds, axis=0, keepdims=True)
                dcq_ref[rows, :] += jnp.sum(ds, axis=-1, keepdims=True)
                return dk, dv, dc

            z = jnp.zeros((t, HEAD), F32)
            on_diagonal = q_step(j, (z, z, jnp.zeros((1, t), F32)), diagonal=True)
            dk, dv, dc = lax.fori_loop(j + 1, n, functools.partial(q_step, diagonal=False), on_diagonal)
            dqkv_ref[pl.ds(joff, t), HEAD:2 * HEAD] = (dk * scale).astype(BF16)
            dqkv_ref[pl.ds(joff, t), 2 * HEAD:3 * HEAD] = dv.astype(BF16)
            dcr_ref[:, pl.ds(joff, t)] = -dc
            return c

        lax.fori_loop(0, n, kv_step, 0)
        dqkv_ref[:, 0:HEAD] = dq_acc[...].astype(BF16)

    col_spec = pl.BlockSpec((None, None, S, 1), lambda b, h: (b, h, 0, 0))
    row_spec = pl.BlockSpec((None, None, 1, S), lambda b, h: (b, h, 0, 0))
    head = pl.BlockSpec((S, HEAD), lambda b, h: (b, h))
    return _hosted_call(
        body, rider, name="fox_bwd", grid=(B, H),
        in_specs=[pl.BlockSpec((S, HEAD), lambda b, h: (b, 3 * h)),
                  pl.BlockSpec((S, HEAD), lambda b, h: (b, 3 * h + 1)),
                  pl.BlockSpec((S, HEAD), lambda b, h: (b, 3 * h + 2)),
                  col_spec, row_spec, head, col_spec, head],
        out_specs=[pl.BlockSpec((S, 3 * HEAD), lambda b, h: (b, h)), col_spec, row_spec],
        out_shape=[jax.ShapeDtypeStruct((T, 3 * H * HEAD), BF16), jax.ShapeDtypeStruct((B, H, S, 1), F32),
                   jax.ShapeDtypeStruct((B, H, 1, S), F32)],
        scratch_shapes=[pltpu.VMEM((S, HEAD), F32), pltpu.VMEM((S, 1), F32)], semantics=("parallel", "parallel"),
    )(p_fox, p_fox, p_fox, c_col, c_row, o, lse, do)


def _small_fn(x, b0, b1, H):
    S = x.shape[0]
    lane = lax.broadcasted_iota(jnp.int32, x.shape, 1)
    z = x + b0
    tail = jnp.log1p(jnp.exp(-jnp.abs(z)))
    softplus = jnp.maximum(z, 0.0) + tail
    logsig = -(jnp.maximum(-z, 0.0) + tail)
    g = -jnp.exp(b1) * softplus
    pre = jnp.where(lane < H, logsig, jnp.where(lane < 2 * H, g, 0.0))
    bl = _tile(S, 256, CHUNK)
    r = lax.broadcasted_iota(jnp.int32, (bl, bl), 0)
    c = lax.broadcasted_iota(jnp.int32, (bl, bl), 1)
    tri = (r >= c).astype(F32)
    tri_chunk = jnp.where((r >= c) & (jnp.right_shift(r, 6) == jnp.right_shift(c, 6)), 1.0, 0.0)
    carry = jnp.zeros((1, x.shape[1]), F32)
    parts = []
    for i in range(S // bl):
        blk = pre[i * bl:(i + 1) * bl, :]
        full = lax.dot_general(tri, blk, _NN, precision=EXACT, preferred_element_type=F32) + carry
        chunked = lax.dot_general(tri_chunk, blk, _NN, precision=EXACT, preferred_element_type=F32)
        parts.append(jnp.where(lane[:bl] < H, full, chunked))
        carry = carry + jnp.sum(blk, axis=0, keepdims=True)
    cum = parts[0] if len(parts) == 1 else jnp.concatenate(parts, axis=0)
    return jnp.where(lane < 2 * H, cum, jnp.where(lane < 3 * H, _sigmoid(x), 0.0))


def small_fwd(p_small, prm, B, S, H):
    T = B * S

    def body(x_ref, p_ref, o_ref):
        o_ref[...] = _small_fn(x_ref[...], p_ref[0:1, :], p_ref[1:2, :], H)

    blk = pl.BlockSpec((S, 128), lambda b: (b, 0))
    return pl.pallas_call(
        body, name="small_fwd", grid=(B,), in_specs=[blk, pl.BlockSpec((8, 128), lambda b: (0, 0))], out_specs=blk,
        out_shape=jax.ShapeDtypeStruct((T, 128), F32), compiler_params=_params("parallel"),
    )(p_small, prm)


def small_bwd(p_small, prm, d_out, B, S, H):
    T = B * S

    def body(x_ref, p_ref, d_ref, dx_ref, dp_ref):
        @pl.when(pl.program_id(0) == 0)
        def _():
            dp_ref[...] = jnp.zeros_like(dp_ref)

        _, vjp = jax.vjp(functools.partial(_small_fn, H=H), x_ref[...], p_ref[0:1, :], p_ref[1:2, :])
        dx, db0, db1 = vjp(d_ref[...])
        dx_ref[...] = dx.astype(BF16)
        dp_ref[0:1, :] += db0
        dp_ref[1:2, :] += db1

    blk = pl.BlockSpec((S, 128), lambda b: (b, 0))
    pblk = pl.BlockSpec((8, 128), lambda b: (0, 0))
    return pl.pallas_call(
        body, name="small_bwd", grid=(B,), in_specs=[blk, pblk, blk], out_specs=[blk, pblk],
        out_shape=[jax.ShapeDtypeStruct((T, 128), BF16), jax.ShapeDtypeStruct((8, 128), F32)],
        compiler_params=_params("arbitrary"),
    )(p_small, prm, d_out)


def gdn_prep_fwd(p_gqkv, cw, B, S, H):
    T = B * S

    def body(x_ref, w_ref, o_ref):
        y = _conv_fwd(x_ref[...], w_ref, GDN_CONV)
        a = y * _sigmoid(y)
        rs = lax.rsqrt(jnp.sum(a * a, axis=-1, keepdims=True) + EPS)
        is_qk = (pl.program_id(1) % 3) < 2
        o_ref[...] = a * jnp.where(is_qk, rs, 1.0)

    blk = pl.BlockSpec((S, HEAD), lambda b, n: (b, n))
    wblk = pl.BlockSpec((GDN_CONV, HEAD), lambda b, n: (0, n))
    return pl.pallas_call(
        body, name="gdn_prep_fwd", grid=(B, 3 * H), in_specs=[blk, wblk], out_specs=blk,
        out_shape=jax.ShapeDtypeStruct((T, 3 * H * HEAD), F32), compiler_params=_params("parallel", "parallel"),
    )(p_gqkv, cw)


def gdn_prep_bwd(p_gqkv, cw, d_out, B, S, H):
    T = B * S

    def body(x_ref, w_ref, d_ref, dx_ref, dw_ref):
        @pl.when(pl.program_id(1) == 0)
        def _():
            dw_ref[...] = jnp.zeros_like(dw_ref)

        x = x_ref[...]
        y = _conv_fwd(x, w_ref, GDN_CONV)
        sg = _sigmoid(y)
        a = y * sg
        rs = lax.rsqrt(jnp.sum(a * a, axis=-1, keepdims=True) + EPS)
        d = d_ref[...]
        out = a * rs
        da_qk = rs * (d - out * jnp.sum(d * out, axis=-1, keepdims=True))
        is_qk = (pl.program_id(0) % 3) < 2
        da = jnp.where(is_qk, da_qk, d)
        dy = da * (sg + y * sg * (1.0 - sg))
        dx, dws = _conv_bwd(x, dy, w_ref, GDN_CONV)
        dx_ref[...] = dx.astype(BF16)
        for i in range(GDN_CONV):
            dw_ref[i:i + 1, :] += dws[i]

    blk = pl.BlockSpec((S, HEAD), lambda n, b: (b, n))
    wblk = pl.BlockSpec((GDN_CONV, HEAD), lambda n, b: (0, n))
    return pl.pallas_call(
        body, name="gdn_prep_bwd", grid=(3 * H, B), in_specs=[blk, wblk, blk], out_specs=[blk, wblk],
        out_shape=[jax.ShapeDtypeStruct((T, 3 * H * HEAD), BF16), jax.ShapeDtypeStruct((GDN_CONV, 3 * H * HEAD), F32)],
        compiler_params=_params("parallel", "arbitrary"),
    )(p_gqkv, cw, d_out)


@jax.custom_vjp
def _given_inverse(a, t):
    return t


def _given_inverse_fwd(a, t):
    return t, t


def _given_inverse_bwd(t, g):
    x = _dg(t, g, _TN, True)
    return -_dg(x, t, _NT, True), jnp.zeros_like(t)


_given_inverse.defvjp(_given_inverse_fwd, _given_inverse_bwd)


def _to_col(row):
    r = lax.broadcasted_iota(jnp.int32, (CHUNK, CHUNK), 0)
    c = lax.broadcasted_iota(jnp.int32, (CHUNK, CHUNK), 1)
    return jnp.sum(jnp.where(r == c, row, 0.0), axis=1, keepdims=True)


def _intra_fn(k, v, beta_r, gcr, ops, t_known=None):
    n = len(k)
    r = lax.broadcasted_iota(jnp.int32, (CHUNK, CHUNK), 0)
    c = lax.broadcasted_iota(jnp.int32, (CHUNK, CHUNK), 1)
    beta = [_to_col(beta_r[i]) for i in range(n)]
    gcc = [_to_col(gcr[i]) for i in range(n)]
    decay = [jnp.exp(jnp.where(r > c, gcc[i] - gcr[i], NEG)) for i in range(n)]
    kb = [k[i] * beta[i] for i in range(n)]
    a = [ops.nt(kb[i], k[i]) * decay[i] for i in range(n)]
    if t_known is None:
        p = [-a[i] for i in range(n)]
        tm = [jnp.where(r == c, 1.0, 0.0) + p[i] for i in range(n)]
        for _ in range(5):
            p = [ops.nn(p[i], p[i], hi=True) for i in range(n)]
            tm = [tm[i] + ops.nn(tm[i], p[i], hi=True) for i in range(n)]
    else:
        tm = [_given_inverse(a[i], t_known[i]) for i in range(n)]
    u_hat = [ops.nn(tm[i], v[i] * beta[i], hi=True) for i in range(n)]
    w = [ops.nn(tm[i], kb[i] * jnp.exp(gcc[i]), hi=True) for i in range(n)]
    return tuple(u_hat), tuple(w), tuple(tm)


INTRA_NB = 8


def gdn_intra_fwd(qkvn, betar5, gcr5, B, S, H, rider=None):
    T = B * S
    N = S // CHUNK
    nb = min(INTRA_NB, N)
    rows = nb * CHUNK
    ns = N // nb

    def body(k_ref, v_ref, b_ref, gr_ref, uh_ref, w_ref, t_ref):
        sls = [slice(ci * CHUNK, (ci + 1) * CHUNK) for ci in range(nb)]
        u_hat, w, tm = _intra_fn(tuple(k_ref[sl, :] for sl in sls), tuple(v_ref[sl, :] for sl in sls),
                                 tuple(b_ref[ci] for ci in range(nb)), tuple(gr_ref[ci] for ci in range(nb)), _RawOps)
        for ci, sl in enumerate(sls):
            uh_ref[sl, :] = u_hat[ci]
            w_ref[sl, :] = w[ci]
            t_ref[ci] = tm[ci]

    rowspec = pl.BlockSpec((None, None, nb, 1, CHUNK), lambda b, h, i: (b, h, i, 0, 0))
    sqspec = pl.BlockSpec((None, None, nb, CHUNK, CHUNK), lambda b, h, i: (b, h, i, 0, 0))
    out = pl.BlockSpec((rows, HEAD), lambda b, h, i: (b * ns + i, h))
    return _hosted_call(
        body, rider, name="gdn_intra_fwd", grid=(B, H, ns),
        in_specs=[pl.BlockSpec((rows, HEAD), lambda b, h, i: (b * ns + i, 3 * h + 1)),
                  pl.BlockSpec((rows, HEAD), lambda b, h, i: (b * ns + i, 3 * h + 2)),
                  rowspec, rowspec],
        out_specs=[out, out, sqspec],
        out_shape=[jax.ShapeDtypeStruct((T, H * HEAD), F32), jax.ShapeDtypeStruct((T, H * HEAD), F32),
                   jax.ShapeDtypeStruct((B, H, N, CHUNK, CHUNK), F32)],
        scratch_shapes=[], semantics=("parallel", "parallel", "parallel"),
    )(qkvn, qkvn, betar5, gcr5)


def gdn_intra_bwd(qkvn, betar5, gcr5, t_inv, d_uh, d_w, dq_in, dk_in, B, S, H):
    T = B * S
    N = S // CHUNK
    nb = min(INTRA_NB, N)
    rows = nb * CHUNK
    ns = N // nb

    def body(k_ref, v_ref, b_ref, gr_ref, t_ref, duh_ref, dw_ref, dq_ref, dk_ref, o_ref, db_ref, dgr_ref):
        sls = [slice(ci * CHUNK, (ci + 1) * CHUNK) for ci in range(nb)]
        chunks = range(nb)
        _, vjp = jax.vjp(
            functools.partial(_intra_fn, ops=_DiffOps, t_known=tuple(t_ref[ci] for ci in chunks)),
            tuple(k_ref[sl, :] for sl in sls), tuple(v_ref[sl, :] for sl in sls), tuple(b_ref[ci] for ci in chunks),
            tuple(gr_ref[ci] for ci in chunks))
        zero = jnp.zeros((CHUNK, CHUNK), F32)
        dk, dv, db, dgr = vjp((tuple(duh_ref[sl, :] for sl in sls), tuple(dw_ref[sl, :] for sl in sls),
                               tuple(zero for _ in chunks)))
        for ci, sl in enumerate(sls):
            o_ref[sl, 0:HEAD] = dq_ref[sl, :]
            o_ref[sl, HEAD:2 * HEAD] = dk[ci] + dk_ref[sl, :]
            o_ref[sl, 2 * HEAD:3 * HEAD] = dv[ci]
            db_ref[ci] = db[ci]
            dgr_ref[ci] = dgr[ci]

    rowspec = pl.BlockSpec((None, None, nb, 1, CHUNK), lambda b, h, i: (b, h, i, 0, 0))
    sqspec = pl.BlockSpec((None, None, nb, CHUNK, CHUNK), lambda b, h, i: (b, h, i, 0, 0))
    head = pl.BlockSpec((rows, HEAD), lambda b, h, i: (b * ns + i, h))
    return pl.pallas_call(
        body, name="gdn_intra_bwd", grid=(B, H, ns),
        in_specs=[pl.BlockSpec((rows, HEAD), lambda b, h, i: (b * ns + i, 3 * h + 1)),
                  pl.BlockSpec((rows, HEAD), lambda b, h, i: (b * ns + i, 3 * h + 2)),
                  rowspec, rowspec, sqspec, head, head, head, head],
        out_specs=[pl.BlockSpec((rows, 3 * HEAD), lambda b, h, i: (b * ns + i, h)), rowspec, rowspec],
        out_shape=[jax.ShapeDtypeStruct((T, 3 * H * HEAD), F32),
                   jax.ShapeDtypeStruct((B, H, N, 1, CHUNK), F32), jax.ShapeDtypeStruct((B, H, N, 1, CHUNK), F32)],
        compiler_params=_params("parallel", "parallel", "parallel"),
    )(qkvn, qkvn, betar5, gcr5, t_inv, d_uh, d_w, dq_in, dk_in)


def _inter_fn(q, k, u_hat, w, gcr, state, ops):
    n = len(q)
    r = lax.broadcasted_iota(jnp.int32, (CHUNK, CHUNK), 0)
    c = lax.broadcasted_iota(jnp.int32, (CHUNK, CHUNK), 1)
    last = lax.broadcasted_iota(jnp.int32, (1, CHUNK), 1) == CHUNK - 1
    gcc = [_to_col(gcr[i]) for i in range(n)]
    gl = [jnp.sum(jnp.where(last, gcr[i], 0.0), axis=1, keepdims=True) for i in range(n)]
    decay = [jnp.exp(jnp.where(r >= c, gcc[i] - gcr[i], NEG)) for i in range(n)]
    qs = [q[i] * (HEAD ** -0.5) for i in range(n)]
    ws = [ops.nn(w[i], state[i]) for i in range(n)]
    qst = [ops.nn(qs[i] * jnp.exp(gcc[i]), state[i]) for i in range(n)]
    attn = [ops.nt(qs[i], k[i]) * decay[i] for i in range(n)]
    u = [u_hat[i] - ws[i] for i in range(n)]
    o = [qst[i] + ops.nn(attn[i], u[i]) for i in range(n)]
    kdu = [ops.tn(k[i] * jnp.exp(gl[i] - gcc[i]), u[i]) for i in range(n)]
    new_state = [state[i] * jnp.exp(gl[i]) + kdu[i] for i in range(n)]
    return tuple(o), tuple(new_state)


INTER_HEADS = 2


def _inter_specs(S, N, hp):
    qk = []
    for hh in range(hp):
        qk.append(pl.BlockSpec((S, HEAD), lambda b, g, hh=hh: (b, 3 * (hp * g + hh))))
        qk.append(pl.BlockSpec((S, HEAD), lambda b, g, hh=hh: (b, 3 * (hp * g + hh) + 1)))
    heads = pl.BlockSpec((S, hp * HEAD), lambda b, g: (b, g))
    rowspec = pl.BlockSpec((None, hp, N, 1, CHUNK), lambda b, g: (b, g, 0, 0, 0))
    stspec = pl.BlockSpec((None, hp, N, HEAD, HEAD), lambda b, g: (b, g, 0, 0, 0))
    return qk, heads, rowspec, stspec


def gdn_inter_fwd(qkvn, u_hat, w, gcr5, B, S, H):
    T = B * S
    N = S // CHUNK
    hp = INTER_HEADS if H % INTER_HEADS == 0 else 1
    hs = range(hp)

    def body(*refs):
        qk_refs, (uh_ref, w_ref, gr_ref, o_ref, st_ref, s_scr) = refs[:2 * hp], refs[2 * hp:]
        s_scr[...] = jnp.zeros_like(s_scr)

        def step(n, c):
            rows = pl.ds(pl.multiple_of(n * CHUNK, CHUNK), CHUNK)
            st = tuple(s_scr[hh] for hh in hs)
            for hh in hs:
                st_ref[hh, n] = st[hh]
            o, new = _inter_fn(tuple(qk_refs[2 * hh][rows, :] for hh in hs), tuple(qk_refs[2 * hh + 1][rows, :] for hh in hs),
                               tuple(uh_ref[rows, hh * HEAD:(hh + 1) * HEAD] for hh in hs),
                               tuple(w_ref[rows, hh * HEAD:(hh + 1) * HEAD] for hh in hs),
                               tuple(gr_ref[hh, n] for hh in hs), st, _RawOps)
            for hh in hs:
                o_ref[rows, hh * HEAD:(hh + 1) * HEAD] = o[hh]
                s_scr[hh] = new[hh]
            return c

        lax.fori_loop(0, N, step, 0)

    qk, heads, rowspec, stspec = _inter_specs(S, N, hp)
    return pl.pallas_call(
        body, name="gdn_inter_fwd", grid=(B, H // hp),
        in_specs=qk + [heads, heads, rowspec], out_specs=[heads, stspec],
        out_shape=[jax.ShapeDtypeStruct((T, H * HEAD), F32), jax.ShapeDtypeStruct((B, H, N, HEAD, HEAD), F32)],
        scratch_shapes=[pltpu.VMEM((hp, HEAD, HEAD), F32)],
        compiler_params=_params("parallel", "parallel"),
    )(*([qkvn] * (2 * hp)), u_hat, w, gcr5)


def gdn_inter_bwd(qkvn, u_hat, w, gcr5, states, d_o, B, S, H, rider=None):
    T = B * S
    N = S // CHUNK
    hp = INTER_HEADS if H % INTER_HEADS == 0 else 1
    hs = range(hp)

    def body(*refs):
        qk_refs = refs[:2 * hp]
        uh_ref, w_ref, gr_ref, st_ref, do_ref, dq_ref, dk_ref, duh_ref, dw_ref, dgr_ref, ds_scr = refs[2 * hp:]
        ds_scr[...] = jnp.zeros_like(ds_scr)
        cols = [slice(hh * HEAD, (hh + 1) * HEAD) for hh in hs]

        def step(i, c):
            n = N - 1 - i
            rows = pl.ds(pl.multiple_of(n * CHUNK, CHUNK), CHUNK)
            _, vjp = jax.vjp(functools.partial(_inter_fn, ops=_DiffOps),
                             tuple(qk_refs[2 * hh][rows, :] for hh in hs), tuple(qk_refs[2 * hh + 1][rows, :] for hh in hs),
                             tuple(uh_ref[rows, cols[hh]] for hh in hs), tuple(w_ref[rows, cols[hh]] for hh in hs),
                             tuple(gr_ref[hh, n] for hh in hs), tuple(st_ref[hh, n] for hh in hs))
            dq, dk, duh, dw, dgr, ds = vjp((tuple(do_ref[rows, cols[hh]] for hh in hs), tuple(ds_scr[hh] for hh in hs)))
            for hh in hs:
                dq_ref[rows, cols[hh]] = dq[hh]
                dk_ref[rows, cols[hh]] = dk[hh]
                duh_ref[rows, cols[hh]] = duh[hh]
                dw_ref[rows, cols[hh]] = dw[hh]
                dgr_ref[hh, n] = dgr[hh]
                ds_scr[hh] = ds[hh]
            return c

        lax.fori_loop(0, N, step, 0)

    qk, heads, rowspec, stspec = _inter_specs(S, N, hp)
    hshape = jax.ShapeDtypeStruct((T, H * HEAD), F32)
    return _hosted_call(
        body, rider, name="gdn_inter_bwd", grid=(B, H // hp),
        in_specs=qk + [heads, heads, rowspec, stspec, heads],
        out_specs=[heads, heads, heads, heads, rowspec],
        out_shape=[hshape, hshape, hshape, hshape, jax.ShapeDtypeStruct((B, H, N, 1, CHUNK), F32)],
        scratch_shapes=[pltpu.VMEM((hp, HEAD, HEAD), F32)], semantics=("parallel", "parallel"),
    )(*([qkvn] * (2 * hp)), u_hat, w, gcr5, states, d_o)


def gdn_post_fwd(o, p_gz, g, H):
    T = o.shape[0]
    tm = _tile(T, 1024, 8)

    def body(o_ref, z_ref, g_ref, y_ref):
        ov, z = o_ref[...], z_ref[...]
        r = lax.rsqrt(jnp.mean(ov * ov, axis=-1, keepdims=True) + EPS)
        y_ref[...] = (ov * r * g_ref[...] * z * _sigmoid(z)).astype(BF16)

    blk = pl.BlockSpec((tm, HEAD), lambda i, h: (i, h))
    return pl.pallas_call(
        body, name="gdn_post_fwd", grid=(T // tm, H), in_specs=[blk, blk, pl.BlockSpec((1, HEAD), lambda i, h: (0, 0))],
        out_specs=blk, out_shape=jax.ShapeDtypeStruct((T, H * HEAD), BF16),
        compiler_params=_params("parallel", "parallel"),
    )(o, p_gz, g)


def gdn_post_bwd(o, p_gz, g, dy, H):
    T = o.shape[0]
    tm = _tile(T, 1024, 8)

    def body(o_ref, z_ref, g_ref, dy_ref, do_ref, dz_ref, dg_ref):
        @pl.when((pl.program_id(0) == 0) & (pl.program_id(1) == 0))
        def _():
            dg_ref[...] = jnp.zeros_like(dg_ref)

        ov, z, d = o_ref[...], z_ref[...], dy_ref[...]
        r = lax.rsqrt(jnp.mean(ov * ov, axis=-1, keepdims=True) + EPS)
        xh = ov * r
        sg = _sigmoid(z)
        sz = z * sg
        d_n = d * sz
        dz_ref[...] = (d * xh * g_ref[...] * (sg + z * sg * (1.0 - sg))).astype(BF16)
        dg_ref[...] += jnp.sum(d_n * xh, axis=0, keepdims=True)
        dxh = d_n * g_ref[...]
        do_ref[...] = r * (dxh - xh * jnp.mean(dxh * xh, axis=-1, keepdims=True))

    blk = pl.BlockSpec((tm, HEAD), lambda i, h: (i, h))
    vec = pl.BlockSpec((1, HEAD), lambda i, h: (0, 0))
    return pl.pallas_call(
        body, name="gdn_post_bwd", grid=(T // tm, H), in_specs=[blk, blk, vec, blk], out_specs=[blk, blk, vec],
        out_shape=[jax.ShapeDtypeStruct((T, H * HEAD), F32), jax.ShapeDtypeStruct((T, H * HEAD), BF16),
                   jax.ShapeDtypeStruct((1, HEAD), F32)],
        compiler_params=_params("arbitrary", "arbitrary"),
    )(o, p_gz, g, dy)


def adamw(w, g, m, v, name):
    shape = w.shape
    lead = (None,) * (w.ndim - 2)
    zeros = (0,) * (w.ndim - 2)
    R, C = shape[-2:]
    g2 = g.reshape(R, C)
    tr = _tile(R, 128, 8)

    def body(w_ref, g_ref, m_ref, v_ref, d_ref, nm_ref, nv_ref):
        gv = g_ref[...]
        nm = ADAM_B1 * m_ref[...] + (1.0 - ADAM_B1) * gv
        nv = ADAM_B2 * v_ref[...] + (1.0 - ADAM_B2) * (gv * gv)
        m_hat = nm / (1.0 - ADAM_B1 ** ADAM_STEP)
        v_hat = nv / (1.0 - ADAM_B2 ** ADAM_STEP)
        d_ref[...] = -ADAM_LR * (m_hat / (jnp.sqrt(v_hat) + ADAM_EPS) + ADAM_WD * w_ref[...])
        nm_ref[...] = nm
        nv_ref[...] = nv

    blk = pl.BlockSpec(lead + (tr, C), lambda i: zeros + (i, 0))
    gblk = pl.BlockSpec((tr, C), lambda i: (i, 0))
    sh = jax.ShapeDtypeStruct(shape, F32)
    return pl.pallas_call(
        body, name=name, grid=(R // tr,), in_specs=[blk, gblk, blk, blk], out_specs=[blk] * 3, out_shape=[sh] * 3,
        compiler_params=_params("parallel"),
    )(w, g2, m, v)


def _place():
    x, y, c = lax.axis_index("x"), lax.axis_index("y"), lax.axis_index("c")
    chips = [(1 - x, y), (x, 1 - y), (1 - x, 1 - y)]
    return x, y, c, chips


_HBM = pl.BlockSpec(memory_space=pltpu.HBM)


def allgather_weights(packs):
    n = len(packs)

    def body(*refs):
        in_refs, out_refs, (send_sems, recv_sems) = refs[:n], refs[n:2 * n], refs[2 * n:]
        x, y, c, chips = _place()
        me_s = 2 * x + y
        me, sibling = (x, y, c), (x, y, 1 - c)
        shards = [2 * chip[0] + chip[1] for chip in chips]

        def copy(a, k, shard, half, to, src=None):
            dst = out_refs[a].at[shard, half]
            return pltpu.make_async_remote_copy(src_ref=dst if src is None else src, dst_ref=dst,
                                                send_sem=send_sems.at[6 * a + k], recv_sem=recv_sems.at[6 * a + k],
                                                device_id=to, device_id_type=MESH)

        first = [copy(a, j, me_s, c, (*chip, c), src=in_refs[a].at[c]) for a in range(n) for j, chip in enumerate(chips)]
        for cp in first:
            cp.start()
        passed = []
        for a in range(n):
            for j in range(3):
                copy(a, j, shards[j], c, me).wait_recv()
                passed.append(copy(a, 3 + j, shards[j], c, sibling))
                passed[-1].start()
        for a in range(n):
            for j in range(3):
                copy(a, 3 + j, shards[j], 1 - c, me).wait_recv()
        for cp in first + passed:
            cp.wait_send()

    return pl.pallas_call(
        body, name="allgather_weights", in_specs=[_HBM] * n, out_specs=[_HBM] * n,
        out_shape=[jax.ShapeDtypeStruct((N_CHIP,) + p.shape, p.dtype) for p in packs],
        scratch_shapes=[pltpu.SemaphoreType.DMA((6 * n,)), pltpu.SemaphoreType.DMA((6 * n,))],
    )(*packs)


class _Rider:
    def __init__(self, inputs, out_shapes, n_sems, sends, recvs, aliases=None):
        self.inputs, self.out_shapes, self.n_sems = list(inputs), list(out_shapes), n_sems
        self.sends, self.recvs, self.aliases = sends, recvs, aliases or {}

    def start(self, *refs):
        for cp in self.sends(*refs):
            cp.start()

    def wait(self, *refs):
        for cp in self.recvs(*refs):
            cp.wait_recv()
        for cp in self.sends(*refs):
            cp.wait_send()


def _remote(src, dst, send_sems, recv_sems, k, to):
    return pltpu.make_async_remote_copy(src_ref=src, dst_ref=dst, send_sem=send_sems.at[k], recv_sem=recv_sems.at[k],
                                        device_id=to, device_id_type=MESH)


def _run_alone(rider, name):
    ri = len(rider.inputs)

    def body(*refs):
        ins, outs, (send_sems, recv_sems) = refs[:ri], refs[ri:-2], refs[-2:]
        rider.start(ins, outs, send_sems, recv_sems)
        rider.wait(ins, outs, send_sems, recv_sems)

    return pl.pallas_call(
        body, name=name, in_specs=[_HBM] * ri, out_specs=[_HBM] * len(rider.out_shapes), out_shape=rider.out_shapes,
        scratch_shapes=[pltpu.SemaphoreType.DMA((rider.n_sems,))] * 2, input_output_aliases=rider.aliases,
    )(*rider.inputs)


def _hosted_call(body, rider, *, name, grid, in_specs, out_specs, out_shape, scratch_shapes, semantics):
    if rider is None:
        return pl.pallas_call(body, name=name, grid=grid, in_specs=in_specs, out_specs=out_specs, out_shape=out_shape,
                              scratch_shapes=scratch_shapes, compiler_params=_params(*semantics))
    n_in, n_out, n_scr = len(in_specs), len(out_specs), len(scratch_shapes)
    ri, ro = len(rider.inputs), len(rider.out_shapes)

    def hosted(*refs):
        parts, p = [], 0
        for cnt in (n_in, ri, n_out, ro, n_scr, 2):
            parts.append(refs[p:p + cnt])
            p += cnt
        ins, rins, outs, routs, scr, (send_sems, recv_sems) = parts
        first = functools.reduce(jnp.logical_and, [pl.program_id(a) == 0 for a in range(len(grid))])
        last = functools.reduce(jnp.logical_and, [pl.program_id(a) == grid[a] - 1 for a in range(len(grid))])

        @pl.when(first)
        def _():
            rider.start(rins, routs, send_sems, recv_sems)

        body(*ins, *outs, *scr)

        @pl.when(last)
        def _():
            rider.wait(rins, routs, send_sems, recv_sems)

    call = pl.pallas_call(
        hosted, name=name, grid=grid, in_specs=list(in_specs) + [_HBM] * ri, out_specs=list(out_specs) + [_HBM] * ro,
        out_shape=list(out_shape) + rider.out_shapes,
        scratch_shapes=list(scratch_shapes) + [pltpu.SemaphoreType.DMA((rider.n_sems,))] * 2,
        input_output_aliases={n_in + i: n_out + o for i, o in rider.aliases.items()},
        compiler_params=_params(*(("arbitrary",) * len(grid))))

    def run(*args):
        res = call(*args, *rider.inputs)
        return res[:n_out], res[n_out:]

    return run


def _ride_gather_ici(packs):
    n = len(packs)

    def sends(ins, outs, send_sems, recv_sems):
        x, y, c, chips = _place()
        return [_remote(ins[a].at[c], outs[a].at[2 * x + y, c], send_sems, recv_sems, 3 * a + j, (*chip, c))
                for a in range(n) for j, chip in enumerate(chips)]

    def recvs(ins, outs, send_sems, recv_sems):
        x, y, c, chips = _place()
        return [_remote(ins[a].at[c], outs[a].at[2 * chip[0] + chip[1], c], send_sems, recv_sems, 3 * a + j, (x, y, c))
                for a in range(n) for j, chip in enumerate(chips)]

    return _Rider(packs, [jax.ShapeDtypeStruct((N_CHIP,) + p.shape, p.dtype) for p in packs], 3 * n, sends, recvs)


def _ride_gather_d2d(gathered):
    n = len(gathered)

    def copies(landing_half, to):
        def build(ins, outs, send_sems, recv_sems):
            x, y, c, chips = _place()
            return [_remote(ins[a].at[2 * chip[0] + chip[1], c], outs[a].at[2 * chip[0] + chip[1], landing_half(c)],
                            send_sems, recv_sems, 3 * a + j, to(x, y, c))
                    for a in range(n) for j, chip in enumerate(chips)]
        return build

    return _Rider(gathered, [jax.ShapeDtypeStruct(g.shape, g.dtype) for g in gathered], 3 * n,
                  copies(lambda c: c, lambda x, y, c: (x, y, 1 - c)), copies(lambda c: 1 - c, lambda x, y, c: (x, y, c)),
                  aliases={a: a for a in range(n)})


def _ride_exchange(gs):
    n = len(gs)

    def copies(ins, outs, send_sems, recv_sems):
        x, y, c, _ = _place()
        return [_remote(ins[a].at[1 - c], outs[a], send_sems, recv_sems, a, (x, y, 1 - c)) for a in range(n)]

    return _Rider(gs, [jax.ShapeDtypeStruct(g.shape[1:], g.dtype) for g in gs], n, copies, copies)


def _ride_scatter(b16s):
    n = len(b16s)

    def sends(ins, outs, send_sems, recv_sems):
        x, y, c, chips = _place()
        return [_remote(ins[a].at[2 * chip[0] + chip[1]], outs[a].at[2 * x + y], send_sems, recv_sems, 3 * a + j, (*chip, c))
                for a in range(n) for j, chip in enumerate(chips)]

    def recvs(ins, outs, send_sems, recv_sems):
        x, y, c, chips = _place()
        return [_remote(ins[a].at[2 * x + y], outs[a].at[2 * chip[0] + chip[1]], send_sems, recv_sems, 3 * a + j, (x, y, c))
                for a in range(n) for j, chip in enumerate(chips)]

    return _Rider(b16s, [jax.ShapeDtypeStruct(b.shape, b.dtype) for b in b16s], 3 * n, sends, recvs)


def add_halves(g, got, idx, name):
    _, ns, r, cols = g.shape
    tr = _tile(r, 256, 16)

    def body(idx_ref, a_ref, b_ref, o32_ref, o16_ref):
        s = a_ref[...] + b_ref[...]
        o32_ref[...] = s
        o16_ref[...] = s.astype(BF16)

    blk = pl.BlockSpec((None, tr, cols), lambda s, i, idx_ref: (s, i, 0))
    return pl.pallas_call(
        body, name=name,
        grid_spec=pltpu.PrefetchScalarGridSpec(
            num_scalar_prefetch=1, grid=(ns, r // tr),
            in_specs=[pl.BlockSpec((None, None, tr, cols), lambda s, i, idx_ref: (idx_ref[0], s, i, 0)), blk],
            out_specs=[blk, blk]),
        out_shape=[jax.ShapeDtypeStruct((ns, r, cols), F32), jax.ShapeDtypeStruct((ns, r, cols), BF16)],
        compiler_params=_params("parallel", "parallel"),
    )(idx, g, got)


def add_chips(a32, got16, idx, name):
    ns, r, cols = a32.shape
    tr = _tile(r, 256, 16)

    def body(idx_ref, a_ref, r1_ref, r2_ref, r3_ref, o_ref):
        o_ref[...] = ((a_ref[...] + r1_ref[...].astype(F32)) + r2_ref[...].astype(F32)) + r3_ref[...].astype(F32)

    def slab(k):
        return pl.BlockSpec((None, tr, cols), lambda i, idx_ref: ((idx_ref[1] + k) % ns, i, 0))

    return pl.pallas_call(
        body, name=name,
        grid_spec=pltpu.PrefetchScalarGridSpec(
            num_scalar_prefetch=1, grid=(r // tr,), in_specs=[slab(0), slab(1), slab(2), slab(3)],
            out_specs=pl.BlockSpec((tr, cols), lambda i, idx_ref: (i, 0))),
        out_shape=jax.ShapeDtypeStruct((r, cols), F32),
        compiler_params=_params("parallel"),
    )(idx, a32, got16, got16, got16)


def share_halves(halves):
    n = len(halves)

    def body(*refs):
        in_refs, out_refs, (send_sems, recv_sems) = refs[:n], refs[n:2 * n], refs[2 * n:]
        x, y, c, _ = _place()
        cps = [pltpu.make_async_remote_copy(src_ref=in_refs[a], dst_ref=out_refs[a], send_sem=send_sems.at[a],
                                            recv_sem=recv_sems.at[a], device_id=(x, y, 1 - c), device_id_type=MESH)
               for a in range(n)]
        for cp in cps:
            cp.start()
        for cp in cps:
            cp.wait()

    return pl.pallas_call(
        body, name="share_halves", in_specs=[_HBM] * n, out_specs=[_HBM] * n,
        out_shape=[jax.ShapeDtypeStruct(h.shape, F32) for h in halves],
        scratch_shapes=[pltpu.SemaphoreType.DMA((n,)), pltpu.SemaphoreType.DMA((n,))],
    )(*halves)


def allreduce_small(v):
    R, _ = v.shape

    def body(in_ref, out_ref, slots, send_sems, recv_sems):
        x, y, c, _ = _place()
        me = 4 * x + 2 * y + c
        slots[me] = in_ref[...]
        cps = []
        for k in range(1, N_DEV):
            to = (x ^ (k >> 2), y ^ ((k >> 1) & 1), c ^ (k & 1))
            cps.append(pltpu.make_async_remote_copy(src_ref=in_ref, dst_ref=slots.at[me], send_sem=send_sems.at[k - 1],
                                                    recv_sem=recv_sems.at[k - 1], device_id=to, device_id_type=MESH))
        for cp in cps:
            cp.start()
        for k in range(1, N_DEV):
            frm = 4 * (x ^ (k >> 2)) + 2 * (y ^ ((k >> 1) & 1)) + (c ^ (k & 1))
            pltpu.make_async_remote_copy(src_ref=in_ref, dst_ref=slots.at[frm], send_sem=send_sems.at[k - 1],
                                         recv_sem=recv_sems.at[k - 1], device_id=(x, y, c), device_id_type=MESH).wait_recv()
        for cp in cps:
            cp.wait_send()
        acc = slots[0]
        for d in range(1, N_DEV):
            acc = acc + slots[d]
        out_ref[...] = acc

    vm = pl.BlockSpec(memory_space=pltpu.VMEM)
    return pl.pallas_call(
        body, name="allreduce_small", in_specs=[vm], out_specs=vm, out_shape=jax.ShapeDtypeStruct((R, ROW), F32),
        scratch_shapes=[pltpu.VMEM((N_DEV, R, ROW), F32), pltpu.SemaphoreType.DMA((N_DEV - 1,)),
                        pltpu.SemaphoreType.DMA((N_DEV - 1,))],
    )(v)


def _rows_of(n, unit=16):
    return -(-n // (unit * ROW)) * unit


def _pack_rows(items, total_rows, dtype, unit=16):
    parts = []
    used = 0
    for a in items:
        flat = a.reshape(-1)
        r = _rows_of(flat.shape[0], unit)
        flat = jnp.pad(flat, (0, r * ROW - flat.shape[0]))
        parts.append(flat.reshape(r, ROW))
        used += r
    if total_rows > used:
        parts.append(jnp.zeros((total_rows - used, ROW), dtype))
    return jnp.concatenate(parts, axis=0)


def _unpack_rows(buf, shapes, unit=16):
    lead = buf.shape[:-2]
    out = []
    off = 0
    for shp in shapes:
        n = math.prod(shp)
        r = _rows_of(n, unit)
        piece = buf[..., off:off + r, :].reshape(*lead, r * ROW)[..., :n].reshape(*lead, *shp)
        out.append(piece)
        off += r
    return out


def _interleave_heads(w, H):
    lead = w.shape[:-1]
    return w.reshape(*lead, 3, H, HEAD).swapaxes(-3, -2).reshape(*lead, 3 * H * HEAD)


def _deinterleave_heads(w, H):
    lead = w.shape[:-1]
    return w.reshape(*lead, H, 3, HEAD).swapaxes(-3, -2).reshape(*lead, 3 * H * HEAD)


def kernel(x, norm_mix, w_in, fox_f_bias, gdn_conv_w, gdn_a_log, gdn_dt_bias, gdn_norm, w_branch_fox, w_branch_gdn, w_out, norm_ffn, w_up, ffn_conv_w, w_down, norm_final, loss_target, m_norm_mix, m_w_in, m_fox_f_bias, m_gdn_conv_w, m_gdn_a_log, m_gdn_dt_bias, m_gdn_norm, m_w_branch_fox, m_w_branch_gdn, m_w_out, m_norm_ffn, m_w_up, m_ffn_conv_w, m_w_down, m_norm_final, v_norm_mix, v_w_in, v_fox_f_bias, v_gdn_conv_w, v_gdn_a_log, v_gdn_dt_bias, v_gdn_norm, v_w_branch_fox, v_w_branch_gdn, v_w_out, v_norm_ffn, v_w_up, v_ffn_conv_w, v_w_down, v_norm_final):
    B, S, D = x.shape
    T = B * S
    H = D // HEAD
    N = S // CHUNK
    FF = w_down.shape[1] * N_CHIP
    d_in = 9 * D + 3 * H
    assert w_in.shape[2] * N_CHIP == d_in and 3 * H <= 128

    cidx = lax.axis_index("c").astype(jnp.int32)
    sidx = (2 * lax.axis_index("x") + lax.axis_index("y")).astype(jnp.int32)
    idx = jnp.stack([cidx, sidx])

    rowed = [w_branch_fox[0], w_branch_gdn[0], w_out[0], w_down[0]]
    convs = [gdn_conv_w[0], ffn_conv_w[0]]
    rowed_shapes = [a.shape for a in rowed]
    conv_shapes = [a.shape + (2,) for a in convs]
    pad_rows = lambda shapes: -(-sum(_rows_of(math.prod(s)) for s in shapes) // 256) * 128
    Rh, Rc = pad_rows(rowed_shapes), pad_rows(conv_shapes)
    halves = lambda a: a.reshape(2, a.shape[0] // 2, a.shape[1])
    packs_a = [halves(w_in[0].astype(BF16)),
               halves(_pack_rows([lax.bitcast_convert_type(a, BF16) for a in convs], 2 * Rc, BF16))]
    packs_b = [halves(w_up[0].astype(BF16)), halves(_pack_rows([a.astype(BF16) for a in rowed], 2 * Rh, BF16))]
    own = lambda gs, ps: [lax.dynamic_update_slice(g, p[None], (sidx, 0, 0, 0)) for g, p in zip(gs, ps)]
    by_cols = lambda g: g.transpose(1, 2, 0, 3).reshape(2 * g.shape[2], N_CHIP * g.shape[3])
    cat_cols = lambda p: jnp.concatenate([p[i] for i in range(N_CHIP)], axis=-1)
    cat_rows = lambda p: p.reshape(-1, p.shape[-1])
    g_in, g_conv = own(allgather_weights(packs_a), packs_a)
    W_in = by_cols(g_in)
    conv_parts = _unpack_rows(g_conv.reshape(N_CHIP, 2 * Rc, ROW), conv_shapes)
    gconv = cat_cols(lax.bitcast_convert_type(conv_parts[0], F32))
    fconv = cat_cols(lax.bitcast_convert_type(conv_parts[1], F32))

    o1, o2 = 3 * D, 3 * D + H
    o3, o4, o5, o6 = o2 + 3 * D, o2 + 3 * D + H, o2 + 3 * D + 2 * H, o2 + 4 * D + 2 * H
    W_fox = _interleave_heads(W_in[:, :o1], H)
    W_gqkv = _interleave_heads(W_in[:, o2:o3], H)
    W_gz = W_in[:, o5:o6]
    W_gates = W_in[:, o6:]
    W_small = jnp.concatenate([W_in[:, o1:o2], W_in[:, o3:o5], jnp.zeros((D, 128 - 3 * H), BF16)], axis=1)
    gconv_i = _interleave_heads(gconv, H)
    fconv_g, fconv_v = fconv[:, :FF], fconv[:, FF:]
    prm = jnp.zeros((8, 128), F32)
    prm = prm.at[0, 0:H].set(fox_f_bias[0]).at[0, H:2 * H].set(gdn_dt_bias[0]).at[1, H:2 * H].set(gdn_a_log[0])

    x2 = x.reshape(T, D)
    tgt = loss_target.reshape(T, D)

    hn1 = rmsnorm_fwd(x2, norm_mix, "rmsnorm_mix")
    p_fox = matmul(hn1, W_fox, "nn", "proj_fox", out_dtype=BF16)
    p_gqkv = matmul(hn1, W_gqkv, "nn", "proj_gqkv")
    p_gz = matmul(hn1, W_gz, "nn", "proj_gz")
    p_gates = matmul(hn1, W_gates, "nn", "proj_gates")
    p_small = matmul(hn1, W_small, "nn", "proj_small")

    sm = small_fwd(p_small, prm, B, S, H)
    heads = lambda a: a.reshape(B, S, H).transpose(0, 2, 1)
    c_bhs, gc_bhs, beta_bhs = heads(sm[:, 0:H]), heads(sm[:, H:2 * H]), heads(sm[:, 2 * H:3 * H])
    c_col, c_row = c_bhs[..., None], c_bhs[:, :, None, :]
    gcr5 = gc_bhs.reshape(B, H, N, 1, CHUNK)
    betar5 = beta_bhs.reshape(B, H, N, 1, CHUNK)

    (o_fox, o_fox16, lse), arriving = fox_fwd(p_fox, c_col, c_row, B, S, H, rider=_ride_gather_ici(packs_b))
    qkvn = gdn_prep_fwd(p_gqkv, gconv_i, B, S, H)
    (u_hat, w_t, t_inv), arrived = gdn_intra_fwd(qkvn, betar5, gcr5, B, S, H, rider=_ride_gather_d2d(arriving))
    g_up, g_rowed = own(arrived, packs_b)
    W_up = by_cols(g_up)
    W_up_g, W_up_v = W_up[:, :FF], W_up[:, FF:]
    W_bf, W_bg, W_out, W_down = (cat_rows(p) for p in _unpack_rows(g_rowed.reshape(N_CHIP, 2 * Rh, ROW), rowed_shapes))
    o_gdn, states = gdn_inter_fwd(qkvn, u_hat, w_t, gcr5, B, S, H)
    y_gdn = gdn_post_fwd(o_gdn, p_gz, gdn_norm, H)
    bf_ = matmul(o_fox16, W_bf, "nn", "branch_fox")
    bg_ = matmul(y_gdn, W_bg, "nn", "branch_gdn")
    y = merge_fwd(p_gates, bf_, bg_)
    h1 = matmul(y, W_out, "nn", "out_proj", add=x2)
    hn2 = rmsnorm_fwd(h1, norm_ffn, "rmsnorm_ffn")
    up_g = matmul(hn2, W_up_g, "nn", "up_gate")
    up_v = matmul(hn2, W_up_v, "nn", "up_val")
    act = ffn_gate_fwd(up_g, up_v, fconv_g, fconv_v, B, S)
    h2 = matmul(act, W_down, "nn", "down_proj", add=h1)
    loss_cols, dh2, dh2_16, d_norm_final = final_loss(h2, norm_final.reshape(1, D), tgt)
    loss = lax.psum(0.5 * jnp.sum(loss_cols) / D, ("x", "y", "c"))

    d_act = matmul(dh2_16, W_down, "nt", "d_act")
    dW_down = matmul(act, dh2_16, "tn", "dw_down")
    d_upg, d_upv, d_fconv_g, d_fconv_v = ffn_gate_bwd(up_g, up_v, fconv_g, fconv_v, d_act, B, S)
    d_hn2 = matmul(d_upg, W_up_g, "nt", "d_hn2_g")
    d_hn2 = matmul(d_upv, W_up_v, "nt", "d_hn2_v", add=d_hn2)
    dW_up = jnp.concatenate([matmul(hn2, d_upg, "tn", "dw_up_g"), matmul(hn2, d_upv, "tn", "dw_up_v")], axis=1)
    dh1, dh1_16, d_norm_ffn = rmsnorm_bwd(h1, norm_ffn, d_hn2, dh2, "rmsnorm_ffn_bwd")
    d_y = matmul(dh1_16, W_out, "nt", "d_y")
    dW_out = matmul(y, dh1_16, "tn", "dw_out")
    d_bf, d_bg, d_gates = merge_bwd(p_gates, bf_, bg_, d_y)
    d_ofox = matmul(d_bf, W_bf, "nt", "d_ofox")
    dW_bf = matmul(o_fox16, d_bf, "tn", "dw_bf")
    d_ygdn = matmul(d_bg, W_bg, "nt", "d_ygdn")
    dW_bg = matmul(y_gdn, d_bg, "tn", "dw_bg")

    d_fconv = jnp.concatenate([d_fconv_g, d_fconv_v], axis=1)
    col_shard = lambda g, s: g[:, s * (g.shape[1] // N_CHIP):(s + 1) * (g.shape[1] // N_CHIP)]
    row_shard = lambda g, s: g[s * (g.shape[0] // N_CHIP):(s + 1) * (g.shape[0] // N_CHIP)]
    shard_items = lambda s: [row_shard(dW_bf, s), row_shard(dW_bg, s), row_shard(dW_out, s), row_shard(dW_down, s),
                             col_shard(d_fconv, s)]
    g_shapes = [a.shape for a in shard_items(0)]
    assert sum(_rows_of(math.prod(s)) for s in g_shapes) <= 2 * Rh
    to_slabs = lambda g: g.reshape(2, g.shape[0] // 2, N_CHIP, g.shape[1] // N_CHIP).transpose(0, 2, 1, 3)
    gpacks_b = [to_slabs(dW_up),
                jnp.stack([_pack_rows(shard_items(s), 2 * Rh, F32).reshape(2, Rh, ROW) for s in range(N_CHIP)], axis=1)]
    (d_pfox, d_ccol, d_crow), gots_b = fox_bwd(p_fox, c_col, c_row, o_fox, lse, d_ofox, B, S, H,
                                              rider=_ride_exchange(gpacks_b))
    sums_b = [add_halves(g, got, idx, "add_halves_b%d" % i) for i, (g, got) in enumerate(zip(gpacks_b, gots_b))]

    d_ogdn, d_gz, d_gdn_norm = gdn_post_bwd(o_gdn, p_gz, gdn_norm, d_ygdn, H)
    (dq_i, dk_i, d_uh, d_wt, dgcr_a), got16_b = gdn_inter_bwd(qkvn, u_hat, w_t, gcr5, states, d_ogdn, B, S, H,
                                                             rider=_ride_scatter([s16 for _, s16 in sums_b]))
    mine_b = [add_chips(s32, g16, idx, "add_chips_b%d" % i) for i, ((s32, _), g16) in enumerate(zip(sums_b, got16_b))]
    d_qkvn, d_betar5, dgcr_b = gdn_intra_bwd(qkvn, betar5, gcr5, t_inv, d_uh, d_wt, dq_i, dk_i, B, S, H)
    d_pgqkv, d_gconv_i = gdn_prep_bwd(p_gqkv, gconv_i, d_qkvn, B, S, H)

    tokens = lambda a: a.reshape(B, H, S).transpose(0, 2, 1).reshape(T, H)
    d_gc = (dgcr_a + dgcr_b).reshape(B, H, S)
    d_sm = jnp.concatenate([tokens(d_ccol.reshape(B, H, S) + d_crow.reshape(B, H, S)), tokens(d_gc), tokens(d_betar5.reshape(B, H, S)),
                            jnp.zeros((T, 128 - 3 * H), F32)], axis=1)
    d_psmall, d_prm = small_bwd(p_small, prm, d_sm, B, S, H)

    d_hn1 = matmul(d_pfox, W_fox, "nt", "d_hn1_fox")
    d_hn1 = matmul(d_pgqkv, W_gqkv, "nt", "d_hn1_gqkv", add=d_hn1)
    d_hn1 = matmul(d_gz, W_gz, "nt", "d_hn1_gz", add=d_hn1)
    d_hn1 = matmul(d_gates, W_gates, "nt", "d_hn1_gates", add=d_hn1)
    d_hn1 = matmul(d_psmall, W_small, "nt", "d_hn1_small", add=d_hn1)
    dW_fox = matmul(hn1, d_pfox, "tn", "dw_fox")
    dW_gqkv = matmul(hn1, d_pgqkv, "tn", "dw_gqkv")
    dW_gz = matmul(hn1, d_gz, "tn", "dw_gz")
    dW_gates = matmul(hn1, d_gates, "tn", "dw_gates")
    dW_small = matmul(hn1, d_psmall, "tn", "dw_small")
    grad_x, _, d_norm_mix = rmsnorm_bwd(x2, norm_mix, d_hn1, dh1, "rmsnorm_mix_bwd")

    dW_in = jnp.concatenate([_deinterleave_heads(dW_fox, H), dW_small[:, 0:H], _deinterleave_heads(dW_gqkv, H),
                             dW_small[:, H:3 * H], dW_gz, dW_gates], axis=1)
    d_gconv = _deinterleave_heads(d_gconv_i, H)

    gpack_a = [to_slabs(dW_in)]
    gots_a = _run_alone(_ride_exchange(gpack_a), "exchange_halves")
    sums_a = [add_halves(gpack_a[0], gots_a[0], idx, "add_halves_a")]
    got16_a = _run_alone(_ride_scatter([sums_a[0][1]]), "scatter_chips")
    mine = [add_chips(sums_a[0][0], got16_a[0], idx, "add_chips_a")] + mine_b
    others = share_halves(mine)
    g_w_in, g_up, g_rows = (jnp.concatenate([jnp.where(cidx == 0, h, o), jnp.where(cidx == 0, o, h)], axis=0)
                            for h, o in zip(mine, others))
    g_bf, g_bg, g_out, g_down, g_fconv = _unpack_rows(g_rows, g_shapes)

    small_items = [d_norm_mix, d_norm_ffn, d_norm_final, d_gdn_norm, d_prm, d_gconv]
    small_shapes = [a.shape for a in small_items]
    sv = allreduce_small(_pack_rows(small_items, 0, F32, unit=8))
    g_norm_mix, g_norm_ffn, g_norm_final, g_gdn_norm, g_prm, g_gconv_all = _unpack_rows(sv, small_shapes, unit=8)
    g_norm_final = g_norm_final.reshape(D)
    g_fbias, g_dtb, g_alog = g_prm[0:1, 0:H], g_prm[0:1, H:2 * H], g_prm[1:2, H:2 * H]
    g_gconv = lax.dynamic_slice_in_dim(g_gconv_all, sidx * (3 * D // N_CHIP), 3 * D // N_CHIP, axis=1)

    names = ["norm_mix", "w_in", "fox_f_bias", "gdn_conv_w", "gdn_a_log", "gdn_dt_bias", "gdn_norm", "w_branch_fox",
             "w_branch_gdn", "w_out", "norm_ffn", "w_up", "ffn_conv_w", "w_down", "norm_final"]
    ws = [norm_mix, w_in, fox_f_bias, gdn_conv_w, gdn_a_log, gdn_dt_bias, gdn_norm, w_branch_fox, w_branch_gdn, w_out,
          norm_ffn, w_up, ffn_conv_w, w_down, norm_final]
    ms = [m_norm_mix, m_w_in, m_fox_f_bias, m_gdn_conv_w, m_gdn_a_log, m_gdn_dt_bias, m_gdn_norm, m_w_branch_fox,
          m_w_branch_gdn, m_w_out, m_norm_ffn, m_w_up, m_ffn_conv_w, m_w_down, m_norm_final]
    vs = [v_norm_mix, v_w_in, v_fox_f_bias, v_gdn_conv_w, v_gdn_a_log, v_gdn_dt_bias, v_gdn_norm, v_w_branch_fox,
          v_w_branch_gdn, v_w_out, v_norm_ffn, v_w_up, v_ffn_conv_w, v_w_down, v_norm_final]
    gs = [g_norm_mix, g_w_in, g_fbias, g_gconv, g_alog, g_dtb, g_gdn_norm, g_bf, g_bg, g_out, g_norm_ffn, g_up,
          g_fconv, g_down, g_norm_final]
    gs = [g.reshape(w.shape) for g, w in zip(gs, ws)]
    deltas, new_ms, new_vs = [], [], []
    for nm, w, g, m, v in zip(names, ws, gs, ms, vs):
        if w.ndim == 1:
            d, a, b = adamw(w.reshape(1, -1), g.reshape(1, -1), m.reshape(1, -1), v.reshape(1, -1), "adamw_" + nm)
            d, a, b = d.reshape(w.shape), a.reshape(w.shape), b.reshape(w.shape)
        else:
            d, a, b = adamw(w, g, m, v, "adamw_" + nm)
        deltas.append(d)
        new_ms.append(a)
        new_vs.append(b)

    return (loss, grad_x.reshape(B, S, D), *gs, *deltas, *new_ms, *new_vs)
```

```python
import functools
import math

import jax
import jax.numpy as jnp
from jax import lax
from jax.experimental import pallas as pl
from jax.experimental.pallas import tpu as pltpu

F32 = jnp.float32
BF16 = jnp.bfloat16
HEAD = 128
CHUNK = 64
GDN_CONV = 4
FFN_CONV = 3
EPS = 1e-6
NEG = -1e30
ROW = 1024
ATT_TILE = 512
MM_WEIGHT_TILE_BYTES = 8 << 20
N_CHIP = 4
N_DEV = 8
MESH = pl.DeviceIdType.MESH
HI = lax.Precision.HIGH
EXACT = lax.Precision.HIGHEST

ADAM_LR, ADAM_B1, ADAM_B2, ADAM_EPS, ADAM_WD, ADAM_STEP = 0.001, 0.9, 0.999, 1e-08, 0.01, 10


def _tile(n, cap, unit=128):
    best = None
    t = unit
    while t <= min(n, cap):
        if n % t == 0:
            best = t
        t += unit
    return best if best is not None else n


def _params(*sem):
    return pltpu.CompilerParams(dimension_semantics=sem)


_NN = (((1,), (0,)), ((), ()))
_NT = (((1,), (1,)), ((), ()))
_TN = (((0,), (0,)), ((), ()))


def _dg(a, b, dims, hi):
    if hi:
        return lax.dot_general(a, b, dims, precision=HI, preferred_element_type=F32)
    return lax.dot_general(a.astype(BF16), b.astype(BF16), dims, preferred_element_type=F32)


class _RawOps:
    @staticmethod
    def nn(a, b, hi=False):
        return _dg(a, b, _NN, hi)

    @staticmethod
    def nt(a, b, hi=False):
        return _dg(a, b, _NT, hi)

    @staticmethod
    def tn(a, b, hi=False):
        return _dg(a, b, _TN, hi)


def _make_diff_ops():
    def build(hi):
        @jax.custom_vjp
        def nn(a, b):
            return _dg(a, b, _NN, hi)

        nn.defvjp(lambda a, b: (_dg(a, b, _NN, hi), (a, b)),
                  lambda r, g: (_dg(g, r[1], _NT, hi), _dg(r[0], g, _TN, hi)))

        @jax.custom_vjp
        def nt(a, b):
            return _dg(a, b, _NT, hi)

        nt.defvjp(lambda a, b: (_dg(a, b, _NT, hi), (a, b)),
                  lambda r, g: (_dg(g, r[1], _NN, hi), _dg(g, r[0], _TN, hi)))

        @jax.custom_vjp
        def tn(a, b):
            return _dg(a, b, _TN, hi)

        tn.defvjp(lambda a, b: (_dg(a, b, _TN, hi), (a, b)),
                  lambda r, g: (_dg(r[1], g, _NT, hi), _dg(r[0], g, _NN, hi)))
        return nn, nt, tn

    lo, hi_ = build(False), build(True)

    class _DiffOps:
        @staticmethod
        def nn(a, b, hi=False):
            return (hi_ if hi else lo)[0](a, b)

        @staticmethod
        def nt(a, b, hi=False):
            return (hi_ if hi else lo)[1](a, b)

        @staticmethod
        def tn(a, b, hi=False):
            return (hi_ if hi else lo)[2](a, b)

    return _DiffOps


_DiffOps = _make_diff_ops()


def _sigmoid(x):
    return 1.0 / (1.0 + jnp.exp(-x))


def _mm_tile(n, pref):
    if n % pref == 0:
        return pref
    if n % 1408 == 0:
        return 1408
    return _tile(n, pref)


def matmul(a, b, mode, name, add=None, out_dtype=F32):
    if mode == "nn":
        (M, K), (K2, N) = a.shape, b.shape
    elif mode == "nt":
        (M, K), (N, K2) = a.shape, b.shape
    else:
        (K, M), (K2, N) = a.shape, b.shape
    assert K == K2, (name, a.shape, b.shape)
    tn = _mm_tile(N, 1024)
    if mode == "tn":
        tm = M if M <= 1408 else _mm_tile(M, 1408)
        tk = _mm_tile(K, 1024)
    else:
        tk = K if K * tn * 2 <= MM_WEIGHT_TILE_BYTES else _mm_tile(K, 1024)
        tm = _mm_tile(M, 1024 if tk <= 2048 else 512)
    nk = K // tk
    dims = {"nn": _NN, "nt": _NT, "tn": _TN}[mode]
    if mode == "tn":
        a_spec = pl.BlockSpec((tk, tm), lambda j, i, k: (k, i))
    else:
        a_spec = pl.BlockSpec((tm, tk), lambda j, i, k: (i, k))
    if mode == "nt":
        b_spec = pl.BlockSpec((tn, tk), lambda j, i, k: (j, k))
    else:
        b_spec = pl.BlockSpec((tk, tn), lambda j, i, k: (k, j))
    o_spec = pl.BlockSpec((tm, tn), lambda j, i, k: (i, j))
    has_add = add is not None

    def body(*refs):
        a_ref, b_ref = refs[:2]
        add_ref = refs[2] if has_add else None
        o_ref = refs[3] if has_add else refs[2]
        prod = lax.dot_general(a_ref[...].astype(BF16), b_ref[...].astype(BF16), dims, preferred_element_type=F32)

        def finish(r):
            if has_add:
                r = r + add_ref[...]
            o_ref[...] = r.astype(out_dtype)

        if nk == 1:
            finish(prod)
            return
        acc_ref = refs[-1]
        k = pl.program_id(2)

        @pl.when(k == 0)
        def _():
            acc_ref[...] = prod

        @pl.when(k > 0)
        def _():
            acc_ref[...] += prod

        @pl.when(k == nk - 1)
        def _():
            finish(acc_ref[...])

    in_specs = [a_spec, b_spec] + ([o_spec] if has_add else [])
    args = (a, b) + ((add,) if has_add else ())
    return pl.pallas_call(
        body, name=name, grid=(N // tn, M // tm, nk), in_specs=in_specs, out_specs=o_spec,
        out_shape=jax.ShapeDtypeStruct((M, N), out_dtype),
        scratch_shapes=[pltpu.VMEM((tm, tn), F32)] if nk > 1 else [],
        compiler_params=_params("parallel", "parallel", "arbitrary"),
    )(*args)


def rmsnorm_fwd(x, g, name):
    T, D = x.shape
    tm = _tile(T, 512, 8)

    def body(x_ref, g_ref, o_ref):
        xv = x_ref[...]
        r = lax.rsqrt(jnp.mean(xv * xv, axis=-1, keepdims=True) + EPS)
        o_ref[...] = (xv * r * g_ref[...]).astype(BF16)

    return pl.pallas_call(
        body, name=name, grid=(T // tm,),
        in_specs=[pl.BlockSpec((tm, D), lambda i: (i, 0)), pl.BlockSpec((1, D), lambda i: (0, 0))],
        out_specs=pl.BlockSpec((tm, D), lambda i: (i, 0)),
        out_shape=jax.ShapeDtypeStruct((T, D), BF16),
        compiler_params=_params("parallel"),
    )(x, g)


def rmsnorm_bwd(x, g, dy, dres, name):
    T, D = x.shape
    tm = _tile(T, 512, 8)

    def body(x_ref, g_ref, dy_ref, dres_ref, dx_ref, dx16_ref, dg_ref):
        @pl.when(pl.program_id(0) == 0)
        def _():
            dg_ref[...] = jnp.zeros_like(dg_ref)

        xv = x_ref[...]
        r = lax.rsqrt(jnp.mean(xv * xv, axis=-1, keepdims=True) + EPS)
        xh = xv * r
        dyv = dy_ref[...]
        dg_ref[...] += jnp.sum(dyv * xh, axis=0, keepdims=True)
        dxh = dyv * g_ref[...]
        dx = dres_ref[...] + r * (dxh - xh * jnp.mean(dxh * xh, axis=-1, keepdims=True))
        dx_ref[...] = dx
        dx16_ref[...] = dx.astype(BF16)

    row = pl.BlockSpec((tm, D), lambda i: (i, 0))
    vec = pl.BlockSpec((1, D), lambda i: (0, 0))
    return pl.pallas_call(
        body, name=name, grid=(T // tm,), in_specs=[row, vec, row, row], out_specs=[row, row, vec],
        out_shape=[jax.ShapeDtypeStruct((T, D), F32), jax.ShapeDtypeStruct((T, D), BF16),
                   jax.ShapeDtypeStruct((1, D), F32)],
        compiler_params=_params("arbitrary"),
    )(x, g, dy, dres)


def final_loss(h, g, target):
    T, D = h.shape
    tm = _tile(T, 512, 8)

    def body(h_ref, g_ref, t_ref, loss_ref, dh_ref, dh16_ref, dg_ref):
        @pl.when(pl.program_id(0) == 0)
        def _():
            loss_ref[...] = jnp.zeros_like(loss_ref)
            dg_ref[...] = jnp.zeros_like(dg_ref)

        hv = h_ref[...]
        r = lax.rsqrt(jnp.mean(hv * hv, axis=-1, keepdims=True) + EPS)
        xh = hv * r
        err = xh * g_ref[...] - t_ref[...]
        loss_ref[...] += jnp.sum(err * err, axis=0, keepdims=True)
        dy = err * (1.0 / D)
        dg_ref[...] += jnp.sum(dy * xh, axis=0, keepdims=True)
        dxh = dy * g_ref[...]
        dh = r * (dxh - xh * jnp.mean(dxh * xh, axis=-1, keepdims=True))
        dh_ref[...] = dh
        dh16_ref[...] = dh.astype(BF16)

    row = pl.BlockSpec((tm, D), lambda i: (i, 0))
    vec = pl.BlockSpec((1, D), lambda i: (0, 0))
    return pl.pallas_call(
        body, name="final_loss", grid=(T // tm,), in_specs=[row, vec, row], out_specs=[vec, row, row, vec],
        out_shape=[jax.ShapeDtypeStruct((1, D), F32), jax.ShapeDtypeStruct((T, D), F32),
                   jax.ShapeDtypeStruct((T, D), BF16), jax.ShapeDtypeStruct((1, D), F32)],
        compiler_params=_params("arbitrary"),
    )(h, g, target)


def _shift_down(x, k):
    if k == 0:
        return x
    rows = lax.broadcasted_iota(jnp.int32, x.shape, 0)
    return jnp.where(rows >= k, pltpu.roll(x, k, 0), 0.0)


def _shift_up(x, k):
    if k == 0:
        return x
    s = x.shape[0]
    rows = lax.broadcasted_iota(jnp.int32, x.shape, 0)
    return jnp.where(rows < s - k, pltpu.roll(x, s - k, 0), 0.0)


def _conv_fwd(x, w_ref, kw):
    y = x * w_ref[kw - 1:kw, :]
    for i in range(kw - 1):
        y = y + _shift_down(x, kw - 1 - i) * w_ref[i:i + 1, :]
    return y


def _conv_bwd(x, dy, w_ref, kw):
    dx = dy * w_ref[kw - 1:kw, :]
    dws = []
    for i in range(kw - 1):
        dx = dx + _shift_up(dy, kw - 1 - i) * w_ref[i:i + 1, :]
        dws.append(jnp.sum(dy * _shift_down(x, kw - 1 - i), axis=0, keepdims=True))
    dws.append(jnp.sum(dy * x, axis=0, keepdims=True))
    return dx, dws


def ffn_gate_fwd(up_g, up_v, cw_g, cw_v, B, S):
    T, Fd = up_g.shape
    tc = _tile(Fd, 256)

    def body(g_ref, v_ref, wg_ref, wv_ref, o_ref):
        ug = _conv_fwd(g_ref[...], wg_ref, FFN_CONV)
        uv = _conv_fwd(v_ref[...], wv_ref, FFN_CONV)
        o_ref[...] = (ug * _sigmoid(ug) * uv).astype(BF16)

    blk = pl.BlockSpec((S, tc), lambda b, j: (b, j))
    wblk = pl.BlockSpec((FFN_CONV, tc), lambda b, j: (0, j))
    return pl.pallas_call(
        body, name="ffn_gate_fwd", grid=(B, Fd // tc), in_specs=[blk, blk, wblk, wblk], out_specs=blk,
        out_shape=jax.ShapeDtypeStruct((T, Fd), BF16), compiler_params=_params("parallel", "parallel"),
    )(up_g, up_v, cw_g, cw_v)


def ffn_gate_bwd(up_g, up_v, cw_g, cw_v, d_act, B, S):
    T, Fd = up_g.shape
    tc = _tile(Fd, 256)

    def body(g_ref, v_ref, wg_ref, wv_ref, da_ref, dg_ref, dv_ref, dwg_ref, dwv_ref):
        @pl.when(pl.program_id(1) == 0)
        def _():
            dwg_ref[...] = jnp.zeros_like(dwg_ref)
            dwv_ref[...] = jnp.zeros_like(dwv_ref)

        xg, xv = g_ref[...], v_ref[...]
        ug = _conv_fwd(xg, wg_ref, FFN_CONV)
        uv = _conv_fwd(xv, wv_ref, FFN_CONV)
        da = da_ref[...]
        sg = _sigmoid(ug)
        d_ug = da * uv * (sg + ug * sg * (1.0 - sg))
        d_uv = da * ug * sg
        dxg, dwg = _conv_bwd(xg, d_ug, wg_ref, FFN_CONV)
        dxv, dwv = _conv_bwd(xv, d_uv, wv_ref, FFN_CONV)
        dg_ref[...] = dxg.astype(BF16)
        dv_ref[...] = dxv.astype(BF16)
        for i in range(FFN_CONV):
            dwg_ref[i:i + 1, :] += dwg[i]
            dwv_ref[i:i + 1, :] += dwv[i]

    blk = pl.BlockSpec((S, tc), lambda j, b: (b, j))
    wblk = pl.BlockSpec((FFN_CONV, tc), lambda j, b: (0, j))
    return pl.pallas_call(
        body, name="ffn_gate_bwd", grid=(Fd // tc, B), in_specs=[blk, blk, wblk, wblk, blk],
        out_specs=[blk, blk, wblk, wblk],
        out_shape=[jax.ShapeDtypeStruct((T, Fd), BF16), jax.ShapeDtypeStruct((T, Fd), BF16),
                   jax.ShapeDtypeStruct((FFN_CONV, Fd), F32), jax.ShapeDtypeStruct((FFN_CONV, Fd), F32)],
        compiler_params=_params("parallel", "arbitrary"),
    )(up_g, up_v, cw_g, cw_v, d_act)


def merge_fwd(p_gates, bf_, bg_):
    T, D = bf_.shape
    tm = _tile(T, 512, 8)

    def body(gf_ref, gg_ref, bf_ref, bg_ref, o_ref):
        o_ref[...] = (_sigmoid(gf_ref[...]) * bf_ref[...] + _sigmoid(gg_ref[...]) * bg_ref[...]).astype(BF16)

    lo = pl.BlockSpec((tm, D), lambda i: (i, 0))
    hi = pl.BlockSpec((tm, D), lambda i: (i, 1))
    return pl.pallas_call(
        body, name="merge_fwd", grid=(T // tm,), in_specs=[lo, hi, lo, lo], out_specs=lo,
        out_shape=jax.ShapeDtypeStruct((T, D), BF16), compiler_params=_params("parallel"),
    )(p_gates, p_gates, bf_, bg_)


def merge_bwd(p_gates, bf_, bg_, dy):
    T, D = bf_.shape
    tm = _tile(T, 512, 8)

    def body(gf_ref, gg_ref, bf_ref, bg_ref, dy_ref, dbf_ref, dbg_ref, dgate_ref):
        d = dy_ref[...]
        sf, sg = _sigmoid(gf_ref[...]), _sigmoid(gg_ref[...])
        dbf_ref[...] = (d * sf).astype(BF16)
        dbg_ref[...] = (d * sg).astype(BF16)
        dgate_ref[:, 0:D] = (d * bf_ref[...] * sf * (1.0 - sf)).astype(BF16)
        dgate_ref[:, D:2 * D] = (d * bg_ref[...] * sg * (1.0 - sg)).astype(BF16)

    lo = pl.BlockSpec((tm, D), lambda i: (i, 0))
    hi = pl.BlockSpec((tm, D), lambda i: (i, 1))
    both = pl.BlockSpec((tm, 2 * D), lambda i: (i, 0))
    return pl.pallas_call(
        body, name="merge_bwd", grid=(T // tm,), in_specs=[lo, hi, lo, lo, lo], out_specs=[lo, lo, both],
        out_shape=[jax.ShapeDtypeStruct((T, D), BF16), jax.ShapeDtypeStruct((T, D), BF16),
                   jax.ShapeDtypeStruct((T, 2 * D), BF16)],
        compiler_params=_params("parallel"),
    )(p_gates, p_gates, bf_, bg_, dy)


def fox_fwd(p_fox, c_col, c_row, B, S, H, rider=None):
    T = B * S
    t = _tile(S, ATT_TILE)
    nq = S // t
    scale = HEAD ** -0.5

    def body(q_ref, k_ref, v_ref, cq_ref, cr_ref, o_ref, o16_ref, lse_ref):
        i = pl.program_id(2)
        q = q_ref[...]
        cq = cq_ref[...]
        row = lax.broadcasted_iota(jnp.int32, (t, t), 0)
        col = lax.broadcasted_iota(jnp.int32, (t, t), 1)

        def step(j, carry, diagonal):
            m, l, acc = carry
            off = pl.multiple_of(j * t, t)
            k = k_ref[pl.ds(off, t), :]
            v = v_ref[pl.ds(off, t), :]
            s = lax.dot_general(q, k, _NT, preferred_element_type=F32) * scale + (cq - cr_ref[:, pl.ds(off, t)])
            if diagonal:
                s = jnp.where(col <= row, s, NEG)
            m_new = jnp.maximum(m, jnp.max(s, axis=-1, keepdims=True))
            alpha = jnp.exp(m - m_new)
            p = jnp.exp(s - m_new)
            l = alpha * l + jnp.sum(p, axis=-1, keepdims=True)
            acc = alpha * acc + lax.dot_general(p.astype(BF16), v, _NN, preferred_element_type=F32)
            return m_new, l, acc

        m0 = jnp.full((t, 1), NEG, F32)
        below = lax.fori_loop(0, i, functools.partial(step, diagonal=False),
                              (m0, jnp.zeros((t, 1), F32), jnp.zeros((t, HEAD), F32)))
        m, l, acc = step(i, below, diagonal=True)
        o = acc / l
        o_ref[...] = o
        o16_ref[...] = o.astype(BF16)
        lse_ref[...] = m + jnp.log(l)

    return _hosted_call(
        body, rider, name="fox_fwd", grid=(B, H, nq),
        in_specs=[pl.BlockSpec((t, HEAD), lambda b, h, i: (b * nq + i, 3 * h)),
                  pl.BlockSpec((S, HEAD), lambda b, h, i: (b, 3 * h + 1)),
                  pl.BlockSpec((S, HEAD), lambda b, h, i: (b, 3 * h + 2)),
                  pl.BlockSpec((None, None, t, 1), lambda b, h, i: (b, h, i, 0)),
                  pl.BlockSpec((None, None, 1, S), lambda b, h, i: (b, h, 0, 0))],
        out_specs=[pl.BlockSpec((t, HEAD), lambda b, h, i: (b * nq + i, h)),
                   pl.BlockSpec((t, HEAD), lambda b, h, i: (b * nq + i, h)),
                   pl.BlockSpec((None, None, t, 1), lambda b, h, i: (b, h, i, 0))],
        out_shape=[jax.ShapeDtypeStruct((T, H * HEAD), F32), jax.ShapeDtypeStruct((T, H * HEAD), BF16),
                   jax.ShapeDtypeStruct((B, H, S, 1), F32)],
        scratch_shapes=[], semantics=("parallel", "parallel", "arbitrary"),
    )(p_fox, p_fox, p_fox, c_col, c_row)


def fox_bwd(p_fox, c_col, c_row, o, lse, do, B, S, H, rider=None):
    T = B * S
    t = _tile(S, ATT_TILE)
    n = S // t
    scale = HEAD ** -0.5

    def body(q_ref, k_ref, v_ref, cq_ref, cr_ref, o_ref, lse_ref, do_ref, dqkv_ref, dcq_ref, dcr_ref, dq_acc, delta_s):
        row = lax.broadcasted_iota(jnp.int32, (t, t), 0)
        col = lax.broadcasted_iota(jnp.int32, (t, t), 1)

        def prep(i, c):
            rows = pl.ds(pl.multiple_of(i * t, t), t)
            delta_s[rows, :] = jnp.sum(do_ref[rows, :] * o_ref[rows, :], axis=-1, keepdims=True)
            dq_acc[rows, :] = jnp.zeros((t, HEAD), F32)
            dcq_ref[rows, :] = jnp.zeros((t, 1), F32)
            return c

        lax.fori_loop(0, n, prep, 0)

        def kv_step(j, c):
            joff = pl.multiple_of(j * t, t)
            k = k_ref[pl.ds(joff, t), :]
            v = v_ref[pl.ds(joff, t), :]
            crj = cr_ref[:, pl.ds(joff, t)]

            def q_step(i, carry, diagonal):
                dk, dv, dc = carry
                rows = pl.ds(pl.multiple_of(i * t, t), t)
                q = q_ref[rows, :]
                dob = do_ref[rows, :].astype(BF16)
                s = lax.dot_general(q, k, _NT, preferred_element_type=F32) * scale + (cq_ref[rows, :] - crj)
                if diagonal:
                    s = jnp.where(col <= row, s, NEG)
                p = jnp.exp(s - lse_ref[rows, :])
                dp = lax.dot_general(dob, v, _NT, preferred_element_type=F32)
                ds = p * (dp - delta_s[rows, :])
                dsb = ds.astype(BF16)
                dv = dv + lax.dot_general(p.astype(BF16), dob, _TN, preferred_element_type=F32)
                dk = dk + lax.dot_general(dsb, q, _TN, preferred_element_type=F32)
                dq_acc[rows, :] += lax.dot_general(dsb, k, _NN, preferred_element_type=F32) * scale
                dc = dc + jnp.sum(ds, axis=0, keepdims=True)
                dcq_ref[rows, :] += jnp.sum(ds, axis=-1, keepdims=True)
                return dk, dv, dc

            z = jnp.zeros((t, HEAD), F32)
            on_diagonal = q_step(j, (z, z, jnp.zeros((1, t), F32)), diagonal=True)
            dk, dv, dc = lax.fori_loop(j + 1, n, functools.partial(q_step, diagonal=False), on_diagonal)
            dqkv_ref[pl.ds(joff, t), HEAD:2 * HEAD] = (dk * scale).astype(BF16)
            dqkv_ref[pl.ds(joff, t), 2 * HEAD:3 * HEAD] = dv.astype(BF16)
            dcr_ref[:, pl.ds(joff, t)] = -dc
            return c

        lax.fori_loop(0, n, kv_step, 0)
        dqkv_ref[:, 0:HEAD] = dq_acc[...].astype(BF16)

    col_spec = pl.BlockSpec((None, None, S, 1), lambda b, h: (b, h, 0, 0))
    row_spec = pl.BlockSpec((None, None, 1, S), lambda b, h: (b, h, 0, 0))
    head = pl.BlockSpec((S, HEAD), lambda b, h: (b, h))
    return _hosted_call(
        body, rider, name="fox_bwd", grid=(B, H),
        in_specs=[pl.BlockSpec((S, HEAD), lambda b, h: (b, 3 * h)),
                  pl.BlockSpec((S, HEAD), lambda b, h: (b, 3 * h + 1)),
                  pl.BlockSpec((S, HEAD), lambda b, h: (b, 3 * h + 2)),
                  col_spec, row_spec, head, col_spec, head],
        out_specs=[pl.BlockSpec((S, 3 * HEAD), lambda b, h: (b, h)), col_spec, row_spec],
        out_shape=[jax.ShapeDtypeStruct((T, 3 * H * HEAD), BF16), jax.ShapeDtypeStruct((B, H, S, 1), F32),
                   jax.ShapeDtypeStruct((B, H, 1, S), F32)],
        scratch_shapes=[pltpu.VMEM((S, HEAD), F32), pltpu.VMEM((S, 1), F32)], semantics=("parallel", "parallel"),
    )(p_fox, p_fox, p_fox, c_col, c_row, o, lse, do)


def _small_fn(x, b0, b1, H):
    S = x.shape[0]
    lane = lax.broadcasted_iota(jnp.int32, x.shape, 1)
    z = x + b0
    tail = jnp.log1p(jnp.exp(-jnp.abs(z)))
    softplus = jnp.maximum(z, 0.0) + tail
    logsig = -(jnp.maximum(-z, 0.0) + tail)
    g = -jnp.exp(b1) * softplus
    pre = jnp.where(lane < H, logsig, jnp.where(lane < 2 * H, g, 0.0))
    bl = _tile(S, 256, CHUNK)
    r = lax.broadcasted_iota(jnp.int32, (bl, bl), 0)
    c = lax.broadcasted_iota(jnp.int32, (bl, bl), 1)
    tri = (r >= c).astype(F32)
    tri_chunk = jnp.where((r >= c) & (jnp.right_shift(r, 6) == jnp.right_shift(c, 6)), 1.0, 0.0)
    carry = jnp.zeros((1, x.shape[1]), F32)
    parts = []
    for i in range(S // bl):
        blk = pre[i * bl:(i + 1) * bl, :]
        full = lax.dot_general(tri, blk, _NN, precision=EXACT, preferred_element_type=F32) + carry
        chunked = lax.dot_general(tri_chunk, blk, _NN, precision=EXACT, preferred_element_type=F32)
        parts.append(jnp.where(lane[:bl] < H, full, chunked))
        carry = carry + jnp.sum(blk, axis=0, keepdims=True)
    cum = parts[0] if len(parts) == 1 else jnp.concatenate(parts, axis=0)
    return jnp.where(lane < 2 * H, cum, jnp.where(lane < 3 * H, _sigmoid(x), 0.0))


def small_fwd(p_small, prm, B, S, H):
    T = B * S

    def body(x_ref, p_ref, o_ref):
        o_ref[...] = _small_fn(x_ref[...], p_ref[0:1, :], p_ref[1:2, :], H)

    blk = pl.BlockSpec((S, 128), lambda b: (b, 0))
    return pl.pallas_call(
        body, name="small_fwd", grid=(B,), in_specs=[blk, pl.BlockSpec((8, 128), lambda b: (0, 0))], out_specs=blk,
        out_shape=jax.ShapeDtypeStruct((T, 128), F32), compiler_params=_params("parallel"),
    )(p_small, prm)


def small_bwd(p_small, prm, d_out, B, S, H):
    T = B * S

    def body(x_ref, p_ref, d_ref, dx_ref, dp_ref):
        @pl.when(pl.program_id(0) == 0)
        def _():
            dp_ref[...] = jnp.zeros_like(dp_ref)

        _, vjp = jax.vjp(functools.partial(_small_fn, H=H), x_ref[...], p_ref[0:1, :], p_ref[1:2, :])
        dx, db0, db1 = vjp(d_ref[...])
        dx_ref[...] = dx.astype(BF16)
        dp_ref[0:1, :] += db0
        dp_ref[1:2, :] += db1

    blk = pl.BlockSpec((S, 128), lambda b: (b, 0))
    pblk = pl.BlockSpec((8, 128), lambda b: (0, 0))
    return pl.pallas_call(
        body, name="small_bwd", grid=(B,), in_specs=[blk, pblk, blk], out_specs=[blk, pblk],
        out_shape=[jax.ShapeDtypeStruct((T, 128), BF16), jax.ShapeDtypeStruct((8, 128), F32)],
        compiler_params=_params("arbitrary"),
    )(p_small, prm, d_out)


def gdn_prep_fwd(p_gqkv, cw, B, S, H):
    T = B * S

    def body(x_ref, w_ref, o_ref):
        y = _conv_fwd(x_ref[...], w_ref, GDN_CONV)
        a = y * _sigmoid(y)
        rs = lax.rsqrt(jnp.sum(a * a, axis=-1, keepdims=True) + EPS)
        is_qk = (pl.program_id(1) % 3) < 2
        o_ref[...] = a * jnp.where(is_qk, rs, 1.0)

    blk = pl.BlockSpec((S, HEAD), lambda b, n: (b, n))
    wblk = pl.BlockSpec((GDN_CONV, HEAD), lambda b, n: (0, n))
    return pl.pallas_call(
        body, name="gdn_prep_fwd", grid=(B, 3 * H), in_specs=[blk, wblk], out_specs=blk,
        out_shape=jax.ShapeDtypeStruct((T, 3 * H * HEAD), F32), compiler_params=_params("parallel", "parallel"),
    )(p_gqkv, cw)


def gdn_prep_bwd(p_gqkv, cw, d_out, B, S, H):
    T = B * S

    def body(x_ref, w_ref, d_ref, dx_ref, dw_ref):
        @pl.when(pl.program_id(1) == 0)
        def _():
            dw_ref[...] = jnp.zeros_like(dw_ref)

        x = x_ref[...]
        y = _conv_fwd(x, w_ref, GDN_CONV)
        sg = _sigmoid(y)
        a = y * sg
        rs = lax.rsqrt(jnp.sum(a * a, axis=-1, keepdims=True) + EPS)
        d = d_ref[...]
        out = a * rs
        da_qk = rs * (d - out * jnp.sum(d * out, axis=-1, keepdims=True))
        is_qk = (pl.program_id(0) % 3) < 2
        da = jnp.where(is_qk, da_qk, d)
        dy = da * (sg + y * sg * (1.0 - sg))
        dx, dws = _conv_bwd(x, dy, w_ref, GDN_CONV)
        dx_ref[...] = dx.astype(BF16)
        for i in range(GDN_CONV):
            dw_ref[i:i + 1, :] += dws[i]

    blk = pl.BlockSpec((S, HEAD), lambda n, b: (b, n))
    wblk = pl.BlockSpec((GDN_CONV, HEAD), lambda n, b: (0, n))
    return pl.pallas_call(
        body, name="gdn_prep_bwd", grid=(3 * H, B), in_specs=[blk, wblk, blk], out_specs=[blk, wblk],
        out_shape=[jax.ShapeDtypeStruct((T, 3 * H * HEAD), BF16), jax.ShapeDtypeStruct((GDN_CONV, 3 * H * HEAD), F32)],
        compiler_params=_params("parallel", "arbitrary"),
    )(p_gqkv, cw, d_out)


@jax.custom_vjp
def _given_inverse(a, t):
    return t


def _given_inverse_fwd(a, t):
    return t, t


def _given_inverse_bwd(t, g):
    x = _dg(t, g, _TN, True)
    return -_dg(x, t, _NT, True), jnp.zeros_like(t)


_given_inverse.defvjp(_given_inverse_fwd, _given_inverse_bwd)


def _to_col(row):
    r = lax.broadcasted_iota(jnp.int32, (CHUNK, CHUNK), 0)
    c = lax.broadcasted_iota(jnp.int32, (CHUNK, CHUNK), 1)
    return jnp.sum(jnp.where(r == c, row, 0.0), axis=1, keepdims=True)


def _intra_fn(k, v, beta_r, gcr, ops, t_known=None):
    n = len(k)
    r = lax.broadcasted_iota(jnp.int32, (CHUNK, CHUNK), 0)
    c = lax.broadcasted_iota(jnp.int32, (CHUNK, CHUNK), 1)
    beta = [_to_col(beta_r[i]) for i in range(n)]
    gcc = [_to_col(gcr[i]) for i in range(n)]
    decay = [jnp.exp(jnp.where(r > c, gcc[i] - gcr[i], NEG)) for i in range(n)]
    kb = [k[i] * beta[i] for i in range(n)]
    a = [ops.nt(kb[i], k[i]) * decay[i] for i in range(n)]
    if t_known is None:
        p = [-a[i] for i in range(n)]
        tm = [jnp.where(r == c, 1.0, 0.0) + p[i] for i in range(n)]
        for _ in range(5):
            p = [ops.nn(p[i], p[i], hi=True) for i in range(n)]
            tm = [tm[i] + ops.nn(tm[i], p[i], hi=True) for i in range(n)]
    else:
        tm = [_given_inverse(a[i], t_known[i]) for i in range(n)]
    u_hat = [ops.nn(tm[i], v[i] * beta[i], hi=True) for i in range(n)]
    w = [ops.nn(tm[i], kb[i] * jnp.exp(gcc[i]), hi=True) for i in range(n)]
    return tuple(u_hat), tuple(w), tuple(tm)


INTRA_NB = 8


def gdn_intra_fwd(qkvn, betar5, gcr5, B, S, H, rider=None):
    T = B * S
    N = S // CHUNK
    nb = min(INTRA_NB, N)
    rows = nb * CHUNK
    ns = N // nb

    def body(k_ref, v_ref, b_ref, gr_ref, uh_ref, w_ref, t_ref):
        sls = [slice(ci * CHUNK, (ci + 1) * CHUNK) for ci in range(nb)]
        u_hat, w, tm = _intra_fn(tuple(k_ref[sl, :] for sl in sls), tuple(v_ref[sl, :] for sl in sls),
                                 tuple(b_ref[ci] for ci in range(nb)), tuple(gr_ref[ci] for ci in range(nb)), _RawOps)
        for ci, sl in enumerate(sls):
            uh_ref[sl, :] = u_hat[ci]
            w_ref[sl, :] = w[ci]
            t_ref[ci] = tm[ci]

    rowspec = pl.BlockSpec((None, None, nb, 1, CHUNK), lambda b, h, i: (b, h, i, 0, 0))
    sqspec = pl.BlockSpec((None, None, nb, CHUNK, CHUNK), lambda b, h, i: (b, h, i, 0, 0))
    out = pl.BlockSpec((rows, HEAD), lambda b, h, i: (b * ns + i, h))
    return _hosted_call(
        body, rider, name="gdn_intra_fwd", grid=(B, H, ns),
        in_specs=[pl.BlockSpec((rows, HEAD), lambda b, h, i: (b * ns + i, 3 * h + 1)),
                  pl.BlockSpec((rows, HEAD), lambda b, h, i: (b * ns + i, 3 * h + 2)),
                  rowspec, rowspec],
        out_specs=[out, out, sqspec],
        out_shape=[jax.ShapeDtypeStruct((T, H * HEAD), F32), jax.ShapeDtypeStruct((T, H * HEAD), F32),
                   jax.ShapeDtypeStruct((B, H, N, CHUNK, CHUNK), F32)],
        scratch_shapes=[], semantics=("parallel", "parallel", "parallel"),
    )(qkvn, qkvn, betar5, gcr5)


def gdn_intra_bwd(qkvn, betar5, gcr5, t_inv, d_uh, d_w, dq_in, dk_in, B, S, H):
    T = B * S
    N = S // CHUNK
    nb = min(INTRA_NB, N)
    rows = nb * CHUNK
    ns = N // nb

    def body(k_ref, v_ref, b_ref, gr_ref, t_ref, duh_ref, dw_ref, dq_ref, dk_ref, o_ref, db_ref, dgr_ref):
        sls = [slice(ci * CHUNK, (ci + 1) * CHUNK) for ci in range(nb)]
        chunks = range(nb)
        _, vjp = jax.vjp(
            functools.partial(_intra_fn, ops=_DiffOps, t_known=tuple(t_ref[ci] for ci in chunks)),
            tuple(k_ref[sl, :] for sl in sls), tuple(v_ref[sl, :] for sl in sls), tuple(b_ref[ci] for ci in chunks),
            tuple(gr_ref[ci] for ci in chunks))
        zero = jnp.zeros((CHUNK, CHUNK), F32)
        dk, dv, db, dgr = vjp((tuple(duh_ref[sl, :] for sl in sls), tuple(dw_ref[sl, :] for sl in sls),
                               tuple(zero for _ in chunks)))
        for ci, sl in enumerate(sls):
            o_ref[sl, 0:HEAD] = dq_ref[sl, :]
            o_ref[sl, HEAD:2 * HEAD] = dk[ci] + dk_ref[sl, :]
            o_ref[sl, 2 * HEAD:3 * HEAD] = dv[ci]
            db_ref[ci] = db[ci]
            dgr_ref[ci] = dgr[ci]

    rowspec = pl.BlockSpec((None, None, nb, 1, CHUNK), lambda b, h, i: (b, h, i, 0, 0))
    sqspec = pl.BlockSpec((None, None, nb, CHUNK, CHUNK), lambda b, h, i: (b, h, i, 0, 0))
    head = pl.BlockSpec((rows, HEAD), lambda b, h, i: (b * ns + i, h))
    return pl.pallas_call(
        body, name="gdn_intra_bwd", grid=(B, H, ns),
        in_specs=[pl.BlockSpec((rows, HEAD), lambda b, h, i: (b * ns + i, 3 * h + 1)),
                  pl.BlockSpec((rows, HEAD), lambda b, h, i: (b * ns + i, 3 * h + 2)),
                  rowspec, rowspec, sqspec, head, head, head, head],
        out_specs=[pl.BlockSpec((rows, 3 * HEAD), lambda b, h, i: (b * ns + i, h)), rowspec, rowspec],
        out_shape=[jax.ShapeDtypeStruct((T, 3 * H * HEAD), F32),
                   jax.ShapeDtypeStruct((B, H, N, 1, CHUNK), F32), jax.ShapeDtypeStruct((B, H, N, 1, CHUNK), F32)],
        compiler_params=_params("parallel", "parallel", "parallel"),
    )(qkvn, qkvn, betar5, gcr5, t_inv, d_uh, d_w, dq_in, dk_in)


def _inter_fn(q, k, u_hat, w, gcr, state, ops):
    n = len(q)
    r = lax.broadcasted_iota(jnp.int32, (CHUNK, CHUNK), 0)
    c = lax.broadcasted_iota(jnp.int32, (CHUNK, CHUNK), 1)
    last = lax.broadcasted_iota(jnp.int32, (1, CHUNK), 1) == CHUNK - 1
    gcc = [_to_col(gcr[i]) for i in range(n)]
    gl = [jnp.sum(jnp.where(last, gcr[i], 0.0), axis=1, keepdims=True) for i in range(n)]
    decay = [jnp.exp(jnp.where(r >= c, gcc[i] - gcr[i], NEG)) for i in range(n)]
    qs = [q[i] * (HEAD ** -0.5) for i in range(n)]
    ws = [ops.nn(w[i], state[i]) for i in range(n)]
    qst = [ops.nn(qs[i] * jnp.exp(gcc[i]), state[i]) for i in range(n)]
    attn = [ops.nt(qs[i], k[i]) * decay[i] for i in range(n)]
    u = [u_hat[i] - ws[i] for i in range(n)]
    o = [qst[i] + ops.nn(attn[i], u[i]) for i in range(n)]
    kdu = [ops.tn(k[i] * jnp.exp(gl[i] - gcc[i]), u[i]) for i in range(n)]
    new_state = [state[i] * jnp.exp(gl[i]) + kdu[i] for i in range(n)]
    return tuple(o), tuple(new_state)


INTER_HEADS = 4
INTER_ROWS = 512


def _inter_specs(ts, ns, hp, backward):
    at = (lambda s: ns - 1 - s) if backward else (lambda s: s)
    nc = ts // CHUNK
    qk = []
    for hh in range(hp):
        qk.append(pl.BlockSpec((ts, HEAD), lambda b, g, s, hh=hh: (b * ns + at(s), 3 * (hp * g + hh))))
        qk.append(pl.BlockSpec((ts, HEAD), lambda b, g, s, hh=hh: (b * ns + at(s), 3 * (hp * g + hh) + 1)))
    heads = pl.BlockSpec((ts, hp * HEAD), lambda b, g, s: (b * ns + at(s), g))
    rowspec = pl.BlockSpec((None, hp, nc, 1, CHUNK), lambda b, g, s: (b, g, at(s), 0, 0))
    stspec = pl.BlockSpec((None, hp, nc, HEAD, HEAD), lambda b, g, s: (b, g, at(s), 0, 0))
    return qk, heads, rowspec, stspec


def gdn_inter_fwd(qkvn, u_hat, w, gcr5, B, S, H):
    T = B * S
    N = S // CHUNK
    hp = INTER_HEADS if H % INTER_HEADS == 0 else 1
    hs = range(hp)
    ts = _tile(S, INTER_ROWS, CHUNK)
    ns, nc = S // ts, ts // CHUNK

    def body(*refs):
        qk_refs, (uh_ref, w_ref, gr_ref, o_ref, st_ref, s_scr) = refs[:2 * hp], refs[2 * hp:]

        @pl.when(pl.program_id(2) == 0)
        def _():
            s_scr[...] = jnp.zeros_like(s_scr)

        def step(n, c):
            rows = pl.ds(pl.multiple_of(n * CHUNK, CHUNK), CHUNK)
            st = tuple(s_scr[hh] for hh in hs)
            for hh in hs:
                st_ref[hh, n] = st[hh]
            o, new = _inter_fn(tuple(qk_refs[2 * hh][rows, :] for hh in hs), tuple(qk_refs[2 * hh + 1][rows, :] for hh in hs),
                               tuple(uh_ref[rows, hh * HEAD:(hh + 1) * HEAD] for hh in hs),
                               tuple(w_ref[rows, hh * HEAD:(hh + 1) * HEAD] for hh in hs),
                               tuple(gr_ref[hh, n] for hh in hs), st, _RawOps)
            for hh in hs:
                o_ref[rows, hh * HEAD:(hh + 1) * HEAD] = o[hh]
                s_scr[hh] = new[hh]
            return c

        lax.fori_loop(0, nc, step, 0)

    qk, heads, rowspec, stspec = _inter_specs(ts, ns, hp, backward=False)
    return pl.pallas_call(
        body, name="gdn_inter_fwd", grid=(B, H // hp, ns),
        in_specs=qk + [heads, heads, rowspec], out_specs=[heads, stspec],
        out_shape=[jax.ShapeDtypeStruct((T, H * HEAD), F32), jax.ShapeDtypeStruct((B, H, N, HEAD, HEAD), F32)],
        scratch_shapes=[pltpu.VMEM((hp, HEAD, HEAD), F32)],
        compiler_params=_params("parallel", "parallel", "arbitrary"),
    )(*([qkvn] * (2 * hp)), u_hat, w, gcr5)


def gdn_inter_bwd(qkvn, u_hat, w, gcr5, states, d_o, B, S, H, rider=None):
    T = B * S
    N = S // CHUNK
    hp = INTER_HEADS if H % INTER_HEADS == 0 else 1
    hs = range(hp)
    ts = _tile(S, INTER_ROWS, CHUNK)
    ns, nc = S // ts, ts // CHUNK

    def body(*refs):
        qk_refs = refs[:2 * hp]
        uh_ref, w_ref, gr_ref, st_ref, do_ref, dq_ref, dk_ref, duh_ref, dw_ref, dgr_ref, ds_scr = refs[2 * hp:]

        @pl.when(pl.program_id(2) == 0)
        def _():
            ds_scr[...] = jnp.zeros_like(ds_scr)

        cols = [slice(hh * HEAD, (hh + 1) * HEAD) for hh in hs]

        def step(i, c):
            n = nc - 1 - i
            rows = pl.ds(pl.multiple_of(n * CHUNK, CHUNK), CHUNK)
            _, vjp = jax.vjp(functools.partial(_inter_fn, ops=_DiffOps),
                             tuple(qk_refs[2 * hh][rows, :] for hh in hs), tuple(qk_refs[2 * hh + 1][rows, :] for hh in hs),
                             tuple(uh_ref[rows, cols[hh]] for hh in hs), tuple(w_ref[rows, cols[hh]] for hh in hs),
                             tuple(gr_ref[hh, n] for hh in hs), tuple(st_ref[hh, n] for hh in hs))
            dq, dk, duh, dw, dgr, ds = vjp((tuple(do_ref[rows, cols[hh]] for hh in hs), tuple(ds_scr[hh] for hh in hs)))
            for hh in hs:
                dq_ref[rows, cols[hh]] = dq[hh]
                dk_ref[rows, cols[hh]] = dk[hh]
                duh_ref[rows, cols[hh]] = duh[hh]
                dw_ref[rows, cols[hh]] = dw[hh]
                dgr_ref[hh, n] = dgr[hh]
                ds_scr[hh] = ds[hh]
            return c

        lax.fori_loop(0, nc, step, 0)

    qk, heads, rowspec, stspec = _inter_specs(ts, ns, hp, backward=True)
    hshape = jax.ShapeDtypeStruct((T, H * HEAD), F32)
    return _hosted_call(
        body, rider, name="gdn_inter_bwd", grid=(B, H // hp, ns),
        in_specs=qk + [heads, heads, rowspec, stspec, heads],
        out_specs=[heads, heads, heads, heads, rowspec],
        out_shape=[hshape, hshape, hshape, hshape, jax.ShapeDtypeStruct((B, H, N, 1, CHUNK), F32)],
        scratch_shapes=[pltpu.VMEM((hp, HEAD, HEAD), F32)], semantics=("parallel", "parallel", "arbitrary"),
    )(*([qkvn] * (2 * hp)), u_hat, w, gcr5, states, d_o)


def gdn_post_fwd(o, p_gz, g, H):
    T = o.shape[0]
    tm = _tile(T, 1024, 8)

    def body(o_ref, z_ref, g_ref, y_ref):
        ov, z = o_ref[...], z_ref[...]
        r = lax.rsqrt(jnp.mean(ov * ov, axis=-1, keepdims=True) + EPS)
        y_ref[...] = (ov * r * g_ref[...] * z * _sigmoid(z)).astype(BF16)

    blk = pl.BlockSpec((tm, HEAD), lambda i, h: (i, h))
    return pl.pallas_call(
        body, name="gdn_post_fwd", grid=(T // tm, H), in_specs=[blk, blk, pl.BlockSpec((1, HEAD), lambda i, h: (0, 0))],
        out_specs=blk, out_shape=jax.ShapeDtypeStruct((T, H * HEAD), BF16),
        compiler_params=_params("parallel", "parallel"),
    )(o, p_gz, g)


def gdn_post_bwd(o, p_gz, g, dy, H):
    T = o.shape[0]
    tm = _tile(T, 1024, 8)

    def body(o_ref, z_ref, g_ref, dy_ref, do_ref, dz_ref, dg_ref):
        @pl.when((pl.program_id(0) == 0) & (pl.program_id(1) == 0))
        def _():
            dg_ref[...] = jnp.zeros_like(dg_ref)

        ov, z, d = o_ref[...], z_ref[...], dy_ref[...]
        r = lax.rsqrt(jnp.mean(ov * ov, axis=-1, keepdims=True) + EPS)
        xh = ov * r
        sg = _sigmoid(z)
        sz = z * sg
        d_n = d * sz
        dz_ref[...] = (d * xh * g_ref[...] * (sg + z * sg * (1.0 - sg))).astype(BF16)
        dg_ref[...] += jnp.sum(d_n * xh, axis=0, keepdims=True)
        dxh = d_n * g_ref[...]
        do_ref[...] = r * (dxh - xh * jnp.mean(dxh * xh, axis=-1, keepdims=True))

    blk = pl.BlockSpec((tm, HEAD), lambda i, h: (i, h))
    vec = pl.BlockSpec((1, HEAD), lambda i, h: (0, 0))
    return pl.pallas_call(
        body, name="gdn_post_bwd", grid=(T // tm, H), in_specs=[blk, blk, vec, blk], out_specs=[blk, blk, vec],
        out_shape=[jax.ShapeDtypeStruct((T, H * HEAD), F32), jax.ShapeDtypeStruct((T, H * HEAD), BF16),
                   jax.ShapeDtypeStruct((1, HEAD), F32)],
        compiler_params=_params("arbitrary", "arbitrary"),
    )(o, p_gz, g, dy)


def adamw(w, g, m, v, name):
    shape = w.shape
    lead = (None,) * (w.ndim - 2)
    zeros = (0,) * (w.ndim - 2)
    R, C = shape[-2:]
    g2 = g.reshape(R, C)
    tr = _tile(R, 128, 8)

    def body(w_ref, g_ref, m_ref, v_ref, d_ref, nm_ref, nv_ref):
        gv = g_ref[...]
        nm = ADAM_B1 * m_ref[...] + (1.0 - ADAM_B1) * gv
        nv = ADAM_B2 * v_ref[...] + (1.0 - ADAM_B2) * (gv * gv)
        m_hat = nm / (1.0 - ADAM_B1 ** ADAM_STEP)
        v_hat = nv / (1.0 - ADAM_B2 ** ADAM_STEP)
        d_ref[...] = -ADAM_LR * (m_hat / (jnp.sqrt(v_hat) + ADAM_EPS) + ADAM_WD * w_ref[...])
        nm_ref[...] = nm
        nv_ref[...] = nv

    blk = pl.BlockSpec(lead + (tr, C), lambda i: zeros + (i, 0))
    gblk = pl.BlockSpec((tr, C), lambda i: (i, 0))
    sh = jax.ShapeDtypeStruct(shape, F32)
    return pl.pallas_call(
        body, name=name, grid=(R // tr,), in_specs=[blk, gblk, blk, blk], out_specs=[blk] * 3, out_shape=[sh] * 3,
        compiler_params=_params("parallel"),
    )(w, g2, m, v)


def _place():
    x, y, c = lax.axis_index("x"), lax.axis_index("y"), lax.axis_index("c")
    chips = [(1 - x, y), (x, 1 - y), (1 - x, 1 - y)]
    return x, y, c, chips


_HBM = pl.BlockSpec(memory_space=pltpu.HBM)


def allgather_weights(packs):
    n = len(packs)

    def body(*refs):
        in_refs, out_refs, (send_sems, recv_sems) = refs[:n], refs[n:2 * n], refs[2 * n:]
        x, y, c, chips = _place()
        me_s = 2 * x + y
        me, sibling = (x, y, c), (x, y, 1 - c)
        shards = [2 * chip[0] + chip[1] for chip in chips]

        def copy(a, k, shard, half, to, src=None):
            dst = out_refs[a].at[shard, half]
            return pltpu.make_async_remote_copy(src_ref=dst if src is None else src, dst_ref=dst,
                                                send_sem=send_sems.at[6 * a + k], recv_sem=recv_sems.at[6 * a + k],
                                                device_id=to, device_id_type=MESH)

        first = [copy(a, j, me_s, c, (*chip, c), src=in_refs[a].at[c]) for a in range(n) for j, chip in enumerate(chips)]
        for cp in first:
            cp.start()
        passed = []
        for a in range(n):
            for j in range(3):
                copy(a, j, shards[j], c, me).wait_recv()
                passed.append(copy(a, 3 + j, shards[j], c, sibling))
                passed[-1].start()
        for a in range(n):
            for j in range(3):
                copy(a, 3 + j, shards[j], 1 - c, me).wait_recv()
        for cp in first + passed:
            cp.wait_send()

    return pl.pallas_call(
        body, name="allgather_weights", in_specs=[_HBM] * n, out_specs=[_HBM] * n,
        out_shape=[jax.ShapeDtypeStruct((N_CHIP,) + p.shape, p.dtype) for p in packs],
        scratch_shapes=[pltpu.SemaphoreType.DMA((6 * n,)), pltpu.SemaphoreType.DMA((6 * n,))],
    )(*packs)


class _Rider:
    def __init__(self, inputs, out_shapes, n_sems, sends, recvs, aliases=None):
        self.inputs, self.out_shapes, self.n_sems = list(inputs), list(out_shapes), n_sems
        self.sends, self.recvs, self.aliases = sends, recvs, aliases or {}

    def start(self, *refs):
        for cp in self.sends(*refs):
            cp.start()

    def wait(self, *refs):
        for cp in self.recvs(*refs):
            cp.wait_recv()
        for cp in self.sends(*refs):
            cp.wait_send()


def _remote(src, dst, send_sems, recv_sems, k, to):
    return pltpu.make_async_remote_copy(src_ref=src, dst_ref=dst, send_sem=send_sems.at[k], recv_sem=recv_sems.at[k],
                                        device_id=to, device_id_type=MESH)


def _run_alone(rider, name):
    ri = len(rider.inputs)

    def body(*refs):
        ins, outs, (send_sems, recv_sems) = refs[:ri], refs[ri:-2], refs[-2:]
        rider.start(ins, outs, send_sems, recv_sems)
        rider.wait(ins, outs, send_sems, recv_sems)

    return pl.pallas_call(
        body, name=name, in_specs=[_HBM] * ri, out_specs=[_HBM] * len(rider.out_shapes), out_shape=rider.out_shapes,
        scratch_shapes=[pltpu.SemaphoreType.DMA((rider.n_sems,))] * 2, input_output_aliases=rider.aliases,
    )(*rider.inputs)


def _hosted_call(body, rider, *, name, grid, in_specs, out_specs, out_shape, scratch_shapes, semantics):
    if rider is None:
        return pl.pallas_call(body, name=name, grid=grid, in_specs=in_specs, out_specs=out_specs, out_shape=out_shape,
                              scratch_shapes=scratch_shapes, compiler_params=_params(*semantics))
    n_in, n_out, n_scr = len(in_specs), len(out_specs), len(scratch_shapes)
    ri, ro = len(rider.inputs), len(rider.out_shapes)

    def hosted(*refs):
        parts, p = [], 0
        for cnt in (n_in, ri, n_out, ro, n_scr, 2):
            parts.append(refs[p:p + cnt])
            p += cnt
        ins, rins, outs, routs, scr, (send_sems, recv_sems) = parts
        first = functools.reduce(jnp.logical_and, [pl.program_id(a) == 0 for a in range(len(grid))])
        last = functools.reduce(jnp.logical_and, [pl.program_id(a) == grid[a] - 1 for a in range(len(grid))])

        @pl.when(first)
        def _():
            rider.start(rins, routs, send_sems, recv_sems)

        body(*ins, *outs, *scr)

        @pl.when(last)
        def _():
            rider.wait(rins, routs, send_sems, recv_sems)

    call = pl.pallas_call(
        hosted, name=name, grid=grid, in_specs=list(in_specs) + [_HBM] * ri, out_specs=list(out_specs) + [_HBM] * ro,
        out_shape=list(out_shape) + rider.out_shapes,
        scratch_shapes=list(scratch_shapes) + [pltpu.SemaphoreType.DMA((rider.n_sems,))] * 2,
        input_output_aliases={n_in + i: n_out + o for i, o in rider.aliases.items()},
        compiler_params=_params(*(("arbitrary",) * len(grid))))

    def run(*args):
        res = call(*args, *rider.inputs)
        return res[:n_out], res[n_out:]

    return run


def _ride_gather_ici(packs):
    n = len(packs)

    def sends(ins, outs, send_sems, recv_sems):
        x, y, c, chips = _place()
        return [_remote(ins[a].at[c], outs[a].at[2 * x + y, c], send_sems, recv_sems, 3 * a + j, (*chip, c))
                for a in range(n) for j, chip in enumerate(chips)]

    def recvs(ins, outs, send_sems, recv_sems):
        x, y, c, chips = _place()
        return [_remote(ins[a].at[c], outs[a].at[2 * chip[0] + chip[1], c], send_sems, recv_sems, 3 * a + j, (x, y, c))
                for a in range(n) for j, chip in enumerate(chips)]

    return _Rider(packs, [jax.ShapeDtypeStruct((N_CHIP,) + p.shape, p.dtype) for p in packs], 3 * n, sends, recvs)


def _ride_gather_d2d(gathered):
    n = len(gathered)

    def copies(landing_half, to):
        def build(ins, outs, send_sems, recv_sems):
            x, y, c, chips = _place()
            return [_remote(ins[a].at[2 * chip[0] + chip[1], c], outs[a].at[2 * chip[0] + chip[1], landing_half(c)],
                            send_sems, recv_sems, 3 * a + j, to(x, y, c))
                    for a in range(n) for j, chip in enumerate(chips)]
        return build

    return _Rider(gathered, [jax.ShapeDtypeStruct(g.shape, g.dtype) for g in gathered], 3 * n,
                  copies(lambda c: c, lambda x, y, c: (x, y, 1 - c)), copies(lambda c: 1 - c, lambda x, y, c: (x, y, c)),
                  aliases={a: a for a in range(n)})


def _ride_exchange(gs):
    n = len(gs)

    def copies(ins, outs, send_sems, recv_sems):
        x, y, c, _ = _place()
        return [_remote(ins[a].at[1 - c], outs[a], send_sems, recv_sems, a, (x, y, 1 - c)) for a in range(n)]

    return _Rider(gs, [jax.ShapeDtypeStruct(g.shape[1:], g.dtype) for g in gs], n, copies, copies)


def _ride_scatter(b16s):
    n = len(b16s)

    def sends(ins, outs, send_sems, recv_sems):
        x, y, c, chips = _place()
        return [_remote(ins[a].at[2 * chip[0] + chip[1]], outs[a].at[2 * x + y], send_sems, recv_sems, 3 * a + j, (*chip, c))
                for a in range(n) for j, chip in enumerate(chips)]

    def recvs(ins, outs, send_sems, recv_sems):
        x, y, c, chips = _place()
        return [_remote(ins[a].at[2 * x + y], outs[a].at[2 * chip[0] + chip[1]], send_sems, recv_sems, 3 * a + j, (x, y, c))
                for a in range(n) for j, chip in enumerate(chips)]

    return _Rider(b16s, [jax.ShapeDtypeStruct(b.shape, b.dtype) for b in b16s], 3 * n, sends, recvs)


def add_halves(g, got, idx, name):
    _, ns, r, cols = g.shape
    tr = _tile(r, 256, 16)

    def body(idx_ref, a_ref, b_ref, o32_ref, o16_ref):
        s = a_ref[...] + b_ref[...]
        o32_ref[...] = s
        o16_ref[...] = s.astype(BF16)

    blk = pl.BlockSpec((None, tr, cols), lambda s, i, idx_ref: (s, i, 0))
    return pl.pallas_call(
        body, name=name,
        grid_spec=pltpu.PrefetchScalarGridSpec(
            num_scalar_prefetch=1, grid=(ns, r // tr),
            in_specs=[pl.BlockSpec((None, None, tr, cols), lambda s, i, idx_ref: (idx_ref[0], s, i, 0)), blk],
            out_specs=[blk, blk]),
        out_shape=[jax.ShapeDtypeStruct((ns, r, cols), F32), jax.ShapeDtypeStruct((ns, r, cols), BF16)],
        compiler_params=_params("parallel", "parallel"),
    )(idx, g, got)


def add_chips(a32, got16, idx, name):
    ns, r, cols = a32.shape
    tr = _tile(r, 256, 16)

    def body(idx_ref, a_ref, r1_ref, r2_ref, r3_ref, o_ref):
        o_ref[...] = ((a_ref[...] + r1_ref[...].astype(F32)) + r2_ref[...].astype(F32)) + r3_ref[...].astype(F32)

    def slab(k):
        return pl.BlockSpec((None, tr, cols), lambda i, idx_ref: ((idx_ref[1] + k) % ns, i, 0))

    return pl.pallas_call(
        body, name=name,
        grid_spec=pltpu.PrefetchScalarGridSpec(
            num_scalar_prefetch=1, grid=(r // tr,), in_specs=[slab(0), slab(1), slab(2), slab(3)],
            out_specs=pl.BlockSpec((tr, cols), lambda i, idx_ref: (i, 0))),
        out_shape=jax.ShapeDtypeStruct((r, cols), F32),
        compiler_params=_params("parallel"),
    )(idx, a32, got16, got16, got16)


def share_halves(halves):
    n = len(halves)

    def body(*refs):
        in_refs, out_refs, (send_sems, recv_sems) = refs[:n], refs[n:2 * n], refs[2 * n:]
        x, y, c, _ = _place()
        cps = [pltpu.make_async_remote_copy(src_ref=in_refs[a], dst_ref=out_refs[a], send_sem=send_sems.at[a],
                                            recv_sem=recv_sems.at[a], device_id=(x, y, 1 - c), device_id_type=MESH)
               for a in range(n)]
        for cp in cps:
            cp.start()
        for cp in cps:
            cp.wait()

    return pl.pallas_call(
        body, name="share_halves", in_specs=[_HBM] * n, out_specs=[_HBM] * n,
        out_shape=[jax.ShapeDtypeStruct(h.shape, F32) for h in halves],
        scratch_shapes=[pltpu.SemaphoreType.DMA((n,)), pltpu.SemaphoreType.DMA((n,))],
    )(*halves)


def allreduce_small(v):
    R, _ = v.shape

    def body(in_ref, out_ref, slots, send_sems, recv_sems):
        x, y, c, _ = _place()
        me = 4 * x + 2 * y + c
        slots[me] = in_ref[...]
        cps = []
        for k in range(1, N_DEV):
            to = (x ^ (k >> 2), y ^ ((k >> 1) & 1), c ^ (k & 1))
            cps.append(pltpu.make_async_remote_copy(src_ref=in_ref, dst_ref=slots.at[me], send_sem=send_sems.at[k - 1],
                                                    recv_sem=recv_sems.at[k - 1], device_id=to, device_id_type=MESH))
        for cp in cps:
            cp.start()
        for k in range(1, N_DEV):
            frm = 4 * (x ^ (k >> 2)) + 2 * (y ^ ((k >> 1) & 1)) + (c ^ (k & 1))
            pltpu.make_async_remote_copy(src_ref=in_ref, dst_ref=slots.at[frm], send_sem=send_sems.at[k - 1],
                                         recv_sem=recv_sems.at[k - 1], device_id=(x, y, c), device_id_type=MESH).wait_recv()
        for cp in cps:
            cp.wait_send()
        acc = slots[0]
        for d in range(1, N_DEV):
            acc = acc + slots[d]
        out_ref[...] = acc

    vm = pl.BlockSpec(memory_space=pltpu.VMEM)
    return pl.pallas_call(
        body, name="allreduce_small", in_specs=[vm], out_specs=vm, out_shape=jax.ShapeDtypeStruct((R, ROW), F32),
        scratch_shapes=[pltpu.VMEM((N_DEV, R, ROW), F32), pltpu.SemaphoreType.DMA((N_DEV - 1,)),
                        pltpu.SemaphoreType.DMA((N_DEV - 1,))],
    )(v)


def _rows_of(n, unit=16):
    return -(-n // (unit * ROW)) * unit


def _pack_rows(items, total_rows, dtype, unit=16):
    parts = []
    used = 0
    for a in items:
        flat = a.reshape(-1)
        r = _rows_of(flat.shape[0], unit)
        flat = jnp.pad(flat, (0, r * ROW - flat.shape[0]))
        parts.append(flat.reshape(r, ROW))
        used += r
    if total_rows > used:
        parts.append(jnp.zeros((total_rows - used, ROW), dtype))
    return jnp.concatenate(parts, axis=0)


def _unpack_rows(buf, shapes, unit=16):
    lead = buf.shape[:-2]
    out = []
    off = 0
    for shp in shapes:
        n = math.prod(shp)
        r = _rows_of(n, unit)
        piece = buf[..., off:off + r, :].reshape(*lead, r * ROW)[..., :n].reshape(*lead, *shp)
        out.append(piece)
        off += r
    return out


def _interleave_heads(w, H):
    lead = w.shape[:-1]
    return w.reshape(*lead, 3, H, HEAD).swapaxes(-3, -2).reshape(*lead, 3 * H * HEAD)


def _deinterleave_heads(w, H):
    lead = w.shape[:-1]
    return w.reshape(*lead, H, 3, HEAD).swapaxes(-3, -2).reshape(*lead, 3 * H * HEAD)


def kernel(x, norm_mix, w_in, fox_f_bias, gdn_conv_w, gdn_a_log, gdn_dt_bias, gdn_norm, w_branch_fox, w_branch_gdn, w_out, norm_ffn, w_up, ffn_conv_w, w_down, norm_final, loss_target, m_norm_mix, m_w_in, m_fox_f_bias, m_gdn_conv_w, m_gdn_a_log, m_gdn_dt_bias, m_gdn_norm, m_w_branch_fox, m_w_branch_gdn, m_w_out, m_norm_ffn, m_w_up, m_ffn_conv_w, m_w_down, m_norm_final, v_norm_mix, v_w_in, v_fox_f_bias, v_gdn_conv_w, v_gdn_a_log, v_gdn_dt_bias, v_gdn_norm, v_w_branch_fox, v_w_branch_gdn, v_w_out, v_norm_ffn, v_w_up, v_ffn_conv_w, v_w_down, v_norm_final):
    B, S, D = x.shape
    T = B * S
    H = D // HEAD
    N = S // CHUNK
    FF = w_down.shape[1] * N_CHIP
    d_in = 9 * D + 3 * H
    assert w_in.shape[2] * N_CHIP == d_in and 3 * H <= 128

    cidx = lax.axis_index("c").astype(jnp.int32)
    sidx = (2 * lax.axis_index("x") + lax.axis_index("y")).astype(jnp.int32)
    idx = jnp.stack([cidx, sidx])

    rowed = [w_branch_fox[0], w_branch_gdn[0], w_out[0], w_down[0]]
    convs = [gdn_conv_w[0], ffn_conv_w[0]]
    rowed_shapes = [a.shape for a in rowed]
    conv_shapes = [a.shape + (2,) for a in convs]
    pad_rows = lambda shapes: -(-sum(_rows_of(math.prod(s)) for s in shapes) // 256) * 128
    Rh, Rc = pad_rows(rowed_shapes), pad_rows(conv_shapes)
    halves = lambda a: a.reshape(2, a.shape[0] // 2, a.shape[1])
    packs_a = [halves(w_in[0].astype(BF16)),
               halves(_pack_rows([lax.bitcast_convert_type(a, BF16) for a in convs], 2 * Rc, BF16))]
    packs_b = [halves(w_up[0].astype(BF16)), halves(_pack_rows([a.astype(BF16) for a in rowed], 2 * Rh, BF16))]
    own = lambda gs, ps: [lax.dynamic_update_slice(g, p[None], (sidx, 0, 0, 0)) for g, p in zip(gs, ps)]
    by_cols = lambda g: g.transpose(1, 2, 0, 3).reshape(2 * g.shape[2], N_CHIP * g.shape[3])
    cat_cols = lambda p: jnp.concatenate([p[i] for i in range(N_CHIP)], axis=-1)
    cat_rows = lambda p: p.reshape(-1, p.shape[-1])
    g_in, g_conv = own(allgather_weights(packs_a), packs_a)
    W_in = by_cols(g_in)
    conv_parts = _unpack_rows(g_conv.reshape(N_CHIP, 2 * Rc, ROW), conv_shapes)
    gconv = cat_cols(lax.bitcast_convert_type(conv_parts[0], F32))
    fconv = cat_cols(lax.bitcast_convert_type(conv_parts[1], F32))

    o1, o2 = 3 * D, 3 * D + H
    o3, o4, o5, o6 = o2 + 3 * D, o2 + 3 * D + H, o2 + 3 * D + 2 * H, o2 + 4 * D + 2 * H
    W_fox = _interleave_heads(W_in[:, :o1], H)
    W_gqkv = _interleave_heads(W_in[:, o2:o3], H)
    W_gz = W_in[:, o5:o6]
    W_gates = W_in[:, o6:]
    W_small = jnp.concatenate([W_in[:, o1:o2], W_in[:, o3:o5], jnp.zeros((D, 128 - 3 * H), BF16)], axis=1)
    gconv_i = _interleave_heads(gconv, H)
    fconv_g, fconv_v = fconv[:, :FF], fconv[:, FF:]
    prm = jnp.zeros((8, 128), F32)
    prm = prm.at[0, 0:H].set(fox_f_bias[0]).at[0, H:2 * H].set(gdn_dt_bias[0]).at[1, H:2 * H].set(gdn_a_log[0])

    x2 = x.reshape(T, D)
    tgt = loss_target.reshape(T, D)

    hn1 = rmsnorm_fwd(x2, norm_mix, "rmsnorm_mix")
    p_fox = matmul(hn1, W_fox, "nn", "proj_fox", out_dtype=BF16)
    p_gqkv = matmul(hn1, W_gqkv, "nn", "proj_gqkv")
    p_gz = matmul(hn1, W_gz, "nn", "proj_gz")
    p_gates = matmul(hn1, W_gates, "nn", "proj_gates")
    p_small = matmul(hn1, W_small, "nn", "proj_small")

    sm = small_fwd(p_small, prm, B, S, H)
    heads = lambda a: a.reshape(B, S, H).transpose(0, 2, 1)
    c_bhs, gc_bhs, beta_bhs = heads(sm[:, 0:H]), heads(sm[:, H:2 * H]), heads(sm[:, 2 * H:3 * H])
    c_col, c_row = c_bhs[..., None], c_bhs[:, :, None, :]
    gcr5 = gc_bhs.reshape(B, H, N, 1, CHUNK)
    betar5 = beta_bhs.reshape(B, H, N, 1, CHUNK)

    (o_fox, o_fox16, lse), arriving = fox_fwd(p_fox, c_col, c_row, B, S, H, rider=_ride_gather_ici(packs_b))
    qkvn = gdn_prep_fwd(p_gqkv, gconv_i, B, S, H)
    (u_hat, w_t, t_inv), arrived = gdn_intra_fwd(qkvn, betar5, gcr5, B, S, H, rider=_ride_gather_d2d(arriving))
    g_up, g_rowed = own(arrived, packs_b)
    W_up = by_cols(g_up)
    W_up_g, W_up_v = W_up[:, :FF], W_up[:, FF:]
    W_bf, W_bg, W_out, W_down = (cat_rows(p) for p in _unpack_rows(g_rowed.reshape(N_CHIP, 2 * Rh, ROW), rowed_shapes))
    o_gdn, states = gdn_inter_fwd(qkvn, u_hat, w_t, gcr5, B, S, H)
    y_gdn = gdn_post_fwd(o_gdn, p_gz, gdn_norm, H)
    bf_ = matmul(o_fox16, W_bf, "nn", "branch_fox")
    bg_ = matmul(y_gdn, W_bg, "nn", "branch_gdn")
    y = merge_fwd(p_gates, bf_, bg_)
    h1 = matmul(y, W_out, "nn", "out_proj", add=x2)
    hn2 = rmsnorm_fwd(h1, norm_ffn, "rmsnorm_ffn")
    up_g = matmul(hn2, W_up_g, "nn", "up_gate")
    up_v = matmul(hn2, W_up_v, "nn", "up_val")
    act = ffn_gate_fwd(up_g, up_v, fconv_g, fconv_v, B, S)
    h2 = matmul(act, W_down, "nn", "down_proj", add=h1)
    loss_cols, dh2, dh2_16, d_norm_final = final_loss(h2, norm_final.reshape(1, D), tgt)
    loss = lax.psum(0.5 * jnp.sum(loss_cols) / D, ("x", "y", "c"))

    d_act = matmul(dh2_16, W_down, "nt", "d_act")
    dW_down = matmul(act, dh2_16, "tn", "dw_down")
    d_upg, d_upv, d_fconv_g, d_fconv_v = ffn_gate_bwd(up_g, up_v, fconv_g, fconv_v, d_act, B, S)
    d_hn2 = matmul(d_upg, W_up_g, "nt", "d_hn2_g")
    d_hn2 = matmul(d_upv, W_up_v, "nt", "d_hn2_v", add=d_hn2)
    dW_up = jnp.concatenate([matmul(hn2, d_upg, "tn", "dw_up_g"), matmul(hn2, d_upv, "tn", "dw_up_v")], axis=1)
    dh1, dh1_16, d_norm_ffn = rmsnorm_bwd(h1, norm_ffn, d_hn2, dh2, "rmsnorm_ffn_bwd")
    d_y = matmul(dh1_16, W_out, "nt", "d_y")
    dW_out = matmul(y, dh1_16, "tn", "dw_out")
    d_bf, d_bg, d_gates = merge_bwd(p_gates, bf_, bg_, d_y)
    d_ofox = matmul(d_bf, W_bf, "nt", "d_ofox")
    dW_bf = matmul(o_fox16, d_bf, "tn", "dw_bf")
    d_ygdn = matmul(d_bg, W_bg, "nt", "d_ygdn")
    dW_bg = matmul(y_gdn, d_bg, "tn", "dw_bg")

    d_fconv = jnp.concatenate([d_fconv_g, d_fconv_v], axis=1)
    col_shard = lambda g, s: g[:, s * (g.shape[1] // N_CHIP):(s + 1) * (g.shape[1] // N_CHIP)]
    row_shard = lambda g, s: g[s * (g.shape[0] // N_CHIP):(s + 1) * (g.shape[0] // N_CHIP)]
    shard_items = lambda s: [row_shard(dW_bf, s), row_shard(dW_bg, s), row_shard(dW_out, s), row_shard(dW_down, s),
                             col_shard(d_fconv, s)]
    g_shapes = [a.shape for a in shard_items(0)]
    assert sum(_rows_of(math.prod(s)) for s in g_shapes) <= 2 * Rh
    to_slabs = lambda g: g.reshape(2, g.shape[0] // 2, N_CHIP, g.shape[1] // N_CHIP).transpose(0, 2, 1, 3)
    gpacks_b = [to_slabs(dW_up),
                jnp.stack([_pack_rows(shard_items(s), 2 * Rh, F32).reshape(2, Rh, ROW) for s in range(N_CHIP)], axis=1)]
    (d_pfox, d_ccol, d_crow), gots_b = fox_bwd(p_fox, c_col, c_row, o_fox, lse, d_ofox, B, S, H,
                                              rider=_ride_exchange(gpacks_b))
    sums_b = [add_halves(g, got, idx, "add_halves_b%d" % i) for i, (g, got) in enumerate(zip(gpacks_b, gots_b))]

    d_ogdn, d_gz, d_gdn_norm = gdn_post_bwd(o_gdn, p_gz, gdn_norm, d_ygdn, H)
    (dq_i, dk_i, d_uh, d_wt, dgcr_a), got16_b = gdn_inter_bwd(qkvn, u_hat, w_t, gcr5, states, d_ogdn, B, S, H,
                                                             rider=_ride_scatter([s16 for _, s16 in sums_b]))
    mine_b = [add_chips(s32, g16, idx, "add_chips_b%d" % i) for i, ((s32, _), g16) in enumerate(zip(sums_b, got16_b))]
    d_qkvn, d_betar5, dgcr_b = gdn_intra_bwd(qkvn, betar5, gcr5, t_inv, d_uh, d_wt, dq_i, dk_i, B, S, H)
    d_pgqkv, d_gconv_i = gdn_prep_bwd(p_gqkv, gconv_i, d_qkvn, B, S, H)

    tokens = lambda a: a.reshape(B, H, S).transpose(0, 2, 1).reshape(T, H)
    d_gc = (dgcr_a + dgcr_b).reshape(B, H, S)
    d_sm = jnp.concatenate([tokens(d_ccol.reshape(B, H, S) + d_crow.reshape(B, H, S)), tokens(d_gc), tokens(d_betar5.reshape(B, H, S)),
                            jnp.zeros((T, 128 - 3 * H), F32)], axis=1)
    d_psmall, d_prm = small_bwd(p_small, prm, d_sm, B, S, H)

    d_hn1 = matmul(d_pfox, W_fox, "nt", "d_hn1_fox")
    d_hn1 = matmul(d_pgqkv, W_gqkv, "nt", "d_hn1_gqkv", add=d_hn1)
    d_hn1 = matmul(d_gz, W_gz, "nt", "d_hn1_gz", add=d_hn1)
    d_hn1 = matmul(d_gates, W_gates, "nt", "d_hn1_gates", add=d_hn1)
    d_hn1 = matmul(d_psmall, W_small, "nt", "d_hn1_small", add=d_hn1)
    dW_fox = matmul(hn1, d_pfox, "tn", "dw_fox")
    dW_gqkv = matmul(hn1, d_pgqkv, "tn", "dw_gqkv")
    dW_gz = matmul(hn1, d_gz, "tn", "dw_gz")
    dW_gates = matmul(hn1, d_gates, "tn", "dw_gates")
    dW_small = matmul(hn1, d_psmall, "tn", "dw_small")
    grad_x, _, d_norm_mix = rmsnorm_bwd(x2, norm_mix, d_hn1, dh1, "rmsnorm_mix_bwd")

    dW_in = jnp.concatenate([_deinterleave_heads(dW_fox, H), dW_small[:, 0:H], _deinterleave_heads(dW_gqkv, H),
                             dW_small[:, H:3 * H], dW_gz, dW_gates], axis=1)
    d_gconv = _deinterleave_heads(d_gconv_i, H)

    gpack_a = [to_slabs(dW_in)]
    gots_a = _run_alone(_ride_exchange(gpack_a), "exchange_halves")
    sums_a = [add_halves(gpack_a[0], gots_a[0], idx, "add_halves_a")]
    got16_a = _run_alone(_ride_scatter([sums_a[0][1]]), "scatter_chips")
    mine = [add_chips(sums_a[0][0], got16_a[0], idx, "add_chips_a")] + mine_b
    others = share_halves(mine)
    g_w_in, g_up, g_rows = (jnp.concatenate([jnp.where(cidx == 0, h, o), jnp.where(cidx == 0, o, h)], axis=0)
                            for h, o in zip(mine, others))
    g_bf, g_bg, g_out, g_down, g_fconv = _unpack_rows(g_rows, g_shapes)

    small_items = [d_norm_mix, d_norm_ffn, d_norm_final, d_gdn_norm, d_prm, d_gconv]
    small_shapes = [a.shape for a in small_items]
    sv = allreduce_small(_pack_rows(small_items, 0, F32, unit=8))
    g_norm_mix, g_norm_ffn, g_norm_final, g_gdn_norm, g_prm, g_gconv_all = _unpack_rows(sv, small_shapes, unit=8)
    g_norm_final = g_norm_final.reshape(D)
    g_fbias, g_dtb, g_alog = g_prm[0:1, 0:H], g_prm[0:1, H:2 * H], g_prm[1:2, H:2 * H]
    g_gconv = lax.dynamic_slice_in_dim(g_gconv_all, sidx * (3 * D // N_CHIP), 3 * D // N_CHIP, axis=1)

    names = ["norm_mix", "w_in", "fox_f_bias", "gdn_conv_w", "gdn_a_log", "gdn_dt_bias", "gdn_norm", "w_branch_fox",
             "w_branch_gdn", "w_out", "norm_ffn", "w_up", "ffn_conv_w", "w_down", "norm_final"]
    ws = [norm_mix, w_in, fox_f_bias, gdn_conv_w, gdn_a_log, gdn_dt_bias, gdn_norm, w_branch_fox, w_branch_gdn, w_out,
          norm_ffn, w_up, ffn_conv_w, w_down, norm_final]
    ms = [m_norm_mix, m_w_in, m_fox_f_bias, m_gdn_conv_w, m_gdn_a_log, m_gdn_dt_bias, m_gdn_norm, m_w_branch_fox,
          m_w_branch_gdn, m_w_out, m_norm_ffn, m_w_up, m_ffn_conv_w, m_w_down, m_norm_final]
    vs = [v_norm_mix, v_w_in, v_fox_f_bias, v_gdn_conv_w, v_gdn_a_log, v_gdn_dt_bias, v_gdn_norm, v_w_branch_fox,
          v_w_branch_gdn, v_w_out, v_norm_ffn, v_w_up, v_ffn_conv_w, v_w_down, v_norm_final]
    gs = [g_norm_mix, g_w_in, g_fbias, g_gconv, g_alog, g_dtb, g_gdn_norm, g_bf, g_bg, g_out, g_norm_ffn, g_up,
          g_fconv, g_down, g_norm_final]
    gs = [g.reshape(w.shape) for g, w in zip(gs, ws)]
    deltas, new_ms, new_vs = [], [], []
    for nm, w, g, m, v in zip(names, ws, gs, ms, vs):
        if w.ndim == 1:
            d, a, b = adamw(w.reshape(1, -1), g.reshape(1, -1), m.reshape(1, -1), v.reshape(1, -1), "adamw_" + nm)
            d, a, b = d.reshape(w.shape), a.reshape(w.shape), b.reshape(w.shape)
        else:
            d, a, b = adamw(w, g, m, v, "adamw_" + nm)
        deltas.append(d)
        new_ms.append(a)
        new_vs.append(b)

    return (loss, grad_x.reshape(B, S, D), *gs, *deltas, *new_ms, *new_vs)
```

```python
import functools
import math

import jax
import jax.numpy as jnp
from jax import lax
from jax.experimental import pallas as pl
from jax.experimental.pallas import tpu as pltpu

F32 = jnp.float32
BF16 = jnp.bfloat16
HEAD = 128
CHUNK = 64
GDN_CONV = 4
FFN_CONV = 3
EPS = 1e-6
NEG = -1e30
ROW = 1024
ATT_TILE = 512
MM_WEIGHT_TILE_BYTES = 8 << 20
N_CHIP = 4
N_DEV = 8
MESH = pl.DeviceIdType.MESH
HI = lax.Precision.HIGH
EXACT = lax.Precision.HIGHEST

ADAM_LR, ADAM_B1, ADAM_B2, ADAM_EPS, ADAM_WD, ADAM_STEP = 0.001, 0.9, 0.999, 1e-08, 0.01, 10


def _tile(n, cap, unit=128):
    best = None
    t = unit
    while t <= min(n, cap):
        if n % t == 0:
            best = t
        t += unit
    return best if best is not None else n


def _params(*sem):
    return pltpu.CompilerParams(dimension_semantics=sem)


_NN = (((1,), (0,)), ((), ()))
_NT = (((1,), (1,)), ((), ()))
_TN = (((0,), (0,)), ((), ()))


def _dg(a, b, dims, hi):
    if hi:
        return lax.dot_general(a, b, dims, precision=HI, preferred_element_type=F32)
    return lax.dot_general(a.astype(BF16), b.astype(BF16), dims, preferred_element_type=F32)


class _RawOps:
    @staticmethod
    def nn(a, b, hi=False):
        return _dg(a, b, _NN, hi)

    @staticmethod
    def nt(a, b, hi=False):
        return _dg(a, b, _NT, hi)

    @staticmethod
    def tn(a, b, hi=False):
        return _dg(a, b, _TN, hi)


def _make_diff_ops():
    def build(hi):
        @jax.custom_vjp
        def nn(a, b):
            return _dg(a, b, _NN, hi)

        nn.defvjp(lambda a, b: (_dg(a, b, _NN, hi), (a, b)),
                  lambda r, g: (_dg(g, r[1], _NT, hi), _dg(r[0], g, _TN, hi)))

        @jax.custom_vjp
        def nt(a, b):
            return _dg(a, b, _NT, hi)

        nt.defvjp(lambda a, b: (_dg(a, b, _NT, hi), (a, b)),
                  lambda r, g: (_dg(g, r[1], _NN, hi), _dg(g, r[0], _TN, hi)))

        @jax.custom_vjp
        def tn(a, b):
            return _dg(a, b, _TN, hi)

        tn.defvjp(lambda a, b: (_dg(a, b, _TN, hi), (a, b)),
                  lambda r, g: (_dg(r[1], g, _NT, hi), _dg(r[0], g, _NN, hi)))
        return nn, nt, tn

    lo, hi_ = build(False), build(True)

    class _DiffOps:
        @staticmethod
        def nn(a, b, hi=False):
            return (hi_ if hi else lo)[0](a, b)

        @staticmethod
        def nt(a, b, hi=False):
            return (hi_ if hi else lo)[1](a, b)

        @staticmethod
        def tn(a, b, hi=False):
            return (hi_ if hi else lo)[2](a, b)

    return _DiffOps


_DiffOps = _make_diff_ops()


def _sigmoid(x):
    return 1.0 / (1.0 + jnp.exp(-x))


def _mm_tile(n, pref):
    if n % pref == 0:
        return pref
    if n % 1408 == 0:
        return 1408
    return _tile(n, pref)


def matmul(a, b, mode, name, add=None, out_dtype=F32):
    if mode == "nn":
        (M, K), (K2, N) = a.shape, b.shape
    elif mode == "nt":
        (M, K), (N, K2) = a.shape, b.shape
    else:
        (K, M), (K2, N) = a.shape, b.shape
    assert K == K2, (name, a.shape, b.shape)
    tn = _mm_tile(N, 1024)
    if mode == "tn":
        tm = M if M <= 1408 else _mm_tile(M, 1408)
        tk = _mm_tile(K, 1024)
    else:
        tk = K if K * tn * 2 <= MM_WEIGHT_TILE_BYTES else _mm_tile(K, 1024)
        tm = _mm_tile(M, 1024 if tk <= 2048 else 512)
    nk = K // tk
    dims = {"nn": _NN, "nt": _NT, "tn": _TN}[mode]
    if mode == "tn":
        a_spec = pl.BlockSpec((tk, tm), lambda j, i, k: (k, i))
    else:
        a_spec = pl.BlockSpec((tm, tk), lambda j, i, k: (i, k))
    if mode == "nt":
        b_spec = pl.BlockSpec((tn, tk), lambda j, i, k: (j, k))
    else:
        b_spec = pl.BlockSpec((tk, tn), lambda j, i, k: (k, j))
    o_spec = pl.BlockSpec((tm, tn), lambda j, i, k: (i, j))
    has_add = add is not None

    def body(*refs):
        a_ref, b_ref = refs[:2]
        add_ref = refs[2] if has_add else None
        o_ref = refs[3] if has_add else refs[2]
        prod = lax.dot_general(a_ref[...].astype(BF16), b_ref[...].astype(BF16), dims, preferred_element_type=F32)

        def finish(r):
            if has_add:
                r = r + add_ref[...]
            o_ref[...] = r.astype(out_dtype)

        if nk == 1:
            finish(prod)
            return
        acc_ref = refs[-1]
        k = pl.program_id(2)

        @pl.when(k == 0)
        def _():
            acc_ref[...] = jnp.zeros_like(acc_ref)

        acc_ref[...] += prod

        @pl.when(k == nk - 1)
        def _():
            finish(acc_ref[...])

    in_specs = [a_spec, b_spec] + ([o_spec] if has_add else [])
    args = (a, b) + ((add,) if has_add else ())
    return pl.pallas_call(
        body, name=name, grid=(N // tn, M // tm, nk), in_specs=in_specs, out_specs=o_spec,
        out_shape=jax.ShapeDtypeStruct((M, N), out_dtype),
        scratch_shapes=[pltpu.VMEM((tm, tn), F32)] if nk > 1 else [],
        compiler_params=_params("parallel", "parallel", "arbitrary"),
    )(*args)


def rmsnorm_fwd(x, g, name):
    T, D = x.shape
    tm = _tile(T, 512, 8)

    def body(x_ref, g_ref, o_ref):
        xv = x_ref[...]
        r = lax.rsqrt(jnp.mean(xv * xv, axis=-1, keepdims=True) + EPS)
        o_ref[...] = (xv * r * g_ref[...]).astype(BF16)

    return pl.pallas_call(
        body, name=name, grid=(T // tm,),
        in_specs=[pl.BlockSpec((tm, D), lambda i: (i, 0)), pl.BlockSpec((1, D), lambda i: (0, 0))],
        out_specs=pl.BlockSpec((tm, D), lambda i: (i, 0)),
        out_shape=jax.ShapeDtypeStruct((T, D), BF16),
        compiler_params=_params("parallel"),
    )(x, g)


def rmsnorm_bwd(x, g, dy, dres, name):
    T, D = x.shape
    tm = _tile(T, 512, 8)

    def body(x_ref, g_ref, dy_ref, dres_ref, dx_ref, dx16_ref, dg_ref):
        @pl.when(pl.program_id(0) == 0)
        def _():
            dg_ref[...] = jnp.zeros_like(dg_ref)

        xv = x_ref[...]
        r = lax.rsqrt(jnp.mean(xv * xv, axis=-1, keepdims=True) + EPS)
        xh = xv * r
        dyv = dy_ref[...]
        dg_ref[...] += jnp.sum(dyv * xh, axis=0, keepdims=True)
        dxh = dyv * g_ref[...]
        dx = dres_ref[...] + r * (dxh - xh * jnp.mean(dxh * xh, axis=-1, keepdims=True))
        dx_ref[...] = dx
        dx16_ref[...] = dx.astype(BF16)

    row = pl.BlockSpec((tm, D), lambda i: (i, 0))
    vec = pl.BlockSpec((1, D), lambda i: (0, 0))
    return pl.pallas_call(
        body, name=name, grid=(T // tm,), in_specs=[row, vec, row, row], out_specs=[row, row, vec],
        out_shape=[jax.ShapeDtypeStruct((T, D), F32), jax.ShapeDtypeStruct((T, D), BF16),
                   jax.ShapeDtypeStruct((1, D), F32)],
        compiler_params=_params("arbitrary"),
    )(x, g, dy, dres)


def final_loss(h, g, target):
    T, D = h.shape
    tm = _tile(T, 512, 8)

    def body(h_ref, g_ref, t_ref, loss_ref, dh_ref, dh16_ref, dg_ref):
        @pl.when(pl.program_id(0) == 0)
        def _():
            loss_ref[...] = jnp.zeros_like(loss_ref)
            dg_ref[...] = jnp.zeros_like(dg_ref)

        hv = h_ref[...]
        r = lax.rsqrt(jnp.mean(hv * hv, axis=-1, keepdims=True) + EPS)
        xh = hv * r
        err = xh * g_ref[...] - t_ref[...]
        loss_ref[...] += jnp.sum(err * err, axis=0, keepdims=True)
        dy = err * (1.0 / D)
        dg_ref[...] += jnp.sum(dy * xh, axis=0, keepdims=True)
        dxh = dy * g_ref[...]
        dh = r * (dxh - xh * jnp.mean(dxh * xh, axis=-1, keepdims=True))
        dh_ref[...] = dh
        dh16_ref[...] = dh.astype(BF16)

    row = pl.BlockSpec((tm, D), lambda i: (i, 0))
    vec = pl.BlockSpec((1, D), lambda i: (0, 0))
    return pl.pallas_call(
        body, name="final_loss", grid=(T // tm,), in_specs=[row, vec, row], out_specs=[vec, row, row, vec],
        out_shape=[jax.ShapeDtypeStruct((1, D), F32), jax.ShapeDtypeStruct((T, D), F32),
                   jax.ShapeDtypeStruct((T, D), BF16), jax.ShapeDtypeStruct((1, D), F32)],
        compiler_params=_params("arbitrary"),
    )(h, g, target)


def _shift_down(x, k):
    if k == 0:
        return x
    rows = lax.broadcasted_iota(jnp.int32, x.shape, 0)
    return jnp.where(rows >= k, pltpu.roll(x, k, 0), 0.0)


def _shift_up(x, k):
    if k == 0:
        return x
    s = x.shape[0]
    rows = lax.broadcasted_iota(jnp.int32, x.shape, 0)
    return jnp.where(rows < s - k, pltpu.roll(x, s - k, 0), 0.0)


def _conv_fwd(x, w_ref, kw):
    y = x * w_ref[kw - 1:kw, :]
    for i in range(kw - 1):
        y = y + _shift_down(x, kw - 1 - i) * w_ref[i:i + 1, :]
    return y


def _conv_bwd(x, dy, w_ref, kw):
    dx = dy * w_ref[kw - 1:kw, :]
    dws = []
    for i in range(kw - 1):
        dx = dx + _shift_up(dy, kw - 1 - i) * w_ref[i:i + 1, :]
        dws.append(jnp.sum(dy * _shift_down(x, kw - 1 - i), axis=0, keepdims=True))
    dws.append(jnp.sum(dy * x, axis=0, keepdims=True))
    return dx, dws


def ffn_gate_fwd(up_g, up_v, cw_g, cw_v, B, S):
    T, Fd = up_g.shape
    tc = _tile(Fd, 256)

    def body(g_ref, v_ref, wg_ref, wv_ref, o_ref):
        ug = _conv_fwd(g_ref[...], wg_ref, FFN_CONV)
        uv = _conv_fwd(v_ref[...], wv_ref, FFN_CONV)
        o_ref[...] = (ug * _sigmoid(ug) * uv).astype(BF16)

    blk = pl.BlockSpec((S, tc), lambda b, j: (b, j))
    wblk = pl.BlockSpec((FFN_CONV, tc), lambda b, j: (0, j))
    return pl.pallas_call(
        body, name="ffn_gate_fwd", grid=(B, Fd // tc), in_specs=[blk, blk, wblk, wblk], out_specs=blk,
        out_shape=jax.ShapeDtypeStruct((T, Fd), BF16), compiler_params=_params("parallel", "parallel"),
    )(up_g, up_v, cw_g, cw_v)


def ffn_gate_bwd(up_g, up_v, cw_g, cw_v, d_act, B, S):
    T, Fd = up_g.shape
    tc = _tile(Fd, 256)

    def body(g_ref, v_ref, wg_ref, wv_ref, da_ref, dg_ref, dv_ref, dwg_ref, dwv_ref):
        @pl.when(pl.program_id(1) == 0)
        def _():
            dwg_ref[...] = jnp.zeros_like(dwg_ref)
            dwv_ref[...] = jnp.zeros_like(dwv_ref)

        xg, xv = g_ref[...], v_ref[...]
        ug = _conv_fwd(xg, wg_ref, FFN_CONV)
        uv = _conv_fwd(xv, wv_ref, FFN_CONV)
        da = da_ref[...]
        sg = _sigmoid(ug)
        d_ug = da * uv * (sg + ug * sg * (1.0 - sg))
        d_uv = da * ug * sg
        dxg, dwg = _conv_bwd(xg, d_ug, wg_ref, FFN_CONV)
        dxv, dwv = _conv_bwd(xv, d_uv, wv_ref, FFN_CONV)
        dg_ref[...] = dxg.astype(BF16)
        dv_ref[...] = dxv.astype(BF16)
        for i in range(FFN_CONV):
            dwg_ref[i:i + 1, :] += dwg[i]
            dwv_ref[i:i + 1, :] += dwv[i]

    blk = pl.BlockSpec((S, tc), lambda j, b: (b, j))
    wblk = pl.BlockSpec((FFN_CONV, tc), lambda j, b: (0, j))
    return pl.pallas_call(
        body, name="ffn_gate_bwd", grid=(Fd // tc, B), in_specs=[blk, blk, wblk, wblk, blk],
        out_specs=[blk, blk, wblk, wblk],
        out_shape=[jax.ShapeDtypeStruct((T, Fd), BF16), jax.ShapeDtypeStruct((T, Fd), BF16),
                   jax.ShapeDtypeStruct((FFN_CONV, Fd), F32), jax.ShapeDtypeStruct((FFN_CONV, Fd), F32)],
        compiler_params=_params("parallel", "arbitrary"),
    )(up_g, up_v, cw_g, cw_v, d_act)


def merge_fwd(p_gates, bf_, bg_):
    T, D = bf_.shape
    tm = _tile(T, 512, 8)

    def body(gf_ref, gg_ref, bf_ref, bg_ref, o_ref):
        o_ref[...] = (_sigmoid(gf_ref[...]) * bf_ref[...] + _sigmoid(gg_ref[...]) * bg_ref[...]).astype(BF16)

    lo = pl.BlockSpec((tm, D), lambda i: (i, 0))
    hi = pl.BlockSpec((tm, D), lambda i: (i, 1))
    return pl.pallas_call(
        body, name="merge_fwd", grid=(T // tm,), in_specs=[lo, hi, lo, lo], out_specs=lo,
        out_shape=jax.ShapeDtypeStruct((T, D), BF16), compiler_params=_params("parallel"),
    )(p_gates, p_gates, bf_, bg_)


def merge_bwd(p_gates, bf_, bg_, dy):
    T, D = bf_.shape
    tm = _tile(T, 512, 8)

    def body(gf_ref, gg_ref, bf_ref, bg_ref, dy_ref, dbf_ref, dbg_ref, dgate_ref):
        d = dy_ref[...]
        sf, sg = _sigmoid(gf_ref[...]), _sigmoid(gg_ref[...])
        dbf_ref[...] = (d * sf).astype(BF16)
        dbg_ref[...] = (d * sg).astype(BF16)
        dgate_ref[:, 0:D] = (d * bf_ref[...] * sf * (1.0 - sf)).astype(BF16)
        dgate_ref[:, D:2 * D] = (d * bg_ref[...] * sg * (1.0 - sg)).astype(BF16)

    lo = pl.BlockSpec((tm, D), lambda i: (i, 0))
    hi = pl.BlockSpec((tm, D), lambda i: (i, 1))
    both = pl.BlockSpec((tm, 2 * D), lambda i: (i, 0))
    return pl.pallas_call(
        body, name="merge_bwd", grid=(T // tm,), in_specs=[lo, hi, lo, lo, lo], out_specs=[lo, lo, both],
        out_shape=[jax.ShapeDtypeStruct((T, D), BF16), jax.ShapeDtypeStruct((T, D), BF16),
                   jax.ShapeDtypeStruct((T, 2 * D), BF16)],
        compiler_params=_params("parallel"),
    )(p_gates, p_gates, bf_, bg_, dy)


def fox_fwd(p_fox, c_col, c_row, B, S, H, rider=None):
    T = B * S
    t = _tile(S, ATT_TILE)
    nq = S // t
    scale = HEAD ** -0.5

    def body(q_ref, k_ref, v_ref, cq_ref, cr_ref, o_ref, o16_ref, lse_ref):
        i = pl.program_id(2)
        q = q_ref[...]
        cq = cq_ref[...]
        row = lax.broadcasted_iota(jnp.int32, (t, t), 0)
        col = lax.broadcasted_iota(jnp.int32, (t, t), 1)

        def step(j, carry, diagonal):
            m, l, acc = carry
            off = pl.multiple_of(j * t, t)
            k = k_ref[pl.ds(off, t), :]
            v = v_ref[pl.ds(off, t), :]
            s = lax.dot_general(q, k, _NT, preferred_element_type=F32) * scale + (cq - cr_ref[:, pl.ds(off, t)])
            if diagonal:
                s = jnp.where(col <= row, s, NEG)
            m_new = jnp.maximum(m, jnp.max(s, axis=-1, keepdims=True))
            alpha = jnp.exp(m - m_new)
            p = jnp.exp(s - m_new)
            l = alpha * l + jnp.sum(p, axis=-1, keepdims=True)
            acc = alpha * acc + lax.dot_general(p.astype(BF16), v, _NN, preferred_element_type=F32)
            return m_new, l, acc

        m0 = jnp.full((t, 1), NEG, F32)
        below = lax.fori_loop(0, i, functools.partial(step, diagonal=False),
                              (m0, jnp.zeros((t, 1), F32), jnp.zeros((t, HEAD), F32)))
        m, l, acc = step(i, below, diagonal=True)
        o = acc / l
        o_ref[...] = o
        o16_ref[...] = o.astype(BF16)
        lse_ref[...] = m + jnp.log(l)

    return _hosted_call(
        body, rider, name="fox_fwd", grid=(B, H, nq),
        in_specs=[pl.BlockSpec((t, HEAD), lambda b, h, i: (b * nq + i, 3 * h)),
                  pl.BlockSpec((S, HEAD), lambda b, h, i: (b, 3 * h + 1)),
                  pl.BlockSpec((S, HEAD), lambda b, h, i: (b, 3 * h + 2)),
                  pl.BlockSpec((None, None, t, 1), lambda b, h, i: (b, h, i, 0)),
                  pl.BlockSpec((None, None, 1, S), lambda b, h, i: (b, h, 0, 0))],
        out_specs=[pl.BlockSpec((t, HEAD), lambda b, h, i: (b * nq + i, h)),
                   pl.BlockSpec((t, HEAD), lambda b, h, i: (b * nq + i, h)),
                   pl.BlockSpec((None, None, t, 1), lambda b, h, i: (b, h, i, 0))],
        out_shape=[jax.ShapeDtypeStruct((T, H * HEAD), F32), jax.ShapeDtypeStruct((T, H * HEAD), BF16),
                   jax.ShapeDtypeStruct((B, H, S, 1), F32)],
        scratch_shapes=[], semantics=("parallel", "parallel", "arbitrary"),
    )(p_fox, p_fox, p_fox, c_col, c_row)


def fox_bwd(p_fox, c_col, c_row, o, lse, do, B, S, H, rider=None):
    T = B * S
    t = _tile(S, ATT_TILE)
    n = S // t
    scale = HEAD ** -0.5

    def body(q_ref, k_ref, v_ref, cq_ref, cr_ref, o_ref, lse_ref, do_ref, dqkv_ref, dcq_ref, dcr_ref, dq_acc, delta_s):
        row = lax.broadcasted_iota(jnp.int32, (t, t), 0)
        col = lax.broadcasted_iota(jnp.int32, (t, t), 1)

        def prep(i, c):
            rows = pl.ds(pl.multiple_of(i * t, t), t)
            delta_s[rows, :] = jnp.sum(do_ref[rows, :] * o_ref[rows, :], axis=-1, keepdims=True)
            dq_acc[rows, :] = jnp.zeros((t, HEAD), F32)
            dcq_ref[rows, :] = jnp.zeros((t, 1), F32)
            return c

        lax.fori_loop(0, n, prep, 0)

        def kv_step(j, c):
            joff = pl.multiple_of(j * t, t)
            k = k_ref[pl.ds(joff, t), :]
            v = v_ref[pl.ds(joff, t), :]
            crj = cr_ref[:, pl.ds(joff, t)]

            def q_step(i, carry, diagonal):
                dk, dv, dc = carry
                rows = pl.ds(pl.multiple_of(i * t, t), t)
                q = q_ref[rows, :]
                dob = do_ref[rows, :].astype(BF16)
                s = lax.dot_general(q, k, _NT, preferred_element_type=F32) * scale + (cq_ref[rows, :] - crj)
                if diagonal:
                    s = jnp.where(col <= row, s, NEG)
                p = jnp.exp(s - lse_ref[rows, :])
                dp = lax.dot_general(dob, v, _NT, preferred_element_type=F32)
                ds = p * (dp - delta_s[rows, :])
                dsb = ds.astype(BF16)
                dv = dv + lax.dot_general(p.astype(BF16), dob, _TN, preferred_element_type=F32)
                dk = dk + lax.dot_general(dsb, q, _TN, preferred_element_type=F32)
                dq_acc[rows, :] += lax.dot_general(dsb, k, _NN, preferred_element_type=F32) * scale
                dc = dc + jnp.sum(ds, axis=0, keepdims=True)
                dcq_ref[rows, :] += jnp.sum(ds, axis=-1, keepdims=True)
                return dk, dv, dc

            z = jnp.zeros((t, HEAD), F32)
            on_diagonal = q_step(j, (z, z, jnp.zeros((1, t), F32)), diagonal=True)
            dk, dv, dc = lax.fori_loop(j + 1, n, functools.partial(q_step, diagonal=False), on_diagonal)
            dqkv_ref[pl.ds(joff, t), HEAD:2 * HEAD] = (dk * scale).astype(BF16)
            dqkv_ref[pl.ds(joff, t), 2 * HEAD:3 * HEAD] = dv.astype(BF16)
            dcr_ref[:, pl.ds(joff, t)] = -dc
            return c

        lax.fori_loop(0, n, kv_step, 0)
        dqkv_ref[:, 0:HEAD] = dq_acc[...].astype(BF16)

    col_spec = pl.BlockSpec((None, None, S, 1), lambda b, h: (b, h, 0, 0))
    row_spec = pl.BlockSpec((None, None, 1, S), lambda b, h: (b, h, 0, 0))
    head = pl.BlockSpec((S, HEAD), lambda b, h: (b, h))
    return _hosted_call(
        body, rider, name="fox_bwd", grid=(B, H),
        in_specs=[pl.BlockSpec((S, HEAD), lambda b, h: (b, 3 * h)),
                  pl.BlockSpec((S, HEAD), lambda b, h: (b, 3 * h + 1)),
                  pl.BlockSpec((S, HEAD), lambda b, h: (b, 3 * h + 2)),
                  col_spec, row_spec, head, col_spec, head],
        out_specs=[pl.BlockSpec((S, 3 * HEAD), lambda b, h: (b, h)), col_spec, row_spec],
        out_shape=[jax.ShapeDtypeStruct((T, 3 * H * HEAD), BF16), jax.ShapeDtypeStruct((B, H, S, 1), F32),
                   jax.ShapeDtypeStruct((B, H, 1, S), F32)],
        scratch_shapes=[pltpu.VMEM((S, HEAD), F32), pltpu.VMEM((S, 1), F32)], semantics=("parallel", "parallel"),
    )(p_fox, p_fox, p_fox, c_col, c_row, o, lse, do)


def _small_fn(x, b0, b1, H):
    S = x.shape[0]
    lane = lax.broadcasted_iota(jnp.int32, x.shape, 1)
    z = x + b0
    tail = jnp.log1p(jnp.exp(-jnp.abs(z)))
    softplus = jnp.maximum(z, 0.0) + tail
    logsig = -(jnp.maximum(-z, 0.0) + tail)
    g = -jnp.exp(b1) * softplus
    pre = jnp.where(lane < H, logsig, jnp.where(lane < 2 * H, g, 0.0))
    bl = _tile(S, 256, CHUNK)
    r = lax.broadcasted_iota(jnp.int32, (bl, bl), 0)
    c = lax.broadcasted_iota(jnp.int32, (bl, bl), 1)
    tri = (r >= c).astype(F32)
    tri_chunk = jnp.where((r >= c) & (jnp.right_shift(r, 6) == jnp.right_shift(c, 6)), 1.0, 0.0)
    carry = jnp.zeros((1, x.shape[1]), F32)
    parts = []
    for i in range(S // bl):
        blk = pre[i * bl:(i + 1) * bl, :]
        full = lax.dot_general(tri, blk, _NN, precision=EXACT, preferred_element_type=F32) + carry
        chunked = lax.dot_general(tri_chunk, blk, _NN, precision=EXACT, preferred_element_type=F32)
        parts.append(jnp.where(lane[:bl] < H, full, chunked))
        carry = carry + jnp.sum(blk, axis=0, keepdims=True)
    cum = parts[0] if len(parts) == 1 else jnp.concatenate(parts, axis=0)
    return jnp.where(lane < 2 * H, cum, jnp.where(lane < 3 * H, _sigmoid(x), 0.0))


def small_fwd(p_small, prm, B, S, H):
    T = B * S

    def body(x_ref, p_ref, o_ref):
        o_ref[...] = _small_fn(x_ref[...], p_ref[0:1, :], p_ref[1:2, :], H)

    blk = pl.BlockSpec((S, 128), lambda b: (b, 0))
    return pl.pallas_call(
        body, name="small_fwd", grid=(B,), in_specs=[blk, pl.BlockSpec((8, 128), lambda b: (0, 0))], out_specs=blk,
        out_shape=jax.ShapeDtypeStruct((T, 128), F32), compiler_params=_params("parallel"),
    )(p_small, prm)


def small_bwd(p_small, prm, d_out, B, S, H):
    T = B * S

    def body(x_ref, p_ref, d_ref, dx_ref, dp_ref):
        @pl.when(pl.program_id(0) == 0)
        def _():
            dp_ref[...] = jnp.zeros_like(dp_ref)

        _, vjp = jax.vjp(functools.partial(_small_fn, H=H), x_ref[...], p_ref[0:1, :], p_ref[1:2, :])
        dx, db0, db1 = vjp(d_ref[...])
        dx_ref[...] = dx.astype(BF16)
        dp_ref[0:1, :] += db0
        dp_ref[1:2, :] += db1

    blk = pl.BlockSpec((S, 128), lambda b: (b, 0))
    pblk = pl.BlockSpec((8, 128), lambda b: (0, 0))
    return pl.pallas_call(
        body, name="small_bwd", grid=(B,), in_specs=[blk, pblk, blk], out_specs=[blk, pblk],
        out_shape=[jax.ShapeDtypeStruct((T, 128), BF16), jax.ShapeDtypeStruct((8, 128), F32)],
        compiler_params=_params("arbitrary"),
    )(p_small, prm, d_out)


def gdn_prep_fwd(p_gqkv, cw, B, S, H):
    T = B * S

    def body(x_ref, w_ref, o_ref):
        y = _conv_fwd(x_ref[...], w_ref, GDN_CONV)
        a = y * _sigmoid(y)
        rs = lax.rsqrt(jnp.sum(a * a, axis=-1, keepdims=True) + EPS)
        is_qk = (pl.program_id(1) % 3) < 2
        o_ref[...] = a * jnp.where(is_qk, rs, 1.0)

    blk = pl.BlockSpec((S, HEAD), lambda b, n: (b, n))
    wblk = pl.BlockSpec((GDN_CONV, HEAD), lambda b, n: (0, n))
    return pl.pallas_call(
        body, name="gdn_prep_fwd", grid=(B, 3 * H), in_specs=[blk, wblk], out_specs=blk,
        out_shape=jax.ShapeDtypeStruct((T, 3 * H * HEAD), F32), compiler_params=_params("parallel", "parallel"),
    )(p_gqkv, cw)


def gdn_prep_bwd(p_gqkv, cw, d_out, B, S, H):
    T = B * S

    def body(x_ref, w_ref, d_ref, dx_ref, dw_ref):
        @pl.when(pl.program_id(1) == 0)
        def _():
            dw_ref[...] = jnp.zeros_like(dw_ref)

        x = x_ref[...]
        y = _conv_fwd(x, w_ref, GDN_CONV)
        sg = _sigmoid(y)
        a = y * sg
        rs = lax.rsqrt(jnp.sum(a * a, axis=-1, keepdims=True) + EPS)
        d = d_ref[...]
        out = a * rs
        da_qk = rs * (d - out * jnp.sum(d * out, axis=-1, keepdims=True))
        is_qk = (pl.program_id(0) % 3) < 2
        da = jnp.where(is_qk, da_qk, d)
        dy = da * (sg + y * sg * (1.0 - sg))
        dx, dws = _conv_bwd(x, dy, w_ref, GDN_CONV)
        dx_ref[...] = dx.astype(BF16)
        for i in range(GDN_CONV):
            dw_ref[i:i + 1, :] += dws[i]

    blk = pl.BlockSpec((S, HEAD), lambda n, b: (b, n))
    wblk = pl.BlockSpec((GDN_CONV, HEAD), lambda n, b: (0, n))
    return pl.pallas_call(
        body, name="gdn_prep_bwd", grid=(3 * H, B), in_specs=[blk, wblk, blk], out_specs=[blk, wblk],
        out_shape=[jax.ShapeDtypeStruct((T, 3 * H * HEAD), BF16), jax.ShapeDtypeStruct((GDN_CONV, 3 * H * HEAD), F32)],
        compiler_params=_params("parallel", "arbitrary"),
    )(p_gqkv, cw, d_out)


@jax.custom_vjp
def _given_inverse(a, t):
    return t


def _given_inverse_fwd(a, t):
    return t, t


def _given_inverse_bwd(t, g):
    x = _dg(t, g, _TN, True)
    return -_dg(x, t, _NT, True), jnp.zeros_like(t)


_given_inverse.defvjp(_given_inverse_fwd, _given_inverse_bwd)


def _to_col(row):
    r = lax.broadcasted_iota(jnp.int32, (CHUNK, CHUNK), 0)
    c = lax.broadcasted_iota(jnp.int32, (CHUNK, CHUNK), 1)
    return jnp.sum(jnp.where(r == c, row, 0.0), axis=1, keepdims=True)


def _intra_fn(k, v, beta_r, gcr, ops, t_known=None):
    n = len(k)
    r = lax.broadcasted_iota(jnp.int32, (CHUNK, CHUNK), 0)
    c = lax.broadcasted_iota(jnp.int32, (CHUNK, CHUNK), 1)
    beta = [_to_col(beta_r[i]) for i in range(n)]
    gcc = [_to_col(gcr[i]) for i in range(n)]
    decay = [jnp.exp(jnp.where(r > c, gcc[i] - gcr[i], NEG)) for i in range(n)]
    kb = [k[i] * beta[i] for i in range(n)]
    a = [ops.nt(kb[i], k[i]) * decay[i] for i in range(n)]
    if t_known is None:
        p = [-a[i] for i in range(n)]
        tm = [jnp.where(r == c, 1.0, 0.0) + p[i] for i in range(n)]
        for _ in range(5):
            p = [ops.nn(p[i], p[i], hi=True) for i in range(n)]
            tm = [tm[i] + ops.nn(tm[i], p[i], hi=True) for i in range(n)]
    else:
        tm = [_given_inverse(a[i], t_known[i]) for i in range(n)]
    u_hat = [ops.nn(tm[i], v[i] * beta[i], hi=True) for i in range(n)]
    w = [ops.nn(tm[i], kb[i] * jnp.exp(gcc[i]), hi=True) for i in range(n)]
    return tuple(u_hat), tuple(w), tuple(tm)


INTRA_NB = 8


def gdn_intra_fwd(qkvn, betar5, gcr5, B, S, H, rider=None):
    T = B * S
    N = S // CHUNK
    nb = min(INTRA_NB, N)
    rows = nb * CHUNK
    ns = N // nb

    def body(k_ref, v_ref, b_ref, gr_ref, uh_ref, w_ref, t_ref):
        sls = [slice(ci * CHUNK, (ci + 1) * CHUNK) for ci in range(nb)]
        u_hat, w, tm = _intra_fn(tuple(k_ref[sl, :] for sl in sls), tuple(v_ref[sl, :] for sl in sls),
                                 tuple(b_ref[ci] for ci in range(nb)), tuple(gr_ref[ci] for ci in range(nb)), _RawOps)
        for ci, sl in enumerate(sls):
            uh_ref[sl, :] = u_hat[ci]
            w_ref[sl, :] = w[ci]
            t_ref[ci] = tm[ci]

    rowspec = pl.BlockSpec((None, None, nb, 1, CHUNK), lambda b, h, i: (b, h, i, 0, 0))
    sqspec = pl.BlockSpec((None, None, nb, CHUNK, CHUNK), lambda b, h, i: (b, h, i, 0, 0))
    out = pl.BlockSpec((rows, HEAD), lambda b, h, i: (b * ns + i, h))
    return _hosted_call(
        body, rider, name="gdn_intra_fwd", grid=(B, H, ns),
        in_specs=[pl.BlockSpec((rows, HEAD), lambda b, h, i: (b * ns + i, 3 * h + 1)),
                  pl.BlockSpec((rows, HEAD), lambda b, h, i: (b * ns + i, 3 * h + 2)),
                  rowspec, rowspec],
        out_specs=[out, out, sqspec],
        out_shape=[jax.ShapeDtypeStruct((T, H * HEAD), F32), jax.ShapeDtypeStruct((T, H * HEAD), F32),
                   jax.ShapeDtypeStruct((B, H, N, CHUNK, CHUNK), F32)],
        scratch_shapes=[], semantics=("parallel", "parallel", "parallel"),
    )(qkvn, qkvn, betar5, gcr5)


def gdn_intra_bwd(qkvn, betar5, gcr5, t_inv, d_uh, d_w, dq_in, dk_in, B, S, H):
    T = B * S
    N = S // CHUNK
    nb = min(INTRA_NB, N)
    rows = nb * CHUNK
    ns = N // nb

    def body(k_ref, v_ref, b_ref, gr_ref, t_ref, duh_ref, dw_ref, dq_ref, dk_ref, o_ref, db_ref, dgr_ref):
        sls = [slice(ci * CHUNK, (ci + 1) * CHUNK) for ci in range(nb)]
        chunks = range(nb)
        _, vjp = jax.vjp(
            functools.partial(_intra_fn, ops=_DiffOps, t_known=tuple(t_ref[ci] for ci in chunks)),
            tuple(k_ref[sl, :] for sl in sls), tuple(v_ref[sl, :] for sl in sls), tuple(b_ref[ci] for ci in chunks),
            tuple(gr_ref[ci] for ci in chunks))
        zero = jnp.zeros((CHUNK, CHUNK), F32)
        dk, dv, db, dgr = vjp((tuple(duh_ref[sl, :] for sl in sls), tuple(dw_ref[sl, :] for sl in sls),
                               tuple(zero for _ in chunks)))
        for ci, sl in enumerate(sls):
            o_ref[sl, 0:HEAD] = dq_ref[sl, :]
            o_ref[sl, HEAD:2 * HEAD] = dk[ci] + dk_ref[sl, :]
            o_ref[sl, 2 * HEAD:3 * HEAD] = dv[ci]
            db_ref[ci] = db[ci]
            dgr_ref[ci] = dgr[ci]

    rowspec = pl.BlockSpec((None, None, nb, 1, CHUNK), lambda b, h, i: (b, h, i, 0, 0))
    sqspec = pl.BlockSpec((None, None, nb, CHUNK, CHUNK), lambda b, h, i: (b, h, i, 0, 0))
    head = pl.BlockSpec((rows, HEAD), lambda b, h, i: (b * ns + i, h))
    return pl.pallas_call(
        body, name="gdn_intra_bwd", grid=(B, H, ns),
        in_specs=[pl.BlockSpec((rows, HEAD), lambda b, h, i: (b * ns + i, 3 * h + 1)),
                  pl.BlockSpec((rows, HEAD), lambda b, h, i: (b * ns + i, 3 * h + 2)),
                  rowspec, rowspec, sqspec, head, head, head, head],
        out_specs=[pl.BlockSpec((rows, 3 * HEAD), lambda b, h, i: (b * ns + i, h)), rowspec, rowspec],
        out_shape=[jax.ShapeDtypeStruct((T, 3 * H * HEAD), F32),
                   jax.ShapeDtypeStruct((B, H, N, 1, CHUNK), F32), jax.ShapeDtypeStruct((B, H, N, 1, CHUNK), F32)],
        compiler_params=_params("parallel", "parallel", "parallel"),
    )(qkvn, qkvn, betar5, gcr5, t_inv, d_uh, d_w, dq_in, dk_in)


def _inter_fn(q, k, u_hat, w, gcr, state, ops):
    n = len(q)
    r = lax.broadcasted_iota(jnp.int32, (CHUNK, CHUNK), 0)
    c = lax.broadcasted_iota(jnp.int32, (CHUNK, CHUNK), 1)
    last = lax.broadcasted_iota(jnp.int32, (1, CHUNK), 1) == CHUNK - 1
    gcc = [_to_col(gcr[i]) for i in range(n)]
    gl = [jnp.sum(jnp.where(last, gcr[i], 0.0), axis=1, keepdims=True) for i in range(n)]
    decay = [jnp.exp(jnp.where(r >= c, gcc[i] - gcr[i], NEG)) for i in range(n)]
    qs = [q[i] * (HEAD ** -0.5) for i in range(n)]
    ws = [ops.nn(w[i], state[i]) for i in range(n)]
    qst = [ops.nn(qs[i] * jnp.exp(gcc[i]), state[i]) for i in range(n)]
    attn = [ops.nt(qs[i], k[i]) * decay[i] for i in range(n)]
    u = [u_hat[i] - ws[i] for i in range(n)]
    o = [qst[i] + ops.nn(attn[i], u[i]) for i in range(n)]
    kdu = [ops.tn(k[i] * jnp.exp(gl[i] - gcc[i]), u[i]) for i in range(n)]
    new_state = [state[i] * jnp.exp(gl[i]) + kdu[i] for i in range(n)]
    return tuple(o), tuple(new_state)


INTER_HEADS = 4
INTER_ROWS = 512


def _inter_specs(ts, ns, hp, backward):
    at = (lambda s: ns - 1 - s) if backward else (lambda s: s)
    nc = ts // CHUNK
    qk = []
    for hh in range(hp):
        qk.append(pl.BlockSpec((ts, HEAD), lambda b, g, s, hh=hh: (b * ns + at(s), 3 * (hp * g + hh))))
        qk.append(pl.BlockSpec((ts, HEAD), lambda b, g, s, hh=hh: (b * ns + at(s), 3 * (hp * g + hh) + 1)))
    heads = pl.BlockSpec((ts, hp * HEAD), lambda b, g, s: (b * ns + at(s), g))
    rowspec = pl.BlockSpec((None, hp, nc, 1, CHUNK), lambda b, g, s: (b, g, at(s), 0, 0))
    stspec = pl.BlockSpec((None, hp, nc, HEAD, HEAD), lambda b, g, s: (b, g, at(s), 0, 0))
    return qk, heads, rowspec, stspec


def gdn_inter_fwd(qkvn, u_hat, w, gcr5, B, S, H):
    T = B * S
    N = S // CHUNK
    hp = INTER_HEADS if H % INTER_HEADS == 0 else 1
    hs = range(hp)
    ts = _tile(S, INTER_ROWS, CHUNK)
    ns, nc = S // ts, ts // CHUNK

    def body(*refs):
        qk_refs, (uh_ref, w_ref, gr_ref, o_ref, st_ref, s_scr) = refs[:2 * hp], refs[2 * hp:]

        @pl.when(pl.program_id(2) == 0)
        def _():
            s_scr[...] = jnp.zeros_like(s_scr)

        def step(n, c):
            rows = pl.ds(pl.multiple_of(n * CHUNK, CHUNK), CHUNK)
            st = tuple(s_scr[hh] for hh in hs)
            for hh in hs:
                st_ref[hh, n] = st[hh]
            o, new = _inter_fn(tuple(qk_refs[2 * hh][rows, :] for hh in hs), tuple(qk_refs[2 * hh + 1][rows, :] for hh in hs),
                               tuple(uh_ref[rows, hh * HEAD:(hh + 1) * HEAD] for hh in hs),
                               tuple(w_ref[rows, hh * HEAD:(hh + 1) * HEAD] for hh in hs),
                               tuple(gr_ref[hh, n] for hh in hs), st, _RawOps)
            for hh in hs:
                o_ref[rows, hh * HEAD:(hh + 1) * HEAD] = o[hh]
                s_scr[hh] = new[hh]
            return c

        lax.fori_loop(0, nc, step, 0)

    qk, heads, rowspec, stspec = _inter_specs(ts, ns, hp, backward=False)
    return pl.pallas_call(
        body, name="gdn_inter_fwd", grid=(B, H // hp, ns),
        in_specs=qk + [heads, heads, rowspec], out_specs=[heads, stspec],
        out_shape=[jax.ShapeDtypeStruct((T, H * HEAD), F32), jax.ShapeDtypeStruct((B, H, N, HEAD, HEAD), F32)],
        scratch_shapes=[pltpu.VMEM((hp, HEAD, HEAD), F32)],
        compiler_params=_params("parallel", "parallel", "arbitrary"),
    )(*([qkvn] * (2 * hp)), u_hat, w, gcr5)


def gdn_inter_bwd(qkvn, u_hat, w, gcr5, states, d_o, B, S, H, rider=None):
    T = B * S
    N = S // CHUNK
    hp = INTER_HEADS if H % INTER_HEADS == 0 else 1
    hs = range(hp)
    ts = _tile(S, INTER_ROWS, CHUNK)
    ns, nc = S // ts, ts // CHUNK

    def body(*refs):
        qk_refs = refs[:2 * hp]
        uh_ref, w_ref, gr_ref, st_ref, do_ref, dq_ref, dk_ref, duh_ref, dw_ref, dgr_ref, ds_scr = refs[2 * hp:]

        @pl.when(pl.program_id(2) == 0)
        def _():
            ds_scr[...] = jnp.zeros_like(ds_scr)

        cols = [slice(hh * HEAD, (hh + 1) * HEAD) for hh in hs]

        def step(i, c):
            n = nc - 1 - i
            rows = pl.ds(pl.multiple_of(n * CHUNK, CHUNK), CHUNK)
            _, vjp = jax.vjp(functools.partial(_inter_fn, ops=_DiffOps),
                             tuple(qk_refs[2 * hh][rows, :] for hh in hs), tuple(qk_refs[2 * hh + 1][rows, :] for hh in hs),
                             tuple(uh_ref[rows, cols[hh]] for hh in hs), tuple(w_ref[rows, cols[hh]] for hh in hs),
                             tuple(gr_ref[hh, n] for hh in hs), tuple(st_ref[hh, n] for hh in hs))
            dq, dk, duh, dw, dgr, ds = vjp((tuple(do_ref[rows, cols[hh]] for hh in hs), tuple(ds_scr[hh] for hh in hs)))
            for hh in hs:
                dq_ref[rows, cols[hh]] = dq[hh]
                dk_ref[rows, cols[hh]] = dk[hh]
                duh_ref[rows, cols[hh]] = duh[hh]
                dw_ref[rows, cols[hh]] = dw[hh]
                dgr_ref[hh, n] = dgr[hh]
                ds_scr[hh] = ds[hh]
            return c

        lax.fori_loop(0, nc, step, 0)

    qk, heads, rowspec, stspec = _inter_specs(ts, ns, hp, backward=True)
    hshape = jax.ShapeDtypeStruct((T, H * HEAD), F32)
    return _hosted_call(
        body, rider, name="gdn_inter_bwd", grid=(B, H // hp, ns),
        in_specs=qk + [heads, heads, rowspec, stspec, heads],
        out_specs=[heads, heads, heads, heads, rowspec],
        out_shape=[hshape, hshape, hshape, hshape, jax.ShapeDtypeStruct((B, H, N, 1, CHUNK), F32)],
        scratch_shapes=[pltpu.VMEM((hp, HEAD, HEAD), F32)], semantics=("parallel", "parallel", "arbitrary"),
    )(*([qkvn] * (2 * hp)), u_hat, w, gcr5, states, d_o)


def gdn_post_fwd(o, p_gz, g, H):
    T = o.shape[0]
    tm = _tile(T, 1024, 8)

    def body(o_ref, z_ref, g_ref, y_ref):
        ov, z = o_ref[...], z_ref[...]
        r = lax.rsqrt(jnp.mean(ov * ov, axis=-1, keepdims=True) + EPS)
        y_ref[...] = (ov * r * g_ref[...] * z * _sigmoid(z)).astype(BF16)

    blk = pl.BlockSpec((tm, HEAD), lambda i, h: (i, h))
    return pl.pallas_call(
        body, name="gdn_post_fwd", grid=(T // tm, H), in_specs=[blk, blk, pl.BlockSpec((1, HEAD), lambda i, h: (0, 0))],
        out_specs=blk, out_shape=jax.ShapeDtypeStruct((T, H * HEAD), BF16),
        compiler_params=_params("parallel", "parallel"),
    )(o, p_gz, g)


def gdn_post_bwd(o, p_gz, g, dy, H):
    T = o.shape[0]
    tm = _tile(T, 1024, 8)

    def body(o_ref, z_ref, g_ref, dy_ref, do_ref, dz_ref, dg_ref):
        @pl.when((pl.program_id(0) == 0) & (pl.program_id(1) == 0))
        def _():
            dg_ref[...] = jnp.zeros_like(dg_ref)

        ov, z, d = o_ref[...], z_ref[...], dy_ref[...]
        r = lax.rsqrt(jnp.mean(ov * ov, axis=-1, keepdims=True) + EPS)
        xh = ov * r
        sg = _sigmoid(z)
        sz = z * sg
        d_n = d * sz
        dz_ref[...] = (d * xh * g_ref[...] * (sg + z * sg * (1.0 - sg))).astype(BF16)
        dg_ref[...] += jnp.sum(d_n * xh, axis=0, keepdims=True)
        dxh = d_n * g_ref[...]
        do_ref[...] = r * (dxh - xh * jnp.mean(dxh * xh, axis=-1, keepdims=True))

    blk = pl.BlockSpec((tm, HEAD), lambda i, h: (i, h))
    vec = pl.BlockSpec((1, HEAD), lambda i, h: (0, 0))
    return pl.pallas_call(
        body, name="gdn_post_bwd", grid=(T // tm, H), in_specs=[blk, blk, vec, blk], out_specs=[blk, blk, vec],
        out_shape=[jax.ShapeDtypeStruct((T, H * HEAD), F32), jax.ShapeDtypeStruct((T, H * HEAD), BF16),
                   jax.ShapeDtypeStruct((1, HEAD), F32)],
        compiler_params=_params("arbitrary", "arbitrary"),
    )(o, p_gz, g, dy)


def adamw(w, g, m, v, name):
    shape = w.shape
    lead = (None,) * (w.ndim - 2)
    zeros = (0,) * (w.ndim - 2)
    R, C = shape[-2:]
    g2 = g.reshape(R, C)
    tr, tc = _tile(R, 128, 8), C
    if tr % 8 and R > 8:
        tr, tc = R, _tile(C, 128)

    def body(w_ref, g_ref, m_ref, v_ref, d_ref, nm_ref, nv_ref):
        gv = g_ref[...]
        nm = ADAM_B1 * m_ref[...] + (1.0 - ADAM_B1) * gv
        nv = ADAM_B2 * v_ref[...] + (1.0 - ADAM_B2) * (gv * gv)
        m_hat = nm / (1.0 - ADAM_B1 ** ADAM_STEP)
        v_hat = nv / (1.0 - ADAM_B2 ** ADAM_STEP)
        d_ref[...] = -ADAM_LR * (m_hat / (jnp.sqrt(v_hat) + ADAM_EPS) + ADAM_WD * w_ref[...])
        nm_ref[...] = nm
        nv_ref[...] = nv

    blk = pl.BlockSpec(lead + (tr, tc), lambda i, j: zeros + (i, j))
    gblk = pl.BlockSpec((tr, tc), lambda i, j: (i, j))
    sh = jax.ShapeDtypeStruct(shape, F32)
    return pl.pallas_call(
        body, name=name, grid=(R // tr, C // tc), in_specs=[blk, gblk, blk, blk], out_specs=[blk] * 3, out_shape=[sh] * 3,
        compiler_params=_params("parallel", "parallel"),
    )(w, g2, m, v)


def _place():
    x, y, c = lax.axis_index("x"), lax.axis_index("y"), lax.axis_index("c")
    chips = [(1 - x, y), (x, 1 - y), (1 - x, 1 - y)]
    return x, y, c, chips


_HBM = pl.BlockSpec(memory_space=pltpu.HBM)


def allgather_weights(packs):
    n = len(packs)

    def body(*refs):
        in_refs, out_refs, (send_sems, recv_sems) = refs[:n], refs[n:2 * n], refs[2 * n:]
        x, y, c, chips = _place()
        me_s = 2 * x + y
        me, sibling = (x, y, c), (x, y, 1 - c)
        shards = [2 * chip[0] + chip[1] for chip in chips]

        def copy(a, k, shard, half, to, src=None):
            dst = out_refs[a].at[shard, half]
            return pltpu.make_async_remote_copy(src_ref=dst if src is None else src, dst_ref=dst,
                                                send_sem=send_sems.at[6 * a + k], recv_sem=recv_sems.at[6 * a + k],
                                                device_id=to, device_id_type=MESH)

        first = [copy(a, j, me_s, c, (*chip, c), src=in_refs[a].at[c]) for a in range(n) for j, chip in enumerate(chips)]
        for cp in first:
            cp.start()
        passed = []
        for a in range(n):
            for j in range(3):
                copy(a, j, shards[j], c, me).wait_recv()
                passed.append(copy(a, 3 + j, shards[j], c, sibling))
                passed[-1].start()
        for a in range(n):
            for j in range(3):
                copy(a, 3 + j, shards[j], 1 - c, me).wait_recv()
        for cp in first + passed:
            cp.wait_send()

    return pl.pallas_call(
        body, name="allgather_weights", in_specs=[_HBM] * n, out_specs=[_HBM] * n,
        out_shape=[jax.ShapeDtypeStruct((N_CHIP,) + p.shape, p.dtype) for p in packs],
        scratch_shapes=[pltpu.SemaphoreType.DMA((6 * n,)), pltpu.SemaphoreType.DMA((6 * n,))],
    )(*packs)


class _Rider:
    def __init__(self, inputs, out_shapes, n_sems, sends, recvs, aliases=None):
        self.inputs, self.out_shapes, self.n_sems = list(inputs), list(out_shapes), n_sems
        self.sends, self.recvs, self.aliases = sends, recvs, aliases or {}

    def start(self, *refs):
        for cp in self.sends(*refs):
            cp.start()

    def wait(self, *refs):
        for cp in self.recvs(*refs):
            cp.wait_recv()
        for cp in self.sends(*refs):
            cp.wait_send()


def _remote(src, dst, send_sems, recv_sems, k, to):
    return pltpu.make_async_remote_copy(src_ref=src, dst_ref=dst, send_sem=send_sems.at[k], recv_sem=recv_sems.at[k],
                                        device_id=to, device_id_type=MESH)


def _run_alone(rider, name):
    ri = len(rider.inputs)

    def body(*refs):
        ins, outs, (send_sems, recv_sems) = refs[:ri], refs[ri:-2], refs[-2:]
        rider.start(ins, outs, send_sems, recv_sems)
        rider.wait(ins, outs, send_sems, recv_sems)

    return pl.pallas_call(
        body, name=name, in_specs=[_HBM] * ri, out_specs=[_HBM] * len(rider.out_shapes), out_shape=rider.out_shapes,
        scratch_shapes=[pltpu.SemaphoreType.DMA((rider.n_sems,))] * 2, input_output_aliases=rider.aliases,
    )(*rider.inputs)


def _hosted_call(body, rider, *, name, grid, in_specs, out_specs, out_shape, scratch_shapes, semantics):
    if rider is None:
        return pl.pallas_call(body, name=name, grid=grid, in_specs=in_specs, out_specs=out_specs, out_shape=out_shape,
                              scratch_shapes=scratch_shapes, compiler_params=_params(*semantics))
    n_in, n_out, n_scr = len(in_specs), len(out_specs), len(scratch_shapes)
    ri, ro = len(rider.inputs), len(rider.out_shapes)

    def hosted(*refs):
        parts, p = [], 0
        for cnt in (n_in, ri, n_out, ro, n_scr, 2):
            parts.append(refs[p:p + cnt])
            p += cnt
        ins, rins, outs, routs, scr, (send_sems, recv_sems) = parts
        first = functools.reduce(jnp.logical_and, [pl.program_id(a) == 0 for a in range(len(grid))])
        last = functools.reduce(jnp.logical_and, [pl.program_id(a) == grid[a] - 1 for a in range(len(grid))])

        @pl.when(first)
        def _():
            rider.start(rins, routs, send_sems, recv_sems)

        body(*ins, *outs, *scr)

        @pl.when(last)
        def _():
            rider.wait(rins, routs, send_sems, recv_sems)

    call = pl.pallas_call(
        hosted, name=name, grid=grid, in_specs=list(in_specs) + [_HBM] * ri, out_specs=list(out_specs) + [_HBM] * ro,
        out_shape=list(out_shape) + rider.out_shapes,
        scratch_shapes=list(scratch_shapes) + [pltpu.SemaphoreType.DMA((rider.n_sems,))] * 2,
        input_output_aliases={n_in + i: n_out + o for i, o in rider.aliases.items()},
        compiler_params=_params(*(("arbitrary",) * len(grid))))

    def run(*args):
        res = call(*args, *rider.inputs)
        return res[:n_out], res[n_out:]

    return run


def _ride_gather_ici(packs):
    n = len(packs)

    def sends(ins, outs, send_sems, recv_sems):
        x, y, c, chips = _place()
        return [_remote(ins[a].at[c], outs[a].at[2 * x + y, c], send_sems, recv_sems, 3 * a + j, (*chip, c))
                for a in range(n) for j, chip in enumerate(chips)]

    def recvs(ins, outs, send_sems, recv_sems):
        x, y, c, chips = _place()
        return [_remote(ins[a].at[c], outs[a].at[2 * chip[0] + chip[1], c], send_sems, recv_sems, 3 * a + j, (x, y, c))
                for a in range(n) for j, chip in enumerate(chips)]

    return _Rider(packs, [jax.ShapeDtypeStruct((N_CHIP,) + p.shape, p.dtype) for p in packs], 3 * n, sends, recvs)


def _ride_gather_d2d(gathered):
    n = len(gathered)

    def copies(landing_half, to):
        def build(ins, outs, send_sems, recv_sems):
            x, y, c, chips = _place()
            return [_remote(ins[a].at[2 * chip[0] + chip[1], c], outs[a].at[2 * chip[0] + chip[1], landing_half(c)],
                            send_sems, recv_sems, 3 * a + j, to(x, y, c))
                    for a in range(n) for j, chip in enumerate(chips)]
        return build

    return _Rider(gathered, [jax.ShapeDtypeStruct(g.shape, g.dtype) for g in gathered], 3 * n,
                  copies(lambda c: c, lambda x, y, c: (x, y, 1 - c)), copies(lambda c: 1 - c, lambda x, y, c: (x, y, c)),
                  aliases={a: a for a in range(n)})


def _ride_exchange(gs):
    n = len(gs)

    def copies(ins, outs, send_sems, recv_sems):
        x, y, c, _ = _place()
        return [_remote(ins[a].at[1 - c], outs[a], send_sems, recv_sems, a, (x, y, 1 - c)) for a in range(n)]

    return _Rider(gs, [jax.ShapeDtypeStruct(g.shape[1:], g.dtype) for g in gs], n, copies, copies)


def _ride_scatter(b16s):
    n = len(b16s)

    def sends(ins, outs, send_sems, recv_sems):
        x, y, c, chips = _place()
        return [_remote(ins[a].at[2 * chip[0] + chip[1]], outs[a].at[2 * x + y], send_sems, recv_sems, 3 * a + j, (*chip, c))
                for a in range(n) for j, chip in enumerate(chips)]

    def recvs(ins, outs, send_sems, recv_sems):
        x, y, c, chips = _place()
        return [_remote(ins[a].at[2 * x + y], outs[a].at[2 * chip[0] + chip[1]], send_sems, recv_sems, 3 * a + j, (x, y, c))
                for a in range(n) for j, chip in enumerate(chips)]

    return _Rider(b16s, [jax.ShapeDtypeStruct(b.shape, b.dtype) for b in b16s], 3 * n, sends, recvs)


def _slab_tile(r, cols):
    tr = _tile(r, 256, 16)
    if tr % 16 == 0:
        return tr, cols
    return r, _tile(cols, 128)


def add_halves(g, got, idx, name):
    _, ns, r, cols = g.shape
    tr, tc = _slab_tile(r, cols)

    def body(idx_ref, a_ref, b_ref, o32_ref, o16_ref):
        s = a_ref[...] + b_ref[...]
        o32_ref[...] = s
        o16_ref[...] = s.astype(BF16)

    blk = pl.BlockSpec((None, tr, tc), lambda s, i, j, idx_ref: (s, i, j))
    return pl.pallas_call(
        body, name=name,
        grid_spec=pltpu.PrefetchScalarGridSpec(
            num_scalar_prefetch=1, grid=(ns, r // tr, cols // tc),
            in_specs=[pl.BlockSpec((None, None, tr, tc), lambda s, i, j, idx_ref: (idx_ref[0], s, i, j)), blk],
            out_specs=[blk, blk]),
        out_shape=[jax.ShapeDtypeStruct((ns, r, cols), F32), jax.ShapeDtypeStruct((ns, r, cols), BF16)],
        compiler_params=_params("parallel", "parallel", "parallel"),
    )(idx, g, got)


def add_chips(a32, got16, idx, name):
    ns, r, cols = a32.shape
    tr, tc = _slab_tile(r, cols)

    def body(idx_ref, a_ref, r1_ref, r2_ref, r3_ref, o_ref):
        o_ref[...] = ((a_ref[...] + r1_ref[...].astype(F32)) + r2_ref[...].astype(F32)) + r3_ref[...].astype(F32)

    def slab(k):
        return pl.BlockSpec((None, tr, tc), lambda i, j, idx_ref: ((idx_ref[1] + k) % ns, i, j))

    return pl.pallas_call(
        body, name=name,
        grid_spec=pltpu.PrefetchScalarGridSpec(
            num_scalar_prefetch=1, grid=(r // tr, cols // tc), in_specs=[slab(0), slab(1), slab(2), slab(3)],
            out_specs=pl.BlockSpec((tr, tc), lambda i, j, idx_ref: (i, j))),
        out_shape=jax.ShapeDtypeStruct((r, cols), F32),
        compiler_params=_params("parallel", "parallel"),
    )(idx, a32, got16, got16, got16)


def share_halves(halves):
    n = len(halves)

    def body(*refs):
        in_refs, out_refs, (send_sems, recv_sems) = refs[:n], refs[n:2 * n], refs[2 * n:]
        x, y, c, _ = _place()
        cps = [pltpu.make_async_remote_copy(src_ref=in_refs[a], dst_ref=out_refs[a], send_sem=send_sems.at[a],
                                            recv_sem=recv_sems.at[a], device_id=(x, y, 1 - c), device_id_type=MESH)
               for a in range(n)]
        for cp in cps:
            cp.start()
        for cp in cps:
            cp.wait()

    return pl.pallas_call(
        body, name="share_halves", in_specs=[_HBM] * n, out_specs=[_HBM] * n,
        out_shape=[jax.ShapeDtypeStruct(h.shape, F32) for h in halves],
        scratch_shapes=[pltpu.SemaphoreType.DMA((n,)), pltpu.SemaphoreType.DMA((n,))],
    )(*halves)


def allreduce_small(v):
    R, _ = v.shape

    def body(in_ref, out_ref, slots, send_sems, recv_sems):
        x, y, c, _ = _place()
        me = 4 * x + 2 * y + c
        slots[me] = in_ref[...]
        cps = []
        for k in range(1, N_DEV):
            to = (x ^ (k >> 2), y ^ ((k >> 1) & 1), c ^ (k & 1))
            cps.append(pltpu.make_async_remote_copy(src_ref=in_ref, dst_ref=slots.at[me], send_sem=send_sems.at[k - 1],
                                                    recv_sem=recv_sems.at[k - 1], device_id=to, device_id_type=MESH))
        for cp in cps:
            cp.start()
        for k in range(1, N_DEV):
            frm = 4 * (x ^ (k >> 2)) + 2 * (y ^ ((k >> 1) & 1)) + (c ^ (k & 1))
            pltpu.make_async_remote_copy(src_ref=in_ref, dst_ref=slots.at[frm], send_sem=send_sems.at[k - 1],
                                         recv_sem=recv_sems.at[k - 1], device_id=(x, y, c), device_id_type=MESH).wait_recv()
        for cp in cps:
            cp.wait_send()
        acc = slots[0]
        for d in range(1, N_DEV):
            acc = acc + slots[d]
        out_ref[...] = acc

    vm = pl.BlockSpec(memory_space=pltpu.VMEM)
    return pl.pallas_call(
        body, name="allreduce_small", in_specs=[vm], out_specs=vm, out_shape=jax.ShapeDtypeStruct((R, ROW), F32),
        scratch_shapes=[pltpu.VMEM((N_DEV, R, ROW), F32), pltpu.SemaphoreType.DMA((N_DEV - 1,)),
                        pltpu.SemaphoreType.DMA((N_DEV - 1,))],
    )(v)


def _rows_of(n, unit=16):
    return -(-n // (unit * ROW)) * unit


def _pack_rows(items, total_rows, dtype, unit=16):
    parts = []
    used = 0
    for a in items:
        flat = a.reshape(-1)
        r = _rows_of(flat.shape[0], unit)
        flat = jnp.pad(flat, (0, r * ROW - flat.shape[0]))
        parts.append(flat.reshape(r, ROW))
        used += r
    if total_rows > used:
        parts.append(jnp.zeros((total_rows - used, ROW), dtype))
    return jnp.concatenate(parts, axis=0)


def _unpack_rows(buf, shapes, unit=16):
    lead = buf.shape[:-2]
    out = []
    off = 0
    for shp in shapes:
        n = math.prod(shp)
        r = _rows_of(n, unit)
        piece = buf[..., off:off + r, :].reshape(*lead, r * ROW)[..., :n].reshape(*lead, *shp)
        out.append(piece)
        off += r
    return out


def _interleave_heads(w, H):
    lead = w.shape[:-1]
    return w.reshape(*lead, 3, H, HEAD).swapaxes(-3, -2).reshape(*lead, 3 * H * HEAD)


def _deinterleave_heads(w, H):
    lead = w.shape[:-1]
    return w.reshape(*lead, H, 3, HEAD).swapaxes(-3, -2).reshape(*lead, 3 * H * HEAD)


def _interleave_head_rows(w, H):
    return w.reshape(3, H, HEAD, w.shape[-1]).swapaxes(0, 1).reshape(3 * H * HEAD, w.shape[-1])


def _deinterleave_head_rows(w, H):
    return w.reshape(H, 3, HEAD, w.shape[-1]).swapaxes(0, 1).reshape(3 * H * HEAD, w.shape[-1])


def kernel(x, norm_mix, w_in, fox_f_bias, gdn_conv_w, gdn_a_log, gdn_dt_bias, gdn_norm, w_branch_fox, w_branch_gdn, w_out, norm_ffn, w_up, ffn_conv_w, w_down, norm_final, loss_target, m_norm_mix, m_w_in, m_fox_f_bias, m_gdn_conv_w, m_gdn_a_log, m_gdn_dt_bias, m_gdn_norm, m_w_branch_fox, m_w_branch_gdn, m_w_out, m_norm_ffn, m_w_up, m_ffn_conv_w, m_w_down, m_norm_final, v_norm_mix, v_w_in, v_fox_f_bias, v_gdn_conv_w, v_gdn_a_log, v_gdn_dt_bias, v_gdn_norm, v_w_branch_fox, v_w_branch_gdn, v_w_out, v_norm_ffn, v_w_up, v_ffn_conv_w, v_w_down, v_norm_final):
    B, S, D = x.shape
    T = B * S
    H = D // HEAD
    N = S // CHUNK
    FF = w_down.shape[1] * N_CHIP
    d_in = 9 * D + 3 * H
    assert w_in.shape[2] * N_CHIP == d_in and 3 * H <= 128

    cidx = lax.axis_index("c").astype(jnp.int32)
    sidx = (2 * lax.axis_index("x") + lax.axis_index("y")).astype(jnp.int32)
    idx = jnp.stack([cidx, sidx])

    rowed = [w_branch_fox[0], w_branch_gdn[0], w_out[0], w_down[0]]
    convs = [gdn_conv_w[0], ffn_conv_w[0]]
    rowed_shapes = [a.shape for a in rowed]
    conv_shapes = [a.shape + (2,) for a in convs]
    pad_rows = lambda shapes: -(-sum(_rows_of(math.prod(s)) for s in shapes) // 256) * 128
    Rh, Rc = pad_rows(rowed_shapes), pad_rows(conv_shapes)
    halves = lambda a: a.reshape(2, a.shape[0] // 2, a.shape[1])
    c_in = w_in.shape[2]
    packs_a = [w_in[0].T.astype(BF16).reshape(c_in, 2, D // 2).transpose(1, 0, 2),
               halves(_pack_rows([lax.bitcast_convert_type(a, BF16) for a in convs], 2 * Rc, BF16))]
    packs_b = [halves(w_up[0].astype(BF16)), halves(_pack_rows([a.astype(BF16) for a in rowed], 2 * Rh, BF16))]
    own = lambda gs, ps: [lax.dynamic_update_slice(g, p[None], (sidx, 0, 0, 0)) for g, p in zip(gs, ps)]
    by_cols = lambda g: g.transpose(1, 2, 0, 3).reshape(2 * g.shape[2], N_CHIP * g.shape[3])
    cat_cols = lambda p: jnp.concatenate([p[i] for i in range(N_CHIP)], axis=-1)
    cat_rows = lambda p: p.reshape(-1, p.shape[-1])
    g_in, g_conv = own(allgather_weights(packs_a), packs_a)
    W_inT = g_in.transpose(0, 2, 1, 3).reshape(N_CHIP * c_in, D)
    conv_parts = _unpack_rows(g_conv.reshape(N_CHIP, 2 * Rc, ROW), conv_shapes)
    gconv = cat_cols(lax.bitcast_convert_type(conv_parts[0], F32))
    fconv = cat_cols(lax.bitcast_convert_type(conv_parts[1], F32))

    o1, o2 = 3 * D, 3 * D + H
    o3, o4, o5, o6 = o2 + 3 * D, o2 + 3 * D + H, o2 + 3 * D + 2 * H, o2 + 4 * D + 2 * H
    W_foxT = _interleave_head_rows(W_inT[:o1], H)
    W_gqkvT = _interleave_head_rows(W_inT[o2:o3], H)
    W_gzT = W_inT[o5:o6]
    W_gatesT = W_inT[o6:]
    W_smallT = jnp.concatenate([W_inT[o1:o2], W_inT[o3:o5], jnp.zeros((128 - 3 * H, D), BF16)], axis=0)
    gconv_i = _interleave_heads(gconv, H)
    fconv_g, fconv_v = fconv[:, :FF], fconv[:, FF:]
    prm = jnp.zeros((8, 128), F32)
    prm = prm.at[0, 0:H].set(fox_f_bias[0]).at[0, H:2 * H].set(gdn_dt_bias[0]).at[1, H:2 * H].set(gdn_a_log[0])

    x2 = x.reshape(T, D)
    tgt = loss_target.reshape(T, D)

    hn1 = rmsnorm_fwd(x2, norm_mix, "rmsnorm_mix")
    p_fox = matmul(hn1, W_foxT, "nt", "proj_fox", out_dtype=BF16)
    p_gqkv = matmul(hn1, W_gqkvT, "nt", "proj_gqkv")
    p_gz = matmul(hn1, W_gzT, "nt", "proj_gz")
    p_gates = matmul(hn1, W_gatesT, "nt", "proj_gates")
    p_small = matmul(hn1, W_smallT, "nt", "proj_small")

    sm = small_fwd(p_small, prm, B, S, H)
    heads = lambda a: a.reshape(B, S, H).transpose(0, 2, 1)
    c_bhs, gc_bhs, beta_bhs = heads(sm[:, 0:H]), heads(sm[:, H:2 * H]), heads(sm[:, 2 * H:3 * H])
    c_col, c_row = c_bhs[..., None], c_bhs[:, :, None, :]
    gcr5 = gc_bhs.reshape(B, H, N, 1, CHUNK)
    betar5 = beta_bhs.reshape(B, H, N, 1, CHUNK)

    (o_fox, o_fox16, lse), arriving = fox_fwd(p_fox, c_col, c_row, B, S, H, rider=_ride_gather_ici(packs_b))
    qkvn = gdn_prep_fwd(p_gqkv, gconv_i, B, S, H)
    (u_hat, w_t, t_inv), arrived = gdn_intra_fwd(qkvn, betar5, gcr5, B, S, H, rider=_ride_gather_d2d(arriving))
    g_up, g_rowed = own(arrived, packs_b)
    W_up = by_cols(g_up)
    W_up_g, W_up_v = W_up[:, :FF], W_up[:, FF:]
    W_bf, W_bg, W_out, W_down = (cat_rows(p) for p in _unpack_rows(g_rowed.reshape(N_CHIP, 2 * Rh, ROW), rowed_shapes))
    o_gdn, states = gdn_inter_fwd(qkvn, u_hat, w_t, gcr5, B, S, H)
    y_gdn = gdn_post_fwd(o_gdn, p_gz, gdn_norm, H)
    bf_ = matmul(o_fox16, W_bf, "nn", "branch_fox")
    bg_ = matmul(y_gdn, W_bg, "nn", "branch_gdn")
    y = merge_fwd(p_gates, bf_, bg_)
    h1 = matmul(y, W_out, "nn", "out_proj", add=x2)
    hn2 = rmsnorm_fwd(h1, norm_ffn, "rmsnorm_ffn")
    up_g = matmul(hn2, W_up_g, "nn", "up_gate")
    up_v = matmul(hn2, W_up_v, "nn", "up_val")
    act = ffn_gate_fwd(up_g, up_v, fconv_g, fconv_v, B, S)
    h2 = matmul(act, W_down, "nn", "down_proj", add=h1)
    loss_cols, dh2, dh2_16, d_norm_final = final_loss(h2, norm_final.reshape(1, D), tgt)
    loss = lax.psum(0.5 * jnp.sum(loss_cols) / D, ("x", "y", "c"))

    d_act = matmul(dh2_16, W_down, "nt", "d_act")
    dW_down = matmul(act, dh2_16, "tn", "dw_down")
    d_upg, d_upv, d_fconv_g, d_fconv_v = ffn_gate_bwd(up_g, up_v, fconv_g, fconv_v, d_act, B, S)
    d_hn2 = matmul(d_upg, W_up_g, "nt", "d_hn2_g")
    d_hn2 = matmul(d_upv, W_up_v, "nt", "d_hn2_v", add=d_hn2)
    dW_up = jnp.concatenate([matmul(hn2, d_upg, "tn", "dw_up_g"), matmul(hn2, d_upv, "tn", "dw_up_v")], axis=1)
    dh1, dh1_16, d_norm_ffn = rmsnorm_bwd(h1, norm_ffn, d_hn2, dh2, "rmsnorm_ffn_bwd")
    d_y = matmul(dh1_16, W_out, "nt", "d_y")
    dW_out = matmul(y, dh1_16, "tn", "dw_out")
    d_bf, d_bg, d_gates = merge_bwd(p_gates, bf_, bg_, d_y)
    d_ofox = matmul(d_bf, W_bf, "nt", "d_ofox")
    dW_bf = matmul(o_fox16, d_bf, "tn", "dw_bf")
    d_ygdn = matmul(d_bg, W_bg, "nt", "d_ygdn")
    dW_bg = matmul(y_gdn, d_bg, "tn", "dw_bg")

    d_fconv = jnp.concatenate([d_fconv_g, d_fconv_v], axis=1)
    col_shard = lambda g, s: g[:, s * (g.shape[1] // N_CHIP):(s + 1) * (g.shape[1] // N_CHIP)]
    row_shard = lambda g, s: g[s * (g.shape[0] // N_CHIP):(s + 1) * (g.shape[0] // N_CHIP)]
    shard_items = lambda s: [row_shard(dW_bf, s), row_shard(dW_bg, s), row_shard(dW_out, s), row_shard(dW_down, s),
                             col_shard(d_fconv, s)]
    g_shapes = [a.shape for a in shard_items(0)]
    assert sum(_rows_of(math.prod(s)) for s in g_shapes) <= 2 * Rh
    to_slabs = lambda g: g.reshape(2, g.shape[0] // 2, N_CHIP, g.shape[1] // N_CHIP).transpose(0, 2, 1, 3)
    gpacks_b = [to_slabs(dW_up),
                jnp.stack([_pack_rows(shard_items(s), 2 * Rh, F32).reshape(2, Rh, ROW) for s in range(N_CHIP)], axis=1)]
    (d_pfox, d_ccol, d_crow), gots_b = fox_bwd(p_fox, c_col, c_row, o_fox, lse, d_ofox, B, S, H,
                                              rider=_ride_exchange(gpacks_b))
    sums_b = [add_halves(g, got, idx, "add_halves_b%d" % i) for i, (g, got) in enumerate(zip(gpacks_b, gots_b))]

    d_ogdn, d_gz, d_gdn_norm = gdn_post_bwd(o_gdn, p_gz, gdn_norm, d_ygdn, H)
    (dq_i, dk_i, d_uh, d_wt, dgcr_a), got16_b = gdn_inter_bwd(qkvn, u_hat, w_t, gcr5, states, d_ogdn, B, S, H,
                                                             rider=_ride_scatter([s16 for _, s16 in sums_b]))
    mine_b = [add_chips(s32, g16, idx, "add_chips_b%d" % i) for i, ((s32, _), g16) in enumerate(zip(sums_b, got16_b))]
    d_qkvn, d_betar5, dgcr_b = gdn_intra_bwd(qkvn, betar5, gcr5, t_inv, d_uh, d_wt, dq_i, dk_i, B, S, H)
    d_pgqkv, d_gconv_i = gdn_prep_bwd(p_gqkv, gconv_i, d_qkvn, B, S, H)

    tokens = lambda a: a.reshape(B, H, S).transpose(0, 2, 1).reshape(T, H)
    d_gc = (dgcr_a + dgcr_b).reshape(B, H, S)
    d_sm = jnp.concatenate([tokens(d_ccol.reshape(B, H, S) + d_crow.reshape(B, H, S)), tokens(d_gc), tokens(d_betar5.reshape(B, H, S)),
                            jnp.zeros((T, 128 - 3 * H), F32)], axis=1)
    d_psmall, d_prm = small_bwd(p_small, prm, d_sm, B, S, H)

    d_hn1 = matmul(d_pfox, W_foxT, "nn", "d_hn1_fox")
    d_hn1 = matmul(d_pgqkv, W_gqkvT, "nn", "d_hn1_gqkv", add=d_hn1)
    d_hn1 = matmul(d_gz, W_gzT, "nn", "d_hn1_gz", add=d_hn1)
    d_hn1 = matmul(d_gates, W_gatesT, "nn", "d_hn1_gates", add=d_hn1)
    d_hn1 = matmul(d_psmall, W_smallT, "nn", "d_hn1_small", add=d_hn1)
    dW_foxT = matmul(d_pfox, hn1, "tn", "dw_fox")
    dW_gqkvT = matmul(d_pgqkv, hn1, "tn", "dw_gqkv")
    dW_gzT = matmul(d_gz, hn1, "tn", "dw_gz")
    dW_gatesT = matmul(d_gates, hn1, "tn", "dw_gates")
    dW_smallT = matmul(d_psmall, hn1, "tn", "dw_small")
    grad_x, _, d_norm_mix = rmsnorm_bwd(x2, norm_mix, d_hn1, dh1, "rmsnorm_mix_bwd")

    dW_inT = jnp.concatenate([_deinterleave_head_rows(dW_foxT, H), dW_smallT[0:H], _deinterleave_head_rows(dW_gqkvT, H),
                              dW_smallT[H:3 * H], dW_gzT, dW_gatesT], axis=0)
    d_gconv = _deinterleave_heads(d_gconv_i, H)

    gpack_a = [dW_inT.reshape(N_CHIP, c_in, 2, D // 2).transpose(2, 0, 1, 3)]
    gots_a = _run_alone(_ride_exchange(gpack_a), "exchange_halves")
    sums_a = [add_halves(gpack_a[0], gots_a[0], idx, "add_halves_a")]
    got16_a = _run_alone(_ride_scatter([sums_a[0][1]]), "scatter_chips")
    mine = [add_chips(sums_a[0][0], got16_a[0], idx, "add_chips_a")] + mine_b
    others = share_halves(mine)
    g_w_inT, g_up, g_rows = (jnp.concatenate([jnp.where(cidx == 0, h, o), jnp.where(cidx == 0, o, h)], axis=ax)
                             for h, o, ax in zip(mine, others, (1, 0, 0)))
    g_w_in = g_w_inT.T
    g_bf, g_bg, g_out, g_down, g_fconv = _unpack_rows(g_rows, g_shapes)

    small_items = [d_norm_mix, d_norm_ffn, d_norm_final, d_gdn_norm, d_prm, d_gconv]
    small_shapes = [a.shape for a in small_items]
    sv = allreduce_small(_pack_rows(small_items, 0, F32, unit=8))
    g_norm_mix, g_norm_ffn, g_norm_final, g_gdn_norm, g_prm, g_gconv_all = _unpack_rows(sv, small_shapes, unit=8)
    g_norm_final = g_norm_final.reshape(D)
    g_fbias, g_dtb, g_alog = g_prm[0:1, 0:H], g_prm[0:1, H:2 * H], g_prm[1:2, H:2 * H]
    g_gconv = lax.dynamic_slice_in_dim(g_gconv_all, sidx * (3 * D // N_CHIP), 3 * D // N_CHIP, axis=1)

    names = ["norm_mix", "w_in", "fox_f_bias", "gdn_conv_w", "gdn_a_log", "gdn_dt_bias", "gdn_norm", "w_branch_fox",
             "w_branch_gdn", "w_out", "norm_ffn", "w_up", "ffn_conv_w", "w_down", "norm_final"]
    ws = [norm_mix, w_in, fox_f_bias, gdn_conv_w, gdn_a_log, gdn_dt_bias, gdn_norm, w_branch_fox, w_branch_gdn, w_out,
          norm_ffn, w_up, ffn_conv_w, w_down, norm_final]
    ms = [m_norm_mix, m_w_in, m_fox_f_bias, m_gdn_conv_w, m_gdn_a_log, m_gdn_dt_bias, m_gdn_norm, m_w_branch_fox,
          m_w_branch_gdn, m_w_out, m_norm_ffn, m_w_up, m_ffn_conv_w, m_w_down, m_norm_final]
    vs = [v_norm_mix, v_w_in, v_fox_f_bias, v_gdn_conv_w, v_gdn_a_log, v_gdn_dt_bias, v_gdn_norm, v_w_branch_fox,
          v_w_branch_gdn, v_w_out, v_norm_ffn, v_w_up, v_ffn_conv_w, v_w_down, v_norm_final]
    gs = [g_norm_mix, g_w_in, g_fbias, g_gconv, g_alog, g_dtb, g_gdn_norm, g_bf, g_bg, g_out, g_norm_ffn, g_up,
          g_fconv, g_down, g_norm_final]
    gs = [g.reshape(w.shape) for g, w in zip(gs, ws)]
    deltas, new_ms, new_vs = [], [], []
    for nm, w, g, m, v in zip(names, ws, gs, ms, vs):
        if w.ndim == 1:
            d, a, b = adamw(w.reshape(1, -1), g.reshape(1, -1), m.reshape(1, -1), v.reshape(1, -1), "adamw_" + nm)
            d, a, b = d.reshape(w.shape), a.reshape(w.shape), b.reshape(w.shape)
        elif nm == "w_in":
            d, a, b = (r.T[None] for r in adamw(w[0].T, g_w_inT, m[0].T, v[0].T, "adamw_" + nm))
        else:
            d, a, b = adamw(w, g, m, v, "adamw_" + nm)
        deltas.append(d)
        new_ms.append(a)
        new_vs.append(b)

    return (loss, grad_x.reshape(B, S, D), *gs, *deltas, *new_ms, *new_vs)
```

```python
import functools
import math

import jax
import jax.numpy as jnp
from jax import lax
from jax.experimental import pallas as pl
from jax.experimental.pallas import tpu as pltpu

F32 = jnp.float32
BF16 = jnp.bfloat16
HEAD = 128
CHUNK = 64
GDN_CONV = 4
FFN_CONV = 3
EPS = 1e-6
NEG = -1e30
ROW = 1024
ATT_TILE = 512
MM_WEIGHT_TILE_BYTES = 8 << 20
N_CHIP = 4
N_DEV = 8
MESH = pl.DeviceIdType.MESH
HI = lax.Precision.HIGH
EXACT = lax.Precision.HIGHEST

ADAM_LR, ADAM_B1, ADAM_B2, ADAM_EPS, ADAM_WD, ADAM_STEP = 0.001, 0.9, 0.999, 1e-08, 0.01, 10


def _tile(n, cap, unit=128):
    best = None
    t = unit
    while t <= min(n, cap):
        if n % t == 0:
            best = t
        t += unit
    return best if best is not None else n


def _params(*sem):
    return pltpu.CompilerParams(dimension_semantics=sem)


_NN = (((1,), (0,)), ((), ()))
_NT = (((1,), (1,)), ((), ()))
_TN = (((0,), (0,)), ((), ()))


def _dg(a, b, dims, hi):
    if hi:
        return lax.dot_general(a, b, dims, precision=HI, preferred_element_type=F32)
    return lax.dot_general(a.astype(BF16), b.astype(BF16), dims, preferred_element_type=F32)


class _RawOps:
    @staticmethod
    def nn(a, b, hi=False):
        return _dg(a, b, _NN, hi)

    @staticmethod
    def nt(a, b, hi=False):
        return _dg(a, b, _NT, hi)

    @staticmethod
    def tn(a, b, hi=False):
        return _dg(a, b, _TN, hi)


def _make_diff_ops():
    def build(hi):
        @jax.custom_vjp
        def nn(a, b):
            return _dg(a, b, _NN, hi)

        nn.defvjp(lambda a, b: (_dg(a, b, _NN, hi), (a, b)),
                  lambda r, g: (_dg(g, r[1], _NT, hi), _dg(r[0], g, _TN, hi)))

        @jax.custom_vjp
        def nt(a, b):
            return _dg(a, b, _NT, hi)

        nt.defvjp(lambda a, b: (_dg(a, b, _NT, hi), (a, b)),
                  lambda r, g: (_dg(g, r[1], _NN, hi), _dg(g, r[0], _TN, hi)))

        @jax.custom_vjp
        def tn(a, b):
            return _dg(a, b, _TN, hi)

        tn.defvjp(lambda a, b: (_dg(a, b, _TN, hi), (a, b)),
                  lambda r, g: (_dg(r[1], g, _NT, hi), _dg(r[0], g, _NN, hi)))
        return nn, nt, tn

    lo, hi_ = build(False), build(True)

    class _DiffOps:
        @staticmethod
        def nn(a, b, hi=False):
            return (hi_ if hi else lo)[0](a, b)

        @staticmethod
        def nt(a, b, hi=False):
            return (hi_ if hi else lo)[1](a, b)

        @staticmethod
        def tn(a, b, hi=False):
            return (hi_ if hi else lo)[2](a, b)

    return _DiffOps


_DiffOps = _make_diff_ops()


def _sigmoid(x):
    return 1.0 / (1.0 + jnp.exp(-x))


def _mm_tile(n, pref):
    if n % pref == 0:
        return pref
    if n % 1408 == 0:
        return 1408
    return _tile(n, pref)


def matmul(a, b, mode, name, add=None, out_dtype=F32):
    if mode == "nn":
        (M, K), (K2, N) = a.shape, b.shape
    elif mode == "nt":
        (M, K), (N, K2) = a.shape, b.shape
    else:
        (K, M), (K2, N) = a.shape, b.shape
    assert K == K2, (name, a.shape, b.shape)
    tn = _mm_tile(N, 1024)
    if mode == "tn":
        tm = M if M <= 1408 else _mm_tile(M, 1408)
        tk = _mm_tile(K, 1024)
    else:
        tk = K if K * tn * 2 <= MM_WEIGHT_TILE_BYTES else _mm_tile(K, 1024)
        tm = _mm_tile(M, 1024 if tk <= 2048 else 512)
    nk = K // tk
    dims = {"nn": _NN, "nt": _NT, "tn": _TN}[mode]
    if mode == "tn":
        a_spec = pl.BlockSpec((tk, tm), lambda j, i, k: (k, i))
    else:
        a_spec = pl.BlockSpec((tm, tk), lambda j, i, k: (i, k))
    if mode == "nt":
        b_spec = pl.BlockSpec((tn, tk), lambda j, i, k: (j, k))
    else:
        b_spec = pl.BlockSpec((tk, tn), lambda j, i, k: (k, j))
    o_spec = pl.BlockSpec((tm, tn), lambda j, i, k: (i, j))
    has_add = add is not None

    def body(*refs):
        a_ref, b_ref = refs[:2]
        add_ref = refs[2] if has_add else None
        o_ref = refs[3] if has_add else refs[2]
        prod = lax.dot_general(a_ref[...].astype(BF16), b_ref[...].astype(BF16), dims, preferred_element_type=F32)

        def finish(r):
            if has_add:
                r = r + add_ref[...]
            o_ref[...] = r.astype(out_dtype)

        if nk == 1:
            finish(prod)
            return
        acc_ref = refs[-1]
        k = pl.program_id(2)

        @pl.when(k == 0)
        def _():
            acc_ref[...] = jnp.zeros_like(acc_ref)

        acc_ref[...] += prod

        @pl.when(k == nk - 1)
        def _():
            finish(acc_ref[...])

    in_specs = [a_spec, b_spec] + ([o_spec] if has_add else [])
    args = (a, b) + ((add,) if has_add else ())
    return pl.pallas_call(
        body, name=name, grid=(N // tn, M // tm, nk), in_specs=in_specs, out_specs=o_spec,
        out_shape=jax.ShapeDtypeStruct((M, N), out_dtype),
        scratch_shapes=[pltpu.VMEM((tm, tn), F32)] if nk > 1 else [],
        compiler_params=_params("parallel", "parallel", "arbitrary"),
    )(*args)


def rmsnorm_fwd(x, g, name):
    T, D = x.shape
    tm = _tile(T, 512, 8)

    def body(x_ref, g_ref, o_ref):
        xv = x_ref[...]
        r = lax.rsqrt(jnp.mean(xv * xv, axis=-1, keepdims=True) + EPS)
        o_ref[...] = (xv * r * g_ref[...]).astype(BF16)

    return pl.pallas_call(
        body, name=name, grid=(T // tm,),
        in_specs=[pl.BlockSpec((tm, D), lambda i: (i, 0)), pl.BlockSpec((1, D), lambda i: (0, 0))],
        out_specs=pl.BlockSpec((tm, D), lambda i: (i, 0)),
        out_shape=jax.ShapeDtypeStruct((T, D), BF16),
        compiler_params=_params("parallel"),
    )(x, g)


def rmsnorm_bwd(x, g, dy, dres, name):
    T, D = x.shape
    tm = _tile(T, 512, 8)

    def body(x_ref, g_ref, dy_ref, dres_ref, dx_ref, dx16_ref, dg_ref):
        @pl.when(pl.program_id(0) == 0)
        def _():
            dg_ref[...] = jnp.zeros_like(dg_ref)

        xv = x_ref[...]
        r = lax.rsqrt(jnp.mean(xv * xv, axis=-1, keepdims=True) + EPS)
        xh = xv * r
        dyv = dy_ref[...]
        dg_ref[...] += jnp.sum(dyv * xh, axis=0, keepdims=True)
        dxh = dyv * g_ref[...]
        dx = dres_ref[...] + r * (dxh - xh * jnp.mean(dxh * xh, axis=-1, keepdims=True))
        dx_ref[...] = dx
        dx16_ref[...] = dx.astype(BF16)

    row = pl.BlockSpec((tm, D), lambda i: (i, 0))
    vec = pl.BlockSpec((1, D), lambda i: (0, 0))
    return pl.pallas_call(
        body, name=name, grid=(T // tm,), in_specs=[row, vec, row, row], out_specs=[row, row, vec],
        out_shape=[jax.ShapeDtypeStruct((T, D), F32), jax.ShapeDtypeStruct((T, D), BF16),
                   jax.ShapeDtypeStruct((1, D), F32)],
        compiler_params=_params("arbitrary"),
    )(x, g, dy, dres)


def final_loss(h, g, target):
    T, D = h.shape
    tm = _tile(T, 512, 8)

    def body(h_ref, g_ref, t_ref, loss_ref, dh_ref, dh16_ref, dg_ref):
        @pl.when(pl.program_id(0) == 0)
        def _():
            loss_ref[...] = jnp.zeros_like(loss_ref)
            dg_ref[...] = jnp.zeros_like(dg_ref)

        hv = h_ref[...]
        r = lax.rsqrt(jnp.mean(hv * hv, axis=-1, keepdims=True) + EPS)
        xh = hv * r
        err = xh * g_ref[...] - t_ref[...]
        loss_ref[...] += jnp.sum(err * err, axis=0, keepdims=True)
        dy = err * (1.0 / D)
        dg_ref[...] += jnp.sum(dy * xh, axis=0, keepdims=True)
        dxh = dy * g_ref[...]
        dh = r * (dxh - xh * jnp.mean(dxh * xh, axis=-1, keepdims=True))
        dh_ref[...] = dh
        dh16_ref[...] = dh.astype(BF16)

    row = pl.BlockSpec((tm, D), lambda i: (i, 0))
    vec = pl.BlockSpec((1, D), lambda i: (0, 0))
    return pl.pallas_call(
        body, name="final_loss", grid=(T // tm,), in_specs=[row, vec, row], out_specs=[vec, row, row, vec],
        out_shape=[jax.ShapeDtypeStruct((1, D), F32), jax.ShapeDtypeStruct((T, D), F32),
                   jax.ShapeDtypeStruct((T, D), BF16), jax.ShapeDtypeStruct((1, D), F32)],
        compiler_params=_params("arbitrary"),
    )(h, g, target)


def _shift_down(x, k):
    if k == 0:
        return x
    rows = lax.broadcasted_iota(jnp.int32, x.shape, 0)
    return jnp.where(rows >= k, pltpu.roll(x, k, 0), 0.0)


def _shift_up(x, k):
    if k == 0:
        return x
    s = x.shape[0]
    rows = lax.broadcasted_iota(jnp.int32, x.shape, 0)
    return jnp.where(rows < s - k, pltpu.roll(x, s - k, 0), 0.0)


EDGE = 8


def _taps(x, w_ref, kw, shifted):
    y = x * w_ref[kw - 1:kw, :]
    for i in range(kw - 1):
        y = y + shifted(x, kw - 1 - i) * w_ref[i:i + 1, :]
    return y


def _conv_fwd(x, w_ref, kw):
    body = _taps(x, w_ref, kw, lambda a, k: pltpu.roll(a, k, 0))
    return jnp.concatenate([_taps(x[:EDGE], w_ref, kw, _shift_down), body[EDGE:]], axis=0)


def _conv_bwd(x, dy, w_ref, kw):
    s = x.shape[0]
    body = _taps(dy, w_ref, kw, lambda a, k: pltpu.roll(a, s - k, 0))
    dx = jnp.concatenate([body[:s - EDGE], _taps(dy[s - EDGE:], w_ref, kw, _shift_up)], axis=0)
    rows = lax.broadcasted_iota(jnp.int32, (EDGE, x.shape[1]), 0)
    dws = []
    for i in range(kw - 1):
        k = kw - 1 - i
        wrapped = jnp.where(rows < k, dy[:EDGE] * pltpu.roll(x[s - EDGE:], k, 0), 0.0)
        dws.append(jnp.sum(dy * pltpu.roll(x, k, 0), axis=0, keepdims=True) - jnp.sum(wrapped, axis=0, keepdims=True))
    dws.append(jnp.sum(dy * x, axis=0, keepdims=True))
    return dx, dws


def ffn_gate_fwd(up_g, up_v, cw_g, cw_v, B, S):
    T, Fd = up_g.shape
    tc = _tile(Fd, 256)

    def body(g_ref, v_ref, wg_ref, wv_ref, o_ref):
        ug = _conv_fwd(g_ref[...], wg_ref, FFN_CONV)
        uv = _conv_fwd(v_ref[...], wv_ref, FFN_CONV)
        o_ref[...] = (ug * _sigmoid(ug) * uv).astype(BF16)

    blk = pl.BlockSpec((S, tc), lambda b, j: (b, j))
    wblk = pl.BlockSpec((FFN_CONV, tc), lambda b, j: (0, j))
    return pl.pallas_call(
        body, name="ffn_gate_fwd", grid=(B, Fd // tc), in_specs=[blk, blk, wblk, wblk], out_specs=blk,
        out_shape=jax.ShapeDtypeStruct((T, Fd), BF16), compiler_params=_params("parallel", "parallel"),
    )(up_g, up_v, cw_g, cw_v)


def ffn_gate_bwd(up_g, up_v, cw_g, cw_v, d_act, B, S):
    T, Fd = up_g.shape
    tc = _tile(Fd, 256)

    def body(g_ref, v_ref, wg_ref, wv_ref, da_ref, dg_ref, dv_ref, dwg_ref, dwv_ref):
        @pl.when(pl.program_id(1) == 0)
        def _():
            dwg_ref[...] = jnp.zeros_like(dwg_ref)
            dwv_ref[...] = jnp.zeros_like(dwv_ref)

        xg, xv = g_ref[...], v_ref[...]
        ug = _conv_fwd(xg, wg_ref, FFN_CONV)
        uv = _conv_fwd(xv, wv_ref, FFN_CONV)
        da = da_ref[...]
        sg = _sigmoid(ug)
        d_ug = da * uv * (sg + ug * sg * (1.0 - sg))
        d_uv = da * ug * sg
        dxg, dwg = _conv_bwd(xg, d_ug, wg_ref, FFN_CONV)
        dxv, dwv = _conv_bwd(xv, d_uv, wv_ref, FFN_CONV)
        dg_ref[...] = dxg.astype(BF16)
        dv_ref[...] = dxv.astype(BF16)
        for i in range(FFN_CONV):
            dwg_ref[i:i + 1, :] += dwg[i]
            dwv_ref[i:i + 1, :] += dwv[i]

    blk = pl.BlockSpec((S, tc), lambda j, b: (b, j))
    wblk = pl.BlockSpec((FFN_CONV, tc), lambda j, b: (0, j))
    return pl.pallas_call(
        body, name="ffn_gate_bwd", grid=(Fd // tc, B), in_specs=[blk, blk, wblk, wblk, blk],
        out_specs=[blk, blk, wblk, wblk],
        out_shape=[jax.ShapeDtypeStruct((T, Fd), BF16), jax.ShapeDtypeStruct((T, Fd), BF16),
                   jax.ShapeDtypeStruct((FFN_CONV, Fd), F32), jax.ShapeDtypeStruct((FFN_CONV, Fd), F32)],
        compiler_params=_params("parallel", "arbitrary"),
    )(up_g, up_v, cw_g, cw_v, d_act)


def merge_fwd(p_gates, bf_, bg_):
    T, D = bf_.shape
    tm = _tile(T, 512, 8)

    def body(gf_ref, gg_ref, bf_ref, bg_ref, o_ref):
        o_ref[...] = (_sigmoid(gf_ref[...]) * bf_ref[...] + _sigmoid(gg_ref[...]) * bg_ref[...]).astype(BF16)

    lo = pl.BlockSpec((tm, D), lambda i: (i, 0))
    hi = pl.BlockSpec((tm, D), lambda i: (i, 1))
    return pl.pallas_call(
        body, name="merge_fwd", grid=(T // tm,), in_specs=[lo, hi, lo, lo], out_specs=lo,
        out_shape=jax.ShapeDtypeStruct((T, D), BF16), compiler_params=_params("parallel"),
    )(p_gates, p_gates, bf_, bg_)


def merge_bwd(p_gates, bf_, bg_, dy):
    T, D = bf_.shape
    tm = _tile(T, 512, 8)

    def body(gf_ref, gg_ref, bf_ref, bg_ref, dy_ref, dbf_ref, dbg_ref, dgate_ref):
        d = dy_ref[...]
        sf, sg = _sigmoid(gf_ref[...]), _sigmoid(gg_ref[...])
        dbf_ref[...] = (d * sf).astype(BF16)
        dbg_ref[...] = (d * sg).astype(BF16)
        dgate_ref[:, 0:D] = (d * bf_ref[...] * sf * (1.0 - sf)).astype(BF16)
        dgate_ref[:, D:2 * D] = (d * bg_ref[...] * sg * (1.0 - sg)).astype(BF16)

    lo = pl.BlockSpec((tm, D), lambda i: (i, 0))
    hi = pl.BlockSpec((tm, D), lambda i: (i, 1))
    both = pl.BlockSpec((tm, 2 * D), lambda i: (i, 0))
    return pl.pallas_call(
        body, name="merge_bwd", grid=(T // tm,), in_specs=[lo, hi, lo, lo, lo], out_specs=[lo, lo, both],
        out_shape=[jax.ShapeDtypeStruct((T, D), BF16), jax.ShapeDtypeStruct((T, D), BF16),
                   jax.ShapeDtypeStruct((T, 2 * D), BF16)],
        compiler_params=_params("parallel"),
    )(p_gates, p_gates, bf_, bg_, dy)


def fox_fwd(p_fox, c_col, c_row, B, S, H, rider=None):
    T = B * S
    t = _tile(S, ATT_TILE)
    nq = S // t
    scale = HEAD ** -0.5

    def body(q_ref, k_ref, v_ref, cq_ref, cr_ref, o_ref, o16_ref, lse_ref):
        i = pl.program_id(2)
        q = q_ref[...]
        cq = cq_ref[...]
        row = lax.broadcasted_iota(jnp.int32, (t, t), 0)
        col = lax.broadcasted_iota(jnp.int32, (t, t), 1)

        def step(j, carry, diagonal):
            m, l, acc = carry
            off = pl.multiple_of(j * t, t)
            k = k_ref[pl.ds(off, t), :]
            v = v_ref[pl.ds(off, t), :]
            s = lax.dot_general(q, k, _NT, preferred_element_type=F32) * scale + (cq - cr_ref[:, pl.ds(off, t)])
            if diagonal:
                s = jnp.where(col <= row, s, NEG)
            m_new = jnp.maximum(m, jnp.max(s, axis=-1, keepdims=True))
            alpha = jnp.exp(m - m_new)
            p = jnp.exp(s - m_new)
            l = alpha * l + jnp.sum(p, axis=-1, keepdims=True)
            acc = alpha * acc + lax.dot_general(p.astype(BF16), v, _NN, preferred_element_type=F32)
            return m_new, l, acc

        m0 = jnp.full((t, 1), NEG, F32)
        below = lax.fori_loop(0, i, functools.partial(step, diagonal=False),
                              (m0, jnp.zeros((t, 1), F32), jnp.zeros((t, HEAD), F32)))
        m, l, acc = step(i, below, diagonal=True)
        o = acc / l
        o_ref[...] = o
        o16_ref[...] = o.astype(BF16)
        lse_ref[...] = m + jnp.log(l)

    return _hosted_call(
        body, rider, name="fox_fwd", grid=(B, H, nq),
        in_specs=[pl.BlockSpec((t, HEAD), lambda b, h, i: (b * nq + i, 3 * h)),
                  pl.BlockSpec((S, HEAD), lambda b, h, i: (b, 3 * h + 1)),
                  pl.BlockSpec((S, HEAD), lambda b, h, i: (b, 3 * h + 2)),
                  pl.BlockSpec((None, None, t, 1), lambda b, h, i: (b, h, i, 0)),
                  pl.BlockSpec((None, None, 1, S), lambda b, h, i: (b, h, 0, 0))],
        out_specs=[pl.BlockSpec((t, HEAD), lambda b, h, i: (b * nq + i, h)),
                   pl.BlockSpec((t, HEAD), lambda b, h, i: (b * nq + i, h)),
                   pl.BlockSpec((None, None, t, 1), lambda b, h, i: (b, h, i, 0))],
        out_shape=[jax.ShapeDtypeStruct((T, H * HEAD), F32), jax.ShapeDtypeStruct((T, H * HEAD), BF16),
                   jax.ShapeDtypeStruct((B, H, S, 1), F32)],
        scratch_shapes=[], semantics=("parallel", "parallel", "arbitrary"),
    )(p_fox, p_fox, p_fox, c_col, c_row)


def fox_bwd(p_fox, c_col, c_row, o, lse, do, B, S, H, rider=None):
    T = B * S
    t = _tile(S, ATT_TILE)
    n = S // t
    scale = HEAD ** -0.5

    def body(q_ref, k_ref, v_ref, cq_ref, cr_ref, o_ref, lse_ref, do_ref, dqkv_ref, dcq_ref, dcr_ref, dq_acc, delta_s):
        row = lax.broadcasted_iota(jnp.int32, (t, t), 0)
        col = lax.broadcasted_iota(jnp.int32, (t, t), 1)

        def prep(i, c):
            rows = pl.ds(pl.multiple_of(i * t, t), t)
            delta_s[rows, :] = jnp.sum(do_ref[rows, :] * o_ref[rows, :], axis=-1, keepdims=True)
            dq_acc[rows, :] = jnp.zeros((t, HEAD), F32)
            dcq_ref[rows, :] = jnp.zeros((t, 1), F32)
            return c

        lax.fori_loop(0, n, prep, 0)

        def kv_step(j, c):
            joff = pl.multiple_of(j * t, t)
            k = k_ref[pl.ds(joff, t), :]
            v = v_ref[pl.ds(joff, t), :]
            crj = cr_ref[:, pl.ds(joff, t)]

            def q_step(i, carry, diagonal):
                dk, dv, dc = carry
                rows = pl.ds(pl.multiple_of(i * t, t), t)
                q = q_ref[rows, :]
                dob = do_ref[rows, :].astype(BF16)
                s = lax.dot_general(q, k, _NT, preferred_element_type=F32) * scale + (cq_ref[rows, :] - crj)
                if diagonal:
                    s = jnp.where(col <= row, s, NEG)
                p = jnp.exp(s - lse_ref[rows, :])
                dp = lax.dot_general(dob, v, _NT, preferred_element_type=F32)
                ds = p * (dp - delta_s[rows, :])
                dsb = ds.astype(BF16)
                dv = dv + lax.dot_general(p.astype(BF16), dob, _TN, preferred_element_type=F32)
                dk = dk + lax.dot_general(dsb, q, _TN, preferred_element_type=F32)
                dq_acc[rows, :] += lax.dot_general(dsb, k, _NN, preferred_element_type=F32) * scale
                dc = dc + jnp.sum(ds, axis=0, keepdims=True)
                dcq_ref[rows, :] += jnp.sum(ds, axis=-1, keepdims=True)
                return dk, dv, dc

            z = jnp.zeros((t, HEAD), F32)
            on_diagonal = q_step(j, (z, z, jnp.zeros((1, t), F32)), diagonal=True)
            dk, dv, dc = lax.fori_loop(j + 1, n, functools.partial(q_step, diagonal=False), on_diagonal)
            dqkv_ref[pl.ds(joff, t), HEAD:2 * HEAD] = (dk * scale).astype(BF16)
            dqkv_ref[pl.ds(joff, t), 2 * HEAD:3 * HEAD] = dv.astype(BF16)
            dcr_ref[:, pl.ds(joff, t)] = -dc
            return c

        lax.fori_loop(0, n, kv_step, 0)
        dqkv_ref[:, 0:HEAD] = dq_acc[...].astype(BF16)

    col_spec = pl.BlockSpec((None, None, S, 1), lambda b, h: (b, h, 0, 0))
    row_spec = pl.BlockSpec((None, None, 1, S), lambda b, h: (b, h, 0, 0))
    head = pl.BlockSpec((S, HEAD), lambda b, h: (b, h))
    return _hosted_call(
        body, rider, name="fox_bwd", grid=(B, H),
        in_specs=[pl.BlockSpec((S, HEAD), lambda b, h: (b, 3 * h)),
                  pl.BlockSpec((S, HEAD), lambda b, h: (b, 3 * h + 1)),
                  pl.BlockSpec((S, HEAD), lambda b, h: (b, 3 * h + 2)),
                  col_spec, row_spec, head, col_spec, head],
        out_specs=[pl.BlockSpec((S, 3 * HEAD), lambda b, h: (b, h)), col_spec, row_spec],
        out_shape=[jax.ShapeDtypeStruct((T, 3 * H * HEAD), BF16), jax.ShapeDtypeStruct((B, H, S, 1), F32),
                   jax.ShapeDtypeStruct((B, H, 1, S), F32)],
        scratch_shapes=[pltpu.VMEM((S, HEAD), F32), pltpu.VMEM((S, 1), F32)], semantics=("parallel", "parallel"),
    )(p_fox, p_fox, p_fox, c_col, c_row, o, lse, do)


def _small_fn(x, b0, b1, H):
    S = x.shape[0]
    lane = lax.broadcasted_iota(jnp.int32, x.shape, 1)
    z = x + b0
    tail = jnp.log1p(jnp.exp(-jnp.abs(z)))
    softplus = jnp.maximum(z, 0.0) + tail
    logsig = -(jnp.maximum(-z, 0.0) + tail)
    g = -jnp.exp(b1) * softplus
    pre = jnp.where(lane < H, logsig, jnp.where(lane < 2 * H, g, 0.0))
    bl = _tile(S, 256, CHUNK)
    r = lax.broadcasted_iota(jnp.int32, (bl, bl), 0)
    c = lax.broadcasted_iota(jnp.int32, (bl, bl), 1)
    tri = (r >= c).astype(F32)
    tri_chunk = jnp.where((r >= c) & (jnp.right_shift(r, 6) == jnp.right_shift(c, 6)), 1.0, 0.0)
    carry = jnp.zeros((1, x.shape[1]), F32)
    parts = []
    for i in range(S // bl):
        blk = pre[i * bl:(i + 1) * bl, :]
        full = lax.dot_general(tri, blk, _NN, precision=EXACT, preferred_element_type=F32) + carry
        chunked = lax.dot_general(tri_chunk, blk, _NN, precision=EXACT, preferred_element_type=F32)
        parts.append(jnp.where(lane[:bl] < H, full, chunked))
        carry = carry + jnp.sum(blk, axis=0, keepdims=True)
    cum = parts[0] if len(parts) == 1 else jnp.concatenate(parts, axis=0)
    return jnp.where(lane < 2 * H, cum, jnp.where(lane < 3 * H, _sigmoid(x), 0.0))


def small_fwd(p_small, prm, B, S, H):
    T = B * S

    def body(x_ref, p_ref, o_ref):
        o_ref[...] = _small_fn(x_ref[...], p_ref[0:1, :], p_ref[1:2, :], H)

    blk = pl.BlockSpec((S, 128), lambda b: (b, 0))
    return pl.pallas_call(
        body, name="small_fwd", grid=(B,), in_specs=[blk, pl.BlockSpec((8, 128), lambda b: (0, 0))], out_specs=blk,
        out_shape=jax.ShapeDtypeStruct((T, 128), F32), compiler_params=_params("parallel"),
    )(p_small, prm)


def small_bwd(p_small, prm, d_out, B, S, H):
    T = B * S

    def body(x_ref, p_ref, d_ref, dx_ref, dp_ref):
        @pl.when(pl.program_id(0) == 0)
        def _():
            dp_ref[...] = jnp.zeros_like(dp_ref)

        _, vjp = jax.vjp(functools.partial(_small_fn, H=H), x_ref[...], p_ref[0:1, :], p_ref[1:2, :])
        dx, db0, db1 = vjp(d_ref[...])
        dx_ref[...] = dx.astype(BF16)
        dp_ref[0:1, :] += db0
        dp_ref[1:2, :] += db1

    blk = pl.BlockSpec((S, 128), lambda b: (b, 0))
    pblk = pl.BlockSpec((8, 128), lambda b: (0, 0))
    return pl.pallas_call(
        body, name="small_bwd", grid=(B,), in_specs=[blk, pblk, blk], out_specs=[blk, pblk],
        out_shape=[jax.ShapeDtypeStruct((T, 128), BF16), jax.ShapeDtypeStruct((8, 128), F32)],
        compiler_params=_params("arbitrary"),
    )(p_small, prm, d_out)


def gdn_prep_fwd(p_gqkv, cw, B, S, H):
    T = B * S

    def body(x_ref, w_ref, o_ref):
        y = _conv_fwd(x_ref[...], w_ref, GDN_CONV)
        a = y * _sigmoid(y)
        rs = lax.rsqrt(jnp.sum(a * a, axis=-1, keepdims=True) + EPS)
        is_qk = (pl.program_id(1) % 3) < 2
        o_ref[...] = a * jnp.where(is_qk, rs, 1.0)

    blk = pl.BlockSpec((S, HEAD), lambda b, n: (b, n))
    wblk = pl.BlockSpec((GDN_CONV, HEAD), lambda b, n: (0, n))
    return pl.pallas_call(
        body, name="gdn_prep_fwd", grid=(B, 3 * H), in_specs=[blk, wblk], out_specs=blk,
        out_shape=jax.ShapeDtypeStruct((T, 3 * H * HEAD), F32), compiler_params=_params("parallel", "parallel"),
    )(p_gqkv, cw)


def gdn_prep_bwd(p_gqkv, cw, d_out, B, S, H):
    T = B * S

    def body(x_ref, w_ref, d_ref, dx_ref, dw_ref):
        @pl.when(pl.program_id(1) == 0)
        def _():
            dw_ref[...] = jnp.zeros_like(dw_ref)

        x = x_ref[...]
        y = _conv_fwd(x, w_ref, GDN_CONV)
        sg = _sigmoid(y)
        a = y * sg
        rs = lax.rsqrt(jnp.sum(a * a, axis=-1, keepdims=True) + EPS)
        d = d_ref[...]
        out = a * rs
        da_qk = rs * (d - out * jnp.sum(d * out, axis=-1, keepdims=True))
        is_qk = (pl.program_id(0) % 3) < 2
        da = jnp.where(is_qk, da_qk, d)
        dy = da * (sg + y * sg * (1.0 - sg))
        dx, dws = _conv_bwd(x, dy, w_ref, GDN_CONV)
        dx_ref[...] = dx.astype(BF16)
        for i in range(GDN_CONV):
            dw_ref[i:i + 1, :] += dws[i]

    blk = pl.BlockSpec((S, HEAD), lambda n, b: (b, n))
    wblk = pl.BlockSpec((GDN_CONV, HEAD), lambda n, b: (0, n))
    return pl.pallas_call(
        body, name="gdn_prep_bwd", grid=(3 * H, B), in_specs=[blk, wblk, blk], out_specs=[blk, wblk],
        out_shape=[jax.ShapeDtypeStruct((T, 3 * H * HEAD), BF16), jax.ShapeDtypeStruct((GDN_CONV, 3 * H * HEAD), F32)],
        compiler_params=_params("parallel", "arbitrary"),
    )(p_gqkv, cw, d_out)


@jax.custom_vjp
def _given_inverse(a, t):
    return t


def _given_inverse_fwd(a, t):
    return t, t


def _given_inverse_bwd(t, g):
    x = _dg(t, g, _TN, True)
    return -_dg(x, t, _NT, True), jnp.zeros_like(t)


_given_inverse.defvjp(_given_inverse_fwd, _given_inverse_bwd)


def _to_col(row):
    n = row.shape[1]
    r = lax.broadcasted_iota(jnp.int32, (n, n), 0)
    c = lax.broadcasted_iota(jnp.int32, (n, n), 1)
    return jnp.sum(jnp.where(r == c, row, 0.0), axis=1, keepdims=True)


def _intra_fn(k, v, beta_r, gcr, ops, t_known=None):
    n = len(k)
    m = k[0].shape[0]
    r = lax.broadcasted_iota(jnp.int32, (m, m), 0)
    c = lax.broadcasted_iota(jnp.int32, (m, m), 1)
    below = (r > c) & (jnp.right_shift(r, 6) == jnp.right_shift(c, 6))
    beta = [_to_col(beta_r[i]) for i in range(n)]
    gcc = [_to_col(gcr[i]) for i in range(n)]
    decay = [jnp.exp(jnp.where(below, gcc[i] - gcr[i], NEG)) for i in range(n)]
    kb = [k[i] * beta[i] for i in range(n)]
    a = [ops.nt(kb[i], k[i]) * decay[i] for i in range(n)]
    if t_known is None:
        p = [-a[i] for i in range(n)]
        tm = [jnp.where(r == c, 1.0, 0.0) + p[i] for i in range(n)]
        for _ in range(5):
            p = [ops.nn(p[i], p[i], hi=True) for i in range(n)]
            tm = [tm[i] + ops.nn(tm[i], p[i], hi=True) for i in range(n)]
    else:
        tm = [_given_inverse(a[i], t_known[i]) for i in range(n)]
    u_hat = [ops.nn(tm[i], v[i] * beta[i], hi=True) for i in range(n)]
    w = [ops.nn(tm[i], kb[i] * jnp.exp(gcc[i]), hi=True) for i in range(n)]
    return tuple(u_hat), tuple(w), tuple(tm)


INTRA_NB = 8
PAIR = 2


def gdn_intra_fwd(qkvn, betar5, gcr5, B, S, H, rider=None):
    T = B * S
    UNIT = PAIR * CHUNK
    N = S // UNIT
    nb = min(INTRA_NB // PAIR, N)
    rows = nb * UNIT
    ns = N // nb

    def body(k_ref, v_ref, b_ref, gr_ref, uh_ref, w_ref, t_ref):
        sls = [slice(ci * UNIT, (ci + 1) * UNIT) for ci in range(nb)]
        u_hat, w, tm = _intra_fn(tuple(k_ref[sl, :] for sl in sls), tuple(v_ref[sl, :] for sl in sls),
                                 tuple(b_ref[ci] for ci in range(nb)), tuple(gr_ref[ci] for ci in range(nb)), _RawOps)
        for ci, sl in enumerate(sls):
            uh_ref[sl, :] = u_hat[ci]
            w_ref[sl, :] = w[ci]
            t_ref[ci] = tm[ci]

    rowspec = pl.BlockSpec((None, None, nb, 1, UNIT), lambda b, h, i: (b, h, i, 0, 0))
    sqspec = pl.BlockSpec((None, None, nb, UNIT, UNIT), lambda b, h, i: (b, h, i, 0, 0))
    out = pl.BlockSpec((rows, HEAD), lambda b, h, i: (b * ns + i, h))
    return _hosted_call(
        body, rider, name="gdn_intra_fwd", grid=(B, H, ns),
        in_specs=[pl.BlockSpec((rows, HEAD), lambda b, h, i: (b * ns + i, 3 * h + 1)),
                  pl.BlockSpec((rows, HEAD), lambda b, h, i: (b * ns + i, 3 * h + 2)),
                  rowspec, rowspec],
        out_specs=[out, out, sqspec],
        out_shape=[jax.ShapeDtypeStruct((T, H * HEAD), F32), jax.ShapeDtypeStruct((T, H * HEAD), F32),
                   jax.ShapeDtypeStruct((B, H, N, UNIT, UNIT), F32)],
        scratch_shapes=[], semantics=("parallel", "parallel", "parallel"),
    )(qkvn, qkvn, betar5, gcr5)


def gdn_intra_bwd(qkvn, betar5, gcr5, t_inv, d_uh, d_w, dq_in, dk_in, B, S, H):
    T = B * S
    UNIT = PAIR * CHUNK
    N = S // UNIT
    nb = min(INTRA_NB // PAIR, N)
    rows = nb * UNIT
    ns = N // nb

    def body(k_ref, v_ref, b_ref, gr_ref, t_ref, duh_ref, dw_ref, dq_ref, dk_ref, o_ref, db_ref, dgr_ref):
        sls = [slice(ci * UNIT, (ci + 1) * UNIT) for ci in range(nb)]
        chunks = range(nb)
        _, vjp = jax.vjp(
            functools.partial(_intra_fn, ops=_DiffOps, t_known=tuple(t_ref[ci] for ci in chunks)),
            tuple(k_ref[sl, :] for sl in sls), tuple(v_ref[sl, :] for sl in sls), tuple(b_ref[ci] for ci in chunks),
            tuple(gr_ref[ci] for ci in chunks))
        zero = jnp.zeros((UNIT, UNIT), F32)
        dk, dv, db, dgr = vjp((tuple(duh_ref[sl, :] for sl in sls), tuple(dw_ref[sl, :] for sl in sls),
                               tuple(zero for _ in chunks)))
        for ci, sl in enumerate(sls):
            o_ref[sl, 0:HEAD] = dq_ref[sl, :]
            o_ref[sl, HEAD:2 * HEAD] = dk[ci] + dk_ref[sl, :]
            o_ref[sl, 2 * HEAD:3 * HEAD] = dv[ci]
            db_ref[ci] = db[ci]
            dgr_ref[ci] = dgr[ci]

    rowspec = pl.BlockSpec((None, None, nb, 1, UNIT), lambda b, h, i: (b, h, i, 0, 0))
    sqspec = pl.BlockSpec((None, None, nb, UNIT, UNIT), lambda b, h, i: (b, h, i, 0, 0))
    head = pl.BlockSpec((rows, HEAD), lambda b, h, i: (b * ns + i, h))
    return pl.pallas_call(
        body, name="gdn_intra_bwd", grid=(B, H, ns),
        in_specs=[pl.BlockSpec((rows, HEAD), lambda b, h, i: (b * ns + i, 3 * h + 1)),
                  pl.BlockSpec((rows, HEAD), lambda b, h, i: (b * ns + i, 3 * h + 2)),
                  rowspec, rowspec, sqspec, head, head, head, head],
        out_specs=[pl.BlockSpec((rows, 3 * HEAD), lambda b, h, i: (b * ns + i, h)), rowspec, rowspec],
        out_shape=[jax.ShapeDtypeStruct((T, 3 * H * HEAD), F32),
                   jax.ShapeDtypeStruct((B, H, N, 1, UNIT), F32), jax.ShapeDtypeStruct((B, H, N, 1, UNIT), F32)],
        compiler_params=_params("parallel", "parallel", "parallel"),
    )(qkvn, qkvn, betar5, gcr5, t_inv, d_uh, d_w, dq_in, dk_in)


def _inter_fn(q, k, u_hat, w, gcr, state, ops):
    n = len(q)
    r = lax.broadcasted_iota(jnp.int32, (CHUNK, CHUNK), 0)
    c = lax.broadcasted_iota(jnp.int32, (CHUNK, CHUNK), 1)
    last = lax.broadcasted_iota(jnp.int32, (1, CHUNK), 1) == CHUNK - 1
    gcc = [_to_col(gcr[i]) for i in range(n)]
    gl = [jnp.sum(jnp.where(last, gcr[i], 0.0), axis=1, keepdims=True) for i in range(n)]
    decay = [jnp.exp(jnp.where(r >= c, gcc[i] - gcr[i], NEG)) for i in range(n)]
    qs = [q[i] * (HEAD ** -0.5) for i in range(n)]
    ws = [ops.nn(w[i], state[i]) for i in range(n)]
    qst = [ops.nn(qs[i] * jnp.exp(gcc[i]), state[i]) for i in range(n)]
    attn = [ops.nt(qs[i], k[i]) * decay[i] for i in range(n)]
    u = [u_hat[i] - ws[i] for i in range(n)]
    o = [qst[i] + ops.nn(attn[i], u[i]) for i in range(n)]
    kdu = [ops.tn(k[i] * jnp.exp(gl[i] - gcc[i]), u[i]) for i in range(n)]
    new_state = [state[i] * jnp.exp(gl[i]) + kdu[i] for i in range(n)]
    return tuple(o), tuple(new_state)


INTER_HEADS = 4
INTER_ROWS = 512


def _inter_specs(ts, ns, hp, backward):
    at = (lambda s: ns - 1 - s) if backward else (lambda s: s)
    nc = ts // CHUNK
    qk = []
    for hh in range(hp):
        qk.append(pl.BlockSpec((ts, HEAD), lambda b, g, s, hh=hh: (b * ns + at(s), 3 * (hp * g + hh))))
        qk.append(pl.BlockSpec((ts, HEAD), lambda b, g, s, hh=hh: (b * ns + at(s), 3 * (hp * g + hh) + 1)))
    heads = pl.BlockSpec((ts, hp * HEAD), lambda b, g, s: (b * ns + at(s), g))
    rowspec = pl.BlockSpec((None, hp, nc, 1, CHUNK), lambda b, g, s: (b, g, at(s), 0, 0))
    stspec = pl.BlockSpec((None, hp, nc, HEAD, HEAD), lambda b, g, s: (b, g, at(s), 0, 0))
    return qk, heads, rowspec, stspec


def gdn_inter_fwd(qkvn, u_hat, w, gcr5, B, S, H):
    T = B * S
    N = S // CHUNK
    hp = INTER_HEADS if H % INTER_HEADS == 0 else 1
    hs = range(hp)
    ts = _tile(S, INTER_ROWS, CHUNK)
    ns, nc = S // ts, ts // CHUNK

    def body(*refs):
        qk_refs, (uh_ref, w_ref, gr_ref, o_ref, st_ref, s_scr) = refs[:2 * hp], refs[2 * hp:]

        @pl.when(pl.program_id(2) == 0)
        def _():
            s_scr[...] = jnp.zeros_like(s_scr)

        def step(n, c):
            rows = pl.ds(pl.multiple_of(n * CHUNK, CHUNK), CHUNK)
            st = tuple(s_scr[hh] for hh in hs)
            for hh in hs:
                st_ref[hh, n] = st[hh]
            o, new = _inter_fn(tuple(qk_refs[2 * hh][rows, :] for hh in hs), tuple(qk_refs[2 * hh + 1][rows, :] for hh in hs),
                               tuple(uh_ref[rows, hh * HEAD:(hh + 1) * HEAD] for hh in hs),
                               tuple(w_ref[rows, hh * HEAD:(hh + 1) * HEAD] for hh in hs),
                               tuple(gr_ref[hh, n] for hh in hs), st, _RawOps)
            for hh in hs:
                o_ref[rows, hh * HEAD:(hh + 1) * HEAD] = o[hh]
                s_scr[hh] = new[hh]
            return c

        lax.fori_loop(0, nc, step, 0)

    qk, heads, rowspec, stspec = _inter_specs(ts, ns, hp, backward=False)
    return pl.pallas_call(
        body, name="gdn_inter_fwd", grid=(B, H // hp, ns),
        in_specs=qk + [heads, heads, rowspec], out_specs=[heads, stspec],
        out_shape=[jax.ShapeDtypeStruct((T, H * HEAD), F32), jax.ShapeDtypeStruct((B, H, N, HEAD, HEAD), F32)],
        scratch_shapes=[pltpu.VMEM((hp, HEAD, HEAD), F32)],
        compiler_params=_params("parallel", "parallel", "arbitrary"),
    )(*([qkvn] * (2 * hp)), u_hat, w, gcr5)


def gdn_inter_bwd(qkvn, u_hat, w, gcr5, states, d_o, B, S, H, rider=None):
    T = B * S
    N = S // CHUNK
    hp = INTER_HEADS if H % INTER_HEADS == 0 else 1
    hs = range(hp)
    ts = _tile(S, INTER_ROWS, CHUNK)
    ns, nc = S // ts, ts // CHUNK

    def body(*refs):
        qk_refs = refs[:2 * hp]
        uh_ref, w_ref, gr_ref, st_ref, do_ref, dq_ref, dk_ref, duh_ref, dw_ref, dgr_ref, ds_scr = refs[2 * hp:]

        @pl.when(pl.program_id(2) == 0)
        def _():
            ds_scr[...] = jnp.zeros_like(ds_scr)

        cols = [slice(hh * HEAD, (hh + 1) * HEAD) for hh in hs]

        def step(i, c):
            n = nc - 1 - i
            rows = pl.ds(pl.multiple_of(n * CHUNK, CHUNK), CHUNK)
            _, vjp = jax.vjp(functools.partial(_inter_fn, ops=_DiffOps),
                             tuple(qk_refs[2 * hh][rows, :] for hh in hs), tuple(qk_refs[2 * hh + 1][rows, :] for hh in hs),
                             tuple(uh_ref[rows, cols[hh]] for hh in hs), tuple(w_ref[rows, cols[hh]] for hh in hs),
                             tuple(gr_ref[hh, n] for hh in hs), tuple(st_ref[hh, n] for hh in hs))
            dq, dk, duh, dw, dgr, ds = vjp((tuple(do_ref[rows, cols[hh]] for hh in hs), tuple(ds_scr[hh] for hh in hs)))
            for hh in hs:
                dq_ref[rows, cols[hh]] = dq[hh]
                dk_ref[rows, cols[hh]] = dk[hh]
                duh_ref[rows, cols[hh]] = duh[hh]
                dw_ref[rows, cols[hh]] = dw[hh]
                dgr_ref[hh, n] = dgr[hh]
                ds_scr[hh] = ds[hh]
            return c

        lax.fori_loop(0, nc, step, 0)

    qk, heads, rowspec, stspec = _inter_specs(ts, ns, hp, backward=True)
    hshape = jax.ShapeDtypeStruct((T, H * HEAD), F32)
    return _hosted_call(
        body, rider, name="gdn_inter_bwd", grid=(B, H // hp, ns),
        in_specs=qk + [heads, heads, rowspec, stspec, heads],
        out_specs=[heads, heads, heads, heads, rowspec],
        out_shape=[hshape, hshape, hshape, hshape, jax.ShapeDtypeStruct((B, H, N, 1, CHUNK), F32)],
        scratch_shapes=[pltpu.VMEM((hp, HEAD, HEAD), F32)], semantics=("parallel", "parallel", "arbitrary"),
    )(*([qkvn] * (2 * hp)), u_hat, w, gcr5, states, d_o)


def gdn_post_fwd(o, p_gz, g, H):
    T = o.shape[0]
    tm = _tile(T, 1024, 8)

    def body(o_ref, z_ref, g_ref, y_ref):
        ov, z = o_ref[...], z_ref[...]
        r = lax.rsqrt(jnp.mean(ov * ov, axis=-1, keepdims=True) + EPS)
        y_ref[...] = (ov * r * g_ref[...] * z * _sigmoid(z)).astype(BF16)

    blk = pl.BlockSpec((tm, HEAD), lambda i, h: (i, h))
    return pl.pallas_call(
        body, name="gdn_post_fwd", grid=(T // tm, H), in_specs=[blk, blk, pl.BlockSpec((1, HEAD), lambda i, h: (0, 0))],
        out_specs=blk, out_shape=jax.ShapeDtypeStruct((T, H * HEAD), BF16),
        compiler_params=_params("parallel", "parallel"),
    )(o, p_gz, g)


def gdn_post_bwd(o, p_gz, g, dy, H):
    T = o.shape[0]
    tm = _tile(T, 1024, 8)

    def body(o_ref, z_ref, g_ref, dy_ref, do_ref, dz_ref, dg_ref):
        @pl.when((pl.program_id(0) == 0) & (pl.program_id(1) == 0))
        def _():
            dg_ref[...] = jnp.zeros_like(dg_ref)

        ov, z, d = o_ref[...], z_ref[...], dy_ref[...]
        r = lax.rsqrt(jnp.mean(ov * ov, axis=-1, keepdims=True) + EPS)
        xh = ov * r
        sg = _sigmoid(z)
        sz = z * sg
        d_n = d * sz
        dz_ref[...] = (d * xh * g_ref[...] * (sg + z * sg * (1.0 - sg))).astype(BF16)
        dg_ref[...] += jnp.sum(d_n * xh, axis=0, keepdims=True)
        dxh = d_n * g_ref[...]
        do_ref[...] = r * (dxh - xh * jnp.mean(dxh * xh, axis=-1, keepdims=True))

    blk = pl.BlockSpec((tm, HEAD), lambda i, h: (i, h))
    vec = pl.BlockSpec((1, HEAD), lambda i, h: (0, 0))
    return pl.pallas_call(
        body, name="gdn_post_bwd", grid=(T // tm, H), in_specs=[blk, blk, vec, blk], out_specs=[blk, blk, vec],
        out_shape=[jax.ShapeDtypeStruct((T, H * HEAD), F32), jax.ShapeDtypeStruct((T, H * HEAD), BF16),
                   jax.ShapeDtypeStruct((1, HEAD), F32)],
        compiler_params=_params("arbitrary", "arbitrary"),
    )(o, p_gz, g, dy)


def adamw(w, g, m, v, name):
    shape = w.shape
    lead = (None,) * (w.ndim - 2)
    zeros = (0,) * (w.ndim - 2)
    R, C = shape[-2:]
    g2 = g.reshape(R, C)
    tr, tc = _tile(R, 128, 8), C
    if tr % 8 and R > 8:
        tr, tc = R, _tile(C, 128)

    def body(w_ref, g_ref, m_ref, v_ref, d_ref, nm_ref, nv_ref):
        gv = g_ref[...]
        nm = ADAM_B1 * m_ref[...] + (1.0 - ADAM_B1) * gv
        nv = ADAM_B2 * v_ref[...] + (1.0 - ADAM_B2) * (gv * gv)
        m_hat = nm / (1.0 - ADAM_B1 ** ADAM_STEP)
        v_hat = nv / (1.0 - ADAM_B2 ** ADAM_STEP)
        d_ref[...] = -ADAM_LR * (m_hat / (jnp.sqrt(v_hat) + ADAM_EPS) + ADAM_WD * w_ref[...])
        nm_ref[...] = nm
        nv_ref[...] = nv

    blk = pl.BlockSpec(lead + (tr, tc), lambda i, j: zeros + (i, j))
    gblk = pl.BlockSpec((tr, tc), lambda i, j: (i, j))
    sh = jax.ShapeDtypeStruct(shape, F32)
    return pl.pallas_call(
        body, name=name, grid=(R // tr, C // tc), in_specs=[blk, gblk, blk, blk], out_specs=[blk] * 3, out_shape=[sh] * 3,
        compiler_params=_params("parallel", "parallel"),
    )(w, g2, m, v)


def _place():
    x, y, c = lax.axis_index("x"), lax.axis_index("y"), lax.axis_index("c")
    chips = [(1 - x, y), (x, 1 - y), (1 - x, 1 - y)]
    return x, y, c, chips


_HBM = pl.BlockSpec(memory_space=pltpu.HBM)


def allgather_weights(packs):
    n = len(packs)

    def body(*refs):
        in_refs, out_refs, (send_sems, recv_sems) = refs[:n], refs[n:2 * n], refs[2 * n:]
        x, y, c, chips = _place()
        me_s = 2 * x + y
        me, sibling = (x, y, c), (x, y, 1 - c)
        shards = [2 * chip[0] + chip[1] for chip in chips]

        def copy(a, k, shard, half, to, src=None):
            dst = out_refs[a].at[shard, half]
            return pltpu.make_async_remote_copy(src_ref=dst if src is None else src, dst_ref=dst,
                                                send_sem=send_sems.at[6 * a + k], recv_sem=recv_sems.at[6 * a + k],
                                                device_id=to, device_id_type=MESH)

        first = [copy(a, j, me_s, c, (*chip, c), src=in_refs[a].at[c]) for a in range(n) for j, chip in enumerate(chips)]
        for cp in first:
            cp.start()
        passed = []
        for a in range(n):
            for j in range(3):
                copy(a, j, shards[j], c, me).wait_recv()
                passed.append(copy(a, 3 + j, shards[j], c, sibling))
                passed[-1].start()
        for a in range(n):
            for j in range(3):
                copy(a, 3 + j, shards[j], 1 - c, me).wait_recv()
        for cp in first + passed:
            cp.wait_send()

    return pl.pallas_call(
        body, name="allgather_weights", in_specs=[_HBM] * n, out_specs=[_HBM] * n,
        out_shape=[jax.ShapeDtypeStruct((N_CHIP,) + p.shape, p.dtype) for p in packs],
        scratch_shapes=[pltpu.SemaphoreType.DMA((6 * n,)), pltpu.SemaphoreType.DMA((6 * n,))],
    )(*packs)


class _Rider:
    def __init__(self, inputs, out_shapes, n_sems, sends, recvs, aliases=None):
        self.inputs, self.out_shapes, self.n_sems = list(inputs), list(out_shapes), n_sems
        self.sends, self.recvs, self.aliases = sends, recvs, aliases or {}

    def start(self, *refs):
        for cp in self.sends(*refs):
            cp.start()

    def wait(self, *refs):
        for cp in self.recvs(*refs):
            cp.wait_recv()
        for cp in self.sends(*refs):
            cp.wait_send()


def _remote(src, dst, send_sems, recv_sems, k, to):
    return pltpu.make_async_remote_copy(src_ref=src, dst_ref=dst, send_sem=send_sems.at[k], recv_sem=recv_sems.at[k],
                                        device_id=to, device_id_type=MESH)


def _run_alone(rider, name):
    ri = len(rider.inputs)

    def body(*refs):
        ins, outs, (send_sems, recv_sems) = refs[:ri], refs[ri:-2], refs[-2:]
        rider.start(ins, outs, send_sems, recv_sems)
        rider.wait(ins, outs, send_sems, recv_sems)

    return pl.pallas_call(
        body, name=name, in_specs=[_HBM] * ri, out_specs=[_HBM] * len(rider.out_shapes), out_shape=rider.out_shapes,
        scratch_shapes=[pltpu.SemaphoreType.DMA((rider.n_sems,))] * 2, input_output_aliases=rider.aliases,
    )(*rider.inputs)


def _hosted_call(body, rider, *, name, grid, in_specs, out_specs, out_shape, scratch_shapes, semantics):
    if rider is None:
        return pl.pallas_call(body, name=name, grid=grid, in_specs=in_specs, out_specs=out_specs, out_shape=out_shape,
                              scratch_shapes=scratch_shapes, compiler_params=_params(*semantics))
    n_in, n_out, n_scr = len(in_specs), len(out_specs), len(scratch_shapes)
    ri, ro = len(rider.inputs), len(rider.out_shapes)

    def hosted(*refs):
        parts, p = [], 0
        for cnt in (n_in, ri, n_out, ro, n_scr, 2):
            parts.append(refs[p:p + cnt])
            p += cnt
        ins, rins, outs, routs, scr, (send_sems, recv_sems) = parts
        first = functools.reduce(jnp.logical_and, [pl.program_id(a) == 0 for a in range(len(grid))])
        last = functools.reduce(jnp.logical_and, [pl.program_id(a) == grid[a] - 1 for a in range(len(grid))])

        @pl.when(first)
        def _():
            rider.start(rins, routs, send_sems, recv_sems)

        body(*ins, *outs, *scr)

        @pl.when(last)
        def _():
            rider.wait(rins, routs, send_sems, recv_sems)

    call = pl.pallas_call(
        hosted, name=name, grid=grid, in_specs=list(in_specs) + [_HBM] * ri, out_specs=list(out_specs) + [_HBM] * ro,
        out_shape=list(out_shape) + rider.out_shapes,
        scratch_shapes=list(scratch_shapes) + [pltpu.SemaphoreType.DMA((rider.n_sems,))] * 2,
        input_output_aliases={n_in + i: n_out + o for i, o in rider.aliases.items()},
        compiler_params=_params(*(("arbitrary",) * len(grid))))

    def run(*args):
        res = call(*args, *rider.inputs)
        return res[:n_out], res[n_out:]

    return run


def _ride_gather_ici(packs):
    n = len(packs)

    def sends(ins, outs, send_sems, recv_sems):
        x, y, c, chips = _place()
        return [_remote(ins[a].at[c], outs[a].at[2 * x + y, c], send_sems, recv_sems, 3 * a + j, (*chip, c))
                for a in range(n) for j, chip in enumerate(chips)]

    def recvs(ins, outs, send_sems, recv_sems):
        x, y, c, chips = _place()
        return [_remote(ins[a].at[c], outs[a].at[2 * chip[0] + chip[1], c], send_sems, recv_sems, 3 * a + j, (x, y, c))
                for a in range(n) for j, chip in enumerate(chips)]

    return _Rider(packs, [jax.ShapeDtypeStruct((N_CHIP,) + p.shape, p.dtype) for p in packs], 3 * n, sends, recvs)


def _ride_gather_d2d(gathered):
    n = len(gathered)

    def copies(landing_half, to):
        def build(ins, outs, send_sems, recv_sems):
            x, y, c, chips = _place()
            return [_remote(ins[a].at[2 * chip[0] + chip[1], c], outs[a].at[2 * chip[0] + chip[1], landing_half(c)],
                            send_sems, recv_sems, 3 * a + j, to(x, y, c))
                    for a in range(n) for j, chip in enumerate(chips)]
        return build

    return _Rider(gathered, [jax.ShapeDtypeStruct(g.shape, g.dtype) for g in gathered], 3 * n,
                  copies(lambda c: c, lambda x, y, c: (x, y, 1 - c)), copies(lambda c: 1 - c, lambda x, y, c: (x, y, c)),
                  aliases={a: a for a in range(n)})


def _ride_exchange(gs):
    n = len(gs)

    def copies(ins, outs, send_sems, recv_sems):
        x, y, c, _ = _place()
        return [_remote(ins[a].at[1 - c], outs[a], send_sems, recv_sems, a, (x, y, 1 - c)) for a in range(n)]

    return _Rider(gs, [jax.ShapeDtypeStruct(g.shape[1:], g.dtype) for g in gs], n, copies, copies)


def _ride_scatter(b16s):
    n = len(b16s)

    def sends(ins, outs, send_sems, recv_sems):
        x, y, c, chips = _place()
        return [_remote(ins[a].at[2 * chip[0] + chip[1]], outs[a].at[2 * x + y], send_sems, recv_sems, 3 * a + j, (*chip, c))
                for a in range(n) for j, chip in enumerate(chips)]

    def recvs(ins, outs, send_sems, recv_sems):
        x, y, c, chips = _place()
        return [_remote(ins[a].at[2 * x + y], outs[a].at[2 * chip[0] + chip[1]], send_sems, recv_sems, 3 * a + j, (x, y, c))
                for a in range(n) for j, chip in enumerate(chips)]

    return _Rider(b16s, [jax.ShapeDtypeStruct(b.shape, b.dtype) for b in b16s], 3 * n, sends, recvs)


def _slab_tile(r, cols):
    tr = _tile(r, 256, 16)
    if tr % 16 == 0:
        return tr, cols
    return r, _tile(cols, 128)


def add_halves(g, got, idx, name):
    _, ns, r, cols = g.shape
    tr, tc = _slab_tile(r, cols)

    def body(idx_ref, a_ref, b_ref, o32_ref, o16_ref):
        s = a_ref[...] + b_ref[...]
        o32_ref[...] = s
        o16_ref[...] = s.astype(BF16)

    blk = pl.BlockSpec((None, tr, tc), lambda s, i, j, idx_ref: (s, i, j))
    return pl.pallas_call(
        body, name=name,
        grid_spec=pltpu.PrefetchScalarGridSpec(
            num_scalar_prefetch=1, grid=(ns, r // tr, cols // tc),
            in_specs=[pl.BlockSpec((None, None, tr, tc), lambda s, i, j, idx_ref: (idx_ref[0], s, i, j)), blk],
            out_specs=[blk, blk]),
        out_shape=[jax.ShapeDtypeStruct((ns, r, cols), F32), jax.ShapeDtypeStruct((ns, r, cols), BF16)],
        compiler_params=_params("parallel", "parallel", "parallel"),
    )(idx, g, got)


def add_chips(a32, got16, idx, name):
    ns, r, cols = a32.shape
    tr, tc = _slab_tile(r, cols)

    def body(idx_ref, a_ref, r1_ref, r2_ref, r3_ref, o_ref):
        o_ref[...] = ((a_ref[...] + r1_ref[...].astype(F32)) + r2_ref[...].astype(F32)) + r3_ref[...].astype(F32)

    def slab(k):
        return pl.BlockSpec((None, tr, tc), lambda i, j, idx_ref: ((idx_ref[1] + k) % ns, i, j))

    return pl.pallas_call(
        body, name=name,
        grid_spec=pltpu.PrefetchScalarGridSpec(
            num_scalar_prefetch=1, grid=(r // tr, cols // tc), in_specs=[slab(0), slab(1), slab(2), slab(3)],
            out_specs=pl.BlockSpec((tr, tc), lambda i, j, idx_ref: (i, j))),
        out_shape=jax.ShapeDtypeStruct((r, cols), F32),
        compiler_params=_params("parallel", "parallel"),
    )(idx, a32, got16, got16, got16)


def share_halves(halves):
    n = len(halves)

    def body(*refs):
        in_refs, out_refs, (send_sems, recv_sems) = refs[:n], refs[n:2 * n], refs[2 * n:]
        x, y, c, _ = _place()
        cps = [pltpu.make_async_remote_copy(src_ref=in_refs[a], dst_ref=out_refs[a], send_sem=send_sems.at[a],
                                            recv_sem=recv_sems.at[a], device_id=(x, y, 1 - c), device_id_type=MESH)
               for a in range(n)]
        for cp in cps:
            cp.start()
        for cp in cps:
            cp.wait()

    return pl.pallas_call(
        body, name="share_halves", in_specs=[_HBM] * n, out_specs=[_HBM] * n,
        out_shape=[jax.ShapeDtypeStruct(h.shape, F32) for h in halves],
        scratch_shapes=[pltpu.SemaphoreType.DMA((n,)), pltpu.SemaphoreType.DMA((n,))],
    )(*halves)


def allreduce_small(v):
    R, _ = v.shape

    def body(in_ref, out_ref, slots, send_sems, recv_sems):
        x, y, c, _ = _place()
        me = 4 * x + 2 * y + c
        slots[me] = in_ref[...]
        cps = []
        for k in range(1, N_DEV):
            to = (x ^ (k >> 2), y ^ ((k >> 1) & 1), c ^ (k & 1))
            cps.append(pltpu.make_async_remote_copy(src_ref=in_ref, dst_ref=slots.at[me], send_sem=send_sems.at[k - 1],
                                                    recv_sem=recv_sems.at[k - 1], device_id=to, device_id_type=MESH))
        for cp in cps:
            cp.start()
        for k in range(1, N_DEV):
            frm = 4 * (x ^ (k >> 2)) + 2 * (y ^ ((k >> 1) & 1)) + (c ^ (k & 1))
            pltpu.make_async_remote_copy(src_ref=in_ref, dst_ref=slots.at[frm], send_sem=send_sems.at[k - 1],
                                         recv_sem=recv_sems.at[k - 1], device_id=(x, y, c), device_id_type=MESH).wait_recv()
        for cp in cps:
            cp.wait_send()
        acc = slots[0]
        for d in range(1, N_DEV):
            acc = acc + slots[d]
        out_ref[...] = acc

    vm = pl.BlockSpec(memory_space=pltpu.VMEM)
    return pl.pallas_call(
        body, name="allreduce_small", in_specs=[vm], out_specs=vm, out_shape=jax.ShapeDtypeStruct((R, ROW), F32),
        scratch_shapes=[pltpu.VMEM((N_DEV, R, ROW), F32), pltpu.SemaphoreType.DMA((N_DEV - 1,)),
                        pltpu.SemaphoreType.DMA((N_DEV - 1,))],
    )(v)


def _rows_of(n, unit=16):
    return -(-n // (unit * ROW)) * unit


def _pack_rows(items, total_rows, dtype, unit=16):
    parts = []
    used = 0
    for a in items:
        flat = a.reshape(-1)
        r = _rows_of(flat.shape[0], unit)
        flat = jnp.pad(flat, (0, r * ROW - flat.shape[0]))
        parts.append(flat.reshape(r, ROW))
        used += r
    if total_rows > used:
        parts.append(jnp.zeros((total_rows - used, ROW), dtype))
    return jnp.concatenate(parts, axis=0)


def _unpack_rows(buf, shapes, unit=16):
    lead = buf.shape[:-2]
    out = []
    off = 0
    for shp in shapes:
        n = math.prod(shp)
        r = _rows_of(n, unit)
        piece = buf[..., off:off + r, :].reshape(*lead, r * ROW)[..., :n].reshape(*lead, *shp)
        out.append(piece)
        off += r
    return out


def _interleave_heads(w, H):
    lead = w.shape[:-1]
    return w.reshape(*lead, 3, H, HEAD).swapaxes(-3, -2).reshape(*lead, 3 * H * HEAD)


def _deinterleave_heads(w, H):
    lead = w.shape[:-1]
    return w.reshape(*lead, H, 3, HEAD).swapaxes(-3, -2).reshape(*lead, 3 * H * HEAD)


def _interleave_head_rows(w, H):
    return w.reshape(3, H, HEAD, w.shape[-1]).swapaxes(0, 1).reshape(3 * H * HEAD, w.shape[-1])


def _deinterleave_head_rows(w, H):
    return w.reshape(H, 3, HEAD, w.shape[-1]).swapaxes(0, 1).reshape(3 * H * HEAD, w.shape[-1])


def kernel(x, norm_mix, w_in, fox_f_bias, gdn_conv_w, gdn_a_log, gdn_dt_bias, gdn_norm, w_branch_fox, w_branch_gdn, w_out, norm_ffn, w_up, ffn_conv_w, w_down, norm_final, loss_target, m_norm_mix, m_w_in, m_fox_f_bias, m_gdn_conv_w, m_gdn_a_log, m_gdn_dt_bias, m_gdn_norm, m_w_branch_fox, m_w_branch_gdn, m_w_out, m_norm_ffn, m_w_up, m_ffn_conv_w, m_w_down, m_norm_final, v_norm_mix, v_w_in, v_fox_f_bias, v_gdn_conv_w, v_gdn_a_log, v_gdn_dt_bias, v_gdn_norm, v_w_branch_fox, v_w_branch_gdn, v_w_out, v_norm_ffn, v_w_up, v_ffn_conv_w, v_w_down, v_norm_final):
    B, S, D = x.shape
    T = B * S
    H = D // HEAD
    N = S // CHUNK
    FF = w_down.shape[1] * N_CHIP
    d_in = 9 * D + 3 * H
    assert w_in.shape[2] * N_CHIP == d_in and 3 * H <= 128

    cidx = lax.axis_index("c").astype(jnp.int32)
    sidx = (2 * lax.axis_index("x") + lax.axis_index("y")).astype(jnp.int32)
    idx = jnp.stack([cidx, sidx])

    rowed = [w_branch_fox[0], w_branch_gdn[0], w_out[0], w_down[0]]
    convs = [gdn_conv_w[0], ffn_conv_w[0]]
    rowed_shapes = [a.shape for a in rowed]
    conv_shapes = [a.shape + (2,) for a in convs]
    pad_rows = lambda shapes: -(-sum(_rows_of(math.prod(s)) for s in shapes) // 256) * 128
    Rh, Rc = pad_rows(rowed_shapes), pad_rows(conv_shapes)
    halves = lambda a: a.reshape(2, a.shape[0] // 2, a.shape[1])
    c_in = w_in.shape[2]
    packs_a = [w_in[0].T.astype(BF16).reshape(c_in, 2, D // 2).transpose(1, 0, 2),
               halves(_pack_rows([lax.bitcast_convert_type(a, BF16) for a in convs], 2 * Rc, BF16))]
    packs_b = [halves(w_up[0].astype(BF16)), halves(_pack_rows([a.astype(BF16) for a in rowed], 2 * Rh, BF16))]
    own = lambda gs, ps: [lax.dynamic_update_slice(g, p[None], (sidx, 0, 0, 0)) for g, p in zip(gs, ps)]
    by_cols = lambda g: g.transpose(1, 2, 0, 3).reshape(2 * g.shape[2], N_CHIP * g.shape[3])
    cat_cols = lambda p: jnp.concatenate([p[i] for i in range(N_CHIP)], axis=-1)
    cat_rows = lambda p: p.reshape(-1, p.shape[-1])
    g_in, g_conv = own(allgather_weights(packs_a), packs_a)
    W_inT = g_in.transpose(0, 2, 1, 3).reshape(N_CHIP * c_in, D)
    conv_parts = _unpack_rows(g_conv.reshape(N_CHIP, 2 * Rc, ROW), conv_shapes)
    gconv = cat_cols(lax.bitcast_convert_type(conv_parts[0], F32))
    fconv = cat_cols(lax.bitcast_convert_type(conv_parts[1], F32))

    o1, o2 = 3 * D, 3 * D + H
    o3, o4, o5, o6 = o2 + 3 * D, o2 + 3 * D + H, o2 + 3 * D + 2 * H, o2 + 4 * D + 2 * H
    W_foxT = _interleave_head_rows(W_inT[:o1], H)
    W_gqkvT = _interleave_head_rows(W_inT[o2:o3], H)
    W_gzT = W_inT[o5:o6]
    W_gatesT = W_inT[o6:]
    W_smallT = jnp.concatenate([W_inT[o1:o2], W_inT[o3:o5], jnp.zeros((128 - 3 * H, D), BF16)], axis=0)
    gconv_i = _interleave_heads(gconv, H)
    fconv_g, fconv_v = fconv[:, :FF], fconv[:, FF:]
    prm = jnp.zeros((8, 128), F32)
    prm = prm.at[0, 0:H].set(fox_f_bias[0]).at[0, H:2 * H].set(gdn_dt_bias[0]).at[1, H:2 * H].set(gdn_a_log[0])

    x2 = x.reshape(T, D)
    tgt = loss_target.reshape(T, D)

    hn1 = rmsnorm_fwd(x2, norm_mix, "rmsnorm_mix")
    p_fox = matmul(hn1, W_foxT, "nt", "proj_fox", out_dtype=BF16)
    p_gqkv = matmul(hn1, W_gqkvT, "nt", "proj_gqkv")
    p_gz = matmul(hn1, W_gzT, "nt", "proj_gz")
    p_gates = matmul(hn1, W_gatesT, "nt", "proj_gates")
    p_small = matmul(hn1, W_smallT, "nt", "proj_small")

    sm = small_fwd(p_small, prm, B, S, H)
    heads = lambda a: a.reshape(B, S, H).transpose(0, 2, 1)
    c_bhs, gc_bhs, beta_bhs = heads(sm[:, 0:H]), heads(sm[:, H:2 * H]), heads(sm[:, 2 * H:3 * H])
    c_col, c_row = c_bhs[..., None], c_bhs[:, :, None, :]
    gcr5 = gc_bhs.reshape(B, H, N, 1, CHUNK)
    gcr_u = gc_bhs.reshape(B, H, N // PAIR, 1, PAIR * CHUNK)
    betar_u = beta_bhs.reshape(B, H, N // PAIR, 1, PAIR * CHUNK)

    (o_fox, o_fox16, lse), arriving = fox_fwd(p_fox, c_col, c_row, B, S, H, rider=_ride_gather_ici(packs_b))
    qkvn = gdn_prep_fwd(p_gqkv, gconv_i, B, S, H)
    (u_hat, w_t, t_inv), arrived = gdn_intra_fwd(qkvn, betar_u, gcr_u, B, S, H, rider=_ride_gather_d2d(arriving))
    g_up, g_rowed = own(arrived, packs_b)
    W_up = by_cols(g_up)
    W_up_g, W_up_v = W_up[:, :FF], W_up[:, FF:]
    W_bf, W_bg, W_out, W_down = (cat_rows(p) for p in _unpack_rows(g_rowed.reshape(N_CHIP, 2 * Rh, ROW), rowed_shapes))
    o_gdn, states = gdn_inter_fwd(qkvn, u_hat, w_t, gcr5, B, S, H)
    y_gdn = gdn_post_fwd(o_gdn, p_gz, gdn_norm, H)
    bf_ = matmul(o_fox16, W_bf, "nn", "branch_fox")
    bg_ = matmul(y_gdn, W_bg, "nn", "branch_gdn")
    y = merge_fwd(p_gates, bf_, bg_)
    h1 = matmul(y, W_out, "nn", "out_proj", add=x2)
    hn2 = rmsnorm_fwd(h1, norm_ffn, "rmsnorm_ffn")
    up_g = matmul(hn2, W_up_g, "nn", "up_gate")
    up_v = matmul(hn2, W_up_v, "nn", "up_val")
    act = ffn_gate_fwd(up_g, up_v, fconv_g, fconv_v, B, S)
    h2 = matmul(act, W_down, "nn", "down_proj", add=h1)
    loss_cols, dh2, dh2_16, d_norm_final = final_loss(h2, norm_final.reshape(1, D), tgt)
    loss = lax.psum(0.5 * jnp.sum(loss_cols) / D, ("x", "y", "c"))

    d_act = matmul(dh2_16, W_down, "nt", "d_act")
    dW_down = matmul(act, dh2_16, "tn", "dw_down")
    d_upg, d_upv, d_fconv_g, d_fconv_v = ffn_gate_bwd(up_g, up_v, fconv_g, fconv_v, d_act, B, S)
    d_hn2 = matmul(d_upg, W_up_g, "nt", "d_hn2_g")
    d_hn2 = matmul(d_upv, W_up_v, "nt", "d_hn2_v", add=d_hn2)
    dW_up = jnp.concatenate([matmul(hn2, d_upg, "tn", "dw_up_g"), matmul(hn2, d_upv, "tn", "dw_up_v")], axis=1)
    dh1, dh1_16, d_norm_ffn = rmsnorm_bwd(h1, norm_ffn, d_hn2, dh2, "rmsnorm_ffn_bwd")
    d_y = matmul(dh1_16, W_out, "nt", "d_y")
    dW_out = matmul(y, dh1_16, "tn", "dw_out")
    d_bf, d_bg, d_gates = merge_bwd(p_gates, bf_, bg_, d_y)
    d_ofox = matmul(d_bf, W_bf, "nt", "d_ofox")
    dW_bf = matmul(o_fox16, d_bf, "tn", "dw_bf")
    d_ygdn = matmul(d_bg, W_bg, "nt", "d_ygdn")
    dW_bg = matmul(y_gdn, d_bg, "tn", "dw_bg")

    d_fconv = jnp.concatenate([d_fconv_g, d_fconv_v], axis=1)
    col_shard = lambda g, s: g[:, s * (g.shape[1] // N_CHIP):(s + 1) * (g.shape[1] // N_CHIP)]
    row_shard = lambda g, s: g[s * (g.shape[0] // N_CHIP):(s + 1) * (g.shape[0] // N_CHIP)]
    shard_items = lambda s: [row_shard(dW_bf, s), row_shard(dW_bg, s), row_shard(dW_out, s), row_shard(dW_down, s),
                             col_shard(d_fconv, s)]
    g_shapes = [a.shape for a in shard_items(0)]
    assert sum(_rows_of(math.prod(s)) for s in g_shapes) <= 2 * Rh
    to_slabs = lambda g: g.reshape(2, g.shape[0] // 2, N_CHIP, g.shape[1] // N_CHIP).transpose(0, 2, 1, 3)
    gpacks_b = [to_slabs(dW_up),
                jnp.stack([_pack_rows(shard_items(s), 2 * Rh, F32).reshape(2, Rh, ROW) for s in range(N_CHIP)], axis=1)]
    (d_pfox, d_ccol, d_crow), gots_b = fox_bwd(p_fox, c_col, c_row, o_fox, lse, d_ofox, B, S, H,
                                              rider=_ride_exchange(gpacks_b))
    sums_b = [add_halves(g, got, idx, "add_halves_b%d" % i) for i, (g, got) in enumerate(zip(gpacks_b, gots_b))]

    d_ogdn, d_gz, d_gdn_norm = gdn_post_bwd(o_gdn, p_gz, gdn_norm, d_ygdn, H)
    (dq_i, dk_i, d_uh, d_wt, dgcr_a), got16_b = gdn_inter_bwd(qkvn, u_hat, w_t, gcr5, states, d_ogdn, B, S, H,
                                                             rider=_ride_scatter([s16 for _, s16 in sums_b]))
    mine_b = [add_chips(s32, g16, idx, "add_chips_b%d" % i) for i, ((s32, _), g16) in enumerate(zip(sums_b, got16_b))]
    d_qkvn, d_betar5, dgcr_b = gdn_intra_bwd(qkvn, betar_u, gcr_u, t_inv, d_uh, d_wt, dq_i, dk_i, B, S, H)
    d_pgqkv, d_gconv_i = gdn_prep_bwd(p_gqkv, gconv_i, d_qkvn, B, S, H)

    tokens = lambda a: a.reshape(B, H, S).transpose(0, 2, 1).reshape(T, H)
    d_gc = dgcr_a.reshape(B, H, S) + dgcr_b.reshape(B, H, S)
    d_sm = jnp.concatenate([tokens(d_ccol.reshape(B, H, S) + d_crow.reshape(B, H, S)), tokens(d_gc), tokens(d_betar5.reshape(B, H, S)),
                            jnp.zeros((T, 128 - 3 * H), F32)], axis=1)
    d_psmall, d_prm = small_bwd(p_small, prm, d_sm, B, S, H)

    d_hn1 = matmul(d_pfox, W_foxT, "nn", "d_hn1_fox")
    d_hn1 = matmul(d_pgqkv, W_gqkvT, "nn", "d_hn1_gqkv", add=d_hn1)
    d_hn1 = matmul(d_gz, W_gzT, "nn", "d_hn1_gz", add=d_hn1)
    d_hn1 = matmul(d_gates, W_gatesT, "nn", "d_hn1_gates", add=d_hn1)
    d_hn1 = matmul(d_psmall, W_smallT, "nn", "d_hn1_small", add=d_hn1)
    dW_foxT = matmul(d_pfox, hn1, "tn", "dw_fox")
    dW_gqkvT = matmul(d_pgqkv, hn1, "tn", "dw_gqkv")
    dW_gzT = matmul(d_gz, hn1, "tn", "dw_gz")
    dW_gatesT = matmul(d_gates, hn1, "tn", "dw_gates")
    dW_smallT = matmul(d_psmall, hn1, "tn", "dw_small")
    grad_x, _, d_norm_mix = rmsnorm_bwd(x2, norm_mix, d_hn1, dh1, "rmsnorm_mix_bwd")

    dW_inT = jnp.concatenate([_deinterleave_head_rows(dW_foxT, H), dW_smallT[0:H], _deinterleave_head_rows(dW_gqkvT, H),
                              dW_smallT[H:3 * H], dW_gzT, dW_gatesT], axis=0)
    d_gconv = _deinterleave_heads(d_gconv_i, H)

    gpack_a = [dW_inT.reshape(N_CHIP, c_in, 2, D // 2).transpose(2, 0, 1, 3)]
    gots_a = _run_alone(_ride_exchange(gpack_a), "exchange_halves")
    sums_a = [add_halves(gpack_a[0], gots_a[0], idx, "add_halves_a")]
    got16_a = _run_alone(_ride_scatter([sums_a[0][1]]), "scatter_chips")
    mine = [add_chips(sums_a[0][0], got16_a[0], idx, "add_chips_a")] + mine_b
    others = share_halves(mine)
    g_w_inT, g_up, g_rows = (jnp.concatenate([jnp.where(cidx == 0, h, o), jnp.where(cidx == 0, o, h)], axis=ax)
                             for h, o, ax in zip(mine, others, (1, 0, 0)))
    g_w_in = g_w_inT.T
    g_bf, g_bg, g_out, g_down, g_fconv = _unpack_rows(g_rows, g_shapes)

    small_items = [d_norm_mix, d_norm_ffn, d_norm_final, d_gdn_norm, d_prm, d_gconv]
    small_shapes = [a.shape for a in small_items]
    sv = allreduce_small(_pack_rows(small_items, 0, F32, unit=8))
    g_norm_mix, g_norm_ffn, g_norm_final, g_gdn_norm, g_prm, g_gconv_all = _unpack_rows(sv, small_shapes, unit=8)
    g_norm_final = g_norm_final.reshape(D)
    g_fbias, g_dtb, g_alog = g_prm[0:1, 0:H], g_prm[0:1, H:2 * H], g_prm[1:2, H:2 * H]
    g_gconv = lax.dynamic_slice_in_dim(g_gconv_all, sidx * (3 * D // N_CHIP), 3 * D // N_CHIP, axis=1)

    names = ["norm_mix", "w_in", "fox_f_bias", "gdn_conv_w", "gdn_a_log", "gdn_dt_bias", "gdn_norm", "w_branch_fox",
             "w_branch_gdn", "w_out", "norm_ffn", "w_up", "ffn_conv_w", "w_down", "norm_final"]
    ws = [norm_mix, w_in, fox_f_bias, gdn_conv_w, gdn_a_log, gdn_dt_bias, gdn_norm, w_branch_fox, w_branch_gdn, w_out,
          norm_ffn, w_up, ffn_conv_w, w_down, norm_final]
    ms = [m_norm_mix, m_w_in, m_fox_f_bias, m_gdn_conv_w, m_gdn_a_log, m_gdn_dt_bias, m_gdn_norm, m_w_branch_fox,
          m_w_branch_gdn, m_w_out, m_norm_ffn, m_w_up, m_ffn_conv_w, m_w_down, m_norm_final]
    vs = [v_norm_mix, v_w_in, v_fox_f_bias, v_gdn_conv_w, v_gdn_a_log, v_gdn_dt_bias, v_gdn_norm, v_w_branch_fox,
          v_w_branch_gdn, v_w_out, v_norm_ffn, v_w_up, v_ffn_conv_w, v_w_down, v_norm_final]
    gs = [g_norm_mix, g_w_in, g_fbias, g_gconv, g_alog, g_dtb, g_gdn_norm, g_bf, g_bg, g_out, g_norm_ffn, g_up,
          g_fconv, g_down, g_norm_final]
    gs = [g.reshape(w.shape) for g, w in zip(gs, ws)]
    deltas, new_ms, new_vs = [], [], []
    for nm, w, g, m, v in zip(names, ws, gs, ms, vs):
        if w.ndim == 1:
            d, a, b = adamw(w.reshape(1, -1), g.reshape(1, -1), m.reshape(1, -1), v.reshape(1, -1), "adamw_" + nm)
            d, a, b = d.reshape(w.shape), a.reshape(w.shape), b.reshape(w.shape)
        elif nm == "w_in":
            d, a, b = (r.T[None] for r in adamw(w[0].T, g_w_inT, m[0].T, v[0].T, "adamw_" + nm))
        else:
            d, a, b = adamw(w, g, m, v, "adamw_" + nm)
        deltas.append(d)
        new_ms.append(a)
        new_vs.append(b)

    return (loss, grad_x.reshape(B, S, D), *gs, *deltas, *new_ms, *new_vs)
```

```python
import functools
import math

import jax
import jax.numpy as jnp
from jax import lax
from jax.experimental import pallas as pl
from jax.experimental.pallas import tpu as pltpu

F32 = jnp.float32
BF16 = jnp.bfloat16
HEAD = 128
CHUNK = 64
GDN_CONV = 4
FFN_CONV = 3
EPS = 1e-6
NEG = -1e30
ROW = 1024
ATT_TILE = 512
MM_WEIGHT_TILE_BYTES = 8 << 20
N_CHIP = 4
N_DEV = 8
MESH = pl.DeviceIdType.MESH
HI = lax.Precision.HIGH
EXACT = lax.Precision.HIGHEST

ADAM_LR, ADAM_B1, ADAM_B2, ADAM_EPS, ADAM_WD, ADAM_STEP = 0.001, 0.9, 0.999, 1e-08, 0.01, 10


def _tile(n, cap, unit=128):
    best = None
    t = unit
    while t <= min(n, cap):
        if n % t == 0:
            best = t
        t += unit
    return best if best is not None else n


def _params(*sem):
    return pltpu.CompilerParams(dimension_semantics=sem)


_NN = (((1,), (0,)), ((), ()))
_NT = (((1,), (1,)), ((), ()))
_TN = (((0,), (0,)), ((), ()))


def _dg(a, b, dims, hi):
    if hi:
        return lax.dot_general(a, b, dims, precision=HI, preferred_element_type=F32)
    return lax.dot_general(a.astype(BF16), b.astype(BF16), dims, preferred_element_type=F32)


class _RawOps:
    @staticmethod
    def nn(a, b, hi=False):
        return _dg(a, b, _NN, hi)

    @staticmethod
    def nt(a, b, hi=False):
        return _dg(a, b, _NT, hi)

    @staticmethod
    def tn(a, b, hi=False):
        return _dg(a, b, _TN, hi)


def _make_diff_ops():
    def build(hi):
        @jax.custom_vjp
        def nn(a, b):
            return _dg(a, b, _NN, hi)

        nn.defvjp(lambda a, b: (_dg(a, b, _NN, hi), (a, b)),
                  lambda r, g: (_dg(g, r[1], _NT, hi), _dg(r[0], g, _TN, hi)))

        @jax.custom_vjp
        def nt(a, b):
            return _dg(a, b, _NT, hi)

        nt.defvjp(lambda a, b: (_dg(a, b, _NT, hi), (a, b)),
                  lambda r, g: (_dg(g, r[1], _NN, hi), _dg(g, r[0], _TN, hi)))

        @jax.custom_vjp
        def tn(a, b):
            return _dg(a, b, _TN, hi)

        tn.defvjp(lambda a, b: (_dg(a, b, _TN, hi), (a, b)),
                  lambda r, g: (_dg(r[1], g, _NT, hi), _dg(r[0], g, _NN, hi)))
        return nn, nt, tn

    lo, hi_ = build(False), build(True)

    class _DiffOps:
        @staticmethod
        def nn(a, b, hi=False):
            return (hi_ if hi else lo)[0](a, b)

        @staticmethod
        def nt(a, b, hi=False):
            return (hi_ if hi else lo)[1](a, b)

        @staticmethod
        def tn(a, b, hi=False):
            return (hi_ if hi else lo)[2](a, b)

    return _DiffOps


_DiffOps = _make_diff_ops()


def _sigmoid(x):
    return 1.0 / (1.0 + jnp.exp(-x))


def _mm_tile(n, pref):
    if n % pref == 0:
        return pref
    if n % 1408 == 0:
        return 1408
    return _tile(n, pref)


class _Post:
    def __init__(self, fn, row_ins=(), vec_ins=(), row_outs=(), acc_outs=(), keep_main=True):
        self.fn, self.keep_main = fn, keep_main
        self.row_ins = [r if isinstance(r, tuple) else (r, r.shape[1], 0) for r in row_ins]
        self.vec_ins, self.row_outs, self.acc_outs = list(vec_ins), list(row_outs), list(acc_outs)


def matmul(a, b, mode, name, add=None, out_dtype=F32, post=None):
    if mode == "nn":
        (M, K), (K2, N) = a.shape, b.shape
    elif mode == "nt":
        (M, K), (N, K2) = a.shape, b.shape
    else:
        (K, M), (K2, N) = a.shape, b.shape
    assert K == K2, (name, a.shape, b.shape)
    tn = _mm_tile(N, 1024)
    if mode == "tn":
        tm = M if M <= 1408 else _mm_tile(M, 1408)
        tk = _mm_tile(K, 1024)
    else:
        tk = K if K * tn * 2 <= MM_WEIGHT_TILE_BYTES else _mm_tile(K, 1024)
        tm = _mm_tile(M, 1024 if tk <= 2048 and post is None else 512)
    nk = K // tk
    assert post is None or (mode != "tn" and tn == N), name
    dims = {"nn": _NN, "nt": _NT, "tn": _TN}[mode]
    if mode == "tn":
        a_spec = pl.BlockSpec((tk, tm), lambda j, i, k: (k, i))
    else:
        a_spec = pl.BlockSpec((tm, tk), lambda j, i, k: (i, k))
    if mode == "nt":
        b_spec = pl.BlockSpec((tn, tk), lambda j, i, k: (j, k))
    else:
        b_spec = pl.BlockSpec((tk, tn), lambda j, i, k: (k, j))
    o_spec = pl.BlockSpec((tm, tn), lambda j, i, k: (i, j))
    has_add = add is not None
    keep_main = post is None or post.keep_main
    counts = [2 + has_add] + ([len(post.row_ins), len(post.vec_ins)] if post else [0, 0]) + [int(keep_main)]
    counts += ([len(post.row_outs), len(post.acc_outs)] if post else [0, 0]) + [int(nk > 1)]

    def body(*refs):
        parts, p = [], 0
        for cnt in counts:
            parts.append(refs[p:p + cnt])
            p += cnt
        core, row_ins, vec_ins, main, row_outs, acc_outs, acc = parts
        a_ref, b_ref = core[:2]
        prod = lax.dot_general(a_ref[...].astype(BF16), b_ref[...].astype(BF16), dims, preferred_element_type=F32)

        def finish(r):
            if has_add:
                r = r + core[2][...]
            if keep_main:
                main[0][...] = r.astype(out_dtype)
            if post is not None:
                @pl.when(pl.program_id(1) == 0)
                def _():
                    for ref in acc_outs:
                        ref[...] = jnp.zeros_like(ref)

                post.fn(r, row_ins, vec_ins, row_outs, acc_outs)

        if nk == 1:
            finish(prod)
            return
        acc_ref = acc[0]
        k = pl.program_id(2)

        @pl.when(k == 0)
        def _():
            acc_ref[...] = jnp.zeros_like(acc_ref)

        acc_ref[...] += prod

        @pl.when(k == nk - 1)
        def _():
            finish(acc_ref[...])

    in_specs = [a_spec, b_spec] + ([o_spec] if has_add else [])
    args = (a, b) + ((add,) if has_add else ())
    out_specs = [o_spec] if keep_main else []
    out_shape = [jax.ShapeDtypeStruct((M, N), out_dtype)] if keep_main else []
    if post is not None:
        in_specs += [pl.BlockSpec((tm, cols), lambda j, i, k, cb=cb: (i, cb)) for _, cols, cb in post.row_ins]
        in_specs += [pl.BlockSpec((1, v.shape[1]), lambda j, i, k: (0, 0)) for v in post.vec_ins]
        args += tuple(r for r, _, _ in post.row_ins) + tuple(post.vec_ins)
        out_specs += [pl.BlockSpec((tm, cols), lambda j, i, k: (i, 0)) for cols, _ in post.row_outs]
        out_specs += [pl.BlockSpec((1, cols), lambda j, i, k: (0, 0)) for cols in post.acc_outs]
        out_shape += [jax.ShapeDtypeStruct((M, cols), dt) for cols, dt in post.row_outs]
        out_shape += [jax.ShapeDtypeStruct((1, cols), F32) for cols in post.acc_outs]
    rows_sem = "arbitrary" if post is not None and post.acc_outs else "parallel"
    res = pl.pallas_call(
        body, name=name, grid=(N // tn, M // tm, nk), in_specs=in_specs, out_specs=out_specs, out_shape=out_shape,
        scratch_shapes=[pltpu.VMEM((tm, tn), F32)] if nk > 1 else [],
        compiler_params=_params("parallel", rows_sem, "arbitrary"),
    )(*args)
    return res[0] if post is None else res


def rmsnorm_fwd(x, g, name):
    T, D = x.shape
    tm = _tile(T, 512, 8)

    def body(x_ref, g_ref, o_ref):
        xv = x_ref[...]
        r = lax.rsqrt(jnp.mean(xv * xv, axis=-1, keepdims=True) + EPS)
        o_ref[...] = (xv * r * g_ref[...]).astype(BF16)

    return pl.pallas_call(
        body, name=name, grid=(T // tm,),
        in_specs=[pl.BlockSpec((tm, D), lambda i: (i, 0)), pl.BlockSpec((1, D), lambda i: (0, 0))],
        out_specs=pl.BlockSpec((tm, D), lambda i: (i, 0)),
        out_shape=jax.ShapeDtypeStruct((T, D), BF16),
        compiler_params=_params("parallel"),
    )(x, g)


def _post_rmsnorm(g):
    def fn(r, row_ins, vec_ins, row_outs, acc_outs):
        rs = lax.rsqrt(jnp.mean(r * r, axis=-1, keepdims=True) + EPS)
        row_outs[0][...] = (r * rs * vec_ins[0][...]).astype(BF16)

    return _Post(fn, vec_ins=[g], row_outs=[(g.shape[1], BF16)])


def _post_rmsnorm_bwd(x, g, dres, with_bf16):
    D = g.shape[1]

    def fn(dy, row_ins, vec_ins, row_outs, acc_outs):
        xv = row_ins[0][...]
        rs = lax.rsqrt(jnp.mean(xv * xv, axis=-1, keepdims=True) + EPS)
        xh = xv * rs
        acc_outs[0][...] += jnp.sum(dy * xh, axis=0, keepdims=True)
        dxh = dy * vec_ins[0][...]
        dx = row_ins[1][...] + rs * (dxh - xh * jnp.mean(dxh * xh, axis=-1, keepdims=True))
        row_outs[0][...] = dx
        if with_bf16:
            row_outs[1][...] = dx.astype(BF16)

    return _Post(fn, row_ins=[x, dres], vec_ins=[g], row_outs=[(D, F32)] + ([(D, BF16)] if with_bf16 else []),
                 acc_outs=[D], keep_main=False)


def _post_loss(g, target):
    D = g.shape[1]

    def fn(hv, row_ins, vec_ins, row_outs, acc_outs):
        rs = lax.rsqrt(jnp.mean(hv * hv, axis=-1, keepdims=True) + EPS)
        xh = hv * rs
        gv = vec_ins[0][...]
        err = xh * gv - row_ins[0][...]
        acc_outs[0][...] += jnp.sum(err * err, axis=0, keepdims=True)
        dy = err * (1.0 / D)
        acc_outs[1][...] += jnp.sum(dy * xh, axis=0, keepdims=True)
        dxh = dy * gv
        dh = rs * (dxh - xh * jnp.mean(dxh * xh, axis=-1, keepdims=True))
        row_outs[0][...] = dh
        row_outs[1][...] = dh.astype(BF16)

    return _Post(fn, row_ins=[target], vec_ins=[g], row_outs=[(D, F32), (D, BF16)], acc_outs=[D, D], keep_main=False)


def _shift_down(x, k):
    if k == 0:
        return x
    rows = lax.broadcasted_iota(jnp.int32, x.shape, 0)
    return jnp.where(rows >= k, pltpu.roll(x, k, 0), 0.0)


def _shift_up(x, k):
    if k == 0:
        return x
    s = x.shape[0]
    rows = lax.broadcasted_iota(jnp.int32, x.shape, 0)
    return jnp.where(rows < s - k, pltpu.roll(x, s - k, 0), 0.0)


def _conv_fwd(x, w_ref, kw):
    y = x * w_ref[kw - 1:kw, :]
    for i in range(kw - 1):
        y = y + _shift_down(x, kw - 1 - i) * w_ref[i:i + 1, :]
    return y


def _conv_bwd(x, dy, w_ref, kw):
    dx = dy * w_ref[kw - 1:kw, :]
    dws = []
    for i in range(kw - 1):
        dx = dx + _shift_up(dy, kw - 1 - i) * w_ref[i:i + 1, :]
        dws.append(jnp.sum(dy * _shift_down(x, kw - 1 - i), axis=0, keepdims=True))
    dws.append(jnp.sum(dy * x, axis=0, keepdims=True))
    return dx, dws


def ffn_gate_fwd(up_g, up_v, cw_g, cw_v, B, S):
    T, Fd = up_g.shape
    tc = _tile(Fd, 256)

    def body(g_ref, v_ref, wg_ref, wv_ref, o_ref):
        ug = _conv_fwd(g_ref[...], wg_ref, FFN_CONV)
        uv = _conv_fwd(v_ref[...], wv_ref, FFN_CONV)
        o_ref[...] = (ug * _sigmoid(ug) * uv).astype(BF16)

    blk = pl.BlockSpec((S, tc), lambda b, j: (b, j))
    wblk = pl.BlockSpec((FFN_CONV, tc), lambda b, j: (0, j))
    return pl.pallas_call(
        body, name="ffn_gate_fwd", grid=(B, Fd // tc), in_specs=[blk, blk, wblk, wblk], out_specs=blk,
        out_shape=jax.ShapeDtypeStruct((T, Fd), BF16), compiler_params=_params("parallel", "parallel"),
    )(up_g, up_v, cw_g, cw_v)


def ffn_gate_bwd(up_g, up_v, cw_g, cw_v, d_act, B, S):
    T, Fd = up_g.shape
    tc = _tile(Fd, 256)

    def body(g_ref, v_ref, wg_ref, wv_ref, da_ref, dg_ref, dv_ref, dwg_ref, dwv_ref):
        @pl.when(pl.program_id(1) == 0)
        def _():
            dwg_ref[...] = jnp.zeros_like(dwg_ref)
            dwv_ref[...] = jnp.zeros_like(dwv_ref)

        xg, xv = g_ref[...], v_ref[...]
        ug = _conv_fwd(xg, wg_ref, FFN_CONV)
        uv = _conv_fwd(xv, wv_ref, FFN_CONV)
        da = da_ref[...]
        sg = _sigmoid(ug)
        d_ug = da * uv * (sg + ug * sg * (1.0 - sg))
        d_uv = da * ug * sg
        dxg, dwg = _conv_bwd(xg, d_ug, wg_ref, FFN_CONV)
        dxv, dwv = _conv_bwd(xv, d_uv, wv_ref, FFN_CONV)
        dg_ref[...] = dxg.astype(BF16)
        dv_ref[...] = dxv.astype(BF16)
        for i in range(FFN_CONV):
            dwg_ref[i:i + 1, :] += dwg[i]
            dwv_ref[i:i + 1, :] += dwv[i]

    blk = pl.BlockSpec((S, tc), lambda j, b: (b, j))
    wblk = pl.BlockSpec((FFN_CONV, tc), lambda j, b: (0, j))
    return pl.pallas_call(
        body, name="ffn_gate_bwd", grid=(Fd // tc, B), in_specs=[blk, blk, wblk, wblk, blk],
        out_specs=[blk, blk, wblk, wblk],
        out_shape=[jax.ShapeDtypeStruct((T, Fd), BF16), jax.ShapeDtypeStruct((T, Fd), BF16),
                   jax.ShapeDtypeStruct((FFN_CONV, Fd), F32), jax.ShapeDtypeStruct((FFN_CONV, Fd), F32)],
        compiler_params=_params("parallel", "arbitrary"),
    )(up_g, up_v, cw_g, cw_v, d_act)


def _post_merge(p_gates, bf_):
    D = bf_.shape[1]

    def fn(bg, row_ins, vec_ins, row_outs, acc_outs):
        gf_ref, gg_ref, bf_ref = row_ins
        row_outs[0][...] = (_sigmoid(gf_ref[...]) * bf_ref[...] + _sigmoid(gg_ref[...]) * bg).astype(BF16)

    return _Post(fn, row_ins=[(p_gates, D, 0), (p_gates, D, 1), bf_], row_outs=[(D, BF16)])


def _post_merge_bwd(p_gates, bf_, bg_):
    D = bf_.shape[1]

    def fn(d, row_ins, vec_ins, row_outs, acc_outs):
        gf_ref, gg_ref, bf_ref, bg_ref = row_ins
        sf, sg = _sigmoid(gf_ref[...]), _sigmoid(gg_ref[...])
        row_outs[0][...] = (d * sf).astype(BF16)
        row_outs[1][...] = (d * sg).astype(BF16)
        row_outs[2][:, 0:D] = (d * bf_ref[...] * sf * (1.0 - sf)).astype(BF16)
        row_outs[2][:, D:2 * D] = (d * bg_ref[...] * sg * (1.0 - sg)).astype(BF16)

    return _Post(fn, row_ins=[(p_gates, D, 0), (p_gates, D, 1), bf_, bg_],
                 row_outs=[(D, BF16), (D, BF16), (2 * D, BF16)], keep_main=False)


def fox_fwd(p_fox, c_col, c_row, B, S, H, rider=None):
    T = B * S
    t = _tile(S, ATT_TILE)
    nq = S // t
    scale = HEAD ** -0.5

    def body(q_ref, k_ref, v_ref, cq_ref, cr_ref, o_ref, o16_ref, lse_ref):
        i = pl.program_id(2)
        q = q_ref[...]
        cq = cq_ref[...]
        row = lax.broadcasted_iota(jnp.int32, (t, t), 0)
        col = lax.broadcasted_iota(jnp.int32, (t, t), 1)

        def step(j, carry, diagonal):
            m, l, acc = carry
            off = pl.multiple_of(j * t, t)
            k = k_ref[pl.ds(off, t), :]
            v = v_ref[pl.ds(off, t), :]
            s = lax.dot_general(q, k, _NT, preferred_element_type=F32) * scale + (cq - cr_ref[:, pl.ds(off, t)])
            if diagonal:
                s = jnp.where(col <= row, s, NEG)
            m_new = jnp.maximum(m, jnp.max(s, axis=-1, keepdims=True))
            alpha = jnp.exp(m - m_new)
            p = jnp.exp(s - m_new)
            l = alpha * l + jnp.sum(p, axis=-1, keepdims=True)
            acc = alpha * acc + lax.dot_general(p.astype(BF16), v, _NN, preferred_element_type=F32)
            return m_new, l, acc

        m0 = jnp.full((t, 1), NEG, F32)
        below = lax.fori_loop(0, i, functools.partial(step, diagonal=False),
                              (m0, jnp.zeros((t, 1), F32), jnp.zeros((t, HEAD), F32)))
        m, l, acc = step(i, below, diagonal=True)
        o = acc / l
        o_ref[...] = o
        o16_ref[...] = o.astype(BF16)
        lse_ref[...] = m + jnp.log(l)

    return _hosted_call(
        body, rider, name="fox_fwd", grid=(B, H, nq),
        in_specs=[pl.BlockSpec((t, HEAD), lambda b, h, i: (b * nq + i, 3 * h)),
                  pl.BlockSpec((S, HEAD), lambda b, h, i: (b, 3 * h + 1)),
                  pl.BlockSpec((S, HEAD), lambda b, h, i: (b, 3 * h + 2)),
                  pl.BlockSpec((None, None, t, 1), lambda b, h, i: (b, h, i, 0)),
                  pl.BlockSpec((None, None, 1, S), lambda b, h, i: (b, h, 0, 0))],
        out_specs=[pl.BlockSpec((t, HEAD), lambda b, h, i: (b * nq + i, h)),
                   pl.BlockSpec((t, HEAD), lambda b, h, i: (b * nq + i, h)),
                   pl.BlockSpec((None, None, t, 1), lambda b, h, i: (b, h, i, 0))],
        out_shape=[jax.ShapeDtypeStruct((T, H * HEAD), F32), jax.ShapeDtypeStruct((T, H * HEAD), BF16),
                   jax.ShapeDtypeStruct((B, H, S, 1), F32)],
        scratch_shapes=[], semantics=("parallel", "parallel", "arbitrary"),
    )(p_fox, p_fox, p_fox, c_col, c_row)


def fox_bwd(p_fox, c_col, c_row, o, lse, do, B, S, H, rider=None):
    T = B * S
    t = _tile(S, ATT_TILE)
    n = S // t
    scale = HEAD ** -0.5

    def body(q_ref, k_ref, v_ref, cq_ref, cr_ref, o_ref, lse_ref, do_ref, dqkv_ref, dcq_ref, dcr_ref, dq_acc, delta_s):
        row = lax.broadcasted_iota(jnp.int32, (t, t), 0)
        col = lax.broadcasted_iota(jnp.int32, (t, t), 1)

        def prep(i, c):
            rows = pl.ds(pl.multiple_of(i * t, t), t)
            delta_s[rows, :] = jnp.sum(do_ref[rows, :] * o_ref[rows, :], axis=-1, keepdims=True)
            dq_acc[rows, :] = jnp.zeros((t, HEAD), F32)
            dcq_ref[rows, :] = jnp.zeros((t, 1), F32)
            return c

        lax.fori_loop(0, n, prep, 0)

        def kv_step(j, c):
            joff = pl.multiple_of(j * t, t)
            k = k_ref[pl.ds(joff, t), :]
            v = v_ref[pl.ds(joff, t), :]
            crj = cr_ref[:, pl.ds(joff, t)]

            def q_step(i, carry, diagonal):
                dk, dv, dc = carry
                rows = pl.ds(pl.multiple_of(i * t, t), t)
                q = q_ref[rows, :]
                dob = do_ref[rows, :].astype(BF16)
                s = lax.dot_general(q, k, _NT, preferred_element_type=F32) * scale + (cq_ref[rows, :] - crj)
                if diagonal:
                    s = jnp.where(col <= row, s, NEG)
                p = jnp.exp(s - lse_ref[rows, :])
                dp = lax.dot_general(dob, v, _NT, preferred_element_type=F32)
                ds = p * (dp - delta_s[rows, :])
                dsb = ds.astype(BF16)
                dv = dv + lax.dot_general(p.astype(BF16), dob, _TN, preferred_element_type=F32)
                dk = dk + lax.dot_general(dsb, q, _TN, preferred_element_type=F32)
                dq_acc[rows, :] += lax.dot_general(dsb, k, _NN, preferred_element_type=F32) * scale
                dc = dc + jnp.sum(ds, axis=0, keepdims=True)
                dcq_ref[rows, :] += jnp.sum(ds, axis=-1, keepdims=True)
                return dk, dv, dc

            z = jnp.zeros((t, HEAD), F32)
            on_diagonal = q_step(j, (z, z, jnp.zeros((1, t), F32)), diagonal=True)
            dk, dv, dc = lax.fori_loop(j + 1, n, functools.partial(q_step, diagonal=False), on_diagonal)
            dqkv_ref[pl.ds(joff, t), HEAD:2 * HEAD] = (dk * scale).astype(BF16)
            dqkv_ref[pl.ds(joff, t), 2 * HEAD:3 * HEAD] = dv.astype(BF16)
            dcr_ref[:, pl.ds(joff, t)] = -dc
            return c

        lax.fori_loop(0, n, kv_step, 0)
        dqkv_ref[:, 0:HEAD] = dq_acc[...].astype(BF16)

    col_spec = pl.BlockSpec((None, None, S, 1), lambda b, h: (b, h, 0, 0))
    row_spec = pl.BlockSpec((None, None, 1, S), lambda b, h: (b, h, 0, 0))
    head = pl.BlockSpec((S, HEAD), lambda b, h: (b, h))
    return _hosted_call(
        body, rider, name="fox_bwd", grid=(B, H),
        in_specs=[pl.BlockSpec((S, HEAD), lambda b, h: (b, 3 * h)),
                  pl.BlockSpec((S, HEAD), lambda b, h: (b, 3 * h + 1)),
                  pl.BlockSpec((S, HEAD), lambda b, h: (b, 3 * h + 2)),
                  col_spec, row_spec, head, col_spec, head],
        out_specs=[pl.BlockSpec((S, 3 * HEAD), lambda b, h: (b, h)), col_spec, row_spec],
        out_shape=[jax.ShapeDtypeStruct((T, 3 * H * HEAD), BF16), jax.ShapeDtypeStruct((B, H, S, 1), F32),
                   jax.ShapeDtypeStruct((B, H, 1, S), F32)],
        scratch_shapes=[pltpu.VMEM((S, HEAD), F32), pltpu.VMEM((S, 1), F32)], semantics=("parallel", "parallel"),
    )(p_fox, p_fox, p_fox, c_col, c_row, o, lse, do)


def _small_fn(x, b0, b1, H):
    S = x.shape[0]
    lane = lax.broadcasted_iota(jnp.int32, x.shape, 1)
    z = x + b0
    tail = jnp.log1p(jnp.exp(-jnp.abs(z)))
    softplus = jnp.maximum(z, 0.0) + tail
    logsig = -(jnp.maximum(-z, 0.0) + tail)
    g = -jnp.exp(b1) * softplus
    pre = jnp.where(lane < H, logsig, jnp.where(lane < 2 * H, g, 0.0))
    bl = _tile(S, 256, CHUNK)
    r = lax.broadcasted_iota(jnp.int32, (bl, bl), 0)
    c = lax.broadcasted_iota(jnp.int32, (bl, bl), 1)
    tri = (r >= c).astype(F32)
    tri_chunk = jnp.where((r >= c) & (jnp.right_shift(r, 6) == jnp.right_shift(c, 6)), 1.0, 0.0)
    carry = jnp.zeros((1, x.shape[1]), F32)
    parts = []
    for i in range(S // bl):
        blk = pre[i * bl:(i + 1) * bl, :]
        full = lax.dot_general(tri, blk, _NN, precision=EXACT, preferred_element_type=F32) + carry
        chunked = lax.dot_general(tri_chunk, blk, _NN, precision=EXACT, preferred_element_type=F32)
        parts.append(jnp.where(lane[:bl] < H, full, chunked))
        carry = carry + jnp.sum(blk, axis=0, keepdims=True)
    cum = parts[0] if len(parts) == 1 else jnp.concatenate(parts, axis=0)
    return jnp.where(lane < 2 * H, cum, jnp.where(lane < 3 * H, _sigmoid(x), 0.0))


def small_fwd(p_small, prm, B, S, H):
    T = B * S

    def body(x_ref, p_ref, o_ref):
        o_ref[...] = _small_fn(x_ref[...], p_ref[0:1, :], p_ref[1:2, :], H)

    blk = pl.BlockSpec((S, 128), lambda b: (b, 0))
    return pl.pallas_call(
        body, name="small_fwd", grid=(B,), in_specs=[blk, pl.BlockSpec((8, 128), lambda b: (0, 0))], out_specs=blk,
        out_shape=jax.ShapeDtypeStruct((T, 128), F32), compiler_params=_params("parallel"),
    )(p_small, prm)


def small_bwd(p_small, prm, d_out, B, S, H):
    T = B * S

    def body(x_ref, p_ref, d_ref, dx_ref, dp_ref):
        @pl.when(pl.program_id(0) == 0)
        def _():
            dp_ref[...] = jnp.zeros_like(dp_ref)

        _, vjp = jax.vjp(functools.partial(_small_fn, H=H), x_ref[...], p_ref[0:1, :], p_ref[1:2, :])
        dx, db0, db1 = vjp(d_ref[...])
        dx_ref[...] = dx.astype(BF16)
        dp_ref[0:1, :] += db0
        dp_ref[1:2, :] += db1

    blk = pl.BlockSpec((S, 128), lambda b: (b, 0))
    pblk = pl.BlockSpec((8, 128), lambda b: (0, 0))
    return pl.pallas_call(
        body, name="small_bwd", grid=(B,), in_specs=[blk, pblk, blk], out_specs=[blk, pblk],
        out_shape=[jax.ShapeDtypeStruct((T, 128), BF16), jax.ShapeDtypeStruct((8, 128), F32)],
        compiler_params=_params("arbitrary"),
    )(p_small, prm, d_out)


def gdn_prep_fwd(p_gqkv, cw, B, S, H):
    T = B * S

    def body(x_ref, w_ref, o_ref):
        y = _conv_fwd(x_ref[...], w_ref, GDN_CONV)
        a = y * _sigmoid(y)
        rs = lax.rsqrt(jnp.sum(a * a, axis=-1, keepdims=True) + EPS)
        is_qk = (pl.program_id(1) % 3) < 2
        o_ref[...] = a * jnp.where(is_qk, rs, 1.0)

    blk = pl.BlockSpec((S, HEAD), lambda b, n: (b, n))
    wblk = pl.BlockSpec((GDN_CONV, HEAD), lambda b, n: (0, n))
    return pl.pallas_call(
        body, name="gdn_prep_fwd", grid=(B, 3 * H), in_specs=[blk, wblk], out_specs=blk,
        out_shape=jax.ShapeDtypeStruct((T, 3 * H * HEAD), F32), compiler_params=_params("parallel", "parallel"),
    )(p_gqkv, cw)


def gdn_prep_bwd(p_gqkv, cw, d_out, B, S, H):
    T = B * S

    def body(x_ref, w_ref, d_ref, dx_ref, dw_ref):
        @pl.when(pl.program_id(1) == 0)
        def _():
            dw_ref[...] = jnp.zeros_like(dw_ref)

        x = x_ref[...]
        y = _conv_fwd(x, w_ref, GDN_CONV)
        sg = _sigmoid(y)
        a = y * sg
        rs = lax.rsqrt(jnp.sum(a * a, axis=-1, keepdims=True) + EPS)
        d = d_ref[...]
        out = a * rs
        da_qk = rs * (d - out * jnp.sum(d * out, axis=-1, keepdims=True))
        is_qk = (pl.program_id(0) % 3) < 2
        da = jnp.where(is_qk, da_qk, d)
        dy = da * (sg + y * sg * (1.0 - sg))
        dx, dws = _conv_bwd(x, dy, w_ref, GDN_CONV)
        dx_ref[...] = dx.astype(BF16)
        for i in range(GDN_CONV):
            dw_ref[i:i + 1, :] += dws[i]

    blk = pl.BlockSpec((S, HEAD), lambda n, b: (b, n))
    wblk = pl.BlockSpec((GDN_CONV, HEAD), lambda n, b: (0, n))
    return pl.pallas_call(
        body, name="gdn_prep_bwd", grid=(3 * H, B), in_specs=[blk, wblk, blk], out_specs=[blk, wblk],
        out_shape=[jax.ShapeDtypeStruct((T, 3 * H * HEAD), BF16), jax.ShapeDtypeStruct((GDN_CONV, 3 * H * HEAD), F32)],
        compiler_params=_params("parallel", "arbitrary"),
    )(p_gqkv, cw, d_out)


@jax.custom_vjp
def _given_inverse(a, t):
    return t


def _given_inverse_fwd(a, t):
    return t, t


def _given_inverse_bwd(t, g):
    x = _dg(t, g, _TN, True)
    return -_dg(x, t, _NT, True), jnp.zeros_like(t)


_given_inverse.defvjp(_given_inverse_fwd, _given_inverse_bwd)


def _to_col(row):
    n = row.shape[1]
    r = lax.broadcasted_iota(jnp.int32, (n, n), 0)
    c = lax.broadcasted_iota(jnp.int32, (n, n), 1)
    return jnp.sum(jnp.where(r == c, row, 0.0), axis=1, keepdims=True)


def _intra_fn(k, v, beta_r, gcr, ops, t_known=None):
    n = len(k)
    m = k[0].shape[0]
    r = lax.broadcasted_iota(jnp.int32, (m, m), 0)
    c = lax.broadcasted_iota(jnp.int32, (m, m), 1)
    below = (r > c) & (jnp.right_shift(r, 6) == jnp.right_shift(c, 6))
    beta = [_to_col(beta_r[i]) for i in range(n)]
    gcc = [_to_col(gcr[i]) for i in range(n)]
    decay = [jnp.exp(jnp.where(below, gcc[i] - gcr[i], NEG)) for i in range(n)]
    kb = [k[i] * beta[i] for i in range(n)]
    a = [ops.nt(kb[i], k[i]) * decay[i] for i in range(n)]
    if t_known is None:
        p = [-a[i] for i in range(n)]
        tm = [jnp.where(r == c, 1.0, 0.0) + p[i] for i in range(n)]
        for _ in range(5):
            p = [ops.nn(p[i], p[i], hi=True) for i in range(n)]
            tm = [tm[i] + ops.nn(tm[i], p[i], hi=True) for i in range(n)]
    else:
        tm = [_given_inverse(a[i], t_known[i]) for i in range(n)]
    u_hat = [ops.nn(tm[i], v[i] * beta[i], hi=True) for i in range(n)]
    w = [ops.nn(tm[i], kb[i] * jnp.exp(gcc[i]), hi=True) for i in range(n)]
    return tuple(u_hat), tuple(w), tuple(tm)


INTRA_NB = 8
PAIR = 1


def gdn_intra_fwd(qkvn, betar5, gcr5, B, S, H, rider=None):
    T = B * S
    UNIT = PAIR * CHUNK
    N = S // UNIT
    nb = min(INTRA_NB // PAIR, N)
    rows = nb * UNIT
    ns = N // nb

    def body(k_ref, v_ref, b_ref, gr_ref, uh_ref, w_ref, t_ref):
        sls = [slice(ci * UNIT, (ci + 1) * UNIT) for ci in range(nb)]
        u_hat, w, tm = _intra_fn(tuple(k_ref[sl, :] for sl in sls), tuple(v_ref[sl, :] for sl in sls),
                                 tuple(b_ref[ci] for ci in range(nb)), tuple(gr_ref[ci] for ci in range(nb)), _RawOps)
        for ci, sl in enumerate(sls):
            uh_ref[sl, :] = u_hat[ci]
            w_ref[sl, :] = w[ci]
            t_ref[ci] = tm[ci]

    rowspec = pl.BlockSpec((None, None, nb, 1, UNIT), lambda b, h, i: (b, h, i, 0, 0))
    sqspec = pl.BlockSpec((None, None, nb, UNIT, UNIT), lambda b, h, i: (b, h, i, 0, 0))
    out = pl.BlockSpec((rows, HEAD), lambda b, h, i: (b * ns + i, h))
    return _hosted_call(
        body, rider, name="gdn_intra_fwd", grid=(B, H, ns),
        in_specs=[pl.BlockSpec((rows, HEAD), lambda b, h, i: (b * ns + i, 3 * h + 1)),
                  pl.BlockSpec((rows, HEAD), lambda b, h, i: (b * ns + i, 3 * h + 2)),
                  rowspec, rowspec],
        out_specs=[out, out, sqspec],
        out_shape=[jax.ShapeDtypeStruct((T, H * HEAD), F32), jax.ShapeDtypeStruct((T, H * HEAD), F32),
                   jax.ShapeDtypeStruct((B, H, N, UNIT, UNIT), F32)],
        scratch_shapes=[], semantics=("parallel", "parallel", "parallel"),
    )(qkvn, qkvn, betar5, gcr5)


def gdn_intra_bwd(qkvn, betar5, gcr5, t_inv, d_uh, d_w, dq_in, dk_in, B, S, H):
    T = B * S
    UNIT = PAIR * CHUNK
    N = S // UNIT
    nb = min(INTRA_NB // PAIR, N)
    rows = nb * UNIT
    ns = N // nb

    def body(k_ref, v_ref, b_ref, gr_ref, t_ref, duh_ref, dw_ref, dq_ref, dk_ref, o_ref, db_ref, dgr_ref):
        sls = [slice(ci * UNIT, (ci + 1) * UNIT) for ci in range(nb)]
        chunks = range(nb)
        _, vjp = jax.vjp(
            functools.partial(_intra_fn, ops=_DiffOps, t_known=tuple(t_ref[ci] for ci in chunks)),
            tuple(k_ref[sl, :] for sl in sls), tuple(v_ref[sl, :] for sl in sls), tuple(b_ref[ci] for ci in chunks),
            tuple(gr_ref[ci] for ci in chunks))
        zero = jnp.zeros((UNIT, UNIT), F32)
        dk, dv, db, dgr = vjp((tuple(duh_ref[sl, :] for sl in sls), tuple(dw_ref[sl, :] for sl in sls),
                               tuple(zero for _ in chunks)))
        for ci, sl in enumerate(sls):
            o_ref[sl, 0:HEAD] = dq_ref[sl, :]
            o_ref[sl, HEAD:2 * HEAD] = dk[ci] + dk_ref[sl, :]
            o_ref[sl, 2 * HEAD:3 * HEAD] = dv[ci]
            db_ref[ci] = db[ci]
            dgr_ref[ci] = dgr[ci]

    rowspec = pl.BlockSpec((None, None, nb, 1, UNIT), lambda b, h, i: (b, h, i, 0, 0))
    sqspec = pl.BlockSpec((None, None, nb, UNIT, UNIT), lambda b, h, i: (b, h, i, 0, 0))
    head = pl.BlockSpec((rows, HEAD), lambda b, h, i: (b * ns + i, h))
    return pl.pallas_call(
        body, name="gdn_intra_bwd", grid=(B, H, ns),
        in_specs=[pl.BlockSpec((rows, HEAD), lambda b, h, i: (b * ns + i, 3 * h + 1)),
                  pl.BlockSpec((rows, HEAD), lambda b, h, i: (b * ns + i, 3 * h + 2)),
                  rowspec, rowspec, sqspec, head, head, head, head],
        out_specs=[pl.BlockSpec((rows, 3 * HEAD), lambda b, h, i: (b * ns + i, h)), rowspec, rowspec],
        out_shape=[jax.ShapeDtypeStruct((T, 3 * H * HEAD), F32),
                   jax.ShapeDtypeStruct((B, H, N, 1, UNIT), F32), jax.ShapeDtypeStruct((B, H, N, 1, UNIT), F32)],
        compiler_params=_params("parallel", "parallel", "parallel"),
    )(qkvn, qkvn, betar5, gcr5, t_inv, d_uh, d_w, dq_in, dk_in)


def _inter_fn(q, k, u_hat, w, gcr, state, ops):
    n = len(q)
    r = lax.broadcasted_iota(jnp.int32, (CHUNK, CHUNK), 0)
    c = lax.broadcasted_iota(jnp.int32, (CHUNK, CHUNK), 1)
    last = lax.broadcasted_iota(jnp.int32, (1, CHUNK), 1) == CHUNK - 1
    gcc = [_to_col(gcr[i]) for i in range(n)]
    gl = [jnp.sum(jnp.where(last, gcr[i], 0.0), axis=1, keepdims=True) for i in range(n)]
    decay = [jnp.exp(jnp.where(r >= c, gcc[i] - gcr[i], NEG)) for i in range(n)]
    qs = [q[i] * (HEAD ** -0.5) for i in range(n)]
    ws = [ops.nn(w[i], state[i]) for i in range(n)]
    qst = [ops.nn(qs[i] * jnp.exp(gcc[i]), state[i]) for i in range(n)]
    attn = [ops.nt(qs[i], k[i]) * decay[i] for i in range(n)]
    u = [u_hat[i] - ws[i] for i in range(n)]
    o = [qst[i] + ops.nn(attn[i], u[i]) for i in range(n)]
    kdu = [ops.tn(k[i] * jnp.exp(gl[i] - gcc[i]), u[i]) for i in range(n)]
    new_state = [state[i] * jnp.exp(gl[i]) + kdu[i] for i in range(n)]
    return tuple(o), tuple(new_state)


INTER_HEADS = 4
INTER_ROWS = 512


def _inter_specs(ts, ns, hp, backward):
    at = (lambda s: ns - 1 - s) if backward else (lambda s: s)
    nc = ts // CHUNK
    qk = []
    for hh in range(hp):
        qk.append(pl.BlockSpec((ts, HEAD), lambda b, g, s, hh=hh: (b * ns + at(s), 3 * (hp * g + hh))))
        qk.append(pl.BlockSpec((ts, HEAD), lambda b, g, s, hh=hh: (b * ns + at(s), 3 * (hp * g + hh) + 1)))
    heads = pl.BlockSpec((ts, hp * HEAD), lambda b, g, s: (b * ns + at(s), g))
    rowspec = pl.BlockSpec((None, hp, nc, 1, CHUNK), lambda b, g, s: (b, g, at(s), 0, 0))
    stspec = pl.BlockSpec((None, hp, nc, HEAD, HEAD), lambda b, g, s: (b, g, at(s), 0, 0))
    return qk, heads, rowspec, stspec


def gdn_inter_fwd(qkvn, u_hat, w, gcr5, B, S, H):
    T = B * S
    N = S // CHUNK
    hp = INTER_HEADS if H % INTER_HEADS == 0 else 1
    hs = range(hp)
    ts = _tile(S, INTER_ROWS, CHUNK)
    ns, nc = S // ts, ts // CHUNK

    def body(*refs):
        qk_refs, (uh_ref, w_ref, gr_ref, o_ref, st_ref, s_scr) = refs[:2 * hp], refs[2 * hp:]

        @pl.when(pl.program_id(2) == 0)
        def _():
            s_scr[...] = jnp.zeros_like(s_scr)

        def step(n, c):
            rows = pl.ds(pl.multiple_of(n * CHUNK, CHUNK), CHUNK)
            st = tuple(s_scr[hh] for hh in hs)
            for hh in hs:
                st_ref[hh, n] = st[hh]
            o, new = _inter_fn(tuple(qk_refs[2 * hh][rows, :] for hh in hs), tuple(qk_refs[2 * hh + 1][rows, :] for hh in hs),
                               tuple(uh_ref[rows, hh * HEAD:(hh + 1) * HEAD] for hh in hs),
                               tuple(w_ref[rows, hh * HEAD:(hh + 1) * HEAD] for hh in hs),
                               tuple(gr_ref[hh, n] for hh in hs), st, _RawOps)
            for hh in hs:
                o_ref[rows, hh * HEAD:(hh + 1) * HEAD] = o[hh]
                s_scr[hh] = new[hh]
            return c

        lax.fori_loop(0, nc, step, 0)

    qk, heads, rowspec, stspec = _inter_specs(ts, ns, hp, backward=False)
    return pl.pallas_call(
        body, name="gdn_inter_fwd", grid=(B, H // hp, ns),
        in_specs=qk + [heads, heads, rowspec], out_specs=[heads, stspec],
        out_shape=[jax.ShapeDtypeStruct((T, H * HEAD), F32), jax.ShapeDtypeStruct((B, H, N, HEAD, HEAD), F32)],
        scratch_shapes=[pltpu.VMEM((hp, HEAD, HEAD), F32)],
        compiler_params=_params("parallel", "parallel", "arbitrary"),
    )(*([qkvn] * (2 * hp)), u_hat, w, gcr5)


def gdn_inter_bwd(qkvn, u_hat, w, gcr5, states, d_o, B, S, H, rider=None):
    T = B * S
    N = S // CHUNK
    hp = INTER_HEADS if H % INTER_HEADS == 0 else 1
    hs = range(hp)
    ts = _tile(S, INTER_ROWS, CHUNK)
    ns, nc = S // ts, ts // CHUNK

    def body(*refs):
        qk_refs = refs[:2 * hp]
        uh_ref, w_ref, gr_ref, st_ref, do_ref, dq_ref, dk_ref, duh_ref, dw_ref, dgr_ref, ds_scr = refs[2 * hp:]

        @pl.when(pl.program_id(2) == 0)
        def _():
            ds_scr[...] = jnp.zeros_like(ds_scr)

        cols = [slice(hh * HEAD, (hh + 1) * HEAD) for hh in hs]

        def step(i, c):
            n = nc - 1 - i
            rows = pl.ds(pl.multiple_of(n * CHUNK, CHUNK), CHUNK)
            _, vjp = jax.vjp(functools.partial(_inter_fn, ops=_DiffOps),
                             tuple(qk_refs[2 * hh][rows, :] for hh in hs), tuple(qk_refs[2 * hh + 1][rows, :] for hh in hs),
                             tuple(uh_ref[rows, cols[hh]] for hh in hs), tuple(w_ref[rows, cols[hh]] for hh in hs),
                             tuple(gr_ref[hh, n] for hh in hs), tuple(st_ref[hh, n] for hh in hs))
            dq, dk, duh, dw, dgr, ds = vjp((tuple(do_ref[rows, cols[hh]] for hh in hs), tuple(ds_scr[hh] for hh in hs)))
            for hh in hs:
                dq_ref[rows, cols[hh]] = dq[hh]
                dk_ref[rows, cols[hh]] = dk[hh]
                duh_ref[rows, cols[hh]] = duh[hh]
                dw_ref[rows, cols[hh]] = dw[hh]
                dgr_ref[hh, n] = dgr[hh]
                ds_scr[hh] = ds[hh]
            return c

        lax.fori_loop(0, nc, step, 0)

    qk, heads, rowspec, stspec = _inter_specs(ts, ns, hp, backward=True)
    hshape = jax.ShapeDtypeStruct((T, H * HEAD), F32)
    return _hosted_call(
        body, rider, name="gdn_inter_bwd", grid=(B, H // hp, ns),
        in_specs=qk + [heads, heads, rowspec, stspec, heads],
        out_specs=[heads, heads, heads, heads, rowspec],
        out_shape=[hshape, hshape, hshape, hshape, jax.ShapeDtypeStruct((B, H, N, 1, CHUNK), F32)],
        scratch_shapes=[pltpu.VMEM((hp, HEAD, HEAD), F32)], semantics=("parallel", "parallel", "arbitrary"),
    )(*([qkvn] * (2 * hp)), u_hat, w, gcr5, states, d_o)


def gdn_post_fwd(o, p_gz, g, H):
    T = o.shape[0]
    tm = _tile(T, 1024, 8)

    def body(o_ref, z_ref, g_ref, y_ref):
        ov, z = o_ref[...], z_ref[...]
        r = lax.rsqrt(jnp.mean(ov * ov, axis=-1, keepdims=True) + EPS)
        y_ref[...] = (ov * r * g_ref[...] * z * _sigmoid(z)).astype(BF16)

    blk = pl.BlockSpec((tm, HEAD), lambda i, h: (i, h))
    return pl.pallas_call(
        body, name="gdn_post_fwd", grid=(T // tm, H), in_specs=[blk, blk, pl.BlockSpec((1, HEAD), lambda i, h: (0, 0))],
        out_specs=blk, out_shape=jax.ShapeDtypeStruct((T, H * HEAD), BF16),
        compiler_params=_params("parallel", "parallel"),
    )(o, p_gz, g)


def gdn_post_bwd(o, p_gz, g, dy, H):
    T = o.shape[0]
    tm = _tile(T, 1024, 8)

    def body(o_ref, z_ref, g_ref, dy_ref, do_ref, dz_ref, dg_ref):
        @pl.when((pl.program_id(0) == 0) & (pl.program_id(1) == 0))
        def _():
            dg_ref[...] = jnp.zeros_like(dg_ref)

        ov, z, d = o_ref[...], z_ref[...], dy_ref[...]
        r = lax.rsqrt(jnp.mean(ov * ov, axis=-1, keepdims=True) + EPS)
        xh = ov * r
        sg = _sigmoid(z)
        sz = z * sg
        d_n = d * sz
        dz_ref[...] = (d * xh * g_ref[...] * (sg + z * sg * (1.0 - sg))).astype(BF16)
        dg_ref[...] += jnp.sum(d_n * xh, axis=0, keepdims=True)
        dxh = d_n * g_ref[...]
        do_ref[...] = r * (dxh - xh * jnp.mean(dxh * xh, axis=-1, keepdims=True))

    blk = pl.BlockSpec((tm, HEAD), lambda i, h: (i, h))
    vec = pl.BlockSpec((1, HEAD), lambda i, h: (0, 0))
    return pl.pallas_call(
        body, name="gdn_post_bwd", grid=(T // tm, H), in_specs=[blk, blk, vec, blk], out_specs=[blk, blk, vec],
        out_shape=[jax.ShapeDtypeStruct((T, H * HEAD), F32), jax.ShapeDtypeStruct((T, H * HEAD), BF16),
                   jax.ShapeDtypeStruct((1, HEAD), F32)],
        compiler_params=_params("arbitrary", "arbitrary"),
    )(o, p_gz, g, dy)


def adamw(w, g, m, v, name):
    shape = w.shape
    lead = (None,) * (w.ndim - 2)
    zeros = (0,) * (w.ndim - 2)
    R, C = shape[-2:]
    g2 = g.reshape(R, C)
    tr, tc = _tile(R, 128, 8), C
    if tr % 8 and R > 8:
        tr, tc = R, _tile(C, 128)

    def body(w_ref, g_ref, m_ref, v_ref, d_ref, nm_ref, nv_ref):
        gv = g_ref[...]
        nm = ADAM_B1 * m_ref[...] + (1.0 - ADAM_B1) * gv
        nv = ADAM_B2 * v_ref[...] + (1.0 - ADAM_B2) * (gv * gv)
        m_hat = nm / (1.0 - ADAM_B1 ** ADAM_STEP)
        v_hat = nv / (1.0 - ADAM_B2 ** ADAM_STEP)
        d_ref[...] = -ADAM_LR * (m_hat / (jnp.sqrt(v_hat) + ADAM_EPS) + ADAM_WD * w_ref[...])
        nm_ref[...] = nm
        nv_ref[...] = nv

    blk = pl.BlockSpec(lead + (tr, tc), lambda i, j: zeros + (i, j))
    gblk = pl.BlockSpec((tr, tc), lambda i, j: (i, j))
    sh = jax.ShapeDtypeStruct(shape, F32)
    return pl.pallas_call(
        body, name=name, grid=(R // tr, C // tc), in_specs=[blk, gblk, blk, blk], out_specs=[blk] * 3, out_shape=[sh] * 3,
        compiler_params=_params("parallel", "parallel"),
    )(w, g2, m, v)


def _place():
    x, y, c = lax.axis_index("x"), lax.axis_index("y"), lax.axis_index("c")
    chips = [(1 - x, y), (x, 1 - y), (1 - x, 1 - y)]
    return x, y, c, chips


_HBM = pl.BlockSpec(memory_space=pltpu.HBM)


def allgather_weights(packs):
    n = len(packs)

    def body(*refs):
        in_refs, out_refs, (send_sems, recv_sems) = refs[:n], refs[n:2 * n], refs[2 * n:]
        x, y, c, chips = _place()
        me_s = 2 * x + y
        me, sibling = (x, y, c), (x, y, 1 - c)
        shards = [2 * chip[0] + chip[1] for chip in chips]

        def copy(a, k, shard, half, to, src=None):
            dst = out_refs[a].at[shard, half]
            return pltpu.make_async_remote_copy(src_ref=dst if src is None else src, dst_ref=dst,
                                                send_sem=send_sems.at[6 * a + k], recv_sem=recv_sems.at[6 * a + k],
                                                device_id=to, device_id_type=MESH)

        first = [copy(a, j, me_s, c, (*chip, c), src=in_refs[a].at[c]) for a in range(n) for j, chip in enumerate(chips)]
        for cp in first:
            cp.start()
        passed = []
        for a in range(n):
            for j in range(3):
                copy(a, j, shards[j], c, me).wait_recv()
                passed.append(copy(a, 3 + j, shards[j], c, sibling))
                passed[-1].start()
        for a in range(n):
            for j in range(3):
                copy(a, 3 + j, shards[j], 1 - c, me).wait_recv()
        for cp in first + passed:
            cp.wait_send()

    return pl.pallas_call(
        body, name="allgather_weights", in_specs=[_HBM] * n, out_specs=[_HBM] * n,
        out_shape=[jax.ShapeDtypeStruct((N_CHIP,) + p.shape, p.dtype) for p in packs],
        scratch_shapes=[pltpu.SemaphoreType.DMA((6 * n,)), pltpu.SemaphoreType.DMA((6 * n,))],
    )(*packs)


class _Rider:
    def __init__(self, inputs, out_shapes, n_sems, sends, recvs, aliases=None):
        self.inputs, self.out_shapes, self.n_sems = list(inputs), list(out_shapes), n_sems
        self.sends, self.recvs, self.aliases = sends, recvs, aliases or {}

    def start(self, *refs):
        for cp in self.sends(*refs):
            cp.start()

    def wait(self, *refs):
        for cp in self.recvs(*refs):
            cp.wait_recv()
        for cp in self.sends(*refs):
            cp.wait_send()


def _remote(src, dst, send_sems, recv_sems, k, to):
    return pltpu.make_async_remote_copy(src_ref=src, dst_ref=dst, send_sem=send_sems.at[k], recv_sem=recv_sems.at[k],
                                        device_id=to, device_id_type=MESH)


def _run_alone(rider, name):
    ri = len(rider.inputs)

    def body(*refs):
        ins, outs, (send_sems, recv_sems) = refs[:ri], refs[ri:-2], refs[-2:]
        rider.start(ins, outs, send_sems, recv_sems)
        rider.wait(ins, outs, send_sems, recv_sems)

    return pl.pallas_call(
        body, name=name, in_specs=[_HBM] * ri, out_specs=[_HBM] * len(rider.out_shapes), out_shape=rider.out_shapes,
        scratch_shapes=[pltpu.SemaphoreType.DMA((rider.n_sems,))] * 2, input_output_aliases=rider.aliases,
    )(*rider.inputs)


def _hosted_call(body, rider, *, name, grid, in_specs, out_specs, out_shape, scratch_shapes, semantics):
    if rider is None:
        return pl.pallas_call(body, name=name, grid=grid, in_specs=in_specs, out_specs=out_specs, out_shape=out_shape,
                              scratch_shapes=scratch_shapes, compiler_params=_params(*semantics))
    n_in, n_out, n_scr = len(in_specs), len(out_specs), len(scratch_shapes)
    ri, ro = len(rider.inputs), len(rider.out_shapes)

    def hosted(*refs):
        parts, p = [], 0
        for cnt in (n_in, ri, n_out, ro, n_scr, 2):
            parts.append(refs[p:p + cnt])
            p += cnt
        ins, rins, outs, routs, scr, (send_sems, recv_sems) = parts
        first = functools.reduce(jnp.logical_and, [pl.program_id(a) == 0 for a in range(len(grid))])
        last = functools.reduce(jnp.logical_and, [pl.program_id(a) == grid[a] - 1 for a in range(len(grid))])

        @pl.when(first)
        def _():
            rider.start(rins, routs, send_sems, recv_sems)

        body(*ins, *outs, *scr)

        @pl.when(last)
        def _():
            rider.wait(rins, routs, send_sems, recv_sems)

    call = pl.pallas_call(
        hosted, name=name, grid=grid, in_specs=list(in_specs) + [_HBM] * ri, out_specs=list(out_specs) + [_HBM] * ro,
        out_shape=list(out_shape) + rider.out_shapes,
        scratch_shapes=list(scratch_shapes) + [pltpu.SemaphoreType.DMA((rider.n_sems,))] * 2,
        input_output_aliases={n_in + i: n_out + o for i, o in rider.aliases.items()},
        compiler_params=_params(*(("arbitrary",) * len(grid))))

    def run(*args):
        res = call(*args, *rider.inputs)
        return res[:n_out], res[n_out:]

    return run


def _ride_gather_ici(packs):
    n = len(packs)

    def sends(ins, outs, send_sems, recv_sems):
        x, y, c, chips = _place()
        return [_remote(ins[a].at[c], outs[a].at[2 * x + y, c], send_sems, recv_sems, 3 * a + j, (*chip, c))
                for a in range(n) for j, chip in enumerate(chips)]

    def recvs(ins, outs, send_sems, recv_sems):
        x, y, c, chips = _place()
        return [_remote(ins[a].at[c], outs[a].at[2 * chip[0] + chip[1], c], send_sems, recv_sems, 3 * a + j, (x, y, c))
                for a in range(n) for j, chip in enumerate(chips)]

    return _Rider(packs, [jax.ShapeDtypeStruct((N_CHIP,) + p.shape, p.dtype) for p in packs], 3 * n, sends, recvs)


def _ride_gather_d2d(gathered):
    n = len(gathered)

    def copies(landing_half, to):
        def build(ins, outs, send_sems, recv_sems):
            x, y, c, chips = _place()
            return [_remote(ins[a].at[2 * chip[0] + chip[1], c], outs[a].at[2 * chip[0] + chip[1], landing_half(c)],
                            send_sems, recv_sems, 3 * a + j, to(x, y, c))
                    for a in range(n) for j, chip in enumerate(chips)]
        return build

    return _Rider(gathered, [jax.ShapeDtypeStruct(g.shape, g.dtype) for g in gathered], 3 * n,
                  copies(lambda c: c, lambda x, y, c: (x, y, 1 - c)), copies(lambda c: 1 - c, lambda x, y, c: (x, y, c)),
                  aliases={a: a for a in range(n)})


def _ride_exchange(gs):
    n = len(gs)

    def copies(ins, outs, send_sems, recv_sems):
        x, y, c, _ = _place()
        return [_remote(ins[a].at[1 - c], outs[a], send_sems, recv_sems, a, (x, y, 1 - c)) for a in range(n)]

    return _Rider(gs, [jax.ShapeDtypeStruct(g.shape[1:], g.dtype) for g in gs], n, copies, copies)


def _ride_scatter(b16s):
    n = len(b16s)

    def sends(ins, outs, send_sems, recv_sems):
        x, y, c, chips = _place()
        return [_remote(ins[a].at[2 * chip[0] + chip[1]], outs[a].at[2 * x + y], send_sems, recv_sems, 3 * a + j, (*chip, c))
                for a in range(n) for j, chip in enumerate(chips)]

    def recvs(ins, outs, send_sems, recv_sems):
        x, y, c, chips = _place()
        return [_remote(ins[a].at[2 * x + y], outs[a].at[2 * chip[0] + chip[1]], send_sems, recv_sems, 3 * a + j, (x, y, c))
                for a in range(n) for j, chip in enumerate(chips)]

    return _Rider(b16s, [jax.ShapeDtypeStruct(b.shape, b.dtype) for b in b16s], 3 * n, sends, recvs)


def _slab_tile(r, cols):
    tr = _tile(r, 256, 16)
    if tr % 16 == 0:
        return tr, cols
    return r, _tile(cols, 128)


def add_halves(g, got, idx, name):
    _, ns, r, cols = g.shape
    tr, tc = _slab_tile(r, cols)

    def body(idx_ref, a_ref, b_ref, o32_ref, o16_ref):
        s = a_ref[...] + b_ref[...]
        o32_ref[...] = s
        o16_ref[...] = s.astype(BF16)

    blk = pl.BlockSpec((None, tr, tc), lambda s, i, j, idx_ref: (s, i, j))
    return pl.pallas_call(
        body, name=name,
        grid_spec=pltpu.PrefetchScalarGridSpec(
            num_scalar_prefetch=1, grid=(ns, r // tr, cols // tc),
            in_specs=[pl.BlockSpec((None, None, tr, tc), lambda s, i, j, idx_ref: (idx_ref[0], s, i, j)), blk],
            out_specs=[blk, blk]),
        out_shape=[jax.ShapeDtypeStruct((ns, r, cols), F32), jax.ShapeDtypeStruct((ns, r, cols), BF16)],
        compiler_params=_params("parallel", "parallel", "parallel"),
    )(idx, g, got)


def add_chips(a32, got16, idx, name):
    ns, r, cols = a32.shape
    tr, tc = _slab_tile(r, cols)

    def body(idx_ref, a_ref, r1_ref, r2_ref, r3_ref, o_ref):
        o_ref[...] = ((a_ref[...] + r1_ref[...].astype(F32)) + r2_ref[...].astype(F32)) + r3_ref[...].astype(F32)

    def slab(k):
        return pl.BlockSpec((None, tr, tc), lambda i, j, idx_ref: ((idx_ref[1] + k) % ns, i, j))

    return pl.pallas_call(
        body, name=name,
        grid_spec=pltpu.PrefetchScalarGridSpec(
            num_scalar_prefetch=1, grid=(r // tr, cols // tc), in_specs=[slab(0), slab(1), slab(2), slab(3)],
            out_specs=pl.BlockSpec((tr, tc), lambda i, j, idx_ref: (i, j))),
        out_shape=jax.ShapeDtypeStruct((r, cols), F32),
        compiler_params=_params("parallel", "parallel"),
    )(idx, a32, got16, got16, got16)


def share_halves(halves):
    n = len(halves)

    def body(*refs):
        in_refs, out_refs, (send_sems, recv_sems) = refs[:n], refs[n:2 * n], refs[2 * n:]
        x, y, c, _ = _place()
        cps = [pltpu.make_async_remote_copy(src_ref=in_refs[a], dst_ref=out_refs[a], send_sem=send_sems.at[a],
                                            recv_sem=recv_sems.at[a], device_id=(x, y, 1 - c), device_id_type=MESH)
               for a in range(n)]
        for cp in cps:
            cp.start()
        for cp in cps:
            cp.wait()

    return pl.pallas_call(
        body, name="share_halves", in_specs=[_HBM] * n, out_specs=[_HBM] * n,
        out_shape=[jax.ShapeDtypeStruct(h.shape, F32) for h in halves],
        scratch_shapes=[pltpu.SemaphoreType.DMA((n,)), pltpu.SemaphoreType.DMA((n,))],
    )(*halves)


def allreduce_small(v):
    R, _ = v.shape

    def body(in_ref, out_ref, slots, send_sems, recv_sems):
        x, y, c, _ = _place()
        me = 4 * x + 2 * y + c
        slots[me] = in_ref[...]
        cps = []
        for k in range(1, N_DEV):
            to = (x ^ (k >> 2), y ^ ((k >> 1) & 1), c ^ (k & 1))
            cps.append(pltpu.make_async_remote_copy(src_ref=in_ref, dst_ref=slots.at[me], send_sem=send_sems.at[k - 1],
                                                    recv_sem=recv_sems.at[k - 1], device_id=to, device_id_type=MESH))
        for cp in cps:
            cp.start()
        for k in range(1, N_DEV):
            frm = 4 * (x ^ (k >> 2)) + 2 * (y ^ ((k >> 1) & 1)) + (c ^ (k & 1))
            pltpu.make_async_remote_copy(src_ref=in_ref, dst_ref=slots.at[frm], send_sem=send_sems.at[k - 1],
                                         recv_sem=recv_sems.at[k - 1], device_id=(x, y, c), device_id_type=MESH).wait_recv()
        for cp in cps:
            cp.wait_send()
        acc = slots[0]
        for d in range(1, N_DEV):
            acc = acc + slots[d]
        out_ref[...] = acc

    vm = pl.BlockSpec(memory_space=pltpu.VMEM)
    return pl.pallas_call(
        body, name="allreduce_small", in_specs=[vm], out_specs=vm, out_shape=jax.ShapeDtypeStruct((R, ROW), F32),
        scratch_shapes=[pltpu.VMEM((N_DEV, R, ROW), F32), pltpu.SemaphoreType.DMA((N_DEV - 1,)),
                        pltpu.SemaphoreType.DMA((N_DEV - 1,))],
    )(v)


def _rows_of(n, unit=16):
    return -(-n // (unit * ROW)) * unit


def _pack_rows(items, total_rows, dtype, unit=16):
    parts = []
    used = 0
    for a in items:
        flat = a.reshape(-1)
        r = _rows_of(flat.shape[0], unit)
        flat = jnp.pad(flat, (0, r * ROW - flat.shape[0]))
        parts.append(flat.reshape(r, ROW))
        used += r
    if total_rows > used:
        parts.append(jnp.zeros((total_rows - used, ROW), dtype))
    return jnp.concatenate(parts, axis=0)


def _unpack_rows(buf, shapes, unit=16):
    lead = buf.shape[:-2]
    out = []
    off = 0
    for shp in shapes:
        n = math.prod(shp)
        r = _rows_of(n, unit)
        piece = buf[..., off:off + r, :].reshape(*lead, r * ROW)[..., :n].reshape(*lead, *shp)
        out.append(piece)
        off += r
    return out


def _interleave_heads(w, H):
    lead = w.shape[:-1]
    return w.reshape(*lead, 3, H, HEAD).swapaxes(-3, -2).reshape(*lead, 3 * H * HEAD)


def _deinterleave_heads(w, H):
    lead = w.shape[:-1]
    return w.reshape(*lead, H, 3, HEAD).swapaxes(-3, -2).reshape(*lead, 3 * H * HEAD)


def _interleave_head_rows(w, H):
    return w.reshape(3, H, HEAD, w.shape[-1]).swapaxes(0, 1).reshape(3 * H * HEAD, w.shape[-1])


def _deinterleave_head_rows(w, H):
    return w.reshape(H, 3, HEAD, w.shape[-1]).swapaxes(0, 1).reshape(3 * H * HEAD, w.shape[-1])


def kernel(x, norm_mix, w_in, fox_f_bias, gdn_conv_w, gdn_a_log, gdn_dt_bias, gdn_norm, w_branch_fox, w_branch_gdn, w_out, norm_ffn, w_up, ffn_conv_w, w_down, norm_final, loss_target, m_norm_mix, m_w_in, m_fox_f_bias, m_gdn_conv_w, m_gdn_a_log, m_gdn_dt_bias, m_gdn_norm, m_w_branch_fox, m_w_branch_gdn, m_w_out, m_norm_ffn, m_w_up, m_ffn_conv_w, m_w_down, m_norm_final, v_norm_mix, v_w_in, v_fox_f_bias, v_gdn_conv_w, v_gdn_a_log, v_gdn_dt_bias, v_gdn_norm, v_w_branch_fox, v_w_branch_gdn, v_w_out, v_norm_ffn, v_w_up, v_ffn_conv_w, v_w_down, v_norm_final):
    B, S, D = x.shape
    T = B * S
    H = D // HEAD
    N = S // CHUNK
    FF = w_down.shape[1] * N_CHIP
    d_in = 9 * D + 3 * H
    assert w_in.shape[2] * N_CHIP == d_in and 3 * H <= 128

    cidx = lax.axis_index("c").astype(jnp.int32)
    sidx = (2 * lax.axis_index("x") + lax.axis_index("y")).astype(jnp.int32)
    idx = jnp.stack([cidx, sidx])

    rowed = [w_branch_fox[0], w_branch_gdn[0], w_out[0], w_down[0]]
    convs = [gdn_conv_w[0], ffn_conv_w[0]]
    rowed_shapes = [a.shape for a in rowed]
    conv_shapes = [a.shape + (2,) for a in convs]
    pad_rows = lambda shapes: -(-sum(_rows_of(math.prod(s)) for s in shapes) // 256) * 128
    Rh, Rc = pad_rows(rowed_shapes), pad_rows(conv_shapes)
    halves = lambda a: a.reshape(2, a.shape[0] // 2, a.shape[1])
    c_in = w_in.shape[2]
    packs_a = [w_in[0].T.astype(BF16).reshape(c_in, 2, D // 2).transpose(1, 0, 2),
               halves(_pack_rows([lax.bitcast_convert_type(a, BF16) for a in convs], 2 * Rc, BF16))]
    packs_b = [halves(w_up[0].astype(BF16)), halves(_pack_rows([a.astype(BF16) for a in rowed], 2 * Rh, BF16))]
    own = lambda gs, ps: [lax.dynamic_update_slice(g, p[None], (sidx, 0, 0, 0)) for g, p in zip(gs, ps)]
    by_cols = lambda g: g.transpose(1, 2, 0, 3).reshape(2 * g.shape[2], N_CHIP * g.shape[3])
    cat_cols = lambda p: jnp.concatenate([p[i] for i in range(N_CHIP)], axis=-1)
    cat_rows = lambda p: p.reshape(-1, p.shape[-1])
    g_in, g_conv = own(allgather_weights(packs_a), packs_a)
    W_inT = g_in.transpose(0, 2, 1, 3).reshape(N_CHIP * c_in, D)
    conv_parts = _unpack_rows(g_conv.reshape(N_CHIP, 2 * Rc, ROW), conv_shapes)
    gconv = cat_cols(lax.bitcast_convert_type(conv_parts[0], F32))
    fconv = cat_cols(lax.bitcast_convert_type(conv_parts[1], F32))

    o1, o2 = 3 * D, 3 * D + H
    o3, o4, o5, o6 = o2 + 3 * D, o2 + 3 * D + H, o2 + 3 * D + 2 * H, o2 + 4 * D + 2 * H
    W_foxT = _interleave_head_rows(W_inT[:o1], H)
    W_gqkvT = _interleave_head_rows(W_inT[o2:o3], H)
    W_gzT = W_inT[o5:o6]
    W_gatesT = W_inT[o6:]
    W_smallT = jnp.concatenate([W_inT[o1:o2], W_inT[o3:o5], jnp.zeros((128 - 3 * H, D), BF16)], axis=0)
    gconv_i = _interleave_heads(gconv, H)
    fconv_g, fconv_v = fconv[:, :FF], fconv[:, FF:]
    prm = jnp.zeros((8, 128), F32)
    prm = prm.at[0, 0:H].set(fox_f_bias[0]).at[0, H:2 * H].set(gdn_dt_bias[0]).at[1, H:2 * H].set(gdn_a_log[0])

    x2 = x.reshape(T, D)
    tgt = loss_target.reshape(T, D)

    hn1 = rmsnorm_fwd(x2, norm_mix, "rmsnorm_mix")
    p_fox = matmul(hn1, W_foxT, "nt", "proj_fox", out_dtype=BF16)
    p_gqkv = matmul(hn1, W_gqkvT, "nt", "proj_gqkv")
    p_gz = matmul(hn1, W_gzT, "nt", "proj_gz")
    p_gates = matmul(hn1, W_gatesT, "nt", "proj_gates")
    p_small = matmul(hn1, W_smallT, "nt", "proj_small")

    sm = small_fwd(p_small, prm, B, S, H)
    heads = lambda a: a.reshape(B, S, H).transpose(0, 2, 1)
    c_bhs, gc_bhs, beta_bhs = heads(sm[:, 0:H]), heads(sm[:, H:2 * H]), heads(sm[:, 2 * H:3 * H])
    c_col, c_row = c_bhs[..., None], c_bhs[:, :, None, :]
    gcr5 = gc_bhs.reshape(B, H, N, 1, CHUNK)
    gcr_u = gc_bhs.reshape(B, H, N // PAIR, 1, PAIR * CHUNK)
    betar_u = beta_bhs.reshape(B, H, N // PAIR, 1, PAIR * CHUNK)

    (o_fox, o_fox16, lse), arriving = fox_fwd(p_fox, c_col, c_row, B, S, H, rider=_ride_gather_ici(packs_b))
    qkvn = gdn_prep_fwd(p_gqkv, gconv_i, B, S, H)
    (u_hat, w_t, t_inv), arrived = gdn_intra_fwd(qkvn, betar_u, gcr_u, B, S, H, rider=_ride_gather_d2d(arriving))
    g_up, g_rowed = own(arrived, packs_b)
    W_up = by_cols(g_up)
    W_up_g, W_up_v = W_up[:, :FF], W_up[:, FF:]
    W_bf, W_bg, W_out, W_down = (cat_rows(p) for p in _unpack_rows(g_rowed.reshape(N_CHIP, 2 * Rh, ROW), rowed_shapes))
    o_gdn, states = gdn_inter_fwd(qkvn, u_hat, w_t, gcr5, B, S, H)
    y_gdn = gdn_post_fwd(o_gdn, p_gz, gdn_norm, H)
    bf_ = matmul(o_fox16, W_bf, "nn", "branch_fox")
    bg_, y = matmul(y_gdn, W_bg, "nn", "branch_gdn", post=_post_merge(p_gates, bf_))
    h1, hn2 = matmul(y, W_out, "nn", "out_proj", add=x2, post=_post_rmsnorm(norm_ffn))
    up_g = matmul(hn2, W_up_g, "nn", "up_gate")
    up_v = matmul(hn2, W_up_v, "nn", "up_val")
    act = ffn_gate_fwd(up_g, up_v, fconv_g, fconv_v, B, S)
    dh2, dh2_16, loss_cols, d_norm_final = matmul(act, W_down, "nn", "down_proj", add=h1,
                                                  post=_post_loss(norm_final.reshape(1, D), tgt))
    loss = lax.psum(0.5 * jnp.sum(loss_cols) / D, ("x", "y", "c"))

    d_act = matmul(dh2_16, W_down, "nt", "d_act")
    dW_down = matmul(act, dh2_16, "tn", "dw_down")
    d_upg, d_upv, d_fconv_g, d_fconv_v = ffn_gate_bwd(up_g, up_v, fconv_g, fconv_v, d_act, B, S)
    d_hn2 = matmul(d_upg, W_up_g, "nt", "d_hn2_g")
    dh1, dh1_16, d_norm_ffn = matmul(d_upv, W_up_v, "nt", "d_hn2_v", add=d_hn2,
                                     post=_post_rmsnorm_bwd(h1, norm_ffn, dh2, True))
    dW_up = jnp.concatenate([matmul(hn2, d_upg, "tn", "dw_up_g"), matmul(hn2, d_upv, "tn", "dw_up_v")], axis=1)
    d_bf, d_bg, d_gates = matmul(dh1_16, W_out, "nt", "d_y", post=_post_merge_bwd(p_gates, bf_, bg_))
    dW_out = matmul(y, dh1_16, "tn", "dw_out")
    d_ofox = matmul(d_bf, W_bf, "nt", "d_ofox")
    dW_bf = matmul(o_fox16, d_bf, "tn", "dw_bf")
    d_ygdn = matmul(d_bg, W_bg, "nt", "d_ygdn")
    dW_bg = matmul(y_gdn, d_bg, "tn", "dw_bg")

    d_fconv = jnp.concatenate([d_fconv_g, d_fconv_v], axis=1)
    col_shard = lambda g, s: g[:, s * (g.shape[1] // N_CHIP):(s + 1) * (g.shape[1] // N_CHIP)]
    row_shard = lambda g, s: g[s * (g.shape[0] // N_CHIP):(s + 1) * (g.shape[0] // N_CHIP)]
    shard_items = lambda s: [row_shard(dW_bf, s), row_shard(dW_bg, s), row_shard(dW_out, s), row_shard(dW_down, s),
                             col_shard(d_fconv, s)]
    g_shapes = [a.shape for a in shard_items(0)]
    assert sum(_rows_of(math.prod(s)) for s in g_shapes) <= 2 * Rh
    to_slabs = lambda g: g.reshape(2, g.shape[0] // 2, N_CHIP, g.shape[1] // N_CHIP).transpose(0, 2, 1, 3)
    gpacks_b = [to_slabs(dW_up),
                jnp.stack([_pack_rows(shard_items(s), 2 * Rh, F32).reshape(2, Rh, ROW) for s in range(N_CHIP)], axis=1)]
    (d_pfox, d_ccol, d_crow), gots_b = fox_bwd(p_fox, c_col, c_row, o_fox, lse, d_ofox, B, S, H,
                                              rider=_ride_exchange(gpacks_b))
    sums_b = [add_halves(g, got, idx, "add_halves_b%d" % i) for i, (g, got) in enumerate(zip(gpacks_b, gots_b))]

    d_ogdn, d_gz, d_gdn_norm = gdn_post_bwd(o_gdn, p_gz, gdn_norm, d_ygdn, H)
    (dq_i, dk_i, d_uh, d_wt, dgcr_a), got16_b = gdn_inter_bwd(qkvn, u_hat, w_t, gcr5, states, d_ogdn, B, S, H,
                                                             rider=_ride_scatter([s16 for _, s16 in sums_b]))
    mine_b = [add_chips(s32, g16, idx, "add_chips_b%d" % i) for i, ((s32, _), g16) in enumerate(zip(sums_b, got16_b))]
    d_qkvn, d_betar5, dgcr_b = gdn_intra_bwd(qkvn, betar_u, gcr_u, t_inv, d_uh, d_wt, dq_i, dk_i, B, S, H)
    d_pgqkv, d_gconv_i = gdn_prep_bwd(p_gqkv, gconv_i, d_qkvn, B, S, H)

    tokens = lambda a: a.reshape(B, H, S).transpose(0, 2, 1).reshape(T, H)
    d_gc = dgcr_a.reshape(B, H, S) + dgcr_b.reshape(B, H, S)
    d_sm = jnp.concatenate([tokens(d_ccol.reshape(B, H, S) + d_crow.reshape(B, H, S)), tokens(d_gc), tokens(d_betar5.reshape(B, H, S)),
                            jnp.zeros((T, 128 - 3 * H), F32)], axis=1)
    d_psmall, d_prm = small_bwd(p_small, prm, d_sm, B, S, H)

    d_hn1 = matmul(d_pfox, W_foxT, "nn", "d_hn1_fox")
    d_hn1 = matmul(d_pgqkv, W_gqkvT, "nn", "d_hn1_gqkv", add=d_hn1)
    d_hn1 = matmul(d_gz, W_gzT, "nn", "d_hn1_gz", add=d_hn1)
    d_hn1 = matmul(d_gates, W_gatesT, "nn", "d_hn1_gates", add=d_hn1)
    grad_x, d_norm_mix = matmul(d_psmall, W_smallT, "nn", "d_hn1_small", add=d_hn1,
                                post=_post_rmsnorm_bwd(x2, norm_mix, dh1, False))
    dW_foxT = matmul(d_pfox, hn1, "tn", "dw_fox")
    dW_gqkvT = matmul(d_pgqkv, hn1, "tn", "dw_gqkv")
    dW_gzT = matmul(d_gz, hn1, "tn", "dw_gz")
    dW_gatesT = matmul(d_gates, hn1, "tn", "dw_gates")
    dW_smallT = matmul(d_psmall, hn1, "tn", "dw_small")

    dW_inT = jnp.concatenate([_deinterleave_head_rows(dW_foxT, H), dW_smallT[0:H], _deinterleave_head_rows(dW_gqkvT, H),
                              dW_smallT[H:3 * H], dW_gzT, dW_gatesT], axis=0)
    d_gconv = _deinterleave_heads(d_gconv_i, H)

    gpack_a = [dW_inT.reshape(N_CHIP, c_in, 2, D // 2).transpose(2, 0, 1, 3)]
    gots_a = _run_alone(_ride_exchange(gpack_a), "exchange_halves")
    sums_a = [add_halves(gpack_a[0], gots_a[0], idx, "add_halves_a")]
    got16_a = _run_alone(_ride_scatter([sums_a[0][1]]), "scatter_chips")
    mine = [add_chips(sums_a[0][0], got16_a[0], idx, "add_chips_a")] + mine_b
    others = share_halves(mine)
    g_w_inT, g_up, g_rows = (jnp.concatenate([jnp.where(cidx == 0, h, o), jnp.where(cidx == 0, o, h)], axis=ax)
                             for h, o, ax in zip(mine, others, (1, 0, 0)))
    g_w_in = g_w_inT.T
    g_bf, g_bg, g_out, g_down, g_fconv = _unpack_rows(g_rows, g_shapes)

    small_items = [d_norm_mix, d_norm_ffn, d_norm_final, d_gdn_norm, d_prm, d_gconv]
    small_shapes = [a.shape for a in small_items]
    sv = allreduce_small(_pack_rows(small_items, 0, F32, unit=8))
    g_norm_mix, g_norm_ffn, g_norm_final, g_gdn_norm, g_prm, g_gconv_all = _unpack_rows(sv, small_shapes, unit=8)
    g_norm_final = g_norm_final.reshape(D)
    g_fbias, g_dtb, g_alog = g_prm[0:1, 0:H], g_prm[0:1, H:2 * H], g_prm[1:2, H:2 * H]
    g_gconv = lax.dynamic_slice_in_dim(g_gconv_all, sidx * (3 * D // N_CHIP), 3 * D // N_CHIP, axis=1)

    names = ["norm_mix", "w_in", "fox_f_bias", "gdn_conv_w", "gdn_a_log", "gdn_dt_bias", "gdn_norm", "w_branch_fox",
             "w_branch_gdn", "w_out", "norm_ffn", "w_up", "ffn_conv_w", "w_down", "norm_final"]
    ws = [norm_mix, w_in, fox_f_bias, gdn_conv_w, gdn_a_log, gdn_dt_bias, gdn_norm, w_branch_fox, w_branch_gdn, w_out,
          norm_ffn, w_up, ffn_conv_w, w_down, norm_final]
    ms = [m_norm_mix, m_w_in, m_fox_f_bias, m_gdn_conv_w, m_gdn_a_log, m_gdn_dt_bias, m_gdn_norm, m_w_branch_fox,
          m_w_branch_gdn, m_w_out, m_norm_ffn, m_w_up, m_ffn_conv_w, m_w_down, m_norm_final]
    vs = [v_norm_mix, v_w_in, v_fox_f_bias, v_gdn_conv_w, v_gdn_a_log, v_gdn_dt_bias, v_gdn_norm, v_w_branch_fox,
          v_w_branch_gdn, v_w_out, v_norm_ffn, v_w_up, v_ffn_conv_w, v_w_down, v_norm_final]
    gs = [g_norm_mix, g_w_in, g_fbias, g_gconv, g_alog, g_dtb, g_gdn_norm, g_bf, g_bg, g_out, g_norm_ffn, g_up,
          g_fconv, g_down, g_norm_final]
    gs = [g.reshape(w.shape) for g, w in zip(gs, ws)]
    deltas, new_ms, new_vs = [], [], []
    for nm, w, g, m, v in zip(names, ws, gs, ms, vs):
        if w.ndim == 1:
            d, a, b = adamw(w.reshape(1, -1), g.reshape(1, -1), m.reshape(1, -1), v.reshape(1, -1), "adamw_" + nm)
            d, a, b = d.reshape(w.shape), a.reshape(w.shape), b.reshape(w.shape)
        elif nm == "w_in":
            d, a, b = (r.T[None] for r in adamw(w[0].T, g_w_inT, m[0].T, v[0].T, "adamw_" + nm))
        else:
            d, a, b = adamw(w, g, m, v, "adamw_" + nm)
        deltas.append(d)
        new_ms.append(a)
        new_vs.append(b)

    return (loss, grad_x.reshape(B, S, D), *gs, *deltas, *new_ms, *new_vs)
```

```python
import functools
import math

import jax
import jax.numpy as jnp
from jax import lax
from jax.experimental import pallas as pl
from jax.experimental.pallas import tpu as pltpu

F32 = jnp.float32
BF16 = jnp.bfloat16
HEAD = 128
CHUNK = 64
GDN_CONV = 4
FFN_CONV = 3
EPS = 1e-6
NEG = -1e30
ROW = 1024
ATT_TILE = 512
MM_WEIGHT_TILE_BYTES = 8 << 20
N_CHIP = 4
N_DEV = 8
MESH = pl.DeviceIdType.MESH
HI = lax.Precision.HIGH
EXACT = lax.Precision.HIGHEST

ADAM_LR, ADAM_B1, ADAM_B2, ADAM_EPS, ADAM_WD, ADAM_STEP = 0.001, 0.9, 0.999, 1e-08, 0.01, 10


def _tile(n, cap, unit=128):
    best = None
    t = unit
    while t <= min(n, cap):
        if n % t == 0:
            best = t
        t += unit
    return best if best is not None else n


def _params(*sem):
    return pltpu.CompilerParams(dimension_semantics=sem)


_NN = (((1,), (0,)), ((), ()))
_NT = (((1,), (1,)), ((), ()))
_TN = (((0,), (0,)), ((), ()))


def _dg(a, b, dims, hi):
    if hi:
        return lax.dot_general(a, b, dims, precision=HI, preferred_element_type=F32)
    return lax.dot_general(a.astype(BF16), b.astype(BF16), dims, preferred_element_type=F32)


class _RawOps:
    @staticmethod
    def nn(a, b, hi=False):
        return _dg(a, b, _NN, hi)

    @staticmethod
    def nt(a, b, hi=False):
        return _dg(a, b, _NT, hi)

    @staticmethod
    def tn(a, b, hi=False):
        return _dg(a, b, _TN, hi)


def _make_diff_ops():
    def build(hi):
        @jax.custom_vjp
        def nn(a, b):
            return _dg(a, b, _NN, hi)

        nn.defvjp(lambda a, b: (_dg(a, b, _NN, hi), (a, b)),
                  lambda r, g: (_dg(g, r[1], _NT, hi), _dg(r[0], g, _TN, hi)))

        @jax.custom_vjp
        def nt(a, b):
            return _dg(a, b, _NT, hi)

        nt.defvjp(lambda a, b: (_dg(a, b, _NT, hi), (a, b)),
                  lambda r, g: (_dg(g, r[1], _NN, hi), _dg(g, r[0], _TN, hi)))

        @jax.custom_vjp
        def tn(a, b):
            return _dg(a, b, _TN, hi)

        tn.defvjp(lambda a, b: (_dg(a, b, _TN, hi), (a, b)),
                  lambda r, g: (_dg(r[1], g, _NT, hi), _dg(r[0], g, _NN, hi)))
        return nn, nt, tn

    lo, hi_ = build(False), build(True)

    class _DiffOps:
        @staticmethod
        def nn(a, b, hi=False):
            return (hi_ if hi else lo)[0](a, b)

        @staticmethod
        def nt(a, b, hi=False):
            return (hi_ if hi else lo)[1](a, b)

        @staticmethod
        def tn(a, b, hi=False):
            return (hi_ if hi else lo)[2](a, b)

    return _DiffOps


_DiffOps = _make_diff_ops()


def _sigmoid(x):
    return 1.0 / (1.0 + jnp.exp(-x))


def _mm_tile(n, pref):
    if n % pref == 0:
        return pref
    if n % 1408 == 0:
        return 1408
    return _tile(n, pref)


class _Post:
    def __init__(self, fn, row_ins=(), vec_ins=(), row_outs=(), acc_outs=(), keep_main=True):
        self.fn, self.keep_main = fn, keep_main
        self.row_ins = [r if isinstance(r, tuple) else (r, r.shape[1], 0) for r in row_ins]
        self.vec_ins, self.row_outs, self.acc_outs = list(vec_ins), list(row_outs), list(acc_outs)


def matmul(a, b, mode, name, add=None, out_dtype=F32, post=None):
    if mode == "nn":
        (M, K), (K2, N) = a.shape, b.shape
    elif mode == "nt":
        (M, K), (N, K2) = a.shape, b.shape
    else:
        (K, M), (K2, N) = a.shape, b.shape
    assert K == K2, (name, a.shape, b.shape)
    tn = _mm_tile(N, 1024)
    if mode == "tn":
        tm = M if M <= 1408 else _mm_tile(M, 1408)
        tk = _mm_tile(K, 1024)
    else:
        tk = K if K * tn * 2 <= MM_WEIGHT_TILE_BYTES else _mm_tile(K, 1024)
        tm = _mm_tile(M, 1024 if tk <= 2048 and post is None else 512)
    nk = K // tk
    assert post is None or (mode != "tn" and tn == N), name
    dims = {"nn": _NN, "nt": _NT, "tn": _TN}[mode]
    if mode == "tn":
        a_spec = pl.BlockSpec((tk, tm), lambda j, i, k: (k, i))
    else:
        a_spec = pl.BlockSpec((tm, tk), lambda j, i, k: (i, k))
    if mode == "nt":
        b_spec = pl.BlockSpec((tn, tk), lambda j, i, k: (j, k))
    else:
        b_spec = pl.BlockSpec((tk, tn), lambda j, i, k: (k, j))
    o_spec = pl.BlockSpec((tm, tn), lambda j, i, k: (i, j))
    has_add = add is not None
    keep_main = post is None or post.keep_main
    counts = [2 + has_add] + ([len(post.row_ins), len(post.vec_ins)] if post else [0, 0]) + [int(keep_main)]
    counts += ([len(post.row_outs), len(post.acc_outs)] if post else [0, 0]) + [int(nk > 1)]

    def body(*refs):
        parts, p = [], 0
        for cnt in counts:
            parts.append(refs[p:p + cnt])
            p += cnt
        core, row_ins, vec_ins, main, row_outs, acc_outs, acc = parts
        a_ref, b_ref = core[:2]
        prod = lax.dot_general(a_ref[...].astype(BF16), b_ref[...].astype(BF16), dims, preferred_element_type=F32)

        def finish(r):
            if has_add:
                r = r + core[2][...]
            if keep_main:
                main[0][...] = r.astype(out_dtype)
            if post is not None:
                @pl.when(pl.program_id(1) == 0)
                def _():
                    for ref in acc_outs:
                        ref[...] = jnp.zeros_like(ref)

                post.fn(r, row_ins, vec_ins, row_outs, acc_outs)

        if nk == 1:
            finish(prod)
            return
        acc_ref = acc[0]
        k = pl.program_id(2)

        @pl.when(k == 0)
        def _():
            acc_ref[...] = jnp.zeros_like(acc_ref)

        acc_ref[...] += prod

        @pl.when(k == nk - 1)
        def _():
            finish(acc_ref[...])

    in_specs = [a_spec, b_spec] + ([o_spec] if has_add else [])
    args = (a, b) + ((add,) if has_add else ())
    out_specs = [o_spec] if keep_main else []
    out_shape = [jax.ShapeDtypeStruct((M, N), out_dtype)] if keep_main else []
    if post is not None:
        in_specs += [pl.BlockSpec((tm, cols), lambda j, i, k, cb=cb: (i, cb)) for _, cols, cb in post.row_ins]
        in_specs += [pl.BlockSpec((1, v.shape[1]), lambda j, i, k: (0, 0)) for v in post.vec_ins]
        args += tuple(r for r, _, _ in post.row_ins) + tuple(post.vec_ins)
        out_specs += [pl.BlockSpec((tm, cols), lambda j, i, k: (i, 0)) for cols, _ in post.row_outs]
        out_specs += [pl.BlockSpec((1, cols), lambda j, i, k: (0, 0)) for cols in post.acc_outs]
        out_shape += [jax.ShapeDtypeStruct((M, cols), dt) for cols, dt in post.row_outs]
        out_shape += [jax.ShapeDtypeStruct((1, cols), F32) for cols in post.acc_outs]
    rows_sem = "arbitrary" if post is not None and post.acc_outs else "parallel"
    res = pl.pallas_call(
        body, name=name, grid=(N // tn, M // tm, nk), in_specs=in_specs, out_specs=out_specs, out_shape=out_shape,
        scratch_shapes=[pltpu.VMEM((tm, tn), F32)] if nk > 1 else [],
        compiler_params=_params("parallel", rows_sem, "arbitrary"),
    )(*args)
    return res[0] if post is None else res


def rmsnorm_fwd(x, g, name):
    T, D = x.shape
    tm = _tile(T, 512, 8)

    def body(x_ref, g_ref, o_ref):
        xv = x_ref[...]
        r = lax.rsqrt(jnp.mean(xv * xv, axis=-1, keepdims=True) + EPS)
        o_ref[...] = (xv * r * g_ref[...]).astype(BF16)

    return pl.pallas_call(
        body, name=name, grid=(T // tm,),
        in_specs=[pl.BlockSpec((tm, D), lambda i: (i, 0)), pl.BlockSpec((1, D), lambda i: (0, 0))],
        out_specs=pl.BlockSpec((tm, D), lambda i: (i, 0)),
        out_shape=jax.ShapeDtypeStruct((T, D), BF16),
        compiler_params=_params("parallel"),
    )(x, g)


def _post_rmsnorm(g):
    def fn(r, row_ins, vec_ins, row_outs, acc_outs):
        rs = lax.rsqrt(jnp.mean(r * r, axis=-1, keepdims=True) + EPS)
        row_outs[0][...] = (r * rs * vec_ins[0][...]).astype(BF16)

    return _Post(fn, vec_ins=[g], row_outs=[(g.shape[1], BF16)])


def _post_rmsnorm_bwd(x, g, dres, with_bf16):
    D = g.shape[1]

    def fn(dy, row_ins, vec_ins, row_outs, acc_outs):
        xv = row_ins[0][...]
        rs = lax.rsqrt(jnp.mean(xv * xv, axis=-1, keepdims=True) + EPS)
        xh = xv * rs
        acc_outs[0][...] += jnp.sum(dy * xh, axis=0, keepdims=True)
        dxh = dy * vec_ins[0][...]
        dx = row_ins[1][...] + rs * (dxh - xh * jnp.mean(dxh * xh, axis=-1, keepdims=True))
        row_outs[0][...] = dx
        if with_bf16:
            row_outs[1][...] = dx.astype(BF16)

    return _Post(fn, row_ins=[x, dres], vec_ins=[g], row_outs=[(D, F32)] + ([(D, BF16)] if with_bf16 else []),
                 acc_outs=[D], keep_main=False)


def _post_loss(g, target):
    D = g.shape[1]

    def fn(hv, row_ins, vec_ins, row_outs, acc_outs):
        rs = lax.rsqrt(jnp.mean(hv * hv, axis=-1, keepdims=True) + EPS)
        xh = hv * rs
        gv = vec_ins[0][...]
        err = xh * gv - row_ins[0][...]
        acc_outs[0][...] += jnp.sum(err * err, axis=0, keepdims=True)
        dy = err * (1.0 / D)
        acc_outs[1][...] += jnp.sum(dy * xh, axis=0, keepdims=True)
        dxh = dy * gv
        dh = rs * (dxh - xh * jnp.mean(dxh * xh, axis=-1, keepdims=True))
        row_outs[0][...] = dh
        row_outs[1][...] = dh.astype(BF16)

    return _Post(fn, row_ins=[target], vec_ins=[g], row_outs=[(D, F32), (D, BF16)], acc_outs=[D, D], keep_main=False)


def _shift_down(x, k):
    if k == 0:
        return x
    rows = lax.broadcasted_iota(jnp.int32, x.shape, 0)
    return jnp.where(rows >= k, pltpu.roll(x, k, 0), 0.0)


def _shift_up(x, k):
    if k == 0:
        return x
    s = x.shape[0]
    rows = lax.broadcasted_iota(jnp.int32, x.shape, 0)
    return jnp.where(rows < s - k, pltpu.roll(x, s - k, 0), 0.0)


def _conv_fwd(x, w_ref, kw):
    y = x * w_ref[kw - 1:kw, :]
    for i in range(kw - 1):
        y = y + _shift_down(x, kw - 1 - i) * w_ref[i:i + 1, :]
    return y


def _conv_bwd(x, dy, w_ref, kw):
    dx = dy * w_ref[kw - 1:kw, :]
    dws = []
    for i in range(kw - 1):
        dx = dx + _shift_up(dy, kw - 1 - i) * w_ref[i:i + 1, :]
        dws.append(jnp.sum(dy * _shift_down(x, kw - 1 - i), axis=0, keepdims=True))
    dws.append(jnp.sum(dy * x, axis=0, keepdims=True))
    return dx, dws


def ffn_gate_fwd(up_g, up_v, cw_g, cw_v, B, S):
    T, Fd = up_g.shape
    tc = _tile(Fd, 256)

    def body(g_ref, v_ref, wg_ref, wv_ref, o_ref):
        ug = _conv_fwd(g_ref[...], wg_ref, FFN_CONV)
        uv = _conv_fwd(v_ref[...], wv_ref, FFN_CONV)
        o_ref[...] = (ug * _sigmoid(ug) * uv).astype(BF16)

    blk = pl.BlockSpec((S, tc), lambda b, j: (b, j))
    wblk = pl.BlockSpec((FFN_CONV, tc), lambda b, j: (0, j))
    return pl.pallas_call(
        body, name="ffn_gate_fwd", grid=(B, Fd // tc), in_specs=[blk, blk, wblk, wblk], out_specs=blk,
        out_shape=jax.ShapeDtypeStruct((T, Fd), BF16), compiler_params=_params("parallel", "parallel"),
    )(up_g, up_v, cw_g, cw_v)


def ffn_gate_bwd(up_g, up_v, cw_g, cw_v, d_act, B, S):
    T, Fd = up_g.shape
    tc = _tile(Fd, 256)

    def body(g_ref, v_ref, wg_ref, wv_ref, da_ref, dg_ref, dv_ref, dwg_ref, dwv_ref):
        @pl.when(pl.program_id(1) == 0)
        def _():
            dwg_ref[...] = jnp.zeros_like(dwg_ref)
            dwv_ref[...] = jnp.zeros_like(dwv_ref)

        xg, xv = g_ref[...], v_ref[...]
        ug = _conv_fwd(xg, wg_ref, FFN_CONV)
        uv = _conv_fwd(xv, wv_ref, FFN_CONV)
        da = da_ref[...]
        sg = _sigmoid(ug)
        d_ug = da * uv * (sg + ug * sg * (1.0 - sg))
        d_uv = da * ug * sg
        dxg, dwg = _conv_bwd(xg, d_ug, wg_ref, FFN_CONV)
        dxv, dwv = _conv_bwd(xv, d_uv, wv_ref, FFN_CONV)
        dg_ref[...] = dxg.astype(BF16)
        dv_ref[...] = dxv.astype(BF16)
        for i in range(FFN_CONV):
            dwg_ref[i:i + 1, :] += dwg[i]
            dwv_ref[i:i + 1, :] += dwv[i]

    blk = pl.BlockSpec((S, tc), lambda j, b: (b, j))
    wblk = pl.BlockSpec((FFN_CONV, tc), lambda j, b: (0, j))
    return pl.pallas_call(
        body, name="ffn_gate_bwd", grid=(Fd // tc, B), in_specs=[blk, blk, wblk, wblk, blk],
        out_specs=[blk, blk, wblk, wblk],
        out_shape=[jax.ShapeDtypeStruct((T, Fd), BF16), jax.ShapeDtypeStruct((T, Fd), BF16),
                   jax.ShapeDtypeStruct((FFN_CONV, Fd), F32), jax.ShapeDtypeStruct((FFN_CONV, Fd), F32)],
        compiler_params=_params("parallel", "arbitrary"),
    )(up_g, up_v, cw_g, cw_v, d_act)


def _post_merge(p_gates, bf_):
    D = bf_.shape[1]

    def fn(bg, row_ins, vec_ins, row_outs, acc_outs):
        gf_ref, gg_ref, bf_ref = row_ins
        row_outs[0][...] = (_sigmoid(gf_ref[...]) * bf_ref[...] + _sigmoid(gg_ref[...]) * bg).astype(BF16)

    return _Post(fn, row_ins=[(p_gates, D, 0), (p_gates, D, 1), bf_], row_outs=[(D, BF16)])


def _post_merge_bwd(p_gates, bf_, bg_):
    D = bf_.shape[1]

    def fn(d, row_ins, vec_ins, row_outs, acc_outs):
        gf_ref, gg_ref, bf_ref, bg_ref = row_ins
        sf, sg = _sigmoid(gf_ref[...]), _sigmoid(gg_ref[...])
        row_outs[0][...] = (d * sf).astype(BF16)
        row_outs[1][...] = (d * sg).astype(BF16)
        row_outs[2][:, 0:D] = (d * bf_ref[...] * sf * (1.0 - sf)).astype(BF16)
        row_outs[2][:, D:2 * D] = (d * bg_ref[...] * sg * (1.0 - sg)).astype(BF16)

    return _Post(fn, row_ins=[(p_gates, D, 0), (p_gates, D, 1), bf_, bg_],
                 row_outs=[(D, BF16), (D, BF16), (2 * D, BF16)], keep_main=False)


def fox_fwd(p_fox, c_col, c_row, B, S, H, rider=None):
    T = B * S
    t = _tile(S, ATT_TILE)
    nq = S // t
    scale = HEAD ** -0.5

    def body(q_ref, k_ref, v_ref, cq_ref, cr_ref, o_ref, o16_ref, lse_ref):
        i = pl.program_id(2)
        q = q_ref[...]
        cq = cq_ref[...]
        row = lax.broadcasted_iota(jnp.int32, (t, t), 0)
        col = lax.broadcasted_iota(jnp.int32, (t, t), 1)

        def step(j, carry, diagonal):
            m, l, acc = carry
            off = pl.multiple_of(j * t, t)
            k = k_ref[pl.ds(off, t), :]
            v = v_ref[pl.ds(off, t), :]
            s = lax.dot_general(q, k, _NT, preferred_element_type=F32) * scale + (cq - cr_ref[:, pl.ds(off, t)])
            if diagonal:
                s = jnp.where(col <= row, s, NEG)
            m_new = jnp.maximum(m, jnp.max(s, axis=-1, keepdims=True))
            alpha = jnp.exp(m - m_new)
            p = jnp.exp(s - m_new)
            l = alpha * l + jnp.sum(p, axis=-1, keepdims=True)
            acc = alpha * acc + lax.dot_general(p.astype(BF16), v, _NN, preferred_element_type=F32)
            return m_new, l, acc

        m0 = jnp.full((t, 1), NEG, F32)
        below = lax.fori_loop(0, i, functools.partial(step, diagonal=False),
                              (m0, jnp.zeros((t, 1), F32), jnp.zeros((t, HEAD), F32)))
        m, l, acc = step(i, below, diagonal=True)
        o = acc / l
        o_ref[...] = o
        o16_ref[...] = o.astype(BF16)
        lse_ref[...] = m + jnp.log(l)

    return _hosted_call(
        body, rider, name="fox_fwd", grid=(B, H, nq),
        in_specs=[pl.BlockSpec((t, HEAD), lambda b, h, i: (b * nq + i, 3 * h)),
                  pl.BlockSpec((S, HEAD), lambda b, h, i: (b, 3 * h + 1)),
                  pl.BlockSpec((S, HEAD), lambda b, h, i: (b, 3 * h + 2)),
                  pl.BlockSpec((None, None, t, 1), lambda b, h, i: (b, h, i, 0)),
                  pl.BlockSpec((None, None, 1, S), lambda b, h, i: (b, h, 0, 0))],
        out_specs=[pl.BlockSpec((t, HEAD), lambda b, h, i: (b * nq + i, h)),
                   pl.BlockSpec((t, HEAD), lambda b, h, i: (b * nq + i, h)),
                   pl.BlockSpec((None, None, t, 1), lambda b, h, i: (b, h, i, 0))],
        out_shape=[jax.ShapeDtypeStruct((T, H * HEAD), F32), jax.ShapeDtypeStruct((T, H * HEAD), BF16),
                   jax.ShapeDtypeStruct((B, H, S, 1), F32)],
        scratch_shapes=[], semantics=("parallel", "parallel", "arbitrary"),
    )(p_fox, p_fox, p_fox, c_col, c_row)


def fox_bwd(p_fox, c_col, c_row, o, lse, do, B, S, H, rider=None):
    T = B * S
    t = _tile(S, ATT_TILE)
    n = S // t
    scale = HEAD ** -0.5

    def body(q_ref, k_ref, v_ref, cq_ref, cr_ref, o_ref, lse_ref, do_ref, dqkv_ref, dcq_ref, dcr_ref, dq_acc, delta_s):
        row = lax.broadcasted_iota(jnp.int32, (t, t), 0)
        col = lax.broadcasted_iota(jnp.int32, (t, t), 1)

        def prep(i, c):
            rows = pl.ds(pl.multiple_of(i * t, t), t)
            delta_s[rows, :] = jnp.sum(do_ref[rows, :] * o_ref[rows, :], axis=-1, keepdims=True)
            dq_acc[rows, :] = jnp.zeros((t, HEAD), F32)
            dcq_ref[rows, :] = jnp.zeros((t, 1), F32)
            return c

        lax.fori_loop(0, n, prep, 0)

        def kv_step(j, c):
            joff = pl.multiple_of(j * t, t)
            k = k_ref[pl.ds(joff, t), :]
            v = v_ref[pl.ds(joff, t), :]
            crj = cr_ref[:, pl.ds(joff, t)]

            def q_step(i, carry, diagonal):
                dk, dv, dc = carry
                rows = pl.ds(pl.multiple_of(i * t, t), t)
                q = q_ref[rows, :]
                dob = do_ref[rows, :].astype(BF16)
                s = lax.dot_general(q, k, _NT, preferred_element_type=F32) * scale + (cq_ref[rows, :] - crj)
                if diagonal:
                    s = jnp.where(col <= row, s, NEG)
                p = jnp.exp(s - lse_ref[rows, :])
                dp = lax.dot_general(dob, v, _NT, preferred_element_type=F32)
                ds = p * (dp - delta_s[rows, :])
                dsb = ds.astype(BF16)
                dv = dv + lax.dot_general(p.astype(BF16), dob, _TN, preferred_element_type=F32)
                dk = dk + lax.dot_general(dsb, q, _TN, preferred_element_type=F32)
                dq_acc[rows, :] += lax.dot_general(dsb, k, _NN, preferred_element_type=F32) * scale
                dc = dc + jnp.sum(ds, axis=0, keepdims=True)
                dcq_ref[rows, :] += jnp.sum(ds, axis=-1, keepdims=True)
                return dk, dv, dc

            z = jnp.zeros((t, HEAD), F32)
            on_diagonal = q_step(j, (z, z, jnp.zeros((1, t), F32)), diagonal=True)
            dk, dv, dc = lax.fori_loop(j + 1, n, functools.partial(q_step, diagonal=False), on_diagonal)
            dqkv_ref[pl.ds(joff, t), HEAD:2 * HEAD] = (dk * scale).astype(BF16)
            dqkv_ref[pl.ds(joff, t), 2 * HEAD:3 * HEAD] = dv.astype(BF16)
            dcr_ref[:, pl.ds(joff, t)] = -dc
            return c

        lax.fori_loop(0, n, kv_step, 0)
        dqkv_ref[:, 0:HEAD] = dq_acc[...].astype(BF16)

    col_spec = pl.BlockSpec((None, None, S, 1), lambda b, h: (b, h, 0, 0))
    row_spec = pl.BlockSpec((None, None, 1, S), lambda b, h: (b, h, 0, 0))
    head = pl.BlockSpec((S, HEAD), lambda b, h: (b, h))
    return _hosted_call(
        body, rider, name="fox_bwd", grid=(B, H),
        in_specs=[pl.BlockSpec((S, HEAD), lambda b, h: (b, 3 * h)),
                  pl.BlockSpec((S, HEAD), lambda b, h: (b, 3 * h + 1)),
                  pl.BlockSpec((S, HEAD), lambda b, h: (b, 3 * h + 2)),
                  col_spec, row_spec, head, col_spec, head],
        out_specs=[pl.BlockSpec((S, 3 * HEAD), lambda b, h: (b, h)), col_spec, row_spec],
        out_shape=[jax.ShapeDtypeStruct((T, 3 * H * HEAD), BF16), jax.ShapeDtypeStruct((B, H, S, 1), F32),
                   jax.ShapeDtypeStruct((B, H, 1, S), F32)],
        scratch_shapes=[pltpu.VMEM((S, HEAD), F32), pltpu.VMEM((S, 1), F32)], semantics=("parallel", "parallel"),
    )(p_fox, p_fox, p_fox, c_col, c_row, o, lse, do)


def _small_fn(x, b0, b1, H):
    S = x.shape[0]
    lane = lax.broadcasted_iota(jnp.int32, x.shape, 1)
    z = x + b0
    tail = jnp.log1p(jnp.exp(-jnp.abs(z)))
    softplus = jnp.maximum(z, 0.0) + tail
    logsig = -(jnp.maximum(-z, 0.0) + tail)
    g = -jnp.exp(b1) * softplus
    pre = jnp.where(lane < H, logsig, jnp.where(lane < 2 * H, g, 0.0))
    bl = _tile(S, 256, CHUNK)
    r = lax.broadcasted_iota(jnp.int32, (bl, bl), 0)
    c = lax.broadcasted_iota(jnp.int32, (bl, bl), 1)
    tri = (r >= c).astype(F32)
    tri_chunk = jnp.where((r >= c) & (jnp.right_shift(r, 6) == jnp.right_shift(c, 6)), 1.0, 0.0)
    carry = jnp.zeros((1, x.shape[1]), F32)
    parts = []
    for i in range(S // bl):
        blk = pre[i * bl:(i + 1) * bl, :]
        full = lax.dot_general(tri, blk, _NN, precision=EXACT, preferred_element_type=F32) + carry
        chunked = lax.dot_general(tri_chunk, blk, _NN, precision=EXACT, preferred_element_type=F32)
        parts.append(jnp.where(lane[:bl] < H, full, chunked))
        carry = carry + jnp.sum(blk, axis=0, keepdims=True)
    cum = parts[0] if len(parts) == 1 else jnp.concatenate(parts, axis=0)
    return jnp.where(lane < 2 * H, cum, jnp.where(lane < 3 * H, _sigmoid(x), 0.0))


def small_fwd(p_small, prm, B, S, H):
    T = B * S

    def body(x_ref, p_ref, o_ref):
        o_ref[...] = _small_fn(x_ref[...], p_ref[0:1, :], p_ref[1:2, :], H)

    blk = pl.BlockSpec((S, 128), lambda b: (b, 0))
    return pl.pallas_call(
        body, name="small_fwd", grid=(B,), in_specs=[blk, pl.BlockSpec((8, 128), lambda b: (0, 0))], out_specs=blk,
        out_shape=jax.ShapeDtypeStruct((T, 128), F32), compiler_params=_params("parallel"),
    )(p_small, prm)


def small_bwd(p_small, prm, d_out, B, S, H):
    T = B * S

    def body(x_ref, p_ref, d_ref, dx_ref, dp_ref):
        @pl.when(pl.program_id(0) == 0)
        def _():
            dp_ref[...] = jnp.zeros_like(dp_ref)

        _, vjp = jax.vjp(functools.partial(_small_fn, H=H), x_ref[...], p_ref[0:1, :], p_ref[1:2, :])
        dx, db0, db1 = vjp(d_ref[...])
        dx_ref[...] = dx.astype(BF16)
        dp_ref[0:1, :] += db0
        dp_ref[1:2, :] += db1

    blk = pl.BlockSpec((S, 128), lambda b: (b, 0))
    pblk = pl.BlockSpec((8, 128), lambda b: (0, 0))
    return pl.pallas_call(
        body, name="small_bwd", grid=(B,), in_specs=[blk, pblk, blk], out_specs=[blk, pblk],
        out_shape=[jax.ShapeDtypeStruct((T, 128), BF16), jax.ShapeDtypeStruct((8, 128), F32)],
        compiler_params=_params("arbitrary"),
    )(p_small, prm, d_out)


def gdn_prep_fwd(p_gqkv, cw, B, S, H):
    T = B * S

    def body(x_ref, w_ref, o_ref):
        y = _conv_fwd(x_ref[...], w_ref, GDN_CONV)
        a = y * _sigmoid(y)
        rs = lax.rsqrt(jnp.sum(a * a, axis=-1, keepdims=True) + EPS)
        is_qk = (pl.program_id(1) % 3) < 2
        o_ref[...] = a * jnp.where(is_qk, rs, 1.0)

    blk = pl.BlockSpec((S, HEAD), lambda b, n: (b, n))
    wblk = pl.BlockSpec((GDN_CONV, HEAD), lambda b, n: (0, n))
    return pl.pallas_call(
        body, name="gdn_prep_fwd", grid=(B, 3 * H), in_specs=[blk, wblk], out_specs=blk,
        out_shape=jax.ShapeDtypeStruct((T, 3 * H * HEAD), F32), compiler_params=_params("parallel", "parallel"),
    )(p_gqkv, cw)


def gdn_prep_bwd(p_gqkv, cw, d_out, B, S, H):
    T = B * S

    def body(x_ref, w_ref, d_ref, dx_ref, dw_ref):
        @pl.when(pl.program_id(1) == 0)
        def _():
            dw_ref[...] = jnp.zeros_like(dw_ref)

        x = x_ref[...]
        y = _conv_fwd(x, w_ref, GDN_CONV)
        sg = _sigmoid(y)
        a = y * sg
        rs = lax.rsqrt(jnp.sum(a * a, axis=-1, keepdims=True) + EPS)
        d = d_ref[...]
        out = a * rs
        da_qk = rs * (d - out * jnp.sum(d * out, axis=-1, keepdims=True))
        is_qk = (pl.program_id(0) % 3) < 2
        da = jnp.where(is_qk, da_qk, d)
        dy = da * (sg + y * sg * (1.0 - sg))
        dx, dws = _conv_bwd(x, dy, w_ref, GDN_CONV)
        dx_ref[...] = dx.astype(BF16)
        for i in range(GDN_CONV):
            dw_ref[i:i + 1, :] += dws[i]

    blk = pl.BlockSpec((S, HEAD), lambda n, b: (b, n))
    wblk = pl.BlockSpec((GDN_CONV, HEAD), lambda n, b: (0, n))
    return pl.pallas_call(
        body, name="gdn_prep_bwd", grid=(3 * H, B), in_specs=[blk, wblk, blk], out_specs=[blk, wblk],
        out_shape=[jax.ShapeDtypeStruct((T, 3 * H * HEAD), BF16), jax.ShapeDtypeStruct((GDN_CONV, 3 * H * HEAD), F32)],
        compiler_params=_params("parallel", "arbitrary"),
    )(p_gqkv, cw, d_out)


@jax.custom_vjp
def _given_inverse(a, t):
    return t


def _given_inverse_fwd(a, t):
    return t, t


def _given_inverse_bwd(t, g):
    x = _dg(t, g, _TN, True)
    return -_dg(x, t, _NT, True), jnp.zeros_like(t)


_given_inverse.defvjp(_given_inverse_fwd, _given_inverse_bwd)


def _to_col(row):
    n = row.shape[1]
    r = lax.broadcasted_iota(jnp.int32, (n, n), 0)
    c = lax.broadcasted_iota(jnp.int32, (n, n), 1)
    return jnp.sum(jnp.where(r == c, row, 0.0), axis=1, keepdims=True)


def _intra_fn(k, v, beta_r, gcr, ops, t_known=None):
    n = len(k)
    m = k[0].shape[0]
    r = lax.broadcasted_iota(jnp.int32, (m, m), 0)
    c = lax.broadcasted_iota(jnp.int32, (m, m), 1)
    below = (r > c) & (jnp.right_shift(r, 6) == jnp.right_shift(c, 6))
    beta = [_to_col(beta_r[i]) for i in range(n)]
    gcc = [_to_col(gcr[i]) for i in range(n)]
    decay = [jnp.exp(jnp.where(below, gcc[i] - gcr[i], NEG)) for i in range(n)]
    kb = [k[i] * beta[i] for i in range(n)]
    a = [ops.nt(kb[i], k[i]) * decay[i] for i in range(n)]
    if t_known is None:
        p = [-a[i] for i in range(n)]
        tm = [jnp.where(r == c, 1.0, 0.0) + p[i] for i in range(n)]
        for _ in range(5):
            p = [ops.nn(p[i], p[i], hi=True) for i in range(n)]
            tm = [tm[i] + ops.nn(tm[i], p[i], hi=True) for i in range(n)]
    else:
        tm = [_given_inverse(a[i], t_known[i]) for i in range(n)]
    u_hat = [ops.nn(tm[i], v[i] * beta[i], hi=True) for i in range(n)]
    w = [ops.nn(tm[i], kb[i] * jnp.exp(gcc[i]), hi=True) for i in range(n)]
    return tuple(u_hat), tuple(w), tuple(tm)


INTRA_NB = 8
PAIR = 1


def gdn_intra_fwd(qkvn, betar5, gcr5, B, S, H, rider=None):
    T = B * S
    UNIT = PAIR * CHUNK
    N = S // UNIT
    nb = min(INTRA_NB // PAIR, N)
    rows = nb * UNIT
    ns = N // nb

    def body(k_ref, v_ref, b_ref, gr_ref, uh_ref, w_ref, t_ref):
        sls = [slice(ci * UNIT, (ci + 1) * UNIT) for ci in range(nb)]
        u_hat, w, tm = _intra_fn(tuple(k_ref[sl, :] for sl in sls), tuple(v_ref[sl, :] for sl in sls),
                                 tuple(b_ref[ci] for ci in range(nb)), tuple(gr_ref[ci] for ci in range(nb)), _RawOps)
        for ci, sl in enumerate(sls):
            uh_ref[sl, :] = u_hat[ci]
            w_ref[sl, :] = w[ci]
            t_ref[ci] = tm[ci]

    rowspec = pl.BlockSpec((None, None, nb, 1, UNIT), lambda b, h, i: (b, h, i, 0, 0))
    sqspec = pl.BlockSpec((None, None, nb, UNIT, UNIT), lambda b, h, i: (b, h, i, 0, 0))
    out = pl.BlockSpec((rows, HEAD), lambda b, h, i: (b * ns + i, h))
    return _hosted_call(
        body, rider, name="gdn_intra_fwd", grid=(B, H, ns),
        in_specs=[pl.BlockSpec((rows, HEAD), lambda b, h, i: (b * ns + i, 3 * h + 1)),
                  pl.BlockSpec((rows, HEAD), lambda b, h, i: (b * ns + i, 3 * h + 2)),
                  rowspec, rowspec],
        out_specs=[out, out, sqspec],
        out_shape=[jax.ShapeDtypeStruct((T, H * HEAD), F32), jax.ShapeDtypeStruct((T, H * HEAD), F32),
                   jax.ShapeDtypeStruct((B, H, N, UNIT, UNIT), F32)],
        scratch_shapes=[], semantics=("parallel", "parallel", "parallel"),
    )(qkvn, qkvn, betar5, gcr5)


def gdn_intra_bwd(qkvn, betar5, gcr5, t_inv, d_uh, d_w, dq_in, dk_in, B, S, H):
    T = B * S
    UNIT = PAIR * CHUNK
    N = S // UNIT
    nb = min(INTRA_NB // PAIR, N)
    rows = nb * UNIT
    ns = N // nb

    def body(k_ref, v_ref, b_ref, gr_ref, t_ref, duh_ref, dw_ref, dq_ref, dk_ref, o_ref, db_ref, dgr_ref):
        sls = [slice(ci * UNIT, (ci + 1) * UNIT) for ci in range(nb)]
        chunks = range(nb)
        _, vjp = jax.vjp(
            functools.partial(_intra_fn, ops=_DiffOps, t_known=tuple(t_ref[ci] for ci in chunks)),
            tuple(k_ref[sl, :] for sl in sls), tuple(v_ref[sl, :] for sl in sls), tuple(b_ref[ci] for ci in chunks),
            tuple(gr_ref[ci] for ci in chunks))
        zero = jnp.zeros((UNIT, UNIT), F32)
        dk, dv, db, dgr = vjp((tuple(duh_ref[sl, :] for sl in sls), tuple(dw_ref[sl, :] for sl in sls),
                               tuple(zero for _ in chunks)))
        for ci, sl in enumerate(sls):
            o_ref[sl, 0:HEAD] = dq_ref[sl, :]
            o_ref[sl, HEAD:2 * HEAD] = dk[ci] + dk_ref[sl, :]
            o_ref[sl, 2 * HEAD:3 * HEAD] = dv[ci]
            db_ref[ci] = db[ci]
            dgr_ref[ci] = dgr[ci]

    rowspec = pl.BlockSpec((None, None, nb, 1, UNIT), lambda b, h, i: (b, h, i, 0, 0))
    sqspec = pl.BlockSpec((None, None, nb, UNIT, UNIT), lambda b, h, i: (b, h, i, 0, 0))
    head = pl.BlockSpec((rows, HEAD), lambda b, h, i: (b * ns + i, h))
    return pl.pallas_call(
        body, name="gdn_intra_bwd", grid=(B, H, ns),
        in_specs=[pl.BlockSpec((rows, HEAD), lambda b, h, i: (b * ns + i, 3 * h + 1)),
                  pl.BlockSpec((rows, HEAD), lambda b, h, i: (b * ns + i, 3 * h + 2)),
                  rowspec, rowspec, sqspec, head, head, head, head],
        out_specs=[pl.BlockSpec((rows, 3 * HEAD), lambda b, h, i: (b * ns + i, h)), rowspec, rowspec],
        out_shape=[jax.ShapeDtypeStruct((T, 3 * H * HEAD), F32),
                   jax.ShapeDtypeStruct((B, H, N, 1, UNIT), F32), jax.ShapeDtypeStruct((B, H, N, 1, UNIT), F32)],
        compiler_params=_params("parallel", "parallel", "parallel"),
    )(qkvn, qkvn, betar5, gcr5, t_inv, d_uh, d_w, dq_in, dk_in)


def _inter_fn(q, k, u_hat, w, gcr, state, ops):
    n = len(q)
    r = lax.broadcasted_iota(jnp.int32, (CHUNK, CHUNK), 0)
    c = lax.broadcasted_iota(jnp.int32, (CHUNK, CHUNK), 1)
    last = lax.broadcasted_iota(jnp.int32, (1, CHUNK), 1) == CHUNK - 1
    gcc = [_to_col(gcr[i]) for i in range(n)]
    gl = [jnp.sum(jnp.where(last, gcr[i], 0.0), axis=1, keepdims=True) for i in range(n)]
    decay = [jnp.exp(jnp.where(r >= c, gcc[i] - gcr[i], NEG)) for i in range(n)]
    qs = [q[i] * (HEAD ** -0.5) for i in range(n)]
    ws = [ops.nn(w[i], state[i]) for i in range(n)]
    qst = [ops.nn(qs[i] * jnp.exp(gcc[i]), state[i]) for i in range(n)]
    attn = [ops.nt(qs[i], k[i]) * decay[i] for i in range(n)]
    u = [u_hat[i] - ws[i] for i in range(n)]
    o = [qst[i] + ops.nn(attn[i], u[i]) for i in range(n)]
    kdu = [ops.tn(k[i] * jnp.exp(gl[i] - gcc[i]), u[i]) for i in range(n)]
    new_state = [state[i] * jnp.exp(gl[i]) + kdu[i] for i in range(n)]
    return tuple(o), tuple(new_state)


INTER_HEADS = 4
INTER_ROWS = 512


def _inter_specs(ts, ns, hp, backward):
    at = (lambda s: ns - 1 - s) if backward else (lambda s: s)
    nc = ts // CHUNK
    qk = []
    for hh in range(hp):
        qk.append(pl.BlockSpec((ts, HEAD), lambda b, g, s, hh=hh: (b * ns + at(s), 3 * (hp * g + hh))))
        qk.append(pl.BlockSpec((ts, HEAD), lambda b, g, s, hh=hh: (b * ns + at(s), 3 * (hp * g + hh) + 1)))
    heads = pl.BlockSpec((ts, hp * HEAD), lambda b, g, s: (b * ns + at(s), g))
    rowspec = pl.BlockSpec((None, hp, nc, 1, CHUNK), lambda b, g, s: (b, g, at(s), 0, 0))
    stspec = pl.BlockSpec((None, hp, nc, HEAD, HEAD), lambda b, g, s: (b, g, at(s), 0, 0))
    return qk, heads, rowspec, stspec


def gdn_inter_fwd(qkvn, u_hat, w, gcr5, p_gz, gnorm, B, S, H):
    T = B * S
    N = S // CHUNK
    hp = INTER_HEADS if H % INTER_HEADS == 0 else 1
    hs = range(hp)
    ts = _tile(S, INTER_ROWS, CHUNK)
    ns, nc = S // ts, ts // CHUNK

    def body(*refs):
        qk_refs, (uh_ref, w_ref, gr_ref, z_ref, gn_ref, o_ref, st_ref, y_ref, s_scr) = refs[:2 * hp], refs[2 * hp:]

        @pl.when(pl.program_id(2) == 0)
        def _():
            s_scr[...] = jnp.zeros_like(s_scr)

        gn = gn_ref[...]

        def step(n, c):
            rows = pl.ds(pl.multiple_of(n * CHUNK, CHUNK), CHUNK)
            st = tuple(s_scr[hh] for hh in hs)
            for hh in hs:
                st_ref[hh, n] = st[hh]
            o, new = _inter_fn(tuple(qk_refs[2 * hh][rows, :] for hh in hs), tuple(qk_refs[2 * hh + 1][rows, :] for hh in hs),
                               tuple(uh_ref[rows, hh * HEAD:(hh + 1) * HEAD] for hh in hs),
                               tuple(w_ref[rows, hh * HEAD:(hh + 1) * HEAD] for hh in hs),
                               tuple(gr_ref[hh, n] for hh in hs), st, _RawOps)
            for hh in hs:
                cols = slice(hh * HEAD, (hh + 1) * HEAD)
                o_ref[rows, cols] = o[hh]
                s_scr[hh] = new[hh]
                z = z_ref[rows, cols]
                r = lax.rsqrt(jnp.mean(o[hh] * o[hh], axis=-1, keepdims=True) + EPS)
                y_ref[rows, cols] = (o[hh] * r * gn * z * _sigmoid(z)).astype(BF16)
            return c

        lax.fori_loop(0, nc, step, 0)

    qk, heads, rowspec, stspec = _inter_specs(ts, ns, hp, backward=False)
    return pl.pallas_call(
        body, name="gdn_inter_fwd", grid=(B, H // hp, ns),
        in_specs=qk + [heads, heads, rowspec, heads, pl.BlockSpec((1, HEAD), lambda b, g, s: (0, 0))],
        out_specs=[heads, stspec, heads],
        out_shape=[jax.ShapeDtypeStruct((T, H * HEAD), F32), jax.ShapeDtypeStruct((B, H, N, HEAD, HEAD), F32),
                   jax.ShapeDtypeStruct((T, H * HEAD), BF16)],
        scratch_shapes=[pltpu.VMEM((hp, HEAD, HEAD), F32)],
        compiler_params=_params("parallel", "parallel", "arbitrary"),
    )(*([qkvn] * (2 * hp)), u_hat, w, gcr5, p_gz, gnorm)


def gdn_inter_bwd(qkvn, u_hat, w, gcr5, states, o, p_gz, gnorm, d_y, B, S, H, rider=None):
    T = B * S
    N = S // CHUNK
    hp = INTER_HEADS if H % INTER_HEADS == 0 else 1
    hs = range(hp)
    ts = _tile(S, INTER_ROWS, CHUNK)
    ns, nc = S // ts, ts // CHUNK

    def body(*refs):
        qk_refs = refs[:2 * hp]
        (uh_ref, w_ref, gr_ref, st_ref, o_ref, z_ref, gn_ref, dy_ref,
         dq_ref, dk_ref, duh_ref, dw_ref, dgr_ref, dz_ref, dgn_ref, ds_scr) = refs[2 * hp:]

        @pl.when(pl.program_id(2) == 0)
        def _():
            ds_scr[...] = jnp.zeros_like(ds_scr)
            dgn_ref[...] = jnp.zeros_like(dgn_ref)

        cols = [slice(hh * HEAD, (hh + 1) * HEAD) for hh in hs]
        gn = gn_ref[...]

        def through_norm(rows, hh):
            ov, z, d = o_ref[rows, cols[hh]], z_ref[rows, cols[hh]], dy_ref[rows, cols[hh]]
            r = lax.rsqrt(jnp.mean(ov * ov, axis=-1, keepdims=True) + EPS)
            xh = ov * r
            sg = _sigmoid(z)
            d_n = d * (z * sg)
            dz_ref[rows, cols[hh]] = (d * xh * gn * (sg + z * sg * (1.0 - sg))).astype(BF16)
            dgn_ref[0:1, :] += jnp.sum(d_n * xh, axis=0, keepdims=True)
            dxh = d_n * gn
            return r * (dxh - xh * jnp.mean(dxh * xh, axis=-1, keepdims=True))

        def step(i, c):
            n = nc - 1 - i
            rows = pl.ds(pl.multiple_of(n * CHUNK, CHUNK), CHUNK)
            _, vjp = jax.vjp(functools.partial(_inter_fn, ops=_DiffOps),
                             tuple(qk_refs[2 * hh][rows, :] for hh in hs), tuple(qk_refs[2 * hh + 1][rows, :] for hh in hs),
                             tuple(uh_ref[rows, cols[hh]] for hh in hs), tuple(w_ref[rows, cols[hh]] for hh in hs),
                             tuple(gr_ref[hh, n] for hh in hs), tuple(st_ref[hh, n] for hh in hs))
            dq, dk, duh, dw, dgr, ds = vjp((tuple(through_norm(rows, hh) for hh in hs), tuple(ds_scr[hh] for hh in hs)))
            for hh in hs:
                dq_ref[rows, cols[hh]] = dq[hh]
                dk_ref[rows, cols[hh]] = dk[hh]
                duh_ref[rows, cols[hh]] = duh[hh]
                dw_ref[rows, cols[hh]] = dw[hh]
                dgr_ref[hh, n] = dgr[hh]
                ds_scr[hh] = ds[hh]
            return c

        lax.fori_loop(0, nc, step, 0)

    qk, heads, rowspec, stspec = _inter_specs(ts, ns, hp, backward=True)
    hshape = jax.ShapeDtypeStruct((T, H * HEAD), F32)
    return _hosted_call(
        body, rider, name="gdn_inter_bwd", grid=(B, H // hp, ns),
        in_specs=qk + [heads, heads, rowspec, stspec, heads, heads, pl.BlockSpec((1, HEAD), lambda b, g, s: (0, 0)), heads],
        out_specs=[heads, heads, heads, heads, rowspec, heads,
                   pl.BlockSpec((None, None, 8, HEAD), lambda b, g, s: (b, g, 0, 0))],
        out_shape=[hshape, hshape, hshape, hshape, jax.ShapeDtypeStruct((B, H, N, 1, CHUNK), F32),
                   jax.ShapeDtypeStruct((T, H * HEAD), BF16), jax.ShapeDtypeStruct((B, H // hp, 8, HEAD), F32)],
        scratch_shapes=[pltpu.VMEM((hp, HEAD, HEAD), F32)], semantics=("parallel", "parallel", "arbitrary"),
    )(*([qkvn] * (2 * hp)), u_hat, w, gcr5, states, o, p_gz, gnorm, d_y)


def adamw(w, g, m, v, name):
    shape = w.shape
    lead = (None,) * (w.ndim - 2)
    zeros = (0,) * (w.ndim - 2)
    R, C = shape[-2:]
    g2 = g.reshape(R, C)
    tr, tc = _tile(R, 128, 8), C
    if tr % 8 and R > 8:
        tr, tc = R, _tile(C, 128)

    def body(w_ref, g_ref, m_ref, v_ref, d_ref, nm_ref, nv_ref):
        gv = g_ref[...]
        nm = ADAM_B1 * m_ref[...] + (1.0 - ADAM_B1) * gv
        nv = ADAM_B2 * v_ref[...] + (1.0 - ADAM_B2) * (gv * gv)
        m_hat = nm / (1.0 - ADAM_B1 ** ADAM_STEP)
        v_hat = nv / (1.0 - ADAM_B2 ** ADAM_STEP)
        d_ref[...] = -ADAM_LR * (m_hat / (jnp.sqrt(v_hat) + ADAM_EPS) + ADAM_WD * w_ref[...])
        nm_ref[...] = nm
        nv_ref[...] = nv

    blk = pl.BlockSpec(lead + (tr, tc), lambda i, j: zeros + (i, j))
    gblk = pl.BlockSpec((tr, tc), lambda i, j: (i, j))
    sh = jax.ShapeDtypeStruct(shape, F32)
    return pl.pallas_call(
        body, name=name, grid=(R // tr, C // tc), in_specs=[blk, gblk, blk, blk], out_specs=[blk] * 3, out_shape=[sh] * 3,
        compiler_params=_params("parallel", "parallel"),
    )(w, g2, m, v)


def _place():
    x, y, c = lax.axis_index("x"), lax.axis_index("y"), lax.axis_index("c")
    chips = [(1 - x, y), (x, 1 - y), (1 - x, 1 - y)]
    return x, y, c, chips


_HBM = pl.BlockSpec(memory_space=pltpu.HBM)


def allgather_weights(packs):
    n = len(packs)

    def body(*refs):
        in_refs, out_refs, (send_sems, recv_sems) = refs[:n], refs[n:2 * n], refs[2 * n:]
        x, y, c, chips = _place()
        me_s = 2 * x + y
        me, sibling = (x, y, c), (x, y, 1 - c)
        shards = [2 * chip[0] + chip[1] for chip in chips]

        def copy(a, k, shard, half, to, src=None):
            dst = out_refs[a].at[shard, half]
            return pltpu.make_async_remote_copy(src_ref=dst if src is None else src, dst_ref=dst,
                                                send_sem=send_sems.at[6 * a + k], recv_sem=recv_sems.at[6 * a + k],
                                                device_id=to, device_id_type=MESH)

        first = [copy(a, j, me_s, c, (*chip, c), src=in_refs[a].at[c]) for a in range(n) for j, chip in enumerate(chips)]
        for cp in first:
            cp.start()
        passed = []
        for a in range(n):
            for j in range(3):
                copy(a, j, shards[j], c, me).wait_recv()
                passed.append(copy(a, 3 + j, shards[j], c, sibling))
                passed[-1].start()
        for a in range(n):
            for j in range(3):
                copy(a, 3 + j, shards[j], 1 - c, me).wait_recv()
        for cp in first + passed:
            cp.wait_send()

    return pl.pallas_call(
        body, name="allgather_weights", in_specs=[_HBM] * n, out_specs=[_HBM] * n,
        out_shape=[jax.ShapeDtypeStruct((N_CHIP,) + p.shape, p.dtype) for p in packs],
        scratch_shapes=[pltpu.SemaphoreType.DMA((6 * n,)), pltpu.SemaphoreType.DMA((6 * n,))],
    )(*packs)


class _Rider:
    def __init__(self, inputs, out_shapes, n_sems, sends, recvs, aliases=None):
        self.inputs, self.out_shapes, self.n_sems = list(inputs), list(out_shapes), n_sems
        self.sends, self.recvs, self.aliases = sends, recvs, aliases or {}

    def start(self, *refs):
        for cp in self.sends(*refs):
            cp.start()

    def wait(self, *refs):
        for cp in self.recvs(*refs):
            cp.wait_recv()
        for cp in self.sends(*refs):
            cp.wait_send()


def _remote(src, dst, send_sems, recv_sems, k, to):
    return pltpu.make_async_remote_copy(src_ref=src, dst_ref=dst, send_sem=send_sems.at[k], recv_sem=recv_sems.at[k],
                                        device_id=to, device_id_type=MESH)


def _run_alone(rider, name):
    ri = len(rider.inputs)

    def body(*refs):
        ins, outs, (send_sems, recv_sems) = refs[:ri], refs[ri:-2], refs[-2:]
        rider.start(ins, outs, send_sems, recv_sems)
        rider.wait(ins, outs, send_sems, recv_sems)

    return pl.pallas_call(
        body, name=name, in_specs=[_HBM] * ri, out_specs=[_HBM] * len(rider.out_shapes), out_shape=rider.out_shapes,
        scratch_shapes=[pltpu.SemaphoreType.DMA((rider.n_sems,))] * 2, input_output_aliases=rider.aliases,
    )(*rider.inputs)


def _hosted_call(body, rider, *, name, grid, in_specs, out_specs, out_shape, scratch_shapes, semantics):
    if rider is None:
        return pl.pallas_call(body, name=name, grid=grid, in_specs=in_specs, out_specs=out_specs, out_shape=out_shape,
                              scratch_shapes=scratch_shapes, compiler_params=_params(*semantics))
    n_in, n_out, n_scr = len(in_specs), len(out_specs), len(scratch_shapes)
    ri, ro = len(rider.inputs), len(rider.out_shapes)

    def hosted(*refs):
        parts, p = [], 0
        for cnt in (n_in, ri, n_out, ro, n_scr, 2):
            parts.append(refs[p:p + cnt])
            p += cnt
        ins, rins, outs, routs, scr, (send_sems, recv_sems) = parts
        first = functools.reduce(jnp.logical_and, [pl.program_id(a) == 0 for a in range(len(grid))])
        last = functools.reduce(jnp.logical_and, [pl.program_id(a) == grid[a] - 1 for a in range(len(grid))])

        @pl.when(first)
        def _():
            rider.start(rins, routs, send_sems, recv_sems)

        body(*ins, *outs, *scr)

        @pl.when(last)
        def _():
            rider.wait(rins, routs, send_sems, recv_sems)

    call = pl.pallas_call(
        hosted, name=name, grid=grid, in_specs=list(in_specs) + [_HBM] * ri, out_specs=list(out_specs) + [_HBM] * ro,
        out_shape=list(out_shape) + rider.out_shapes,
        scratch_shapes=list(scratch_shapes) + [pltpu.SemaphoreType.DMA((rider.n_sems,))] * 2,
        input_output_aliases={n_in + i: n_out + o for i, o in rider.aliases.items()},
        compiler_params=_params(*(("arbitrary",) * len(grid))))

    def run(*args):
        res = call(*args, *rider.inputs)
        return res[:n_out], res[n_out:]

    return run


def _ride_gather_ici(packs):
    n = len(packs)

    def sends(ins, outs, send_sems, recv_sems):
        x, y, c, chips = _place()
        return [_remote(ins[a].at[c], outs[a].at[2 * x + y, c], send_sems, recv_sems, 3 * a + j, (*chip, c))
                for a in range(n) for j, chip in enumerate(chips)]

    def recvs(ins, outs, send_sems, recv_sems):
        x, y, c, chips = _place()
        return [_remote(ins[a].at[c], outs[a].at[2 * chip[0] + chip[1], c], send_sems, recv_sems, 3 * a + j, (x, y, c))
                for a in range(n) for j, chip in enumerate(chips)]

    return _Rider(packs, [jax.ShapeDtypeStruct((N_CHIP,) + p.shape, p.dtype) for p in packs], 3 * n, sends, recvs)


def _ride_gather_d2d(gathered):
    n = len(gathered)

    def copies(landing_half, to):
        def build(ins, outs, send_sems, recv_sems):
            x, y, c, chips = _place()
            return [_remote(ins[a].at[2 * chip[0] + chip[1], c], outs[a].at[2 * chip[0] + chip[1], landing_half(c)],
                            send_sems, recv_sems, 3 * a + j, to(x, y, c))
                    for a in range(n) for j, chip in enumerate(chips)]
        return build

    return _Rider(gathered, [jax.ShapeDtypeStruct(g.shape, g.dtype) for g in gathered], 3 * n,
                  copies(lambda c: c, lambda x, y, c: (x, y, 1 - c)), copies(lambda c: 1 - c, lambda x, y, c: (x, y, c)),
                  aliases={a: a for a in range(n)})


def _ride_exchange(gs):
    n = len(gs)

    def copies(ins, outs, send_sems, recv_sems):
        x, y, c, _ = _place()
        return [_remote(ins[a].at[1 - c], outs[a], send_sems, recv_sems, a, (x, y, 1 - c)) for a in range(n)]

    return _Rider(gs, [jax.ShapeDtypeStruct(g.shape[1:], g.dtype) for g in gs], n, copies, copies)


def _ride_scatter(b16s):
    n = len(b16s)

    def sends(ins, outs, send_sems, recv_sems):
        x, y, c, chips = _place()
        return [_remote(ins[a].at[2 * chip[0] + chip[1]], outs[a].at[2 * x + y], send_sems, recv_sems, 3 * a + j, (*chip, c))
                for a in range(n) for j, chip in enumerate(chips)]

    def recvs(ins, outs, send_sems, recv_sems):
        x, y, c, chips = _place()
        return [_remote(ins[a].at[2 * x + y], outs[a].at[2 * chip[0] + chip[1]], send_sems, recv_sems, 3 * a + j, (x, y, c))
                for a in range(n) for j, chip in enumerate(chips)]

    return _Rider(b16s, [jax.ShapeDtypeStruct(b.shape, b.dtype) for b in b16s], 3 * n, sends, recvs)


def _slab_tile(r, cols):
    tr = _tile(r, 256, 16)
    if tr % 16 == 0:
        return tr, cols
    return r, _tile(cols, 128)


def add_halves(g, got, idx, name):
    _, ns, r, cols = g.shape
    tr, tc = _slab_tile(r, cols)

    def body(idx_ref, a_ref, b_ref, o32_ref, o16_ref):
        s = a_ref[...] + b_ref[...]
        o32_ref[...] = s
        o16_ref[...] = s.astype(BF16)

    blk = pl.BlockSpec((None, tr, tc), lambda s, i, j, idx_ref: (s, i, j))
    return pl.pallas_call(
        body, name=name,
        grid_spec=pltpu.PrefetchScalarGridSpec(
            num_scalar_prefetch=1, grid=(ns, r // tr, cols // tc),
            in_specs=[pl.BlockSpec((None, None, tr, tc), lambda s, i, j, idx_ref: (idx_ref[0], s, i, j)), blk],
            out_specs=[blk, blk]),
        out_shape=[jax.ShapeDtypeStruct((ns, r, cols), F32), jax.ShapeDtypeStruct((ns, r, cols), BF16)],
        compiler_params=_params("parallel", "parallel", "parallel"),
    )(idx, g, got)


def add_chips(a32, got16, idx, name):
    ns, r, cols = a32.shape
    tr, tc = _slab_tile(r, cols)

    def body(idx_ref, a_ref, r1_ref, r2_ref, r3_ref, o_ref):
        o_ref[...] = ((a_ref[...] + r1_ref[...].astype(F32)) + r2_ref[...].astype(F32)) + r3_ref[...].astype(F32)

    def slab(k):
        return pl.BlockSpec((None, tr, tc), lambda i, j, idx_ref: ((idx_ref[1] + k) % ns, i, j))

    return pl.pallas_call(
        body, name=name,
        grid_spec=pltpu.PrefetchScalarGridSpec(
            num_scalar_prefetch=1, grid=(r // tr, cols // tc), in_specs=[slab(0), slab(1), slab(2), slab(3)],
            out_specs=pl.BlockSpec((tr, tc), lambda i, j, idx_ref: (i, j))),
        out_shape=jax.ShapeDtypeStruct((r, cols), F32),
        compiler_params=_params("parallel", "parallel"),
    )(idx, a32, got16, got16, got16)


def share_halves(halves):
    n = len(halves)

    def body(*refs):
        in_refs, out_refs, (send_sems, recv_sems) = refs[:n], refs[n:2 * n], refs[2 * n:]
        x, y, c, _ = _place()
        cps = [pltpu.make_async_remote_copy(src_ref=in_refs[a], dst_ref=out_refs[a], send_sem=send_sems.at[a],
                                            recv_sem=recv_sems.at[a], device_id=(x, y, 1 - c), device_id_type=MESH)
               for a in range(n)]
        for cp in cps:
            cp.start()
        for cp in cps:
            cp.wait()

    return pl.pallas_call(
        body, name="share_halves", in_specs=[_HBM] * n, out_specs=[_HBM] * n,
        out_shape=[jax.ShapeDtypeStruct(h.shape, F32) for h in halves],
        scratch_shapes=[pltpu.SemaphoreType.DMA((n,)), pltpu.SemaphoreType.DMA((n,))],
    )(*halves)


def allreduce_small(v):
    R, _ = v.shape

    def body(in_ref, out_ref, slots, send_sems, recv_sems):
        x, y, c, _ = _place()
        me = 4 * x + 2 * y + c
        slots[me] = in_ref[...]
        cps = []
        for k in range(1, N_DEV):
            to = (x ^ (k >> 2), y ^ ((k >> 1) & 1), c ^ (k & 1))
            cps.append(pltpu.make_async_remote_copy(src_ref=in_ref, dst_ref=slots.at[me], send_sem=send_sems.at[k - 1],
                                                    recv_sem=recv_sems.at[k - 1], device_id=to, device_id_type=MESH))
        for cp in cps:
            cp.start()
        for k in range(1, N_DEV):
            frm = 4 * (x ^ (k >> 2)) + 2 * (y ^ ((k >> 1) & 1)) + (c ^ (k & 1))
            pltpu.make_async_remote_copy(src_ref=in_ref, dst_ref=slots.at[frm], send_sem=send_sems.at[k - 1],
                                         recv_sem=recv_sems.at[k - 1], device_id=(x, y, c), device_id_type=MESH).wait_recv()
        for cp in cps:
            cp.wait_send()
        acc = slots[0]
        for d in range(1, N_DEV):
            acc = acc + slots[d]
        out_ref[...] = acc

    vm = pl.BlockSpec(memory_space=pltpu.VMEM)
    return pl.pallas_call(
        body, name="allreduce_small", in_specs=[vm], out_specs=vm, out_shape=jax.ShapeDtypeStruct((R, ROW), F32),
        scratch_shapes=[pltpu.VMEM((N_DEV, R, ROW), F32), pltpu.SemaphoreType.DMA((N_DEV - 1,)),
                        pltpu.SemaphoreType.DMA((N_DEV - 1,))],
    )(v)


def _rows_of(n, unit=16):
    return -(-n // (unit * ROW)) * unit


def _pack_rows(items, total_rows, dtype, unit=16):
    parts = []
    used = 0
    for a in items:
        flat = a.reshape(-1)
        r = _rows_of(flat.shape[0], unit)
        flat = jnp.pad(flat, (0, r * ROW - flat.shape[0]))
        parts.append(flat.reshape(r, ROW))
        used += r
    if total_rows > used:
        parts.append(jnp.zeros((total_rows - used, ROW), dtype))
    return jnp.concatenate(parts, axis=0)


def _unpack_rows(buf, shapes, unit=16):
    lead = buf.shape[:-2]
    out = []
    off = 0
    for shp in shapes:
        n = math.prod(shp)
        r = _rows_of(n, unit)
        piece = buf[..., off:off + r, :].reshape(*lead, r * ROW)[..., :n].reshape(*lead, *shp)
        out.append(piece)
        off += r
    return out


def _interleave_heads(w, H):
    lead = w.shape[:-1]
    return w.reshape(*lead, 3, H, HEAD).swapaxes(-3, -2).reshape(*lead, 3 * H * HEAD)


def _deinterleave_heads(w, H):
    lead = w.shape[:-1]
    return w.reshape(*lead, H, 3, HEAD).swapaxes(-3, -2).reshape(*lead, 3 * H * HEAD)


def _interleave_head_rows(w, H):
    return w.reshape(3, H, HEAD, w.shape[-1]).swapaxes(0, 1).reshape(3 * H * HEAD, w.shape[-1])


def _deinterleave_head_rows(w, H):
    return w.reshape(H, 3, HEAD, w.shape[-1]).swapaxes(0, 1).reshape(3 * H * HEAD, w.shape[-1])


def kernel(x, norm_mix, w_in, fox_f_bias, gdn_conv_w, gdn_a_log, gdn_dt_bias, gdn_norm, w_branch_fox, w_branch_gdn, w_out, norm_ffn, w_up, ffn_conv_w, w_down, norm_final, loss_target, m_norm_mix, m_w_in, m_fox_f_bias, m_gdn_conv_w, m_gdn_a_log, m_gdn_dt_bias, m_gdn_norm, m_w_branch_fox, m_w_branch_gdn, m_w_out, m_norm_ffn, m_w_up, m_ffn_conv_w, m_w_down, m_norm_final, v_norm_mix, v_w_in, v_fox_f_bias, v_gdn_conv_w, v_gdn_a_log, v_gdn_dt_bias, v_gdn_norm, v_w_branch_fox, v_w_branch_gdn, v_w_out, v_norm_ffn, v_w_up, v_ffn_conv_w, v_w_down, v_norm_final):
    B, S, D = x.shape
    T = B * S
    H = D // HEAD
    N = S // CHUNK
    FF = w_down.shape[1] * N_CHIP
    d_in = 9 * D + 3 * H
    assert w_in.shape[2] * N_CHIP == d_in and 3 * H <= 128

    cidx = lax.axis_index("c").astype(jnp.int32)
    sidx = (2 * lax.axis_index("x") + lax.axis_index("y")).astype(jnp.int32)
    idx = jnp.stack([cidx, sidx])

    rowed = [w_branch_fox[0], w_branch_gdn[0], w_out[0], w_down[0]]
    convs = [gdn_conv_w[0], ffn_conv_w[0]]
    rowed_shapes = [a.shape for a in rowed]
    conv_shapes = [a.shape + (2,) for a in convs]
    pad_rows = lambda shapes: -(-sum(_rows_of(math.prod(s)) for s in shapes) // 256) * 128
    Rh, Rc = pad_rows(rowed_shapes), pad_rows(conv_shapes)
    halves = lambda a: a.reshape(2, a.shape[0] // 2, a.shape[1])
    c_in = w_in.shape[2]
    packs_a = [w_in[0].T.astype(BF16).reshape(c_in, 2, D // 2).transpose(1, 0, 2),
               halves(_pack_rows([lax.bitcast_convert_type(a, BF16) for a in convs], 2 * Rc, BF16))]
    packs_b = [halves(w_up[0].astype(BF16)), halves(_pack_rows([a.astype(BF16) for a in rowed], 2 * Rh, BF16))]
    own = lambda gs, ps: [lax.dynamic_update_slice(g, p[None], (sidx, 0, 0, 0)) for g, p in zip(gs, ps)]
    by_cols = lambda g: g.transpose(1, 2, 0, 3).reshape(2 * g.shape[2], N_CHIP * g.shape[3])
    cat_cols = lambda p: jnp.concatenate([p[i] for i in range(N_CHIP)], axis=-1)
    cat_rows = lambda p: p.reshape(-1, p.shape[-1])
    g_in, g_conv = own(allgather_weights(packs_a), packs_a)
    W_inT = g_in.transpose(0, 2, 1, 3).reshape(N_CHIP * c_in, D)
    conv_parts = _unpack_rows(g_conv.reshape(N_CHIP, 2 * Rc, ROW), conv_shapes)
    gconv = cat_cols(lax.bitcast_convert_type(conv_parts[0], F32))
    fconv = cat_cols(lax.bitcast_convert_type(conv_parts[1], F32))

    o1, o2 = 3 * D, 3 * D + H
    o3, o4, o5, o6 = o2 + 3 * D, o2 + 3 * D + H, o2 + 3 * D + 2 * H, o2 + 4 * D + 2 * H
    W_foxT = _interleave_head_rows(W_inT[:o1], H)
    W_gqkvT = _interleave_head_rows(W_inT[o2:o3], H)
    W_gzT = W_inT[o5:o6]
    W_gatesT = W_inT[o6:]
    W_smallT = jnp.concatenate([W_inT[o1:o2], W_inT[o3:o5], jnp.zeros((128 - 3 * H, D), BF16)], axis=0)
    gconv_i = _interleave_heads(gconv, H)
    fconv_g, fconv_v = fconv[:, :FF], fconv[:, FF:]
    prm = jnp.zeros((8, 128), F32)
    prm = prm.at[0, 0:H].set(fox_f_bias[0]).at[0, H:2 * H].set(gdn_dt_bias[0]).at[1, H:2 * H].set(gdn_a_log[0])

    x2 = x.reshape(T, D)
    tgt = loss_target.reshape(T, D)

    hn1 = rmsnorm_fwd(x2, norm_mix, "rmsnorm_mix")
    p_fox = matmul(hn1, W_foxT, "nt", "proj_fox", out_dtype=BF16)
    p_gqkv = matmul(hn1, W_gqkvT, "nt", "proj_gqkv")
    p_gz = matmul(hn1, W_gzT, "nt", "proj_gz")
    p_gates = matmul(hn1, W_gatesT, "nt", "proj_gates")
    p_small = matmul(hn1, W_smallT, "nt", "proj_small")

    sm = small_fwd(p_small, prm, B, S, H)
    heads = lambda a: a.reshape(B, S, H).transpose(0, 2, 1)
    c_bhs, gc_bhs, beta_bhs = heads(sm[:, 0:H]), heads(sm[:, H:2 * H]), heads(sm[:, 2 * H:3 * H])
    c_col, c_row = c_bhs[..., None], c_bhs[:, :, None, :]
    gcr5 = gc_bhs.reshape(B, H, N, 1, CHUNK)
    gcr_u = gc_bhs.reshape(B, H, N // PAIR, 1, PAIR * CHUNK)
    betar_u = beta_bhs.reshape(B, H, N // PAIR, 1, PAIR * CHUNK)

    (o_fox, o_fox16, lse), arriving = fox_fwd(p_fox, c_col, c_row, B, S, H, rider=_ride_gather_ici(packs_b))
    qkvn = gdn_prep_fwd(p_gqkv, gconv_i, B, S, H)
    (u_hat, w_t, t_inv), arrived = gdn_intra_fwd(qkvn, betar_u, gcr_u, B, S, H, rider=_ride_gather_d2d(arriving))
    g_up, g_rowed = own(arrived, packs_b)
    W_up = by_cols(g_up)
    W_up_g, W_up_v = W_up[:, :FF], W_up[:, FF:]
    W_bf, W_bg, W_out, W_down = (cat_rows(p) for p in _unpack_rows(g_rowed.reshape(N_CHIP, 2 * Rh, ROW), rowed_shapes))
    o_gdn, states, y_gdn = gdn_inter_fwd(qkvn, u_hat, w_t, gcr5, p_gz, gdn_norm, B, S, H)
    bf_ = matmul(o_fox16, W_bf, "nn", "branch_fox")
    bg_, y = matmul(y_gdn, W_bg, "nn", "branch_gdn", post=_post_merge(p_gates, bf_))
    h1, hn2 = matmul(y, W_out, "nn", "out_proj", add=x2, post=_post_rmsnorm(norm_ffn))
    up_g = matmul(hn2, W_up_g, "nn", "up_gate")
    up_v = matmul(hn2, W_up_v, "nn", "up_val")
    act = ffn_gate_fwd(up_g, up_v, fconv_g, fconv_v, B, S)
    dh2, dh2_16, loss_cols, d_norm_final = matmul(act, W_down, "nn", "down_proj", add=h1,
                                                  post=_post_loss(norm_final.reshape(1, D), tgt))
    loss = lax.psum(0.5 * jnp.sum(loss_cols) / D, ("x", "y", "c"))

    d_act = matmul(dh2_16, W_down, "nt", "d_act")
    dW_down = matmul(act, dh2_16, "tn", "dw_down")
    d_upg, d_upv, d_fconv_g, d_fconv_v = ffn_gate_bwd(up_g, up_v, fconv_g, fconv_v, d_act, B, S)
    d_hn2 = matmul(d_upg, W_up_g, "nt", "d_hn2_g")
    dh1, dh1_16, d_norm_ffn = matmul(d_upv, W_up_v, "nt", "d_hn2_v", add=d_hn2,
                                     post=_post_rmsnorm_bwd(h1, norm_ffn, dh2, True))
    dW_up = jnp.concatenate([matmul(hn2, d_upg, "tn", "dw_up_g"), matmul(hn2, d_upv, "tn", "dw_up_v")], axis=1)
    d_bf, d_bg, d_gates = matmul(dh1_16, W_out, "nt", "d_y", post=_post_merge_bwd(p_gates, bf_, bg_))
    dW_out = matmul(y, dh1_16, "tn", "dw_out")
    d_ofox = matmul(d_bf, W_bf, "nt", "d_ofox")
    dW_bf = matmul(o_fox16, d_bf, "tn", "dw_bf")
    d_ygdn = matmul(d_bg, W_bg, "nt", "d_ygdn")
    dW_bg = matmul(y_gdn, d_bg, "tn", "dw_bg")

    d_fconv = jnp.concatenate([d_fconv_g, d_fconv_v], axis=1)
    col_shard = lambda g, s: g[:, s * (g.shape[1] // N_CHIP):(s + 1) * (g.shape[1] // N_CHIP)]
    row_shard = lambda g, s: g[s * (g.shape[0] // N_CHIP):(s + 1) * (g.shape[0] // N_CHIP)]
    shard_items = lambda s: [row_shard(dW_bf, s), row_shard(dW_bg, s), row_shard(dW_out, s), row_shard(dW_down, s),
                             col_shard(d_fconv, s)]
    g_shapes = [a.shape for a in shard_items(0)]
    assert sum(_rows_of(math.prod(s)) for s in g_shapes) <= 2 * Rh
    to_slabs = lambda g: g.reshape(2, g.shape[0] // 2, N_CHIP, g.shape[1] // N_CHIP).transpose(0, 2, 1, 3)
    gpacks_b = [to_slabs(dW_up),
                jnp.stack([_pack_rows(shard_items(s), 2 * Rh, F32).reshape(2, Rh, ROW) for s in range(N_CHIP)], axis=1)]
    (d_pfox, d_ccol, d_crow), gots_b = fox_bwd(p_fox, c_col, c_row, o_fox, lse, d_ofox, B, S, H,
                                              rider=_ride_exchange(gpacks_b))
    sums_b = [add_halves(g, got, idx, "add_halves_b%d" % i) for i, (g, got) in enumerate(zip(gpacks_b, gots_b))]

    (dq_i, dk_i, d_uh, d_wt, dgcr_a, d_gz, d_gn_parts), got16_b = gdn_inter_bwd(
        qkvn, u_hat, w_t, gcr5, states, o_gdn, p_gz, gdn_norm, d_ygdn, B, S, H,
        rider=_ride_scatter([s16 for _, s16 in sums_b]))
    d_gdn_norm = jnp.sum(d_gn_parts[:, :, 0, :], axis=(0, 1))[None]
    mine_b = [add_chips(s32, g16, idx, "add_chips_b%d" % i) for i, ((s32, _), g16) in enumerate(zip(sums_b, got16_b))]
    d_qkvn, d_betar5, dgcr_b = gdn_intra_bwd(qkvn, betar_u, gcr_u, t_inv, d_uh, d_wt, dq_i, dk_i, B, S, H)
    d_pgqkv, d_gconv_i = gdn_prep_bwd(p_gqkv, gconv_i, d_qkvn, B, S, H)

    tokens = lambda a: a.reshape(B, H, S).transpose(0, 2, 1).reshape(T, H)
    d_gc = dgcr_a.reshape(B, H, S) + dgcr_b.reshape(B, H, S)
    d_sm = jnp.concatenate([tokens(d_ccol.reshape(B, H, S) + d_crow.reshape(B, H, S)), tokens(d_gc), tokens(d_betar5.reshape(B, H, S)),
                            jnp.zeros((T, 128 - 3 * H), F32)], axis=1)
    d_psmall, d_prm = small_bwd(p_small, prm, d_sm, B, S, H)

    d_hn1 = matmul(d_pfox, W_foxT, "nn", "d_hn1_fox")
    d_hn1 = matmul(d_pgqkv, W_gqkvT, "nn", "d_hn1_gqkv", add=d_hn1)
    d_hn1 = matmul(d_gz, W_gzT, "nn", "d_hn1_gz", add=d_hn1)
    d_hn1 = matmul(d_gates, W_gatesT, "nn", "d_hn1_gates", add=d_hn1)
    grad_x, d_norm_mix = matmul(d_psmall, W_smallT, "nn", "d_hn1_small", add=d_hn1,
                                post=_post_rmsnorm_bwd(x2, norm_mix, dh1, False))
    dW_foxT = matmul(d_pfox, hn1, "tn", "dw_fox")
    dW_gqkvT = matmul(d_pgqkv, hn1, "tn", "dw_gqkv")
    dW_gzT = matmul(d_gz, hn1, "tn", "dw_gz")
    dW_gatesT = matmul(d_gates, hn1, "tn", "dw_gates")
    dW_smallT = matmul(d_psmall, hn1, "tn", "dw_small")

    dW_inT = jnp.concatenate([_deinterleave_head_rows(dW_foxT, H), dW_smallT[0:H], _deinterleave_head_rows(dW_gqkvT, H),
                              dW_smallT[H:3 * H], dW_gzT, dW_gatesT], axis=0)
    d_gconv = _deinterleave_heads(d_gconv_i, H)

    gpack_a = [dW_inT.reshape(N_CHIP, c_in, 2, D // 2).transpose(2, 0, 1, 3)]
    gots_a = _run_alone(_ride_exchange(gpack_a), "exchange_halves")
    sums_a = [add_halves(gpack_a[0], gots_a[0], idx, "add_halves_a")]
    got16_a = _run_alone(_ride_scatter([sums_a[0][1]]), "scatter_chips")
    mine = [add_chips(sums_a[0][0], got16_a[0], idx, "add_chips_a")] + mine_b
    others = share_halves(mine)
    g_w_inT, g_up, g_rows = (jnp.concatenate([jnp.where(cidx == 0, h, o), jnp.where(cidx == 0, o, h)], axis=ax)
                             for h, o, ax in zip(mine, others, (1, 0, 0)))
    g_w_in = g_w_inT.T
    g_bf, g_bg, g_out, g_down, g_fconv = _unpack_rows(g_rows, g_shapes)

    small_items = [d_norm_mix, d_norm_ffn, d_norm_final, d_gdn_norm, d_prm, d_gconv]
    small_shapes = [a.shape for a in small_items]
    sv = allreduce_small(_pack_rows(small_items, 0, F32, unit=8))
    g_norm_mix, g_norm_ffn, g_norm_final, g_gdn_norm, g_prm, g_gconv_all = _unpack_rows(sv, small_shapes, unit=8)
    g_norm_final = g_norm_final.reshape(D)
    g_fbias, g_dtb, g_alog = g_prm[0:1, 0:H], g_prm[0:1, H:2 * H], g_prm[1:2, H:2 * H]
    g_gconv = lax.dynamic_slice_in_dim(g_gconv_all, sidx * (3 * D // N_CHIP), 3 * D // N_CHIP, axis=1)

    names = ["norm_mix", "w_in", "fox_f_bias", "gdn_conv_w", "gdn_a_log", "gdn_dt_bias", "gdn_norm", "w_branch_fox",
             "w_branch_gdn", "w_out", "norm_ffn", "w_up", "ffn_conv_w", "w_down", "norm_final"]
    ws = [norm_mix, w_in, fox_f_bias, gdn_conv_w, gdn_a_log, gdn_dt_bias, gdn_norm, w_branch_fox, w_branch_gdn, w_out,
          norm_ffn, w_up, ffn_conv_w, w_down, norm_final]
    ms = [m_norm_mix, m_w_in, m_fox_f_bias, m_gdn_conv_w, m_gdn_a_log, m_gdn_dt_bias, m_gdn_norm, m_w_branch_fox,
          m_w_branch_gdn, m_w_out, m_norm_ffn, m_w_up, m_ffn_conv_w, m_w_down, m_norm_final]
    vs = [v_norm_mix, v_w_in, v_fox_f_bias, v_gdn_conv_w, v_gdn_a_log, v_gdn_dt_bias, v_gdn_norm, v_w_branch_fox,
          v_w_branch_gdn, v_w_out, v_norm_ffn, v_w_up, v_ffn_conv_w, v_w_down, v_norm_final]
    gs = [g_norm_mix, g_w_in, g_fbias, g_gconv, g_alog, g_dtb, g_gdn_norm, g_bf, g_bg, g_out, g_norm_ffn, g_up,
          g_fconv, g_down, g_norm_final]
    gs = [g.reshape(w.shape) for g, w in zip(gs, ws)]
    deltas, new_ms, new_vs = [], [], []
    for nm, w, g, m, v in zip(names, ws, gs, ms, vs):
        if w.ndim == 1:
            d, a, b = adamw(w.reshape(1, -1), g.reshape(1, -1), m.reshape(1, -1), v.reshape(1, -1), "adamw_" + nm)
            d, a, b = d.reshape(w.shape), a.reshape(w.shape), b.reshape(w.shape)
        elif nm == "w_in":
            d, a, b = (r.T[None] for r in adamw(w[0].T, g_w_inT, m[0].T, v[0].T, "adamw_" + nm))
        else:
            d, a, b = adamw(w, g, m, v, "adamw_" + nm)
        deltas.append(d)
        new_ms.append(a)
        new_vs.append(b)

    return (loss, grad_x.reshape(B, S, D), *gs, *deltas, *new_ms, *new_vs)
```

```python
import functools
import math

import jax
import jax.numpy as jnp
from jax import lax
from jax.experimental import pallas as pl
from jax.experimental.pallas import tpu as pltpu

F32 = jnp.float32
BF16 = jnp.bfloat16
HEAD = 128
CHUNK = 64
GDN_CONV = 4
FFN_CONV = 3
EPS = 1e-6
NEG = -1e30
ROW = 1024
ATT_TILE = 512
MM_WEIGHT_TILE_BYTES = 8 << 20
N_CHIP = 4
N_DEV = 8
MESH = pl.DeviceIdType.MESH
HI = lax.Precision.HIGH
EXACT = lax.Precision.HIGHEST

ADAM_LR, ADAM_B1, ADAM_B2, ADAM_EPS, ADAM_WD, ADAM_STEP = 0.001, 0.9, 0.999, 1e-08, 0.01, 10


def _tile(n, cap, unit=128):
    best = None
    t = unit
    while t <= min(n, cap):
        if n % t == 0:
            best = t
        t += unit
    return best if best is not None else n


def _params(*sem):
    return pltpu.CompilerParams(dimension_semantics=sem)


_NN = (((1,), (0,)), ((), ()))
_NT = (((1,), (1,)), ((), ()))
_TN = (((0,), (0,)), ((), ()))


def _dg(a, b, dims, hi):
    if hi:
        return lax.dot_general(a, b, dims, precision=HI, preferred_element_type=F32)
    return lax.dot_general(a.astype(BF16), b.astype(BF16), dims, preferred_element_type=F32)


class _RawOps:
    @staticmethod
    def nn(a, b, hi=False):
        return _dg(a, b, _NN, hi)

    @staticmethod
    def nt(a, b, hi=False):
        return _dg(a, b, _NT, hi)

    @staticmethod
    def tn(a, b, hi=False):
        return _dg(a, b, _TN, hi)


def _make_diff_ops():
    def build(hi):
        @jax.custom_vjp
        def nn(a, b):
            return _dg(a, b, _NN, hi)

        nn.defvjp(lambda a, b: (_dg(a, b, _NN, hi), (a, b)),
                  lambda r, g: (_dg(g, r[1], _NT, hi), _dg(r[0], g, _TN, hi)))

        @jax.custom_vjp
        def nt(a, b):
            return _dg(a, b, _NT, hi)

        nt.defvjp(lambda a, b: (_dg(a, b, _NT, hi), (a, b)),
                  lambda r, g: (_dg(g, r[1], _NN, hi), _dg(g, r[0], _TN, hi)))

        @jax.custom_vjp
        def tn(a, b):
            return _dg(a, b, _TN, hi)

        tn.defvjp(lambda a, b: (_dg(a, b, _TN, hi), (a, b)),
                  lambda r, g: (_dg(r[1], g, _NT, hi), _dg(r[0], g, _NN, hi)))
        return nn, nt, tn

    lo, hi_ = build(False), build(True)

    class _DiffOps:
        @staticmethod
        def nn(a, b, hi=False):
            return (hi_ if hi else lo)[0](a, b)

        @staticmethod
        def nt(a, b, hi=False):
            return (hi_ if hi else lo)[1](a, b)

        @staticmethod
        def tn(a, b, hi=False):
            return (hi_ if hi else lo)[2](a, b)

    return _DiffOps


_DiffOps = _make_diff_ops()


def _sigmoid(x):
    return 1.0 / (1.0 + jnp.exp(-x))


def _mm_tile(n, pref):
    if n % pref == 0:
        return pref
    if n % 1408 == 0:
        return 1408
    return _tile(n, pref)


class _Post:
    def __init__(self, fn, row_ins=(), vec_ins=(), row_outs=(), acc_outs=(), keep_main=True):
        self.fn, self.keep_main = fn, keep_main
        self.row_ins = [r if isinstance(r, tuple) else (r, r.shape[1], 0) for r in row_ins]
        self.vec_ins, self.row_outs, self.acc_outs = list(vec_ins), list(row_outs), list(acc_outs)


def matmul(a, b, mode, name, add=None, out_dtype=F32, post=None):
    if mode == "nn":
        (M, K), (K2, N) = a.shape, b.shape
    elif mode == "nt":
        (M, K), (N, K2) = a.shape, b.shape
    else:
        (K, M), (K2, N) = a.shape, b.shape
    assert K == K2, (name, a.shape, b.shape)
    tn = _mm_tile(N, 1024)
    if mode == "tn":
        tm = M if M <= 1408 else _mm_tile(M, 1408)
        tk = _mm_tile(K, 2048)
    else:
        tk = K if K * tn * 2 <= MM_WEIGHT_TILE_BYTES else _mm_tile(K, 1024)
        tm = _mm_tile(M, 1024 if tk <= 2048 and post is None else 512)
    nk = K // tk
    assert post is None or (mode != "tn" and tn == N), name
    dims = {"nn": _NN, "nt": _NT, "tn": _TN}[mode]
    if mode == "tn":
        a_spec = pl.BlockSpec((tk, tm), lambda j, i, k: (k, i))
    else:
        a_spec = pl.BlockSpec((tm, tk), lambda j, i, k: (i, k))
    if mode == "nt":
        b_spec = pl.BlockSpec((tn, tk), lambda j, i, k: (j, k))
    else:
        b_spec = pl.BlockSpec((tk, tn), lambda j, i, k: (k, j))
    o_spec = pl.BlockSpec((tm, tn), lambda j, i, k: (i, j))
    has_add = add is not None
    keep_main = post is None or post.keep_main
    counts = [2 + has_add] + ([len(post.row_ins), len(post.vec_ins)] if post else [0, 0]) + [int(keep_main)]
    counts += ([len(post.row_outs), len(post.acc_outs)] if post else [0, 0]) + [int(nk > 1)]

    def body(*refs):
        parts, p = [], 0
        for cnt in counts:
            parts.append(refs[p:p + cnt])
            p += cnt
        core, row_ins, vec_ins, main, row_outs, acc_outs, acc = parts
        a_ref, b_ref = core[:2]
        prod = lax.dot_general(a_ref[...].astype(BF16), b_ref[...].astype(BF16), dims, preferred_element_type=F32)

        def finish(r):
            if has_add:
                r = r + core[2][...]
            if keep_main:
                main[0][...] = r.astype(out_dtype)
            if post is not None:
                @pl.when(pl.program_id(1) == 0)
                def _():
                    for ref in acc_outs:
                        ref[...] = jnp.zeros_like(ref)

                post.fn(r, row_ins, vec_ins, row_outs, acc_outs)

        if nk == 1:
            finish(prod)
            return
        acc_ref = acc[0]
        k = pl.program_id(2)

        @pl.when(k == 0)
        def _():
            acc_ref[...] = jnp.zeros_like(acc_ref)

        acc_ref[...] += prod

        @pl.when(k == nk - 1)
        def _():
            finish(acc_ref[...])

    in_specs = [a_spec, b_spec] + ([o_spec] if has_add else [])
    args = (a, b) + ((add,) if has_add else ())
    out_specs = [o_spec] if keep_main else []
    out_shape = [jax.ShapeDtypeStruct((M, N), out_dtype)] if keep_main else []
    if post is not None:
        in_specs += [pl.BlockSpec((tm, cols), lambda j, i, k, cb=cb: (i, cb)) for _, cols, cb in post.row_ins]
        in_specs += [pl.BlockSpec((1, v.shape[1]), lambda j, i, k: (0, 0)) for v in post.vec_ins]
        args += tuple(r for r, _, _ in post.row_ins) + tuple(post.vec_ins)
        out_specs += [pl.BlockSpec((tm, cols), lambda j, i, k: (i, 0)) for cols, _ in post.row_outs]
        out_specs += [pl.BlockSpec((1, cols), lambda j, i, k: (0, 0)) for cols in post.acc_outs]
        out_shape += [jax.ShapeDtypeStruct((M, cols), dt) for cols, dt in post.row_outs]
        out_shape += [jax.ShapeDtypeStruct((1, cols), F32) for cols in post.acc_outs]
    rows_sem = "arbitrary" if post is not None and post.acc_outs else "parallel"
    res = pl.pallas_call(
        body, name=name, grid=(N // tn, M // tm, nk), in_specs=in_specs, out_specs=out_specs, out_shape=out_shape,
        scratch_shapes=[pltpu.VMEM((tm, tn), F32)] if nk > 1 else [],
        compiler_params=_params("parallel", rows_sem, "arbitrary"),
    )(*args)
    return res[0] if post is None else res


def rmsnorm_fwd(x, g, name):
    T, D = x.shape
    tm = _tile(T, 512, 8)

    def body(x_ref, g_ref, o_ref):
        xv = x_ref[...]
        r = lax.rsqrt(jnp.mean(xv * xv, axis=-1, keepdims=True) + EPS)
        o_ref[...] = (xv * r * g_ref[...]).astype(BF16)

    return pl.pallas_call(
        body, name=name, grid=(T // tm,),
        in_specs=[pl.BlockSpec((tm, D), lambda i: (i, 0)), pl.BlockSpec((1, D), lambda i: (0, 0))],
        out_specs=pl.BlockSpec((tm, D), lambda i: (i, 0)),
        out_shape=jax.ShapeDtypeStruct((T, D), BF16),
        compiler_params=_params("parallel"),
    )(x, g)


def _post_rmsnorm(g):
    def fn(r, row_ins, vec_ins, row_outs, acc_outs):
        rs = lax.rsqrt(jnp.mean(r * r, axis=-1, keepdims=True) + EPS)
        row_outs[0][...] = (r * rs * vec_ins[0][...]).astype(BF16)

    return _Post(fn, vec_ins=[g], row_outs=[(g.shape[1], BF16)])


def _post_rmsnorm_bwd(x, g, dres, with_bf16):
    D = g.shape[1]

    def fn(dy, row_ins, vec_ins, row_outs, acc_outs):
        xv = row_ins[0][...]
        rs = lax.rsqrt(jnp.mean(xv * xv, axis=-1, keepdims=True) + EPS)
        xh = xv * rs
        acc_outs[0][...] += jnp.sum(dy * xh, axis=0, keepdims=True)
        dxh = dy * vec_ins[0][...]
        dx = row_ins[1][...] + rs * (dxh - xh * jnp.mean(dxh * xh, axis=-1, keepdims=True))
        row_outs[0][...] = dx
        if with_bf16:
            row_outs[1][...] = dx.astype(BF16)

    return _Post(fn, row_ins=[x, dres], vec_ins=[g], row_outs=[(D, F32)] + ([(D, BF16)] if with_bf16 else []),
                 acc_outs=[D], keep_main=False)


def _post_loss(g, target):
    D = g.shape[1]

    def fn(hv, row_ins, vec_ins, row_outs, acc_outs):
        rs = lax.rsqrt(jnp.mean(hv * hv, axis=-1, keepdims=True) + EPS)
        xh = hv * rs
        gv = vec_ins[0][...]
        err = xh * gv - row_ins[0][...]
        acc_outs[0][...] += jnp.sum(err * err, axis=0, keepdims=True)
        dy = err * (1.0 / D)
        acc_outs[1][...] += jnp.sum(dy * xh, axis=0, keepdims=True)
        dxh = dy * gv
        dh = rs * (dxh - xh * jnp.mean(dxh * xh, axis=-1, keepdims=True))
        row_outs[0][...] = dh
        row_outs[1][...] = dh.astype(BF16)

    return _Post(fn, row_ins=[target], vec_ins=[g], row_outs=[(D, F32), (D, BF16)], acc_outs=[D, D], keep_main=False)


def _shift_down(x, k):
    if k == 0:
        return x
    rows = lax.broadcasted_iota(jnp.int32, x.shape, 0)
    return jnp.where(rows >= k, pltpu.roll(x, k, 0), 0.0)


def _shift_up(x, k):
    if k == 0:
        return x
    s = x.shape[0]
    rows = lax.broadcasted_iota(jnp.int32, x.shape, 0)
    return jnp.where(rows < s - k, pltpu.roll(x, s - k, 0), 0.0)


def _conv_fwd(x, w_ref, kw):
    y = x * w_ref[kw - 1:kw, :]
    for i in range(kw - 1):
        y = y + _shift_down(x, kw - 1 - i) * w_ref[i:i + 1, :]
    return y


def _conv_bwd(x, dy, w_ref, kw):
    dx = dy * w_ref[kw - 1:kw, :]
    dws = []
    for i in range(kw - 1):
        dx = dx + _shift_up(dy, kw - 1 - i) * w_ref[i:i + 1, :]
        dws.append(jnp.sum(dy * _shift_down(x, kw - 1 - i), axis=0, keepdims=True))
    dws.append(jnp.sum(dy * x, axis=0, keepdims=True))
    return dx, dws


def ffn_gate_fwd(up_g, up_v, cw_g, cw_v, B, S):
    T, Fd = up_g.shape
    tc = _tile(Fd, 256)

    def body(g_ref, v_ref, wg_ref, wv_ref, o_ref):
        ug = _conv_fwd(g_ref[...], wg_ref, FFN_CONV)
        uv = _conv_fwd(v_ref[...], wv_ref, FFN_CONV)
        o_ref[...] = (ug * _sigmoid(ug) * uv).astype(BF16)

    blk = pl.BlockSpec((S, tc), lambda b, j: (b, j))
    wblk = pl.BlockSpec((FFN_CONV, tc), lambda b, j: (0, j))
    return pl.pallas_call(
        body, name="ffn_gate_fwd", grid=(B, Fd // tc), in_specs=[blk, blk, wblk, wblk], out_specs=blk,
        out_shape=jax.ShapeDtypeStruct((T, Fd), BF16), compiler_params=_params("parallel", "parallel"),
    )(up_g, up_v, cw_g, cw_v)


def ffn_gate_bwd(up_g, up_v, cw_g, cw_v, d_act, B, S):
    T, Fd = up_g.shape
    tc = _tile(Fd, 256)

    def body(g_ref, v_ref, wg_ref, wv_ref, da_ref, dg_ref, dv_ref, dwg_ref, dwv_ref):
        @pl.when(pl.program_id(1) == 0)
        def _():
            dwg_ref[...] = jnp.zeros_like(dwg_ref)
            dwv_ref[...] = jnp.zeros_like(dwv_ref)

        xg, xv = g_ref[...], v_ref[...]
        ug = _conv_fwd(xg, wg_ref, FFN_CONV)
        uv = _conv_fwd(xv, wv_ref, FFN_CONV)
        da = da_ref[...]
        sg = _sigmoid(ug)
        d_ug = da * uv * (sg + ug * sg * (1.0 - sg))
        d_uv = da * ug * sg
        dxg, dwg = _conv_bwd(xg, d_ug, wg_ref, FFN_CONV)
        dxv, dwv = _conv_bwd(xv, d_uv, wv_ref, FFN_CONV)
        dg_ref[...] = dxg.astype(BF16)
        dv_ref[...] = dxv.astype(BF16)
        for i in range(FFN_CONV):
            dwg_ref[i:i + 1, :] += dwg[i]
            dwv_ref[i:i + 1, :] += dwv[i]

    blk = pl.BlockSpec((S, tc), lambda j, b: (b, j))
    wblk = pl.BlockSpec((FFN_CONV, tc), lambda j, b: (0, j))
    return pl.pallas_call(
        body, name="ffn_gate_bwd", grid=(Fd // tc, B), in_specs=[blk, blk, wblk, wblk, blk],
        out_specs=[blk, blk, wblk, wblk],
        out_shape=[jax.ShapeDtypeStruct((T, Fd), BF16), jax.ShapeDtypeStruct((T, Fd), BF16),
                   jax.ShapeDtypeStruct((FFN_CONV, Fd), F32), jax.ShapeDtypeStruct((FFN_CONV, Fd), F32)],
        compiler_params=_params("parallel", "arbitrary"),
    )(up_g, up_v, cw_g, cw_v, d_act)


def _post_merge(p_gates, bf_):
    D = bf_.shape[1]

    def fn(bg, row_ins, vec_ins, row_outs, acc_outs):
        gf_ref, gg_ref, bf_ref = row_ins
        row_outs[0][...] = (_sigmoid(gf_ref[...]) * bf_ref[...] + _sigmoid(gg_ref[...]) * bg).astype(BF16)

    return _Post(fn, row_ins=[(p_gates, D, 0), (p_gates, D, 1), bf_], row_outs=[(D, BF16)])


def _post_merge_bwd(p_gates, bf_, bg_):
    D = bf_.shape[1]

    def fn(d, row_ins, vec_ins, row_outs, acc_outs):
        gf_ref, gg_ref, bf_ref, bg_ref = row_ins
        sf, sg = _sigmoid(gf_ref[...]), _sigmoid(gg_ref[...])
        row_outs[0][...] = (d * sf).astype(BF16)
        row_outs[1][...] = (d * sg).astype(BF16)
        row_outs[2][:, 0:D] = (d * bf_ref[...] * sf * (1.0 - sf)).astype(BF16)
        row_outs[2][:, D:2 * D] = (d * bg_ref[...] * sg * (1.0 - sg)).astype(BF16)

    return _Post(fn, row_ins=[(p_gates, D, 0), (p_gates, D, 1), bf_, bg_],
                 row_outs=[(D, BF16), (D, BF16), (2 * D, BF16)], keep_main=False)


def fox_fwd(p_fox, c_col, c_row, B, S, H, rider=None):
    T = B * S
    t = _tile(S, ATT_TILE)
    nq = S // t
    scale = HEAD ** -0.5

    def body(q_ref, k_ref, v_ref, cq_ref, cr_ref, o_ref, o16_ref, lse_ref):
        i = pl.program_id(2)
        q = q_ref[...]
        cq = cq_ref[...]
        row = lax.broadcasted_iota(jnp.int32, (t, t), 0)
        col = lax.broadcasted_iota(jnp.int32, (t, t), 1)

        def step(j, carry, diagonal):
            m, l, acc = carry
            off = pl.multiple_of(j * t, t)
            k = k_ref[pl.ds(off, t), :]
            v = v_ref[pl.ds(off, t), :]
            s = lax.dot_general(q, k, _NT, preferred_element_type=F32) * scale + (cq - cr_ref[:, pl.ds(off, t)])
            if diagonal:
                s = jnp.where(col <= row, s, NEG)
            m_new = jnp.maximum(m, jnp.max(s, axis=-1, keepdims=True))
            alpha = jnp.exp(m - m_new)
            p = jnp.exp(s - m_new)
            l = alpha * l + jnp.sum(p, axis=-1, keepdims=True)
            acc = alpha * acc + lax.dot_general(p.astype(BF16), v, _NN, preferred_element_type=F32)
            return m_new, l, acc

        m0 = jnp.full((t, 1), NEG, F32)
        below = lax.fori_loop(0, i, functools.partial(step, diagonal=False),
                              (m0, jnp.zeros((t, 1), F32), jnp.zeros((t, HEAD), F32)))
        m, l, acc = step(i, below, diagonal=True)
        o = acc / l
        o_ref[...] = o
        o16_ref[...] = o.astype(BF16)
        lse_ref[...] = m + jnp.log(l)

    return _hosted_call(
        body, rider, name="fox_fwd", grid=(B, H, nq),
        in_specs=[pl.BlockSpec((t, HEAD), lambda b, h, i: (b * nq + i, 3 * h)),
                  pl.BlockSpec((S, HEAD), lambda b, h, i: (b, 3 * h + 1)),
                  pl.BlockSpec((S, HEAD), lambda b, h, i: (b, 3 * h + 2)),
                  pl.BlockSpec((None, None, t, 1), lambda b, h, i: (b, h, i, 0)),
                  pl.BlockSpec((None, None, 1, S), lambda b, h, i: (b, h, 0, 0))],
        out_specs=[pl.BlockSpec((t, HEAD), lambda b, h, i: (b * nq + i, h)),
                   pl.BlockSpec((t, HEAD), lambda b, h, i: (b * nq + i, h)),
                   pl.BlockSpec((None, None, t, 1), lambda b, h, i: (b, h, i, 0))],
        out_shape=[jax.ShapeDtypeStruct((T, H * HEAD), F32), jax.ShapeDtypeStruct((T, H * HEAD), BF16),
                   jax.ShapeDtypeStruct((B, H, S, 1), F32)],
        scratch_shapes=[], semantics=("parallel", "parallel", "arbitrary"),
    )(p_fox, p_fox, p_fox, c_col, c_row)


def fox_bwd(p_fox, c_col, c_row, o, lse, do, B, S, H, rider=None):
    T = B * S
    t = _tile(S, ATT_TILE)
    n = S // t
    scale = HEAD ** -0.5

    def body(q_ref, k_ref, v_ref, cq_ref, cr_ref, o_ref, lse_ref, do_ref, dqkv_ref, dcq_ref, dcr_ref, dq_acc, delta_s):
        row = lax.broadcasted_iota(jnp.int32, (t, t), 0)
        col = lax.broadcasted_iota(jnp.int32, (t, t), 1)

        def prep(i, c):
            rows = pl.ds(pl.multiple_of(i * t, t), t)
            delta_s[rows, :] = jnp.sum(do_ref[rows, :] * o_ref[rows, :], axis=-1, keepdims=True)
            dq_acc[rows, :] = jnp.zeros((t, HEAD), F32)
            dcq_ref[rows, :] = jnp.zeros((t, 1), F32)
            return c

        lax.fori_loop(0, n, prep, 0)

        def kv_step(j, c):
            joff = pl.multiple_of(j * t, t)
            k = k_ref[pl.ds(joff, t), :]
            v = v_ref[pl.ds(joff, t), :]
            crj = cr_ref[:, pl.ds(joff, t)]

            def q_step(i, carry, diagonal):
                dk, dv, dc = carry
                rows = pl.ds(pl.multiple_of(i * t, t), t)
                q = q_ref[rows, :]
                dob = do_ref[rows, :].astype(BF16)
                s = lax.dot_general(q, k, _NT, preferred_element_type=F32) * scale + (cq_ref[rows, :] - crj)
                if diagonal:
                    s = jnp.where(col <= row, s, NEG)
                p = jnp.exp(s - lse_ref[rows, :])
                dp = lax.dot_general(dob, v, _NT, preferred_element_type=F32)
                ds = p * (dp - delta_s[rows, :])
                dsb = ds.astype(BF16)
                dv = dv + lax.dot_general(p.astype(BF16), dob, _TN, preferred_element_type=F32)
                dk = dk + lax.dot_general(dsb, q, _TN, preferred_element_type=F32)
                dq_acc[rows, :] += lax.dot_general(dsb, k, _NN, preferred_element_type=F32) * scale
                dc = dc + jnp.sum(ds, axis=0, keepdims=True)
                dcq_ref[rows, :] += jnp.sum(ds, axis=-1, keepdims=True)
                return dk, dv, dc

            z = jnp.zeros((t, HEAD), F32)
            on_diagonal = q_step(j, (z, z, jnp.zeros((1, t), F32)), diagonal=True)
            dk, dv, dc = lax.fori_loop(j + 1, n, functools.partial(q_step, diagonal=False), on_diagonal)
            dqkv_ref[pl.ds(joff, t), HEAD:2 * HEAD] = (dk * scale).astype(BF16)
            dqkv_ref[pl.ds(joff, t), 2 * HEAD:3 * HEAD] = dv.astype(BF16)
            dcr_ref[:, pl.ds(joff, t)] = -dc
            return c

        lax.fori_loop(0, n, kv_step, 0)
        dqkv_ref[:, 0:HEAD] = dq_acc[...].astype(BF16)

    col_spec = pl.BlockSpec((None, None, S, 1), lambda b, h: (b, h, 0, 0))
    row_spec = pl.BlockSpec((None, None, 1, S), lambda b, h: (b, h, 0, 0))
    head = pl.BlockSpec((S, HEAD), lambda b, h: (b, h))
    return _hosted_call(
        body, rider, name="fox_bwd", grid=(B, H),
        in_specs=[pl.BlockSpec((S, HEAD), lambda b, h: (b, 3 * h)),
                  pl.BlockSpec((S, HEAD), lambda b, h: (b, 3 * h + 1)),
                  pl.BlockSpec((S, HEAD), lambda b, h: (b, 3 * h + 2)),
                  col_spec, row_spec, head, col_spec, head],
        out_specs=[pl.BlockSpec((S, 3 * HEAD), lambda b, h: (b, h)), col_spec, row_spec],
        out_shape=[jax.ShapeDtypeStruct((T, 3 * H * HEAD), BF16), jax.ShapeDtypeStruct((B, H, S, 1), F32),
                   jax.ShapeDtypeStruct((B, H, 1, S), F32)],
        scratch_shapes=[pltpu.VMEM((S, HEAD), F32), pltpu.VMEM((S, 1), F32)], semantics=("parallel", "parallel"),
    )(p_fox, p_fox, p_fox, c_col, c_row, o, lse, do)


def _small_fn(x, b0, b1, H):
    S = x.shape[0]
    lane = lax.broadcasted_iota(jnp.int32, x.shape, 1)
    z = x + b0
    tail = jnp.log1p(jnp.exp(-jnp.abs(z)))
    softplus = jnp.maximum(z, 0.0) + tail
    logsig = -(jnp.maximum(-z, 0.0) + tail)
    g = -jnp.exp(b1) * softplus
    pre = jnp.where(lane < H, logsig, jnp.where(lane < 2 * H, g, 0.0))
    bl = _tile(S, 256, CHUNK)
    r = lax.broadcasted_iota(jnp.int32, (bl, bl), 0)
    c = lax.broadcasted_iota(jnp.int32, (bl, bl), 1)
    tri = (r >= c).astype(F32)
    tri_chunk = jnp.where((r >= c) & (jnp.right_shift(r, 6) == jnp.right_shift(c, 6)), 1.0, 0.0)
    carry = jnp.zeros((1, x.shape[1]), F32)
    parts = []
    for i in range(S // bl):
        blk = pre[i * bl:(i + 1) * bl, :]
        full = lax.dot_general(tri, blk, _NN, precision=EXACT, preferred_element_type=F32) + carry
        chunked = lax.dot_general(tri_chunk, blk, _NN, precision=EXACT, preferred_element_type=F32)
        parts.append(jnp.where(lane[:bl] < H, full, chunked))
        carry = carry + jnp.sum(blk, axis=0, keepdims=True)
    cum = parts[0] if len(parts) == 1 else jnp.concatenate(parts, axis=0)
    return jnp.where(lane < 2 * H, cum, jnp.where(lane < 3 * H, _sigmoid(x), 0.0))


def small_fwd(p_small, prm, B, S, H):
    T = B * S

    def body(x_ref, p_ref, o_ref):
        o_ref[...] = _small_fn(x_ref[...], p_ref[0:1, :], p_ref[1:2, :], H)

    blk = pl.BlockSpec((S, 128), lambda b: (b, 0))
    return pl.pallas_call(
        body, name="small_fwd", grid=(B,), in_specs=[blk, pl.BlockSpec((8, 128), lambda b: (0, 0))], out_specs=blk,
        out_shape=jax.ShapeDtypeStruct((T, 128), F32), compiler_params=_params("parallel"),
    )(p_small, prm)


def small_bwd(p_small, prm, d_out, B, S, H):
    T = B * S

    def body(x_ref, p_ref, d_ref, dx_ref, dp_ref):
        @pl.when(pl.program_id(0) == 0)
        def _():
            dp_ref[...] = jnp.zeros_like(dp_ref)

        _, vjp = jax.vjp(functools.partial(_small_fn, H=H), x_ref[...], p_ref[0:1, :], p_ref[1:2, :])
        dx, db0, db1 = vjp(d_ref[...])
        dx_ref[...] = dx.astype(BF16)
        dp_ref[0:1, :] += db0
        dp_ref[1:2, :] += db1

    blk = pl.BlockSpec((S, 128), lambda b: (b, 0))
    pblk = pl.BlockSpec((8, 128), lambda b: (0, 0))
    return pl.pallas_call(
        body, name="small_bwd", grid=(B,), in_specs=[blk, pblk, blk], out_specs=[blk, pblk],
        out_shape=[jax.ShapeDtypeStruct((T, 128), BF16), jax.ShapeDtypeStruct((8, 128), F32)],
        compiler_params=_params("arbitrary"),
    )(p_small, prm, d_out)


def gdn_prep_fwd(p_gqkv, cw, B, S, H):
    T = B * S

    def body(x_ref, w_ref, o_ref):
        y = _conv_fwd(x_ref[...], w_ref, GDN_CONV)
        a = y * _sigmoid(y)
        rs = lax.rsqrt(jnp.sum(a * a, axis=-1, keepdims=True) + EPS)
        is_qk = (pl.program_id(1) % 3) < 2
        o_ref[...] = a * jnp.where(is_qk, rs, 1.0)

    blk = pl.BlockSpec((S, HEAD), lambda b, n: (b, n))
    wblk = pl.BlockSpec((GDN_CONV, HEAD), lambda b, n: (0, n))
    return pl.pallas_call(
        body, name="gdn_prep_fwd", grid=(B, 3 * H), in_specs=[blk, wblk], out_specs=blk,
        out_shape=jax.ShapeDtypeStruct((T, 3 * H * HEAD), F32), compiler_params=_params("parallel", "parallel"),
    )(p_gqkv, cw)


def gdn_prep_bwd(p_gqkv, cw, d_out, B, S, H):
    T = B * S

    def body(x_ref, w_ref, d_ref, dx_ref, dw_ref):
        @pl.when(pl.program_id(1) == 0)
        def _():
            dw_ref[...] = jnp.zeros_like(dw_ref)

        x = x_ref[...]
        y = _conv_fwd(x, w_ref, GDN_CONV)
        sg = _sigmoid(y)
        a = y * sg
        rs = lax.rsqrt(jnp.sum(a * a, axis=-1, keepdims=True) + EPS)
        d = d_ref[...]
        out = a * rs
        da_qk = rs * (d - out * jnp.sum(d * out, axis=-1, keepdims=True))
        is_qk = (pl.program_id(0) % 3) < 2
        da = jnp.where(is_qk, da_qk, d)
        dy = da * (sg + y * sg * (1.0 - sg))
        dx, dws = _conv_bwd(x, dy, w_ref, GDN_CONV)
        dx_ref[...] = dx.astype(BF16)
        for i in range(GDN_CONV):
            dw_ref[i:i + 1, :] += dws[i]

    blk = pl.BlockSpec((S, HEAD), lambda n, b: (b, n))
    wblk = pl.BlockSpec((GDN_CONV, HEAD), lambda n, b: (0, n))
    return pl.pallas_call(
        body, name="gdn_prep_bwd", grid=(3 * H, B), in_specs=[blk, wblk, blk], out_specs=[blk, wblk],
        out_shape=[jax.ShapeDtypeStruct((T, 3 * H * HEAD), BF16), jax.ShapeDtypeStruct((GDN_CONV, 3 * H * HEAD), F32)],
        compiler_params=_params("parallel", "arbitrary"),
    )(p_gqkv, cw, d_out)


@jax.custom_vjp
def _given_inverse(a, t):
    return t


def _given_inverse_fwd(a, t):
    return t, t


def _given_inverse_bwd(t, g):
    x = _dg(t, g, _TN, True)
    return -_dg(x, t, _NT, True), jnp.zeros_like(t)


_given_inverse.defvjp(_given_inverse_fwd, _given_inverse_bwd)


def _to_col(row):
    n = row.shape[1]
    r = lax.broadcasted_iota(jnp.int32, (n, n), 0)
    c = lax.broadcasted_iota(jnp.int32, (n, n), 1)
    return jnp.sum(jnp.where(r == c, row, 0.0), axis=1, keepdims=True)


def _intra_fn(k, v, beta_r, gcr, ops, t_known=None):
    n = len(k)
    m = k[0].shape[0]
    r = lax.broadcasted_iota(jnp.int32, (m, m), 0)
    c = lax.broadcasted_iota(jnp.int32, (m, m), 1)
    below = (r > c) & (jnp.right_shift(r, 6) == jnp.right_shift(c, 6))
    beta = [_to_col(beta_r[i]) for i in range(n)]
    gcc = [_to_col(gcr[i]) for i in range(n)]
    decay = [jnp.exp(jnp.where(below, gcc[i] - gcr[i], NEG)) for i in range(n)]
    kb = [k[i] * beta[i] for i in range(n)]
    a = [ops.nt(kb[i], k[i]) * decay[i] for i in range(n)]
    if t_known is None:
        p = [-a[i] for i in range(n)]
        tm = [jnp.where(r == c, 1.0, 0.0) + p[i] for i in range(n)]
        for _ in range(5):
            p = [ops.nn(p[i], p[i], hi=True) for i in range(n)]
            tm = [tm[i] + ops.nn(tm[i], p[i], hi=True) for i in range(n)]
    else:
        tm = [_given_inverse(a[i], t_known[i]) for i in range(n)]
    u_hat = [ops.nn(tm[i], v[i] * beta[i], hi=True) for i in range(n)]
    w = [ops.nn(tm[i], kb[i] * jnp.exp(gcc[i]), hi=True) for i in range(n)]
    return tuple(u_hat), tuple(w), tuple(tm)


INTRA_NB = 16
PAIR = 1


def gdn_intra_fwd(qkvn, betar5, gcr5, B, S, H, rider=None):
    T = B * S
    UNIT = PAIR * CHUNK
    N = S // UNIT
    nb = min(INTRA_NB // PAIR, N)
    rows = nb * UNIT
    ns = N // nb

    def body(k_ref, v_ref, b_ref, gr_ref, uh_ref, w_ref, t_ref):
        sls = [slice(ci * UNIT, (ci + 1) * UNIT) for ci in range(nb)]
        u_hat, w, tm = _intra_fn(tuple(k_ref[sl, :] for sl in sls), tuple(v_ref[sl, :] for sl in sls),
                                 tuple(b_ref[ci] for ci in range(nb)), tuple(gr_ref[ci] for ci in range(nb)), _RawOps)
        for ci, sl in enumerate(sls):
            uh_ref[sl, :] = u_hat[ci]
            w_ref[sl, :] = w[ci]
            t_ref[ci] = tm[ci]

    rowspec = pl.BlockSpec((None, None, nb, 1, UNIT), lambda b, h, i: (b, h, i, 0, 0))
    sqspec = pl.BlockSpec((None, None, nb, UNIT, UNIT), lambda b, h, i: (b, h, i, 0, 0))
    out = pl.BlockSpec((rows, HEAD), lambda b, h, i: (b * ns + i, h))
    return _hosted_call(
        body, rider, name="gdn_intra_fwd", grid=(B, H, ns),
        in_specs=[pl.BlockSpec((rows, HEAD), lambda b, h, i: (b * ns + i, 3 * h + 1)),
                  pl.BlockSpec((rows, HEAD), lambda b, h, i: (b * ns + i, 3 * h + 2)),
                  rowspec, rowspec],
        out_specs=[out, out, sqspec],
        out_shape=[jax.ShapeDtypeStruct((T, H * HEAD), F32), jax.ShapeDtypeStruct((T, H * HEAD), F32),
                   jax.ShapeDtypeStruct((B, H, N, UNIT, UNIT), F32)],
        scratch_shapes=[], semantics=("parallel", "parallel", "parallel"),
    )(qkvn, qkvn, betar5, gcr5)


def gdn_intra_bwd(qkvn, betar5, gcr5, t_inv, d_uh, d_w, dq_in, dk_in, B, S, H):
    T = B * S
    UNIT = PAIR * CHUNK
    N = S // UNIT
    nb = min(INTRA_NB // PAIR, N)
    rows = nb * UNIT
    ns = N // nb

    def body(k_ref, v_ref, b_ref, gr_ref, t_ref, duh_ref, dw_ref, dq_ref, dk_ref, o_ref, db_ref, dgr_ref):
        sls = [slice(ci * UNIT, (ci + 1) * UNIT) for ci in range(nb)]
        chunks = range(nb)
        _, vjp = jax.vjp(
            functools.partial(_intra_fn, ops=_DiffOps, t_known=tuple(t_ref[ci] for ci in chunks)),
            tuple(k_ref[sl, :] for sl in sls), tuple(v_ref[sl, :] for sl in sls), tuple(b_ref[ci] for ci in chunks),
            tuple(gr_ref[ci] for ci in chunks))
        zero = jnp.zeros((UNIT, UNIT), F32)
        dk, dv, db, dgr = vjp((tuple(duh_ref[sl, :] for sl in sls), tuple(dw_ref[sl, :] for sl in sls),
                               tuple(zero for _ in chunks)))
        for ci, sl in enumerate(sls):
            o_ref[sl, 0:HEAD] = dq_ref[sl, :]
            o_ref[sl, HEAD:2 * HEAD] = dk[ci] + dk_ref[sl, :]
            o_ref[sl, 2 * HEAD:3 * HEAD] = dv[ci]
            db_ref[ci] = db[ci]
            dgr_ref[ci] = dgr[ci]

    rowspec = pl.BlockSpec((None, None, nb, 1, UNIT), lambda b, h, i: (b, h, i, 0, 0))
    sqspec = pl.BlockSpec((None, None, nb, UNIT, UNIT), lambda b, h, i: (b, h, i, 0, 0))
    head = pl.BlockSpec((rows, HEAD), lambda b, h, i: (b * ns + i, h))
    return pl.pallas_call(
        body, name="gdn_intra_bwd", grid=(B, H, ns),
        in_specs=[pl.BlockSpec((rows, HEAD), lambda b, h, i: (b * ns + i, 3 * h + 1)),
                  pl.BlockSpec((rows, HEAD), lambda b, h, i: (b * ns + i, 3 * h + 2)),
                  rowspec, rowspec, sqspec, head, head, head, head],
        out_specs=[pl.BlockSpec((rows, 3 * HEAD), lambda b, h, i: (b * ns + i, h)), rowspec, rowspec],
        out_shape=[jax.ShapeDtypeStruct((T, 3 * H * HEAD), F32),
                   jax.ShapeDtypeStruct((B, H, N, 1, UNIT), F32), jax.ShapeDtypeStruct((B, H, N, 1, UNIT), F32)],
        compiler_params=_params("parallel", "parallel", "parallel"),
    )(qkvn, qkvn, betar5, gcr5, t_inv, d_uh, d_w, dq_in, dk_in)


def _inter_fn(q, k, u_hat, w, gcr, state, ops):
    n = len(q)
    r = lax.broadcasted_iota(jnp.int32, (CHUNK, CHUNK), 0)
    c = lax.broadcasted_iota(jnp.int32, (CHUNK, CHUNK), 1)
    last = lax.broadcasted_iota(jnp.int32, (1, CHUNK), 1) == CHUNK - 1
    gcc = [_to_col(gcr[i]) for i in range(n)]
    gl = [jnp.sum(jnp.where(last, gcr[i], 0.0), axis=1, keepdims=True) for i in range(n)]
    decay = [jnp.exp(jnp.where(r >= c, gcc[i] - gcr[i], NEG)) for i in range(n)]
    qs = [q[i] * (HEAD ** -0.5) for i in range(n)]
    ws = [ops.nn(w[i], state[i]) for i in range(n)]
    qst = [ops.nn(qs[i] * jnp.exp(gcc[i]), state[i]) for i in range(n)]
    attn = [ops.nt(qs[i], k[i]) * decay[i] for i in range(n)]
    u = [u_hat[i] - ws[i] for i in range(n)]
    o = [qst[i] + ops.nn(attn[i], u[i]) for i in range(n)]
    kdu = [ops.tn(k[i] * jnp.exp(gl[i] - gcc[i]), u[i]) for i in range(n)]
    new_state = [state[i] * jnp.exp(gl[i]) + kdu[i] for i in range(n)]
    return tuple(o), tuple(new_state)


INTER_HEADS = 4
INTER_HEADS_FWD = 8
INTER_ROWS = 512


def _inter_specs(ts, ns, hp, backward):
    at = (lambda s: ns - 1 - s) if backward else (lambda s: s)
    nc = ts // CHUNK
    qk = []
    for hh in range(hp):
        qk.append(pl.BlockSpec((ts, HEAD), lambda b, g, s, hh=hh: (b * ns + at(s), 3 * (hp * g + hh))))
        qk.append(pl.BlockSpec((ts, HEAD), lambda b, g, s, hh=hh: (b * ns + at(s), 3 * (hp * g + hh) + 1)))
    heads = pl.BlockSpec((ts, hp * HEAD), lambda b, g, s: (b * ns + at(s), g))
    rowspec = pl.BlockSpec((None, hp, nc, 1, CHUNK), lambda b, g, s: (b, g, at(s), 0, 0))
    stspec = pl.BlockSpec((None, hp, nc, HEAD, HEAD), lambda b, g, s: (b, g, at(s), 0, 0))
    return qk, heads, rowspec, stspec


def gdn_inter_fwd(qkvn, u_hat, w, gcr5, p_gz, gnorm, B, S, H):
    T = B * S
    N = S // CHUNK
    hp = INTER_HEADS_FWD if H % INTER_HEADS_FWD == 0 else (INTER_HEADS if H % INTER_HEADS == 0 else 1)
    hs = range(hp)
    ts = _tile(S, INTER_ROWS, CHUNK)
    ns, nc = S // ts, ts // CHUNK

    def body(*refs):
        qk_refs, (uh_ref, w_ref, gr_ref, z_ref, gn_ref, o_ref, st_ref, y_ref, s_scr) = refs[:2 * hp], refs[2 * hp:]

        @pl.when(pl.program_id(2) == 0)
        def _():
            s_scr[...] = jnp.zeros_like(s_scr)

        gn = gn_ref[...]

        def step(n, c):
            rows = pl.ds(pl.multiple_of(n * CHUNK, CHUNK), CHUNK)
            st = tuple(s_scr[hh] for hh in hs)
            for hh in hs:
                st_ref[hh, n] = st[hh]
            o, new = _inter_fn(tuple(qk_refs[2 * hh][rows, :] for hh in hs), tuple(qk_refs[2 * hh + 1][rows, :] for hh in hs),
                               tuple(uh_ref[rows, hh * HEAD:(hh + 1) * HEAD] for hh in hs),
                               tuple(w_ref[rows, hh * HEAD:(hh + 1) * HEAD] for hh in hs),
                               tuple(gr_ref[hh, n] for hh in hs), st, _RawOps)
            for hh in hs:
                cols = slice(hh * HEAD, (hh + 1) * HEAD)
                o_ref[rows, cols] = o[hh]
                s_scr[hh] = new[hh]
                z = z_ref[rows, cols]
                r = lax.rsqrt(jnp.mean(o[hh] * o[hh], axis=-1, keepdims=True) + EPS)
                y_ref[rows, cols] = (o[hh] * r * gn * z * _sigmoid(z)).astype(BF16)
            return c

        lax.fori_loop(0, nc, step, 0)

    qk, heads, rowspec, stspec = _inter_specs(ts, ns, hp, backward=False)
    return pl.pallas_call(
        body, name="gdn_inter_fwd", grid=(B, H // hp, ns),
        in_specs=qk + [heads, heads, rowspec, heads, pl.BlockSpec((1, HEAD), lambda b, g, s: (0, 0))],
        out_specs=[heads, stspec, heads],
        out_shape=[jax.ShapeDtypeStruct((T, H * HEAD), F32), jax.ShapeDtypeStruct((B, H, N, HEAD, HEAD), F32),
                   jax.ShapeDtypeStruct((T, H * HEAD), BF16)],
        scratch_shapes=[pltpu.VMEM((hp, HEAD, HEAD), F32)],
        compiler_params=_params("parallel", "parallel", "arbitrary"),
    )(*([qkvn] * (2 * hp)), u_hat, w, gcr5, p_gz, gnorm)


def gdn_inter_bwd(qkvn, u_hat, w, gcr5, states, o, p_gz, gnorm, d_y, B, S, H, rider=None):
    T = B * S
    N = S // CHUNK
    hp = INTER_HEADS if H % INTER_HEADS == 0 else 1
    hs = range(hp)
    ts = _tile(S, INTER_ROWS, CHUNK)
    ns, nc = S // ts, ts // CHUNK

    def body(*refs):
        qk_refs = refs[:2 * hp]
        (uh_ref, w_ref, gr_ref, st_ref, o_ref, z_ref, gn_ref, dy_ref,
         dq_ref, dk_ref, duh_ref, dw_ref, dgr_ref, dz_ref, dgn_ref, ds_scr) = refs[2 * hp:]

        @pl.when(pl.program_id(2) == 0)
        def _():
            ds_scr[...] = jnp.zeros_like(ds_scr)
            dgn_ref[...] = jnp.zeros_like(dgn_ref)

        cols = [slice(hh * HEAD, (hh + 1) * HEAD) for hh in hs]
        gn = gn_ref[...]

        def through_norm(rows, hh):
            ov, z, d = o_ref[rows, cols[hh]], z_ref[rows, cols[hh]], dy_ref[rows, cols[hh]]
            r = lax.rsqrt(jnp.mean(ov * ov, axis=-1, keepdims=True) + EPS)
            xh = ov * r
            sg = _sigmoid(z)
            d_n = d * (z * sg)
            dz_ref[rows, cols[hh]] = (d * xh * gn * (sg + z * sg * (1.0 - sg))).astype(BF16)
            dgn_ref[0:1, :] += jnp.sum(d_n * xh, axis=0, keepdims=True)
            dxh = d_n * gn
            return r * (dxh - xh * jnp.mean(dxh * xh, axis=-1, keepdims=True))

        def step(i, c):
            n = nc - 1 - i
            rows = pl.ds(pl.multiple_of(n * CHUNK, CHUNK), CHUNK)
            _, vjp = jax.vjp(functools.partial(_inter_fn, ops=_DiffOps),
                             tuple(qk_refs[2 * hh][rows, :] for hh in hs), tuple(qk_refs[2 * hh + 1][rows, :] for hh in hs),
                             tuple(uh_ref[rows, cols[hh]] for hh in hs), tuple(w_ref[rows, cols[hh]] for hh in hs),
                             tuple(gr_ref[hh, n] for hh in hs), tuple(st_ref[hh, n] for hh in hs))
            dq, dk, duh, dw, dgr, ds = vjp((tuple(through_norm(rows, hh) for hh in hs), tuple(ds_scr[hh] for hh in hs)))
            for hh in hs:
                dq_ref[rows, cols[hh]] = dq[hh]
                dk_ref[rows, cols[hh]] = dk[hh]
                duh_ref[rows, cols[hh]] = duh[hh]
                dw_ref[rows, cols[hh]] = dw[hh]
                dgr_ref[hh, n] = dgr[hh]
                ds_scr[hh] = ds[hh]
            return c

        lax.fori_loop(0, nc, step, 0)

    qk, heads, rowspec, stspec = _inter_specs(ts, ns, hp, backward=True)
    hshape = jax.ShapeDtypeStruct((T, H * HEAD), F32)
    return _hosted_call(
        body, rider, name="gdn_inter_bwd", grid=(B, H // hp, ns),
        in_specs=qk + [heads, heads, rowspec, stspec, heads, heads, pl.BlockSpec((1, HEAD), lambda b, g, s: (0, 0)), heads],
        out_specs=[heads, heads, heads, heads, rowspec, heads,
                   pl.BlockSpec((None, None, 8, HEAD), lambda b, g, s: (b, g, 0, 0))],
        out_shape=[hshape, hshape, hshape, hshape, jax.ShapeDtypeStruct((B, H, N, 1, CHUNK), F32),
                   jax.ShapeDtypeStruct((T, H * HEAD), BF16), jax.ShapeDtypeStruct((B, H // hp, 8, HEAD), F32)],
        scratch_shapes=[pltpu.VMEM((hp, HEAD, HEAD), F32)], semantics=("parallel", "parallel", "arbitrary"),
    )(*([qkvn] * (2 * hp)), u_hat, w, gcr5, states, o, p_gz, gnorm, d_y)


def adamw(w, g, m, v, name):
    shape = w.shape
    lead = (None,) * (w.ndim - 2)
    zeros = (0,) * (w.ndim - 2)
    R, C = shape[-2:]
    g2 = g.reshape(R, C)
    tr, tc = _tile(R, 128, 8), C
    if tr % 8 and R > 8:
        tr, tc = R, _tile(C, 128)

    def body(w_ref, g_ref, m_ref, v_ref, d_ref, nm_ref, nv_ref):
        gv = g_ref[...]
        nm = ADAM_B1 * m_ref[...] + (1.0 - ADAM_B1) * gv
        nv = ADAM_B2 * v_ref[...] + (1.0 - ADAM_B2) * (gv * gv)
        m_hat = nm / (1.0 - ADAM_B1 ** ADAM_STEP)
        v_hat = nv / (1.0 - ADAM_B2 ** ADAM_STEP)
        d_ref[...] = -ADAM_LR * (m_hat / (jnp.sqrt(v_hat) + ADAM_EPS) + ADAM_WD * w_ref[...])
        nm_ref[...] = nm
        nv_ref[...] = nv

    blk = pl.BlockSpec(lead + (tr, tc), lambda i, j: zeros + (i, j))
    gblk = pl.BlockSpec((tr, tc), lambda i, j: (i, j))
    sh = jax.ShapeDtypeStruct(shape, F32)
    return pl.pallas_call(
        body, name=name, grid=(R // tr, C // tc), in_specs=[blk, gblk, blk, blk], out_specs=[blk] * 3, out_shape=[sh] * 3,
        compiler_params=_params("parallel", "parallel"),
    )(w, g2, m, v)


def _place():
    x, y, c = lax.axis_index("x"), lax.axis_index("y"), lax.axis_index("c")
    chips = [(1 - x, y), (x, 1 - y), (1 - x, 1 - y)]
    return x, y, c, chips


_HBM = pl.BlockSpec(memory_space=pltpu.HBM)


def allgather_weights(packs):
    n = len(packs)

    def body(*refs):
        in_refs, out_refs, (send_sems, recv_sems) = refs[:n], refs[n:2 * n], refs[2 * n:]
        x, y, c, chips = _place()
        me_s = 2 * x + y
        me, sibling = (x, y, c), (x, y, 1 - c)
        shards = [2 * chip[0] + chip[1] for chip in chips]

        def copy(a, k, shard, half, to, src=None):
            dst = out_refs[a].at[shard, half]
            return pltpu.make_async_remote_copy(src_ref=dst if src is None else src, dst_ref=dst,
                                                send_sem=send_sems.at[6 * a + k], recv_sem=recv_sems.at[6 * a + k],
                                                device_id=to, device_id_type=MESH)

        first = [copy(a, j, me_s, c, (*chip, c), src=in_refs[a].at[c]) for a in range(n) for j, chip in enumerate(chips)]
        for cp in first:
            cp.start()
        passed = []
        for a in range(n):
            for j in range(3):
                copy(a, j, shards[j], c, me).wait_recv()
                passed.append(copy(a, 3 + j, shards[j], c, sibling))
                passed[-1].start()
        for a in range(n):
            for j in range(3):
                copy(a, 3 + j, shards[j], 1 - c, me).wait_recv()
        for cp in first + passed:
            cp.wait_send()

    return pl.pallas_call(
        body, name="allgather_weights", in_specs=[_HBM] * n, out_specs=[_HBM] * n,
        out_shape=[jax.ShapeDtypeStruct((N_CHIP,) + p.shape, p.dtype) for p in packs],
        scratch_shapes=[pltpu.SemaphoreType.DMA((6 * n,)), pltpu.SemaphoreType.DMA((6 * n,))],
    )(*packs)


class _Rider:
    def __init__(self, inputs, out_shapes, n_sems, sends, recvs, aliases=None):
        self.inputs, self.out_shapes, self.n_sems = list(inputs), list(out_shapes), n_sems
        self.sends, self.recvs, self.aliases = sends, recvs, aliases or {}

    def start(self, *refs):
        for cp in self.sends(*refs):
            cp.start()

    def wait(self, *refs):
        for cp in self.recvs(*refs):
            cp.wait_recv()
        for cp in self.sends(*refs):
            cp.wait_send()


def _remote(src, dst, send_sems, recv_sems, k, to):
    return pltpu.make_async_remote_copy(src_ref=src, dst_ref=dst, send_sem=send_sems.at[k], recv_sem=recv_sems.at[k],
                                        device_id=to, device_id_type=MESH)


def _run_alone(rider, name):
    ri = len(rider.inputs)

    def body(*refs):
        ins, outs, (send_sems, recv_sems) = refs[:ri], refs[ri:-2], refs[-2:]
        rider.start(ins, outs, send_sems, recv_sems)
        rider.wait(ins, outs, send_sems, recv_sems)

    return pl.pallas_call(
        body, name=name, in_specs=[_HBM] * ri, out_specs=[_HBM] * len(rider.out_shapes), out_shape=rider.out_shapes,
        scratch_shapes=[pltpu.SemaphoreType.DMA((rider.n_sems,))] * 2, input_output_aliases=rider.aliases,
    )(*rider.inputs)


def _hosted_call(body, rider, *, name, grid, in_specs, out_specs, out_shape, scratch_shapes, semantics):
    if rider is None:
        return pl.pallas_call(body, name=name, grid=grid, in_specs=in_specs, out_specs=out_specs, out_shape=out_shape,
                              scratch_shapes=scratch_shapes, compiler_params=_params(*semantics))
    n_in, n_out, n_scr = len(in_specs), len(out_specs), len(scratch_shapes)
    ri, ro = len(rider.inputs), len(rider.out_shapes)

    def hosted(*refs):
        parts, p = [], 0
        for cnt in (n_in, ri, n_out, ro, n_scr, 2):
            parts.append(refs[p:p + cnt])
            p += cnt
        ins, rins, outs, routs, scr, (send_sems, recv_sems) = parts
        first = functools.reduce(jnp.logical_and, [pl.program_id(a) == 0 for a in range(len(grid))])
        last = functools.reduce(jnp.logical_and, [pl.program_id(a) == grid[a] - 1 for a in range(len(grid))])

        @pl.when(first)
        def _():
            rider.start(rins, routs, send_sems, recv_sems)

        body(*ins, *outs, *scr)

        @pl.when(last)
        def _():
            rider.wait(rins, routs, send_sems, recv_sems)

    call = pl.pallas_call(
        hosted, name=name, grid=grid, in_specs=list(in_specs) + [_HBM] * ri, out_specs=list(out_specs) + [_HBM] * ro,
        out_shape=list(out_shape) + rider.out_shapes,
        scratch_shapes=list(scratch_shapes) + [pltpu.SemaphoreType.DMA((rider.n_sems,))] * 2,
        input_output_aliases={n_in + i: n_out + o for i, o in rider.aliases.items()},
        compiler_params=_params(*(("arbitrary",) * len(grid))))

    def run(*args):
        res = call(*args, *rider.inputs)
        return res[:n_out], res[n_out:]

    return run


def _ride_gather_ici(packs):
    n = len(packs)

    def sends(ins, outs, send_sems, recv_sems):
        x, y, c, chips = _place()
        return [_remote(ins[a].at[c], outs[a].at[2 * x + y, c], send_sems, recv_sems, 3 * a + j, (*chip, c))
                for a in range(n) for j, chip in enumerate(chips)]

    def recvs(ins, outs, send_sems, recv_sems):
        x, y, c, chips = _place()
        return [_remote(ins[a].at[c], outs[a].at[2 * chip[0] + chip[1], c], send_sems, recv_sems, 3 * a + j, (x, y, c))
                for a in range(n) for j, chip in enumerate(chips)]

    return _Rider(packs, [jax.ShapeDtypeStruct((N_CHIP,) + p.shape, p.dtype) for p in packs], 3 * n, sends, recvs)


def _ride_gather_d2d(gathered):
    n = len(gathered)

    def copies(landing_half, to):
        def build(ins, outs, send_sems, recv_sems):
            x, y, c, chips = _place()
            return [_remote(ins[a].at[2 * chip[0] + chip[1], c], outs[a].at[2 * chip[0] + chip[1], landing_half(c)],
                            send_sems, recv_sems, 3 * a + j, to(x, y, c))
                    for a in range(n) for j, chip in enumerate(chips)]
        return build

    return _Rider(gathered, [jax.ShapeDtypeStruct(g.shape, g.dtype) for g in gathered], 3 * n,
                  copies(lambda c: c, lambda x, y, c: (x, y, 1 - c)), copies(lambda c: 1 - c, lambda x, y, c: (x, y, c)),
                  aliases={a: a for a in range(n)})


def _ride_exchange(gs):
    n = len(gs)

    def copies(ins, outs, send_sems, recv_sems):
        x, y, c, _ = _place()
        return [_remote(ins[a].at[1 - c], outs[a], send_sems, recv_sems, a, (x, y, 1 - c)) for a in range(n)]

    return _Rider(gs, [jax.ShapeDtypeStruct(g.shape[1:], g.dtype) for g in gs], n, copies, copies)


def _ride_scatter(b16s):
    n = len(b16s)

    def sends(ins, outs, send_sems, recv_sems):
        x, y, c, chips = _place()
        return [_remote(ins[a].at[2 * chip[0] + chip[1]], outs[a].at[2 * x + y], send_sems, recv_sems, 3 * a + j, (*chip, c))
                for a in range(n) for j, chip in enumerate(chips)]

    def recvs(ins, outs, send_sems, recv_sems):
        x, y, c, chips = _place()
        return [_remote(ins[a].at[2 * x + y], outs[a].at[2 * chip[0] + chip[1]], send_sems, recv_sems, 3 * a + j, (x, y, c))
                for a in range(n) for j, chip in enumerate(chips)]

    return _Rider(b16s, [jax.ShapeDtypeStruct(b.shape, b.dtype) for b in b16s], 3 * n, sends, recvs)


def _slab_tile(r, cols):
    tr = _tile(r, 256, 16)
    if tr % 16 == 0:
        return tr, cols
    return r, _tile(cols, 128)


def add_halves(g, got, idx, name):
    _, ns, r, cols = g.shape
    tr, tc = _slab_tile(r, cols)

    def body(idx_ref, a_ref, b_ref, o32_ref, o16_ref):
        s = a_ref[...] + b_ref[...]
        o32_ref[...] = s
        o16_ref[...] = s.astype(BF16)

    blk = pl.BlockSpec((None, tr, tc), lambda s, i, j, idx_ref: (s, i, j))
    return pl.pallas_call(
        body, name=name,
        grid_spec=pltpu.PrefetchScalarGridSpec(
            num_scalar_prefetch=1, grid=(ns, r // tr, cols // tc),
            in_specs=[pl.BlockSpec((None, None, tr, tc), lambda s, i, j, idx_ref: (idx_ref[0], s, i, j)), blk],
            out_specs=[blk, blk]),
        out_shape=[jax.ShapeDtypeStruct((ns, r, cols), F32), jax.ShapeDtypeStruct((ns, r, cols), BF16)],
        compiler_params=_params("parallel", "parallel", "parallel"),
    )(idx, g, got)


def add_chips(a32, got16, idx, name):
    ns, r, cols = a32.shape
    tr, tc = _slab_tile(r, cols)

    def body(idx_ref, a_ref, r1_ref, r2_ref, r3_ref, o_ref):
        o_ref[...] = ((a_ref[...] + r1_ref[...].astype(F32)) + r2_ref[...].astype(F32)) + r3_ref[...].astype(F32)

    def slab(k):
        return pl.BlockSpec((None, tr, tc), lambda i, j, idx_ref: ((idx_ref[1] + k) % ns, i, j))

    return pl.pallas_call(
        body, name=name,
        grid_spec=pltpu.PrefetchScalarGridSpec(
            num_scalar_prefetch=1, grid=(r // tr, cols // tc), in_specs=[slab(0), slab(1), slab(2), slab(3)],
            out_specs=pl.BlockSpec((tr, tc), lambda i, j, idx_ref: (i, j))),
        out_shape=jax.ShapeDtypeStruct((r, cols), F32),
        compiler_params=_params("parallel", "parallel"),
    )(idx, a32, got16, got16, got16)


def share_halves(halves):
    n = len(halves)

    def body(*refs):
        in_refs, out_refs, (send_sems, recv_sems) = refs[:n], refs[n:2 * n], refs[2 * n:]
        x, y, c, _ = _place()
        cps = [pltpu.make_async_remote_copy(src_ref=in_refs[a], dst_ref=out_refs[a], send_sem=send_sems.at[a],
                                            recv_sem=recv_sems.at[a], device_id=(x, y, 1 - c), device_id_type=MESH)
               for a in range(n)]
        for cp in cps:
            cp.start()
        for cp in cps:
            cp.wait()

    return pl.pallas_call(
        body, name="share_halves", in_specs=[_HBM] * n, out_specs=[_HBM] * n,
        out_shape=[jax.ShapeDtypeStruct(h.shape, F32) for h in halves],
        scratch_shapes=[pltpu.SemaphoreType.DMA((n,)), pltpu.SemaphoreType.DMA((n,))],
    )(*halves)


def allreduce_small(v):
    R, _ = v.shape

    def body(in_ref, out_ref, slots, send_sems, recv_sems):
        x, y, c, _ = _place()
        me = 4 * x + 2 * y + c
        slots[me] = in_ref[...]
        cps = []
        for k in range(1, N_DEV):
            to = (x ^ (k >> 2), y ^ ((k >> 1) & 1), c ^ (k & 1))
            cps.append(pltpu.make_async_remote_copy(src_ref=in_ref, dst_ref=slots.at[me], send_sem=send_sems.at[k - 1],
                                                    recv_sem=recv_sems.at[k - 1], device_id=to, device_id_type=MESH))
        for cp in cps:
            cp.start()
        for k in range(1, N_DEV):
            frm = 4 * (x ^ (k >> 2)) + 2 * (y ^ ((k >> 1) & 1)) + (c ^ (k & 1))
            pltpu.make_async_remote_copy(src_ref=in_ref, dst_ref=slots.at[frm], send_sem=send_sems.at[k - 1],
                                         recv_sem=recv_sems.at[k - 1], device_id=(x, y, c), device_id_type=MESH).wait_recv()
        for cp in cps:
            cp.wait_send()
        acc = slots[0]
        for d in range(1, N_DEV):
            acc = acc + slots[d]
        out_ref[...] = acc

    vm = pl.BlockSpec(memory_space=pltpu.VMEM)
    return pl.pallas_call(
        body, name="allreduce_small", in_specs=[vm], out_specs=vm, out_shape=jax.ShapeDtypeStruct((R, ROW), F32),
        scratch_shapes=[pltpu.VMEM((N_DEV, R, ROW), F32), pltpu.SemaphoreType.DMA((N_DEV - 1,)),
                        pltpu.SemaphoreType.DMA((N_DEV - 1,))],
    )(v)


def _rows_of(n, unit=16):
    return -(-n // (unit * ROW)) * unit


def _pack_rows(items, total_rows, dtype, unit=16):
    parts = []
    used = 0
    for a in items:
        flat = a.reshape(-1)
        r = _rows_of(flat.shape[0], unit)
        flat = jnp.pad(flat, (0, r * ROW - flat.shape[0]))
        parts.append(flat.reshape(r, ROW))
        used += r
    if total_rows > used:
        parts.append(jnp.zeros((total_rows - used, ROW), dtype))
    return jnp.concatenate(parts, axis=0)


def _unpack_rows(buf, shapes, unit=16):
    lead = buf.shape[:-2]
    out = []
    off = 0
    for shp in shapes:
        n = math.prod(shp)
        r = _rows_of(n, unit)
        piece = buf[..., off:off + r, :].reshape(*lead, r * ROW)[..., :n].reshape(*lead, *shp)
        out.append(piece)
        off += r
    return out


def _interleave_heads(w, H):
    lead = w.shape[:-1]
    return w.reshape(*lead, 3, H, HEAD).swapaxes(-3, -2).reshape(*lead, 3 * H * HEAD)


def _deinterleave_heads(w, H):
    lead = w.shape[:-1]
    return w.reshape(*lead, H, 3, HEAD).swapaxes(-3, -2).reshape(*lead, 3 * H * HEAD)


def _interleave_head_rows(w, H):
    return w.reshape(3, H, HEAD, w.shape[-1]).swapaxes(0, 1).reshape(3 * H * HEAD, w.shape[-1])


def _deinterleave_head_rows(w, H):
    return w.reshape(H, 3, HEAD, w.shape[-1]).swapaxes(0, 1).reshape(3 * H * HEAD, w.shape[-1])


def kernel(x, norm_mix, w_in, fox_f_bias, gdn_conv_w, gdn_a_log, gdn_dt_bias, gdn_norm, w_branch_fox, w_branch_gdn, w_out, norm_ffn, w_up, ffn_conv_w, w_down, norm_final, loss_target, m_norm_mix, m_w_in, m_fox_f_bias, m_gdn_conv_w, m_gdn_a_log, m_gdn_dt_bias, m_gdn_norm, m_w_branch_fox, m_w_branch_gdn, m_w_out, m_norm_ffn, m_w_up, m_ffn_conv_w, m_w_down, m_norm_final, v_norm_mix, v_w_in, v_fox_f_bias, v_gdn_conv_w, v_gdn_a_log, v_gdn_dt_bias, v_gdn_norm, v_w_branch_fox, v_w_branch_gdn, v_w_out, v_norm_ffn, v_w_up, v_ffn_conv_w, v_w_down, v_norm_final):
    B, S, D = x.shape
    T = B * S
    H = D // HEAD
    N = S // CHUNK
    FF = w_down.shape[1] * N_CHIP
    d_in = 9 * D + 3 * H
    assert w_in.shape[2] * N_CHIP == d_in and 3 * H <= 128

    cidx = lax.axis_index("c").astype(jnp.int32)
    sidx = (2 * lax.axis_index("x") + lax.axis_index("y")).astype(jnp.int32)
    idx = jnp.stack([cidx, sidx])

    rowed = [w_branch_fox[0], w_branch_gdn[0], w_out[0], w_down[0]]
    convs = [gdn_conv_w[0], ffn_conv_w[0]]
    rowed_shapes = [a.shape for a in rowed]
    conv_shapes = [a.shape + (2,) for a in convs]
    pad_rows = lambda shapes: -(-sum(_rows_of(math.prod(s)) for s in shapes) // 256) * 128
    Rh, Rc = pad_rows(rowed_shapes), pad_rows(conv_shapes)
    halves = lambda a: a.reshape(2, a.shape[0] // 2, a.shape[1])
    c_in = w_in.shape[2]
    packs_a = [w_in[0].T.astype(BF16).reshape(c_in, 2, D // 2).transpose(1, 0, 2),
               halves(_pack_rows([lax.bitcast_convert_type(a, BF16) for a in convs], 2 * Rc, BF16))]
    packs_b = [halves(w_up[0].astype(BF16)), halves(_pack_rows([a.astype(BF16) for a in rowed], 2 * Rh, BF16))]
    own = lambda gs, ps: [lax.dynamic_update_slice(g, p[None], (sidx, 0, 0, 0)) for g, p in zip(gs, ps)]
    by_cols = lambda g: g.transpose(1, 2, 0, 3).reshape(2 * g.shape[2], N_CHIP * g.shape[3])
    cat_cols = lambda p: jnp.concatenate([p[i] for i in range(N_CHIP)], axis=-1)
    cat_rows = lambda p: p.reshape(-1, p.shape[-1])
    g_in, g_conv = own(allgather_weights(packs_a), packs_a)
    W_inT = g_in.transpose(0, 2, 1, 3).reshape(N_CHIP * c_in, D)
    conv_parts = _unpack_rows(g_conv.reshape(N_CHIP, 2 * Rc, ROW), conv_shapes)
    gconv = cat_cols(lax.bitcast_convert_type(conv_parts[0], F32))
    fconv = cat_cols(lax.bitcast_convert_type(conv_parts[1], F32))

    o1, o2 = 3 * D, 3 * D + H
    o3, o4, o5, o6 = o2 + 3 * D, o2 + 3 * D + H, o2 + 3 * D + 2 * H, o2 + 4 * D + 2 * H
    W_foxT = _interleave_head_rows(W_inT[:o1], H)
    W_gqkvT = _interleave_head_rows(W_inT[o2:o3], H)
    W_gzT = W_inT[o5:o6]
    W_gatesT = W_inT[o6:]
    W_smallT = jnp.concatenate([W_inT[o1:o2], W_inT[o3:o5], jnp.zeros((128 - 3 * H, D), BF16)], axis=0)
    gconv_i = _interleave_heads(gconv, H)
    fconv_g, fconv_v = fconv[:, :FF], fconv[:, FF:]
    prm = jnp.zeros((8, 128), F32)
    prm = prm.at[0, 0:H].set(fox_f_bias[0]).at[0, H:2 * H].set(gdn_dt_bias[0]).at[1, H:2 * H].set(gdn_a_log[0])

    x2 = x.reshape(T, D)
    tgt = loss_target.reshape(T, D)

    hn1 = rmsnorm_fwd(x2, norm_mix, "rmsnorm_mix")
    p_fox = matmul(hn1, W_foxT, "nt", "proj_fox", out_dtype=BF16)
    p_gqkv = matmul(hn1, W_gqkvT, "nt", "proj_gqkv")
    p_gz = matmul(hn1, W_gzT, "nt", "proj_gz")
    p_gates = matmul(hn1, W_gatesT, "nt", "proj_gates")
    p_small = matmul(hn1, W_smallT, "nt", "proj_small")

    sm = small_fwd(p_small, prm, B, S, H)
    heads = lambda a: a.reshape(B, S, H).transpose(0, 2, 1)
    c_bhs, gc_bhs, beta_bhs = heads(sm[:, 0:H]), heads(sm[:, H:2 * H]), heads(sm[:, 2 * H:3 * H])
    c_col, c_row = c_bhs[..., None], c_bhs[:, :, None, :]
    gcr5 = gc_bhs.reshape(B, H, N, 1, CHUNK)
    gcr_u = gc_bhs.reshape(B, H, N // PAIR, 1, PAIR * CHUNK)
    betar_u = beta_bhs.reshape(B, H, N // PAIR, 1, PAIR * CHUNK)

    (o_fox, o_fox16, lse), arriving = fox_fwd(p_fox, c_col, c_row, B, S, H, rider=_ride_gather_ici(packs_b))
    qkvn = gdn_prep_fwd(p_gqkv, gconv_i, B, S, H)
    (u_hat, w_t, t_inv), arrived = gdn_intra_fwd(qkvn, betar_u, gcr_u, B, S, H, rider=_ride_gather_d2d(arriving))
    g_up, g_rowed = own(arrived, packs_b)
    W_up = by_cols(g_up)
    W_up_g, W_up_v = W_up[:, :FF], W_up[:, FF:]
    W_bf, W_bg, W_out, W_down = (cat_rows(p) for p in _unpack_rows(g_rowed.reshape(N_CHIP, 2 * Rh, ROW), rowed_shapes))
    o_gdn, states, y_gdn = gdn_inter_fwd(qkvn, u_hat, w_t, gcr5, p_gz, gdn_norm, B, S, H)
    bf_ = matmul(o_fox16, W_bf, "nn", "branch_fox")
    bg_, y = matmul(y_gdn, W_bg, "nn", "branch_gdn", post=_post_merge(p_gates, bf_))
    h1, hn2 = matmul(y, W_out, "nn", "out_proj", add=x2, post=_post_rmsnorm(norm_ffn))
    up_g = matmul(hn2, W_up_g, "nn", "up_gate")
    up_v = matmul(hn2, W_up_v, "nn", "up_val")
    act = ffn_gate_fwd(up_g, up_v, fconv_g, fconv_v, B, S)
    dh2, dh2_16, loss_cols, d_norm_final = matmul(act, W_down, "nn", "down_proj", add=h1,
                                                  post=_post_loss(norm_final.reshape(1, D), tgt))
    loss = lax.psum(0.5 * jnp.sum(loss_cols) / D, ("x", "y", "c"))

    d_act = matmul(dh2_16, W_down, "nt", "d_act")
    dW_down = matmul(act, dh2_16, "tn", "dw_down")
    d_upg, d_upv, d_fconv_g, d_fconv_v = ffn_gate_bwd(up_g, up_v, fconv_g, fconv_v, d_act, B, S)
    d_hn2 = matmul(d_upg, W_up_g, "nt", "d_hn2_g")
    dh1, dh1_16, d_norm_ffn = matmul(d_upv, W_up_v, "nt", "d_hn2_v", add=d_hn2,
                                     post=_post_rmsnorm_bwd(h1, norm_ffn, dh2, True))
    dW_up = jnp.concatenate([matmul(hn2, d_upg, "tn", "dw_up_g"), matmul(hn2, d_upv, "tn", "dw_up_v")], axis=1)
    d_bf, d_bg, d_gates = matmul(dh1_16, W_out, "nt", "d_y", post=_post_merge_bwd(p_gates, bf_, bg_))
    dW_out = matmul(y, dh1_16, "tn", "dw_out")
    d_ofox = matmul(d_bf, W_bf, "nt", "d_ofox")
    dW_bf = matmul(o_fox16, d_bf, "tn", "dw_bf")
    d_ygdn = matmul(d_bg, W_bg, "nt", "d_ygdn")
    dW_bg = matmul(y_gdn, d_bg, "tn", "dw_bg")

    d_fconv = jnp.concatenate([d_fconv_g, d_fconv_v], axis=1)
    col_shard = lambda g, s: g[:, s * (g.shape[1] // N_CHIP):(s + 1) * (g.shape[1] // N_CHIP)]
    row_shard = lambda g, s: g[s * (g.shape[0] // N_CHIP):(s + 1) * (g.shape[0] // N_CHIP)]
    shard_items = lambda s: [row_shard(dW_bf, s), row_shard(dW_bg, s), row_shard(dW_out, s), row_shard(dW_down, s),
                             col_shard(d_fconv, s)]
    g_shapes = [a.shape for a in shard_items(0)]
    assert sum(_rows_of(math.prod(s)) for s in g_shapes) <= 2 * Rh
    to_slabs = lambda g: g.reshape(2, g.shape[0] // 2, N_CHIP, g.shape[1] // N_CHIP).transpose(0, 2, 1, 3)
    gpacks_b = [to_slabs(dW_up),
                jnp.stack([_pack_rows(shard_items(s), 2 * Rh, F32).reshape(2, Rh, ROW) for s in range(N_CHIP)], axis=1)]
    (d_pfox, d_ccol, d_crow), gots_b = fox_bwd(p_fox, c_col, c_row, o_fox, lse, d_ofox, B, S, H,
                                              rider=_ride_exchange(gpacks_b))
    sums_b = [add_halves(g, got, idx, "add_halves_b%d" % i) for i, (g, got) in enumerate(zip(gpacks_b, gots_b))]

    (dq_i, dk_i, d_uh, d_wt, dgcr_a, d_gz, d_gn_parts), got16_b = gdn_inter_bwd(
        qkvn, u_hat, w_t, gcr5, states, o_gdn, p_gz, gdn_norm, d_ygdn, B, S, H,
        rider=_ride_scatter([s16 for _, s16 in sums_b]))
    d_gdn_norm = jnp.sum(d_gn_parts[:, :, 0, :], axis=(0, 1))[None]
    mine_b = [add_chips(s32, g16, idx, "add_chips_b%d" % i) for i, ((s32, _), g16) in enumerate(zip(sums_b, got16_b))]
    d_qkvn, d_betar5, dgcr_b = gdn_intra_bwd(qkvn, betar_u, gcr_u, t_inv, d_uh, d_wt, dq_i, dk_i, B, S, H)
    d_pgqkv, d_gconv_i = gdn_prep_bwd(p_gqkv, gconv_i, d_qkvn, B, S, H)

    tokens = lambda a: a.reshape(B, H, S).transpose(0, 2, 1).reshape(T, H)
    d_gc = dgcr_a.reshape(B, H, S) + dgcr_b.reshape(B, H, S)
    d_sm = jnp.concatenate([tokens(d_ccol.reshape(B, H, S) + d_crow.reshape(B, H, S)), tokens(d_gc), tokens(d_betar5.reshape(B, H, S)),
                            jnp.zeros((T, 128 - 3 * H), F32)], axis=1)
    d_psmall, d_prm = small_bwd(p_small, prm, d_sm, B, S, H)

    d_hn1 = matmul(d_pfox, W_foxT, "nn", "d_hn1_fox")
    d_hn1 = matmul(d_pgqkv, W_gqkvT, "nn", "d_hn1_gqkv", add=d_hn1)
    d_hn1 = matmul(d_gz, W_gzT, "nn", "d_hn1_gz", add=d_hn1)
    d_hn1 = matmul(d_gates, W_gatesT, "nn", "d_hn1_gates", add=d_hn1)
    grad_x, d_norm_mix = matmul(d_psmall, W_smallT, "nn", "d_hn1_small", add=d_hn1,
                                post=_post_rmsnorm_bwd(x2, norm_mix, dh1, False))
    dW_foxT = matmul(d_pfox, hn1, "tn", "dw_fox")
    dW_gqkvT = matmul(d_pgqkv, hn1, "tn", "dw_gqkv")
    dW_gzT = matmul(d_gz, hn1, "tn", "dw_gz")
    dW_gatesT = matmul(d_gates, hn1, "tn", "dw_gates")
    dW_smallT = matmul(d_psmall, hn1, "tn", "dw_small")

    dW_inT = jnp.concatenate([_deinterleave_head_rows(dW_foxT, H), dW_smallT[0:H], _deinterleave_head_rows(dW_gqkvT, H),
                              dW_smallT[H:3 * H], dW_gzT, dW_gatesT], axis=0)
    d_gconv = _deinterleave_heads(d_gconv_i, H)

    gpack_a = [dW_inT.reshape(N_CHIP, c_in, 2, D // 2).transpose(2, 0, 1, 3)]
    gots_a = _run_alone(_ride_exchange(gpack_a), "exchange_halves")
    sums_a = [add_halves(gpack_a[0], gots_a[0], idx, "add_halves_a")]
    got16_a = _run_alone(_ride_scatter([sums_a[0][1]]), "scatter_chips")
    mine = [add_chips(sums_a[0][0], got16_a[0], idx, "add_chips_a")] + mine_b
    others = share_halves(mine)
    g_w_inT, g_up, g_rows = (jnp.concatenate([jnp.where(cidx == 0, h, o), jnp.where(cidx == 0, o, h)], axis=ax)
                             for h, o, ax in zip(mine, others, (1, 0, 0)))
    g_w_in = g_w_inT.T
    g_bf, g_bg, g_out, g_down, g_fconv = _unpack_rows(g_rows, g_shapes)

    small_items = [d_norm_mix, d_norm_ffn, d_norm_final, d_gdn_norm, d_prm, d_gconv]
    small_shapes = [a.shape for a in small_items]
    sv = allreduce_small(_pack_rows(small_items, 0, F32, unit=8))
    g_norm_mix, g_norm_ffn, g_norm_final, g_gdn_norm, g_prm, g_gconv_all = _unpack_rows(sv, small_shapes, unit=8)
    g_norm_final = g_norm_final.reshape(D)
    g_fbias, g_dtb, g_alog = g_prm[0:1, 0:H], g_prm[0:1, H:2 * H], g_prm[1:2, H:2 * H]
    g_gconv = lax.dynamic_slice_in_dim(g_gconv_all, sidx * (3 * D // N_CHIP), 3 * D // N_CHIP, axis=1)

    names = ["norm_mix", "w_in", "fox_f_bias", "gdn_conv_w", "gdn_a_log", "gdn_dt_bias", "gdn_norm", "w_branch_fox",
             "w_branch_gdn", "w_out", "norm_ffn", "w_up", "ffn_conv_w", "w_down", "norm_final"]
    ws = [norm_mix, w_in, fox_f_bias, gdn_conv_w, gdn_a_log, gdn_dt_bias, gdn_norm, w_branch_fox, w_branch_gdn, w_out,
          norm_ffn, w_up, ffn_conv_w, w_down, norm_final]
    ms = [m_norm_mix, m_w_in, m_fox_f_bias, m_gdn_conv_w, m_gdn_a_log, m_gdn_dt_bias, m_gdn_norm, m_w_branch_fox,
          m_w_branch_gdn, m_w_out, m_norm_ffn, m_w_up, m_ffn_conv_w, m_w_down, m_norm_final]
    vs = [v_norm_mix, v_w_in, v_fox_f_bias, v_gdn_conv_w, v_gdn_a_log, v_gdn_dt_bias, v_gdn_norm, v_w_branch_fox,
          v_w_branch_gdn, v_w_out, v_norm_ffn, v_w_up, v_ffn_conv_w, v_w_down, v_norm_final]
    gs = [g_norm_mix, g_w_in, g_fbias, g_gconv, g_alog, g_dtb, g_gdn_norm, g_bf, g_bg, g_out, g_norm_ffn, g_up,
          g_fconv, g_down, g_norm_final]
    gs = [g.reshape(w.shape) for g, w in zip(gs, ws)]
    deltas, new_ms, new_vs = [], [], []
    for nm, w, g, m, v in zip(names, ws, gs, ms, vs):
        if w.ndim == 1:
            d, a, b = adamw(w.reshape(1, -1), g.reshape(1, -1), m.reshape(1, -1), v.reshape(1, -1), "adamw_" + nm)
            d, a, b = d.reshape(w.shape), a.reshape(w.shape), b.reshape(w.shape)
        elif nm == "w_in":
            d, a, b = (r.T[None] for r in adamw(w[0].T, g_w_inT, m[0].T, v[0].T, "adamw_" + nm))
        else:
            d, a, b = adamw(w, g, m, v, "adamw_" + nm)
        deltas.append(d)
        new_ms.append(a)
        new_vs.append(b)

    return (loss, grad_x.reshape(B, S, D), *gs, *deltas, *new_ms, *new_vs)
```

```python
import functools
import math

import jax
import jax.numpy as jnp
from jax import lax
from jax.experimental import pallas as pl
from jax.experimental.pallas import tpu as pltpu

F32 = jnp.float32
BF16 = jnp.bfloat16
HEAD = 128
CHUNK = 64
GDN_CONV = 4
FFN_CONV = 3
EPS = 1e-6
NEG = -1e30
ROW = 1024
ATT_TILE = 512
MM_WEIGHT_TILE_BYTES = 8 << 20
N_CHIP = 4
N_DEV = 8
MESH = pl.DeviceIdType.MESH
HI = lax.Precision.HIGH
EXACT = lax.Precision.HIGHEST

ADAM_LR, ADAM_B1, ADAM_B2, ADAM_EPS, ADAM_WD, ADAM_STEP = 0.001, 0.9, 0.999, 1e-08, 0.01, 10


def _tile(n, cap, unit=128):
    best = None
    t = unit
    while t <= min(n, cap):
        if n % t == 0:
            best = t
        t += unit
    return best if best is not None else n


def _params(*sem):
    return pltpu.CompilerParams(dimension_semantics=sem)


_NN = (((1,), (0,)), ((), ()))
_NT = (((1,), (1,)), ((), ()))
_TN = (((0,), (0,)), ((), ()))


def _dg(a, b, dims, hi):
    if hi:
        return lax.dot_general(a, b, dims, precision=HI, preferred_element_type=F32)
    return lax.dot_general(a.astype(BF16), b.astype(BF16), dims, preferred_element_type=F32)


class _RawOps:
    @staticmethod
    def nn(a, b, hi=False):
        return _dg(a, b, _NN, hi)

    @staticmethod
    def nt(a, b, hi=False):
        return _dg(a, b, _NT, hi)

    @staticmethod
    def tn(a, b, hi=False):
        return _dg(a, b, _TN, hi)


def _make_diff_ops():
    def build(hi):
        @jax.custom_vjp
        def nn(a, b):
            return _dg(a, b, _NN, hi)

        nn.defvjp(lambda a, b: (_dg(a, b, _NN, hi), (a, b)),
                  lambda r, g: (_dg(g, r[1], _NT, hi), _dg(r[0], g, _TN, hi)))

        @jax.custom_vjp
        def nt(a, b):
            return _dg(a, b, _NT, hi)

        nt.defvjp(lambda a, b: (_dg(a, b, _NT, hi), (a, b)),
                  lambda r, g: (_dg(g, r[1], _NN, hi), _dg(g, r[0], _TN, hi)))

        @jax.custom_vjp
        def tn(a, b):
            return _dg(a, b, _TN, hi)

        tn.defvjp(lambda a, b: (_dg(a, b, _TN, hi), (a, b)),
                  lambda r, g: (_dg(r[1], g, _NT, hi), _dg(r[0], g, _NN, hi)))
        return nn, nt, tn

    lo, hi_ = build(False), build(True)

    class _DiffOps:
        @staticmethod
        def nn(a, b, hi=False):
            return (hi_ if hi else lo)[0](a, b)

        @staticmethod
        def nt(a, b, hi=False):
            return (hi_ if hi else lo)[1](a, b)

        @staticmethod
        def tn(a, b, hi=False):
            return (hi_ if hi else lo)[2](a, b)

    return _DiffOps


_DiffOps = _make_diff_ops()


def _sigmoid(x):
    return 1.0 / (1.0 + jnp.exp(-x))


def _mm_tile(n, pref):
    if n % pref == 0:
        return pref
    if n % 1408 == 0:
        return 1408
    return _tile(n, pref)


class _Post:
    def __init__(self, fn, row_ins=(), vec_ins=(), row_outs=(), acc_outs=(), keep_main=True):
        self.fn, self.keep_main = fn, keep_main
        self.row_ins = [r if isinstance(r, tuple) else (r, r.shape[1], 0) for r in row_ins]
        self.vec_ins, self.row_outs, self.acc_outs = list(vec_ins), list(row_outs), list(acc_outs)


def matmul(a, b, mode, name, add=None, out_dtype=F32, post=None):
    if mode == "nn":
        (M, K), (K2, N) = a.shape, b.shape
    elif mode == "nt":
        (M, K), (N, K2) = a.shape, b.shape
    else:
        (K, M), (K2, N) = a.shape, b.shape
    assert K == K2, (name, a.shape, b.shape)
    tn = _mm_tile(N, 1024)
    if mode == "tn":
        tm = M if M <= 1408 else _mm_tile(M, 1408)
        tk = _mm_tile(K, 2048)
    else:
        tk = K if K * tn * 2 <= MM_WEIGHT_TILE_BYTES else _mm_tile(K, 1024)
        tm = _mm_tile(M, 1024 if tk <= 2048 and post is None else 512)
    nk = K // tk
    assert post is None or (mode != "tn" and tn == N), name
    dims = {"nn": _NN, "nt": _NT, "tn": _TN}[mode]
    if mode == "tn":
        a_spec = pl.BlockSpec((tk, tm), lambda j, i, k: (k, i))
    else:
        a_spec = pl.BlockSpec((tm, tk), lambda j, i, k: (i, k))
    if mode == "nt":
        b_spec = pl.BlockSpec((tn, tk), lambda j, i, k: (j, k))
    else:
        b_spec = pl.BlockSpec((tk, tn), lambda j, i, k: (k, j))
    o_spec = pl.BlockSpec((tm, tn), lambda j, i, k: (i, j))
    has_add = add is not None
    keep_main = post is None or post.keep_main
    counts = [2 + has_add] + ([len(post.row_ins), len(post.vec_ins)] if post else [0, 0]) + [int(keep_main)]
    counts += ([len(post.row_outs), len(post.acc_outs)] if post else [0, 0]) + [int(nk > 1)]

    def body(*refs):
        parts, p = [], 0
        for cnt in counts:
            parts.append(refs[p:p + cnt])
            p += cnt
        core, row_ins, vec_ins, main, row_outs, acc_outs, acc = parts
        a_ref, b_ref = core[:2]
        prod = lax.dot_general(a_ref[...].astype(BF16), b_ref[...].astype(BF16), dims, preferred_element_type=F32)

        def finish(r):
            if has_add:
                r = r + core[2][...]
            if keep_main:
                main[0][...] = r.astype(out_dtype)
            if post is not None:
                @pl.when(pl.program_id(1) == 0)
                def _():
                    for ref in acc_outs:
                        ref[...] = jnp.zeros_like(ref)

                post.fn(r, row_ins, vec_ins, row_outs, acc_outs)

        if nk == 1:
            finish(prod)
            return
        acc_ref = acc[0]
        k = pl.program_id(2)

        @pl.when(k == 0)
        def _():
            acc_ref[...] = jnp.zeros_like(acc_ref)

        acc_ref[...] += prod

        @pl.when(k == nk - 1)
        def _():
            finish(acc_ref[...])

    in_specs = [a_spec, b_spec] + ([o_spec] if has_add else [])
    args = (a, b) + ((add,) if has_add else ())
    out_specs = [o_spec] if keep_main else []
    out_shape = [jax.ShapeDtypeStruct((M, N), out_dtype)] if keep_main else []
    if post is not None:
        in_specs += [pl.BlockSpec((tm, cols), lambda j, i, k, cb=cb: (i, cb)) for _, cols, cb in post.row_ins]
        in_specs += [pl.BlockSpec((1, v.shape[1]), lambda j, i, k: (0, 0)) for v in post.vec_ins]
        args += tuple(r for r, _, _ in post.row_ins) + tuple(post.vec_ins)
        out_specs += [pl.BlockSpec((tm, cols), lambda j, i, k: (i, 0)) for cols, _ in post.row_outs]
        out_specs += [pl.BlockSpec((1, cols), lambda j, i, k: (0, 0)) for cols in post.acc_outs]
        out_shape += [jax.ShapeDtypeStruct((M, cols), dt) for cols, dt in post.row_outs]
        out_shape += [jax.ShapeDtypeStruct((1, cols), F32) for cols in post.acc_outs]
    rows_sem = "arbitrary" if post is not None and post.acc_outs else "parallel"
    res = pl.pallas_call(
        body, name=name, grid=(N // tn, M // tm, nk), in_specs=in_specs, out_specs=out_specs, out_shape=out_shape,
        scratch_shapes=[pltpu.VMEM((tm, tn), F32)] if nk > 1 else [],
        compiler_params=_params("parallel", rows_sem, "arbitrary"),
    )(*args)
    return res[0] if post is None else res


def rmsnorm_fwd(x, g, name):
    T, D = x.shape
    tm = _tile(T, 512, 8)

    def body(x_ref, g_ref, o_ref):
        xv = x_ref[...]
        r = lax.rsqrt(jnp.mean(xv * xv, axis=-1, keepdims=True) + EPS)
        o_ref[...] = (xv * r * g_ref[...]).astype(BF16)

    return pl.pallas_call(
        body, name=name, grid=(T // tm,),
        in_specs=[pl.BlockSpec((tm, D), lambda i: (i, 0)), pl.BlockSpec((1, D), lambda i: (0, 0))],
        out_specs=pl.BlockSpec((tm, D), lambda i: (i, 0)),
        out_shape=jax.ShapeDtypeStruct((T, D), BF16),
        compiler_params=_params("parallel"),
    )(x, g)


def _post_rmsnorm(g):
    def fn(r, row_ins, vec_ins, row_outs, acc_outs):
        rs = lax.rsqrt(jnp.mean(r * r, axis=-1, keepdims=True) + EPS)
        row_outs[0][...] = (r * rs * vec_ins[0][...]).astype(BF16)

    return _Post(fn, vec_ins=[g], row_outs=[(g.shape[1], BF16)])


def _post_rmsnorm_bwd(x, g, dres, with_bf16):
    D = g.shape[1]

    def fn(dy, row_ins, vec_ins, row_outs, acc_outs):
        xv = row_ins[0][...]
        rs = lax.rsqrt(jnp.mean(xv * xv, axis=-1, keepdims=True) + EPS)
        xh = xv * rs
        acc_outs[0][...] += jnp.sum(dy * xh, axis=0, keepdims=True)
        dxh = dy * vec_ins[0][...]
        dx = row_ins[1][...] + rs * (dxh - xh * jnp.mean(dxh * xh, axis=-1, keepdims=True))
        row_outs[0][...] = dx
        if with_bf16:
            row_outs[1][...] = dx.astype(BF16)

    return _Post(fn, row_ins=[x, dres], vec_ins=[g], row_outs=[(D, F32)] + ([(D, BF16)] if with_bf16 else []),
                 acc_outs=[D], keep_main=False)


def _post_loss(g, target):
    D = g.shape[1]

    def fn(hv, row_ins, vec_ins, row_outs, acc_outs):
        rs = lax.rsqrt(jnp.mean(hv * hv, axis=-1, keepdims=True) + EPS)
        xh = hv * rs
        gv = vec_ins[0][...]
        err = xh * gv - row_ins[0][...]
        acc_outs[0][...] += jnp.sum(err * err, axis=0, keepdims=True)
        dy = err * (1.0 / D)
        acc_outs[1][...] += jnp.sum(dy * xh, axis=0, keepdims=True)
        dxh = dy * gv
        dh = rs * (dxh - xh * jnp.mean(dxh * xh, axis=-1, keepdims=True))
        row_outs[0][...] = dh
        row_outs[1][...] = dh.astype(BF16)

    return _Post(fn, row_ins=[target], vec_ins=[g], row_outs=[(D, F32), (D, BF16)], acc_outs=[D, D], keep_main=False)


def _shift_down(x, k):
    if k == 0:
        return x
    rows = lax.broadcasted_iota(jnp.int32, x.shape, 0)
    return jnp.where(rows >= k, pltpu.roll(x, k, 0), 0.0)


def _shift_up(x, k):
    if k == 0:
        return x
    s = x.shape[0]
    rows = lax.broadcasted_iota(jnp.int32, x.shape, 0)
    return jnp.where(rows < s - k, pltpu.roll(x, s - k, 0), 0.0)


def _conv_fwd(x, w_ref, kw):
    y = x * w_ref[kw - 1:kw, :]
    for i in range(kw - 1):
        y = y + _shift_down(x, kw - 1 - i) * w_ref[i:i + 1, :]
    return y


def _conv_bwd(x, dy, w_ref, kw):
    dx = dy * w_ref[kw - 1:kw, :]
    dws = []
    for i in range(kw - 1):
        dx = dx + _shift_up(dy, kw - 1 - i) * w_ref[i:i + 1, :]
        dws.append(jnp.sum(dy * _shift_down(x, kw - 1 - i), axis=0, keepdims=True))
    dws.append(jnp.sum(dy * x, axis=0, keepdims=True))
    return dx, dws


def ffn_gate_fwd(up_g, up_v, cw_g, cw_v, B, S):
    T, Fd = up_g.shape
    tc = _tile(Fd, 256)

    def body(g_ref, v_ref, wg_ref, wv_ref, o_ref):
        ug = _conv_fwd(g_ref[...], wg_ref, FFN_CONV)
        uv = _conv_fwd(v_ref[...], wv_ref, FFN_CONV)
        o_ref[...] = (ug * _sigmoid(ug) * uv).astype(BF16)

    blk = pl.BlockSpec((S, tc), lambda b, j: (b, j))
    wblk = pl.BlockSpec((FFN_CONV, tc), lambda b, j: (0, j))
    return pl.pallas_call(
        body, name="ffn_gate_fwd", grid=(B, Fd // tc), in_specs=[blk, blk, wblk, wblk], out_specs=blk,
        out_shape=jax.ShapeDtypeStruct((T, Fd), BF16), compiler_params=_params("parallel", "parallel"),
    )(up_g, up_v, cw_g, cw_v)


def ffn_gate_bwd(up_g, up_v, cw_g, cw_v, d_act, B, S):
    T, Fd = up_g.shape
    tc = _tile(Fd, 256)

    def body(g_ref, v_ref, wg_ref, wv_ref, da_ref, dg_ref, dv_ref, dwg_ref, dwv_ref):
        @pl.when(pl.program_id(1) == 0)
        def _():
            dwg_ref[...] = jnp.zeros_like(dwg_ref)
            dwv_ref[...] = jnp.zeros_like(dwv_ref)

        xg, xv = g_ref[...], v_ref[...]
        ug = _conv_fwd(xg, wg_ref, FFN_CONV)
        uv = _conv_fwd(xv, wv_ref, FFN_CONV)
        da = da_ref[...]
        sg = _sigmoid(ug)
        d_ug = da * uv * (sg + ug * sg * (1.0 - sg))
        d_uv = da * ug * sg
        dxg, dwg = _conv_bwd(xg, d_ug, wg_ref, FFN_CONV)
        dxv, dwv = _conv_bwd(xv, d_uv, wv_ref, FFN_CONV)
        dg_ref[...] = dxg.astype(BF16)
        dv_ref[...] = dxv.astype(BF16)
        for i in range(FFN_CONV):
            dwg_ref[i:i + 1, :] += dwg[i]
            dwv_ref[i:i + 1, :] += dwv[i]

    blk = pl.BlockSpec((S, tc), lambda j, b: (b, j))
    wblk = pl.BlockSpec((FFN_CONV, tc), lambda j, b: (0, j))
    return pl.pallas_call(
        body, name="ffn_gate_bwd", grid=(Fd // tc, B), in_specs=[blk, blk, wblk, wblk, blk],
        out_specs=[blk, blk, wblk, wblk],
        out_shape=[jax.ShapeDtypeStruct((T, Fd), BF16), jax.ShapeDtypeStruct((T, Fd), BF16),
                   jax.ShapeDtypeStruct((FFN_CONV, Fd), F32), jax.ShapeDtypeStruct((FFN_CONV, Fd), F32)],
        compiler_params=_params("parallel", "arbitrary"),
    )(up_g, up_v, cw_g, cw_v, d_act)


def _post_merge(p_gates, bf_):
    D = bf_.shape[1]

    def fn(bg, row_ins, vec_ins, row_outs, acc_outs):
        gf_ref, gg_ref, bf_ref = row_ins
        row_outs[0][...] = (_sigmoid(gf_ref[...]) * bf_ref[...] + _sigmoid(gg_ref[...]) * bg).astype(BF16)

    return _Post(fn, row_ins=[(p_gates, D, 0), (p_gates, D, 1), bf_], row_outs=[(D, BF16)])


def _post_merge_bwd(p_gates, bf_, bg_):
    D = bf_.shape[1]

    def fn(d, row_ins, vec_ins, row_outs, acc_outs):
        gf_ref, gg_ref, bf_ref, bg_ref = row_ins
        sf, sg = _sigmoid(gf_ref[...]), _sigmoid(gg_ref[...])
        row_outs[0][...] = (d * sf).astype(BF16)
        row_outs[1][...] = (d * sg).astype(BF16)
        row_outs[2][:, 0:D] = (d * bf_ref[...] * sf * (1.0 - sf)).astype(BF16)
        row_outs[2][:, D:2 * D] = (d * bg_ref[...] * sg * (1.0 - sg)).astype(BF16)

    return _Post(fn, row_ins=[(p_gates, D, 0), (p_gates, D, 1), bf_, bg_],
                 row_outs=[(D, BF16), (D, BF16), (2 * D, BF16)], keep_main=False)


def fox_fwd(p_fox, c_col, c_row, B, S, H, rider=None):
    T = B * S
    t = _tile(S, ATT_TILE)
    nq = S // t
    scale = HEAD ** -0.5

    def body(q_ref, k_ref, v_ref, cq_ref, cr_ref, o_ref, o16_ref, lse_ref):
        i = pl.program_id(2)
        q = q_ref[...]
        cq = cq_ref[...]
        row = lax.broadcasted_iota(jnp.int32, (t, t), 0)
        col = lax.broadcasted_iota(jnp.int32, (t, t), 1)

        def step(j, carry, diagonal):
            m, l, acc = carry
            off = pl.multiple_of(j * t, t)
            k = k_ref[pl.ds(off, t), :]
            v = v_ref[pl.ds(off, t), :]
            s = lax.dot_general(q, k, _NT, preferred_element_type=F32) * scale + (cq - cr_ref[:, pl.ds(off, t)])
            if diagonal:
                s = jnp.where(col <= row, s, NEG)
            m_new = jnp.maximum(m, jnp.max(s, axis=-1, keepdims=True))
            alpha = jnp.exp(m - m_new)
            p = jnp.exp(s - m_new)
            l = alpha * l + jnp.sum(p, axis=-1, keepdims=True)
            acc = alpha * acc + lax.dot_general(p.astype(BF16), v, _NN, preferred_element_type=F32)
            return m_new, l, acc

        m0 = jnp.full((t, 1), NEG, F32)
        below = lax.fori_loop(0, i, functools.partial(step, diagonal=False),
                              (m0, jnp.zeros((t, 1), F32), jnp.zeros((t, HEAD), F32)))
        m, l, acc = step(i, below, diagonal=True)
        o = acc / l
        o_ref[...] = o
        o16_ref[...] = o.astype(BF16)
        lse_ref[...] = m + jnp.log(l)

    return _hosted_call(
        body, rider, name="fox_fwd", grid=(B, H, nq),
        in_specs=[pl.BlockSpec((t, HEAD), lambda b, h, i: (b * nq + i, 3 * h)),
                  pl.BlockSpec((S, HEAD), lambda b, h, i: (b, 3 * h + 1)),
                  pl.BlockSpec((S, HEAD), lambda b, h, i: (b, 3 * h + 2)),
                  pl.BlockSpec((None, None, t, 1), lambda b, h, i: (b, h, i, 0)),
                  pl.BlockSpec((None, None, 1, S), lambda b, h, i: (b, h, 0, 0))],
        out_specs=[pl.BlockSpec((t, HEAD), lambda b, h, i: (b * nq + i, h)),
                   pl.BlockSpec((t, HEAD), lambda b, h, i: (b * nq + i, h)),
                   pl.BlockSpec((None, None, t, 1), lambda b, h, i: (b, h, i, 0))],
        out_shape=[jax.ShapeDtypeStruct((T, H * HEAD), F32), jax.ShapeDtypeStruct((T, H * HEAD), BF16),
                   jax.ShapeDtypeStruct((B, H, S, 1), F32)],
        scratch_shapes=[], semantics=("parallel", "parallel", "arbitrary"),
    )(p_fox, p_fox, p_fox, c_col, c_row)


def fox_bwd(p_fox, c_col, c_row, o, lse, do, B, S, H, rider=None):
    T = B * S
    t = _tile(S, ATT_TILE)
    n = S // t
    scale = HEAD ** -0.5

    def body(q_ref, k_ref, v_ref, cq_ref, cr_ref, o_ref, lse_ref, do_ref, dqkv_ref, dcq_ref, dcr_ref, dq_acc, delta_s):
        row = lax.broadcasted_iota(jnp.int32, (t, t), 0)
        col = lax.broadcasted_iota(jnp.int32, (t, t), 1)

        def prep(i, c):
            rows = pl.ds(pl.multiple_of(i * t, t), t)
            delta_s[rows, :] = jnp.sum(do_ref[rows, :] * o_ref[rows, :], axis=-1, keepdims=True)
            dq_acc[rows, :] = jnp.zeros((t, HEAD), F32)
            dcq_ref[rows, :] = jnp.zeros((t, 1), F32)
            return c

        lax.fori_loop(0, n, prep, 0)

        def kv_step(j, c):
            joff = pl.multiple_of(j * t, t)
            k = k_ref[pl.ds(joff, t), :]
            v = v_ref[pl.ds(joff, t), :]
            crj = cr_ref[:, pl.ds(joff, t)]

            def q_step(i, carry, diagonal):
                dk, dv, dc = carry
                rows = pl.ds(pl.multiple_of(i * t, t), t)
                q = q_ref[rows, :]
                dob = do_ref[rows, :].astype(BF16)
                s = lax.dot_general(q, k, _NT, preferred_element_type=F32) * scale + (cq_ref[rows, :] - crj)
                if diagonal:
                    s = jnp.where(col <= row, s, NEG)
                p = jnp.exp(s - lse_ref[rows, :])
                dp = lax.dot_general(dob, v, _NT, preferred_element_type=F32)
                ds = p * (dp - delta_s[rows, :])
                dsb = ds.astype(BF16)
                dv = dv + lax.dot_general(p.astype(BF16), dob, _TN, preferred_element_type=F32)
                dk = dk + lax.dot_general(dsb, q, _TN, preferred_element_type=F32)
                dq_acc[rows, :] += lax.dot_general(dsb, k, _NN, preferred_element_type=F32) * scale
                dc = dc + jnp.sum(ds, axis=0, keepdims=True)
                dcq_ref[rows, :] += jnp.sum(ds, axis=-1, keepdims=True)
                return dk, dv, dc

            z = jnp.zeros((t, HEAD), F32)
            on_diagonal = q_step(j, (z, z, jnp.zeros((1, t), F32)), diagonal=True)
            dk, dv, dc = lax.fori_loop(j + 1, n, functools.partial(q_step, diagonal=False), on_diagonal)
            dqkv_ref[pl.ds(joff, t), HEAD:2 * HEAD] = (dk * scale).astype(BF16)
            dqkv_ref[pl.ds(joff, t), 2 * HEAD:3 * HEAD] = dv.astype(BF16)
            dcr_ref[:, pl.ds(joff, t)] = -dc
            return c

        lax.fori_loop(0, n, kv_step, 0)
        dqkv_ref[:, 0:HEAD] = dq_acc[...].astype(BF16)

    col_spec = pl.BlockSpec((None, None, S, 1), lambda b, h: (b, h, 0, 0))
    row_spec = pl.BlockSpec((None, None, 1, S), lambda b, h: (b, h, 0, 0))
    head = pl.BlockSpec((S, HEAD), lambda b, h: (b, h))
    return _hosted_call(
        body, rider, name="fox_bwd", grid=(B, H),
        in_specs=[pl.BlockSpec((S, HEAD), lambda b, h: (b, 3 * h)),
                  pl.BlockSpec((S, HEAD), lambda b, h: (b, 3 * h + 1)),
                  pl.BlockSpec((S, HEAD), lambda b, h: (b, 3 * h + 2)),
                  col_spec, row_spec, head, col_spec, head],
        out_specs=[pl.BlockSpec((S, 3 * HEAD), lambda b, h: (b, h)), col_spec, row_spec],
        out_shape=[jax.ShapeDtypeStruct((T, 3 * H * HEAD), BF16), jax.ShapeDtypeStruct((B, H, S, 1), F32),
                   jax.ShapeDtypeStruct((B, H, 1, S), F32)],
        scratch_shapes=[pltpu.VMEM((S, HEAD), F32), pltpu.VMEM((S, 1), F32)], semantics=("parallel", "parallel"),
    )(p_fox, p_fox, p_fox, c_col, c_row, o, lse, do)


def _small_fn(x, b0, b1, H):
    S = x.shape[0]
    lane = lax.broadcasted_iota(jnp.int32, x.shape, 1)
    z = x + b0
    tail = jnp.log1p(jnp.exp(-jnp.abs(z)))
    softplus = jnp.maximum(z, 0.0) + tail
    logsig = -(jnp.maximum(-z, 0.0) + tail)
    g = -jnp.exp(b1) * softplus
    pre = jnp.where(lane < H, logsig, jnp.where(lane < 2 * H, g, 0.0))
    bl = _tile(S, 256, CHUNK)
    r = lax.broadcasted_iota(jnp.int32, (bl, bl), 0)
    c = lax.broadcasted_iota(jnp.int32, (bl, bl), 1)
    tri = (r >= c).astype(F32)
    tri_chunk = jnp.where((r >= c) & (jnp.right_shift(r, 6) == jnp.right_shift(c, 6)), 1.0, 0.0)
    carry = jnp.zeros((1, x.shape[1]), F32)
    parts = []
    for i in range(S // bl):
        blk = pre[i * bl:(i + 1) * bl, :]
        full = lax.dot_general(tri, blk, _NN, precision=EXACT, preferred_element_type=F32) + carry
        chunked = lax.dot_general(tri_chunk, blk, _NN, precision=EXACT, preferred_element_type=F32)
        parts.append(jnp.where(lane[:bl] < H, full, chunked))
        carry = carry + jnp.sum(blk, axis=0, keepdims=True)
    cum = parts[0] if len(parts) == 1 else jnp.concatenate(parts, axis=0)
    return jnp.where(lane < 2 * H, cum, jnp.where(lane < 3 * H, _sigmoid(x), 0.0))


def small_fwd(p_small, prm, B, S, H):
    T = B * S

    def body(x_ref, p_ref, o_ref):
        o_ref[...] = _small_fn(x_ref[...], p_ref[0:1, :], p_ref[1:2, :], H)

    blk = pl.BlockSpec((S, 128), lambda b: (b, 0))
    return pl.pallas_call(
        body, name="small_fwd", grid=(B,), in_specs=[blk, pl.BlockSpec((8, 128), lambda b: (0, 0))], out_specs=blk,
        out_shape=jax.ShapeDtypeStruct((T, 128), F32), compiler_params=_params("parallel"),
    )(p_small, prm)


def small_bwd(p_small, prm, d_out, B, S, H):
    T = B * S

    def body(x_ref, p_ref, d_ref, dx_ref, dp_ref):
        @pl.when(pl.program_id(0) == 0)
        def _():
            dp_ref[...] = jnp.zeros_like(dp_ref)

        _, vjp = jax.vjp(functools.partial(_small_fn, H=H), x_ref[...], p_ref[0:1, :], p_ref[1:2, :])
        dx, db0, db1 = vjp(d_ref[...])
        dx_ref[...] = dx.astype(BF16)
        dp_ref[0:1, :] += db0
        dp_ref[1:2, :] += db1

    blk = pl.BlockSpec((S, 128), lambda b: (b, 0))
    pblk = pl.BlockSpec((8, 128), lambda b: (0, 0))
    return pl.pallas_call(
        body, name="small_bwd", grid=(B,), in_specs=[blk, pblk, blk], out_specs=[blk, pblk],
        out_shape=[jax.ShapeDtypeStruct((T, 128), BF16), jax.ShapeDtypeStruct((8, 128), F32)],
        compiler_params=_params("arbitrary"),
    )(p_small, prm, d_out)


def gdn_prep_fwd(p_gqkv, cw, B, S, H):
    T = B * S

    def body(x_ref, w_ref, o_ref):
        y = _conv_fwd(x_ref[...], w_ref, GDN_CONV)
        a = y * _sigmoid(y)
        rs = lax.rsqrt(jnp.sum(a * a, axis=-1, keepdims=True) + EPS)
        is_qk = (pl.program_id(1) % 3) < 2
        o_ref[...] = a * jnp.where(is_qk, rs, 1.0)

    blk = pl.BlockSpec((S, HEAD), lambda b, n: (b, n))
    wblk = pl.BlockSpec((GDN_CONV, HEAD), lambda b, n: (0, n))
    return pl.pallas_call(
        body, name="gdn_prep_fwd", grid=(B, 3 * H), in_specs=[blk, wblk], out_specs=blk,
        out_shape=jax.ShapeDtypeStruct((T, 3 * H * HEAD), F32), compiler_params=_params("parallel", "parallel"),
    )(p_gqkv, cw)


def gdn_prep_bwd(p_gqkv, cw, d_out, B, S, H):
    T = B * S

    def body(x_ref, w_ref, d_ref, dx_ref, dw_ref):
        @pl.when(pl.program_id(1) == 0)
        def _():
            dw_ref[...] = jnp.zeros_like(dw_ref)

        x = x_ref[...]
        y = _conv_fwd(x, w_ref, GDN_CONV)
        sg = _sigmoid(y)
        a = y * sg
        rs = lax.rsqrt(jnp.sum(a * a, axis=-1, keepdims=True) + EPS)
        d = d_ref[...]
        out = a * rs
        da_qk = rs * (d - out * jnp.sum(d * out, axis=-1, keepdims=True))
        is_qk = (pl.program_id(0) % 3) < 2
        da = jnp.where(is_qk, da_qk, d)
        dy = da * (sg + y * sg * (1.0 - sg))
        dx, dws = _conv_bwd(x, dy, w_ref, GDN_CONV)
        dx_ref[...] = dx.astype(BF16)
        for i in range(GDN_CONV):
            dw_ref[i:i + 1, :] += dws[i]

    blk = pl.BlockSpec((S, HEAD), lambda n, b: (b, n))
    wblk = pl.BlockSpec((GDN_CONV, HEAD), lambda n, b: (0, n))
    return pl.pallas_call(
        body, name="gdn_prep_bwd", grid=(3 * H, B), in_specs=[blk, wblk, blk], out_specs=[blk, wblk],
        out_shape=[jax.ShapeDtypeStruct((T, 3 * H * HEAD), BF16), jax.ShapeDtypeStruct((GDN_CONV, 3 * H * HEAD), F32)],
        compiler_params=_params("parallel", "arbitrary"),
    )(p_gqkv, cw, d_out)


@jax.custom_vjp
def _given_inverse(a, t):
    return t


def _given_inverse_fwd(a, t):
    return t, t


def _given_inverse_bwd(t, g):
    x = _dg(t, g, _TN, True)
    return -_dg(x, t, _NT, True), jnp.zeros_like(t)


_given_inverse.defvjp(_given_inverse_fwd, _given_inverse_bwd)


def _to_col(row):
    n = row.shape[1]
    r = lax.broadcasted_iota(jnp.int32, (n, n), 0)
    c = lax.broadcasted_iota(jnp.int32, (n, n), 1)
    return jnp.sum(jnp.where(r == c, row, 0.0), axis=1, keepdims=True)


def _intra_fn(k, v, beta_r, gcr, ops, t_known=None):
    n = len(k)
    m = k[0].shape[0]
    r = lax.broadcasted_iota(jnp.int32, (m, m), 0)
    c = lax.broadcasted_iota(jnp.int32, (m, m), 1)
    below = (r > c) & (jnp.right_shift(r, 6) == jnp.right_shift(c, 6))
    beta = [_to_col(beta_r[i]) for i in range(n)]
    gcc = [_to_col(gcr[i]) for i in range(n)]
    decay = [jnp.exp(jnp.where(below, gcc[i] - gcr[i], NEG)) for i in range(n)]
    kb = [k[i] * beta[i] for i in range(n)]
    a = [ops.nt(kb[i], k[i]) * decay[i] for i in range(n)]
    if t_known is None:
        p = [-a[i] for i in range(n)]
        tm = [jnp.where(r == c, 1.0, 0.0) + p[i] for i in range(n)]
        for _ in range(5):
            p = [ops.nn(p[i], p[i], hi=True) for i in range(n)]
            tm = [tm[i] + ops.nn(tm[i], p[i], hi=True) for i in range(n)]
    else:
        tm = [_given_inverse(a[i], t_known[i]) for i in range(n)]
    u_hat = [ops.nn(tm[i], v[i] * beta[i], hi=True) for i in range(n)]
    w = [ops.nn(tm[i], kb[i] * jnp.exp(gcc[i]), hi=True) for i in range(n)]
    return tuple(u_hat), tuple(w), tuple(tm)


INTRA_NB = 32
PAIR = 1


def gdn_intra_fwd(qkvn, betar5, gcr5, B, S, H, rider=None):
    T = B * S
    UNIT = PAIR * CHUNK
    N = S // UNIT
    nb = min(INTRA_NB // PAIR, N)
    rows = nb * UNIT
    ns = N // nb

    def body(k_ref, v_ref, b_ref, gr_ref, uh_ref, w_ref, t_ref):
        sls = [slice(ci * UNIT, (ci + 1) * UNIT) for ci in range(nb)]
        u_hat, w, tm = _intra_fn(tuple(k_ref[sl, :] for sl in sls), tuple(v_ref[sl, :] for sl in sls),
                                 tuple(b_ref[ci] for ci in range(nb)), tuple(gr_ref[ci] for ci in range(nb)), _RawOps)
        for ci, sl in enumerate(sls):
            uh_ref[sl, :] = u_hat[ci]
            w_ref[sl, :] = w[ci]
            t_ref[ci] = tm[ci]

    rowspec = pl.BlockSpec((None, None, nb, 1, UNIT), lambda b, h, i: (b, h, i, 0, 0))
    sqspec = pl.BlockSpec((None, None, nb, UNIT, UNIT), lambda b, h, i: (b, h, i, 0, 0))
    out = pl.BlockSpec((rows, HEAD), lambda b, h, i: (b * ns + i, h))
    return _hosted_call(
        body, rider, name="gdn_intra_fwd", grid=(B, H, ns),
        in_specs=[pl.BlockSpec((rows, HEAD), lambda b, h, i: (b * ns + i, 3 * h + 1)),
                  pl.BlockSpec((rows, HEAD), lambda b, h, i: (b * ns + i, 3 * h + 2)),
                  rowspec, rowspec],
        out_specs=[out, out, sqspec],
        out_shape=[jax.ShapeDtypeStruct((T, H * HEAD), F32), jax.ShapeDtypeStruct((T, H * HEAD), F32),
                   jax.ShapeDtypeStruct((B, H, N, UNIT, UNIT), F32)],
        scratch_shapes=[], semantics=("parallel", "parallel", "parallel"),
    )(qkvn, qkvn, betar5, gcr5)


def gdn_intra_bwd(qkvn, betar5, gcr5, t_inv, d_uh, d_w, dq_in, dk_in, B, S, H):
    T = B * S
    UNIT = PAIR * CHUNK
    N = S // UNIT
    nb = min(INTRA_NB // PAIR, N)
    rows = nb * UNIT
    ns = N // nb

    def body(k_ref, v_ref, b_ref, gr_ref, t_ref, duh_ref, dw_ref, dq_ref, dk_ref, o_ref, db_ref, dgr_ref):
        sls = [slice(ci * UNIT, (ci + 1) * UNIT) for ci in range(nb)]
        chunks = range(nb)
        _, vjp = jax.vjp(
            functools.partial(_intra_fn, ops=_DiffOps, t_known=tuple(t_ref[ci] for ci in chunks)),
            tuple(k_ref[sl, :] for sl in sls), tuple(v_ref[sl, :] for sl in sls), tuple(b_ref[ci] for ci in chunks),
            tuple(gr_ref[ci] for ci in chunks))
        zero = jnp.zeros((UNIT, UNIT), F32)
        dk, dv, db, dgr = vjp((tuple(duh_ref[sl, :] for sl in sls), tuple(dw_ref[sl, :] for sl in sls),
                               tuple(zero for _ in chunks)))
        for ci, sl in enumerate(sls):
            o_ref[sl, 0:HEAD] = dq_ref[sl, :]
            o_ref[sl, HEAD:2 * HEAD] = dk[ci] + dk_ref[sl, :]
            o_ref[sl, 2 * HEAD:3 * HEAD] = dv[ci]
            db_ref[ci] = db[ci]
            dgr_ref[ci] = dgr[ci]

    rowspec = pl.BlockSpec((None, None, nb, 1, UNIT), lambda b, h, i: (b, h, i, 0, 0))
    sqspec = pl.BlockSpec((None, None, nb, UNIT, UNIT), lambda b, h, i: (b, h, i, 0, 0))
    head = pl.BlockSpec((rows, HEAD), lambda b, h, i: (b * ns + i, h))
    return pl.pallas_call(
        body, name="gdn_intra_bwd", grid=(B, H, ns),
        in_specs=[pl.BlockSpec((rows, HEAD), lambda b, h, i: (b * ns + i, 3 * h + 1)),
                  pl.BlockSpec((rows, HEAD), lambda b, h, i: (b * ns + i, 3 * h + 2)),
                  rowspec, rowspec, sqspec, head, head, head, head],
        out_specs=[pl.BlockSpec((rows, 3 * HEAD), lambda b, h, i: (b * ns + i, h)), rowspec, rowspec],
        out_shape=[jax.ShapeDtypeStruct((T, 3 * H * HEAD), F32),
                   jax.ShapeDtypeStruct((B, H, N, 1, UNIT), F32), jax.ShapeDtypeStruct((B, H, N, 1, UNIT), F32)],
        compiler_params=_params("parallel", "parallel", "parallel"),
    )(qkvn, qkvn, betar5, gcr5, t_inv, d_uh, d_w, dq_in, dk_in)


def _inter_fn(q, k, u_hat, w, gcr, state, ops):
    n = len(q)
    r = lax.broadcasted_iota(jnp.int32, (CHUNK, CHUNK), 0)
    c = lax.broadcasted_iota(jnp.int32, (CHUNK, CHUNK), 1)
    last = lax.broadcasted_iota(jnp.int32, (1, CHUNK), 1) == CHUNK - 1
    gcc = [_to_col(gcr[i]) for i in range(n)]
    gl = [jnp.sum(jnp.where(last, gcr[i], 0.0), axis=1, keepdims=True) for i in range(n)]
    decay = [jnp.exp(jnp.where(r >= c, gcc[i] - gcr[i], NEG)) for i in range(n)]
    qs = [q[i] * (HEAD ** -0.5) for i in range(n)]
    ws = [ops.nn(w[i], state[i]) for i in range(n)]
    qst = [ops.nn(qs[i] * jnp.exp(gcc[i]), state[i]) for i in range(n)]
    attn = [ops.nt(qs[i], k[i]) * decay[i] for i in range(n)]
    u = [u_hat[i] - ws[i] for i in range(n)]
    o = [qst[i] + ops.nn(attn[i], u[i]) for i in range(n)]
    kdu = [ops.tn(k[i] * jnp.exp(gl[i] - gcc[i]), u[i]) for i in range(n)]
    new_state = [state[i] * jnp.exp(gl[i]) + kdu[i] for i in range(n)]
    return tuple(o), tuple(new_state)


INTER_HEADS = 8
INTER_ROWS = 512
INTER_ROWS_BWD = 256


def _inter_heads(H):
    return INTER_HEADS if H % INTER_HEADS == 0 else (4 if H % 4 == 0 else 1)


def _inter_specs(ts, ns, hp, backward):
    at = (lambda s: ns - 1 - s) if backward else (lambda s: s)
    nc = ts // CHUNK
    qk = []
    for hh in range(hp):
        qk.append(pl.BlockSpec((ts, HEAD), lambda b, g, s, hh=hh: (b * ns + at(s), 3 * (hp * g + hh))))
        qk.append(pl.BlockSpec((ts, HEAD), lambda b, g, s, hh=hh: (b * ns + at(s), 3 * (hp * g + hh) + 1)))
    heads = pl.BlockSpec((ts, hp * HEAD), lambda b, g, s: (b * ns + at(s), g))
    rowspec = pl.BlockSpec((None, hp, nc, 1, CHUNK), lambda b, g, s: (b, g, at(s), 0, 0))
    stspec = pl.BlockSpec((None, hp, nc, HEAD, HEAD), lambda b, g, s: (b, g, at(s), 0, 0))
    return qk, heads, rowspec, stspec


def gdn_inter_fwd(qkvn, u_hat, w, gcr5, p_gz, gnorm, B, S, H):
    T = B * S
    N = S // CHUNK
    hp = _inter_heads(H)
    hs = range(hp)
    ts = _tile(S, INTER_ROWS, CHUNK)
    ns, nc = S // ts, ts // CHUNK

    def body(*refs):
        qk_refs, (uh_ref, w_ref, gr_ref, z_ref, gn_ref, o_ref, st_ref, y_ref, s_scr) = refs[:2 * hp], refs[2 * hp:]

        @pl.when(pl.program_id(2) == 0)
        def _():
            s_scr[...] = jnp.zeros_like(s_scr)

        gn = gn_ref[...]

        def step(n, c):
            rows = pl.ds(pl.multiple_of(n * CHUNK, CHUNK), CHUNK)
            st = tuple(s_scr[hh] for hh in hs)
            for hh in hs:
                st_ref[hh, n] = st[hh]
            o, new = _inter_fn(tuple(qk_refs[2 * hh][rows, :] for hh in hs), tuple(qk_refs[2 * hh + 1][rows, :] for hh in hs),
                               tuple(uh_ref[rows, hh * HEAD:(hh + 1) * HEAD] for hh in hs),
                               tuple(w_ref[rows, hh * HEAD:(hh + 1) * HEAD] for hh in hs),
                               tuple(gr_ref[hh, n] for hh in hs), st, _RawOps)
            for hh in hs:
                cols = slice(hh * HEAD, (hh + 1) * HEAD)
                o_ref[rows, cols] = o[hh]
                s_scr[hh] = new[hh]
                z = z_ref[rows, cols]
                r = lax.rsqrt(jnp.mean(o[hh] * o[hh], axis=-1, keepdims=True) + EPS)
                y_ref[rows, cols] = (o[hh] * r * gn * z * _sigmoid(z)).astype(BF16)
            return c

        lax.fori_loop(0, nc, step, 0)

    qk, heads, rowspec, stspec = _inter_specs(ts, ns, hp, backward=False)
    return pl.pallas_call(
        body, name="gdn_inter_fwd", grid=(B, H // hp, ns),
        in_specs=qk + [heads, heads, rowspec, heads, pl.BlockSpec((1, HEAD), lambda b, g, s: (0, 0))],
        out_specs=[heads, stspec, heads],
        out_shape=[jax.ShapeDtypeStruct((T, H * HEAD), F32), jax.ShapeDtypeStruct((B, H, N, HEAD, HEAD), F32),
                   jax.ShapeDtypeStruct((T, H * HEAD), BF16)],
        scratch_shapes=[pltpu.VMEM((hp, HEAD, HEAD), F32)],
        compiler_params=_params("parallel", "parallel", "arbitrary"),
    )(*([qkvn] * (2 * hp)), u_hat, w, gcr5, p_gz, gnorm)


def gdn_inter_bwd(qkvn, u_hat, w, gcr5, states, o, p_gz, gnorm, d_y, B, S, H, rider=None):
    T = B * S
    N = S // CHUNK
    hp = _inter_heads(H)
    hs = range(hp)
    ts = _tile(S, INTER_ROWS_BWD, CHUNK)
    ns, nc = S // ts, ts // CHUNK

    def body(*refs):
        qk_refs = refs[:2 * hp]
        (uh_ref, w_ref, gr_ref, st_ref, o_ref, z_ref, gn_ref, dy_ref,
         dq_ref, dk_ref, duh_ref, dw_ref, dgr_ref, dz_ref, dgn_ref, ds_scr) = refs[2 * hp:]

        @pl.when(pl.program_id(2) == 0)
        def _():
            ds_scr[...] = jnp.zeros_like(ds_scr)
            dgn_ref[...] = jnp.zeros_like(dgn_ref)

        cols = [slice(hh * HEAD, (hh + 1) * HEAD) for hh in hs]
        gn = gn_ref[...]

        def through_norm(rows, hh):
            ov, z, d = o_ref[rows, cols[hh]], z_ref[rows, cols[hh]], dy_ref[rows, cols[hh]]
            r = lax.rsqrt(jnp.mean(ov * ov, axis=-1, keepdims=True) + EPS)
            xh = ov * r
            sg = _sigmoid(z)
            d_n = d * (z * sg)
            dz_ref[rows, cols[hh]] = (d * xh * gn * (sg + z * sg * (1.0 - sg))).astype(BF16)
            dgn_ref[0:1, :] += jnp.sum(d_n * xh, axis=0, keepdims=True)
            dxh = d_n * gn
            return r * (dxh - xh * jnp.mean(dxh * xh, axis=-1, keepdims=True))

        def step(i, c):
            n = nc - 1 - i
            rows = pl.ds(pl.multiple_of(n * CHUNK, CHUNK), CHUNK)
            _, vjp = jax.vjp(functools.partial(_inter_fn, ops=_DiffOps),
                             tuple(qk_refs[2 * hh][rows, :] for hh in hs), tuple(qk_refs[2 * hh + 1][rows, :] for hh in hs),
                             tuple(uh_ref[rows, cols[hh]] for hh in hs), tuple(w_ref[rows, cols[hh]] for hh in hs),
                             tuple(gr_ref[hh, n] for hh in hs), tuple(st_ref[hh, n] for hh in hs))
            dq, dk, duh, dw, dgr, ds = vjp((tuple(through_norm(rows, hh) for hh in hs), tuple(ds_scr[hh] for hh in hs)))
            for hh in hs:
                dq_ref[rows, cols[hh]] = dq[hh]
                dk_ref[rows, cols[hh]] = dk[hh]
                duh_ref[rows, cols[hh]] = duh[hh]
                dw_ref[rows, cols[hh]] = dw[hh]
                dgr_ref[hh, n] = dgr[hh]
                ds_scr[hh] = ds[hh]
            return c

        lax.fori_loop(0, nc, step, 0)

    qk, heads, rowspec, stspec = _inter_specs(ts, ns, hp, backward=True)
    hshape = jax.ShapeDtypeStruct((T, H * HEAD), F32)
    return _hosted_call(
        body, rider, name="gdn_inter_bwd", grid=(B, H // hp, ns),
        in_specs=qk + [heads, heads, rowspec, stspec, heads, heads, pl.BlockSpec((1, HEAD), lambda b, g, s: (0, 0)), heads],
        out_specs=[heads, heads, heads, heads, rowspec, heads,
                   pl.BlockSpec((None, None, 8, HEAD), lambda b, g, s: (b, g, 0, 0))],
        out_shape=[hshape, hshape, hshape, hshape, jax.ShapeDtypeStruct((B, H, N, 1, CHUNK), F32),
                   jax.ShapeDtypeStruct((T, H * HEAD), BF16), jax.ShapeDtypeStruct((B, H // hp, 8, HEAD), F32)],
        scratch_shapes=[pltpu.VMEM((hp, HEAD, HEAD), F32)], semantics=("parallel", "parallel", "arbitrary"),
    )(*([qkvn] * (2 * hp)), u_hat, w, gcr5, states, o, p_gz, gnorm, d_y)


def adamw(w, g, m, v, name):
    shape = w.shape
    lead = (None,) * (w.ndim - 2)
    zeros = (0,) * (w.ndim - 2)
    R, C = shape[-2:]
    g2 = g.reshape(R, C)
    tr, tc = _tile(R, 128, 8), C
    if tr % 8 and R > 8:
        tr, tc = R, _tile(C, 128)

    def body(w_ref, g_ref, m_ref, v_ref, d_ref, nm_ref, nv_ref):
        gv = g_ref[...]
        nm = ADAM_B1 * m_ref[...] + (1.0 - ADAM_B1) * gv
        nv = ADAM_B2 * v_ref[...] + (1.0 - ADAM_B2) * (gv * gv)
        m_hat = nm / (1.0 - ADAM_B1 ** ADAM_STEP)
        v_hat = nv / (1.0 - ADAM_B2 ** ADAM_STEP)
        d_ref[...] = -ADAM_LR * (m_hat / (jnp.sqrt(v_hat) + ADAM_EPS) + ADAM_WD * w_ref[...])
        nm_ref[...] = nm
        nv_ref[...] = nv

    blk = pl.BlockSpec(lead + (tr, tc), lambda i, j: zeros + (i, j))
    gblk = pl.BlockSpec((tr, tc), lambda i, j: (i, j))
    sh = jax.ShapeDtypeStruct(shape, F32)
    return pl.pallas_call(
        body, name=name, grid=(R // tr, C // tc), in_specs=[blk, gblk, blk, blk], out_specs=[blk] * 3, out_shape=[sh] * 3,
        compiler_params=_params("parallel", "parallel"),
    )(w, g2, m, v)


def _place():
    x, y, c = lax.axis_index("x"), lax.axis_index("y"), lax.axis_index("c")
    chips = [(1 - x, y), (x, 1 - y), (1 - x, 1 - y)]
    return x, y, c, chips


_HBM = pl.BlockSpec(memory_space=pltpu.HBM)


def allgather_weights(packs):
    n = len(packs)

    def body(*refs):
        in_refs, out_refs, (send_sems, recv_sems) = refs[:n], refs[n:2 * n], refs[2 * n:]
        x, y, c, chips = _place()
        me_s = 2 * x + y
        me, sibling = (x, y, c), (x, y, 1 - c)
        shards = [2 * chip[0] + chip[1] for chip in chips]

        def copy(a, k, shard, half, to, src=None):
            dst = out_refs[a].at[shard, half]
            return pltpu.make_async_remote_copy(src_ref=dst if src is None else src, dst_ref=dst,
                                                send_sem=send_sems.at[6 * a + k], recv_sem=recv_sems.at[6 * a + k],
                                                device_id=to, device_id_type=MESH)

        first = [copy(a, j, me_s, c, (*chip, c), src=in_refs[a].at[c]) for a in range(n) for j, chip in enumerate(chips)]
        for cp in first:
            cp.start()
        passed = []
        for a in range(n):
            for j in range(3):
                copy(a, j, shards[j], c, me).wait_recv()
                passed.append(copy(a, 3 + j, shards[j], c, sibling))
                passed[-1].start()
        for a in range(n):
            for j in range(3):
                copy(a, 3 + j, shards[j], 1 - c, me).wait_recv()
        for cp in first + passed:
            cp.wait_send()

    return pl.pallas_call(
        body, name="allgather_weights", in_specs=[_HBM] * n, out_specs=[_HBM] * n,
        out_shape=[jax.ShapeDtypeStruct((N_CHIP,) + p.shape, p.dtype) for p in packs],
        scratch_shapes=[pltpu.SemaphoreType.DMA((6 * n,)), pltpu.SemaphoreType.DMA((6 * n,))],
    )(*packs)


class _Rider:
    def __init__(self, inputs, out_shapes, n_sems, sends, recvs, aliases=None):
        self.inputs, self.out_shapes, self.n_sems = list(inputs), list(out_shapes), n_sems
        self.sends, self.recvs, self.aliases = sends, recvs, aliases or {}

    def start(self, *refs):
        for cp in self.sends(*refs):
            cp.start()

    def wait(self, *refs):
        for cp in self.recvs(*refs):
            cp.wait_recv()
        for cp in self.sends(*refs):
            cp.wait_send()


def _remote(src, dst, send_sems, recv_sems, k, to):
    return pltpu.make_async_remote_copy(src_ref=src, dst_ref=dst, send_sem=send_sems.at[k], recv_sem=recv_sems.at[k],
                                        device_id=to, device_id_type=MESH)


def _run_alone(rider, name):
    ri = len(rider.inputs)

    def body(*refs):
        ins, outs, (send_sems, recv_sems) = refs[:ri], refs[ri:-2], refs[-2:]
        rider.start(ins, outs, send_sems, recv_sems)
        rider.wait(ins, outs, send_sems, recv_sems)

    return pl.pallas_call(
        body, name=name, in_specs=[_HBM] * ri, out_specs=[_HBM] * len(rider.out_shapes), out_shape=rider.out_shapes,
        scratch_shapes=[pltpu.SemaphoreType.DMA((rider.n_sems,))] * 2, input_output_aliases=rider.aliases,
    )(*rider.inputs)


def _hosted_call(body, rider, *, name, grid, in_specs, out_specs, out_shape, scratch_shapes, semantics):
    if rider is None:
        return pl.pallas_call(body, name=name, grid=grid, in_specs=in_specs, out_specs=out_specs, out_shape=out_shape,
                              scratch_shapes=scratch_shapes, compiler_params=_params(*semantics))
    n_in, n_out, n_scr = len(in_specs), len(out_specs), len(scratch_shapes)
    ri, ro = len(rider.inputs), len(rider.out_shapes)

    def hosted(*refs):
        parts, p = [], 0
        for cnt in (n_in, ri, n_out, ro, n_scr, 2):
            parts.append(refs[p:p + cnt])
            p += cnt
        ins, rins, outs, routs, scr, (send_sems, recv_sems) = parts
        first = functools.reduce(jnp.logical_and, [pl.program_id(a) == 0 for a in range(len(grid))])
        last = functools.reduce(jnp.logical_and, [pl.program_id(a) == grid[a] - 1 for a in range(len(grid))])

        @pl.when(first)
        def _():
            rider.start(rins, routs, send_sems, recv_sems)

        body(*ins, *outs, *scr)

        @pl.when(last)
        def _():
            rider.wait(rins, routs, send_sems, recv_sems)

    call = pl.pallas_call(
        hosted, name=name, grid=grid, in_specs=list(in_specs) + [_HBM] * ri, out_specs=list(out_specs) + [_HBM] * ro,
        out_shape=list(out_shape) + rider.out_shapes,
        scratch_shapes=list(scratch_shapes) + [pltpu.SemaphoreType.DMA((rider.n_sems,))] * 2,
        input_output_aliases={n_in + i: n_out + o for i, o in rider.aliases.items()},
        compiler_params=_params(*(("arbitrary",) * len(grid))))

    def run(*args):
        res = call(*args, *rider.inputs)
        return res[:n_out], res[n_out:]

    return run


def _ride_gather_ici(packs):
    n = len(packs)

    def sends(ins, outs, send_sems, recv_sems):
        x, y, c, chips = _place()
        return [_remote(ins[a].at[c], outs[a].at[2 * x + y, c], send_sems, recv_sems, 3 * a + j, (*chip, c))
                for a in range(n) for j, chip in enumerate(chips)]

    def recvs(ins, outs, send_sems, recv_sems):
        x, y, c, chips = _place()
        return [_remote(ins[a].at[c], outs[a].at[2 * chip[0] + chip[1], c], send_sems, recv_sems, 3 * a + j, (x, y, c))
                for a in range(n) for j, chip in enumerate(chips)]

    return _Rider(packs, [jax.ShapeDtypeStruct((N_CHIP,) + p.shape, p.dtype) for p in packs], 3 * n, sends, recvs)


def _ride_gather_d2d(gathered):
    n = len(gathered)

    def copies(landing_half, to):
        def build(ins, outs, send_sems, recv_sems):
            x, y, c, chips = _place()
            return [_remote(ins[a].at[2 * chip[0] + chip[1], c], outs[a].at[2 * chip[0] + chip[1], landing_half(c)],
                            send_sems, recv_sems, 3 * a + j, to(x, y, c))
                    for a in range(n) for j, chip in enumerate(chips)]
        return build

    return _Rider(gathered, [jax.ShapeDtypeStruct(g.shape, g.dtype) for g in gathered], 3 * n,
                  copies(lambda c: c, lambda x, y, c: (x, y, 1 - c)), copies(lambda c: 1 - c, lambda x, y, c: (x, y, c)),
                  aliases={a: a for a in range(n)})


def _ride_exchange(gs):
    n = len(gs)

    def copies(ins, outs, send_sems, recv_sems):
        x, y, c, _ = _place()
        return [_remote(ins[a].at[1 - c], outs[a], send_sems, recv_sems, a, (x, y, 1 - c)) for a in range(n)]

    return _Rider(gs, [jax.ShapeDtypeStruct(g.shape[1:], g.dtype) for g in gs], n, copies, copies)


def _ride_scatter(b16s):
    n = len(b16s)

    def sends(ins, outs, send_sems, recv_sems):
        x, y, c, chips = _place()
        return [_remote(ins[a].at[2 * chip[0] + chip[1]], outs[a].at[2 * x + y], send_sems, recv_sems, 3 * a + j, (*chip, c))
                for a in range(n) for j, chip in enumerate(chips)]

    def recvs(ins, outs, send_sems, recv_sems):
        x, y, c, chips = _place()
        return [_remote(ins[a].at[2 * x + y], outs[a].at[2 * chip[0] + chip[1]], send_sems, recv_sems, 3 * a + j, (x, y, c))
                for a in range(n) for j, chip in enumerate(chips)]

    return _Rider(b16s, [jax.ShapeDtypeStruct(b.shape, b.dtype) for b in b16s], 3 * n, sends, recvs)


def _slab_tile(r, cols):
    tr = _tile(r, 256, 16)
    if tr % 16 == 0:
        return tr, cols
    return r, _tile(cols, 128)


def add_halves(g, got, idx, name):
    _, ns, r, cols = g.shape
    tr, tc = _slab_tile(r, cols)

    def body(idx_ref, a_ref, b_ref, o32_ref, o16_ref):
        s = a_ref[...] + b_ref[...]
        o32_ref[...] = s
        o16_ref[...] = s.astype(BF16)

    blk = pl.BlockSpec((None, tr, tc), lambda s, i, j, idx_ref: (s, i, j))
    return pl.pallas_call(
        body, name=name,
        grid_spec=pltpu.PrefetchScalarGridSpec(
            num_scalar_prefetch=1, grid=(ns, r // tr, cols // tc),
            in_specs=[pl.BlockSpec((None, None, tr, tc), lambda s, i, j, idx_ref: (idx_ref[0], s, i, j)), blk],
            out_specs=[blk, blk]),
        out_shape=[jax.ShapeDtypeStruct((ns, r, cols), F32), jax.ShapeDtypeStruct((ns, r, cols), BF16)],
        compiler_params=_params("parallel", "parallel", "parallel"),
    )(idx, g, got)


def add_chips(a32, got16, idx, name):
    ns, r, cols = a32.shape
    tr, tc = _slab_tile(r, cols)

    def body(idx_ref, a_ref, r1_ref, r2_ref, r3_ref, o_ref):
        o_ref[...] = ((a_ref[...] + r1_ref[...].astype(F32)) + r2_ref[...].astype(F32)) + r3_ref[...].astype(F32)

    def slab(k):
        return pl.BlockSpec((None, tr, tc), lambda i, j, idx_ref: ((idx_ref[1] + k) % ns, i, j))

    return pl.pallas_call(
        body, name=name,
        grid_spec=pltpu.PrefetchScalarGridSpec(
            num_scalar_prefetch=1, grid=(r // tr, cols // tc), in_specs=[slab(0), slab(1), slab(2), slab(3)],
            out_specs=pl.BlockSpec((tr, tc), lambda i, j, idx_ref: (i, j))),
        out_shape=jax.ShapeDtypeStruct((r, cols), F32),
        compiler_params=_params("parallel", "parallel"),
    )(idx, a32, got16, got16, got16)


def share_halves(halves):
    n = len(halves)

    def body(*refs):
        in_refs, out_refs, (send_sems, recv_sems) = refs[:n], refs[n:2 * n], refs[2 * n:]
        x, y, c, _ = _place()
        cps = [pltpu.make_async_remote_copy(src_ref=in_refs[a], dst_ref=out_refs[a], send_sem=send_sems.at[a],
                                            recv_sem=recv_sems.at[a], device_id=(x, y, 1 - c), device_id_type=MESH)
               for a in range(n)]
        for cp in cps:
            cp.start()
        for cp in cps:
            cp.wait()

    return pl.pallas_call(
        body, name="share_halves", in_specs=[_HBM] * n, out_specs=[_HBM] * n,
        out_shape=[jax.ShapeDtypeStruct(h.shape, F32) for h in halves],
        scratch_shapes=[pltpu.SemaphoreType.DMA((n,)), pltpu.SemaphoreType.DMA((n,))],
    )(*halves)


def allreduce_small(v):
    R, _ = v.shape

    def body(in_ref, out_ref, slots, send_sems, recv_sems):
        x, y, c, _ = _place()
        me = 4 * x + 2 * y + c
        slots[me] = in_ref[...]
        cps = []
        for k in range(1, N_DEV):
            to = (x ^ (k >> 2), y ^ ((k >> 1) & 1), c ^ (k & 1))
            cps.append(pltpu.make_async_remote_copy(src_ref=in_ref, dst_ref=slots.at[me], send_sem=send_sems.at[k - 1],
                                                    recv_sem=recv_sems.at[k - 1], device_id=to, device_id_type=MESH))
        for cp in cps:
            cp.start()
        for k in range(1, N_DEV):
            frm = 4 * (x ^ (k >> 2)) + 2 * (y ^ ((k >> 1) & 1)) + (c ^ (k & 1))
            pltpu.make_async_remote_copy(src_ref=in_ref, dst_ref=slots.at[frm], send_sem=send_sems.at[k - 1],
                                         recv_sem=recv_sems.at[k - 1], device_id=(x, y, c), device_id_type=MESH).wait_recv()
        for cp in cps:
            cp.wait_send()
        acc = slots[0]
        for d in range(1, N_DEV):
            acc = acc + slots[d]
        out_ref[...] = acc

    vm = pl.BlockSpec(memory_space=pltpu.VMEM)
    return pl.pallas_call(
        body, name="allreduce_small", in_specs=[vm], out_specs=vm, out_shape=jax.ShapeDtypeStruct((R, ROW), F32),
        scratch_shapes=[pltpu.VMEM((N_DEV, R, ROW), F32), pltpu.SemaphoreType.DMA((N_DEV - 1,)),
                        pltpu.SemaphoreType.DMA((N_DEV - 1,))],
    )(v)


def _rows_of(n, unit=16):
    return -(-n // (unit * ROW)) * unit


def _pack_rows(items, total_rows, dtype, unit=16):
    parts = []
    used = 0
    for a in items:
        flat = a.reshape(-1)
        r = _rows_of(flat.shape[0], unit)
        flat = jnp.pad(flat, (0, r * ROW - flat.shape[0]))
        parts.append(flat.reshape(r, ROW))
        used += r
    if total_rows > used:
        parts.append(jnp.zeros((total_rows - used, ROW), dtype))
    return jnp.concatenate(parts, axis=0)


def _unpack_rows(buf, shapes, unit=16):
    lead = buf.shape[:-2]
    out = []
    off = 0
    for shp in shapes:
        n = math.prod(shp)
        r = _rows_of(n, unit)
        piece = buf[..., off:off + r, :].reshape(*lead, r * ROW)[..., :n].reshape(*lead, *shp)
        out.append(piece)
        off += r
    return out


def _interleave_heads(w, H):
    lead = w.shape[:-1]
    return w.reshape(*lead, 3, H, HEAD).swapaxes(-3, -2).reshape(*lead, 3 * H * HEAD)


def _deinterleave_heads(w, H):
    lead = w.shape[:-1]
    return w.reshape(*lead, H, 3, HEAD).swapaxes(-3, -2).reshape(*lead, 3 * H * HEAD)


def _interleave_head_rows(w, H):
    return w.reshape(3, H, HEAD, w.shape[-1]).swapaxes(0, 1).reshape(3 * H * HEAD, w.shape[-1])


def _deinterleave_head_rows(w, H):
    return w.reshape(H, 3, HEAD, w.shape[-1]).swapaxes(0, 1).reshape(3 * H * HEAD, w.shape[-1])


def kernel(x, norm_mix, w_in, fox_f_bias, gdn_conv_w, gdn_a_log, gdn_dt_bias, gdn_norm, w_branch_fox, w_branch_gdn, w_out, norm_ffn, w_up, ffn_conv_w, w_down, norm_final, loss_target, m_norm_mix, m_w_in, m_fox_f_bias, m_gdn_conv_w, m_gdn_a_log, m_gdn_dt_bias, m_gdn_norm, m_w_branch_fox, m_w_branch_gdn, m_w_out, m_norm_ffn, m_w_up, m_ffn_conv_w, m_w_down, m_norm_final, v_norm_mix, v_w_in, v_fox_f_bias, v_gdn_conv_w, v_gdn_a_log, v_gdn_dt_bias, v_gdn_norm, v_w_branch_fox, v_w_branch_gdn, v_w_out, v_norm_ffn, v_w_up, v_ffn_conv_w, v_w_down, v_norm_final):
    B, S, D = x.shape
    T = B * S
    H = D // HEAD
    N = S // CHUNK
    FF = w_down.shape[1] * N_CHIP
    d_in = 9 * D + 3 * H
    assert w_in.shape[2] * N_CHIP == d_in and 3 * H <= 128

    cidx = lax.axis_index("c").astype(jnp.int32)
    sidx = (2 * lax.axis_index("x") + lax.axis_index("y")).astype(jnp.int32)
    idx = jnp.stack([cidx, sidx])

    rowed = [w_branch_fox[0], w_branch_gdn[0], w_out[0], w_down[0]]
    convs = [gdn_conv_w[0], ffn_conv_w[0]]
    rowed_shapes = [a.shape for a in rowed]
    conv_shapes = [a.shape + (2,) for a in convs]
    pad_rows = lambda shapes: -(-sum(_rows_of(math.prod(s)) for s in shapes) // 256) * 128
    Rh, Rc = pad_rows(rowed_shapes), pad_rows(conv_shapes)
    halves = lambda a: a.reshape(2, a.shape[0] // 2, a.shape[1])
    c_in = w_in.shape[2]
    packs_a = [w_in[0].T.astype(BF16).reshape(c_in, 2, D // 2).transpose(1, 0, 2),
               halves(_pack_rows([lax.bitcast_convert_type(a, BF16) for a in convs], 2 * Rc, BF16))]
    packs_b = [halves(w_up[0].astype(BF16)), halves(_pack_rows([a.astype(BF16) for a in rowed], 2 * Rh, BF16))]
    own = lambda gs, ps: [lax.dynamic_update_slice(g, p[None], (sidx, 0, 0, 0)) for g, p in zip(gs, ps)]
    by_cols = lambda g: g.transpose(1, 2, 0, 3).reshape(2 * g.shape[2], N_CHIP * g.shape[3])
    cat_cols = lambda p: jnp.concatenate([p[i] for i in range(N_CHIP)], axis=-1)
    cat_rows = lambda p: p.reshape(-1, p.shape[-1])
    g_in, g_conv = own(allgather_weights(packs_a), packs_a)
    W_inT = g_in.transpose(0, 2, 1, 3).reshape(N_CHIP * c_in, D)
    conv_parts = _unpack_rows(g_conv.reshape(N_CHIP, 2 * Rc, ROW), conv_shapes)
    gconv = cat_cols(lax.bitcast_convert_type(conv_parts[0], F32))
    fconv = cat_cols(lax.bitcast_convert_type(conv_parts[1], F32))

    o1, o2 = 3 * D, 3 * D + H
    o3, o4, o5, o6 = o2 + 3 * D, o2 + 3 * D + H, o2 + 3 * D + 2 * H, o2 + 4 * D + 2 * H
    W_foxT = _interleave_head_rows(W_inT[:o1], H)
    W_gqkvT = _interleave_head_rows(W_inT[o2:o3], H)
    W_gzT = W_inT[o5:o6]
    W_gatesT = W_inT[o6:]
    W_smallT = jnp.concatenate([W_inT[o1:o2], W_inT[o3:o5], jnp.zeros((128 - 3 * H, D), BF16)], axis=0)
    gconv_i = _interleave_heads(gconv, H)
    fconv_g, fconv_v = fconv[:, :FF], fconv[:, FF:]
    prm = jnp.zeros((8, 128), F32)
    prm = prm.at[0, 0:H].set(fox_f_bias[0]).at[0, H:2 * H].set(gdn_dt_bias[0]).at[1, H:2 * H].set(gdn_a_log[0])

    x2 = x.reshape(T, D)
    tgt = loss_target.reshape(T, D)

    hn1 = rmsnorm_fwd(x2, norm_mix, "rmsnorm_mix")
    p_fox = matmul(hn1, W_foxT, "nt", "proj_fox", out_dtype=BF16)
    p_gqkv = matmul(hn1, W_gqkvT, "nt", "proj_gqkv")
    p_gz = matmul(hn1, W_gzT, "nt", "proj_gz")
    p_gates = matmul(hn1, W_gatesT, "nt", "proj_gates")
    p_small = matmul(hn1, W_smallT, "nt", "proj_small")

    sm = small_fwd(p_small, prm, B, S, H)
    heads = lambda a: a.reshape(B, S, H).transpose(0, 2, 1)
    c_bhs, gc_bhs, beta_bhs = heads(sm[:, 0:H]), heads(sm[:, H:2 * H]), heads(sm[:, 2 * H:3 * H])
    c_col, c_row = c_bhs[..., None], c_bhs[:, :, None, :]
    gcr5 = gc_bhs.reshape(B, H, N, 1, CHUNK)
    gcr_u = gc_bhs.reshape(B, H, N // PAIR, 1, PAIR * CHUNK)
    betar_u = beta_bhs.reshape(B, H, N // PAIR, 1, PAIR * CHUNK)

    (o_fox, o_fox16, lse), arriving = fox_fwd(p_fox, c_col, c_row, B, S, H, rider=_ride_gather_ici(packs_b))
    qkvn = gdn_prep_fwd(p_gqkv, gconv_i, B, S, H)
    (u_hat, w_t, t_inv), arrived = gdn_intra_fwd(qkvn, betar_u, gcr_u, B, S, H, rider=_ride_gather_d2d(arriving))
    g_up, g_rowed = own(arrived, packs_b)
    W_up = by_cols(g_up)
    W_up_g, W_up_v = W_up[:, :FF], W_up[:, FF:]
    W_bf, W_bg, W_out, W_down = (cat_rows(p) for p in _unpack_rows(g_rowed.reshape(N_CHIP, 2 * Rh, ROW), rowed_shapes))
    o_gdn, states, y_gdn = gdn_inter_fwd(qkvn, u_hat, w_t, gcr5, p_gz, gdn_norm, B, S, H)
    bf_ = matmul(o_fox16, W_bf, "nn", "branch_fox")
    bg_, y = matmul(y_gdn, W_bg, "nn", "branch_gdn", post=_post_merge(p_gates, bf_))
    h1, hn2 = matmul(y, W_out, "nn", "out_proj", add=x2, post=_post_rmsnorm(norm_ffn))
    up_g = matmul(hn2, W_up_g, "nn", "up_gate")
    up_v = matmul(hn2, W_up_v, "nn", "up_val")
    act = ffn_gate_fwd(up_g, up_v, fconv_g, fconv_v, B, S)
    dh2, dh2_16, loss_cols, d_norm_final = matmul(act, W_down, "nn", "down_proj", add=h1,
                                                  post=_post_loss(norm_final.reshape(1, D), tgt))
    loss = lax.psum(0.5 * jnp.sum(loss_cols) / D, ("x", "y", "c"))

    d_act = matmul(dh2_16, W_down, "nt", "d_act")
    dW_down = matmul(act, dh2_16, "tn", "dw_down")
    d_upg, d_upv, d_fconv_g, d_fconv_v = ffn_gate_bwd(up_g, up_v, fconv_g, fconv_v, d_act, B, S)
    d_hn2 = matmul(d_upg, W_up_g, "nt", "d_hn2_g")
    dh1, dh1_16, d_norm_ffn = matmul(d_upv, W_up_v, "nt", "d_hn2_v", add=d_hn2,
                                     post=_post_rmsnorm_bwd(h1, norm_ffn, dh2, True))
    dW_up = jnp.concatenate([matmul(hn2, d_upg, "tn", "dw_up_g"), matmul(hn2, d_upv, "tn", "dw_up_v")], axis=1)
    d_bf, d_bg, d_gates = matmul(dh1_16, W_out, "nt", "d_y", post=_post_merge_bwd(p_gates, bf_, bg_))
    dW_out = matmul(y, dh1_16, "tn", "dw_out")
    d_ofox = matmul(d_bf, W_bf, "nt", "d_ofox")
    dW_bf = matmul(o_fox16, d_bf, "tn", "dw_bf")
    d_ygdn = matmul(d_bg, W_bg, "nt", "d_ygdn")
    dW_bg = matmul(y_gdn, d_bg, "tn", "dw_bg")

    d_fconv = jnp.concatenate([d_fconv_g, d_fconv_v], axis=1)
    col_shard = lambda g, s: g[:, s * (g.shape[1] // N_CHIP):(s + 1) * (g.shape[1] // N_CHIP)]
    row_shard = lambda g, s: g[s * (g.shape[0] // N_CHIP):(s + 1) * (g.shape[0] // N_CHIP)]
    shard_items = lambda s: [row_shard(dW_bf, s), row_shard(dW_bg, s), row_shard(dW_out, s), row_shard(dW_down, s),
                             col_shard(d_fconv, s)]
    g_shapes = [a.shape for a in shard_items(0)]
    assert sum(_rows_of(math.prod(s)) for s in g_shapes) <= 2 * Rh
    to_slabs = lambda g: g.reshape(2, g.shape[0] // 2, N_CHIP, g.shape[1] // N_CHIP).transpose(0, 2, 1, 3)
    gpacks_b = [to_slabs(dW_up),
                jnp.stack([_pack_rows(shard_items(s), 2 * Rh, F32).reshape(2, Rh, ROW) for s in range(N_CHIP)], axis=1)]
    (d_pfox, d_ccol, d_crow), gots_b = fox_bwd(p_fox, c_col, c_row, o_fox, lse, d_ofox, B, S, H,
                                              rider=_ride_exchange(gpacks_b))
    sums_b = [add_halves(g, got, idx, "add_halves_b%d" % i) for i, (g, got) in enumerate(zip(gpacks_b, gots_b))]

    (dq_i, dk_i, d_uh, d_wt, dgcr_a, d_gz, d_gn_parts), got16_b = gdn_inter_bwd(
        qkvn, u_hat, w_t, gcr5, states, o_gdn, p_gz, gdn_norm, d_ygdn, B, S, H,
        rider=_ride_scatter([s16 for _, s16 in sums_b]))
    d_gdn_norm = jnp.sum(d_gn_parts[:, :, 0, :], axis=(0, 1))[None]
    mine_b = [add_chips(s32, g16, idx, "add_chips_b%d" % i) for i, ((s32, _), g16) in enumerate(zip(sums_b, got16_b))]
    d_qkvn, d_betar5, dgcr_b = gdn_intra_bwd(qkvn, betar_u, gcr_u, t_inv, d_uh, d_wt, dq_i, dk_i, B, S, H)
    d_pgqkv, d_gconv_i = gdn_prep_bwd(p_gqkv, gconv_i, d_qkvn, B, S, H)

    tokens = lambda a: a.reshape(B, H, S).transpose(0, 2, 1).reshape(T, H)
    d_gc = dgcr_a.reshape(B, H, S) + dgcr_b.reshape(B, H, S)
    d_sm = jnp.concatenate([tokens(d_ccol.reshape(B, H, S) + d_crow.reshape(B, H, S)), tokens(d_gc), tokens(d_betar5.reshape(B, H, S)),
                            jnp.zeros((T, 128 - 3 * H), F32)], axis=1)
    d_psmall, d_prm = small_bwd(p_small, prm, d_sm, B, S, H)

    d_hn1 = matmul(d_pfox, W_foxT, "nn", "d_hn1_fox")
    d_hn1 = matmul(d_pgqkv, W_gqkvT, "nn", "d_hn1_gqkv", add=d_hn1)
    d_hn1 = matmul(d_gz, W_gzT, "nn", "d_hn1_gz", add=d_hn1)
    d_hn1 = matmul(d_gates, W_gatesT, "nn", "d_hn1_gates", add=d_hn1)
    grad_x, d_norm_mix = matmul(d_psmall, W_smallT, "nn", "d_hn1_small", add=d_hn1,
                                post=_post_rmsnorm_bwd(x2, norm_mix, dh1, False))
    dW_foxT = matmul(d_pfox, hn1, "tn", "dw_fox")
    dW_gqkvT = matmul(d_pgqkv, hn1, "tn", "dw_gqkv")
    dW_gzT = matmul(d_gz, hn1, "tn", "dw_gz")
    dW_gatesT = matmul(d_gates, hn1, "tn", "dw_gates")
    dW_smallT = matmul(d_psmall, hn1, "tn", "dw_small")

    dW_inT = jnp.concatenate([_deinterleave_head_rows(dW_foxT, H), dW_smallT[0:H], _deinterleave_head_rows(dW_gqkvT, H),
                              dW_smallT[H:3 * H], dW_gzT, dW_gatesT], axis=0)
    d_gconv = _deinterleave_heads(d_gconv_i, H)

    gpack_a = [dW_inT.reshape(N_CHIP, c_in, 2, D // 2).transpose(2, 0, 1, 3)]
    gots_a = _run_alone(_ride_exchange(gpack_a), "exchange_halves")
    sums_a = [add_halves(gpack_a[0], gots_a[0], idx, "add_halves_a")]
    got16_a = _run_alone(_ride_scatter([sums_a[0][1]]), "scatter_chips")
    mine = [add_chips(sums_a[0][0], got16_a[0], idx, "add_chips_a")] + mine_b
    others = share_halves(mine)
    g_w_inT, g_up, g_rows = (jnp.concatenate([jnp.where(cidx == 0, h, o), jnp.where(cidx == 0, o, h)], axis=ax)
                             for h, o, ax in zip(mine, others, (1, 0, 0)))
    g_w_in = g_w_inT.T
    g_bf, g_bg, g_out, g_down, g_fconv = _unpack_rows(g_rows, g_shapes)

    small_items = [d_norm_mix, d_norm_ffn, d_norm_final, d_gdn_norm, d_prm, d_gconv]
    small_shapes = [a.shape for a in small_items]
    sv = allreduce_small(_pack_rows(small_items, 0, F32, unit=8))
    g_norm_mix, g_norm_ffn, g_norm_final, g_gdn_norm, g_prm, g_gconv_all = _unpack_rows(sv, small_shapes, unit=8)
    g_norm_final = g_norm_final.reshape(D)
    g_fbias, g_dtb, g_alog = g_prm[0:1, 0:H], g_prm[0:1, H:2 * H], g_prm[1:2, H:2 * H]
    g_gconv = lax.dynamic_slice_in_dim(g_gconv_all, sidx * (3 * D // N_CHIP), 3 * D // N_CHIP, axis=1)

    names = ["norm_mix", "w_in", "fox_f_bias", "gdn_conv_w", "gdn_a_log", "gdn_dt_bias", "gdn_norm", "w_branch_fox",
             "w_branch_gdn", "w_out", "norm_ffn", "w_up", "ffn_conv_w", "w_down", "norm_final"]
    ws = [norm_mix, w_in, fox_f_bias, gdn_conv_w, gdn_a_log, gdn_dt_bias, gdn_norm, w_branch_fox, w_branch_gdn, w_out,
          norm_ffn, w_up, ffn_conv_w, w_down, norm_final]
    ms = [m_norm_mix, m_w_in, m_fox_f_bias, m_gdn_conv_w, m_gdn_a_log, m_gdn_dt_bias, m_gdn_norm, m_w_branch_fox,
          m_w_branch_gdn, m_w_out, m_norm_ffn, m_w_up, m_ffn_conv_w, m_w_down, m_norm_final]
    vs = [v_norm_mix, v_w_in, v_fox_f_bias, v_gdn_conv_w, v_gdn_a_log, v_gdn_dt_bias, v_gdn_norm, v_w_branch_fox,
          v_w_branch_gdn, v_w_out, v_norm_ffn, v_w_up, v_ffn_conv_w, v_w_down, v_norm_final]
    gs = [g_norm_mix, g_w_in, g_fbias, g_gconv, g_alog, g_dtb, g_gdn_norm, g_bf, g_bg, g_out, g_norm_ffn, g_up,
          g_fconv, g_down, g_norm_final]
    gs = [g.reshape(w.shape) for g, w in zip(gs, ws)]
    deltas, new_ms, new_vs = [], [], []
    for nm, w, g, m, v in zip(names, ws, gs, ms, vs):
        if w.ndim == 1:
            d, a, b = adamw(w.reshape(1, -1), g.reshape(1, -1), m.reshape(1, -1), v.reshape(1, -1), "adamw_" + nm)
            d, a, b = d.reshape(w.shape), a.reshape(w.shape), b.reshape(w.shape)
        elif nm == "w_in":
            d, a, b = (r.T[None] for r in adamw(w[0].T, g_w_inT, m[0].T, v[0].T, "adamw_" + nm))
        else:
            d, a, b = adamw(w, g, m, v, "adamw_" + nm)
        deltas.append(d)
        new_ms.append(a)
        new_vs.append(b)

    return (loss, grad_x.reshape(B, S, D), *gs, *deltas, *new_ms, *new_vs)
```

```python
import functools
import math

import jax
import jax.numpy as jnp
from jax import lax
from jax.experimental import pallas as pl
from jax.experimental.pallas import tpu as pltpu

F32 = jnp.float32
BF16 = jnp.bfloat16
HEAD = 128
CHUNK = 64
GDN_CONV = 4
FFN_CONV = 3
EPS = 1e-6
NEG = -1e30
ROW = 1024
ATT_TILE = 512
MM_WEIGHT_TILE_BYTES = 8 << 20
N_CHIP = 4
N_DEV = 8
MESH = pl.DeviceIdType.MESH
HI = lax.Precision.HIGH
EXACT = lax.Precision.HIGHEST

ADAM_LR, ADAM_B1, ADAM_B2, ADAM_EPS, ADAM_WD, ADAM_STEP = 0.001, 0.9, 0.999, 1e-08, 0.01, 10


def _tile(n, cap, unit=128):
    best = None
    t = unit
    while t <= min(n, cap):
        if n % t == 0:
            best = t
        t += unit
    return best if best is not None else n


def _params(*sem):
    return pltpu.CompilerParams(dimension_semantics=sem)


_NN = (((1,), (0,)), ((), ()))
_NT = (((1,), (1,)), ((), ()))
_TN = (((0,), (0,)), ((), ()))


def _dg(a, b, dims, hi):
    if hi:
        return lax.dot_general(a, b, dims, precision=HI, preferred_element_type=F32)
    return lax.dot_general(a.astype(BF16), b.astype(BF16), dims, preferred_element_type=F32)


class _RawOps:
    @staticmethod
    def nn(a, b, hi=False):
        return _dg(a, b, _NN, hi)

    @staticmethod
    def nt(a, b, hi=False):
        return _dg(a, b, _NT, hi)

    @staticmethod
    def tn(a, b, hi=False):
        return _dg(a, b, _TN, hi)


def _make_diff_ops():
    def build(hi):
        @jax.custom_vjp
        def nn(a, b):
            return _dg(a, b, _NN, hi)

        nn.defvjp(lambda a, b: (_dg(a, b, _NN, hi), (a, b)),
                  lambda r, g: (_dg(g, r[1], _NT, hi), _dg(r[0], g, _TN, hi)))

        @jax.custom_vjp
        def nt(a, b):
            return _dg(a, b, _NT, hi)

        nt.defvjp(lambda a, b: (_dg(a, b, _NT, hi), (a, b)),
                  lambda r, g: (_dg(g, r[1], _NN, hi), _dg(g, r[0], _TN, hi)))

        @jax.custom_vjp
        def tn(a, b):
            return _dg(a, b, _TN, hi)

        tn.defvjp(lambda a, b: (_dg(a, b, _TN, hi), (a, b)),
                  lambda r, g: (_dg(r[1], g, _NT, hi), _dg(r[0], g, _NN, hi)))
        return nn, nt, tn

    lo, hi_ = build(False), build(True)

    class _DiffOps:
        @staticmethod
        def nn(a, b, hi=False):
            return (hi_ if hi else lo)[0](a, b)

        @staticmethod
        def nt(a, b, hi=False):
            return (hi_ if hi else lo)[1](a, b)

        @staticmethod
        def tn(a, b, hi=False):
            return (hi_ if hi else lo)[2](a, b)

    return _DiffOps


_DiffOps = _make_diff_ops()


def _sigmoid(x):
    return 1.0 / (1.0 + jnp.exp(-x))


def _mm_tile(n, pref):
    if n % pref == 0:
        return pref
    if n % 1408 == 0:
        return 1408
    return _tile(n, pref)


class _Post:
    def __init__(self, fn, row_ins=(), vec_ins=(), row_outs=(), acc_outs=(), keep_main=True):
        self.fn, self.keep_main = fn, keep_main
        self.row_ins = [r if isinstance(r, tuple) else (r, r.shape[1], 0) for r in row_ins]
        self.vec_ins, self.row_outs, self.acc_outs = list(vec_ins), list(row_outs), list(acc_outs)


def matmul(a, b, mode, name, add=None, out_dtype=F32, post=None, rider=None):
    if mode == "nn":
        (M, K), (K2, N) = a.shape, b.shape
    elif mode == "nt":
        (M, K), (N, K2) = a.shape, b.shape
    else:
        (K, M), (K2, N) = a.shape, b.shape
    assert K == K2, (name, a.shape, b.shape)
    tn = _mm_tile(N, 1024)
    if mode == "tn":
        tm = M if M <= 1408 else _mm_tile(M, 1408)
        tk = _mm_tile(K, 2048)
    else:
        tk = K if K * tn * 2 <= MM_WEIGHT_TILE_BYTES else _mm_tile(K, 1024)
        tm = _mm_tile(M, 1024 if tk <= 2048 and post is None else 512)
    nk = K // tk
    assert post is None or (mode != "tn" and tn == N), name
    dims = {"nn": _NN, "nt": _NT, "tn": _TN}[mode]
    if mode == "tn":
        a_spec = pl.BlockSpec((tk, tm), lambda j, i, k: (k, i))
    else:
        a_spec = pl.BlockSpec((tm, tk), lambda j, i, k: (i, k))
    if mode == "nt":
        b_spec = pl.BlockSpec((tn, tk), lambda j, i, k: (j, k))
    else:
        b_spec = pl.BlockSpec((tk, tn), lambda j, i, k: (k, j))
    o_spec = pl.BlockSpec((tm, tn), lambda j, i, k: (i, j))
    has_add = add is not None
    keep_main = post is None or post.keep_main
    counts = [2 + has_add] + ([len(post.row_ins), len(post.vec_ins)] if post else [0, 0]) + [int(keep_main)]
    counts += ([len(post.row_outs), len(post.acc_outs)] if post else [0, 0]) + [int(nk > 1)]

    def body(*refs):
        parts, p = [], 0
        for cnt in counts:
            parts.append(refs[p:p + cnt])
            p += cnt
        core, row_ins, vec_ins, main, row_outs, acc_outs, acc = parts
        a_ref, b_ref = core[:2]
        prod = lax.dot_general(a_ref[...].astype(BF16), b_ref[...].astype(BF16), dims, preferred_element_type=F32)

        def finish(r):
            if has_add:
                r = r + core[2][...]
            if keep_main:
                main[0][...] = r.astype(out_dtype)
            if post is not None:
                @pl.when(pl.program_id(1) == 0)
                def _():
                    for ref in acc_outs:
                        ref[...] = jnp.zeros_like(ref)

                post.fn(r, row_ins, vec_ins, row_outs, acc_outs)

        if nk == 1:
            finish(prod)
            return
        acc_ref = acc[0]
        k = pl.program_id(2)

        @pl.when(k == 0)
        def _():
            acc_ref[...] = jnp.zeros_like(acc_ref)

        acc_ref[...] += prod

        @pl.when(k == nk - 1)
        def _():
            finish(acc_ref[...])

    in_specs = [a_spec, b_spec] + ([o_spec] if has_add else [])
    args = (a, b) + ((add,) if has_add else ())
    out_specs = [o_spec] if keep_main else []
    out_shape = [jax.ShapeDtypeStruct((M, N), out_dtype)] if keep_main else []
    if post is not None:
        in_specs += [pl.BlockSpec((tm, cols), lambda j, i, k, cb=cb: (i, cb)) for _, cols, cb in post.row_ins]
        in_specs += [pl.BlockSpec((1, v.shape[1]), lambda j, i, k: (0, 0)) for v in post.vec_ins]
        args += tuple(r for r, _, _ in post.row_ins) + tuple(post.vec_ins)
        out_specs += [pl.BlockSpec((tm, cols), lambda j, i, k: (i, 0)) for cols, _ in post.row_outs]
        out_specs += [pl.BlockSpec((1, cols), lambda j, i, k: (0, 0)) for cols in post.acc_outs]
        out_shape += [jax.ShapeDtypeStruct((M, cols), dt) for cols, dt in post.row_outs]
        out_shape += [jax.ShapeDtypeStruct((1, cols), F32) for cols in post.acc_outs]
    rows_sem = "arbitrary" if post is not None and post.acc_outs else "parallel"
    res = _hosted_call(
        body, rider, name=name, grid=(N // tn, M // tm, nk), in_specs=in_specs, out_specs=out_specs, out_shape=out_shape,
        scratch_shapes=[pltpu.VMEM((tm, tn), F32)] if nk > 1 else [], semantics=("parallel", rows_sem, "arbitrary"),
    )(*args)
    if rider is not None:
        res, carried = res
        return (res[0] if post is None else res), carried
    return res[0] if post is None else res


def rmsnorm_fwd(x, g, name):
    T, D = x.shape
    tm = _tile(T, 512, 8)

    def body(x_ref, g_ref, o_ref):
        xv = x_ref[...]
        r = lax.rsqrt(jnp.mean(xv * xv, axis=-1, keepdims=True) + EPS)
        o_ref[...] = (xv * r * g_ref[...]).astype(BF16)

    return pl.pallas_call(
        body, name=name, grid=(T // tm,),
        in_specs=[pl.BlockSpec((tm, D), lambda i: (i, 0)), pl.BlockSpec((1, D), lambda i: (0, 0))],
        out_specs=pl.BlockSpec((tm, D), lambda i: (i, 0)),
        out_shape=jax.ShapeDtypeStruct((T, D), BF16),
        compiler_params=_params("parallel"),
    )(x, g)


def _post_rmsnorm(g):
    def fn(r, row_ins, vec_ins, row_outs, acc_outs):
        rs = lax.rsqrt(jnp.mean(r * r, axis=-1, keepdims=True) + EPS)
        row_outs[0][...] = (r * rs * vec_ins[0][...]).astype(BF16)

    return _Post(fn, vec_ins=[g], row_outs=[(g.shape[1], BF16)])


def _post_rmsnorm_bwd(x, g, dres, with_bf16):
    D = g.shape[1]

    def fn(dy, row_ins, vec_ins, row_outs, acc_outs):
        xv = row_ins[0][...]
        rs = lax.rsqrt(jnp.mean(xv * xv, axis=-1, keepdims=True) + EPS)
        xh = xv * rs
        acc_outs[0][...] += jnp.sum(dy * xh, axis=0, keepdims=True)
        dxh = dy * vec_ins[0][...]
        dx = row_ins[1][...] + rs * (dxh - xh * jnp.mean(dxh * xh, axis=-1, keepdims=True))
        row_outs[0][...] = dx
        if with_bf16:
            row_outs[1][...] = dx.astype(BF16)

    return _Post(fn, row_ins=[x, dres], vec_ins=[g], row_outs=[(D, F32)] + ([(D, BF16)] if with_bf16 else []),
                 acc_outs=[D], keep_main=False)


def _post_loss(g, target):
    D = g.shape[1]

    def fn(hv, row_ins, vec_ins, row_outs, acc_outs):
        rs = lax.rsqrt(jnp.mean(hv * hv, axis=-1, keepdims=True) + EPS)
        xh = hv * rs
        gv = vec_ins[0][...]
        err = xh * gv - row_ins[0][...]
        acc_outs[0][...] += jnp.sum(err * err, axis=0, keepdims=True)
        dy = err * (1.0 / D)
        acc_outs[1][...] += jnp.sum(dy * xh, axis=0, keepdims=True)
        dxh = dy * gv
        dh = rs * (dxh - xh * jnp.mean(dxh * xh, axis=-1, keepdims=True))
        row_outs[0][...] = dh
        row_outs[1][...] = dh.astype(BF16)

    return _Post(fn, row_ins=[target], vec_ins=[g], row_outs=[(D, F32), (D, BF16)], acc_outs=[D, D], keep_main=False)


def _shift_down(x, k):
    if k == 0:
        return x
    rows = lax.broadcasted_iota(jnp.int32, x.shape, 0)
    return jnp.where(rows >= k, pltpu.roll(x, k, 0), 0.0)


def _shift_up(x, k):
    if k == 0:
        return x
    s = x.shape[0]
    rows = lax.broadcasted_iota(jnp.int32, x.shape, 0)
    return jnp.where(rows < s - k, pltpu.roll(x, s - k, 0), 0.0)


def _conv_fwd(x, w_ref, kw):
    y = x * w_ref[kw - 1:kw, :]
    for i in range(kw - 1):
        y = y + _shift_down(x, kw - 1 - i) * w_ref[i:i + 1, :]
    return y


def _conv_bwd(x, dy, w_ref, kw):
    dx = dy * w_ref[kw - 1:kw, :]
    dws = []
    for i in range(kw - 1):
        dx = dx + _shift_up(dy, kw - 1 - i) * w_ref[i:i + 1, :]
        dws.append(jnp.sum(dy * _shift_down(x, kw - 1 - i), axis=0, keepdims=True))
    dws.append(jnp.sum(dy * x, axis=0, keepdims=True))
    return dx, dws


def ffn_gate_fwd(up_g, up_v, cw_g, cw_v, B, S):
    T, Fd = up_g.shape
    tc = _tile(Fd, 256)

    def body(g_ref, v_ref, wg_ref, wv_ref, o_ref):
        ug = _conv_fwd(g_ref[...], wg_ref, FFN_CONV)
        uv = _conv_fwd(v_ref[...], wv_ref, FFN_CONV)
        o_ref[...] = (ug * _sigmoid(ug) * uv).astype(BF16)

    blk = pl.BlockSpec((S, tc), lambda b, j: (b, j))
    wblk = pl.BlockSpec((FFN_CONV, tc), lambda b, j: (0, j))
    return pl.pallas_call(
        body, name="ffn_gate_fwd", grid=(B, Fd // tc), in_specs=[blk, blk, wblk, wblk], out_specs=blk,
        out_shape=jax.ShapeDtypeStruct((T, Fd), BF16), compiler_params=_params("parallel", "parallel"),
    )(up_g, up_v, cw_g, cw_v)


def ffn_gate_bwd(up_g, up_v, cw_g, cw_v, d_act, B, S):
    T, Fd = up_g.shape
    tc = _tile(Fd, 256)

    def body(g_ref, v_ref, wg_ref, wv_ref, da_ref, dg_ref, dv_ref, dwg_ref, dwv_ref):
        @pl.when(pl.program_id(1) == 0)
        def _():
            dwg_ref[...] = jnp.zeros_like(dwg_ref)
            dwv_ref[...] = jnp.zeros_like(dwv_ref)

        xg, xv = g_ref[...], v_ref[...]
        ug = _conv_fwd(xg, wg_ref, FFN_CONV)
        uv = _conv_fwd(xv, wv_ref, FFN_CONV)
        da = da_ref[...]
        sg = _sigmoid(ug)
        d_ug = da * uv * (sg + ug * sg * (1.0 - sg))
        d_uv = da * ug * sg
        dxg, dwg = _conv_bwd(xg, d_ug, wg_ref, FFN_CONV)
        dxv, dwv = _conv_bwd(xv, d_uv, wv_ref, FFN_CONV)
        dg_ref[...] = dxg.astype(BF16)
        dv_ref[...] = dxv.astype(BF16)
        for i in range(FFN_CONV):
            dwg_ref[i:i + 1, :] += dwg[i]
            dwv_ref[i:i + 1, :] += dwv[i]

    blk = pl.BlockSpec((S, tc), lambda j, b: (b, j))
    wblk = pl.BlockSpec((FFN_CONV, tc), lambda j, b: (0, j))
    return pl.pallas_call(
        body, name="ffn_gate_bwd", grid=(Fd // tc, B), in_specs=[blk, blk, wblk, wblk, blk],
        out_specs=[blk, blk, wblk, wblk],
        out_shape=[jax.ShapeDtypeStruct((T, Fd), BF16), jax.ShapeDtypeStruct((T, Fd), BF16),
                   jax.ShapeDtypeStruct((FFN_CONV, Fd), F32), jax.ShapeDtypeStruct((FFN_CONV, Fd), F32)],
        compiler_params=_params("parallel", "arbitrary"),
    )(up_g, up_v, cw_g, cw_v, d_act)


def _post_merge(p_gates, bf_):
    D = bf_.shape[1]

    def fn(bg, row_ins, vec_ins, row_outs, acc_outs):
        gf_ref, gg_ref, bf_ref = row_ins
        row_outs[0][...] = (_sigmoid(gf_ref[...]) * bf_ref[...] + _sigmoid(gg_ref[...]) * bg).astype(BF16)

    return _Post(fn, row_ins=[(p_gates, D, 0), (p_gates, D, 1), bf_], row_outs=[(D, BF16)])


def _post_merge_bwd(p_gates, bf_, bg_):
    D = bf_.shape[1]

    def fn(d, row_ins, vec_ins, row_outs, acc_outs):
        gf_ref, gg_ref, bf_ref, bg_ref = row_ins
        sf, sg = _sigmoid(gf_ref[...]), _sigmoid(gg_ref[...])
        row_outs[0][...] = (d * sf).astype(BF16)
        row_outs[1][...] = (d * sg).astype(BF16)
        row_outs[2][:, 0:D] = (d * bf_ref[...] * sf * (1.0 - sf)).astype(BF16)
        row_outs[2][:, D:2 * D] = (d * bg_ref[...] * sg * (1.0 - sg)).astype(BF16)

    return _Post(fn, row_ins=[(p_gates, D, 0), (p_gates, D, 1), bf_, bg_],
                 row_outs=[(D, BF16), (D, BF16), (2 * D, BF16)], keep_main=False)


def fox_fwd(p_fox, c_col, c_row, B, S, H, rider=None):
    T = B * S
    t = _tile(S, ATT_TILE)
    nq = S // t
    scale = HEAD ** -0.5

    def body(q_ref, k_ref, v_ref, cq_ref, cr_ref, o_ref, o16_ref, lse_ref):
        i = pl.program_id(2)
        q = q_ref[...]
        cq = cq_ref[...]
        row = lax.broadcasted_iota(jnp.int32, (t, t), 0)
        col = lax.broadcasted_iota(jnp.int32, (t, t), 1)

        def step(j, carry, diagonal):
            m, l, acc = carry
            off = pl.multiple_of(j * t, t)
            k = k_ref[pl.ds(off, t), :]
            v = v_ref[pl.ds(off, t), :]
            s = lax.dot_general(q, k, _NT, preferred_element_type=F32) * scale - cr_ref[:, pl.ds(off, t)]
            if diagonal:
                s = jnp.where(col <= row, s, NEG)
            m_new = jnp.maximum(m, jnp.max(s, axis=-1, keepdims=True))
            alpha = jnp.exp(m - m_new)
            p = jnp.exp(s - m_new)
            l = alpha * l + jnp.sum(p, axis=-1, keepdims=True)
            acc = alpha * acc + lax.dot_general(p.astype(BF16), v, _NN, preferred_element_type=F32)
            return m_new, l, acc

        m0 = jnp.full((t, 1), NEG, F32)
        below = lax.fori_loop(0, i, functools.partial(step, diagonal=False),
                              (m0, jnp.zeros((t, 1), F32), jnp.zeros((t, HEAD), F32)))
        m, l, acc = step(i, below, diagonal=True)
        o = acc / l
        o_ref[...] = o
        o16_ref[...] = o.astype(BF16)
        lse_ref[...] = cq + m + jnp.log(l)

    return _hosted_call(
        body, rider, name="fox_fwd", grid=(B, H, nq),
        in_specs=[pl.BlockSpec((t, HEAD), lambda b, h, i: (b * nq + i, 3 * h)),
                  pl.BlockSpec((S, HEAD), lambda b, h, i: (b, 3 * h + 1)),
                  pl.BlockSpec((S, HEAD), lambda b, h, i: (b, 3 * h + 2)),
                  pl.BlockSpec((None, None, t, 1), lambda b, h, i: (b, h, i, 0)),
                  pl.BlockSpec((None, None, 1, S), lambda b, h, i: (b, h, 0, 0))],
        out_specs=[pl.BlockSpec((t, HEAD), lambda b, h, i: (b * nq + i, h)),
                   pl.BlockSpec((t, HEAD), lambda b, h, i: (b * nq + i, h)),
                   pl.BlockSpec((None, None, t, 1), lambda b, h, i: (b, h, i, 0))],
        out_shape=[jax.ShapeDtypeStruct((T, H * HEAD), F32), jax.ShapeDtypeStruct((T, H * HEAD), BF16),
                   jax.ShapeDtypeStruct((B, H, S, 1), F32)],
        scratch_shapes=[], semantics=("parallel", "parallel", "arbitrary"),
    )(p_fox, p_fox, p_fox, c_col, c_row)


def fox_bwd(p_fox, c_col, c_row, o, lse, do, B, S, H, rider=None):
    T = B * S
    t = _tile(S, ATT_TILE)
    n = S // t
    scale = HEAD ** -0.5

    def body(q_ref, k_ref, v_ref, cq_ref, cr_ref, o_ref, lse_ref, do_ref, dqkv_ref, dcq_ref, dcr_ref,
             dq_acc, delta_s, lse_s):
        row = lax.broadcasted_iota(jnp.int32, (t, t), 0)
        col = lax.broadcasted_iota(jnp.int32, (t, t), 1)

        def prep(i, c):
            rows = pl.ds(pl.multiple_of(i * t, t), t)
            delta_s[rows, :] = jnp.sum(do_ref[rows, :] * o_ref[rows, :], axis=-1, keepdims=True)
            lse_s[rows, :] = lse_ref[rows, :] - cq_ref[rows, :]
            dq_acc[rows, :] = jnp.zeros((t, HEAD), F32)
            dcq_ref[rows, :] = jnp.zeros((t, 1), F32)
            return c

        lax.fori_loop(0, n, prep, 0)

        def kv_step(j, c):
            joff = pl.multiple_of(j * t, t)
            k = k_ref[pl.ds(joff, t), :]
            v = v_ref[pl.ds(joff, t), :]
            crj = cr_ref[:, pl.ds(joff, t)]

            def q_step(i, carry, diagonal):
                dk, dv, dc = carry
                rows = pl.ds(pl.multiple_of(i * t, t), t)
                q = q_ref[rows, :]
                dob = do_ref[rows, :].astype(BF16)
                s = lax.dot_general(q, k, _NT, preferred_element_type=F32) * scale - crj
                if diagonal:
                    s = jnp.where(col <= row, s, NEG)
                p = jnp.exp(s - lse_s[rows, :])
                dp = lax.dot_general(dob, v, _NT, preferred_element_type=F32)
                ds = p * (dp - delta_s[rows, :])
                dsb = ds.astype(BF16)
                dv = dv + lax.dot_general(p.astype(BF16), dob, _TN, preferred_element_type=F32)
                dk = dk + lax.dot_general(dsb, q, _TN, preferred_element_type=F32)
                dq_acc[rows, :] += lax.dot_general(dsb, k, _NN, preferred_element_type=F32) * scale
                dc = dc + jnp.sum(ds, axis=0, keepdims=True)
                dcq_ref[rows, :] += jnp.sum(ds, axis=-1, keepdims=True)
                return dk, dv, dc

            z = jnp.zeros((t, HEAD), F32)
            on_diagonal = q_step(j, (z, z, jnp.zeros((1, t), F32)), diagonal=True)
            dk, dv, dc = lax.fori_loop(j + 1, n, functools.partial(q_step, diagonal=False), on_diagonal)
            dqkv_ref[pl.ds(joff, t), HEAD:2 * HEAD] = (dk * scale).astype(BF16)
            dqkv_ref[pl.ds(joff, t), 2 * HEAD:3 * HEAD] = dv.astype(BF16)
            dcr_ref[:, pl.ds(joff, t)] = -dc
            return c

        lax.fori_loop(0, n, kv_step, 0)
        dqkv_ref[:, 0:HEAD] = dq_acc[...].astype(BF16)

    col_spec = pl.BlockSpec((None, None, S, 1), lambda b, h: (b, h, 0, 0))
    row_spec = pl.BlockSpec((None, None, 1, S), lambda b, h: (b, h, 0, 0))
    head = pl.BlockSpec((S, HEAD), lambda b, h: (b, h))
    return _hosted_call(
        body, rider, name="fox_bwd", grid=(B, H),
        in_specs=[pl.BlockSpec((S, HEAD), lambda b, h: (b, 3 * h)),
                  pl.BlockSpec((S, HEAD), lambda b, h: (b, 3 * h + 1)),
                  pl.BlockSpec((S, HEAD), lambda b, h: (b, 3 * h + 2)),
                  col_spec, row_spec, head, col_spec, head],
        out_specs=[pl.BlockSpec((S, 3 * HEAD), lambda b, h: (b, h)), col_spec, row_spec],
        out_shape=[jax.ShapeDtypeStruct((T, 3 * H * HEAD), BF16), jax.ShapeDtypeStruct((B, H, S, 1), F32),
                   jax.ShapeDtypeStruct((B, H, 1, S), F32)],
        scratch_shapes=[pltpu.VMEM((S, HEAD), F32), pltpu.VMEM((S, 1), F32), pltpu.VMEM((S, 1), F32)],
        semantics=("parallel", "parallel"),
    )(p_fox, p_fox, p_fox, c_col, c_row, o, lse, do)


def _small_fn(x, b0, b1, H):
    S = x.shape[0]
    lane = lax.broadcasted_iota(jnp.int32, x.shape, 1)
    z = x + b0
    tail = jnp.log1p(jnp.exp(-jnp.abs(z)))
    softplus = jnp.maximum(z, 0.0) + tail
    logsig = -(jnp.maximum(-z, 0.0) + tail)
    g = -jnp.exp(b1) * softplus
    pre = jnp.where(lane < H, logsig, jnp.where(lane < 2 * H, g, 0.0))
    bl = _tile(S, 256, CHUNK)
    r = lax.broadcasted_iota(jnp.int32, (bl, bl), 0)
    c = lax.broadcasted_iota(jnp.int32, (bl, bl), 1)
    tri = (r >= c).astype(F32)
    tri_chunk = jnp.where((r >= c) & (jnp.right_shift(r, 6) == jnp.right_shift(c, 6)), 1.0, 0.0)
    carry = jnp.zeros((1, x.shape[1]), F32)
    parts = []
    for i in range(S // bl):
        blk = pre[i * bl:(i + 1) * bl, :]
        full = lax.dot_general(tri, blk, _NN, precision=EXACT, preferred_element_type=F32) + carry
        chunked = lax.dot_general(tri_chunk, blk, _NN, precision=EXACT, preferred_element_type=F32)
        parts.append(jnp.where(lane[:bl] < H, full, chunked))
        carry = carry + jnp.sum(blk, axis=0, keepdims=True)
    cum = parts[0] if len(parts) == 1 else jnp.concatenate(parts, axis=0)
    return jnp.where(lane < 2 * H, cum, jnp.where(lane < 3 * H, _sigmoid(x), 0.0))


def small_fwd(p_small, prm, B, S, H):
    T = B * S

    def body(x_ref, p_ref, o_ref):
        o_ref[...] = _small_fn(x_ref[...], p_ref[0:1, :], p_ref[1:2, :], H)

    blk = pl.BlockSpec((S, 128), lambda b: (b, 0))
    return pl.pallas_call(
        body, name="small_fwd", grid=(B,), in_specs=[blk, pl.BlockSpec((8, 128), lambda b: (0, 0))], out_specs=blk,
        out_shape=jax.ShapeDtypeStruct((T, 128), F32), compiler_params=_params("parallel"),
    )(p_small, prm)


def small_bwd(p_small, prm, d_out, B, S, H):
    T = B * S

    def body(x_ref, p_ref, d_ref, dx_ref, dp_ref):
        @pl.when(pl.program_id(0) == 0)
        def _():
            dp_ref[...] = jnp.zeros_like(dp_ref)

        _, vjp = jax.vjp(functools.partial(_small_fn, H=H), x_ref[...], p_ref[0:1, :], p_ref[1:2, :])
        dx, db0, db1 = vjp(d_ref[...])
        dx_ref[...] = dx.astype(BF16)
        dp_ref[0:1, :] += db0
        dp_ref[1:2, :] += db1

    blk = pl.BlockSpec((S, 128), lambda b: (b, 0))
    pblk = pl.BlockSpec((8, 128), lambda b: (0, 0))
    return pl.pallas_call(
        body, name="small_bwd", grid=(B,), in_specs=[blk, pblk, blk], out_specs=[blk, pblk],
        out_shape=[jax.ShapeDtypeStruct((T, 128), BF16), jax.ShapeDtypeStruct((8, 128), F32)],
        compiler_params=_params("arbitrary"),
    )(p_small, prm, d_out)


def gdn_prep_fwd(p_gqkv, cw, B, S, H):
    T = B * S

    def body(x_ref, w_ref, o_ref):
        y = _conv_fwd(x_ref[...], w_ref, GDN_CONV)
        a = y * _sigmoid(y)
        rs = lax.rsqrt(jnp.sum(a * a, axis=-1, keepdims=True) + EPS)
        is_qk = (pl.program_id(1) % 3) < 2
        o_ref[...] = a * jnp.where(is_qk, rs, 1.0)

    blk = pl.BlockSpec((S, HEAD), lambda b, n: (b, n))
    wblk = pl.BlockSpec((GDN_CONV, HEAD), lambda b, n: (0, n))
    return pl.pallas_call(
        body, name="gdn_prep_fwd", grid=(B, 3 * H), in_specs=[blk, wblk], out_specs=blk,
        out_shape=jax.ShapeDtypeStruct((T, 3 * H * HEAD), F32), compiler_params=_params("parallel", "parallel"),
    )(p_gqkv, cw)


def gdn_prep_bwd(p_gqkv, cw, d_out, B, S, H):
    T = B * S

    def body(x_ref, w_ref, d_ref, dx_ref, dw_ref):
        @pl.when(pl.program_id(1) == 0)
        def _():
            dw_ref[...] = jnp.zeros_like(dw_ref)

        x = x_ref[...]
        y = _conv_fwd(x, w_ref, GDN_CONV)
        sg = _sigmoid(y)
        a = y * sg
        rs = lax.rsqrt(jnp.sum(a * a, axis=-1, keepdims=True) + EPS)
        d = d_ref[...]
        out = a * rs
        da_qk = rs * (d - out * jnp.sum(d * out, axis=-1, keepdims=True))
        is_qk = (pl.program_id(0) % 3) < 2
        da = jnp.where(is_qk, da_qk, d)
        dy = da * (sg + y * sg * (1.0 - sg))
        dx, dws = _conv_bwd(x, dy, w_ref, GDN_CONV)
        dx_ref[...] = dx.astype(BF16)
        for i in range(GDN_CONV):
            dw_ref[i:i + 1, :] += dws[i]

    blk = pl.BlockSpec((S, HEAD), lambda n, b: (b, n))
    wblk = pl.BlockSpec((GDN_CONV, HEAD), lambda n, b: (0, n))
    return pl.pallas_call(
        body, name="gdn_prep_bwd", grid=(3 * H, B), in_specs=[blk, wblk, blk], out_specs=[blk, wblk],
        out_shape=[jax.ShapeDtypeStruct((T, 3 * H * HEAD), BF16), jax.ShapeDtypeStruct((GDN_CONV, 3 * H * HEAD), F32)],
        compiler_params=_params("parallel", "arbitrary"),
    )(p_gqkv, cw, d_out)


@jax.custom_vjp
def _given_inverse(a, t):
    return t


def _given_inverse_fwd(a, t):
    return t, t


def _given_inverse_bwd(t, g):
    x = _dg(t, g, _TN, True)
    return -_dg(x, t, _NT, True), jnp.zeros_like(t)


_given_inverse.defvjp(_given_inverse_fwd, _given_inverse_bwd)


def _to_col(row):
    n = row.shape[1]
    r = lax.broadcasted_iota(jnp.int32, (n, n), 0)
    c = lax.broadcasted_iota(jnp.int32, (n, n), 1)
    return jnp.sum(jnp.where(r == c, row, 0.0), axis=1, keepdims=True)


def _intra_fn(k, v, beta_r, gcr, ops, t_known=None):
    n = len(k)
    m = k[0].shape[0]
    r = lax.broadcasted_iota(jnp.int32, (m, m), 0)
    c = lax.broadcasted_iota(jnp.int32, (m, m), 1)
    below = (r > c) & (jnp.right_shift(r, 6) == jnp.right_shift(c, 6))
    beta = [_to_col(beta_r[i]) for i in range(n)]
    gcc = [_to_col(gcr[i]) for i in range(n)]
    decay = [jnp.exp(jnp.where(below, gcc[i] - gcr[i], NEG)) for i in range(n)]
    kb = [k[i] * beta[i] for i in range(n)]
    a = [ops.nt(kb[i], k[i]) * decay[i] for i in range(n)]
    if t_known is None:
        p = [-a[i] for i in range(n)]
        tm = [jnp.where(r == c, 1.0, 0.0) + p[i] for i in range(n)]
        for _ in range(5):
            p = [ops.nn(p[i], p[i], hi=True) for i in range(n)]
            tm = [tm[i] + ops.nn(tm[i], p[i], hi=True) for i in range(n)]
    else:
        tm = [_given_inverse(a[i], t_known[i]) for i in range(n)]
    u_hat = [ops.nn(tm[i], v[i] * beta[i], hi=True) for i in range(n)]
    w = [ops.nn(tm[i], kb[i] * jnp.exp(gcc[i]), hi=True) for i in range(n)]
    return tuple(u_hat), tuple(w), tuple(tm)


INTRA_NB = 32
PAIR = 1


def gdn_intra_fwd(qkvn, betar5, gcr5, B, S, H, rider=None):
    T = B * S
    UNIT = PAIR * CHUNK
    N = S // UNIT
    nb = min(INTRA_NB // PAIR, N)
    rows = nb * UNIT
    ns = N // nb

    def body(k_ref, v_ref, b_ref, gr_ref, uh_ref, w_ref, t_ref):
        sls = [slice(ci * UNIT, (ci + 1) * UNIT) for ci in range(nb)]
        u_hat, w, tm = _intra_fn(tuple(k_ref[sl, :] for sl in sls), tuple(v_ref[sl, :] for sl in sls),
                                 tuple(b_ref[ci] for ci in range(nb)), tuple(gr_ref[ci] for ci in range(nb)), _RawOps)
        for ci, sl in enumerate(sls):
            uh_ref[sl, :] = u_hat[ci]
            w_ref[sl, :] = w[ci]
            t_ref[ci] = tm[ci]

    rowspec = pl.BlockSpec((None, None, nb, 1, UNIT), lambda b, h, i: (b, h, i, 0, 0))
    sqspec = pl.BlockSpec((None, None, nb, UNIT, UNIT), lambda b, h, i: (b, h, i, 0, 0))
    out = pl.BlockSpec((rows, HEAD), lambda b, h, i: (b * ns + i, h))
    return _hosted_call(
        body, rider, name="gdn_intra_fwd", grid=(B, H, ns),
        in_specs=[pl.BlockSpec((rows, HEAD), lambda b, h, i: (b * ns + i, 3 * h + 1)),
                  pl.BlockSpec((rows, HEAD), lambda b, h, i: (b * ns + i, 3 * h + 2)),
                  rowspec, rowspec],
        out_specs=[out, out, sqspec],
        out_shape=[jax.ShapeDtypeStruct((T, H * HEAD), F32), jax.ShapeDtypeStruct((T, H * HEAD), F32),
                   jax.ShapeDtypeStruct((B, H, N, UNIT, UNIT), F32)],
        scratch_shapes=[], semantics=("parallel", "parallel", "parallel"),
    )(qkvn, qkvn, betar5, gcr5)


def gdn_intra_bwd(qkvn, betar5, gcr5, t_inv, d_uh, d_w, dq_in, dk_in, B, S, H):
    T = B * S
    UNIT = PAIR * CHUNK
    N = S // UNIT
    nb = min(INTRA_NB // PAIR, N)
    rows = nb * UNIT
    ns = N // nb

    def body(k_ref, v_ref, b_ref, gr_ref, t_ref, duh_ref, dw_ref, dq_ref, dk_ref, o_ref, db_ref, dgr_ref):
        sls = [slice(ci * UNIT, (ci + 1) * UNIT) for ci in range(nb)]
        chunks = range(nb)
        _, vjp = jax.vjp(
            functools.partial(_intra_fn, ops=_DiffOps, t_known=tuple(t_ref[ci] for ci in chunks)),
            tuple(k_ref[sl, :] for sl in sls), tuple(v_ref[sl, :] for sl in sls), tuple(b_ref[ci] for ci in chunks),
            tuple(gr_ref[ci] for ci in chunks))
        zero = jnp.zeros((UNIT, UNIT), F32)
        dk, dv, db, dgr = vjp((tuple(duh_ref[sl, :] for sl in sls), tuple(dw_ref[sl, :] for sl in sls),
                               tuple(zero for _ in chunks)))
        for ci, sl in enumerate(sls):
            o_ref[sl, 0:HEAD] = dq_ref[sl, :]
            o_ref[sl, HEAD:2 * HEAD] = dk[ci] + dk_ref[sl, :]
            o_ref[sl, 2 * HEAD:3 * HEAD] = dv[ci]
            db_ref[ci] = db[ci]
            dgr_ref[ci] = dgr[ci]

    rowspec = pl.BlockSpec((None, None, nb, 1, UNIT), lambda b, h, i: (b, h, i, 0, 0))
    sqspec = pl.BlockSpec((None, None, nb, UNIT, UNIT), lambda b, h, i: (b, h, i, 0, 0))
    head = pl.BlockSpec((rows, HEAD), lambda b, h, i: (b * ns + i, h))
    return pl.pallas_call(
        body, name="gdn_intra_bwd", grid=(B, H, ns),
        in_specs=[pl.BlockSpec((rows, HEAD), lambda b, h, i: (b * ns + i, 3 * h + 1)),
                  pl.BlockSpec((rows, HEAD), lambda b, h, i: (b * ns + i, 3 * h + 2)),
                  rowspec, rowspec, sqspec, head, head, head, head],
        out_specs=[pl.BlockSpec((rows, 3 * HEAD), lambda b, h, i: (b * ns + i, h)), rowspec, rowspec],
        out_shape=[jax.ShapeDtypeStruct((T, 3 * H * HEAD), F32),
                   jax.ShapeDtypeStruct((B, H, N, 1, UNIT), F32), jax.ShapeDtypeStruct((B, H, N, 1, UNIT), F32)],
        compiler_params=_params("parallel", "parallel", "parallel"),
    )(qkvn, qkvn, betar5, gcr5, t_inv, d_uh, d_w, dq_in, dk_in)


def _inter_fn(q, k, u_hat, w, gcr, state, ops):
    n = len(q)
    r = lax.broadcasted_iota(jnp.int32, (CHUNK, CHUNK), 0)
    c = lax.broadcasted_iota(jnp.int32, (CHUNK, CHUNK), 1)
    last = lax.broadcasted_iota(jnp.int32, (1, CHUNK), 1) == CHUNK - 1
    gcc = [_to_col(gcr[i]) for i in range(n)]
    gl = [jnp.sum(jnp.where(last, gcr[i], 0.0), axis=1, keepdims=True) for i in range(n)]
    decay = [jnp.exp(jnp.where(r >= c, gcc[i] - gcr[i], NEG)) for i in range(n)]
    qs = [q[i] * (HEAD ** -0.5) for i in range(n)]
    ws = [ops.nn(w[i], state[i]) for i in range(n)]
    qst = [ops.nn(qs[i] * jnp.exp(gcc[i]), state[i]) for i in range(n)]
    attn = [ops.nt(qs[i], k[i]) * decay[i] for i in range(n)]
    u = [u_hat[i] - ws[i] for i in range(n)]
    o = [qst[i] + ops.nn(attn[i], u[i]) for i in range(n)]
    kdu = [ops.tn(k[i] * jnp.exp(gl[i] - gcc[i]), u[i]) for i in range(n)]
    new_state = [state[i] * jnp.exp(gl[i]) + kdu[i] for i in range(n)]
    return tuple(o), tuple(new_state)


INTER_HEADS = 8
INTER_ROWS = 512
INTER_ROWS_BWD = 256


def _inter_heads(H):
    return INTER_HEADS if H % INTER_HEADS == 0 else (4 if H % 4 == 0 else 1)


def _inter_specs(ts, ns, hp, backward):
    at = (lambda s: ns - 1 - s) if backward else (lambda s: s)
    nc = ts // CHUNK
    qk = []
    for hh in range(hp):
        qk.append(pl.BlockSpec((ts, HEAD), lambda b, g, s, hh=hh: (b * ns + at(s), 3 * (hp * g + hh))))
        qk.append(pl.BlockSpec((ts, HEAD), lambda b, g, s, hh=hh: (b * ns + at(s), 3 * (hp * g + hh) + 1)))
    heads = pl.BlockSpec((ts, hp * HEAD), lambda b, g, s: (b * ns + at(s), g))
    rowspec = pl.BlockSpec((None, hp, nc, 1, CHUNK), lambda b, g, s: (b, g, at(s), 0, 0))
    stspec = pl.BlockSpec((None, hp, nc, HEAD, HEAD), lambda b, g, s: (b, g, at(s), 0, 0))
    return qk, heads, rowspec, stspec


def gdn_inter_fwd(qkvn, u_hat, w, gcr5, p_gz, gnorm, B, S, H):
    T = B * S
    N = S // CHUNK
    hp = _inter_heads(H)
    hs = range(hp)
    ts = _tile(S, INTER_ROWS, CHUNK)
    ns, nc = S // ts, ts // CHUNK

    def body(*refs):
        qk_refs, (uh_ref, w_ref, gr_ref, z_ref, gn_ref, o_ref, st_ref, y_ref, s_scr) = refs[:2 * hp], refs[2 * hp:]

        @pl.when(pl.program_id(2) == 0)
        def _():
            s_scr[...] = jnp.zeros_like(s_scr)

        gn = gn_ref[...]

        def step(n, c):
            rows = pl.ds(pl.multiple_of(n * CHUNK, CHUNK), CHUNK)
            st = tuple(s_scr[hh] for hh in hs)
            for hh in hs:
                st_ref[hh, n] = st[hh]
            o, new = _inter_fn(tuple(qk_refs[2 * hh][rows, :] for hh in hs), tuple(qk_refs[2 * hh + 1][rows, :] for hh in hs),
                               tuple(uh_ref[rows, hh * HEAD:(hh + 1) * HEAD] for hh in hs),
                               tuple(w_ref[rows, hh * HEAD:(hh + 1) * HEAD] for hh in hs),
                               tuple(gr_ref[hh, n] for hh in hs), st, _RawOps)
            for hh in hs:
                cols = slice(hh * HEAD, (hh + 1) * HEAD)
                o_ref[rows, cols] = o[hh]
                s_scr[hh] = new[hh]
                z = z_ref[rows, cols]
                r = lax.rsqrt(jnp.mean(o[hh] * o[hh], axis=-1, keepdims=True) + EPS)
                y_ref[rows, cols] = (o[hh] * r * gn * z * _sigmoid(z)).astype(BF16)
            return c

        lax.fori_loop(0, nc, step, 0)

    qk, heads, rowspec, stspec = _inter_specs(ts, ns, hp, backward=False)
    return pl.pallas_call(
        body, name="gdn_inter_fwd", grid=(B, H // hp, ns),
        in_specs=qk + [heads, heads, rowspec, heads, pl.BlockSpec((1, HEAD), lambda b, g, s: (0, 0))],
        out_specs=[heads, stspec, heads],
        out_shape=[jax.ShapeDtypeStruct((T, H * HEAD), F32), jax.ShapeDtypeStruct((B, H, N, HEAD, HEAD), F32),
                   jax.ShapeDtypeStruct((T, H * HEAD), BF16)],
        scratch_shapes=[pltpu.VMEM((hp, HEAD, HEAD), F32)],
        compiler_params=_params("parallel", "parallel", "arbitrary"),
    )(*([qkvn] * (2 * hp)), u_hat, w, gcr5, p_gz, gnorm)


def gdn_inter_bwd(qkvn, u_hat, w, gcr5, states, o, p_gz, gnorm, d_y, B, S, H, rider=None):
    T = B * S
    N = S // CHUNK
    hp = _inter_heads(H)
    hs = range(hp)
    ts = _tile(S, INTER_ROWS_BWD, CHUNK)
    ns, nc = S // ts, ts // CHUNK

    def body(*refs):
        qk_refs = refs[:2 * hp]
        (uh_ref, w_ref, gr_ref, st_ref, o_ref, z_ref, gn_ref, dy_ref,
         dq_ref, dk_ref, duh_ref, dw_ref, dgr_ref, dz_ref, dgn_ref, ds_scr) = refs[2 * hp:]

        @pl.when(pl.program_id(2) == 0)
        def _():
            ds_scr[...] = jnp.zeros_like(ds_scr)
            dgn_ref[...] = jnp.zeros_like(dgn_ref)

        cols = [slice(hh * HEAD, (hh + 1) * HEAD) for hh in hs]
        gn = gn_ref[...]

        def through_norm(rows, hh):
            ov, z, d = o_ref[rows, cols[hh]], z_ref[rows, cols[hh]], dy_ref[rows, cols[hh]]
            r = lax.rsqrt(jnp.mean(ov * ov, axis=-1, keepdims=True) + EPS)
            xh = ov * r
            sg = _sigmoid(z)
            d_n = d * (z * sg)
            dz_ref[rows, cols[hh]] = (d * xh * gn * (sg + z * sg * (1.0 - sg))).astype(BF16)
            dgn_ref[0:1, :] += jnp.sum(d_n * xh, axis=0, keepdims=True)
            dxh = d_n * gn
            return r * (dxh - xh * jnp.mean(dxh * xh, axis=-1, keepdims=True))

        def step(i, c):
            n = nc - 1 - i
            rows = pl.ds(pl.multiple_of(n * CHUNK, CHUNK), CHUNK)
            _, vjp = jax.vjp(functools.partial(_inter_fn, ops=_DiffOps),
                             tuple(qk_refs[2 * hh][rows, :] for hh in hs), tuple(qk_refs[2 * hh + 1][rows, :] for hh in hs),
                             tuple(uh_ref[rows, cols[hh]] for hh in hs), tuple(w_ref[rows, cols[hh]] for hh in hs),
                             tuple(gr_ref[hh, n] for hh in hs), tuple(st_ref[hh, n] for hh in hs))
            dq, dk, duh, dw, dgr, ds = vjp((tuple(through_norm(rows, hh) for hh in hs), tuple(ds_scr[hh] for hh in hs)))
            for hh in hs:
                dq_ref[rows, cols[hh]] = dq[hh]
                dk_ref[rows, cols[hh]] = dk[hh]
                duh_ref[rows, cols[hh]] = duh[hh]
                dw_ref[rows, cols[hh]] = dw[hh]
                dgr_ref[hh, n] = dgr[hh]
                ds_scr[hh] = ds[hh]
            return c

        lax.fori_loop(0, nc, step, 0)

    qk, heads, rowspec, stspec = _inter_specs(ts, ns, hp, backward=True)
    hshape = jax.ShapeDtypeStruct((T, H * HEAD), F32)
    return _hosted_call(
        body, rider, name="gdn_inter_bwd", grid=(B, H // hp, ns),
        in_specs=qk + [heads, heads, rowspec, stspec, heads, heads, pl.BlockSpec((1, HEAD), lambda b, g, s: (0, 0)), heads],
        out_specs=[heads, heads, heads, heads, rowspec, heads,
                   pl.BlockSpec((None, None, 8, HEAD), lambda b, g, s: (b, g, 0, 0))],
        out_shape=[hshape, hshape, hshape, hshape, jax.ShapeDtypeStruct((B, H, N, 1, CHUNK), F32),
                   jax.ShapeDtypeStruct((T, H * HEAD), BF16), jax.ShapeDtypeStruct((B, H // hp, 8, HEAD), F32)],
        scratch_shapes=[pltpu.VMEM((hp, HEAD, HEAD), F32)], semantics=("parallel", "parallel", "arbitrary"),
    )(*([qkvn] * (2 * hp)), u_hat, w, gcr5, states, o, p_gz, gnorm, d_y)


def adamw(w, g, m, v, name):
    shape = w.shape
    lead = (None,) * (w.ndim - 2)
    zeros = (0,) * (w.ndim - 2)
    R, C = shape[-2:]
    g2 = g.reshape(R, C)
    tr, tc = _tile(R, 128, 8), C
    if tr % 8 and R > 8:
        tr, tc = R, _tile(C, 128)

    def body(w_ref, g_ref, m_ref, v_ref, d_ref, nm_ref, nv_ref):
        gv = g_ref[...]
        nm = ADAM_B1 * m_ref[...] + (1.0 - ADAM_B1) * gv
        nv = ADAM_B2 * v_ref[...] + (1.0 - ADAM_B2) * (gv * gv)
        m_hat = nm / (1.0 - ADAM_B1 ** ADAM_STEP)
        v_hat = nv / (1.0 - ADAM_B2 ** ADAM_STEP)
        d_ref[...] = -ADAM_LR * (m_hat / (jnp.sqrt(v_hat) + ADAM_EPS) + ADAM_WD * w_ref[...])
        nm_ref[...] = nm
        nv_ref[...] = nv

    blk = pl.BlockSpec(lead + (tr, tc), lambda i, j: zeros + (i, j))
    gblk = pl.BlockSpec((tr, tc), lambda i, j: (i, j))
    sh = jax.ShapeDtypeStruct(shape, F32)
    return pl.pallas_call(
        body, name=name, grid=(R // tr, C // tc), in_specs=[blk, gblk, blk, blk], out_specs=[blk] * 3, out_shape=[sh] * 3,
        compiler_params=_params("parallel", "parallel"),
    )(w, g2, m, v)


def _place():
    x, y, c = lax.axis_index("x"), lax.axis_index("y"), lax.axis_index("c")
    chips = [(1 - x, y), (x, 1 - y), (1 - x, 1 - y)]
    return x, y, c, chips


_HBM = pl.BlockSpec(memory_space=pltpu.HBM)


def allgather_weights(packs):
    n = len(packs)

    def body(*refs):
        in_refs, out_refs, (send_sems, recv_sems) = refs[:n], refs[n:2 * n], refs[2 * n:]
        x, y, c, chips = _place()
        me_s = 2 * x + y
        me, sibling = (x, y, c), (x, y, 1 - c)
        shards = [2 * chip[0] + chip[1] for chip in chips]

        def copy(a, k, shard, half, to, src=None):
            dst = out_refs[a].at[shard, half]
            return pltpu.make_async_remote_copy(src_ref=dst if src is None else src, dst_ref=dst,
                                                send_sem=send_sems.at[6 * a + k], recv_sem=recv_sems.at[6 * a + k],
                                                device_id=to, device_id_type=MESH)

        first = [copy(a, j, me_s, c, (*chip, c), src=in_refs[a].at[c]) for a in range(n) for j, chip in enumerate(chips)]
        for cp in first:
            cp.start()
        passed = []
        for a in range(n):
            for j in range(3):
                copy(a, j, shards[j], c, me).wait_recv()
                passed.append(copy(a, 3 + j, shards[j], c, sibling))
                passed[-1].start()
        for a in range(n):
            for j in range(3):
                copy(a, 3 + j, shards[j], 1 - c, me).wait_recv()
        for cp in first + passed:
            cp.wait_send()

    return pl.pallas_call(
        body, name="allgather_weights", in_specs=[_HBM] * n, out_specs=[_HBM] * n,
        out_shape=[jax.ShapeDtypeStruct((N_CHIP,) + p.shape, p.dtype) for p in packs],
        scratch_shapes=[pltpu.SemaphoreType.DMA((6 * n,)), pltpu.SemaphoreType.DMA((6 * n,))],
    )(*packs)


class _Rider:
    def __init__(self, inputs, out_shapes, n_sems, sends, recvs, aliases=None):
        self.inputs, self.out_shapes, self.n_sems = list(inputs), list(out_shapes), n_sems
        self.sends, self.recvs, self.aliases = sends, recvs, aliases or {}

    def start(self, *refs):
        for cp in self.sends(*refs):
            cp.start()

    def wait(self, *refs):
        for cp in self.recvs(*refs):
            cp.wait_recv()
        for cp in self.sends(*refs):
            cp.wait_send()


def _remote(src, dst, send_sems, recv_sems, k, to):
    return pltpu.make_async_remote_copy(src_ref=src, dst_ref=dst, send_sem=send_sems.at[k], recv_sem=recv_sems.at[k],
                                        device_id=to, device_id_type=MESH)


def _run_alone(rider, name):
    ri = len(rider.inputs)

    def body(*refs):
        ins, outs, (send_sems, recv_sems) = refs[:ri], refs[ri:-2], refs[-2:]
        rider.start(ins, outs, send_sems, recv_sems)
        rider.wait(ins, outs, send_sems, recv_sems)

    return pl.pallas_call(
        body, name=name, in_specs=[_HBM] * ri, out_specs=[_HBM] * len(rider.out_shapes), out_shape=rider.out_shapes,
        scratch_shapes=[pltpu.SemaphoreType.DMA((rider.n_sems,))] * 2, input_output_aliases=rider.aliases,
    )(*rider.inputs)


def _hosted_call(body, rider, *, name, grid, in_specs, out_specs, out_shape, scratch_shapes, semantics):
    if rider is None:
        return pl.pallas_call(body, name=name, grid=grid, in_specs=in_specs, out_specs=out_specs, out_shape=out_shape,
                              scratch_shapes=scratch_shapes, compiler_params=_params(*semantics))
    n_in, n_out, n_scr = len(in_specs), len(out_specs), len(scratch_shapes)
    ri, ro = len(rider.inputs), len(rider.out_shapes)

    def hosted(*refs):
        parts, p = [], 0
        for cnt in (n_in, ri, n_out, ro, n_scr, 2):
            parts.append(refs[p:p + cnt])
            p += cnt
        ins, rins, outs, routs, scr, (send_sems, recv_sems) = parts
        first = functools.reduce(jnp.logical_and, [pl.program_id(a) == 0 for a in range(len(grid))])
        last = functools.reduce(jnp.logical_and, [pl.program_id(a) == grid[a] - 1 for a in range(len(grid))])

        @pl.when(first)
        def _():
            rider.start(rins, routs, send_sems, recv_sems)

        body(*ins, *outs, *scr)

        @pl.when(last)
        def _():
            rider.wait(rins, routs, send_sems, recv_sems)

    call = pl.pallas_call(
        hosted, name=name, grid=grid, in_specs=list(in_specs) + [_HBM] * ri, out_specs=list(out_specs) + [_HBM] * ro,
        out_shape=list(out_shape) + rider.out_shapes,
        scratch_shapes=list(scratch_shapes) + [pltpu.SemaphoreType.DMA((rider.n_sems,))] * 2,
        input_output_aliases={n_in + i: n_out + o for i, o in rider.aliases.items()},
        compiler_params=_params(*(("arbitrary",) * len(grid))))

    def run(*args):
        res = call(*args, *rider.inputs)
        return res[:n_out], res[n_out:]

    return run


def _ride_gather_ici(packs):
    n = len(packs)

    def sends(ins, outs, send_sems, recv_sems):
        x, y, c, chips = _place()
        return [_remote(ins[a].at[c], outs[a].at[2 * x + y, c], send_sems, recv_sems, 3 * a + j, (*chip, c))
                for a in range(n) for j, chip in enumerate(chips)]

    def recvs(ins, outs, send_sems, recv_sems):
        x, y, c, chips = _place()
        return [_remote(ins[a].at[c], outs[a].at[2 * chip[0] + chip[1], c], send_sems, recv_sems, 3 * a + j, (x, y, c))
                for a in range(n) for j, chip in enumerate(chips)]

    return _Rider(packs, [jax.ShapeDtypeStruct((N_CHIP,) + p.shape, p.dtype) for p in packs], 3 * n, sends, recvs)


def _ride_gather_d2d(gathered):
    n = len(gathered)

    def copies(landing_half, to):
        def build(ins, outs, send_sems, recv_sems):
            x, y, c, chips = _place()
            return [_remote(ins[a].at[2 * chip[0] + chip[1], c], outs[a].at[2 * chip[0] + chip[1], landing_half(c)],
                            send_sems, recv_sems, 3 * a + j, to(x, y, c))
                    for a in range(n) for j, chip in enumerate(chips)]
        return build

    return _Rider(gathered, [jax.ShapeDtypeStruct(g.shape, g.dtype) for g in gathered], 3 * n,
                  copies(lambda c: c, lambda x, y, c: (x, y, 1 - c)), copies(lambda c: 1 - c, lambda x, y, c: (x, y, c)),
                  aliases={a: a for a in range(n)})


def _ride_exchange(gs):
    n = len(gs)

    def copies(ins, outs, send_sems, recv_sems):
        x, y, c, _ = _place()
        return [_remote(ins[a].at[1 - c], outs[a], send_sems, recv_sems, a, (x, y, 1 - c)) for a in range(n)]

    return _Rider(gs, [jax.ShapeDtypeStruct(g.shape[1:], g.dtype) for g in gs], n, copies, copies)


def _ride_scatter(b16s):
    n = len(b16s)

    def sends(ins, outs, send_sems, recv_sems):
        x, y, c, chips = _place()
        return [_remote(ins[a].at[2 * chip[0] + chip[1]], outs[a].at[2 * x + y], send_sems, recv_sems, 3 * a + j, (*chip, c))
                for a in range(n) for j, chip in enumerate(chips)]

    def recvs(ins, outs, send_sems, recv_sems):
        x, y, c, chips = _place()
        return [_remote(ins[a].at[2 * x + y], outs[a].at[2 * chip[0] + chip[1]], send_sems, recv_sems, 3 * a + j, (x, y, c))
                for a in range(n) for j, chip in enumerate(chips)]

    return _Rider(b16s, [jax.ShapeDtypeStruct(b.shape, b.dtype) for b in b16s], 3 * n, sends, recvs)


def _slab_tile(r, cols):
    tr = _tile(r, 256, 16)
    if tr % 16 == 0:
        return tr, cols
    return r, _tile(cols, 128)


def add_halves(g, got, idx, name):
    _, ns, r, cols = g.shape
    tr, tc = _slab_tile(r, cols)

    def body(idx_ref, a_ref, b_ref, o32_ref, o16_ref):
        s = a_ref[...] + b_ref[...]
        o32_ref[...] = s
        o16_ref[...] = s.astype(BF16)

    blk = pl.BlockSpec((None, tr, tc), lambda s, i, j, idx_ref: (s, i, j))
    return pl.pallas_call(
        body, name=name,
        grid_spec=pltpu.PrefetchScalarGridSpec(
            num_scalar_prefetch=1, grid=(ns, r // tr, cols // tc),
            in_specs=[pl.BlockSpec((None, None, tr, tc), lambda s, i, j, idx_ref: (idx_ref[0], s, i, j)), blk],
            out_specs=[blk, blk]),
        out_shape=[jax.ShapeDtypeStruct((ns, r, cols), F32), jax.ShapeDtypeStruct((ns, r, cols), BF16)],
        compiler_params=_params("parallel", "parallel", "parallel"),
    )(idx, g, got)


def add_chips(a32, got16, idx, name):
    ns, r, cols = a32.shape
    tr, tc = _slab_tile(r, cols)

    def body(idx_ref, a_ref, r1_ref, r2_ref, r3_ref, o_ref):
        o_ref[...] = ((a_ref[...] + r1_ref[...].astype(F32)) + r2_ref[...].astype(F32)) + r3_ref[...].astype(F32)

    def slab(k):
        return pl.BlockSpec((None, tr, tc), lambda i, j, idx_ref: ((idx_ref[1] + k) % ns, i, j))

    return pl.pallas_call(
        body, name=name,
        grid_spec=pltpu.PrefetchScalarGridSpec(
            num_scalar_prefetch=1, grid=(r // tr, cols // tc), in_specs=[slab(0), slab(1), slab(2), slab(3)],
            out_specs=pl.BlockSpec((tr, tc), lambda i, j, idx_ref: (i, j))),
        out_shape=jax.ShapeDtypeStruct((r, cols), F32),
        compiler_params=_params("parallel", "parallel"),
    )(idx, a32, got16, got16, got16)


def share_halves(halves):
    n = len(halves)

    def body(*refs):
        in_refs, out_refs, (send_sems, recv_sems) = refs[:n], refs[n:2 * n], refs[2 * n:]
        x, y, c, _ = _place()
        cps = [pltpu.make_async_remote_copy(src_ref=in_refs[a], dst_ref=out_refs[a], send_sem=send_sems.at[a],
                                            recv_sem=recv_sems.at[a], device_id=(x, y, 1 - c), device_id_type=MESH)
               for a in range(n)]
        for cp in cps:
            cp.start()
        for cp in cps:
            cp.wait()

    return pl.pallas_call(
        body, name="share_halves", in_specs=[_HBM] * n, out_specs=[_HBM] * n,
        out_shape=[jax.ShapeDtypeStruct(h.shape, F32) for h in halves],
        scratch_shapes=[pltpu.SemaphoreType.DMA((n,)), pltpu.SemaphoreType.DMA((n,))],
    )(*halves)


def allreduce_small(v):
    R, _ = v.shape

    def body(in_ref, out_ref, slots, send_sems, recv_sems):
        x, y, c, _ = _place()
        me = 4 * x + 2 * y + c
        slots[me] = in_ref[...]
        cps = []
        for k in range(1, N_DEV):
            to = (x ^ (k >> 2), y ^ ((k >> 1) & 1), c ^ (k & 1))
            cps.append(pltpu.make_async_remote_copy(src_ref=in_ref, dst_ref=slots.at[me], send_sem=send_sems.at[k - 1],
                                                    recv_sem=recv_sems.at[k - 1], device_id=to, device_id_type=MESH))
        for cp in cps:
            cp.start()
        for k in range(1, N_DEV):
            frm = 4 * (x ^ (k >> 2)) + 2 * (y ^ ((k >> 1) & 1)) + (c ^ (k & 1))
            pltpu.make_async_remote_copy(src_ref=in_ref, dst_ref=slots.at[frm], send_sem=send_sems.at[k - 1],
                                         recv_sem=recv_sems.at[k - 1], device_id=(x, y, c), device_id_type=MESH).wait_recv()
        for cp in cps:
            cp.wait_send()
        acc = slots[0]
        for d in range(1, N_DEV):
            acc = acc + slots[d]
        out_ref[...] = acc

    vm = pl.BlockSpec(memory_space=pltpu.VMEM)
    return pl.pallas_call(
        body, name="allreduce_small", in_specs=[vm], out_specs=vm, out_shape=jax.ShapeDtypeStruct((R, ROW), F32),
        scratch_shapes=[pltpu.VMEM((N_DEV, R, ROW), F32), pltpu.SemaphoreType.DMA((N_DEV - 1,)),
                        pltpu.SemaphoreType.DMA((N_DEV - 1,))],
    )(v)


def _rows_of(n, unit=16):
    return -(-n // (unit * ROW)) * unit


def _pack_rows(items, total_rows, dtype, unit=16):
    parts = []
    used = 0
    for a in items:
        flat = a.reshape(-1)
        r = _rows_of(flat.shape[0], unit)
        flat = jnp.pad(flat, (0, r * ROW - flat.shape[0]))
        parts.append(flat.reshape(r, ROW))
        used += r
    if total_rows > used:
        parts.append(jnp.zeros((total_rows - used, ROW), dtype))
    return jnp.concatenate(parts, axis=0)


def _unpack_rows(buf, shapes, unit=16):
    lead = buf.shape[:-2]
    out = []
    off = 0
    for shp in shapes:
        n = math.prod(shp)
        r = _rows_of(n, unit)
        piece = buf[..., off:off + r, :].reshape(*lead, r * ROW)[..., :n].reshape(*lead, *shp)
        out.append(piece)
        off += r
    return out


def _interleave_heads(w, H):
    lead = w.shape[:-1]
    return w.reshape(*lead, 3, H, HEAD).swapaxes(-3, -2).reshape(*lead, 3 * H * HEAD)


def _deinterleave_heads(w, H):
    lead = w.shape[:-1]
    return w.reshape(*lead, H, 3, HEAD).swapaxes(-3, -2).reshape(*lead, 3 * H * HEAD)


def _interleave_head_rows(w, H):
    return w.reshape(3, H, HEAD, w.shape[-1]).swapaxes(0, 1).reshape(3 * H * HEAD, w.shape[-1])


def _deinterleave_head_rows(w, H):
    return w.reshape(H, 3, HEAD, w.shape[-1]).swapaxes(0, 1).reshape(3 * H * HEAD, w.shape[-1])


def kernel(x, norm_mix, w_in, fox_f_bias, gdn_conv_w, gdn_a_log, gdn_dt_bias, gdn_norm, w_branch_fox, w_branch_gdn, w_out, norm_ffn, w_up, ffn_conv_w, w_down, norm_final, loss_target, m_norm_mix, m_w_in, m_fox_f_bias, m_gdn_conv_w, m_gdn_a_log, m_gdn_dt_bias, m_gdn_norm, m_w_branch_fox, m_w_branch_gdn, m_w_out, m_norm_ffn, m_w_up, m_ffn_conv_w, m_w_down, m_norm_final, v_norm_mix, v_w_in, v_fox_f_bias, v_gdn_conv_w, v_gdn_a_log, v_gdn_dt_bias, v_gdn_norm, v_w_branch_fox, v_w_branch_gdn, v_w_out, v_norm_ffn, v_w_up, v_ffn_conv_w, v_w_down, v_norm_final):
    B, S, D = x.shape
    T = B * S
    H = D // HEAD
    N = S // CHUNK
    FF = w_down.shape[1] * N_CHIP
    d_in = 9 * D + 3 * H
    assert w_in.shape[2] * N_CHIP == d_in and 3 * H <= 128

    cidx = lax.axis_index("c").astype(jnp.int32)
    sidx = (2 * lax.axis_index("x") + lax.axis_index("y")).astype(jnp.int32)
    idx = jnp.stack([cidx, sidx])

    rowed = [w_branch_fox[0], w_branch_gdn[0], w_out[0], w_down[0]]
    convs = [gdn_conv_w[0], ffn_conv_w[0]]
    rowed_shapes = [a.shape for a in rowed]
    conv_shapes = [a.shape + (2,) for a in convs]
    pad_rows = lambda shapes: -(-sum(_rows_of(math.prod(s)) for s in shapes) // 256) * 128
    Rh, Rc = pad_rows(rowed_shapes), pad_rows(conv_shapes)
    halves = lambda a: a.reshape(2, a.shape[0] // 2, a.shape[1])
    c_in = w_in.shape[2]
    packs_a = [w_in[0].T.astype(BF16).reshape(c_in, 2, D // 2).transpose(1, 0, 2),
               halves(_pack_rows([lax.bitcast_convert_type(a, BF16) for a in convs], 2 * Rc, BF16))]
    packs_b = [halves(w_up[0].astype(BF16)), halves(_pack_rows([a.astype(BF16) for a in rowed], 2 * Rh, BF16))]
    own = lambda gs, ps: [lax.dynamic_update_slice(g, p[None], (sidx, 0, 0, 0)) for g, p in zip(gs, ps)]
    by_cols = lambda g: g.transpose(1, 2, 0, 3).reshape(2 * g.shape[2], N_CHIP * g.shape[3])
    cat_cols = lambda p: jnp.concatenate([p[i] for i in range(N_CHIP)], axis=-1)
    cat_rows = lambda p: p.reshape(-1, p.shape[-1])
    g_in, g_conv = own(allgather_weights(packs_a), packs_a)
    W_inT = g_in.transpose(0, 2, 1, 3).reshape(N_CHIP * c_in, D)
    conv_parts = _unpack_rows(g_conv.reshape(N_CHIP, 2 * Rc, ROW), conv_shapes)
    gconv = cat_cols(lax.bitcast_convert_type(conv_parts[0], F32))
    fconv = cat_cols(lax.bitcast_convert_type(conv_parts[1], F32))

    o1, o2 = 3 * D, 3 * D + H
    o3, o4, o5, o6 = o2 + 3 * D, o2 + 3 * D + H, o2 + 3 * D + 2 * H, o2 + 4 * D + 2 * H
    W_foxT = _interleave_head_rows(W_inT[:o1], H)
    W_gqkvT = _interleave_head_rows(W_inT[o2:o3], H)
    W_gzT = W_inT[o5:o6]
    W_gatesT = W_inT[o6:]
    W_smallT = jnp.concatenate([W_inT[o1:o2], W_inT[o3:o5], jnp.zeros((128 - 3 * H, D), BF16)], axis=0)
    gconv_i = _interleave_heads(gconv, H)
    fconv_g, fconv_v = fconv[:, :FF], fconv[:, FF:]
    prm = jnp.zeros((8, 128), F32)
    prm = prm.at[0, 0:H].set(fox_f_bias[0]).at[0, H:2 * H].set(gdn_dt_bias[0]).at[1, H:2 * H].set(gdn_a_log[0])

    x2 = x.reshape(T, D)
    tgt = loss_target.reshape(T, D)

    hn1 = rmsnorm_fwd(x2, norm_mix, "rmsnorm_mix")
    p_fox = matmul(hn1, W_foxT, "nt", "proj_fox", out_dtype=BF16)
    p_gqkv = matmul(hn1, W_gqkvT, "nt", "proj_gqkv")
    p_gz = matmul(hn1, W_gzT, "nt", "proj_gz")
    p_gates = matmul(hn1, W_gatesT, "nt", "proj_gates")
    p_small = matmul(hn1, W_smallT, "nt", "proj_small")

    sm = small_fwd(p_small, prm, B, S, H)
    heads = lambda a: a.reshape(B, S, H).transpose(0, 2, 1)
    c_bhs, gc_bhs, beta_bhs = heads(sm[:, 0:H]), heads(sm[:, H:2 * H]), heads(sm[:, 2 * H:3 * H])
    c_col, c_row = c_bhs[..., None], c_bhs[:, :, None, :]
    gcr5 = gc_bhs.reshape(B, H, N, 1, CHUNK)
    gcr_u = gc_bhs.reshape(B, H, N // PAIR, 1, PAIR * CHUNK)
    betar_u = beta_bhs.reshape(B, H, N // PAIR, 1, PAIR * CHUNK)

    (o_fox, o_fox16, lse), arriving = fox_fwd(p_fox, c_col, c_row, B, S, H, rider=_ride_gather_ici(packs_b))
    qkvn = gdn_prep_fwd(p_gqkv, gconv_i, B, S, H)
    (u_hat, w_t, t_inv), arrived = gdn_intra_fwd(qkvn, betar_u, gcr_u, B, S, H, rider=_ride_gather_d2d(arriving))
    g_up, g_rowed = own(arrived, packs_b)
    W_up = by_cols(g_up)
    W_up_g, W_up_v = W_up[:, :FF], W_up[:, FF:]
    W_bf, W_bg, W_out, W_down = (cat_rows(p) for p in _unpack_rows(g_rowed.reshape(N_CHIP, 2 * Rh, ROW), rowed_shapes))
    o_gdn, states, y_gdn = gdn_inter_fwd(qkvn, u_hat, w_t, gcr5, p_gz, gdn_norm, B, S, H)
    bf_ = matmul(o_fox16, W_bf, "nn", "branch_fox")
    bg_, y = matmul(y_gdn, W_bg, "nn", "branch_gdn", post=_post_merge(p_gates, bf_))
    h1, hn2 = matmul(y, W_out, "nn", "out_proj", add=x2, post=_post_rmsnorm(norm_ffn))
    up_g = matmul(hn2, W_up_g, "nn", "up_gate")
    up_v = matmul(hn2, W_up_v, "nn", "up_val")
    act = ffn_gate_fwd(up_g, up_v, fconv_g, fconv_v, B, S)
    dh2, dh2_16, loss_cols, d_norm_final = matmul(act, W_down, "nn", "down_proj", add=h1,
                                                  post=_post_loss(norm_final.reshape(1, D), tgt))
    loss = lax.psum(0.5 * jnp.sum(loss_cols) / D, ("x", "y", "c"))

    d_act = matmul(dh2_16, W_down, "nt", "d_act")
    dW_down = matmul(act, dh2_16, "tn", "dw_down")
    d_upg, d_upv, d_fconv_g, d_fconv_v = ffn_gate_bwd(up_g, up_v, fconv_g, fconv_v, d_act, B, S)
    d_hn2 = matmul(d_upg, W_up_g, "nt", "d_hn2_g")
    dh1, dh1_16, d_norm_ffn = matmul(d_upv, W_up_v, "nt", "d_hn2_v", add=d_hn2,
                                     post=_post_rmsnorm_bwd(h1, norm_ffn, dh2, True))
    dW_up = jnp.concatenate([matmul(hn2, d_upg, "tn", "dw_up_g"), matmul(hn2, d_upv, "tn", "dw_up_v")], axis=1)
    d_bf, d_bg, d_gates = matmul(dh1_16, W_out, "nt", "d_y", post=_post_merge_bwd(p_gates, bf_, bg_))
    dW_out = matmul(y, dh1_16, "tn", "dw_out")
    d_ofox = matmul(d_bf, W_bf, "nt", "d_ofox")
    dW_bf = matmul(o_fox16, d_bf, "tn", "dw_bf")
    d_ygdn = matmul(d_bg, W_bg, "nt", "d_ygdn")
    dW_bg = matmul(y_gdn, d_bg, "tn", "dw_bg")

    d_fconv = jnp.concatenate([d_fconv_g, d_fconv_v], axis=1)
    col_shard = lambda g, s: g[:, s * (g.shape[1] // N_CHIP):(s + 1) * (g.shape[1] // N_CHIP)]
    row_shard = lambda g, s: g[s * (g.shape[0] // N_CHIP):(s + 1) * (g.shape[0] // N_CHIP)]
    shard_items = lambda s: [row_shard(dW_bf, s), row_shard(dW_bg, s), row_shard(dW_out, s), row_shard(dW_down, s),
                             col_shard(d_fconv, s)]
    g_shapes = [a.shape for a in shard_items(0)]
    assert sum(_rows_of(math.prod(s)) for s in g_shapes) <= 2 * Rh
    to_slabs = lambda g: g.reshape(2, g.shape[0] // 2, N_CHIP, g.shape[1] // N_CHIP).transpose(0, 2, 1, 3)
    gpacks_b = [to_slabs(dW_up),
                jnp.stack([_pack_rows(shard_items(s), 2 * Rh, F32).reshape(2, Rh, ROW) for s in range(N_CHIP)], axis=1)]
    (d_pfox, d_ccol, d_crow), gots_b = fox_bwd(p_fox, c_col, c_row, o_fox, lse, d_ofox, B, S, H,
                                              rider=_ride_exchange(gpacks_b))
    sums_b = [add_halves(g, got, idx, "add_halves_b%d" % i) for i, (g, got) in enumerate(zip(gpacks_b, gots_b))]

    (dq_i, dk_i, d_uh, d_wt, dgcr_a, d_gz, d_gn_parts), got16_b = gdn_inter_bwd(
        qkvn, u_hat, w_t, gcr5, states, o_gdn, p_gz, gdn_norm, d_ygdn, B, S, H,
        rider=_ride_scatter([s16 for _, s16 in sums_b]))
    d_gdn_norm = jnp.sum(d_gn_parts[:, :, 0, :], axis=(0, 1))[None]
    mine_b = [add_chips(s32, g16, idx, "add_chips_b%d" % i) for i, ((s32, _), g16) in enumerate(zip(sums_b, got16_b))]
    d_qkvn, d_betar5, dgcr_b = gdn_intra_bwd(qkvn, betar_u, gcr_u, t_inv, d_uh, d_wt, dq_i, dk_i, B, S, H)
    d_pgqkv, d_gconv_i = gdn_prep_bwd(p_gqkv, gconv_i, d_qkvn, B, S, H)

    tokens = lambda a: a.reshape(B, H, S).transpose(0, 2, 1).reshape(T, H)
    d_gc = dgcr_a.reshape(B, H, S) + dgcr_b.reshape(B, H, S)
    d_sm = jnp.concatenate([tokens(d_ccol.reshape(B, H, S) + d_crow.reshape(B, H, S)), tokens(d_gc), tokens(d_betar5.reshape(B, H, S)),
                            jnp.zeros((T, 128 - 3 * H), F32)], axis=1)
    d_psmall, d_prm = small_bwd(p_small, prm, d_sm, B, S, H)

    dW_foxT = matmul(d_pfox, hn1, "tn", "dw_fox")
    dW_gqkvT = matmul(d_pgqkv, hn1, "tn", "dw_gqkv")
    dW_gzT = matmul(d_gz, hn1, "tn", "dw_gz")
    dW_gatesT = matmul(d_gates, hn1, "tn", "dw_gates")
    dW_smallT = matmul(d_psmall, hn1, "tn", "dw_small")
    dW_inT = jnp.concatenate([_deinterleave_head_rows(dW_foxT, H), dW_smallT[0:H], _deinterleave_head_rows(dW_gqkvT, H),
                              dW_smallT[H:3 * H], dW_gzT, dW_gatesT], axis=0)
    d_gconv = _deinterleave_heads(d_gconv_i, H)

    gpack_a = [dW_inT.reshape(N_CHIP, c_in, 2, D // 2).transpose(2, 0, 1, 3)]
    d_hn1, gots_a = matmul(d_pfox, W_foxT, "nn", "d_hn1_fox", rider=_ride_exchange(gpack_a))
    sums_a = [add_halves(gpack_a[0], gots_a[0], idx, "add_halves_a")]
    d_hn1, got16_a = matmul(d_pgqkv, W_gqkvT, "nn", "d_hn1_gqkv", add=d_hn1, rider=_ride_scatter([sums_a[0][1]]))
    mine = [add_chips(sums_a[0][0], got16_a[0], idx, "add_chips_a")] + mine_b
    d_hn1 = matmul(d_gz, W_gzT, "nn", "d_hn1_gz", add=d_hn1)
    d_hn1 = matmul(d_gates, W_gatesT, "nn", "d_hn1_gates", add=d_hn1)
    grad_x, d_norm_mix = matmul(d_psmall, W_smallT, "nn", "d_hn1_small", add=d_hn1,
                                post=_post_rmsnorm_bwd(x2, norm_mix, dh1, False))

    others = share_halves(mine)
    g_w_inT, g_up, g_rows = (jnp.concatenate([jnp.where(cidx == 0, h, o), jnp.where(cidx == 0, o, h)], axis=ax)
                             for h, o, ax in zip(mine, others, (1, 0, 0)))
    g_w_in = g_w_inT.T
    g_bf, g_bg, g_out, g_down, g_fconv = _unpack_rows(g_rows, g_shapes)

    small_items = [d_norm_mix, d_norm_ffn, d_norm_final, d_gdn_norm, d_prm, d_gconv]
    small_shapes = [a.shape for a in small_items]
    sv = allreduce_small(_pack_rows(small_items, 0, F32, unit=8))
    g_norm_mix, g_norm_ffn, g_norm_final, g_gdn_norm, g_prm, g_gconv_all = _unpack_rows(sv, small_shapes, unit=8)
    g_norm_final = g_norm_final.reshape(D)
    g_fbias, g_dtb, g_alog = g_prm[0:1, 0:H], g_prm[0:1, H:2 * H], g_prm[1:2, H:2 * H]
    g_gconv = lax.dynamic_slice_in_dim(g_gconv_all, sidx * (3 * D // N_CHIP), 3 * D // N_CHIP, axis=1)

    names = ["norm_mix", "w_in", "fox_f_bias", "gdn_conv_w", "gdn_a_log", "gdn_dt_bias", "gdn_norm", "w_branch_fox",
             "w_branch_gdn", "w_out", "norm_ffn", "w_up", "ffn_conv_w", "w_down", "norm_final"]
    ws = [norm_mix, w_in, fox_f_bias, gdn_conv_w, gdn_a_log, gdn_dt_bias, gdn_norm, w_branch_fox, w_branch_gdn, w_out,
          norm_ffn, w_up, ffn_conv_w, w_down, norm_final]
    ms = [m_norm_mix, m_w_in, m_fox_f_bias, m_gdn_conv_w, m_gdn_a_log, m_gdn_dt_bias, m_gdn_norm, m_w_branch_fox,
          m_w_branch_gdn, m_w_out, m_norm_ffn, m_w_up, m_ffn_conv_w, m_w_down, m_norm_final]
    vs = [v_norm_mix, v_w_in, v_fox_f_bias, v_gdn_conv_w, v_gdn_a_log, v_gdn_dt_bias, v_gdn_norm, v_w_branch_fox,
          v_w_branch_gdn, v_w_out, v_norm_ffn, v_w_up, v_ffn_conv_w, v_w_down, v_norm_final]
    gs = [g_norm_mix, g_w_in, g_fbias, g_gconv, g_alog, g_dtb, g_gdn_norm, g_bf, g_bg, g_out, g_norm_ffn, g_up,
          g_fconv, g_down, g_norm_final]
    gs = [g.reshape(w.shape) for g, w in zip(gs, ws)]
    deltas, new_ms, new_vs = [], [], []
    for nm, w, g, m, v in zip(names, ws, gs, ms, vs):
        if w.ndim == 1:
            d, a, b = adamw(w.reshape(1, -1), g.reshape(1, -1), m.reshape(1, -1), v.reshape(1, -1), "adamw_" + nm)
            d, a, b = d.reshape(w.shape), a.reshape(w.shape), b.reshape(w.shape)
        elif nm == "w_in":
            d, a, b = (r.T[None] for r in adamw(w[0].T, g_w_inT, m[0].T, v[0].T, "adamw_" + nm))
        else:
            d, a, b = adamw(w, g, m, v, "adamw_" + nm)
        deltas.append(d)
        new_ms.append(a)
        new_vs.append(b)

    return (loss, grad_x.reshape(B, S, D), *gs, *deltas, *new_ms, *new_vs)
```

```python
import functools
import math

import jax
import jax.numpy as jnp
from jax import lax
from jax.experimental import pallas as pl
from jax.experimental.pallas import tpu as pltpu

F32 = jnp.float32
BF16 = jnp.bfloat16
HEAD = 128
CHUNK = 64
GDN_CONV = 4
FFN_CONV = 3
EPS = 1e-6
NEG = -1e30
ROW = 1024
ATT_TILE = 512
MM_WEIGHT_TILE_BYTES = 8 << 20
N_CHIP = 4
N_DEV = 8
MESH = pl.DeviceIdType.MESH
HI = lax.Precision.HIGH
EXACT = lax.Precision.HIGHEST

ADAM_LR, ADAM_B1, ADAM_B2, ADAM_EPS, ADAM_WD, ADAM_STEP = 0.001, 0.9, 0.999, 1e-08, 0.01, 10


def _tile(n, cap, unit=128):
    best = None
    t = unit
    while t <= min(n, cap):
        if n % t == 0:
            best = t
        t += unit
    return best if best is not None else n


def _params(*sem):
    return pltpu.CompilerParams(dimension_semantics=sem)


_NN = (((1,), (0,)), ((), ()))
_NT = (((1,), (1,)), ((), ()))
_TN = (((0,), (0,)), ((), ()))


def _dg(a, b, dims, hi):
    if hi:
        return lax.dot_general(a, b, dims, precision=HI, preferred_element_type=F32)
    return lax.dot_general(a.astype(BF16), b.astype(BF16), dims, preferred_element_type=F32)


class _RawOps:
    @staticmethod
    def nn(a, b, hi=False):
        return _dg(a, b, _NN, hi)

    @staticmethod
    def nt(a, b, hi=False):
        return _dg(a, b, _NT, hi)

    @staticmethod
    def tn(a, b, hi=False):
        return _dg(a, b, _TN, hi)


def _make_diff_ops():
    def build(hi):
        @jax.custom_vjp
        def nn(a, b):
            return _dg(a, b, _NN, hi)

        nn.defvjp(lambda a, b: (_dg(a, b, _NN, hi), (a, b)),
                  lambda r, g: (_dg(g, r[1], _NT, hi), _dg(r[0], g, _TN, hi)))

        @jax.custom_vjp
        def nt(a, b):
            return _dg(a, b, _NT, hi)

        nt.defvjp(lambda a, b: (_dg(a, b, _NT, hi), (a, b)),
                  lambda r, g: (_dg(g, r[1], _NN, hi), _dg(g, r[0], _TN, hi)))

        @jax.custom_vjp
        def tn(a, b):
            return _dg(a, b, _TN, hi)

        tn.defvjp(lambda a, b: (_dg(a, b, _TN, hi), (a, b)),
                  lambda r, g: (_dg(r[1], g, _NT, hi), _dg(r[0], g, _NN, hi)))
        return nn, nt, tn

    lo, hi_ = build(False), build(True)

    class _DiffOps:
        @staticmethod
        def nn(a, b, hi=False):
            return (hi_ if hi else lo)[0](a, b)

        @staticmethod
        def nt(a, b, hi=False):
            return (hi_ if hi else lo)[1](a, b)

        @staticmethod
        def tn(a, b, hi=False):
            return (hi_ if hi else lo)[2](a, b)

    return _DiffOps


_DiffOps = _make_diff_ops()


def _sigmoid(x):
    return 1.0 / (1.0 + jnp.exp(-x))


def _mm_tile(n, pref):
    if n % pref == 0:
        return pref
    if n % 1408 == 0:
        return 1408
    return _tile(n, pref)


class _Post:
    def __init__(self, fn, row_ins=(), vec_ins=(), row_outs=(), acc_outs=(), keep_main=True):
        self.fn, self.keep_main = fn, keep_main
        self.row_ins = [r if isinstance(r, tuple) else (r, r.shape[1], 0) for r in row_ins]
        self.vec_ins, self.row_outs, self.acc_outs = list(vec_ins), list(row_outs), list(acc_outs)


def matmul(a, b, mode, name, add=None, out_dtype=F32, post=None, rider=None):
    if mode == "nn":
        (M, K), (K2, N) = a.shape, b.shape
    elif mode == "nt":
        (M, K), (N, K2) = a.shape, b.shape
    else:
        (K, M), (K2, N) = a.shape, b.shape
    assert K == K2, (name, a.shape, b.shape)
    tn = _mm_tile(N, 1024)
    if mode == "tn":
        tm = M if M <= 1408 else _mm_tile(M, 1408)
        tk = _mm_tile(K, 2048)
    else:
        tk = K if K * tn * 2 <= MM_WEIGHT_TILE_BYTES else _mm_tile(K, 1024)
        tm = _mm_tile(M, 1024 if tk <= 2048 and post is None else 512)
    nk = K // tk
    assert post is None or (mode != "tn" and tn == N), name
    dims = {"nn": _NN, "nt": _NT, "tn": _TN}[mode]
    if mode == "tn":
        a_spec = pl.BlockSpec((tk, tm), lambda j, i, k: (k, i))
    else:
        a_spec = pl.BlockSpec((tm, tk), lambda j, i, k: (i, k))
    if mode == "nt":
        b_spec = pl.BlockSpec((tn, tk), lambda j, i, k: (j, k))
    else:
        b_spec = pl.BlockSpec((tk, tn), lambda j, i, k: (k, j))
    o_spec = pl.BlockSpec((tm, tn), lambda j, i, k: (i, j))
    has_add = add is not None
    keep_main = post is None or post.keep_main
    counts = [2 + has_add] + ([len(post.row_ins), len(post.vec_ins)] if post else [0, 0]) + [int(keep_main)]
    counts += ([len(post.row_outs), len(post.acc_outs)] if post else [0, 0]) + [int(nk > 1)]

    def body(*refs):
        parts, p = [], 0
        for cnt in counts:
            parts.append(refs[p:p + cnt])
            p += cnt
        core, row_ins, vec_ins, main, row_outs, acc_outs, acc = parts
        a_ref, b_ref = core[:2]
        prod = lax.dot_general(a_ref[...].astype(BF16), b_ref[...].astype(BF16), dims, preferred_element_type=F32)

        def finish(r):
            if has_add:
                r = r + core[2][...]
            if keep_main:
                main[0][...] = r.astype(out_dtype)
            if post is not None:
                @pl.when(pl.program_id(1) == 0)
                def _():
                    for ref in acc_outs:
                        ref[...] = jnp.zeros_like(ref)

                post.fn(r, row_ins, vec_ins, row_outs, acc_outs)

        if nk == 1:
            finish(prod)
            return
        acc_ref = acc[0]
        k = pl.program_id(2)

        @pl.when(k == 0)
        def _():
            acc_ref[...] = jnp.zeros_like(acc_ref)

        acc_ref[...] += prod

        @pl.when(k == nk - 1)
        def _():
            finish(acc_ref[...])

    in_specs = [a_spec, b_spec] + ([o_spec] if has_add else [])
    args = (a, b) + ((add,) if has_add else ())
    out_specs = [o_spec] if keep_main else []
    out_shape = [jax.ShapeDtypeStruct((M, N), out_dtype)] if keep_main else []
    if post is not None:
        in_specs += [pl.BlockSpec((tm, cols), lambda j, i, k, cb=cb: (i, cb)) for _, cols, cb in post.row_ins]
        in_specs += [pl.BlockSpec((1, v.shape[1]), lambda j, i, k: (0, 0)) for v in post.vec_ins]
        args += tuple(r for r, _, _ in post.row_ins) + tuple(post.vec_ins)
        out_specs += [pl.BlockSpec((tm, cols), lambda j, i, k: (i, 0)) for cols, _ in post.row_outs]
        out_specs += [pl.BlockSpec((1, cols), lambda j, i, k: (0, 0)) for cols in post.acc_outs]
        out_shape += [jax.ShapeDtypeStruct((M, cols), dt) for cols, dt in post.row_outs]
        out_shape += [jax.ShapeDtypeStruct((1, cols), F32) for cols in post.acc_outs]
    rows_sem = "arbitrary" if post is not None and post.acc_outs else "parallel"
    res = _hosted_call(
        body, rider, name=name, grid=(N // tn, M // tm, nk), in_specs=in_specs, out_specs=out_specs, out_shape=out_shape,
        scratch_shapes=[pltpu.VMEM((tm, tn), F32)] if nk > 1 else [], semantics=("parallel", rows_sem, "arbitrary"),
    )(*args)
    if rider is not None:
        res, carried = res
        return (res[0] if post is None else res), carried
    return res[0] if post is None else res


def rmsnorm_fwd(x, g, name):
    T, D = x.shape
    tm = _tile(T, 512, 8)

    def body(x_ref, g_ref, o_ref):
        xv = x_ref[...]
        r = lax.rsqrt(jnp.mean(xv * xv, axis=-1, keepdims=True) + EPS)
        o_ref[...] = (xv * r * g_ref[...]).astype(BF16)

    return pl.pallas_call(
        body, name=name, grid=(T // tm,),
        in_specs=[pl.BlockSpec((tm, D), lambda i: (i, 0)), pl.BlockSpec((1, D), lambda i: (0, 0))],
        out_specs=pl.BlockSpec((tm, D), lambda i: (i, 0)),
        out_shape=jax.ShapeDtypeStruct((T, D), BF16),
        compiler_params=_params("parallel"),
    )(x, g)


def _post_rmsnorm(g):
    def fn(r, row_ins, vec_ins, row_outs, acc_outs):
        rs = lax.rsqrt(jnp.mean(r * r, axis=-1, keepdims=True) + EPS)
        row_outs[0][...] = (r * rs * vec_ins[0][...]).astype(BF16)

    return _Post(fn, vec_ins=[g], row_outs=[(g.shape[1], BF16)])


def _post_rmsnorm_bwd(x, g, dres, with_bf16):
    D = g.shape[1]

    def fn(dy, row_ins, vec_ins, row_outs, acc_outs):
        xv = row_ins[0][...]
        rs = lax.rsqrt(jnp.mean(xv * xv, axis=-1, keepdims=True) + EPS)
        xh = xv * rs
        acc_outs[0][...] += jnp.sum(dy * xh, axis=0, keepdims=True)
        dxh = dy * vec_ins[0][...]
        dx = row_ins[1][...] + rs * (dxh - xh * jnp.mean(dxh * xh, axis=-1, keepdims=True))
        row_outs[0][...] = dx
        if with_bf16:
            row_outs[1][...] = dx.astype(BF16)

    return _Post(fn, row_ins=[x, dres], vec_ins=[g], row_outs=[(D, F32)] + ([(D, BF16)] if with_bf16 else []),
                 acc_outs=[D], keep_main=False)


def _post_loss(g, target):
    D = g.shape[1]

    def fn(hv, row_ins, vec_ins, row_outs, acc_outs):
        rs = lax.rsqrt(jnp.mean(hv * hv, axis=-1, keepdims=True) + EPS)
        xh = hv * rs
        gv = vec_ins[0][...]
        err = xh * gv - row_ins[0][...]
        acc_outs[0][...] += jnp.sum(err * err, axis=0, keepdims=True)
        dy = err * (1.0 / D)
        acc_outs[1][...] += jnp.sum(dy * xh, axis=0, keepdims=True)
        dxh = dy * gv
        dh = rs * (dxh - xh * jnp.mean(dxh * xh, axis=-1, keepdims=True))
        row_outs[0][...] = dh
        row_outs[1][...] = dh.astype(BF16)

    return _Post(fn, row_ins=[target], vec_ins=[g], row_outs=[(D, F32), (D, BF16)], acc_outs=[D, D], keep_main=False)


def _shift_down(x, k):
    if k == 0:
        return x
    rows = lax.broadcasted_iota(jnp.int32, x.shape, 0)
    return jnp.where(rows >= k, pltpu.roll(x, k, 0), 0.0)


def _shift_up(x, k):
    if k == 0:
        return x
    s = x.shape[0]
    rows = lax.broadcasted_iota(jnp.int32, x.shape, 0)
    return jnp.where(rows < s - k, pltpu.roll(x, s - k, 0), 0.0)


def _conv_fwd(x, w_ref, kw):
    y = x * w_ref[kw - 1:kw, :]
    for i in range(kw - 1):
        y = y + _shift_down(x, kw - 1 - i) * w_ref[i:i + 1, :]
    return y


def _conv_bwd(x, dy, w_ref, kw):
    dx = dy * w_ref[kw - 1:kw, :]
    dws = []
    for i in range(kw - 1):
        dx = dx + _shift_up(dy, kw - 1 - i) * w_ref[i:i + 1, :]
        dws.append(jnp.sum(dy * _shift_down(x, kw - 1 - i), axis=0, keepdims=True))
    dws.append(jnp.sum(dy * x, axis=0, keepdims=True))
    return dx, dws


def ffn_gate_fwd(up_g, up_v, cw_g, cw_v, B, S):
    T, Fd = up_g.shape
    tc = _tile(Fd, 256)

    def body(g_ref, v_ref, wg_ref, wv_ref, o_ref):
        ug = _conv_fwd(g_ref[...], wg_ref, FFN_CONV)
        uv = _conv_fwd(v_ref[...], wv_ref, FFN_CONV)
        o_ref[...] = (ug * _sigmoid(ug) * uv).astype(BF16)

    blk = pl.BlockSpec((S, tc), lambda b, j: (b, j))
    wblk = pl.BlockSpec((FFN_CONV, tc), lambda b, j: (0, j))
    return pl.pallas_call(
        body, name="ffn_gate_fwd", grid=(B, Fd // tc), in_specs=[blk, blk, wblk, wblk], out_specs=blk,
        out_shape=jax.ShapeDtypeStruct((T, Fd), BF16), compiler_params=_params("parallel", "parallel"),
    )(up_g, up_v, cw_g, cw_v)


def ffn_gate_bwd(up_g, up_v, cw_g, cw_v, d_act, B, S):
    T, Fd = up_g.shape
    tc = _tile(Fd, 256)

    def body(g_ref, v_ref, wg_ref, wv_ref, da_ref, dg_ref, dv_ref, dwg_ref, dwv_ref):
        @pl.when(pl.program_id(1) == 0)
        def _():
            dwg_ref[...] = jnp.zeros_like(dwg_ref)
            dwv_ref[...] = jnp.zeros_like(dwv_ref)

        xg, xv = g_ref[...], v_ref[...]
        ug = _conv_fwd(xg, wg_ref, FFN_CONV)
        uv = _conv_fwd(xv, wv_ref, FFN_CONV)
        da = da_ref[...]
        sg = _sigmoid(ug)
        d_ug = da * uv * (sg + ug * sg * (1.0 - sg))
        d_uv = da * ug * sg
        dxg, dwg = _conv_bwd(xg, d_ug, wg_ref, FFN_CONV)
        dxv, dwv = _conv_bwd(xv, d_uv, wv_ref, FFN_CONV)
        dg_ref[...] = dxg.astype(BF16)
        dv_ref[...] = dxv.astype(BF16)
        for i in range(FFN_CONV):
            dwg_ref[i:i + 1, :] += dwg[i]
            dwv_ref[i:i + 1, :] += dwv[i]

    blk = pl.BlockSpec((S, tc), lambda j, b: (b, j))
    wblk = pl.BlockSpec((FFN_CONV, tc), lambda j, b: (0, j))
    return pl.pallas_call(
        body, name="ffn_gate_bwd", grid=(Fd // tc, B), in_specs=[blk, blk, wblk, wblk, blk],
        out_specs=[blk, blk, wblk, wblk],
        out_shape=[jax.ShapeDtypeStruct((T, Fd), BF16), jax.ShapeDtypeStruct((T, Fd), BF16),
                   jax.ShapeDtypeStruct((FFN_CONV, Fd), F32), jax.ShapeDtypeStruct((FFN_CONV, Fd), F32)],
        compiler_params=_params("parallel", "arbitrary"),
    )(up_g, up_v, cw_g, cw_v, d_act)


def _post_merge(p_gates, bf_):
    D = bf_.shape[1]

    def fn(bg, row_ins, vec_ins, row_outs, acc_outs):
        gf_ref, gg_ref, bf_ref = row_ins
        row_outs[0][...] = (_sigmoid(gf_ref[...]) * bf_ref[...] + _sigmoid(gg_ref[...]) * bg).astype(BF16)

    return _Post(fn, row_ins=[(p_gates, D, 0), (p_gates, D, 1), bf_], row_outs=[(D, BF16)])


def _post_merge_bwd(p_gates, bf_, bg_):
    D = bf_.shape[1]

    def fn(d, row_ins, vec_ins, row_outs, acc_outs):
        gf_ref, gg_ref, bf_ref, bg_ref = row_ins
        sf, sg = _sigmoid(gf_ref[...]), _sigmoid(gg_ref[...])
        row_outs[0][...] = (d * sf).astype(BF16)
        row_outs[1][...] = (d * sg).astype(BF16)
        row_outs[2][:, 0:D] = (d * bf_ref[...] * sf * (1.0 - sf)).astype(BF16)
        row_outs[2][:, D:2 * D] = (d * bg_ref[...] * sg * (1.0 - sg)).astype(BF16)

    return _Post(fn, row_ins=[(p_gates, D, 0), (p_gates, D, 1), bf_, bg_],
                 row_outs=[(D, BF16), (D, BF16), (2 * D, BF16)], keep_main=False)


def fox_fwd(p_fox, c_col, c_row, B, S, H, rider=None):
    T = B * S
    t = _tile(S, ATT_TILE)
    nq = S // t
    scale = HEAD ** -0.5

    def body(q_ref, k_ref, v_ref, cq_ref, cr_ref, o_ref, o16_ref, lse_ref):
        i = pl.program_id(2)
        q = q_ref[...]
        cq = cq_ref[...]
        row = lax.broadcasted_iota(jnp.int32, (t, t), 0)
        col = lax.broadcasted_iota(jnp.int32, (t, t), 1)

        def step(j, carry, diagonal):
            m, l, acc = carry
            off = pl.multiple_of(j * t, t)
            k = k_ref[pl.ds(off, t), :]
            v = v_ref[pl.ds(off, t), :]
            s = lax.dot_general(q, k, _NT, preferred_element_type=F32) * scale - cr_ref[:, pl.ds(off, t)]
            if diagonal:
                s = jnp.where(col <= row, s, NEG)
            m_new = jnp.maximum(m, jnp.max(s, axis=-1, keepdims=True))
            alpha = jnp.exp(m - m_new)
            p = jnp.exp(s - m_new)
            l = alpha * l + jnp.sum(p, axis=-1, keepdims=True)
            acc = alpha * acc + lax.dot_general(p.astype(BF16), v, _NN, preferred_element_type=F32)
            return m_new, l, acc

        m0 = jnp.full((t, 1), NEG, F32)
        below = lax.fori_loop(0, i, functools.partial(step, diagonal=False),
                              (m0, jnp.zeros((t, 1), F32), jnp.zeros((t, HEAD), F32)))
        m, l, acc = step(i, below, diagonal=True)
        o = acc / l
        o_ref[...] = o
        o16_ref[...] = o.astype(BF16)
        lse_ref[...] = cq + m + jnp.log(l)

    return _hosted_call(
        body, rider, name="fox_fwd", grid=(B, H, nq),
        in_specs=[pl.BlockSpec((t, HEAD), lambda b, h, i: (b * nq + i, 3 * h)),
                  pl.BlockSpec((S, HEAD), lambda b, h, i: (b, 3 * h + 1)),
                  pl.BlockSpec((S, HEAD), lambda b, h, i: (b, 3 * h + 2)),
                  pl.BlockSpec((None, None, t, 1), lambda b, h, i: (b, h, i, 0)),
                  pl.BlockSpec((None, None, 1, S), lambda b, h, i: (b, h, 0, 0))],
        out_specs=[pl.BlockSpec((t, HEAD), lambda b, h, i: (b * nq + i, h)),
                   pl.BlockSpec((t, HEAD), lambda b, h, i: (b * nq + i, h)),
                   pl.BlockSpec((None, None, t, 1), lambda b, h, i: (b, h, i, 0))],
        out_shape=[jax.ShapeDtypeStruct((T, H * HEAD), F32), jax.ShapeDtypeStruct((T, H * HEAD), BF16),
                   jax.ShapeDtypeStruct((B, H, S, 1), F32)],
        scratch_shapes=[], semantics=("parallel", "parallel", "arbitrary"),
    )(p_fox, p_fox, p_fox, c_col, c_row)


def fox_bwd(p_fox, c_col, c_row, o, lse, do, B, S, H, rider=None):
    T = B * S
    t = _tile(S, ATT_TILE)
    n = S // t
    scale = HEAD ** -0.5

    def body(q_ref, k_ref, v_ref, cq_ref, cr_ref, o_ref, lse_ref, do_ref, dqkv_ref, dcq_ref, dcr_ref,
             dq_acc, delta_s, lse_s):
        row = lax.broadcasted_iota(jnp.int32, (t, t), 0)
        col = lax.broadcasted_iota(jnp.int32, (t, t), 1)

        def prep(i, c):
            rows = pl.ds(pl.multiple_of(i * t, t), t)
            delta_s[rows, :] = jnp.sum(do_ref[rows, :] * o_ref[rows, :], axis=-1, keepdims=True)
            lse_s[rows, :] = lse_ref[rows, :] - cq_ref[rows, :]
            dq_acc[rows, :] = jnp.zeros((t, HEAD), F32)
            dcq_ref[rows, :] = jnp.zeros((t, 1), F32)
            return c

        lax.fori_loop(0, n, prep, 0)

        def kv_step(j, c):
            joff = pl.multiple_of(j * t, t)
            k = k_ref[pl.ds(joff, t), :]
            v = v_ref[pl.ds(joff, t), :]
            crj = cr_ref[:, pl.ds(joff, t)]

            def q_step(i, carry, diagonal):
                dk, dv, dc = carry
                rows = pl.ds(pl.multiple_of(i * t, t), t)
                q = q_ref[rows, :]
                dob = do_ref[rows, :].astype(BF16)
                s = lax.dot_general(q, k, _NT, preferred_element_type=F32) * scale - crj
                if diagonal:
                    s = jnp.where(col <= row, s, NEG)
                p = jnp.exp(s - lse_s[rows, :])
                dp = lax.dot_general(dob, v, _NT, preferred_element_type=F32)
                ds = p * (dp - delta_s[rows, :])
                dsb = ds.astype(BF16)
                dv = dv + lax.dot_general(p.astype(BF16), dob, _TN, preferred_element_type=F32)
                dk = dk + lax.dot_general(dsb, q, _TN, preferred_element_type=F32)
                dq_acc[rows, :] += lax.dot_general(dsb, k, _NN, preferred_element_type=F32) * scale
                dc = dc + jnp.sum(ds, axis=0, keepdims=True)
                dcq_ref[rows, :] += jnp.sum(ds, axis=-1, keepdims=True)
                return dk, dv, dc

            z = jnp.zeros((t, HEAD), F32)
            on_diagonal = q_step(j, (z, z, jnp.zeros((1, t), F32)), diagonal=True)
            dk, dv, dc = lax.fori_loop(j + 1, n, functools.partial(q_step, diagonal=False), on_diagonal)
            dqkv_ref[pl.ds(joff, t), HEAD:2 * HEAD] = (dk * scale).astype(BF16)
            dqkv_ref[pl.ds(joff, t), 2 * HEAD:3 * HEAD] = dv.astype(BF16)
            dcr_ref[:, pl.ds(joff, t)] = -dc
            return c

        lax.fori_loop(0, n, kv_step, 0)
        dqkv_ref[:, 0:HEAD] = dq_acc[...].astype(BF16)

    col_spec = pl.BlockSpec((None, None, S, 1), lambda b, h: (b, h, 0, 0))
    row_spec = pl.BlockSpec((None, None, 1, S), lambda b, h: (b, h, 0, 0))
    head = pl.BlockSpec((S, HEAD), lambda b, h: (b, h))
    return _hosted_call(
        body, rider, name="fox_bwd", grid=(B, H),
        in_specs=[pl.BlockSpec((S, HEAD), lambda b, h: (b, 3 * h)),
                  pl.BlockSpec((S, HEAD), lambda b, h: (b, 3 * h + 1)),
                  pl.BlockSpec((S, HEAD), lambda b, h: (b, 3 * h + 2)),
                  col_spec, row_spec, head, col_spec, head],
        out_specs=[pl.BlockSpec((S, 3 * HEAD), lambda b, h: (b, h)), col_spec, row_spec],
        out_shape=[jax.ShapeDtypeStruct((T, 3 * H * HEAD), BF16), jax.ShapeDtypeStruct((B, H, S, 1), F32),
                   jax.ShapeDtypeStruct((B, H, 1, S), F32)],
        scratch_shapes=[pltpu.VMEM((S, HEAD), F32), pltpu.VMEM((S, 1), F32), pltpu.VMEM((S, 1), F32)],
        semantics=("parallel", "parallel"),
    )(p_fox, p_fox, p_fox, c_col, c_row, o, lse, do)


def _small_fn(x, b0, b1, H):
    S = x.shape[0]
    lane = lax.broadcasted_iota(jnp.int32, x.shape, 1)
    z = x + b0
    tail = jnp.log1p(jnp.exp(-jnp.abs(z)))
    softplus = jnp.maximum(z, 0.0) + tail
    logsig = -(jnp.maximum(-z, 0.0) + tail)
    g = -jnp.exp(b1) * softplus
    pre = jnp.where(lane < H, logsig, jnp.where(lane < 2 * H, g, 0.0))
    bl = _tile(S, 256, CHUNK)
    r = lax.broadcasted_iota(jnp.int32, (bl, bl), 0)
    c = lax.broadcasted_iota(jnp.int32, (bl, bl), 1)
    tri = (r >= c).astype(F32)
    tri_chunk = jnp.where((r >= c) & (jnp.right_shift(r, 6) == jnp.right_shift(c, 6)), 1.0, 0.0)
    carry = jnp.zeros((1, x.shape[1]), F32)
    parts = []
    for i in range(S // bl):
        blk = pre[i * bl:(i + 1) * bl, :]
        full = lax.dot_general(tri, blk, _NN, precision=EXACT, preferred_element_type=F32) + carry
        chunked = lax.dot_general(tri_chunk, blk, _NN, precision=EXACT, preferred_element_type=F32)
        parts.append(jnp.where(lane[:bl] < H, full, chunked))
        carry = carry + jnp.sum(blk, axis=0, keepdims=True)
    cum = parts[0] if len(parts) == 1 else jnp.concatenate(parts, axis=0)
    return jnp.where(lane < 2 * H, cum, jnp.where(lane < 3 * H, _sigmoid(x), 0.0))


def small_fwd(p_small, prm, B, S, H):
    T = B * S

    def body(x_ref, p_ref, o_ref):
        o_ref[...] = _small_fn(x_ref[...], p_ref[0:1, :], p_ref[1:2, :], H)

    blk = pl.BlockSpec((S, 128), lambda b: (b, 0))
    return pl.pallas_call(
        body, name="small_fwd", grid=(B,), in_specs=[blk, pl.BlockSpec((8, 128), lambda b: (0, 0))], out_specs=blk,
        out_shape=jax.ShapeDtypeStruct((T, 128), F32), compiler_params=_params("parallel"),
    )(p_small, prm)


def small_bwd(p_small, prm, d_out, B, S, H):
    T = B * S

    def body(x_ref, p_ref, d_ref, dx_ref, dp_ref):
        @pl.when(pl.program_id(0) == 0)
        def _():
            dp_ref[...] = jnp.zeros_like(dp_ref)

        _, vjp = jax.vjp(functools.partial(_small_fn, H=H), x_ref[...], p_ref[0:1, :], p_ref[1:2, :])
        dx, db0, db1 = vjp(d_ref[...])
        dx_ref[...] = dx.astype(BF16)
        dp_ref[0:1, :] += db0
        dp_ref[1:2, :] += db1

    blk = pl.BlockSpec((S, 128), lambda b: (b, 0))
    pblk = pl.BlockSpec((8, 128), lambda b: (0, 0))
    return pl.pallas_call(
        body, name="small_bwd", grid=(B,), in_specs=[blk, pblk, blk], out_specs=[blk, pblk],
        out_shape=[jax.ShapeDtypeStruct((T, 128), BF16), jax.ShapeDtypeStruct((8, 128), F32)],
        compiler_params=_params("arbitrary"),
    )(p_small, prm, d_out)


def gdn_prep_fwd(p_gqkv, cw, B, S, H):
    T = B * S

    def body(x_ref, w_ref, o_ref):
        y = _conv_fwd(x_ref[...], w_ref, GDN_CONV)
        a = y * _sigmoid(y)
        rs = lax.rsqrt(jnp.sum(a * a, axis=-1, keepdims=True) + EPS)
        is_qk = (pl.program_id(1) % 3) < 2
        o_ref[...] = a * jnp.where(is_qk, rs, 1.0)

    blk = pl.BlockSpec((S, HEAD), lambda b, n: (b, n))
    wblk = pl.BlockSpec((GDN_CONV, HEAD), lambda b, n: (0, n))
    return pl.pallas_call(
        body, name="gdn_prep_fwd", grid=(B, 3 * H), in_specs=[blk, wblk], out_specs=blk,
        out_shape=jax.ShapeDtypeStruct((T, 3 * H * HEAD), F32), compiler_params=_params("parallel", "parallel"),
    )(p_gqkv, cw)


def gdn_prep_bwd(p_gqkv, cw, d_out, B, S, H):
    T = B * S

    def body(x_ref, w_ref, d_ref, dx_ref, dw_ref):
        @pl.when(pl.program_id(1) == 0)
        def _():
            dw_ref[...] = jnp.zeros_like(dw_ref)

        x = x_ref[...]
        y = _conv_fwd(x, w_ref, GDN_CONV)
        sg = _sigmoid(y)
        a = y * sg
        rs = lax.rsqrt(jnp.sum(a * a, axis=-1, keepdims=True) + EPS)
        d = d_ref[...]
        out = a * rs
        da_qk = rs * (d - out * jnp.sum(d * out, axis=-1, keepdims=True))
        is_qk = (pl.program_id(0) % 3) < 2
        da = jnp.where(is_qk, da_qk, d)
        dy = da * (sg + y * sg * (1.0 - sg))
        dx, dws = _conv_bwd(x, dy, w_ref, GDN_CONV)
        dx_ref[...] = dx.astype(BF16)
        for i in range(GDN_CONV):
            dw_ref[i:i + 1, :] += dws[i]

    blk = pl.BlockSpec((S, HEAD), lambda n, b: (b, n))
    wblk = pl.BlockSpec((GDN_CONV, HEAD), lambda n, b: (0, n))
    return pl.pallas_call(
        body, name="gdn_prep_bwd", grid=(3 * H, B), in_specs=[blk, wblk, blk], out_specs=[blk, wblk],
        out_shape=[jax.ShapeDtypeStruct((T, 3 * H * HEAD), BF16), jax.ShapeDtypeStruct((GDN_CONV, 3 * H * HEAD), F32)],
        compiler_params=_params("parallel", "arbitrary"),
    )(p_gqkv, cw, d_out)


@jax.custom_vjp
def _given_inverse(a, t):
    return t


def _given_inverse_fwd(a, t):
    return t, t


def _given_inverse_bwd(t, g):
    x = _dg(t, g, _TN, True)
    return -_dg(x, t, _NT, True), jnp.zeros_like(t)


_given_inverse.defvjp(_given_inverse_fwd, _given_inverse_bwd)


def _to_col(row):
    n = row.shape[1]
    r = lax.broadcasted_iota(jnp.int32, (n, n), 0)
    c = lax.broadcasted_iota(jnp.int32, (n, n), 1)
    return jnp.sum(jnp.where(r == c, row, 0.0), axis=1, keepdims=True)


def _intra_fn(k, v, beta_r, gcr, ops, t_known=None):
    n = len(k)
    m = k[0].shape[0]
    r = lax.broadcasted_iota(jnp.int32, (m, m), 0)
    c = lax.broadcasted_iota(jnp.int32, (m, m), 1)
    below = (r > c) & (jnp.right_shift(r, 6) == jnp.right_shift(c, 6))
    beta = [_to_col(beta_r[i]) for i in range(n)]
    gcc = [_to_col(gcr[i]) for i in range(n)]
    decay = [jnp.exp(jnp.where(below, gcc[i] - gcr[i], NEG)) for i in range(n)]
    kb = [k[i] * beta[i] for i in range(n)]
    a = [ops.nt(kb[i], k[i]) * decay[i] for i in range(n)]
    if t_known is None:
        p = [-a[i] for i in range(n)]
        tm = [jnp.where(r == c, 1.0, 0.0) + p[i] for i in range(n)]
        for _ in range(5):
            p = [ops.nn(p[i], p[i], hi=True) for i in range(n)]
            tm = [tm[i] + ops.nn(tm[i], p[i], hi=True) for i in range(n)]
    else:
        tm = [_given_inverse(a[i], t_known[i]) for i in range(n)]
    u_hat = [ops.nn(tm[i], v[i] * beta[i], hi=True) for i in range(n)]
    w = [ops.nn(tm[i], kb[i] * jnp.exp(gcc[i]), hi=True) for i in range(n)]
    return tuple(u_hat), tuple(w), tuple(tm)


INTRA_NB = 32
PAIR = 1


def gdn_intra_fwd(qkvn, betar5, gcr5, B, S, H, rider=None):
    T = B * S
    UNIT = PAIR * CHUNK
    N = S // UNIT
    nb = min(INTRA_NB // PAIR, N)
    rows = nb * UNIT
    ns = N // nb

    def body(k_ref, v_ref, b_ref, gr_ref, uh_ref, w_ref, t_ref):
        sls = [slice(ci * UNIT, (ci + 1) * UNIT) for ci in range(nb)]
        u_hat, w, tm = _intra_fn(tuple(k_ref[sl, :] for sl in sls), tuple(v_ref[sl, :] for sl in sls),
                                 tuple(b_ref[ci] for ci in range(nb)), tuple(gr_ref[ci] for ci in range(nb)), _RawOps)
        for ci, sl in enumerate(sls):
            uh_ref[sl, :] = u_hat[ci]
            w_ref[sl, :] = w[ci]
            t_ref[ci] = tm[ci]

    rowspec = pl.BlockSpec((None, None, nb, 1, UNIT), lambda b, h, i: (b, h, i, 0, 0))
    sqspec = pl.BlockSpec((None, None, nb, UNIT, UNIT), lambda b, h, i: (b, h, i, 0, 0))
    out = pl.BlockSpec((rows, HEAD), lambda b, h, i: (b * ns + i, h))
    return _hosted_call(
        body, rider, name="gdn_intra_fwd", grid=(B, H, ns),
        in_specs=[pl.BlockSpec((rows, HEAD), lambda b, h, i: (b * ns + i, 3 * h + 1)),
                  pl.BlockSpec((rows, HEAD), lambda b, h, i: (b * ns + i, 3 * h + 2)),
                  rowspec, rowspec],
        out_specs=[out, out, sqspec],
        out_shape=[jax.ShapeDtypeStruct((T, H * HEAD), F32), jax.ShapeDtypeStruct((T, H * HEAD), F32),
                   jax.ShapeDtypeStruct((B, H, N, UNIT, UNIT), F32)],
        scratch_shapes=[], semantics=("parallel", "parallel", "parallel"),
    )(qkvn, qkvn, betar5, gcr5)


def gdn_intra_bwd(qkvn, betar5, gcr5, t_inv, d_uh, d_w, dq_in, dk_in, B, S, H):
    T = B * S
    UNIT = PAIR * CHUNK
    N = S // UNIT
    nb = min(INTRA_NB // PAIR, N)
    rows = nb * UNIT
    ns = N // nb

    def body(k_ref, v_ref, b_ref, gr_ref, t_ref, duh_ref, dw_ref, dq_ref, dk_ref, o_ref, db_ref, dgr_ref):
        sls = [slice(ci * UNIT, (ci + 1) * UNIT) for ci in range(nb)]
        chunks = range(nb)
        _, vjp = jax.vjp(
            functools.partial(_intra_fn, ops=_DiffOps, t_known=tuple(t_ref[ci] for ci in chunks)),
            tuple(k_ref[sl, :] for sl in sls), tuple(v_ref[sl, :] for sl in sls), tuple(b_ref[ci] for ci in chunks),
            tuple(gr_ref[ci] for ci in chunks))
        zero = jnp.zeros((UNIT, UNIT), F32)
        dk, dv, db, dgr = vjp((tuple(duh_ref[sl, :] for sl in sls), tuple(dw_ref[sl, :] for sl in sls),
                               tuple(zero for _ in chunks)))
        for ci, sl in enumerate(sls):
            o_ref[sl, 0:HEAD] = dq_ref[sl, :]
            o_ref[sl, HEAD:2 * HEAD] = dk[ci] + dk_ref[sl, :]
            o_ref[sl, 2 * HEAD:3 * HEAD] = dv[ci]
            db_ref[ci] = db[ci]
            dgr_ref[ci] = dgr[ci]

    rowspec = pl.BlockSpec((None, None, nb, 1, UNIT), lambda b, h, i: (b, h, i, 0, 0))
    sqspec = pl.BlockSpec((None, None, nb, UNIT, UNIT), lambda b, h, i: (b, h, i, 0, 0))
    head = pl.BlockSpec((rows, HEAD), lambda b, h, i: (b * ns + i, h))
    return pl.pallas_call(
        body, name="gdn_intra_bwd", grid=(B, H, ns),
        in_specs=[pl.BlockSpec((rows, HEAD), lambda b, h, i: (b * ns + i, 3 * h + 1)),
                  pl.BlockSpec((rows, HEAD), lambda b, h, i: (b * ns + i, 3 * h + 2)),
                  rowspec, rowspec, sqspec, head, head, head, head],
        out_specs=[pl.BlockSpec((rows, 3 * HEAD), lambda b, h, i: (b * ns + i, h)), rowspec, rowspec],
        out_shape=[jax.ShapeDtypeStruct((T, 3 * H * HEAD), F32),
                   jax.ShapeDtypeStruct((B, H, N, 1, UNIT), F32), jax.ShapeDtypeStruct((B, H, N, 1, UNIT), F32)],
        compiler_params=_params("parallel", "parallel", "parallel"),
    )(qkvn, qkvn, betar5, gcr5, t_inv, d_uh, d_w, dq_in, dk_in)


def _inter_fn(q, k, u_hat, w, gcr, state, ops):
    n = len(q)
    r = lax.broadcasted_iota(jnp.int32, (CHUNK, CHUNK), 0)
    c = lax.broadcasted_iota(jnp.int32, (CHUNK, CHUNK), 1)
    last = lax.broadcasted_iota(jnp.int32, (1, CHUNK), 1) == CHUNK - 1
    gcc = [_to_col(gcr[i]) for i in range(n)]
    gl = [jnp.sum(jnp.where(last, gcr[i], 0.0), axis=1, keepdims=True) for i in range(n)]
    decay = [jnp.exp(jnp.where(r >= c, gcc[i] - gcr[i], NEG)) for i in range(n)]
    qs = [q[i] * (HEAD ** -0.5) for i in range(n)]
    ws = [ops.nn(w[i], state[i]) for i in range(n)]
    qst = [ops.nn(qs[i] * jnp.exp(gcc[i]), state[i]) for i in range(n)]
    attn = [ops.nt(qs[i], k[i]) * decay[i] for i in range(n)]
    u = [u_hat[i] - ws[i] for i in range(n)]
    o = [qst[i] + ops.nn(attn[i], u[i]) for i in range(n)]
    kdu = [ops.tn(k[i] * jnp.exp(gl[i] - gcc[i]), u[i]) for i in range(n)]
    new_state = [state[i] * jnp.exp(gl[i]) + kdu[i] for i in range(n)]
    return tuple(o), tuple(new_state)


INTER_HEADS = 8
INTER_ROWS = 512
INTER_ROWS_BWD = 256


def _inter_heads(H):
    return INTER_HEADS if H % INTER_HEADS == 0 else (4 if H % 4 == 0 else 1)


def _inter_specs(ts, ns, hp, backward):
    at = (lambda s: ns - 1 - s) if backward else (lambda s: s)
    nc = ts // CHUNK
    qk = []
    for hh in range(hp):
        qk.append(pl.BlockSpec((ts, HEAD), lambda b, g, s, hh=hh: (b * ns + at(s), 3 * (hp * g + hh))))
        qk.append(pl.BlockSpec((ts, HEAD), lambda b, g, s, hh=hh: (b * ns + at(s), 3 * (hp * g + hh) + 1)))
    heads = pl.BlockSpec((ts, hp * HEAD), lambda b, g, s: (b * ns + at(s), g))
    rowspec = pl.BlockSpec((None, hp, nc, 1, CHUNK), lambda b, g, s: (b, g, at(s), 0, 0))
    stspec = pl.BlockSpec((None, hp, nc, HEAD, HEAD), lambda b, g, s: (b, g, at(s), 0, 0))
    return qk, heads, rowspec, stspec


def gdn_inter_fwd(qkvn, u_hat, w, gcr5, p_gz, gnorm, B, S, H):
    T = B * S
    N = S // CHUNK
    hp = _inter_heads(H)
    hs = range(hp)
    ts = _tile(S, INTER_ROWS, CHUNK)
    ns, nc = S // ts, ts // CHUNK

    def body(*refs):
        qk_refs, (uh_ref, w_ref, gr_ref, z_ref, gn_ref, o_ref, st_ref, y_ref, s_scr) = refs[:2 * hp], refs[2 * hp:]

        @pl.when(pl.program_id(2) == 0)
        def _():
            s_scr[...] = jnp.zeros_like(s_scr)

        gn = gn_ref[...]

        def step(n, c):
            rows = pl.ds(pl.multiple_of(n * CHUNK, CHUNK), CHUNK)
            st = tuple(s_scr[hh] for hh in hs)
            for hh in hs:
                st_ref[hh, n] = st[hh]
            o, new = _inter_fn(tuple(qk_refs[2 * hh][rows, :] for hh in hs), tuple(qk_refs[2 * hh + 1][rows, :] for hh in hs),
                               tuple(uh_ref[rows, hh * HEAD:(hh + 1) * HEAD] for hh in hs),
                               tuple(w_ref[rows, hh * HEAD:(hh + 1) * HEAD] for hh in hs),
                               tuple(gr_ref[hh, n] for hh in hs), st, _RawOps)
            for hh in hs:
                cols = slice(hh * HEAD, (hh + 1) * HEAD)
                o_ref[rows, cols] = o[hh]
                s_scr[hh] = new[hh]
                z = z_ref[rows, cols]
                r = lax.rsqrt(jnp.mean(o[hh] * o[hh], axis=-1, keepdims=True) + EPS)
                y_ref[rows, cols] = (o[hh] * r * gn * z * _sigmoid(z)).astype(BF16)
            return c

        lax.fori_loop(0, nc, step, 0)

    qk, heads, rowspec, stspec = _inter_specs(ts, ns, hp, backward=False)
    return pl.pallas_call(
        body, name="gdn_inter_fwd", grid=(B, H // hp, ns),
        in_specs=qk + [heads, heads, rowspec, heads, pl.BlockSpec((1, HEAD), lambda b, g, s: (0, 0))],
        out_specs=[heads, stspec, heads],
        out_shape=[jax.ShapeDtypeStruct((T, H * HEAD), F32), jax.ShapeDtypeStruct((B, H, N, HEAD, HEAD), F32),
                   jax.ShapeDtypeStruct((T, H * HEAD), BF16)],
        scratch_shapes=[pltpu.VMEM((hp, HEAD, HEAD), F32)],
        compiler_params=_params("parallel", "parallel", "arbitrary"),
    )(*([qkvn] * (2 * hp)), u_hat, w, gcr5, p_gz, gnorm)


def gdn_inter_bwd(qkvn, u_hat, w, gcr5, states, o, p_gz, gnorm, d_y, B, S, H, rider=None):
    T = B * S
    N = S // CHUNK
    hp = _inter_heads(H)
    hs = range(hp)
    ts = _tile(S, INTER_ROWS_BWD, CHUNK)
    ns, nc = S // ts, ts // CHUNK

    def body(*refs):
        qk_refs = refs[:2 * hp]
        (uh_ref, w_ref, gr_ref, st_ref, o_ref, z_ref, gn_ref, dy_ref,
         dq_ref, dk_ref, duh_ref, dw_ref, dgr_ref, dz_ref, dgn_ref, ds_scr) = refs[2 * hp:]

        @pl.when(pl.program_id(2) == 0)
        def _():
            ds_scr[...] = jnp.zeros_like(ds_scr)
            dgn_ref[...] = jnp.zeros_like(dgn_ref)

        cols = [slice(hh * HEAD, (hh + 1) * HEAD) for hh in hs]
        gn = gn_ref[...]

        def through_norm(rows, hh):
            ov, z, d = o_ref[rows, cols[hh]], z_ref[rows, cols[hh]], dy_ref[rows, cols[hh]]
            r = lax.rsqrt(jnp.mean(ov * ov, axis=-1, keepdims=True) + EPS)
            xh = ov * r
            sg = _sigmoid(z)
            d_n = d * (z * sg)
            dz_ref[rows, cols[hh]] = (d * xh * gn * (sg + z * sg * (1.0 - sg))).astype(BF16)
            dgn_ref[0:1, :] += jnp.sum(d_n * xh, axis=0, keepdims=True)
            dxh = d_n * gn
            return r * (dxh - xh * jnp.mean(dxh * xh, axis=-1, keepdims=True))

        def step(i, c):
            n = nc - 1 - i
            rows = pl.ds(pl.multiple_of(n * CHUNK, CHUNK), CHUNK)
            _, vjp = jax.vjp(functools.partial(_inter_fn, ops=_DiffOps),
                             tuple(qk_refs[2 * hh][rows, :] for hh in hs), tuple(qk_refs[2 * hh + 1][rows, :] for hh in hs),
                             tuple(uh_ref[rows, cols[hh]] for hh in hs), tuple(w_ref[rows, cols[hh]] for hh in hs),
                             tuple(gr_ref[hh, n] for hh in hs), tuple(st_ref[hh, n] for hh in hs))
            dq, dk, duh, dw, dgr, ds = vjp((tuple(through_norm(rows, hh) for hh in hs), tuple(ds_scr[hh] for hh in hs)))
            for hh in hs:
                dq_ref[rows, cols[hh]] = dq[hh]
                dk_ref[rows, cols[hh]] = dk[hh]
                duh_ref[rows, cols[hh]] = duh[hh]
                dw_ref[rows, cols[hh]] = dw[hh]
                dgr_ref[hh, n] = dgr[hh]
                ds_scr[hh] = ds[hh]
            return c

        lax.fori_loop(0, nc, step, 0)

    qk, heads, rowspec, stspec = _inter_specs(ts, ns, hp, backward=True)
    hshape = jax.ShapeDtypeStruct((T, H * HEAD), F32)
    return _hosted_call(
        body, rider, name="gdn_inter_bwd", grid=(B, H // hp, ns),
        in_specs=qk + [heads, heads, rowspec, stspec, heads, heads, pl.BlockSpec((1, HEAD), lambda b, g, s: (0, 0)), heads],
        out_specs=[heads, heads, heads, heads, rowspec, heads,
                   pl.BlockSpec((None, None, 8, HEAD), lambda b, g, s: (b, g, 0, 0))],
        out_shape=[hshape, hshape, hshape, hshape, jax.ShapeDtypeStruct((B, H, N, 1, CHUNK), F32),
                   jax.ShapeDtypeStruct((T, H * HEAD), BF16), jax.ShapeDtypeStruct((B, H // hp, 8, HEAD), F32)],
        scratch_shapes=[pltpu.VMEM((hp, HEAD, HEAD), F32)], semantics=("parallel", "parallel", "arbitrary"),
    )(*([qkvn] * (2 * hp)), u_hat, w, gcr5, states, o, p_gz, gnorm, d_y)


def adamw(w, g, m, v, name):
    shape = w.shape
    lead = (None,) * (w.ndim - 2)
    zeros = (0,) * (w.ndim - 2)
    R, C = shape[-2:]
    g2 = g.reshape(R, C)
    tr, tc = _tile(R, 128, 8), C
    if tr % 8 and R > 8:
        tr, tc = R, _tile(C, 128)

    def body(w_ref, g_ref, m_ref, v_ref, d_ref, nm_ref, nv_ref):
        gv = g_ref[...]
        nm = ADAM_B1 * m_ref[...] + (1.0 - ADAM_B1) * gv
        nv = ADAM_B2 * v_ref[...] + (1.0 - ADAM_B2) * (gv * gv)
        m_hat = nm / (1.0 - ADAM_B1 ** ADAM_STEP)
        v_hat = nv / (1.0 - ADAM_B2 ** ADAM_STEP)
        d_ref[...] = -ADAM_LR * (m_hat / (jnp.sqrt(v_hat) + ADAM_EPS) + ADAM_WD * w_ref[...])
        nm_ref[...] = nm
        nv_ref[...] = nv

    blk = pl.BlockSpec(lead + (tr, tc), lambda i, j: zeros + (i, j))
    gblk = pl.BlockSpec((tr, tc), lambda i, j: (i, j))
    sh = jax.ShapeDtypeStruct(shape, F32)
    return pl.pallas_call(
        body, name=name, grid=(R // tr, C // tc), in_specs=[blk, gblk, blk, blk], out_specs=[blk] * 3, out_shape=[sh] * 3,
        compiler_params=_params("parallel", "parallel"),
    )(w, g2, m, v)


def _place():
    x, y, c = lax.axis_index("x"), lax.axis_index("y"), lax.axis_index("c")
    chips = [(1 - x, y), (x, 1 - y), (1 - x, 1 - y)]
    return x, y, c, chips


_HBM = pl.BlockSpec(memory_space=pltpu.HBM)


def allgather_weights(packs):
    n = len(packs)

    def body(*refs):
        in_refs, out_refs, (send_sems, recv_sems) = refs[:n], refs[n:2 * n], refs[2 * n:]
        x, y, c, chips = _place()
        me_s = 2 * x + y
        me, sibling = (x, y, c), (x, y, 1 - c)
        shards = [2 * chip[0] + chip[1] for chip in chips]

        def copy(a, k, shard, half, to, src=None):
            dst = out_refs[a].at[shard, half]
            return pltpu.make_async_remote_copy(src_ref=dst if src is None else src, dst_ref=dst,
                                                send_sem=send_sems.at[6 * a + k], recv_sem=recv_sems.at[6 * a + k],
                                                device_id=to, device_id_type=MESH)

        first = [copy(a, j, me_s, c, (*chip, c), src=in_refs[a].at[c]) for a in range(n) for j, chip in enumerate(chips)]
        for cp in first:
            cp.start()
        passed = []
        for a in range(n):
            for j in range(3):
                copy(a, j, shards[j], c, me).wait_recv()
                passed.append(copy(a, 3 + j, shards[j], c, sibling))
                passed[-1].start()
        for a in range(n):
            for j in range(3):
                copy(a, 3 + j, shards[j], 1 - c, me).wait_recv()
        for cp in first + passed:
            cp.wait_send()

    return pl.pallas_call(
        body, name="allgather_weights", in_specs=[_HBM] * n, out_specs=[_HBM] * n,
        out_shape=[jax.ShapeDtypeStruct((N_CHIP,) + p.shape, p.dtype) for p in packs],
        scratch_shapes=[pltpu.SemaphoreType.DMA((6 * n,)), pltpu.SemaphoreType.DMA((6 * n,))],
    )(*packs)


class _Rider:
    def __init__(self, inputs, out_shapes, n_sems, sends, recvs, aliases=None):
        self.inputs, self.out_shapes, self.n_sems = list(inputs), list(out_shapes), n_sems
        self.sends, self.recvs, self.aliases = sends, recvs, aliases or {}

    def start(self, *refs):
        for cp in self.sends(*refs):
            cp.start()

    def wait(self, *refs):
        for cp in self.recvs(*refs):
            cp.wait_recv()
        for cp in self.sends(*refs):
            cp.wait_send()


def _remote(src, dst, send_sems, recv_sems, k, to):
    return pltpu.make_async_remote_copy(src_ref=src, dst_ref=dst, send_sem=send_sems.at[k], recv_sem=recv_sems.at[k],
                                        device_id=to, device_id_type=MESH)


def _run_alone(rider, name):
    ri = len(rider.inputs)

    def body(*refs):
        ins, outs, (send_sems, recv_sems) = refs[:ri], refs[ri:-2], refs[-2:]
        rider.start(ins, outs, send_sems, recv_sems)
        rider.wait(ins, outs, send_sems, recv_sems)

    return pl.pallas_call(
        body, name=name, in_specs=[_HBM] * ri, out_specs=[_HBM] * len(rider.out_shapes), out_shape=rider.out_shapes,
        scratch_shapes=[pltpu.SemaphoreType.DMA((rider.n_sems,))] * 2, input_output_aliases=rider.aliases,
    )(*rider.inputs)


def _hosted_call(body, rider, *, name, grid, in_specs, out_specs, out_shape, scratch_shapes, semantics):
    if rider is None:
        return pl.pallas_call(body, name=name, grid=grid, in_specs=in_specs, out_specs=out_specs, out_shape=out_shape,
                              scratch_shapes=scratch_shapes, compiler_params=_params(*semantics))
    n_in, n_out, n_scr = len(in_specs), len(out_specs), len(scratch_shapes)
    ri, ro = len(rider.inputs), len(rider.out_shapes)

    def hosted(*refs):
        parts, p = [], 0
        for cnt in (n_in, ri, n_out, ro, n_scr, 2):
            parts.append(refs[p:p + cnt])
            p += cnt
        ins, rins, outs, routs, scr, (send_sems, recv_sems) = parts
        first = functools.reduce(jnp.logical_and, [pl.program_id(a) == 0 for a in range(len(grid))])
        last = functools.reduce(jnp.logical_and, [pl.program_id(a) == grid[a] - 1 for a in range(len(grid))])

        @pl.when(first)
        def _():
            rider.start(rins, routs, send_sems, recv_sems)

        body(*ins, *outs, *scr)

        @pl.when(last)
        def _():
            rider.wait(rins, routs, send_sems, recv_sems)

    call = pl.pallas_call(
        hosted, name=name, grid=grid, in_specs=list(in_specs) + [_HBM] * ri, out_specs=list(out_specs) + [_HBM] * ro,
        out_shape=list(out_shape) + rider.out_shapes,
        scratch_shapes=list(scratch_shapes) + [pltpu.SemaphoreType.DMA((rider.n_sems,))] * 2,
        input_output_aliases={n_in + i: n_out + o for i, o in rider.aliases.items()},
        compiler_params=_params(*(("arbitrary",) * len(grid))))

    def run(*args):
        res = call(*args, *rider.inputs)
        return res[:n_out], res[n_out:]

    return run


def _ride_gather_ici(packs):
    n = len(packs)

    def sends(ins, outs, send_sems, recv_sems):
        x, y, c, chips = _place()
        return [_remote(ins[a].at[c], outs[a].at[2 * x + y, c], send_sems, recv_sems, 3 * a + j, (*chip, c))
                for a in range(n) for j, chip in enumerate(chips)]

    def recvs(ins, outs, send_sems, recv_sems):
        x, y, c, chips = _place()
        return [_remote(ins[a].at[c], outs[a].at[2 * chip[0] + chip[1], c], send_sems, recv_sems, 3 * a + j, (x, y, c))
                for a in range(n) for j, chip in enumerate(chips)]

    return _Rider(packs, [jax.ShapeDtypeStruct((N_CHIP,) + p.shape, p.dtype) for p in packs], 3 * n, sends, recvs)


def _ride_gather_d2d(gathered):
    n = len(gathered)

    def copies(landing_half, to):
        def build(ins, outs, send_sems, recv_sems):
            x, y, c, chips = _place()
            return [_remote(ins[a].at[2 * chip[0] + chip[1], c], outs[a].at[2 * chip[0] + chip[1], landing_half(c)],
                            send_sems, recv_sems, 3 * a + j, to(x, y, c))
                    for a in range(n) for j, chip in enumerate(chips)]
        return build

    return _Rider(gathered, [jax.ShapeDtypeStruct(g.shape, g.dtype) for g in gathered], 3 * n,
                  copies(lambda c: c, lambda x, y, c: (x, y, 1 - c)), copies(lambda c: 1 - c, lambda x, y, c: (x, y, c)),
                  aliases={a: a for a in range(n)})


def _ride_exchange(gs):
    n = len(gs)
    shapes = [g.shape[1:] if g.ndim == 4 else g.shape[:2] + (g.shape[2] // 2,) for g in gs]

    def copies(ins, outs, send_sems, recv_sems):
        x, y, c, _ = _place()

        def theirs(a):
            if gs[a].ndim == 4:
                return ins[a].at[1 - c]
            cols = shapes[a][2]
            return ins[a].at[:, :, pl.ds((1 - c) * cols, cols)]

        return [_remote(theirs(a), outs[a], send_sems, recv_sems, a, (x, y, 1 - c)) for a in range(n)]

    return _Rider(gs, [jax.ShapeDtypeStruct(sh, g.dtype) for sh, g in zip(shapes, gs)], n, copies, copies)


def _ride_scatter(b16s):
    n = len(b16s)

    def sends(ins, outs, send_sems, recv_sems):
        x, y, c, chips = _place()
        return [_remote(ins[a].at[2 * chip[0] + chip[1]], outs[a].at[2 * x + y], send_sems, recv_sems, 3 * a + j, (*chip, c))
                for a in range(n) for j, chip in enumerate(chips)]

    def recvs(ins, outs, send_sems, recv_sems):
        x, y, c, chips = _place()
        return [_remote(ins[a].at[2 * x + y], outs[a].at[2 * chip[0] + chip[1]], send_sems, recv_sems, 3 * a + j, (x, y, c))
                for a in range(n) for j, chip in enumerate(chips)]

    return _Rider(b16s, [jax.ShapeDtypeStruct(b.shape, b.dtype) for b in b16s], 3 * n, sends, recvs)


def _slab_tile(r, cols):
    tr = _tile(r, 256, 16)
    if tr % 16 == 0:
        return tr, cols
    return r, _tile(cols, 128)


def add_halves(g, got, idx, name):
    ns, r, cols = got.shape
    tr, tc = _slab_tile(r, cols)
    if g.ndim == 4:
        mine = pl.BlockSpec((None, None, tr, tc), lambda s, i, j, idx_ref: (idx_ref[0], s, i, j))
    else:
        mine = pl.BlockSpec((None, tr, tc), lambda s, i, j, idx_ref: (s, i, idx_ref[0] * (cols // tc) + j))

    def body(idx_ref, a_ref, b_ref, o32_ref, o16_ref):
        s = a_ref[...] + b_ref[...]
        o32_ref[...] = s
        o16_ref[...] = s.astype(BF16)

    blk = pl.BlockSpec((None, tr, tc), lambda s, i, j, idx_ref: (s, i, j))
    return pl.pallas_call(
        body, name=name,
        grid_spec=pltpu.PrefetchScalarGridSpec(
            num_scalar_prefetch=1, grid=(ns, r // tr, cols // tc),
            in_specs=[mine, blk], out_specs=[blk, blk]),
        out_shape=[jax.ShapeDtypeStruct((ns, r, cols), F32), jax.ShapeDtypeStruct((ns, r, cols), BF16)],
        compiler_params=_params("parallel", "parallel", "parallel"),
    )(idx, g, got)


def add_chips(a32, got16, idx, name):
    ns, r, cols = a32.shape
    tr, tc = _slab_tile(r, cols)

    def body(idx_ref, a_ref, r1_ref, r2_ref, r3_ref, o_ref):
        o_ref[...] = ((a_ref[...] + r1_ref[...].astype(F32)) + r2_ref[...].astype(F32)) + r3_ref[...].astype(F32)

    def slab(k):
        return pl.BlockSpec((None, tr, tc), lambda i, j, idx_ref: ((idx_ref[1] + k) % ns, i, j))

    return pl.pallas_call(
        body, name=name,
        grid_spec=pltpu.PrefetchScalarGridSpec(
            num_scalar_prefetch=1, grid=(r // tr, cols // tc), in_specs=[slab(0), slab(1), slab(2), slab(3)],
            out_specs=pl.BlockSpec((tr, tc), lambda i, j, idx_ref: (i, j))),
        out_shape=jax.ShapeDtypeStruct((r, cols), F32),
        compiler_params=_params("parallel", "parallel"),
    )(idx, a32, got16, got16, got16)


def share_halves(halves):
    n = len(halves)

    def body(*refs):
        in_refs, out_refs, (send_sems, recv_sems) = refs[:n], refs[n:2 * n], refs[2 * n:]
        x, y, c, _ = _place()
        cps = [pltpu.make_async_remote_copy(src_ref=in_refs[a], dst_ref=out_refs[a], send_sem=send_sems.at[a],
                                            recv_sem=recv_sems.at[a], device_id=(x, y, 1 - c), device_id_type=MESH)
               for a in range(n)]
        for cp in cps:
            cp.start()
        for cp in cps:
            cp.wait()

    return pl.pallas_call(
        body, name="share_halves", in_specs=[_HBM] * n, out_specs=[_HBM] * n,
        out_shape=[jax.ShapeDtypeStruct(h.shape, F32) for h in halves],
        scratch_shapes=[pltpu.SemaphoreType.DMA((n,)), pltpu.SemaphoreType.DMA((n,))],
    )(*halves)


def allreduce_small(v):
    R, _ = v.shape

    def body(in_ref, out_ref, slots, send_sems, recv_sems):
        x, y, c, _ = _place()
        me = 4 * x + 2 * y + c
        slots[me] = in_ref[...]
        cps = []
        for k in range(1, N_DEV):
            to = (x ^ (k >> 2), y ^ ((k >> 1) & 1), c ^ (k & 1))
            cps.append(pltpu.make_async_remote_copy(src_ref=in_ref, dst_ref=slots.at[me], send_sem=send_sems.at[k - 1],
                                                    recv_sem=recv_sems.at[k - 1], device_id=to, device_id_type=MESH))
        for cp in cps:
            cp.start()
        for k in range(1, N_DEV):
            frm = 4 * (x ^ (k >> 2)) + 2 * (y ^ ((k >> 1) & 1)) + (c ^ (k & 1))
            pltpu.make_async_remote_copy(src_ref=in_ref, dst_ref=slots.at[frm], send_sem=send_sems.at[k - 1],
                                         recv_sem=recv_sems.at[k - 1], device_id=(x, y, c), device_id_type=MESH).wait_recv()
        for cp in cps:
            cp.wait_send()
        acc = slots[0]
        for d in range(1, N_DEV):
            acc = acc + slots[d]
        out_ref[...] = acc

    vm = pl.BlockSpec(memory_space=pltpu.VMEM)
    return pl.pallas_call(
        body, name="allreduce_small", in_specs=[vm], out_specs=vm, out_shape=jax.ShapeDtypeStruct((R, ROW), F32),
        scratch_shapes=[pltpu.VMEM((N_DEV, R, ROW), F32), pltpu.SemaphoreType.DMA((N_DEV - 1,)),
                        pltpu.SemaphoreType.DMA((N_DEV - 1,))],
    )(v)


def _rows_of(n, unit=16):
    return -(-n // (unit * ROW)) * unit


def _pack_rows(items, total_rows, dtype, unit=16):
    parts = []
    used = 0
    for a in items:
        flat = a.reshape(-1)
        r = _rows_of(flat.shape[0], unit)
        flat = jnp.pad(flat, (0, r * ROW - flat.shape[0]))
        parts.append(flat.reshape(r, ROW))
        used += r
    if total_rows > used:
        parts.append(jnp.zeros((total_rows - used, ROW), dtype))
    return jnp.concatenate(parts, axis=0)


def _unpack_rows(buf, shapes, unit=16):
    lead = buf.shape[:-2]
    out = []
    off = 0
    for shp in shapes:
        n = math.prod(shp)
        r = _rows_of(n, unit)
        piece = buf[..., off:off + r, :].reshape(*lead, r * ROW)[..., :n].reshape(*lead, *shp)
        out.append(piece)
        off += r
    return out


def _interleave_heads(w, H):
    lead = w.shape[:-1]
    return w.reshape(*lead, 3, H, HEAD).swapaxes(-3, -2).reshape(*lead, 3 * H * HEAD)


def _deinterleave_heads(w, H):
    lead = w.shape[:-1]
    return w.reshape(*lead, H, 3, HEAD).swapaxes(-3, -2).reshape(*lead, 3 * H * HEAD)


def _interleave_head_rows(w, H):
    return w.reshape(3, H, HEAD, w.shape[-1]).swapaxes(0, 1).reshape(3 * H * HEAD, w.shape[-1])


def _deinterleave_head_rows(w, H):
    return w.reshape(H, 3, HEAD, w.shape[-1]).swapaxes(0, 1).reshape(3 * H * HEAD, w.shape[-1])


def kernel(x, norm_mix, w_in, fox_f_bias, gdn_conv_w, gdn_a_log, gdn_dt_bias, gdn_norm, w_branch_fox, w_branch_gdn, w_out, norm_ffn, w_up, ffn_conv_w, w_down, norm_final, loss_target, m_norm_mix, m_w_in, m_fox_f_bias, m_gdn_conv_w, m_gdn_a_log, m_gdn_dt_bias, m_gdn_norm, m_w_branch_fox, m_w_branch_gdn, m_w_out, m_norm_ffn, m_w_up, m_ffn_conv_w, m_w_down, m_norm_final, v_norm_mix, v_w_in, v_fox_f_bias, v_gdn_conv_w, v_gdn_a_log, v_gdn_dt_bias, v_gdn_norm, v_w_branch_fox, v_w_branch_gdn, v_w_out, v_norm_ffn, v_w_up, v_ffn_conv_w, v_w_down, v_norm_final):
    B, S, D = x.shape
    T = B * S
    H = D // HEAD
    N = S // CHUNK
    FF = w_down.shape[1] * N_CHIP
    d_in = 9 * D + 3 * H
    assert w_in.shape[2] * N_CHIP == d_in and 3 * H <= 128

    cidx = lax.axis_index("c").astype(jnp.int32)
    sidx = (2 * lax.axis_index("x") + lax.axis_index("y")).astype(jnp.int32)
    idx = jnp.stack([cidx, sidx])

    rowed = [w_branch_fox[0], w_branch_gdn[0], w_out[0], w_down[0]]
    convs = [gdn_conv_w[0], ffn_conv_w[0]]
    rowed_shapes = [a.shape for a in rowed]
    conv_shapes = [a.shape + (2,) for a in convs]
    pad_rows = lambda shapes: -(-sum(_rows_of(math.prod(s)) for s in shapes) // 256) * 128
    Rh, Rc = pad_rows(rowed_shapes), pad_rows(conv_shapes)
    halves = lambda a: a.reshape(2, a.shape[0] // 2, a.shape[1])
    c_in = w_in.shape[2]
    packs_a = [w_in[0].T.astype(BF16).reshape(c_in, 2, D // 2).transpose(1, 0, 2),
               halves(_pack_rows([lax.bitcast_convert_type(a, BF16) for a in convs], 2 * Rc, BF16))]
    packs_b = [halves(w_up[0].astype(BF16)), halves(_pack_rows([a.astype(BF16) for a in rowed], 2 * Rh, BF16))]
    own = lambda gs, ps: [lax.dynamic_update_slice(g, p[None], (sidx, 0, 0, 0)) for g, p in zip(gs, ps)]
    by_cols = lambda g: g.transpose(1, 2, 0, 3).reshape(2 * g.shape[2], N_CHIP * g.shape[3])
    cat_cols = lambda p: jnp.concatenate([p[i] for i in range(N_CHIP)], axis=-1)
    cat_rows = lambda p: p.reshape(-1, p.shape[-1])
    g_in, g_conv = own(allgather_weights(packs_a), packs_a)
    W_inT = g_in.transpose(0, 2, 1, 3).reshape(N_CHIP * c_in, D)
    conv_parts = _unpack_rows(g_conv.reshape(N_CHIP, 2 * Rc, ROW), conv_shapes)
    gconv = cat_cols(lax.bitcast_convert_type(conv_parts[0], F32))
    fconv = cat_cols(lax.bitcast_convert_type(conv_parts[1], F32))

    o1, o2 = 3 * D, 3 * D + H
    o3, o4, o5, o6 = o2 + 3 * D, o2 + 3 * D + H, o2 + 3 * D + 2 * H, o2 + 4 * D + 2 * H
    W_foxT = _interleave_head_rows(W_inT[:o1], H)
    W_gqkvT = _interleave_head_rows(W_inT[o2:o3], H)
    W_gzT = W_inT[o5:o6]
    W_gatesT = W_inT[o6:]
    W_smallT = jnp.concatenate([W_inT[o1:o2], W_inT[o3:o5], jnp.zeros((128 - 3 * H, D), BF16)], axis=0)
    gconv_i = _interleave_heads(gconv, H)
    fconv_g, fconv_v = fconv[:, :FF], fconv[:, FF:]
    prm = jnp.zeros((8, 128), F32)
    prm = prm.at[0, 0:H].set(fox_f_bias[0]).at[0, H:2 * H].set(gdn_dt_bias[0]).at[1, H:2 * H].set(gdn_a_log[0])

    x2 = x.reshape(T, D)
    tgt = loss_target.reshape(T, D)

    hn1 = rmsnorm_fwd(x2, norm_mix, "rmsnorm_mix")
    p_fox = matmul(hn1, W_foxT, "nt", "proj_fox", out_dtype=BF16)
    p_gqkv = matmul(hn1, W_gqkvT, "nt", "proj_gqkv")
    p_gz = matmul(hn1, W_gzT, "nt", "proj_gz")
    p_gates = matmul(hn1, W_gatesT, "nt", "proj_gates")
    p_small = matmul(hn1, W_smallT, "nt", "proj_small")

    sm = small_fwd(p_small, prm, B, S, H)
    heads = lambda a: a.reshape(B, S, H).transpose(0, 2, 1)
    c_bhs, gc_bhs, beta_bhs = heads(sm[:, 0:H]), heads(sm[:, H:2 * H]), heads(sm[:, 2 * H:3 * H])
    c_col, c_row = c_bhs[..., None], c_bhs[:, :, None, :]
    gcr5 = gc_bhs.reshape(B, H, N, 1, CHUNK)
    gcr_u = gc_bhs.reshape(B, H, N // PAIR, 1, PAIR * CHUNK)
    betar_u = beta_bhs.reshape(B, H, N // PAIR, 1, PAIR * CHUNK)

    (o_fox, o_fox16, lse), arriving = fox_fwd(p_fox, c_col, c_row, B, S, H, rider=_ride_gather_ici(packs_b))
    qkvn = gdn_prep_fwd(p_gqkv, gconv_i, B, S, H)
    (u_hat, w_t, t_inv), arrived = gdn_intra_fwd(qkvn, betar_u, gcr_u, B, S, H, rider=_ride_gather_d2d(arriving))
    g_up, g_rowed = own(arrived, packs_b)
    W_up = by_cols(g_up)
    W_up_g, W_up_v = W_up[:, :FF], W_up[:, FF:]
    W_bf, W_bg, W_out, W_down = (cat_rows(p) for p in _unpack_rows(g_rowed.reshape(N_CHIP, 2 * Rh, ROW), rowed_shapes))
    o_gdn, states, y_gdn = gdn_inter_fwd(qkvn, u_hat, w_t, gcr5, p_gz, gdn_norm, B, S, H)
    bf_ = matmul(o_fox16, W_bf, "nn", "branch_fox")
    bg_, y = matmul(y_gdn, W_bg, "nn", "branch_gdn", post=_post_merge(p_gates, bf_))
    h1, hn2 = matmul(y, W_out, "nn", "out_proj", add=x2, post=_post_rmsnorm(norm_ffn))
    up_g = matmul(hn2, W_up_g, "nn", "up_gate")
    up_v = matmul(hn2, W_up_v, "nn", "up_val")
    act = ffn_gate_fwd(up_g, up_v, fconv_g, fconv_v, B, S)
    dh2, dh2_16, loss_cols, d_norm_final = matmul(act, W_down, "nn", "down_proj", add=h1,
                                                  post=_post_loss(norm_final.reshape(1, D), tgt))
    loss = lax.psum(0.5 * jnp.sum(loss_cols) / D, ("x", "y", "c"))

    d_act = matmul(dh2_16, W_down, "nt", "d_act")
    dW_down = matmul(act, dh2_16, "tn", "dw_down")
    d_upg, d_upv, d_fconv_g, d_fconv_v = ffn_gate_bwd(up_g, up_v, fconv_g, fconv_v, d_act, B, S)
    d_hn2 = matmul(d_upg, W_up_g, "nt", "d_hn2_g")
    dh1, dh1_16, d_norm_ffn = matmul(d_upv, W_up_v, "nt", "d_hn2_v", add=d_hn2,
                                     post=_post_rmsnorm_bwd(h1, norm_ffn, dh2, True))
    dW_up = jnp.concatenate([matmul(hn2, d_upg, "tn", "dw_up_g"), matmul(hn2, d_upv, "tn", "dw_up_v")], axis=1)
    d_bf, d_bg, d_gates = matmul(dh1_16, W_out, "nt", "d_y", post=_post_merge_bwd(p_gates, bf_, bg_))
    dW_out = matmul(y, dh1_16, "tn", "dw_out")
    d_ofox = matmul(d_bf, W_bf, "nt", "d_ofox")
    dW_bf = matmul(o_fox16, d_bf, "tn", "dw_bf")
    d_ygdn = matmul(d_bg, W_bg, "nt", "d_ygdn")
    dW_bg = matmul(y_gdn, d_bg, "tn", "dw_bg")

    d_fconv = jnp.concatenate([d_fconv_g, d_fconv_v], axis=1)
    col_shard = lambda g, s: g[:, s * (g.shape[1] // N_CHIP):(s + 1) * (g.shape[1] // N_CHIP)]
    row_shard = lambda g, s: g[s * (g.shape[0] // N_CHIP):(s + 1) * (g.shape[0] // N_CHIP)]
    shard_items = lambda s: [row_shard(dW_bf, s), row_shard(dW_bg, s), row_shard(dW_out, s), row_shard(dW_down, s),
                             col_shard(d_fconv, s)]
    g_shapes = [a.shape for a in shard_items(0)]
    assert sum(_rows_of(math.prod(s)) for s in g_shapes) <= 2 * Rh
    to_slabs = lambda g: g.reshape(2, g.shape[0] // 2, N_CHIP, g.shape[1] // N_CHIP).transpose(0, 2, 1, 3)
    gpacks_b = [to_slabs(dW_up),
                jnp.stack([_pack_rows(shard_items(s), 2 * Rh, F32).reshape(2, Rh, ROW) for s in range(N_CHIP)], axis=1)]
    (d_pfox, d_ccol, d_crow), gots_b = fox_bwd(p_fox, c_col, c_row, o_fox, lse, d_ofox, B, S, H,
                                              rider=_ride_exchange(gpacks_b))
    sums_b = [add_halves(g, got, idx, "add_halves_b%d" % i) for i, (g, got) in enumerate(zip(gpacks_b, gots_b))]

    (dq_i, dk_i, d_uh, d_wt, dgcr_a, d_gz, d_gn_parts), got16_b = gdn_inter_bwd(
        qkvn, u_hat, w_t, gcr5, states, o_gdn, p_gz, gdn_norm, d_ygdn, B, S, H,
        rider=_ride_scatter([s16 for _, s16 in sums_b]))
    d_gdn_norm = jnp.sum(d_gn_parts[:, :, 0, :], axis=(0, 1))[None]
    mine_b = [add_chips(s32, g16, idx, "add_chips_b%d" % i) for i, ((s32, _), g16) in enumerate(zip(sums_b, got16_b))]
    d_qkvn, d_betar5, dgcr_b = gdn_intra_bwd(qkvn, betar_u, gcr_u, t_inv, d_uh, d_wt, dq_i, dk_i, B, S, H)
    d_pgqkv, d_gconv_i = gdn_prep_bwd(p_gqkv, gconv_i, d_qkvn, B, S, H)

    tokens = lambda a: a.reshape(B, H, S).transpose(0, 2, 1).reshape(T, H)
    d_gc = dgcr_a.reshape(B, H, S) + dgcr_b.reshape(B, H, S)
    d_sm = jnp.concatenate([tokens(d_ccol.reshape(B, H, S) + d_crow.reshape(B, H, S)), tokens(d_gc), tokens(d_betar5.reshape(B, H, S)),
                            jnp.zeros((T, 128 - 3 * H), F32)], axis=1)
    d_psmall, d_prm = small_bwd(p_small, prm, d_sm, B, S, H)

    dW_foxT = matmul(d_pfox, hn1, "tn", "dw_fox")
    dW_gqkvT = matmul(d_pgqkv, hn1, "tn", "dw_gqkv")
    dW_gzT = matmul(d_gz, hn1, "tn", "dw_gz")
    dW_gatesT = matmul(d_gates, hn1, "tn", "dw_gates")
    dW_smallT = matmul(d_psmall, hn1, "tn", "dw_small")
    dW_inT = jnp.concatenate([_deinterleave_head_rows(dW_foxT, H), dW_smallT[0:H], _deinterleave_head_rows(dW_gqkvT, H),
                              dW_smallT[H:3 * H], dW_gzT, dW_gatesT], axis=0)
    d_gconv = _deinterleave_heads(d_gconv_i, H)

    gpack_a = [dW_inT.reshape(N_CHIP, c_in, D)]
    d_hn1, gots_a = matmul(d_pfox, W_foxT, "nn", "d_hn1_fox", rider=_ride_exchange(gpack_a))
    sums_a = [add_halves(gpack_a[0], gots_a[0], idx, "add_halves_a")]
    d_hn1, got16_a = matmul(d_pgqkv, W_gqkvT, "nn", "d_hn1_gqkv", add=d_hn1, rider=_ride_scatter([sums_a[0][1]]))
    mine = [add_chips(sums_a[0][0], got16_a[0], idx, "add_chips_a")] + mine_b
    d_hn1 = matmul(d_gz, W_gzT, "nn", "d_hn1_gz", add=d_hn1)
    d_hn1 = matmul(d_gates, W_gatesT, "nn", "d_hn1_gates", add=d_hn1)
    grad_x, d_norm_mix = matmul(d_psmall, W_smallT, "nn", "d_hn1_small", add=d_hn1,
                                post=_post_rmsnorm_bwd(x2, norm_mix, dh1, False))

    others = share_halves(mine)
    g_w_inT, g_up, g_rows = (jnp.concatenate([jnp.where(cidx == 0, h, o), jnp.where(cidx == 0, o, h)], axis=ax)
                             for h, o, ax in zip(mine, others, (1, 0, 0)))
    g_w_in = g_w_inT.T
    g_bf, g_bg, g_out, g_down, g_fconv = _unpack_rows(g_rows, g_shapes)

    small_items = [d_norm_mix, d_norm_ffn, d_norm_final, d_gdn_norm, d_prm, d_gconv]
    small_shapes = [a.shape for a in small_items]
    sv = allreduce_small(_pack_rows(small_items, 0, F32, unit=8))
    g_norm_mix, g_norm_ffn, g_norm_final, g_gdn_norm, g_prm, g_gconv_all = _unpack_rows(sv, small_shapes, unit=8)
    g_norm_final = g_norm_final.reshape(D)
    g_fbias, g_dtb, g_alog = g_prm[0:1, 0:H], g_prm[0:1, H:2 * H], g_prm[1:2, H:2 * H]
    g_gconv = lax.dynamic_slice_in_dim(g_gconv_all, sidx * (3 * D // N_CHIP), 3 * D // N_CHIP, axis=1)

    names = ["norm_mix", "w_in", "fox_f_bias", "gdn_conv_w", "gdn_a_log", "gdn_dt_bias", "gdn_norm", "w_branch_fox",
             "w_branch_gdn", "w_out", "norm_ffn", "w_up", "ffn_conv_w", "w_down", "norm_final"]
    ws = [norm_mix, w_in, fox_f_bias, gdn_conv_w, gdn_a_log, gdn_dt_bias, gdn_norm, w_branch_fox, w_branch_gdn, w_out,
          norm_ffn, w_up, ffn_conv_w, w_down, norm_final]
    ms = [m_norm_mix, m_w_in, m_fox_f_bias, m_gdn_conv_w, m_gdn_a_log, m_gdn_dt_bias, m_gdn_norm, m_w_branch_fox,
          m_w_branch_gdn, m_w_out, m_norm_ffn, m_w_up, m_ffn_conv_w, m_w_down, m_norm_final]
    vs = [v_norm_mix, v_w_in, v_fox_f_bias, v_gdn_conv_w, v_gdn_a_log, v_gdn_dt_bias, v_gdn_norm, v_w_branch_fox,
          v_w_branch_gdn, v_w_out, v_norm_ffn, v_w_up, v_ffn_conv_w, v_w_down, v_norm_final]
    gs = [g_norm_mix, g_w_in, g_fbias, g_gconv, g_alog, g_dtb, g_gdn_norm, g_bf, g_bg, g_out, g_norm_ffn, g_up,
          g_fconv, g_down, g_norm_final]
    gs = [g.reshape(w.shape) for g, w in zip(gs, ws)]
    deltas, new_ms, new_vs = [], [], []
    for nm, w, g, m, v in zip(names, ws, gs, ms, vs):
        if w.ndim == 1:
            d, a, b = adamw(w.reshape(1, -1), g.reshape(1, -1), m.reshape(1, -1), v.reshape(1, -1), "adamw_" + nm)
            d, a, b = d.reshape(w.shape), a.reshape(w.shape), b.reshape(w.shape)
        elif nm == "w_in":
            d, a, b = (r.T[None] for r in adamw(w[0].T, g_w_inT, m[0].T, v[0].T, "adamw_" + nm))
        else:
            d, a, b = adamw(w, g, m, v, "adamw_" + nm)
        deltas.append(d)
        new_ms.append(a)
        new_vs.append(b)

    return (loss, grad_x.reshape(B, S, D), *gs, *deltas, *new_ms, *new_vs)
```

```python
import functools
import math

import jax
import jax.numpy as jnp
from jax import lax
from jax.experimental import pallas as pl
from jax.experimental.pallas import tpu as pltpu

F32 = jnp.float32
BF16 = jnp.bfloat16
HEAD = 128
CHUNK = 64
GDN_CONV = 4
FFN_CONV = 3
EPS = 1e-6
NEG = -1e30
ROW = 1024
ATT_TILE = 512
MM_WEIGHT_TILE_BYTES = 8 << 20
N_CHIP = 4
N_DEV = 8
MESH = pl.DeviceIdType.MESH
HI = lax.Precision.HIGH
EXACT = lax.Precision.HIGHEST

ADAM_LR, ADAM_B1, ADAM_B2, ADAM_EPS, ADAM_WD, ADAM_STEP = 0.001, 0.9, 0.999, 1e-08, 0.01, 10


def _tile(n, cap, unit=128):
    best = None
    t = unit
    while t <= min(n, cap):
        if n % t == 0:
            best = t
        t += unit
    return best if best is not None else n


def _params(*sem):
    return pltpu.CompilerParams(dimension_semantics=sem)


_NN = (((1,), (0,)), ((), ()))
_NT = (((1,), (1,)), ((), ()))
_TN = (((0,), (0,)), ((), ()))


def _dg(a, b, dims, hi):
    if hi:
        return lax.dot_general(a, b, dims, precision=HI, preferred_element_type=F32)
    return lax.dot_general(a.astype(BF16), b.astype(BF16), dims, preferred_element_type=F32)


class _RawOps:
    @staticmethod
    def nn(a, b, hi=False):
        return _dg(a, b, _NN, hi)

    @staticmethod
    def nt(a, b, hi=False):
        return _dg(a, b, _NT, hi)

    @staticmethod
    def tn(a, b, hi=False):
        return _dg(a, b, _TN, hi)


def _make_diff_ops():
    def build(hi):
        @jax.custom_vjp
        def nn(a, b):
            return _dg(a, b, _NN, hi)

        nn.defvjp(lambda a, b: (_dg(a, b, _NN, hi), (a, b)),
                  lambda r, g: (_dg(g, r[1], _NT, hi), _dg(r[0], g, _TN, hi)))

        @jax.custom_vjp
        def nt(a, b):
            return _dg(a, b, _NT, hi)

        nt.defvjp(lambda a, b: (_dg(a, b, _NT, hi), (a, b)),
                  lambda r, g: (_dg(g, r[1], _NN, hi), _dg(g, r[0], _TN, hi)))

        @jax.custom_vjp
        def tn(a, b):
            return _dg(a, b, _TN, hi)

        tn.defvjp(lambda a, b: (_dg(a, b, _TN, hi), (a, b)),
                  lambda r, g: (_dg(r[1], g, _NT, hi), _dg(r[0], g, _NN, hi)))
        return nn, nt, tn

    lo, hi_ = build(False), build(True)

    class _DiffOps:
        @staticmethod
        def nn(a, b, hi=False):
            return (hi_ if hi else lo)[0](a, b)

        @staticmethod
        def nt(a, b, hi=False):
            return (hi_ if hi else lo)[1](a, b)

        @staticmethod
        def tn(a, b, hi=False):
            return (hi_ if hi else lo)[2](a, b)

    return _DiffOps


_DiffOps = _make_diff_ops()


def _sigmoid(x):
    return 1.0 / (1.0 + jnp.exp(-x))


def _mm_tile(n, pref):
    if n % pref == 0:
        return pref
    if n % 1408 == 0:
        return 1408
    return _tile(n, pref)


class _Post:
    def __init__(self, fn, row_ins=(), vec_ins=(), row_outs=(), acc_outs=(), keep_main=True):
        self.fn, self.keep_main = fn, keep_main
        self.row_ins = [r if isinstance(r, tuple) else (r, r.shape[1], 0) for r in row_ins]
        self.vec_ins, self.row_outs, self.acc_outs = list(vec_ins), list(row_outs), list(acc_outs)


def matmul(a, b, mode, name, add=None, out_dtype=F32, post=None, rider=None):
    if mode == "nn":
        (M, K), (K2, N) = a.shape, b.shape
    elif mode == "nt":
        (M, K), (N, K2) = a.shape, b.shape
    else:
        (K, M), (K2, N) = a.shape, b.shape
    assert K == K2, (name, a.shape, b.shape)
    tn = _mm_tile(N, 1024)
    if mode == "tn":
        tm = M if M <= 1408 else _mm_tile(M, 1408)
        tk = _mm_tile(K, 2048)
    else:
        tk = K if K * tn * 2 <= MM_WEIGHT_TILE_BYTES else _mm_tile(K, 1024)
        tm = _mm_tile(M, 1024 if tk <= 2048 and post is None else 512)
    nk = K // tk
    assert post is None or (mode != "tn" and tn == N), name
    dims = {"nn": _NN, "nt": _NT, "tn": _TN}[mode]
    if mode == "tn":
        a_spec = pl.BlockSpec((tk, tm), lambda j, i, k: (k, i))
    else:
        a_spec = pl.BlockSpec((tm, tk), lambda j, i, k: (i, k))
    if mode == "nt":
        b_spec = pl.BlockSpec((tn, tk), lambda j, i, k: (j, k))
    else:
        b_spec = pl.BlockSpec((tk, tn), lambda j, i, k: (k, j))
    o_spec = pl.BlockSpec((tm, tn), lambda j, i, k: (i, j))
    has_add = add is not None
    keep_main = post is None or post.keep_main
    counts = [2 + has_add] + ([len(post.row_ins), len(post.vec_ins)] if post else [0, 0]) + [int(keep_main)]
    counts += ([len(post.row_outs), len(post.acc_outs)] if post else [0, 0]) + [int(nk > 1)]

    def body(*refs):
        parts, p = [], 0
        for cnt in counts:
            parts.append(refs[p:p + cnt])
            p += cnt
        core, row_ins, vec_ins, main, row_outs, acc_outs, acc = parts
        a_ref, b_ref = core[:2]
        prod = lax.dot_general(a_ref[...].astype(BF16), b_ref[...].astype(BF16), dims, preferred_element_type=F32)

        def finish(r):
            if has_add:
                r = r + core[2][...]
            if keep_main:
                main[0][...] = r.astype(out_dtype)
            if post is not None:
                @pl.when(pl.program_id(1) == 0)
                def _():
                    for ref in acc_outs:
                        ref[...] = jnp.zeros_like(ref)

                post.fn(r, row_ins, vec_ins, row_outs, acc_outs)

        if nk == 1:
            finish(prod)
            return
        acc_ref = acc[0]
        k = pl.program_id(2)

        @pl.when(k == 0)
        def _():
            acc_ref[...] = jnp.zeros_like(acc_ref)

        acc_ref[...] += prod

        @pl.when(k == nk - 1)
        def _():
            finish(acc_ref[...])

    in_specs = [a_spec, b_spec] + ([o_spec] if has_add else [])
    args = (a, b) + ((add,) if has_add else ())
    out_specs = [o_spec] if keep_main else []
    out_shape = [jax.ShapeDtypeStruct((M, N), out_dtype)] if keep_main else []
    if post is not None:
        in_specs += [pl.BlockSpec((tm, cols), lambda j, i, k, cb=cb: (i, cb)) for _, cols, cb in post.row_ins]
        in_specs += [pl.BlockSpec((1, v.shape[1]), lambda j, i, k: (0, 0)) for v in post.vec_ins]
        args += tuple(r for r, _, _ in post.row_ins) + tuple(post.vec_ins)
        out_specs += [pl.BlockSpec((tm, cols), lambda j, i, k: (i, 0)) for cols, _ in post.row_outs]
        out_specs += [pl.BlockSpec((1, cols), lambda j, i, k: (0, 0)) for cols in post.acc_outs]
        out_shape += [jax.ShapeDtypeStruct((M, cols), dt) for cols, dt in post.row_outs]
        out_shape += [jax.ShapeDtypeStruct((1, cols), F32) for cols in post.acc_outs]
    rows_sem = "arbitrary" if post is not None and post.acc_outs else "parallel"
    res = _hosted_call(
        body, rider, name=name, grid=(N // tn, M // tm, nk), in_specs=in_specs, out_specs=out_specs, out_shape=out_shape,
        scratch_shapes=[pltpu.VMEM((tm, tn), F32)] if nk > 1 else [], semantics=("parallel", rows_sem, "arbitrary"),
    )(*args)
    if rider is not None:
        res, carried = res
        return (res[0] if post is None else res), carried
    return res[0] if post is None else res


def rmsnorm_fwd(x, g, name):
    T, D = x.shape
    tm = _tile(T, 512, 8)

    def body(x_ref, g_ref, o_ref):
        xv = x_ref[...]
        r = lax.rsqrt(jnp.mean(xv * xv, axis=-1, keepdims=True) + EPS)
        o_ref[...] = (xv * r * g_ref[...]).astype(BF16)

    return pl.pallas_call(
        body, name=name, grid=(T // tm,),
        in_specs=[pl.BlockSpec((tm, D), lambda i: (i, 0)), pl.BlockSpec((1, D), lambda i: (0, 0))],
        out_specs=pl.BlockSpec((tm, D), lambda i: (i, 0)),
        out_shape=jax.ShapeDtypeStruct((T, D), BF16),
        compiler_params=_params("parallel"),
    )(x, g)


def _post_rmsnorm(g):
    def fn(r, row_ins, vec_ins, row_outs, acc_outs):
        rs = lax.rsqrt(jnp.mean(r * r, axis=-1, keepdims=True) + EPS)
        row_outs[0][...] = (r * rs * vec_ins[0][...]).astype(BF16)

    return _Post(fn, vec_ins=[g], row_outs=[(g.shape[1], BF16)])


def _post_rmsnorm_bwd(x, g, dres, with_bf16):
    D = g.shape[1]

    def fn(dy, row_ins, vec_ins, row_outs, acc_outs):
        xv = row_ins[0][...]
        rs = lax.rsqrt(jnp.mean(xv * xv, axis=-1, keepdims=True) + EPS)
        xh = xv * rs
        acc_outs[0][...] += jnp.sum(dy * xh, axis=0, keepdims=True)
        dxh = dy * vec_ins[0][...]
        dx = row_ins[1][...] + rs * (dxh - xh * jnp.mean(dxh * xh, axis=-1, keepdims=True))
        row_outs[0][...] = dx
        if with_bf16:
            row_outs[1][...] = dx.astype(BF16)

    return _Post(fn, row_ins=[x, dres], vec_ins=[g], row_outs=[(D, F32)] + ([(D, BF16)] if with_bf16 else []),
                 acc_outs=[D], keep_main=False)


def _post_loss(g, target):
    D = g.shape[1]

    def fn(hv, row_ins, vec_ins, row_outs, acc_outs):
        rs = lax.rsqrt(jnp.mean(hv * hv, axis=-1, keepdims=True) + EPS)
        xh = hv * rs
        gv = vec_ins[0][...]
        err = xh * gv - row_ins[0][...]
        acc_outs[0][...] += jnp.sum(err * err, axis=0, keepdims=True)
        dy = err * (1.0 / D)
        acc_outs[1][...] += jnp.sum(dy * xh, axis=0, keepdims=True)
        dxh = dy * gv
        dh = rs * (dxh - xh * jnp.mean(dxh * xh, axis=-1, keepdims=True))
        row_outs[0][...] = dh
        row_outs[1][...] = dh.astype(BF16)

    return _Post(fn, row_ins=[target], vec_ins=[g], row_outs=[(D, F32), (D, BF16)], acc_outs=[D, D], keep_main=False)


def _shift_down(x, k):
    if k == 0:
        return x
    rows = lax.broadcasted_iota(jnp.int32, x.shape, 0)
    return jnp.where(rows >= k, pltpu.roll(x, k, 0), 0.0)


def _shift_up(x, k):
    if k == 0:
        return x
    s = x.shape[0]
    rows = lax.broadcasted_iota(jnp.int32, x.shape, 0)
    return jnp.where(rows < s - k, pltpu.roll(x, s - k, 0), 0.0)


def _conv_fwd(x, w_ref, kw, keep_shifted=False):
    shifted = [_shift_down(x, kw - 1 - i) for i in range(kw - 1)]
    y = x * w_ref[kw - 1:kw, :]
    for i in range(kw - 1):
        y = y + shifted[i] * w_ref[i:i + 1, :]
    return (y, shifted) if keep_shifted else y


def _conv_bwd(x, shifted, dy, w_ref, kw):
    dx = dy * w_ref[kw - 1:kw, :]
    dws = []
    for i in range(kw - 1):
        dx = dx + _shift_up(dy, kw - 1 - i) * w_ref[i:i + 1, :]
        dws.append(jnp.sum(dy * shifted[i], axis=0, keepdims=True))
    dws.append(jnp.sum(dy * x, axis=0, keepdims=True))
    return dx, dws


def ffn_gate_fwd(up_g, up_v, cw_g, cw_v, B, S):
    T, Fd = up_g.shape
    tc = _tile(Fd, 256)

    def body(g_ref, v_ref, wg_ref, wv_ref, o_ref):
        ug = _conv_fwd(g_ref[...], wg_ref, FFN_CONV)
        uv = _conv_fwd(v_ref[...], wv_ref, FFN_CONV)
        o_ref[...] = (ug * _sigmoid(ug) * uv).astype(BF16)

    blk = pl.BlockSpec((S, tc), lambda b, j: (b, j))
    wblk = pl.BlockSpec((FFN_CONV, tc), lambda b, j: (0, j))
    return pl.pallas_call(
        body, name="ffn_gate_fwd", grid=(B, Fd // tc), in_specs=[blk, blk, wblk, wblk], out_specs=blk,
        out_shape=jax.ShapeDtypeStruct((T, Fd), BF16), compiler_params=_params("parallel", "parallel"),
    )(up_g, up_v, cw_g, cw_v)


def ffn_gate_bwd(up_g, up_v, cw_g, cw_v, d_act, B, S):
    T, Fd = up_g.shape
    tc = _tile(Fd, 256)

    def body(g_ref, v_ref, wg_ref, wv_ref, da_ref, dg_ref, dv_ref, dwg_ref, dwv_ref):
        @pl.when(pl.program_id(1) == 0)
        def _():
            dwg_ref[...] = jnp.zeros_like(dwg_ref)
            dwv_ref[...] = jnp.zeros_like(dwv_ref)

        xg, xv = g_ref[...], v_ref[...]
        ug, sh_g = _conv_fwd(xg, wg_ref, FFN_CONV, keep_shifted=True)
        uv, sh_v = _conv_fwd(xv, wv_ref, FFN_CONV, keep_shifted=True)
        da = da_ref[...]
        sg = _sigmoid(ug)
        d_ug = da * uv * (sg + ug * sg * (1.0 - sg))
        d_uv = da * ug * sg
        dxg, dwg = _conv_bwd(xg, sh_g, d_ug, wg_ref, FFN_CONV)
        dxv, dwv = _conv_bwd(xv, sh_v, d_uv, wv_ref, FFN_CONV)
        dg_ref[...] = dxg.astype(BF16)
        dv_ref[...] = dxv.astype(BF16)
        for i in range(FFN_CONV):
            dwg_ref[i:i + 1, :] += dwg[i]
            dwv_ref[i:i + 1, :] += dwv[i]

    blk = pl.BlockSpec((S, tc), lambda j, b: (b, j))
    wblk = pl.BlockSpec((FFN_CONV, tc), lambda j, b: (0, j))
    return pl.pallas_call(
        body, name="ffn_gate_bwd", grid=(Fd // tc, B), in_specs=[blk, blk, wblk, wblk, blk],
        out_specs=[blk, blk, wblk, wblk],
        out_shape=[jax.ShapeDtypeStruct((T, Fd), BF16), jax.ShapeDtypeStruct((T, Fd), BF16),
                   jax.ShapeDtypeStruct((FFN_CONV, Fd), F32), jax.ShapeDtypeStruct((FFN_CONV, Fd), F32)],
        compiler_params=_params("parallel", "arbitrary"),
    )(up_g, up_v, cw_g, cw_v, d_act)


def _post_merge(p_gates, bf_):
    D = bf_.shape[1]

    def fn(bg, row_ins, vec_ins, row_outs, acc_outs):
        gf_ref, gg_ref, bf_ref = row_ins
        row_outs[0][...] = (_sigmoid(gf_ref[...]) * bf_ref[...] + _sigmoid(gg_ref[...]) * bg).astype(BF16)

    return _Post(fn, row_ins=[(p_gates, D, 0), (p_gates, D, 1), bf_], row_outs=[(D, BF16)])


def _post_merge_bwd(p_gates, bf_, bg_):
    D = bf_.shape[1]

    def fn(d, row_ins, vec_ins, row_outs, acc_outs):
        gf_ref, gg_ref, bf_ref, bg_ref = row_ins
        sf, sg = _sigmoid(gf_ref[...]), _sigmoid(gg_ref[...])
        row_outs[0][...] = (d * sf).astype(BF16)
        row_outs[1][...] = (d * sg).astype(BF16)
        row_outs[2][:, 0:D] = (d * bf_ref[...] * sf * (1.0 - sf)).astype(BF16)
        row_outs[2][:, D:2 * D] = (d * bg_ref[...] * sg * (1.0 - sg)).astype(BF16)

    return _Post(fn, row_ins=[(p_gates, D, 0), (p_gates, D, 1), bf_, bg_],
                 row_outs=[(D, BF16), (D, BF16), (2 * D, BF16)], keep_main=False)


def fox_fwd(p_fox, c_col, c_row, B, S, H, rider=None):
    T = B * S
    t = _tile(S, ATT_TILE)
    nq = S // t
    scale = HEAD ** -0.5

    def body(q_ref, k_ref, v_ref, cq_ref, cr_ref, o_ref, o16_ref, lse_ref):
        i = pl.program_id(2)
        q = q_ref[...]
        cq = cq_ref[...]
        row = lax.broadcasted_iota(jnp.int32, (t, t), 0)
        col = lax.broadcasted_iota(jnp.int32, (t, t), 1)

        def step(j, carry, diagonal):
            m, l, acc = carry
            off = pl.multiple_of(j * t, t)
            k = k_ref[pl.ds(off, t), :]
            v = v_ref[pl.ds(off, t), :]
            s = lax.dot_general(q, k, _NT, preferred_element_type=F32) * scale - cr_ref[:, pl.ds(off, t)]
            if diagonal:
                s = jnp.where(col <= row, s, NEG)
            m_new = jnp.maximum(m, jnp.max(s, axis=-1, keepdims=True))
            alpha = jnp.exp(m - m_new)
            p = jnp.exp(s - m_new)
            l = alpha * l + jnp.sum(p, axis=-1, keepdims=True)
            acc = alpha * acc + lax.dot_general(p.astype(BF16), v, _NN, preferred_element_type=F32)
            return m_new, l, acc

        m0 = jnp.full((t, 1), NEG, F32)
        below = lax.fori_loop(0, i, functools.partial(step, diagonal=False),
                              (m0, jnp.zeros((t, 1), F32), jnp.zeros((t, HEAD), F32)))
        m, l, acc = step(i, below, diagonal=True)
        o = acc / l
        o_ref[...] = o
        o16_ref[...] = o.astype(BF16)
        lse_ref[...] = cq + m + jnp.log(l)

    return _hosted_call(
        body, rider, name="fox_fwd", grid=(B, H, nq),
        in_specs=[pl.BlockSpec((t, HEAD), lambda b, h, i: (b * nq + i, 3 * h)),
                  pl.BlockSpec((S, HEAD), lambda b, h, i: (b, 3 * h + 1)),
                  pl.BlockSpec((S, HEAD), lambda b, h, i: (b, 3 * h + 2)),
                  pl.BlockSpec((None, None, t, 1), lambda b, h, i: (b, h, i, 0)),
                  pl.BlockSpec((None, None, 1, S), lambda b, h, i: (b, h, 0, 0))],
        out_specs=[pl.BlockSpec((t, HEAD), lambda b, h, i: (b * nq + i, h)),
                   pl.BlockSpec((t, HEAD), lambda b, h, i: (b * nq + i, h)),
                   pl.BlockSpec((None, None, t, 1), lambda b, h, i: (b, h, i, 0))],
        out_shape=[jax.ShapeDtypeStruct((T, H * HEAD), F32), jax.ShapeDtypeStruct((T, H * HEAD), BF16),
                   jax.ShapeDtypeStruct((B, H, S, 1), F32)],
        scratch_shapes=[], semantics=("parallel", "parallel", "arbitrary"),
    )(p_fox, p_fox, p_fox, c_col, c_row)


def fox_bwd(p_fox, c_col, c_row, o, lse, do, B, S, H, rider=None):
    T = B * S
    t = _tile(S, ATT_TILE)
    n = S // t
    scale = HEAD ** -0.5

    def body(q_ref, k_ref, v_ref, cq_ref, cr_ref, o_ref, lse_ref, do_ref, dqkv_ref, dcq_ref, dcr_ref,
             dq_acc, delta_s, lse_s):
        row = lax.broadcasted_iota(jnp.int32, (t, t), 0)
        col = lax.broadcasted_iota(jnp.int32, (t, t), 1)

        def prep(i, c):
            rows = pl.ds(pl.multiple_of(i * t, t), t)
            delta_s[rows, :] = jnp.sum(do_ref[rows, :] * o_ref[rows, :], axis=-1, keepdims=True)
            lse_s[rows, :] = lse_ref[rows, :] - cq_ref[rows, :]
            dq_acc[rows, :] = jnp.zeros((t, HEAD), F32)
            dcq_ref[rows, :] = jnp.zeros((t, 1), F32)
            return c

        lax.fori_loop(0, n, prep, 0)

        def kv_step(j, c):
            joff = pl.multiple_of(j * t, t)
            k = k_ref[pl.ds(joff, t), :]
            v = v_ref[pl.ds(joff, t), :]
            crj = cr_ref[:, pl.ds(joff, t)]

            def q_step(i, carry, diagonal):
                dk, dv, dc = carry
                rows = pl.ds(pl.multiple_of(i * t, t), t)
                q = q_ref[rows, :]
                dob = do_ref[rows, :].astype(BF16)
                s = lax.dot_general(q, k, _NT, preferred_element_type=F32) * scale - crj
                if diagonal:
                    s = jnp.where(col <= row, s, NEG)
                p = jnp.exp(s - lse_s[rows, :])
                dp = lax.dot_general(dob, v, _NT, preferred_element_type=F32)
                ds = p * (dp - delta_s[rows, :])
                dsb = ds.astype(BF16)
                dv = dv + lax.dot_general(p.astype(BF16), dob, _TN, preferred_element_type=F32)
                dk = dk + lax.dot_general(dsb, q, _TN, preferred_element_type=F32)
                dq_acc[rows, :] += lax.dot_general(dsb, k, _NN, preferred_element_type=F32) * scale
                dc = dc + jnp.sum(ds, axis=0, keepdims=True)
                dcq_ref[rows, :] += jnp.sum(ds, axis=-1, keepdims=True)
                return dk, dv, dc

            z = jnp.zeros((t, HEAD), F32)
            on_diagonal = q_step(j, (z, z, jnp.zeros((1, t), F32)), diagonal=True)
            dk, dv, dc = lax.fori_loop(j + 1, n, functools.partial(q_step, diagonal=False), on_diagonal)
            dqkv_ref[pl.ds(joff, t), HEAD:2 * HEAD] = (dk * scale).astype(BF16)
            dqkv_ref[pl.ds(joff, t), 2 * HEAD:3 * HEAD] = dv.astype(BF16)
            dcr_ref[:, pl.ds(joff, t)] = -dc
            return c

        lax.fori_loop(0, n, kv_step, 0)
        dqkv_ref[:, 0:HEAD] = dq_acc[...].astype(BF16)

    col_spec = pl.BlockSpec((None, None, S, 1), lambda b, h: (b, h, 0, 0))
    row_spec = pl.BlockSpec((None, None, 1, S), lambda b, h: (b, h, 0, 0))
    head = pl.BlockSpec((S, HEAD), lambda b, h: (b, h))
    return _hosted_call(
        body, rider, name="fox_bwd", grid=(B, H),
        in_specs=[pl.BlockSpec((S, HEAD), lambda b, h: (b, 3 * h)),
                  pl.BlockSpec((S, HEAD), lambda b, h: (b, 3 * h + 1)),
                  pl.BlockSpec((S, HEAD), lambda b, h: (b, 3 * h + 2)),
                  col_spec, row_spec, head, col_spec, head],
        out_specs=[pl.BlockSpec((S, 3 * HEAD), lambda b, h: (b, h)), col_spec, row_spec],
        out_shape=[jax.ShapeDtypeStruct((T, 3 * H * HEAD), BF16), jax.ShapeDtypeStruct((B, H, S, 1), F32),
                   jax.ShapeDtypeStruct((B, H, 1, S), F32)],
        scratch_shapes=[pltpu.VMEM((S, HEAD), F32), pltpu.VMEM((S, 1), F32), pltpu.VMEM((S, 1), F32)],
        semantics=("parallel", "parallel"),
    )(p_fox, p_fox, p_fox, c_col, c_row, o, lse, do)


def _small_fn(x, b0, b1, H):
    S = x.shape[0]
    lane = lax.broadcasted_iota(jnp.int32, x.shape, 1)
    z = x + b0
    tail = jnp.log1p(jnp.exp(-jnp.abs(z)))
    softplus = jnp.maximum(z, 0.0) + tail
    logsig = -(jnp.maximum(-z, 0.0) + tail)
    g = -jnp.exp(b1) * softplus
    pre = jnp.where(lane < H, logsig, jnp.where(lane < 2 * H, g, 0.0))
    bl = _tile(S, 256, CHUNK)
    r = lax.broadcasted_iota(jnp.int32, (bl, bl), 0)
    c = lax.broadcasted_iota(jnp.int32, (bl, bl), 1)
    tri = (r >= c).astype(F32)
    tri_chunk = jnp.where((r >= c) & (jnp.right_shift(r, 6) == jnp.right_shift(c, 6)), 1.0, 0.0)
    carry = jnp.zeros((1, x.shape[1]), F32)
    parts = []
    for i in range(S // bl):
        blk = pre[i * bl:(i + 1) * bl, :]
        full = lax.dot_general(tri, blk, _NN, precision=EXACT, preferred_element_type=F32) + carry
        chunked = lax.dot_general(tri_chunk, blk, _NN, precision=EXACT, preferred_element_type=F32)
        parts.append(jnp.where(lane[:bl] < H, full, chunked))
        carry = carry + jnp.sum(blk, axis=0, keepdims=True)
    cum = parts[0] if len(parts) == 1 else jnp.concatenate(parts, axis=0)
    return jnp.where(lane < 2 * H, cum, jnp.where(lane < 3 * H, _sigmoid(x), 0.0))


def small_fwd(p_small, prm, B, S, H):
    T = B * S

    def body(x_ref, p_ref, o_ref):
        o_ref[...] = _small_fn(x_ref[...], p_ref[0:1, :], p_ref[1:2, :], H)

    blk = pl.BlockSpec((S, 128), lambda b: (b, 0))
    return pl.pallas_call(
        body, name="small_fwd", grid=(B,), in_specs=[blk, pl.BlockSpec((8, 128), lambda b: (0, 0))], out_specs=blk,
        out_shape=jax.ShapeDtypeStruct((T, 128), F32), compiler_params=_params("parallel"),
    )(p_small, prm)


def small_bwd(p_small, prm, d_out, B, S, H):
    T = B * S

    def body(x_ref, p_ref, d_ref, dx_ref, dp_ref):
        @pl.when(pl.program_id(0) == 0)
        def _():
            dp_ref[...] = jnp.zeros_like(dp_ref)

        _, vjp = jax.vjp(functools.partial(_small_fn, H=H), x_ref[...], p_ref[0:1, :], p_ref[1:2, :])
        dx, db0, db1 = vjp(d_ref[...])
        dx_ref[...] = dx.astype(BF16)
        dp_ref[0:1, :] += db0
        dp_ref[1:2, :] += db1

    blk = pl.BlockSpec((S, 128), lambda b: (b, 0))
    pblk = pl.BlockSpec((8, 128), lambda b: (0, 0))
    return pl.pallas_call(
        body, name="small_bwd", grid=(B,), in_specs=[blk, pblk, blk], out_specs=[blk, pblk],
        out_shape=[jax.ShapeDtypeStruct((T, 128), BF16), jax.ShapeDtypeStruct((8, 128), F32)],
        compiler_params=_params("arbitrary"),
    )(p_small, prm, d_out)


def gdn_prep_fwd(p_gqkv, cw, B, S, H):
    T = B * S

    def body(x_ref, w_ref, o_ref):
        y = _conv_fwd(x_ref[...], w_ref, GDN_CONV)
        a = y * _sigmoid(y)
        rs = lax.rsqrt(jnp.sum(a * a, axis=-1, keepdims=True) + EPS)
        is_qk = (pl.program_id(1) % 3) < 2
        o_ref[...] = a * jnp.where(is_qk, rs, 1.0)

    blk = pl.BlockSpec((S, HEAD), lambda b, n: (b, n))
    wblk = pl.BlockSpec((GDN_CONV, HEAD), lambda b, n: (0, n))
    return pl.pallas_call(
        body, name="gdn_prep_fwd", grid=(B, 3 * H), in_specs=[blk, wblk], out_specs=blk,
        out_shape=jax.ShapeDtypeStruct((T, 3 * H * HEAD), F32), compiler_params=_params("parallel", "parallel"),
    )(p_gqkv, cw)


def gdn_prep_bwd(p_gqkv, cw, d_out, B, S, H):
    T = B * S

    def body(x_ref, w_ref, d_ref, dx_ref, dw_ref):
        @pl.when(pl.program_id(1) == 0)
        def _():
            dw_ref[...] = jnp.zeros_like(dw_ref)

        x = x_ref[...]
        y, shifted = _conv_fwd(x, w_ref, GDN_CONV, keep_shifted=True)
        sg = _sigmoid(y)
        a = y * sg
        rs = lax.rsqrt(jnp.sum(a * a, axis=-1, keepdims=True) + EPS)
        d = d_ref[...]
        out = a * rs
        da_qk = rs * (d - out * jnp.sum(d * out, axis=-1, keepdims=True))
        is_qk = (pl.program_id(0) % 3) < 2
        da = jnp.where(is_qk, da_qk, d)
        dy = da * (sg + y * sg * (1.0 - sg))
        dx, dws = _conv_bwd(x, shifted, dy, w_ref, GDN_CONV)
        dx_ref[...] = dx.astype(BF16)
        for i in range(GDN_CONV):
            dw_ref[i:i + 1, :] += dws[i]

    blk = pl.BlockSpec((S, HEAD), lambda n, b: (b, n))
    wblk = pl.BlockSpec((GDN_CONV, HEAD), lambda n, b: (0, n))
    return pl.pallas_call(
        body, name="gdn_prep_bwd", grid=(3 * H, B), in_specs=[blk, wblk, blk], out_specs=[blk, wblk],
        out_shape=[jax.ShapeDtypeStruct((T, 3 * H * HEAD), BF16), jax.ShapeDtypeStruct((GDN_CONV, 3 * H * HEAD), F32)],
        compiler_params=_params("parallel", "arbitrary"),
    )(p_gqkv, cw, d_out)


@jax.custom_vjp
def _given_inverse(a, t):
    return t


def _given_inverse_fwd(a, t):
    return t, t


def _given_inverse_bwd(t, g):
    x = _dg(t, g, _TN, True)
    return -_dg(x, t, _NT, True), jnp.zeros_like(t)


_given_inverse.defvjp(_given_inverse_fwd, _given_inverse_bwd)


def _to_col(row):
    n = row.shape[1]
    r = lax.broadcasted_iota(jnp.int32, (n, n), 0)
    c = lax.broadcasted_iota(jnp.int32, (n, n), 1)
    return jnp.sum(jnp.where(r == c, row, 0.0), axis=1, keepdims=True)


def _intra_fn(k, v, beta_r, gcr, ops, t_known=None):
    n = len(k)
    m = k[0].shape[0]
    r = lax.broadcasted_iota(jnp.int32, (m, m), 0)
    c = lax.broadcasted_iota(jnp.int32, (m, m), 1)
    below = (r > c) & (jnp.right_shift(r, 6) == jnp.right_shift(c, 6))
    beta = [_to_col(beta_r[i]) for i in range(n)]
    gcc = [_to_col(gcr[i]) for i in range(n)]
    decay = [jnp.exp(jnp.where(below, gcc[i] - gcr[i], NEG)) for i in range(n)]
    kb = [k[i] * beta[i] for i in range(n)]
    a = [ops.nt(kb[i], k[i]) * decay[i] for i in range(n)]
    if t_known is None:
        p = [-a[i] for i in range(n)]
        tm = [jnp.where(r == c, 1.0, 0.0) + p[i] for i in range(n)]
        for _ in range(5):
            p = [ops.nn(p[i], p[i], hi=True) for i in range(n)]
            tm = [tm[i] + ops.nn(tm[i], p[i], hi=True) for i in range(n)]
    else:
        tm = [_given_inverse(a[i], t_known[i]) for i in range(n)]
    u_hat = [ops.nn(tm[i], v[i] * beta[i], hi=True) for i in range(n)]
    w = [ops.nn(tm[i], kb[i] * jnp.exp(gcc[i]), hi=True) for i in range(n)]
    return tuple(u_hat), tuple(w), tuple(tm)


INTRA_NB = 32
PAIR = 1


def gdn_intra_fwd(qkvn, betar5, gcr5, B, S, H, rider=None):
    T = B * S
    UNIT = PAIR * CHUNK
    N = S // UNIT
    nb = min(INTRA_NB // PAIR, N)
    rows = nb * UNIT
    ns = N // nb

    def body(k_ref, v_ref, b_ref, gr_ref, uh_ref, w_ref, t_ref):
        sls = [slice(ci * UNIT, (ci + 1) * UNIT) for ci in range(nb)]
        u_hat, w, tm = _intra_fn(tuple(k_ref[sl, :] for sl in sls), tuple(v_ref[sl, :] for sl in sls),
                                 tuple(b_ref[ci] for ci in range(nb)), tuple(gr_ref[ci] for ci in range(nb)), _RawOps)
        for ci, sl in enumerate(sls):
            uh_ref[sl, :] = u_hat[ci]
            w_ref[sl, :] = w[ci]
            t_ref[ci] = tm[ci]

    rowspec = pl.BlockSpec((None, None, nb, 1, UNIT), lambda b, h, i: (b, h, i, 0, 0))
    sqspec = pl.BlockSpec((None, None, nb, UNIT, UNIT), lambda b, h, i: (b, h, i, 0, 0))
    out = pl.BlockSpec((rows, HEAD), lambda b, h, i: (b * ns + i, h))
    return _hosted_call(
        body, rider, name="gdn_intra_fwd", grid=(B, H, ns),
        in_specs=[pl.BlockSpec((rows, HEAD), lambda b, h, i: (b * ns + i, 3 * h + 1)),
                  pl.BlockSpec((rows, HEAD), lambda b, h, i: (b * ns + i, 3 * h + 2)),
                  rowspec, rowspec],
        out_specs=[out, out, sqspec],
        out_shape=[jax.ShapeDtypeStruct((T, H * HEAD), F32), jax.ShapeDtypeStruct((T, H * HEAD), F32),
                   jax.ShapeDtypeStruct((B, H, N, UNIT, UNIT), F32)],
        scratch_shapes=[], semantics=("parallel", "parallel", "parallel"),
    )(qkvn, qkvn, betar5, gcr5)


def gdn_intra_bwd(qkvn, betar5, gcr5, t_inv, d_uh, d_w, dq_in, dk_in, B, S, H):
    T = B * S
    UNIT = PAIR * CHUNK
    N = S // UNIT
    nb = min(INTRA_NB // PAIR, N)
    rows = nb * UNIT
    ns = N // nb

    def body(k_ref, v_ref, b_ref, gr_ref, t_ref, duh_ref, dw_ref, dq_ref, dk_ref, o_ref, db_ref, dgr_ref):
        sls = [slice(ci * UNIT, (ci + 1) * UNIT) for ci in range(nb)]
        chunks = range(nb)
        _, vjp = jax.vjp(
            functools.partial(_intra_fn, ops=_DiffOps, t_known=tuple(t_ref[ci] for ci in chunks)),
            tuple(k_ref[sl, :] for sl in sls), tuple(v_ref[sl, :] for sl in sls), tuple(b_ref[ci] for ci in chunks),
            tuple(gr_ref[ci] for ci in chunks))
        zero = jnp.zeros((UNIT, UNIT), F32)
        dk, dv, db, dgr = vjp((tuple(duh_ref[sl, :] for sl in sls), tuple(dw_ref[sl, :] for sl in sls),
                               tuple(zero for _ in chunks)))
        for ci, sl in enumerate(sls):
            o_ref[sl, 0:HEAD] = dq_ref[sl, :]
            o_ref[sl, HEAD:2 * HEAD] = dk[ci] + dk_ref[sl, :]
            o_ref[sl, 2 * HEAD:3 * HEAD] = dv[ci]
            db_ref[ci] = db[ci]
            dgr_ref[ci] = dgr[ci]

    rowspec = pl.BlockSpec((None, None, nb, 1, UNIT), lambda b, h, i: (b, h, i, 0, 0))
    sqspec = pl.BlockSpec((None, None, nb, UNIT, UNIT), lambda b, h, i: (b, h, i, 0, 0))
    head = pl.BlockSpec((rows, HEAD), lambda b, h, i: (b * ns + i, h))
    return pl.pallas_call(
        body, name="gdn_intra_bwd", grid=(B, H, ns),
        in_specs=[pl.BlockSpec((rows, HEAD), lambda b, h, i: (b * ns + i, 3 * h + 1)),
                  pl.BlockSpec((rows, HEAD), lambda b, h, i: (b * ns + i, 3 * h + 2)),
                  rowspec, rowspec, sqspec, head, head, head, head],
        out_specs=[pl.BlockSpec((rows, 3 * HEAD), lambda b, h, i: (b * ns + i, h)), rowspec, rowspec],
        out_shape=[jax.ShapeDtypeStruct((T, 3 * H * HEAD), F32),
                   jax.ShapeDtypeStruct((B, H, N, 1, UNIT), F32), jax.ShapeDtypeStruct((B, H, N, 1, UNIT), F32)],
        compiler_params=_params("parallel", "parallel", "parallel"),
    )(qkvn, qkvn, betar5, gcr5, t_inv, d_uh, d_w, dq_in, dk_in)


def _inter_fn(q, k, u_hat, w, gcr, state, ops):
    n = len(q)
    r = lax.broadcasted_iota(jnp.int32, (CHUNK, CHUNK), 0)
    c = lax.broadcasted_iota(jnp.int32, (CHUNK, CHUNK), 1)
    last = lax.broadcasted_iota(jnp.int32, (1, CHUNK), 1) == CHUNK - 1
    gcc = [_to_col(gcr[i]) for i in range(n)]
    gl = [jnp.sum(jnp.where(last, gcr[i], 0.0), axis=1, keepdims=True) for i in range(n)]
    decay = [jnp.exp(jnp.where(r >= c, gcc[i] - gcr[i], NEG)) for i in range(n)]
    qs = [q[i] * (HEAD ** -0.5) for i in range(n)]
    ws = [ops.nn(w[i], state[i]) for i in range(n)]
    qst = [ops.nn(qs[i] * jnp.exp(gcc[i]), state[i]) for i in range(n)]
    attn = [ops.nt(qs[i], k[i]) * decay[i] for i in range(n)]
    u = [u_hat[i] - ws[i] for i in range(n)]
    o = [qst[i] + ops.nn(attn[i], u[i]) for i in range(n)]
    kdu = [ops.tn(k[i] * jnp.exp(gl[i] - gcc[i]), u[i]) for i in range(n)]
    new_state = [state[i] * jnp.exp(gl[i]) + kdu[i] for i in range(n)]
    return tuple(o), tuple(new_state)


INTER_HEADS = 8
INTER_ROWS = 512
INTER_ROWS_BWD = 256


def _inter_heads(H):
    return INTER_HEADS if H % INTER_HEADS == 0 else (4 if H % 4 == 0 else 1)


def _inter_specs(ts, ns, hp, backward):
    at = (lambda s: ns - 1 - s) if backward else (lambda s: s)
    nc = ts // CHUNK
    qk = []
    for hh in range(hp):
        qk.append(pl.BlockSpec((ts, HEAD), lambda b, g, s, hh=hh: (b * ns + at(s), 3 * (hp * g + hh))))
        qk.append(pl.BlockSpec((ts, HEAD), lambda b, g, s, hh=hh: (b * ns + at(s), 3 * (hp * g + hh) + 1)))
    heads = pl.BlockSpec((ts, hp * HEAD), lambda b, g, s: (b * ns + at(s), g))
    rowspec = pl.BlockSpec((None, hp, nc, 1, CHUNK), lambda b, g, s: (b, g, at(s), 0, 0))
    stspec = pl.BlockSpec((None, hp, nc, HEAD, HEAD), lambda b, g, s: (b, g, at(s), 0, 0))
    return qk, heads, rowspec, stspec


def gdn_inter_fwd(qkvn, u_hat, w, gcr5, p_gz, gnorm, B, S, H):
    T = B * S
    N = S // CHUNK
    hp = _inter_heads(H)
    hs = range(hp)
    ts = _tile(S, INTER_ROWS, CHUNK)
    ns, nc = S // ts, ts // CHUNK

    def body(*refs):
        qk_refs, (uh_ref, w_ref, gr_ref, z_ref, gn_ref, o_ref, st_ref, y_ref, s_scr) = refs[:2 * hp], refs[2 * hp:]

        @pl.when(pl.program_id(2) == 0)
        def _():
            s_scr[...] = jnp.zeros_like(s_scr)

        gn = gn_ref[...]

        def step(n, c):
            rows = pl.ds(pl.multiple_of(n * CHUNK, CHUNK), CHUNK)
            st = tuple(s_scr[hh] for hh in hs)
            for hh in hs:
                st_ref[hh, n] = st[hh]
            o, new = _inter_fn(tuple(qk_refs[2 * hh][rows, :] for hh in hs), tuple(qk_refs[2 * hh + 1][rows, :] for hh in hs),
                               tuple(uh_ref[rows, hh * HEAD:(hh + 1) * HEAD] for hh in hs),
                               tuple(w_ref[rows, hh * HEAD:(hh + 1) * HEAD] for hh in hs),
                               tuple(gr_ref[hh, n] for hh in hs), st, _RawOps)
            for hh in hs:
                cols = slice(hh * HEAD, (hh + 1) * HEAD)
                o_ref[rows, cols] = o[hh]
                s_scr[hh] = new[hh]
                z = z_ref[rows, cols]
                r = lax.rsqrt(jnp.mean(o[hh] * o[hh], axis=-1, keepdims=True) + EPS)
                y_ref[rows, cols] = (o[hh] * r * gn * z * _sigmoid(z)).astype(BF16)
            return c

        lax.fori_loop(0, nc, step, 0)

    qk, heads, rowspec, stspec = _inter_specs(ts, ns, hp, backward=False)
    return pl.pallas_call(
        body, name="gdn_inter_fwd", grid=(B, H // hp, ns),
        in_specs=qk + [heads, heads, rowspec, heads, pl.BlockSpec((1, HEAD), lambda b, g, s: (0, 0))],
        out_specs=[heads, stspec, heads],
        out_shape=[jax.ShapeDtypeStruct((T, H * HEAD), F32), jax.ShapeDtypeStruct((B, H, N, HEAD, HEAD), F32),
                   jax.ShapeDtypeStruct((T, H * HEAD), BF16)],
        scratch_shapes=[pltpu.VMEM((hp, HEAD, HEAD), F32)],
        compiler_params=_params("parallel", "parallel", "arbitrary"),
    )(*([qkvn] * (2 * hp)), u_hat, w, gcr5, p_gz, gnorm)


def gdn_inter_bwd(qkvn, u_hat, w, gcr5, states, o, p_gz, gnorm, d_y, B, S, H, rider=None):
    T = B * S
    N = S // CHUNK
    hp = _inter_heads(H)
    hs = range(hp)
    ts = _tile(S, INTER_ROWS_BWD, CHUNK)
    ns, nc = S // ts, ts // CHUNK

    def body(*refs):
        qk_refs = refs[:2 * hp]
        (uh_ref, w_ref, gr_ref, st_ref, o_ref, z_ref, gn_ref, dy_ref,
         dq_ref, dk_ref, duh_ref, dw_ref, dgr_ref, dz_ref, dgn_ref, ds_scr) = refs[2 * hp:]

        @pl.when(pl.program_id(2) == 0)
        def _():
            ds_scr[...] = jnp.zeros_like(ds_scr)
            dgn_ref[...] = jnp.zeros_like(dgn_ref)

        cols = [slice(hh * HEAD, (hh + 1) * HEAD) for hh in hs]
        gn = gn_ref[...]

        def through_norm(rows, hh):
            ov, z, d = o_ref[rows, cols[hh]], z_ref[rows, cols[hh]], dy_ref[rows, cols[hh]]
            r = lax.rsqrt(jnp.mean(ov * ov, axis=-1, keepdims=True) + EPS)
            xh = ov * r
            sg = _sigmoid(z)
            d_n = d * (z * sg)
            dz_ref[rows, cols[hh]] = (d * xh * gn * (sg + z * sg * (1.0 - sg))).astype(BF16)
            dgn_ref[0:1, :] += jnp.sum(d_n * xh, axis=0, keepdims=True)
            dxh = d_n * gn
            return r * (dxh - xh * jnp.mean(dxh * xh, axis=-1, keepdims=True))

        def step(i, c):
            n = nc - 1 - i
            rows = pl.ds(pl.multiple_of(n * CHUNK, CHUNK), CHUNK)
            _, vjp = jax.vjp(functools.partial(_inter_fn, ops=_DiffOps),
                             tuple(qk_refs[2 * hh][rows, :] for hh in hs), tuple(qk_refs[2 * hh + 1][rows, :] for hh in hs),
                             tuple(uh_ref[rows, cols[hh]] for hh in hs), tuple(w_ref[rows, cols[hh]] for hh in hs),
                             tuple(gr_ref[hh, n] for hh in hs), tuple(st_ref[hh, n] for hh in hs))
            dq, dk, duh, dw, dgr, ds = vjp((tuple(through_norm(rows, hh) for hh in hs), tuple(ds_scr[hh] for hh in hs)))
            for hh in hs:
                dq_ref[rows, cols[hh]] = dq[hh]
                dk_ref[rows, cols[hh]] = dk[hh]
                duh_ref[rows, cols[hh]] = duh[hh]
                dw_ref[rows, cols[hh]] = dw[hh]
                dgr_ref[hh, n] = dgr[hh]
                ds_scr[hh] = ds[hh]
            return c

        lax.fori_loop(0, nc, step, 0)

    qk, heads, rowspec, stspec = _inter_specs(ts, ns, hp, backward=True)
    hshape = jax.ShapeDtypeStruct((T, H * HEAD), F32)
    return _hosted_call(
        body, rider, name="gdn_inter_bwd", grid=(B, H // hp, ns),
        in_specs=qk + [heads, heads, rowspec, stspec, heads, heads, pl.BlockSpec((1, HEAD), lambda b, g, s: (0, 0)), heads],
        out_specs=[heads, heads, heads, heads, rowspec, heads,
                   pl.BlockSpec((None, None, 8, HEAD), lambda b, g, s: (b, g, 0, 0))],
        out_shape=[hshape, hshape, hshape, hshape, jax.ShapeDtypeStruct((B, H, N, 1, CHUNK), F32),
                   jax.ShapeDtypeStruct((T, H * HEAD), BF16), jax.ShapeDtypeStruct((B, H // hp, 8, HEAD), F32)],
        scratch_shapes=[pltpu.VMEM((hp, HEAD, HEAD), F32)], semantics=("parallel", "parallel", "arbitrary"),
    )(*([qkvn] * (2 * hp)), u_hat, w, gcr5, states, o, p_gz, gnorm, d_y)


def adamw(w, g, m, v, name):
    shape = w.shape
    lead = (None,) * (w.ndim - 2)
    zeros = (0,) * (w.ndim - 2)
    R, C = shape[-2:]
    g2 = g.reshape(R, C)
    tr, tc = _tile(R, 128, 8), C
    if tr % 8 and R > 8:
        tr, tc = R, _tile(C, 128)

    def body(w_ref, g_ref, m_ref, v_ref, d_ref, nm_ref, nv_ref):
        gv = g_ref[...]
        nm = ADAM_B1 * m_ref[...] + (1.0 - ADAM_B1) * gv
        nv = ADAM_B2 * v_ref[...] + (1.0 - ADAM_B2) * (gv * gv)
        m_hat = nm / (1.0 - ADAM_B1 ** ADAM_STEP)
        v_hat = nv / (1.0 - ADAM_B2 ** ADAM_STEP)
        d_ref[...] = -ADAM_LR * (m_hat / (jnp.sqrt(v_hat) + ADAM_EPS) + ADAM_WD * w_ref[...])
        nm_ref[...] = nm
        nv_ref[...] = nv

    blk = pl.BlockSpec(lead + (tr, tc), lambda i, j: zeros + (i, j))
    gblk = pl.BlockSpec((tr, tc), lambda i, j: (i, j))
    sh = jax.ShapeDtypeStruct(shape, F32)
    return pl.pallas_call(
        body, name=name, grid=(R // tr, C // tc), in_specs=[blk, gblk, blk, blk], out_specs=[blk] * 3, out_shape=[sh] * 3,
        compiler_params=_params("parallel", "parallel"),
    )(w, g2, m, v)


def _place():
    x, y, c = lax.axis_index("x"), lax.axis_index("y"), lax.axis_index("c")
    chips = [(1 - x, y), (x, 1 - y), (1 - x, 1 - y)]
    return x, y, c, chips


_HBM = pl.BlockSpec(memory_space=pltpu.HBM)


def allgather_weights(packs):
    n = len(packs)

    def body(*refs):
        in_refs, out_refs, (send_sems, recv_sems) = refs[:n], refs[n:2 * n], refs[2 * n:]
        x, y, c, chips = _place()
        me_s = 2 * x + y
        me, sibling = (x, y, c), (x, y, 1 - c)
        shards = [2 * chip[0] + chip[1] for chip in chips]

        def copy(a, k, shard, half, to, src=None):
            dst = out_refs[a].at[shard, half]
            return pltpu.make_async_remote_copy(src_ref=dst if src is None else src, dst_ref=dst,
                                                send_sem=send_sems.at[6 * a + k], recv_sem=recv_sems.at[6 * a + k],
                                                device_id=to, device_id_type=MESH)

        first = [copy(a, j, me_s, c, (*chip, c), src=in_refs[a].at[c]) for a in range(n) for j, chip in enumerate(chips)]
        for cp in first:
            cp.start()
        passed = []
        for a in range(n):
            for j in range(3):
                copy(a, j, shards[j], c, me).wait_recv()
                passed.append(copy(a, 3 + j, shards[j], c, sibling))
                passed[-1].start()
        for a in range(n):
            for j in range(3):
                copy(a, 3 + j, shards[j], 1 - c, me).wait_recv()
        for cp in first + passed:
            cp.wait_send()

    return pl.pallas_call(
        body, name="allgather_weights", in_specs=[_HBM] * n, out_specs=[_HBM] * n,
        out_shape=[jax.ShapeDtypeStruct((N_CHIP,) + p.shape, p.dtype) for p in packs],
        scratch_shapes=[pltpu.SemaphoreType.DMA((6 * n,)), pltpu.SemaphoreType.DMA((6 * n,))],
    )(*packs)


class _Rider:
    def __init__(self, inputs, out_shapes, n_sems, sends, recvs, aliases=None):
        self.inputs, self.out_shapes, self.n_sems = list(inputs), list(out_shapes), n_sems
        self.sends, self.recvs, self.aliases = sends, recvs, aliases or {}

    def start(self, *refs):
        for cp in self.sends(*refs):
            cp.start()

    def wait(self, *refs):
        for cp in self.recvs(*refs):
            cp.wait_recv()
        for cp in self.sends(*refs):
            cp.wait_send()


def _remote(src, dst, send_sems, recv_sems, k, to):
    return pltpu.make_async_remote_copy(src_ref=src, dst_ref=dst, send_sem=send_sems.at[k], recv_sem=recv_sems.at[k],
                                        device_id=to, device_id_type=MESH)


def _run_alone(rider, name):
    ri = len(rider.inputs)

    def body(*refs):
        ins, outs, (send_sems, recv_sems) = refs[:ri], refs[ri:-2], refs[-2:]
        rider.start(ins, outs, send_sems, recv_sems)
        rider.wait(ins, outs, send_sems, recv_sems)

    return pl.pallas_call(
        body, name=name, in_specs=[_HBM] * ri, out_specs=[_HBM] * len(rider.out_shapes), out_shape=rider.out_shapes,
        scratch_shapes=[pltpu.SemaphoreType.DMA((rider.n_sems,))] * 2, input_output_aliases=rider.aliases,
    )(*rider.inputs)


def _hosted_call(body, rider, *, name, grid, in_specs, out_specs, out_shape, scratch_shapes, semantics):
    if rider is None:
        return pl.pallas_call(body, name=name, grid=grid, in_specs=in_specs, out_specs=out_specs, out_shape=out_shape,
                              scratch_shapes=scratch_shapes, compiler_params=_params(*semantics))
    n_in, n_out, n_scr = len(in_specs), len(out_specs), len(scratch_shapes)
    ri, ro = len(rider.inputs), len(rider.out_shapes)

    def hosted(*refs):
        parts, p = [], 0
        for cnt in (n_in, ri, n_out, ro, n_scr, 2):
            parts.append(refs[p:p + cnt])
            p += cnt
        ins, rins, outs, routs, scr, (send_sems, recv_sems) = parts
        first = functools.reduce(jnp.logical_and, [pl.program_id(a) == 0 for a in range(len(grid))])
        last = functools.reduce(jnp.logical_and, [pl.program_id(a) == grid[a] - 1 for a in range(len(grid))])

        @pl.when(first)
        def _():
            rider.start(rins, routs, send_sems, recv_sems)

        body(*ins, *outs, *scr)

        @pl.when(last)
        def _():
            rider.wait(rins, routs, send_sems, recv_sems)

    call = pl.pallas_call(
        hosted, name=name, grid=grid, in_specs=list(in_specs) + [_HBM] * ri, out_specs=list(out_specs) + [_HBM] * ro,
        out_shape=list(out_shape) + rider.out_shapes,
        scratch_shapes=list(scratch_shapes) + [pltpu.SemaphoreType.DMA((rider.n_sems,))] * 2,
        input_output_aliases={n_in + i: n_out + o for i, o in rider.aliases.items()},
        compiler_params=_params(*(("arbitrary",) * len(grid))))

    def run(*args):
        res = call(*args, *rider.inputs)
        return res[:n_out], res[n_out:]

    return run


def _ride_gather_ici(packs):
    n = len(packs)

    def sends(ins, outs, send_sems, recv_sems):
        x, y, c, chips = _place()
        return [_remote(ins[a].at[c], outs[a].at[2 * x + y, c], send_sems, recv_sems, 3 * a + j, (*chip, c))
                for a in range(n) for j, chip in enumerate(chips)]

    def recvs(ins, outs, send_sems, recv_sems):
        x, y, c, chips = _place()
        return [_remote(ins[a].at[c], outs[a].at[2 * chip[0] + chip[1], c], send_sems, recv_sems, 3 * a + j, (x, y, c))
                for a in range(n) for j, chip in enumerate(chips)]

    return _Rider(packs, [jax.ShapeDtypeStruct((N_CHIP,) + p.shape, p.dtype) for p in packs], 3 * n, sends, recvs)


def _ride_gather_d2d(gathered):
    n = len(gathered)

    def copies(landing_half, to):
        def build(ins, outs, send_sems, recv_sems):
            x, y, c, chips = _place()
            return [_remote(ins[a].at[2 * chip[0] + chip[1], c], outs[a].at[2 * chip[0] + chip[1], landing_half(c)],
                            send_sems, recv_sems, 3 * a + j, to(x, y, c))
                    for a in range(n) for j, chip in enumerate(chips)]
        return build

    return _Rider(gathered, [jax.ShapeDtypeStruct(g.shape, g.dtype) for g in gathered], 3 * n,
                  copies(lambda c: c, lambda x, y, c: (x, y, 1 - c)), copies(lambda c: 1 - c, lambda x, y, c: (x, y, c)),
                  aliases={a: a for a in range(n)})


def _ride_exchange(gs):
    n = len(gs)
    shapes = [g.shape[1:] if g.ndim == 4 else g.shape[:2] + (g.shape[2] // 2,) for g in gs]

    def copies(ins, outs, send_sems, recv_sems):
        x, y, c, _ = _place()

        def theirs(a):
            if gs[a].ndim == 4:
                return ins[a].at[1 - c]
            cols = shapes[a][2]
            return ins[a].at[:, :, pl.ds((1 - c) * cols, cols)]

        return [_remote(theirs(a), outs[a], send_sems, recv_sems, a, (x, y, 1 - c)) for a in range(n)]

    return _Rider(gs, [jax.ShapeDtypeStruct(sh, g.dtype) for sh, g in zip(shapes, gs)], n, copies, copies)


def _ride_scatter(b16s, to=(0, 1, 2), landing=None):
    n = len(b16s)

    def sends(ins, outs, send_sems, recv_sems):
        x, y, c, chips = _place()
        return [_remote(ins[a].at[2 * chips[j][0] + chips[j][1]], outs[a].at[2 * x + y], send_sems, recv_sems, 3 * a + j,
                        (*chips[j], c)) for a in range(n) for j in to]

    def recvs(ins, outs, send_sems, recv_sems):
        x, y, c, chips = _place()
        return [_remote(ins[a].at[2 * x + y], outs[a].at[2 * chips[j][0] + chips[j][1]], send_sems, recv_sems, 3 * a + j,
                        (x, y, c)) for a in range(n) for j in to]

    return _Rider(list(b16s) + list(landing or []), [jax.ShapeDtypeStruct(b.shape, b.dtype) for b in b16s], 3 * n,
                  sends, recvs, aliases={n + a: a for a in range(n)} if landing else None)


def _slab_tile(r, cols):
    tr = _tile(r, 256, 16)
    if tr % 16 == 0:
        return tr, cols
    return r, _tile(cols, 128)


def add_halves(g, got, idx, name):
    ns, r, cols = got.shape
    tr, tc = _slab_tile(r, cols)
    if g.ndim == 4:
        mine = pl.BlockSpec((None, None, tr, tc), lambda s, i, j, idx_ref: (idx_ref[0], s, i, j))
    else:
        mine = pl.BlockSpec((None, tr, tc), lambda s, i, j, idx_ref: (s, i, idx_ref[0] * (cols // tc) + j))

    def body(idx_ref, a_ref, b_ref, o32_ref, o16_ref):
        s = a_ref[...] + b_ref[...]
        o32_ref[...] = s
        o16_ref[...] = s.astype(BF16)

    blk = pl.BlockSpec((None, tr, tc), lambda s, i, j, idx_ref: (s, i, j))
    return pl.pallas_call(
        body, name=name,
        grid_spec=pltpu.PrefetchScalarGridSpec(
            num_scalar_prefetch=1, grid=(ns, r // tr, cols // tc),
            in_specs=[mine, blk], out_specs=[blk, blk]),
        out_shape=[jax.ShapeDtypeStruct((ns, r, cols), F32), jax.ShapeDtypeStruct((ns, r, cols), BF16)],
        compiler_params=_params("parallel", "parallel", "parallel"),
    )(idx, g, got)


def add_chips(a32, got16, idx, name):
    ns, r, cols = a32.shape
    tr, tc = _slab_tile(r, cols)

    def body(idx_ref, a_ref, r1_ref, r2_ref, r3_ref, o_ref):
        o_ref[...] = ((a_ref[...] + r1_ref[...].astype(F32)) + r2_ref[...].astype(F32)) + r3_ref[...].astype(F32)

    def slab(k):
        return pl.BlockSpec((None, tr, tc), lambda i, j, idx_ref: ((idx_ref[1] + k) % ns, i, j))

    return pl.pallas_call(
        body, name=name,
        grid_spec=pltpu.PrefetchScalarGridSpec(
            num_scalar_prefetch=1, grid=(r // tr, cols // tc), in_specs=[slab(0), slab(1), slab(2), slab(3)],
            out_specs=pl.BlockSpec((tr, tc), lambda i, j, idx_ref: (i, j))),
        out_shape=jax.ShapeDtypeStruct((r, cols), F32),
        compiler_params=_params("parallel", "parallel"),
    )(idx, a32, got16, got16, got16)


def share_halves(halves):
    n = len(halves)

    def body(*refs):
        in_refs, out_refs, (send_sems, recv_sems) = refs[:n], refs[n:2 * n], refs[2 * n:]
        x, y, c, _ = _place()
        cps = [pltpu.make_async_remote_copy(src_ref=in_refs[a], dst_ref=out_refs[a], send_sem=send_sems.at[a],
                                            recv_sem=recv_sems.at[a], device_id=(x, y, 1 - c), device_id_type=MESH)
               for a in range(n)]
        for cp in cps:
            cp.start()
        for cp in cps:
            cp.wait()

    return pl.pallas_call(
        body, name="share_halves", in_specs=[_HBM] * n, out_specs=[_HBM] * n,
        out_shape=[jax.ShapeDtypeStruct(h.shape, F32) for h in halves],
        scratch_shapes=[pltpu.SemaphoreType.DMA((n,)), pltpu.SemaphoreType.DMA((n,))],
    )(*halves)


def allreduce_small(v):
    R, _ = v.shape

    def body(in_ref, out_ref, slots, send_sems, recv_sems):
        x, y, c, _ = _place()
        me = 4 * x + 2 * y + c
        slots[me] = in_ref[...]
        cps = []
        for k in range(1, N_DEV):
            to = (x ^ (k >> 2), y ^ ((k >> 1) & 1), c ^ (k & 1))
            cps.append(pltpu.make_async_remote_copy(src_ref=in_ref, dst_ref=slots.at[me], send_sem=send_sems.at[k - 1],
                                                    recv_sem=recv_sems.at[k - 1], device_id=to, device_id_type=MESH))
        for cp in cps:
            cp.start()
        for k in range(1, N_DEV):
            frm = 4 * (x ^ (k >> 2)) + 2 * (y ^ ((k >> 1) & 1)) + (c ^ (k & 1))
            pltpu.make_async_remote_copy(src_ref=in_ref, dst_ref=slots.at[frm], send_sem=send_sems.at[k - 1],
                                         recv_sem=recv_sems.at[k - 1], device_id=(x, y, c), device_id_type=MESH).wait_recv()
        for cp in cps:
            cp.wait_send()
        acc = slots[0]
        for d in range(1, N_DEV):
            acc = acc + slots[d]
        out_ref[...] = acc

    vm = pl.BlockSpec(memory_space=pltpu.VMEM)
    return pl.pallas_call(
        body, name="allreduce_small", in_specs=[vm], out_specs=vm, out_shape=jax.ShapeDtypeStruct((R, ROW), F32),
        scratch_shapes=[pltpu.VMEM((N_DEV, R, ROW), F32), pltpu.SemaphoreType.DMA((N_DEV - 1,)),
                        pltpu.SemaphoreType.DMA((N_DEV - 1,))],
    )(v)


def _rows_of(n, unit=16):
    return -(-n // (unit * ROW)) * unit


def _pack_rows(items, total_rows, dtype, unit=16):
    parts = []
    used = 0
    for a in items:
        flat = a.reshape(-1)
        r = _rows_of(flat.shape[0], unit)
        flat = jnp.pad(flat, (0, r * ROW - flat.shape[0]))
        parts.append(flat.reshape(r, ROW))
        used += r
    if total_rows > used:
        parts.append(jnp.zeros((total_rows - used, ROW), dtype))
    return jnp.concatenate(parts, axis=0)


def _unpack_rows(buf, shapes, unit=16):
    lead = buf.shape[:-2]
    out = []
    off = 0
    for shp in shapes:
        n = math.prod(shp)
        r = _rows_of(n, unit)
        piece = buf[..., off:off + r, :].reshape(*lead, r * ROW)[..., :n].reshape(*lead, *shp)
        out.append(piece)
        off += r
    return out


def _interleave_heads(w, H):
    lead = w.shape[:-1]
    return w.reshape(*lead, 3, H, HEAD).swapaxes(-3, -2).reshape(*lead, 3 * H * HEAD)


def _deinterleave_heads(w, H):
    lead = w.shape[:-1]
    return w.reshape(*lead, H, 3, HEAD).swapaxes(-3, -2).reshape(*lead, 3 * H * HEAD)


def _interleave_head_rows(w, H):
    return w.reshape(3, H, HEAD, w.shape[-1]).swapaxes(0, 1).reshape(3 * H * HEAD, w.shape[-1])


def _deinterleave_head_rows(w, H):
    return w.reshape(H, 3, HEAD, w.shape[-1]).swapaxes(0, 1).reshape(3 * H * HEAD, w.shape[-1])


def kernel(x, norm_mix, w_in, fox_f_bias, gdn_conv_w, gdn_a_log, gdn_dt_bias, gdn_norm, w_branch_fox, w_branch_gdn, w_out, norm_ffn, w_up, ffn_conv_w, w_down, norm_final, loss_target, m_norm_mix, m_w_in, m_fox_f_bias, m_gdn_conv_w, m_gdn_a_log, m_gdn_dt_bias, m_gdn_norm, m_w_branch_fox, m_w_branch_gdn, m_w_out, m_norm_ffn, m_w_up, m_ffn_conv_w, m_w_down, m_norm_final, v_norm_mix, v_w_in, v_fox_f_bias, v_gdn_conv_w, v_gdn_a_log, v_gdn_dt_bias, v_gdn_norm, v_w_branch_fox, v_w_branch_gdn, v_w_out, v_norm_ffn, v_w_up, v_ffn_conv_w, v_w_down, v_norm_final):
    B, S, D = x.shape
    T = B * S
    H = D // HEAD
    N = S // CHUNK
    FF = w_down.shape[1] * N_CHIP
    d_in = 9 * D + 3 * H
    assert w_in.shape[2] * N_CHIP == d_in and 3 * H <= 128

    cidx = lax.axis_index("c").astype(jnp.int32)
    sidx = (2 * lax.axis_index("x") + lax.axis_index("y")).astype(jnp.int32)
    idx = jnp.stack([cidx, sidx])

    rowed = [w_branch_fox[0], w_branch_gdn[0], w_out[0], w_down[0]]
    convs = [gdn_conv_w[0], ffn_conv_w[0]]
    rowed_shapes = [a.shape for a in rowed]
    conv_shapes = [a.shape + (2,) for a in convs]
    pad_rows = lambda shapes: -(-sum(_rows_of(math.prod(s)) for s in shapes) // 256) * 128
    Rh, Rc = pad_rows(rowed_shapes), pad_rows(conv_shapes)
    halves = lambda a: a.reshape(2, a.shape[0] // 2, a.shape[1])
    c_in = w_in.shape[2]
    packs_a = [w_in[0].T.astype(BF16).reshape(c_in, 2, D // 2).transpose(1, 0, 2),
               halves(_pack_rows([lax.bitcast_convert_type(a, BF16) for a in convs], 2 * Rc, BF16))]
    packs_b = [halves(w_up[0].astype(BF16)), halves(_pack_rows([a.astype(BF16) for a in rowed], 2 * Rh, BF16))]
    own = lambda gs, ps: [lax.dynamic_update_slice(g, p[None], (sidx, 0, 0, 0)) for g, p in zip(gs, ps)]
    by_cols = lambda g: g.transpose(1, 2, 0, 3).reshape(2 * g.shape[2], N_CHIP * g.shape[3])
    cat_cols = lambda p: jnp.concatenate([p[i] for i in range(N_CHIP)], axis=-1)
    cat_rows = lambda p: p.reshape(-1, p.shape[-1])
    g_in, g_conv = own(allgather_weights(packs_a), packs_a)
    W_inT = g_in.transpose(0, 2, 1, 3).reshape(N_CHIP * c_in, D)
    conv_parts = _unpack_rows(g_conv.reshape(N_CHIP, 2 * Rc, ROW), conv_shapes)
    gconv = cat_cols(lax.bitcast_convert_type(conv_parts[0], F32))
    fconv = cat_cols(lax.bitcast_convert_type(conv_parts[1], F32))

    o1, o2 = 3 * D, 3 * D + H
    o3, o4, o5, o6 = o2 + 3 * D, o2 + 3 * D + H, o2 + 3 * D + 2 * H, o2 + 4 * D + 2 * H
    W_foxT = _interleave_head_rows(W_inT[:o1], H)
    W_gqkvT = _interleave_head_rows(W_inT[o2:o3], H)
    W_gzT = W_inT[o5:o6]
    W_gatesT = W_inT[o6:]
    W_smallT = jnp.concatenate([W_inT[o1:o2], W_inT[o3:o5], jnp.zeros((128 - 3 * H, D), BF16)], axis=0)
    gconv_i = _interleave_heads(gconv, H)
    fconv_g, fconv_v = fconv[:, :FF], fconv[:, FF:]
    prm = jnp.zeros((8, 128), F32)
    prm = prm.at[0, 0:H].set(fox_f_bias[0]).at[0, H:2 * H].set(gdn_dt_bias[0]).at[1, H:2 * H].set(gdn_a_log[0])

    x2 = x.reshape(T, D)
    tgt = loss_target.reshape(T, D)

    hn1 = rmsnorm_fwd(x2, norm_mix, "rmsnorm_mix")
    p_fox = matmul(hn1, W_foxT, "nt", "proj_fox", out_dtype=BF16)
    p_gqkv = matmul(hn1, W_gqkvT, "nt", "proj_gqkv")
    p_gz = matmul(hn1, W_gzT, "nt", "proj_gz")
    p_gates = matmul(hn1, W_gatesT, "nt", "proj_gates")
    p_small = matmul(hn1, W_smallT, "nt", "proj_small")

    sm = small_fwd(p_small, prm, B, S, H)
    heads = lambda a: a.reshape(B, S, H).transpose(0, 2, 1)
    c_bhs, gc_bhs, beta_bhs = heads(sm[:, 0:H]), heads(sm[:, H:2 * H]), heads(sm[:, 2 * H:3 * H])
    c_col, c_row = c_bhs[..., None], c_bhs[:, :, None, :]
    gcr5 = gc_bhs.reshape(B, H, N, 1, CHUNK)
    gcr_u = gc_bhs.reshape(B, H, N // PAIR, 1, PAIR * CHUNK)
    betar_u = beta_bhs.reshape(B, H, N // PAIR, 1, PAIR * CHUNK)

    (o_fox, o_fox16, lse), arriving = fox_fwd(p_fox, c_col, c_row, B, S, H, rider=_ride_gather_ici(packs_b))
    qkvn = gdn_prep_fwd(p_gqkv, gconv_i, B, S, H)
    (u_hat, w_t, t_inv), arrived = gdn_intra_fwd(qkvn, betar_u, gcr_u, B, S, H, rider=_ride_gather_d2d(arriving))
    g_up, g_rowed = own(arrived, packs_b)
    W_up = by_cols(g_up)
    W_up_g, W_up_v = W_up[:, :FF], W_up[:, FF:]
    W_bf, W_bg, W_out, W_down = (cat_rows(p) for p in _unpack_rows(g_rowed.reshape(N_CHIP, 2 * Rh, ROW), rowed_shapes))
    o_gdn, states, y_gdn = gdn_inter_fwd(qkvn, u_hat, w_t, gcr5, p_gz, gdn_norm, B, S, H)
    bf_ = matmul(o_fox16, W_bf, "nn", "branch_fox")
    bg_, y = matmul(y_gdn, W_bg, "nn", "branch_gdn", post=_post_merge(p_gates, bf_))
    h1, hn2 = matmul(y, W_out, "nn", "out_proj", add=x2, post=_post_rmsnorm(norm_ffn))
    up_g = matmul(hn2, W_up_g, "nn", "up_gate")
    up_v = matmul(hn2, W_up_v, "nn", "up_val")
    act = ffn_gate_fwd(up_g, up_v, fconv_g, fconv_v, B, S)
    dh2, dh2_16, loss_cols, d_norm_final = matmul(act, W_down, "nn", "down_proj", add=h1,
                                                  post=_post_loss(norm_final.reshape(1, D), tgt))
    loss = lax.psum(0.5 * jnp.sum(loss_cols) / D, ("x", "y", "c"))

    d_act = matmul(dh2_16, W_down, "nt", "d_act")
    dW_down = matmul(act, dh2_16, "tn", "dw_down")
    d_upg, d_upv, d_fconv_g, d_fconv_v = ffn_gate_bwd(up_g, up_v, fconv_g, fconv_v, d_act, B, S)
    d_hn2 = matmul(d_upg, W_up_g, "nt", "d_hn2_g")
    dh1, dh1_16, d_norm_ffn = matmul(d_upv, W_up_v, "nt", "d_hn2_v", add=d_hn2,
                                     post=_post_rmsnorm_bwd(h1, norm_ffn, dh2, True))
    dW_up = jnp.concatenate([matmul(hn2, d_upg, "tn", "dw_up_g"), matmul(hn2, d_upv, "tn", "dw_up_v")], axis=1)
    d_bf, d_bg, d_gates = matmul(dh1_16, W_out, "nt", "d_y", post=_post_merge_bwd(p_gates, bf_, bg_))
    dW_out = matmul(y, dh1_16, "tn", "dw_out")
    d_ofox = matmul(d_bf, W_bf, "nt", "d_ofox")
    dW_bf = matmul(o_fox16, d_bf, "tn", "dw_bf")
    d_ygdn = matmul(d_bg, W_bg, "nt", "d_ygdn")
    dW_bg = matmul(y_gdn, d_bg, "tn", "dw_bg")

    d_fconv = jnp.concatenate([d_fconv_g, d_fconv_v], axis=1)
    col_shard = lambda g, s: g[:, s * (g.shape[1] // N_CHIP):(s + 1) * (g.shape[1] // N_CHIP)]
    row_shard = lambda g, s: g[s * (g.shape[0] // N_CHIP):(s + 1) * (g.shape[0] // N_CHIP)]
    shard_items = lambda s: [row_shard(dW_bf, s), row_shard(dW_bg, s), row_shard(dW_out, s), row_shard(dW_down, s),
                             col_shard(d_fconv, s)]
    g_shapes = [a.shape for a in shard_items(0)]
    assert sum(_rows_of(math.prod(s)) for s in g_shapes) <= 2 * Rh
    to_slabs = lambda g: g.reshape(2, g.shape[0] // 2, N_CHIP, g.shape[1] // N_CHIP).transpose(0, 2, 1, 3)
    gpacks_b = [to_slabs(dW_up),
                jnp.stack([_pack_rows(shard_items(s), 2 * Rh, F32).reshape(2, Rh, ROW) for s in range(N_CHIP)], axis=1)]
    (d_pfox, d_ccol, d_crow), gots_b = fox_bwd(p_fox, c_col, c_row, o_fox, lse, d_ofox, B, S, H,
                                              rider=_ride_exchange(gpacks_b))
    sums_b = [add_halves(g, got, idx, "add_halves_b%d" % i) for i, (g, got) in enumerate(zip(gpacks_b, gots_b))]

    (dq_i, dk_i, d_uh, d_wt, dgcr_a, d_gz, d_gn_parts), got16_b = gdn_inter_bwd(
        qkvn, u_hat, w_t, gcr5, states, o_gdn, p_gz, gdn_norm, d_ygdn, B, S, H,
        rider=_ride_scatter([s16 for _, s16 in sums_b]))
    d_gdn_norm = jnp.sum(d_gn_parts[:, :, 0, :], axis=(0, 1))[None]
    mine_b = [add_chips(s32, g16, idx, "add_chips_b%d" % i) for i, ((s32, _), g16) in enumerate(zip(sums_b, got16_b))]
    d_qkvn, d_betar5, dgcr_b = gdn_intra_bwd(qkvn, betar_u, gcr_u, t_inv, d_uh, d_wt, dq_i, dk_i, B, S, H)
    d_pgqkv, d_gconv_i = gdn_prep_bwd(p_gqkv, gconv_i, d_qkvn, B, S, H)

    tokens = lambda a: a.reshape(B, H, S).transpose(0, 2, 1).reshape(T, H)
    d_gc = dgcr_a.reshape(B, H, S) + dgcr_b.reshape(B, H, S)
    d_sm = jnp.concatenate([tokens(d_ccol.reshape(B, H, S) + d_crow.reshape(B, H, S)), tokens(d_gc), tokens(d_betar5.reshape(B, H, S)),
                            jnp.zeros((T, 128 - 3 * H), F32)], axis=1)
    d_psmall, d_prm = small_bwd(p_small, prm, d_sm, B, S, H)

    dW_foxT = matmul(d_pfox, hn1, "tn", "dw_fox")
    dW_gqkvT = matmul(d_pgqkv, hn1, "tn", "dw_gqkv")
    dW_gzT = matmul(d_gz, hn1, "tn", "dw_gz")
    dW_gatesT = matmul(d_gates, hn1, "tn", "dw_gates")
    dW_smallT = matmul(d_psmall, hn1, "tn", "dw_small")
    dW_inT = jnp.concatenate([_deinterleave_head_rows(dW_foxT, H), dW_smallT[0:H], _deinterleave_head_rows(dW_gqkvT, H),
                              dW_smallT[H:3 * H], dW_gzT, dW_gatesT], axis=0)
    d_gconv = _deinterleave_heads(d_gconv_i, H)

    gpack_a = [dW_inT.reshape(N_CHIP, c_in, D)]
    d_hn1, gots_a = matmul(d_pfox, W_foxT, "nn", "d_hn1_fox", rider=_ride_exchange(gpack_a))
    sums_a = [add_halves(gpack_a[0], gots_a[0], idx, "add_halves_a")]
    d_hn1, landing_a = matmul(d_pgqkv, W_gqkvT, "nn", "d_hn1_gqkv", add=d_hn1,
                              rider=_ride_scatter([sums_a[0][1]], to=(0, 1)))
    d_hn1 = matmul(d_gz, W_gzT, "nn", "d_hn1_gz", add=d_hn1)
    d_hn1, got16_a = matmul(d_gates, W_gatesT, "nn", "d_hn1_gates", add=d_hn1,
                            rider=_ride_scatter([sums_a[0][1]], to=(2,), landing=landing_a))
    mine = [add_chips(sums_a[0][0], got16_a[0], idx, "add_chips_a")] + mine_b
    grad_x, d_norm_mix = matmul(d_psmall, W_smallT, "nn", "d_hn1_small", add=d_hn1,
                                post=_post_rmsnorm_bwd(x2, norm_mix, dh1, False))

    others = share_halves(mine)
    g_w_inT, g_up, g_rows = (jnp.concatenate([jnp.where(cidx == 0, h, o), jnp.where(cidx == 0, o, h)], axis=ax)
                             for h, o, ax in zip(mine, others, (1, 0, 0)))
    g_w_in = g_w_inT.T
    g_bf, g_bg, g_out, g_down, g_fconv = _unpack_rows(g_rows, g_shapes)

    small_items = [d_norm_mix, d_norm_ffn, d_norm_final, d_gdn_norm, d_prm, d_gconv]
    small_shapes = [a.shape for a in small_items]
    sv = allreduce_small(_pack_rows(small_items, 0, F32, unit=8))
    g_norm_mix, g_norm_ffn, g_norm_final, g_gdn_norm, g_prm, g_gconv_all = _unpack_rows(sv, small_shapes, unit=8)
    g_norm_final = g_norm_final.reshape(D)
    g_fbias, g_dtb, g_alog = g_prm[0:1, 0:H], g_prm[0:1, H:2 * H], g_prm[1:2, H:2 * H]
    g_gconv = lax.dynamic_slice_in_dim(g_gconv_all, sidx * (3 * D // N_CHIP), 3 * D // N_CHIP, axis=1)

    names = ["norm_mix", "w_in", "fox_f_bias", "gdn_conv_w", "gdn_a_log", "gdn_dt_bias", "gdn_norm", "w_branch_fox",
             "w_branch_gdn", "w_out", "norm_ffn", "w_up", "ffn_conv_w", "w_down", "norm_final"]
    ws = [norm_mix, w_in, fox_f_bias, gdn_conv_w, gdn_a_log, gdn_dt_bias, gdn_norm, w_branch_fox, w_branch_gdn, w_out,
          norm_ffn, w_up, ffn_conv_w, w_down, norm_final]
    ms = [m_norm_mix, m_w_in, m_fox_f_bias, m_gdn_conv_w, m_gdn_a_log, m_gdn_dt_bias, m_gdn_norm, m_w_branch_fox,
          m_w_branch_gdn, m_w_out, m_norm_ffn, m_w_up, m_ffn_conv_w, m_w_down, m_norm_final]
    vs = [v_norm_mix, v_w_in, v_fox_f_bias, v_gdn_conv_w, v_gdn_a_log, v_gdn_dt_bias, v_gdn_norm, v_w_branch_fox,
          v_w_branch_gdn, v_w_out, v_norm_ffn, v_w_up, v_ffn_conv_w, v_w_down, v_norm_final]
    gs = [g_norm_mix, g_w_in, g_fbias, g_gconv, g_alog, g_dtb, g_gdn_norm, g_bf, g_bg, g_out, g_norm_ffn, g_up,
          g_fconv, g_down, g_norm_final]
    gs = [g.reshape(w.shape) for g, w in zip(gs, ws)]
    deltas, new_ms, new_vs = [], [], []
    for nm, w, g, m, v in zip(names, ws, gs, ms, vs):
        if w.ndim == 1:
            d, a, b = adamw(w.reshape(1, -1), g.reshape(1, -1), m.reshape(1, -1), v.reshape(1, -1), "adamw_" + nm)
            d, a, b = d.reshape(w.shape), a.reshape(w.shape), b.reshape(w.shape)
        elif nm == "w_in":
            d, a, b = (r.T[None] for r in adamw(w[0].T, g_w_inT, m[0].T, v[0].T, "adamw_" + nm))
        else:
            d, a, b = adamw(w, g, m, v, "adamw_" + nm)
        deltas.append(d)
        new_ms.append(a)
        new_vs.append(b)

    return (loss, grad_x.reshape(B, S, D), *gs, *deltas, *new_ms, *new_vs)
```

```python
import functools
import math

import jax
import jax.numpy as jnp
from jax import lax
from jax.experimental import pallas as pl
from jax.experimental.pallas import tpu as pltpu

F32 = jnp.float32
BF16 = jnp.bfloat16
HEAD = 128
CHUNK = 64
GDN_CONV = 4
FFN_CONV = 3
EPS = 1e-6
NEG = -1e30
ROW = 1024
ATT_TILE = 512
MM_WEIGHT_TILE_BYTES = 8 << 20
N_CHIP = 4
N_DEV = 8
MESH = pl.DeviceIdType.MESH
HI = lax.Precision.HIGH
EXACT = lax.Precision.HIGHEST

ADAM_LR, ADAM_B1, ADAM_B2, ADAM_EPS, ADAM_WD, ADAM_STEP = 0.001, 0.9, 0.999, 1e-08, 0.01, 10


def _tile(n, cap, unit=128):
    best = None
    t = unit
    while t <= min(n, cap):
        if n % t == 0:
            best = t
        t += unit
    return best if best is not None else n


def _params(*sem):
    return pltpu.CompilerParams(dimension_semantics=sem)


_NN = (((1,), (0,)), ((), ()))
_NT = (((1,), (1,)), ((), ()))
_TN = (((0,), (0,)), ((), ()))


def _dg(a, b, dims, hi):
    if hi:
        return lax.dot_general(a, b, dims, precision=HI, preferred_element_type=F32)
    return lax.dot_general(a.astype(BF16), b.astype(BF16), dims, preferred_element_type=F32)


class _RawOps:
    @staticmethod
    def nn(a, b, hi=False):
        return _dg(a, b, _NN, hi)

    @staticmethod
    def nt(a, b, hi=False):
        return _dg(a, b, _NT, hi)

    @staticmethod
    def tn(a, b, hi=False):
        return _dg(a, b, _TN, hi)


def _make_diff_ops():
    def build(hi):
        @jax.custom_vjp
        def nn(a, b):
            return _dg(a, b, _NN, hi)

        nn.defvjp(lambda a, b: (_dg(a, b, _NN, hi), (a, b)),
                  lambda r, g: (_dg(g, r[1], _NT, hi), _dg(r[0], g, _TN, hi)))

        @jax.custom_vjp
        def nt(a, b):
            return _dg(a, b, _NT, hi)

        nt.defvjp(lambda a, b: (_dg(a, b, _NT, hi), (a, b)),
                  lambda r, g: (_dg(g, r[1], _NN, hi), _dg(g, r[0], _TN, hi)))

        @jax.custom_vjp
        def tn(a, b):
            return _dg(a, b, _TN, hi)

        tn.defvjp(lambda a, b: (_dg(a, b, _TN, hi), (a, b)),
                  lambda r, g: (_dg(r[1], g, _NT, hi), _dg(r[0], g, _NN, hi)))
        return nn, nt, tn

    lo, hi_ = build(False), build(True)

    class _DiffOps:
        @staticmethod
        def nn(a, b, hi=False):
            return (hi_ if hi else lo)[0](a, b)

        @staticmethod
        def nt(a, b, hi=False):
            return (hi_ if hi else lo)[1](a, b)

        @staticmethod
        def tn(a, b, hi=False):
            return (hi_ if hi else lo)[2](a, b)

    return _DiffOps


_DiffOps = _make_diff_ops()


def _sigmoid(x):
    return 1.0 / (1.0 + jnp.exp(-x))


def _mm_tile(n, pref):
    if n % pref == 0:
        return pref
    if n % 1408 == 0:
        return 1408
    return _tile(n, pref)


class _Post:
    def __init__(self, fn, row_ins=(), vec_ins=(), row_outs=(), acc_outs=(), keep_main=True):
        self.fn, self.keep_main = fn, keep_main
        self.row_ins = [r if isinstance(r, tuple) else (r, r.shape[1], 0) for r in row_ins]
        self.vec_ins, self.row_outs, self.acc_outs = list(vec_ins), list(row_outs), list(acc_outs)


def matmul(a, b, mode, name, add=None, out_dtype=F32, post=None, rider=None):
    if mode == "nn":
        (M, K), (K2, N) = a.shape, b.shape
    elif mode == "nt":
        (M, K), (N, K2) = a.shape, b.shape
    else:
        (K, M), (K2, N) = a.shape, b.shape
    assert K == K2, (name, a.shape, b.shape)
    tn = _mm_tile(N, 1024)
    if mode == "tn":
        tm = M if M <= 1408 else _mm_tile(M, 1408)
        tk = _mm_tile(K, 2048)
    else:
        tk = K if K * tn * 2 <= MM_WEIGHT_TILE_BYTES else _mm_tile(K, 1024)
        tm = _mm_tile(M, 1024 if tk <= 2048 and post is None else 512)
    nk = K // tk
    assert post is None or (mode != "tn" and tn == N), name
    dims = {"nn": _NN, "nt": _NT, "tn": _TN}[mode]
    if mode == "tn":
        a_spec = pl.BlockSpec((tk, tm), lambda j, i, k: (k, i))
    else:
        a_spec = pl.BlockSpec((tm, tk), lambda j, i, k: (i, k))
    if mode == "nt":
        b_spec = pl.BlockSpec((tn, tk), lambda j, i, k: (j, k))
    else:
        b_spec = pl.BlockSpec((tk, tn), lambda j, i, k: (k, j))
    o_spec = pl.BlockSpec((tm, tn), lambda j, i, k: (i, j))
    has_add = add is not None
    keep_main = post is None or post.keep_main
    counts = [2 + has_add] + ([len(post.row_ins), len(post.vec_ins)] if post else [0, 0]) + [int(keep_main)]
    counts += ([len(post.row_outs), len(post.acc_outs)] if post else [0, 0]) + [int(nk > 1)]

    def body(*refs):
        parts, p = [], 0
        for cnt in counts:
            parts.append(refs[p:p + cnt])
            p += cnt
        core, row_ins, vec_ins, main, row_outs, acc_outs, acc = parts
        a_ref, b_ref = core[:2]
        prod = lax.dot_general(a_ref[...].astype(BF16), b_ref[...].astype(BF16), dims, preferred_element_type=F32)

        def finish(r):
            if has_add:
                r = r + core[2][...]
            if keep_main:
                main[0][...] = r.astype(out_dtype)
            if post is not None:
                @pl.when(pl.program_id(1) == 0)
                def _():
                    for ref in acc_outs:
                        ref[...] = jnp.zeros_like(ref)

                post.fn(r, row_ins, vec_ins, row_outs, acc_outs)

        if nk == 1:
            finish(prod)
            return
        acc_ref = acc[0]
        k = pl.program_id(2)

        @pl.when(k == 0)
        def _():
            acc_ref[...] = jnp.zeros_like(acc_ref)

        acc_ref[...] += prod

        @pl.when(k == nk - 1)
        def _():
            finish(acc_ref[...])

    in_specs = [a_spec, b_spec] + ([o_spec] if has_add else [])
    args = (a, b) + ((add,) if has_add else ())
    out_specs = [o_spec] if keep_main else []
    out_shape = [jax.ShapeDtypeStruct((M, N), out_dtype)] if keep_main else []
    if post is not None:
        in_specs += [pl.BlockSpec((tm, cols), lambda j, i, k, cb=cb: (i, cb)) for _, cols, cb in post.row_ins]
        in_specs += [pl.BlockSpec((1, v.shape[1]), lambda j, i, k: (0, 0)) for v in post.vec_ins]
        args += tuple(r for r, _, _ in post.row_ins) + tuple(post.vec_ins)
        out_specs += [pl.BlockSpec((tm, cols), lambda j, i, k: (i, 0)) for cols, _ in post.row_outs]
        out_specs += [pl.BlockSpec((1, cols), lambda j, i, k: (0, 0)) for cols in post.acc_outs]
        out_shape += [jax.ShapeDtypeStruct((M, cols), dt) for cols, dt in post.row_outs]
        out_shape += [jax.ShapeDtypeStruct((1, cols), F32) for cols in post.acc_outs]
    rows_sem = "arbitrary" if post is not None and post.acc_outs else "parallel"
    res = _hosted_call(
        body, rider, name=name, grid=(N // tn, M // tm, nk), in_specs=in_specs, out_specs=out_specs, out_shape=out_shape,
        scratch_shapes=[pltpu.VMEM((tm, tn), F32)] if nk > 1 else [], semantics=("parallel", rows_sem, "arbitrary"),
    )(*args)
    if rider is not None:
        res, carried = res
        return (res[0] if post is None else res), carried
    return res[0] if post is None else res


def rmsnorm_fwd(x, g, name):
    T, D = x.shape
    tm = _tile(T, 512, 8)

    def body(x_ref, g_ref, o_ref):
        xv = x_ref[...]
        r = lax.rsqrt(jnp.mean(xv * xv, axis=-1, keepdims=True) + EPS)
        o_ref[...] = (xv * r * g_ref[...]).astype(BF16)

    return pl.pallas_call(
        body, name=name, grid=(T // tm,),
        in_specs=[pl.BlockSpec((tm, D), lambda i: (i, 0)), pl.BlockSpec((1, D), lambda i: (0, 0))],
        out_specs=pl.BlockSpec((tm, D), lambda i: (i, 0)),
        out_shape=jax.ShapeDtypeStruct((T, D), BF16),
        compiler_params=_params("parallel"),
    )(x, g)


def _post_rmsnorm(g):
    def fn(r, row_ins, vec_ins, row_outs, acc_outs):
        rs = lax.rsqrt(jnp.mean(r * r, axis=-1, keepdims=True) + EPS)
        row_outs[0][...] = (r * rs * vec_ins[0][...]).astype(BF16)

    return _Post(fn, vec_ins=[g], row_outs=[(g.shape[1], BF16)])


def _post_rmsnorm_bwd(x, g, dres, with_bf16):
    D = g.shape[1]

    def fn(dy, row_ins, vec_ins, row_outs, acc_outs):
        xv = row_ins[0][...]
        rs = lax.rsqrt(jnp.mean(xv * xv, axis=-1, keepdims=True) + EPS)
        xh = xv * rs
        acc_outs[0][...] += jnp.sum(dy * xh, axis=0, keepdims=True)
        dxh = dy * vec_ins[0][...]
        dx = row_ins[1][...] + rs * (dxh - xh * jnp.mean(dxh * xh, axis=-1, keepdims=True))
        row_outs[0][...] = dx
        if with_bf16:
            row_outs[1][...] = dx.astype(BF16)

    return _Post(fn, row_ins=[x, dres], vec_ins=[g], row_outs=[(D, F32)] + ([(D, BF16)] if with_bf16 else []),
                 acc_outs=[D], keep_main=False)


def _post_loss(g, target):
    D = g.shape[1]

    def fn(hv, row_ins, vec_ins, row_outs, acc_outs):
        rs = lax.rsqrt(jnp.mean(hv * hv, axis=-1, keepdims=True) + EPS)
        xh = hv * rs
        gv = vec_ins[0][...]
        err = xh * gv - row_ins[0][...]
        acc_outs[0][...] += jnp.sum(err * err, axis=0, keepdims=True)
        dy = err * (1.0 / D)
        acc_outs[1][...] += jnp.sum(dy * xh, axis=0, keepdims=True)
        dxh = dy * gv
        dh = rs * (dxh - xh * jnp.mean(dxh * xh, axis=-1, keepdims=True))
        row_outs[0][...] = dh
        row_outs[1][...] = dh.astype(BF16)

    return _Post(fn, row_ins=[target], vec_ins=[g], row_outs=[(D, F32), (D, BF16)], acc_outs=[D, D], keep_main=False)


def _shift_down(x, k):
    if k == 0:
        return x
    rows = lax.broadcasted_iota(jnp.int32, x.shape, 0)
    return jnp.where(rows >= k, pltpu.roll(x, k, 0), 0.0)


def _shift_up(x, k):
    if k == 0:
        return x
    s = x.shape[0]
    rows = lax.broadcasted_iota(jnp.int32, x.shape, 0)
    return jnp.where(rows < s - k, pltpu.roll(x, s - k, 0), 0.0)


def _conv_fwd(x, w_ref, kw, keep_shifted=False):
    shifted = [_shift_down(x, kw - 1 - i) for i in range(kw - 1)]
    y = x * w_ref[kw - 1:kw, :]
    for i in range(kw - 1):
        y = y + shifted[i] * w_ref[i:i + 1, :]
    return (y, shifted) if keep_shifted else y


def _conv_bwd(x, shifted, dy, w_ref, kw):
    dx = dy * w_ref[kw - 1:kw, :]
    dws = []
    for i in range(kw - 1):
        dx = dx + _shift_up(dy, kw - 1 - i) * w_ref[i:i + 1, :]
        dws.append(jnp.sum(dy * shifted[i], axis=0, keepdims=True))
    dws.append(jnp.sum(dy * x, axis=0, keepdims=True))
    return dx, dws


def ffn_gate_fwd(up_g, up_v, cw_g, cw_v, B, S):
    T, Fd = up_g.shape
    tc = _tile(Fd, 256)

    def body(g_ref, v_ref, wg_ref, wv_ref, o_ref):
        ug = _conv_fwd(g_ref[...], wg_ref, FFN_CONV)
        uv = _conv_fwd(v_ref[...], wv_ref, FFN_CONV)
        o_ref[...] = (ug * _sigmoid(ug) * uv).astype(BF16)

    blk = pl.BlockSpec((S, tc), lambda b, j: (b, j))
    wblk = pl.BlockSpec((FFN_CONV, tc), lambda b, j: (0, j))
    return pl.pallas_call(
        body, name="ffn_gate_fwd", grid=(B, Fd // tc), in_specs=[blk, blk, wblk, wblk], out_specs=blk,
        out_shape=jax.ShapeDtypeStruct((T, Fd), BF16), compiler_params=_params("parallel", "parallel"),
    )(up_g, up_v, cw_g, cw_v)


def ffn_gate_bwd(up_g, up_v, cw_g, cw_v, d_act, B, S):
    T, Fd = up_g.shape
    tc = _tile(Fd, 256)

    def body(g_ref, v_ref, wg_ref, wv_ref, da_ref, dg_ref, dv_ref, dwg_ref, dwv_ref):
        @pl.when(pl.program_id(1) == 0)
        def _():
            dwg_ref[...] = jnp.zeros_like(dwg_ref)
            dwv_ref[...] = jnp.zeros_like(dwv_ref)

        xg, xv = g_ref[...], v_ref[...]
        ug, sh_g = _conv_fwd(xg, wg_ref, FFN_CONV, keep_shifted=True)
        uv, sh_v = _conv_fwd(xv, wv_ref, FFN_CONV, keep_shifted=True)
        da = da_ref[...]
        sg = _sigmoid(ug)
        d_ug = da * uv * (sg + ug * sg * (1.0 - sg))
        d_uv = da * ug * sg
        dxg, dwg = _conv_bwd(xg, sh_g, d_ug, wg_ref, FFN_CONV)
        dxv, dwv = _conv_bwd(xv, sh_v, d_uv, wv_ref, FFN_CONV)
        dg_ref[...] = dxg.astype(BF16)
        dv_ref[...] = dxv.astype(BF16)
        for i in range(FFN_CONV):
            dwg_ref[i:i + 1, :] += dwg[i]
            dwv_ref[i:i + 1, :] += dwv[i]

    blk = pl.BlockSpec((S, tc), lambda j, b: (b, j))
    wblk = pl.BlockSpec((FFN_CONV, tc), lambda j, b: (0, j))
    return pl.pallas_call(
        body, name="ffn_gate_bwd", grid=(Fd // tc, B), in_specs=[blk, blk, wblk, wblk, blk],
        out_specs=[blk, blk, wblk, wblk],
        out_shape=[jax.ShapeDtypeStruct((T, Fd), BF16), jax.ShapeDtypeStruct((T, Fd), BF16),
                   jax.ShapeDtypeStruct((FFN_CONV, Fd), F32), jax.ShapeDtypeStruct((FFN_CONV, Fd), F32)],
        compiler_params=_params("parallel", "arbitrary"),
    )(up_g, up_v, cw_g, cw_v, d_act)


def _post_merge(p_gates, bf_):
    D = bf_.shape[1]

    def fn(bg, row_ins, vec_ins, row_outs, acc_outs):
        gf_ref, gg_ref, bf_ref = row_ins
        row_outs[0][...] = (_sigmoid(gf_ref[...]) * bf_ref[...] + _sigmoid(gg_ref[...]) * bg).astype(BF16)

    return _Post(fn, row_ins=[(p_gates, D, 0), (p_gates, D, 1), bf_], row_outs=[(D, BF16)])


def _post_merge_bwd(p_gates, bf_, bg_):
    D = bf_.shape[1]

    def fn(d, row_ins, vec_ins, row_outs, acc_outs):
        gf_ref, gg_ref, bf_ref, bg_ref = row_ins
        sf, sg = _sigmoid(gf_ref[...]), _sigmoid(gg_ref[...])
        row_outs[0][...] = (d * sf).astype(BF16)
        row_outs[1][...] = (d * sg).astype(BF16)
        row_outs[2][:, 0:D] = (d * bf_ref[...] * sf * (1.0 - sf)).astype(BF16)
        row_outs[2][:, D:2 * D] = (d * bg_ref[...] * sg * (1.0 - sg)).astype(BF16)

    return _Post(fn, row_ins=[(p_gates, D, 0), (p_gates, D, 1), bf_, bg_],
                 row_outs=[(D, BF16), (D, BF16), (2 * D, BF16)], keep_main=False)


def fox_fwd(p_fox, c_col, c_row, B, S, H, rider=None):
    T = B * S
    t = _tile(S, ATT_TILE)
    nq = S // t
    scale = HEAD ** -0.5

    def body(q_ref, k_ref, v_ref, cq_ref, cr_ref, o_ref, o16_ref, lse_ref):
        i = pl.program_id(2)
        q = q_ref[...]
        h = t // 2
        row = lax.broadcasted_iota(jnp.int32, (h, h), 0)
        col = lax.broadcasted_iota(jnp.int32, (h, h), 1)

        def attend(qv, carry, off, width, diagonal):
            m, l, acc = carry
            k = k_ref[pl.ds(off, width), :]
            v = v_ref[pl.ds(off, width), :]
            s = lax.dot_general(qv, k, _NT, preferred_element_type=F32) * scale - cr_ref[:, pl.ds(off, width)]
            if diagonal:
                s = jnp.where(col <= row, s, NEG)
            m_new = jnp.maximum(m, jnp.max(s, axis=-1, keepdims=True))
            alpha = jnp.exp(m - m_new)
            p = jnp.exp(s - m_new)
            l = alpha * l + jnp.sum(p, axis=-1, keepdims=True)
            acc = alpha * acc + lax.dot_general(p.astype(BF16), v, _NN, preferred_element_type=F32)
            return m_new, l, acc

        m0 = jnp.full((t, 1), NEG, F32)
        m, l, acc = lax.fori_loop(0, i, lambda j, c: attend(q, c, pl.multiple_of(j * t, t), t, False),
                                  (m0, jnp.zeros((t, 1), F32), jnp.zeros((t, HEAD), F32)))
        base = pl.multiple_of(i * t, t)
        upper = attend(q[:h], (m[:h], l[:h], acc[:h]), base, h, True)
        lower = attend(q[h:], (m[h:], l[h:], acc[h:]), base, h, False)
        lower = attend(q[h:], lower, pl.multiple_of(base + h, h), h, True)
        for rows, (m, l, acc) in ((slice(0, h), upper), (slice(h, t), lower)):
            o = acc / l
            o_ref[rows, :] = o
            o16_ref[rows, :] = o.astype(BF16)
            lse_ref[rows, :] = cq_ref[rows, :] + m + jnp.log(l)

    return _hosted_call(
        body, rider, name="fox_fwd", grid=(B, H, nq),
        in_specs=[pl.BlockSpec((t, HEAD), lambda b, h, i: (b * nq + i, 3 * h)),
                  pl.BlockSpec((S, HEAD), lambda b, h, i: (b, 3 * h + 1)),
                  pl.BlockSpec((S, HEAD), lambda b, h, i: (b, 3 * h + 2)),
                  pl.BlockSpec((None, None, t, 1), lambda b, h, i: (b, h, i, 0)),
                  pl.BlockSpec((None, None, 1, S), lambda b, h, i: (b, h, 0, 0))],
        out_specs=[pl.BlockSpec((t, HEAD), lambda b, h, i: (b * nq + i, h)),
                   pl.BlockSpec((t, HEAD), lambda b, h, i: (b * nq + i, h)),
                   pl.BlockSpec((None, None, t, 1), lambda b, h, i: (b, h, i, 0))],
        out_shape=[jax.ShapeDtypeStruct((T, H * HEAD), F32), jax.ShapeDtypeStruct((T, H * HEAD), BF16),
                   jax.ShapeDtypeStruct((B, H, S, 1), F32)],
        scratch_shapes=[], semantics=("parallel", "parallel", "arbitrary"),
    )(p_fox, p_fox, p_fox, c_col, c_row)


def fox_bwd(p_fox, c_col, c_row, o, lse, do, B, S, H, rider=None):
    T = B * S
    t = _tile(S, ATT_TILE)
    n = S // t
    scale = HEAD ** -0.5

    def body(q_ref, k_ref, v_ref, cq_ref, cr_ref, o_ref, lse_ref, do_ref, dqkv_ref, dcq_ref, dcr_ref,
             dq_acc, delta_s, lse_s):
        h = t // 2
        row = lax.broadcasted_iota(jnp.int32, (h, h), 0)
        col = lax.broadcasted_iota(jnp.int32, (h, h), 1)

        def prep(i, c):
            rows = pl.ds(pl.multiple_of(i * t, t), t)
            delta_s[rows, :] = jnp.sum(do_ref[rows, :] * o_ref[rows, :], axis=-1, keepdims=True)
            lse_s[rows, :] = lse_ref[rows, :] - cq_ref[rows, :]
            dq_acc[rows, :] = jnp.zeros((t, HEAD), F32)
            dcq_ref[rows, :] = jnp.zeros((t, 1), F32)
            return c

        lax.fori_loop(0, n, prep, 0)

        def kv_step(j, c):
            joff = pl.multiple_of(j * t, t)
            k = k_ref[pl.ds(joff, t), :]
            v = v_ref[pl.ds(joff, t), :]
            crj = cr_ref[:, pl.ds(joff, t)]

            def pair(rows, kk, vv, cr, carry, diagonal):
                dk, dv, dc = carry
                q = q_ref[rows, :]
                dob = do_ref[rows, :].astype(BF16)
                s = lax.dot_general(q, kk, _NT, preferred_element_type=F32) * scale - cr
                if diagonal:
                    s = jnp.where(col <= row, s, NEG)
                p = jnp.exp(s - lse_s[rows, :])
                dp = lax.dot_general(dob, vv, _NT, preferred_element_type=F32)
                ds = p * (dp - delta_s[rows, :])
                dsb = ds.astype(BF16)
                dv = dv + lax.dot_general(p.astype(BF16), dob, _TN, preferred_element_type=F32)
                dk = dk + lax.dot_general(dsb, q, _TN, preferred_element_type=F32)
                dq_acc[rows, :] += lax.dot_general(dsb, kk, _NN, preferred_element_type=F32) * scale
                dc = dc + jnp.sum(ds, axis=0, keepdims=True)
                dcq_ref[rows, :] += jnp.sum(ds, axis=-1, keepdims=True)
                return dk, dv, dc

            zh = jnp.zeros((h, HEAD), F32)
            zero = (zh, zh, jnp.zeros((1, h), F32))
            up, low = pl.ds(joff, h), pl.ds(pl.multiple_of(joff + h, h), h)
            left = pair(up, k[:h], v[:h], crj[:, :h], zero, True)
            left = pair(low, k[:h], v[:h], crj[:, :h], left, False)
            right = pair(low, k[h:], v[h:], crj[:, h:], zero, True)
            on_diagonal = tuple(jnp.concatenate([a, b], axis=ax) for a, b, ax in zip(left, right, (0, 0, 1)))

            def q_step(i, carry):
                return pair(pl.ds(pl.multiple_of(i * t, t), t), k, v, crj, carry, False)

            dk, dv, dc = lax.fori_loop(j + 1, n, q_step, on_diagonal)
            dqkv_ref[pl.ds(joff, t), HEAD:2 * HEAD] = (dk * scale).astype(BF16)
            dqkv_ref[pl.ds(joff, t), 2 * HEAD:3 * HEAD] = dv.astype(BF16)
            dcr_ref[:, pl.ds(joff, t)] = -dc
            return c

        lax.fori_loop(0, n, kv_step, 0)
        dqkv_ref[:, 0:HEAD] = dq_acc[...].astype(BF16)

    col_spec = pl.BlockSpec((None, None, S, 1), lambda b, h: (b, h, 0, 0))
    row_spec = pl.BlockSpec((None, None, 1, S), lambda b, h: (b, h, 0, 0))
    head = pl.BlockSpec((S, HEAD), lambda b, h: (b, h))
    return _hosted_call(
        body, rider, name="fox_bwd", grid=(B, H),
        in_specs=[pl.BlockSpec((S, HEAD), lambda b, h: (b, 3 * h)),
                  pl.BlockSpec((S, HEAD), lambda b, h: (b, 3 * h + 1)),
                  pl.BlockSpec((S, HEAD), lambda b, h: (b, 3 * h + 2)),
                  col_spec, row_spec, head, col_spec, head],
        out_specs=[pl.BlockSpec((S, 3 * HEAD), lambda b, h: (b, h)), col_spec, row_spec],
        out_shape=[jax.ShapeDtypeStruct((T, 3 * H * HEAD), BF16), jax.ShapeDtypeStruct((B, H, S, 1), F32),
                   jax.ShapeDtypeStruct((B, H, 1, S), F32)],
        scratch_shapes=[pltpu.VMEM((S, HEAD), F32), pltpu.VMEM((S, 1), F32), pltpu.VMEM((S, 1), F32)],
        semantics=("parallel", "parallel"),
    )(p_fox, p_fox, p_fox, c_col, c_row, o, lse, do)


def _small_fn(x, b0, b1, H):
    S = x.shape[0]
    lane = lax.broadcasted_iota(jnp.int32, x.shape, 1)
    z = x + b0
    tail = jnp.log1p(jnp.exp(-jnp.abs(z)))
    softplus = jnp.maximum(z, 0.0) + tail
    logsig = -(jnp.maximum(-z, 0.0) + tail)
    g = -jnp.exp(b1) * softplus
    pre = jnp.where(lane < H, logsig, jnp.where(lane < 2 * H, g, 0.0))
    bl = _tile(S, 256, CHUNK)
    r = lax.broadcasted_iota(jnp.int32, (bl, bl), 0)
    c = lax.broadcasted_iota(jnp.int32, (bl, bl), 1)
    tri = (r >= c).astype(F32)
    tri_chunk = jnp.where((r >= c) & (jnp.right_shift(r, 6) == jnp.right_shift(c, 6)), 1.0, 0.0)
    carry = jnp.zeros((1, x.shape[1]), F32)
    parts = []
    for i in range(S // bl):
        blk = pre[i * bl:(i + 1) * bl, :]
        full = lax.dot_general(tri, blk, _NN, precision=EXACT, preferred_element_type=F32) + carry
        chunked = lax.dot_general(tri_chunk, blk, _NN, precision=EXACT, preferred_element_type=F32)
        parts.append(jnp.where(lane[:bl] < H, full, chunked))
        carry = carry + jnp.sum(blk, axis=0, keepdims=True)
    cum = parts[0] if len(parts) == 1 else jnp.concatenate(parts, axis=0)
    return jnp.where(lane < 2 * H, cum, jnp.where(lane < 3 * H, _sigmoid(x), 0.0))


def small_fwd(p_small, prm, B, S, H):
    T = B * S

    def body(x_ref, p_ref, o_ref):
        o_ref[...] = _small_fn(x_ref[...], p_ref[0:1, :], p_ref[1:2, :], H)

    blk = pl.BlockSpec((S, 128), lambda b: (b, 0))
    return pl.pallas_call(
        body, name="small_fwd", grid=(B,), in_specs=[blk, pl.BlockSpec((8, 128), lambda b: (0, 0))], out_specs=blk,
        out_shape=jax.ShapeDtypeStruct((T, 128), F32), compiler_params=_params("parallel"),
    )(p_small, prm)


def small_bwd(p_small, prm, d_out, B, S, H):
    T = B * S

    def body(x_ref, p_ref, d_ref, dx_ref, dp_ref):
        @pl.when(pl.program_id(0) == 0)
        def _():
            dp_ref[...] = jnp.zeros_like(dp_ref)

        _, vjp = jax.vjp(functools.partial(_small_fn, H=H), x_ref[...], p_ref[0:1, :], p_ref[1:2, :])
        dx, db0, db1 = vjp(d_ref[...])
        dx_ref[...] = dx.astype(BF16)
        dp_ref[0:1, :] += db0
        dp_ref[1:2, :] += db1

    blk = pl.BlockSpec((S, 128), lambda b: (b, 0))
    pblk = pl.BlockSpec((8, 128), lambda b: (0, 0))
    return pl.pallas_call(
        body, name="small_bwd", grid=(B,), in_specs=[blk, pblk, blk], out_specs=[blk, pblk],
        out_shape=[jax.ShapeDtypeStruct((T, 128), BF16), jax.ShapeDtypeStruct((8, 128), F32)],
        compiler_params=_params("arbitrary"),
    )(p_small, prm, d_out)


def gdn_prep_fwd(p_gqkv, cw, B, S, H):
    T = B * S

    def body(x_ref, w_ref, o_ref):
        y = _conv_fwd(x_ref[...], w_ref, GDN_CONV)
        a = y * _sigmoid(y)
        rs = lax.rsqrt(jnp.sum(a * a, axis=-1, keepdims=True) + EPS)
        is_qk = (pl.program_id(1) % 3) < 2
        o_ref[...] = a * jnp.where(is_qk, rs, 1.0)

    blk = pl.BlockSpec((S, HEAD), lambda b, n: (b, n))
    wblk = pl.BlockSpec((GDN_CONV, HEAD), lambda b, n: (0, n))
    return pl.pallas_call(
        body, name="gdn_prep_fwd", grid=(B, 3 * H), in_specs=[blk, wblk], out_specs=blk,
        out_shape=jax.ShapeDtypeStruct((T, 3 * H * HEAD), F32), compiler_params=_params("parallel", "parallel"),
    )(p_gqkv, cw)


def gdn_prep_bwd(p_gqkv, cw, d_out, B, S, H):
    T = B * S

    def body(x_ref, w_ref, d_ref, dx_ref, dw_ref):
        @pl.when(pl.program_id(1) == 0)
        def _():
            dw_ref[...] = jnp.zeros_like(dw_ref)

        x = x_ref[...]
        y, shifted = _conv_fwd(x, w_ref, GDN_CONV, keep_shifted=True)
        sg = _sigmoid(y)
        a = y * sg
        rs = lax.rsqrt(jnp.sum(a * a, axis=-1, keepdims=True) + EPS)
        d = d_ref[...]
        out = a * rs
        da_qk = rs * (d - out * jnp.sum(d * out, axis=-1, keepdims=True))
        is_qk = (pl.program_id(0) % 3) < 2
        da = jnp.where(is_qk, da_qk, d)
        dy = da * (sg + y * sg * (1.0 - sg))
        dx, dws = _conv_bwd(x, shifted, dy, w_ref, GDN_CONV)
        dx_ref[...] = dx.astype(BF16)
        for i in range(GDN_CONV):
            dw_ref[i:i + 1, :] += dws[i]

    blk = pl.BlockSpec((S, HEAD), lambda n, b: (b, n))
    wblk = pl.BlockSpec((GDN_CONV, HEAD), lambda n, b: (0, n))
    return pl.pallas_call(
        body, name="gdn_prep_bwd", grid=(3 * H, B), in_specs=[blk, wblk, blk], out_specs=[blk, wblk],
        out_shape=[jax.ShapeDtypeStruct((T, 3 * H * HEAD), BF16), jax.ShapeDtypeStruct((GDN_CONV, 3 * H * HEAD), F32)],
        compiler_params=_params("parallel", "arbitrary"),
    )(p_gqkv, cw, d_out)


@jax.custom_vjp
def _given_inverse(a, t):
    return t


def _given_inverse_fwd(a, t):
    return t, t


def _given_inverse_bwd(t, g):
    x = _dg(t, g, _TN, True)
    return -_dg(x, t, _NT, True), jnp.zeros_like(t)


_given_inverse.defvjp(_given_inverse_fwd, _given_inverse_bwd)


def _to_col(row):
    n = row.shape[1]
    r = lax.broadcasted_iota(jnp.int32, (n, n), 0)
    c = lax.broadcasted_iota(jnp.int32, (n, n), 1)
    return jnp.sum(jnp.where(r == c, row, 0.0), axis=1, keepdims=True)


def _intra_fn(k, v, beta_r, gcr, ops, t_known=None):
    n = len(k)
    m = k[0].shape[0]
    r = lax.broadcasted_iota(jnp.int32, (m, m), 0)
    c = lax.broadcasted_iota(jnp.int32, (m, m), 1)
    below = (r > c) & (jnp.right_shift(r, 6) == jnp.right_shift(c, 6))
    beta = [_to_col(beta_r[i]) for i in range(n)]
    gcc = [_to_col(gcr[i]) for i in range(n)]
    decay = [jnp.exp(jnp.where(below, gcc[i] - gcr[i], NEG)) for i in range(n)]
    kb = [k[i] * beta[i] for i in range(n)]
    a = [ops.nt(kb[i], k[i]) * decay[i] for i in range(n)]
    if t_known is None:
        p = [-a[i] for i in range(n)]
        tm = [jnp.where(r == c, 1.0, 0.0) + p[i] for i in range(n)]
        for _ in range(5):
            p = [ops.nn(p[i], p[i], hi=True) for i in range(n)]
            tm = [tm[i] + ops.nn(tm[i], p[i], hi=True) for i in range(n)]
    else:
        tm = [_given_inverse(a[i], t_known[i]) for i in range(n)]
    both = [ops.nn(tm[i], jnp.concatenate([v[i] * beta[i], kb[i] * jnp.exp(gcc[i])], axis=1), hi=True) for i in range(n)]
    u_hat = [both[i][:, :HEAD] for i in range(n)]
    w = [both[i][:, HEAD:] for i in range(n)]
    return tuple(u_hat), tuple(w), tuple(tm)


INTRA_NB = 32
PAIR = 1


def gdn_intra_fwd(qkvn, betar5, gcr5, B, S, H, rider=None):
    T = B * S
    UNIT = PAIR * CHUNK
    N = S // UNIT
    nb = min(INTRA_NB // PAIR, N)
    rows = nb * UNIT
    ns = N // nb

    def body(k_ref, v_ref, b_ref, gr_ref, uh_ref, w_ref, t_ref):
        sls = [slice(ci * UNIT, (ci + 1) * UNIT) for ci in range(nb)]
        u_hat, w, tm = _intra_fn(tuple(k_ref[sl, :] for sl in sls), tuple(v_ref[sl, :] for sl in sls),
                                 tuple(b_ref[ci] for ci in range(nb)), tuple(gr_ref[ci] for ci in range(nb)), _RawOps)
        for ci, sl in enumerate(sls):
            uh_ref[sl, :] = u_hat[ci]
            w_ref[sl, :] = w[ci]
            t_ref[ci] = tm[ci]

    rowspec = pl.BlockSpec((None, None, nb, 1, UNIT), lambda b, h, i: (b, h, i, 0, 0))
    sqspec = pl.BlockSpec((None, None, nb, UNIT, UNIT), lambda b, h, i: (b, h, i, 0, 0))
    out = pl.BlockSpec((rows, HEAD), lambda b, h, i: (b * ns + i, h))
    return _hosted_call(
        body, rider, name="gdn_intra_fwd", grid=(B, H, ns),
        in_specs=[pl.BlockSpec((rows, HEAD), lambda b, h, i: (b * ns + i, 3 * h + 1)),
                  pl.BlockSpec((rows, HEAD), lambda b, h, i: (b * ns + i, 3 * h + 2)),
                  rowspec, rowspec],
        out_specs=[out, out, sqspec],
        out_shape=[jax.ShapeDtypeStruct((T, H * HEAD), F32), jax.ShapeDtypeStruct((T, H * HEAD), F32),
                   jax.ShapeDtypeStruct((B, H, N, UNIT, UNIT), F32)],
        scratch_shapes=[], semantics=("parallel", "parallel", "parallel"),
    )(qkvn, qkvn, betar5, gcr5)


def gdn_intra_bwd(qkvn, betar5, gcr5, t_inv, d_uh, d_w, dq_in, dk_in, B, S, H):
    T = B * S
    UNIT = PAIR * CHUNK
    N = S // UNIT
    nb = min(INTRA_NB // PAIR, N)
    rows = nb * UNIT
    ns = N // nb

    def body(k_ref, v_ref, b_ref, gr_ref, t_ref, duh_ref, dw_ref, dq_ref, dk_ref, o_ref, db_ref, dgr_ref):
        sls = [slice(ci * UNIT, (ci + 1) * UNIT) for ci in range(nb)]
        chunks = range(nb)
        _, vjp = jax.vjp(
            functools.partial(_intra_fn, ops=_DiffOps, t_known=tuple(t_ref[ci] for ci in chunks)),
            tuple(k_ref[sl, :] for sl in sls), tuple(v_ref[sl, :] for sl in sls), tuple(b_ref[ci] for ci in chunks),
            tuple(gr_ref[ci] for ci in chunks))
        zero = jnp.zeros((UNIT, UNIT), F32)
        dk, dv, db, dgr = vjp((tuple(duh_ref[sl, :] for sl in sls), tuple(dw_ref[sl, :] for sl in sls),
                               tuple(zero for _ in chunks)))
        for ci, sl in enumerate(sls):
            o_ref[sl, 0:HEAD] = dq_ref[sl, :]
            o_ref[sl, HEAD:2 * HEAD] = dk[ci] + dk_ref[sl, :]
            o_ref[sl, 2 * HEAD:3 * HEAD] = dv[ci]
            db_ref[ci] = db[ci]
            dgr_ref[ci] = dgr[ci]

    rowspec = pl.BlockSpec((None, None, nb, 1, UNIT), lambda b, h, i: (b, h, i, 0, 0))
    sqspec = pl.BlockSpec((None, None, nb, UNIT, UNIT), lambda b, h, i: (b, h, i, 0, 0))
    head = pl.BlockSpec((rows, HEAD), lambda b, h, i: (b * ns + i, h))
    return pl.pallas_call(
        body, name="gdn_intra_bwd", grid=(B, H, ns),
        in_specs=[pl.BlockSpec((rows, HEAD), lambda b, h, i: (b * ns + i, 3 * h + 1)),
                  pl.BlockSpec((rows, HEAD), lambda b, h, i: (b * ns + i, 3 * h + 2)),
                  rowspec, rowspec, sqspec, head, head, head, head],
        out_specs=[pl.BlockSpec((rows, 3 * HEAD), lambda b, h, i: (b * ns + i, h)), rowspec, rowspec],
        out_shape=[jax.ShapeDtypeStruct((T, 3 * H * HEAD), F32),
                   jax.ShapeDtypeStruct((B, H, N, 1, UNIT), F32), jax.ShapeDtypeStruct((B, H, N, 1, UNIT), F32)],
        compiler_params=_params("parallel", "parallel", "parallel"),
    )(qkvn, qkvn, betar5, gcr5, t_inv, d_uh, d_w, dq_in, dk_in)


def _inter_fn(q, k, u_hat, w, gcr, state, ops):
    n = len(q)
    r = lax.broadcasted_iota(jnp.int32, (CHUNK, CHUNK), 0)
    c = lax.broadcasted_iota(jnp.int32, (CHUNK, CHUNK), 1)
    last = lax.broadcasted_iota(jnp.int32, (1, CHUNK), 1) == CHUNK - 1
    gcc = [_to_col(gcr[i]) for i in range(n)]
    gl = [jnp.sum(jnp.where(last, gcr[i], 0.0), axis=1, keepdims=True) for i in range(n)]
    decay = [jnp.exp(jnp.where(r >= c, gcc[i] - gcr[i], NEG)) for i in range(n)]
    qs = [q[i] * (HEAD ** -0.5) for i in range(n)]
    ws = [ops.nn(w[i], state[i]) for i in range(n)]
    qst = [ops.nn(qs[i] * jnp.exp(gcc[i]), state[i]) for i in range(n)]
    attn = [ops.nt(qs[i], k[i]) * decay[i] for i in range(n)]
    u = [u_hat[i] - ws[i] for i in range(n)]
    o = [qst[i] + ops.nn(attn[i], u[i]) for i in range(n)]
    kdu = [ops.tn(k[i] * jnp.exp(gl[i] - gcc[i]), u[i]) for i in range(n)]
    new_state = [state[i] * jnp.exp(gl[i]) + kdu[i] for i in range(n)]
    return tuple(o), tuple(new_state)


INTER_HEADS = 8
INTER_ROWS = 512
INTER_ROWS_BWD = 256


def _inter_heads(H):
    return INTER_HEADS if H % INTER_HEADS == 0 else (4 if H % 4 == 0 else 1)


def _inter_specs(ts, ns, hp, backward):
    at = (lambda s: ns - 1 - s) if backward else (lambda s: s)
    nc = ts // CHUNK
    qk = []
    for hh in range(hp):
        qk.append(pl.BlockSpec((ts, HEAD), lambda b, g, s, hh=hh: (b * ns + at(s), 3 * (hp * g + hh))))
        qk.append(pl.BlockSpec((ts, HEAD), lambda b, g, s, hh=hh: (b * ns + at(s), 3 * (hp * g + hh) + 1)))
    heads = pl.BlockSpec((ts, hp * HEAD), lambda b, g, s: (b * ns + at(s), g))
    rowspec = pl.BlockSpec((None, hp, nc, 1, CHUNK), lambda b, g, s: (b, g, at(s), 0, 0))
    stspec = pl.BlockSpec((None, hp, nc, HEAD, HEAD), lambda b, g, s: (b, g, at(s), 0, 0))
    return qk, heads, rowspec, stspec


def gdn_inter_fwd(qkvn, u_hat, w, gcr5, p_gz, gnorm, B, S, H):
    T = B * S
    N = S // CHUNK
    hp = _inter_heads(H)
    hs = range(hp)
    ts = _tile(S, INTER_ROWS, CHUNK)
    ns, nc = S // ts, ts // CHUNK

    def body(*refs):
        qk_refs, (uh_ref, w_ref, gr_ref, z_ref, gn_ref, o_ref, st_ref, y_ref, s_scr) = refs[:2 * hp], refs[2 * hp:]

        @pl.when(pl.program_id(2) == 0)
        def _():
            s_scr[...] = jnp.zeros_like(s_scr)

        gn = gn_ref[...]

        def step(n, c):
            rows = pl.ds(pl.multiple_of(n * CHUNK, CHUNK), CHUNK)
            st = tuple(s_scr[hh] for hh in hs)
            for hh in hs:
                st_ref[hh, n] = st[hh]
            o, new = _inter_fn(tuple(qk_refs[2 * hh][rows, :] for hh in hs), tuple(qk_refs[2 * hh + 1][rows, :] for hh in hs),
                               tuple(uh_ref[rows, hh * HEAD:(hh + 1) * HEAD] for hh in hs),
                               tuple(w_ref[rows, hh * HEAD:(hh + 1) * HEAD] for hh in hs),
                               tuple(gr_ref[hh, n] for hh in hs), st, _RawOps)
            for hh in hs:
                cols = slice(hh * HEAD, (hh + 1) * HEAD)
                o_ref[rows, cols] = o[hh]
                s_scr[hh] = new[hh]
                z = z_ref[rows, cols]
                r = lax.rsqrt(jnp.mean(o[hh] * o[hh], axis=-1, keepdims=True) + EPS)
                y_ref[rows, cols] = (o[hh] * r * gn * z * _sigmoid(z)).astype(BF16)
            return c

        lax.fori_loop(0, nc, step, 0)

    qk, heads, rowspec, stspec = _inter_specs(ts, ns, hp, backward=False)
    return pl.pallas_call(
        body, name="gdn_inter_fwd", grid=(B, H // hp, ns),
        in_specs=qk + [heads, heads, rowspec, heads, pl.BlockSpec((1, HEAD), lambda b, g, s: (0, 0))],
        out_specs=[heads, stspec, heads],
        out_shape=[jax.ShapeDtypeStruct((T, H * HEAD), F32), jax.ShapeDtypeStruct((B, H, N, HEAD, HEAD), F32),
                   jax.ShapeDtypeStruct((T, H * HEAD), BF16)],
        scratch_shapes=[pltpu.VMEM((hp, HEAD, HEAD), F32)],
        compiler_params=_params("parallel", "parallel", "arbitrary"),
    )(*([qkvn] * (2 * hp)), u_hat, w, gcr5, p_gz, gnorm)


def gdn_inter_bwd(qkvn, u_hat, w, gcr5, states, o, p_gz, gnorm, d_y, B, S, H, rider=None):
    T = B * S
    N = S // CHUNK
    hp = _inter_heads(H)
    hs = range(hp)
    ts = _tile(S, INTER_ROWS_BWD, CHUNK)
    ns, nc = S // ts, ts // CHUNK

    def body(*refs):
        qk_refs = refs[:2 * hp]
        (uh_ref, w_ref, gr_ref, st_ref, o_ref, z_ref, gn_ref, dy_ref,
         dq_ref, dk_ref, duh_ref, dw_ref, dgr_ref, dz_ref, dgn_ref, ds_scr) = refs[2 * hp:]

        @pl.when(pl.program_id(2) == 0)
        def _():
            ds_scr[...] = jnp.zeros_like(ds_scr)
            dgn_ref[...] = jnp.zeros_like(dgn_ref)

        cols = [slice(hh * HEAD, (hh + 1) * HEAD) for hh in hs]
        gn = gn_ref[...]

        def through_norm(rows, hh):
            ov, z, d = o_ref[rows, cols[hh]], z_ref[rows, cols[hh]], dy_ref[rows, cols[hh]]
            r = lax.rsqrt(jnp.mean(ov * ov, axis=-1, keepdims=True) + EPS)
            xh = ov * r
            sg = _sigmoid(z)
            d_n = d * (z * sg)
            dz_ref[rows, cols[hh]] = (d * xh * gn * (sg + z * sg * (1.0 - sg))).astype(BF16)
            dgn_ref[0:1, :] += jnp.sum(d_n * xh, axis=0, keepdims=True)
            dxh = d_n * gn
            return r * (dxh - xh * jnp.mean(dxh * xh, axis=-1, keepdims=True))

        def step(i, c):
            n = nc - 1 - i
            rows = pl.ds(pl.multiple_of(n * CHUNK, CHUNK), CHUNK)
            _, vjp = jax.vjp(functools.partial(_inter_fn, ops=_DiffOps),
                             tuple(qk_refs[2 * hh][rows, :] for hh in hs), tuple(qk_refs[2 * hh + 1][rows, :] for hh in hs),
                             tuple(uh_ref[rows, cols[hh]] for hh in hs), tuple(w_ref[rows, cols[hh]] for hh in hs),
                             tuple(gr_ref[hh, n] for hh in hs), tuple(st_ref[hh, n] for hh in hs))
            dq, dk, duh, dw, dgr, ds = vjp((tuple(through_norm(rows, hh) for hh in hs), tuple(ds_scr[hh] for hh in hs)))
            for hh in hs:
                dq_ref[rows, cols[hh]] = dq[hh]
                dk_ref[rows, cols[hh]] = dk[hh]
                duh_ref[rows, cols[hh]] = duh[hh]
                dw_ref[rows, cols[hh]] = dw[hh]
                dgr_ref[hh, n] = dgr[hh]
                ds_scr[hh] = ds[hh]
            return c

        lax.fori_loop(0, nc, step, 0)

    qk, heads, rowspec, stspec = _inter_specs(ts, ns, hp, backward=True)
    hshape = jax.ShapeDtypeStruct((T, H * HEAD), F32)
    return _hosted_call(
        body, rider, name="gdn_inter_bwd", grid=(B, H // hp, ns),
        in_specs=qk + [heads, heads, rowspec, stspec, heads, heads, pl.BlockSpec((1, HEAD), lambda b, g, s: (0, 0)), heads],
        out_specs=[heads, heads, heads, heads, rowspec, heads,
                   pl.BlockSpec((None, None, 8, HEAD), lambda b, g, s: (b, g, 0, 0))],
        out_shape=[hshape, hshape, hshape, hshape, jax.ShapeDtypeStruct((B, H, N, 1, CHUNK), F32),
                   jax.ShapeDtypeStruct((T, H * HEAD), BF16), jax.ShapeDtypeStruct((B, H // hp, 8, HEAD), F32)],
        scratch_shapes=[pltpu.VMEM((hp, HEAD, HEAD), F32)], semantics=("parallel", "parallel", "arbitrary"),
    )(*([qkvn] * (2 * hp)), u_hat, w, gcr5, states, o, p_gz, gnorm, d_y)


def adamw(w, g, m, v, name):
    shape = w.shape
    lead = (None,) * (w.ndim - 2)
    zeros = (0,) * (w.ndim - 2)
    R, C = shape[-2:]
    g2 = g.reshape(R, C)
    tr, tc = _tile(R, 128, 8), C
    if tr % 8 and R > 8:
        tr, tc = R, _tile(C, 128)

    def body(w_ref, g_ref, m_ref, v_ref, d_ref, nm_ref, nv_ref):
        gv = g_ref[...]
        nm = ADAM_B1 * m_ref[...] + (1.0 - ADAM_B1) * gv
        nv = ADAM_B2 * v_ref[...] + (1.0 - ADAM_B2) * (gv * gv)
        m_hat = nm / (1.0 - ADAM_B1 ** ADAM_STEP)
        v_hat = nv / (1.0 - ADAM_B2 ** ADAM_STEP)
        d_ref[...] = -ADAM_LR * (m_hat / (jnp.sqrt(v_hat) + ADAM_EPS) + ADAM_WD * w_ref[...])
        nm_ref[...] = nm
        nv_ref[...] = nv

    blk = pl.BlockSpec(lead + (tr, tc), lambda i, j: zeros + (i, j))
    gblk = pl.BlockSpec((tr, tc), lambda i, j: (i, j))
    sh = jax.ShapeDtypeStruct(shape, F32)
    return pl.pallas_call(
        body, name=name, grid=(R // tr, C // tc), in_specs=[blk, gblk, blk, blk], out_specs=[blk] * 3, out_shape=[sh] * 3,
        compiler_params=_params("parallel", "parallel"),
    )(w, g2, m, v)


def _place():
    x, y, c = lax.axis_index("x"), lax.axis_index("y"), lax.axis_index("c")
    chips = [(1 - x, y), (x, 1 - y), (1 - x, 1 - y)]
    return x, y, c, chips


_HBM = pl.BlockSpec(memory_space=pltpu.HBM)


def allgather_weights(packs):
    n = len(packs)

    def body(*refs):
        in_refs, out_refs, (send_sems, recv_sems) = refs[:n], refs[n:2 * n], refs[2 * n:]
        x, y, c, chips = _place()
        me_s = 2 * x + y
        me, sibling = (x, y, c), (x, y, 1 - c)
        shards = [2 * chip[0] + chip[1] for chip in chips]

        def copy(a, k, shard, half, to, src=None):
            dst = out_refs[a].at[shard, half]
            return pltpu.make_async_remote_copy(src_ref=dst if src is None else src, dst_ref=dst,
                                                send_sem=send_sems.at[6 * a + k], recv_sem=recv_sems.at[6 * a + k],
                                                device_id=to, device_id_type=MESH)

        first = [copy(a, j, me_s, c, (*chip, c), src=in_refs[a].at[c]) for a in range(n) for j, chip in enumerate(chips)]
        for cp in first:
            cp.start()
        passed = []
        for a in range(n):
            for j in range(3):
                copy(a, j, shards[j], c, me).wait_recv()
                passed.append(copy(a, 3 + j, shards[j], c, sibling))
                passed[-1].start()
        for a in range(n):
            for j in range(3):
                copy(a, 3 + j, shards[j], 1 - c, me).wait_recv()
        for cp in first + passed:
            cp.wait_send()

    return pl.pallas_call(
        body, name="allgather_weights", in_specs=[_HBM] * n, out_specs=[_HBM] * n,
        out_shape=[jax.ShapeDtypeStruct((N_CHIP,) + p.shape, p.dtype) for p in packs],
        scratch_shapes=[pltpu.SemaphoreType.DMA((6 * n,)), pltpu.SemaphoreType.DMA((6 * n,))],
    )(*packs)


class _Rider:
    def __init__(self, inputs, out_shapes, n_sems, sends, recvs, aliases=None):
        self.inputs, self.out_shapes, self.n_sems = list(inputs), list(out_shapes), n_sems
        self.sends, self.recvs, self.aliases = sends, recvs, aliases or {}

    def start(self, *refs):
        for cp in self.sends(*refs):
            cp.start()

    def wait(self, *refs):
        for cp in self.recvs(*refs):
            cp.wait_recv()
        for cp in self.sends(*refs):
            cp.wait_send()


def _remote(src, dst, send_sems, recv_sems, k, to):
    return pltpu.make_async_remote_copy(src_ref=src, dst_ref=dst, send_sem=send_sems.at[k], recv_sem=recv_sems.at[k],
                                        device_id=to, device_id_type=MESH)


def _run_alone(rider, name):
    ri = len(rider.inputs)

    def body(*refs):
        ins, outs, (send_sems, recv_sems) = refs[:ri], refs[ri:-2], refs[-2:]
        rider.start(ins, outs, send_sems, recv_sems)
        rider.wait(ins, outs, send_sems, recv_sems)

    return pl.pallas_call(
        body, name=name, in_specs=[_HBM] * ri, out_specs=[_HBM] * len(rider.out_shapes), out_shape=rider.out_shapes,
        scratch_shapes=[pltpu.SemaphoreType.DMA((rider.n_sems,))] * 2, input_output_aliases=rider.aliases,
    )(*rider.inputs)


def _hosted_call(body, rider, *, name, grid, in_specs, out_specs, out_shape, scratch_shapes, semantics):
    if rider is None:
        return pl.pallas_call(body, name=name, grid=grid, in_specs=in_specs, out_specs=out_specs, out_shape=out_shape,
                              scratch_shapes=scratch_shapes, compiler_params=_params(*semantics))
    n_in, n_out, n_scr = len(in_specs), len(out_specs), len(scratch_shapes)
    ri, ro = len(rider.inputs), len(rider.out_shapes)

    def hosted(*refs):
        parts, p = [], 0
        for cnt in (n_in, ri, n_out, ro, n_scr, 2):
            parts.append(refs[p:p + cnt])
            p += cnt
        ins, rins, outs, routs, scr, (send_sems, recv_sems) = parts
        first = functools.reduce(jnp.logical_and, [pl.program_id(a) == 0 for a in range(len(grid))])
        last = functools.reduce(jnp.logical_and, [pl.program_id(a) == grid[a] - 1 for a in range(len(grid))])

        @pl.when(first)
        def _():
            rider.start(rins, routs, send_sems, recv_sems)

        body(*ins, *outs, *scr)

        @pl.when(last)
        def _():
            rider.wait(rins, routs, send_sems, recv_sems)

    call = pl.pallas_call(
        hosted, name=name, grid=grid, in_specs=list(in_specs) + [_HBM] * ri, out_specs=list(out_specs) + [_HBM] * ro,
        out_shape=list(out_shape) + rider.out_shapes,
        scratch_shapes=list(scratch_shapes) + [pltpu.SemaphoreType.DMA((rider.n_sems,))] * 2,
        input_output_aliases={n_in + i: n_out + o for i, o in rider.aliases.items()},
        compiler_params=_params(*(("arbitrary",) * len(grid))))

    def run(*args):
        res = call(*args, *rider.inputs)
        return res[:n_out], res[n_out:]

    return run


def _ride_gather_ici(packs):
    n = len(packs)

    def sends(ins, outs, send_sems, recv_sems):
        x, y, c, chips = _place()
        return [_remote(ins[a].at[c], outs[a].at[2 * x + y, c], send_sems, recv_sems, 3 * a + j, (*chip, c))
                for a in range(n) for j, chip in enumerate(chips)]

    def recvs(ins, outs, send_sems, recv_sems):
        x, y, c, chips = _place()
        return [_remote(ins[a].at[c], outs[a].at[2 * chip[0] + chip[1], c], send_sems, recv_sems, 3 * a + j, (x, y, c))
                for a in range(n) for j, chip in enumerate(chips)]

    return _Rider(packs, [jax.ShapeDtypeStruct((N_CHIP,) + p.shape, p.dtype) for p in packs], 3 * n, sends, recvs)


def _ride_gather_d2d(gathered):
    n = len(gathered)

    def copies(landing_half, to):
        def build(ins, outs, send_sems, recv_sems):
            x, y, c, chips = _place()
            return [_remote(ins[a].at[2 * chip[0] + chip[1], c], outs[a].at[2 * chip[0] + chip[1], landing_half(c)],
                            send_sems, recv_sems, 3 * a + j, to(x, y, c))
                    for a in range(n) for j, chip in enumerate(chips)]
        return build

    return _Rider(gathered, [jax.ShapeDtypeStruct(g.shape, g.dtype) for g in gathered], 3 * n,
                  copies(lambda c: c, lambda x, y, c: (x, y, 1 - c)), copies(lambda c: 1 - c, lambda x, y, c: (x, y, c)),
                  aliases={a: a for a in range(n)})


def _ride_exchange(gs):
    n = len(gs)
    shapes = [g.shape[1:] if g.ndim == 4 else g.shape[:2] + (g.shape[2] // 2,) for g in gs]

    def copies(ins, outs, send_sems, recv_sems):
        x, y, c, _ = _place()

        def theirs(a):
            if gs[a].ndim == 4:
                return ins[a].at[1 - c]
            cols = shapes[a][2]
            return ins[a].at[:, :, pl.ds((1 - c) * cols, cols)]

        return [_remote(theirs(a), outs[a], send_sems, recv_sems, a, (x, y, 1 - c)) for a in range(n)]

    return _Rider(gs, [jax.ShapeDtypeStruct(sh, g.dtype) for sh, g in zip(shapes, gs)], n, copies, copies)


def _ride_scatter(b16s, to=(0, 1, 2), landing=None):
    n = len(b16s)

    def sends(ins, outs, send_sems, recv_sems):
        x, y, c, chips = _place()
        return [_remote(ins[a].at[2 * chips[j][0] + chips[j][1]], outs[a].at[2 * x + y], send_sems, recv_sems, 3 * a + j,
                        (*chips[j], c)) for a in range(n) for j in to]

    def recvs(ins, outs, send_sems, recv_sems):
        x, y, c, chips = _place()
        return [_remote(ins[a].at[2 * x + y], outs[a].at[2 * chips[j][0] + chips[j][1]], send_sems, recv_sems, 3 * a + j,
                        (x, y, c)) for a in range(n) for j in to]

    return _Rider(list(b16s) + list(landing or []), [jax.ShapeDtypeStruct(b.shape, b.dtype) for b in b16s], 3 * n,
                  sends, recvs, aliases={n + a: a for a in range(n)} if landing else None)


def _slab_tile(r, cols):
    tr = _tile(r, 256, 16)
    if tr % 16 == 0:
        return tr, cols
    return r, _tile(cols, 128)


def add_halves(g, got, idx, name):
    ns, r, cols = got.shape
    tr, tc = _slab_tile(r, cols)
    if g.ndim == 4:
        mine = pl.BlockSpec((None, None, tr, tc), lambda s, i, j, idx_ref: (idx_ref[0], s, i, j))
    else:
        mine = pl.BlockSpec((None, tr, tc), lambda s, i, j, idx_ref: (s, i, idx_ref[0] * (cols // tc) + j))

    def body(idx_ref, a_ref, b_ref, o32_ref, o16_ref):
        s = a_ref[...] + b_ref[...]
        o32_ref[...] = s
        o16_ref[...] = s.astype(BF16)

    blk = pl.BlockSpec((None, tr, tc), lambda s, i, j, idx_ref: (s, i, j))
    return pl.pallas_call(
        body, name=name,
        grid_spec=pltpu.PrefetchScalarGridSpec(
            num_scalar_prefetch=1, grid=(ns, r // tr, cols // tc),
            in_specs=[mine, blk], out_specs=[blk, blk]),
        out_shape=[jax.ShapeDtypeStruct((ns, r, cols), F32), jax.ShapeDtypeStruct((ns, r, cols), BF16)],
        compiler_params=_params("parallel", "parallel", "parallel"),
    )(idx, g, got)


def add_chips(a32, got16, idx, name):
    ns, r, cols = a32.shape
    tr, tc = _slab_tile(r, cols)

    def body(idx_ref, a_ref, r1_ref, r2_ref, r3_ref, o_ref):
        o_ref[...] = ((a_ref[...] + r1_ref[...].astype(F32)) + r2_ref[...].astype(F32)) + r3_ref[...].astype(F32)

    def slab(k):
        return pl.BlockSpec((None, tr, tc), lambda i, j, idx_ref: ((idx_ref[1] + k) % ns, i, j))

    return pl.pallas_call(
        body, name=name,
        grid_spec=pltpu.PrefetchScalarGridSpec(
            num_scalar_prefetch=1, grid=(r // tr, cols // tc), in_specs=[slab(0), slab(1), slab(2), slab(3)],
            out_specs=pl.BlockSpec((tr, tc), lambda i, j, idx_ref: (i, j))),
        out_shape=jax.ShapeDtypeStruct((r, cols), F32),
        compiler_params=_params("parallel", "parallel"),
    )(idx, a32, got16, got16, got16)


def share_halves(halves):
    n = len(halves)

    def body(*refs):
        in_refs, out_refs, (send_sems, recv_sems) = refs[:n], refs[n:2 * n], refs[2 * n:]
        x, y, c, _ = _place()
        cps = [pltpu.make_async_remote_copy(src_ref=in_refs[a], dst_ref=out_refs[a], send_sem=send_sems.at[a],
                                            recv_sem=recv_sems.at[a], device_id=(x, y, 1 - c), device_id_type=MESH)
               for a in range(n)]
        for cp in cps:
            cp.start()
        for cp in cps:
            cp.wait()

    return pl.pallas_call(
        body, name="share_halves", in_specs=[_HBM] * n, out_specs=[_HBM] * n,
        out_shape=[jax.ShapeDtypeStruct(h.shape, F32) for h in halves],
        scratch_shapes=[pltpu.SemaphoreType.DMA((n,)), pltpu.SemaphoreType.DMA((n,))],
    )(*halves)


def allreduce_small(v):
    R, _ = v.shape

    def body(in_ref, out_ref, slots, send_sems, recv_sems):
        x, y, c, _ = _place()
        me = 4 * x + 2 * y + c
        slots[me] = in_ref[...]
        cps = []
        for k in range(1, N_DEV):
            to = (x ^ (k >> 2), y ^ ((k >> 1) & 1), c ^ (k & 1))
            cps.append(pltpu.make_async_remote_copy(src_ref=in_ref, dst_ref=slots.at[me], send_sem=send_sems.at[k - 1],
                                                    recv_sem=recv_sems.at[k - 1], device_id=to, device_id_type=MESH))
        for cp in cps:
            cp.start()
        for k in range(1, N_DEV):
            frm = 4 * (x ^ (k >> 2)) + 2 * (y ^ ((k >> 1) & 1)) + (c ^ (k & 1))
            pltpu.make_async_remote_copy(src_ref=in_ref, dst_ref=slots.at[frm], send_sem=send_sems.at[k - 1],
                                         recv_sem=recv_sems.at[k - 1], device_id=(x, y, c), device_id_type=MESH).wait_recv()
        for cp in cps:
            cp.wait_send()
        acc = slots[0]
        for d in range(1, N_DEV):
            acc = acc + slots[d]
        out_ref[...] = acc

    vm = pl.BlockSpec(memory_space=pltpu.VMEM)
    return pl.pallas_call(
        body, name="allreduce_small", in_specs=[vm], out_specs=vm, out_shape=jax.ShapeDtypeStruct((R, ROW), F32),
        scratch_shapes=[pltpu.VMEM((N_DEV, R, ROW), F32), pltpu.SemaphoreType.DMA((N_DEV - 1,)),
                        pltpu.SemaphoreType.DMA((N_DEV - 1,))],
    )(v)


def _rows_of(n, unit=16):
    return -(-n // (unit * ROW)) * unit


def _pack_rows(items, total_rows, dtype, unit=16):
    parts = []
    used = 0
    for a in items:
        flat = a.reshape(-1)
        r = _rows_of(flat.shape[0], unit)
        flat = jnp.pad(flat, (0, r * ROW - flat.shape[0]))
        parts.append(flat.reshape(r, ROW))
        used += r
    if total_rows > used:
        parts.append(jnp.zeros((total_rows - used, ROW), dtype))
    return jnp.concatenate(parts, axis=0)


def _unpack_rows(buf, shapes, unit=16):
    lead = buf.shape[:-2]
    out = []
    off = 0
    for shp in shapes:
        n = math.prod(shp)
        r = _rows_of(n, unit)
        piece = buf[..., off:off + r, :].reshape(*lead, r * ROW)[..., :n].reshape(*lead, *shp)
        out.append(piece)
        off += r
    return out


def _interleave_heads(w, H):
    lead = w.shape[:-1]
    return w.reshape(*lead, 3, H, HEAD).swapaxes(-3, -2).reshape(*lead, 3 * H * HEAD)


def _deinterleave_heads(w, H):
    lead = w.shape[:-1]
    return w.reshape(*lead, H, 3, HEAD).swapaxes(-3, -2).reshape(*lead, 3 * H * HEAD)


def _interleave_head_rows(w, H):
    return w.reshape(3, H, HEAD, w.shape[-1]).swapaxes(0, 1).reshape(3 * H * HEAD, w.shape[-1])


def _deinterleave_head_rows(w, H):
    return w.reshape(H, 3, HEAD, w.shape[-1]).swapaxes(0, 1).reshape(3 * H * HEAD, w.shape[-1])


def kernel(x, norm_mix, w_in, fox_f_bias, gdn_conv_w, gdn_a_log, gdn_dt_bias, gdn_norm, w_branch_fox, w_branch_gdn, w_out, norm_ffn, w_up, ffn_conv_w, w_down, norm_final, loss_target, m_norm_mix, m_w_in, m_fox_f_bias, m_gdn_conv_w, m_gdn_a_log, m_gdn_dt_bias, m_gdn_norm, m_w_branch_fox, m_w_branch_gdn, m_w_out, m_norm_ffn, m_w_up, m_ffn_conv_w, m_w_down, m_norm_final, v_norm_mix, v_w_in, v_fox_f_bias, v_gdn_conv_w, v_gdn_a_log, v_gdn_dt_bias, v_gdn_norm, v_w_branch_fox, v_w_branch_gdn, v_w_out, v_norm_ffn, v_w_up, v_ffn_conv_w, v_w_down, v_norm_final):
    B, S, D = x.shape
    T = B * S
    H = D // HEAD
    N = S // CHUNK
    FF = w_down.shape[1] * N_CHIP
    d_in = 9 * D + 3 * H
    assert w_in.shape[2] * N_CHIP == d_in and 3 * H <= 128

    cidx = lax.axis_index("c").astype(jnp.int32)
    sidx = (2 * lax.axis_index("x") + lax.axis_index("y")).astype(jnp.int32)
    idx = jnp.stack([cidx, sidx])

    rowed = [w_branch_fox[0], w_branch_gdn[0], w_out[0], w_down[0]]
    convs = [gdn_conv_w[0], ffn_conv_w[0]]
    rowed_shapes = [a.shape for a in rowed]
    conv_shapes = [a.shape + (2,) for a in convs]
    pad_rows = lambda shapes: -(-sum(_rows_of(math.prod(s)) for s in shapes) // 256) * 128
    Rh, Rc = pad_rows(rowed_shapes), pad_rows(conv_shapes)
    halves = lambda a: a.reshape(2, a.shape[0] // 2, a.shape[1])
    c_in = w_in.shape[2]
    packs_a = [w_in[0].T.astype(BF16).reshape(c_in, 2, D // 2).transpose(1, 0, 2),
               halves(_pack_rows([lax.bitcast_convert_type(a, BF16) for a in convs], 2 * Rc, BF16))]
    packs_b = [halves(w_up[0].astype(BF16)), halves(_pack_rows([a.astype(BF16) for a in rowed], 2 * Rh, BF16))]
    own = lambda gs, ps: [lax.dynamic_update_slice(g, p[None], (sidx, 0, 0, 0)) for g, p in zip(gs, ps)]
    by_cols = lambda g: g.transpose(1, 2, 0, 3).reshape(2 * g.shape[2], N_CHIP * g.shape[3])
    cat_cols = lambda p: jnp.concatenate([p[i] for i in range(N_CHIP)], axis=-1)
    cat_rows = lambda p: p.reshape(-1, p.shape[-1])
    g_in, g_conv = own(allgather_weights(packs_a), packs_a)
    W_inT = g_in.transpose(0, 2, 1, 3).reshape(N_CHIP * c_in, D)
    conv_parts = _unpack_rows(g_conv.reshape(N_CHIP, 2 * Rc, ROW), conv_shapes)
    gconv = cat_cols(lax.bitcast_convert_type(conv_parts[0], F32))
    fconv = cat_cols(lax.bitcast_convert_type(conv_parts[1], F32))

    o1, o2 = 3 * D, 3 * D + H
    o3, o4, o5, o6 = o2 + 3 * D, o2 + 3 * D + H, o2 + 3 * D + 2 * H, o2 + 4 * D + 2 * H
    W_foxT = _interleave_head_rows(W_inT[:o1], H)
    W_gqkvT = _interleave_head_rows(W_inT[o2:o3], H)
    W_gzT = W_inT[o5:o6]
    W_gatesT = W_inT[o6:]
    W_smallT = jnp.concatenate([W_inT[o1:o2], W_inT[o3:o5], jnp.zeros((128 - 3 * H, D), BF16)], axis=0)
    gconv_i = _interleave_heads(gconv, H)
    fconv_g, fconv_v = fconv[:, :FF], fconv[:, FF:]
    prm = jnp.zeros((8, 128), F32)
    prm = prm.at[0, 0:H].set(fox_f_bias[0]).at[0, H:2 * H].set(gdn_dt_bias[0]).at[1, H:2 * H].set(gdn_a_log[0])

    x2 = x.reshape(T, D)
    tgt = loss_target.reshape(T, D)

    hn1 = rmsnorm_fwd(x2, norm_mix, "rmsnorm_mix")
    p_fox = matmul(hn1, W_foxT, "nt", "proj_fox", out_dtype=BF16)
    p_gqkv = matmul(hn1, W_gqkvT, "nt", "proj_gqkv")
    p_gz = matmul(hn1, W_gzT, "nt", "proj_gz")
    p_gates = matmul(hn1, W_gatesT, "nt", "proj_gates")
    p_small = matmul(hn1, W_smallT, "nt", "proj_small")

    sm = small_fwd(p_small, prm, B, S, H)
    heads = lambda a: a.reshape(B, S, H).transpose(0, 2, 1)
    c_bhs, gc_bhs, beta_bhs = heads(sm[:, 0:H]), heads(sm[:, H:2 * H]), heads(sm[:, 2 * H:3 * H])
    c_col, c_row = c_bhs[..., None], c_bhs[:, :, None, :]
    gcr5 = gc_bhs.reshape(B, H, N, 1, CHUNK)
    gcr_u = gc_bhs.reshape(B, H, N // PAIR, 1, PAIR * CHUNK)
    betar_u = beta_bhs.reshape(B, H, N // PAIR, 1, PAIR * CHUNK)

    (o_fox, o_fox16, lse), arriving = fox_fwd(p_fox, c_col, c_row, B, S, H, rider=_ride_gather_ici(packs_b))
    qkvn = gdn_prep_fwd(p_gqkv, gconv_i, B, S, H)
    (u_hat, w_t, t_inv), arrived = gdn_intra_fwd(qkvn, betar_u, gcr_u, B, S, H, rider=_ride_gather_d2d(arriving))
    g_up, g_rowed = own(arrived, packs_b)
    W_up = by_cols(g_up)
    W_up_g, W_up_v = W_up[:, :FF], W_up[:, FF:]
    W_bf, W_bg, W_out, W_down = (cat_rows(p) for p in _unpack_rows(g_rowed.reshape(N_CHIP, 2 * Rh, ROW), rowed_shapes))
    o_gdn, states, y_gdn = gdn_inter_fwd(qkvn, u_hat, w_t, gcr5, p_gz, gdn_norm, B, S, H)
    bf_ = matmul(o_fox16, W_bf, "nn", "branch_fox")
    bg_, y = matmul(y_gdn, W_bg, "nn", "branch_gdn", post=_post_merge(p_gates, bf_))
    h1, hn2 = matmul(y, W_out, "nn", "out_proj", add=x2, post=_post_rmsnorm(norm_ffn))
    up_g = matmul(hn2, W_up_g, "nn", "up_gate")
    up_v = matmul(hn2, W_up_v, "nn", "up_val")
    act = ffn_gate_fwd(up_g, up_v, fconv_g, fconv_v, B, S)
    dh2, dh2_16, loss_cols, d_norm_final = matmul(act, W_down, "nn", "down_proj", add=h1,
                                                  post=_post_loss(norm_final.reshape(1, D), tgt))
    loss = lax.psum(0.5 * jnp.sum(loss_cols) / D, ("x", "y", "c"))

    d_act = matmul(dh2_16, W_down, "nt", "d_act")
    dW_down = matmul(act, dh2_16, "tn", "dw_down")
    d_upg, d_upv, d_fconv_g, d_fconv_v = ffn_gate_bwd(up_g, up_v, fconv_g, fconv_v, d_act, B, S)
    d_hn2 = matmul(d_upg, W_up_g, "nt", "d_hn2_g")
    dh1, dh1_16, d_norm_ffn = matmul(d_upv, W_up_v, "nt", "d_hn2_v", add=d_hn2,
                                     post=_post_rmsnorm_bwd(h1, norm_ffn, dh2, True))
    dW_up = jnp.concatenate([matmul(hn2, d_upg, "tn", "dw_up_g"), matmul(hn2, d_upv, "tn", "dw_up_v")], axis=1)
    d_bf, d_bg, d_gates = matmul(dh1_16, W_out, "nt", "d_y", post=_post_merge_bwd(p_gates, bf_, bg_))
    dW_out = matmul(y, dh1_16, "tn", "dw_out")
    d_ofox = matmul(d_bf, W_bf, "nt", "d_ofox")
    dW_bf = matmul(o_fox16, d_bf, "tn", "dw_bf")
    d_ygdn = matmul(d_bg, W_bg, "nt", "d_ygdn")
    dW_bg = matmul(y_gdn, d_bg, "tn", "dw_bg")

    d_fconv = jnp.concatenate([d_fconv_g, d_fconv_v], axis=1)
    col_shard = lambda g, s: g[:, s * (g.shape[1] // N_CHIP):(s + 1) * (g.shape[1] // N_CHIP)]
    row_shard = lambda g, s: g[s * (g.shape[0] // N_CHIP):(s + 1) * (g.shape[0] // N_CHIP)]
    shard_items = lambda s: [row_shard(dW_bf, s), row_shard(dW_bg, s), row_shard(dW_out, s), row_shard(dW_down, s),
                             col_shard(d_fconv, s)]
    g_shapes = [a.shape for a in shard_items(0)]
    assert sum(_rows_of(math.prod(s)) for s in g_shapes) <= 2 * Rh
    to_slabs = lambda g: g.reshape(2, g.shape[0] // 2, N_CHIP, g.shape[1] // N_CHIP).transpose(0, 2, 1, 3)
    gpacks_b = [to_slabs(dW_up),
                jnp.stack([_pack_rows(shard_items(s), 2 * Rh, F32).reshape(2, Rh, ROW) for s in range(N_CHIP)], axis=1)]
    (d_pfox, d_ccol, d_crow), gots_b = fox_bwd(p_fox, c_col, c_row, o_fox, lse, d_ofox, B, S, H,
                                              rider=_ride_exchange(gpacks_b))
    sums_b = [add_halves(g, got, idx, "add_halves_b%d" % i) for i, (g, got) in enumerate(zip(gpacks_b, gots_b))]

    (dq_i, dk_i, d_uh, d_wt, dgcr_a, d_gz, d_gn_parts), got16_b = gdn_inter_bwd(
        qkvn, u_hat, w_t, gcr5, states, o_gdn, p_gz, gdn_norm, d_ygdn, B, S, H,
        rider=_ride_scatter([s16 for _, s16 in sums_b]))
    d_gdn_norm = jnp.sum(d_gn_parts[:, :, 0, :], axis=(0, 1))[None]
    mine_b = [add_chips(s32, g16, idx, "add_chips_b%d" % i) for i, ((s32, _), g16) in enumerate(zip(sums_b, got16_b))]
    d_qkvn, d_betar5, dgcr_b = gdn_intra_bwd(qkvn, betar_u, gcr_u, t_inv, d_uh, d_wt, dq_i, dk_i, B, S, H)
    d_pgqkv, d_gconv_i = gdn_prep_bwd(p_gqkv, gconv_i, d_qkvn, B, S, H)

    tokens = lambda a: a.reshape(B, H, S).transpose(0, 2, 1).reshape(T, H)
    d_gc = dgcr_a.reshape(B, H, S) + dgcr_b.reshape(B, H, S)
    d_sm = jnp.concatenate([tokens(d_ccol.reshape(B, H, S) + d_crow.reshape(B, H, S)), tokens(d_gc), tokens(d_betar5.reshape(B, H, S)),
                            jnp.zeros((T, 128 - 3 * H), F32)], axis=1)
    d_psmall, d_prm = small_bwd(p_small, prm, d_sm, B, S, H)

    dW_foxT = matmul(d_pfox, hn1, "tn", "dw_fox")
    dW_gqkvT = matmul(d_pgqkv, hn1, "tn", "dw_gqkv")
    dW_gzT = matmul(d_gz, hn1, "tn", "dw_gz")
    dW_gatesT = matmul(d_gates, hn1, "tn", "dw_gates")
    dW_smallT = matmul(d_psmall, hn1, "tn", "dw_small")
    dW_inT = jnp.concatenate([_deinterleave_head_rows(dW_foxT, H), dW_smallT[0:H], _deinterleave_head_rows(dW_gqkvT, H),
                              dW_smallT[H:3 * H], dW_gzT, dW_gatesT], axis=0)
    d_gconv = _deinterleave_heads(d_gconv_i, H)

    gpack_a = [dW_inT.reshape(N_CHIP, c_in, D)]
    d_hn1, gots_a = matmul(d_pfox, W_foxT, "nn", "d_hn1_fox", rider=_ride_exchange(gpack_a))
    sums_a = [add_halves(gpack_a[0], gots_a[0], idx, "add_halves_a")]
    d_hn1, landing_a = matmul(d_pgqkv, W_gqkvT, "nn", "d_hn1_gqkv", add=d_hn1,
                              rider=_ride_scatter([sums_a[0][1]], to=(0, 1)))
    d_hn1 = matmul(d_gz, W_gzT, "nn", "d_hn1_gz", add=d_hn1)
    d_hn1, got16_a = matmul(d_gates, W_gatesT, "nn", "d_hn1_gates", add=d_hn1,
                            rider=_ride_scatter([sums_a[0][1]], to=(2,), landing=landing_a))
    mine = [add_chips(sums_a[0][0], got16_a[0], idx, "add_chips_a")] + mine_b
    grad_x, d_norm_mix = matmul(d_psmall, W_smallT, "nn", "d_hn1_small", add=d_hn1,
                                post=_post_rmsnorm_bwd(x2, norm_mix, dh1, False))

    others = share_halves(mine)
    g_w_inT, g_up, g_rows = (jnp.concatenate([jnp.where(cidx == 0, h, o), jnp.where(cidx == 0, o, h)], axis=ax)
                             for h, o, ax in zip(mine, others, (1, 0, 0)))
    g_w_in = g_w_inT.T
    g_bf, g_bg, g_out, g_down, g_fconv = _unpack_rows(g_rows, g_shapes)

    small_items = [d_norm_mix, d_norm_ffn, d_norm_final, d_gdn_norm, d_prm, d_gconv]
    small_shapes = [a.shape for a in small_items]
    sv = allreduce_small(_pack_rows(small_items, 0, F32, unit=8))
    g_norm_mix, g_norm_ffn, g_norm_final, g_gdn_norm, g_prm, g_gconv_all = _unpack_rows(sv, small_shapes, unit=8)
    g_norm_final = g_norm_final.reshape(D)
    g_fbias, g_dtb, g_alog = g_prm[0:1, 0:H], g_prm[0:1, H:2 * H], g_prm[1:2, H:2 * H]
    g_gconv = lax.dynamic_slice_in_dim(g_gconv_all, sidx * (3 * D // N_CHIP), 3 * D // N_CHIP, axis=1)

    names = ["norm_mix", "w_in", "fox_f_bias", "gdn_conv_w", "gdn_a_log", "gdn_dt_bias", "gdn_norm", "w_branch_fox",
             "w_branch_gdn", "w_out", "norm_ffn", "w_up", "ffn_conv_w", "w_down", "norm_final"]
    ws = [norm_mix, w_in, fox_f_bias, gdn_conv_w, gdn_a_log, gdn_dt_bias, gdn_norm, w_branch_fox, w_branch_gdn, w_out,
          norm_ffn, w_up, ffn_conv_w, w_down, norm_final]
    ms = [m_norm_mix, m_w_in, m_fox_f_bias, m_gdn_conv_w, m_gdn_a_log, m_gdn_dt_bias, m_gdn_norm, m_w_branch_fox,
          m_w_branch_gdn, m_w_out, m_norm_ffn, m_w_up, m_ffn_conv_w, m_w_down, m_norm_final]
    vs = [v_norm_mix, v_w_in, v_fox_f_bias, v_gdn_conv_w, v_gdn_a_log, v_gdn_dt_bias, v_gdn_norm, v_w_branch_fox,
          v_w_branch_gdn, v_w_out, v_norm_ffn, v_w_up, v_ffn_conv_w, v_w_down, v_norm_final]
    gs = [g_norm_mix, g_w_in, g_fbias, g_gconv, g_alog, g_dtb, g_gdn_norm, g_bf, g_bg, g_out, g_norm_ffn, g_up,
          g_fconv, g_down, g_norm_final]
    gs = [g.reshape(w.shape) for g, w in zip(gs, ws)]
    deltas, new_ms, new_vs = [], [], []
    for nm, w, g, m, v in zip(names, ws, gs, ms, vs):
        if w.ndim == 1:
            d, a, b = adamw(w.reshape(1, -1), g.reshape(1, -1), m.reshape(1, -1), v.reshape(1, -1), "adamw_" + nm)
            d, a, b = d.reshape(w.shape), a.reshape(w.shape), b.reshape(w.shape)
        elif nm == "w_in":
            d, a, b = (r.T[None] for r in adamw(w[0].T, g_w_inT, m[0].T, v[0].T, "adamw_" + nm))
        else:
            d, a, b = adamw(w, g, m, v, "adamw_" + nm)
        deltas.append(d)
        new_ms.append(a)
        new_vs.append(b)

    return (loss, grad_x.reshape(B, S, D), *gs, *deltas, *new_ms, *new_vs)
```

```python
import functools
import math

import jax
import jax.numpy as jnp
from jax import lax
from jax.experimental import pallas as pl
from jax.experimental.pallas import tpu as pltpu

F32 = jnp.float32
BF16 = jnp.bfloat16
HEAD = 128
CHUNK = 64
GDN_CONV = 4
FFN_CONV = 3
EPS = 1e-6
NEG = -1e30
ROW = 1024
ATT_TILE = 512
MM_WEIGHT_TILE_BYTES = 8 << 20
MM_TN_OPERAND_BYTES = 32 << 20
N_CHIP = 4
N_DEV = 8
MESH = pl.DeviceIdType.MESH
HI = lax.Precision.HIGH
EXACT = lax.Precision.HIGHEST

ADAM_LR, ADAM_B1, ADAM_B2, ADAM_EPS, ADAM_WD, ADAM_STEP = 0.001, 0.9, 0.999, 1e-08, 0.01, 10


def _tile(n, cap, unit=128):
    best = None
    t = unit
    while t <= min(n, cap):
        if n % t == 0:
            best = t
        t += unit
    return best if best is not None else n


def _params(*sem):
    return pltpu.CompilerParams(dimension_semantics=sem)


_NN = (((1,), (0,)), ((), ()))
_NT = (((1,), (1,)), ((), ()))
_TN = (((0,), (0,)), ((), ()))


def _dg(a, b, dims, hi):
    if hi:
        return lax.dot_general(a, b, dims, precision=HI, preferred_element_type=F32)
    return lax.dot_general(a.astype(BF16), b.astype(BF16), dims, preferred_element_type=F32)


class _RawOps:
    @staticmethod
    def nn(a, b, hi=False):
        return _dg(a, b, _NN, hi)

    @staticmethod
    def nt(a, b, hi=False):
        return _dg(a, b, _NT, hi)

    @staticmethod
    def tn(a, b, hi=False):
        return _dg(a, b, _TN, hi)


def _make_diff_ops():
    def build(hi):
        @jax.custom_vjp
        def nn(a, b):
            return _dg(a, b, _NN, hi)

        nn.defvjp(lambda a, b: (_dg(a, b, _NN, hi), (a, b)),
                  lambda r, g: (_dg(g, r[1], _NT, hi), _dg(r[0], g, _TN, hi)))

        @jax.custom_vjp
        def nt(a, b):
            return _dg(a, b, _NT, hi)

        nt.defvjp(lambda a, b: (_dg(a, b, _NT, hi), (a, b)),
                  lambda r, g: (_dg(g, r[1], _NN, hi), _dg(g, r[0], _TN, hi)))

        @jax.custom_vjp
        def tn(a, b):
            return _dg(a, b, _TN, hi)

        tn.defvjp(lambda a, b: (_dg(a, b, _TN, hi), (a, b)),
                  lambda r, g: (_dg(r[1], g, _NT, hi), _dg(r[0], g, _NN, hi)))
        return nn, nt, tn

    lo, hi_ = build(False), build(True)

    class _DiffOps:
        @staticmethod
        def nn(a, b, hi=False):
            return (hi_ if hi else lo)[0](a, b)

        @staticmethod
        def nt(a, b, hi=False):
            return (hi_ if hi else lo)[1](a, b)

        @staticmethod
        def tn(a, b, hi=False):
            return (hi_ if hi else lo)[2](a, b)

    return _DiffOps


_DiffOps = _make_diff_ops()


def _sigmoid(x):
    return 1.0 / (1.0 + jnp.exp(-x))


def _mm_tile(n, pref):
    if n % pref == 0:
        return pref
    if n % 1408 == 0:
        return 1408
    return _tile(n, pref)


class _Post:
    def __init__(self, fn, row_ins=(), vec_ins=(), row_outs=(), acc_outs=(), keep_main=True):
        self.fn, self.keep_main = fn, keep_main
        self.row_ins = [r if isinstance(r, tuple) else (r, r.shape[1], 0) for r in row_ins]
        self.vec_ins, self.row_outs, self.acc_outs = list(vec_ins), list(row_outs), list(acc_outs)


def matmul(a, b, mode, name, add=None, out_dtype=F32, post=None, rider=None):
    if mode == "nn":
        (M, K), (K2, N) = a.shape, b.shape
    elif mode == "nt":
        (M, K), (N, K2) = a.shape, b.shape
    else:
        (K, M), (K2, N) = a.shape, b.shape
    assert K == K2, (name, a.shape, b.shape)
    tn = _mm_tile(N, 1024)
    if mode == "tn":
        tm = M if M <= 1408 else _mm_tile(M, 1408)
        row_bytes = 2 * (tm * a.dtype.itemsize + tn * b.dtype.itemsize)
        tk = next((t for t in (4096, 2048) if K % t == 0 and t * row_bytes <= MM_TN_OPERAND_BYTES), _mm_tile(K, 1024))
    else:
        tk = K if K * tn * 2 <= MM_WEIGHT_TILE_BYTES else _mm_tile(K, 1024)
        tm = _mm_tile(M, 1024 if tk <= 2048 and post is None else 512)
    nk = K // tk
    assert post is None or (mode != "tn" and tn == N), name
    dims = {"nn": _NN, "nt": _NT, "tn": _TN}[mode]
    if mode == "tn":
        a_spec = pl.BlockSpec((tk, tm), lambda j, i, k: (k, i))
    else:
        a_spec = pl.BlockSpec((tm, tk), lambda j, i, k: (i, k))
    if mode == "nt":
        b_spec = pl.BlockSpec((tn, tk), lambda j, i, k: (j, k))
    else:
        b_spec = pl.BlockSpec((tk, tn), lambda j, i, k: (k, j))
    o_spec = pl.BlockSpec((tm, tn), lambda j, i, k: (i, j))
    has_add = add is not None
    keep_main = post is None or post.keep_main
    counts = [2 + has_add] + ([len(post.row_ins), len(post.vec_ins)] if post else [0, 0]) + [int(keep_main)]
    counts += ([len(post.row_outs), len(post.acc_outs)] if post else [0, 0]) + [int(nk > 1)]

    def body(*refs):
        parts, p = [], 0
        for cnt in counts:
            parts.append(refs[p:p + cnt])
            p += cnt
        core, row_ins, vec_ins, main, row_outs, acc_outs, acc = parts
        a_ref, b_ref = core[:2]
        prod = lax.dot_general(a_ref[...].astype(BF16), b_ref[...].astype(BF16), dims, preferred_element_type=F32)

        def finish(r):
            if has_add:
                r = r + core[2][...]
            if keep_main:
                main[0][...] = r.astype(out_dtype)
            if post is not None:
                @pl.when(pl.program_id(1) == 0)
                def _():
                    for ref in acc_outs:
                        ref[...] = jnp.zeros_like(ref)

                post.fn(r, row_ins, vec_ins, row_outs, acc_outs)

        if nk == 1:
            finish(prod)
            return
        acc_ref = acc[0]
        k = pl.program_id(2)

        @pl.when(k == 0)
        def _():
            acc_ref[...] = jnp.zeros_like(acc_ref)

        acc_ref[...] += prod

        @pl.when(k == nk - 1)
        def _():
            finish(acc_ref[...])

    in_specs = [a_spec, b_spec] + ([o_spec] if has_add else [])
    args = (a, b) + ((add,) if has_add else ())
    out_specs = [o_spec] if keep_main else []
    out_shape = [jax.ShapeDtypeStruct((M, N), out_dtype)] if keep_main else []
    if post is not None:
        in_specs += [pl.BlockSpec((tm, cols), lambda j, i, k, cb=cb: (i, cb)) for _, cols, cb in post.row_ins]
        in_specs += [pl.BlockSpec((1, v.shape[1]), lambda j, i, k: (0, 0)) for v in post.vec_ins]
        args += tuple(r for r, _, _ in post.row_ins) + tuple(post.vec_ins)
        out_specs += [pl.BlockSpec((tm, cols), lambda j, i, k: (i, 0)) for cols, _ in post.row_outs]
        out_specs += [pl.BlockSpec((1, cols), lambda j, i, k: (0, 0)) for cols in post.acc_outs]
        out_shape += [jax.ShapeDtypeStruct((M, cols), dt) for cols, dt in post.row_outs]
        out_shape += [jax.ShapeDtypeStruct((1, cols), F32) for cols in post.acc_outs]
    rows_sem = "arbitrary" if post is not None and post.acc_outs else "parallel"
    res = _hosted_call(
        body, rider, name=name, grid=(N // tn, M // tm, nk), in_specs=in_specs, out_specs=out_specs, out_shape=out_shape,
        scratch_shapes=[pltpu.VMEM((tm, tn), F32)] if nk > 1 else [], semantics=("parallel", rows_sem, "arbitrary"),
    )(*args)
    if rider is not None:
        res, carried = res
        return (res[0] if post is None else res), carried
    return res[0] if post is None else res


def rmsnorm_fwd(x, g, name):
    T, D = x.shape
    tm = _tile(T, 512, 8)

    def body(x_ref, g_ref, o_ref):
        xv = x_ref[...]
        r = lax.rsqrt(jnp.mean(xv * xv, axis=-1, keepdims=True) + EPS)
        o_ref[...] = (xv * r * g_ref[...]).astype(BF16)

    return pl.pallas_call(
        body, name=name, grid=(T // tm,),
        in_specs=[pl.BlockSpec((tm, D), lambda i: (i, 0)), pl.BlockSpec((1, D), lambda i: (0, 0))],
        out_specs=pl.BlockSpec((tm, D), lambda i: (i, 0)),
        out_shape=jax.ShapeDtypeStruct((T, D), BF16),
        compiler_params=_params("parallel"),
    )(x, g)


def _post_rmsnorm(g):
    def fn(r, row_ins, vec_ins, row_outs, acc_outs):
        rs = lax.rsqrt(jnp.mean(r * r, axis=-1, keepdims=True) + EPS)
        row_outs[0][...] = (r * rs * vec_ins[0][...]).astype(BF16)

    return _Post(fn, vec_ins=[g], row_outs=[(g.shape[1], BF16)])


def _post_rmsnorm_bwd(x, g, dres, with_bf16):
    D = g.shape[1]

    def fn(dy, row_ins, vec_ins, row_outs, acc_outs):
        xv = row_ins[0][...]
        rs = lax.rsqrt(jnp.mean(xv * xv, axis=-1, keepdims=True) + EPS)
        xh = xv * rs
        acc_outs[0][...] += jnp.sum(dy * xh, axis=0, keepdims=True)
        dxh = dy * vec_ins[0][...]
        dx = row_ins[1][...] + rs * (dxh - xh * jnp.mean(dxh * xh, axis=-1, keepdims=True))
        row_outs[0][...] = dx
        if with_bf16:
            row_outs[1][...] = dx.astype(BF16)

    return _Post(fn, row_ins=[x, dres], vec_ins=[g], row_outs=[(D, F32)] + ([(D, BF16)] if with_bf16 else []),
                 acc_outs=[D], keep_main=False)


def _post_loss(g, target):
    D = g.shape[1]

    def fn(hv, row_ins, vec_ins, row_outs, acc_outs):
        rs = lax.rsqrt(jnp.mean(hv * hv, axis=-1, keepdims=True) + EPS)
        xh = hv * rs
        gv = vec_ins[0][...]
        err = xh * gv - row_ins[0][...]
        acc_outs[0][...] += jnp.sum(err * err, axis=0, keepdims=True)
        dy = err * (1.0 / D)
        acc_outs[1][...] += jnp.sum(dy * xh, axis=0, keepdims=True)
        dxh = dy * gv
        dh = rs * (dxh - xh * jnp.mean(dxh * xh, axis=-1, keepdims=True))
        row_outs[0][...] = dh
        row_outs[1][...] = dh.astype(BF16)

    return _Post(fn, row_ins=[target], vec_ins=[g], row_outs=[(D, F32), (D, BF16)], acc_outs=[D, D], keep_main=False)


def _shift_down(x, k):
    if k == 0:
        return x
    rows = lax.broadcasted_iota(jnp.int32, x.shape, 0)
    return jnp.where(rows >= k, pltpu.roll(x, k, 0), 0.0)


def _shift_up(x, k):
    if k == 0:
        return x
    s = x.shape[0]
    rows = lax.broadcasted_iota(jnp.int32, x.shape, 0)
    return jnp.where(rows < s - k, pltpu.roll(x, s - k, 0), 0.0)


def _conv_fwd(x, w_ref, kw, keep_shifted=False):
    shifted = [_shift_down(x, kw - 1 - i) for i in range(kw - 1)]
    y = x * w_ref[kw - 1:kw, :]
    for i in range(kw - 1):
        y = y + shifted[i] * w_ref[i:i + 1, :]
    return (y, shifted) if keep_shifted else y


def _conv_bwd(x, shifted, dy, w_ref, kw):
    dx = dy * w_ref[kw - 1:kw, :]
    dws = []
    for i in range(kw - 1):
        dx = dx + _shift_up(dy, kw - 1 - i) * w_ref[i:i + 1, :]
        dws.append(jnp.sum(dy * shifted[i], axis=0, keepdims=True))
    dws.append(jnp.sum(dy * x, axis=0, keepdims=True))
    return dx, dws


def ffn_gate_fwd(up_g, up_v, cw_g, cw_v, B, S):
    T, Fd = up_g.shape
    tc = _tile(Fd, 256)

    def body(g_ref, v_ref, wg_ref, wv_ref, o_ref):
        ug = _conv_fwd(g_ref[...], wg_ref, FFN_CONV)
        uv = _conv_fwd(v_ref[...], wv_ref, FFN_CONV)
        o_ref[...] = (ug * _sigmoid(ug) * uv).astype(BF16)

    blk = pl.BlockSpec((S, tc), lambda b, j: (b, j))
    wblk = pl.BlockSpec((FFN_CONV, tc), lambda b, j: (0, j))
    return pl.pallas_call(
        body, name="ffn_gate_fwd", grid=(B, Fd // tc), in_specs=[blk, blk, wblk, wblk], out_specs=blk,
        out_shape=jax.ShapeDtypeStruct((T, Fd), BF16), compiler_params=_params("parallel", "parallel"),
    )(up_g, up_v, cw_g, cw_v)


def ffn_gate_bwd(up_g, up_v, cw_g, cw_v, d_act, B, S):
    T, Fd = up_g.shape
    tc = _tile(Fd, 256)

    def body(g_ref, v_ref, wg_ref, wv_ref, da_ref, dg_ref, dv_ref, dwg_ref, dwv_ref):
        @pl.when(pl.program_id(1) == 0)
        def _():
            dwg_ref[...] = jnp.zeros_like(dwg_ref)
            dwv_ref[...] = jnp.zeros_like(dwv_ref)

        xg, xv = g_ref[...], v_ref[...]
        ug, sh_g = _conv_fwd(xg, wg_ref, FFN_CONV, keep_shifted=True)
        uv, sh_v = _conv_fwd(xv, wv_ref, FFN_CONV, keep_shifted=True)
        da = da_ref[...]
        sg = _sigmoid(ug)
        d_ug = da * uv * (sg + ug * sg * (1.0 - sg))
        d_uv = da * ug * sg
        dxg, dwg = _conv_bwd(xg, sh_g, d_ug, wg_ref, FFN_CONV)
        dxv, dwv = _conv_bwd(xv, sh_v, d_uv, wv_ref, FFN_CONV)
        dg_ref[...] = dxg.astype(BF16)
        dv_ref[...] = dxv.astype(BF16)
        for i in range(FFN_CONV):
            dwg_ref[i:i + 1, :] += dwg[i]
            dwv_ref[i:i + 1, :] += dwv[i]

    blk = pl.BlockSpec((S, tc), lambda j, b: (b, j))
    wblk = pl.BlockSpec((FFN_CONV, tc), lambda j, b: (0, j))
    return pl.pallas_call(
        body, name="ffn_gate_bwd", grid=(Fd // tc, B), in_specs=[blk, blk, wblk, wblk, blk],
        out_specs=[blk, blk, wblk, wblk],
        out_shape=[jax.ShapeDtypeStruct((T, Fd), BF16), jax.ShapeDtypeStruct((T, Fd), BF16),
                   jax.ShapeDtypeStruct((FFN_CONV, Fd), F32), jax.ShapeDtypeStruct((FFN_CONV, Fd), F32)],
        compiler_params=_params("parallel", "arbitrary"),
    )(up_g, up_v, cw_g, cw_v, d_act)


def _post_merge(p_gates, bf_):
    D = bf_.shape[1]

    def fn(bg, row_ins, vec_ins, row_outs, acc_outs):
        gf_ref, gg_ref, bf_ref = row_ins
        row_outs[0][...] = (_sigmoid(gf_ref[...]) * bf_ref[...] + _sigmoid(gg_ref[...]) * bg).astype(BF16)

    return _Post(fn, row_ins=[(p_gates, D, 0), (p_gates, D, 1), bf_], row_outs=[(D, BF16)])


def _post_merge_bwd(p_gates, bf_, bg_):
    D = bf_.shape[1]

    def fn(d, row_ins, vec_ins, row_outs, acc_outs):
        gf_ref, gg_ref, bf_ref, bg_ref = row_ins
        sf, sg = _sigmoid(gf_ref[...]), _sigmoid(gg_ref[...])
        row_outs[0][...] = (d * sf).astype(BF16)
        row_outs[1][...] = (d * sg).astype(BF16)
        row_outs[2][:, 0:D] = (d * bf_ref[...] * sf * (1.0 - sf)).astype(BF16)
        row_outs[2][:, D:2 * D] = (d * bg_ref[...] * sg * (1.0 - sg)).astype(BF16)

    return _Post(fn, row_ins=[(p_gates, D, 0), (p_gates, D, 1), bf_, bg_],
                 row_outs=[(D, BF16), (D, BF16), (2 * D, BF16)], keep_main=False)


def fox_fwd(p_fox, c_col, c_row, B, S, H, rider=None):
    T = B * S
    t = _tile(S, ATT_TILE)
    nq = S // t
    scale = HEAD ** -0.5

    def body(q_ref, k_ref, v_ref, cq_ref, cr_ref, o_ref, o16_ref, lse_ref):
        i = pl.program_id(2)
        q = q_ref[...]
        cq = cq_ref[...]
        row = lax.broadcasted_iota(jnp.int32, (t, t), 0)
        col = lax.broadcasted_iota(jnp.int32, (t, t), 1)

        def step(j, carry, diagonal):
            m, l, acc = carry
            off = pl.multiple_of(j * t, t)
            k = k_ref[pl.ds(off, t), :]
            v = v_ref[pl.ds(off, t), :]
            s = lax.dot_general(q, k, _NT, preferred_element_type=F32) * scale - cr_ref[:, pl.ds(off, t)]
            if diagonal:
                s = jnp.where(col <= row, s, NEG)
            m_new = jnp.maximum(m, jnp.max(s, axis=-1, keepdims=True))
            alpha = jnp.exp(m - m_new)
            p = jnp.exp(s - m_new)
            l = alpha * l + jnp.sum(p, axis=-1, keepdims=True)
            acc = alpha * acc + lax.dot_general(p.astype(BF16), v, _NN, preferred_element_type=F32)
            return m_new, l, acc

        m0 = jnp.full((t, 1), NEG, F32)
        below = lax.fori_loop(0, i, functools.partial(step, diagonal=False),
                              (m0, jnp.zeros((t, 1), F32), jnp.zeros((t, HEAD), F32)))
        m, l, acc = step(i, below, diagonal=True)
        o = acc / l
        o_ref[...] = o
        o16_ref[...] = o.astype(BF16)
        lse_ref[...] = cq + m + jnp.log(l)

    return _hosted_call(
        body, rider, name="fox_fwd", grid=(B, H, nq),
        in_specs=[pl.BlockSpec((t, HEAD), lambda b, h, i: (b * nq + i, 3 * h)),
                  pl.BlockSpec((S, HEAD), lambda b, h, i: (b, 3 * h + 1)),
                  pl.BlockSpec((S, HEAD), lambda b, h, i: (b, 3 * h + 2)),
                  pl.BlockSpec((None, None, t, 1), lambda b, h, i: (b, h, i, 0)),
                  pl.BlockSpec((None, None, 1, S), lambda b, h, i: (b, h, 0, 0))],
        out_specs=[pl.BlockSpec((t, HEAD), lambda b, h, i: (b * nq + i, h)),
                   pl.BlockSpec((t, HEAD), lambda b, h, i: (b * nq + i, h)),
                   pl.BlockSpec((None, None, t, 1), lambda b, h, i: (b, h, i, 0))],
        out_shape=[jax.ShapeDtypeStruct((T, H * HEAD), F32), jax.ShapeDtypeStruct((T, H * HEAD), BF16),
                   jax.ShapeDtypeStruct((B, H, S, 1), F32)],
        scratch_shapes=[], semantics=("parallel", "parallel", "arbitrary"),
    )(p_fox, p_fox, p_fox, c_col, c_row)


def fox_bwd(p_fox, c_col, c_row, o, lse, do, B, S, H, rider=None):
    T = B * S
    t = _tile(S, ATT_TILE)
    n = S // t
    scale = HEAD ** -0.5

    def body(q_ref, k_ref, v_ref, cq_ref, cr_ref, o_ref, lse_ref, do_ref, dqkv_ref, dcq_ref, dcr_ref,
             dq_acc, delta_s, lse_s):
        row = lax.broadcasted_iota(jnp.int32, (t, t), 0)
        col = lax.broadcasted_iota(jnp.int32, (t, t), 1)

        def prep(i, c):
            rows = pl.ds(pl.multiple_of(i * t, t), t)
            delta_s[rows, :] = jnp.sum(do_ref[rows, :] * o_ref[rows, :], axis=-1, keepdims=True)
            lse_s[rows, :] = lse_ref[rows, :] - cq_ref[rows, :]
            dq_acc[rows, :] = jnp.zeros((t, HEAD), F32)
            dcq_ref[rows, :] = jnp.zeros((t, 1), F32)
            return c

        lax.fori_loop(0, n, prep, 0)

        def kv_step(j, c):
            joff = pl.multiple_of(j * t, t)
            k = k_ref[pl.ds(joff, t), :]
            v = v_ref[pl.ds(joff, t), :]
            crj = cr_ref[:, pl.ds(joff, t)]

            def q_step(i, carry, diagonal):
                dk, dv, dc = carry
                rows = pl.ds(pl.multiple_of(i * t, t), t)
                q = q_ref[rows, :]
                dob = do_ref[rows, :].astype(BF16)
                s = lax.dot_general(q, k, _NT, preferred_element_type=F32) * scale - crj
                if diagonal:
                    s = jnp.where(col <= row, s, NEG)
                p = jnp.exp(s - lse_s[rows, :])
                dp = lax.dot_general(dob, v, _NT, preferred_element_type=F32)
                ds = p * (dp - delta_s[rows, :])
                dsb = ds.astype(BF16)
                dv = dv + lax.dot_general(p.astype(BF16), dob, _TN, preferred_element_type=F32)
                dk = dk + lax.dot_general(dsb, q, _TN, preferred_element_type=F32)
                dq_acc[rows, :] += lax.dot_general(dsb, k, _NN, preferred_element_type=F32) * scale
                dc = dc + jnp.sum(ds, axis=0, keepdims=True)
                dcq_ref[rows, :] += jnp.sum(ds, axis=-1, keepdims=True)
                return dk, dv, dc

            z = jnp.zeros((t, HEAD), F32)
            on_diagonal = q_step(j, (z, z, jnp.zeros((1, t), F32)), diagonal=True)
            dk, dv, dc = lax.fori_loop(j + 1, n, functools.partial(q_step, diagonal=False), on_diagonal)
            dqkv_ref[pl.ds(joff, t), HEAD:2 * HEAD] = (dk * scale).astype(BF16)
            dqkv_ref[pl.ds(joff, t), 2 * HEAD:3 * HEAD] = dv.astype(BF16)
            dcr_ref[:, pl.ds(joff, t)] = -dc
            return c

        lax.fori_loop(0, n, kv_step, 0)
        dqkv_ref[:, 0:HEAD] = dq_acc[...].astype(BF16)

    col_spec = pl.BlockSpec((None, None, S, 1), lambda b, h: (b, h, 0, 0))
    row_spec = pl.BlockSpec((None, None, 1, S), lambda b, h: (b, h, 0, 0))
    head = pl.BlockSpec((S, HEAD), lambda b, h: (b, h))
    return _hosted_call(
        body, rider, name="fox_bwd", grid=(B, H),
        in_specs=[pl.BlockSpec((S, HEAD), lambda b, h: (b, 3 * h)),
                  pl.BlockSpec((S, HEAD), lambda b, h: (b, 3 * h + 1)),
                  pl.BlockSpec((S, HEAD), lambda b, h: (b, 3 * h + 2)),
                  col_spec, row_spec, head, col_spec, head],
        out_specs=[pl.BlockSpec((S, 3 * HEAD), lambda b, h: (b, h)), col_spec, row_spec],
        out_shape=[jax.ShapeDtypeStruct((T, 3 * H * HEAD), BF16), jax.ShapeDtypeStruct((B, H, S, 1), F32),
                   jax.ShapeDtypeStruct((B, H, 1, S), F32)],
        scratch_shapes=[pltpu.VMEM((S, HEAD), F32), pltpu.VMEM((S, 1), F32), pltpu.VMEM((S, 1), F32)],
        semantics=("parallel", "parallel"),
    )(p_fox, p_fox, p_fox, c_col, c_row, o, lse, do)


def _small_fn(x, b0, b1, H):
    S = x.shape[0]
    lane = lax.broadcasted_iota(jnp.int32, x.shape, 1)
    z = x + b0
    tail = jnp.log1p(jnp.exp(-jnp.abs(z)))
    softplus = jnp.maximum(z, 0.0) + tail
    logsig = -(jnp.maximum(-z, 0.0) + tail)
    g = -jnp.exp(b1) * softplus
    pre = jnp.where(lane < H, logsig, jnp.where(lane < 2 * H, g, 0.0))
    bl = _tile(S, 256, CHUNK)
    r = lax.broadcasted_iota(jnp.int32, (bl, bl), 0)
    c = lax.broadcasted_iota(jnp.int32, (bl, bl), 1)
    tri = (r >= c).astype(F32)
    tri_chunk = jnp.where((r >= c) & (jnp.right_shift(r, 6) == jnp.right_shift(c, 6)), 1.0, 0.0)
    carry = jnp.zeros((1, x.shape[1]), F32)
    parts = []
    for i in range(S // bl):
        blk = pre[i * bl:(i + 1) * bl, :]
        full = lax.dot_general(tri, blk, _NN, precision=EXACT, preferred_element_type=F32) + carry
        chunked = lax.dot_general(tri_chunk, blk, _NN, precision=EXACT, preferred_element_type=F32)
        parts.append(jnp.where(lane[:bl] < H, full, chunked))
        carry = carry + jnp.sum(blk, axis=0, keepdims=True)
    cum = parts[0] if len(parts) == 1 else jnp.concatenate(parts, axis=0)
    return jnp.where(lane < 2 * H, cum, jnp.where(lane < 3 * H, _sigmoid(x), 0.0))


def small_fwd(p_small, prm, B, S, H):
    T = B * S

    def body(x_ref, p_ref, o_ref):
        o_ref[...] = _small_fn(x_ref[...], p_ref[0:1, :], p_ref[1:2, :], H)

    blk = pl.BlockSpec((S, 128), lambda b: (b, 0))
    return pl.pallas_call(
        body, name="small_fwd", grid=(B,), in_specs=[blk, pl.BlockSpec((8, 128), lambda b: (0, 0))], out_specs=blk,
        out_shape=jax.ShapeDtypeStruct((T, 128), F32), compiler_params=_params("parallel"),
    )(p_small, prm)


def small_bwd(p_small, prm, d_out, B, S, H):
    T = B * S

    def body(x_ref, p_ref, d_ref, dx_ref, dp_ref):
        @pl.when(pl.program_id(0) == 0)
        def _():
            dp_ref[...] = jnp.zeros_like(dp_ref)

        _, vjp = jax.vjp(functools.partial(_small_fn, H=H), x_ref[...], p_ref[0:1, :], p_ref[1:2, :])
        dx, db0, db1 = vjp(d_ref[...])
        dx_ref[...] = dx.astype(BF16)
        dp_ref[0:1, :] += db0
        dp_ref[1:2, :] += db1

    blk = pl.BlockSpec((S, 128), lambda b: (b, 0))
    pblk = pl.BlockSpec((8, 128), lambda b: (0, 0))
    return pl.pallas_call(
        body, name="small_bwd", grid=(B,), in_specs=[blk, pblk, blk], out_specs=[blk, pblk],
        out_shape=[jax.ShapeDtypeStruct((T, 128), BF16), jax.ShapeDtypeStruct((8, 128), F32)],
        compiler_params=_params("arbitrary"),
    )(p_small, prm, d_out)


def gdn_prep_fwd(p_gqkv, cw, B, S, H):
    T = B * S

    def body(x_ref, w_ref, o_ref):
        for part in range(3):
            cols = slice(part * HEAD, (part + 1) * HEAD)
            y = _conv_fwd(x_ref[:, cols], w_ref.at[:, cols], GDN_CONV)
            a = y * _sigmoid(y)
            if part < 2:
                a = a * lax.rsqrt(jnp.sum(a * a, axis=-1, keepdims=True) + EPS)
            o_ref[:, cols] = a

    blk = pl.BlockSpec((S, 3 * HEAD), lambda b, h: (b, h))
    wblk = pl.BlockSpec((GDN_CONV, 3 * HEAD), lambda b, h: (0, h))
    return pl.pallas_call(
        body, name="gdn_prep_fwd", grid=(B, H), in_specs=[blk, wblk], out_specs=blk,
        out_shape=jax.ShapeDtypeStruct((T, 3 * H * HEAD), F32), compiler_params=_params("parallel", "parallel"),
    )(p_gqkv, cw)


def gdn_prep_bwd(p_gqkv, cw, d_out, B, S, H):
    T = B * S

    def body(x_ref, w_ref, d_ref, dx_ref, dw_ref):
        @pl.when(pl.program_id(1) == 0)
        def _():
            dw_ref[...] = jnp.zeros_like(dw_ref)

        for part in range(3):
            cols = slice(part * HEAD, (part + 1) * HEAD)
            x = x_ref[:, cols]
            y, shifted = _conv_fwd(x, w_ref.at[:, cols], GDN_CONV, keep_shifted=True)
            sg = _sigmoid(y)
            a = y * sg
            da = d_ref[:, cols]
            if part < 2:
                rs = lax.rsqrt(jnp.sum(a * a, axis=-1, keepdims=True) + EPS)
                out = a * rs
                da = rs * (da - out * jnp.sum(da * out, axis=-1, keepdims=True))
            dy = da * (sg + y * sg * (1.0 - sg))
            dx, dws = _conv_bwd(x, shifted, dy, w_ref.at[:, cols], GDN_CONV)
            dx_ref[:, cols] = dx.astype(BF16)
            for i in range(GDN_CONV):
                dw_ref[i:i + 1, cols] += dws[i]

    blk = pl.BlockSpec((S, 3 * HEAD), lambda h, b: (b, h))
    wblk = pl.BlockSpec((GDN_CONV, 3 * HEAD), lambda h, b: (0, h))
    return pl.pallas_call(
        body, name="gdn_prep_bwd", grid=(H, B), in_specs=[blk, wblk, blk], out_specs=[blk, wblk],
        out_shape=[jax.ShapeDtypeStruct((T, 3 * H * HEAD), BF16), jax.ShapeDtypeStruct((GDN_CONV, 3 * H * HEAD), F32)],
        compiler_params=_params("parallel", "arbitrary"),
    )(p_gqkv, cw, d_out)


@jax.custom_vjp
def _given_inverse(a, t):
    return t


def _given_inverse_fwd(a, t):
    return t, t


def _given_inverse_bwd(t, g):
    x = _dg(t, g, _TN, True)
    return -_dg(x, t, _NT, True), jnp.zeros_like(t)


_given_inverse.defvjp(_given_inverse_fwd, _given_inverse_bwd)


def _to_col(row):
    n = row.shape[1]
    r = lax.broadcasted_iota(jnp.int32, (n, n), 0)
    c = lax.broadcasted_iota(jnp.int32, (n, n), 1)
    return jnp.sum(jnp.where(r == c, row, 0.0), axis=1, keepdims=True)


def _intra_fn(k, v, beta_r, gcr, ops, t_known=None):
    n = len(k)
    m = k[0].shape[0]
    r = lax.broadcasted_iota(jnp.int32, (m, m), 0)
    c = lax.broadcasted_iota(jnp.int32, (m, m), 1)
    below = (r > c) & (jnp.right_shift(r, 6) == jnp.right_shift(c, 6))
    beta = [_to_col(beta_r[i]) for i in range(n)]
    gcc = [_to_col(gcr[i]) for i in range(n)]
    decay = [jnp.exp(jnp.where(below, gcc[i] - gcr[i], NEG)) for i in range(n)]
    kb = [k[i] * beta[i] for i in range(n)]
    a = [ops.nt(kb[i], k[i]) * decay[i] for i in range(n)]
    if t_known is None:
        p = [-a[i] for i in range(n)]
        tm = [jnp.where(r == c, 1.0, 0.0) + p[i] for i in range(n)]
        for _ in range(5):
            p = [ops.nn(p[i], p[i], hi=True) for i in range(n)]
            tm = [tm[i] + ops.nn(tm[i], p[i], hi=True) for i in range(n)]
    else:
        tm = [_given_inverse(a[i], t_known[i]) for i in range(n)]
    both = [ops.nn(tm[i], jnp.concatenate([v[i] * beta[i], kb[i] * jnp.exp(gcc[i])], axis=1), hi=True) for i in range(n)]
    u_hat = [both[i][:, :HEAD] for i in range(n)]
    w = [both[i][:, HEAD:] for i in range(n)]
    return tuple(u_hat), tuple(w), tuple(tm)


INTRA_NB = 32
PAIR = 1


def gdn_intra_fwd(qkvn, betar5, gcr5, B, S, H, rider=None):
    T = B * S
    UNIT = PAIR * CHUNK
    N = S // UNIT
    nb = min(INTRA_NB // PAIR, N)
    rows = nb * UNIT
    ns = N // nb

    def body(k_ref, v_ref, b_ref, gr_ref, uh_ref, w_ref, t_ref):
        sls = [slice(ci * UNIT, (ci + 1) * UNIT) for ci in range(nb)]
        u_hat, w, tm = _intra_fn(tuple(k_ref[sl, :] for sl in sls), tuple(v_ref[sl, :] for sl in sls),
                                 tuple(b_ref[ci] for ci in range(nb)), tuple(gr_ref[ci] for ci in range(nb)), _RawOps)
        for ci, sl in enumerate(sls):
            uh_ref[sl, :] = u_hat[ci]
            w_ref[sl, :] = w[ci]
            t_ref[ci] = tm[ci]

    rowspec = pl.BlockSpec((None, None, nb, 1, UNIT), lambda b, h, i: (b, h, i, 0, 0))
    sqspec = pl.BlockSpec((None, None, nb, UNIT, UNIT), lambda b, h, i: (b, h, i, 0, 0))
    out = pl.BlockSpec((rows, HEAD), lambda b, h, i: (b * ns + i, h))
    return _hosted_call(
        body, rider, name="gdn_intra_fwd", grid=(B, H, ns),
        in_specs=[pl.BlockSpec((rows, HEAD), lambda b, h, i: (b * ns + i, 3 * h + 1)),
                  pl.BlockSpec((rows, HEAD), lambda b, h, i: (b * ns + i, 3 * h + 2)),
                  rowspec, rowspec],
        out_specs=[out, out, sqspec],
        out_shape=[jax.ShapeDtypeStruct((T, H * HEAD), F32), jax.ShapeDtypeStruct((T, H * HEAD), F32),
                   jax.ShapeDtypeStruct((B, H, N, UNIT, UNIT), F32)],
        scratch_shapes=[], semantics=("parallel", "parallel", "parallel"),
    )(qkvn, qkvn, betar5, gcr5)


def gdn_intra_bwd(qkvn, betar5, gcr5, t_inv, d_uh, d_w, dq_in, dk_in, B, S, H):
    T = B * S
    UNIT = PAIR * CHUNK
    N = S // UNIT
    nb = min(INTRA_NB // PAIR, N)
    rows = nb * UNIT
    ns = N // nb

    def body(k_ref, v_ref, b_ref, gr_ref, t_ref, duh_ref, dw_ref, dq_ref, dk_ref, o_ref, db_ref, dgr_ref):
        sls = [slice(ci * UNIT, (ci + 1) * UNIT) for ci in range(nb)]
        chunks = range(nb)
        _, vjp = jax.vjp(
            functools.partial(_intra_fn, ops=_DiffOps, t_known=tuple(t_ref[ci] for ci in chunks)),
            tuple(k_ref[sl, :] for sl in sls), tuple(v_ref[sl, :] for sl in sls), tuple(b_ref[ci] for ci in chunks),
            tuple(gr_ref[ci] for ci in chunks))
        zero = jnp.zeros((UNIT, UNIT), F32)
        dk, dv, db, dgr = vjp((tuple(duh_ref[sl, :] for sl in sls), tuple(dw_ref[sl, :] for sl in sls),
                               tuple(zero for _ in chunks)))
        for ci, sl in enumerate(sls):
            o_ref[sl, 0:HEAD] = dq_ref[sl, :]
            o_ref[sl, HEAD:2 * HEAD] = dk[ci] + dk_ref[sl, :]
            o_ref[sl, 2 * HEAD:3 * HEAD] = dv[ci]
            db_ref[ci] = db[ci]
            dgr_ref[ci] = dgr[ci]

    rowspec = pl.BlockSpec((None, None, nb, 1, UNIT), lambda b, h, i: (b, h, i, 0, 0))
    sqspec = pl.BlockSpec((None, None, nb, UNIT, UNIT), lambda b, h, i: (b, h, i, 0, 0))
    head = pl.BlockSpec((rows, HEAD), lambda b, h, i: (b * ns + i, h))
    return pl.pallas_call(
        body, name="gdn_intra_bwd", grid=(B, H, ns),
        in_specs=[pl.BlockSpec((rows, HEAD), lambda b, h, i: (b * ns + i, 3 * h + 1)),
                  pl.BlockSpec((rows, HEAD), lambda b, h, i: (b * ns + i, 3 * h + 2)),
                  rowspec, rowspec, sqspec, head, head, head, head],
        out_specs=[pl.BlockSpec((rows, 3 * HEAD), lambda b, h, i: (b * ns + i, h)), rowspec, rowspec],
        out_shape=[jax.ShapeDtypeStruct((T, 3 * H * HEAD), F32),
                   jax.ShapeDtypeStruct((B, H, N, 1, UNIT), F32), jax.ShapeDtypeStruct((B, H, N, 1, UNIT), F32)],
        compiler_params=_params("parallel", "parallel", "parallel"),
    )(qkvn, qkvn, betar5, gcr5, t_inv, d_uh, d_w, dq_in, dk_in)


def _inter_fn(q, k, u_hat, w, gcr, state, ops):
    n = len(q)
    r = lax.broadcasted_iota(jnp.int32, (CHUNK, CHUNK), 0)
    c = lax.broadcasted_iota(jnp.int32, (CHUNK, CHUNK), 1)
    last = lax.broadcasted_iota(jnp.int32, (1, CHUNK), 1) == CHUNK - 1
    gcc = [_to_col(gcr[i]) for i in range(n)]
    gl = [jnp.sum(jnp.where(last, gcr[i], 0.0), axis=1, keepdims=True) for i in range(n)]
    decay = [jnp.exp(jnp.where(r >= c, gcc[i] - gcr[i], NEG)) for i in range(n)]
    qs = [q[i] * (HEAD ** -0.5) for i in range(n)]
    ws = [ops.nn(w[i], state[i]) for i in range(n)]
    qst = [ops.nn(qs[i] * jnp.exp(gcc[i]), state[i]) for i in range(n)]
    attn = [ops.nt(qs[i], k[i]) * decay[i] for i in range(n)]
    u = [u_hat[i] - ws[i] for i in range(n)]
    o = [qst[i] + ops.nn(attn[i], u[i]) for i in range(n)]
    kdu = [ops.tn(k[i] * jnp.exp(gl[i] - gcc[i]), u[i]) for i in range(n)]
    new_state = [state[i] * jnp.exp(gl[i]) + kdu[i] for i in range(n)]
    return tuple(o), tuple(new_state)


INTER_HEADS = 8
INTER_ROWS = 512
INTER_ROWS_BWD = 256


def _inter_heads(H):
    return INTER_HEADS if H % INTER_HEADS == 0 else (4 if H % 4 == 0 else 1)


def _inter_specs(ts, ns, hp, backward):
    at = (lambda s: ns - 1 - s) if backward else (lambda s: s)
    nc = ts // CHUNK
    qk = []
    for hh in range(hp):
        qk.append(pl.BlockSpec((ts, HEAD), lambda b, g, s, hh=hh: (b * ns + at(s), 3 * (hp * g + hh))))
        qk.append(pl.BlockSpec((ts, HEAD), lambda b, g, s, hh=hh: (b * ns + at(s), 3 * (hp * g + hh) + 1)))
    heads = pl.BlockSpec((ts, hp * HEAD), lambda b, g, s: (b * ns + at(s), g))
    rowspec = pl.BlockSpec((None, hp, nc, 1, CHUNK), lambda b, g, s: (b, g, at(s), 0, 0))
    stspec = pl.BlockSpec((None, hp, nc, HEAD, HEAD), lambda b, g, s: (b, g, at(s), 0, 0))
    return qk, heads, rowspec, stspec


def gdn_inter_fwd(qkvn, u_hat, w, gcr5, p_gz, gnorm, B, S, H):
    T = B * S
    N = S // CHUNK
    hp = _inter_heads(H)
    hs = range(hp)
    ts = _tile(S, INTER_ROWS, CHUNK)
    ns, nc = S // ts, ts // CHUNK

    def body(*refs):
        qk_refs, (uh_ref, w_ref, gr_ref, z_ref, gn_ref, o_ref, st_ref, y_ref, s_scr) = refs[:2 * hp], refs[2 * hp:]

        @pl.when(pl.program_id(2) == 0)
        def _():
            s_scr[...] = jnp.zeros_like(s_scr)

        gn = gn_ref[...]

        def step(n, c):
            rows = pl.ds(pl.multiple_of(n * CHUNK, CHUNK), CHUNK)
            st = tuple(s_scr[hh] for hh in hs)
            for hh in hs:
                st_ref[hh, n] = st[hh]
            o, new = _inter_fn(tuple(qk_refs[2 * hh][rows, :] for hh in hs), tuple(qk_refs[2 * hh + 1][rows, :] for hh in hs),
                               tuple(uh_ref[rows, hh * HEAD:(hh + 1) * HEAD] for hh in hs),
                               tuple(w_ref[rows, hh * HEAD:(hh + 1) * HEAD] for hh in hs),
                               tuple(gr_ref[hh, n] for hh in hs), st, _RawOps)
            for hh in hs:
                cols = slice(hh * HEAD, (hh + 1) * HEAD)
                o_ref[rows, cols] = o[hh]
                s_scr[hh] = new[hh]
                z = z_ref[rows, cols]
                r = lax.rsqrt(jnp.mean(o[hh] * o[hh], axis=-1, keepdims=True) + EPS)
                y_ref[rows, cols] = (o[hh] * r * gn * z * _sigmoid(z)).astype(BF16)
            return c

        lax.fori_loop(0, nc, step, 0)

    qk, heads, rowspec, stspec = _inter_specs(ts, ns, hp, backward=False)
    return pl.pallas_call(
        body, name="gdn_inter_fwd", grid=(B, H // hp, ns),
        in_specs=qk + [heads, heads, rowspec, heads, pl.BlockSpec((1, HEAD), lambda b, g, s: (0, 0))],
        out_specs=[heads, stspec, heads],
        out_shape=[jax.ShapeDtypeStruct((T, H * HEAD), F32), jax.ShapeDtypeStruct((B, H, N, HEAD, HEAD), F32),
                   jax.ShapeDtypeStruct((T, H * HEAD), BF16)],
        scratch_shapes=[pltpu.VMEM((hp, HEAD, HEAD), F32)],
        compiler_params=_params("parallel", "parallel", "arbitrary"),
    )(*([qkvn] * (2 * hp)), u_hat, w, gcr5, p_gz, gnorm)


def gdn_inter_bwd(qkvn, u_hat, w, gcr5, states, o, p_gz, gnorm, d_y, B, S, H, rider=None):
    T = B * S
    N = S // CHUNK
    hp = _inter_heads(H)
    hs = range(hp)
    ts = _tile(S, INTER_ROWS_BWD, CHUNK)
    ns, nc = S // ts, ts // CHUNK

    def body(*refs):
        qk_refs = refs[:2 * hp]
        (uh_ref, w_ref, gr_ref, st_ref, o_ref, z_ref, gn_ref, dy_ref,
         dq_ref, dk_ref, duh_ref, dw_ref, dgr_ref, dz_ref, dgn_ref, ds_scr) = refs[2 * hp:]

        @pl.when(pl.program_id(2) == 0)
        def _():
            ds_scr[...] = jnp.zeros_like(ds_scr)
            dgn_ref[...] = jnp.zeros_like(dgn_ref)

        cols = [slice(hh * HEAD, (hh + 1) * HEAD) for hh in hs]
        gn = gn_ref[...]

        def through_norm(rows, hh):
            ov, z, d = o_ref[rows, cols[hh]], z_ref[rows, cols[hh]], dy_ref[rows, cols[hh]]
            r = lax.rsqrt(jnp.mean(ov * ov, axis=-1, keepdims=True) + EPS)
            xh = ov * r
            sg = _sigmoid(z)
            d_n = d * (z * sg)
            dz_ref[rows, cols[hh]] = (d * xh * gn * (sg + z * sg * (1.0 - sg))).astype(BF16)
            dgn_ref[0:1, :] += jnp.sum(d_n * xh, axis=0, keepdims=True)
            dxh = d_n * gn
            return r * (dxh - xh * jnp.mean(dxh * xh, axis=-1, keepdims=True))

        def step(i, c):
            n = nc - 1 - i
            rows = pl.ds(pl.multiple_of(n * CHUNK, CHUNK), CHUNK)
            _, vjp = jax.vjp(functools.partial(_inter_fn, ops=_DiffOps),
                             tuple(qk_refs[2 * hh][rows, :] for hh in hs), tuple(qk_refs[2 * hh + 1][rows, :] for hh in hs),
                             tuple(uh_ref[rows, cols[hh]] for hh in hs), tuple(w_ref[rows, cols[hh]] for hh in hs),
                             tuple(gr_ref[hh, n] for hh in hs), tuple(st_ref[hh, n] for hh in hs))
            dq, dk, duh, dw, dgr, ds = vjp((tuple(through_norm(rows, hh) for hh in hs), tuple(ds_scr[hh] for hh in hs)))
            for hh in hs:
                dq_ref[rows, cols[hh]] = dq[hh]
                dk_ref[rows, cols[hh]] = dk[hh]
                duh_ref[rows, cols[hh]] = duh[hh]
                dw_ref[rows, cols[hh]] = dw[hh]
                dgr_ref[hh, n] = dgr[hh]
                ds_scr[hh] = ds[hh]
            return c

        lax.fori_loop(0, nc, step, 0)

    qk, heads, rowspec, stspec = _inter_specs(ts, ns, hp, backward=True)
    hshape = jax.ShapeDtypeStruct((T, H * HEAD), F32)
    return _hosted_call(
        body, rider, name="gdn_inter_bwd", grid=(B, H // hp, ns),
        in_specs=qk + [heads, heads, rowspec, stspec, heads, heads, pl.BlockSpec((1, HEAD), lambda b, g, s: (0, 0)), heads],
        out_specs=[heads, heads, heads, heads, rowspec, heads,
                   pl.BlockSpec((None, None, 8, HEAD), lambda b, g, s: (b, g, 0, 0))],
        out_shape=[hshape, hshape, hshape, hshape, jax.ShapeDtypeStruct((B, H, N, 1, CHUNK), F32),
                   jax.ShapeDtypeStruct((T, H * HEAD), BF16), jax.ShapeDtypeStruct((B, H // hp, 8, HEAD), F32)],
        scratch_shapes=[pltpu.VMEM((hp, HEAD, HEAD), F32)], semantics=("parallel", "parallel", "arbitrary"),
    )(*([qkvn] * (2 * hp)), u_hat, w, gcr5, states, o, p_gz, gnorm, d_y)


def adamw(w, g, m, v, name):
    shape = w.shape
    lead = (None,) * (w.ndim - 2)
    zeros = (0,) * (w.ndim - 2)
    R, C = shape[-2:]
    g2 = g.reshape(R, C)
    tr, tc = _tile(R, 128, 8), C
    if tr % 8 and R > 8:
        tr, tc = R, _tile(C, 128)

    def body(w_ref, g_ref, m_ref, v_ref, d_ref, nm_ref, nv_ref):
        gv = g_ref[...]
        nm = ADAM_B1 * m_ref[...] + (1.0 - ADAM_B1) * gv
        nv = ADAM_B2 * v_ref[...] + (1.0 - ADAM_B2) * (gv * gv)
        m_hat = nm / (1.0 - ADAM_B1 ** ADAM_STEP)
        v_hat = nv / (1.0 - ADAM_B2 ** ADAM_STEP)
        d_ref[...] = -ADAM_LR * (m_hat / (jnp.sqrt(v_hat) + ADAM_EPS) + ADAM_WD * w_ref[...])
        nm_ref[...] = nm
        nv_ref[...] = nv

    blk = pl.BlockSpec(lead + (tr, tc), lambda i, j: zeros + (i, j))
    gblk = pl.BlockSpec((tr, tc), lambda i, j: (i, j))
    sh = jax.ShapeDtypeStruct(shape, F32)
    return pl.pallas_call(
        body, name=name, grid=(R // tr, C // tc), in_specs=[blk, gblk, blk, blk], out_specs=[blk] * 3, out_shape=[sh] * 3,
        compiler_params=_params("parallel", "parallel"),
    )(w, g2, m, v)


def _place():
    x, y, c = lax.axis_index("x"), lax.axis_index("y"), lax.axis_index("c")
    chips = [(1 - x, y), (x, 1 - y), (1 - x, 1 - y)]
    return x, y, c, chips


_HBM = pl.BlockSpec(memory_space=pltpu.HBM)


def allgather_weights(packs):
    n = len(packs)

    def body(*refs):
        in_refs, out_refs, (send_sems, recv_sems) = refs[:n], refs[n:2 * n], refs[2 * n:]
        x, y, c, chips = _place()
        me_s = 2 * x + y
        me, sibling = (x, y, c), (x, y, 1 - c)
        shards = [2 * chip[0] + chip[1] for chip in chips]

        def copy(a, k, shard, half, to, src=None):
            dst = out_refs[a].at[shard, half]
            return pltpu.make_async_remote_copy(src_ref=dst if src is None else src, dst_ref=dst,
                                                send_sem=send_sems.at[6 * a + k], recv_sem=recv_sems.at[6 * a + k],
                                                device_id=to, device_id_type=MESH)

        first = [copy(a, j, me_s, c, (*chip, c), src=in_refs[a].at[c]) for a in range(n) for j, chip in enumerate(chips)]
        for cp in first:
            cp.start()
        passed = []
        for a in range(n):
            for j in range(3):
                copy(a, j, shards[j], c, me).wait_recv()
                passed.append(copy(a, 3 + j, shards[j], c, sibling))
                passed[-1].start()
        for a in range(n):
            for j in range(3):
                copy(a, 3 + j, shards[j], 1 - c, me).wait_recv()
        for cp in first + passed:
            cp.wait_send()

    return pl.pallas_call(
        body, name="allgather_weights", in_specs=[_HBM] * n, out_specs=[_HBM] * n,
        out_shape=[jax.ShapeDtypeStruct((N_CHIP,) + p.shape, p.dtype) for p in packs],
        scratch_shapes=[pltpu.SemaphoreType.DMA((6 * n,)), pltpu.SemaphoreType.DMA((6 * n,))],
    )(*packs)


class _Rider:
    def __init__(self, inputs, out_shapes, n_sems, sends, recvs, aliases=None):
        self.inputs, self.out_shapes, self.n_sems = list(inputs), list(out_shapes), n_sems
        self.sends, self.recvs, self.aliases = sends, recvs, aliases or {}

    def start(self, *refs):
        for cp in self.sends(*refs):
            cp.start()

    def wait(self, *refs):
        for cp in self.recvs(*refs):
            cp.wait_recv()
        for cp in self.sends(*refs):
            cp.wait_send()


def _remote(src, dst, send_sems, recv_sems, k, to):
    return pltpu.make_async_remote_copy(src_ref=src, dst_ref=dst, send_sem=send_sems.at[k], recv_sem=recv_sems.at[k],
                                        device_id=to, device_id_type=MESH)


def _run_alone(rider, name):
    ri = len(rider.inputs)

    def body(*refs):
        ins, outs, (send_sems, recv_sems) = refs[:ri], refs[ri:-2], refs[-2:]
        rider.start(ins, outs, send_sems, recv_sems)
        rider.wait(ins, outs, send_sems, recv_sems)

    return pl.pallas_call(
        body, name=name, in_specs=[_HBM] * ri, out_specs=[_HBM] * len(rider.out_shapes), out_shape=rider.out_shapes,
        scratch_shapes=[pltpu.SemaphoreType.DMA((rider.n_sems,))] * 2, input_output_aliases=rider.aliases,
    )(*rider.inputs)


def _hosted_call(body, rider, *, name, grid, in_specs, out_specs, out_shape, scratch_shapes, semantics):
    if rider is None:
        return pl.pallas_call(body, name=name, grid=grid, in_specs=in_specs, out_specs=out_specs, out_shape=out_shape,
                              scratch_shapes=scratch_shapes, compiler_params=_params(*semantics))
    n_in, n_out, n_scr = len(in_specs), len(out_specs), len(scratch_shapes)
    ri, ro = len(rider.inputs), len(rider.out_shapes)

    def hosted(*refs):
        parts, p = [], 0
        for cnt in (n_in, ri, n_out, ro, n_scr, 2):
            parts.append(refs[p:p + cnt])
            p += cnt
        ins, rins, outs, routs, scr, (send_sems, recv_sems) = parts
        first = functools.reduce(jnp.logical_and, [pl.program_id(a) == 0 for a in range(len(grid))])
        last = functools.reduce(jnp.logical_and, [pl.program_id(a) == grid[a] - 1 for a in range(len(grid))])

        @pl.when(first)
        def _():
            rider.start(rins, routs, send_sems, recv_sems)

        body(*ins, *outs, *scr)

        @pl.when(last)
        def _():
            rider.wait(rins, routs, send_sems, recv_sems)

    call = pl.pallas_call(
        hosted, name=name, grid=grid, in_specs=list(in_specs) + [_HBM] * ri, out_specs=list(out_specs) + [_HBM] * ro,
        out_shape=list(out_shape) + rider.out_shapes,
        scratch_shapes=list(scratch_shapes) + [pltpu.SemaphoreType.DMA((rider.n_sems,))] * 2,
        input_output_aliases={n_in + i: n_out + o for i, o in rider.aliases.items()},
        compiler_params=_params(*(("arbitrary",) * len(grid))))

    def run(*args):
        res = call(*args, *rider.inputs)
        return res[:n_out], res[n_out:]

    return run


def _ride_gather_ici(packs):
    n = len(packs)

    def sends(ins, outs, send_sems, recv_sems):
        x, y, c, chips = _place()
        return [_remote(ins[a].at[c], outs[a].at[2 * x + y, c], send_sems, recv_sems, 3 * a + j, (*chip, c))
                for a in range(n) for j, chip in enumerate(chips)]

    def recvs(ins, outs, send_sems, recv_sems):
        x, y, c, chips = _place()
        return [_remote(ins[a].at[c], outs[a].at[2 * chip[0] + chip[1], c], send_sems, recv_sems, 3 * a + j, (x, y, c))
                for a in range(n) for j, chip in enumerate(chips)]

    return _Rider(packs, [jax.ShapeDtypeStruct((N_CHIP,) + p.shape, p.dtype) for p in packs], 3 * n, sends, recvs)


def _ride_gather_d2d(gathered):
    n = len(gathered)

    def copies(landing_half, to):
        def build(ins, outs, send_sems, recv_sems):
            x, y, c, chips = _place()
            return [_remote(ins[a].at[2 * chip[0] + chip[1], c], outs[a].at[2 * chip[0] + chip[1], landing_half(c)],
                            send_sems, recv_sems, 3 * a + j, to(x, y, c))
                    for a in range(n) for j, chip in enumerate(chips)]
        return build

    return _Rider(gathered, [jax.ShapeDtypeStruct(g.shape, g.dtype) for g in gathered], 3 * n,
                  copies(lambda c: c, lambda x, y, c: (x, y, 1 - c)), copies(lambda c: 1 - c, lambda x, y, c: (x, y, c)),
                  aliases={a: a for a in range(n)})


def _ride_exchange(gs):
    n = len(gs)
    shapes = [g.shape[1:] if g.ndim == 4 else g.shape[:2] + (g.shape[2] // 2,) for g in gs]

    def copies(ins, outs, send_sems, recv_sems):
        x, y, c, _ = _place()

        def theirs(a):
            if gs[a].ndim == 4:
                return ins[a].at[1 - c]
            cols = shapes[a][2]
            return ins[a].at[:, :, pl.ds((1 - c) * cols, cols)]

        return [_remote(theirs(a), outs[a], send_sems, recv_sems, a, (x, y, 1 - c)) for a in range(n)]

    return _Rider(gs, [jax.ShapeDtypeStruct(sh, g.dtype) for sh, g in zip(shapes, gs)], n, copies, copies)


def _ride_scatter(b16s, to=(0, 1, 2), landing=None):
    n = len(b16s)

    def sends(ins, outs, send_sems, recv_sems):
        x, y, c, chips = _place()
        return [_remote(ins[a].at[2 * chips[j][0] + chips[j][1]], outs[a].at[2 * x + y], send_sems, recv_sems, 3 * a + j,
                        (*chips[j], c)) for a in range(n) for j in to]

    def recvs(ins, outs, send_sems, recv_sems):
        x, y, c, chips = _place()
        return [_remote(ins[a].at[2 * x + y], outs[a].at[2 * chips[j][0] + chips[j][1]], send_sems, recv_sems, 3 * a + j,
                        (x, y, c)) for a in range(n) for j in to]

    return _Rider(list(b16s) + list(landing or []), [jax.ShapeDtypeStruct(b.shape, b.dtype) for b in b16s], 3 * n,
                  sends, recvs, aliases={n + a: a for a in range(n)} if landing else None)


def _slab_tile(r, cols):
    tr = _tile(r, 256, 16)
    if tr % 16 == 0:
        return tr, cols
    return r, _tile(cols, 128)


def add_halves(g, got, idx, name):
    ns, r, cols = got.shape
    tr, tc = _slab_tile(r, cols)
    if g.ndim == 4:
        mine = pl.BlockSpec((None, None, tr, tc), lambda s, i, j, idx_ref: (idx_ref[0], s, i, j))
    else:
        mine = pl.BlockSpec((None, tr, tc), lambda s, i, j, idx_ref: (s, i, idx_ref[0] * (cols // tc) + j))

    def body(idx_ref, a_ref, b_ref, o32_ref, o16_ref):
        s = a_ref[...] + b_ref[...]
        o32_ref[...] = s
        o16_ref[...] = s.astype(BF16)

    blk = pl.BlockSpec((None, tr, tc), lambda s, i, j, idx_ref: (s, i, j))
    return pl.pallas_call(
        body, name=name,
        grid_spec=pltpu.PrefetchScalarGridSpec(
            num_scalar_prefetch=1, grid=(ns, r // tr, cols // tc),
            in_specs=[mine, blk], out_specs=[blk, blk]),
        out_shape=[jax.ShapeDtypeStruct((ns, r, cols), F32), jax.ShapeDtypeStruct((ns, r, cols), BF16)],
        compiler_params=_params("parallel", "parallel", "parallel"),
    )(idx, g, got)


def add_chips(a32, got16, idx, name):
    ns, r, cols = a32.shape
    tr, tc = _slab_tile(r, cols)

    def body(idx_ref, a_ref, r1_ref, r2_ref, r3_ref, o_ref):
        o_ref[...] = ((a_ref[...] + r1_ref[...].astype(F32)) + r2_ref[...].astype(F32)) + r3_ref[...].astype(F32)

    def slab(k):
        return pl.BlockSpec((None, tr, tc), lambda i, j, idx_ref: ((idx_ref[1] + k) % ns, i, j))

    return pl.pallas_call(
        body, name=name,
        grid_spec=pltpu.PrefetchScalarGridSpec(
            num_scalar_prefetch=1, grid=(r // tr, cols // tc), in_specs=[slab(0), slab(1), slab(2), slab(3)],
            out_specs=pl.BlockSpec((tr, tc), lambda i, j, idx_ref: (i, j))),
        out_shape=jax.ShapeDtypeStruct((r, cols), F32),
        compiler_params=_params("parallel", "parallel"),
    )(idx, a32, got16, got16, got16)


def share_halves(halves):
    n = len(halves)

    def body(*refs):
        in_refs, out_refs, (send_sems, recv_sems) = refs[:n], refs[n:2 * n], refs[2 * n:]
        x, y, c, _ = _place()
        cps = [pltpu.make_async_remote_copy(src_ref=in_refs[a], dst_ref=out_refs[a], send_sem=send_sems.at[a],
                                            recv_sem=recv_sems.at[a], device_id=(x, y, 1 - c), device_id_type=MESH)
               for a in range(n)]
        for cp in cps:
            cp.start()
        for cp in cps:
            cp.wait()

    return pl.pallas_call(
        body, name="share_halves", in_specs=[_HBM] * n, out_specs=[_HBM] * n,
        out_shape=[jax.ShapeDtypeStruct(h.shape, F32) for h in halves],
        scratch_shapes=[pltpu.SemaphoreType.DMA((n,)), pltpu.SemaphoreType.DMA((n,))],
    )(*halves)


def allreduce_small(v):
    R, _ = v.shape

    def body(in_ref, out_ref, slots, send_sems, recv_sems):
        x, y, c, _ = _place()
        me = 4 * x + 2 * y + c
        slots[me] = in_ref[...]
        cps = []
        for k in range(1, N_DEV):
            to = (x ^ (k >> 2), y ^ ((k >> 1) & 1), c ^ (k & 1))
            cps.append(pltpu.make_async_remote_copy(src_ref=in_ref, dst_ref=slots.at[me], send_sem=send_sems.at[k - 1],
                                                    recv_sem=recv_sems.at[k - 1], device_id=to, device_id_type=MESH))
        for cp in cps:
            cp.start()
        for k in range(1, N_DEV):
            frm = 4 * (x ^ (k >> 2)) + 2 * (y ^ ((k >> 1) & 1)) + (c ^ (k & 1))
            pltpu.make_async_remote_copy(src_ref=in_ref, dst_ref=slots.at[frm], send_sem=send_sems.at[k - 1],
                                         recv_sem=recv_sems.at[k - 1], device_id=(x, y, c), device_id_type=MESH).wait_recv()
        for cp in cps:
            cp.wait_send()
        acc = slots[0]
        for d in range(1, N_DEV):
            acc = acc + slots[d]
        out_ref[...] = acc

    vm = pl.BlockSpec(memory_space=pltpu.VMEM)
    return pl.pallas_call(
        body, name="allreduce_small", in_specs=[vm], out_specs=vm, out_shape=jax.ShapeDtypeStruct((R, ROW), F32),
        scratch_shapes=[pltpu.VMEM((N_DEV, R, ROW), F32), pltpu.SemaphoreType.DMA((N_DEV - 1,)),
                        pltpu.SemaphoreType.DMA((N_DEV - 1,))],
    )(v)


def _rows_of(n, unit=16):
    return -(-n // (unit * ROW)) * unit


def _pack_rows(items, total_rows, dtype, unit=16):
    parts = []
    used = 0
    for a in items:
        flat = a.reshape(-1)
        r = _rows_of(flat.shape[0], unit)
        flat = jnp.pad(flat, (0, r * ROW - flat.shape[0]))
        parts.append(flat.reshape(r, ROW))
        used += r
    if total_rows > used:
        parts.append(jnp.zeros((total_rows - used, ROW), dtype))
    return jnp.concatenate(parts, axis=0)


def _unpack_rows(buf, shapes, unit=16):
    lead = buf.shape[:-2]
    out = []
    off = 0
    for shp in shapes:
        n = math.prod(shp)
        r = _rows_of(n, unit)
        piece = buf[..., off:off + r, :].reshape(*lead, r * ROW)[..., :n].reshape(*lead, *shp)
        out.append(piece)
        off += r
    return out


def _interleave_heads(w, H):
    lead = w.shape[:-1]
    return w.reshape(*lead, 3, H, HEAD).swapaxes(-3, -2).reshape(*lead, 3 * H * HEAD)


def _deinterleave_heads(w, H):
    lead = w.shape[:-1]
    return w.reshape(*lead, H, 3, HEAD).swapaxes(-3, -2).reshape(*lead, 3 * H * HEAD)


def _interleave_head_rows(w, H):
    return w.reshape(3, H, HEAD, w.shape[-1]).swapaxes(0, 1).reshape(3 * H * HEAD, w.shape[-1])


def _deinterleave_head_rows(w, H):
    return w.reshape(H, 3, HEAD, w.shape[-1]).swapaxes(0, 1).reshape(3 * H * HEAD, w.shape[-1])


def kernel(x, norm_mix, w_in, fox_f_bias, gdn_conv_w, gdn_a_log, gdn_dt_bias, gdn_norm, w_branch_fox, w_branch_gdn, w_out, norm_ffn, w_up, ffn_conv_w, w_down, norm_final, loss_target, m_norm_mix, m_w_in, m_fox_f_bias, m_gdn_conv_w, m_gdn_a_log, m_gdn_dt_bias, m_gdn_norm, m_w_branch_fox, m_w_branch_gdn, m_w_out, m_norm_ffn, m_w_up, m_ffn_conv_w, m_w_down, m_norm_final, v_norm_mix, v_w_in, v_fox_f_bias, v_gdn_conv_w, v_gdn_a_log, v_gdn_dt_bias, v_gdn_norm, v_w_branch_fox, v_w_branch_gdn, v_w_out, v_norm_ffn, v_w_up, v_ffn_conv_w, v_w_down, v_norm_final):
    B, S, D = x.shape
    T = B * S
    H = D // HEAD
    N = S // CHUNK
    FF = w_down.shape[1] * N_CHIP
    d_in = 9 * D + 3 * H
    assert w_in.shape[2] * N_CHIP == d_in and 3 * H <= 128

    cidx = lax.axis_index("c").astype(jnp.int32)
    sidx = (2 * lax.axis_index("x") + lax.axis_index("y")).astype(jnp.int32)
    idx = jnp.stack([cidx, sidx])

    rowed = [w_branch_fox[0], w_branch_gdn[0], w_out[0], w_down[0]]
    convs = [gdn_conv_w[0], ffn_conv_w[0]]
    rowed_shapes = [a.shape for a in rowed]
    conv_shapes = [a.shape + (2,) for a in convs]
    pad_rows = lambda shapes: -(-sum(_rows_of(math.prod(s)) for s in shapes) // 256) * 128
    Rh, Rc = pad_rows(rowed_shapes), pad_rows(conv_shapes)
    halves = lambda a: a.reshape(2, a.shape[0] // 2, a.shape[1])
    c_in = w_in.shape[2]
    packs_a = [w_in[0].T.astype(BF16).reshape(c_in, 2, D // 2).transpose(1, 0, 2),
               halves(_pack_rows([lax.bitcast_convert_type(a, BF16) for a in convs], 2 * Rc, BF16))]
    packs_b = [halves(w_up[0].astype(BF16)), halves(_pack_rows([a.astype(BF16) for a in rowed], 2 * Rh, BF16))]
    own = lambda gs, ps: [lax.dynamic_update_slice(g, p[None], (sidx, 0, 0, 0)) for g, p in zip(gs, ps)]
    by_cols = lambda g: g.transpose(1, 2, 0, 3).reshape(2 * g.shape[2], N_CHIP * g.shape[3])
    cat_cols = lambda p: jnp.concatenate([p[i] for i in range(N_CHIP)], axis=-1)
    cat_rows = lambda p: p.reshape(-1, p.shape[-1])
    g_in, g_conv = own(allgather_weights(packs_a), packs_a)
    W_inT = g_in.transpose(0, 2, 1, 3).reshape(N_CHIP * c_in, D)
    conv_parts = _unpack_rows(g_conv.reshape(N_CHIP, 2 * Rc, ROW), conv_shapes)
    gconv = cat_cols(lax.bitcast_convert_type(conv_parts[0], F32))
    fconv = cat_cols(lax.bitcast_convert_type(conv_parts[1], F32))

    o1, o2 = 3 * D, 3 * D + H
    o3, o4, o5, o6 = o2 + 3 * D, o2 + 3 * D + H, o2 + 3 * D + 2 * H, o2 + 4 * D + 2 * H
    W_foxT = _interleave_head_rows(W_inT[:o1], H)
    W_gqkvT = _interleave_head_rows(W_inT[o2:o3], H)
    W_gzT = W_inT[o5:o6]
    W_gatesT = W_inT[o6:]
    W_smallT = jnp.concatenate([W_inT[o1:o2], W_inT[o3:o5], jnp.zeros((128 - 3 * H, D), BF16)], axis=0)
    gconv_i = _interleave_heads(gconv, H)
    fconv_g, fconv_v = fconv[:, :FF], fconv[:, FF:]
    prm = jnp.zeros((8, 128), F32)
    prm = prm.at[0, 0:H].set(fox_f_bias[0]).at[0, H:2 * H].set(gdn_dt_bias[0]).at[1, H:2 * H].set(gdn_a_log[0])

    x2 = x.reshape(T, D)
    tgt = loss_target.reshape(T, D)

    hn1 = rmsnorm_fwd(x2, norm_mix, "rmsnorm_mix")
    p_fox = matmul(hn1, W_foxT, "nt", "proj_fox", out_dtype=BF16)
    p_gqkv = matmul(hn1, W_gqkvT, "nt", "proj_gqkv")
    p_gz = matmul(hn1, W_gzT, "nt", "proj_gz")
    p_gates = matmul(hn1, W_gatesT, "nt", "proj_gates")
    p_small = matmul(hn1, W_smallT, "nt", "proj_small")

    sm = small_fwd(p_small, prm, B, S, H)
    heads = lambda a: a.reshape(B, S, H).transpose(0, 2, 1)
    c_bhs, gc_bhs, beta_bhs = heads(sm[:, 0:H]), heads(sm[:, H:2 * H]), heads(sm[:, 2 * H:3 * H])
    c_col, c_row = c_bhs[..., None], c_bhs[:, :, None, :]
    gcr5 = gc_bhs.reshape(B, H, N, 1, CHUNK)
    gcr_u = gc_bhs.reshape(B, H, N // PAIR, 1, PAIR * CHUNK)
    betar_u = beta_bhs.reshape(B, H, N // PAIR, 1, PAIR * CHUNK)

    (o_fox, o_fox16, lse), arriving = fox_fwd(p_fox, c_col, c_row, B, S, H, rider=_ride_gather_ici(packs_b))
    qkvn = gdn_prep_fwd(p_gqkv, gconv_i, B, S, H)
    (u_hat, w_t, t_inv), arrived = gdn_intra_fwd(qkvn, betar_u, gcr_u, B, S, H, rider=_ride_gather_d2d(arriving))
    g_up, g_rowed = own(arrived, packs_b)
    W_up = by_cols(g_up)
    W_up_g, W_up_v = W_up[:, :FF], W_up[:, FF:]
    W_bf, W_bg, W_out, W_down = (cat_rows(p) for p in _unpack_rows(g_rowed.reshape(N_CHIP, 2 * Rh, ROW), rowed_shapes))
    o_gdn, states, y_gdn = gdn_inter_fwd(qkvn, u_hat, w_t, gcr5, p_gz, gdn_norm, B, S, H)
    bf_ = matmul(o_fox16, W_bf, "nn", "branch_fox")
    bg_, y = matmul(y_gdn, W_bg, "nn", "branch_gdn", post=_post_merge(p_gates, bf_))
    h1, hn2 = matmul(y, W_out, "nn", "out_proj", add=x2, post=_post_rmsnorm(norm_ffn))
    up_g = matmul(hn2, W_up_g, "nn", "up_gate")
    up_v = matmul(hn2, W_up_v, "nn", "up_val")
    act = ffn_gate_fwd(up_g, up_v, fconv_g, fconv_v, B, S)
    dh2, dh2_16, loss_cols, d_norm_final = matmul(act, W_down, "nn", "down_proj", add=h1,
                                                  post=_post_loss(norm_final.reshape(1, D), tgt))
    loss = lax.psum(0.5 * jnp.sum(loss_cols) / D, ("x", "y", "c"))

    d_act = matmul(dh2_16, W_down, "nt", "d_act")
    dW_down = matmul(act, dh2_16, "tn", "dw_down")
    d_upg, d_upv, d_fconv_g, d_fconv_v = ffn_gate_bwd(up_g, up_v, fconv_g, fconv_v, d_act, B, S)
    d_hn2 = matmul(d_upg, W_up_g, "nt", "d_hn2_g")
    dh1, dh1_16, d_norm_ffn = matmul(d_upv, W_up_v, "nt", "d_hn2_v", add=d_hn2,
                                     post=_post_rmsnorm_bwd(h1, norm_ffn, dh2, True))
    dW_up = jnp.concatenate([matmul(hn2, d_upg, "tn", "dw_up_g"), matmul(hn2, d_upv, "tn", "dw_up_v")], axis=1)
    d_bf, d_bg, d_gates = matmul(dh1_16, W_out, "nt", "d_y", post=_post_merge_bwd(p_gates, bf_, bg_))
    dW_out = matmul(y, dh1_16, "tn", "dw_out")
    d_ofox = matmul(d_bf, W_bf, "nt", "d_ofox")
    dW_bf = matmul(o_fox16, d_bf, "tn", "dw_bf")
    d_ygdn = matmul(d_bg, W_bg, "nt", "d_ygdn")
    dW_bg = matmul(y_gdn, d_bg, "tn", "dw_bg")

    d_fconv = jnp.concatenate([d_fconv_g, d_fconv_v], axis=1)
    col_shard = lambda g, s: g[:, s * (g.shape[1] // N_CHIP):(s + 1) * (g.shape[1] // N_CHIP)]
    row_shard = lambda g, s: g[s * (g.shape[0] // N_CHIP):(s + 1) * (g.shape[0] // N_CHIP)]
    shard_items = lambda s: [row_shard(dW_bf, s), row_shard(dW_bg, s), row_shard(dW_out, s), row_shard(dW_down, s),
                             col_shard(d_fconv, s)]
    g_shapes = [a.shape for a in shard_items(0)]
    assert sum(_rows_of(math.prod(s)) for s in g_shapes) <= 2 * Rh
    to_slabs = lambda g: g.reshape(2, g.shape[0] // 2, N_CHIP, g.shape[1] // N_CHIP).transpose(0, 2, 1, 3)
    gpacks_b = [to_slabs(dW_up),
                jnp.stack([_pack_rows(shard_items(s), 2 * Rh, F32).reshape(2, Rh, ROW) for s in range(N_CHIP)], axis=1)]
    (d_pfox, d_ccol, d_crow), gots_b = fox_bwd(p_fox, c_col, c_row, o_fox, lse, d_ofox, B, S, H,
                                              rider=_ride_exchange(gpacks_b))
    sums_b = [add_halves(g, got, idx, "add_halves_b%d" % i) for i, (g, got) in enumerate(zip(gpacks_b, gots_b))]

    (dq_i, dk_i, d_uh, d_wt, dgcr_a, d_gz, d_gn_parts), got16_b = gdn_inter_bwd(
        qkvn, u_hat, w_t, gcr5, states, o_gdn, p_gz, gdn_norm, d_ygdn, B, S, H,
        rider=_ride_scatter([s16 for _, s16 in sums_b]))
    d_gdn_norm = jnp.sum(d_gn_parts[:, :, 0, :], axis=(0, 1))[None]
    mine_b = [add_chips(s32, g16, idx, "add_chips_b%d" % i) for i, ((s32, _), g16) in enumerate(zip(sums_b, got16_b))]
    d_qkvn, d_betar5, dgcr_b = gdn_intra_bwd(qkvn, betar_u, gcr_u, t_inv, d_uh, d_wt, dq_i, dk_i, B, S, H)
    d_pgqkv, d_gconv_i = gdn_prep_bwd(p_gqkv, gconv_i, d_qkvn, B, S, H)

    tokens = lambda a: a.reshape(B, H, S).transpose(0, 2, 1).reshape(T, H)
    d_gc = dgcr_a.reshape(B, H, S) + dgcr_b.reshape(B, H, S)
    d_sm = jnp.concatenate([tokens(d_ccol.reshape(B, H, S) + d_crow.reshape(B, H, S)), tokens(d_gc), tokens(d_betar5.reshape(B, H, S)),
                            jnp.zeros((T, 128 - 3 * H), F32)], axis=1)
    d_psmall, d_prm = small_bwd(p_small, prm, d_sm, B, S, H)

    dW_foxT = matmul(d_pfox, hn1, "tn", "dw_fox")
    dW_gqkvT = matmul(d_pgqkv, hn1, "tn", "dw_gqkv")
    dW_gzT = matmul(d_gz, hn1, "tn", "dw_gz")
    dW_gatesT = matmul(d_gates, hn1, "tn", "dw_gates")
    dW_smallT = matmul(d_psmall, hn1, "tn", "dw_small")
    dW_inT = jnp.concatenate([_deinterleave_head_rows(dW_foxT, H), dW_smallT[0:H], _deinterleave_head_rows(dW_gqkvT, H),
                              dW_smallT[H:3 * H], dW_gzT, dW_gatesT], axis=0)
    d_gconv = _deinterleave_heads(d_gconv_i, H)

    gpack_a = [dW_inT.reshape(N_CHIP, c_in, D)]
    d_hn1, gots_a = matmul(d_pfox, W_foxT, "nn", "d_hn1_fox", rider=_ride_exchange(gpack_a))
    sums_a = [add_halves(gpack_a[0], gots_a[0], idx, "add_halves_a")]
    d_hn1, landing_a = matmul(d_pgqkv, W_gqkvT, "nn", "d_hn1_gqkv", add=d_hn1,
                              rider=_ride_scatter([sums_a[0][1]], to=(0, 1)))
    d_hn1 = matmul(d_gz, W_gzT, "nn", "d_hn1_gz", add=d_hn1)
    d_hn1, got16_a = matmul(d_gates, W_gatesT, "nn", "d_hn1_gates", add=d_hn1,
                            rider=_ride_scatter([sums_a[0][1]], to=(2,), landing=landing_a))
    mine = [add_chips(sums_a[0][0], got16_a[0], idx, "add_chips_a")] + mine_b
    grad_x, d_norm_mix = matmul(d_psmall, W_smallT, "nn", "d_hn1_small", add=d_hn1,
                                post=_post_rmsnorm_bwd(x2, norm_mix, dh1, False))

    others = share_halves(mine)
    g_w_inT, g_up, g_rows = (jnp.concatenate([jnp.where(cidx == 0, h, o), jnp.where(cidx == 0, o, h)], axis=ax)
                             for h, o, ax in zip(mine, others, (1, 0, 0)))
    g_w_in = g_w_inT.T
    g_bf, g_bg, g_out, g_down, g_fconv = _unpack_rows(g_rows, g_shapes)

    small_items = [d_norm_mix, d_norm_ffn, d_norm_final, d_gdn_norm, d_prm, d_gconv]
    small_shapes = [a.shape for a in small_items]
    sv = allreduce_small(_pack_rows(small_items, 0, F32, unit=8))
    g_norm_mix, g_norm_ffn, g_norm_final, g_gdn_norm, g_prm, g_gconv_all = _unpack_rows(sv, small_shapes, unit=8)
    g_norm_final = g_norm_final.reshape(D)
    g_fbias, g_dtb, g_alog = g_prm[0:1, 0:H], g_prm[0:1, H:2 * H], g_prm[1:2, H:2 * H]
    g_gconv = lax.dynamic_slice_in_dim(g_gconv_all, sidx * (3 * D // N_CHIP), 3 * D // N_CHIP, axis=1)

    names = ["norm_mix", "w_in", "fox_f_bias", "gdn_conv_w", "gdn_a_log", "gdn_dt_bias", "gdn_norm", "w_branch_fox",
             "w_branch_gdn", "w_out", "norm_ffn", "w_up", "ffn_conv_w", "w_down", "norm_final"]
    ws = [norm_mix, w_in, fox_f_bias, gdn_conv_w, gdn_a_log, gdn_dt_bias, gdn_norm, w_branch_fox, w_branch_gdn, w_out,
          norm_ffn, w_up, ffn_conv_w, w_down, norm_final]
    ms = [m_norm_mix, m_w_in, m_fox_f_bias, m_gdn_conv_w, m_gdn_a_log, m_gdn_dt_bias, m_gdn_norm, m_w_branch_fox,
          m_w_branch_gdn, m_w_out, m_norm_ffn, m_w_up, m_ffn_conv_w, m_w_down, m_norm_final]
    vs = [v_norm_mix, v_w_in, v_fox_f_bias, v_gdn_conv_w, v_gdn_a_log, v_gdn_dt_bias, v_gdn_norm, v_w_branch_fox,
          v_w_branch_gdn, v_w_out, v_norm_ffn, v_w_up, v_ffn_conv_w, v_w_down, v_norm_final]
    gs = [g_norm_mix, g_w_in, g_fbias, g_gconv, g_alog, g_dtb, g_gdn_norm, g_bf, g_bg, g_out, g_norm_ffn, g_up,
          g_fconv, g_down, g_norm_final]
    gs = [g.reshape(w.shape) for g, w in zip(gs, ws)]
    deltas, new_ms, new_vs = [], [], []
    for nm, w, g, m, v in zip(names, ws, gs, ms, vs):
        if w.ndim == 1:
            d, a, b = adamw(w.reshape(1, -1), g.reshape(1, -1), m.reshape(1, -1), v.reshape(1, -1), "adamw_" + nm)
            d, a, b = d.reshape(w.shape), a.reshape(w.shape), b.reshape(w.shape)
        elif nm == "w_in":
            d, a, b = (r.T[None] for r in adamw(w[0].T, g_w_inT, m[0].T, v[0].T, "adamw_" + nm))
        else:
            d, a, b = adamw(w, g, m, v, "adamw_" + nm)
        deltas.append(d)
        new_ms.append(a)
        new_vs.append(b)

    return (loss, grad_x.reshape(B, S, D), *gs, *deltas, *new_ms, *new_vs)
```

```python
import functools
import math

import jax
import jax.numpy as jnp
from jax import lax
from jax.experimental import pallas as pl
from jax.experimental.pallas import tpu as pltpu

F32 = jnp.float32
BF16 = jnp.bfloat16
HEAD = 128
CHUNK = 64
GDN_CONV = 4
FFN_CONV = 3
EPS = 1e-6
NEG = -1e30
ROW = 1024
ATT_TILE = 512
MM_WEIGHT_TILE_BYTES = 8 << 20
MM_TN_OPERAND_BYTES = 32 << 20
N_CHIP = 4
N_DEV = 8
MESH = pl.DeviceIdType.MESH
HI = lax.Precision.HIGH
EXACT = lax.Precision.HIGHEST

ADAM_LR, ADAM_B1, ADAM_B2, ADAM_EPS, ADAM_WD, ADAM_STEP = 0.001, 0.9, 0.999, 1e-08, 0.01, 10


def _tile(n, cap, unit=128):
    best = None
    t = unit
    while t <= min(n, cap):
        if n % t == 0:
            best = t
        t += unit
    return best if best is not None else n


def _params(*sem):
    return pltpu.CompilerParams(dimension_semantics=sem)


_NN = (((1,), (0,)), ((), ()))
_NT = (((1,), (1,)), ((), ()))
_TN = (((0,), (0,)), ((), ()))


def _dg(a, b, dims, hi):
    if hi:
        return lax.dot_general(a, b, dims, precision=HI, preferred_element_type=F32)
    return lax.dot_general(a.astype(BF16), b.astype(BF16), dims, preferred_element_type=F32)


class _RawOps:
    @staticmethod
    def nn(a, b, hi=False):
        return _dg(a, b, _NN, hi)

    @staticmethod
    def nt(a, b, hi=False):
        return _dg(a, b, _NT, hi)

    @staticmethod
    def tn(a, b, hi=False):
        return _dg(a, b, _TN, hi)


def _make_diff_ops():
    def build(hi):
        @jax.custom_vjp
        def nn(a, b):
            return _dg(a, b, _NN, hi)

        nn.defvjp(lambda a, b: (_dg(a, b, _NN, hi), (a, b)),
                  lambda r, g: (_dg(g, r[1], _NT, hi), _dg(r[0], g, _TN, hi)))

        @jax.custom_vjp
        def nt(a, b):
            return _dg(a, b, _NT, hi)

        nt.defvjp(lambda a, b: (_dg(a, b, _NT, hi), (a, b)),
                  lambda r, g: (_dg(g, r[1], _NN, hi), _dg(g, r[0], _TN, hi)))

        @jax.custom_vjp
        def tn(a, b):
            return _dg(a, b, _TN, hi)

        tn.defvjp(lambda a, b: (_dg(a, b, _TN, hi), (a, b)),
                  lambda r, g: (_dg(r[1], g, _NT, hi), _dg(r[0], g, _NN, hi)))
        return nn, nt, tn

    lo, hi_ = build(False), build(True)

    class _DiffOps:
        @staticmethod
        def nn(a, b, hi=False):
            return (hi_ if hi else lo)[0](a, b)

        @staticmethod
        def nt(a, b, hi=False):
            return (hi_ if hi else lo)[1](a, b)

        @staticmethod
        def tn(a, b, hi=False):
            return (hi_ if hi else lo)[2](a, b)

    return _DiffOps


_DiffOps = _make_diff_ops()


def _sigmoid(x):
    return 1.0 / (1.0 + jnp.exp(-x))


def _mm_tile(n, pref):
    if n % pref == 0:
        return pref
    if n % 1408 == 0:
        return 1408
    return _tile(n, pref)


class _Post:
    def __init__(self, fn, row_ins=(), vec_ins=(), row_outs=(), acc_outs=(), keep_main=True):
        self.fn, self.keep_main = fn, keep_main
        self.row_ins = [r if isinstance(r, tuple) else (r, r.shape[1], 0) for r in row_ins]
        self.vec_ins, self.row_outs, self.acc_outs = list(vec_ins), list(row_outs), list(acc_outs)


def matmul(a, b, mode, name, add=None, out_dtype=F32, post=None, rider=None, slab=None):
    if mode == "nn":
        (M, K), (K2, N) = a.shape, b.shape
    elif mode == "nt":
        (M, K), (N, K2) = a.shape, b.shape
    else:
        (K, M), (K2, N) = a.shape, b.shape
    assert K == K2, (name, a.shape, b.shape)
    tn = slab[1] if slab else _mm_tile(N, 1024)
    if mode == "tn":
        tm = slab[0] if slab else (M if M <= 1408 else _mm_tile(M, 1408))
        row_bytes = 2 * (tm * a.dtype.itemsize + tn * b.dtype.itemsize)
        tk = next((t for t in (4096, 2048) if K % t == 0 and t * row_bytes <= MM_TN_OPERAND_BYTES), _mm_tile(K, 1024))
    else:
        tk = K if K * tn * 2 <= MM_WEIGHT_TILE_BYTES else _mm_tile(K, 1024)
        tm = _mm_tile(M, 1024 if tk <= 2048 and post is None else 512)
    nk = K // tk
    assert post is None or (mode != "tn" and tn == N), name
    dims = {"nn": _NN, "nt": _NT, "tn": _TN}[mode]
    if mode == "tn":
        a_spec = pl.BlockSpec((tk, tm), lambda j, i, k: (k, i))
    else:
        a_spec = pl.BlockSpec((tm, tk), lambda j, i, k: (i, k))
    if mode == "nt":
        b_spec = pl.BlockSpec((tn, tk), lambda j, i, k: (j, k))
    else:
        b_spec = pl.BlockSpec((tk, tn), lambda j, i, k: (k, j))
    o_spec = pl.BlockSpec((tm, tn), lambda j, i, k: (i, j))
    has_add = add is not None
    keep_main = post is None or post.keep_main
    counts = [2 + has_add] + ([len(post.row_ins), len(post.vec_ins)] if post else [0, 0]) + [int(keep_main)]
    counts += ([len(post.row_outs), len(post.acc_outs)] if post else [0, 0]) + [int(nk > 1)]

    def body(*refs):
        parts, p = [], 0
        for cnt in counts:
            parts.append(refs[p:p + cnt])
            p += cnt
        core, row_ins, vec_ins, main, row_outs, acc_outs, acc = parts
        a_ref, b_ref = core[:2]
        prod = lax.dot_general(a_ref[...].astype(BF16), b_ref[...].astype(BF16), dims, preferred_element_type=F32)

        def finish(r):
            if has_add:
                r = r + core[2][...]
            if keep_main:
                main[0][...] = r.astype(out_dtype)
            if post is not None:
                @pl.when(pl.program_id(1) == 0)
                def _():
                    for ref in acc_outs:
                        ref[...] = jnp.zeros_like(ref)

                post.fn(r, row_ins, vec_ins, row_outs, acc_outs)

        if nk == 1:
            finish(prod)
            return
        acc_ref = acc[0]
        k = pl.program_id(2)

        @pl.when(k == 0)
        def _():
            acc_ref[...] = jnp.zeros_like(acc_ref)

        acc_ref[...] += prod

        @pl.when(k == nk - 1)
        def _():
            finish(acc_ref[...])

    in_specs = [a_spec, b_spec] + ([o_spec] if has_add else [])
    args = (a, b) + ((add,) if has_add else ())
    out_specs = [o_spec] if keep_main else []
    out_shape = [jax.ShapeDtypeStruct((M, N), out_dtype)] if keep_main else []
    if slab:
        out_specs = [pl.BlockSpec((None, None, tm, tn), lambda j, i, k: (i, j, 0, 0))]
        out_shape = [jax.ShapeDtypeStruct((M // tm, N // tn, tm, tn), out_dtype)]
    if post is not None:
        in_specs += [pl.BlockSpec((tm, cols), lambda j, i, k, cb=cb: (i, cb)) for _, cols, cb in post.row_ins]
        in_specs += [pl.BlockSpec((1, v.shape[1]), lambda j, i, k: (0, 0)) for v in post.vec_ins]
        args += tuple(r for r, _, _ in post.row_ins) + tuple(post.vec_ins)
        out_specs += [pl.BlockSpec((tm, cols), lambda j, i, k: (i, 0)) for cols, _ in post.row_outs]
        out_specs += [pl.BlockSpec((1, cols), lambda j, i, k: (0, 0)) for cols in post.acc_outs]
        out_shape += [jax.ShapeDtypeStruct((M, cols), dt) for cols, dt in post.row_outs]
        out_shape += [jax.ShapeDtypeStruct((1, cols), F32) for cols in post.acc_outs]
    rows_sem = "arbitrary" if post is not None and post.acc_outs else "parallel"
    res = _hosted_call(
        body, rider, name=name, grid=(N // tn, M // tm, nk), in_specs=in_specs, out_specs=out_specs, out_shape=out_shape,
        scratch_shapes=[pltpu.VMEM((tm, tn), F32)] if nk > 1 else [], semantics=("parallel", rows_sem, "arbitrary"),
    )(*args)
    if rider is not None:
        res, carried = res
        return (res[0] if post is None else res), carried
    return res[0] if post is None else res


def rmsnorm_fwd(x, g, name):
    T, D = x.shape
    tm = _tile(T, 512, 8)

    def body(x_ref, g_ref, o_ref):
        xv = x_ref[...]
        r = lax.rsqrt(jnp.mean(xv * xv, axis=-1, keepdims=True) + EPS)
        o_ref[...] = (xv * r * g_ref[...]).astype(BF16)

    return pl.pallas_call(
        body, name=name, grid=(T // tm,),
        in_specs=[pl.BlockSpec((tm, D), lambda i: (i, 0)), pl.BlockSpec((1, D), lambda i: (0, 0))],
        out_specs=pl.BlockSpec((tm, D), lambda i: (i, 0)),
        out_shape=jax.ShapeDtypeStruct((T, D), BF16),
        compiler_params=_params("parallel"),
    )(x, g)


def _post_rmsnorm(g):
    def fn(r, row_ins, vec_ins, row_outs, acc_outs):
        rs = lax.rsqrt(jnp.mean(r * r, axis=-1, keepdims=True) + EPS)
        row_outs[0][...] = (r * rs * vec_ins[0][...]).astype(BF16)

    return _Post(fn, vec_ins=[g], row_outs=[(g.shape[1], BF16)])


def _post_rmsnorm_bwd(x, g, dres, with_bf16):
    D = g.shape[1]

    def fn(dy, row_ins, vec_ins, row_outs, acc_outs):
        xv = row_ins[0][...]
        rs = lax.rsqrt(jnp.mean(xv * xv, axis=-1, keepdims=True) + EPS)
        xh = xv * rs
        acc_outs[0][...] += jnp.sum(dy * xh, axis=0, keepdims=True)
        dxh = dy * vec_ins[0][...]
        dx = row_ins[1][...] + rs * (dxh - xh * jnp.mean(dxh * xh, axis=-1, keepdims=True))
        row_outs[0][...] = dx
        if with_bf16:
            row_outs[1][...] = dx.astype(BF16)

    return _Post(fn, row_ins=[x, dres], vec_ins=[g], row_outs=[(D, F32)] + ([(D, BF16)] if with_bf16 else []),
                 acc_outs=[D], keep_main=False)


def _post_loss(g, target):
    D = g.shape[1]

    def fn(hv, row_ins, vec_ins, row_outs, acc_outs):
        rs = lax.rsqrt(jnp.mean(hv * hv, axis=-1, keepdims=True) + EPS)
        xh = hv * rs
        gv = vec_ins[0][...]
        err = xh * gv - row_ins[0][...]
        acc_outs[0][...] += jnp.sum(err * err, axis=0, keepdims=True)
        dy = err * (1.0 / D)
        acc_outs[1][...] += jnp.sum(dy * xh, axis=0, keepdims=True)
        dxh = dy * gv
        dh = rs * (dxh - xh * jnp.mean(dxh * xh, axis=-1, keepdims=True))
        row_outs[0][...] = dh
        row_outs[1][...] = dh.astype(BF16)

    return _Post(fn, row_ins=[target], vec_ins=[g], row_outs=[(D, F32), (D, BF16)], acc_outs=[D, D], keep_main=False)


def _shift_down(x, k):
    if k == 0:
        return x
    rows = lax.broadcasted_iota(jnp.int32, x.shape, 0)
    return jnp.where(rows >= k, pltpu.roll(x, k, 0), 0.0)


def _shift_up(x, k):
    if k == 0:
        return x
    s = x.shape[0]
    rows = lax.broadcasted_iota(jnp.int32, x.shape, 0)
    return jnp.where(rows < s - k, pltpu.roll(x, s - k, 0), 0.0)


def _conv_fwd(x, w_ref, kw, keep_shifted=False):
    shifted = [_shift_down(x, kw - 1 - i) for i in range(kw - 1)]
    y = x * w_ref[kw - 1:kw, :]
    for i in range(kw - 1):
        y = y + shifted[i] * w_ref[i:i + 1, :]
    return (y, shifted) if keep_shifted else y


def _conv_bwd(x, shifted, dy, w_ref, kw):
    dx = dy * w_ref[kw - 1:kw, :]
    dws = []
    for i in range(kw - 1):
        dx = dx + _shift_up(dy, kw - 1 - i) * w_ref[i:i + 1, :]
        dws.append(jnp.sum(dy * shifted[i], axis=0, keepdims=True))
    dws.append(jnp.sum(dy * x, axis=0, keepdims=True))
    return dx, dws


def ffn_gate_fwd(up_g, up_v, cw_g, cw_v, B, S):
    T, Fd = up_g.shape
    tc = _tile(Fd, 256)

    def body(g_ref, v_ref, wg_ref, wv_ref, o_ref):
        ug = _conv_fwd(g_ref[...], wg_ref, FFN_CONV)
        uv = _conv_fwd(v_ref[...], wv_ref, FFN_CONV)
        o_ref[...] = (ug * _sigmoid(ug) * uv).astype(BF16)

    blk = pl.BlockSpec((S, tc), lambda b, j: (b, j))
    wblk = pl.BlockSpec((FFN_CONV, tc), lambda b, j: (0, j))
    return pl.pallas_call(
        body, name="ffn_gate_fwd", grid=(B, Fd // tc), in_specs=[blk, blk, wblk, wblk], out_specs=blk,
        out_shape=jax.ShapeDtypeStruct((T, Fd), BF16), compiler_params=_params("parallel", "parallel"),
    )(up_g, up_v, cw_g, cw_v)


def ffn_gate_bwd(up_g, up_v, cw_g, cw_v, d_act, B, S):
    T, Fd = up_g.shape
    tc = _tile(Fd, 256)

    def body(g_ref, v_ref, wg_ref, wv_ref, da_ref, dg_ref, dv_ref, dwg_ref, dwv_ref):
        @pl.when(pl.program_id(1) == 0)
        def _():
            dwg_ref[...] = jnp.zeros_like(dwg_ref)
            dwv_ref[...] = jnp.zeros_like(dwv_ref)

        xg, xv = g_ref[...], v_ref[...]
        ug, sh_g = _conv_fwd(xg, wg_ref, FFN_CONV, keep_shifted=True)
        uv, sh_v = _conv_fwd(xv, wv_ref, FFN_CONV, keep_shifted=True)
        da = da_ref[...]
        sg = _sigmoid(ug)
        d_ug = da * uv * (sg + ug * sg * (1.0 - sg))
        d_uv = da * ug * sg
        dxg, dwg = _conv_bwd(xg, sh_g, d_ug, wg_ref, FFN_CONV)
        dxv, dwv = _conv_bwd(xv, sh_v, d_uv, wv_ref, FFN_CONV)
        dg_ref[...] = dxg.astype(BF16)
        dv_ref[...] = dxv.astype(BF16)
        for i in range(FFN_CONV):
            dwg_ref[i:i + 1, :] += dwg[i]
            dwv_ref[i:i + 1, :] += dwv[i]

    blk = pl.BlockSpec((S, tc), lambda j, b: (b, j))
    wblk = pl.BlockSpec((FFN_CONV, tc), lambda j, b: (0, j))
    return pl.pallas_call(
        body, name="ffn_gate_bwd", grid=(Fd // tc, B), in_specs=[blk, blk, wblk, wblk, blk],
        out_specs=[blk, blk, wblk, wblk],
        out_shape=[jax.ShapeDtypeStruct((T, Fd), BF16), jax.ShapeDtypeStruct((T, Fd), BF16),
                   jax.ShapeDtypeStruct((FFN_CONV, Fd), F32), jax.ShapeDtypeStruct((FFN_CONV, Fd), F32)],
        compiler_params=_params("parallel", "arbitrary"),
    )(up_g, up_v, cw_g, cw_v, d_act)


def _post_merge(p_gates, bf_):
    D = bf_.shape[1]

    def fn(bg, row_ins, vec_ins, row_outs, acc_outs):
        gf_ref, gg_ref, bf_ref = row_ins
        row_outs[0][...] = (_sigmoid(gf_ref[...]) * bf_ref[...] + _sigmoid(gg_ref[...]) * bg).astype(BF16)

    return _Post(fn, row_ins=[(p_gates, D, 0), (p_gates, D, 1), bf_], row_outs=[(D, BF16)])


def _post_merge_bwd(p_gates, bf_, bg_):
    D = bf_.shape[1]

    def fn(d, row_ins, vec_ins, row_outs, acc_outs):
        gf_ref, gg_ref, bf_ref, bg_ref = row_ins
        sf, sg = _sigmoid(gf_ref[...]), _sigmoid(gg_ref[...])
        row_outs[0][...] = (d * sf).astype(BF16)
        row_outs[1][...] = (d * sg).astype(BF16)
        row_outs[2][:, 0:D] = (d * bf_ref[...] * sf * (1.0 - sf)).astype(BF16)
        row_outs[2][:, D:2 * D] = (d * bg_ref[...] * sg * (1.0 - sg)).astype(BF16)

    return _Post(fn, row_ins=[(p_gates, D, 0), (p_gates, D, 1), bf_, bg_],
                 row_outs=[(D, BF16), (D, BF16), (2 * D, BF16)], keep_main=False)


def fox_fwd(p_fox, c_col, c_row, B, S, H, rider=None):
    T = B * S
    t = _tile(S, ATT_TILE)
    nq = S // t
    scale = HEAD ** -0.5

    def body(q_ref, k_ref, v_ref, cq_ref, cr_ref, o_ref, o16_ref, lse_ref):
        i = pl.program_id(2)
        q = q_ref[...]
        cq = cq_ref[...]
        row = lax.broadcasted_iota(jnp.int32, (t, t), 0)
        col = lax.broadcasted_iota(jnp.int32, (t, t), 1)

        def step(j, carry, diagonal):
            m, l, acc = carry
            off = pl.multiple_of(j * t, t)
            k = k_ref[pl.ds(off, t), :]
            v = v_ref[pl.ds(off, t), :]
            s = lax.dot_general(q, k, _NT, preferred_element_type=F32) * scale - cr_ref[:, pl.ds(off, t)]
            if diagonal:
                s = jnp.where(col <= row, s, NEG)
            m_new = jnp.maximum(m, jnp.max(s, axis=-1, keepdims=True))
            alpha = jnp.exp(m - m_new)
            p = jnp.exp(s - m_new)
            l = alpha * l + jnp.sum(p, axis=-1, keepdims=True)
            acc = alpha * acc + lax.dot_general(p.astype(BF16), v, _NN, preferred_element_type=F32)
            return m_new, l, acc

        m0 = jnp.full((t, 1), NEG, F32)
        below = lax.fori_loop(0, i, functools.partial(step, diagonal=False),
                              (m0, jnp.zeros((t, 1), F32), jnp.zeros((t, HEAD), F32)))
        m, l, acc = step(i, below, diagonal=True)
        o = acc / l
        o_ref[...] = o
        o16_ref[...] = o.astype(BF16)
        lse_ref[...] = cq + m + jnp.log(l)

    return _hosted_call(
        body, rider, name="fox_fwd", grid=(B, H, nq),
        in_specs=[pl.BlockSpec((t, HEAD), lambda b, h, i: (b * nq + i, 3 * h)),
                  pl.BlockSpec((S, HEAD), lambda b, h, i: (b, 3 * h + 1)),
                  pl.BlockSpec((S, HEAD), lambda b, h, i: (b, 3 * h + 2)),
                  pl.BlockSpec((None, None, t, 1), lambda b, h, i: (b, h, i, 0)),
                  pl.BlockSpec((None, None, 1, S), lambda b, h, i: (b, h, 0, 0))],
        out_specs=[pl.BlockSpec((t, HEAD), lambda b, h, i: (b * nq + i, h)),
                   pl.BlockSpec((t, HEAD), lambda b, h, i: (b * nq + i, h)),
                   pl.BlockSpec((None, None, t, 1), lambda b, h, i: (b, h, i, 0))],
        out_shape=[jax.ShapeDtypeStruct((T, H * HEAD), F32), jax.ShapeDtypeStruct((T, H * HEAD), BF16),
                   jax.ShapeDtypeStruct((B, H, S, 1), F32)],
        scratch_shapes=[], semantics=("parallel", "parallel", "arbitrary"),
    )(p_fox, p_fox, p_fox, c_col, c_row)


def fox_bwd(p_fox, c_col, c_row, o, lse, do, B, S, H, rider=None):
    T = B * S
    t = _tile(S, ATT_TILE)
    n = S // t
    scale = HEAD ** -0.5

    def body(q_ref, k_ref, v_ref, cq_ref, cr_ref, o_ref, lse_ref, do_ref, dqkv_ref, dcq_ref, dcr_ref,
             dq_acc, delta_s, lse_s):
        row = lax.broadcasted_iota(jnp.int32, (t, t), 0)
        col = lax.broadcasted_iota(jnp.int32, (t, t), 1)

        def prep(i, c):
            rows = pl.ds(pl.multiple_of(i * t, t), t)
            delta_s[rows, :] = jnp.sum(do_ref[rows, :] * o_ref[rows, :], axis=-1, keepdims=True)
            lse_s[rows, :] = lse_ref[rows, :] - cq_ref[rows, :]
            dq_acc[rows, :] = jnp.zeros((t, HEAD), F32)
            dcq_ref[rows, :] = jnp.zeros((t, 1), F32)
            return c

        lax.fori_loop(0, n, prep, 0)

        def kv_step(j, c):
            joff = pl.multiple_of(j * t, t)
            k = k_ref[pl.ds(joff, t), :]
            v = v_ref[pl.ds(joff, t), :]
            crj = cr_ref[:, pl.ds(joff, t)]

            def q_step(i, carry, diagonal):
                dk, dv, dc = carry
                rows = pl.ds(pl.multiple_of(i * t, t), t)
                q = q_ref[rows, :]
                dob = do_ref[rows, :].astype(BF16)
                s = lax.dot_general(q, k, _NT, preferred_element_type=F32) * scale - crj
                if diagonal:
                    s = jnp.where(col <= row, s, NEG)
                p = jnp.exp(s - lse_s[rows, :])
                dp = lax.dot_general(dob, v, _NT, preferred_element_type=F32)
                ds = p * (dp - delta_s[rows, :])
                dsb = ds.astype(BF16)
                dv = dv + lax.dot_general(p.astype(BF16), dob, _TN, preferred_element_type=F32)
                dk = dk + lax.dot_general(dsb, q, _TN, preferred_element_type=F32)
                dq_acc[rows, :] += lax.dot_general(dsb, k, _NN, preferred_element_type=F32) * scale
                dc = dc + jnp.sum(ds, axis=0, keepdims=True)
                dcq_ref[rows, :] += jnp.sum(ds, axis=-1, keepdims=True)
                return dk, dv, dc

            z = jnp.zeros((t, HEAD), F32)
            on_diagonal = q_step(j, (z, z, jnp.zeros((1, t), F32)), diagonal=True)
            dk, dv, dc = lax.fori_loop(j + 1, n, functools.partial(q_step, diagonal=False), on_diagonal)
            dqkv_ref[pl.ds(joff, t), HEAD:2 * HEAD] = (dk * scale).astype(BF16)
            dqkv_ref[pl.ds(joff, t), 2 * HEAD:3 * HEAD] = dv.astype(BF16)
            dcr_ref[:, pl.ds(joff, t)] = -dc
            return c

        lax.fori_loop(0, n, kv_step, 0)
        dqkv_ref[:, 0:HEAD] = dq_acc[...].astype(BF16)

    col_spec = pl.BlockSpec((None, None, S, 1), lambda b, h: (b, h, 0, 0))
    row_spec = pl.BlockSpec((None, None, 1, S), lambda b, h: (b, h, 0, 0))
    head = pl.BlockSpec((S, HEAD), lambda b, h: (b, h))
    return _hosted_call(
        body, rider, name="fox_bwd", grid=(B, H),
        in_specs=[pl.BlockSpec((S, HEAD), lambda b, h: (b, 3 * h)),
                  pl.BlockSpec((S, HEAD), lambda b, h: (b, 3 * h + 1)),
                  pl.BlockSpec((S, HEAD), lambda b, h: (b, 3 * h + 2)),
                  col_spec, row_spec, head, col_spec, head],
        out_specs=[pl.BlockSpec((S, 3 * HEAD), lambda b, h: (b, h)), col_spec, row_spec],
        out_shape=[jax.ShapeDtypeStruct((T, 3 * H * HEAD), BF16), jax.ShapeDtypeStruct((B, H, S, 1), F32),
                   jax.ShapeDtypeStruct((B, H, 1, S), F32)],
        scratch_shapes=[pltpu.VMEM((S, HEAD), F32), pltpu.VMEM((S, 1), F32), pltpu.VMEM((S, 1), F32)],
        semantics=("parallel", "parallel"),
    )(p_fox, p_fox, p_fox, c_col, c_row, o, lse, do)


def _small_fn(x, b0, b1, H):
    S = x.shape[0]
    lane = lax.broadcasted_iota(jnp.int32, x.shape, 1)
    z = x + b0
    tail = jnp.log1p(jnp.exp(-jnp.abs(z)))
    softplus = jnp.maximum(z, 0.0) + tail
    logsig = -(jnp.maximum(-z, 0.0) + tail)
    g = -jnp.exp(b1) * softplus
    pre = jnp.where(lane < H, logsig, jnp.where(lane < 2 * H, g, 0.0))
    bl = _tile(S, 256, CHUNK)
    r = lax.broadcasted_iota(jnp.int32, (bl, bl), 0)
    c = lax.broadcasted_iota(jnp.int32, (bl, bl), 1)
    tri = (r >= c).astype(F32)
    tri_chunk = jnp.where((r >= c) & (jnp.right_shift(r, 6) == jnp.right_shift(c, 6)), 1.0, 0.0)
    carry = jnp.zeros((1, x.shape[1]), F32)
    parts = []
    for i in range(S // bl):
        blk = pre[i * bl:(i + 1) * bl, :]
        full = lax.dot_general(tri, blk, _NN, precision=EXACT, preferred_element_type=F32) + carry
        chunked = lax.dot_general(tri_chunk, blk, _NN, precision=EXACT, preferred_element_type=F32)
        parts.append(jnp.where(lane[:bl] < H, full, chunked))
        carry = carry + jnp.sum(blk, axis=0, keepdims=True)
    cum = parts[0] if len(parts) == 1 else jnp.concatenate(parts, axis=0)
    return jnp.where(lane < 2 * H, cum, jnp.where(lane < 3 * H, _sigmoid(x), 0.0))


def small_fwd(p_small, prm, B, S, H):
    T = B * S

    def body(x_ref, p_ref, o_ref):
        o_ref[...] = _small_fn(x_ref[...], p_ref[0:1, :], p_ref[1:2, :], H)

    blk = pl.BlockSpec((S, 128), lambda b: (b, 0))
    return pl.pallas_call(
        body, name="small_fwd", grid=(B,), in_specs=[blk, pl.BlockSpec((8, 128), lambda b: (0, 0))], out_specs=blk,
        out_shape=jax.ShapeDtypeStruct((T, 128), F32), compiler_params=_params("parallel"),
    )(p_small, prm)


def small_bwd(p_small, prm, d_out, B, S, H):
    T = B * S

    def body(x_ref, p_ref, d_ref, dx_ref, dp_ref):
        @pl.when(pl.program_id(0) == 0)
        def _():
            dp_ref[...] = jnp.zeros_like(dp_ref)

        _, vjp = jax.vjp(functools.partial(_small_fn, H=H), x_ref[...], p_ref[0:1, :], p_ref[1:2, :])
        dx, db0, db1 = vjp(d_ref[...])
        dx_ref[...] = dx.astype(BF16)
        dp_ref[0:1, :] += db0
        dp_ref[1:2, :] += db1

    blk = pl.BlockSpec((S, 128), lambda b: (b, 0))
    pblk = pl.BlockSpec((8, 128), lambda b: (0, 0))
    return pl.pallas_call(
        body, name="small_bwd", grid=(B,), in_specs=[blk, pblk, blk], out_specs=[blk, pblk],
        out_shape=[jax.ShapeDtypeStruct((T, 128), BF16), jax.ShapeDtypeStruct((8, 128), F32)],
        compiler_params=_params("arbitrary"),
    )(p_small, prm, d_out)


def gdn_prep_fwd(p_gqkv, cw, B, S, H):
    T = B * S

    def body(x_ref, w_ref, o_ref):
        for part in range(3):
            cols = slice(part * HEAD, (part + 1) * HEAD)
            y = _conv_fwd(x_ref[:, cols], w_ref.at[:, cols], GDN_CONV)
            a = y * _sigmoid(y)
            if part < 2:
                a = a * lax.rsqrt(jnp.sum(a * a, axis=-1, keepdims=True) + EPS)
            o_ref[:, cols] = a

    blk = pl.BlockSpec((S, 3 * HEAD), lambda b, h: (b, h))
    wblk = pl.BlockSpec((GDN_CONV, 3 * HEAD), lambda b, h: (0, h))
    return pl.pallas_call(
        body, name="gdn_prep_fwd", grid=(B, H), in_specs=[blk, wblk], out_specs=blk,
        out_shape=jax.ShapeDtypeStruct((T, 3 * H * HEAD), F32), compiler_params=_params("parallel", "parallel"),
    )(p_gqkv, cw)


def gdn_prep_bwd(p_gqkv, cw, d_out, B, S, H):
    T = B * S

    def body(x_ref, w_ref, d_ref, dx_ref, dw_ref):
        @pl.when(pl.program_id(1) == 0)
        def _():
            dw_ref[...] = jnp.zeros_like(dw_ref)

        x = x_ref[...]
        y, shifted = _conv_fwd(x, w_ref, GDN_CONV, keep_shifted=True)
        sg = _sigmoid(y)
        a = y * sg
        rs = lax.rsqrt(jnp.sum(a * a, axis=-1, keepdims=True) + EPS)
        d = d_ref[...]
        out = a * rs
        da_qk = rs * (d - out * jnp.sum(d * out, axis=-1, keepdims=True))
        is_qk = (pl.program_id(0) % 3) < 2
        da = jnp.where(is_qk, da_qk, d)
        dy = da * (sg + y * sg * (1.0 - sg))
        dx, dws = _conv_bwd(x, shifted, dy, w_ref, GDN_CONV)
        dx_ref[...] = dx.astype(BF16)
        for i in range(GDN_CONV):
            dw_ref[i:i + 1, :] += dws[i]

    blk = pl.BlockSpec((S, HEAD), lambda n, b: (b, n))
    wblk = pl.BlockSpec((GDN_CONV, HEAD), lambda n, b: (0, n))
    return pl.pallas_call(
        body, name="gdn_prep_bwd", grid=(3 * H, B), in_specs=[blk, wblk, blk], out_specs=[blk, wblk],
        out_shape=[jax.ShapeDtypeStruct((T, 3 * H * HEAD), BF16), jax.ShapeDtypeStruct((GDN_CONV, 3 * H * HEAD), F32)],
        compiler_params=_params("parallel", "arbitrary"),
    )(p_gqkv, cw, d_out)


@jax.custom_vjp
def _given_inverse(a, t):
    return t


def _given_inverse_fwd(a, t):
    return t, t


def _given_inverse_bwd(t, g):
    x = _dg(t, g, _TN, True)
    return -_dg(x, t, _NT, True), jnp.zeros_like(t)


_given_inverse.defvjp(_given_inverse_fwd, _given_inverse_bwd)


def _to_col(row):
    n = row.shape[1]
    r = lax.broadcasted_iota(jnp.int32, (n, n), 0)
    c = lax.broadcasted_iota(jnp.int32, (n, n), 1)
    return jnp.sum(jnp.where(r == c, row, 0.0), axis=1, keepdims=True)


def _intra_fn(k, v, beta_r, gcr, ops, t_known=None):
    n = len(k)
    m = k[0].shape[0]
    r = lax.broadcasted_iota(jnp.int32, (m, m), 0)
    c = lax.broadcasted_iota(jnp.int32, (m, m), 1)
    below = (r > c) & (jnp.right_shift(r, 6) == jnp.right_shift(c, 6))
    beta = [_to_col(beta_r[i]) for i in range(n)]
    gcc = [_to_col(gcr[i]) for i in range(n)]
    decay = [jnp.exp(jnp.where(below, gcc[i] - gcr[i], NEG)) for i in range(n)]
    kb = [k[i] * beta[i] for i in range(n)]
    a = [ops.nt(kb[i], k[i]) * decay[i] for i in range(n)]
    if t_known is None:
        p = [-a[i] for i in range(n)]
        tm = [jnp.where(r == c, 1.0, 0.0) + p[i] for i in range(n)]
        for _ in range(5):
            p = [ops.nn(p[i], p[i], hi=True) for i in range(n)]
            tm = [tm[i] + ops.nn(tm[i], p[i], hi=True) for i in range(n)]
    else:
        tm = [_given_inverse(a[i], t_known[i]) for i in range(n)]
    both = [ops.nn(tm[i], jnp.concatenate([v[i] * beta[i], kb[i] * jnp.exp(gcc[i])], axis=1), hi=True) for i in range(n)]
    u_hat = [both[i][:, :HEAD] for i in range(n)]
    w = [both[i][:, HEAD:] for i in range(n)]
    return tuple(u_hat), tuple(w), tuple(tm)


INTRA_NB = 32
PAIR = 1


def gdn_intra_fwd(qkvn, betar5, gcr5, B, S, H, rider=None):
    T = B * S
    UNIT = PAIR * CHUNK
    N = S // UNIT
    nb = min(INTRA_NB // PAIR, N)
    rows = nb * UNIT
    ns = N // nb

    def body(k_ref, v_ref, b_ref, gr_ref, uh_ref, w_ref, t_ref):
        sls = [slice(ci * UNIT, (ci + 1) * UNIT) for ci in range(nb)]
        u_hat, w, tm = _intra_fn(tuple(k_ref[sl, :] for sl in sls), tuple(v_ref[sl, :] for sl in sls),
                                 tuple(b_ref[ci] for ci in range(nb)), tuple(gr_ref[ci] for ci in range(nb)), _RawOps)
        for ci, sl in enumerate(sls):
            uh_ref[sl, :] = u_hat[ci]
            w_ref[sl, :] = w[ci]
            t_ref[ci] = tm[ci]

    rowspec = pl.BlockSpec((None, None, nb, 1, UNIT), lambda b, h, i: (b, h, i, 0, 0))
    sqspec = pl.BlockSpec((None, None, nb, UNIT, UNIT), lambda b, h, i: (b, h, i, 0, 0))
    out = pl.BlockSpec((rows, HEAD), lambda b, h, i: (b * ns + i, h))
    return _hosted_call(
        body, rider, name="gdn_intra_fwd", grid=(B, H, ns),
        in_specs=[pl.BlockSpec((rows, HEAD), lambda b, h, i: (b * ns + i, 3 * h + 1)),
                  pl.BlockSpec((rows, HEAD), lambda b, h, i: (b * ns + i, 3 * h + 2)),
                  rowspec, rowspec],
        out_specs=[out, out, sqspec],
        out_shape=[jax.ShapeDtypeStruct((T, H * HEAD), F32), jax.ShapeDtypeStruct((T, H * HEAD), F32),
                   jax.ShapeDtypeStruct((B, H, N, UNIT, UNIT), F32)],
        scratch_shapes=[], semantics=("parallel", "parallel", "parallel"),
    )(qkvn, qkvn, betar5, gcr5)


def gdn_intra_bwd(qkvn, betar5, gcr5, t_inv, d_uh, d_w, dq_in, dk_in, B, S, H):
    T = B * S
    UNIT = PAIR * CHUNK
    N = S // UNIT
    nb = min(INTRA_NB // PAIR, N)
    rows = nb * UNIT
    ns = N // nb

    def body(k_ref, v_ref, b_ref, gr_ref, t_ref, duh_ref, dw_ref, dq_ref, dk_ref, o_ref, db_ref, dgr_ref):
        sls = [slice(ci * UNIT, (ci + 1) * UNIT) for ci in range(nb)]
        chunks = range(nb)
        _, vjp = jax.vjp(
            functools.partial(_intra_fn, ops=_DiffOps, t_known=tuple(t_ref[ci] for ci in chunks)),
            tuple(k_ref[sl, :] for sl in sls), tuple(v_ref[sl, :] for sl in sls), tuple(b_ref[ci] for ci in chunks),
            tuple(gr_ref[ci] for ci in chunks))
        zero = jnp.zeros((UNIT, UNIT), F32)
        dk, dv, db, dgr = vjp((tuple(duh_ref[sl, :] for sl in sls), tuple(dw_ref[sl, :] for sl in sls),
                               tuple(zero for _ in chunks)))
        for ci, sl in enumerate(sls):
            o_ref[sl, 0:HEAD] = dq_ref[sl, :]
            o_ref[sl, HEAD:2 * HEAD] = dk[ci] + dk_ref[sl, :]
            o_ref[sl, 2 * HEAD:3 * HEAD] = dv[ci]
            db_ref[ci] = db[ci]
            dgr_ref[ci] = dgr[ci]

    rowspec = pl.BlockSpec((None, None, nb, 1, UNIT), lambda b, h, i: (b, h, i, 0, 0))
    sqspec = pl.BlockSpec((None, None, nb, UNIT, UNIT), lambda b, h, i: (b, h, i, 0, 0))
    head = pl.BlockSpec((rows, HEAD), lambda b, h, i: (b * ns + i, h))
    return pl.pallas_call(
        body, name="gdn_intra_bwd", grid=(B, H, ns),
        in_specs=[pl.BlockSpec((rows, HEAD), lambda b, h, i: (b * ns + i, 3 * h + 1)),
                  pl.BlockSpec((rows, HEAD), lambda b, h, i: (b * ns + i, 3 * h + 2)),
                  rowspec, rowspec, sqspec, head, head, head, head],
        out_specs=[pl.BlockSpec((rows, 3 * HEAD), lambda b, h, i: (b * ns + i, h)), rowspec, rowspec],
        out_shape=[jax.ShapeDtypeStruct((T, 3 * H * HEAD), F32),
                   jax.ShapeDtypeStruct((B, H, N, 1, UNIT), F32), jax.ShapeDtypeStruct((B, H, N, 1, UNIT), F32)],
        compiler_params=_params("parallel", "parallel", "parallel"),
    )(qkvn, qkvn, betar5, gcr5, t_inv, d_uh, d_w, dq_in, dk_in)


def _inter_fn(q, k, u_hat, w, gcr, state, ops):
    n = len(q)
    r = lax.broadcasted_iota(jnp.int32, (CHUNK, CHUNK), 0)
    c = lax.broadcasted_iota(jnp.int32, (CHUNK, CHUNK), 1)
    last = lax.broadcasted_iota(jnp.int32, (1, CHUNK), 1) == CHUNK - 1
    gcc = [_to_col(gcr[i]) for i in range(n)]
    gl = [jnp.sum(jnp.where(last, gcr[i], 0.0), axis=1, keepdims=True) for i in range(n)]
    decay = [jnp.exp(jnp.where(r >= c, gcc[i] - gcr[i], NEG)) for i in range(n)]
    qs = [q[i] * (HEAD ** -0.5) for i in range(n)]
    ws = [ops.nn(w[i], state[i]) for i in range(n)]
    qst = [ops.nn(qs[i] * jnp.exp(gcc[i]), state[i]) for i in range(n)]
    attn = [ops.nt(qs[i], k[i]) * decay[i] for i in range(n)]
    u = [u_hat[i] - ws[i] for i in range(n)]
    o = [qst[i] + ops.nn(attn[i], u[i]) for i in range(n)]
    kdu = [ops.tn(k[i] * jnp.exp(gl[i] - gcc[i]), u[i]) for i in range(n)]
    new_state = [state[i] * jnp.exp(gl[i]) + kdu[i] for i in range(n)]
    return tuple(o), tuple(new_state)


INTER_HEADS = 8
INTER_ROWS = 512
INTER_ROWS_BWD = 256


def _inter_heads(H):
    return INTER_HEADS if H % INTER_HEADS == 0 else (4 if H % 4 == 0 else 1)


def _inter_specs(ts, ns, hp, backward):
    at = (lambda s: ns - 1 - s) if backward else (lambda s: s)
    nc = ts // CHUNK
    qk = []
    for hh in range(hp):
        qk.append(pl.BlockSpec((ts, HEAD), lambda b, g, s, hh=hh: (b * ns + at(s), 3 * (hp * g + hh))))
        qk.append(pl.BlockSpec((ts, HEAD), lambda b, g, s, hh=hh: (b * ns + at(s), 3 * (hp * g + hh) + 1)))
    heads = pl.BlockSpec((ts, hp * HEAD), lambda b, g, s: (b * ns + at(s), g))
    rowspec = pl.BlockSpec((None, hp, nc, 1, CHUNK), lambda b, g, s: (b, g, at(s), 0, 0))
    stspec = pl.BlockSpec((None, hp, nc, HEAD, HEAD), lambda b, g, s: (b, g, at(s), 0, 0))
    return qk, heads, rowspec, stspec


def gdn_inter_fwd(qkvn, u_hat, w, gcr5, p_gz, gnorm, B, S, H):
    T = B * S
    N = S // CHUNK
    hp = _inter_heads(H)
    hs = range(hp)
    ts = _tile(S, INTER_ROWS, CHUNK)
    ns, nc = S // ts, ts // CHUNK

    def body(*refs):
        qk_refs, (uh_ref, w_ref, gr_ref, z_ref, gn_ref, o_ref, st_ref, y_ref, s_scr) = refs[:2 * hp], refs[2 * hp:]

        @pl.when(pl.program_id(2) == 0)
        def _():
            s_scr[...] = jnp.zeros_like(s_scr)

        gn = gn_ref[...]

        def step(n, c):
            rows = pl.ds(pl.multiple_of(n * CHUNK, CHUNK), CHUNK)
            st = tuple(s_scr[hh] for hh in hs)
            for hh in hs:
                st_ref[hh, n] = st[hh]
            o, new = _inter_fn(tuple(qk_refs[2 * hh][rows, :] for hh in hs), tuple(qk_refs[2 * hh + 1][rows, :] for hh in hs),
                               tuple(uh_ref[rows, hh * HEAD:(hh + 1) * HEAD] for hh in hs),
                               tuple(w_ref[rows, hh * HEAD:(hh + 1) * HEAD] for hh in hs),
                               tuple(gr_ref[hh, n] for hh in hs), st, _RawOps)
            for hh in hs:
                cols = slice(hh * HEAD, (hh + 1) * HEAD)
                o_ref[rows, cols] = o[hh]
                s_scr[hh] = new[hh]
                z = z_ref[rows, cols]
                r = lax.rsqrt(jnp.mean(o[hh] * o[hh], axis=-1, keepdims=True) + EPS)
                y_ref[rows, cols] = (o[hh] * r * gn * z * _sigmoid(z)).astype(BF16)
            return c

        lax.fori_loop(0, nc, step, 0)

    qk, heads, rowspec, stspec = _inter_specs(ts, ns, hp, backward=False)
    return pl.pallas_call(
        body, name="gdn_inter_fwd", grid=(B, H // hp, ns),
        in_specs=qk + [heads, heads, rowspec, heads, pl.BlockSpec((1, HEAD), lambda b, g, s: (0, 0))],
        out_specs=[heads, stspec, heads],
        out_shape=[jax.ShapeDtypeStruct((T, H * HEAD), F32), jax.ShapeDtypeStruct((B, H, N, HEAD, HEAD), F32),
                   jax.ShapeDtypeStruct((T, H * HEAD), BF16)],
        scratch_shapes=[pltpu.VMEM((hp, HEAD, HEAD), F32)],
        compiler_params=_params("parallel", "parallel", "arbitrary"),
    )(*([qkvn] * (2 * hp)), u_hat, w, gcr5, p_gz, gnorm)


def gdn_inter_bwd(qkvn, u_hat, w, gcr5, states, o, p_gz, gnorm, d_y, B, S, H, rider=None):
    T = B * S
    N = S // CHUNK
    hp = _inter_heads(H)
    hs = range(hp)
    ts = _tile(S, INTER_ROWS_BWD, CHUNK)
    ns, nc = S // ts, ts // CHUNK

    def body(*refs):
        qk_refs = refs[:2 * hp]
        (uh_ref, w_ref, gr_ref, st_ref, o_ref, z_ref, gn_ref, dy_ref,
         dq_ref, dk_ref, duh_ref, dw_ref, dgr_ref, dz_ref, dgn_ref, ds_scr) = refs[2 * hp:]

        @pl.when(pl.program_id(2) == 0)
        def _():
            ds_scr[...] = jnp.zeros_like(ds_scr)
            dgn_ref[...] = jnp.zeros_like(dgn_ref)

        cols = [slice(hh * HEAD, (hh + 1) * HEAD) for hh in hs]
        gn = gn_ref[...]

        def through_norm(rows, hh):
            ov, z, d = o_ref[rows, cols[hh]], z_ref[rows, cols[hh]], dy_ref[rows, cols[hh]]
            r = lax.rsqrt(jnp.mean(ov * ov, axis=-1, keepdims=True) + EPS)
            xh = ov * r
            sg = _sigmoid(z)
            d_n = d * (z * sg)
            dz_ref[rows, cols[hh]] = (d * xh * gn * (sg + z * sg * (1.0 - sg))).astype(BF16)
            dgn_ref[0:1, :] += jnp.sum(d_n * xh, axis=0, keepdims=True)
            dxh = d_n * gn
            return r * (dxh - xh * jnp.mean(dxh * xh, axis=-1, keepdims=True))

        def step(i, c):
            n = nc - 1 - i
            rows = pl.ds(pl.multiple_of(n * CHUNK, CHUNK), CHUNK)
            _, vjp = jax.vjp(functools.partial(_inter_fn, ops=_DiffOps),
                             tuple(qk_refs[2 * hh][rows, :] for hh in hs), tuple(qk_refs[2 * hh + 1][rows, :] for hh in hs),
                             tuple(uh_ref[rows, cols[hh]] for hh in hs), tuple(w_ref[rows, cols[hh]] for hh in hs),
                             tuple(gr_ref[hh, n] for hh in hs), tuple(st_ref[hh, n] for hh in hs))
            dq, dk, duh, dw, dgr, ds = vjp((tuple(through_norm(rows, hh) for hh in hs), tuple(ds_scr[hh] for hh in hs)))
            for hh in hs:
                dq_ref[rows, cols[hh]] = dq[hh]
                dk_ref[rows, cols[hh]] = dk[hh]
                duh_ref[rows, cols[hh]] = duh[hh]
                dw_ref[rows, cols[hh]] = dw[hh]
                dgr_ref[hh, n] = dgr[hh]
                ds_scr[hh] = ds[hh]
            return c

        lax.fori_loop(0, nc, step, 0)

    qk, heads, rowspec, stspec = _inter_specs(ts, ns, hp, backward=True)
    hshape = jax.ShapeDtypeStruct((T, H * HEAD), F32)
    return _hosted_call(
        body, rider, name="gdn_inter_bwd", grid=(B, H // hp, ns),
        in_specs=qk + [heads, heads, rowspec, stspec, heads, heads, pl.BlockSpec((1, HEAD), lambda b, g, s: (0, 0)), heads],
        out_specs=[heads, heads, heads, heads, rowspec, heads,
                   pl.BlockSpec((None, None, 8, HEAD), lambda b, g, s: (b, g, 0, 0))],
        out_shape=[hshape, hshape, hshape, hshape, jax.ShapeDtypeStruct((B, H, N, 1, CHUNK), F32),
                   jax.ShapeDtypeStruct((T, H * HEAD), BF16), jax.ShapeDtypeStruct((B, H // hp, 8, HEAD), F32)],
        scratch_shapes=[pltpu.VMEM((hp, HEAD, HEAD), F32)], semantics=("parallel", "parallel", "arbitrary"),
    )(*([qkvn] * (2 * hp)), u_hat, w, gcr5, states, o, p_gz, gnorm, d_y)


def adamw(w, g, m, v, name):
    shape = w.shape
    lead = (None,) * (w.ndim - 2)
    zeros = (0,) * (w.ndim - 2)
    R, C = shape[-2:]
    g2 = g.reshape(R, C)
    tr, tc = _tile(R, 128, 8), C
    if tr % 8 and R > 8:
        tr, tc = R, _tile(C, 128)

    def body(w_ref, g_ref, m_ref, v_ref, d_ref, nm_ref, nv_ref):
        gv = g_ref[...]
        nm = ADAM_B1 * m_ref[...] + (1.0 - ADAM_B1) * gv
        nv = ADAM_B2 * v_ref[...] + (1.0 - ADAM_B2) * (gv * gv)
        m_hat = nm / (1.0 - ADAM_B1 ** ADAM_STEP)
        v_hat = nv / (1.0 - ADAM_B2 ** ADAM_STEP)
        d_ref[...] = -ADAM_LR * (m_hat / (jnp.sqrt(v_hat) + ADAM_EPS) + ADAM_WD * w_ref[...])
        nm_ref[...] = nm
        nv_ref[...] = nv

    blk = pl.BlockSpec(lead + (tr, tc), lambda i, j: zeros + (i, j))
    gblk = pl.BlockSpec((tr, tc), lambda i, j: (i, j))
    sh = jax.ShapeDtypeStruct(shape, F32)
    return pl.pallas_call(
        body, name=name, grid=(R // tr, C // tc), in_specs=[blk, gblk, blk, blk], out_specs=[blk] * 3, out_shape=[sh] * 3,
        compiler_params=_params("parallel", "parallel"),
    )(w, g2, m, v)


def _place():
    x, y, c = lax.axis_index("x"), lax.axis_index("y"), lax.axis_index("c")
    chips = [(1 - x, y), (x, 1 - y), (1 - x, 1 - y)]
    return x, y, c, chips


_HBM = pl.BlockSpec(memory_space=pltpu.HBM)


def allgather_weights(packs):
    n = len(packs)

    def body(*refs):
        in_refs, out_refs, (send_sems, recv_sems) = refs[:n], refs[n:2 * n], refs[2 * n:]
        x, y, c, chips = _place()
        me_s = 2 * x + y
        me, sibling = (x, y, c), (x, y, 1 - c)
        shards = [2 * chip[0] + chip[1] for chip in chips]

        def copy(a, k, shard, half, to, src=None):
            dst = out_refs[a].at[shard, half]
            return pltpu.make_async_remote_copy(src_ref=dst if src is None else src, dst_ref=dst,
                                                send_sem=send_sems.at[6 * a + k], recv_sem=recv_sems.at[6 * a + k],
                                                device_id=to, device_id_type=MESH)

        first = [copy(a, j, me_s, c, (*chip, c), src=in_refs[a].at[c]) for a in range(n) for j, chip in enumerate(chips)]
        for cp in first:
            cp.start()
        passed = []
        for a in range(n):
            for j in range(3):
                copy(a, j, shards[j], c, me).wait_recv()
                passed.append(copy(a, 3 + j, shards[j], c, sibling))
                passed[-1].start()
        for a in range(n):
            for j in range(3):
                copy(a, 3 + j, shards[j], 1 - c, me).wait_recv()
        for cp in first + passed:
            cp.wait_send()

    return pl.pallas_call(
        body, name="allgather_weights", in_specs=[_HBM] * n, out_specs=[_HBM] * n,
        out_shape=[jax.ShapeDtypeStruct((N_CHIP,) + p.shape, p.dtype) for p in packs],
        scratch_shapes=[pltpu.SemaphoreType.DMA((6 * n,)), pltpu.SemaphoreType.DMA((6 * n,))],
    )(*packs)


class _Rider:
    def __init__(self, inputs, out_shapes, n_sems, sends, recvs, aliases=None):
        self.inputs, self.out_shapes, self.n_sems = list(inputs), list(out_shapes), n_sems
        self.sends, self.recvs, self.aliases = sends, recvs, aliases or {}

    def start(self, *refs):
        for cp in self.sends(*refs):
            cp.start()

    def wait(self, *refs):
        for cp in self.recvs(*refs):
            cp.wait_recv()
        for cp in self.sends(*refs):
            cp.wait_send()


def _remote(src, dst, send_sems, recv_sems, k, to):
    return pltpu.make_async_remote_copy(src_ref=src, dst_ref=dst, send_sem=send_sems.at[k], recv_sem=recv_sems.at[k],
                                        device_id=to, device_id_type=MESH)


def _run_alone(rider, name):
    ri = len(rider.inputs)

    def body(*refs):
        ins, outs, (send_sems, recv_sems) = refs[:ri], refs[ri:-2], refs[-2:]
        rider.start(ins, outs, send_sems, recv_sems)
        rider.wait(ins, outs, send_sems, recv_sems)

    return pl.pallas_call(
        body, name=name, in_specs=[_HBM] * ri, out_specs=[_HBM] * len(rider.out_shapes), out_shape=rider.out_shapes,
        scratch_shapes=[pltpu.SemaphoreType.DMA((rider.n_sems,))] * 2, input_output_aliases=rider.aliases,
    )(*rider.inputs)


def _hosted_call(body, rider, *, name, grid, in_specs, out_specs, out_shape, scratch_shapes, semantics):
    if rider is None:
        return pl.pallas_call(body, name=name, grid=grid, in_specs=in_specs, out_specs=out_specs, out_shape=out_shape,
                              scratch_shapes=scratch_shapes, compiler_params=_params(*semantics))
    n_in, n_out, n_scr = len(in_specs), len(out_specs), len(scratch_shapes)
    ri, ro = len(rider.inputs), len(rider.out_shapes)

    def hosted(*refs):
        parts, p = [], 0
        for cnt in (n_in, ri, n_out, ro, n_scr, 2):
            parts.append(refs[p:p + cnt])
            p += cnt
        ins, rins, outs, routs, scr, (send_sems, recv_sems) = parts
        first = functools.reduce(jnp.logical_and, [pl.program_id(a) == 0 for a in range(len(grid))])
        last = functools.reduce(jnp.logical_and, [pl.program_id(a) == grid[a] - 1 for a in range(len(grid))])

        @pl.when(first)
        def _():
            rider.start(rins, routs, send_sems, recv_sems)

        body(*ins, *outs, *scr)

        @pl.when(last)
        def _():
            rider.wait(rins, routs, send_sems, recv_sems)

    call = pl.pallas_call(
        hosted, name=name, grid=grid, in_specs=list(in_specs) + [_HBM] * ri, out_specs=list(out_specs) + [_HBM] * ro,
        out_shape=list(out_shape) + rider.out_shapes,
        scratch_shapes=list(scratch_shapes) + [pltpu.SemaphoreType.DMA((rider.n_sems,))] * 2,
        input_output_aliases={n_in + i: n_out + o for i, o in rider.aliases.items()},
        compiler_params=_params(*(("arbitrary",) * len(grid))))

    def run(*args):
        res = call(*args, *rider.inputs)
        return res[:n_out], res[n_out:]

    return run


def _ride_gather_ici(packs):
    n = len(packs)

    def sends(ins, outs, send_sems, recv_sems):
        x, y, c, chips = _place()
        return [_remote(ins[a].at[c], outs[a].at[2 * x + y, c], send_sems, recv_sems, 3 * a + j, (*chip, c))
                for a in range(n) for j, chip in enumerate(chips)]

    def recvs(ins, outs, send_sems, recv_sems):
        x, y, c, chips = _place()
        return [_remote(ins[a].at[c], outs[a].at[2 * chip[0] + chip[1], c], send_sems, recv_sems, 3 * a + j, (x, y, c))
                for a in range(n) for j, chip in enumerate(chips)]

    return _Rider(packs, [jax.ShapeDtypeStruct((N_CHIP,) + p.shape, p.dtype) for p in packs], 3 * n, sends, recvs)


def _ride_gather_d2d(gathered):
    n = len(gathered)

    def copies(landing_half, to):
        def build(ins, outs, send_sems, recv_sems):
            x, y, c, chips = _place()
            return [_remote(ins[a].at[2 * chip[0] + chip[1], c], outs[a].at[2 * chip[0] + chip[1], landing_half(c)],
                            send_sems, recv_sems, 3 * a + j, to(x, y, c))
                    for a in range(n) for j, chip in enumerate(chips)]
        return build

    return _Rider(gathered, [jax.ShapeDtypeStruct(g.shape, g.dtype) for g in gathered], 3 * n,
                  copies(lambda c: c, lambda x, y, c: (x, y, 1 - c)), copies(lambda c: 1 - c, lambda x, y, c: (x, y, c)),
                  aliases={a: a for a in range(n)})


def _ride_exchange(gs):
    n = len(gs)
    shapes = [g.shape[1:] if g.ndim == 4 else g.shape[:2] + (g.shape[2] // 2,) for g in gs]

    def copies(ins, outs, send_sems, recv_sems):
        x, y, c, _ = _place()

        def theirs(a):
            if gs[a].ndim == 4:
                return ins[a].at[1 - c]
            cols = shapes[a][2]
            return ins[a].at[:, :, pl.ds((1 - c) * cols, cols)]

        return [_remote(theirs(a), outs[a], send_sems, recv_sems, a, (x, y, 1 - c)) for a in range(n)]

    return _Rider(gs, [jax.ShapeDtypeStruct(sh, g.dtype) for sh, g in zip(shapes, gs)], n, copies, copies)


def _ride_scatter(b16s, to=(0, 1, 2), landing=None):
    n = len(b16s)

    def sends(ins, outs, send_sems, recv_sems):
        x, y, c, chips = _place()
        return [_remote(ins[a].at[2 * chips[j][0] + chips[j][1]], outs[a].at[2 * x + y], send_sems, recv_sems, 3 * a + j,
                        (*chips[j], c)) for a in range(n) for j in to]

    def recvs(ins, outs, send_sems, recv_sems):
        x, y, c, chips = _place()
        return [_remote(ins[a].at[2 * x + y], outs[a].at[2 * chips[j][0] + chips[j][1]], send_sems, recv_sems, 3 * a + j,
                        (x, y, c)) for a in range(n) for j in to]

    return _Rider(list(b16s) + list(landing or []), [jax.ShapeDtypeStruct(b.shape, b.dtype) for b in b16s], 3 * n,
                  sends, recvs, aliases={n + a: a for a in range(n)} if landing else None)


def _slab_tile(r, cols):
    tr = _tile(r, 256, 16)
    if tr % 16 == 0:
        return tr, cols
    return r, _tile(cols, 128)


def add_halves(g, got, idx, name):
    ns, r, cols = got.shape
    tr, tc = _slab_tile(r, cols)
    if g.ndim == 4:
        mine = pl.BlockSpec((None, None, tr, tc), lambda s, i, j, idx_ref: (idx_ref[0], s, i, j))
    else:
        mine = pl.BlockSpec((None, tr, tc), lambda s, i, j, idx_ref: (s, i, idx_ref[0] * (cols // tc) + j))

    def body(idx_ref, a_ref, b_ref, o32_ref, o16_ref):
        s = a_ref[...] + b_ref[...]
        o32_ref[...] = s
        o16_ref[...] = s.astype(BF16)

    blk = pl.BlockSpec((None, tr, tc), lambda s, i, j, idx_ref: (s, i, j))
    return pl.pallas_call(
        body, name=name,
        grid_spec=pltpu.PrefetchScalarGridSpec(
            num_scalar_prefetch=1, grid=(ns, r // tr, cols // tc),
            in_specs=[mine, blk], out_specs=[blk, blk]),
        out_shape=[jax.ShapeDtypeStruct((ns, r, cols), F32), jax.ShapeDtypeStruct((ns, r, cols), BF16)],
        compiler_params=_params("parallel", "parallel", "parallel"),
    )(idx, g, got)


def add_chips(a32, got16, idx, name):
    ns, r, cols = a32.shape
    tr, tc = _slab_tile(r, cols)

    def body(idx_ref, a_ref, r1_ref, r2_ref, r3_ref, o_ref):
        o_ref[...] = ((a_ref[...] + r1_ref[...].astype(F32)) + r2_ref[...].astype(F32)) + r3_ref[...].astype(F32)

    def slab(k):
        return pl.BlockSpec((None, tr, tc), lambda i, j, idx_ref: ((idx_ref[1] + k) % ns, i, j))

    return pl.pallas_call(
        body, name=name,
        grid_spec=pltpu.PrefetchScalarGridSpec(
            num_scalar_prefetch=1, grid=(r // tr, cols // tc), in_specs=[slab(0), slab(1), slab(2), slab(3)],
            out_specs=pl.BlockSpec((tr, tc), lambda i, j, idx_ref: (i, j))),
        out_shape=jax.ShapeDtypeStruct((r, cols), F32),
        compiler_params=_params("parallel", "parallel"),
    )(idx, a32, got16, got16, got16)


def share_halves(halves):
    n = len(halves)

    def body(*refs):
        in_refs, out_refs, (send_sems, recv_sems) = refs[:n], refs[n:2 * n], refs[2 * n:]
        x, y, c, _ = _place()
        cps = [pltpu.make_async_remote_copy(src_ref=in_refs[a], dst_ref=out_refs[a], send_sem=send_sems.at[a],
                                            recv_sem=recv_sems.at[a], device_id=(x, y, 1 - c), device_id_type=MESH)
               for a in range(n)]
        for cp in cps:
            cp.start()
        for cp in cps:
            cp.wait()

    return pl.pallas_call(
        body, name="share_halves", in_specs=[_HBM] * n, out_specs=[_HBM] * n,
        out_shape=[jax.ShapeDtypeStruct(h.shape, F32) for h in halves],
        scratch_shapes=[pltpu.SemaphoreType.DMA((n,)), pltpu.SemaphoreType.DMA((n,))],
    )(*halves)


def allreduce_small(v):
    R, _ = v.shape

    def body(in_ref, out_ref, slots, send_sems, recv_sems):
        x, y, c, _ = _place()
        me = 4 * x + 2 * y + c
        slots[me] = in_ref[...]
        cps = []
        for k in range(1, N_DEV):
            to = (x ^ (k >> 2), y ^ ((k >> 1) & 1), c ^ (k & 1))
            cps.append(pltpu.make_async_remote_copy(src_ref=in_ref, dst_ref=slots.at[me], send_sem=send_sems.at[k - 1],
                                                    recv_sem=recv_sems.at[k - 1], device_id=to, device_id_type=MESH))
        for cp in cps:
            cp.start()
        for k in range(1, N_DEV):
            frm = 4 * (x ^ (k >> 2)) + 2 * (y ^ ((k >> 1) & 1)) + (c ^ (k & 1))
            pltpu.make_async_remote_copy(src_ref=in_ref, dst_ref=slots.at[frm], send_sem=send_sems.at[k - 1],
                                         recv_sem=recv_sems.at[k - 1], device_id=(x, y, c), device_id_type=MESH).wait_recv()
        for cp in cps:
            cp.wait_send()
        acc = slots[0]
        for d in range(1, N_DEV):
            acc = acc + slots[d]
        out_ref[...] = acc

    vm = pl.BlockSpec(memory_space=pltpu.VMEM)
    return pl.pallas_call(
        body, name="allreduce_small", in_specs=[vm], out_specs=vm, out_shape=jax.ShapeDtypeStruct((R, ROW), F32),
        scratch_shapes=[pltpu.VMEM((N_DEV, R, ROW), F32), pltpu.SemaphoreType.DMA((N_DEV - 1,)),
                        pltpu.SemaphoreType.DMA((N_DEV - 1,))],
    )(v)


def _rows_of(n, unit=16):
    return -(-n // (unit * ROW)) * unit


def _pack_rows(items, total_rows, dtype, unit=16):
    parts = []
    used = 0
    for a in items:
        flat = a.reshape(-1)
        r = _rows_of(flat.shape[0], unit)
        flat = jnp.pad(flat, (0, r * ROW - flat.shape[0]))
        parts.append(flat.reshape(r, ROW))
        used += r
    if total_rows > used:
        parts.append(jnp.zeros((total_rows - used, ROW), dtype))
    return jnp.concatenate(parts, axis=0)


def _unpack_rows(buf, shapes, unit=16):
    lead = buf.shape[:-2]
    out = []
    off = 0
    for shp in shapes:
        n = math.prod(shp)
        r = _rows_of(n, unit)
        piece = buf[..., off:off + r, :].reshape(*lead, r * ROW)[..., :n].reshape(*lead, *shp)
        out.append(piece)
        off += r
    return out


def _interleave_heads(w, H):
    lead = w.shape[:-1]
    return w.reshape(*lead, 3, H, HEAD).swapaxes(-3, -2).reshape(*lead, 3 * H * HEAD)


def _deinterleave_heads(w, H):
    lead = w.shape[:-1]
    return w.reshape(*lead, H, 3, HEAD).swapaxes(-3, -2).reshape(*lead, 3 * H * HEAD)


def _interleave_head_rows(w, H):
    return w.reshape(3, H, HEAD, w.shape[-1]).swapaxes(0, 1).reshape(3 * H * HEAD, w.shape[-1])


def _deinterleave_head_rows(w, H):
    return w.reshape(H, 3, HEAD, w.shape[-1]).swapaxes(0, 1).reshape(3 * H * HEAD, w.shape[-1])


def kernel(x, norm_mix, w_in, fox_f_bias, gdn_conv_w, gdn_a_log, gdn_dt_bias, gdn_norm, w_branch_fox, w_branch_gdn, w_out, norm_ffn, w_up, ffn_conv_w, w_down, norm_final, loss_target, m_norm_mix, m_w_in, m_fox_f_bias, m_gdn_conv_w, m_gdn_a_log, m_gdn_dt_bias, m_gdn_norm, m_w_branch_fox, m_w_branch_gdn, m_w_out, m_norm_ffn, m_w_up, m_ffn_conv_w, m_w_down, m_norm_final, v_norm_mix, v_w_in, v_fox_f_bias, v_gdn_conv_w, v_gdn_a_log, v_gdn_dt_bias, v_gdn_norm, v_w_branch_fox, v_w_branch_gdn, v_w_out, v_norm_ffn, v_w_up, v_ffn_conv_w, v_w_down, v_norm_final):
    B, S, D = x.shape
    T = B * S
    H = D // HEAD
    N = S // CHUNK
    FF = w_down.shape[1] * N_CHIP
    d_in = 9 * D + 3 * H
    assert w_in.shape[2] * N_CHIP == d_in and 3 * H <= 128

    cidx = lax.axis_index("c").astype(jnp.int32)
    sidx = (2 * lax.axis_index("x") + lax.axis_index("y")).astype(jnp.int32)
    idx = jnp.stack([cidx, sidx])

    rowed = [w_branch_fox[0], w_branch_gdn[0], w_out[0], w_down[0]]
    convs = [gdn_conv_w[0], ffn_conv_w[0]]
    rowed_shapes = [a.shape for a in rowed]
    conv_shapes = [a.shape + (2,) for a in convs]
    pad_rows = lambda shapes: -(-sum(_rows_of(math.prod(s)) for s in shapes) // 256) * 128
    Rh, Rc = pad_rows(rowed_shapes), pad_rows(conv_shapes)
    halves = lambda a: a.reshape(2, a.shape[0] // 2, a.shape[1])
    c_in = w_in.shape[2]
    packs_a = [w_in[0].T.astype(BF16).reshape(c_in, 2, D // 2).transpose(1, 0, 2),
               halves(_pack_rows([lax.bitcast_convert_type(a, BF16) for a in convs], 2 * Rc, BF16))]
    packs_b = [halves(w_up[0].astype(BF16)), halves(_pack_rows([a.astype(BF16) for a in rowed], 2 * Rh, BF16))]
    own = lambda gs, ps: [lax.dynamic_update_slice(g, p[None], (sidx, 0, 0, 0)) for g, p in zip(gs, ps)]
    by_cols = lambda g: g.transpose(1, 2, 0, 3).reshape(2 * g.shape[2], N_CHIP * g.shape[3])
    cat_cols = lambda p: jnp.concatenate([p[i] for i in range(N_CHIP)], axis=-1)
    cat_rows = lambda p: p.reshape(-1, p.shape[-1])
    g_in, g_conv = own(allgather_weights(packs_a), packs_a)
    W_inT = g_in.transpose(0, 2, 1, 3).reshape(N_CHIP * c_in, D)
    conv_parts = _unpack_rows(g_conv.reshape(N_CHIP, 2 * Rc, ROW), conv_shapes)
    gconv = cat_cols(lax.bitcast_convert_type(conv_parts[0], F32))
    fconv = cat_cols(lax.bitcast_convert_type(conv_parts[1], F32))

    o1, o2 = 3 * D, 3 * D + H
    o3, o4, o5, o6 = o2 + 3 * D, o2 + 3 * D + H, o2 + 3 * D + 2 * H, o2 + 4 * D + 2 * H
    W_foxT = _interleave_head_rows(W_inT[:o1], H)
    W_gqkvT = _interleave_head_rows(W_inT[o2:o3], H)
    W_gzT = W_inT[o5:o6]
    W_gatesT = W_inT[o6:]
    W_smallT = jnp.concatenate([W_inT[o1:o2], W_inT[o3:o5], jnp.zeros((128 - 3 * H, D), BF16)], axis=0)
    gconv_i = _interleave_heads(gconv, H)
    fconv_g, fconv_v = fconv[:, :FF], fconv[:, FF:]
    prm = jnp.zeros((8, 128), F32)
    prm = prm.at[0, 0:H].set(fox_f_bias[0]).at[0, H:2 * H].set(gdn_dt_bias[0]).at[1, H:2 * H].set(gdn_a_log[0])

    x2 = x.reshape(T, D)
    tgt = loss_target.reshape(T, D)

    hn1 = rmsnorm_fwd(x2, norm_mix, "rmsnorm_mix")
    p_fox = matmul(hn1, W_foxT, "nt", "proj_fox", out_dtype=BF16)
    p_gqkv = matmul(hn1, W_gqkvT, "nt", "proj_gqkv")
    p_gz = matmul(hn1, W_gzT, "nt", "proj_gz")
    p_gates = matmul(hn1, W_gatesT, "nt", "proj_gates")
    p_small = matmul(hn1, W_smallT, "nt", "proj_small")

    sm = small_fwd(p_small, prm, B, S, H)
    heads = lambda a: a.reshape(B, S, H).transpose(0, 2, 1)
    c_bhs, gc_bhs, beta_bhs = heads(sm[:, 0:H]), heads(sm[:, H:2 * H]), heads(sm[:, 2 * H:3 * H])
    c_col, c_row = c_bhs[..., None], c_bhs[:, :, None, :]
    gcr5 = gc_bhs.reshape(B, H, N, 1, CHUNK)
    gcr_u = gc_bhs.reshape(B, H, N // PAIR, 1, PAIR * CHUNK)
    betar_u = beta_bhs.reshape(B, H, N // PAIR, 1, PAIR * CHUNK)

    (o_fox, o_fox16, lse), arriving = fox_fwd(p_fox, c_col, c_row, B, S, H, rider=_ride_gather_ici(packs_b))
    qkvn = gdn_prep_fwd(p_gqkv, gconv_i, B, S, H)
    (u_hat, w_t, t_inv), arrived = gdn_intra_fwd(qkvn, betar_u, gcr_u, B, S, H, rider=_ride_gather_d2d(arriving))
    g_up, g_rowed = own(arrived, packs_b)
    W_up = by_cols(g_up)
    W_up_g, W_up_v = W_up[:, :FF], W_up[:, FF:]
    W_bf, W_bg, W_out, W_down = (cat_rows(p) for p in _unpack_rows(g_rowed.reshape(N_CHIP, 2 * Rh, ROW), rowed_shapes))
    o_gdn, states, y_gdn = gdn_inter_fwd(qkvn, u_hat, w_t, gcr5, p_gz, gdn_norm, B, S, H)
    bf_ = matmul(o_fox16, W_bf, "nn", "branch_fox")
    bg_, y = matmul(y_gdn, W_bg, "nn", "branch_gdn", post=_post_merge(p_gates, bf_))
    h1, hn2 = matmul(y, W_out, "nn", "out_proj", add=x2, post=_post_rmsnorm(norm_ffn))
    up_g = matmul(hn2, W_up_g, "nn", "up_gate")
    up_v = matmul(hn2, W_up_v, "nn", "up_val")
    act = ffn_gate_fwd(up_g, up_v, fconv_g, fconv_v, B, S)
    dh2, dh2_16, loss_cols, d_norm_final = matmul(act, W_down, "nn", "down_proj", add=h1,
                                                  post=_post_loss(norm_final.reshape(1, D), tgt))
    loss_here = (0.5 / D) * jnp.sum(loss_cols)

    d_act = matmul(dh2_16, W_down, "nt", "d_act")
    dW_down = matmul(act, dh2_16, "tn", "dw_down")
    d_upg, d_upv, d_fconv_g, d_fconv_v = ffn_gate_bwd(up_g, up_v, fconv_g, fconv_v, d_act, B, S)
    d_hn2 = matmul(d_upg, W_up_g, "nt", "d_hn2_g")
    dh1, dh1_16, d_norm_ffn = matmul(d_upv, W_up_v, "nt", "d_hn2_v", add=d_hn2,
                                     post=_post_rmsnorm_bwd(h1, norm_ffn, dh2, True))
    dW_up_slabs = jnp.concatenate([matmul(hn2, d_upg, "tn", "dw_up_g", slab=(D // 2, 2 * FF // N_CHIP)),
                                   matmul(hn2, d_upv, "tn", "dw_up_v", slab=(D // 2, 2 * FF // N_CHIP))], axis=1)
    d_bf, d_bg, d_gates = matmul(dh1_16, W_out, "nt", "d_y", post=_post_merge_bwd(p_gates, bf_, bg_))
    dW_out = matmul(y, dh1_16, "tn", "dw_out")
    d_ofox = matmul(d_bf, W_bf, "nt", "d_ofox")
    dW_bf = matmul(o_fox16, d_bf, "tn", "dw_bf")
    d_ygdn = matmul(d_bg, W_bg, "nt", "d_ygdn")
    dW_bg = matmul(y_gdn, d_bg, "tn", "dw_bg")

    d_fconv = jnp.concatenate([d_fconv_g, d_fconv_v], axis=1)
    col_shard = lambda g, s: g[:, s * (g.shape[1] // N_CHIP):(s + 1) * (g.shape[1] // N_CHIP)]
    row_shard = lambda g, s: g[s * (g.shape[0] // N_CHIP):(s + 1) * (g.shape[0] // N_CHIP)]
    shard_items = lambda s: [row_shard(dW_bf, s), row_shard(dW_bg, s), row_shard(dW_out, s), row_shard(dW_down, s),
                             col_shard(d_fconv, s)]
    g_shapes = [a.shape for a in shard_items(0)]
    assert sum(_rows_of(math.prod(s)) for s in g_shapes) <= 2 * Rh
    to_slabs = lambda g: g.reshape(2, g.shape[0] // 2, N_CHIP, g.shape[1] // N_CHIP).transpose(0, 2, 1, 3)
    gpacks_b = [dW_up_slabs,
                jnp.stack([_pack_rows(shard_items(s), 2 * Rh, F32).reshape(2, Rh, ROW) for s in range(N_CHIP)], axis=1)]
    (d_pfox, d_ccol, d_crow), gots_b = fox_bwd(p_fox, c_col, c_row, o_fox, lse, d_ofox, B, S, H,
                                              rider=_ride_exchange(gpacks_b))
    sums_b = [add_halves(g, got, idx, "add_halves_b%d" % i) for i, (g, got) in enumerate(zip(gpacks_b, gots_b))]

    (dq_i, dk_i, d_uh, d_wt, dgcr_a, d_gz, d_gn_parts), got16_b = gdn_inter_bwd(
        qkvn, u_hat, w_t, gcr5, states, o_gdn, p_gz, gdn_norm, d_ygdn, B, S, H,
        rider=_ride_scatter([s16 for _, s16 in sums_b]))
    d_gdn_norm = jnp.sum(d_gn_parts[:, :, 0, :], axis=(0, 1))[None]
    mine_b = [add_chips(s32, g16, idx, "add_chips_b%d" % i) for i, ((s32, _), g16) in enumerate(zip(sums_b, got16_b))]
    d_qkvn, d_betar5, dgcr_b = gdn_intra_bwd(qkvn, betar_u, gcr_u, t_inv, d_uh, d_wt, dq_i, dk_i, B, S, H)
    d_pgqkv, d_gconv_i = gdn_prep_bwd(p_gqkv, gconv_i, d_qkvn, B, S, H)

    tokens = lambda a: a.reshape(B, H, S).transpose(0, 2, 1).reshape(T, H)
    d_gc = dgcr_a.reshape(B, H, S) + dgcr_b.reshape(B, H, S)
    d_sm = jnp.concatenate([tokens(d_ccol.reshape(B, H, S) + d_crow.reshape(B, H, S)), tokens(d_gc), tokens(d_betar5.reshape(B, H, S)),
                            jnp.zeros((T, 128 - 3 * H), F32)], axis=1)
    d_psmall, d_prm = small_bwd(p_small, prm, d_sm, B, S, H)

    dW_foxT = matmul(d_pfox, hn1, "tn", "dw_fox")
    dW_gqkvT = matmul(d_pgqkv, hn1, "tn", "dw_gqkv")
    dW_gzT = matmul(d_gz, hn1, "tn", "dw_gz")
    dW_gatesT = matmul(d_gates, hn1, "tn", "dw_gates")
    dW_smallT = matmul(d_psmall, hn1, "tn", "dw_small")
    dW_inT = jnp.concatenate([_deinterleave_head_rows(dW_foxT, H), dW_smallT[0:H], _deinterleave_head_rows(dW_gqkvT, H),
                              dW_smallT[H:3 * H], dW_gzT, dW_gatesT], axis=0)
    d_gconv = _deinterleave_heads(d_gconv_i, H)

    gpack_a = [dW_inT.reshape(N_CHIP, c_in, D)]
    d_hn1, gots_a = matmul(d_pfox, W_foxT, "nn", "d_hn1_fox", rider=_ride_exchange(gpack_a))
    sums_a = [add_halves(gpack_a[0], gots_a[0], idx, "add_halves_a")]
    d_hn1, landing_a = matmul(d_pgqkv, W_gqkvT, "nn", "d_hn1_gqkv", add=d_hn1,
                              rider=_ride_scatter([sums_a[0][1]], to=(0, 1)))
    d_hn1 = matmul(d_gz, W_gzT, "nn", "d_hn1_gz", add=d_hn1)
    d_hn1, got16_a = matmul(d_gates, W_gatesT, "nn", "d_hn1_gates", add=d_hn1,
                            rider=_ride_scatter([sums_a[0][1]], to=(2,), landing=landing_a))
    mine = [add_chips(sums_a[0][0], got16_a[0], idx, "add_chips_a")] + mine_b
    grad_x, d_norm_mix = matmul(d_psmall, W_smallT, "nn", "d_hn1_small", add=d_hn1,
                                post=_post_rmsnorm_bwd(x2, norm_mix, dh1, False))

    others = share_halves(mine)
    g_w_inT, g_up, g_rows = (jnp.concatenate([jnp.where(cidx == 0, h, o), jnp.where(cidx == 0, o, h)], axis=ax)
                             for h, o, ax in zip(mine, others, (1, 0, 0)))
    g_w_in = g_w_inT.T
    g_bf, g_bg, g_out, g_down, g_fconv = _unpack_rows(g_rows, g_shapes)

    small_items = [d_norm_mix, d_norm_ffn, d_norm_final, d_gdn_norm, d_prm, d_gconv, loss_here.reshape(1, 1)]
    small_shapes = [a.shape for a in small_items]
    sv = allreduce_small(_pack_rows(small_items, 0, F32, unit=8))
    g_norm_mix, g_norm_ffn, g_norm_final, g_gdn_norm, g_prm, g_gconv_all, loss = _unpack_rows(sv, small_shapes, unit=8)
    loss = loss[0, 0]
    g_norm_final = g_norm_final.reshape(D)
    g_fbias, g_dtb, g_alog = g_prm[0:1, 0:H], g_prm[0:1, H:2 * H], g_prm[1:2, H:2 * H]
    g_gconv = lax.dynamic_slice_in_dim(g_gconv_all, sidx * (3 * D // N_CHIP), 3 * D // N_CHIP, axis=1)

    names = ["norm_mix", "w_in", "fox_f_bias", "gdn_conv_w", "gdn_a_log", "gdn_dt_bias", "gdn_norm", "w_branch_fox",
             "w_branch_gdn", "w_out", "norm_ffn", "w_up", "ffn_conv_w", "w_down", "norm_final"]
    ws = [norm_mix, w_in, fox_f_bias, gdn_conv_w, gdn_a_log, gdn_dt_bias, gdn_norm, w_branch_fox, w_branch_gdn, w_out,
          norm_ffn, w_up, ffn_conv_w, w_down, norm_final]
    ms = [m_norm_mix, m_w_in, m_fox_f_bias, m_gdn_conv_w, m_gdn_a_log, m_gdn_dt_bias, m_gdn_norm, m_w_branch_fox,
          m_w_branch_gdn, m_w_out, m_norm_ffn, m_w_up, m_ffn_conv_w, m_w_down, m_norm_final]
    vs = [v_norm_mix, v_w_in, v_fox_f_bias, v_gdn_conv_w, v_gdn_a_log, v_gdn_dt_bias, v_gdn_norm, v_w_branch_fox,
          v_w_branch_gdn, v_w_out, v_norm_ffn, v_w_up, v_ffn_conv_w, v_w_down, v_norm_final]
    gs = [g_norm_mix, g_w_in, g_fbias, g_gconv, g_alog, g_dtb, g_gdn_norm, g_bf, g_bg, g_out, g_norm_ffn, g_up,
          g_fconv, g_down, g_norm_final]
    gs = [g.reshape(w.shape) for g, w in zip(gs, ws)]
    deltas, new_ms, new_vs = [], [], []
    for nm, w, g, m, v in zip(names, ws, gs, ms, vs):
        if w.ndim == 1:
            d, a, b = adamw(w.reshape(1, -1), g.reshape(1, -1), m.reshape(1, -1), v.reshape(1, -1), "adamw_" + nm)
            d, a, b = d.reshape(w.shape), a.reshape(w.shape), b.reshape(w.shape)
        elif nm == "w_in":
            d, a, b = (r.T[None] for r in adamw(w[0].T, g_w_inT, m[0].T, v[0].T, "adamw_" + nm))
        else:
            d, a, b = adamw(w, g, m, v, "adamw_" + nm)
        deltas.append(d)
        new_ms.append(a)
        new_vs.append(b)

    return (loss, grad_x.reshape(B, S, D), *gs, *deltas, *new_ms, *new_vs)
```

```python
import functools
import math

import jax
import jax.numpy as jnp
from jax import lax
from jax.experimental import pallas as pl
from jax.experimental.pallas import tpu as pltpu

F32 = jnp.float32
BF16 = jnp.bfloat16
HEAD = 128
CHUNK = 64
GDN_CONV = 4
FFN_CONV = 3
EPS = 1e-6
NEG = -1e30
ROW = 1024
ATT_TILE = 512
MM_WEIGHT_TILE_BYTES = 8 << 20
MM_TN_OPERAND_BYTES = 32 << 20
N_CHIP = 4
N_DEV = 8
MESH = pl.DeviceIdType.MESH
HI = lax.Precision.HIGH
EXACT = lax.Precision.HIGHEST

ADAM_LR, ADAM_B1, ADAM_B2, ADAM_EPS, ADAM_WD, ADAM_STEP = 0.001, 0.9, 0.999, 1e-08, 0.01, 10


def _tile(n, cap, unit=128):
    best = None
    t = unit
    while t <= min(n, cap):
        if n % t == 0:
            best = t
        t += unit
    return best if best is not None else n


def _params(*sem):
    return pltpu.CompilerParams(dimension_semantics=sem)


_NN = (((1,), (0,)), ((), ()))
_NT = (((1,), (1,)), ((), ()))
_TN = (((0,), (0,)), ((), ()))


def _dg(a, b, dims, hi):
    if hi:
        return lax.dot_general(a, b, dims, precision=HI, preferred_element_type=F32)
    return lax.dot_general(a.astype(BF16), b.astype(BF16), dims, preferred_element_type=F32)


class _RawOps:
    @staticmethod
    def nn(a, b, hi=False):
        return _dg(a, b, _NN, hi)

    @staticmethod
    def nt(a, b, hi=False):
        return _dg(a, b, _NT, hi)

    @staticmethod
    def tn(a, b, hi=False):
        return _dg(a, b, _TN, hi)


def _make_diff_ops():
    def build(hi):
        @jax.custom_vjp
        def nn(a, b):
            return _dg(a, b, _NN, hi)

        nn.defvjp(lambda a, b: (_dg(a, b, _NN, hi), (a, b)),
                  lambda r, g: (_dg(g, r[1], _NT, hi), _dg(r[0], g, _TN, hi)))

        @jax.custom_vjp
        def nt(a, b):
            return _dg(a, b, _NT, hi)

        nt.defvjp(lambda a, b: (_dg(a, b, _NT, hi), (a, b)),
                  lambda r, g: (_dg(g, r[1], _NN, hi), _dg(g, r[0], _TN, hi)))

        @jax.custom_vjp
        def tn(a, b):
            return _dg(a, b, _TN, hi)

        tn.defvjp(lambda a, b: (_dg(a, b, _TN, hi), (a, b)),
                  lambda r, g: (_dg(r[1], g, _NT, hi), _dg(r[0], g, _NN, hi)))
        return nn, nt, tn

    lo, hi_ = build(False), build(True)

    class _DiffOps:
        @staticmethod
        def nn(a, b, hi=False):
            return (hi_ if hi else lo)[0](a, b)

        @staticmethod
        def nt(a, b, hi=False):
            return (hi_ if hi else lo)[1](a, b)

        @staticmethod
        def tn(a, b, hi=False):
            return (hi_ if hi else lo)[2](a, b)

    return _DiffOps


_DiffOps = _make_diff_ops()


def _sigmoid(x):
    return 1.0 / (1.0 + jnp.exp(-x))


def _mm_tile(n, pref):
    if n % pref == 0:
        return pref
    if n % 1408 == 0:
        return 1408
    return _tile(n, pref)


class _Post:
    def __init__(self, fn, row_ins=(), vec_ins=(), row_outs=(), acc_outs=(), keep_main=True):
        self.fn, self.keep_main = fn, keep_main
        self.row_ins = [r if isinstance(r, tuple) else (r, r.shape[1], 0) for r in row_ins]
        self.vec_ins, self.row_outs, self.acc_outs = list(vec_ins), list(row_outs), list(acc_outs)


def matmul(a, b, mode, name, add=None, out_dtype=F32, post=None, rider=None, slab=None):
    if mode == "nn":
        (M, K), (K2, N) = a.shape, b.shape
    elif mode == "nt":
        (M, K), (N, K2) = a.shape, b.shape
    else:
        (K, M), (K2, N) = a.shape, b.shape
    assert K == K2, (name, a.shape, b.shape)
    tn = slab[1] if slab else _mm_tile(N, 1024)
    if mode == "tn":
        tm = slab[0] if slab else (M if M <= 1408 else _mm_tile(M, 1408))
        row_bytes = 2 * (tm * a.dtype.itemsize + tn * b.dtype.itemsize)
        tk = next((t for t in (4096, 2048) if K % t == 0 and t * row_bytes <= MM_TN_OPERAND_BYTES), _mm_tile(K, 1024))
    else:
        tk = K if K * tn * 2 <= MM_WEIGHT_TILE_BYTES else _mm_tile(K, 1024)
        tm = _mm_tile(M, 1024 if tk <= 2048 and post is None else 512)
    nk = K // tk
    assert post is None or (mode != "tn" and tn == N), name
    dims = {"nn": _NN, "nt": _NT, "tn": _TN}[mode]
    if mode == "tn":
        a_spec = pl.BlockSpec((tk, tm), lambda j, i, k: (k, i))
    else:
        a_spec = pl.BlockSpec((tm, tk), lambda j, i, k: (i, k))
    if mode == "nt":
        b_spec = pl.BlockSpec((tn, tk), lambda j, i, k: (j, k))
    else:
        b_spec = pl.BlockSpec((tk, tn), lambda j, i, k: (k, j))
    o_spec = pl.BlockSpec((tm, tn), lambda j, i, k: (i, j))
    has_add = add is not None
    keep_main = post is None or post.keep_main
    counts = [2 + has_add] + ([len(post.row_ins), len(post.vec_ins)] if post else [0, 0]) + [int(keep_main)]
    counts += ([len(post.row_outs), len(post.acc_outs)] if post else [0, 0]) + [int(nk > 1)]

    def body(*refs):
        parts, p = [], 0
        for cnt in counts:
            parts.append(refs[p:p + cnt])
            p += cnt
        core, row_ins, vec_ins, main, row_outs, acc_outs, acc = parts
        a_ref, b_ref = core[:2]
        prod = lax.dot_general(a_ref[...].astype(BF16), b_ref[...].astype(BF16), dims, preferred_element_type=F32)

        def finish(r):
            if has_add:
                r = r + core[2][...]
            if keep_main:
                main[0][...] = r.astype(out_dtype)
            if post is not None:
                @pl.when(pl.program_id(1) == 0)
                def _():
                    for ref in acc_outs:
                        ref[...] = jnp.zeros_like(ref)

                post.fn(r, row_ins, vec_ins, row_outs, acc_outs)

        if nk == 1:
            finish(prod)
            return
        acc_ref = acc[0]
        k = pl.program_id(2)

        @pl.when(k == 0)
        def _():
            acc_ref[...] = jnp.zeros_like(acc_ref)

        acc_ref[...] += prod

        @pl.when(k == nk - 1)
        def _():
            finish(acc_ref[...])

    in_specs = [a_spec, b_spec] + ([o_spec] if has_add else [])
    args = (a, b) + ((add,) if has_add else ())
    out_specs = [o_spec] if keep_main else []
    out_shape = [jax.ShapeDtypeStruct((M, N), out_dtype)] if keep_main else []
    if slab:
        out_specs = [pl.BlockSpec((None, None, tm, tn), lambda j, i, k: (i, j, 0, 0))]
        out_shape = [jax.ShapeDtypeStruct((M // tm, N // tn, tm, tn), out_dtype)]
    if post is not None:
        in_specs += [pl.BlockSpec((tm, cols), lambda j, i, k, cb=cb: (i, cb)) for _, cols, cb in post.row_ins]
        in_specs += [pl.BlockSpec((1, v.shape[1]), lambda j, i, k: (0, 0)) for v in post.vec_ins]
        args += tuple(r for r, _, _ in post.row_ins) + tuple(post.vec_ins)
        out_specs += [pl.BlockSpec((tm, cols), lambda j, i, k: (i, 0)) for cols, _ in post.row_outs]
        out_specs += [pl.BlockSpec((1, cols), lambda j, i, k: (0, 0)) for cols in post.acc_outs]
        out_shape += [jax.ShapeDtypeStruct((M, cols), dt) for cols, dt in post.row_outs]
        out_shape += [jax.ShapeDtypeStruct((1, cols), F32) for cols in post.acc_outs]
    rows_sem = "arbitrary" if post is not None and post.acc_outs else "parallel"
    res = _hosted_call(
        body, rider, name=name, grid=(N // tn, M // tm, nk), in_specs=in_specs, out_specs=out_specs, out_shape=out_shape,
        scratch_shapes=[pltpu.VMEM((tm, tn), F32)] if nk > 1 else [], semantics=("parallel", rows_sem, "arbitrary"),
    )(*args)
    if rider is not None:
        res, carried = res
        return (res[0] if post is None else res), carried
    return res[0] if post is None else res


def rmsnorm_fwd(x, g, name):
    T, D = x.shape
    tm = _tile(T, 512, 8)

    def body(x_ref, g_ref, o_ref):
        xv = x_ref[...]
        r = lax.rsqrt(jnp.mean(xv * xv, axis=-1, keepdims=True) + EPS)
        o_ref[...] = (xv * r * g_ref[...]).astype(BF16)

    return pl.pallas_call(
        body, name=name, grid=(T // tm,),
        in_specs=[pl.BlockSpec((tm, D), lambda i: (i, 0)), pl.BlockSpec((1, D), lambda i: (0, 0))],
        out_specs=pl.BlockSpec((tm, D), lambda i: (i, 0)),
        out_shape=jax.ShapeDtypeStruct((T, D), BF16),
        compiler_params=_params("parallel"),
    )(x, g)


def _post_rmsnorm(g):
    def fn(r, row_ins, vec_ins, row_outs, acc_outs):
        rs = lax.rsqrt(jnp.mean(r * r, axis=-1, keepdims=True) + EPS)
        row_outs[0][...] = (r * rs * vec_ins[0][...]).astype(BF16)

    return _Post(fn, vec_ins=[g], row_outs=[(g.shape[1], BF16)])


def _post_rmsnorm_bwd(x, g, dres, with_bf16):
    D = g.shape[1]

    def fn(dy, row_ins, vec_ins, row_outs, acc_outs):
        xv = row_ins[0][...]
        rs = lax.rsqrt(jnp.mean(xv * xv, axis=-1, keepdims=True) + EPS)
        xh = xv * rs
        acc_outs[0][...] += jnp.sum(dy * xh, axis=0, keepdims=True)
        dxh = dy * vec_ins[0][...]
        dx = row_ins[1][...] + rs * (dxh - xh * jnp.mean(dxh * xh, axis=-1, keepdims=True))
        row_outs[0][...] = dx
        if with_bf16:
            row_outs[1][...] = dx.astype(BF16)

    return _Post(fn, row_ins=[x, dres], vec_ins=[g], row_outs=[(D, F32)] + ([(D, BF16)] if with_bf16 else []),
                 acc_outs=[D], keep_main=False)


def _post_loss(g, target):
    D = g.shape[1]

    def fn(hv, row_ins, vec_ins, row_outs, acc_outs):
        rs = lax.rsqrt(jnp.mean(hv * hv, axis=-1, keepdims=True) + EPS)
        xh = hv * rs
        gv = vec_ins[0][...]
        err = xh * gv - row_ins[0][...]
        acc_outs[0][...] += jnp.sum(err * err, axis=0, keepdims=True)
        dy = err * (1.0 / D)
        acc_outs[1][...] += jnp.sum(dy * xh, axis=0, keepdims=True)
        dxh = dy * gv
        dh = rs * (dxh - xh * jnp.mean(dxh * xh, axis=-1, keepdims=True))
        row_outs[0][...] = dh
        row_outs[1][...] = dh.astype(BF16)

    return _Post(fn, row_ins=[target], vec_ins=[g], row_outs=[(D, F32), (D, BF16)], acc_outs=[D, D], keep_main=False)


def _shift_down(x, k):
    if k == 0:
        return x
    rows = lax.broadcasted_iota(jnp.int32, x.shape, 0)
    return jnp.where(rows >= k, pltpu.roll(x, k, 0), 0.0)


def _shift_up(x, k):
    if k == 0:
        return x
    s = x.shape[0]
    rows = lax.broadcasted_iota(jnp.int32, x.shape, 0)
    return jnp.where(rows < s - k, pltpu.roll(x, s - k, 0), 0.0)


def _conv_fwd(x, w_ref, kw, keep_shifted=False):
    shifted = [_shift_down(x, kw - 1 - i) for i in range(kw - 1)]
    y = x * w_ref[kw - 1:kw, :]
    for i in range(kw - 1):
        y = y + shifted[i] * w_ref[i:i + 1, :]
    return (y, shifted) if keep_shifted else y


def _conv_bwd(x, shifted, dy, w_ref, kw):
    dx = dy * w_ref[kw - 1:kw, :]
    dws = []
    for i in range(kw - 1):
        dx = dx + _shift_up(dy, kw - 1 - i) * w_ref[i:i + 1, :]
        dws.append(jnp.sum(dy * shifted[i], axis=0, keepdims=True))
    dws.append(jnp.sum(dy * x, axis=0, keepdims=True))
    return dx, dws


def ffn_gate_fwd(up_g, up_v, cw_g, cw_v, B, S):
    T, Fd = up_g.shape
    tc = _tile(Fd, 256)

    def body(g_ref, v_ref, wg_ref, wv_ref, o_ref):
        ug = _conv_fwd(g_ref[...], wg_ref, FFN_CONV)
        uv = _conv_fwd(v_ref[...], wv_ref, FFN_CONV)
        o_ref[...] = (ug * _sigmoid(ug) * uv).astype(BF16)

    blk = pl.BlockSpec((S, tc), lambda b, j: (b, j))
    wblk = pl.BlockSpec((FFN_CONV, tc), lambda b, j: (0, j))
    return pl.pallas_call(
        body, name="ffn_gate_fwd", grid=(B, Fd // tc), in_specs=[blk, blk, wblk, wblk], out_specs=blk,
        out_shape=jax.ShapeDtypeStruct((T, Fd), BF16), compiler_params=_params("parallel", "parallel"),
    )(up_g, up_v, cw_g, cw_v)


def ffn_gate_bwd(up_g, up_v, cw_g, cw_v, d_act, B, S):
    T, Fd = up_g.shape
    tc = _tile(Fd, 256)

    def body(g_ref, v_ref, wg_ref, wv_ref, da_ref, dg_ref, dv_ref, dwg_ref, dwv_ref):
        @pl.when(pl.program_id(1) == 0)
        def _():
            dwg_ref[...] = jnp.zeros_like(dwg_ref)
            dwv_ref[...] = jnp.zeros_like(dwv_ref)

        xg, xv = g_ref[...], v_ref[...]
        ug, sh_g = _conv_fwd(xg, wg_ref, FFN_CONV, keep_shifted=True)
        uv, sh_v = _conv_fwd(xv, wv_ref, FFN_CONV, keep_shifted=True)
        da = da_ref[...]
        sg = _sigmoid(ug)
        d_ug = da * uv * (sg + ug * sg * (1.0 - sg))
        d_uv = da * ug * sg
        dxg, dwg = _conv_bwd(xg, sh_g, d_ug, wg_ref, FFN_CONV)
        dxv, dwv = _conv_bwd(xv, sh_v, d_uv, wv_ref, FFN_CONV)
        dg_ref[...] = dxg.astype(BF16)
        dv_ref[...] = dxv.astype(BF16)
        for i in range(FFN_CONV):
            dwg_ref[i:i + 1, :] += dwg[i]
            dwv_ref[i:i + 1, :] += dwv[i]

    blk = pl.BlockSpec((S, tc), lambda j, b: (b, j))
    wblk = pl.BlockSpec((FFN_CONV, tc), lambda j, b: (0, j))
    return pl.pallas_call(
        body, name="ffn_gate_bwd", grid=(Fd // tc, B), in_specs=[blk, blk, wblk, wblk, blk],
        out_specs=[blk, blk, wblk, wblk],
        out_shape=[jax.ShapeDtypeStruct((T, Fd), BF16), jax.ShapeDtypeStruct((T, Fd), BF16),
                   jax.ShapeDtypeStruct((FFN_CONV, Fd), F32), jax.ShapeDtypeStruct((FFN_CONV, Fd), F32)],
        compiler_params=_params("parallel", "arbitrary"),
    )(up_g, up_v, cw_g, cw_v, d_act)


def _post_merge(p_gates, bf_):
    D = bf_.shape[1]

    def fn(bg, row_ins, vec_ins, row_outs, acc_outs):
        gf_ref, gg_ref, bf_ref = row_ins
        row_outs[0][...] = (_sigmoid(gf_ref[...]) * bf_ref[...] + _sigmoid(gg_ref[...]) * bg).astype(BF16)

    return _Post(fn, row_ins=[(p_gates, D, 0), (p_gates, D, 1), bf_], row_outs=[(D, BF16)])


def _post_merge_bwd(p_gates, bf_, bg_):
    D = bf_.shape[1]

    def fn(d, row_ins, vec_ins, row_outs, acc_outs):
        gf_ref, gg_ref, bf_ref, bg_ref = row_ins
        sf, sg = _sigmoid(gf_ref[...]), _sigmoid(gg_ref[...])
        row_outs[0][...] = (d * sf).astype(BF16)
        row_outs[1][...] = (d * sg).astype(BF16)
        row_outs[2][:, 0:D] = (d * bf_ref[...] * sf * (1.0 - sf)).astype(BF16)
        row_outs[2][:, D:2 * D] = (d * bg_ref[...] * sg * (1.0 - sg)).astype(BF16)

    return _Post(fn, row_ins=[(p_gates, D, 0), (p_gates, D, 1), bf_, bg_],
                 row_outs=[(D, BF16), (D, BF16), (2 * D, BF16)], keep_main=False)


ATT_HEADS = 2


def fox_fwd(p_fox, c_col, c_row, B, S, H, rider=None):
    T = B * S
    t = _tile(S, ATT_TILE)
    nq = S // t
    scale = HEAD ** -0.5
    hp = ATT_HEADS if H % ATT_HEADS == 0 else 1
    hs = range(hp)

    def body(*refs):
        qkv_refs, (cq_ref, cr_ref, o_ref, o16_ref, lse_ref) = refs[:3 * hp], refs[3 * hp:]
        i = pl.program_id(2)
        q = [qkv_refs[3 * hh][...] for hh in hs]
        row = lax.broadcasted_iota(jnp.int32, (t, t), 0)
        col = lax.broadcasted_iota(jnp.int32, (t, t), 1)

        def step(j, carry, diagonal):
            off = pl.multiple_of(j * t, t)
            k = [qkv_refs[3 * hh + 1][pl.ds(off, t), :] for hh in hs]
            v = [qkv_refs[3 * hh + 2][pl.ds(off, t), :] for hh in hs]
            s = [lax.dot_general(q[hh], k[hh], _NT, preferred_element_type=F32) * scale - cr_ref[hh, :, pl.ds(off, t)]
                 for hh in hs]
            if diagonal:
                s = [jnp.where(col <= row, s[hh], NEG) for hh in hs]
            m_new = [jnp.maximum(carry[hh][0], jnp.max(s[hh], axis=-1, keepdims=True)) for hh in hs]
            alpha = [jnp.exp(carry[hh][0] - m_new[hh]) for hh in hs]
            p = [jnp.exp(s[hh] - m_new[hh]) for hh in hs]
            l = [alpha[hh] * carry[hh][1] + jnp.sum(p[hh], axis=-1, keepdims=True) for hh in hs]
            acc = [alpha[hh] * carry[hh][2] + lax.dot_general(p[hh].astype(BF16), v[hh], _NN, preferred_element_type=F32)
                   for hh in hs]
            return tuple((m_new[hh], l[hh], acc[hh]) for hh in hs)

        start = (jnp.full((t, 1), NEG, F32), jnp.zeros((t, 1), F32), jnp.zeros((t, HEAD), F32))
        below = lax.fori_loop(0, i, functools.partial(step, diagonal=False), tuple(start for _ in hs))
        done = step(i, below, diagonal=True)
        for hh, (m, l, acc) in enumerate(done):
            cols = slice(hh * HEAD, (hh + 1) * HEAD)
            o = acc / l
            o_ref[:, cols] = o
            o16_ref[:, cols] = o.astype(BF16)
            lse_ref[hh] = cq_ref[hh] + m + jnp.log(l)

    qkv = []
    for hh in hs:
        qkv.append(pl.BlockSpec((t, HEAD), lambda b, g, i, hh=hh: (b * nq + i, 3 * (hp * g + hh))))
        qkv.append(pl.BlockSpec((S, HEAD), lambda b, g, i, hh=hh: (b, 3 * (hp * g + hh) + 1)))
        qkv.append(pl.BlockSpec((S, HEAD), lambda b, g, i, hh=hh: (b, 3 * (hp * g + hh) + 2)))
    heads = pl.BlockSpec((t, hp * HEAD), lambda b, g, i: (b * nq + i, g))
    return _hosted_call(
        body, rider, name="fox_fwd", grid=(B, H // hp, nq),
        in_specs=qkv + [pl.BlockSpec((None, hp, t, 1), lambda b, g, i: (b, g, i, 0)),
                        pl.BlockSpec((None, hp, 1, S), lambda b, g, i: (b, g, 0, 0))],
        out_specs=[heads, heads, pl.BlockSpec((None, hp, t, 1), lambda b, g, i: (b, g, i, 0))],
        out_shape=[jax.ShapeDtypeStruct((T, H * HEAD), F32), jax.ShapeDtypeStruct((T, H * HEAD), BF16),
                   jax.ShapeDtypeStruct((B, H, S, 1), F32)],
        scratch_shapes=[], semantics=("parallel", "parallel", "arbitrary"),
    )(*([p_fox] * (3 * hp)), c_col, c_row)


def fox_bwd(p_fox, c_col, c_row, o, lse, do, B, S, H, rider=None):
    T = B * S
    t = _tile(S, ATT_TILE)
    n = S // t
    scale = HEAD ** -0.5

    def body(q_ref, k_ref, v_ref, cq_ref, cr_ref, o_ref, lse_ref, do_ref, dqkv_ref, dcq_ref, dcr_ref,
             dq_acc, delta_s, lse_s):
        row = lax.broadcasted_iota(jnp.int32, (t, t), 0)
        col = lax.broadcasted_iota(jnp.int32, (t, t), 1)

        def prep(i, c):
            rows = pl.ds(pl.multiple_of(i * t, t), t)
            delta_s[rows, :] = jnp.sum(do_ref[rows, :] * o_ref[rows, :], axis=-1, keepdims=True)
            lse_s[rows, :] = lse_ref[rows, :] - cq_ref[rows, :]
            dq_acc[rows, :] = jnp.zeros((t, HEAD), F32)
            dcq_ref[rows, :] = jnp.zeros((t, 1), F32)
            return c

        lax.fori_loop(0, n, prep, 0)

        def kv_step(j, c):
            joff = pl.multiple_of(j * t, t)
            k = k_ref[pl.ds(joff, t), :]
            v = v_ref[pl.ds(joff, t), :]
            crj = cr_ref[:, pl.ds(joff, t)]

            def q_step(i, carry, diagonal):
                dk, dv, dc = carry
                rows = pl.ds(pl.multiple_of(i * t, t), t)
                q = q_ref[rows, :]
                dob = do_ref[rows, :].astype(BF16)
                s = lax.dot_general(q, k, _NT, preferred_element_type=F32) * scale - crj
                if diagonal:
                    s = jnp.where(col <= row, s, NEG)
                p = jnp.exp(s - lse_s[rows, :])
                dp = lax.dot_general(dob, v, _NT, preferred_element_type=F32)
                ds = p * (dp - delta_s[rows, :])
                dsb = ds.astype(BF16)
                dv = dv + lax.dot_general(p.astype(BF16), dob, _TN, preferred_element_type=F32)
                dk = dk + lax.dot_general(dsb, q, _TN, preferred_element_type=F32)
                dq_acc[rows, :] += lax.dot_general(dsb, k, _NN, preferred_element_type=F32) * scale
                dc = dc + jnp.sum(ds, axis=0, keepdims=True)
                dcq_ref[rows, :] += jnp.sum(ds, axis=-1, keepdims=True)
                return dk, dv, dc

            z = jnp.zeros((t, HEAD), F32)
            on_diagonal = q_step(j, (z, z, jnp.zeros((1, t), F32)), diagonal=True)
            dk, dv, dc = lax.fori_loop(j + 1, n, functools.partial(q_step, diagonal=False), on_diagonal)
            dqkv_ref[pl.ds(joff, t), HEAD:2 * HEAD] = (dk * scale).astype(BF16)
            dqkv_ref[pl.ds(joff, t), 2 * HEAD:3 * HEAD] = dv.astype(BF16)
            dcr_ref[:, pl.ds(joff, t)] = -dc
            return c

        lax.fori_loop(0, n, kv_step, 0)
        dqkv_ref[:, 0:HEAD] = dq_acc[...].astype(BF16)

    col_spec = pl.BlockSpec((None, None, S, 1), lambda b, h: (b, h, 0, 0))
    row_spec = pl.BlockSpec((None, None, 1, S), lambda b, h: (b, h, 0, 0))
    head = pl.BlockSpec((S, HEAD), lambda b, h: (b, h))
    return _hosted_call(
        body, rider, name="fox_bwd", grid=(B, H),
        in_specs=[pl.BlockSpec((S, HEAD), lambda b, h: (b, 3 * h)),
                  pl.BlockSpec((S, HEAD), lambda b, h: (b, 3 * h + 1)),
                  pl.BlockSpec((S, HEAD), lambda b, h: (b, 3 * h + 2)),
                  col_spec, row_spec, head, col_spec, head],
        out_specs=[pl.BlockSpec((S, 3 * HEAD), lambda b, h: (b, h)), col_spec, row_spec],
        out_shape=[jax.ShapeDtypeStruct((T, 3 * H * HEAD), BF16), jax.ShapeDtypeStruct((B, H, S, 1), F32),
                   jax.ShapeDtypeStruct((B, H, 1, S), F32)],
        scratch_shapes=[pltpu.VMEM((S, HEAD), F32), pltpu.VMEM((S, 1), F32), pltpu.VMEM((S, 1), F32)],
        semantics=("parallel", "parallel"),
    )(p_fox, p_fox, p_fox, c_col, c_row, o, lse, do)


def _small_fn(x, b0, b1, H):
    S = x.shape[0]
    lane = lax.broadcasted_iota(jnp.int32, x.shape, 1)
    z = x + b0
    tail = jnp.log1p(jnp.exp(-jnp.abs(z)))
    softplus = jnp.maximum(z, 0.0) + tail
    logsig = -(jnp.maximum(-z, 0.0) + tail)
    g = -jnp.exp(b1) * softplus
    pre = jnp.where(lane < H, logsig, jnp.where(lane < 2 * H, g, 0.0))
    bl = _tile(S, 256, CHUNK)
    r = lax.broadcasted_iota(jnp.int32, (bl, bl), 0)
    c = lax.broadcasted_iota(jnp.int32, (bl, bl), 1)
    tri = (r >= c).astype(F32)
    tri_chunk = jnp.where((r >= c) & (jnp.right_shift(r, 6) == jnp.right_shift(c, 6)), 1.0, 0.0)
    carry = jnp.zeros((1, x.shape[1]), F32)
    parts = []
    for i in range(S // bl):
        blk = pre[i * bl:(i + 1) * bl, :]
        full = lax.dot_general(tri, blk, _NN, precision=EXACT, preferred_element_type=F32) + carry
        chunked = lax.dot_general(tri_chunk, blk, _NN, precision=EXACT, preferred_element_type=F32)
        parts.append(jnp.where(lane[:bl] < H, full, chunked))
        carry = carry + jnp.sum(blk, axis=0, keepdims=True)
    cum = parts[0] if len(parts) == 1 else jnp.concatenate(parts, axis=0)
    return jnp.where(lane < 2 * H, cum, jnp.where(lane < 3 * H, _sigmoid(x), 0.0))


def small_fwd(p_small, prm, B, S, H):
    T = B * S

    def body(x_ref, p_ref, o_ref):
        o_ref[...] = _small_fn(x_ref[...], p_ref[0:1, :], p_ref[1:2, :], H)

    blk = pl.BlockSpec((S, 128), lambda b: (b, 0))
    return pl.pallas_call(
        body, name="small_fwd", grid=(B,), in_specs=[blk, pl.BlockSpec((8, 128), lambda b: (0, 0))], out_specs=blk,
        out_shape=jax.ShapeDtypeStruct((T, 128), F32), compiler_params=_params("parallel"),
    )(p_small, prm)


def small_bwd(p_small, prm, d_out, B, S, H):
    T = B * S

    def body(x_ref, p_ref, d_ref, dx_ref, dp_ref):
        @pl.when(pl.program_id(0) == 0)
        def _():
            dp_ref[...] = jnp.zeros_like(dp_ref)

        _, vjp = jax.vjp(functools.partial(_small_fn, H=H), x_ref[...], p_ref[0:1, :], p_ref[1:2, :])
        dx, db0, db1 = vjp(d_ref[...])
        dx_ref[...] = dx.astype(BF16)
        dp_ref[0:1, :] += db0
        dp_ref[1:2, :] += db1

    blk = pl.BlockSpec((S, 128), lambda b: (b, 0))
    pblk = pl.BlockSpec((8, 128), lambda b: (0, 0))
    return pl.pallas_call(
        body, name="small_bwd", grid=(B,), in_specs=[blk, pblk, blk], out_specs=[blk, pblk],
        out_shape=[jax.ShapeDtypeStruct((T, 128), BF16), jax.ShapeDtypeStruct((8, 128), F32)],
        compiler_params=_params("arbitrary"),
    )(p_small, prm, d_out)


def gdn_prep_fwd(p_gqkv, cw, B, S, H):
    T = B * S

    def body(x_ref, w_ref, o_ref):
        for part in range(3):
            cols = slice(part * HEAD, (part + 1) * HEAD)
            y = _conv_fwd(x_ref[:, cols], w_ref.at[:, cols], GDN_CONV)
            a = y * _sigmoid(y)
            if part < 2:
                a = a * lax.rsqrt(jnp.sum(a * a, axis=-1, keepdims=True) + EPS)
            o_ref[:, cols] = a

    blk = pl.BlockSpec((S, 3 * HEAD), lambda b, h: (b, h))
    wblk = pl.BlockSpec((GDN_CONV, 3 * HEAD), lambda b, h: (0, h))
    return pl.pallas_call(
        body, name="gdn_prep_fwd", grid=(B, H), in_specs=[blk, wblk], out_specs=blk,
        out_shape=jax.ShapeDtypeStruct((T, 3 * H * HEAD), F32), compiler_params=_params("parallel", "parallel"),
    )(p_gqkv, cw)


def gdn_prep_bwd(p_gqkv, cw, d_out, B, S, H):
    T = B * S

    def body(x_ref, w_ref, d_ref, dx_ref, dw_ref):
        @pl.when(pl.program_id(1) == 0)
        def _():
            dw_ref[...] = jnp.zeros_like(dw_ref)

        x = x_ref[...]
        y, shifted = _conv_fwd(x, w_ref, GDN_CONV, keep_shifted=True)
        sg = _sigmoid(y)
        a = y * sg
        rs = lax.rsqrt(jnp.sum(a * a, axis=-1, keepdims=True) + EPS)
        d = d_ref[...]
        out = a * rs
        da_qk = rs * (d - out * jnp.sum(d * out, axis=-1, keepdims=True))
        is_qk = (pl.program_id(0) % 3) < 2
        da = jnp.where(is_qk, da_qk, d)
        dy = da * (sg + y * sg * (1.0 - sg))
        dx, dws = _conv_bwd(x, shifted, dy, w_ref, GDN_CONV)
        dx_ref[...] = dx.astype(BF16)
        for i in range(GDN_CONV):
            dw_ref[i:i + 1, :] += dws[i]

    blk = pl.BlockSpec((S, HEAD), lambda n, b: (b, n))
    wblk = pl.BlockSpec((GDN_CONV, HEAD), lambda n, b: (0, n))
    return pl.pallas_call(
        body, name="gdn_prep_bwd", grid=(3 * H, B), in_specs=[blk, wblk, blk], out_specs=[blk, wblk],
        out_shape=[jax.ShapeDtypeStruct((T, 3 * H * HEAD), BF16), jax.ShapeDtypeStruct((GDN_CONV, 3 * H * HEAD), F32)],
        compiler_params=_params("parallel", "arbitrary"),
    )(p_gqkv, cw, d_out)


@jax.custom_vjp
def _given_inverse(a, t):
    return t


def _given_inverse_fwd(a, t):
    return t, t


def _given_inverse_bwd(t, g):
    x = _dg(t, g, _TN, True)
    return -_dg(x, t, _NT, True), jnp.zeros_like(t)


_given_inverse.defvjp(_given_inverse_fwd, _given_inverse_bwd)


def _to_col(row):
    n = row.shape[1]
    r = lax.broadcasted_iota(jnp.int32, (n, n), 0)
    c = lax.broadcasted_iota(jnp.int32, (n, n), 1)
    return jnp.sum(jnp.where(r == c, row, 0.0), axis=1, keepdims=True)


def _intra_fn(k, v, beta_r, gcr, ops, t_known=None):
    n = len(k)
    m = k[0].shape[0]
    r = lax.broadcasted_iota(jnp.int32, (m, m), 0)
    c = lax.broadcasted_iota(jnp.int32, (m, m), 1)
    below = (r > c) & (jnp.right_shift(r, 6) == jnp.right_shift(c, 6))
    beta = [_to_col(beta_r[i]) for i in range(n)]
    gcc = [_to_col(gcr[i]) for i in range(n)]
    decay = [jnp.exp(jnp.where(below, gcc[i] - gcr[i], NEG)) for i in range(n)]
    kb = [k[i] * beta[i] for i in range(n)]
    a = [ops.nt(kb[i], k[i]) * decay[i] for i in range(n)]
    if t_known is None:
        p = [-a[i] for i in range(n)]
        tm = [jnp.where(r == c, 1.0, 0.0) + p[i] for i in range(n)]
        for _ in range(5):
            p = [ops.nn(p[i], p[i], hi=True) for i in range(n)]
            tm = [tm[i] + ops.nn(tm[i], p[i], hi=True) for i in range(n)]
    else:
        tm = [_given_inverse(a[i], t_known[i]) for i in range(n)]
    both = [ops.nn(tm[i], jnp.concatenate([v[i] * beta[i], kb[i] * jnp.exp(gcc[i])], axis=1), hi=True) for i in range(n)]
    u_hat = [both[i][:, :HEAD] for i in range(n)]
    w = [both[i][:, HEAD:] for i in range(n)]
    return tuple(u_hat), tuple(w), tuple(tm)


INTRA_NB = 32
PAIR = 1


def gdn_intra_fwd(qkvn, betar5, gcr5, B, S, H, rider=None):
    T = B * S
    UNIT = PAIR * CHUNK
    N = S // UNIT
    nb = min(INTRA_NB // PAIR, N)
    rows = nb * UNIT
    ns = N // nb

    def body(k_ref, v_ref, b_ref, gr_ref, uh_ref, w_ref, t_ref):
        sls = [slice(ci * UNIT, (ci + 1) * UNIT) for ci in range(nb)]
        u_hat, w, tm = _intra_fn(tuple(k_ref[sl, :] for sl in sls), tuple(v_ref[sl, :] for sl in sls),
                                 tuple(b_ref[ci] for ci in range(nb)), tuple(gr_ref[ci] for ci in range(nb)), _RawOps)
        for ci, sl in enumerate(sls):
            uh_ref[sl, :] = u_hat[ci]
            w_ref[sl, :] = w[ci]
            t_ref[ci] = tm[ci]

    rowspec = pl.BlockSpec((None, None, nb, 1, UNIT), lambda b, h, i: (b, h, i, 0, 0))
    sqspec = pl.BlockSpec((None, None, nb, UNIT, UNIT), lambda b, h, i: (b, h, i, 0, 0))
    out = pl.BlockSpec((rows, HEAD), lambda b, h, i: (b * ns + i, h))
    return _hosted_call(
        body, rider, name="gdn_intra_fwd", grid=(B, H, ns),
        in_specs=[pl.BlockSpec((rows, HEAD), lambda b, h, i: (b * ns + i, 3 * h + 1)),
                  pl.BlockSpec((rows, HEAD), lambda b, h, i: (b * ns + i, 3 * h + 2)),
                  rowspec, rowspec],
        out_specs=[out, out, sqspec],
        out_shape=[jax.ShapeDtypeStruct((T, H * HEAD), F32), jax.ShapeDtypeStruct((T, H * HEAD), F32),
                   jax.ShapeDtypeStruct((B, H, N, UNIT, UNIT), F32)],
        scratch_shapes=[], semantics=("parallel", "parallel", "parallel"),
    )(qkvn, qkvn, betar5, gcr5)


def gdn_intra_bwd(qkvn, betar5, gcr5, t_inv, d_uh, d_w, dq_in, dk_in, B, S, H):
    T = B * S
    UNIT = PAIR * CHUNK
    N = S // UNIT
    nb = min(INTRA_NB // PAIR, N)
    rows = nb * UNIT
    ns = N // nb

    def body(k_ref, v_ref, b_ref, gr_ref, t_ref, duh_ref, dw_ref, dq_ref, dk_ref, o_ref, db_ref, dgr_ref):
        sls = [slice(ci * UNIT, (ci + 1) * UNIT) for ci in range(nb)]
        chunks = range(nb)
        _, vjp = jax.vjp(
            functools.partial(_intra_fn, ops=_DiffOps, t_known=tuple(t_ref[ci] for ci in chunks)),
            tuple(k_ref[sl, :] for sl in sls), tuple(v_ref[sl, :] for sl in sls), tuple(b_ref[ci] for ci in chunks),
            tuple(gr_ref[ci] for ci in chunks))
        zero = jnp.zeros((UNIT, UNIT), F32)
        dk, dv, db, dgr = vjp((tuple(duh_ref[sl, :] for sl in sls), tuple(dw_ref[sl, :] for sl in sls),
                               tuple(zero for _ in chunks)))
        for ci, sl in enumerate(sls):
            o_ref[sl, 0:HEAD] = dq_ref[sl, :]
            o_ref[sl, HEAD:2 * HEAD] = dk[ci] + dk_ref[sl, :]
            o_ref[sl, 2 * HEAD:3 * HEAD] = dv[ci]
            db_ref[ci] = db[ci]
            dgr_ref[ci] = dgr[ci]

    rowspec = pl.BlockSpec((None, None, nb, 1, UNIT), lambda b, h, i: (b, h, i, 0, 0))
    sqspec = pl.BlockSpec((None, None, nb, UNIT, UNIT), lambda b, h, i: (b, h, i, 0, 0))
    head = pl.BlockSpec((rows, HEAD), lambda b, h, i: (b * ns + i, h))
    return pl.pallas_call(
        body, name="gdn_intra_bwd", grid=(B, H, ns),
        in_specs=[pl.BlockSpec((rows, HEAD), lambda b, h, i: (b * ns + i, 3 * h + 1)),
                  pl.BlockSpec((rows, HEAD), lambda b, h, i: (b * ns + i, 3 * h + 2)),
                  rowspec, rowspec, sqspec, head, head, head, head],
        out_specs=[pl.BlockSpec((rows, 3 * HEAD), lambda b, h, i: (b * ns + i, h)), rowspec, rowspec],
        out_shape=[jax.ShapeDtypeStruct((T, 3 * H * HEAD), F32),
                   jax.ShapeDtypeStruct((B, H, N, 1, UNIT), F32), jax.ShapeDtypeStruct((B, H, N, 1, UNIT), F32)],
        compiler_params=_params("parallel", "parallel", "parallel"),
    )(qkvn, qkvn, betar5, gcr5, t_inv, d_uh, d_w, dq_in, dk_in)


def _inter_fn(q, k, u_hat, w, gcr, state, ops):
    n = len(q)
    r = lax.broadcasted_iota(jnp.int32, (CHUNK, CHUNK), 0)
    c = lax.broadcasted_iota(jnp.int32, (CHUNK, CHUNK), 1)
    last = lax.broadcasted_iota(jnp.int32, (1, CHUNK), 1) == CHUNK - 1
    gcc = [_to_col(gcr[i]) for i in range(n)]
    gl = [jnp.sum(jnp.where(last, gcr[i], 0.0), axis=1, keepdims=True) for i in range(n)]
    decay = [jnp.exp(jnp.where(r >= c, gcc[i] - gcr[i], NEG)) for i in range(n)]
    qs = [q[i] * (HEAD ** -0.5) for i in range(n)]
    ws = [ops.nn(w[i], state[i]) for i in range(n)]
    qst = [ops.nn(qs[i] * jnp.exp(gcc[i]), state[i]) for i in range(n)]
    attn = [ops.nt(qs[i], k[i]) * decay[i] for i in range(n)]
    u = [u_hat[i] - ws[i] for i in range(n)]
    o = [qst[i] + ops.nn(attn[i], u[i]) for i in range(n)]
    kdu = [ops.tn(k[i] * jnp.exp(gl[i] - gcc[i]), u[i]) for i in range(n)]
    new_state = [state[i] * jnp.exp(gl[i]) + kdu[i] for i in range(n)]
    return tuple(o), tuple(new_state)


INTER_HEADS = 8
INTER_ROWS = 512
INTER_ROWS_BWD = 256


def _inter_heads(H):
    return INTER_HEADS if H % INTER_HEADS == 0 else (4 if H % 4 == 0 else 1)


def _inter_specs(ts, ns, hp, backward):
    at = (lambda s: ns - 1 - s) if backward else (lambda s: s)
    nc = ts // CHUNK
    qk = []
    for hh in range(hp):
        qk.append(pl.BlockSpec((ts, HEAD), lambda b, g, s, hh=hh: (b * ns + at(s), 3 * (hp * g + hh))))
        qk.append(pl.BlockSpec((ts, HEAD), lambda b, g, s, hh=hh: (b * ns + at(s), 3 * (hp * g + hh) + 1)))
    heads = pl.BlockSpec((ts, hp * HEAD), lambda b, g, s: (b * ns + at(s), g))
    rowspec = pl.BlockSpec((None, hp, nc, 1, CHUNK), lambda b, g, s: (b, g, at(s), 0, 0))
    stspec = pl.BlockSpec((None, hp, nc, HEAD, HEAD), lambda b, g, s: (b, g, at(s), 0, 0))
    return qk, heads, rowspec, stspec


def gdn_inter_fwd(qkvn, u_hat, w, gcr5, p_gz, gnorm, B, S, H):
    T = B * S
    N = S // CHUNK
    hp = _inter_heads(H)
    hs = range(hp)
    ts = _tile(S, INTER_ROWS, CHUNK)
    ns, nc = S // ts, ts // CHUNK

    def body(*refs):
        qk_refs, (uh_ref, w_ref, gr_ref, z_ref, gn_ref, o_ref, st_ref, y_ref, s_scr) = refs[:2 * hp], refs[2 * hp:]

        @pl.when(pl.program_id(2) == 0)
        def _():
            s_scr[...] = jnp.zeros_like(s_scr)

        gn = gn_ref[...]

        def step(n, c):
            rows = pl.ds(pl.multiple_of(n * CHUNK, CHUNK), CHUNK)
            st = tuple(s_scr[hh] for hh in hs)
            for hh in hs:
                st_ref[hh, n] = st[hh]
            o, new = _inter_fn(tuple(qk_refs[2 * hh][rows, :] for hh in hs), tuple(qk_refs[2 * hh + 1][rows, :] for hh in hs),
                               tuple(uh_ref[rows, hh * HEAD:(hh + 1) * HEAD] for hh in hs),
                               tuple(w_ref[rows, hh * HEAD:(hh + 1) * HEAD] for hh in hs),
                               tuple(gr_ref[hh, n] for hh in hs), st, _RawOps)
            for hh in hs:
                cols = slice(hh * HEAD, (hh + 1) * HEAD)
                o_ref[rows, cols] = o[hh]
                s_scr[hh] = new[hh]
                z = z_ref[rows, cols]
                r = lax.rsqrt(jnp.mean(o[hh] * o[hh], axis=-1, keepdims=True) + EPS)
                y_ref[rows, cols] = (o[hh] * r * gn * z * _sigmoid(z)).astype(BF16)
            return c

        lax.fori_loop(0, nc, step, 0)

    qk, heads, rowspec, stspec = _inter_specs(ts, ns, hp, backward=False)
    return pl.pallas_call(
        body, name="gdn_inter_fwd", grid=(B, H // hp, ns),
        in_specs=qk + [heads, heads, rowspec, heads, pl.BlockSpec((1, HEAD), lambda b, g, s: (0, 0))],
        out_specs=[heads, stspec, heads],
        out_shape=[jax.ShapeDtypeStruct((T, H * HEAD), F32), jax.ShapeDtypeStruct((B, H, N, HEAD, HEAD), F32),
                   jax.ShapeDtypeStruct((T, H * HEAD), BF16)],
        scratch_shapes=[pltpu.VMEM((hp, HEAD, HEAD), F32)],
        compiler_params=_params("parallel", "parallel", "arbitrary"),
    )(*([qkvn] * (2 * hp)), u_hat, w, gcr5, p_gz, gnorm)


def gdn_inter_bwd(qkvn, u_hat, w, gcr5, states, o, p_gz, gnorm, d_y, B, S, H, rider=None):
    T = B * S
    N = S // CHUNK
    hp = _inter_heads(H)
    hs = range(hp)
    ts = _tile(S, INTER_ROWS_BWD, CHUNK)
    ns, nc = S // ts, ts // CHUNK

    def body(*refs):
        qk_refs = refs[:2 * hp]
        (uh_ref, w_ref, gr_ref, st_ref, o_ref, z_ref, gn_ref, dy_ref,
         dq_ref, dk_ref, duh_ref, dw_ref, dgr_ref, dz_ref, dgn_ref, ds_scr) = refs[2 * hp:]

        @pl.when(pl.program_id(2) == 0)
        def _():
            ds_scr[...] = jnp.zeros_like(ds_scr)
            dgn_ref[...] = jnp.zeros_like(dgn_ref)

        cols = [slice(hh * HEAD, (hh + 1) * HEAD) for hh in hs]
        gn = gn_ref[...]

        def through_norm(rows, hh):
            ov, z, d = o_ref[rows, cols[hh]], z_ref[rows, cols[hh]], dy_ref[rows, cols[hh]]
            r = lax.rsqrt(jnp.mean(ov * ov, axis=-1, keepdims=True) + EPS)
            xh = ov * r
            sg = _sigmoid(z)
            d_n = d * (z * sg)
            dz_ref[rows, cols[hh]] = (d * xh * gn * (sg + z * sg * (1.0 - sg))).astype(BF16)
            dgn_ref[0:1, :] += jnp.sum(d_n * xh, axis=0, keepdims=True)
            dxh = d_n * gn
            return r * (dxh - xh * jnp.mean(dxh * xh, axis=-1, keepdims=True))

        def step(i, c):
            n = nc - 1 - i
            rows = pl.ds(pl.multiple_of(n * CHUNK, CHUNK), CHUNK)
            _, vjp = jax.vjp(functools.partial(_inter_fn, ops=_DiffOps),
                             tuple(qk_refs[2 * hh][rows, :] for hh in hs), tuple(qk_refs[2 * hh + 1][rows, :] for hh in hs),
                             tuple(uh_ref[rows, cols[hh]] for hh in hs), tuple(w_ref[rows, cols[hh]] for hh in hs),
                             tuple(gr_ref[hh, n] for hh in hs), tuple(st_ref[hh, n] for hh in hs))
            dq, dk, duh, dw, dgr, ds = vjp((tuple(through_norm(rows, hh) for hh in hs), tuple(ds_scr[hh] for hh in hs)))
            for hh in hs:
                dq_ref[rows, cols[hh]] = dq[hh]
                dk_ref[rows, cols[hh]] = dk[hh]
                duh_ref[rows, cols[hh]] = duh[hh]
                dw_ref[rows, cols[hh]] = dw[hh]
                dgr_ref[hh, n] = dgr[hh]
                ds_scr[hh] = ds[hh]
            return c

        lax.fori_loop(0, nc, step, 0)

    qk, heads, rowspec, stspec = _inter_specs(ts, ns, hp, backward=True)
    hshape = jax.ShapeDtypeStruct((T, H * HEAD), F32)
    return _hosted_call(
        body, rider, name="gdn_inter_bwd", grid=(B, H // hp, ns),
        in_specs=qk + [heads, heads, rowspec, stspec, heads, heads, pl.BlockSpec((1, HEAD), lambda b, g, s: (0, 0)), heads],
        out_specs=[heads, heads, heads, heads, rowspec, heads,
                   pl.BlockSpec((None, None, 8, HEAD), lambda b, g, s: (b, g, 0, 0))],
        out_shape=[hshape, hshape, hshape, hshape, jax.ShapeDtypeStruct((B, H, N, 1, CHUNK), F32),
                   jax.ShapeDtypeStruct((T, H * HEAD), BF16), jax.ShapeDtypeStruct((B, H // hp, 8, HEAD), F32)],
        scratch_shapes=[pltpu.VMEM((hp, HEAD, HEAD), F32)], semantics=("parallel", "parallel", "arbitrary"),
    )(*([qkvn] * (2 * hp)), u_hat, w, gcr5, states, o, p_gz, gnorm, d_y)


def adamw(w, g, m, v, name):
    shape = w.shape
    lead = (None,) * (w.ndim - 2)
    zeros = (0,) * (w.ndim - 2)
    R, C = shape[-2:]
    g2 = g.reshape(R, C)
    tr, tc = _tile(R, 128, 8), C
    if tr % 8 and R > 8:
        tr, tc = R, _tile(C, 128)

    def body(w_ref, g_ref, m_ref, v_ref, d_ref, nm_ref, nv_ref):
        gv = g_ref[...]
        nm = ADAM_B1 * m_ref[...] + (1.0 - ADAM_B1) * gv
        nv = ADAM_B2 * v_ref[...] + (1.0 - ADAM_B2) * (gv * gv)
        m_hat = nm / (1.0 - ADAM_B1 ** ADAM_STEP)
        v_hat = nv / (1.0 - ADAM_B2 ** ADAM_STEP)
        d_ref[...] = -ADAM_LR * (m_hat / (jnp.sqrt(v_hat) + ADAM_EPS) + ADAM_WD * w_ref[...])
        nm_ref[...] = nm
        nv_ref[...] = nv

    blk = pl.BlockSpec(lead + (tr, tc), lambda i, j: zeros + (i, j))
    gblk = pl.BlockSpec((tr, tc), lambda i, j: (i, j))
    sh = jax.ShapeDtypeStruct(shape, F32)
    return pl.pallas_call(
        body, name=name, grid=(R // tr, C // tc), in_specs=[blk, gblk, blk, blk], out_specs=[blk] * 3, out_shape=[sh] * 3,
        compiler_params=_params("parallel", "parallel"),
    )(w, g2, m, v)


def _place():
    x, y, c = lax.axis_index("x"), lax.axis_index("y"), lax.axis_index("c")
    chips = [(1 - x, y), (x, 1 - y), (1 - x, 1 - y)]
    return x, y, c, chips


_HBM = pl.BlockSpec(memory_space=pltpu.HBM)


def allgather_weights(packs):
    n = len(packs)

    def body(*refs):
        in_refs, out_refs, (send_sems, recv_sems) = refs[:n], refs[n:2 * n], refs[2 * n:]
        x, y, c, chips = _place()
        me_s = 2 * x + y
        me, sibling = (x, y, c), (x, y, 1 - c)
        shards = [2 * chip[0] + chip[1] for chip in chips]

        def copy(a, k, shard, half, to, src=None):
            dst = out_refs[a].at[shard, half]
            return pltpu.make_async_remote_copy(src_ref=dst if src is None else src, dst_ref=dst,
                                                send_sem=send_sems.at[6 * a + k], recv_sem=recv_sems.at[6 * a + k],
                                                device_id=to, device_id_type=MESH)

        first = [copy(a, j, me_s, c, (*chip, c), src=in_refs[a].at[c]) for a in range(n) for j, chip in enumerate(chips)]
        for cp in first:
            cp.start()
        passed = []
        for a in range(n):
            for j in range(3):
                copy(a, j, shards[j], c, me).wait_recv()
                passed.append(copy(a, 3 + j, shards[j], c, sibling))
                passed[-1].start()
        for a in range(n):
            for j in range(3):
                copy(a, 3 + j, shards[j], 1 - c, me).wait_recv()
        for cp in first + passed:
            cp.wait_send()

    return pl.pallas_call(
        body, name="allgather_weights", in_specs=[_HBM] * n, out_specs=[_HBM] * n,
        out_shape=[jax.ShapeDtypeStruct((N_CHIP,) + p.shape, p.dtype) for p in packs],
        scratch_shapes=[pltpu.SemaphoreType.DMA((6 * n,)), pltpu.SemaphoreType.DMA((6 * n,))],
    )(*packs)


class _Rider:
    def __init__(self, inputs, out_shapes, n_sems, sends, recvs, aliases=None):
        self.inputs, self.out_shapes, self.n_sems = list(inputs), list(out_shapes), n_sems
        self.sends, self.recvs, self.aliases = sends, recvs, aliases or {}

    def start(self, *refs):
        for cp in self.sends(*refs):
            cp.start()

    def wait(self, *refs):
        for cp in self.recvs(*refs):
            cp.wait_recv()
        for cp in self.sends(*refs):
            cp.wait_send()


def _remote(src, dst, send_sems, recv_sems, k, to):
    return pltpu.make_async_remote_copy(src_ref=src, dst_ref=dst, send_sem=send_sems.at[k], recv_sem=recv_sems.at[k],
                                        device_id=to, device_id_type=MESH)


def _run_alone(rider, name):
    ri = len(rider.inputs)

    def body(*refs):
        ins, outs, (send_sems, recv_sems) = refs[:ri], refs[ri:-2], refs[-2:]
        rider.start(ins, outs, send_sems, recv_sems)
        rider.wait(ins, outs, send_sems, recv_sems)

    return pl.pallas_call(
        body, name=name, in_specs=[_HBM] * ri, out_specs=[_HBM] * len(rider.out_shapes), out_shape=rider.out_shapes,
        scratch_shapes=[pltpu.SemaphoreType.DMA((rider.n_sems,))] * 2, input_output_aliases=rider.aliases,
    )(*rider.inputs)


def _hosted_call(body, rider, *, name, grid, in_specs, out_specs, out_shape, scratch_shapes, semantics):
    if rider is None:
        return pl.pallas_call(body, name=name, grid=grid, in_specs=in_specs, out_specs=out_specs, out_shape=out_shape,
                              scratch_shapes=scratch_shapes, compiler_params=_params(*semantics))
    n_in, n_out, n_scr = len(in_specs), len(out_specs), len(scratch_shapes)
    ri, ro = len(rider.inputs), len(rider.out_shapes)

    def hosted(*refs):
        parts, p = [], 0
        for cnt in (n_in, ri, n_out, ro, n_scr, 2):
            parts.append(refs[p:p + cnt])
            p += cnt
        ins, rins, outs, routs, scr, (send_sems, recv_sems) = parts
        first = functools.reduce(jnp.logical_and, [pl.program_id(a) == 0 for a in range(len(grid))])
        last = functools.reduce(jnp.logical_and, [pl.program_id(a) == grid[a] - 1 for a in range(len(grid))])

        @pl.when(first)
        def _():
            rider.start(rins, routs, send_sems, recv_sems)

        body(*ins, *outs, *scr)

        @pl.when(last)
        def _():
            rider.wait(rins, routs, send_sems, recv_sems)

    call = pl.pallas_call(
        hosted, name=name, grid=grid, in_specs=list(in_specs) + [_HBM] * ri, out_specs=list(out_specs) + [_HBM] * ro,
        out_shape=list(out_shape) + rider.out_shapes,
        scratch_shapes=list(scratch_shapes) + [pltpu.SemaphoreType.DMA((rider.n_sems,))] * 2,
        input_output_aliases={n_in + i: n_out + o for i, o in rider.aliases.items()},
        compiler_params=_params(*(("arbitrary",) * len(grid))))

    def run(*args):
        res = call(*args, *rider.inputs)
        return res[:n_out], res[n_out:]

    return run


def _ride_gather_ici(packs):
    n = len(packs)

    def sends(ins, outs, send_sems, recv_sems):
        x, y, c, chips = _place()
        return [_remote(ins[a].at[c], outs[a].at[2 * x + y, c], send_sems, recv_sems, 3 * a + j, (*chip, c))
                for a in range(n) for j, chip in enumerate(chips)]

    def recvs(ins, outs, send_sems, recv_sems):
        x, y, c, chips = _place()
        return [_remote(ins[a].at[c], outs[a].at[2 * chip[0] + chip[1], c], send_sems, recv_sems, 3 * a + j, (x, y, c))
                for a in range(n) for j, chip in enumerate(chips)]

    return _Rider(packs, [jax.ShapeDtypeStruct((N_CHIP,) + p.shape, p.dtype) for p in packs], 3 * n, sends, recvs)


def _ride_gather_d2d(gathered):
    n = len(gathered)

    def copies(landing_half, to):
        def build(ins, outs, send_sems, recv_sems):
            x, y, c, chips = _place()
            return [_remote(ins[a].at[2 * chip[0] + chip[1], c], outs[a].at[2 * chip[0] + chip[1], landing_half(c)],
                            send_sems, recv_sems, 3 * a + j, to(x, y, c))
                    for a in range(n) for j, chip in enumerate(chips)]
        return build

    return _Rider(gathered, [jax.ShapeDtypeStruct(g.shape, g.dtype) for g in gathered], 3 * n,
                  copies(lambda c: c, lambda x, y, c: (x, y, 1 - c)), copies(lambda c: 1 - c, lambda x, y, c: (x, y, c)),
                  aliases={a: a for a in range(n)})


def _ride_exchange(gs):
    n = len(gs)
    shapes = [g.shape[1:] if g.ndim == 4 else g.shape[:2] + (g.shape[2] // 2,) for g in gs]

    def copies(ins, outs, send_sems, recv_sems):
        x, y, c, _ = _place()

        def theirs(a):
            if gs[a].ndim == 4:
                return ins[a].at[1 - c]
            cols = shapes[a][2]
            return ins[a].at[:, :, pl.ds((1 - c) * cols, cols)]

        return [_remote(theirs(a), outs[a], send_sems, recv_sems, a, (x, y, 1 - c)) for a in range(n)]

    return _Rider(gs, [jax.ShapeDtypeStruct(sh, g.dtype) for sh, g in zip(shapes, gs)], n, copies, copies)


def _ride_scatter(b16s, to=(0, 1, 2), landing=None):
    n = len(b16s)

    def sends(ins, outs, send_sems, recv_sems):
        x, y, c, chips = _place()
        return [_remote(ins[a].at[2 * chips[j][0] + chips[j][1]], outs[a].at[2 * x + y], send_sems, recv_sems, 3 * a + j,
                        (*chips[j], c)) for a in range(n) for j in to]

    def recvs(ins, outs, send_sems, recv_sems):
        x, y, c, chips = _place()
        return [_remote(ins[a].at[2 * x + y], outs[a].at[2 * chips[j][0] + chips[j][1]], send_sems, recv_sems, 3 * a + j,
                        (x, y, c)) for a in range(n) for j in to]

    return _Rider(list(b16s) + list(landing or []), [jax.ShapeDtypeStruct(b.shape, b.dtype) for b in b16s], 3 * n,
                  sends, recvs, aliases={n + a: a for a in range(n)} if landing else None)


def _slab_tile(r, cols):
    tr = _tile(r, 256, 16)
    if tr % 16 == 0:
        return tr, cols
    return r, _tile(cols, 128)


def add_halves(g, got, idx, name):
    ns, r, cols = got.shape
    tr, tc = _slab_tile(r, cols)
    if g.ndim == 4:
        mine = pl.BlockSpec((None, None, tr, tc), lambda s, i, j, idx_ref: (idx_ref[0], s, i, j))
    else:
        mine = pl.BlockSpec((None, tr, tc), lambda s, i, j, idx_ref: (s, i, idx_ref[0] * (cols // tc) + j))

    def body(idx_ref, a_ref, b_ref, o32_ref, o16_ref):
        s = a_ref[...] + b_ref[...]
        o32_ref[...] = s
        o16_ref[...] = s.astype(BF16)

    blk = pl.BlockSpec((None, tr, tc), lambda s, i, j, idx_ref: (s, i, j))
    return pl.pallas_call(
        body, name=name,
        grid_spec=pltpu.PrefetchScalarGridSpec(
            num_scalar_prefetch=1, grid=(ns, r // tr, cols // tc),
            in_specs=[mine, blk], out_specs=[blk, blk]),
        out_shape=[jax.ShapeDtypeStruct((ns, r, cols), F32), jax.ShapeDtypeStruct((ns, r, cols), BF16)],
        compiler_params=_params("parallel", "parallel", "parallel"),
    )(idx, g, got)


def add_chips(a32, got16, idx, name):
    ns, r, cols = a32.shape
    tr, tc = _slab_tile(r, cols)

    def body(idx_ref, a_ref, r1_ref, r2_ref, r3_ref, o_ref):
        o_ref[...] = ((a_ref[...] + r1_ref[...].astype(F32)) + r2_ref[...].astype(F32)) + r3_ref[...].astype(F32)

    def slab(k):
        return pl.BlockSpec((None, tr, tc), lambda i, j, idx_ref: ((idx_ref[1] + k) % ns, i, j))

    return pl.pallas_call(
        body, name=name,
        grid_spec=pltpu.PrefetchScalarGridSpec(
            num_scalar_prefetch=1, grid=(r // tr, cols // tc), in_specs=[slab(0), slab(1), slab(2), slab(3)],
            out_specs=pl.BlockSpec((tr, tc), lambda i, j, idx_ref: (i, j))),
        out_shape=jax.ShapeDtypeStruct((r, cols), F32),
        compiler_params=_params("parallel", "parallel"),
    )(idx, a32, got16, got16, got16)


def share_halves(halves):
    n = len(halves)

    def body(*refs):
        in_refs, out_refs, (send_sems, recv_sems) = refs[:n], refs[n:2 * n], refs[2 * n:]
        x, y, c, _ = _place()
        cps = [pltpu.make_async_remote_copy(src_ref=in_refs[a], dst_ref=out_refs[a], send_sem=send_sems.at[a],
                                            recv_sem=recv_sems.at[a], device_id=(x, y, 1 - c), device_id_type=MESH)
               for a in range(n)]
        for cp in cps:
            cp.start()
        for cp in cps:
            cp.wait()

    return pl.pallas_call(
        body, name="share_halves", in_specs=[_HBM] * n, out_specs=[_HBM] * n,
        out_shape=[jax.ShapeDtypeStruct(h.shape, F32) for h in halves],
        scratch_shapes=[pltpu.SemaphoreType.DMA((n,)), pltpu.SemaphoreType.DMA((n,))],
    )(*halves)


def allreduce_small(v):
    R, _ = v.shape

    def body(in_ref, out_ref, slots, send_sems, recv_sems):
        x, y, c, _ = _place()
        me = 4 * x + 2 * y + c
        slots[me] = in_ref[...]
        cps = []
        for k in range(1, N_DEV):
            to = (x ^ (k >> 2), y ^ ((k >> 1) & 1), c ^ (k & 1))
            cps.append(pltpu.make_async_remote_copy(src_ref=in_ref, dst_ref=slots.at[me], send_sem=send_sems.at[k - 1],
                                                    recv_sem=recv_sems.at[k - 1], device_id=to, device_id_type=MESH))
        for cp in cps:
            cp.start()
        for k in range(1, N_DEV):
            frm = 4 * (x ^ (k >> 2)) + 2 * (y ^ ((k >> 1) & 1)) + (c ^ (k & 1))
            pltpu.make_async_remote_copy(src_ref=in_ref, dst_ref=slots.at[frm], send_sem=send_sems.at[k - 1],
                                         recv_sem=recv_sems.at[k - 1], device_id=(x, y, c), device_id_type=MESH).wait_recv()
        for cp in cps:
            cp.wait_send()
        acc = slots[0]
        for d in range(1, N_DEV):
            acc = acc + slots[d]
        out_ref[...] = acc

    vm = pl.BlockSpec(memory_space=pltpu.VMEM)
    return pl.pallas_call(
        body, name="allreduce_small", in_specs=[vm], out_specs=vm, out_shape=jax.ShapeDtypeStruct((R, ROW), F32),
        scratch_shapes=[pltpu.VMEM((N_DEV, R, ROW), F32), pltpu.SemaphoreType.DMA((N_DEV - 1,)),
                        pltpu.SemaphoreType.DMA((N_DEV - 1,))],
    )(v)


def _rows_of(n, unit=16):
    return -(-n // (unit * ROW)) * unit


def _pack_rows(items, total_rows, dtype, unit=16):
    parts = []
    used = 0
    for a in items:
        flat = a.reshape(-1)
        r = _rows_of(flat.shape[0], unit)
        flat = jnp.pad(flat, (0, r * ROW - flat.shape[0]))
        parts.append(flat.reshape(r, ROW))
        used += r
    if total_rows > used:
        parts.append(jnp.zeros((total_rows - used, ROW), dtype))
    return jnp.concatenate(parts, axis=0)


def _unpack_rows(buf, shapes, unit=16):
    lead = buf.shape[:-2]
    out = []
    off = 0
    for shp in shapes:
        n = math.prod(shp)
        r = _rows_of(n, unit)
        piece = buf[..., off:off + r, :].reshape(*lead, r * ROW)[..., :n].reshape(*lead, *shp)
        out.append(piece)
        off += r
    return out


def _interleave_heads(w, H):
    lead = w.shape[:-1]
    return w.reshape(*lead, 3, H, HEAD).swapaxes(-3, -2).reshape(*lead, 3 * H * HEAD)


def _deinterleave_heads(w, H):
    lead = w.shape[:-1]
    return w.reshape(*lead, H, 3, HEAD).swapaxes(-3, -2).reshape(*lead, 3 * H * HEAD)


def _interleave_head_rows(w, H):
    return w.reshape(3, H, HEAD, w.shape[-1]).swapaxes(0, 1).reshape(3 * H * HEAD, w.shape[-1])


def _deinterleave_head_rows(w, H):
    return w.reshape(H, 3, HEAD, w.shape[-1]).swapaxes(0, 1).reshape(3 * H * HEAD, w.shape[-1])


def kernel(x, norm_mix, w_in, fox_f_bias, gdn_conv_w, gdn_a_log, gdn_dt_bias, gdn_norm, w_branch_fox, w_branch_gdn, w_out, norm_ffn, w_up, ffn_conv_w, w_down, norm_final, loss_target, m_norm_mix, m_w_in, m_fox_f_bias, m_gdn_conv_w, m_gdn_a_log, m_gdn_dt_bias, m_gdn_norm, m_w_branch_fox, m_w_branch_gdn, m_w_out, m_norm_ffn, m_w_up, m_ffn_conv_w, m_w_down, m_norm_final, v_norm_mix, v_w_in, v_fox_f_bias, v_gdn_conv_w, v_gdn_a_log, v_gdn_dt_bias, v_gdn_norm, v_w_branch_fox, v_w_branch_gdn, v_w_out, v_norm_ffn, v_w_up, v_ffn_conv_w, v_w_down, v_norm_final):
    B, S, D = x.shape
    T = B * S
    H = D // HEAD
    N = S // CHUNK
    FF = w_down.shape[1] * N_CHIP
    d_in = 9 * D + 3 * H
    assert w_in.shape[2] * N_CHIP == d_in and 3 * H <= 128

    cidx = lax.axis_index("c").astype(jnp.int32)
    sidx = (2 * lax.axis_index("x") + lax.axis_index("y")).astype(jnp.int32)
    idx = jnp.stack([cidx, sidx])

    rowed = [w_branch_fox[0], w_branch_gdn[0], w_out[0], w_down[0]]
    convs = [gdn_conv_w[0], ffn_conv_w[0]]
    rowed_shapes = [a.shape for a in rowed]
    conv_shapes = [a.shape + (2,) for a in convs]
    pad_rows = lambda shapes: -(-sum(_rows_of(math.prod(s)) for s in shapes) // 256) * 128
    Rh, Rc = pad_rows(rowed_shapes), pad_rows(conv_shapes)
    halves = lambda a: a.reshape(2, a.shape[0] // 2, a.shape[1])
    c_in = w_in.shape[2]
    packs_a = [w_in[0].T.astype(BF16).reshape(c_in, 2, D // 2).transpose(1, 0, 2),
               halves(_pack_rows([lax.bitcast_convert_type(a, BF16) for a in convs], 2 * Rc, BF16))]
    packs_b = [halves(w_up[0].astype(BF16)), halves(_pack_rows([a.astype(BF16) for a in rowed], 2 * Rh, BF16))]
    own = lambda gs, ps: [lax.dynamic_update_slice(g, p[None], (sidx, 0, 0, 0)) for g, p in zip(gs, ps)]
    by_cols = lambda g: g.transpose(1, 2, 0, 3).reshape(2 * g.shape[2], N_CHIP * g.shape[3])
    cat_cols = lambda p: jnp.concatenate([p[i] for i in range(N_CHIP)], axis=-1)
    cat_rows = lambda p: p.reshape(-1, p.shape[-1])
    g_in, g_conv = own(allgather_weights(packs_a), packs_a)
    W_inT = g_in.transpose(0, 2, 1, 3).reshape(N_CHIP * c_in, D)
    conv_parts = _unpack_rows(g_conv.reshape(N_CHIP, 2 * Rc, ROW), conv_shapes)
    gconv = cat_cols(lax.bitcast_convert_type(conv_parts[0], F32))
    fconv = cat_cols(lax.bitcast_convert_type(conv_parts[1], F32))

    o1, o2 = 3 * D, 3 * D + H
    o3, o4, o5, o6 = o2 + 3 * D, o2 + 3 * D + H, o2 + 3 * D + 2 * H, o2 + 4 * D + 2 * H
    W_foxT = _interleave_head_rows(W_inT[:o1], H)
    W_gqkvT = _interleave_head_rows(W_inT[o2:o3], H)
    W_gzT = W_inT[o5:o6]
    W_gatesT = W_inT[o6:]
    W_smallT = jnp.concatenate([W_inT[o1:o2], W_inT[o3:o5], jnp.zeros((128 - 3 * H, D), BF16)], axis=0)
    gconv_i = _interleave_heads(gconv, H)
    fconv_g, fconv_v = fconv[:, :FF], fconv[:, FF:]
    prm = jnp.zeros((8, 128), F32)
    prm = prm.at[0, 0:H].set(fox_f_bias[0]).at[0, H:2 * H].set(gdn_dt_bias[0]).at[1, H:2 * H].set(gdn_a_log[0])

    x2 = x.reshape(T, D)
    tgt = loss_target.reshape(T, D)

    hn1 = rmsnorm_fwd(x2, norm_mix, "rmsnorm_mix")
    p_fox = matmul(hn1, W_foxT, "nt", "proj_fox", out_dtype=BF16)
    p_gqkv = matmul(hn1, W_gqkvT, "nt", "proj_gqkv")
    p_gz = matmul(hn1, W_gzT, "nt", "proj_gz")
    p_gates = matmul(hn1, W_gatesT, "nt", "proj_gates")
    p_small = matmul(hn1, W_smallT, "nt", "proj_small")

    sm = small_fwd(p_small, prm, B, S, H)
    heads = lambda a: a.reshape(B, S, H).transpose(0, 2, 1)
    c_bhs, gc_bhs, beta_bhs = heads(sm[:, 0:H]), heads(sm[:, H:2 * H]), heads(sm[:, 2 * H:3 * H])
    c_col, c_row = c_bhs[..., None], c_bhs[:, :, None, :]
    gcr5 = gc_bhs.reshape(B, H, N, 1, CHUNK)
    gcr_u = gc_bhs.reshape(B, H, N // PAIR, 1, PAIR * CHUNK)
    betar_u = beta_bhs.reshape(B, H, N // PAIR, 1, PAIR * CHUNK)

    (o_fox, o_fox16, lse), arriving = fox_fwd(p_fox, c_col, c_row, B, S, H, rider=_ride_gather_ici(packs_b))
    qkvn = gdn_prep_fwd(p_gqkv, gconv_i, B, S, H)
    (u_hat, w_t, t_inv), arrived = gdn_intra_fwd(qkvn, betar_u, gcr_u, B, S, H, rider=_ride_gather_d2d(arriving))
    g_up, g_rowed = own(arrived, packs_b)
    W_up = by_cols(g_up)
    W_up_g, W_up_v = W_up[:, :FF], W_up[:, FF:]
    W_bf, W_bg, W_out, W_down = (cat_rows(p) for p in _unpack_rows(g_rowed.reshape(N_CHIP, 2 * Rh, ROW), rowed_shapes))
    o_gdn, states, y_gdn = gdn_inter_fwd(qkvn, u_hat, w_t, gcr5, p_gz, gdn_norm, B, S, H)
    bf_ = matmul(o_fox16, W_bf, "nn", "branch_fox")
    bg_, y = matmul(y_gdn, W_bg, "nn", "branch_gdn", post=_post_merge(p_gates, bf_))
    h1, hn2 = matmul(y, W_out, "nn", "out_proj", add=x2, post=_post_rmsnorm(norm_ffn))
    up_g = matmul(hn2, W_up_g, "nn", "up_gate")
    up_v = matmul(hn2, W_up_v, "nn", "up_val")
    act = ffn_gate_fwd(up_g, up_v, fconv_g, fconv_v, B, S)
    dh2, dh2_16, loss_cols, d_norm_final = matmul(act, W_down, "nn", "down_proj", add=h1,
                                                  post=_post_loss(norm_final.reshape(1, D), tgt))
    loss_here = (0.5 / D) * jnp.sum(loss_cols)

    d_act = matmul(dh2_16, W_down, "nt", "d_act")
    dW_down = matmul(act, dh2_16, "tn", "dw_down")
    d_upg, d_upv, d_fconv_g, d_fconv_v = ffn_gate_bwd(up_g, up_v, fconv_g, fconv_v, d_act, B, S)
    d_hn2 = matmul(d_upg, W_up_g, "nt", "d_hn2_g")
    dh1, dh1_16, d_norm_ffn = matmul(d_upv, W_up_v, "nt", "d_hn2_v", add=d_hn2,
                                     post=_post_rmsnorm_bwd(h1, norm_ffn, dh2, True))
    dW_up_slabs = jnp.concatenate([matmul(hn2, d_upg, "tn", "dw_up_g", slab=(D // 2, 2 * FF // N_CHIP)),
                                   matmul(hn2, d_upv, "tn", "dw_up_v", slab=(D // 2, 2 * FF // N_CHIP))], axis=1)
    d_bf, d_bg, d_gates = matmul(dh1_16, W_out, "nt", "d_y", post=_post_merge_bwd(p_gates, bf_, bg_))
    dW_out = matmul(y, dh1_16, "tn", "dw_out")
    d_ofox = matmul(d_bf, W_bf, "nt", "d_ofox")
    dW_bf = matmul(o_fox16, d_bf, "tn", "dw_bf")
    d_ygdn = matmul(d_bg, W_bg, "nt", "d_ygdn")
    dW_bg = matmul(y_gdn, d_bg, "tn", "dw_bg")

    d_fconv = jnp.concatenate([d_fconv_g, d_fconv_v], axis=1)
    col_shard = lambda g, s: g[:, s * (g.shape[1] // N_CHIP):(s + 1) * (g.shape[1] // N_CHIP)]
    row_shard = lambda g, s: g[s * (g.shape[0] // N_CHIP):(s + 1) * (g.shape[0] // N_CHIP)]
    shard_items = lambda s: [row_shard(dW_bf, s), row_shard(dW_bg, s), row_shard(dW_out, s), row_shard(dW_down, s),
                             col_shard(d_fconv, s)]
    g_shapes = [a.shape for a in shard_items(0)]
    assert sum(_rows_of(math.prod(s)) for s in g_shapes) <= 2 * Rh
    to_slabs = lambda g: g.reshape(2, g.shape[0] // 2, N_CHIP, g.shape[1] // N_CHIP).transpose(0, 2, 1, 3)
    gpacks_b = [dW_up_slabs,
                jnp.stack([_pack_rows(shard_items(s), 2 * Rh, F32).reshape(2, Rh, ROW) for s in range(N_CHIP)], axis=1)]
    (d_pfox, d_ccol, d_crow), gots_b = fox_bwd(p_fox, c_col, c_row, o_fox, lse, d_ofox, B, S, H,
                                              rider=_ride_exchange(gpacks_b))
    sums_b = [add_halves(g, got, idx, "add_halves_b%d" % i) for i, (g, got) in enumerate(zip(gpacks_b, gots_b))]

    (dq_i, dk_i, d_uh, d_wt, dgcr_a, d_gz, d_gn_parts), got16_b = gdn_inter_bwd(
        qkvn, u_hat, w_t, gcr5, states, o_gdn, p_gz, gdn_norm, d_ygdn, B, S, H,
        rider=_ride_scatter([s16 for _, s16 in sums_b]))
    d_gdn_norm = jnp.sum(d_gn_parts[:, :, 0, :], axis=(0, 1))[None]
    mine_b = [add_chips(s32, g16, idx, "add_chips_b%d" % i) for i, ((s32, _), g16) in enumerate(zip(sums_b, got16_b))]
    d_qkvn, d_betar5, dgcr_b = gdn_intra_bwd(qkvn, betar_u, gcr_u, t_inv, d_uh, d_wt, dq_i, dk_i, B, S, H)
    d_pgqkv, d_gconv_i = gdn_prep_bwd(p_gqkv, gconv_i, d_qkvn, B, S, H)

    tokens = lambda a: a.reshape(B, H, S).transpose(0, 2, 1).reshape(T, H)
    d_gc = dgcr_a.reshape(B, H, S) + dgcr_b.reshape(B, H, S)
    d_sm = jnp.concatenate([tokens(d_ccol.reshape(B, H, S) + d_crow.reshape(B, H, S)), tokens(d_gc), tokens(d_betar5.reshape(B, H, S)),
                            jnp.zeros((T, 128 - 3 * H), F32)], axis=1)
    d_psmall, d_prm = small_bwd(p_small, prm, d_sm, B, S, H)

    dW_foxT = matmul(d_pfox, hn1, "tn", "dw_fox")
    dW_gqkvT = matmul(d_pgqkv, hn1, "tn", "dw_gqkv")
    dW_gzT = matmul(d_gz, hn1, "tn", "dw_gz")
    dW_gatesT = matmul(d_gates, hn1, "tn", "dw_gates")
    dW_smallT = matmul(d_psmall, hn1, "tn", "dw_small")
    dW_inT = jnp.concatenate([_deinterleave_head_rows(dW_foxT, H), dW_smallT[0:H], _deinterleave_head_rows(dW_gqkvT, H),
                              dW_smallT[H:3 * H], dW_gzT, dW_gatesT], axis=0)
    d_gconv = _deinterleave_heads(d_gconv_i, H)

    gpack_a = [dW_inT.reshape(N_CHIP, c_in, D)]
    d_hn1, gots_a = matmul(d_pfox, W_foxT, "nn", "d_hn1_fox", rider=_ride_exchange(gpack_a))
    sums_a = [add_halves(gpack_a[0], gots_a[0], idx, "add_halves_a")]
    d_hn1, landing_a = matmul(d_pgqkv, W_gqkvT, "nn", "d_hn1_gqkv", add=d_hn1,
                              rider=_ride_scatter([sums_a[0][1]], to=(0, 1)))
    d_hn1 = matmul(d_gz, W_gzT, "nn", "d_hn1_gz", add=d_hn1)
    d_hn1, got16_a = matmul(d_gates, W_gatesT, "nn", "d_hn1_gates", add=d_hn1,
                            rider=_ride_scatter([sums_a[0][1]], to=(2,), landing=landing_a))
    mine = [add_chips(sums_a[0][0], got16_a[0], idx, "add_chips_a")] + mine_b
    grad_x, d_norm_mix = matmul(d_psmall, W_smallT, "nn", "d_hn1_small", add=d_hn1,
                                post=_post_rmsnorm_bwd(x2, norm_mix, dh1, False))

    others = share_halves(mine)
    g_w_inT, g_up, g_rows = (jnp.concatenate([jnp.where(cidx == 0, h, o), jnp.where(cidx == 0, o, h)], axis=ax)
                             for h, o, ax in zip(mine, others, (1, 0, 0)))
    g_w_in = g_w_inT.T
    g_bf, g_bg, g_out, g_down, g_fconv = _unpack_rows(g_rows, g_shapes)

    small_items = [d_norm_mix, d_norm_ffn, d_norm_final, d_gdn_norm, d_prm, d_gconv, loss_here.reshape(1, 1)]
    small_shapes = [a.shape for a in small_items]
    sv = allreduce_small(_pack_rows(small_items, 0, F32, unit=8))
    g_norm_mix, g_norm_ffn, g_norm_final, g_gdn_norm, g_prm, g_gconv_all, loss = _unpack_rows(sv, small_shapes, unit=8)
    loss = loss[0, 0]
    g_norm_final = g_norm_final.reshape(D)
    g_fbias, g_dtb, g_alog = g_prm[0:1, 0:H], g_prm[0:1, H:2 * H], g_prm[1:2, H:2 * H]
    g_gconv = lax.dynamic_slice_in_dim(g_gconv_all, sidx * (3 * D // N_CHIP), 3 * D // N_CHIP, axis=1)

    names = ["norm_mix", "w_in", "fox_f_bias", "gdn_conv_w", "gdn_a_log", "gdn_dt_bias", "gdn_norm", "w_branch_fox",
             "w_branch_gdn", "w_out", "norm_ffn", "w_up", "ffn_conv_w", "w_down", "norm_final"]
    ws = [norm_mix, w_in, fox_f_bias, gdn_conv_w, gdn_a_log, gdn_dt_bias, gdn_norm, w_branch_fox, w_branch_gdn, w_out,
          norm_ffn, w_up, ffn_conv_w, w_down, norm_final]
    ms = [m_norm_mix, m_w_in, m_fox_f_bias, m_gdn_conv_w, m_gdn_a_log, m_gdn_dt_bias, m_gdn_norm, m_w_branch_fox,
          m_w_branch_gdn, m_w_out, m_norm_ffn, m_w_up, m_ffn_conv_w, m_w_down, m_norm_final]
    vs = [v_norm_mix, v_w_in, v_fox_f_bias, v_gdn_conv_w, v_gdn_a_log, v_gdn_dt_bias, v_gdn_norm, v_w_branch_fox,
          v_w_branch_gdn, v_w_out, v_norm_ffn, v_w_up, v_ffn_conv_w, v_w_down, v_norm_final]
    gs = [g_norm_mix, g_w_in, g_fbias, g_gconv, g_alog, g_dtb, g_gdn_norm, g_bf, g_bg, g_out, g_norm_ffn, g_up,
          g_fconv, g_down, g_norm_final]
    gs = [g.reshape(w.shape) for g, w in zip(gs, ws)]
    deltas, new_ms, new_vs = [], [], []
    for nm, w, g, m, v in zip(names, ws, gs, ms, vs):
        if w.ndim == 1:
            d, a, b = adamw(w.reshape(1, -1), g.reshape(1, -1), m.reshape(1, -1), v.reshape(1, -1), "adamw_" + nm)
            d, a, b = d.reshape(w.shape), a.reshape(w.shape), b.reshape(w.shape)
        elif nm == "w_in":
            d, a, b = (r.T[None] for r in adamw(w[0].T, g_w_inT, m[0].T, v[0].T, "adamw_" + nm))
        else:
            d, a, b = adamw(w, g, m, v, "adamw_" + nm)
        deltas.append(d)
        new_ms.append(a)
        new_vs.append(b)

    return (loss, grad_x.reshape(B, S, D), *gs, *deltas, *new_ms, *new_vs)
```

```python
import functools
import math

import jax
import jax.numpy as jnp
from jax import lax
from jax.experimental import pallas as pl
from jax.experimental.pallas import tpu as pltpu

F32 = jnp.float32
BF16 = jnp.bfloat16
HEAD = 128
CHUNK = 64
GDN_CONV = 4
FFN_CONV = 3
EPS = 1e-6
NEG = -1e30
ROW = 1024
ATT_TILE = 512
MM_WEIGHT_TILE_BYTES = 8 << 20
MM_TN_OPERAND_BYTES = 32 << 20
N_CHIP = 4
N_DEV = 8
MESH = pl.DeviceIdType.MESH
HI = lax.Precision.HIGH
EXACT = lax.Precision.HIGHEST

ADAM_LR, ADAM_B1, ADAM_B2, ADAM_EPS, ADAM_WD, ADAM_STEP = 0.001, 0.9, 0.999, 1e-08, 0.01, 10


def _tile(n, cap, unit=128):
    best = None
    t = unit
    while t <= min(n, cap):
        if n % t == 0:
            best = t
        t += unit
    return best if best is not None else n


def _params(*sem):
    return pltpu.CompilerParams(dimension_semantics=sem)


_NN = (((1,), (0,)), ((), ()))
_NT = (((1,), (1,)), ((), ()))
_TN = (((0,), (0,)), ((), ()))


def _dg(a, b, dims, hi):
    if hi:
        return lax.dot_general(a, b, dims, precision=HI, preferred_element_type=F32)
    return lax.dot_general(a.astype(BF16), b.astype(BF16), dims, preferred_element_type=F32)


class _RawOps:
    @staticmethod
    def nn(a, b, hi=False):
        return _dg(a, b, _NN, hi)

    @staticmethod
    def nt(a, b, hi=False):
        return _dg(a, b, _NT, hi)

    @staticmethod
    def tn(a, b, hi=False):
        return _dg(a, b, _TN, hi)


def _make_diff_ops():
    def build(hi):
        @jax.custom_vjp
        def nn(a, b):
            return _dg(a, b, _NN, hi)

        nn.defvjp(lambda a, b: (_dg(a, b, _NN, hi), (a, b)),
                  lambda r, g: (_dg(g, r[1], _NT, hi), _dg(r[0], g, _TN, hi)))

        @jax.custom_vjp
        def nt(a, b):
            return _dg(a, b, _NT, hi)

        nt.defvjp(lambda a, b: (_dg(a, b, _NT, hi), (a, b)),
                  lambda r, g: (_dg(g, r[1], _NN, hi), _dg(g, r[0], _TN, hi)))

        @jax.custom_vjp
        def tn(a, b):
            return _dg(a, b, _TN, hi)

        tn.defvjp(lambda a, b: (_dg(a, b, _TN, hi), (a, b)),
                  lambda r, g: (_dg(r[1], g, _NT, hi), _dg(r[0], g, _NN, hi)))
        return nn, nt, tn

    lo, hi_ = build(False), build(True)

    class _DiffOps:
        @staticmethod
        def nn(a, b, hi=False):
            return (hi_ if hi else lo)[0](a, b)

        @staticmethod
        def nt(a, b, hi=False):
            return (hi_ if hi else lo)[1](a, b)

        @staticmethod
        def tn(a, b, hi=False):
            return (hi_ if hi else lo)[2](a, b)

    return _DiffOps


_DiffOps = _make_diff_ops()


def _sigmoid(x):
    return 1.0 / (1.0 + jnp.exp(-x))


def _mm_tile(n, pref):
    if n % pref == 0:
        return pref
    if n % 1408 == 0:
        return 1408
    return _tile(n, pref)


class _Post:
    def __init__(self, fn, row_ins=(), vec_ins=(), row_outs=(), acc_outs=(), keep_main=True):
        self.fn, self.keep_main = fn, keep_main
        self.row_ins = [r if isinstance(r, tuple) else (r, r.shape[1], 0) for r in row_ins]
        self.vec_ins, self.row_outs, self.acc_outs = list(vec_ins), list(row_outs), list(acc_outs)


def matmul(a, b, mode, name, add=None, out_dtype=F32, post=None, rider=None, slab=None):
    if mode == "nn":
        (M, K), (K2, N) = a.shape, b.shape
    elif mode == "nt":
        (M, K), (N, K2) = a.shape, b.shape
    else:
        (K, M), (K2, N) = a.shape, b.shape
    assert K == K2, (name, a.shape, b.shape)
    tn = slab[1] if slab else _mm_tile(N, 1024)
    if mode == "tn":
        tm = slab[0] if slab else (M if M <= 1408 else _mm_tile(M, 1408))
        row_bytes = 2 * (tm * a.dtype.itemsize + tn * b.dtype.itemsize)
        tk = next((t for t in (4096, 2048) if K % t == 0 and t * row_bytes <= MM_TN_OPERAND_BYTES), _mm_tile(K, 1024))
    else:
        tk = K if K * tn * 2 <= MM_WEIGHT_TILE_BYTES else _mm_tile(K, 1024)
        tm = _mm_tile(M, 1024 if tk <= 2048 and post is None else 512)
    nk = K // tk
    assert post is None or (mode != "tn" and tn == N), name
    dims = {"nn": _NN, "nt": _NT, "tn": _TN}[mode]
    if mode == "tn":
        a_spec = pl.BlockSpec((tk, tm), lambda j, i, k: (k, i))
    else:
        a_spec = pl.BlockSpec((tm, tk), lambda j, i, k: (i, k))
    if mode == "nt":
        b_spec = pl.BlockSpec((tn, tk), lambda j, i, k: (j, k))
    else:
        b_spec = pl.BlockSpec((tk, tn), lambda j, i, k: (k, j))
    o_spec = pl.BlockSpec((tm, tn), lambda j, i, k: (i, j))
    has_add = add is not None
    keep_main = post is None or post.keep_main
    counts = [2 + has_add] + ([len(post.row_ins), len(post.vec_ins)] if post else [0, 0]) + [int(keep_main)]
    counts += ([len(post.row_outs), len(post.acc_outs)] if post else [0, 0]) + [int(nk > 1)]

    def body(*refs):
        parts, p = [], 0
        for cnt in counts:
            parts.append(refs[p:p + cnt])
            p += cnt
        core, row_ins, vec_ins, main, row_outs, acc_outs, acc = parts
        a_ref, b_ref = core[:2]
        prod = lax.dot_general(a_ref[...].astype(BF16), b_ref[...].astype(BF16), dims, preferred_element_type=F32)

        def finish(r):
            if has_add:
                r = r + core[2][...]
            if keep_main:
                main[0][...] = r.astype(out_dtype)
            if post is not None:
                @pl.when(pl.program_id(1) == 0)
                def _():
                    for ref in acc_outs:
                        ref[...] = jnp.zeros_like(ref)

                post.fn(r, row_ins, vec_ins, row_outs, acc_outs)

        if nk == 1:
            finish(prod)
            return
        acc_ref = acc[0]
        k = pl.program_id(2)

        @pl.when(k == 0)
        def _():
            acc_ref[...] = jnp.zeros_like(acc_ref)

        acc_ref[...] += prod

        @pl.when(k == nk - 1)
        def _():
            finish(acc_ref[...])

    in_specs = [a_spec, b_spec] + ([o_spec] if has_add else [])
    args = (a, b) + ((add,) if has_add else ())
    out_specs = [o_spec] if keep_main else []
    out_shape = [jax.ShapeDtypeStruct((M, N), out_dtype)] if keep_main else []
    if slab:
        out_specs = [pl.BlockSpec((None, None, tm, tn), lambda j, i, k: (i, j, 0, 0))]
        out_shape = [jax.ShapeDtypeStruct((M // tm, N // tn, tm, tn), out_dtype)]
    if post is not None:
        in_specs += [pl.BlockSpec((tm, cols), lambda j, i, k, cb=cb: (i, cb)) for _, cols, cb in post.row_ins]
        in_specs += [pl.BlockSpec((1, v.shape[1]), lambda j, i, k: (0, 0)) for v in post.vec_ins]
        args += tuple(r for r, _, _ in post.row_ins) + tuple(post.vec_ins)
        out_specs += [pl.BlockSpec((tm, cols), lambda j, i, k: (i, 0)) for cols, _ in post.row_outs]
        out_specs += [pl.BlockSpec((1, cols), lambda j, i, k: (0, 0)) for cols in post.acc_outs]
        out_shape += [jax.ShapeDtypeStruct((M, cols), dt) for cols, dt in post.row_outs]
        out_shape += [jax.ShapeDtypeStruct((1, cols), F32) for cols in post.acc_outs]
    rows_sem = "arbitrary" if post is not None and post.acc_outs else "parallel"
    res = _hosted_call(
        body, rider, name=name, grid=(N // tn, M // tm, nk), in_specs=in_specs, out_specs=out_specs, out_shape=out_shape,
        scratch_shapes=[pltpu.VMEM((tm, tn), F32)] if nk > 1 else [], semantics=("parallel", rows_sem, "arbitrary"),
    )(*args)
    if rider is not None:
        res, carried = res
        return (res[0] if post is None else res), carried
    return res[0] if post is None else res


def rmsnorm_fwd(x, g, name):
    T, D = x.shape
    tm = _tile(T, 512, 8)

    def body(x_ref, g_ref, o_ref):
        xv = x_ref[...]
        r = lax.rsqrt(jnp.mean(xv * xv, axis=-1, keepdims=True) + EPS)
        o_ref[...] = (xv * r * g_ref[...]).astype(BF16)

    return pl.pallas_call(
        body, name=name, grid=(T // tm,),
        in_specs=[pl.BlockSpec((tm, D), lambda i: (i, 0)), pl.BlockSpec((1, D), lambda i: (0, 0))],
        out_specs=pl.BlockSpec((tm, D), lambda i: (i, 0)),
        out_shape=jax.ShapeDtypeStruct((T, D), BF16),
        compiler_params=_params("parallel"),
    )(x, g)


def _post_rmsnorm(g):
    def fn(r, row_ins, vec_ins, row_outs, acc_outs):
        rs = lax.rsqrt(jnp.mean(r * r, axis=-1, keepdims=True) + EPS)
        row_outs[0][...] = (r * rs * vec_ins[0][...]).astype(BF16)

    return _Post(fn, vec_ins=[g], row_outs=[(g.shape[1], BF16)])


def _post_rmsnorm_bwd(x, g, dres, with_bf16):
    D = g.shape[1]

    def fn(dy, row_ins, vec_ins, row_outs, acc_outs):
        xv = row_ins[0][...]
        rs = lax.rsqrt(jnp.mean(xv * xv, axis=-1, keepdims=True) + EPS)
        xh = xv * rs
        acc_outs[0][...] += jnp.sum(dy * xh, axis=0, keepdims=True)
        dxh = dy * vec_ins[0][...]
        dx = row_ins[1][...] + rs * (dxh - xh * jnp.mean(dxh * xh, axis=-1, keepdims=True))
        row_outs[0][...] = dx
        if with_bf16:
            row_outs[1][...] = dx.astype(BF16)

    return _Post(fn, row_ins=[x, dres], vec_ins=[g], row_outs=[(D, F32)] + ([(D, BF16)] if with_bf16 else []),
                 acc_outs=[D], keep_main=False)


def _post_loss(g, target):
    D = g.shape[1]

    def fn(hv, row_ins, vec_ins, row_outs, acc_outs):
        rs = lax.rsqrt(jnp.mean(hv * hv, axis=-1, keepdims=True) + EPS)
        xh = hv * rs
        gv = vec_ins[0][...]
        err = xh * gv - row_ins[0][...]
        acc_outs[0][...] += jnp.sum(err * err, axis=0, keepdims=True)
        dy = err * (1.0 / D)
        acc_outs[1][...] += jnp.sum(dy * xh, axis=0, keepdims=True)
        dxh = dy * gv
        dh = rs * (dxh - xh * jnp.mean(dxh * xh, axis=-1, keepdims=True))
        row_outs[0][...] = dh
        row_outs[1][...] = dh.astype(BF16)

    return _Post(fn, row_ins=[target], vec_ins=[g], row_outs=[(D, F32), (D, BF16)], acc_outs=[D, D], keep_main=False)


def _shift_down(x, k):
    if k == 0:
        return x
    rows = lax.broadcasted_iota(jnp.int32, x.shape, 0)
    return jnp.where(rows >= k, pltpu.roll(x, k, 0), 0.0)


def _shift_up(x, k):
    if k == 0:
        return x
    s = x.shape[0]
    rows = lax.broadcasted_iota(jnp.int32, x.shape, 0)
    return jnp.where(rows < s - k, pltpu.roll(x, s - k, 0), 0.0)


def _conv_fwd(x, w_ref, kw, keep_shifted=False):
    shifted = [_shift_down(x, kw - 1 - i) for i in range(kw - 1)]
    y = x * w_ref[kw - 1:kw, :]
    for i in range(kw - 1):
        y = y + shifted[i] * w_ref[i:i + 1, :]
    return (y, shifted) if keep_shifted else y


def _conv_bwd(x, shifted, dy, w_ref, kw):
    dx = dy * w_ref[kw - 1:kw, :]
    dws = []
    for i in range(kw - 1):
        dx = dx + _shift_up(dy, kw - 1 - i) * w_ref[i:i + 1, :]
        dws.append(jnp.sum(dy * shifted[i], axis=0, keepdims=True))
    dws.append(jnp.sum(dy * x, axis=0, keepdims=True))
    return dx, dws


def ffn_gate_fwd(up_g, up_v, cw_g, cw_v, B, S):
    T, Fd = up_g.shape
    tc = _tile(Fd, 256)

    def body(g_ref, v_ref, wg_ref, wv_ref, o_ref):
        ug = _conv_fwd(g_ref[...], wg_ref, FFN_CONV)
        uv = _conv_fwd(v_ref[...], wv_ref, FFN_CONV)
        o_ref[...] = (ug * _sigmoid(ug) * uv).astype(BF16)

    blk = pl.BlockSpec((S, tc), lambda b, j: (b, j))
    wblk = pl.BlockSpec((FFN_CONV, tc), lambda b, j: (0, j))
    return pl.pallas_call(
        body, name="ffn_gate_fwd", grid=(B, Fd // tc), in_specs=[blk, blk, wblk, wblk], out_specs=blk,
        out_shape=jax.ShapeDtypeStruct((T, Fd), BF16), compiler_params=_params("parallel", "parallel"),
    )(up_g, up_v, cw_g, cw_v)


def ffn_gate_bwd(up_g, up_v, cw_g, cw_v, d_act, B, S):
    T, Fd = up_g.shape
    tc = _tile(Fd, 256)

    def body(g_ref, v_ref, wg_ref, wv_ref, da_ref, dg_ref, dv_ref, dwg_ref, dwv_ref):
        @pl.when(pl.program_id(1) == 0)
        def _():
            dwg_ref[...] = jnp.zeros_like(dwg_ref)
            dwv_ref[...] = jnp.zeros_like(dwv_ref)

        xg, xv = g_ref[...], v_ref[...]
        ug, sh_g = _conv_fwd(xg, wg_ref, FFN_CONV, keep_shifted=True)
        uv, sh_v = _conv_fwd(xv, wv_ref, FFN_CONV, keep_shifted=True)
        da = da_ref[...]
        sg = _sigmoid(ug)
        d_ug = da * uv * (sg + ug * sg * (1.0 - sg))
        d_uv = da * ug * sg
        dxg, dwg = _conv_bwd(xg, sh_g, d_ug, wg_ref, FFN_CONV)
        dxv, dwv = _conv_bwd(xv, sh_v, d_uv, wv_ref, FFN_CONV)
        dg_ref[...] = dxg.astype(BF16)
        dv_ref[...] = dxv.astype(BF16)
        for i in range(FFN_CONV):
            dwg_ref[i:i + 1, :] += dwg[i]
            dwv_ref[i:i + 1, :] += dwv[i]

    blk = pl.BlockSpec((S, tc), lambda j, b: (b, j))
    wblk = pl.BlockSpec((FFN_CONV, tc), lambda j, b: (0, j))
    return pl.pallas_call(
        body, name="ffn_gate_bwd", grid=(Fd // tc, B), in_specs=[blk, blk, wblk, wblk, blk],
        out_specs=[blk, blk, wblk, wblk],
        out_shape=[jax.ShapeDtypeStruct((T, Fd), BF16), jax.ShapeDtypeStruct((T, Fd), BF16),
                   jax.ShapeDtypeStruct((FFN_CONV, Fd), F32), jax.ShapeDtypeStruct((FFN_CONV, Fd), F32)],
        compiler_params=_params("parallel", "arbitrary"),
    )(up_g, up_v, cw_g, cw_v, d_act)


def _post_merge(p_gates, bf_):
    D = bf_.shape[1]

    def fn(bg, row_ins, vec_ins, row_outs, acc_outs):
        gf_ref, gg_ref, bf_ref = row_ins
        row_outs[0][...] = (_sigmoid(gf_ref[...]) * bf_ref[...] + _sigmoid(gg_ref[...]) * bg).astype(BF16)

    return _Post(fn, row_ins=[(p_gates, D, 0), (p_gates, D, 1), bf_], row_outs=[(D, BF16)])


def _post_merge_bwd(p_gates, bf_, bg_):
    D = bf_.shape[1]

    def fn(d, row_ins, vec_ins, row_outs, acc_outs):
        gf_ref, gg_ref, bf_ref, bg_ref = row_ins
        sf, sg = _sigmoid(gf_ref[...]), _sigmoid(gg_ref[...])
        row_outs[0][...] = (d * sf).astype(BF16)
        row_outs[1][...] = (d * sg).astype(BF16)
        row_outs[2][:, 0:D] = (d * bf_ref[...] * sf * (1.0 - sf)).astype(BF16)
        row_outs[2][:, D:2 * D] = (d * bg_ref[...] * sg * (1.0 - sg)).astype(BF16)

    return _Post(fn, row_ins=[(p_gates, D, 0), (p_gates, D, 1), bf_, bg_],
                 row_outs=[(D, BF16), (D, BF16), (2 * D, BF16)], keep_main=False)


ATT_HEADS = 2


def fox_fwd(p_fox, c_col, c_row, B, S, H, rider=None):
    T = B * S
    t = _tile(S, ATT_TILE)
    nq = S // t
    scale = HEAD ** -0.5
    hp = ATT_HEADS if H % ATT_HEADS == 0 else 1
    hs = range(hp)

    def body(*refs):
        qkv_refs, (cq_ref, cr_ref, o_ref, o16_ref, lse_ref) = refs[:3 * hp], refs[3 * hp:]
        i = pl.program_id(2)
        q = [qkv_refs[3 * hh][...] for hh in hs]
        row = lax.broadcasted_iota(jnp.int32, (t, t), 0)
        col = lax.broadcasted_iota(jnp.int32, (t, t), 1)

        def step(j, carry, diagonal):
            off = pl.multiple_of(j * t, t)
            k = [qkv_refs[3 * hh + 1][pl.ds(off, t), :] for hh in hs]
            v = [qkv_refs[3 * hh + 2][pl.ds(off, t), :] for hh in hs]
            s = [lax.dot_general(q[hh], k[hh], _NT, preferred_element_type=F32) * scale - cr_ref[hh, :, pl.ds(off, t)]
                 for hh in hs]
            if diagonal:
                s = [jnp.where(col <= row, s[hh], NEG) for hh in hs]
            m_new = [jnp.maximum(carry[hh][0], jnp.max(s[hh], axis=-1, keepdims=True)) for hh in hs]
            alpha = [jnp.exp(carry[hh][0] - m_new[hh]) for hh in hs]
            p = [jnp.exp(s[hh] - m_new[hh]) for hh in hs]
            l = [alpha[hh] * carry[hh][1] + jnp.sum(p[hh], axis=-1, keepdims=True) for hh in hs]
            acc = [alpha[hh] * carry[hh][2] + lax.dot_general(p[hh].astype(BF16), v[hh], _NN, preferred_element_type=F32)
                   for hh in hs]
            return tuple((m_new[hh], l[hh], acc[hh]) for hh in hs)

        start = (jnp.full((t, 1), NEG, F32), jnp.zeros((t, 1), F32), jnp.zeros((t, HEAD), F32))
        below = lax.fori_loop(0, i, functools.partial(step, diagonal=False), tuple(start for _ in hs))
        done = step(i, below, diagonal=True)
        for hh, (m, l, acc) in enumerate(done):
            cols = slice(hh * HEAD, (hh + 1) * HEAD)
            o = acc / l
            o_ref[:, cols] = o
            o16_ref[:, cols] = o.astype(BF16)
            lse_ref[hh] = cq_ref[hh] + m + jnp.log(l)

    qkv = []
    for hh in hs:
        qkv.append(pl.BlockSpec((t, HEAD), lambda b, g, i, hh=hh: (b * nq + i, 3 * (hp * g + hh))))
        qkv.append(pl.BlockSpec((S, HEAD), lambda b, g, i, hh=hh: (b, 3 * (hp * g + hh) + 1)))
        qkv.append(pl.BlockSpec((S, HEAD), lambda b, g, i, hh=hh: (b, 3 * (hp * g + hh) + 2)))
    heads = pl.BlockSpec((t, hp * HEAD), lambda b, g, i: (b * nq + i, g))
    return _hosted_call(
        body, rider, name="fox_fwd", grid=(B, H // hp, nq),
        in_specs=qkv + [pl.BlockSpec((None, hp, t, 1), lambda b, g, i: (b, g, i, 0)),
                        pl.BlockSpec((None, hp, 1, S), lambda b, g, i: (b, g, 0, 0))],
        out_specs=[heads, heads, pl.BlockSpec((None, hp, t, 1), lambda b, g, i: (b, g, i, 0))],
        out_shape=[jax.ShapeDtypeStruct((T, H * HEAD), F32), jax.ShapeDtypeStruct((T, H * HEAD), BF16),
                   jax.ShapeDtypeStruct((B, H, S, 1), F32)],
        scratch_shapes=[], semantics=("parallel", "parallel", "arbitrary"),
    )(*([p_fox] * (3 * hp)), c_col, c_row)


def fox_bwd(p_fox, c_col, c_row, o, lse, do, B, S, H, rider=None):
    T = B * S
    t = _tile(S, ATT_TILE)
    n = S // t
    scale = HEAD ** -0.5
    hp = ATT_HEADS if H % ATT_HEADS == 0 else 1
    hs = range(hp)

    def body(*refs):
        qkv_refs = refs[:3 * hp]
        cq_ref, cr_ref, o_ref, lse_ref, do_ref, dqkv_ref, dcq_ref, dcr_ref, dq_acc, delta_s, lse_s = refs[3 * hp:]
        row = lax.broadcasted_iota(jnp.int32, (t, t), 0)
        col = lax.broadcasted_iota(jnp.int32, (t, t), 1)
        cols = [slice(hh * HEAD, (hh + 1) * HEAD) for hh in hs]

        def prep(i, c):
            rows = pl.ds(pl.multiple_of(i * t, t), t)
            for hh in hs:
                delta_s[hh, rows, :] = jnp.sum(do_ref[rows, cols[hh]] * o_ref[rows, cols[hh]], axis=-1, keepdims=True)
                lse_s[hh, rows, :] = lse_ref[hh, rows, :] - cq_ref[hh, rows, :]
                dq_acc[hh, rows, :] = jnp.zeros((t, HEAD), F32)
                dcq_ref[hh, rows, :] = jnp.zeros((t, 1), F32)
            return c

        lax.fori_loop(0, n, prep, 0)

        def kv_step(j, c):
            joff = pl.multiple_of(j * t, t)
            k = [qkv_refs[3 * hh + 1][pl.ds(joff, t), :] for hh in hs]
            v = [qkv_refs[3 * hh + 2][pl.ds(joff, t), :] for hh in hs]
            crj = [cr_ref[hh, :, pl.ds(joff, t)] for hh in hs]

            def q_step(i, carry, diagonal):
                rows = pl.ds(pl.multiple_of(i * t, t), t)
                q = [qkv_refs[3 * hh][rows, :] for hh in hs]
                dob = [do_ref[rows, cols[hh]].astype(BF16) for hh in hs]
                s = [lax.dot_general(q[hh], k[hh], _NT, preferred_element_type=F32) * scale - crj[hh] for hh in hs]
                dp = [lax.dot_general(dob[hh], v[hh], _NT, preferred_element_type=F32) for hh in hs]
                if diagonal:
                    s = [jnp.where(col <= row, s[hh], NEG) for hh in hs]
                p = [jnp.exp(s[hh] - lse_s[hh, rows, :]) for hh in hs]
                ds = [p[hh] * (dp[hh] - delta_s[hh, rows, :]) for hh in hs]
                dsb = [ds[hh].astype(BF16) for hh in hs]
                dv = [carry[hh][1] + lax.dot_general(p[hh].astype(BF16), dob[hh], _TN, preferred_element_type=F32)
                      for hh in hs]
                dk = [carry[hh][0] + lax.dot_general(dsb[hh], q[hh], _TN, preferred_element_type=F32) for hh in hs]
                for hh in hs:
                    dq_acc[hh, rows, :] += lax.dot_general(dsb[hh], k[hh], _NN, preferred_element_type=F32) * scale
                    dcq_ref[hh, rows, :] += jnp.sum(ds[hh], axis=-1, keepdims=True)
                dc = [carry[hh][2] + jnp.sum(ds[hh], axis=0, keepdims=True) for hh in hs]
                return tuple((dk[hh], dv[hh], dc[hh]) for hh in hs)

            z = jnp.zeros((t, HEAD), F32)
            zero = tuple((z, z, jnp.zeros((1, t), F32)) for _ in hs)
            on_diagonal = q_step(j, zero, diagonal=True)
            done = lax.fori_loop(j + 1, n, functools.partial(q_step, diagonal=False), on_diagonal)
            for hh, (dk, dv, dc) in enumerate(done):
                base = 3 * HEAD * hh
                dqkv_ref[pl.ds(joff, t), base + HEAD:base + 2 * HEAD] = (dk * scale).astype(BF16)
                dqkv_ref[pl.ds(joff, t), base + 2 * HEAD:base + 3 * HEAD] = dv.astype(BF16)
                dcr_ref[hh, :, pl.ds(joff, t)] = -dc
            return c

        lax.fori_loop(0, n, kv_step, 0)
        for hh in hs:
            dqkv_ref[:, 3 * HEAD * hh:3 * HEAD * hh + HEAD] = dq_acc[hh].astype(BF16)

    qkv = []
    for hh in hs:
        for part in range(3):
            qkv.append(pl.BlockSpec((S, HEAD), lambda b, g, hh=hh, part=part: (b, 3 * (hp * g + hh) + part)))
    col_spec = pl.BlockSpec((None, hp, S, 1), lambda b, g: (b, g, 0, 0))
    row_spec = pl.BlockSpec((None, hp, 1, S), lambda b, g: (b, g, 0, 0))
    heads = pl.BlockSpec((S, hp * HEAD), lambda b, g: (b, g))
    return _hosted_call(
        body, rider, name="fox_bwd", grid=(B, H // hp),
        in_specs=qkv + [col_spec, row_spec, heads, col_spec, heads],
        out_specs=[pl.BlockSpec((S, 3 * hp * HEAD), lambda b, g: (b, g)), col_spec, row_spec],
        out_shape=[jax.ShapeDtypeStruct((T, 3 * H * HEAD), BF16), jax.ShapeDtypeStruct((B, H, S, 1), F32),
                   jax.ShapeDtypeStruct((B, H, 1, S), F32)],
        scratch_shapes=[pltpu.VMEM((hp, S, HEAD), F32), pltpu.VMEM((hp, S, 1), F32), pltpu.VMEM((hp, S, 1), F32)],
        semantics=("parallel", "parallel"),
    )(*([p_fox] * (3 * hp)), c_col, c_row, o, lse, do)


def _small_fn(x, b0, b1, H):
    S = x.shape[0]
    lane = lax.broadcasted_iota(jnp.int32, x.shape, 1)
    z = x + b0
    tail = jnp.log1p(jnp.exp(-jnp.abs(z)))
    softplus = jnp.maximum(z, 0.0) + tail
    logsig = -(jnp.maximum(-z, 0.0) + tail)
    g = -jnp.exp(b1) * softplus
    pre = jnp.where(lane < H, logsig, jnp.where(lane < 2 * H, g, 0.0))
    bl = _tile(S, 256, CHUNK)
    r = lax.broadcasted_iota(jnp.int32, (bl, bl), 0)
    c = lax.broadcasted_iota(jnp.int32, (bl, bl), 1)
    tri = (r >= c).astype(F32)
    tri_chunk = jnp.where((r >= c) & (jnp.right_shift(r, 6) == jnp.right_shift(c, 6)), 1.0, 0.0)
    carry = jnp.zeros((1, x.shape[1]), F32)
    parts = []
    for i in range(S // bl):
        blk = pre[i * bl:(i + 1) * bl, :]
        full = lax.dot_general(tri, blk, _NN, precision=EXACT, preferred_element_type=F32) + carry
        chunked = lax.dot_general(tri_chunk, blk, _NN, precision=EXACT, preferred_element_type=F32)
        parts.append(jnp.where(lane[:bl] < H, full, chunked))
        carry = carry + jnp.sum(blk, axis=0, keepdims=True)
    cum = parts[0] if len(parts) == 1 else jnp.concatenate(parts, axis=0)
    return jnp.where(lane < 2 * H, cum, jnp.where(lane < 3 * H, _sigmoid(x), 0.0))


def small_fwd(p_small, prm, B, S, H):
    T = B * S

    def body(x_ref, p_ref, o_ref):
        o_ref[...] = _small_fn(x_ref[...], p_ref[0:1, :], p_ref[1:2, :], H)

    blk = pl.BlockSpec((S, 128), lambda b: (b, 0))
    return pl.pallas_call(
        body, name="small_fwd", grid=(B,), in_specs=[blk, pl.BlockSpec((8, 128), lambda b: (0, 0))], out_specs=blk,
        out_shape=jax.ShapeDtypeStruct((T, 128), F32), compiler_params=_params("parallel"),
    )(p_small, prm)


def small_bwd(p_small, prm, d_out, B, S, H):
    T = B * S

    def body(x_ref, p_ref, d_ref, dx_ref, dp_ref):
        @pl.when(pl.program_id(0) == 0)
        def _():
            dp_ref[...] = jnp.zeros_like(dp_ref)

        _, vjp = jax.vjp(functools.partial(_small_fn, H=H), x_ref[...], p_ref[0:1, :], p_ref[1:2, :])
        dx, db0, db1 = vjp(d_ref[...])
        dx_ref[...] = dx.astype(BF16)
        dp_ref[0:1, :] += db0
        dp_ref[1:2, :] += db1

    blk = pl.BlockSpec((S, 128), lambda b: (b, 0))
    pblk = pl.BlockSpec((8, 128), lambda b: (0, 0))
    return pl.pallas_call(
        body, name="small_bwd", grid=(B,), in_specs=[blk, pblk, blk], out_specs=[blk, pblk],
        out_shape=[jax.ShapeDtypeStruct((T, 128), BF16), jax.ShapeDtypeStruct((8, 128), F32)],
        compiler_params=_params("arbitrary"),
    )(p_small, prm, d_out)


def gdn_prep_fwd(p_gqkv, cw, B, S, H):
    T = B * S

    def body(x_ref, w_ref, o_ref):
        for part in range(3):
            cols = slice(part * HEAD, (part + 1) * HEAD)
            y = _conv_fwd(x_ref[:, cols], w_ref.at[:, cols], GDN_CONV)
            a = y * _sigmoid(y)
            if part < 2:
                a = a * lax.rsqrt(jnp.sum(a * a, axis=-1, keepdims=True) + EPS)
            o_ref[:, cols] = a

    blk = pl.BlockSpec((S, 3 * HEAD), lambda b, h: (b, h))
    wblk = pl.BlockSpec((GDN_CONV, 3 * HEAD), lambda b, h: (0, h))
    return pl.pallas_call(
        body, name="gdn_prep_fwd", grid=(B, H), in_specs=[blk, wblk], out_specs=blk,
        out_shape=jax.ShapeDtypeStruct((T, 3 * H * HEAD), F32), compiler_params=_params("parallel", "parallel"),
    )(p_gqkv, cw)


def gdn_prep_bwd(p_gqkv, cw, d_out, B, S, H):
    T = B * S

    def body(x_ref, w_ref, d_ref, dx_ref, dw_ref):
        @pl.when(pl.program_id(1) == 0)
        def _():
            dw_ref[...] = jnp.zeros_like(dw_ref)

        x = x_ref[...]
        y, shifted = _conv_fwd(x, w_ref, GDN_CONV, keep_shifted=True)
        sg = _sigmoid(y)
        a = y * sg
        rs = lax.rsqrt(jnp.sum(a * a, axis=-1, keepdims=True) + EPS)
        d = d_ref[...]
        out = a * rs
        da_qk = rs * (d - out * jnp.sum(d * out, axis=-1, keepdims=True))
        is_qk = (pl.program_id(0) % 3) < 2
        da = jnp.where(is_qk, da_qk, d)
        dy = da * (sg + y * sg * (1.0 - sg))
        dx, dws = _conv_bwd(x, shifted, dy, w_ref, GDN_CONV)
        dx_ref[...] = dx.astype(BF16)
        for i in range(GDN_CONV):
            dw_ref[i:i + 1, :] += dws[i]

    blk = pl.BlockSpec((S, HEAD), lambda n, b: (b, n))
    wblk = pl.BlockSpec((GDN_CONV, HEAD), lambda n, b: (0, n))
    return pl.pallas_call(
        body, name="gdn_prep_bwd", grid=(3 * H, B), in_specs=[blk, wblk, blk], out_specs=[blk, wblk],
        out_shape=[jax.ShapeDtypeStruct((T, 3 * H * HEAD), BF16), jax.ShapeDtypeStruct((GDN_CONV, 3 * H * HEAD), F32)],
        compiler_params=_params("parallel", "arbitrary"),
    )(p_gqkv, cw, d_out)


@jax.custom_vjp
def _given_inverse(a, t):
    return t


def _given_inverse_fwd(a, t):
    return t, t


def _given_inverse_bwd(t, g):
    x = _dg(t, g, _TN, True)
    return -_dg(x, t, _NT, True), jnp.zeros_like(t)


_given_inverse.defvjp(_given_inverse_fwd, _given_inverse_bwd)


def _to_col(row):
    n = row.shape[1]
    r = lax.broadcasted_iota(jnp.int32, (n, n), 0)
    c = lax.broadcasted_iota(jnp.int32, (n, n), 1)
    return jnp.sum(jnp.where(r == c, row, 0.0), axis=1, keepdims=True)


def _intra_fn(k, v, beta_r, gcr, ops, t_known=None):
    n = len(k)
    m = k[0].shape[0]
    r = lax.broadcasted_iota(jnp.int32, (m, m), 0)
    c = lax.broadcasted_iota(jnp.int32, (m, m), 1)
    below = (r > c) & (jnp.right_shift(r, 6) == jnp.right_shift(c, 6))
    beta = [_to_col(beta_r[i]) for i in range(n)]
    gcc = [_to_col(gcr[i]) for i in range(n)]
    decay = [jnp.exp(jnp.where(below, gcc[i] - gcr[i], NEG)) for i in range(n)]
    kb = [k[i] * beta[i] for i in range(n)]
    a = [ops.nt(kb[i], k[i]) * decay[i] for i in range(n)]
    if t_known is None:
        p = [-a[i] for i in range(n)]
        tm = [jnp.where(r == c, 1.0, 0.0) + p[i] for i in range(n)]
        for _ in range(5):
            p = [ops.nn(p[i], p[i], hi=True) for i in range(n)]
            tm = [tm[i] + ops.nn(tm[i], p[i], hi=True) for i in range(n)]
    else:
        tm = [_given_inverse(a[i], t_known[i]) for i in range(n)]
    both = [ops.nn(tm[i], jnp.concatenate([v[i] * beta[i], kb[i] * jnp.exp(gcc[i])], axis=1), hi=True) for i in range(n)]
    u_hat = [both[i][:, :HEAD] for i in range(n)]
    w = [both[i][:, HEAD:] for i in range(n)]
    return tuple(u_hat), tuple(w), tuple(tm)


INTRA_NB = 32
PAIR = 1


def gdn_intra_fwd(qkvn, betar5, gcr5, B, S, H, rider=None):
    T = B * S
    UNIT = PAIR * CHUNK
    N = S // UNIT
    nb = min(INTRA_NB // PAIR, N)
    rows = nb * UNIT
    ns = N // nb

    def body(k_ref, v_ref, b_ref, gr_ref, uh_ref, w_ref, t_ref):
        sls = [slice(ci * UNIT, (ci + 1) * UNIT) for ci in range(nb)]
        u_hat, w, tm = _intra_fn(tuple(k_ref[sl, :] for sl in sls), tuple(v_ref[sl, :] for sl in sls),
                                 tuple(b_ref[ci] for ci in range(nb)), tuple(gr_ref[ci] for ci in range(nb)), _RawOps)
        for ci, sl in enumerate(sls):
            uh_ref[sl, :] = u_hat[ci]
            w_ref[sl, :] = w[ci]
            t_ref[ci] = tm[ci]

    rowspec = pl.BlockSpec((None, None, nb, 1, UNIT), lambda b, h, i: (b, h, i, 0, 0))
    sqspec = pl.BlockSpec((None, None, nb, UNIT, UNIT), lambda b, h, i: (b, h, i, 0, 0))
    out = pl.BlockSpec((rows, HEAD), lambda b, h, i: (b * ns + i, h))
    return _hosted_call(
        body, rider, name="gdn_intra_fwd", grid=(B, H, ns),
        in_specs=[pl.BlockSpec((rows, HEAD), lambda b, h, i: (b * ns + i, 3 * h + 1)),
                  pl.BlockSpec((rows, HEAD), lambda b, h, i: (b * ns + i, 3 * h + 2)),
                  rowspec, rowspec],
        out_specs=[out, out, sqspec],
        out_shape=[jax.ShapeDtypeStruct((T, H * HEAD), F32), jax.ShapeDtypeStruct((T, H * HEAD), F32),
                   jax.ShapeDtypeStruct((B, H, N, UNIT, UNIT), F32)],
        scratch_shapes=[], semantics=("parallel", "parallel", "parallel"),
    )(qkvn, qkvn, betar5, gcr5)


def gdn_intra_bwd(qkvn, betar5, gcr5, t_inv, d_uh, d_w, dq_in, dk_in, B, S, H):
    T = B * S
    UNIT = PAIR * CHUNK
    N = S // UNIT
    nb = min(INTRA_NB // PAIR, N)
    rows = nb * UNIT
    ns = N // nb

    def body(k_ref, v_ref, b_ref, gr_ref, t_ref, duh_ref, dw_ref, dq_ref, dk_ref, o_ref, db_ref, dgr_ref):
        sls = [slice(ci * UNIT, (ci + 1) * UNIT) for ci in range(nb)]
        chunks = range(nb)
        _, vjp = jax.vjp(
            functools.partial(_intra_fn, ops=_DiffOps, t_known=tuple(t_ref[ci] for ci in chunks)),
            tuple(k_ref[sl, :] for sl in sls), tuple(v_ref[sl, :] for sl in sls), tuple(b_ref[ci] for ci in chunks),
            tuple(gr_ref[ci] for ci in chunks))
        zero = jnp.zeros((UNIT, UNIT), F32)
        dk, dv, db, dgr = vjp((tuple(duh_ref[sl, :] for sl in sls), tuple(dw_ref[sl, :] for sl in sls),
                               tuple(zero for _ in chunks)))
        for ci, sl in enumerate(sls):
            o_ref[sl, 0:HEAD] = dq_ref[sl, :]
            o_ref[sl, HEAD:2 * HEAD] = dk[ci] + dk_ref[sl, :]
            o_ref[sl, 2 * HEAD:3 * HEAD] = dv[ci]
            db_ref[ci] = db[ci]
            dgr_ref[ci] = dgr[ci]

    rowspec = pl.BlockSpec((None, None, nb, 1, UNIT), lambda b, h, i: (b, h, i, 0, 0))
    sqspec = pl.BlockSpec((None, None, nb, UNIT, UNIT), lambda b, h, i: (b, h, i, 0, 0))
    head = pl.BlockSpec((rows, HEAD), lambda b, h, i: (b * ns + i, h))
    return pl.pallas_call(
        body, name="gdn_intra_bwd", grid=(B, H, ns),
        in_specs=[pl.BlockSpec((rows, HEAD), lambda b, h, i: (b * ns + i, 3 * h + 1)),
                  pl.BlockSpec((rows, HEAD), lambda b, h, i: (b * ns + i, 3 * h + 2)),
                  rowspec, rowspec, sqspec, head, head, head, head],
        out_specs=[pl.BlockSpec((rows, 3 * HEAD), lambda b, h, i: (b * ns + i, h)), rowspec, rowspec],
        out_shape=[jax.ShapeDtypeStruct((T, 3 * H * HEAD), F32),
                   jax.ShapeDtypeStruct((B, H, N, 1, UNIT), F32), jax.ShapeDtypeStruct((B, H, N, 1, UNIT), F32)],
        compiler_params=_params("parallel", "parallel", "parallel"),
    )(qkvn, qkvn, betar5, gcr5, t_inv, d_uh, d_w, dq_in, dk_in)


def _inter_fn(q, k, u_hat, w, gcr, state, ops):
    n = len(q)
    r = lax.broadcasted_iota(jnp.int32, (CHUNK, CHUNK), 0)
    c = lax.broadcasted_iota(jnp.int32, (CHUNK, CHUNK), 1)
    last = lax.broadcasted_iota(jnp.int32, (1, CHUNK), 1) == CHUNK - 1
    gcc = [_to_col(gcr[i]) for i in range(n)]
    gl = [jnp.sum(jnp.where(last, gcr[i], 0.0), axis=1, keepdims=True) for i in range(n)]
    decay = [jnp.exp(jnp.where(r >= c, gcc[i] - gcr[i], NEG)) for i in range(n)]
    qs = [q[i] * (HEAD ** -0.5) for i in range(n)]
    ws = [ops.nn(w[i], state[i]) for i in range(n)]
    qst = [ops.nn(qs[i] * jnp.exp(gcc[i]), state[i]) for i in range(n)]
    attn = [ops.nt(qs[i], k[i]) * decay[i] for i in range(n)]
    u = [u_hat[i] - ws[i] for i in range(n)]
    o = [qst[i] + ops.nn(attn[i], u[i]) for i in range(n)]
    kdu = [ops.tn(k[i] * jnp.exp(gl[i] - gcc[i]), u[i]) for i in range(n)]
    new_state = [state[i] * jnp.exp(gl[i]) + kdu[i] for i in range(n)]
    return tuple(o), tuple(new_state)


INTER_HEADS = 8
INTER_ROWS = 512
INTER_ROWS_BWD = 256


def _inter_heads(H):
    return INTER_HEADS if H % INTER_HEADS == 0 else (4 if H % 4 == 0 else 1)


def _inter_specs(ts, ns, hp, backward):
    at = (lambda s: ns - 1 - s) if backward else (lambda s: s)
    nc = ts // CHUNK
    qk = []
    for hh in range(hp):
        qk.append(pl.BlockSpec((ts, HEAD), lambda b, g, s, hh=hh: (b * ns + at(s), 3 * (hp * g + hh))))
        qk.append(pl.BlockSpec((ts, HEAD), lambda b, g, s, hh=hh: (b * ns + at(s), 3 * (hp * g + hh) + 1)))
    heads = pl.BlockSpec((ts, hp * HEAD), lambda b, g, s: (b * ns + at(s), g))
    rowspec = pl.BlockSpec((None, hp, nc, 1, CHUNK), lambda b, g, s: (b, g, at(s), 0, 0))
    stspec = pl.BlockSpec((None, hp, nc, HEAD, HEAD), lambda b, g, s: (b, g, at(s), 0, 0))
    return qk, heads, rowspec, stspec


def gdn_inter_fwd(qkvn, u_hat, w, gcr5, p_gz, gnorm, B, S, H):
    T = B * S
    N = S // CHUNK
    hp = _inter_heads(H)
    hs = range(hp)
    ts = _tile(S, INTER_ROWS, CHUNK)
    ns, nc = S // ts, ts // CHUNK

    def body(*refs):
        qk_refs, (uh_ref, w_ref, gr_ref, z_ref, gn_ref, o_ref, st_ref, y_ref, s_scr) = refs[:2 * hp], refs[2 * hp:]

        @pl.when(pl.program_id(2) == 0)
        def _():
            s_scr[...] = jnp.zeros_like(s_scr)

        gn = gn_ref[...]

        def step(n, c):
            rows = pl.ds(pl.multiple_of(n * CHUNK, CHUNK), CHUNK)
            st = tuple(s_scr[hh] for hh in hs)
            for hh in hs:
                st_ref[hh, n] = st[hh]
            o, new = _inter_fn(tuple(qk_refs[2 * hh][rows, :] for hh in hs), tuple(qk_refs[2 * hh + 1][rows, :] for hh in hs),
                               tuple(uh_ref[rows, hh * HEAD:(hh + 1) * HEAD] for hh in hs),
                               tuple(w_ref[rows, hh * HEAD:(hh + 1) * HEAD] for hh in hs),
                               tuple(gr_ref[hh, n] for hh in hs), st, _RawOps)
            for hh in hs:
                cols = slice(hh * HEAD, (hh + 1) * HEAD)
                o_ref[rows, cols] = o[hh]
                s_scr[hh] = new[hh]
                z = z_ref[rows, cols]
                r = lax.rsqrt(jnp.mean(o[hh] * o[hh], axis=-1, keepdims=True) + EPS)
                y_ref[rows, cols] = (o[hh] * r * gn * z * _sigmoid(z)).astype(BF16)
            return c

        lax.fori_loop(0, nc, step, 0)

    qk, heads, rowspec, stspec = _inter_specs(ts, ns, hp, backward=False)
    return pl.pallas_call(
        body, name="gdn_inter_fwd", grid=(B, H // hp, ns),
        in_specs=qk + [heads, heads, rowspec, heads, pl.BlockSpec((1, HEAD), lambda b, g, s: (0, 0))],
        out_specs=[heads, stspec, heads],
        out_shape=[jax.ShapeDtypeStruct((T, H * HEAD), F32), jax.ShapeDtypeStruct((B, H, N, HEAD, HEAD), F32),
                   jax.ShapeDtypeStruct((T, H * HEAD), BF16)],
        scratch_shapes=[pltpu.VMEM((hp, HEAD, HEAD), F32)],
        compiler_params=_params("parallel", "parallel", "arbitrary"),
    )(*([qkvn] * (2 * hp)), u_hat, w, gcr5, p_gz, gnorm)


def gdn_inter_bwd(qkvn, u_hat, w, gcr5, states, o, p_gz, gnorm, d_y, B, S, H, rider=None):
    T = B * S
    N = S // CHUNK
    hp = _inter_heads(H)
    hs = range(hp)
    ts = _tile(S, INTER_ROWS_BWD, CHUNK)
    ns, nc = S // ts, ts // CHUNK

    def body(*refs):
        qk_refs = refs[:2 * hp]
        (uh_ref, w_ref, gr_ref, st_ref, o_ref, z_ref, gn_ref, dy_ref,
         dq_ref, dk_ref, duh_ref, dw_ref, dgr_ref, dz_ref, dgn_ref, ds_scr) = refs[2 * hp:]

        @pl.when(pl.program_id(2) == 0)
        def _():
            ds_scr[...] = jnp.zeros_like(ds_scr)
            dgn_ref[...] = jnp.zeros_like(dgn_ref)

        cols = [slice(hh * HEAD, (hh + 1) * HEAD) for hh in hs]
        gn = gn_ref[...]

        def through_norm(rows, hh):
            ov, z, d = o_ref[rows, cols[hh]], z_ref[rows, cols[hh]], dy_ref[rows, cols[hh]]
            r = lax.rsqrt(jnp.mean(ov * ov, axis=-1, keepdims=True) + EPS)
            xh = ov * r
            sg = _sigmoid(z)
            d_n = d * (z * sg)
            dz_ref[rows, cols[hh]] = (d * xh * gn * (sg + z * sg * (1.0 - sg))).astype(BF16)
            dgn_ref[0:1, :] += jnp.sum(d_n * xh, axis=0, keepdims=True)
            dxh = d_n * gn
            return r * (dxh - xh * jnp.mean(dxh * xh, axis=-1, keepdims=True))

        def step(i, c):
            n = nc - 1 - i
            rows = pl.ds(pl.multiple_of(n * CHUNK, CHUNK), CHUNK)
            _, vjp = jax.vjp(functools.partial(_inter_fn, ops=_DiffOps),
                             tuple(qk_refs[2 * hh][rows, :] for hh in hs), tuple(qk_refs[2 * hh + 1][rows, :] for hh in hs),
                             tuple(uh_ref[rows, cols[hh]] for hh in hs), tuple(w_ref[rows, cols[hh]] for hh in hs),
                             tuple(gr_ref[hh, n] for hh in hs), tuple(st_ref[hh, n] for hh in hs))
            dq, dk, duh, dw, dgr, ds = vjp((tuple(through_norm(rows, hh) for hh in hs), tuple(ds_scr[hh] for hh in hs)))
            for hh in hs:
                dq_ref[rows, cols[hh]] = dq[hh]
                dk_ref[rows, cols[hh]] = dk[hh]
                duh_ref[rows, cols[hh]] = duh[hh]
                dw_ref[rows, cols[hh]] = dw[hh]
                dgr_ref[hh, n] = dgr[hh]
                ds_scr[hh] = ds[hh]
            return c

        lax.fori_loop(0, nc, step, 0)

    qk, heads, rowspec, stspec = _inter_specs(ts, ns, hp, backward=True)
    hshape = jax.ShapeDtypeStruct((T, H * HEAD), F32)
    return _hosted_call(
        body, rider, name="gdn_inter_bwd", grid=(B, H // hp, ns),
        in_specs=qk + [heads, heads, rowspec, stspec, heads, heads, pl.BlockSpec((1, HEAD), lambda b, g, s: (0, 0)), heads],
        out_specs=[heads, heads, heads, heads, rowspec, heads,
                   pl.BlockSpec((None, None, 8, HEAD), lambda b, g, s: (b, g, 0, 0))],
        out_shape=[hshape, hshape, hshape, hshape, jax.ShapeDtypeStruct((B, H, N, 1, CHUNK), F32),
                   jax.ShapeDtypeStruct((T, H * HEAD), BF16), jax.ShapeDtypeStruct((B, H // hp, 8, HEAD), F32)],
        scratch_shapes=[pltpu.VMEM((hp, HEAD, HEAD), F32)], semantics=("parallel", "parallel", "arbitrary"),
    )(*([qkvn] * (2 * hp)), u_hat, w, gcr5, states, o, p_gz, gnorm, d_y)


def adamw(w, g, m, v, name):
    shape = w.shape
    lead = (None,) * (w.ndim - 2)
    zeros = (0,) * (w.ndim - 2)
    R, C = shape[-2:]
    g2 = g.reshape(R, C)
    tr, tc = _tile(R, 128, 8), C
    if tr % 8 and R > 8:
        tr, tc = R, _tile(C, 128)

    def body(w_ref, g_ref, m_ref, v_ref, d_ref, nm_ref, nv_ref):
        gv = g_ref[...]
        nm = ADAM_B1 * m_ref[...] + (1.0 - ADAM_B1) * gv
        nv = ADAM_B2 * v_ref[...] + (1.0 - ADAM_B2) * (gv * gv)
        m_hat = nm / (1.0 - ADAM_B1 ** ADAM_STEP)
        v_hat = nv / (1.0 - ADAM_B2 ** ADAM_STEP)
        d_ref[...] = -ADAM_LR * (m_hat / (jnp.sqrt(v_hat) + ADAM_EPS) + ADAM_WD * w_ref[...])
        nm_ref[...] = nm
        nv_ref[...] = nv

    blk = pl.BlockSpec(lead + (tr, tc), lambda i, j: zeros + (i, j))
    gblk = pl.BlockSpec((tr, tc), lambda i, j: (i, j))
    sh = jax.ShapeDtypeStruct(shape, F32)
    return pl.pallas_call(
        body, name=name, grid=(R // tr, C // tc), in_specs=[blk, gblk, blk, blk], out_specs=[blk] * 3, out_shape=[sh] * 3,
        compiler_params=_params("parallel", "parallel"),
    )(w, g2, m, v)


def _place():
    x, y, c = lax.axis_index("x"), lax.axis_index("y"), lax.axis_index("c")
    chips = [(1 - x, y), (x, 1 - y), (1 - x, 1 - y)]
    return x, y, c, chips


_HBM = pl.BlockSpec(memory_space=pltpu.HBM)


def allgather_weights(packs):
    n = len(packs)

    def body(*refs):
        in_refs, out_refs, (send_sems, recv_sems) = refs[:n], refs[n:2 * n], refs[2 * n:]
        x, y, c, chips = _place()
        me_s = 2 * x + y
        me, sibling = (x, y, c), (x, y, 1 - c)
        shards = [2 * chip[0] + chip[1] for chip in chips]

        def copy(a, k, shard, half, to, src=None):
            dst = out_refs[a].at[shard, half]
            return pltpu.make_async_remote_copy(src_ref=dst if src is None else src, dst_ref=dst,
                                                send_sem=send_sems.at[6 * a + k], recv_sem=recv_sems.at[6 * a + k],
                                                device_id=to, device_id_type=MESH)

        first = [copy(a, j, me_s, c, (*chip, c), src=in_refs[a].at[c]) for a in range(n) for j, chip in enumerate(chips)]
        for cp in first:
            cp.start()
        passed = []
        for a in range(n):
            for j in range(3):
                copy(a, j, shards[j], c, me).wait_recv()
                passed.append(copy(a, 3 + j, shards[j], c, sibling))
                passed[-1].start()
        for a in range(n):
            for j in range(3):
                copy(a, 3 + j, shards[j], 1 - c, me).wait_recv()
        for cp in first + passed:
            cp.wait_send()

    return pl.pallas_call(
        body, name="allgather_weights", in_specs=[_HBM] * n, out_specs=[_HBM] * n,
        out_shape=[jax.ShapeDtypeStruct((N_CHIP,) + p.shape, p.dtype) for p in packs],
        scratch_shapes=[pltpu.SemaphoreType.DMA((6 * n,)), pltpu.SemaphoreType.DMA((6 * n,))],
    )(*packs)


class _Rider:
    def __init__(self, inputs, out_shapes, n_sems, sends, recvs, aliases=None):
        self.inputs, self.out_shapes, self.n_sems = list(inputs), list(out_shapes), n_sems
        self.sends, self.recvs, self.aliases = sends, recvs, aliases or {}

    def start(self, *refs):
        for cp in self.sends(*refs):
            cp.start()

    def wait(self, *refs):
        for cp in self.recvs(*refs):
            cp.wait_recv()
        for cp in self.sends(*refs):
            cp.wait_send()


def _remote(src, dst, send_sems, recv_sems, k, to):
    return pltpu.make_async_remote_copy(src_ref=src, dst_ref=dst, send_sem=send_sems.at[k], recv_sem=recv_sems.at[k],
                                        device_id=to, device_id_type=MESH)


def _run_alone(rider, name):
    ri = len(rider.inputs)

    def body(*refs):
        ins, outs, (send_sems, recv_sems) = refs[:ri], refs[ri:-2], refs[-2:]
        rider.start(ins, outs, send_sems, recv_sems)
        rider.wait(ins, outs, send_sems, recv_sems)

    return pl.pallas_call(
        body, name=name, in_specs=[_HBM] * ri, out_specs=[_HBM] * len(rider.out_shapes), out_shape=rider.out_shapes,
        scratch_shapes=[pltpu.SemaphoreType.DMA((rider.n_sems,))] * 2, input_output_aliases=rider.aliases,
    )(*rider.inputs)


def _hosted_call(body, rider, *, name, grid, in_specs, out_specs, out_shape, scratch_shapes, semantics):
    if rider is None:
        return pl.pallas_call(body, name=name, grid=grid, in_specs=in_specs, out_specs=out_specs, out_shape=out_shape,
                              scratch_shapes=scratch_shapes, compiler_params=_params(*semantics))
    n_in, n_out, n_scr = len(in_specs), len(out_specs), len(scratch_shapes)
    ri, ro = len(rider.inputs), len(rider.out_shapes)

    def hosted(*refs):
        parts, p = [], 0
        for cnt in (n_in, ri, n_out, ro, n_scr, 2):
            parts.append(refs[p:p + cnt])
            p += cnt
        ins, rins, outs, routs, scr, (send_sems, recv_sems) = parts
        first = functools.reduce(jnp.logical_and, [pl.program_id(a) == 0 for a in range(len(grid))])
        last = functools.reduce(jnp.logical_and, [pl.program_id(a) == grid[a] - 1 for a in range(len(grid))])

        @pl.when(first)
        def _():
            rider.start(rins, routs, send_sems, recv_sems)

        body(*ins, *outs, *scr)

        @pl.when(last)
        def _():
            rider.wait(rins, routs, send_sems, recv_sems)

    call = pl.pallas_call(
        hosted, name=name, grid=grid, in_specs=list(in_specs) + [_HBM] * ri, out_specs=list(out_specs) + [_HBM] * ro,
        out_shape=list(out_shape) + rider.out_shapes,
        scratch_shapes=list(scratch_shapes) + [pltpu.SemaphoreType.DMA((rider.n_sems,))] * 2,
        input_output_aliases={n_in + i: n_out + o for i, o in rider.aliases.items()},
        compiler_params=_params(*(("arbitrary",) * len(grid))))

    def run(*args):
        res = call(*args, *rider.inputs)
        return res[:n_out], res[n_out:]

    return run


def _ride_gather_ici(packs):
    n = len(packs)

    def sends(ins, outs, send_sems, recv_sems):
        x, y, c, chips = _place()
        return [_remote(ins[a].at[c], outs[a].at[2 * x + y, c], send_sems, recv_sems, 3 * a + j, (*chip, c))
                for a in range(n) for j, chip in enumerate(chips)]

    def recvs(ins, outs, send_sems, recv_sems):
        x, y, c, chips = _place()
        return [_remote(ins[a].at[c], outs[a].at[2 * chip[0] + chip[1], c], send_sems, recv_sems, 3 * a + j, (x, y, c))
                for a in range(n) for j, chip in enumerate(chips)]

    return _Rider(packs, [jax.ShapeDtypeStruct((N_CHIP,) + p.shape, p.dtype) for p in packs], 3 * n, sends, recvs)


def _ride_gather_d2d(gathered):
    n = len(gathered)

    def copies(landing_half, to):
        def build(ins, outs, send_sems, recv_sems):
            x, y, c, chips = _place()
            return [_remote(ins[a].at[2 * chip[0] + chip[1], c], outs[a].at[2 * chip[0] + chip[1], landing_half(c)],
                            send_sems, recv_sems, 3 * a + j, to(x, y, c))
                    for a in range(n) for j, chip in enumerate(chips)]
        return build

    return _Rider(gathered, [jax.ShapeDtypeStruct(g.shape, g.dtype) for g in gathered], 3 * n,
                  copies(lambda c: c, lambda x, y, c: (x, y, 1 - c)), copies(lambda c: 1 - c, lambda x, y, c: (x, y, c)),
                  aliases={a: a for a in range(n)})


def _ride_exchange(gs):
    n = len(gs)
    shapes = [g.shape[1:] if g.ndim == 4 else g.shape[:2] + (g.shape[2] // 2,) for g in gs]

    def copies(ins, outs, send_sems, recv_sems):
        x, y, c, _ = _place()

        def theirs(a):
            if gs[a].ndim == 4:
                return ins[a].at[1 - c]
            cols = shapes[a][2]
            return ins[a].at[:, :, pl.ds((1 - c) * cols, cols)]

        return [_remote(theirs(a), outs[a], send_sems, recv_sems, a, (x, y, 1 - c)) for a in range(n)]

    return _Rider(gs, [jax.ShapeDtypeStruct(sh, g.dtype) for sh, g in zip(shapes, gs)], n, copies, copies)


def _ride_scatter(b16s, to=(0, 1, 2), landing=None):
    n = len(b16s)

    def sends(ins, outs, send_sems, recv_sems):
        x, y, c, chips = _place()
        return [_remote(ins[a].at[2 * chips[j][0] + chips[j][1]], outs[a].at[2 * x + y], send_sems, recv_sems, 3 * a + j,
                        (*chips[j], c)) for a in range(n) for j in to]

    def recvs(ins, outs, send_sems, recv_sems):
        x, y, c, chips = _place()
        return [_remote(ins[a].at[2 * x + y], outs[a].at[2 * chips[j][0] + chips[j][1]], send_sems, recv_sems, 3 * a + j,
                        (x, y, c)) for a in range(n) for j in to]

    return _Rider(list(b16s) + list(landing or []), [jax.ShapeDtypeStruct(b.shape, b.dtype) for b in b16s], 3 * n,
                  sends, recvs, aliases={n + a: a for a in range(n)} if landing else None)


def _slab_tile(r, cols):
    tr = _tile(r, 256, 16)
    if tr % 16 == 0:
        return tr, cols
    return r, _tile(cols, 128)


def add_halves(g, got, idx, name):
    ns, r, cols = got.shape
    tr, tc = _slab_tile(r, cols)
    if g.ndim == 4:
        mine = pl.BlockSpec((None, None, tr, tc), lambda s, i, j, idx_ref: (idx_ref[0], s, i, j))
    else:
        mine = pl.BlockSpec((None, tr, tc), lambda s, i, j, idx_ref: (s, i, idx_ref[0] * (cols // tc) + j))

    def body(idx_ref, a_ref, b_ref, o32_ref, o16_ref):
        s = a_ref[...] + b_ref[...]
        o32_ref[...] = s
        o16_ref[...] = s.astype(BF16)

    blk = pl.BlockSpec((None, tr, tc), lambda s, i, j, idx_ref: (s, i, j))
    return pl.pallas_call(
        body, name=name,
        grid_spec=pltpu.PrefetchScalarGridSpec(
            num_scalar_prefetch=1, grid=(ns, r // tr, cols // tc),
            in_specs=[mine, blk], out_specs=[blk, blk]),
        out_shape=[jax.ShapeDtypeStruct((ns, r, cols), F32), jax.ShapeDtypeStruct((ns, r, cols), BF16)],
        compiler_params=_params("parallel", "parallel", "parallel"),
    )(idx, g, got)


def add_chips(a32, got16, idx, name):
    ns, r, cols = a32.shape
    tr, tc = _slab_tile(r, cols)

    def body(idx_ref, a_ref, r1_ref, r2_ref, r3_ref, o_ref):
        o_ref[...] = ((a_ref[...] + r1_ref[...].astype(F32)) + r2_ref[...].astype(F32)) + r3_ref[...].astype(F32)

    def slab(k):
        return pl.BlockSpec((None, tr, tc), lambda i, j, idx_ref: ((idx_ref[1] + k) % ns, i, j))

    return pl.pallas_call(
        body, name=name,
        grid_spec=pltpu.PrefetchScalarGridSpec(
            num_scalar_prefetch=1, grid=(r // tr, cols // tc), in_specs=[slab(0), slab(1), slab(2), slab(3)],
            out_specs=pl.BlockSpec((tr, tc), lambda i, j, idx_ref: (i, j))),
        out_shape=jax.ShapeDtypeStruct((r, cols), F32),
        compiler_params=_params("parallel", "parallel"),
    )(idx, a32, got16, got16, got16)


def share_halves(halves):
    n = len(halves)

    def body(*refs):
        in_refs, out_refs, (send_sems, recv_sems) = refs[:n], refs[n:2 * n], refs[2 * n:]
        x, y, c, _ = _place()
        cps = [pltpu.make_async_remote_copy(src_ref=in_refs[a], dst_ref=out_refs[a], send_sem=send_sems.at[a],
                                            recv_sem=recv_sems.at[a], device_id=(x, y, 1 - c), device_id_type=MESH)
               for a in range(n)]
        for cp in cps:
            cp.start()
        for cp in cps:
            cp.wait()

    return pl.pallas_call(
        body, name="share_halves", in_specs=[_HBM] * n, out_specs=[_HBM] * n,
        out_shape=[jax.ShapeDtypeStruct(h.shape, F32) for h in halves],
        scratch_shapes=[pltpu.SemaphoreType.DMA((n,)), pltpu.SemaphoreType.DMA((n,))],
    )(*halves)


def allreduce_small(v):
    R, _ = v.shape

    def body(in_ref, out_ref, slots, send_sems, recv_sems):
        x, y, c, _ = _place()
        me = 4 * x + 2 * y + c
        slots[me] = in_ref[...]
        cps = []
        for k in range(1, N_DEV):
            to = (x ^ (k >> 2), y ^ ((k >> 1) & 1), c ^ (k & 1))
            cps.append(pltpu.make_async_remote_copy(src_ref=in_ref, dst_ref=slots.at[me], send_sem=send_sems.at[k - 1],
                                                    recv_sem=recv_sems.at[k - 1], device_id=to, device_id_type=MESH))
        for cp in cps:
            cp.start()
        for k in range(1, N_DEV):
            frm = 4 * (x ^ (k >> 2)) + 2 * (y ^ ((k >> 1) & 1)) + (c ^ (k & 1))
            pltpu.make_async_remote_copy(src_ref=in_ref, dst_ref=slots.at[frm], send_sem=send_sems.at[k - 1],
                                         recv_sem=recv_sems.at[k - 1], device_id=(x, y, c), device_id_type=MESH).wait_recv()
        for cp in cps:
            cp.wait_send()
        acc = slots[0]
        for d in range(1, N_DEV):
            acc = acc + slots[d]
        out_ref[...] = acc

    vm = pl.BlockSpec(memory_space=pltpu.VMEM)
    return pl.pallas_call(
        body, name="allreduce_small", in_specs=[vm], out_specs=vm, out_shape=jax.ShapeDtypeStruct((R, ROW), F32),
        scratch_shapes=[pltpu.VMEM((N_DEV, R, ROW), F32), pltpu.SemaphoreType.DMA((N_DEV - 1,)),
                        pltpu.SemaphoreType.DMA((N_DEV - 1,))],
    )(v)


def _rows_of(n, unit=16):
    return -(-n // (unit * ROW)) * unit


def _pack_rows(items, total_rows, dtype, unit=16):
    parts = []
    used = 0
    for a in items:
        flat = a.reshape(-1)
        r = _rows_of(flat.shape[0], unit)
        flat = jnp.pad(flat, (0, r * ROW - flat.shape[0]))
        parts.append(flat.reshape(r, ROW))
        used += r
    if total_rows > used:
        parts.append(jnp.zeros((total_rows - used, ROW), dtype))
    return jnp.concatenate(parts, axis=0)


def _unpack_rows(buf, shapes, unit=16):
    lead = buf.shape[:-2]
    out = []
    off = 0
    for shp in shapes:
        n = math.prod(shp)
        r = _rows_of(n, unit)
        piece = buf[..., off:off + r, :].reshape(*lead, r * ROW)[..., :n].reshape(*lead, *shp)
        out.append(piece)
        off += r
    return out


def _interleave_heads(w, H):
    lead = w.shape[:-1]
    return w.reshape(*lead, 3, H, HEAD).swapaxes(-3, -2).reshape(*lead, 3 * H * HEAD)


def _deinterleave_heads(w, H):
    lead = w.shape[:-1]
    return w.reshape(*lead, H, 3, HEAD).swapaxes(-3, -2).reshape(*lead, 3 * H * HEAD)


def _interleave_head_rows(w, H):
    return w.reshape(3, H, HEAD, w.shape[-1]).swapaxes(0, 1).reshape(3 * H * HEAD, w.shape[-1])


def _deinterleave_head_rows(w, H):
    return w.reshape(H, 3, HEAD, w.shape[-1]).swapaxes(0, 1).reshape(3 * H * HEAD, w.shape[-1])


def kernel(x, norm_mix, w_in, fox_f_bias, gdn_conv_w, gdn_a_log, gdn_dt_bias, gdn_norm, w_branch_fox, w_branch_gdn, w_out, norm_ffn, w_up, ffn_conv_w, w_down, norm_final, loss_target, m_norm_mix, m_w_in, m_fox_f_bias, m_gdn_conv_w, m_gdn_a_log, m_gdn_dt_bias, m_gdn_norm, m_w_branch_fox, m_w_branch_gdn, m_w_out, m_norm_ffn, m_w_up, m_ffn_conv_w, m_w_down, m_norm_final, v_norm_mix, v_w_in, v_fox_f_bias, v_gdn_conv_w, v_gdn_a_log, v_gdn_dt_bias, v_gdn_norm, v_w_branch_fox, v_w_branch_gdn, v_w_out, v_norm_ffn, v_w_up, v_ffn_conv_w, v_w_down, v_norm_final):
    B, S, D = x.shape
    T = B * S
    H = D // HEAD
    N = S // CHUNK
    FF = w_down.shape[1] * N_CHIP
    d_in = 9 * D + 3 * H
    assert w_in.shape[2] * N_CHIP == d_in and 3 * H <= 128

    cidx = lax.axis_index("c").astype(jnp.int32)
    sidx = (2 * lax.axis_index("x") + lax.axis_index("y")).astype(jnp.int32)
    idx = jnp.stack([cidx, sidx])

    rowed = [w_branch_fox[0], w_branch_gdn[0], w_out[0], w_down[0]]
    convs = [gdn_conv_w[0], ffn_conv_w[0]]
    rowed_shapes = [a.shape for a in rowed]
    conv_shapes = [a.shape + (2,) for a in convs]
    pad_rows = lambda shapes: -(-sum(_rows_of(math.prod(s)) for s in shapes) // 256) * 128
    Rh, Rc = pad_rows(rowed_shapes), pad_rows(conv_shapes)
    halves = lambda a: a.reshape(2, a.shape[0] // 2, a.shape[1])
    c_in = w_in.shape[2]
    packs_a = [w_in[0].T.astype(BF16).reshape(c_in, 2, D // 2).transpose(1, 0, 2),
               halves(_pack_rows([lax.bitcast_convert_type(a, BF16) for a in convs], 2 * Rc, BF16))]
    packs_b = [halves(w_up[0].astype(BF16)), halves(_pack_rows([a.astype(BF16) for a in rowed], 2 * Rh, BF16))]
    own = lambda gs, ps: [lax.dynamic_update_slice(g, p[None], (sidx, 0, 0, 0)) for g, p in zip(gs, ps)]
    by_cols = lambda g: g.transpose(1, 2, 0, 3).reshape(2 * g.shape[2], N_CHIP * g.shape[3])
    cat_cols = lambda p: jnp.concatenate([p[i] for i in range(N_CHIP)], axis=-1)
    cat_rows = lambda p: p.reshape(-1, p.shape[-1])
    g_in, g_conv = own(allgather_weights(packs_a), packs_a)
    W_inT = g_in.transpose(0, 2, 1, 3).reshape(N_CHIP * c_in, D)
    conv_parts = _unpack_rows(g_conv.reshape(N_CHIP, 2 * Rc, ROW), conv_shapes)
    gconv = cat_cols(lax.bitcast_convert_type(conv_parts[0], F32))
    fconv = cat_cols(lax.bitcast_convert_type(conv_parts[1], F32))

    o1, o2 = 3 * D, 3 * D + H
    o3, o4, o5, o6 = o2 + 3 * D, o2 + 3 * D + H, o2 + 3 * D + 2 * H, o2 + 4 * D + 2 * H
    W_foxT = _interleave_head_rows(W_inT[:o1], H)
    W_gqkvT = _interleave_head_rows(W_inT[o2:o3], H)
    W_gzT = W_inT[o5:o6]
    W_gatesT = W_inT[o6:]
    W_smallT = jnp.concatenate([W_inT[o1:o2], W_inT[o3:o5], jnp.zeros((128 - 3 * H, D), BF16)], axis=0)
    gconv_i = _interleave_heads(gconv, H)
    fconv_g, fconv_v = fconv[:, :FF], fconv[:, FF:]
    prm = jnp.zeros((8, 128), F32)
    prm = prm.at[0, 0:H].set(fox_f_bias[0]).at[0, H:2 * H].set(gdn_dt_bias[0]).at[1, H:2 * H].set(gdn_a_log[0])

    x2 = x.reshape(T, D)
    tgt = loss_target.reshape(T, D)

    hn1 = rmsnorm_fwd(x2, norm_mix, "rmsnorm_mix")
    p_fox = matmul(hn1, W_foxT, "nt", "proj_fox", out_dtype=BF16)
    p_gqkv = matmul(hn1, W_gqkvT, "nt", "proj_gqkv")
    p_gz = matmul(hn1, W_gzT, "nt", "proj_gz")
    p_gates = matmul(hn1, W_gatesT, "nt", "proj_gates")
    p_small = matmul(hn1, W_smallT, "nt", "proj_small")

    sm = small_fwd(p_small, prm, B, S, H)
    heads = lambda a: a.reshape(B, S, H).transpose(0, 2, 1)
    c_bhs, gc_bhs, beta_bhs = heads(sm[:, 0:H]), heads(sm[:, H:2 * H]), heads(sm[:, 2 * H:3 * H])
    c_col, c_row = c_bhs[..., None], c_bhs[:, :, None, :]
    gcr5 = gc_bhs.reshape(B, H, N, 1, CHUNK)
    gcr_u = gc_bhs.reshape(B, H, N // PAIR, 1, PAIR * CHUNK)
    betar_u = beta_bhs.reshape(B, H, N // PAIR, 1, PAIR * CHUNK)

    (o_fox, o_fox16, lse), arriving = fox_fwd(p_fox, c_col, c_row, B, S, H, rider=_ride_gather_ici(packs_b))
    qkvn = gdn_prep_fwd(p_gqkv, gconv_i, B, S, H)
    (u_hat, w_t, t_inv), arrived = gdn_intra_fwd(qkvn, betar_u, gcr_u, B, S, H, rider=_ride_gather_d2d(arriving))
    g_up, g_rowed = own(arrived, packs_b)
    W_up = by_cols(g_up)
    W_up_g, W_up_v = W_up[:, :FF], W_up[:, FF:]
    W_bf, W_bg, W_out, W_down = (cat_rows(p) for p in _unpack_rows(g_rowed.reshape(N_CHIP, 2 * Rh, ROW), rowed_shapes))
    o_gdn, states, y_gdn = gdn_inter_fwd(qkvn, u_hat, w_t, gcr5, p_gz, gdn_norm, B, S, H)
    bf_ = matmul(o_fox16, W_bf, "nn", "branch_fox")
    bg_, y = matmul(y_gdn, W_bg, "nn", "branch_gdn", post=_post_merge(p_gates, bf_))
    h1, hn2 = matmul(y, W_out, "nn", "out_proj", add=x2, post=_post_rmsnorm(norm_ffn))
    up_g = matmul(hn2, W_up_g, "nn", "up_gate")
    up_v = matmul(hn2, W_up_v, "nn", "up_val")
    act = ffn_gate_fwd(up_g, up_v, fconv_g, fconv_v, B, S)
    dh2, dh2_16, loss_cols, d_norm_final = matmul(act, W_down, "nn", "down_proj", add=h1,
                                                  post=_post_loss(norm_final.reshape(1, D), tgt))
    loss_here = (0.5 / D) * jnp.sum(loss_cols)

    d_act = matmul(dh2_16, W_down, "nt", "d_act")
    dW_down = matmul(act, dh2_16, "tn", "dw_down")
    d_upg, d_upv, d_fconv_g, d_fconv_v = ffn_gate_bwd(up_g, up_v, fconv_g, fconv_v, d_act, B, S)
    d_hn2 = matmul(d_upg, W_up_g, "nt", "d_hn2_g")
    dh1, dh1_16, d_norm_ffn = matmul(d_upv, W_up_v, "nt", "d_hn2_v", add=d_hn2,
                                     post=_post_rmsnorm_bwd(h1, norm_ffn, dh2, True))
    dW_up_slabs = jnp.concatenate([matmul(hn2, d_upg, "tn", "dw_up_g", slab=(D // 2, 2 * FF // N_CHIP)),
                                   matmul(hn2, d_upv, "tn", "dw_up_v", slab=(D // 2, 2 * FF // N_CHIP))], axis=1)
    d_bf, d_bg, d_gates = matmul(dh1_16, W_out, "nt", "d_y", post=_post_merge_bwd(p_gates, bf_, bg_))
    dW_out = matmul(y, dh1_16, "tn", "dw_out")
    d_ofox = matmul(d_bf, W_bf, "nt", "d_ofox")
    dW_bf = matmul(o_fox16, d_bf, "tn", "dw_bf")
    d_ygdn = matmul(d_bg, W_bg, "nt", "d_ygdn")
    dW_bg = matmul(y_gdn, d_bg, "tn", "dw_bg")

    d_fconv = jnp.concatenate([d_fconv_g, d_fconv_v], axis=1)
    col_shard = lambda g, s: g[:, s * (g.shape[1] // N_CHIP):(s + 1) * (g.shape[1] // N_CHIP)]
    row_shard = lambda g, s: g[s * (g.shape[0] // N_CHIP):(s + 1) * (g.shape[0] // N_CHIP)]
    shard_items = lambda s: [row_shard(dW_bf, s), row_shard(dW_bg, s), row_shard(dW_out, s), row_shard(dW_down, s),
                             col_shard(d_fconv, s)]
    g_shapes = [a.shape for a in shard_items(0)]
    assert sum(_rows_of(math.prod(s)) for s in g_shapes) <= 2 * Rh
    to_slabs = lambda g: g.reshape(2, g.shape[0] // 2, N_CHIP, g.shape[1] // N_CHIP).transpose(0, 2, 1, 3)
    gpacks_b = [dW_up_slabs,
                jnp.stack([_pack_rows(shard_items(s), 2 * Rh, F32).reshape(2, Rh, ROW) for s in range(N_CHIP)], axis=1)]
    (d_pfox, d_ccol, d_crow), gots_b = fox_bwd(p_fox, c_col, c_row, o_fox, lse, d_ofox, B, S, H,
                                              rider=_ride_exchange(gpacks_b))
    sums_b = [add_halves(g, got, idx, "add_halves_b%d" % i) for i, (g, got) in enumerate(zip(gpacks_b, gots_b))]

    (dq_i, dk_i, d_uh, d_wt, dgcr_a, d_gz, d_gn_parts), got16_b = gdn_inter_bwd(
        qkvn, u_hat, w_t, gcr5, states, o_gdn, p_gz, gdn_norm, d_ygdn, B, S, H,
        rider=_ride_scatter([s16 for _, s16 in sums_b]))
    d_gdn_norm = jnp.sum(d_gn_parts[:, :, 0, :], axis=(0, 1))[None]
    mine_b = [add_chips(s32, g16, idx, "add_chips_b%d" % i) for i, ((s32, _), g16) in enumerate(zip(sums_b, got16_b))]
    d_qkvn, d_betar5, dgcr_b = gdn_intra_bwd(qkvn, betar_u, gcr_u, t_inv, d_uh, d_wt, dq_i, dk_i, B, S, H)
    d_pgqkv, d_gconv_i = gdn_prep_bwd(p_gqkv, gconv_i, d_qkvn, B, S, H)

    tokens = lambda a: a.reshape(B, H, S).transpose(0, 2, 1).reshape(T, H)
    d_gc = dgcr_a.reshape(B, H, S) + dgcr_b.reshape(B, H, S)
    d_sm = jnp.concatenate([tokens(d_ccol.reshape(B, H, S) + d_crow.reshape(B, H, S)), tokens(d_gc), tokens(d_betar5.reshape(B, H, S)),
                            jnp.zeros((T, 128 - 3 * H), F32)], axis=1)
    d_psmall, d_prm = small_bwd(p_small, prm, d_sm, B, S, H)

    dW_foxT = matmul(d_pfox, hn1, "tn", "dw_fox")
    dW_gqkvT = matmul(d_pgqkv, hn1, "tn", "dw_gqkv")
    dW_gzT = matmul(d_gz, hn1, "tn", "dw_gz")
    dW_gatesT = matmul(d_gates, hn1, "tn", "dw_gates")
    dW_smallT = matmul(d_psmall, hn1, "tn", "dw_small")
    dW_inT = jnp.concatenate([_deinterleave_head_rows(dW_foxT, H), dW_smallT[0:H], _deinterleave_head_rows(dW_gqkvT, H),
                              dW_smallT[H:3 * H], dW_gzT, dW_gatesT], axis=0)
    d_gconv = _deinterleave_heads(d_gconv_i, H)

    gpack_a = [dW_inT.reshape(N_CHIP, c_in, D)]
    d_hn1, gots_a = matmul(d_pfox, W_foxT, "nn", "d_hn1_fox", rider=_ride_exchange(gpack_a))
    sums_a = [add_halves(gpack_a[0], gots_a[0], idx, "add_halves_a")]
    d_hn1, landing_a = matmul(d_pgqkv, W_gqkvT, "nn", "d_hn1_gqkv", add=d_hn1,
                              rider=_ride_scatter([sums_a[0][1]], to=(0, 1)))
    d_hn1 = matmul(d_gz, W_gzT, "nn", "d_hn1_gz", add=d_hn1)
    d_hn1, got16_a = matmul(d_gates, W_gatesT, "nn", "d_hn1_gates", add=d_hn1,
                            rider=_ride_scatter([sums_a[0][1]], to=(2,), landing=landing_a))
    mine = [add_chips(sums_a[0][0], got16_a[0], idx, "add_chips_a")] + mine_b
    grad_x, d_norm_mix = matmul(d_psmall, W_smallT, "nn", "d_hn1_small", add=d_hn1,
                                post=_post_rmsnorm_bwd(x2, norm_mix, dh1, False))

    others = share_halves(mine)
    g_w_inT, g_up, g_rows = (jnp.concatenate([jnp.where(cidx == 0, h, o), jnp.where(cidx == 0, o, h)], axis=ax)
                             for h, o, ax in zip(mine, others, (1, 0, 0)))
    g_w_in = g_w_inT.T
    g_bf, g_bg, g_out, g_down, g_fconv = _unpack_rows(g_rows, g_shapes)

    small_items = [d_norm_mix, d_norm_ffn, d_norm_final, d_gdn_norm, d_prm, d_gconv, loss_here.reshape(1, 1)]
    small_shapes = [a.shape for a in small_items]
    sv = allreduce_small(_pack_rows(small_items, 0, F32, unit=8))
    g_norm_mix, g_norm_ffn, g_norm_final, g_gdn_norm, g_prm, g_gconv_all, loss = _unpack_rows(sv, small_shapes, unit=8)
    loss = loss[0, 0]
    g_norm_final = g_norm_final.reshape(D)
    g_fbias, g_dtb, g_alog = g_prm[0:1, 0:H], g_prm[0:1, H:2 * H], g_prm[1:2, H:2 * H]
    g_gconv = lax.dynamic_slice_in_dim(g_gconv_all, sidx * (3 * D // N_CHIP), 3 * D // N_CHIP, axis=1)

    names = ["norm_mix", "w_in", "fox_f_bias", "gdn_conv_w", "gdn_a_log", "gdn_dt_bias", "gdn_norm", "w_branch_fox",
             "w_branch_gdn", "w_out", "norm_ffn", "w_up", "ffn_conv_w", "w_down", "norm_final"]
    ws = [norm_mix, w_in, fox_f_bias, gdn_conv_w, gdn_a_log, gdn_dt_bias, gdn_norm, w_branch_fox, w_branch_gdn, w_out,
          norm_ffn, w_up, ffn_conv_w, w_down, norm_final]
    ms = [m_norm_mix, m_w_in, m_fox_f_bias, m_gdn_conv_w, m_gdn_a_log, m_gdn_dt_bias, m_gdn_norm, m_w_branch_fox,
          m_w_branch_gdn, m_w_out, m_norm_ffn, m_w_up, m_ffn_conv_w, m_w_down, m_norm_final]
    vs = [v_norm_mix, v_w_in, v_fox_f_bias, v_gdn_conv_w, v_gdn_a_log, v_gdn_dt_bias, v_gdn_norm, v_w_branch_fox,
          v_w_branch_gdn, v_w_out, v_norm_ffn, v_w_up, v_ffn_conv_w, v_w_down, v_norm_final]
    gs = [g_norm_mix, g_w_in, g_fbias, g_gconv, g_alog, g_dtb, g_gdn_norm, g_bf, g_bg, g_out, g_norm_ffn, g_up,
          g_fconv, g_down, g_norm_final]
    gs = [g.reshape(w.shape) for g, w in zip(gs, ws)]
    deltas, new_ms, new_vs = [], [], []
    for nm, w, g, m, v in zip(names, ws, gs, ms, vs):
        if w.ndim == 1:
            d, a, b = adamw(w.reshape(1, -1), g.reshape(1, -1), m.reshape(1, -1), v.reshape(1, -1), "adamw_" + nm)
            d, a, b = d.reshape(w.shape), a.reshape(w.shape), b.reshape(w.shape)
        elif nm == "w_in":
            d, a, b = (r.T[None] for r in adamw(w[0].T, g_w_inT, m[0].T, v[0].T, "adamw_" + nm))
        else:
            d, a, b = adamw(w, g, m, v, "adamw_" + nm)
        deltas.append(d)
        new_ms.append(a)
        new_vs.append(b)

    return (loss, grad_x.reshape(B, S, D), *gs, *deltas, *new_ms, *new_vs)
```

```python
import functools
import math

import jax
import jax.numpy as jnp
from jax import lax
from jax.experimental import pallas as pl
from jax.experimental.pallas import tpu as pltpu

F32 = jnp.float32
BF16 = jnp.bfloat16
HEAD = 128
CHUNK = 64
GDN_CONV = 4
FFN_CONV = 3
EPS = 1e-6
NEG = -1e30
ROW = 1024
ATT_TILE = 512
MM_WEIGHT_TILE_BYTES = 8 << 20
MM_TN_OPERAND_BYTES = 32 << 20
N_CHIP = 4
N_DEV = 8
MESH = pl.DeviceIdType.MESH
HI = lax.Precision.HIGH
EXACT = lax.Precision.HIGHEST

ADAM_LR, ADAM_B1, ADAM_B2, ADAM_EPS, ADAM_WD, ADAM_STEP = 0.001, 0.9, 0.999, 1e-08, 0.01, 10


def _tile(n, cap, unit=128):
    best = None
    t = unit
    while t <= min(n, cap):
        if n % t == 0:
            best = t
        t += unit
    return best if best is not None else n


def _params(*sem):
    return pltpu.CompilerParams(dimension_semantics=sem)


_NN = (((1,), (0,)), ((), ()))
_NT = (((1,), (1,)), ((), ()))
_TN = (((0,), (0,)), ((), ()))


def _dg(a, b, dims, hi):
    if hi:
        return lax.dot_general(a, b, dims, precision=HI, preferred_element_type=F32)
    return lax.dot_general(a.astype(BF16), b.astype(BF16), dims, preferred_element_type=F32)


class _RawOps:
    @staticmethod
    def nn(a, b, hi=False):
        return _dg(a, b, _NN, hi)

    @staticmethod
    def nt(a, b, hi=False):
        return _dg(a, b, _NT, hi)

    @staticmethod
    def tn(a, b, hi=False):
        return _dg(a, b, _TN, hi)


def _make_diff_ops():
    def build(hi):
        @jax.custom_vjp
        def nn(a, b):
            return _dg(a, b, _NN, hi)

        nn.defvjp(lambda a, b: (_dg(a, b, _NN, hi), (a, b)),
                  lambda r, g: (_dg(g, r[1], _NT, hi), _dg(r[0], g, _TN, hi)))

        @jax.custom_vjp
        def nt(a, b):
            return _dg(a, b, _NT, hi)

        nt.defvjp(lambda a, b: (_dg(a, b, _NT, hi), (a, b)),
                  lambda r, g: (_dg(g, r[1], _NN, hi), _dg(g, r[0], _TN, hi)))

        @jax.custom_vjp
        def tn(a, b):
            return _dg(a, b, _TN, hi)

        tn.defvjp(lambda a, b: (_dg(a, b, _TN, hi), (a, b)),
                  lambda r, g: (_dg(r[1], g, _NT, hi), _dg(r[0], g, _NN, hi)))
        return nn, nt, tn

    lo, hi_ = build(False), build(True)

    class _DiffOps:
        @staticmethod
        def nn(a, b, hi=False):
            return (hi_ if hi else lo)[0](a, b)

        @staticmethod
        def nt(a, b, hi=False):
            return (hi_ if hi else lo)[1](a, b)

        @staticmethod
        def tn(a, b, hi=False):
            return (hi_ if hi else lo)[2](a, b)

    return _DiffOps


_DiffOps = _make_diff_ops()


def _sigmoid(x):
    return 1.0 / (1.0 + jnp.exp(-x))


def _mm_tile(n, pref):
    if n % pref == 0:
        return pref
    if n % 1408 == 0:
        return 1408
    return _tile(n, pref)


class _Post:
    def __init__(self, fn, row_ins=(), vec_ins=(), row_outs=(), acc_outs=(), keep_main=True):
        self.fn, self.keep_main = fn, keep_main
        self.row_ins = [r if isinstance(r, tuple) else (r, r.shape[1], 0) for r in row_ins]
        self.vec_ins, self.row_outs, self.acc_outs = list(vec_ins), list(row_outs), list(acc_outs)


def matmul(a, b, mode, name, add=None, out_dtype=F32, post=None, rider=None, slab=None):
    if mode == "nn":
        (M, K), (K2, N) = a.shape, b.shape
    elif mode == "nt":
        (M, K), (N, K2) = a.shape, b.shape
    else:
        (K, M), (K2, N) = a.shape, b.shape
    assert K == K2, (name, a.shape, b.shape)
    tn = slab[1] if slab else _mm_tile(N, 1024)
    if mode == "tn":
        tm = slab[0] if slab else (M if M <= 1408 else _mm_tile(M, 1408))
        row_bytes = 2 * (tm * a.dtype.itemsize + tn * b.dtype.itemsize)
        tk = next((t for t in (4096, 2048) if K % t == 0 and t * row_bytes <= MM_TN_OPERAND_BYTES), _mm_tile(K, 1024))
    else:
        tk = K if K * tn * 2 <= MM_WEIGHT_TILE_BYTES else _mm_tile(K, 1024)
        tm = _mm_tile(M, 1024 if tk <= 2048 and post is None else 512)
    nk = K // tk
    assert post is None or (mode != "tn" and tn == N), name
    dims = {"nn": _NN, "nt": _NT, "tn": _TN}[mode]
    if mode == "tn":
        a_spec = pl.BlockSpec((tk, tm), lambda j, i, k: (k, i))
    else:
        a_spec = pl.BlockSpec((tm, tk), lambda j, i, k: (i, k))
    if mode == "nt":
        b_spec = pl.BlockSpec((tn, tk), lambda j, i, k: (j, k))
    else:
        b_spec = pl.BlockSpec((tk, tn), lambda j, i, k: (k, j))
    o_spec = pl.BlockSpec((tm, tn), lambda j, i, k: (i, j))
    has_add = add is not None
    keep_main = post is None or post.keep_main
    counts = [2 + has_add] + ([len(post.row_ins), len(post.vec_ins)] if post else [0, 0]) + [int(keep_main)]
    counts += ([len(post.row_outs), len(post.acc_outs)] if post else [0, 0]) + [int(nk > 1)]

    def body(*refs):
        parts, p = [], 0
        for cnt in counts:
            parts.append(refs[p:p + cnt])
            p += cnt
        core, row_ins, vec_ins, main, row_outs, acc_outs, acc = parts
        a_ref, b_ref = core[:2]
        prod = lax.dot_general(a_ref[...].astype(BF16), b_ref[...].astype(BF16), dims, preferred_element_type=F32)

        def finish(r):
            if has_add:
                r = r + core[2][...]
            if keep_main:
                main[0][...] = r.astype(out_dtype)
            if post is not None:
                @pl.when(pl.program_id(1) == 0)
                def _():
                    for ref in acc_outs:
                        ref[...] = jnp.zeros_like(ref)

                post.fn(r, row_ins, vec_ins, row_outs, acc_outs)

        if nk == 1:
            finish(prod)
            return
        acc_ref = acc[0]
        k = pl.program_id(2)

        @pl.when(k == 0)
        def _():
            acc_ref[...] = jnp.zeros_like(acc_ref)

        acc_ref[...] += prod

        @pl.when(k == nk - 1)
        def _():
            finish(acc_ref[...])

    in_specs = [a_spec, b_spec] + ([o_spec] if has_add else [])
    args = (a, b) + ((add,) if has_add else ())
    out_specs = [o_spec] if keep_main else []
    out_shape = [jax.ShapeDtypeStruct((M, N), out_dtype)] if keep_main else []
    if slab:
        out_specs = [pl.BlockSpec((None, None, tm, tn), lambda j, i, k: (i, j, 0, 0))]
        out_shape = [jax.ShapeDtypeStruct((M // tm, N // tn, tm, tn), out_dtype)]
    if post is not None:
        in_specs += [pl.BlockSpec((tm, cols), lambda j, i, k, cb=cb: (i, cb)) for _, cols, cb in post.row_ins]
        in_specs += [pl.BlockSpec((1, v.shape[1]), lambda j, i, k: (0, 0)) for v in post.vec_ins]
        args += tuple(r for r, _, _ in post.row_ins) + tuple(post.vec_ins)
        out_specs += [pl.BlockSpec((tm, cols), lambda j, i, k: (i, 0)) for cols, _ in post.row_outs]
        out_specs += [pl.BlockSpec((1, cols), lambda j, i, k: (0, 0)) for cols in post.acc_outs]
        out_shape += [jax.ShapeDtypeStruct((M, cols), dt) for cols, dt in post.row_outs]
        out_shape += [jax.ShapeDtypeStruct((1, cols), F32) for cols in post.acc_outs]
    rows_sem = "arbitrary" if post is not None and post.acc_outs else "parallel"
    res = _hosted_call(
        body, rider, name=name, grid=(N // tn, M // tm, nk), in_specs=in_specs, out_specs=out_specs, out_shape=out_shape,
        scratch_shapes=[pltpu.VMEM((tm, tn), F32)] if nk > 1 else [], semantics=("parallel", rows_sem, "arbitrary"),
    )(*args)
    if rider is not None:
        res, carried = res
        return (res[0] if post is None else res), carried
    return res[0] if post is None else res


def rmsnorm_fwd(x, g, name):
    T, D = x.shape
    tm = _tile(T, 512, 8)

    def body(x_ref, g_ref, o_ref):
        xv = x_ref[...]
        r = lax.rsqrt(jnp.mean(xv * xv, axis=-1, keepdims=True) + EPS)
        o_ref[...] = (xv * r * g_ref[...]).astype(BF16)

    return pl.pallas_call(
        body, name=name, grid=(T // tm,),
        in_specs=[pl.BlockSpec((tm, D), lambda i: (i, 0)), pl.BlockSpec((1, D), lambda i: (0, 0))],
        out_specs=pl.BlockSpec((tm, D), lambda i: (i, 0)),
        out_shape=jax.ShapeDtypeStruct((T, D), BF16),
        compiler_params=_params("parallel"),
    )(x, g)


def _post_rmsnorm(g):
    def fn(r, row_ins, vec_ins, row_outs, acc_outs):
        rs = lax.rsqrt(jnp.mean(r * r, axis=-1, keepdims=True) + EPS)
        row_outs[0][...] = (r * rs * vec_ins[0][...]).astype(BF16)

    return _Post(fn, vec_ins=[g], row_outs=[(g.shape[1], BF16)])


def _post_rmsnorm_bwd(x, g, dres, with_bf16):
    D = g.shape[1]

    def fn(dy, row_ins, vec_ins, row_outs, acc_outs):
        xv = row_ins[0][...]
        rs = lax.rsqrt(jnp.mean(xv * xv, axis=-1, keepdims=True) + EPS)
        xh = xv * rs
        acc_outs[0][...] += jnp.sum(dy * xh, axis=0, keepdims=True)
        dxh = dy * vec_ins[0][...]
        dx = row_ins[1][...] + rs * (dxh - xh * jnp.mean(dxh * xh, axis=-1, keepdims=True))
        row_outs[0][...] = dx
        if with_bf16:
            row_outs[1][...] = dx.astype(BF16)

    return _Post(fn, row_ins=[x, dres], vec_ins=[g], row_outs=[(D, F32)] + ([(D, BF16)] if with_bf16 else []),
                 acc_outs=[D], keep_main=False)


def _post_loss(g, target):
    D = g.shape[1]

    def fn(hv, row_ins, vec_ins, row_outs, acc_outs):
        rs = lax.rsqrt(jnp.mean(hv * hv, axis=-1, keepdims=True) + EPS)
        xh = hv * rs
        gv = vec_ins[0][...]
        err = xh * gv - row_ins[0][...]
        acc_outs[0][...] += jnp.sum(err * err, axis=0, keepdims=True)
        dy = err * (1.0 / D)
        acc_outs[1][...] += jnp.sum(dy * xh, axis=0, keepdims=True)
        dxh = dy * gv
        dh = rs * (dxh - xh * jnp.mean(dxh * xh, axis=-1, keepdims=True))
        row_outs[0][...] = dh
        row_outs[1][...] = dh.astype(BF16)

    return _Post(fn, row_ins=[target], vec_ins=[g], row_outs=[(D, F32), (D, BF16)], acc_outs=[D, D], keep_main=False)


def _shift_down(x, k):
    if k == 0:
        return x
    rows = lax.broadcasted_iota(jnp.int32, x.shape, 0)
    return jnp.where(rows >= k, pltpu.roll(x, k, 0), 0.0)


def _shift_up(x, k):
    if k == 0:
        return x
    s = x.shape[0]
    rows = lax.broadcasted_iota(jnp.int32, x.shape, 0)
    return jnp.where(rows < s - k, pltpu.roll(x, s - k, 0), 0.0)


def _conv_fwd(x, w_ref, kw, keep_shifted=False):
    shifted = [_shift_down(x, kw - 1 - i) for i in range(kw - 1)]
    y = x * w_ref[kw - 1:kw, :]
    for i in range(kw - 1):
        y = y + shifted[i] * w_ref[i:i + 1, :]
    return (y, shifted) if keep_shifted else y


def _conv_bwd(x, shifted, dy, w_ref, kw):
    dx = dy * w_ref[kw - 1:kw, :]
    dws = []
    for i in range(kw - 1):
        dx = dx + _shift_up(dy, kw - 1 - i) * w_ref[i:i + 1, :]
        dws.append(jnp.sum(dy * shifted[i], axis=0, keepdims=True))
    dws.append(jnp.sum(dy * x, axis=0, keepdims=True))
    return dx, dws


def ffn_gate_fwd(up_g, up_v, cw_g, cw_v, B, S):
    T, Fd = up_g.shape
    tc = _tile(Fd, 256)

    def body(g_ref, v_ref, wg_ref, wv_ref, o_ref):
        ug = _conv_fwd(g_ref[...], wg_ref, FFN_CONV)
        uv = _conv_fwd(v_ref[...], wv_ref, FFN_CONV)
        o_ref[...] = (ug * _sigmoid(ug) * uv).astype(BF16)

    blk = pl.BlockSpec((S, tc), lambda b, j: (b, j))
    wblk = pl.BlockSpec((FFN_CONV, tc), lambda b, j: (0, j))
    return pl.pallas_call(
        body, name="ffn_gate_fwd", grid=(B, Fd // tc), in_specs=[blk, blk, wblk, wblk], out_specs=blk,
        out_shape=jax.ShapeDtypeStruct((T, Fd), BF16), compiler_params=_params("parallel", "parallel"),
    )(up_g, up_v, cw_g, cw_v)


def ffn_gate_bwd(up_g, up_v, cw_g, cw_v, d_act, B, S):
    T, Fd = up_g.shape
    tc = _tile(Fd, 256)

    def body(g_ref, v_ref, wg_ref, wv_ref, da_ref, dg_ref, dv_ref, dwg_ref, dwv_ref):
        @pl.when(pl.program_id(1) == 0)
        def _():
            dwg_ref[...] = jnp.zeros_like(dwg_ref)
            dwv_ref[...] = jnp.zeros_like(dwv_ref)

        xg, xv = g_ref[...], v_ref[...]
        ug, sh_g = _conv_fwd(xg, wg_ref, FFN_CONV, keep_shifted=True)
        uv, sh_v = _conv_fwd(xv, wv_ref, FFN_CONV, keep_shifted=True)
        da = da_ref[...]
        sg = _sigmoid(ug)
        d_ug = da * uv * (sg + ug * sg * (1.0 - sg))
        d_uv = da * ug * sg
        dxg, dwg = _conv_bwd(xg, sh_g, d_ug, wg_ref, FFN_CONV)
        dxv, dwv = _conv_bwd(xv, sh_v, d_uv, wv_ref, FFN_CONV)
        dg_ref[...] = dxg.astype(BF16)
        dv_ref[...] = dxv.astype(BF16)
        for i in range(FFN_CONV):
            dwg_ref[i:i + 1, :] += dwg[i]
            dwv_ref[i:i + 1, :] += dwv[i]

    blk = pl.BlockSpec((S, tc), lambda j, b: (b, j))
    wblk = pl.BlockSpec((FFN_CONV, tc), lambda j, b: (0, j))
    return pl.pallas_call(
        body, name="ffn_gate_bwd", grid=(Fd // tc, B), in_specs=[blk, blk, wblk, wblk, blk],
        out_specs=[blk, blk, wblk, wblk],
        out_shape=[jax.ShapeDtypeStruct((T, Fd), BF16), jax.ShapeDtypeStruct((T, Fd), BF16),
                   jax.ShapeDtypeStruct((FFN_CONV, Fd), F32), jax.ShapeDtypeStruct((FFN_CONV, Fd), F32)],
        compiler_params=_params("parallel", "arbitrary"),
    )(up_g, up_v, cw_g, cw_v, d_act)


def _post_merge(p_gates, bf_):
    D = bf_.shape[1]

    def fn(bg, row_ins, vec_ins, row_outs, acc_outs):
        gf_ref, gg_ref, bf_ref = row_ins
        row_outs[0][...] = (_sigmoid(gf_ref[...]) * bf_ref[...] + _sigmoid(gg_ref[...]) * bg).astype(BF16)

    return _Post(fn, row_ins=[(p_gates, D, 0), (p_gates, D, 1), bf_], row_outs=[(D, BF16)])


def _post_merge_bwd(p_gates, bf_, bg_):
    D = bf_.shape[1]

    def fn(d, row_ins, vec_ins, row_outs, acc_outs):
        gf_ref, gg_ref, bf_ref, bg_ref = row_ins
        sf, sg = _sigmoid(gf_ref[...]), _sigmoid(gg_ref[...])
        row_outs[0][...] = (d * sf).astype(BF16)
        row_outs[1][...] = (d * sg).astype(BF16)
        row_outs[2][:, 0:D] = (d * bf_ref[...] * sf * (1.0 - sf)).astype(BF16)
        row_outs[2][:, D:2 * D] = (d * bg_ref[...] * sg * (1.0 - sg)).astype(BF16)

    return _Post(fn, row_ins=[(p_gates, D, 0), (p_gates, D, 1), bf_, bg_],
                 row_outs=[(D, BF16), (D, BF16), (2 * D, BF16)], keep_main=False)


ATT_HEADS = 2
ATT_HEADS_FWD = 4


def fox_fwd(p_fox, c_col, c_row, B, S, H, rider=None):
    T = B * S
    t = _tile(S, ATT_TILE)
    nq = S // t
    scale = HEAD ** -0.5
    hp = next((n for n in (ATT_HEADS_FWD, ATT_HEADS) if H % n == 0), 1)
    hs = range(hp)

    def body(*refs):
        qkv_refs, (cq_ref, cr_ref, o_ref, o16_ref, lse_ref) = refs[:3 * hp], refs[3 * hp:]
        i = pl.program_id(2)
        q = [qkv_refs[3 * hh][...] for hh in hs]
        row = lax.broadcasted_iota(jnp.int32, (t, t), 0)
        col = lax.broadcasted_iota(jnp.int32, (t, t), 1)

        def step(j, carry, diagonal):
            off = pl.multiple_of(j * t, t)
            k = [qkv_refs[3 * hh + 1][pl.ds(off, t), :] for hh in hs]
            v = [qkv_refs[3 * hh + 2][pl.ds(off, t), :] for hh in hs]
            s = [lax.dot_general(q[hh], k[hh], _NT, preferred_element_type=F32) * scale - cr_ref[hh, :, pl.ds(off, t)]
                 for hh in hs]
            if diagonal:
                s = [jnp.where(col <= row, s[hh], NEG) for hh in hs]
            m_new = [jnp.maximum(carry[hh][0], jnp.max(s[hh], axis=-1, keepdims=True)) for hh in hs]
            alpha = [jnp.exp(carry[hh][0] - m_new[hh]) for hh in hs]
            p = [jnp.exp(s[hh] - m_new[hh]) for hh in hs]
            l = [alpha[hh] * carry[hh][1] + jnp.sum(p[hh], axis=-1, keepdims=True) for hh in hs]
            acc = [alpha[hh] * carry[hh][2] + lax.dot_general(p[hh].astype(BF16), v[hh], _NN, preferred_element_type=F32)
                   for hh in hs]
            return tuple((m_new[hh], l[hh], acc[hh]) for hh in hs)

        start = (jnp.full((t, 1), NEG, F32), jnp.zeros((t, 1), F32), jnp.zeros((t, HEAD), F32))
        below = lax.fori_loop(0, i, functools.partial(step, diagonal=False), tuple(start for _ in hs))
        done = step(i, below, diagonal=True)
        for hh, (m, l, acc) in enumerate(done):
            cols = slice(hh * HEAD, (hh + 1) * HEAD)
            o = acc / l
            o_ref[:, cols] = o
            o16_ref[:, cols] = o.astype(BF16)
            lse_ref[hh] = cq_ref[hh] + m + jnp.log(l)

    qkv = []
    for hh in hs:
        qkv.append(pl.BlockSpec((t, HEAD), lambda b, g, i, hh=hh: (b * nq + i, 3 * (hp * g + hh))))
        qkv.append(pl.BlockSpec((S, HEAD), lambda b, g, i, hh=hh: (b, 3 * (hp * g + hh) + 1)))
        qkv.append(pl.BlockSpec((S, HEAD), lambda b, g, i, hh=hh: (b, 3 * (hp * g + hh) + 2)))
    heads = pl.BlockSpec((t, hp * HEAD), lambda b, g, i: (b * nq + i, g))
    return _hosted_call(
        body, rider, name="fox_fwd", grid=(B, H // hp, nq),
        in_specs=qkv + [pl.BlockSpec((None, hp, t, 1), lambda b, g, i: (b, g, i, 0)),
                        pl.BlockSpec((None, hp, 1, S), lambda b, g, i: (b, g, 0, 0))],
        out_specs=[heads, heads, pl.BlockSpec((None, hp, t, 1), lambda b, g, i: (b, g, i, 0))],
        out_shape=[jax.ShapeDtypeStruct((T, H * HEAD), F32), jax.ShapeDtypeStruct((T, H * HEAD), BF16),
                   jax.ShapeDtypeStruct((B, H, S, 1), F32)],
        scratch_shapes=[], semantics=("parallel", "parallel", "arbitrary"),
    )(*([p_fox] * (3 * hp)), c_col, c_row)


def fox_bwd(p_fox, c_col, c_row, o, lse, do, B, S, H, rider=None):
    T = B * S
    t = _tile(S, ATT_TILE)
    n = S // t
    scale = HEAD ** -0.5
    hp = ATT_HEADS if H % ATT_HEADS == 0 else 1
    hs = range(hp)

    def body(*refs):
        qkv_refs = refs[:3 * hp]
        cq_ref, cr_ref, o_ref, lse_ref, do_ref, dqkv_ref, dcq_ref, dcr_ref, dq_acc, delta_s, lse_s = refs[3 * hp:]
        row = lax.broadcasted_iota(jnp.int32, (t, t), 0)
        col = lax.broadcasted_iota(jnp.int32, (t, t), 1)
        cols = [slice(hh * HEAD, (hh + 1) * HEAD) for hh in hs]

        def prep(i, c):
            rows = pl.ds(pl.multiple_of(i * t, t), t)
            for hh in hs:
                delta_s[hh, rows, :] = jnp.sum(do_ref[rows, cols[hh]] * o_ref[rows, cols[hh]], axis=-1, keepdims=True)
                lse_s[hh, rows, :] = lse_ref[hh, rows, :] - cq_ref[hh, rows, :]
                dq_acc[hh, rows, :] = jnp.zeros((t, HEAD), F32)
                dcq_ref[hh, rows, :] = jnp.zeros((t, 1), F32)
            return c

        lax.fori_loop(0, n, prep, 0)

        def kv_step(j, c):
            joff = pl.multiple_of(j * t, t)
            k = [qkv_refs[3 * hh + 1][pl.ds(joff, t), :] for hh in hs]
            v = [qkv_refs[3 * hh + 2][pl.ds(joff, t), :] for hh in hs]
            crj = [cr_ref[hh, :, pl.ds(joff, t)] for hh in hs]

            def q_step(i, carry, diagonal):
                rows = pl.ds(pl.multiple_of(i * t, t), t)
                q = [qkv_refs[3 * hh][rows, :] for hh in hs]
                dob = [do_ref[rows, cols[hh]].astype(BF16) for hh in hs]
                s = [lax.dot_general(q[hh], k[hh], _NT, preferred_element_type=F32) * scale - crj[hh] for hh in hs]
                dp = [lax.dot_general(dob[hh], v[hh], _NT, preferred_element_type=F32) for hh in hs]
                if diagonal:
                    s = [jnp.where(col <= row, s[hh], NEG) for hh in hs]
                p = [jnp.exp(s[hh] - lse_s[hh, rows, :]) for hh in hs]
                ds = [p[hh] * (dp[hh] - delta_s[hh, rows, :]) for hh in hs]
                dsb = [ds[hh].astype(BF16) for hh in hs]
                dv = [carry[hh][1] + lax.dot_general(p[hh].astype(BF16), dob[hh], _TN, preferred_element_type=F32)
                      for hh in hs]
                dk = [carry[hh][0] + lax.dot_general(dsb[hh], q[hh], _TN, preferred_element_type=F32) for hh in hs]
                for hh in hs:
                    dq_acc[hh, rows, :] += lax.dot_general(dsb[hh], k[hh], _NN, preferred_element_type=F32) * scale
                    dcq_ref[hh, rows, :] += jnp.sum(ds[hh], axis=-1, keepdims=True)
                dc = [carry[hh][2] + jnp.sum(ds[hh], axis=0, keepdims=True) for hh in hs]
                return tuple((dk[hh], dv[hh], dc[hh]) for hh in hs)

            z = jnp.zeros((t, HEAD), F32)
            zero = tuple((z, z, jnp.zeros((1, t), F32)) for _ in hs)
            on_diagonal = q_step(j, zero, diagonal=True)
            done = lax.fori_loop(j + 1, n, functools.partial(q_step, diagonal=False), on_diagonal)
            for hh, (dk, dv, dc) in enumerate(done):
                base = 3 * HEAD * hh
                dqkv_ref[pl.ds(joff, t), base + HEAD:base + 2 * HEAD] = (dk * scale).astype(BF16)
                dqkv_ref[pl.ds(joff, t), base + 2 * HEAD:base + 3 * HEAD] = dv.astype(BF16)
                dcr_ref[hh, :, pl.ds(joff, t)] = -dc
            return c

        lax.fori_loop(0, n, kv_step, 0)
        for hh in hs:
            dqkv_ref[:, 3 * HEAD * hh:3 * HEAD * hh + HEAD] = dq_acc[hh].astype(BF16)

    qkv = []
    for hh in hs:
        for part in range(3):
            qkv.append(pl.BlockSpec((S, HEAD), lambda b, g, hh=hh, part=part: (b, 3 * (hp * g + hh) + part)))
    col_spec = pl.BlockSpec((None, hp, S, 1), lambda b, g: (b, g, 0, 0))
    row_spec = pl.BlockSpec((None, hp, 1, S), lambda b, g: (b, g, 0, 0))
    heads = pl.BlockSpec((S, hp * HEAD), lambda b, g: (b, g))
    return _hosted_call(
        body, rider, name="fox_bwd", grid=(B, H // hp),
        in_specs=qkv + [col_spec, row_spec, heads, col_spec, heads],
        out_specs=[pl.BlockSpec((S, 3 * hp * HEAD), lambda b, g: (b, g)), col_spec, row_spec],
        out_shape=[jax.ShapeDtypeStruct((T, 3 * H * HEAD), BF16), jax.ShapeDtypeStruct((B, H, S, 1), F32),
                   jax.ShapeDtypeStruct((B, H, 1, S), F32)],
        scratch_shapes=[pltpu.VMEM((hp, S, HEAD), F32), pltpu.VMEM((hp, S, 1), F32), pltpu.VMEM((hp, S, 1), F32)],
        semantics=("parallel", "parallel"),
    )(*([p_fox] * (3 * hp)), c_col, c_row, o, lse, do)


def _small_fn(x, b0, b1, H):
    S = x.shape[0]
    lane = lax.broadcasted_iota(jnp.int32, x.shape, 1)
    z = x + b0
    tail = jnp.log1p(jnp.exp(-jnp.abs(z)))
    softplus = jnp.maximum(z, 0.0) + tail
    logsig = -(jnp.maximum(-z, 0.0) + tail)
    g = -jnp.exp(b1) * softplus
    pre = jnp.where(lane < H, logsig, jnp.where(lane < 2 * H, g, 0.0))
    bl = _tile(S, 256, CHUNK)
    r = lax.broadcasted_iota(jnp.int32, (bl, bl), 0)
    c = lax.broadcasted_iota(jnp.int32, (bl, bl), 1)
    tri = (r >= c).astype(F32)
    tri_chunk = jnp.where((r >= c) & (jnp.right_shift(r, 6) == jnp.right_shift(c, 6)), 1.0, 0.0)
    carry = jnp.zeros((1, x.shape[1]), F32)
    parts = []
    for i in range(S // bl):
        blk = pre[i * bl:(i + 1) * bl, :]
        full = lax.dot_general(tri, blk, _NN, precision=EXACT, preferred_element_type=F32) + carry
        chunked = lax.dot_general(tri_chunk, blk, _NN, precision=EXACT, preferred_element_type=F32)
        parts.append(jnp.where(lane[:bl] < H, full, chunked))
        carry = carry + jnp.sum(blk, axis=0, keepdims=True)
    cum = parts[0] if len(parts) == 1 else jnp.concatenate(parts, axis=0)
    return jnp.where(lane < 2 * H, cum, jnp.where(lane < 3 * H, _sigmoid(x), 0.0))


def small_fwd(p_small, prm, B, S, H):
    T = B * S

    def body(x_ref, p_ref, o_ref):
        o_ref[...] = _small_fn(x_ref[...], p_ref[0:1, :], p_ref[1:2, :], H)

    blk = pl.BlockSpec((S, 128), lambda b: (b, 0))
    return pl.pallas_call(
        body, name="small_fwd", grid=(B,), in_specs=[blk, pl.BlockSpec((8, 128), lambda b: (0, 0))], out_specs=blk,
        out_shape=jax.ShapeDtypeStruct((T, 128), F32), compiler_params=_params("parallel"),
    )(p_small, prm)


def small_bwd(p_small, prm, d_out, B, S, H):
    T = B * S

    def body(x_ref, p_ref, d_ref, dx_ref, dp_ref):
        @pl.when(pl.program_id(0) == 0)
        def _():
            dp_ref[...] = jnp.zeros_like(dp_ref)

        _, vjp = jax.vjp(functools.partial(_small_fn, H=H), x_ref[...], p_ref[0:1, :], p_ref[1:2, :])
        dx, db0, db1 = vjp(d_ref[...])
        dx_ref[...] = dx.astype(BF16)
        dp_ref[0:1, :] += db0
        dp_ref[1:2, :] += db1

    blk = pl.BlockSpec((S, 128), lambda b: (b, 0))
    pblk = pl.BlockSpec((8, 128), lambda b: (0, 0))
    return pl.pallas_call(
        body, name="small_bwd", grid=(B,), in_specs=[blk, pblk, blk], out_specs=[blk, pblk],
        out_shape=[jax.ShapeDtypeStruct((T, 128), BF16), jax.ShapeDtypeStruct((8, 128), F32)],
        compiler_params=_params("arbitrary"),
    )(p_small, prm, d_out)


def gdn_prep_fwd(p_gqkv, cw, B, S, H):
    T = B * S

    def body(x_ref, w_ref, o_ref):
        for part in range(3):
            cols = slice(part * HEAD, (part + 1) * HEAD)
            y = _conv_fwd(x_ref[:, cols], w_ref.at[:, cols], GDN_CONV)
            a = y * _sigmoid(y)
            if part < 2:
                a = a * lax.rsqrt(jnp.sum(a * a, axis=-1, keepdims=True) + EPS)
            o_ref[:, cols] = a

    blk = pl.BlockSpec((S, 3 * HEAD), lambda b, h: (b, h))
    wblk = pl.BlockSpec((GDN_CONV, 3 * HEAD), lambda b, h: (0, h))
    return pl.pallas_call(
        body, name="gdn_prep_fwd", grid=(B, H), in_specs=[blk, wblk], out_specs=blk,
        out_shape=jax.ShapeDtypeStruct((T, 3 * H * HEAD), F32), compiler_params=_params("parallel", "parallel"),
    )(p_gqkv, cw)


def gdn_prep_bwd(p_gqkv, cw, d_out, B, S, H):
    T = B * S

    def body(x_ref, w_ref, d_ref, dx_ref, dw_ref):
        @pl.when(pl.program_id(1) == 0)
        def _():
            dw_ref[...] = jnp.zeros_like(dw_ref)

        x = x_ref[...]
        y, shifted = _conv_fwd(x, w_ref, GDN_CONV, keep_shifted=True)
        sg = _sigmoid(y)
        a = y * sg
        rs = lax.rsqrt(jnp.sum(a * a, axis=-1, keepdims=True) + EPS)
        d = d_ref[...]
        out = a * rs
        da_qk = rs * (d - out * jnp.sum(d * out, axis=-1, keepdims=True))
        is_qk = (pl.program_id(0) % 3) < 2
        da = jnp.where(is_qk, da_qk, d)
        dy = da * (sg + y * sg * (1.0 - sg))
        dx, dws = _conv_bwd(x, shifted, dy, w_ref, GDN_CONV)
        dx_ref[...] = dx.astype(BF16)
        for i in range(GDN_CONV):
            dw_ref[i:i + 1, :] += dws[i]

    blk = pl.BlockSpec((S, HEAD), lambda n, b: (b, n))
    wblk = pl.BlockSpec((GDN_CONV, HEAD), lambda n, b: (0, n))
    return pl.pallas_call(
        body, name="gdn_prep_bwd", grid=(3 * H, B), in_specs=[blk, wblk, blk], out_specs=[blk, wblk],
        out_shape=[jax.ShapeDtypeStruct((T, 3 * H * HEAD), BF16), jax.ShapeDtypeStruct((GDN_CONV, 3 * H * HEAD), F32)],
        compiler_params=_params("parallel", "arbitrary"),
    )(p_gqkv, cw, d_out)


@jax.custom_vjp
def _given_inverse(a, t):
    return t


def _given_inverse_fwd(a, t):
    return t, t


def _given_inverse_bwd(t, g):
    x = _dg(t, g, _TN, True)
    return -_dg(x, t, _NT, True), jnp.zeros_like(t)


_given_inverse.defvjp(_given_inverse_fwd, _given_inverse_bwd)


def _to_col(row):
    n = row.shape[1]
    r = lax.broadcasted_iota(jnp.int32, (n, n), 0)
    c = lax.broadcasted_iota(jnp.int32, (n, n), 1)
    return jnp.sum(jnp.where(r == c, row, 0.0), axis=1, keepdims=True)


def _intra_fn(k, v, beta_r, gcr, ops, t_known=None):
    n = len(k)
    m = k[0].shape[0]
    r = lax.broadcasted_iota(jnp.int32, (m, m), 0)
    c = lax.broadcasted_iota(jnp.int32, (m, m), 1)
    below = (r > c) & (jnp.right_shift(r, 6) == jnp.right_shift(c, 6))
    beta = [_to_col(beta_r[i]) for i in range(n)]
    gcc = [_to_col(gcr[i]) for i in range(n)]
    decay = [jnp.exp(jnp.where(below, gcc[i] - gcr[i], NEG)) for i in range(n)]
    kb = [k[i] * beta[i] for i in range(n)]
    a = [ops.nt(kb[i], k[i]) * decay[i] for i in range(n)]
    if t_known is None:
        p = [-a[i] for i in range(n)]
        tm = [jnp.where(r == c, 1.0, 0.0) + p[i] for i in range(n)]
        for _ in range(5):
            p = [ops.nn(p[i], p[i], hi=True) for i in range(n)]
            tm = [tm[i] + ops.nn(tm[i], p[i], hi=True) for i in range(n)]
    else:
        tm = [_given_inverse(a[i], t_known[i]) for i in range(n)]
    both = [ops.nn(tm[i], jnp.concatenate([v[i] * beta[i], kb[i] * jnp.exp(gcc[i])], axis=1), hi=True) for i in range(n)]
    u_hat = [both[i][:, :HEAD] for i in range(n)]
    w = [both[i][:, HEAD:] for i in range(n)]
    return tuple(u_hat), tuple(w), tuple(tm)


INTRA_NB = 32
PAIR = 1


def gdn_intra_fwd(qkvn, betar5, gcr5, B, S, H, rider=None):
    T = B * S
    UNIT = PAIR * CHUNK
    N = S // UNIT
    nb = min(INTRA_NB // PAIR, N)
    rows = nb * UNIT
    ns = N // nb

    def body(k_ref, v_ref, b_ref, gr_ref, uh_ref, w_ref, t_ref):
        sls = [slice(ci * UNIT, (ci + 1) * UNIT) for ci in range(nb)]
        u_hat, w, tm = _intra_fn(tuple(k_ref[sl, :] for sl in sls), tuple(v_ref[sl, :] for sl in sls),
                                 tuple(b_ref[ci] for ci in range(nb)), tuple(gr_ref[ci] for ci in range(nb)), _RawOps)
        for ci, sl in enumerate(sls):
            uh_ref[sl, :] = u_hat[ci]
            w_ref[sl, :] = w[ci]
            t_ref[ci] = tm[ci]

    rowspec = pl.BlockSpec((None, None, nb, 1, UNIT), lambda b, h, i: (b, h, i, 0, 0))
    sqspec = pl.BlockSpec((None, None, nb, UNIT, UNIT), lambda b, h, i: (b, h, i, 0, 0))
    out = pl.BlockSpec((rows, HEAD), lambda b, h, i: (b * ns + i, h))
    return _hosted_call(
        body, rider, name="gdn_intra_fwd", grid=(B, H, ns),
        in_specs=[pl.BlockSpec((rows, HEAD), lambda b, h, i: (b * ns + i, 3 * h + 1)),
                  pl.BlockSpec((rows, HEAD), lambda b, h, i: (b * ns + i, 3 * h + 2)),
                  rowspec, rowspec],
        out_specs=[out, out, sqspec],
        out_shape=[jax.ShapeDtypeStruct((T, H * HEAD), F32), jax.ShapeDtypeStruct((T, H * HEAD), F32),
                   jax.ShapeDtypeStruct((B, H, N, UNIT, UNIT), F32)],
        scratch_shapes=[], semantics=("parallel", "parallel", "parallel"),
    )(qkvn, qkvn, betar5, gcr5)


def gdn_intra_bwd(qkvn, betar5, gcr5, t_inv, d_uh, d_w, dq_in, dk_in, B, S, H):
    T = B * S
    UNIT = PAIR * CHUNK
    N = S // UNIT
    nb = min(INTRA_NB // PAIR, N)
    rows = nb * UNIT
    ns = N // nb

    def body(k_ref, v_ref, b_ref, gr_ref, t_ref, duh_ref, dw_ref, dq_ref, dk_ref, o_ref, db_ref, dgr_ref):
        sls = [slice(ci * UNIT, (ci + 1) * UNIT) for ci in range(nb)]
        chunks = range(nb)
        _, vjp = jax.vjp(
            functools.partial(_intra_fn, ops=_DiffOps, t_known=tuple(t_ref[ci] for ci in chunks)),
            tuple(k_ref[sl, :] for sl in sls), tuple(v_ref[sl, :] for sl in sls), tuple(b_ref[ci] for ci in chunks),
            tuple(gr_ref[ci] for ci in chunks))
        zero = jnp.zeros((UNIT, UNIT), F32)
        dk, dv, db, dgr = vjp((tuple(duh_ref[sl, :] for sl in sls), tuple(dw_ref[sl, :] for sl in sls),
                               tuple(zero for _ in chunks)))
        for ci, sl in enumerate(sls):
            o_ref[sl, 0:HEAD] = dq_ref[sl, :]
            o_ref[sl, HEAD:2 * HEAD] = dk[ci] + dk_ref[sl, :]
            o_ref[sl, 2 * HEAD:3 * HEAD] = dv[ci]
            db_ref[ci] = db[ci]
            dgr_ref[ci] = dgr[ci]

    rowspec = pl.BlockSpec((None, None, nb, 1, UNIT), lambda b, h, i: (b, h, i, 0, 0))
    sqspec = pl.BlockSpec((None, None, nb, UNIT, UNIT), lambda b, h, i: (b, h, i, 0, 0))
    head = pl.BlockSpec((rows, HEAD), lambda b, h, i: (b * ns + i, h))
    return pl.pallas_call(
        body, name="gdn_intra_bwd", grid=(B, H, ns),
        in_specs=[pl.BlockSpec((rows, HEAD), lambda b, h, i: (b * ns + i, 3 * h + 1)),
                  pl.BlockSpec((rows, HEAD), lambda b, h, i: (b * ns + i, 3 * h + 2)),
                  rowspec, rowspec, sqspec, head, head, head, head],
        out_specs=[pl.BlockSpec((rows, 3 * HEAD), lambda b, h, i: (b * ns + i, h)), rowspec, rowspec],
        out_shape=[jax.ShapeDtypeStruct((T, 3 * H * HEAD), F32),
                   jax.ShapeDtypeStruct((B, H, N, 1, UNIT), F32), jax.ShapeDtypeStruct((B, H, N, 1, UNIT), F32)],
        compiler_params=_params("parallel", "parallel", "parallel"),
    )(qkvn, qkvn, betar5, gcr5, t_inv, d_uh, d_w, dq_in, dk_in)


def _inter_fn(q, k, u_hat, w, gcr, state, ops):
    n = len(q)
    r = lax.broadcasted_iota(jnp.int32, (CHUNK, CHUNK), 0)
    c = lax.broadcasted_iota(jnp.int32, (CHUNK, CHUNK), 1)
    last = lax.broadcasted_iota(jnp.int32, (1, CHUNK), 1) == CHUNK - 1
    gcc = [_to_col(gcr[i]) for i in range(n)]
    gl = [jnp.sum(jnp.where(last, gcr[i], 0.0), axis=1, keepdims=True) for i in range(n)]
    decay = [jnp.exp(jnp.where(r >= c, gcc[i] - gcr[i], NEG)) for i in range(n)]
    qs = [q[i] * (HEAD ** -0.5) for i in range(n)]
    ws = [ops.nn(w[i], state[i]) for i in range(n)]
    qst = [ops.nn(qs[i] * jnp.exp(gcc[i]), state[i]) for i in range(n)]
    attn = [ops.nt(qs[i], k[i]) * decay[i] for i in range(n)]
    u = [u_hat[i] - ws[i] for i in range(n)]
    o = [qst[i] + ops.nn(attn[i], u[i]) for i in range(n)]
    kdu = [ops.tn(k[i] * jnp.exp(gl[i] - gcc[i]), u[i]) for i in range(n)]
    new_state = [state[i] * jnp.exp(gl[i]) + kdu[i] for i in range(n)]
    return tuple(o), tuple(new_state)


INTER_HEADS = 8
INTER_ROWS = 512
INTER_ROWS_BWD = 256


def _inter_heads(H):
    return INTER_HEADS if H % INTER_HEADS == 0 else (4 if H % 4 == 0 else 1)


def _inter_specs(ts, ns, hp, backward):
    at = (lambda s: ns - 1 - s) if backward else (lambda s: s)
    nc = ts // CHUNK
    qk = []
    for hh in range(hp):
        qk.append(pl.BlockSpec((ts, HEAD), lambda b, g, s, hh=hh: (b * ns + at(s), 3 * (hp * g + hh))))
        qk.append(pl.BlockSpec((ts, HEAD), lambda b, g, s, hh=hh: (b * ns + at(s), 3 * (hp * g + hh) + 1)))
    heads = pl.BlockSpec((ts, hp * HEAD), lambda b, g, s: (b * ns + at(s), g))
    rowspec = pl.BlockSpec((None, hp, nc, 1, CHUNK), lambda b, g, s: (b, g, at(s), 0, 0))
    stspec = pl.BlockSpec((None, hp, nc, HEAD, HEAD), lambda b, g, s: (b, g, at(s), 0, 0))
    return qk, heads, rowspec, stspec


def gdn_inter_fwd(qkvn, u_hat, w, gcr5, p_gz, gnorm, B, S, H):
    T = B * S
    N = S // CHUNK
    hp = _inter_heads(H)
    hs = range(hp)
    ts = _tile(S, INTER_ROWS, CHUNK)
    ns, nc = S // ts, ts // CHUNK

    def body(*refs):
        qk_refs, (uh_ref, w_ref, gr_ref, z_ref, gn_ref, o_ref, st_ref, y_ref, s_scr) = refs[:2 * hp], refs[2 * hp:]

        @pl.when(pl.program_id(2) == 0)
        def _():
            s_scr[...] = jnp.zeros_like(s_scr)

        gn = gn_ref[...]

        def step(n, c):
            rows = pl.ds(pl.multiple_of(n * CHUNK, CHUNK), CHUNK)
            st = tuple(s_scr[hh] for hh in hs)
            for hh in hs:
                st_ref[hh, n] = st[hh]
            o, new = _inter_fn(tuple(qk_refs[2 * hh][rows, :] for hh in hs), tuple(qk_refs[2 * hh + 1][rows, :] for hh in hs),
                               tuple(uh_ref[rows, hh * HEAD:(hh + 1) * HEAD] for hh in hs),
                               tuple(w_ref[rows, hh * HEAD:(hh + 1) * HEAD] for hh in hs),
                               tuple(gr_ref[hh, n] for hh in hs), st, _RawOps)
            for hh in hs:
                cols = slice(hh * HEAD, (hh + 1) * HEAD)
                o_ref[rows, cols] = o[hh]
                s_scr[hh] = new[hh]
                z = z_ref[rows, cols]
                r = lax.rsqrt(jnp.mean(o[hh] * o[hh], axis=-1, keepdims=True) + EPS)
                y_ref[rows, cols] = (o[hh] * r * gn * z * _sigmoid(z)).astype(BF16)
            return c

        lax.fori_loop(0, nc, step, 0)

    qk, heads, rowspec, stspec = _inter_specs(ts, ns, hp, backward=False)
    return pl.pallas_call(
        body, name="gdn_inter_fwd", grid=(B, H // hp, ns),
        in_specs=qk + [heads, heads, rowspec, heads, pl.BlockSpec((1, HEAD), lambda b, g, s: (0, 0))],
        out_specs=[heads, stspec, heads],
        out_shape=[jax.ShapeDtypeStruct((T, H * HEAD), F32), jax.ShapeDtypeStruct((B, H, N, HEAD, HEAD), F32),
                   jax.ShapeDtypeStruct((T, H * HEAD), BF16)],
        scratch_shapes=[pltpu.VMEM((hp, HEAD, HEAD), F32)],
        compiler_params=_params("parallel", "parallel", "arbitrary"),
    )(*([qkvn] * (2 * hp)), u_hat, w, gcr5, p_gz, gnorm)


def gdn_inter_bwd(qkvn, u_hat, w, gcr5, states, o, p_gz, gnorm, d_y, B, S, H, rider=None):
    T = B * S
    N = S // CHUNK
    hp = _inter_heads(H)
    hs = range(hp)
    ts = _tile(S, INTER_ROWS_BWD, CHUNK)
    ns, nc = S // ts, ts // CHUNK

    def body(*refs):
        qk_refs = refs[:2 * hp]
        (uh_ref, w_ref, gr_ref, st_ref, o_ref, z_ref, gn_ref, dy_ref,
         dq_ref, dk_ref, duh_ref, dw_ref, dgr_ref, dz_ref, dgn_ref, ds_scr) = refs[2 * hp:]

        @pl.when(pl.program_id(2) == 0)
        def _():
            ds_scr[...] = jnp.zeros_like(ds_scr)
            dgn_ref[...] = jnp.zeros_like(dgn_ref)

        cols = [slice(hh * HEAD, (hh + 1) * HEAD) for hh in hs]
        gn = gn_ref[...]

        def through_norm(rows, hh):
            ov, z, d = o_ref[rows, cols[hh]], z_ref[rows, cols[hh]], dy_ref[rows, cols[hh]]
            r = lax.rsqrt(jnp.mean(ov * ov, axis=-1, keepdims=True) + EPS)
            xh = ov * r
            sg = _sigmoid(z)
            d_n = d * (z * sg)
            dz_ref[rows, cols[hh]] = (d * xh * gn * (sg + z * sg * (1.0 - sg))).astype(BF16)
            dgn_ref[0:1, :] += jnp.sum(d_n * xh, axis=0, keepdims=True)
            dxh = d_n * gn
            return r * (dxh - xh * jnp.mean(dxh * xh, axis=-1, keepdims=True))

        def step(i, c):
            n = nc - 1 - i
            rows = pl.ds(pl.multiple_of(n * CHUNK, CHUNK), CHUNK)
            _, vjp = jax.vjp(functools.partial(_inter_fn, ops=_DiffOps),
                             tuple(qk_refs[2 * hh][rows, :] for hh in hs), tuple(qk_refs[2 * hh + 1][rows, :] for hh in hs),
                             tuple(uh_ref[rows, cols[hh]] for hh in hs), tuple(w_ref[rows, cols[hh]] for hh in hs),
                             tuple(gr_ref[hh, n] for hh in hs), tuple(st_ref[hh, n] for hh in hs))
            dq, dk, duh, dw, dgr, ds = vjp((tuple(through_norm(rows, hh) for hh in hs), tuple(ds_scr[hh] for hh in hs)))
            for hh in hs:
                dq_ref[rows, cols[hh]] = dq[hh]
                dk_ref[rows, cols[hh]] = dk[hh]
                duh_ref[rows, cols[hh]] = duh[hh]
                dw_ref[rows, cols[hh]] = dw[hh]
                dgr_ref[hh, n] = dgr[hh]
                ds_scr[hh] = ds[hh]
            return c

        lax.fori_loop(0, nc, step, 0)

    qk, heads, rowspec, stspec = _inter_specs(ts, ns, hp, backward=True)
    hshape = jax.ShapeDtypeStruct((T, H * HEAD), F32)
    return _hosted_call(
        body, rider, name="gdn_inter_bwd", grid=(B, H // hp, ns),
        in_specs=qk + [heads, heads, rowspec, stspec, heads, heads, pl.BlockSpec((1, HEAD), lambda b, g, s: (0, 0)), heads],
        out_specs=[heads, heads, heads, heads, rowspec, heads,
                   pl.BlockSpec((None, None, 8, HEAD), lambda b, g, s: (b, g, 0, 0))],
        out_shape=[hshape, hshape, hshape, hshape, jax.ShapeDtypeStruct((B, H, N, 1, CHUNK), F32),
                   jax.ShapeDtypeStruct((T, H * HEAD), BF16), jax.ShapeDtypeStruct((B, H // hp, 8, HEAD), F32)],
        scratch_shapes=[pltpu.VMEM((hp, HEAD, HEAD), F32)], semantics=("parallel", "parallel", "arbitrary"),
    )(*([qkvn] * (2 * hp)), u_hat, w, gcr5, states, o, p_gz, gnorm, d_y)


def adamw(w, g, m, v, name):
    shape = w.shape
    lead = (None,) * (w.ndim - 2)
    zeros = (0,) * (w.ndim - 2)
    R, C = shape[-2:]
    g2 = g.reshape(R, C)
    tr, tc = _tile(R, 128, 8), C
    if tr % 8 and R > 8:
        tr, tc = R, _tile(C, 128)

    def body(w_ref, g_ref, m_ref, v_ref, d_ref, nm_ref, nv_ref):
        gv = g_ref[...]
        nm = ADAM_B1 * m_ref[...] + (1.0 - ADAM_B1) * gv
        nv = ADAM_B2 * v_ref[...] + (1.0 - ADAM_B2) * (gv * gv)
        m_hat = nm / (1.0 - ADAM_B1 ** ADAM_STEP)
        v_hat = nv / (1.0 - ADAM_B2 ** ADAM_STEP)
        d_ref[...] = -ADAM_LR * (m_hat / (jnp.sqrt(v_hat) + ADAM_EPS) + ADAM_WD * w_ref[...])
        nm_ref[...] = nm
        nv_ref[...] = nv

    blk = pl.BlockSpec(lead + (tr, tc), lambda i, j: zeros + (i, j))
    gblk = pl.BlockSpec((tr, tc), lambda i, j: (i, j))
    sh = jax.ShapeDtypeStruct(shape, F32)
    return pl.pallas_call(
        body, name=name, grid=(R // tr, C // tc), in_specs=[blk, gblk, blk, blk], out_specs=[blk] * 3, out_shape=[sh] * 3,
        compiler_params=_params("parallel", "parallel"),
    )(w, g2, m, v)


def _place():
    x, y, c = lax.axis_index("x"), lax.axis_index("y"), lax.axis_index("c")
    chips = [(1 - x, y), (x, 1 - y), (1 - x, 1 - y)]
    return x, y, c, chips


_HBM = pl.BlockSpec(memory_space=pltpu.HBM)


def allgather_weights(packs):
    n = len(packs)

    def body(*refs):
        in_refs, out_refs, (send_sems, recv_sems) = refs[:n], refs[n:2 * n], refs[2 * n:]
        x, y, c, chips = _place()
        me_s = 2 * x + y
        me, sibling = (x, y, c), (x, y, 1 - c)
        shards = [2 * chip[0] + chip[1] for chip in chips]

        def copy(a, k, shard, half, to, src=None):
            dst = out_refs[a].at[shard, half]
            return pltpu.make_async_remote_copy(src_ref=dst if src is None else src, dst_ref=dst,
                                                send_sem=send_sems.at[6 * a + k], recv_sem=recv_sems.at[6 * a + k],
                                                device_id=to, device_id_type=MESH)

        first = [copy(a, j, me_s, c, (*chip, c), src=in_refs[a].at[c]) for a in range(n) for j, chip in enumerate(chips)]
        for cp in first:
            cp.start()
        passed = []
        for a in range(n):
            for j in range(3):
                copy(a, j, shards[j], c, me).wait_recv()
                passed.append(copy(a, 3 + j, shards[j], c, sibling))
                passed[-1].start()
        for a in range(n):
            for j in range(3):
                copy(a, 3 + j, shards[j], 1 - c, me).wait_recv()
        for cp in first + passed:
            cp.wait_send()

    return pl.pallas_call(
        body, name="allgather_weights", in_specs=[_HBM] * n, out_specs=[_HBM] * n,
        out_shape=[jax.ShapeDtypeStruct((N_CHIP,) + p.shape, p.dtype) for p in packs],
        scratch_shapes=[pltpu.SemaphoreType.DMA((6 * n,)), pltpu.SemaphoreType.DMA((6 * n,))],
    )(*packs)


class _Rider:
    def __init__(self, inputs, out_shapes, n_sems, sends, recvs, aliases=None):
        self.inputs, self.out_shapes, self.n_sems = list(inputs), list(out_shapes), n_sems
        self.sends, self.recvs, self.aliases = sends, recvs, aliases or {}

    def start(self, *refs):
        for cp in self.sends(*refs):
            cp.start()

    def wait(self, *refs):
        for cp in self.recvs(*refs):
            cp.wait_recv()
        for cp in self.sends(*refs):
            cp.wait_send()


def _remote(src, dst, send_sems, recv_sems, k, to):
    return pltpu.make_async_remote_copy(src_ref=src, dst_ref=dst, send_sem=send_sems.at[k], recv_sem=recv_sems.at[k],
                                        device_id=to, device_id_type=MESH)


def _run_alone(rider, name):
    ri = len(rider.inputs)

    def body(*refs):
        ins, outs, (send_sems, recv_sems) = refs[:ri], refs[ri:-2], refs[-2:]
        rider.start(ins, outs, send_sems, recv_sems)
        rider.wait(ins, outs, send_sems, recv_sems)

    return pl.pallas_call(
        body, name=name, in_specs=[_HBM] * ri, out_specs=[_HBM] * len(rider.out_shapes), out_shape=rider.out_shapes,
        scratch_shapes=[pltpu.SemaphoreType.DMA((rider.n_sems,))] * 2, input_output_aliases=rider.aliases,
    )(*rider.inputs)


def _hosted_call(body, rider, *, name, grid, in_specs, out_specs, out_shape, scratch_shapes, semantics):
    if rider is None:
        return pl.pallas_call(body, name=name, grid=grid, in_specs=in_specs, out_specs=out_specs, out_shape=out_shape,
                              scratch_shapes=scratch_shapes, compiler_params=_params(*semantics))
    n_in, n_out, n_scr = len(in_specs), len(out_specs), len(scratch_shapes)
    ri, ro = len(rider.inputs), len(rider.out_shapes)

    def hosted(*refs):
        parts, p = [], 0
        for cnt in (n_in, ri, n_out, ro, n_scr, 2):
            parts.append(refs[p:p + cnt])
            p += cnt
        ins, rins, outs, routs, scr, (send_sems, recv_sems) = parts
        first = functools.reduce(jnp.logical_and, [pl.program_id(a) == 0 for a in range(len(grid))])
        last = functools.reduce(jnp.logical_and, [pl.program_id(a) == grid[a] - 1 for a in range(len(grid))])

        @pl.when(first)
        def _():
            rider.start(rins, routs, send_sems, recv_sems)

        body(*ins, *outs, *scr)

        @pl.when(last)
        def _():
            rider.wait(rins, routs, send_sems, recv_sems)

    call = pl.pallas_call(
        hosted, name=name, grid=grid, in_specs=list(in_specs) + [_HBM] * ri, out_specs=list(out_specs) + [_HBM] * ro,
        out_shape=list(out_shape) + rider.out_shapes,
        scratch_shapes=list(scratch_shapes) + [pltpu.SemaphoreType.DMA((rider.n_sems,))] * 2,
        input_output_aliases={n_in + i: n_out + o for i, o in rider.aliases.items()},
        compiler_params=_params(*(("arbitrary",) * len(grid))))

    def run(*args):
        res = call(*args, *rider.inputs)
        return res[:n_out], res[n_out:]

    return run


def _ride_gather_ici(packs):
    n = len(packs)

    def sends(ins, outs, send_sems, recv_sems):
        x, y, c, chips = _place()
        return [_remote(ins[a].at[c], outs[a].at[2 * x + y, c], send_sems, recv_sems, 3 * a + j, (*chip, c))
                for a in range(n) for j, chip in enumerate(chips)]

    def recvs(ins, outs, send_sems, recv_sems):
        x, y, c, chips = _place()
        return [_remote(ins[a].at[c], outs[a].at[2 * chip[0] + chip[1], c], send_sems, recv_sems, 3 * a + j, (x, y, c))
                for a in range(n) for j, chip in enumerate(chips)]

    return _Rider(packs, [jax.ShapeDtypeStruct((N_CHIP,) + p.shape, p.dtype) for p in packs], 3 * n, sends, recvs)


def _ride_gather_d2d(gathered):
    n = len(gathered)

    def copies(landing_half, to):
        def build(ins, outs, send_sems, recv_sems):
            x, y, c, chips = _place()
            return [_remote(ins[a].at[2 * chip[0] + chip[1], c], outs[a].at[2 * chip[0] + chip[1], landing_half(c)],
                            send_sems, recv_sems, 3 * a + j, to(x, y, c))
                    for a in range(n) for j, chip in enumerate(chips)]
        return build

    return _Rider(gathered, [jax.ShapeDtypeStruct(g.shape, g.dtype) for g in gathered], 3 * n,
                  copies(lambda c: c, lambda x, y, c: (x, y, 1 - c)), copies(lambda c: 1 - c, lambda x, y, c: (x, y, c)),
                  aliases={a: a for a in range(n)})


def _ride_exchange(gs):
    n = len(gs)
    shapes = [g.shape[1:] if g.ndim == 4 else g.shape[:2] + (g.shape[2] // 2,) for g in gs]

    def copies(ins, outs, send_sems, recv_sems):
        x, y, c, _ = _place()

        def theirs(a):
            if gs[a].ndim == 4:
                return ins[a].at[1 - c]
            cols = shapes[a][2]
            return ins[a].at[:, :, pl.ds((1 - c) * cols, cols)]

        return [_remote(theirs(a), outs[a], send_sems, recv_sems, a, (x, y, 1 - c)) for a in range(n)]

    return _Rider(gs, [jax.ShapeDtypeStruct(sh, g.dtype) for sh, g in zip(shapes, gs)], n, copies, copies)


def _ride_scatter(b16s, to=(0, 1, 2), landing=None):
    n = len(b16s)

    def sends(ins, outs, send_sems, recv_sems):
        x, y, c, chips = _place()
        return [_remote(ins[a].at[2 * chips[j][0] + chips[j][1]], outs[a].at[2 * x + y], send_sems, recv_sems, 3 * a + j,
                        (*chips[j], c)) for a in range(n) for j in to]

    def recvs(ins, outs, send_sems, recv_sems):
        x, y, c, chips = _place()
        return [_remote(ins[a].at[2 * x + y], outs[a].at[2 * chips[j][0] + chips[j][1]], send_sems, recv_sems, 3 * a + j,
                        (x, y, c)) for a in range(n) for j in to]

    return _Rider(list(b16s) + list(landing or []), [jax.ShapeDtypeStruct(b.shape, b.dtype) for b in b16s], 3 * n,
                  sends, recvs, aliases={n + a: a for a in range(n)} if landing else None)


def _slab_tile(r, cols):
    tr = _tile(r, 256, 16)
    if tr % 16 == 0:
        return tr, cols
    return r, _tile(cols, 128)


def add_halves(g, got, idx, name):
    ns, r, cols = got.shape
    tr, tc = _slab_tile(r, cols)
    if g.ndim == 4:
        mine = pl.BlockSpec((None, None, tr, tc), lambda s, i, j, idx_ref: (idx_ref[0], s, i, j))
    else:
        mine = pl.BlockSpec((None, tr, tc), lambda s, i, j, idx_ref: (s, i, idx_ref[0] * (cols // tc) + j))

    def body(idx_ref, a_ref, b_ref, o32_ref, o16_ref):
        s = a_ref[...] + b_ref[...]
        o32_ref[...] = s
        o16_ref[...] = s.astype(BF16)

    blk = pl.BlockSpec((None, tr, tc), lambda s, i, j, idx_ref: (s, i, j))
    return pl.pallas_call(
        body, name=name,
        grid_spec=pltpu.PrefetchScalarGridSpec(
            num_scalar_prefetch=1, grid=(ns, r // tr, cols // tc),
            in_specs=[mine, blk], out_specs=[blk, blk]),
        out_shape=[jax.ShapeDtypeStruct((ns, r, cols), F32), jax.ShapeDtypeStruct((ns, r, cols), BF16)],
        compiler_params=_params("parallel", "parallel", "parallel"),
    )(idx, g, got)


def add_chips(a32, got16, idx, name):
    ns, r, cols = a32.shape
    tr, tc = _slab_tile(r, cols)

    def body(idx_ref, a_ref, r1_ref, r2_ref, r3_ref, o_ref):
        o_ref[...] = ((a_ref[...] + r1_ref[...].astype(F32)) + r2_ref[...].astype(F32)) + r3_ref[...].astype(F32)

    def slab(k):
        return pl.BlockSpec((None, tr, tc), lambda i, j, idx_ref: ((idx_ref[1] + k) % ns, i, j))

    return pl.pallas_call(
        body, name=name,
        grid_spec=pltpu.PrefetchScalarGridSpec(
            num_scalar_prefetch=1, grid=(r // tr, cols // tc), in_specs=[slab(0), slab(1), slab(2), slab(3)],
            out_specs=pl.BlockSpec((tr, tc), lambda i, j, idx_ref: (i, j))),
        out_shape=jax.ShapeDtypeStruct((r, cols), F32),
        compiler_params=_params("parallel", "parallel"),
    )(idx, a32, got16, got16, got16)


def share_halves(halves):
    n = len(halves)

    def body(*refs):
        in_refs, out_refs, (send_sems, recv_sems) = refs[:n], refs[n:2 * n], refs[2 * n:]
        x, y, c, _ = _place()
        cps = [pltpu.make_async_remote_copy(src_ref=in_refs[a], dst_ref=out_refs[a], send_sem=send_sems.at[a],
                                            recv_sem=recv_sems.at[a], device_id=(x, y, 1 - c), device_id_type=MESH)
               for a in range(n)]
        for cp in cps:
            cp.start()
        for cp in cps:
            cp.wait()

    return pl.pallas_call(
        body, name="share_halves", in_specs=[_HBM] * n, out_specs=[_HBM] * n,
        out_shape=[jax.ShapeDtypeStruct(h.shape, F32) for h in halves],
        scratch_shapes=[pltpu.SemaphoreType.DMA((n,)), pltpu.SemaphoreType.DMA((n,))],
    )(*halves)


def allreduce_small(v):
    R, _ = v.shape

    def body(in_ref, out_ref, slots, send_sems, recv_sems):
        x, y, c, _ = _place()
        me = 4 * x + 2 * y + c
        slots[me] = in_ref[...]
        cps = []
        for k in range(1, N_DEV):
            to = (x ^ (k >> 2), y ^ ((k >> 1) & 1), c ^ (k & 1))
            cps.append(pltpu.make_async_remote_copy(src_ref=in_ref, dst_ref=slots.at[me], send_sem=send_sems.at[k - 1],
                                                    recv_sem=recv_sems.at[k - 1], device_id=to, device_id_type=MESH))
        for cp in cps:
            cp.start()
        for k in range(1, N_DEV):
            frm = 4 * (x ^ (k >> 2)) + 2 * (y ^ ((k >> 1) & 1)) + (c ^ (k & 1))
            pltpu.make_async_remote_copy(src_ref=in_ref, dst_ref=slots.at[frm], send_sem=send_sems.at[k - 1],
                                         recv_sem=recv_sems.at[k - 1], device_id=(x, y, c), device_id_type=MESH).wait_recv()
        for cp in cps:
            cp.wait_send()
        acc = slots[0]
        for d in range(1, N_DEV):
            acc = acc + slots[d]
        out_ref[...] = acc

    vm = pl.BlockSpec(memory_space=pltpu.VMEM)
    return pl.pallas_call(
        body, name="allreduce_small", in_specs=[vm], out_specs=vm, out_shape=jax.ShapeDtypeStruct((R, ROW), F32),
        scratch_shapes=[pltpu.VMEM((N_DEV, R, ROW), F32), pltpu.SemaphoreType.DMA((N_DEV - 1,)),
                        pltpu.SemaphoreType.DMA((N_DEV - 1,))],
    )(v)


def _rows_of(n, unit=16):
    return -(-n // (unit * ROW)) * unit


def _pack_rows(items, total_rows, dtype, unit=16):
    parts = []
    used = 0
    for a in items:
        flat = a.reshape(-1)
        r = _rows_of(flat.shape[0], unit)
        flat = jnp.pad(flat, (0, r * ROW - flat.shape[0]))
        parts.append(flat.reshape(r, ROW))
        used += r
    if total_rows > used:
        parts.append(jnp.zeros((total_rows - used, ROW), dtype))
    return jnp.concatenate(parts, axis=0)


def _unpack_rows(buf, shapes, unit=16):
    lead = buf.shape[:-2]
    out = []
    off = 0
    for shp in shapes:
        n = math.prod(shp)
        r = _rows_of(n, unit)
        piece = buf[..., off:off + r, :].reshape(*lead, r * ROW)[..., :n].reshape(*lead, *shp)
        out.append(piece)
        off += r
    return out


def _interleave_heads(w, H):
    lead = w.shape[:-1]
    return w.reshape(*lead, 3, H, HEAD).swapaxes(-3, -2).reshape(*lead, 3 * H * HEAD)


def _deinterleave_heads(w, H):
    lead = w.shape[:-1]
    return w.reshape(*lead, H, 3, HEAD).swapaxes(-3, -2).reshape(*lead, 3 * H * HEAD)


def _interleave_head_rows(w, H):
    return w.reshape(3, H, HEAD, w.shape[-1]).swapaxes(0, 1).reshape(3 * H * HEAD, w.shape[-1])


def _deinterleave_head_rows(w, H):
    return w.reshape(H, 3, HEAD, w.shape[-1]).swapaxes(0, 1).reshape(3 * H * HEAD, w.shape[-1])


def kernel(x, norm_mix, w_in, fox_f_bias, gdn_conv_w, gdn_a_log, gdn_dt_bias, gdn_norm, w_branch_fox, w_branch_gdn, w_out, norm_ffn, w_up, ffn_conv_w, w_down, norm_final, loss_target, m_norm_mix, m_w_in, m_fox_f_bias, m_gdn_conv_w, m_gdn_a_log, m_gdn_dt_bias, m_gdn_norm, m_w_branch_fox, m_w_branch_gdn, m_w_out, m_norm_ffn, m_w_up, m_ffn_conv_w, m_w_down, m_norm_final, v_norm_mix, v_w_in, v_fox_f_bias, v_gdn_conv_w, v_gdn_a_log, v_gdn_dt_bias, v_gdn_norm, v_w_branch_fox, v_w_branch_gdn, v_w_out, v_norm_ffn, v_w_up, v_ffn_conv_w, v_w_down, v_norm_final):
    B, S, D = x.shape
    T = B * S
    H = D // HEAD
    N = S // CHUNK
    FF = w_down.shape[1] * N_CHIP
    d_in = 9 * D + 3 * H
    assert w_in.shape[2] * N_CHIP == d_in and 3 * H <= 128

    cidx = lax.axis_index("c").astype(jnp.int32)
    sidx = (2 * lax.axis_index("x") + lax.axis_index("y")).astype(jnp.int32)
    idx = jnp.stack([cidx, sidx])

    rowed = [w_branch_fox[0], w_branch_gdn[0], w_out[0], w_down[0]]
    convs = [gdn_conv_w[0], ffn_conv_w[0]]
    rowed_shapes = [a.shape for a in rowed]
    conv_shapes = [a.shape + (2,) for a in convs]
    pad_rows = lambda shapes: -(-sum(_rows_of(math.prod(s)) for s in shapes) // 256) * 128
    Rh, Rc = pad_rows(rowed_shapes), pad_rows(conv_shapes)
    halves = lambda a: a.reshape(2, a.shape[0] // 2, a.shape[1])
    c_in = w_in.shape[2]
    packs_a = [w_in[0].T.astype(BF16).reshape(c_in, 2, D // 2).transpose(1, 0, 2),
               halves(_pack_rows([lax.bitcast_convert_type(a, BF16) for a in convs], 2 * Rc, BF16))]
    packs_b = [halves(w_up[0].astype(BF16)), halves(_pack_rows([a.astype(BF16) for a in rowed], 2 * Rh, BF16))]
    own = lambda gs, ps: [lax.dynamic_update_slice(g, p[None], (sidx, 0, 0, 0)) for g, p in zip(gs, ps)]
    by_cols = lambda g: g.transpose(1, 2, 0, 3).reshape(2 * g.shape[2], N_CHIP * g.shape[3])
    cat_cols = lambda p: jnp.concatenate([p[i] for i in range(N_CHIP)], axis=-1)
    cat_rows = lambda p: p.reshape(-1, p.shape[-1])
    g_in, g_conv = own(allgather_weights(packs_a), packs_a)
    W_inT = g_in.transpose(0, 2, 1, 3).reshape(N_CHIP * c_in, D)
    conv_parts = _unpack_rows(g_conv.reshape(N_CHIP, 2 * Rc, ROW), conv_shapes)
    gconv = cat_cols(lax.bitcast_convert_type(conv_parts[0], F32))
    fconv = cat_cols(lax.bitcast_convert_type(conv_parts[1], F32))

    o1, o2 = 3 * D, 3 * D + H
    o3, o4, o5, o6 = o2 + 3 * D, o2 + 3 * D + H, o2 + 3 * D + 2 * H, o2 + 4 * D + 2 * H
    W_foxT = _interleave_head_rows(W_inT[:o1], H)
    W_gqkvT = _interleave_head_rows(W_inT[o2:o3], H)
    W_gzT = W_inT[o5:o6]
    W_gatesT = W_inT[o6:]
    W_smallT = jnp.concatenate([W_inT[o1:o2], W_inT[o3:o5], jnp.zeros((128 - 3 * H, D), BF16)], axis=0)
    gconv_i = _interleave_heads(gconv, H)
    fconv_g, fconv_v = fconv[:, :FF], fconv[:, FF:]
    prm = jnp.zeros((8, 128), F32)
    prm = prm.at[0, 0:H].set(fox_f_bias[0]).at[0, H:2 * H].set(gdn_dt_bias[0]).at[1, H:2 * H].set(gdn_a_log[0])

    x2 = x.reshape(T, D)
    tgt = loss_target.reshape(T, D)

    hn1 = rmsnorm_fwd(x2, norm_mix, "rmsnorm_mix")
    p_fox = matmul(hn1, W_foxT, "nt", "proj_fox", out_dtype=BF16)
    p_gqkv = matmul(hn1, W_gqkvT, "nt", "proj_gqkv")
    p_gz = matmul(hn1, W_gzT, "nt", "proj_gz")
    p_gates = matmul(hn1, W_gatesT, "nt", "proj_gates")
    p_small = matmul(hn1, W_smallT, "nt", "proj_small")

    sm = small_fwd(p_small, prm, B, S, H)
    heads = lambda a: a.reshape(B, S, H).transpose(0, 2, 1)
    c_bhs, gc_bhs, beta_bhs = heads(sm[:, 0:H]), heads(sm[:, H:2 * H]), heads(sm[:, 2 * H:3 * H])
    c_col, c_row = c_bhs[..., None], c_bhs[:, :, None, :]
    gcr5 = gc_bhs.reshape(B, H, N, 1, CHUNK)
    gcr_u = gc_bhs.reshape(B, H, N // PAIR, 1, PAIR * CHUNK)
    betar_u = beta_bhs.reshape(B, H, N // PAIR, 1, PAIR * CHUNK)

    (o_fox, o_fox16, lse), arriving = fox_fwd(p_fox, c_col, c_row, B, S, H, rider=_ride_gather_ici(packs_b))
    qkvn = gdn_prep_fwd(p_gqkv, gconv_i, B, S, H)
    (u_hat, w_t, t_inv), arrived = gdn_intra_fwd(qkvn, betar_u, gcr_u, B, S, H, rider=_ride_gather_d2d(arriving))
    g_up, g_rowed = own(arrived, packs_b)
    W_up = by_cols(g_up)
    W_up_g, W_up_v = W_up[:, :FF], W_up[:, FF:]
    W_bf, W_bg, W_out, W_down = (cat_rows(p) for p in _unpack_rows(g_rowed.reshape(N_CHIP, 2 * Rh, ROW), rowed_shapes))
    o_gdn, states, y_gdn = gdn_inter_fwd(qkvn, u_hat, w_t, gcr5, p_gz, gdn_norm, B, S, H)
    bf_ = matmul(o_fox16, W_bf, "nn", "branch_fox")
    bg_, y = matmul(y_gdn, W_bg, "nn", "branch_gdn", post=_post_merge(p_gates, bf_))
    h1, hn2 = matmul(y, W_out, "nn", "out_proj", add=x2, post=_post_rmsnorm(norm_ffn))
    up_g = matmul(hn2, W_up_g, "nn", "up_gate")
    up_v = matmul(hn2, W_up_v, "nn", "up_val")
    act = ffn_gate_fwd(up_g, up_v, fconv_g, fconv_v, B, S)
    dh2, dh2_16, loss_cols, d_norm_final = matmul(act, W_down, "nn", "down_proj", add=h1,
                                                  post=_post_loss(norm_final.reshape(1, D), tgt))
    loss_here = (0.5 / D) * jnp.sum(loss_cols)

    d_act = matmul(dh2_16, W_down, "nt", "d_act")
    dW_down = matmul(act, dh2_16, "tn", "dw_down")
    d_upg, d_upv, d_fconv_g, d_fconv_v = ffn_gate_bwd(up_g, up_v, fconv_g, fconv_v, d_act, B, S)
    d_hn2 = matmul(d_upg, W_up_g, "nt", "d_hn2_g")
    dh1, dh1_16, d_norm_ffn = matmul(d_upv, W_up_v, "nt", "d_hn2_v", add=d_hn2,
                                     post=_post_rmsnorm_bwd(h1, norm_ffn, dh2, True))
    dW_up_slabs = jnp.concatenate([matmul(hn2, d_upg, "tn", "dw_up_g", slab=(D // 2, 2 * FF // N_CHIP)),
                                   matmul(hn2, d_upv, "tn", "dw_up_v", slab=(D // 2, 2 * FF // N_CHIP))], axis=1)
    d_bf, d_bg, d_gates = matmul(dh1_16, W_out, "nt", "d_y", post=_post_merge_bwd(p_gates, bf_, bg_))
    dW_out = matmul(y, dh1_16, "tn", "dw_out")
    d_ofox = matmul(d_bf, W_bf, "nt", "d_ofox")
    dW_bf = matmul(o_fox16, d_bf, "tn", "dw_bf")
    d_ygdn = matmul(d_bg, W_bg, "nt", "d_ygdn")
    dW_bg = matmul(y_gdn, d_bg, "tn", "dw_bg")

    d_fconv = jnp.concatenate([d_fconv_g, d_fconv_v], axis=1)
    col_shard = lambda g, s: g[:, s * (g.shape[1] // N_CHIP):(s + 1) * (g.shape[1] // N_CHIP)]
    row_shard = lambda g, s: g[s * (g.shape[0] // N_CHIP):(s + 1) * (g.shape[0] // N_CHIP)]
    shard_items = lambda s: [row_shard(dW_bf, s), row_shard(dW_bg, s), row_shard(dW_out, s), row_shard(dW_down, s),
                             col_shard(d_fconv, s)]
    g_shapes = [a.shape for a in shard_items(0)]
    assert sum(_rows_of(math.prod(s)) for s in g_shapes) <= 2 * Rh
    to_slabs = lambda g: g.reshape(2, g.shape[0] // 2, N_CHIP, g.shape[1] // N_CHIP).transpose(0, 2, 1, 3)
    gpacks_b = [dW_up_slabs,
                jnp.stack([_pack_rows(shard_items(s), 2 * Rh, F32).reshape(2, Rh, ROW) for s in range(N_CHIP)], axis=1)]
    (d_pfox, d_ccol, d_crow), gots_b = fox_bwd(p_fox, c_col, c_row, o_fox, lse, d_ofox, B, S, H,
                                              rider=_ride_exchange(gpacks_b))
    sums_b = [add_halves(g, got, idx, "add_halves_b%d" % i) for i, (g, got) in enumerate(zip(gpacks_b, gots_b))]

    (dq_i, dk_i, d_uh, d_wt, dgcr_a, d_gz, d_gn_parts), got16_b = gdn_inter_bwd(
        qkvn, u_hat, w_t, gcr5, states, o_gdn, p_gz, gdn_norm, d_ygdn, B, S, H,
        rider=_ride_scatter([s16 for _, s16 in sums_b]))
    d_gdn_norm = jnp.sum(d_gn_parts[:, :, 0, :], axis=(0, 1))[None]
    mine_b = [add_chips(s32, g16, idx, "add_chips_b%d" % i) for i, ((s32, _), g16) in enumerate(zip(sums_b, got16_b))]
    d_qkvn, d_betar5, dgcr_b = gdn_intra_bwd(qkvn, betar_u, gcr_u, t_inv, d_uh, d_wt, dq_i, dk_i, B, S, H)
    d_pgqkv, d_gconv_i = gdn_prep_bwd(p_gqkv, gconv_i, d_qkvn, B, S, H)

    tokens = lambda a: a.reshape(B, H, S).transpose(0, 2, 1).reshape(T, H)
    d_gc = dgcr_a.reshape(B, H, S) + dgcr_b.reshape(B, H, S)
    d_sm = jnp.concatenate([tokens(d_ccol.reshape(B, H, S) + d_crow.reshape(B, H, S)), tokens(d_gc), tokens(d_betar5.reshape(B, H, S)),
                            jnp.zeros((T, 128 - 3 * H), F32)], axis=1)
    d_psmall, d_prm = small_bwd(p_small, prm, d_sm, B, S, H)

    dW_foxT = matmul(d_pfox, hn1, "tn", "dw_fox")
    dW_gqkvT = matmul(d_pgqkv, hn1, "tn", "dw_gqkv")
    dW_gzT = matmul(d_gz, hn1, "tn", "dw_gz")
    dW_gatesT = matmul(d_gates, hn1, "tn", "dw_gates")
    dW_smallT = matmul(d_psmall, hn1, "tn", "dw_small")
    dW_inT = jnp.concatenate([_deinterleave_head_rows(dW_foxT, H), dW_smallT[0:H], _deinterleave_head_rows(dW_gqkvT, H),
                              dW_smallT[H:3 * H], dW_gzT, dW_gatesT], axis=0)
    d_gconv = _deinterleave_heads(d_gconv_i, H)

    gpack_a = [dW_inT.reshape(N_CHIP, c_in, D)]
    d_hn1, gots_a = matmul(d_pfox, W_foxT, "nn", "d_hn1_fox", rider=_ride_exchange(gpack_a))
    sums_a = [add_halves(gpack_a[0], gots_a[0], idx, "add_halves_a")]
    d_hn1, landing_a = matmul(d_pgqkv, W_gqkvT, "nn", "d_hn1_gqkv", add=d_hn1,
                              rider=_ride_scatter([sums_a[0][1]], to=(0, 1)))
    d_hn1 = matmul(d_gz, W_gzT, "nn", "d_hn1_gz", add=d_hn1)
    d_hn1, got16_a = matmul(d_gates, W_gatesT, "nn", "d_hn1_gates", add=d_hn1,
                            rider=_ride_scatter([sums_a[0][1]], to=(2,), landing=landing_a))
    mine = [add_chips(sums_a[0][0], got16_a[0], idx, "add_chips_a")] + mine_b
    grad_x, d_norm_mix = matmul(d_psmall, W_smallT, "nn", "d_hn1_small", add=d_hn1,
                                post=_post_rmsnorm_bwd(x2, norm_mix, dh1, False))

    others = share_halves(mine)
    g_w_inT, g_up, g_rows = (jnp.concatenate([jnp.where(cidx == 0, h, o), jnp.where(cidx == 0, o, h)], axis=ax)
                             for h, o, ax in zip(mine, others, (1, 0, 0)))
    g_w_in = g_w_inT.T
    g_bf, g_bg, g_out, g_down, g_fconv = _unpack_rows(g_rows, g_shapes)

    small_items = [d_norm_mix, d_norm_ffn, d_norm_final, d_gdn_norm, d_prm, d_gconv, loss_here.reshape(1, 1)]
    small_shapes = [a.shape for a in small_items]
    sv = allreduce_small(_pack_rows(small_items, 0, F32, unit=8))
    g_norm_mix, g_norm_ffn, g_norm_final, g_gdn_norm, g_prm, g_gconv_all, loss = _unpack_rows(sv, small_shapes, unit=8)
    loss = loss[0, 0]
    g_norm_final = g_norm_final.reshape(D)
    g_fbias, g_dtb, g_alog = g_prm[0:1, 0:H], g_prm[0:1, H:2 * H], g_prm[1:2, H:2 * H]
    g_gconv = lax.dynamic_slice_in_dim(g_gconv_all, sidx * (3 * D // N_CHIP), 3 * D // N_CHIP, axis=1)

    names = ["norm_mix", "w_in", "fox_f_bias", "gdn_conv_w", "gdn_a_log", "gdn_dt_bias", "gdn_norm", "w_branch_fox",
             "w_branch_gdn", "w_out", "norm_ffn", "w_up", "ffn_conv_w", "w_down", "norm_final"]
    ws = [norm_mix, w_in, fox_f_bias, gdn_conv_w, gdn_a_log, gdn_dt_bias, gdn_norm, w_branch_fox, w_branch_gdn, w_out,
          norm_ffn, w_up, ffn_conv_w, w_down, norm_final]
    ms = [m_norm_mix, m_w_in, m_fox_f_bias, m_gdn_conv_w, m_gdn_a_log, m_gdn_dt_bias, m_gdn_norm, m_w_branch_fox,
          m_w_branch_gdn, m_w_out, m_norm_ffn, m_w_up, m_ffn_conv_w, m_w_down, m_norm_final]
    vs = [v_norm_mix, v_w_in, v_fox_f_bias, v_gdn_conv_w, v_gdn_a_log, v_gdn_dt_bias, v_gdn_norm, v_w_branch_fox,
          v_w_branch_gdn, v_w_out, v_norm_ffn, v_w_up, v_ffn_conv_w, v_w_down, v_norm_final]
    gs = [g_norm_mix, g_w_in, g_fbias, g_gconv, g_alog, g_dtb, g_gdn_norm, g_bf, g_bg, g_out, g_norm_ffn, g_up,
          g_fconv, g_down, g_norm_final]
    gs = [g.reshape(w.shape) for g, w in zip(gs, ws)]
    deltas, new_ms, new_vs = [], [], []
    for nm, w, g, m, v in zip(names, ws, gs, ms, vs):
        if w.ndim == 1:
            d, a, b = adamw(w.reshape(1, -1), g.reshape(1, -1), m.reshape(1, -1), v.reshape(1, -1), "adamw_" + nm)
            d, a, b = d.reshape(w.shape), a.reshape(w.shape), b.reshape(w.shape)
        elif nm == "w_in":
            d, a, b = (r.T[None] for r in adamw(w[0].T, g_w_inT, m[0].T, v[0].T, "adamw_" + nm))
        else:
            d, a, b = adamw(w, g, m, v, "adamw_" + nm)
        deltas.append(d)
        new_ms.append(a)
        new_vs.append(b)

    return (loss, grad_x.reshape(B, S, D), *gs, *deltas, *new_ms, *new_vs)
```

```python
import functools
import math

import jax
import jax.numpy as jnp
from jax import lax
from jax.experimental import pallas as pl
from jax.experimental.pallas import tpu as pltpu

F32 = jnp.float32
BF16 = jnp.bfloat16
HEAD = 128
CHUNK = 64
GDN_CONV = 4
FFN_CONV = 3
EPS = 1e-6
NEG = -1e30
ROW = 1024
ATT_TILE = 512
MM_WEIGHT_TILE_BYTES = 8 << 20
MM_TN_OPERAND_BYTES = 32 << 20
N_CHIP = 4
N_DEV = 8
MESH = pl.DeviceIdType.MESH
HI = lax.Precision.HIGH
EXACT = lax.Precision.HIGHEST

ADAM_LR, ADAM_B1, ADAM_B2, ADAM_EPS, ADAM_WD, ADAM_STEP = 0.001, 0.9, 0.999, 1e-08, 0.01, 10


def _tile(n, cap, unit=128):
    best = None
    t = unit
    while t <= min(n, cap):
        if n % t == 0:
            best = t
        t += unit
    return best if best is not None else n


def _params(*sem):
    return pltpu.CompilerParams(dimension_semantics=sem)


_NN = (((1,), (0,)), ((), ()))
_NT = (((1,), (1,)), ((), ()))
_TN = (((0,), (0,)), ((), ()))


def _dg(a, b, dims, hi):
    if hi:
        return lax.dot_general(a, b, dims, precision=HI, preferred_element_type=F32)
    return lax.dot_general(a.astype(BF16), b.astype(BF16), dims, preferred_element_type=F32)


class _RawOps:
    @staticmethod
    def nn(a, b, hi=False):
        return _dg(a, b, _NN, hi)

    @staticmethod
    def nt(a, b, hi=False):
        return _dg(a, b, _NT, hi)

    @staticmethod
    def tn(a, b, hi=False):
        return _dg(a, b, _TN, hi)


def _make_diff_ops():
    def build(hi):
        @jax.custom_vjp
        def nn(a, b):
            return _dg(a, b, _NN, hi)

        nn.defvjp(lambda a, b: (_dg(a, b, _NN, hi), (a, b)),
                  lambda r, g: (_dg(g, r[1], _NT, hi), _dg(r[0], g, _TN, hi)))

        @jax.custom_vjp
        def nt(a, b):
            return _dg(a, b, _NT, hi)

        nt.defvjp(lambda a, b: (_dg(a, b, _NT, hi), (a, b)),
                  lambda r, g: (_dg(g, r[1], _NN, hi), _dg(g, r[0], _TN, hi)))

        @jax.custom_vjp
        def tn(a, b):
            return _dg(a, b, _TN, hi)

        tn.defvjp(lambda a, b: (_dg(a, b, _TN, hi), (a, b)),
                  lambda r, g: (_dg(r[1], g, _NT, hi), _dg(r[0], g, _NN, hi)))
        return nn, nt, tn

    lo, hi_ = build(False), build(True)

    class _DiffOps:
        @staticmethod
        def nn(a, b, hi=False):
            return (hi_ if hi else lo)[0](a, b)

        @staticmethod
        def nt(a, b, hi=False):
            return (hi_ if hi else lo)[1](a, b)

        @staticmethod
        def tn(a, b, hi=False):
            return (hi_ if hi else lo)[2](a, b)

    return _DiffOps


_DiffOps = _make_diff_ops()


def _sigmoid(x):
    return 1.0 / (1.0 + jnp.exp(-x))


def _mm_tile(n, pref):
    if n % pref == 0:
        return pref
    if n % 1408 == 0:
        return 1408
    return _tile(n, pref)


class _Post:
    def __init__(self, fn, row_ins=(), vec_ins=(), row_outs=(), acc_outs=(), keep_main=True):
        self.fn, self.keep_main = fn, keep_main
        self.row_ins = [r if isinstance(r, tuple) else (r, r.shape[1], 0) for r in row_ins]
        self.vec_ins, self.row_outs, self.acc_outs = list(vec_ins), list(row_outs), list(acc_outs)


def matmul(a, b, mode, name, add=None, out_dtype=F32, post=None, rider=None, slab=None):
    if mode == "nn":
        (M, K), (K2, N) = a.shape, b.shape
    elif mode == "nt":
        (M, K), (N, K2) = a.shape, b.shape
    else:
        (K, M), (K2, N) = a.shape, b.shape
    assert K == K2, (name, a.shape, b.shape)
    tn = slab[1] if slab else _mm_tile(N, 1024)
    if mode == "tn":
        tm = slab[0] if slab else (M if M <= 1408 else _mm_tile(M, 1408))
        row_bytes = 2 * (tm * a.dtype.itemsize + tn * b.dtype.itemsize)
        tk = next((t for t in (4096, 2048) if K % t == 0 and t * row_bytes <= MM_TN_OPERAND_BYTES), _mm_tile(K, 1024))
    else:
        tk = K if K * tn * 2 <= MM_WEIGHT_TILE_BYTES else _mm_tile(K, 1024)
        tm = _mm_tile(M, 1024 if tk <= 2048 and post is None else 512)
    nk = K // tk
    assert post is None or (mode != "tn" and tn == N), name
    dims = {"nn": _NN, "nt": _NT, "tn": _TN}[mode]
    if mode == "tn":
        a_spec = pl.BlockSpec((tk, tm), lambda j, i, k: (k, i))
    else:
        a_spec = pl.BlockSpec((tm, tk), lambda j, i, k: (i, k))
    if mode == "nt":
        b_spec = pl.BlockSpec((tn, tk), lambda j, i, k: (j, k))
    else:
        b_spec = pl.BlockSpec((tk, tn), lambda j, i, k: (k, j))
    o_spec = pl.BlockSpec((tm, tn), lambda j, i, k: (i, j))
    has_add = add is not None
    keep_main = post is None or post.keep_main
    counts = [2 + has_add] + ([len(post.row_ins), len(post.vec_ins)] if post else [0, 0]) + [int(keep_main)]
    counts += ([len(post.row_outs), len(post.acc_outs)] if post else [0, 0]) + [int(nk > 1)]

    def body(*refs):
        parts, p = [], 0
        for cnt in counts:
            parts.append(refs[p:p + cnt])
            p += cnt
        core, row_ins, vec_ins, main, row_outs, acc_outs, acc = parts
        a_ref, b_ref = core[:2]
        prod = lax.dot_general(a_ref[...].astype(BF16), b_ref[...].astype(BF16), dims, preferred_element_type=F32)

        def finish(r):
            if has_add:
                r = r + core[2][...]
            if keep_main:
                main[0][...] = r.astype(out_dtype)
            if post is not None:
                @pl.when(pl.program_id(1) == 0)
                def _():
                    for ref in acc_outs:
                        ref[...] = jnp.zeros_like(ref)

                post.fn(r, row_ins, vec_ins, row_outs, acc_outs)

        if nk == 1:
            finish(prod)
            return
        acc_ref = acc[0]
        k = pl.program_id(2)

        @pl.when(k == 0)
        def _():
            acc_ref[...] = jnp.zeros_like(acc_ref)

        acc_ref[...] += prod

        @pl.when(k == nk - 1)
        def _():
            finish(acc_ref[...])

    in_specs = [a_spec, b_spec] + ([o_spec] if has_add else [])
    args = (a, b) + ((add,) if has_add else ())
    out_specs = [o_spec] if keep_main else []
    out_shape = [jax.ShapeDtypeStruct((M, N), out_dtype)] if keep_main else []
    if slab:
        out_specs = [pl.BlockSpec((None, None, tm, tn), lambda j, i, k: (i, j, 0, 0))]
        out_shape = [jax.ShapeDtypeStruct((M // tm, N // tn, tm, tn), out_dtype)]
    if post is not None:
        in_specs += [pl.BlockSpec((tm, cols), lambda j, i, k, cb=cb: (i, cb)) for _, cols, cb in post.row_ins]
        in_specs += [pl.BlockSpec((1, v.shape[1]), lambda j, i, k: (0, 0)) for v in post.vec_ins]
        args += tuple(r for r, _, _ in post.row_ins) + tuple(post.vec_ins)
        out_specs += [pl.BlockSpec((tm, cols), lambda j, i, k: (i, 0)) for cols, _ in post.row_outs]
        out_specs += [pl.BlockSpec((1, cols), lambda j, i, k: (0, 0)) for cols in post.acc_outs]
        out_shape += [jax.ShapeDtypeStruct((M, cols), dt) for cols, dt in post.row_outs]
        out_shape += [jax.ShapeDtypeStruct((1, cols), F32) for cols in post.acc_outs]
    rows_sem = "arbitrary" if post is not None and post.acc_outs else "parallel"
    res = _hosted_call(
        body, rider, name=name, grid=(N // tn, M // tm, nk), in_specs=in_specs, out_specs=out_specs, out_shape=out_shape,
        scratch_shapes=[pltpu.VMEM((tm, tn), F32)] if nk > 1 else [], semantics=("parallel", rows_sem, "arbitrary"),
    )(*args)
    if rider is not None:
        res, carried = res
        return (res[0] if post is None else res), carried
    return res[0] if post is None else res


def rmsnorm_fwd(x, g, name, rider=None):
    T, D = x.shape
    tm = _tile(T, 512, 8)

    def body(x_ref, g_ref, o_ref):
        xv = x_ref[...]
        r = lax.rsqrt(jnp.mean(xv * xv, axis=-1, keepdims=True) + EPS)
        o_ref[...] = (xv * r * g_ref[...]).astype(BF16)

    res = _hosted_call(
        body, rider, name=name, grid=(T // tm,),
        in_specs=[pl.BlockSpec((tm, D), lambda i: (i, 0)), pl.BlockSpec((1, D), lambda i: (0, 0))],
        out_specs=[pl.BlockSpec((tm, D), lambda i: (i, 0))], out_shape=[jax.ShapeDtypeStruct((T, D), BF16)],
        scratch_shapes=[], semantics=("parallel",),
    )(x, g)
    return res[0] if rider is None else (res[0][0], res[1])


def _post_rmsnorm(g):
    def fn(r, row_ins, vec_ins, row_outs, acc_outs):
        rs = lax.rsqrt(jnp.mean(r * r, axis=-1, keepdims=True) + EPS)
        row_outs[0][...] = (r * rs * vec_ins[0][...]).astype(BF16)

    return _Post(fn, vec_ins=[g], row_outs=[(g.shape[1], BF16)])


def _post_rmsnorm_bwd(x, g, dres, with_bf16):
    D = g.shape[1]

    def fn(dy, row_ins, vec_ins, row_outs, acc_outs):
        xv = row_ins[0][...]
        rs = lax.rsqrt(jnp.mean(xv * xv, axis=-1, keepdims=True) + EPS)
        xh = xv * rs
        acc_outs[0][...] += jnp.sum(dy * xh, axis=0, keepdims=True)
        dxh = dy * vec_ins[0][...]
        dx = row_ins[1][...] + rs * (dxh - xh * jnp.mean(dxh * xh, axis=-1, keepdims=True))
        row_outs[0][...] = dx
        if with_bf16:
            row_outs[1][...] = dx.astype(BF16)

    return _Post(fn, row_ins=[x, dres], vec_ins=[g], row_outs=[(D, F32)] + ([(D, BF16)] if with_bf16 else []),
                 acc_outs=[D], keep_main=False)


def _post_loss(g, target):
    D = g.shape[1]

    def fn(hv, row_ins, vec_ins, row_outs, acc_outs):
        rs = lax.rsqrt(jnp.mean(hv * hv, axis=-1, keepdims=True) + EPS)
        xh = hv * rs
        gv = vec_ins[0][...]
        err = xh * gv - row_ins[0][...]
        acc_outs[0][...] += jnp.sum(err * err, axis=0, keepdims=True)
        dy = err * (1.0 / D)
        acc_outs[1][...] += jnp.sum(dy * xh, axis=0, keepdims=True)
        dxh = dy * gv
        dh = rs * (dxh - xh * jnp.mean(dxh * xh, axis=-1, keepdims=True))
        row_outs[0][...] = dh
        row_outs[1][...] = dh.astype(BF16)

    return _Post(fn, row_ins=[target], vec_ins=[g], row_outs=[(D, F32), (D, BF16)], acc_outs=[D, D], keep_main=False)


def _shift_down(x, k):
    if k == 0:
        return x
    rows = lax.broadcasted_iota(jnp.int32, x.shape, 0)
    return jnp.where(rows >= k, pltpu.roll(x, k, 0), 0.0)


def _shift_up(x, k):
    if k == 0:
        return x
    s = x.shape[0]
    rows = lax.broadcasted_iota(jnp.int32, x.shape, 0)
    return jnp.where(rows < s - k, pltpu.roll(x, s - k, 0), 0.0)


def _conv_fwd(x, w_ref, kw, keep_shifted=False):
    shifted = [_shift_down(x, kw - 1 - i) for i in range(kw - 1)]
    y = x * w_ref[kw - 1:kw, :]
    for i in range(kw - 1):
        y = y + shifted[i] * w_ref[i:i + 1, :]
    return (y, shifted) if keep_shifted else y


def _conv_bwd(x, shifted, dy, w_ref, kw):
    dx = dy * w_ref[kw - 1:kw, :]
    dws = []
    for i in range(kw - 1):
        dx = dx + _shift_up(dy, kw - 1 - i) * w_ref[i:i + 1, :]
        dws.append(jnp.sum(dy * shifted[i], axis=0, keepdims=True))
    dws.append(jnp.sum(dy * x, axis=0, keepdims=True))
    return dx, dws


def ffn_gate_fwd(up_g, up_v, cw_g, cw_v, B, S):
    T, Fd = up_g.shape
    tc = _tile(Fd, 256)

    def body(g_ref, v_ref, wg_ref, wv_ref, o_ref):
        ug = _conv_fwd(g_ref[...], wg_ref, FFN_CONV)
        uv = _conv_fwd(v_ref[...], wv_ref, FFN_CONV)
        o_ref[...] = (ug * _sigmoid(ug) * uv).astype(BF16)

    blk = pl.BlockSpec((S, tc), lambda b, j: (b, j))
    wblk = pl.BlockSpec((FFN_CONV, tc), lambda b, j: (0, j))
    return pl.pallas_call(
        body, name="ffn_gate_fwd", grid=(B, Fd // tc), in_specs=[blk, blk, wblk, wblk], out_specs=blk,
        out_shape=jax.ShapeDtypeStruct((T, Fd), BF16), compiler_params=_params("parallel", "parallel"),
    )(up_g, up_v, cw_g, cw_v)


def ffn_gate_bwd(up_g, up_v, cw_g, cw_v, d_act, B, S):
    T, Fd = up_g.shape
    tc = _tile(Fd, 256)

    def body(g_ref, v_ref, wg_ref, wv_ref, da_ref, dg_ref, dv_ref, dwg_ref, dwv_ref):
        @pl.when(pl.program_id(1) == 0)
        def _():
            dwg_ref[...] = jnp.zeros_like(dwg_ref)
            dwv_ref[...] = jnp.zeros_like(dwv_ref)

        xg, xv = g_ref[...], v_ref[...]
        ug, sh_g = _conv_fwd(xg, wg_ref, FFN_CONV, keep_shifted=True)
        uv, sh_v = _conv_fwd(xv, wv_ref, FFN_CONV, keep_shifted=True)
        da = da_ref[...]
        sg = _sigmoid(ug)
        d_ug = da * uv * (sg + ug * sg * (1.0 - sg))
        d_uv = da * ug * sg
        dxg, dwg = _conv_bwd(xg, sh_g, d_ug, wg_ref, FFN_CONV)
        dxv, dwv = _conv_bwd(xv, sh_v, d_uv, wv_ref, FFN_CONV)
        dg_ref[...] = dxg.astype(BF16)
        dv_ref[...] = dxv.astype(BF16)
        for i in range(FFN_CONV):
            dwg_ref[i:i + 1, :] += dwg[i]
            dwv_ref[i:i + 1, :] += dwv[i]

    blk = pl.BlockSpec((S, tc), lambda j, b: (b, j))
    wblk = pl.BlockSpec((FFN_CONV, tc), lambda j, b: (0, j))
    return pl.pallas_call(
        body, name="ffn_gate_bwd", grid=(Fd // tc, B), in_specs=[blk, blk, wblk, wblk, blk],
        out_specs=[blk, blk, wblk, wblk],
        out_shape=[jax.ShapeDtypeStruct((T, Fd), BF16), jax.ShapeDtypeStruct((T, Fd), BF16),
                   jax.ShapeDtypeStruct((FFN_CONV, Fd), F32), jax.ShapeDtypeStruct((FFN_CONV, Fd), F32)],
        compiler_params=_params("parallel", "arbitrary"),
    )(up_g, up_v, cw_g, cw_v, d_act)


def _post_merge(p_gates, bf_):
    D = bf_.shape[1]

    def fn(bg, row_ins, vec_ins, row_outs, acc_outs):
        gf_ref, gg_ref, bf_ref = row_ins
        row_outs[0][...] = (_sigmoid(gf_ref[...]) * bf_ref[...] + _sigmoid(gg_ref[...]) * bg).astype(BF16)

    return _Post(fn, row_ins=[(p_gates, D, 0), (p_gates, D, 1), bf_], row_outs=[(D, BF16)])


def _post_merge_bwd(p_gates, bf_, bg_):
    D = bf_.shape[1]

    def fn(d, row_ins, vec_ins, row_outs, acc_outs):
        gf_ref, gg_ref, bf_ref, bg_ref = row_ins
        sf, sg = _sigmoid(gf_ref[...]), _sigmoid(gg_ref[...])
        row_outs[0][...] = (d * sf).astype(BF16)
        row_outs[1][...] = (d * sg).astype(BF16)
        row_outs[2][:, 0:D] = (d * bf_ref[...] * sf * (1.0 - sf)).astype(BF16)
        row_outs[2][:, D:2 * D] = (d * bg_ref[...] * sg * (1.0 - sg)).astype(BF16)

    return _Post(fn, row_ins=[(p_gates, D, 0), (p_gates, D, 1), bf_, bg_],
                 row_outs=[(D, BF16), (D, BF16), (2 * D, BF16)], keep_main=False)


ATT_HEADS = 2
ATT_HEADS_FWD = 4


def fox_fwd(p_fox, c_col, c_row, B, S, H, rider=None):
    T = B * S
    t = _tile(S, ATT_TILE)
    nq = S // t
    scale = HEAD ** -0.5
    hp = next((n for n in (ATT_HEADS_FWD, ATT_HEADS) if H % n == 0), 1)
    hs = range(hp)

    def body(*refs):
        qkv_refs, (cq_ref, cr_ref, o_ref, o16_ref, lse_ref) = refs[:3 * hp], refs[3 * hp:]
        i = pl.program_id(2)
        q = [qkv_refs[3 * hh][...] for hh in hs]
        row = lax.broadcasted_iota(jnp.int32, (t, t), 0)
        col = lax.broadcasted_iota(jnp.int32, (t, t), 1)

        def step(j, carry, diagonal):
            off = pl.multiple_of(j * t, t)
            k = [qkv_refs[3 * hh + 1][pl.ds(off, t), :] for hh in hs]
            v = [qkv_refs[3 * hh + 2][pl.ds(off, t), :] for hh in hs]
            s = [lax.dot_general(q[hh], k[hh], _NT, preferred_element_type=F32) * scale - cr_ref[hh, :, pl.ds(off, t)]
                 for hh in hs]
            if diagonal:
                s = [jnp.where(col <= row, s[hh], NEG) for hh in hs]
            m_new = [jnp.maximum(carry[hh][0], jnp.max(s[hh], axis=-1, keepdims=True)) for hh in hs]
            alpha = [jnp.exp(carry[hh][0] - m_new[hh]) for hh in hs]
            p = [jnp.exp(s[hh] - m_new[hh]) for hh in hs]
            l = [alpha[hh] * carry[hh][1] + jnp.sum(p[hh], axis=-1, keepdims=True) for hh in hs]
            acc = [alpha[hh] * carry[hh][2] + lax.dot_general(p[hh].astype(BF16), v[hh], _NN, preferred_element_type=F32)
                   for hh in hs]
            return tuple((m_new[hh], l[hh], acc[hh]) for hh in hs)

        start = (jnp.full((t, 1), NEG, F32), jnp.zeros((t, 1), F32), jnp.zeros((t, HEAD), F32))
        below = lax.fori_loop(0, i, functools.partial(step, diagonal=False), tuple(start for _ in hs))
        done = step(i, below, diagonal=True)
        for hh, (m, l, acc) in enumerate(done):
            cols = slice(hh * HEAD, (hh + 1) * HEAD)
            o = acc / l
            o_ref[:, cols] = o
            o16_ref[:, cols] = o.astype(BF16)
            lse_ref[hh] = cq_ref[hh] + m + jnp.log(l)

    qkv = []
    for hh in hs:
        qkv.append(pl.BlockSpec((t, HEAD), lambda b, g, i, hh=hh: (b * nq + i, 3 * (hp * g + hh))))
        qkv.append(pl.BlockSpec((S, HEAD), lambda b, g, i, hh=hh: (b, 3 * (hp * g + hh) + 1)))
        qkv.append(pl.BlockSpec((S, HEAD), lambda b, g, i, hh=hh: (b, 3 * (hp * g + hh) + 2)))
    heads = pl.BlockSpec((t, hp * HEAD), lambda b, g, i: (b * nq + i, g))
    return _hosted_call(
        body, rider, name="fox_fwd", grid=(B, H // hp, nq),
        in_specs=qkv + [pl.BlockSpec((None, hp, t, 1), lambda b, g, i: (b, g, i, 0)),
                        pl.BlockSpec((None, hp, 1, S), lambda b, g, i: (b, g, 0, 0))],
        out_specs=[heads, heads, pl.BlockSpec((None, hp, t, 1), lambda b, g, i: (b, g, i, 0))],
        out_shape=[jax.ShapeDtypeStruct((T, H * HEAD), F32), jax.ShapeDtypeStruct((T, H * HEAD), BF16),
                   jax.ShapeDtypeStruct((B, H, S, 1), F32)],
        scratch_shapes=[], semantics=("parallel", "parallel", "arbitrary"),
    )(*([p_fox] * (3 * hp)), c_col, c_row)


def fox_bwd(p_fox, c_col, c_row, o, lse, do, B, S, H, rider=None):
    T = B * S
    t = _tile(S, ATT_TILE)
    n = S // t
    scale = HEAD ** -0.5
    hp = ATT_HEADS if H % ATT_HEADS == 0 else 1
    hs = range(hp)

    def body(*refs):
        qkv_refs = refs[:3 * hp]
        cq_ref, cr_ref, o_ref, lse_ref, do_ref, dqkv_ref, dcq_ref, dcr_ref, dq_acc, delta_s, lse_s = refs[3 * hp:]
        row = lax.broadcasted_iota(jnp.int32, (t, t), 0)
        col = lax.broadcasted_iota(jnp.int32, (t, t), 1)
        cols = [slice(hh * HEAD, (hh + 1) * HEAD) for hh in hs]

        def prep(i, c):
            rows = pl.ds(pl.multiple_of(i * t, t), t)
            for hh in hs:
                delta_s[hh, rows, :] = jnp.sum(do_ref[rows, cols[hh]] * o_ref[rows, cols[hh]], axis=-1, keepdims=True)
                lse_s[hh, rows, :] = lse_ref[hh, rows, :] - cq_ref[hh, rows, :]
                dq_acc[hh, rows, :] = jnp.zeros((t, HEAD), F32)
                dcq_ref[hh, rows, :] = jnp.zeros((t, 1), F32)
            return c

        lax.fori_loop(0, n, prep, 0)

        def kv_step(j, c):
            joff = pl.multiple_of(j * t, t)
            k = [qkv_refs[3 * hh + 1][pl.ds(joff, t), :] for hh in hs]
            v = [qkv_refs[3 * hh + 2][pl.ds(joff, t), :] for hh in hs]
            crj = [cr_ref[hh, :, pl.ds(joff, t)] for hh in hs]

            def q_step(i, carry, diagonal):
                rows = pl.ds(pl.multiple_of(i * t, t), t)
                q = [qkv_refs[3 * hh][rows, :] for hh in hs]
                dob = [do_ref[rows, cols[hh]].astype(BF16) for hh in hs]
                s = [lax.dot_general(q[hh], k[hh], _NT, preferred_element_type=F32) * scale - crj[hh] for hh in hs]
                dp = [lax.dot_general(dob[hh], v[hh], _NT, preferred_element_type=F32) for hh in hs]
                if diagonal:
                    s = [jnp.where(col <= row, s[hh], NEG) for hh in hs]
                p = [jnp.exp(s[hh] - lse_s[hh, rows, :]) for hh in hs]
                ds = [p[hh] * (dp[hh] - delta_s[hh, rows, :]) for hh in hs]
                dsb = [ds[hh].astype(BF16) for hh in hs]
                dv = [carry[hh][1] + lax.dot_general(p[hh].astype(BF16), dob[hh], _TN, preferred_element_type=F32)
                      for hh in hs]
                dk = [carry[hh][0] + lax.dot_general(dsb[hh], q[hh], _TN, preferred_element_type=F32) for hh in hs]
                for hh in hs:
                    dq_acc[hh, rows, :] += lax.dot_general(dsb[hh], k[hh], _NN, preferred_element_type=F32) * scale
                    dcq_ref[hh, rows, :] += jnp.sum(ds[hh], axis=-1, keepdims=True)
                dc = [carry[hh][2] + jnp.sum(ds[hh], axis=0, keepdims=True) for hh in hs]
                return tuple((dk[hh], dv[hh], dc[hh]) for hh in hs)

            z = jnp.zeros((t, HEAD), F32)
            zero = tuple((z, z, jnp.zeros((1, t), F32)) for _ in hs)
            on_diagonal = q_step(j, zero, diagonal=True)
            done = lax.fori_loop(j + 1, n, functools.partial(q_step, diagonal=False), on_diagonal)
            for hh, (dk, dv, dc) in enumerate(done):
                base = 3 * HEAD * hh
                dqkv_ref[pl.ds(joff, t), base + HEAD:base + 2 * HEAD] = (dk * scale).astype(BF16)
                dqkv_ref[pl.ds(joff, t), base + 2 * HEAD:base + 3 * HEAD] = dv.astype(BF16)
                dcr_ref[hh, :, pl.ds(joff, t)] = -dc
            return c

        lax.fori_loop(0, n, kv_step, 0)
        for hh in hs:
            dqkv_ref[:, 3 * HEAD * hh:3 * HEAD * hh + HEAD] = dq_acc[hh].astype(BF16)

    qkv = []
    for hh in hs:
        for part in range(3):
            qkv.append(pl.BlockSpec((S, HEAD), lambda b, g, hh=hh, part=part: (b, 3 * (hp * g + hh) + part)))
    col_spec = pl.BlockSpec((None, hp, S, 1), lambda b, g: (b, g, 0, 0))
    row_spec = pl.BlockSpec((None, hp, 1, S), lambda b, g: (b, g, 0, 0))
    heads = pl.BlockSpec((S, hp * HEAD), lambda b, g: (b, g))
    return _hosted_call(
        body, rider, name="fox_bwd", grid=(B, H // hp),
        in_specs=qkv + [col_spec, row_spec, heads, col_spec, heads],
        out_specs=[pl.BlockSpec((S, 3 * hp * HEAD), lambda b, g: (b, g)), col_spec, row_spec],
        out_shape=[jax.ShapeDtypeStruct((T, 3 * H * HEAD), BF16), jax.ShapeDtypeStruct((B, H, S, 1), F32),
                   jax.ShapeDtypeStruct((B, H, 1, S), F32)],
        scratch_shapes=[pltpu.VMEM((hp, S, HEAD), F32), pltpu.VMEM((hp, S, 1), F32), pltpu.VMEM((hp, S, 1), F32)],
        semantics=("parallel", "parallel"),
    )(*([p_fox] * (3 * hp)), c_col, c_row, o, lse, do)


def _small_fn(x, b0, b1, H):
    S = x.shape[0]
    lane = lax.broadcasted_iota(jnp.int32, x.shape, 1)
    z = x + b0
    tail = jnp.log1p(jnp.exp(-jnp.abs(z)))
    softplus = jnp.maximum(z, 0.0) + tail
    logsig = -(jnp.maximum(-z, 0.0) + tail)
    g = -jnp.exp(b1) * softplus
    pre = jnp.where(lane < H, logsig, jnp.where(lane < 2 * H, g, 0.0))
    bl = _tile(S, 256, CHUNK)
    r = lax.broadcasted_iota(jnp.int32, (bl, bl), 0)
    c = lax.broadcasted_iota(jnp.int32, (bl, bl), 1)
    tri = (r >= c).astype(F32)
    tri_chunk = jnp.where((r >= c) & (jnp.right_shift(r, 6) == jnp.right_shift(c, 6)), 1.0, 0.0)
    carry = jnp.zeros((1, x.shape[1]), F32)
    parts = []
    for i in range(S // bl):
        blk = pre[i * bl:(i + 1) * bl, :]
        full = lax.dot_general(tri, blk, _NN, precision=EXACT, preferred_element_type=F32) + carry
        chunked = lax.dot_general(tri_chunk, blk, _NN, precision=EXACT, preferred_element_type=F32)
        parts.append(jnp.where(lane[:bl] < H, full, chunked))
        carry = carry + jnp.sum(blk, axis=0, keepdims=True)
    cum = parts[0] if len(parts) == 1 else jnp.concatenate(parts, axis=0)
    return jnp.where(lane < 2 * H, cum, jnp.where(lane < 3 * H, _sigmoid(x), 0.0))


def small_fwd(p_small, prm, B, S, H):
    T = B * S

    def body(x_ref, p_ref, o_ref):
        o_ref[...] = _small_fn(x_ref[...], p_ref[0:1, :], p_ref[1:2, :], H)

    blk = pl.BlockSpec((S, 128), lambda b: (b, 0))
    return pl.pallas_call(
        body, name="small_fwd", grid=(B,), in_specs=[blk, pl.BlockSpec((8, 128), lambda b: (0, 0))], out_specs=blk,
        out_shape=jax.ShapeDtypeStruct((T, 128), F32), compiler_params=_params("parallel"),
    )(p_small, prm)


def small_bwd(p_small, prm, d_out, B, S, H):
    T = B * S

    def body(x_ref, p_ref, d_ref, dx_ref, dp_ref):
        @pl.when(pl.program_id(0) == 0)
        def _():
            dp_ref[...] = jnp.zeros_like(dp_ref)

        _, vjp = jax.vjp(functools.partial(_small_fn, H=H), x_ref[...], p_ref[0:1, :], p_ref[1:2, :])
        dx, db0, db1 = vjp(d_ref[...])
        dx_ref[...] = dx.astype(BF16)
        dp_ref[0:1, :] += db0
        dp_ref[1:2, :] += db1

    blk = pl.BlockSpec((S, 128), lambda b: (b, 0))
    pblk = pl.BlockSpec((8, 128), lambda b: (0, 0))
    return pl.pallas_call(
        body, name="small_bwd", grid=(B,), in_specs=[blk, pblk, blk], out_specs=[blk, pblk],
        out_shape=[jax.ShapeDtypeStruct((T, 128), BF16), jax.ShapeDtypeStruct((8, 128), F32)],
        compiler_params=_params("arbitrary"),
    )(p_small, prm, d_out)


def gdn_prep_fwd(p_gqkv, cw, B, S, H):
    T = B * S

    def body(x_ref, w_ref, o_ref):
        for part in range(3):
            cols = slice(part * HEAD, (part + 1) * HEAD)
            y = _conv_fwd(x_ref[:, cols], w_ref.at[:, cols], GDN_CONV)
            a = y * _sigmoid(y)
            if part < 2:
                a = a * lax.rsqrt(jnp.sum(a * a, axis=-1, keepdims=True) + EPS)
            o_ref[:, cols] = a

    blk = pl.BlockSpec((S, 3 * HEAD), lambda b, h: (b, h))
    wblk = pl.BlockSpec((GDN_CONV, 3 * HEAD), lambda b, h: (0, h))
    return pl.pallas_call(
        body, name="gdn_prep_fwd", grid=(B, H), in_specs=[blk, wblk], out_specs=blk,
        out_shape=jax.ShapeDtypeStruct((T, 3 * H * HEAD), F32), compiler_params=_params("parallel", "parallel"),
    )(p_gqkv, cw)


def gdn_prep_bwd(p_gqkv, cw, d_out, B, S, H):
    T = B * S

    def body(x_ref, w_ref, d_ref, dx_ref, dw_ref):
        @pl.when(pl.program_id(1) == 0)
        def _():
            dw_ref[...] = jnp.zeros_like(dw_ref)

        x = x_ref[...]
        y, shifted = _conv_fwd(x, w_ref, GDN_CONV, keep_shifted=True)
        sg = _sigmoid(y)
        a = y * sg
        rs = lax.rsqrt(jnp.sum(a * a, axis=-1, keepdims=True) + EPS)
        d = d_ref[...]
        out = a * rs
        da_qk = rs * (d - out * jnp.sum(d * out, axis=-1, keepdims=True))
        is_qk = (pl.program_id(0) % 3) < 2
        da = jnp.where(is_qk, da_qk, d)
        dy = da * (sg + y * sg * (1.0 - sg))
        dx, dws = _conv_bwd(x, shifted, dy, w_ref, GDN_CONV)
        dx_ref[...] = dx.astype(BF16)
        for i in range(GDN_CONV):
            dw_ref[i:i + 1, :] += dws[i]

    blk = pl.BlockSpec((S, HEAD), lambda n, b: (b, n))
    wblk = pl.BlockSpec((GDN_CONV, HEAD), lambda n, b: (0, n))
    return pl.pallas_call(
        body, name="gdn_prep_bwd", grid=(3 * H, B), in_specs=[blk, wblk, blk], out_specs=[blk, wblk],
        out_shape=[jax.ShapeDtypeStruct((T, 3 * H * HEAD), BF16), jax.ShapeDtypeStruct((GDN_CONV, 3 * H * HEAD), F32)],
        compiler_params=_params("parallel", "arbitrary"),
    )(p_gqkv, cw, d_out)


@jax.custom_vjp
def _given_inverse(a, t):
    return t


def _given_inverse_fwd(a, t):
    return t, t


def _given_inverse_bwd(t, g):
    x = _dg(t, g, _TN, True)
    return -_dg(x, t, _NT, True), jnp.zeros_like(t)


_given_inverse.defvjp(_given_inverse_fwd, _given_inverse_bwd)


def _to_col(row):
    n = row.shape[1]
    r = lax.broadcasted_iota(jnp.int32, (n, n), 0)
    c = lax.broadcasted_iota(jnp.int32, (n, n), 1)
    return jnp.sum(jnp.where(r == c, row, 0.0), axis=1, keepdims=True)


def _intra_fn(k, v, beta_r, gcr, ops, t_known=None):
    n = len(k)
    m = k[0].shape[0]
    r = lax.broadcasted_iota(jnp.int32, (m, m), 0)
    c = lax.broadcasted_iota(jnp.int32, (m, m), 1)
    below = (r > c) & (jnp.right_shift(r, 6) == jnp.right_shift(c, 6))
    beta = [_to_col(beta_r[i]) for i in range(n)]
    gcc = [_to_col(gcr[i]) for i in range(n)]
    decay = [jnp.exp(jnp.where(below, gcc[i] - gcr[i], NEG)) for i in range(n)]
    kb = [k[i] * beta[i] for i in range(n)]
    a = [ops.nt(kb[i], k[i]) * decay[i] for i in range(n)]
    if t_known is None:
        p = [-a[i] for i in range(n)]
        tm = [jnp.where(r == c, 1.0, 0.0) + p[i] for i in range(n)]
        for _ in range(5):
            p = [ops.nn(p[i], p[i], hi=True) for i in range(n)]
            tm = [tm[i] + ops.nn(tm[i], p[i], hi=True) for i in range(n)]
    else:
        tm = [_given_inverse(a[i], t_known[i]) for i in range(n)]
    both = [ops.nn(tm[i], jnp.concatenate([v[i] * beta[i], kb[i] * jnp.exp(gcc[i])], axis=1), hi=True) for i in range(n)]
    u_hat = [both[i][:, :HEAD] for i in range(n)]
    w = [both[i][:, HEAD:] for i in range(n)]
    return tuple(u_hat), tuple(w), tuple(tm)


INTRA_NB = 32
PAIR = 1


def gdn_intra_fwd(qkvn, betar5, gcr5, B, S, H, rider=None):
    T = B * S
    UNIT = PAIR * CHUNK
    N = S // UNIT
    nb = min(INTRA_NB // PAIR, N)
    rows = nb * UNIT
    ns = N // nb

    def body(k_ref, v_ref, b_ref, gr_ref, uh_ref, w_ref, t_ref):
        sls = [slice(ci * UNIT, (ci + 1) * UNIT) for ci in range(nb)]
        u_hat, w, tm = _intra_fn(tuple(k_ref[sl, :] for sl in sls), tuple(v_ref[sl, :] for sl in sls),
                                 tuple(b_ref[ci] for ci in range(nb)), tuple(gr_ref[ci] for ci in range(nb)), _RawOps)
        for ci, sl in enumerate(sls):
            uh_ref[sl, :] = u_hat[ci]
            w_ref[sl, :] = w[ci]
            t_ref[ci] = tm[ci]

    rowspec = pl.BlockSpec((None, None, nb, 1, UNIT), lambda b, h, i: (b, h, i, 0, 0))
    sqspec = pl.BlockSpec((None, None, nb, UNIT, UNIT), lambda b, h, i: (b, h, i, 0, 0))
    out = pl.BlockSpec((rows, HEAD), lambda b, h, i: (b * ns + i, h))
    return _hosted_call(
        body, rider, name="gdn_intra_fwd", grid=(B, H, ns),
        in_specs=[pl.BlockSpec((rows, HEAD), lambda b, h, i: (b * ns + i, 3 * h + 1)),
                  pl.BlockSpec((rows, HEAD), lambda b, h, i: (b * ns + i, 3 * h + 2)),
                  rowspec, rowspec],
        out_specs=[out, out, sqspec],
        out_shape=[jax.ShapeDtypeStruct((T, H * HEAD), F32), jax.ShapeDtypeStruct((T, H * HEAD), F32),
                   jax.ShapeDtypeStruct((B, H, N, UNIT, UNIT), F32)],
        scratch_shapes=[], semantics=("parallel", "parallel", "parallel"),
    )(qkvn, qkvn, betar5, gcr5)


def gdn_intra_bwd(qkvn, betar5, gcr5, t_inv, d_uh, d_w, dq_in, dk_in, B, S, H):
    T = B * S
    UNIT = PAIR * CHUNK
    N = S // UNIT
    nb = min(INTRA_NB // PAIR, N)
    rows = nb * UNIT
    ns = N // nb

    def body(k_ref, v_ref, b_ref, gr_ref, t_ref, duh_ref, dw_ref, dq_ref, dk_ref, o_ref, db_ref, dgr_ref):
        sls = [slice(ci * UNIT, (ci + 1) * UNIT) for ci in range(nb)]
        chunks = range(nb)
        _, vjp = jax.vjp(
            functools.partial(_intra_fn, ops=_DiffOps, t_known=tuple(t_ref[ci] for ci in chunks)),
            tuple(k_ref[sl, :] for sl in sls), tuple(v_ref[sl, :] for sl in sls), tuple(b_ref[ci] for ci in chunks),
            tuple(gr_ref[ci] for ci in chunks))
        zero = jnp.zeros((UNIT, UNIT), F32)
        dk, dv, db, dgr = vjp((tuple(duh_ref[sl, :] for sl in sls), tuple(dw_ref[sl, :] for sl in sls),
                               tuple(zero for _ in chunks)))
        for ci, sl in enumerate(sls):
            o_ref[sl, 0:HEAD] = dq_ref[sl, :]
            o_ref[sl, HEAD:2 * HEAD] = dk[ci] + dk_ref[sl, :]
            o_ref[sl, 2 * HEAD:3 * HEAD] = dv[ci]
            db_ref[ci] = db[ci]
            dgr_ref[ci] = dgr[ci]

    rowspec = pl.BlockSpec((None, None, nb, 1, UNIT), lambda b, h, i: (b, h, i, 0, 0))
    sqspec = pl.BlockSpec((None, None, nb, UNIT, UNIT), lambda b, h, i: (b, h, i, 0, 0))
    head = pl.BlockSpec((rows, HEAD), lambda b, h, i: (b * ns + i, h))
    return pl.pallas_call(
        body, name="gdn_intra_bwd", grid=(B, H, ns),
        in_specs=[pl.BlockSpec((rows, HEAD), lambda b, h, i: (b * ns + i, 3 * h + 1)),
                  pl.BlockSpec((rows, HEAD), lambda b, h, i: (b * ns + i, 3 * h + 2)),
                  rowspec, rowspec, sqspec, head, head, head, head],
        out_specs=[pl.BlockSpec((rows, 3 * HEAD), lambda b, h, i: (b * ns + i, h)), rowspec, rowspec],
        out_shape=[jax.ShapeDtypeStruct((T, 3 * H * HEAD), F32),
                   jax.ShapeDtypeStruct((B, H, N, 1, UNIT), F32), jax.ShapeDtypeStruct((B, H, N, 1, UNIT), F32)],
        compiler_params=_params("parallel", "parallel", "parallel"),
    )(qkvn, qkvn, betar5, gcr5, t_inv, d_uh, d_w, dq_in, dk_in)


def _inter_fn(q, k, u_hat, w, gcr, state, ops):
    n = len(q)
    r = lax.broadcasted_iota(jnp.int32, (CHUNK, CHUNK), 0)
    c = lax.broadcasted_iota(jnp.int32, (CHUNK, CHUNK), 1)
    last = lax.broadcasted_iota(jnp.int32, (1, CHUNK), 1) == CHUNK - 1
    gcc = [_to_col(gcr[i]) for i in range(n)]
    gl = [jnp.sum(jnp.where(last, gcr[i], 0.0), axis=1, keepdims=True) for i in range(n)]
    decay = [jnp.exp(jnp.where(r >= c, gcc[i] - gcr[i], NEG)) for i in range(n)]
    qs = [q[i] * (HEAD ** -0.5) for i in range(n)]
    ws = [ops.nn(w[i], state[i]) for i in range(n)]
    qst = [ops.nn(qs[i] * jnp.exp(gcc[i]), state[i]) for i in range(n)]
    attn = [ops.nt(qs[i], k[i]) * decay[i] for i in range(n)]
    u = [u_hat[i] - ws[i] for i in range(n)]
    o = [qst[i] + ops.nn(attn[i], u[i]) for i in range(n)]
    kdu = [ops.tn(k[i] * jnp.exp(gl[i] - gcc[i]), u[i]) for i in range(n)]
    new_state = [state[i] * jnp.exp(gl[i]) + kdu[i] for i in range(n)]
    return tuple(o), tuple(new_state)


INTER_HEADS = 8
INTER_ROWS = 512
INTER_ROWS_BWD = 256


def _inter_heads(H):
    return INTER_HEADS if H % INTER_HEADS == 0 else (4 if H % 4 == 0 else 1)


def _inter_specs(ts, ns, hp, backward):
    at = (lambda s: ns - 1 - s) if backward else (lambda s: s)
    nc = ts // CHUNK
    qk = []
    for hh in range(hp):
        qk.append(pl.BlockSpec((ts, HEAD), lambda b, g, s, hh=hh: (b * ns + at(s), 3 * (hp * g + hh))))
        qk.append(pl.BlockSpec((ts, HEAD), lambda b, g, s, hh=hh: (b * ns + at(s), 3 * (hp * g + hh) + 1)))
    heads = pl.BlockSpec((ts, hp * HEAD), lambda b, g, s: (b * ns + at(s), g))
    rowspec = pl.BlockSpec((None, hp, nc, 1, CHUNK), lambda b, g, s: (b, g, at(s), 0, 0))
    stspec = pl.BlockSpec((None, hp, nc, HEAD, HEAD), lambda b, g, s: (b, g, at(s), 0, 0))
    return qk, heads, rowspec, stspec


def gdn_inter_fwd(qkvn, u_hat, w, gcr5, p_gz, gnorm, B, S, H):
    T = B * S
    N = S // CHUNK
    hp = _inter_heads(H)
    hs = range(hp)
    ts = _tile(S, INTER_ROWS, CHUNK)
    ns, nc = S // ts, ts // CHUNK

    def body(*refs):
        qk_refs, (uh_ref, w_ref, gr_ref, z_ref, gn_ref, o_ref, st_ref, y_ref, s_scr) = refs[:2 * hp], refs[2 * hp:]

        @pl.when(pl.program_id(2) == 0)
        def _():
            s_scr[...] = jnp.zeros_like(s_scr)

        gn = gn_ref[...]

        def step(n, c):
            rows = pl.ds(pl.multiple_of(n * CHUNK, CHUNK), CHUNK)
            st = tuple(s_scr[hh] for hh in hs)
            for hh in hs:
                st_ref[hh, n] = st[hh]
            o, new = _inter_fn(tuple(qk_refs[2 * hh][rows, :] for hh in hs), tuple(qk_refs[2 * hh + 1][rows, :] for hh in hs),
                               tuple(uh_ref[rows, hh * HEAD:(hh + 1) * HEAD] for hh in hs),
                               tuple(w_ref[rows, hh * HEAD:(hh + 1) * HEAD] for hh in hs),
                               tuple(gr_ref[hh, n] for hh in hs), st, _RawOps)
            for hh in hs:
                cols = slice(hh * HEAD, (hh + 1) * HEAD)
                o_ref[rows, cols] = o[hh]
                s_scr[hh] = new[hh]
                z = z_ref[rows, cols]
                r = lax.rsqrt(jnp.mean(o[hh] * o[hh], axis=-1, keepdims=True) + EPS)
                y_ref[rows, cols] = (o[hh] * r * gn * z * _sigmoid(z)).astype(BF16)
            return c

        lax.fori_loop(0, nc, step, 0)

    qk, heads, rowspec, stspec = _inter_specs(ts, ns, hp, backward=False)
    return pl.pallas_call(
        body, name="gdn_inter_fwd", grid=(B, H // hp, ns),
        in_specs=qk + [heads, heads, rowspec, heads, pl.BlockSpec((1, HEAD), lambda b, g, s: (0, 0))],
        out_specs=[heads, stspec, heads],
        out_shape=[jax.ShapeDtypeStruct((T, H * HEAD), F32), jax.ShapeDtypeStruct((B, H, N, HEAD, HEAD), F32),
                   jax.ShapeDtypeStruct((T, H * HEAD), BF16)],
        scratch_shapes=[pltpu.VMEM((hp, HEAD, HEAD), F32)],
        compiler_params=_params("parallel", "parallel", "arbitrary"),
    )(*([qkvn] * (2 * hp)), u_hat, w, gcr5, p_gz, gnorm)


def gdn_inter_bwd(qkvn, u_hat, w, gcr5, states, o, p_gz, gnorm, d_y, B, S, H, rider=None):
    T = B * S
    N = S // CHUNK
    hp = _inter_heads(H)
    hs = range(hp)
    ts = _tile(S, INTER_ROWS_BWD, CHUNK)
    ns, nc = S // ts, ts // CHUNK

    def body(*refs):
        qk_refs = refs[:2 * hp]
        (uh_ref, w_ref, gr_ref, st_ref, o_ref, z_ref, gn_ref, dy_ref,
         dq_ref, dk_ref, duh_ref, dw_ref, dgr_ref, dz_ref, dgn_ref, ds_scr) = refs[2 * hp:]

        @pl.when(pl.program_id(2) == 0)
        def _():
            ds_scr[...] = jnp.zeros_like(ds_scr)
            dgn_ref[...] = jnp.zeros_like(dgn_ref)

        cols = [slice(hh * HEAD, (hh + 1) * HEAD) for hh in hs]
        gn = gn_ref[...]

        def through_norm(rows, hh):
            ov, z, d = o_ref[rows, cols[hh]], z_ref[rows, cols[hh]], dy_ref[rows, cols[hh]]
            r = lax.rsqrt(jnp.mean(ov * ov, axis=-1, keepdims=True) + EPS)
            xh = ov * r
            sg = _sigmoid(z)
            d_n = d * (z * sg)
            dz_ref[rows, cols[hh]] = (d * xh * gn * (sg + z * sg * (1.0 - sg))).astype(BF16)
            dgn_ref[0:1, :] += jnp.sum(d_n * xh, axis=0, keepdims=True)
            dxh = d_n * gn
            return r * (dxh - xh * jnp.mean(dxh * xh, axis=-1, keepdims=True))

        def step(i, c):
            n = nc - 1 - i
            rows = pl.ds(pl.multiple_of(n * CHUNK, CHUNK), CHUNK)
            _, vjp = jax.vjp(functools.partial(_inter_fn, ops=_DiffOps),
                             tuple(qk_refs[2 * hh][rows, :] for hh in hs), tuple(qk_refs[2 * hh + 1][rows, :] for hh in hs),
                             tuple(uh_ref[rows, cols[hh]] for hh in hs), tuple(w_ref[rows, cols[hh]] for hh in hs),
                             tuple(gr_ref[hh, n] for hh in hs), tuple(st_ref[hh, n] for hh in hs))
            dq, dk, duh, dw, dgr, ds = vjp((tuple(through_norm(rows, hh) for hh in hs), tuple(ds_scr[hh] for hh in hs)))
            for hh in hs:
                dq_ref[rows, cols[hh]] = dq[hh]
                dk_ref[rows, cols[hh]] = dk[hh]
                duh_ref[rows, cols[hh]] = duh[hh]
                dw_ref[rows, cols[hh]] = dw[hh]
                dgr_ref[hh, n] = dgr[hh]
                ds_scr[hh] = ds[hh]
            return c

        lax.fori_loop(0, nc, step, 0)

    qk, heads, rowspec, stspec = _inter_specs(ts, ns, hp, backward=True)
    hshape = jax.ShapeDtypeStruct((T, H * HEAD), F32)
    return _hosted_call(
        body, rider, name="gdn_inter_bwd", grid=(B, H // hp, ns),
        in_specs=qk + [heads, heads, rowspec, stspec, heads, heads, pl.BlockSpec((1, HEAD), lambda b, g, s: (0, 0)), heads],
        out_specs=[heads, heads, heads, heads, rowspec, heads,
                   pl.BlockSpec((None, None, 8, HEAD), lambda b, g, s: (b, g, 0, 0))],
        out_shape=[hshape, hshape, hshape, hshape, jax.ShapeDtypeStruct((B, H, N, 1, CHUNK), F32),
                   jax.ShapeDtypeStruct((T, H * HEAD), BF16), jax.ShapeDtypeStruct((B, H // hp, 8, HEAD), F32)],
        scratch_shapes=[pltpu.VMEM((hp, HEAD, HEAD), F32)], semantics=("parallel", "parallel", "arbitrary"),
    )(*([qkvn] * (2 * hp)), u_hat, w, gcr5, states, o, p_gz, gnorm, d_y)


def adamw(w, g, m, v, name):
    shape = w.shape
    lead = (None,) * (w.ndim - 2)
    zeros = (0,) * (w.ndim - 2)
    R, C = shape[-2:]
    g2 = g.reshape(R, C)
    tr, tc = _tile(R, 128, 8), C
    if tr % 8 and R > 8:
        tr, tc = R, _tile(C, 128)

    def body(w_ref, g_ref, m_ref, v_ref, d_ref, nm_ref, nv_ref):
        gv = g_ref[...]
        nm = ADAM_B1 * m_ref[...] + (1.0 - ADAM_B1) * gv
        nv = ADAM_B2 * v_ref[...] + (1.0 - ADAM_B2) * (gv * gv)
        m_hat = nm / (1.0 - ADAM_B1 ** ADAM_STEP)
        v_hat = nv / (1.0 - ADAM_B2 ** ADAM_STEP)
        d_ref[...] = -ADAM_LR * (m_hat / (jnp.sqrt(v_hat) + ADAM_EPS) + ADAM_WD * w_ref[...])
        nm_ref[...] = nm
        nv_ref[...] = nv

    blk = pl.BlockSpec(lead + (tr, tc), lambda i, j: zeros + (i, j))
    gblk = pl.BlockSpec((tr, tc), lambda i, j: (i, j))
    sh = jax.ShapeDtypeStruct(shape, F32)
    return pl.pallas_call(
        body, name=name, grid=(R // tr, C // tc), in_specs=[blk, gblk, blk, blk], out_specs=[blk] * 3, out_shape=[sh] * 3,
        compiler_params=_params("parallel", "parallel"),
    )(w, g2, m, v)


def _place():
    x, y, c = lax.axis_index("x"), lax.axis_index("y"), lax.axis_index("c")
    chips = [(1 - x, y), (x, 1 - y), (1 - x, 1 - y)]
    return x, y, c, chips


_HBM = pl.BlockSpec(memory_space=pltpu.HBM)


class _Rider:
    def __init__(self, inputs, out_shapes, n_sems, sends, recvs, aliases=None):
        self.inputs, self.out_shapes, self.n_sems = list(inputs), list(out_shapes), n_sems
        self.sends, self.recvs, self.aliases = sends, recvs, aliases or {}

    def start(self, *refs):
        for cp in self.sends(*refs):
            cp.start()

    def wait(self, *refs):
        for cp in self.recvs(*refs):
            cp.wait_recv()
        for cp in self.sends(*refs):
            cp.wait_send()


def _remote(src, dst, send_sems, recv_sems, k, to):
    return pltpu.make_async_remote_copy(src_ref=src, dst_ref=dst, send_sem=send_sems.at[k], recv_sem=recv_sems.at[k],
                                        device_id=to, device_id_type=MESH)


def _run_alone(rider, name):
    ri = len(rider.inputs)

    def body(*refs):
        ins, outs, (send_sems, recv_sems) = refs[:ri], refs[ri:-2], refs[-2:]
        rider.start(ins, outs, send_sems, recv_sems)
        rider.wait(ins, outs, send_sems, recv_sems)

    return pl.pallas_call(
        body, name=name, in_specs=[_HBM] * ri, out_specs=[_HBM] * len(rider.out_shapes), out_shape=rider.out_shapes,
        scratch_shapes=[pltpu.SemaphoreType.DMA((rider.n_sems,))] * 2, input_output_aliases=rider.aliases,
    )(*rider.inputs)


def _hosted_call(body, rider, *, name, grid, in_specs, out_specs, out_shape, scratch_shapes, semantics):
    if rider is None:
        return pl.pallas_call(body, name=name, grid=grid, in_specs=in_specs, out_specs=out_specs, out_shape=out_shape,
                              scratch_shapes=scratch_shapes, compiler_params=_params(*semantics))
    n_in, n_out, n_scr = len(in_specs), len(out_specs), len(scratch_shapes)
    ri, ro = len(rider.inputs), len(rider.out_shapes)

    def hosted(*refs):
        parts, p = [], 0
        for cnt in (n_in, ri, n_out, ro, n_scr, 2):
            parts.append(refs[p:p + cnt])
            p += cnt
        ins, rins, outs, routs, scr, (send_sems, recv_sems) = parts
        first = functools.reduce(jnp.logical_and, [pl.program_id(a) == 0 for a in range(len(grid))])
        last = functools.reduce(jnp.logical_and, [pl.program_id(a) == grid[a] - 1 for a in range(len(grid))])

        @pl.when(first)
        def _():
            rider.start(rins, routs, send_sems, recv_sems)

        body(*ins, *outs, *scr)

        @pl.when(last)
        def _():
            rider.wait(rins, routs, send_sems, recv_sems)

    call = pl.pallas_call(
        hosted, name=name, grid=grid, in_specs=list(in_specs) + [_HBM] * ri, out_specs=list(out_specs) + [_HBM] * ro,
        out_shape=list(out_shape) + rider.out_shapes,
        scratch_shapes=list(scratch_shapes) + [pltpu.SemaphoreType.DMA((rider.n_sems,))] * 2,
        input_output_aliases={n_in + i: n_out + o for i, o in rider.aliases.items()},
        compiler_params=_params(*(("arbitrary",) * len(grid))))

    def run(*args):
        res = call(*args, *rider.inputs)
        return res[:n_out], res[n_out:]

    return run


def _ride_gather_ici(packs):
    n = len(packs)

    def sends(ins, outs, send_sems, recv_sems):
        x, y, c, chips = _place()
        return [_remote(ins[a].at[c], outs[a].at[2 * x + y, c], send_sems, recv_sems, 3 * a + j, (*chip, c))
                for a in range(n) for j, chip in enumerate(chips)]

    def recvs(ins, outs, send_sems, recv_sems):
        x, y, c, chips = _place()
        return [_remote(ins[a].at[c], outs[a].at[2 * chip[0] + chip[1], c], send_sems, recv_sems, 3 * a + j, (x, y, c))
                for a in range(n) for j, chip in enumerate(chips)]

    return _Rider(packs, [jax.ShapeDtypeStruct((N_CHIP,) + p.shape, p.dtype) for p in packs], 3 * n, sends, recvs)


def _ride_gather_d2d(gathered):
    n = len(gathered)

    def copies(landing_half, to):
        def build(ins, outs, send_sems, recv_sems):
            x, y, c, chips = _place()
            return [_remote(ins[a].at[2 * chip[0] + chip[1], c], outs[a].at[2 * chip[0] + chip[1], landing_half(c)],
                            send_sems, recv_sems, 3 * a + j, to(x, y, c))
                    for a in range(n) for j, chip in enumerate(chips)]
        return build

    return _Rider(gathered, [jax.ShapeDtypeStruct(g.shape, g.dtype) for g in gathered], 3 * n,
                  copies(lambda c: c, lambda x, y, c: (x, y, 1 - c)), copies(lambda c: 1 - c, lambda x, y, c: (x, y, c)),
                  aliases={a: a for a in range(n)})


def _ride_exchange(gs):
    n = len(gs)
    shapes = [g.shape[1:] if g.ndim == 4 else g.shape[:2] + (g.shape[2] // 2,) for g in gs]

    def copies(ins, outs, send_sems, recv_sems):
        x, y, c, _ = _place()

        def theirs(a):
            if gs[a].ndim == 4:
                return ins[a].at[1 - c]
            cols = shapes[a][2]
            return ins[a].at[:, :, pl.ds((1 - c) * cols, cols)]

        return [_remote(theirs(a), outs[a], send_sems, recv_sems, a, (x, y, 1 - c)) for a in range(n)]

    return _Rider(gs, [jax.ShapeDtypeStruct(sh, g.dtype) for sh, g in zip(shapes, gs)], n, copies, copies)


def _ride_scatter(b16s, to=(0, 1, 2), landing=None):
    n = len(b16s)

    def sends(ins, outs, send_sems, recv_sems):
        x, y, c, chips = _place()
        return [_remote(ins[a].at[2 * chips[j][0] + chips[j][1]], outs[a].at[2 * x + y], send_sems, recv_sems, 3 * a + j,
                        (*chips[j], c)) for a in range(n) for j in to]

    def recvs(ins, outs, send_sems, recv_sems):
        x, y, c, chips = _place()
        return [_remote(ins[a].at[2 * x + y], outs[a].at[2 * chips[j][0] + chips[j][1]], send_sems, recv_sems, 3 * a + j,
                        (x, y, c)) for a in range(n) for j in to]

    return _Rider(list(b16s) + list(landing or []), [jax.ShapeDtypeStruct(b.shape, b.dtype) for b in b16s], 3 * n,
                  sends, recvs, aliases={n + a: a for a in range(n)} if landing else None)


def _slab_tile(r, cols):
    tr = _tile(r, 256, 16)
    if tr % 16 == 0:
        return tr, cols
    return r, _tile(cols, 128)


def add_halves(g, got, idx, name):
    ns, r, cols = got.shape
    tr, tc = _slab_tile(r, cols)
    if g.ndim == 4:
        mine = pl.BlockSpec((None, None, tr, tc), lambda s, i, j, idx_ref: (idx_ref[0], s, i, j))
    else:
        mine = pl.BlockSpec((None, tr, tc), lambda s, i, j, idx_ref: (s, i, idx_ref[0] * (cols // tc) + j))

    def body(idx_ref, a_ref, b_ref, o32_ref, o16_ref):
        s = a_ref[...] + b_ref[...]
        o32_ref[...] = s
        o16_ref[...] = s.astype(BF16)

    blk = pl.BlockSpec((None, tr, tc), lambda s, i, j, idx_ref: (s, i, j))
    return pl.pallas_call(
        body, name=name,
        grid_spec=pltpu.PrefetchScalarGridSpec(
            num_scalar_prefetch=1, grid=(ns, r // tr, cols // tc),
            in_specs=[mine, blk], out_specs=[blk, blk]),
        out_shape=[jax.ShapeDtypeStruct((ns, r, cols), F32), jax.ShapeDtypeStruct((ns, r, cols), BF16)],
        compiler_params=_params("parallel", "parallel", "parallel"),
    )(idx, g, got)


def add_chips(a32, got16, idx, name):
    ns, r, cols = a32.shape
    tr, tc = _slab_tile(r, cols)

    def body(idx_ref, a_ref, r1_ref, r2_ref, r3_ref, o_ref):
        o_ref[...] = ((a_ref[...] + r1_ref[...].astype(F32)) + r2_ref[...].astype(F32)) + r3_ref[...].astype(F32)

    def slab(k):
        return pl.BlockSpec((None, tr, tc), lambda i, j, idx_ref: ((idx_ref[1] + k) % ns, i, j))

    return pl.pallas_call(
        body, name=name,
        grid_spec=pltpu.PrefetchScalarGridSpec(
            num_scalar_prefetch=1, grid=(r // tr, cols // tc), in_specs=[slab(0), slab(1), slab(2), slab(3)],
            out_specs=pl.BlockSpec((tr, tc), lambda i, j, idx_ref: (i, j))),
        out_shape=jax.ShapeDtypeStruct((r, cols), F32),
        compiler_params=_params("parallel", "parallel"),
    )(idx, a32, got16, got16, got16)


def share_halves(halves):
    n = len(halves)

    def body(*refs):
        in_refs, out_refs, (send_sems, recv_sems) = refs[:n], refs[n:2 * n], refs[2 * n:]
        x, y, c, _ = _place()
        cps = [pltpu.make_async_remote_copy(src_ref=in_refs[a], dst_ref=out_refs[a], send_sem=send_sems.at[a],
                                            recv_sem=recv_sems.at[a], device_id=(x, y, 1 - c), device_id_type=MESH)
               for a in range(n)]
        for cp in cps:
            cp.start()
        for cp in cps:
            cp.wait()

    return pl.pallas_call(
        body, name="share_halves", in_specs=[_HBM] * n, out_specs=[_HBM] * n,
        out_shape=[jax.ShapeDtypeStruct(h.shape, F32) for h in halves],
        scratch_shapes=[pltpu.SemaphoreType.DMA((n,)), pltpu.SemaphoreType.DMA((n,))],
    )(*halves)


def allreduce_small(v):
    R, _ = v.shape

    def body(in_ref, out_ref, slots, send_sems, recv_sems):
        x, y, c, _ = _place()
        me = 4 * x + 2 * y + c
        slots[me] = in_ref[...]
        cps = []
        for k in range(1, N_DEV):
            to = (x ^ (k >> 2), y ^ ((k >> 1) & 1), c ^ (k & 1))
            cps.append(pltpu.make_async_remote_copy(src_ref=in_ref, dst_ref=slots.at[me], send_sem=send_sems.at[k - 1],
                                                    recv_sem=recv_sems.at[k - 1], device_id=to, device_id_type=MESH))
        for cp in cps:
            cp.start()
        for k in range(1, N_DEV):
            frm = 4 * (x ^ (k >> 2)) + 2 * (y ^ ((k >> 1) & 1)) + (c ^ (k & 1))
            pltpu.make_async_remote_copy(src_ref=in_ref, dst_ref=slots.at[frm], send_sem=send_sems.at[k - 1],
                                         recv_sem=recv_sems.at[k - 1], device_id=(x, y, c), device_id_type=MESH).wait_recv()
        for cp in cps:
            cp.wait_send()
        acc = slots[0]
        for d in range(1, N_DEV):
            acc = acc + slots[d]
        out_ref[...] = acc

    vm = pl.BlockSpec(memory_space=pltpu.VMEM)
    return pl.pallas_call(
        body, name="allreduce_small", in_specs=[vm], out_specs=vm, out_shape=jax.ShapeDtypeStruct((R, ROW), F32),
        scratch_shapes=[pltpu.VMEM((N_DEV, R, ROW), F32), pltpu.SemaphoreType.DMA((N_DEV - 1,)),
                        pltpu.SemaphoreType.DMA((N_DEV - 1,))],
    )(v)


def _rows_of(n, unit=16):
    return -(-n // (unit * ROW)) * unit


def _pack_rows(items, total_rows, dtype, unit=16):
    parts = []
    used = 0
    for a in items:
        flat = a.reshape(-1)
        r = _rows_of(flat.shape[0], unit)
        flat = jnp.pad(flat, (0, r * ROW - flat.shape[0]))
        parts.append(flat.reshape(r, ROW))
        used += r
    if total_rows > used:
        parts.append(jnp.zeros((total_rows - used, ROW), dtype))
    return jnp.concatenate(parts, axis=0)


def _unpack_rows(buf, shapes, unit=16):
    lead = buf.shape[:-2]
    out = []
    off = 0
    for shp in shapes:
        n = math.prod(shp)
        r = _rows_of(n, unit)
        piece = buf[..., off:off + r, :].reshape(*lead, r * ROW)[..., :n].reshape(*lead, *shp)
        out.append(piece)
        off += r
    return out


def _interleave_heads(w, H):
    lead = w.shape[:-1]
    return w.reshape(*lead, 3, H, HEAD).swapaxes(-3, -2).reshape(*lead, 3 * H * HEAD)


def _deinterleave_heads(w, H):
    lead = w.shape[:-1]
    return w.reshape(*lead, H, 3, HEAD).swapaxes(-3, -2).reshape(*lead, 3 * H * HEAD)


def _interleave_head_rows(w, H):
    return w.reshape(3, H, HEAD, w.shape[-1]).swapaxes(0, 1).reshape(3 * H * HEAD, w.shape[-1])


def _deinterleave_head_rows(w, H):
    return w.reshape(H, 3, HEAD, w.shape[-1]).swapaxes(0, 1).reshape(3 * H * HEAD, w.shape[-1])


def kernel(x, norm_mix, w_in, fox_f_bias, gdn_conv_w, gdn_a_log, gdn_dt_bias, gdn_norm, w_branch_fox, w_branch_gdn, w_out, norm_ffn, w_up, ffn_conv_w, w_down, norm_final, loss_target, m_norm_mix, m_w_in, m_fox_f_bias, m_gdn_conv_w, m_gdn_a_log, m_gdn_dt_bias, m_gdn_norm, m_w_branch_fox, m_w_branch_gdn, m_w_out, m_norm_ffn, m_w_up, m_ffn_conv_w, m_w_down, m_norm_final, v_norm_mix, v_w_in, v_fox_f_bias, v_gdn_conv_w, v_gdn_a_log, v_gdn_dt_bias, v_gdn_norm, v_w_branch_fox, v_w_branch_gdn, v_w_out, v_norm_ffn, v_w_up, v_ffn_conv_w, v_w_down, v_norm_final):
    B, S, D = x.shape
    T = B * S
    H = D // HEAD
    N = S // CHUNK
    FF = w_down.shape[1] * N_CHIP
    d_in = 9 * D + 3 * H
    assert w_in.shape[2] * N_CHIP == d_in and 3 * H <= 128

    cidx = lax.axis_index("c").astype(jnp.int32)
    sidx = (2 * lax.axis_index("x") + lax.axis_index("y")).astype(jnp.int32)
    idx = jnp.stack([cidx, sidx])

    rowed = [w_branch_fox[0], w_branch_gdn[0], w_out[0], w_down[0]]
    convs = [gdn_conv_w[0], ffn_conv_w[0]]
    rowed_shapes = [a.shape for a in rowed]
    conv_shapes = [a.shape + (2,) for a in convs]
    pad_rows = lambda shapes: -(-sum(_rows_of(math.prod(s)) for s in shapes) // 256) * 128
    Rh, Rc = pad_rows(rowed_shapes), pad_rows(conv_shapes)
    halves = lambda a: a.reshape(2, a.shape[0] // 2, a.shape[1])
    c_in = w_in.shape[2]
    packs_a = [w_in[0].T.astype(BF16).reshape(c_in, 2, D // 2).transpose(1, 0, 2),
               halves(_pack_rows([lax.bitcast_convert_type(a, BF16) for a in convs], 2 * Rc, BF16))]
    packs_b = [halves(w_up[0].astype(BF16)), halves(_pack_rows([a.astype(BF16) for a in rowed], 2 * Rh, BF16))]
    own = lambda gs, ps: [lax.dynamic_update_slice(g, p[None], (sidx, 0, 0, 0)) for g, p in zip(gs, ps)]
    by_cols = lambda g: g.transpose(1, 2, 0, 3).reshape(2 * g.shape[2], N_CHIP * g.shape[3])
    cat_cols = lambda p: jnp.concatenate([p[i] for i in range(N_CHIP)], axis=-1)
    cat_rows = lambda p: p.reshape(-1, p.shape[-1])
    x2 = x.reshape(T, D)
    hn1, landed = rmsnorm_fwd(x2, norm_mix, "rmsnorm_mix", rider=_ride_gather_ici(packs_a))
    g_in, g_conv = own(_run_alone(_ride_gather_d2d(landed), "gather_to_sibling"), packs_a)
    W_inT = g_in.transpose(0, 2, 1, 3).reshape(N_CHIP * c_in, D)
    conv_parts = _unpack_rows(g_conv.reshape(N_CHIP, 2 * Rc, ROW), conv_shapes)
    gconv = cat_cols(lax.bitcast_convert_type(conv_parts[0], F32))
    fconv = cat_cols(lax.bitcast_convert_type(conv_parts[1], F32))

    o1, o2 = 3 * D, 3 * D + H
    o3, o4, o5, o6 = o2 + 3 * D, o2 + 3 * D + H, o2 + 3 * D + 2 * H, o2 + 4 * D + 2 * H
    W_foxT = _interleave_head_rows(W_inT[:o1], H)
    W_gqkvT = _interleave_head_rows(W_inT[o2:o3], H)
    W_gzT = W_inT[o5:o6]
    W_gatesT = W_inT[o6:]
    W_smallT = jnp.concatenate([W_inT[o1:o2], W_inT[o3:o5], jnp.zeros((128 - 3 * H, D), BF16)], axis=0)
    gconv_i = _interleave_heads(gconv, H)
    fconv_g, fconv_v = fconv[:, :FF], fconv[:, FF:]
    prm = jnp.zeros((8, 128), F32)
    prm = prm.at[0, 0:H].set(fox_f_bias[0]).at[0, H:2 * H].set(gdn_dt_bias[0]).at[1, H:2 * H].set(gdn_a_log[0])

    tgt = loss_target.reshape(T, D)

    p_fox = matmul(hn1, W_foxT, "nt", "proj_fox", out_dtype=BF16)
    p_gqkv = matmul(hn1, W_gqkvT, "nt", "proj_gqkv")
    p_gz = matmul(hn1, W_gzT, "nt", "proj_gz")
    p_gates = matmul(hn1, W_gatesT, "nt", "proj_gates")
    p_small = matmul(hn1, W_smallT, "nt", "proj_small")

    sm = small_fwd(p_small, prm, B, S, H)
    heads = lambda a: a.reshape(B, S, H).transpose(0, 2, 1)
    c_bhs, gc_bhs, beta_bhs = heads(sm[:, 0:H]), heads(sm[:, H:2 * H]), heads(sm[:, 2 * H:3 * H])
    c_col, c_row = c_bhs[..., None], c_bhs[:, :, None, :]
    gcr5 = gc_bhs.reshape(B, H, N, 1, CHUNK)
    gcr_u = gc_bhs.reshape(B, H, N // PAIR, 1, PAIR * CHUNK)
    betar_u = beta_bhs.reshape(B, H, N // PAIR, 1, PAIR * CHUNK)

    (o_fox, o_fox16, lse), arriving = fox_fwd(p_fox, c_col, c_row, B, S, H, rider=_ride_gather_ici(packs_b))
    qkvn = gdn_prep_fwd(p_gqkv, gconv_i, B, S, H)
    (u_hat, w_t, t_inv), arrived = gdn_intra_fwd(qkvn, betar_u, gcr_u, B, S, H, rider=_ride_gather_d2d(arriving))
    g_up, g_rowed = own(arrived, packs_b)
    W_up = by_cols(g_up)
    W_up_g, W_up_v = W_up[:, :FF], W_up[:, FF:]
    W_bf, W_bg, W_out, W_down = (cat_rows(p) for p in _unpack_rows(g_rowed.reshape(N_CHIP, 2 * Rh, ROW), rowed_shapes))
    o_gdn, states, y_gdn = gdn_inter_fwd(qkvn, u_hat, w_t, gcr5, p_gz, gdn_norm, B, S, H)
    bf_ = matmul(o_fox16, W_bf, "nn", "branch_fox")
    bg_, y = matmul(y_gdn, W_bg, "nn", "branch_gdn", post=_post_merge(p_gates, bf_))
    h1, hn2 = matmul(y, W_out, "nn", "out_proj", add=x2, post=_post_rmsnorm(norm_ffn))
    up_g = matmul(hn2, W_up_g, "nn", "up_gate")
    up_v = matmul(hn2, W_up_v, "nn", "up_val")
    act = ffn_gate_fwd(up_g, up_v, fconv_g, fconv_v, B, S)
    dh2, dh2_16, loss_cols, d_norm_final = matmul(act, W_down, "nn", "down_proj", add=h1,
                                                  post=_post_loss(norm_final.reshape(1, D), tgt))
    loss_here = (0.5 / D) * jnp.sum(loss_cols)

    d_act = matmul(dh2_16, W_down, "nt", "d_act")
    dW_down = matmul(act, dh2_16, "tn", "dw_down")
    d_upg, d_upv, d_fconv_g, d_fconv_v = ffn_gate_bwd(up_g, up_v, fconv_g, fconv_v, d_act, B, S)
    d_hn2 = matmul(d_upg, W_up_g, "nt", "d_hn2_g")
    dh1, dh1_16, d_norm_ffn = matmul(d_upv, W_up_v, "nt", "d_hn2_v", add=d_hn2,
                                     post=_post_rmsnorm_bwd(h1, norm_ffn, dh2, True))
    dW_up_slabs = jnp.concatenate([matmul(hn2, d_upg, "tn", "dw_up_g", slab=(D // 2, 2 * FF // N_CHIP)),
                                   matmul(hn2, d_upv, "tn", "dw_up_v", slab=(D // 2, 2 * FF // N_CHIP))], axis=1)
    d_bf, d_bg, d_gates = matmul(dh1_16, W_out, "nt", "d_y", post=_post_merge_bwd(p_gates, bf_, bg_))
    dW_out = matmul(y, dh1_16, "tn", "dw_out")
    d_ofox = matmul(d_bf, W_bf, "nt", "d_ofox")
    dW_bf = matmul(o_fox16, d_bf, "tn", "dw_bf")
    d_ygdn = matmul(d_bg, W_bg, "nt", "d_ygdn")
    dW_bg = matmul(y_gdn, d_bg, "tn", "dw_bg")

    d_fconv = jnp.concatenate([d_fconv_g, d_fconv_v], axis=1)
    col_shard = lambda g, s: g[:, s * (g.shape[1] // N_CHIP):(s + 1) * (g.shape[1] // N_CHIP)]
    row_shard = lambda g, s: g[s * (g.shape[0] // N_CHIP):(s + 1) * (g.shape[0] // N_CHIP)]
    shard_items = lambda s: [row_shard(dW_bf, s), row_shard(dW_bg, s), row_shard(dW_out, s), row_shard(dW_down, s),
                             col_shard(d_fconv, s)]
    g_shapes = [a.shape for a in shard_items(0)]
    assert sum(_rows_of(math.prod(s)) for s in g_shapes) <= 2 * Rh
    to_slabs = lambda g: g.reshape(2, g.shape[0] // 2, N_CHIP, g.shape[1] // N_CHIP).transpose(0, 2, 1, 3)
    gpacks_b = [dW_up_slabs,
                jnp.stack([_pack_rows(shard_items(s), 2 * Rh, F32).reshape(2, Rh, ROW) for s in range(N_CHIP)], axis=1)]
    (d_pfox, d_ccol, d_crow), gots_b = fox_bwd(p_fox, c_col, c_row, o_fox, lse, d_ofox, B, S, H,
                                              rider=_ride_exchange(gpacks_b))
    sums_b = [add_halves(g, got, idx, "add_halves_b%d" % i) for i, (g, got) in enumerate(zip(gpacks_b, gots_b))]

    (dq_i, dk_i, d_uh, d_wt, dgcr_a, d_gz, d_gn_parts), got16_b = gdn_inter_bwd(
        qkvn, u_hat, w_t, gcr5, states, o_gdn, p_gz, gdn_norm, d_ygdn, B, S, H,
        rider=_ride_scatter([s16 for _, s16 in sums_b]))
    d_gdn_norm = jnp.sum(d_gn_parts[:, :, 0, :], axis=(0, 1))[None]
    mine_b = [add_chips(s32, g16, idx, "add_chips_b%d" % i) for i, ((s32, _), g16) in enumerate(zip(sums_b, got16_b))]
    d_qkvn, d_betar5, dgcr_b = gdn_intra_bwd(qkvn, betar_u, gcr_u, t_inv, d_uh, d_wt, dq_i, dk_i, B, S, H)
    d_pgqkv, d_gconv_i = gdn_prep_bwd(p_gqkv, gconv_i, d_qkvn, B, S, H)

    tokens = lambda a: a.reshape(B, H, S).transpose(0, 2, 1).reshape(T, H)
    d_gc = dgcr_a.reshape(B, H, S) + dgcr_b.reshape(B, H, S)
    d_sm = jnp.concatenate([tokens(d_ccol.reshape(B, H, S) + d_crow.reshape(B, H, S)), tokens(d_gc), tokens(d_betar5.reshape(B, H, S)),
                            jnp.zeros((T, 128 - 3 * H), F32)], axis=1)
    d_psmall, d_prm = small_bwd(p_small, prm, d_sm, B, S, H)

    dW_foxT = matmul(d_pfox, hn1, "tn", "dw_fox")
    dW_gqkvT = matmul(d_pgqkv, hn1, "tn", "dw_gqkv")
    dW_gzT = matmul(d_gz, hn1, "tn", "dw_gz")
    dW_gatesT = matmul(d_gates, hn1, "tn", "dw_gates")
    dW_smallT = matmul(d_psmall, hn1, "tn", "dw_small")
    dW_inT = jnp.concatenate([_deinterleave_head_rows(dW_foxT, H), dW_smallT[0:H], _deinterleave_head_rows(dW_gqkvT, H),
                              dW_smallT[H:3 * H], dW_gzT, dW_gatesT], axis=0)
    d_gconv = _deinterleave_heads(d_gconv_i, H)

    gpack_a = [dW_inT.reshape(N_CHIP, c_in, D)]
    d_hn1, gots_a = matmul(d_pfox, W_foxT, "nn", "d_hn1_fox", rider=_ride_exchange(gpack_a))
    sums_a = [add_halves(gpack_a[0], gots_a[0], idx, "add_halves_a")]
    d_hn1, landing_a = matmul(d_pgqkv, W_gqkvT, "nn", "d_hn1_gqkv", add=d_hn1,
                              rider=_ride_scatter([sums_a[0][1]], to=(0, 1)))
    d_hn1 = matmul(d_gz, W_gzT, "nn", "d_hn1_gz", add=d_hn1)
    d_hn1, got16_a = matmul(d_gates, W_gatesT, "nn", "d_hn1_gates", add=d_hn1,
                            rider=_ride_scatter([sums_a[0][1]], to=(2,), landing=landing_a))
    mine = [add_chips(sums_a[0][0], got16_a[0], idx, "add_chips_a")] + mine_b
    grad_x, d_norm_mix = matmul(d_psmall, W_smallT, "nn", "d_hn1_small", add=d_hn1,
                                post=_post_rmsnorm_bwd(x2, norm_mix, dh1, False))

    others = share_halves(mine)
    g_w_inT, g_up, g_rows = (jnp.concatenate([jnp.where(cidx == 0, h, o), jnp.where(cidx == 0, o, h)], axis=ax)
                             for h, o, ax in zip(mine, others, (1, 0, 0)))
    g_w_in = g_w_inT.T
    g_bf, g_bg, g_out, g_down, g_fconv = _unpack_rows(g_rows, g_shapes)

    small_items = [d_norm_mix, d_norm_ffn, d_norm_final, d_gdn_norm, d_prm, d_gconv, loss_here.reshape(1, 1)]
    small_shapes = [a.shape for a in small_items]
    sv = allreduce_small(_pack_rows(small_items, 0, F32, unit=8))
    g_norm_mix, g_norm_ffn, g_norm_final, g_gdn_norm, g_prm, g_gconv_all, loss = _unpack_rows(sv, small_shapes, unit=8)
    loss = loss[0, 0]
    g_norm_final = g_norm_final.reshape(D)
    g_fbias, g_dtb, g_alog = g_prm[0:1, 0:H], g_prm[0:1, H:2 * H], g_prm[1:2, H:2 * H]
    g_gconv = lax.dynamic_slice_in_dim(g_gconv_all, sidx * (3 * D // N_CHIP), 3 * D // N_CHIP, axis=1)

    names = ["norm_mix", "w_in", "fox_f_bias", "gdn_conv_w", "gdn_a_log", "gdn_dt_bias", "gdn_norm", "w_branch_fox",
             "w_branch_gdn", "w_out", "norm_ffn", "w_up", "ffn_conv_w", "w_down", "norm_final"]
    ws = [norm_mix, w_in, fox_f_bias, gdn_conv_w, gdn_a_log, gdn_dt_bias, gdn_norm, w_branch_fox, w_branch_gdn, w_out,
          norm_ffn, w_up, ffn_conv_w, w_down, norm_final]
    ms = [m_norm_mix, m_w_in, m_fox_f_bias, m_gdn_conv_w, m_gdn_a_log, m_gdn_dt_bias, m_gdn_norm, m_w_branch_fox,
          m_w_branch_gdn, m_w_out, m_norm_ffn, m_w_up, m_ffn_conv_w, m_w_down, m_norm_final]
    vs = [v_norm_mix, v_w_in, v_fox_f_bias, v_gdn_conv_w, v_gdn_a_log, v_gdn_dt_bias, v_gdn_norm, v_w_branch_fox,
          v_w_branch_gdn, v_w_out, v_norm_ffn, v_w_up, v_ffn_conv_w, v_w_down, v_norm_final]
    gs = [g_norm_mix, g_w_in, g_fbias, g_gconv, g_alog, g_dtb, g_gdn_norm, g_bf, g_bg, g_out, g_norm_ffn, g_up,
          g_fconv, g_down, g_norm_final]
    gs = [g.reshape(w.shape) for g, w in zip(gs, ws)]
    deltas, new_ms, new_vs = [], [], []
    for nm, w, g, m, v in zip(names, ws, gs, ms, vs):
        if w.ndim == 1:
            d, a, b = adamw(w.reshape(1, -1), g.reshape(1, -1), m.reshape(1, -1), v.reshape(1, -1), "adamw_" + nm)
            d, a, b = d.reshape(w.shape), a.reshape(w.shape), b.reshape(w.shape)
        elif nm == "w_in":
            d, a, b = (r.T[None] for r in adamw(w[0].T, g_w_inT, m[0].T, v[0].T, "adamw_" + nm))
        else:
            d, a, b = adamw(w, g, m, v, "adamw_" + nm)
        deltas.append(d)
        new_ms.append(a)
        new_vs.append(b)

    return (loss, grad_x.reshape(B, S, D), *gs, *deltas, *new_ms, *new_vs)
```

```python
import functools
import math

import jax
import jax.numpy as jnp
from jax import lax
from jax.experimental import pallas as pl
from jax.experimental.pallas import tpu as pltpu

F32 = jnp.float32
BF16 = jnp.bfloat16
HEAD = 128
CHUNK = 64
GDN_CONV = 4
FFN_CONV = 3
EPS = 1e-6
NEG = -1e30
ROW = 1024
ATT_TILE = 512
MM_WEIGHT_TILE_BYTES = 8 << 20
MM_TN_OPERAND_BYTES = 32 << 20
N_CHIP = 4
N_DEV = 8
MESH = pl.DeviceIdType.MESH
HI = lax.Precision.HIGH
EXACT = lax.Precision.HIGHEST

ADAM_LR, ADAM_B1, ADAM_B2, ADAM_EPS, ADAM_WD, ADAM_STEP = 0.001, 0.9, 0.999, 1e-08, 0.01, 10


def _tile(n, cap, unit=128):
    best = None
    t = unit
    while t <= min(n, cap):
        if n % t == 0:
            best = t
        t += unit
    return best if best is not None else n


def _params(*sem):
    return pltpu.CompilerParams(dimension_semantics=sem)


_NN = (((1,), (0,)), ((), ()))
_NT = (((1,), (1,)), ((), ()))
_TN = (((0,), (0,)), ((), ()))


def _dg(a, b, dims, hi):
    if hi:
        return lax.dot_general(a, b, dims, precision=HI, preferred_element_type=F32)
    return lax.dot_general(a.astype(BF16), b.astype(BF16), dims, preferred_element_type=F32)


class _RawOps:
    @staticmethod
    def nn(a, b, hi=False):
        return _dg(a, b, _NN, hi)

    @staticmethod
    def nt(a, b, hi=False):
        return _dg(a, b, _NT, hi)

    @staticmethod
    def tn(a, b, hi=False):
        return _dg(a, b, _TN, hi)


def _make_diff_ops():
    def build(hi):
        @jax.custom_vjp
        def nn(a, b):
            return _dg(a, b, _NN, hi)

        nn.defvjp(lambda a, b: (_dg(a, b, _NN, hi), (a, b)),
                  lambda r, g: (_dg(g, r[1], _NT, hi), _dg(r[0], g, _TN, hi)))

        @jax.custom_vjp
        def nt(a, b):
            return _dg(a, b, _NT, hi)

        nt.defvjp(lambda a, b: (_dg(a, b, _NT, hi), (a, b)),
                  lambda r, g: (_dg(g, r[1], _NN, hi), _dg(g, r[0], _TN, hi)))

        @jax.custom_vjp
        def tn(a, b):
            return _dg(a, b, _TN, hi)

        tn.defvjp(lambda a, b: (_dg(a, b, _TN, hi), (a, b)),
                  lambda r, g: (_dg(r[1], g, _NT, hi), _dg(r[0], g, _NN, hi)))
        return nn, nt, tn

    lo, hi_ = build(False), build(True)

    class _DiffOps:
        @staticmethod
        def nn(a, b, hi=False):
            return (hi_ if hi else lo)[0](a, b)

        @staticmethod
        def nt(a, b, hi=False):
            return (hi_ if hi else lo)[1](a, b)

        @staticmethod
        def tn(a, b, hi=False):
            return (hi_ if hi else lo)[2](a, b)

    return _DiffOps


_DiffOps = _make_diff_ops()


def _sigmoid(x):
    return 1.0 / (1.0 + jnp.exp(-x))


def _mm_tile(n, pref):
    if n % pref == 0:
        return pref
    if n % 1408 == 0:
        return 1408
    return _tile(n, pref)


class _Post:
    def __init__(self, fn, row_ins=(), vec_ins=(), row_outs=(), acc_outs=(), keep_main=True):
        self.fn, self.keep_main = fn, keep_main
        self.row_ins = [r if isinstance(r, tuple) else (r, r.shape[1], 0) for r in row_ins]
        self.vec_ins, self.row_outs, self.acc_outs = list(vec_ins), list(row_outs), list(acc_outs)


def matmul(a, b, mode, name, add=None, out_dtype=F32, post=None, rider=None, slab=None):
    if mode == "nn":
        (M, K), (K2, N) = a.shape, b.shape
    elif mode == "nt":
        (M, K), (N, K2) = a.shape, b.shape
    else:
        (K, M), (K2, N) = a.shape, b.shape
    assert K == K2, (name, a.shape, b.shape)
    tn = slab[1] if slab else _mm_tile(N, 1024)
    if mode == "tn":
        tm = slab[0] if slab else (M if M <= 1408 else _mm_tile(M, 1408))
        row_bytes = 2 * (tm * a.dtype.itemsize + tn * b.dtype.itemsize)
        tk = next((t for t in (4096, 2048) if K % t == 0 and t * row_bytes <= MM_TN_OPERAND_BYTES), _mm_tile(K, 1024))
    else:
        tk = K if K * tn * 2 <= MM_WEIGHT_TILE_BYTES else _mm_tile(K, 1024)
        tm = _mm_tile(M, 1024 if tk <= 2048 and post is None else 512)
    nk = K // tk
    assert post is None or (mode != "tn" and tn == N), name
    dims = {"nn": _NN, "nt": _NT, "tn": _TN}[mode]
    if mode == "tn":
        a_spec = pl.BlockSpec((tk, tm), lambda j, i, k: (k, i))
    else:
        a_spec = pl.BlockSpec((tm, tk), lambda j, i, k: (i, k))
    if mode == "nt":
        b_spec = pl.BlockSpec((tn, tk), lambda j, i, k: (j, k))
    else:
        b_spec = pl.BlockSpec((tk, tn), lambda j, i, k: (k, j))
    o_spec = pl.BlockSpec((tm, tn), lambda j, i, k: (i, j))
    has_add = add is not None
    keep_main = post is None or post.keep_main
    counts = [2 + has_add] + ([len(post.row_ins), len(post.vec_ins)] if post else [0, 0]) + [int(keep_main)]
    counts += ([len(post.row_outs), len(post.acc_outs)] if post else [0, 0]) + [int(nk > 1)]

    def body(*refs):
        parts, p = [], 0
        for cnt in counts:
            parts.append(refs[p:p + cnt])
            p += cnt
        core, row_ins, vec_ins, main, row_outs, acc_outs, acc = parts
        a_ref, b_ref = core[:2]
        prod = lax.dot_general(a_ref[...].astype(BF16), b_ref[...].astype(BF16), dims, preferred_element_type=F32)

        def finish(r):
            if has_add:
                r = r + core[2][...]
            if keep_main:
                main[0][...] = r.astype(out_dtype)
            if post is not None:
                @pl.when(pl.program_id(1) == 0)
                def _():
                    for ref in acc_outs:
                        ref[...] = jnp.zeros_like(ref)

                post.fn(r, row_ins, vec_ins, row_outs, acc_outs)

        if nk == 1:
            finish(prod)
            return
        acc_ref = acc[0]
        k = pl.program_id(2)

        @pl.when(k == 0)
        def _():
            acc_ref[...] = jnp.zeros_like(acc_ref)

        acc_ref[...] += prod

        @pl.when(k == nk - 1)
        def _():
            finish(acc_ref[...])

    in_specs = [a_spec, b_spec] + ([o_spec] if has_add else [])
    args = (a, b) + ((add,) if has_add else ())
    out_specs = [o_spec] if keep_main else []
    out_shape = [jax.ShapeDtypeStruct((M, N), out_dtype)] if keep_main else []
    if slab:
        out_specs = [pl.BlockSpec((None, None, tm, tn), lambda j, i, k: (i, j, 0, 0))]
        out_shape = [jax.ShapeDtypeStruct((M // tm, N // tn, tm, tn), out_dtype)]
    if post is not None:
        in_specs += [pl.BlockSpec((tm, cols), lambda j, i, k, cb=cb: (i, cb)) for _, cols, cb in post.row_ins]
        in_specs += [pl.BlockSpec((1, v.shape[1]), lambda j, i, k: (0, 0)) for v in post.vec_ins]
        args += tuple(r for r, _, _ in post.row_ins) + tuple(post.vec_ins)
        out_specs += [pl.BlockSpec((tm, cols), lambda j, i, k: (i, 0)) for cols, _ in post.row_outs]
        out_specs += [pl.BlockSpec((1, cols), lambda j, i, k: (0, 0)) for cols in post.acc_outs]
        out_shape += [jax.ShapeDtypeStruct((M, cols), dt) for cols, dt in post.row_outs]
        out_shape += [jax.ShapeDtypeStruct((1, cols), F32) for cols in post.acc_outs]
    rows_sem = "arbitrary" if post is not None and post.acc_outs else "parallel"
    res = _hosted_call(
        body, rider, name=name, grid=(N // tn, M // tm, nk), in_specs=in_specs, out_specs=out_specs, out_shape=out_shape,
        scratch_shapes=[pltpu.VMEM((tm, tn), F32)] if nk > 1 else [], semantics=("parallel", rows_sem, "arbitrary"),
    )(*args)
    if rider is not None:
        res, carried = res
        return (res[0] if post is None else res), carried
    return res[0] if post is None else res


def rmsnorm_fwd(x, g, name, rider=None):
    T, D = x.shape
    tm = _tile(T, 512, 8)

    def body(x_ref, g_ref, o_ref):
        xv = x_ref[...]
        r = lax.rsqrt(jnp.mean(xv * xv, axis=-1, keepdims=True) + EPS)
        o_ref[...] = (xv * r * g_ref[...]).astype(BF16)

    res = _hosted_call(
        body, rider, name=name, grid=(T // tm,),
        in_specs=[pl.BlockSpec((tm, D), lambda i: (i, 0)), pl.BlockSpec((1, D), lambda i: (0, 0))],
        out_specs=[pl.BlockSpec((tm, D), lambda i: (i, 0))], out_shape=[jax.ShapeDtypeStruct((T, D), BF16)],
        scratch_shapes=[], semantics=("parallel",),
    )(x, g)
    return res[0] if rider is None else (res[0][0], res[1])


def _post_rmsnorm(g):
    def fn(r, row_ins, vec_ins, row_outs, acc_outs):
        rs = lax.rsqrt(jnp.mean(r * r, axis=-1, keepdims=True) + EPS)
        row_outs[0][...] = (r * rs * vec_ins[0][...]).astype(BF16)

    return _Post(fn, vec_ins=[g], row_outs=[(g.shape[1], BF16)])


def _post_rmsnorm_bwd(x, g, dres, with_bf16):
    D = g.shape[1]

    def fn(dy, row_ins, vec_ins, row_outs, acc_outs):
        xv = row_ins[0][...]
        rs = lax.rsqrt(jnp.mean(xv * xv, axis=-1, keepdims=True) + EPS)
        xh = xv * rs
        acc_outs[0][...] += jnp.sum(dy * xh, axis=0, keepdims=True)
        dxh = dy * vec_ins[0][...]
        dx = row_ins[1][...] + rs * (dxh - xh * jnp.mean(dxh * xh, axis=-1, keepdims=True))
        row_outs[0][...] = dx
        if with_bf16:
            row_outs[1][...] = dx.astype(BF16)

    return _Post(fn, row_ins=[x, dres], vec_ins=[g], row_outs=[(D, F32)] + ([(D, BF16)] if with_bf16 else []),
                 acc_outs=[D], keep_main=False)


def _post_loss(g, target):
    D = g.shape[1]

    def fn(hv, row_ins, vec_ins, row_outs, acc_outs):
        rs = lax.rsqrt(jnp.mean(hv * hv, axis=-1, keepdims=True) + EPS)
        xh = hv * rs
        gv = vec_ins[0][...]
        err = xh * gv - row_ins[0][...]
        acc_outs[0][...] += jnp.sum(err * err, axis=0, keepdims=True)
        dy = err * (1.0 / D)
        acc_outs[1][...] += jnp.sum(dy * xh, axis=0, keepdims=True)
        dxh = dy * gv
        dh = rs * (dxh - xh * jnp.mean(dxh * xh, axis=-1, keepdims=True))
        row_outs[0][...] = dh
        row_outs[1][...] = dh.astype(BF16)

    return _Post(fn, row_ins=[target], vec_ins=[g], row_outs=[(D, F32), (D, BF16)], acc_outs=[D, D], keep_main=False)


def _shift_down(x, k):
    if k == 0:
        return x
    rows = lax.broadcasted_iota(jnp.int32, x.shape, 0)
    return jnp.where(rows >= k, pltpu.roll(x, k, 0), 0.0)


def _shift_up(x, k):
    if k == 0:
        return x
    s = x.shape[0]
    rows = lax.broadcasted_iota(jnp.int32, x.shape, 0)
    return jnp.where(rows < s - k, pltpu.roll(x, s - k, 0), 0.0)


def _conv_fwd(x, w_ref, kw, keep_shifted=False):
    shifted = [_shift_down(x, kw - 1 - i) for i in range(kw - 1)]
    y = x * w_ref[kw - 1:kw, :]
    for i in range(kw - 1):
        y = y + shifted[i] * w_ref[i:i + 1, :]
    return (y, shifted) if keep_shifted else y


def _conv_bwd(x, shifted, dy, w_ref, kw):
    dx = dy * w_ref[kw - 1:kw, :]
    dws = []
    for i in range(kw - 1):
        dx = dx + _shift_up(dy, kw - 1 - i) * w_ref[i:i + 1, :]
        dws.append(jnp.sum(dy * shifted[i], axis=0, keepdims=True))
    dws.append(jnp.sum(dy * x, axis=0, keepdims=True))
    return dx, dws


def ffn_gate_fwd(up_g, up_v, cw_g, cw_v, B, S):
    T, Fd = up_g.shape
    tc = _tile(Fd, 256)

    def body(g_ref, v_ref, wg_ref, wv_ref, o_ref):
        ug = _conv_fwd(g_ref[...], wg_ref, FFN_CONV)
        uv = _conv_fwd(v_ref[...], wv_ref, FFN_CONV)
        o_ref[...] = (ug * _sigmoid(ug) * uv).astype(BF16)

    blk = pl.BlockSpec((S, tc), lambda b, j: (b, j))
    wblk = pl.BlockSpec((FFN_CONV, tc), lambda b, j: (0, j))
    return pl.pallas_call(
        body, name="ffn_gate_fwd", grid=(B, Fd // tc), in_specs=[blk, blk, wblk, wblk], out_specs=blk,
        out_shape=jax.ShapeDtypeStruct((T, Fd), BF16), compiler_params=_params("parallel", "parallel"),
    )(up_g, up_v, cw_g, cw_v)


def ffn_gate_bwd(up_g, up_v, cw_g, cw_v, d_act, B, S):
    T, Fd = up_g.shape
    tc = _tile(Fd, 256)

    def body(g_ref, v_ref, wg_ref, wv_ref, da_ref, dg_ref, dv_ref, dwg_ref, dwv_ref):
        @pl.when(pl.program_id(1) == 0)
        def _():
            dwg_ref[...] = jnp.zeros_like(dwg_ref)
            dwv_ref[...] = jnp.zeros_like(dwv_ref)

        xg, xv = g_ref[...], v_ref[...]
        ug, sh_g = _conv_fwd(xg, wg_ref, FFN_CONV, keep_shifted=True)
        uv, sh_v = _conv_fwd(xv, wv_ref, FFN_CONV, keep_shifted=True)
        da = da_ref[...]
        sg = _sigmoid(ug)
        d_ug = da * uv * (sg + ug * sg * (1.0 - sg))
        d_uv = da * ug * sg
        dxg, dwg = _conv_bwd(xg, sh_g, d_ug, wg_ref, FFN_CONV)
        dxv, dwv = _conv_bwd(xv, sh_v, d_uv, wv_ref, FFN_CONV)
        dg_ref[...] = dxg.astype(BF16)
        dv_ref[...] = dxv.astype(BF16)
        for i in range(FFN_CONV):
            dwg_ref[i:i + 1, :] += dwg[i]
            dwv_ref[i:i + 1, :] += dwv[i]

    blk = pl.BlockSpec((S, tc), lambda j, b: (b, j))
    wblk = pl.BlockSpec((FFN_CONV, tc), lambda j, b: (0, j))
    return pl.pallas_call(
        body, name="ffn_gate_bwd", grid=(Fd // tc, B), in_specs=[blk, blk, wblk, wblk, blk],
        out_specs=[blk, blk, wblk, wblk],
        out_shape=[jax.ShapeDtypeStruct((T, Fd), BF16), jax.ShapeDtypeStruct((T, Fd), BF16),
                   jax.ShapeDtypeStruct((FFN_CONV, Fd), F32), jax.ShapeDtypeStruct((FFN_CONV, Fd), F32)],
        compiler_params=_params("parallel", "arbitrary"),
    )(up_g, up_v, cw_g, cw_v, d_act)


def _post_merge(p_gates, bf_):
    D = bf_.shape[1]

    def fn(bg, row_ins, vec_ins, row_outs, acc_outs):
        gf_ref, gg_ref, bf_ref = row_ins
        row_outs[0][...] = (_sigmoid(gf_ref[...]) * bf_ref[...] + _sigmoid(gg_ref[...]) * bg).astype(BF16)

    return _Post(fn, row_ins=[(p_gates, D, 0), (p_gates, D, 1), bf_], row_outs=[(D, BF16)])


def _post_merge_bwd(p_gates, bf_, bg_):
    D = bf_.shape[1]

    def fn(d, row_ins, vec_ins, row_outs, acc_outs):
        gf_ref, gg_ref, bf_ref, bg_ref = row_ins
        sf, sg = _sigmoid(gf_ref[...]), _sigmoid(gg_ref[...])
        row_outs[0][...] = (d * sf).astype(BF16)
        row_outs[1][...] = (d * sg).astype(BF16)
        row_outs[2][:, 0:D] = (d * bf_ref[...] * sf * (1.0 - sf)).astype(BF16)
        row_outs[2][:, D:2 * D] = (d * bg_ref[...] * sg * (1.0 - sg)).astype(BF16)

    return _Post(fn, row_ins=[(p_gates, D, 0), (p_gates, D, 1), bf_, bg_],
                 row_outs=[(D, BF16), (D, BF16), (2 * D, BF16)], keep_main=False)


ATT_HEADS = 2
ATT_HEADS_FWD = 4


def fox_fwd(p_fox, c_col, c_row, B, S, H, rider=None):
    T = B * S
    t = _tile(S, ATT_TILE)
    nq = S // t
    scale = HEAD ** -0.5
    hp = next((n for n in (ATT_HEADS_FWD, ATT_HEADS) if H % n == 0), 1)
    hs = range(hp)

    def body(*refs):
        qkv_refs, (cq_ref, cr_ref, o_ref, o16_ref, lse_ref) = refs[:3 * hp], refs[3 * hp:]
        i = pl.program_id(2)
        q = [qkv_refs[3 * hh][...] for hh in hs]
        row = lax.broadcasted_iota(jnp.int32, (t, t), 0)
        col = lax.broadcasted_iota(jnp.int32, (t, t), 1)

        def step(j, carry, diagonal):
            off = pl.multiple_of(j * t, t)
            k = [qkv_refs[3 * hh + 1][pl.ds(off, t), :] for hh in hs]
            v = [qkv_refs[3 * hh + 2][pl.ds(off, t), :] for hh in hs]
            s = [lax.dot_general(q[hh], k[hh], _NT, preferred_element_type=F32) * scale - cr_ref[hh, :, pl.ds(off, t)]
                 for hh in hs]
            if diagonal:
                s = [jnp.where(col <= row, s[hh], NEG) for hh in hs]
            m_new = [jnp.maximum(carry[hh][0], jnp.max(s[hh], axis=-1, keepdims=True)) for hh in hs]
            alpha = [jnp.exp(carry[hh][0] - m_new[hh]) for hh in hs]
            p = [jnp.exp(s[hh] - m_new[hh]) for hh in hs]
            l = [alpha[hh] * carry[hh][1] + jnp.sum(p[hh], axis=-1, keepdims=True) for hh in hs]
            acc = [alpha[hh] * carry[hh][2] + lax.dot_general(p[hh].astype(BF16), v[hh], _NN, preferred_element_type=F32)
                   for hh in hs]
            return tuple((m_new[hh], l[hh], acc[hh]) for hh in hs)

        start = (jnp.full((t, 1), NEG, F32), jnp.zeros((t, 1), F32), jnp.zeros((t, HEAD), F32))
        below = lax.fori_loop(0, i, functools.partial(step, diagonal=False), tuple(start for _ in hs))
        done = step(i, below, diagonal=True)
        for hh, (m, l, acc) in enumerate(done):
            cols = slice(hh * HEAD, (hh + 1) * HEAD)
            o = acc / l
            o_ref[:, cols] = o
            o16_ref[:, cols] = o.astype(BF16)
            lse_ref[hh] = cq_ref[hh] + m + jnp.log(l)

    qkv = []
    for hh in hs:
        qkv.append(pl.BlockSpec((t, HEAD), lambda b, g, i, hh=hh: (b * nq + i, 3 * (hp * g + hh))))
        qkv.append(pl.BlockSpec((S, HEAD), lambda b, g, i, hh=hh: (b, 3 * (hp * g + hh) + 1)))
        qkv.append(pl.BlockSpec((S, HEAD), lambda b, g, i, hh=hh: (b, 3 * (hp * g + hh) + 2)))
    heads = pl.BlockSpec((t, hp * HEAD), lambda b, g, i: (b * nq + i, g))
    return _hosted_call(
        body, rider, name="fox_fwd", grid=(B, H // hp, nq),
        in_specs=qkv + [pl.BlockSpec((None, hp, t, 1), lambda b, g, i: (b, g, i, 0)),
                        pl.BlockSpec((None, hp, 1, S), lambda b, g, i: (b, g, 0, 0))],
        out_specs=[heads, heads, pl.BlockSpec((None, hp, t, 1), lambda b, g, i: (b, g, i, 0))],
        out_shape=[jax.ShapeDtypeStruct((T, H * HEAD), F32), jax.ShapeDtypeStruct((T, H * HEAD), BF16),
                   jax.ShapeDtypeStruct((B, H, S, 1), F32)],
        scratch_shapes=[], semantics=("parallel", "parallel", "arbitrary"),
    )(*([p_fox] * (3 * hp)), c_col, c_row)


def fox_bwd(p_fox, c_col, c_row, o, lse, do, B, S, H, rider=None):
    T = B * S
    t = _tile(S, ATT_TILE)
    n = S // t
    scale = HEAD ** -0.5
    hp = ATT_HEADS if H % ATT_HEADS == 0 else 1
    hs = range(hp)

    def body(*refs):
        qkv_refs = refs[:3 * hp]
        cq_ref, cr_ref, o_ref, lse_ref, do_ref, dqkv_ref, dcq_ref, dcr_ref, dq_acc, delta_s, lse_s = refs[3 * hp:]
        row = lax.broadcasted_iota(jnp.int32, (t, t), 0)
        col = lax.broadcasted_iota(jnp.int32, (t, t), 1)
        cols = [slice(hh * HEAD, (hh + 1) * HEAD) for hh in hs]

        def prep(i, c):
            rows = pl.ds(pl.multiple_of(i * t, t), t)
            for hh in hs:
                delta_s[hh, rows, :] = jnp.sum(do_ref[rows, cols[hh]] * o_ref[rows, cols[hh]], axis=-1, keepdims=True)
                lse_s[hh, rows, :] = lse_ref[hh, rows, :] - cq_ref[hh, rows, :]
                dq_acc[hh, rows, :] = jnp.zeros((t, HEAD), F32)
                dcq_ref[hh, rows, :] = jnp.zeros((t, 1), F32)
            return c

        lax.fori_loop(0, n, prep, 0)

        def kv_step(j, c):
            joff = pl.multiple_of(j * t, t)
            w = t // 2
            halves = (0, 1)
            k = [[qkv_refs[3 * hh + 1][pl.ds(pl.multiple_of(joff + e * w, w), w), :] for e in halves] for hh in hs]
            v = [[qkv_refs[3 * hh + 2][pl.ds(pl.multiple_of(joff + e * w, w), w), :] for e in halves] for hh in hs]
            crj = [[cr_ref[hh, :, pl.ds(pl.multiple_of(joff + e * w, w), w)] for e in halves] for hh in hs]
            row = lax.broadcasted_iota(jnp.int32, (t, w), 0)
            col = lax.broadcasted_iota(jnp.int32, (t, w), 1)

            def q_step(i, carry, diagonal):
                rows = pl.ds(pl.multiple_of(i * t, t), t)
                q = [qkv_refs[3 * hh][rows, :] for hh in hs]
                dob = [do_ref[rows, cols[hh]].astype(BF16) for hh in hs]
                out = [[None, None] for _ in hs]
                for e in halves:
                    s = [lax.dot_general(q[hh], k[hh][e], _NT, preferred_element_type=F32) * scale - crj[hh][e] for hh in hs]
                    dp = [lax.dot_general(dob[hh], v[hh][e], _NT, preferred_element_type=F32) for hh in hs]
                    if diagonal:
                        s = [jnp.where(col + e * w <= row, s[hh], NEG) for hh in hs]
                    p = [jnp.exp(s[hh] - lse_s[hh, rows, :]) for hh in hs]
                    ds = [p[hh] * (dp[hh] - delta_s[hh, rows, :]) for hh in hs]
                    dsb = [ds[hh].astype(BF16) for hh in hs]
                    dv = [carry[hh][e][1] + lax.dot_general(p[hh].astype(BF16), dob[hh], _TN, preferred_element_type=F32)
                          for hh in hs]
                    dk = [carry[hh][e][0] + lax.dot_general(dsb[hh], q[hh], _TN, preferred_element_type=F32) for hh in hs]
                    for hh in hs:
                        dq_acc[hh, rows, :] += lax.dot_general(dsb[hh], k[hh][e], _NN, preferred_element_type=F32) * scale
                        dcq_ref[hh, rows, :] += jnp.sum(ds[hh], axis=-1, keepdims=True)
                        out[hh][e] = (dk[hh], dv[hh], carry[hh][e][2] + jnp.sum(ds[hh], axis=0, keepdims=True))
                return tuple(tuple(o) for o in out)

            z = jnp.zeros((w, HEAD), F32)
            zero = tuple(tuple((z, z, jnp.zeros((1, w), F32)) for _ in halves) for _ in hs)
            on_diagonal = q_step(j, zero, diagonal=True)
            done = lax.fori_loop(j + 1, n, functools.partial(q_step, diagonal=False), on_diagonal)
            for hh in hs:
                base = 3 * HEAD * hh
                for e, (dk, dv, dc) in enumerate(done[hh]):
                    at = pl.ds(pl.multiple_of(joff + e * w, w), w)
                    dqkv_ref[at, base + HEAD:base + 2 * HEAD] = (dk * scale).astype(BF16)
                    dqkv_ref[at, base + 2 * HEAD:base + 3 * HEAD] = dv.astype(BF16)
                    dcr_ref[hh, :, at] = -dc
            return c

        lax.fori_loop(0, n, kv_step, 0)
        for hh in hs:
            dqkv_ref[:, 3 * HEAD * hh:3 * HEAD * hh + HEAD] = dq_acc[hh].astype(BF16)

    qkv = []
    for hh in hs:
        for part in range(3):
            qkv.append(pl.BlockSpec((S, HEAD), lambda b, g, hh=hh, part=part: (b, 3 * (hp * g + hh) + part)))
    col_spec = pl.BlockSpec((None, hp, S, 1), lambda b, g: (b, g, 0, 0))
    row_spec = pl.BlockSpec((None, hp, 1, S), lambda b, g: (b, g, 0, 0))
    heads = pl.BlockSpec((S, hp * HEAD), lambda b, g: (b, g))
    return _hosted_call(
        body, rider, name="fox_bwd", grid=(B, H // hp),
        in_specs=qkv + [col_spec, row_spec, heads, col_spec, heads],
        out_specs=[pl.BlockSpec((S, 3 * hp * HEAD), lambda b, g: (b, g)), col_spec, row_spec],
        out_shape=[jax.ShapeDtypeStruct((T, 3 * H * HEAD), BF16), jax.ShapeDtypeStruct((B, H, S, 1), F32),
                   jax.ShapeDtypeStruct((B, H, 1, S), F32)],
        scratch_shapes=[pltpu.VMEM((hp, S, HEAD), F32), pltpu.VMEM((hp, S, 1), F32), pltpu.VMEM((hp, S, 1), F32)],
        semantics=("parallel", "parallel"),
    )(*([p_fox] * (3 * hp)), c_col, c_row, o, lse, do)


def _small_fn(x, b0, b1, H):
    S = x.shape[0]
    lane = lax.broadcasted_iota(jnp.int32, x.shape, 1)
    z = x + b0
    tail = jnp.log1p(jnp.exp(-jnp.abs(z)))
    softplus = jnp.maximum(z, 0.0) + tail
    logsig = -(jnp.maximum(-z, 0.0) + tail)
    g = -jnp.exp(b1) * softplus
    pre = jnp.where(lane < H, logsig, jnp.where(lane < 2 * H, g, 0.0))
    bl = _tile(S, 256, CHUNK)
    r = lax.broadcasted_iota(jnp.int32, (bl, bl), 0)
    c = lax.broadcasted_iota(jnp.int32, (bl, bl), 1)
    tri = (r >= c).astype(F32)
    tri_chunk = jnp.where((r >= c) & (jnp.right_shift(r, 6) == jnp.right_shift(c, 6)), 1.0, 0.0)
    carry = jnp.zeros((1, x.shape[1]), F32)
    parts = []
    for i in range(S // bl):
        blk = pre[i * bl:(i + 1) * bl, :]
        full = lax.dot_general(tri, blk, _NN, precision=EXACT, preferred_element_type=F32) + carry
        chunked = lax.dot_general(tri_chunk, blk, _NN, precision=EXACT, preferred_element_type=F32)
        parts.append(jnp.where(lane[:bl] < H, full, chunked))
        carry = carry + jnp.sum(blk, axis=0, keepdims=True)
    cum = parts[0] if len(parts) == 1 else jnp.concatenate(parts, axis=0)
    return jnp.where(lane < 2 * H, cum, jnp.where(lane < 3 * H, _sigmoid(x), 0.0))


def small_fwd(p_small, prm, B, S, H):
    T = B * S

    def body(x_ref, p_ref, o_ref):
        o_ref[...] = _small_fn(x_ref[...], p_ref[0:1, :], p_ref[1:2, :], H)

    blk = pl.BlockSpec((S, 128), lambda b: (b, 0))
    return pl.pallas_call(
        body, name="small_fwd", grid=(B,), in_specs=[blk, pl.BlockSpec((8, 128), lambda b: (0, 0))], out_specs=blk,
        out_shape=jax.ShapeDtypeStruct((T, 128), F32), compiler_params=_params("parallel"),
    )(p_small, prm)


def small_bwd(p_small, prm, d_out, B, S, H):
    T = B * S

    def body(x_ref, p_ref, d_ref, dx_ref, dp_ref):
        @pl.when(pl.program_id(0) == 0)
        def _():
            dp_ref[...] = jnp.zeros_like(dp_ref)

        _, vjp = jax.vjp(functools.partial(_small_fn, H=H), x_ref[...], p_ref[0:1, :], p_ref[1:2, :])
        dx, db0, db1 = vjp(d_ref[...])
        dx_ref[...] = dx.astype(BF16)
        dp_ref[0:1, :] += db0
        dp_ref[1:2, :] += db1

    blk = pl.BlockSpec((S, 128), lambda b: (b, 0))
    pblk = pl.BlockSpec((8, 128), lambda b: (0, 0))
    return pl.pallas_call(
        body, name="small_bwd", grid=(B,), in_specs=[blk, pblk, blk], out_specs=[blk, pblk],
        out_shape=[jax.ShapeDtypeStruct((T, 128), BF16), jax.ShapeDtypeStruct((8, 128), F32)],
        compiler_params=_params("arbitrary"),
    )(p_small, prm, d_out)


def gdn_prep_fwd(p_gqkv, cw, B, S, H):
    T = B * S

    def body(x_ref, w_ref, o_ref):
        for part in range(3):
            cols = slice(part * HEAD, (part + 1) * HEAD)
            y = _conv_fwd(x_ref[:, cols], w_ref.at[:, cols], GDN_CONV)
            a = y * _sigmoid(y)
            if part < 2:
                a = a * lax.rsqrt(jnp.sum(a * a, axis=-1, keepdims=True) + EPS)
            o_ref[:, cols] = a

    blk = pl.BlockSpec((S, 3 * HEAD), lambda b, h: (b, h))
    wblk = pl.BlockSpec((GDN_CONV, 3 * HEAD), lambda b, h: (0, h))
    return pl.pallas_call(
        body, name="gdn_prep_fwd", grid=(B, H), in_specs=[blk, wblk], out_specs=blk,
        out_shape=jax.ShapeDtypeStruct((T, 3 * H * HEAD), F32), compiler_params=_params("parallel", "parallel"),
    )(p_gqkv, cw)


def gdn_prep_bwd(p_gqkv, cw, d_out, B, S, H):
    T = B * S

    def body(x_ref, w_ref, d_ref, dx_ref, dw_ref):
        @pl.when(pl.program_id(1) == 0)
        def _():
            dw_ref[...] = jnp.zeros_like(dw_ref)

        x = x_ref[...]
        y, shifted = _conv_fwd(x, w_ref, GDN_CONV, keep_shifted=True)
        sg = _sigmoid(y)
        a = y * sg
        rs = lax.rsqrt(jnp.sum(a * a, axis=-1, keepdims=True) + EPS)
        d = d_ref[...]
        out = a * rs
        da_qk = rs * (d - out * jnp.sum(d * out, axis=-1, keepdims=True))
        is_qk = (pl.program_id(0) % 3) < 2
        da = jnp.where(is_qk, da_qk, d)
        dy = da * (sg + y * sg * (1.0 - sg))
        dx, dws = _conv_bwd(x, shifted, dy, w_ref, GDN_CONV)
        dx_ref[...] = dx.astype(BF16)
        for i in range(GDN_CONV):
            dw_ref[i:i + 1, :] += dws[i]

    blk = pl.BlockSpec((S, HEAD), lambda n, b: (b, n))
    wblk = pl.BlockSpec((GDN_CONV, HEAD), lambda n, b: (0, n))
    return pl.pallas_call(
        body, name="gdn_prep_bwd", grid=(3 * H, B), in_specs=[blk, wblk, blk], out_specs=[blk, wblk],
        out_shape=[jax.ShapeDtypeStruct((T, 3 * H * HEAD), BF16), jax.ShapeDtypeStruct((GDN_CONV, 3 * H * HEAD), F32)],
        compiler_params=_params("parallel", "arbitrary"),
    )(p_gqkv, cw, d_out)


@jax.custom_vjp
def _given_inverse(a, t):
    return t


def _given_inverse_fwd(a, t):
    return t, t


def _given_inverse_bwd(t, g):
    x = _dg(t, g, _TN, True)
    return -_dg(x, t, _NT, True), jnp.zeros_like(t)


_given_inverse.defvjp(_given_inverse_fwd, _given_inverse_bwd)


def _to_col(row):
    n = row.shape[1]
    r = lax.broadcasted_iota(jnp.int32, (n, n), 0)
    c = lax.broadcasted_iota(jnp.int32, (n, n), 1)
    return jnp.sum(jnp.where(r == c, row, 0.0), axis=1, keepdims=True)


def _intra_fn(k, v, beta_r, gcr, ops, t_known=None):
    n = len(k)
    m = k[0].shape[0]
    r = lax.broadcasted_iota(jnp.int32, (m, m), 0)
    c = lax.broadcasted_iota(jnp.int32, (m, m), 1)
    below = (r > c) & (jnp.right_shift(r, 6) == jnp.right_shift(c, 6))
    beta = [_to_col(beta_r[i]) for i in range(n)]
    gcc = [_to_col(gcr[i]) for i in range(n)]
    decay = [jnp.exp(jnp.where(below, gcc[i] - gcr[i], NEG)) for i in range(n)]
    kb = [k[i] * beta[i] for i in range(n)]
    a = [ops.nt(kb[i], k[i]) * decay[i] for i in range(n)]
    if t_known is None:
        p = [-a[i] for i in range(n)]
        tm = [jnp.where(r == c, 1.0, 0.0) + p[i] for i in range(n)]
        for _ in range(5):
            p = [ops.nn(p[i], p[i], hi=True) for i in range(n)]
            tm = [tm[i] + ops.nn(tm[i], p[i], hi=True) for i in range(n)]
    else:
        tm = [_given_inverse(a[i], t_known[i]) for i in range(n)]
    both = [ops.nn(tm[i], jnp.concatenate([v[i] * beta[i], kb[i] * jnp.exp(gcc[i])], axis=1), hi=True) for i in range(n)]
    u_hat = [both[i][:, :HEAD] for i in range(n)]
    w = [both[i][:, HEAD:] for i in range(n)]
    return tuple(u_hat), tuple(w), tuple(tm)


INTRA_NB = 32
PAIR = 1


def gdn_intra_fwd(qkvn, betar5, gcr5, B, S, H, rider=None):
    T = B * S
    UNIT = PAIR * CHUNK
    N = S // UNIT
    nb = min(INTRA_NB // PAIR, N)
    rows = nb * UNIT
    ns = N // nb

    def body(k_ref, v_ref, b_ref, gr_ref, uh_ref, w_ref, t_ref):
        sls = [slice(ci * UNIT, (ci + 1) * UNIT) for ci in range(nb)]
        u_hat, w, tm = _intra_fn(tuple(k_ref[sl, :] for sl in sls), tuple(v_ref[sl, :] for sl in sls),
                                 tuple(b_ref[ci] for ci in range(nb)), tuple(gr_ref[ci] for ci in range(nb)), _RawOps)
        for ci, sl in enumerate(sls):
            uh_ref[sl, :] = u_hat[ci]
            w_ref[sl, :] = w[ci]
            t_ref[ci] = tm[ci]

    rowspec = pl.BlockSpec((None, None, nb, 1, UNIT), lambda b, h, i: (b, h, i, 0, 0))
    sqspec = pl.BlockSpec((None, None, nb, UNIT, UNIT), lambda b, h, i: (b, h, i, 0, 0))
    out = pl.BlockSpec((rows, HEAD), lambda b, h, i: (b * ns + i, h))
    return _hosted_call(
        body, rider, name="gdn_intra_fwd", grid=(B, H, ns),
        in_specs=[pl.BlockSpec((rows, HEAD), lambda b, h, i: (b * ns + i, 3 * h + 1)),
                  pl.BlockSpec((rows, HEAD), lambda b, h, i: (b * ns + i, 3 * h + 2)),
                  rowspec, rowspec],
        out_specs=[out, out, sqspec],
        out_shape=[jax.ShapeDtypeStruct((T, H * HEAD), F32), jax.ShapeDtypeStruct((T, H * HEAD), F32),
                   jax.ShapeDtypeStruct((B, H, N, UNIT, UNIT), F32)],
        scratch_shapes=[], semantics=("parallel", "parallel", "parallel"),
    )(qkvn, qkvn, betar5, gcr5)


def gdn_intra_bwd(qkvn, betar5, gcr5, t_inv, d_uh, d_w, dq_in, dk_in, B, S, H):
    T = B * S
    UNIT = PAIR * CHUNK
    N = S // UNIT
    nb = min(INTRA_NB // PAIR, N)
    rows = nb * UNIT
    ns = N // nb

    def body(k_ref, v_ref, b_ref, gr_ref, t_ref, duh_ref, dw_ref, dq_ref, dk_ref, o_ref, db_ref, dgr_ref):
        sls = [slice(ci * UNIT, (ci + 1) * UNIT) for ci in range(nb)]
        chunks = range(nb)
        _, vjp = jax.vjp(
            functools.partial(_intra_fn, ops=_DiffOps, t_known=tuple(t_ref[ci] for ci in chunks)),
            tuple(k_ref[sl, :] for sl in sls), tuple(v_ref[sl, :] for sl in sls), tuple(b_ref[ci] for ci in chunks),
            tuple(gr_ref[ci] for ci in chunks))
        zero = jnp.zeros((UNIT, UNIT), F32)
        dk, dv, db, dgr = vjp((tuple(duh_ref[sl, :] for sl in sls), tuple(dw_ref[sl, :] for sl in sls),
                               tuple(zero for _ in chunks)))
        for ci, sl in enumerate(sls):
            o_ref[sl, 0:HEAD] = dq_ref[sl, :]
            o_ref[sl, HEAD:2 * HEAD] = dk[ci] + dk_ref[sl, :]
            o_ref[sl, 2 * HEAD:3 * HEAD] = dv[ci]
            db_ref[ci] = db[ci]
            dgr_ref[ci] = dgr[ci]

    rowspec = pl.BlockSpec((None, None, nb, 1, UNIT), lambda b, h, i: (b, h, i, 0, 0))
    sqspec = pl.BlockSpec((None, None, nb, UNIT, UNIT), lambda b, h, i: (b, h, i, 0, 0))
    head = pl.BlockSpec((rows, HEAD), lambda b, h, i: (b * ns + i, h))
    return pl.pallas_call(
        body, name="gdn_intra_bwd", grid=(B, H, ns),
        in_specs=[pl.BlockSpec((rows, HEAD), lambda b, h, i: (b * ns + i, 3 * h + 1)),
                  pl.BlockSpec((rows, HEAD), lambda b, h, i: (b * ns + i, 3 * h + 2)),
                  rowspec, rowspec, sqspec, head, head, head, head],
        out_specs=[pl.BlockSpec((rows, 3 * HEAD), lambda b, h, i: (b * ns + i, h)), rowspec, rowspec],
        out_shape=[jax.ShapeDtypeStruct((T, 3 * H * HEAD), F32),
                   jax.ShapeDtypeStruct((B, H, N, 1, UNIT), F32), jax.ShapeDtypeStruct((B, H, N, 1, UNIT), F32)],
        compiler_params=_params("parallel", "parallel", "parallel"),
    )(qkvn, qkvn, betar5, gcr5, t_inv, d_uh, d_w, dq_in, dk_in)


def _inter_fn(q, k, u_hat, w, gcr, state, ops):
    n = len(q)
    r = lax.broadcasted_iota(jnp.int32, (CHUNK, CHUNK), 0)
    c = lax.broadcasted_iota(jnp.int32, (CHUNK, CHUNK), 1)
    last = lax.broadcasted_iota(jnp.int32, (1, CHUNK), 1) == CHUNK - 1
    gcc = [_to_col(gcr[i]) for i in range(n)]
    gl = [jnp.sum(jnp.where(last, gcr[i], 0.0), axis=1, keepdims=True) for i in range(n)]
    decay = [jnp.exp(jnp.where(r >= c, gcc[i] - gcr[i], NEG)) for i in range(n)]
    qs = [q[i] * (HEAD ** -0.5) for i in range(n)]
    ws = [ops.nn(w[i], state[i]) for i in range(n)]
    qst = [ops.nn(qs[i] * jnp.exp(gcc[i]), state[i]) for i in range(n)]
    attn = [ops.nt(qs[i], k[i]) * decay[i] for i in range(n)]
    u = [u_hat[i] - ws[i] for i in range(n)]
    o = [qst[i] + ops.nn(attn[i], u[i]) for i in range(n)]
    kdu = [ops.tn(k[i] * jnp.exp(gl[i] - gcc[i]), u[i]) for i in range(n)]
    new_state = [state[i] * jnp.exp(gl[i]) + kdu[i] for i in range(n)]
    return tuple(o), tuple(new_state)


INTER_HEADS = 8
INTER_ROWS = 512
INTER_ROWS_BWD = 256


def _inter_heads(H):
    return INTER_HEADS if H % INTER_HEADS == 0 else (4 if H % 4 == 0 else 1)


def _inter_specs(ts, ns, hp, backward):
    at = (lambda s: ns - 1 - s) if backward else (lambda s: s)
    nc = ts // CHUNK
    qk = []
    for hh in range(hp):
        qk.append(pl.BlockSpec((ts, HEAD), lambda b, g, s, hh=hh: (b * ns + at(s), 3 * (hp * g + hh))))
        qk.append(pl.BlockSpec((ts, HEAD), lambda b, g, s, hh=hh: (b * ns + at(s), 3 * (hp * g + hh) + 1)))
    heads = pl.BlockSpec((ts, hp * HEAD), lambda b, g, s: (b * ns + at(s), g))
    rowspec = pl.BlockSpec((None, hp, nc, 1, CHUNK), lambda b, g, s: (b, g, at(s), 0, 0))
    stspec = pl.BlockSpec((None, hp, nc, HEAD, HEAD), lambda b, g, s: (b, g, at(s), 0, 0))
    return qk, heads, rowspec, stspec


def gdn_inter_fwd(qkvn, u_hat, w, gcr5, p_gz, gnorm, B, S, H):
    T = B * S
    N = S // CHUNK
    hp = _inter_heads(H)
    hs = range(hp)
    ts = _tile(S, INTER_ROWS, CHUNK)
    ns, nc = S // ts, ts // CHUNK

    def body(*refs):
        qk_refs, (uh_ref, w_ref, gr_ref, z_ref, gn_ref, o_ref, st_ref, y_ref, s_scr) = refs[:2 * hp], refs[2 * hp:]

        @pl.when(pl.program_id(2) == 0)
        def _():
            s_scr[...] = jnp.zeros_like(s_scr)

        gn = gn_ref[...]

        def step(n, c):
            rows = pl.ds(pl.multiple_of(n * CHUNK, CHUNK), CHUNK)
            st = tuple(s_scr[hh] for hh in hs)
            for hh in hs:
                st_ref[hh, n] = st[hh]
            o, new = _inter_fn(tuple(qk_refs[2 * hh][rows, :] for hh in hs), tuple(qk_refs[2 * hh + 1][rows, :] for hh in hs),
                               tuple(uh_ref[rows, hh * HEAD:(hh + 1) * HEAD] for hh in hs),
                               tuple(w_ref[rows, hh * HEAD:(hh + 1) * HEAD] for hh in hs),
                               tuple(gr_ref[hh, n] for hh in hs), st, _RawOps)
            for hh in hs:
                cols = slice(hh * HEAD, (hh + 1) * HEAD)
                o_ref[rows, cols] = o[hh]
                s_scr[hh] = new[hh]
                z = z_ref[rows, cols]
                r = lax.rsqrt(jnp.mean(o[hh] * o[hh], axis=-1, keepdims=True) + EPS)
                y_ref[rows, cols] = (o[hh] * r * gn * z * _sigmoid(z)).astype(BF16)
            return c

        lax.fori_loop(0, nc, step, 0)

    qk, heads, rowspec, stspec = _inter_specs(ts, ns, hp, backward=False)
    return pl.pallas_call(
        body, name="gdn_inter_fwd", grid=(B, H // hp, ns),
        in_specs=qk + [heads, heads, rowspec, heads, pl.BlockSpec((1, HEAD), lambda b, g, s: (0, 0))],
        out_specs=[heads, stspec, heads],
        out_shape=[jax.ShapeDtypeStruct((T, H * HEAD), F32), jax.ShapeDtypeStruct((B, H, N, HEAD, HEAD), F32),
                   jax.ShapeDtypeStruct((T, H * HEAD), BF16)],
        scratch_shapes=[pltpu.VMEM((hp, HEAD, HEAD), F32)],
        compiler_params=_params("parallel", "parallel", "arbitrary"),
    )(*([qkvn] * (2 * hp)), u_hat, w, gcr5, p_gz, gnorm)


def gdn_inter_bwd(qkvn, u_hat, w, gcr5, states, o, p_gz, gnorm, d_y, B, S, H, rider=None):
    T = B * S
    N = S // CHUNK
    hp = _inter_heads(H)
    hs = range(hp)
    ts = _tile(S, INTER_ROWS_BWD, CHUNK)
    ns, nc = S // ts, ts // CHUNK

    def body(*refs):
        qk_refs = refs[:2 * hp]
        (uh_ref, w_ref, gr_ref, st_ref, o_ref, z_ref, gn_ref, dy_ref,
         dq_ref, dk_ref, duh_ref, dw_ref, dgr_ref, dz_ref, dgn_ref, ds_scr) = refs[2 * hp:]

        @pl.when(pl.program_id(2) == 0)
        def _():
            ds_scr[...] = jnp.zeros_like(ds_scr)
            dgn_ref[...] = jnp.zeros_like(dgn_ref)

        cols = [slice(hh * HEAD, (hh + 1) * HEAD) for hh in hs]
        gn = gn_ref[...]

        def through_norm(rows, hh):
            ov, z, d = o_ref[rows, cols[hh]], z_ref[rows, cols[hh]], dy_ref[rows, cols[hh]]
            r = lax.rsqrt(jnp.mean(ov * ov, axis=-1, keepdims=True) + EPS)
            xh = ov * r
            sg = _sigmoid(z)
            d_n = d * (z * sg)
            dz_ref[rows, cols[hh]] = (d * xh * gn * (sg + z * sg * (1.0 - sg))).astype(BF16)
            dgn_ref[0:1, :] += jnp.sum(d_n * xh, axis=0, keepdims=True)
            dxh = d_n * gn
            return r * (dxh - xh * jnp.mean(dxh * xh, axis=-1, keepdims=True))

        def step(i, c):
            n = nc - 1 - i
            rows = pl.ds(pl.multiple_of(n * CHUNK, CHUNK), CHUNK)
            _, vjp = jax.vjp(functools.partial(_inter_fn, ops=_DiffOps),
                             tuple(qk_refs[2 * hh][rows, :] for hh in hs), tuple(qk_refs[2 * hh + 1][rows, :] for hh in hs),
                             tuple(uh_ref[rows, cols[hh]] for hh in hs), tuple(w_ref[rows, cols[hh]] for hh in hs),
                             tuple(gr_ref[hh, n] for hh in hs), tuple(st_ref[hh, n] for hh in hs))
            dq, dk, duh, dw, dgr, ds = vjp((tuple(through_norm(rows, hh) for hh in hs), tuple(ds_scr[hh] for hh in hs)))
            for hh in hs:
                dq_ref[rows, cols[hh]] = dq[hh]
                dk_ref[rows, cols[hh]] = dk[hh]
                duh_ref[rows, cols[hh]] = duh[hh]
                dw_ref[rows, cols[hh]] = dw[hh]
                dgr_ref[hh, n] = dgr[hh]
                ds_scr[hh] = ds[hh]
            return c

        lax.fori_loop(0, nc, step, 0)

    qk, heads, rowspec, stspec = _inter_specs(ts, ns, hp, backward=True)
    hshape = jax.ShapeDtypeStruct((T, H * HEAD), F32)
    return _hosted_call(
        body, rider, name="gdn_inter_bwd", grid=(B, H // hp, ns),
        in_specs=qk + [heads, heads, rowspec, stspec, heads, heads, pl.BlockSpec((1, HEAD), lambda b, g, s: (0, 0)), heads],
        out_specs=[heads, heads, heads, heads, rowspec, heads,
                   pl.BlockSpec((None, None, 8, HEAD), lambda b, g, s: (b, g, 0, 0))],
        out_shape=[hshape, hshape, hshape, hshape, jax.ShapeDtypeStruct((B, H, N, 1, CHUNK), F32),
                   jax.ShapeDtypeStruct((T, H * HEAD), BF16), jax.ShapeDtypeStruct((B, H // hp, 8, HEAD), F32)],
        scratch_shapes=[pltpu.VMEM((hp, HEAD, HEAD), F32)], semantics=("parallel", "parallel", "arbitrary"),
    )(*([qkvn] * (2 * hp)), u_hat, w, gcr5, states, o, p_gz, gnorm, d_y)


def adamw(w, g, m, v, name):
    shape = w.shape
    lead = (None,) * (w.ndim - 2)
    zeros = (0,) * (w.ndim - 2)
    R, C = shape[-2:]
    g2 = g.reshape(R, C)
    tr, tc = _tile(R, 128, 8), C
    if tr % 8 and R > 8:
        tr, tc = R, _tile(C, 128)

    def body(w_ref, g_ref, m_ref, v_ref, d_ref, nm_ref, nv_ref):
        gv = g_ref[...]
        nm = ADAM_B1 * m_ref[...] + (1.0 - ADAM_B1) * gv
        nv = ADAM_B2 * v_ref[...] + (1.0 - ADAM_B2) * (gv * gv)
        m_hat = nm / (1.0 - ADAM_B1 ** ADAM_STEP)
        v_hat = nv / (1.0 - ADAM_B2 ** ADAM_STEP)
        d_ref[...] = -ADAM_LR * (m_hat / (jnp.sqrt(v_hat) + ADAM_EPS) + ADAM_WD * w_ref[...])
        nm_ref[...] = nm
        nv_ref[...] = nv

    blk = pl.BlockSpec(lead + (tr, tc), lambda i, j: zeros + (i, j))
    gblk = pl.BlockSpec((tr, tc), lambda i, j: (i, j))
    sh = jax.ShapeDtypeStruct(shape, F32)
    return pl.pallas_call(
        body, name=name, grid=(R // tr, C // tc), in_specs=[blk, gblk, blk, blk], out_specs=[blk] * 3, out_shape=[sh] * 3,
        compiler_params=_params("parallel", "parallel"),
    )(w, g2, m, v)


def _place():
    x, y, c = lax.axis_index("x"), lax.axis_index("y"), lax.axis_index("c")
    chips = [(1 - x, y), (x, 1 - y), (1 - x, 1 - y)]
    return x, y, c, chips


_HBM = pl.BlockSpec(memory_space=pltpu.HBM)


class _Rider:
    def __init__(self, inputs, out_shapes, n_sems, sends, recvs, aliases=None):
        self.inputs, self.out_shapes, self.n_sems = list(inputs), list(out_shapes), n_sems
        self.sends, self.recvs, self.aliases = sends, recvs, aliases or {}

    def start(self, *refs):
        for cp in self.sends(*refs):
            cp.start()

    def wait(self, *refs):
        for cp in self.recvs(*refs):
            cp.wait_recv()
        for cp in self.sends(*refs):
            cp.wait_send()


def _remote(src, dst, send_sems, recv_sems, k, to):
    return pltpu.make_async_remote_copy(src_ref=src, dst_ref=dst, send_sem=send_sems.at[k], recv_sem=recv_sems.at[k],
                                        device_id=to, device_id_type=MESH)


def _run_alone(rider, name):
    ri = len(rider.inputs)

    def body(*refs):
        ins, outs, (send_sems, recv_sems) = refs[:ri], refs[ri:-2], refs[-2:]
        rider.start(ins, outs, send_sems, recv_sems)
        rider.wait(ins, outs, send_sems, recv_sems)

    return pl.pallas_call(
        body, name=name, in_specs=[_HBM] * ri, out_specs=[_HBM] * len(rider.out_shapes), out_shape=rider.out_shapes,
        scratch_shapes=[pltpu.SemaphoreType.DMA((rider.n_sems,))] * 2, input_output_aliases=rider.aliases,
    )(*rider.inputs)


def _hosted_call(body, rider, *, name, grid, in_specs, out_specs, out_shape, scratch_shapes, semantics):
    if rider is None:
        return pl.pallas_call(body, name=name, grid=grid, in_specs=in_specs, out_specs=out_specs, out_shape=out_shape,
                              scratch_shapes=scratch_shapes, compiler_params=_params(*semantics))
    n_in, n_out, n_scr = len(in_specs), len(out_specs), len(scratch_shapes)
    ri, ro = len(rider.inputs), len(rider.out_shapes)

    def hosted(*refs):
        parts, p = [], 0
        for cnt in (n_in, ri, n_out, ro, n_scr, 2):
            parts.append(refs[p:p + cnt])
            p += cnt
        ins, rins, outs, routs, scr, (send_sems, recv_sems) = parts
        first = functools.reduce(jnp.logical_and, [pl.program_id(a) == 0 for a in range(len(grid))])
        last = functools.reduce(jnp.logical_and, [pl.program_id(a) == grid[a] - 1 for a in range(len(grid))])

        @pl.when(first)
        def _():
            rider.start(rins, routs, send_sems, recv_sems)

        body(*ins, *outs, *scr)

        @pl.when(last)
        def _():
            rider.wait(rins, routs, send_sems, recv_sems)

    call = pl.pallas_call(
        hosted, name=name, grid=grid, in_specs=list(in_specs) + [_HBM] * ri, out_specs=list(out_specs) + [_HBM] * ro,
        out_shape=list(out_shape) + rider.out_shapes,
        scratch_shapes=list(scratch_shapes) + [pltpu.SemaphoreType.DMA((rider.n_sems,))] * 2,
        input_output_aliases={n_in + i: n_out + o for i, o in rider.aliases.items()},
        compiler_params=_params(*(("arbitrary",) * len(grid))))

    def run(*args):
        res = call(*args, *rider.inputs)
        return res[:n_out], res[n_out:]

    return run


def _ride_gather_ici(packs):
    n = len(packs)

    def sends(ins, outs, send_sems, recv_sems):
        x, y, c, chips = _place()
        return [_remote(ins[a].at[c], outs[a].at[2 * x + y, c], send_sems, recv_sems, 3 * a + j, (*chip, c))
                for a in range(n) for j, chip in enumerate(chips)]

    def recvs(ins, outs, send_sems, recv_sems):
        x, y, c, chips = _place()
        return [_remote(ins[a].at[c], outs[a].at[2 * chip[0] + chip[1], c], send_sems, recv_sems, 3 * a + j, (x, y, c))
                for a in range(n) for j, chip in enumerate(chips)]

    return _Rider(packs, [jax.ShapeDtypeStruct((N_CHIP,) + p.shape, p.dtype) for p in packs], 3 * n, sends, recvs)


def _ride_gather_d2d(gathered):
    n = len(gathered)

    def copies(landing_half, to):
        def build(ins, outs, send_sems, recv_sems):
            x, y, c, chips = _place()
            return [_remote(ins[a].at[2 * chip[0] + chip[1], c], outs[a].at[2 * chip[0] + chip[1], landing_half(c)],
                            send_sems, recv_sems, 3 * a + j, to(x, y, c))
                    for a in range(n) for j, chip in enumerate(chips)]
        return build

    return _Rider(gathered, [jax.ShapeDtypeStruct(g.shape, g.dtype) for g in gathered], 3 * n,
                  copies(lambda c: c, lambda x, y, c: (x, y, 1 - c)), copies(lambda c: 1 - c, lambda x, y, c: (x, y, c)),
                  aliases={a: a for a in range(n)})


def _ride_exchange(gs):
    n = len(gs)
    shapes = [g.shape[1:] if g.ndim == 4 else g.shape[:2] + (g.shape[2] // 2,) for g in gs]

    def copies(ins, outs, send_sems, recv_sems):
        x, y, c, _ = _place()

        def theirs(a):
            if gs[a].ndim == 4:
                return ins[a].at[1 - c]
            cols = shapes[a][2]
            return ins[a].at[:, :, pl.ds((1 - c) * cols, cols)]

        return [_remote(theirs(a), outs[a], send_sems, recv_sems, a, (x, y, 1 - c)) for a in range(n)]

    return _Rider(gs, [jax.ShapeDtypeStruct(sh, g.dtype) for sh, g in zip(shapes, gs)], n, copies, copies)


def _ride_scatter(b16s, to=(0, 1, 2), landing=None):
    n = len(b16s)

    def sends(ins, outs, send_sems, recv_sems):
        x, y, c, chips = _place()
        return [_remote(ins[a].at[2 * chips[j][0] + chips[j][1]], outs[a].at[2 * x + y], send_sems, recv_sems, 3 * a + j,
                        (*chips[j], c)) for a in range(n) for j in to]

    def recvs(ins, outs, send_sems, recv_sems):
        x, y, c, chips = _place()
        return [_remote(ins[a].at[2 * x + y], outs[a].at[2 * chips[j][0] + chips[j][1]], send_sems, recv_sems, 3 * a + j,
                        (x, y, c)) for a in range(n) for j in to]

    return _Rider(list(b16s) + list(landing or []), [jax.ShapeDtypeStruct(b.shape, b.dtype) for b in b16s], 3 * n,
                  sends, recvs, aliases={n + a: a for a in range(n)} if landing else None)


def _slab_tile(r, cols):
    tr = _tile(r, 256, 16)
    if tr % 16 == 0:
        return tr, cols
    return r, _tile(cols, 128)


def add_halves(g, got, idx, name):
    ns, r, cols = got.shape
    tr, tc = _slab_tile(r, cols)
    if g.ndim == 4:
        mine = pl.BlockSpec((None, None, tr, tc), lambda s, i, j, idx_ref: (idx_ref[0], s, i, j))
    else:
        mine = pl.BlockSpec((None, tr, tc), lambda s, i, j, idx_ref: (s, i, idx_ref[0] * (cols // tc) + j))

    def body(idx_ref, a_ref, b_ref, o32_ref, o16_ref):
        s = a_ref[...] + b_ref[...]
        o32_ref[...] = s
        o16_ref[...] = s.astype(BF16)

    blk = pl.BlockSpec((None, tr, tc), lambda s, i, j, idx_ref: (s, i, j))
    return pl.pallas_call(
        body, name=name,
        grid_spec=pltpu.PrefetchScalarGridSpec(
            num_scalar_prefetch=1, grid=(ns, r // tr, cols // tc),
            in_specs=[mine, blk], out_specs=[blk, blk]),
        out_shape=[jax.ShapeDtypeStruct((ns, r, cols), F32), jax.ShapeDtypeStruct((ns, r, cols), BF16)],
        compiler_params=_params("parallel", "parallel", "parallel"),
    )(idx, g, got)


def add_chips(a32, got16, idx, name):
    ns, r, cols = a32.shape
    tr, tc = _slab_tile(r, cols)

    def body(idx_ref, a_ref, r1_ref, r2_ref, r3_ref, o_ref):
        o_ref[...] = ((a_ref[...] + r1_ref[...].astype(F32)) + r2_ref[...].astype(F32)) + r3_ref[...].astype(F32)

    def slab(k):
        return pl.BlockSpec((None, tr, tc), lambda i, j, idx_ref: ((idx_ref[1] + k) % ns, i, j))

    return pl.pallas_call(
        body, name=name,
        grid_spec=pltpu.PrefetchScalarGridSpec(
            num_scalar_prefetch=1, grid=(r // tr, cols // tc), in_specs=[slab(0), slab(1), slab(2), slab(3)],
            out_specs=pl.BlockSpec((tr, tc), lambda i, j, idx_ref: (i, j))),
        out_shape=jax.ShapeDtypeStruct((r, cols), F32),
        compiler_params=_params("parallel", "parallel"),
    )(idx, a32, got16, got16, got16)


def share_halves(halves):
    n = len(halves)

    def body(*refs):
        in_refs, out_refs, (send_sems, recv_sems) = refs[:n], refs[n:2 * n], refs[2 * n:]
        x, y, c, _ = _place()
        cps = [pltpu.make_async_remote_copy(src_ref=in_refs[a], dst_ref=out_refs[a], send_sem=send_sems.at[a],
                                            recv_sem=recv_sems.at[a], device_id=(x, y, 1 - c), device_id_type=MESH)
               for a in range(n)]
        for cp in cps:
            cp.start()
        for cp in cps:
            cp.wait()

    return pl.pallas_call(
        body, name="share_halves", in_specs=[_HBM] * n, out_specs=[_HBM] * n,
        out_shape=[jax.ShapeDtypeStruct(h.shape, F32) for h in halves],
        scratch_shapes=[pltpu.SemaphoreType.DMA((n,)), pltpu.SemaphoreType.DMA((n,))],
    )(*halves)


def allreduce_small(v):
    R, _ = v.shape

    def body(in_ref, out_ref, slots, send_sems, recv_sems):
        x, y, c, _ = _place()
        me = 4 * x + 2 * y + c
        slots[me] = in_ref[...]
        cps = []
        for k in range(1, N_DEV):
            to = (x ^ (k >> 2), y ^ ((k >> 1) & 1), c ^ (k & 1))
            cps.append(pltpu.make_async_remote_copy(src_ref=in_ref, dst_ref=slots.at[me], send_sem=send_sems.at[k - 1],
                                                    recv_sem=recv_sems.at[k - 1], device_id=to, device_id_type=MESH))
        for cp in cps:
            cp.start()
        for k in range(1, N_DEV):
            frm = 4 * (x ^ (k >> 2)) + 2 * (y ^ ((k >> 1) & 1)) + (c ^ (k & 1))
            pltpu.make_async_remote_copy(src_ref=in_ref, dst_ref=slots.at[frm], send_sem=send_sems.at[k - 1],
                                         recv_sem=recv_sems.at[k - 1], device_id=(x, y, c), device_id_type=MESH).wait_recv()
        for cp in cps:
            cp.wait_send()
        acc = slots[0]
        for d in range(1, N_DEV):
            acc = acc + slots[d]
        out_ref[...] = acc

    vm = pl.BlockSpec(memory_space=pltpu.VMEM)
    return pl.pallas_call(
        body, name="allreduce_small", in_specs=[vm], out_specs=vm, out_shape=jax.ShapeDtypeStruct((R, ROW), F32),
        scratch_shapes=[pltpu.VMEM((N_DEV, R, ROW), F32), pltpu.SemaphoreType.DMA((N_DEV - 1,)),
                        pltpu.SemaphoreType.DMA((N_DEV - 1,))],
    )(v)


def _rows_of(n, unit=16):
    return -(-n // (unit * ROW)) * unit


def _pack_rows(items, total_rows, dtype, unit=16):
    parts = []
    used = 0
    for a in items:
        flat = a.reshape(-1)
        r = _rows_of(flat.shape[0], unit)
        flat = jnp.pad(flat, (0, r * ROW - flat.shape[0]))
        parts.append(flat.reshape(r, ROW))
        used += r
    if total_rows > used:
        parts.append(jnp.zeros((total_rows - used, ROW), dtype))
    return jnp.concatenate(parts, axis=0)


def _unpack_rows(buf, shapes, unit=16):
    lead = buf.shape[:-2]
    out = []
    off = 0
    for shp in shapes:
        n = math.prod(shp)
        r = _rows_of(n, unit)
        piece = buf[..., off:off + r, :].reshape(*lead, r * ROW)[..., :n].reshape(*lead, *shp)
        out.append(piece)
        off += r
    return out


def _interleave_heads(w, H):
    lead = w.shape[:-1]
    return w.reshape(*lead, 3, H, HEAD).swapaxes(-3, -2).reshape(*lead, 3 * H * HEAD)


def _deinterleave_heads(w, H):
    lead = w.shape[:-1]
    return w.reshape(*lead, H, 3, HEAD).swapaxes(-3, -2).reshape(*lead, 3 * H * HEAD)


def _interleave_head_rows(w, H):
    return w.reshape(3, H, HEAD, w.shape[-1]).swapaxes(0, 1).reshape(3 * H * HEAD, w.shape[-1])


def _deinterleave_head_rows(w, H):
    return w.reshape(H, 3, HEAD, w.shape[-1]).swapaxes(0, 1).reshape(3 * H * HEAD, w.shape[-1])


def kernel(x, norm_mix, w_in, fox_f_bias, gdn_conv_w, gdn_a_log, gdn_dt_bias, gdn_norm, w_branch_fox, w_branch_gdn, w_out, norm_ffn, w_up, ffn_conv_w, w_down, norm_final, loss_target, m_norm_mix, m_w_in, m_fox_f_bias, m_gdn_conv_w, m_gdn_a_log, m_gdn_dt_bias, m_gdn_norm, m_w_branch_fox, m_w_branch_gdn, m_w_out, m_norm_ffn, m_w_up, m_ffn_conv_w, m_w_down, m_norm_final, v_norm_mix, v_w_in, v_fox_f_bias, v_gdn_conv_w, v_gdn_a_log, v_gdn_dt_bias, v_gdn_norm, v_w_branch_fox, v_w_branch_gdn, v_w_out, v_norm_ffn, v_w_up, v_ffn_conv_w, v_w_down, v_norm_final):
    B, S, D = x.shape
    T = B * S
    H = D // HEAD
    N = S // CHUNK
    FF = w_down.shape[1] * N_CHIP
    d_in = 9 * D + 3 * H
    assert w_in.shape[2] * N_CHIP == d_in and 3 * H <= 128

    cidx = lax.axis_index("c").astype(jnp.int32)
    sidx = (2 * lax.axis_index("x") + lax.axis_index("y")).astype(jnp.int32)
    idx = jnp.stack([cidx, sidx])

    rowed = [w_branch_fox[0], w_branch_gdn[0], w_out[0], w_down[0]]
    convs = [gdn_conv_w[0], ffn_conv_w[0]]
    rowed_shapes = [a.shape for a in rowed]
    conv_shapes = [a.shape + (2,) for a in convs]
    pad_rows = lambda shapes: -(-sum(_rows_of(math.prod(s)) for s in shapes) // 256) * 128
    Rh, Rc = pad_rows(rowed_shapes), pad_rows(conv_shapes)
    halves = lambda a: a.reshape(2, a.shape[0] // 2, a.shape[1])
    c_in = w_in.shape[2]
    packs_a = [w_in[0].T.astype(BF16).reshape(c_in, 2, D // 2).transpose(1, 0, 2),
               halves(_pack_rows([lax.bitcast_convert_type(a, BF16) for a in convs], 2 * Rc, BF16))]
    packs_b = [halves(w_up[0].astype(BF16)), halves(_pack_rows([a.astype(BF16) for a in rowed], 2 * Rh, BF16))]
    own = lambda gs, ps: [lax.dynamic_update_slice(g, p[None], (sidx, 0, 0, 0)) for g, p in zip(gs, ps)]
    by_cols = lambda g: g.transpose(1, 2, 0, 3).reshape(2 * g.shape[2], N_CHIP * g.shape[3])
    cat_cols = lambda p: jnp.concatenate([p[i] for i in range(N_CHIP)], axis=-1)
    cat_rows = lambda p: p.reshape(-1, p.shape[-1])
    x2 = x.reshape(T, D)
    hn1, landed = rmsnorm_fwd(x2, norm_mix, "rmsnorm_mix", rider=_ride_gather_ici(packs_a))
    g_in, g_conv = own(_run_alone(_ride_gather_d2d(landed), "gather_to_sibling"), packs_a)
    W_inT = g_in.transpose(0, 2, 1, 3).reshape(N_CHIP * c_in, D)
    conv_parts = _unpack_rows(g_conv.reshape(N_CHIP, 2 * Rc, ROW), conv_shapes)
    gconv = cat_cols(lax.bitcast_convert_type(conv_parts[0], F32))
    fconv = cat_cols(lax.bitcast_convert_type(conv_parts[1], F32))

    o1, o2 = 3 * D, 3 * D + H
    o3, o4, o5, o6 = o2 + 3 * D, o2 + 3 * D + H, o2 + 3 * D + 2 * H, o2 + 4 * D + 2 * H
    W_foxT = _interleave_head_rows(W_inT[:o1], H)
    W_gqkvT = _interleave_head_rows(W_inT[o2:o3], H)
    W_gzT = W_inT[o5:o6]
    W_gatesT = W_inT[o6:]
    W_smallT = jnp.concatenate([W_inT[o1:o2], W_inT[o3:o5], jnp.zeros((128 - 3 * H, D), BF16)], axis=0)
    gconv_i = _interleave_heads(gconv, H)
    fconv_g, fconv_v = fconv[:, :FF], fconv[:, FF:]
    prm = jnp.zeros((8, 128), F32)
    prm = prm.at[0, 0:H].set(fox_f_bias[0]).at[0, H:2 * H].set(gdn_dt_bias[0]).at[1, H:2 * H].set(gdn_a_log[0])

    tgt = loss_target.reshape(T, D)

    p_fox = matmul(hn1, W_foxT, "nt", "proj_fox", out_dtype=BF16)
    p_gqkv = matmul(hn1, W_gqkvT, "nt", "proj_gqkv")
    p_gz = matmul(hn1, W_gzT, "nt", "proj_gz")
    p_gates = matmul(hn1, W_gatesT, "nt", "proj_gates")
    p_small = matmul(hn1, W_smallT, "nt", "proj_small")

    sm = small_fwd(p_small, prm, B, S, H)
    heads = lambda a: a.reshape(B, S, H).transpose(0, 2, 1)
    c_bhs, gc_bhs, beta_bhs = heads(sm[:, 0:H]), heads(sm[:, H:2 * H]), heads(sm[:, 2 * H:3 * H])
    c_col, c_row = c_bhs[..., None], c_bhs[:, :, None, :]
    gcr5 = gc_bhs.reshape(B, H, N, 1, CHUNK)
    gcr_u = gc_bhs.reshape(B, H, N // PAIR, 1, PAIR * CHUNK)
    betar_u = beta_bhs.reshape(B, H, N // PAIR, 1, PAIR * CHUNK)

    (o_fox, o_fox16, lse), arriving = fox_fwd(p_fox, c_col, c_row, B, S, H, rider=_ride_gather_ici(packs_b))
    qkvn = gdn_prep_fwd(p_gqkv, gconv_i, B, S, H)
    (u_hat, w_t, t_inv), arrived = gdn_intra_fwd(qkvn, betar_u, gcr_u, B, S, H, rider=_ride_gather_d2d(arriving))
    g_up, g_rowed = own(arrived, packs_b)
    W_up = by_cols(g_up)
    W_up_g, W_up_v = W_up[:, :FF], W_up[:, FF:]
    W_bf, W_bg, W_out, W_down = (cat_rows(p) for p in _unpack_rows(g_rowed.reshape(N_CHIP, 2 * Rh, ROW), rowed_shapes))
    o_gdn, states, y_gdn = gdn_inter_fwd(qkvn, u_hat, w_t, gcr5, p_gz, gdn_norm, B, S, H)
    bf_ = matmul(o_fox16, W_bf, "nn", "branch_fox")
    bg_, y = matmul(y_gdn, W_bg, "nn", "branch_gdn", post=_post_merge(p_gates, bf_))
    h1, hn2 = matmul(y, W_out, "nn", "out_proj", add=x2, post=_post_rmsnorm(norm_ffn))
    up_g = matmul(hn2, W_up_g, "nn", "up_gate")
    up_v = matmul(hn2, W_up_v, "nn", "up_val")
    act = ffn_gate_fwd(up_g, up_v, fconv_g, fconv_v, B, S)
    dh2, dh2_16, loss_cols, d_norm_final = matmul(act, W_down, "nn", "down_proj", add=h1,
                                                  post=_post_loss(norm_final.reshape(1, D), tgt))
    loss_here = (0.5 / D) * jnp.sum(loss_cols)

    d_act = matmul(dh2_16, W_down, "nt", "d_act")
    dW_down = matmul(act, dh2_16, "tn", "dw_down")
    d_upg, d_upv, d_fconv_g, d_fconv_v = ffn_gate_bwd(up_g, up_v, fconv_g, fconv_v, d_act, B, S)
    d_hn2 = matmul(d_upg, W_up_g, "nt", "d_hn2_g")
    dh1, dh1_16, d_norm_ffn = matmul(d_upv, W_up_v, "nt", "d_hn2_v", add=d_hn2,
                                     post=_post_rmsnorm_bwd(h1, norm_ffn, dh2, True))
    dW_up_slabs = jnp.concatenate([matmul(hn2, d_upg, "tn", "dw_up_g", slab=(D // 2, 2 * FF // N_CHIP)),
                                   matmul(hn2, d_upv, "tn", "dw_up_v", slab=(D // 2, 2 * FF // N_CHIP))], axis=1)
    d_bf, d_bg, d_gates = matmul(dh1_16, W_out, "nt", "d_y", post=_post_merge_bwd(p_gates, bf_, bg_))
    dW_out = matmul(y, dh1_16, "tn", "dw_out")
    d_ofox = matmul(d_bf, W_bf, "nt", "d_ofox")
    dW_bf = matmul(o_fox16, d_bf, "tn", "dw_bf")
    d_ygdn = matmul(d_bg, W_bg, "nt", "d_ygdn")
    dW_bg = matmul(y_gdn, d_bg, "tn", "dw_bg")

    d_fconv = jnp.concatenate([d_fconv_g, d_fconv_v], axis=1)
    col_shard = lambda g, s: g[:, s * (g.shape[1] // N_CHIP):(s + 1) * (g.shape[1] // N_CHIP)]
    row_shard = lambda g, s: g[s * (g.shape[0] // N_CHIP):(s + 1) * (g.shape[0] // N_CHIP)]
    shard_items = lambda s: [row_shard(dW_bf, s), row_shard(dW_bg, s), row_shard(dW_out, s), row_shard(dW_down, s),
                             col_shard(d_fconv, s)]
    g_shapes = [a.shape for a in shard_items(0)]
    assert sum(_rows_of(math.prod(s)) for s in g_shapes) <= 2 * Rh
    to_slabs = lambda g: g.reshape(2, g.shape[0] // 2, N_CHIP, g.shape[1] // N_CHIP).transpose(0, 2, 1, 3)
    gpacks_b = [dW_up_slabs,
                jnp.stack([_pack_rows(shard_items(s), 2 * Rh, F32).reshape(2, Rh, ROW) for s in range(N_CHIP)], axis=1)]
    (d_pfox, d_ccol, d_crow), gots_b = fox_bwd(p_fox, c_col, c_row, o_fox, lse, d_ofox, B, S, H,
                                              rider=_ride_exchange(gpacks_b))
    sums_b = [add_halves(g, got, idx, "add_halves_b%d" % i) for i, (g, got) in enumerate(zip(gpacks_b, gots_b))]

    (dq_i, dk_i, d_uh, d_wt, dgcr_a, d_gz, d_gn_parts), got16_b = gdn_inter_bwd(
        qkvn, u_hat, w_t, gcr5, states, o_gdn, p_gz, gdn_norm, d_ygdn, B, S, H,
        rider=_ride_scatter([s16 for _, s16 in sums_b]))
    d_gdn_norm = jnp.sum(d_gn_parts[:, :, 0, :], axis=(0, 1))[None]
    mine_b = [add_chips(s32, g16, idx, "add_chips_b%d" % i) for i, ((s32, _), g16) in enumerate(zip(sums_b, got16_b))]
    d_qkvn, d_betar5, dgcr_b = gdn_intra_bwd(qkvn, betar_u, gcr_u, t_inv, d_uh, d_wt, dq_i, dk_i, B, S, H)
    d_pgqkv, d_gconv_i = gdn_prep_bwd(p_gqkv, gconv_i, d_qkvn, B, S, H)

    tokens = lambda a: a.reshape(B, H, S).transpose(0, 2, 1).reshape(T, H)
    d_gc = dgcr_a.reshape(B, H, S) + dgcr_b.reshape(B, H, S)
    d_sm = jnp.concatenate([tokens(d_ccol.reshape(B, H, S) + d_crow.reshape(B, H, S)), tokens(d_gc), tokens(d_betar5.reshape(B, H, S)),
                            jnp.zeros((T, 128 - 3 * H), F32)], axis=1)
    d_psmall, d_prm = small_bwd(p_small, prm, d_sm, B, S, H)

    dW_foxT = matmul(d_pfox, hn1, "tn", "dw_fox")
    dW_gqkvT = matmul(d_pgqkv, hn1, "tn", "dw_gqkv")
    dW_gzT = matmul(d_gz, hn1, "tn", "dw_gz")
    dW_gatesT = matmul(d_gates, hn1, "tn", "dw_gates")
    dW_smallT = matmul(d_psmall, hn1, "tn", "dw_small")
    dW_inT = jnp.concatenate([_deinterleave_head_rows(dW_foxT, H), dW_smallT[0:H], _deinterleave_head_rows(dW_gqkvT, H),
                              dW_smallT[H:3 * H], dW_gzT, dW_gatesT], axis=0)
    d_gconv = _deinterleave_heads(d_gconv_i, H)

    gpack_a = [dW_inT.reshape(N_CHIP, c_in, D)]
    d_hn1, gots_a = matmul(d_pfox, W_foxT, "nn", "d_hn1_fox", rider=_ride_exchange(gpack_a))
    sums_a = [add_halves(gpack_a[0], gots_a[0], idx, "add_halves_a")]
    d_hn1, landing_a = matmul(d_pgqkv, W_gqkvT, "nn", "d_hn1_gqkv", add=d_hn1,
                              rider=_ride_scatter([sums_a[0][1]], to=(0, 1)))
    d_hn1 = matmul(d_gz, W_gzT, "nn", "d_hn1_gz", add=d_hn1)
    d_hn1, got16_a = matmul(d_gates, W_gatesT, "nn", "d_hn1_gates", add=d_hn1,
                            rider=_ride_scatter([sums_a[0][1]], to=(2,), landing=landing_a))
    mine = [add_chips(sums_a[0][0], got16_a[0], idx, "add_chips_a")] + mine_b
    grad_x, d_norm_mix = matmul(d_psmall, W_smallT, "nn", "d_hn1_small", add=d_hn1,
                                post=_post_rmsnorm_bwd(x2, norm_mix, dh1, False))

    others = share_halves(mine)
    g_w_inT, g_up, g_rows = (jnp.concatenate([jnp.where(cidx == 0, h, o), jnp.where(cidx == 0, o, h)], axis=ax)
                             for h, o, ax in zip(mine, others, (1, 0, 0)))
    g_w_in = g_w_inT.T
    g_bf, g_bg, g_out, g_down, g_fconv = _unpack_rows(g_rows, g_shapes)

    small_items = [d_norm_mix, d_norm_ffn, d_norm_final, d_gdn_norm, d_prm, d_gconv, loss_here.reshape(1, 1)]
    small_shapes = [a.shape for a in small_items]
    sv = allreduce_small(_pack_rows(small_items, 0, F32, unit=8))
    g_norm_mix, g_norm_ffn, g_norm_final, g_gdn_norm, g_prm, g_gconv_all, loss = _unpack_rows(sv, small_shapes, unit=8)
    loss = loss[0, 0]
    g_norm_final = g_norm_final.reshape(D)
    g_fbias, g_dtb, g_alog = g_prm[0:1, 0:H], g_prm[0:1, H:2 * H], g_prm[1:2, H:2 * H]
    g_gconv = lax.dynamic_slice_in_dim(g_gconv_all, sidx * (3 * D // N_CHIP), 3 * D // N_CHIP, axis=1)

    names = ["norm_mix", "w_in", "fox_f_bias", "gdn_conv_w", "gdn_a_log", "gdn_dt_bias", "gdn_norm", "w_branch_fox",
             "w_branch_gdn", "w_out", "norm_ffn", "w_up", "ffn_conv_w", "w_down", "norm_final"]
    ws = [norm_mix, w_in, fox_f_bias, gdn_conv_w, gdn_a_log, gdn_dt_bias, gdn_norm, w_branch_fox, w_branch_gdn, w_out,
          norm_ffn, w_up, ffn_conv_w, w_down, norm_final]
    ms = [m_norm_mix, m_w_in, m_fox_f_bias, m_gdn_conv_w, m_gdn_a_log, m_gdn_dt_bias, m_gdn_norm, m_w_branch_fox,
          m_w_branch_gdn, m_w_out, m_norm_ffn, m_w_up, m_ffn_conv_w, m_w_down, m_norm_final]
    vs = [v_norm_mix, v_w_in, v_fox_f_bias, v_gdn_conv_w, v_gdn_a_log, v_gdn_dt_bias, v_gdn_norm, v_w_branch_fox,
          v_w_branch_gdn, v_w_out, v_norm_ffn, v_w_up, v_ffn_conv_w, v_w_down, v_norm_final]
    gs = [g_norm_mix, g_w_in, g_fbias, g_gconv, g_alog, g_dtb, g_gdn_norm, g_bf, g_bg, g_out, g_norm_ffn, g_up,
          g_fconv, g_down, g_norm_final]
    gs = [g.reshape(w.shape) for g, w in zip(gs, ws)]
    deltas, new_ms, new_vs = [], [], []
    for nm, w, g, m, v in zip(names, ws, gs, ms, vs):
        if w.ndim == 1:
            d, a, b = adamw(w.reshape(1, -1), g.reshape(1, -1), m.reshape(1, -1), v.reshape(1, -1), "adamw_" + nm)
            d, a, b = d.reshape(w.shape), a.reshape(w.shape), b.reshape(w.shape)
        elif nm == "w_in":
            d, a, b = (r.T[None] for r in adamw(w[0].T, g_w_inT, m[0].T, v[0].T, "adamw_" + nm))
        else:
            d, a, b = adamw(w, g, m, v, "adamw_" + nm)
        deltas.append(d)
        new_ms.append(a)
        new_vs.append(b)

    return (loss, grad_x.reshape(B, S, D), *gs, *deltas, *new_ms, *new_vs)
```

```python
import functools
import math

import jax
import jax.numpy as jnp
from jax import lax
from jax.experimental import pallas as pl
from jax.experimental.pallas import tpu as pltpu

F32 = jnp.float32
BF16 = jnp.bfloat16
HEAD = 128
CHUNK = 64
GDN_CONV = 4
FFN_CONV = 3
EPS = 1e-6
NEG = -1e30
ROW = 1024
ATT_TILE = 512
MM_WEIGHT_TILE_BYTES = 8 << 20
MM_TN_OPERAND_BYTES = 32 << 20
N_CHIP = 4
N_DEV = 8
MESH = pl.DeviceIdType.MESH
HI = lax.Precision.HIGH
EXACT = lax.Precision.HIGHEST

ADAM_LR, ADAM_B1, ADAM_B2, ADAM_EPS, ADAM_WD, ADAM_STEP = 0.001, 0.9, 0.999, 1e-08, 0.01, 10


def _tile(n, cap, unit=128):
    best = None
    t = unit
    while t <= min(n, cap):
        if n % t == 0:
            best = t
        t += unit
    return best if best is not None else n


def _params(*sem):
    return pltpu.CompilerParams(dimension_semantics=sem)


_NN = (((1,), (0,)), ((), ()))
_NT = (((1,), (1,)), ((), ()))
_TN = (((0,), (0,)), ((), ()))


def _dg(a, b, dims, hi):
    if hi:
        return lax.dot_general(a, b, dims, precision=HI, preferred_element_type=F32)
    return lax.dot_general(a.astype(BF16), b.astype(BF16), dims, preferred_element_type=F32)


class _RawOps:
    @staticmethod
    def nn(a, b, hi=False):
        return _dg(a, b, _NN, hi)

    @staticmethod
    def nt(a, b, hi=False):
        return _dg(a, b, _NT, hi)

    @staticmethod
    def tn(a, b, hi=False):
        return _dg(a, b, _TN, hi)


def _make_diff_ops():
    def build(hi):
        @jax.custom_vjp
        def nn(a, b):
            return _dg(a, b, _NN, hi)

        nn.defvjp(lambda a, b: (_dg(a, b, _NN, hi), (a, b)),
                  lambda r, g: (_dg(g, r[1], _NT, hi), _dg(r[0], g, _TN, hi)))

        @jax.custom_vjp
        def nt(a, b):
            return _dg(a, b, _NT, hi)

        nt.defvjp(lambda a, b: (_dg(a, b, _NT, hi), (a, b)),
                  lambda r, g: (_dg(g, r[1], _NN, hi), _dg(g, r[0], _TN, hi)))

        @jax.custom_vjp
        def tn(a, b):
            return _dg(a, b, _TN, hi)

        tn.defvjp(lambda a, b: (_dg(a, b, _TN, hi), (a, b)),
                  lambda r, g: (_dg(r[1], g, _NT, hi), _dg(r[0], g, _NN, hi)))
        return nn, nt, tn

    lo, hi_ = build(False), build(True)

    class _DiffOps:
        @staticmethod
        def nn(a, b, hi=False):
            return (hi_ if hi else lo)[0](a, b)

        @staticmethod
        def nt(a, b, hi=False):
            return (hi_ if hi else lo)[1](a, b)

        @staticmethod
        def tn(a, b, hi=False):
            return (hi_ if hi else lo)[2](a, b)

    return _DiffOps


_DiffOps = _make_diff_ops()


def _sigmoid(x):
    return 1.0 / (1.0 + jnp.exp(-x))


def _mm_tile(n, pref):
    if n % pref == 0:
        return pref
    if n % 1408 == 0:
        return 1408
    return _tile(n, pref)


class _Post:
    def __init__(self, fn, row_ins=(), vec_ins=(), row_outs=(), acc_outs=(), keep_main=True):
        self.fn, self.keep_main = fn, keep_main
        self.row_ins = [r if isinstance(r, tuple) else (r, r.shape[1], 0) for r in row_ins]
        self.vec_ins, self.row_outs, self.acc_outs = list(vec_ins), list(row_outs), list(acc_outs)


def matmul(a, b, mode, name, add=None, out_dtype=F32, post=None, rider=None, slab=None):
    if mode == "nn":
        (M, K), (K2, N) = a.shape, b.shape
    elif mode == "nt":
        (M, K), (N, K2) = a.shape, b.shape
    else:
        (K, M), (K2, N) = a.shape, b.shape
    assert K == K2, (name, a.shape, b.shape)
    tn = slab[1] if slab else _mm_tile(N, 1024)
    if mode == "tn":
        tm = slab[0] if slab else (M if M <= 1408 else _mm_tile(M, 1408))
        row_bytes = 2 * (tm * a.dtype.itemsize + tn * b.dtype.itemsize)
        tk = next((t for t in (4096, 2048) if K % t == 0 and t * row_bytes <= MM_TN_OPERAND_BYTES), _mm_tile(K, 1024))
    else:
        tk = K if K * tn * 2 <= MM_WEIGHT_TILE_BYTES else _mm_tile(K, 1024)
        tm = _mm_tile(M, 1024 if tk <= 2048 and post is None else 512)
    nk = K // tk
    assert post is None or (mode != "tn" and tn == N), name
    dims = {"nn": _NN, "nt": _NT, "tn": _TN}[mode]
    if mode == "tn":
        a_spec = pl.BlockSpec((tk, tm), lambda j, i, k: (k, i))
    else:
        a_spec = pl.BlockSpec((tm, tk), lambda j, i, k: (i, k))
    if mode == "nt":
        b_spec = pl.BlockSpec((tn, tk), lambda j, i, k: (j, k))
    else:
        b_spec = pl.BlockSpec((tk, tn), lambda j, i, k: (k, j))
    o_spec = pl.BlockSpec((tm, tn), lambda j, i, k: (i, j))
    has_add = add is not None
    keep_main = post is None or post.keep_main
    counts = [2 + has_add] + ([len(post.row_ins), len(post.vec_ins)] if post else [0, 0]) + [int(keep_main)]
    counts += ([len(post.row_outs), len(post.acc_outs)] if post else [0, 0]) + [int(nk > 1)]

    def body(*refs):
        parts, p = [], 0
        for cnt in counts:
            parts.append(refs[p:p + cnt])
            p += cnt
        core, row_ins, vec_ins, main, row_outs, acc_outs, acc = parts
        a_ref, b_ref = core[:2]
        prod = lax.dot_general(a_ref[...].astype(BF16), b_ref[...].astype(BF16), dims, preferred_element_type=F32)

        def finish(r):
            if has_add:
                r = r + core[2][...]
            if keep_main:
                main[0][...] = r.astype(out_dtype)
            if post is not None:
                @pl.when(pl.program_id(1) == 0)
                def _():
                    for ref in acc_outs:
                        ref[...] = jnp.zeros_like(ref)

                post.fn(r, row_ins, vec_ins, row_outs, acc_outs)

        if nk == 1:
            finish(prod)
            return
        acc_ref = acc[0]
        k = pl.program_id(2)

        @pl.when(k == 0)
        def _():
            acc_ref[...] = jnp.zeros_like(acc_ref)

        acc_ref[...] += prod

        @pl.when(k == nk - 1)
        def _():
            finish(acc_ref[...])

    in_specs = [a_spec, b_spec] + ([o_spec] if has_add else [])
    args = (a, b) + ((add,) if has_add else ())
    out_specs = [o_spec] if keep_main else []
    out_shape = [jax.ShapeDtypeStruct((M, N), out_dtype)] if keep_main else []
    if slab:
        out_specs = [pl.BlockSpec((None, None, tm, tn), lambda j, i, k: (i, j, 0, 0))]
        out_shape = [jax.ShapeDtypeStruct((M // tm, N // tn, tm, tn), out_dtype)]
    if post is not None:
        in_specs += [pl.BlockSpec((tm, cols), lambda j, i, k, cb=cb: (i, cb)) for _, cols, cb in post.row_ins]
        in_specs += [pl.BlockSpec((1, v.shape[1]), lambda j, i, k: (0, 0)) for v in post.vec_ins]
        args += tuple(r for r, _, _ in post.row_ins) + tuple(post.vec_ins)
        out_specs += [pl.BlockSpec((tm, cols), lambda j, i, k: (i, 0)) for cols, _ in post.row_outs]
        out_specs += [pl.BlockSpec((1, cols), lambda j, i, k: (0, 0)) for cols in post.acc_outs]
        out_shape += [jax.ShapeDtypeStruct((M, cols), dt) for cols, dt in post.row_outs]
        out_shape += [jax.ShapeDtypeStruct((1, cols), F32) for cols in post.acc_outs]
    rows_sem = "arbitrary" if post is not None and post.acc_outs else "parallel"
    res = _hosted_call(
        body, rider, name=name, grid=(N // tn, M // tm, nk), in_specs=in_specs, out_specs=out_specs, out_shape=out_shape,
        scratch_shapes=[pltpu.VMEM((tm, tn), F32)] if nk > 1 else [], semantics=("parallel", rows_sem, "arbitrary"),
    )(*args)
    if rider is not None:
        res, carried = res
        return (res[0] if post is None else res), carried
    return res[0] if post is None else res


def rmsnorm_fwd(x, g, name, rider=None):
    T, D = x.shape
    tm = _tile(T, 512, 8)

    def body(x_ref, g_ref, o_ref):
        xv = x_ref[...]
        r = lax.rsqrt(jnp.mean(xv * xv, axis=-1, keepdims=True) + EPS)
        o_ref[...] = (xv * r * g_ref[...]).astype(BF16)

    res = _hosted_call(
        body, rider, name=name, grid=(T // tm,),
        in_specs=[pl.BlockSpec((tm, D), lambda i: (i, 0)), pl.BlockSpec((1, D), lambda i: (0, 0))],
        out_specs=[pl.BlockSpec((tm, D), lambda i: (i, 0))], out_shape=[jax.ShapeDtypeStruct((T, D), BF16)],
        scratch_shapes=[], semantics=("parallel",),
    )(x, g)
    return res[0] if rider is None else (res[0][0], res[1])


def _post_rmsnorm(g):
    def fn(r, row_ins, vec_ins, row_outs, acc_outs):
        rs = lax.rsqrt(jnp.mean(r * r, axis=-1, keepdims=True) + EPS)
        row_outs[0][...] = (r * rs * vec_ins[0][...]).astype(BF16)

    return _Post(fn, vec_ins=[g], row_outs=[(g.shape[1], BF16)])


def _post_rmsnorm_bwd(x, g, dres, with_bf16):
    D = g.shape[1]

    def fn(dy, row_ins, vec_ins, row_outs, acc_outs):
        xv = row_ins[0][...]
        rs = lax.rsqrt(jnp.mean(xv * xv, axis=-1, keepdims=True) + EPS)
        xh = xv * rs
        acc_outs[0][...] += jnp.sum(dy * xh, axis=0, keepdims=True)
        dxh = dy * vec_ins[0][...]
        dx = row_ins[1][...] + rs * (dxh - xh * jnp.mean(dxh * xh, axis=-1, keepdims=True))
        row_outs[0][...] = dx
        if with_bf16:
            row_outs[1][...] = dx.astype(BF16)

    return _Post(fn, row_ins=[x, dres], vec_ins=[g], row_outs=[(D, F32)] + ([(D, BF16)] if with_bf16 else []),
                 acc_outs=[D], keep_main=False)


def _post_loss(g, target):
    D = g.shape[1]

    def fn(hv, row_ins, vec_ins, row_outs, acc_outs):
        rs = lax.rsqrt(jnp.mean(hv * hv, axis=-1, keepdims=True) + EPS)
        xh = hv * rs
        gv = vec_ins[0][...]
        err = xh * gv - row_ins[0][...]
        acc_outs[0][...] += jnp.sum(err * err, axis=0, keepdims=True)
        dy = err * (1.0 / D)
        acc_outs[1][...] += jnp.sum(dy * xh, axis=0, keepdims=True)
        dxh = dy * gv
        dh = rs * (dxh - xh * jnp.mean(dxh * xh, axis=-1, keepdims=True))
        row_outs[0][...] = dh
        row_outs[1][...] = dh.astype(BF16)

    return _Post(fn, row_ins=[target], vec_ins=[g], row_outs=[(D, F32), (D, BF16)], acc_outs=[D, D], keep_main=False)


def _shift_down(x, k):
    if k == 0:
        return x
    rows = lax.broadcasted_iota(jnp.int32, x.shape, 0)
    return jnp.where(rows >= k, pltpu.roll(x, k, 0), 0.0)


def _shift_up(x, k):
    if k == 0:
        return x
    s = x.shape[0]
    rows = lax.broadcasted_iota(jnp.int32, x.shape, 0)
    return jnp.where(rows < s - k, pltpu.roll(x, s - k, 0), 0.0)


def _conv_fwd(x, w_ref, kw, keep_shifted=False):
    shifted = [_shift_down(x, kw - 1 - i) for i in range(kw - 1)]
    y = x * w_ref[kw - 1:kw, :]
    for i in range(kw - 1):
        y = y + shifted[i] * w_ref[i:i + 1, :]
    return (y, shifted) if keep_shifted else y


def _conv_bwd(x, shifted, dy, w_ref, kw):
    dx = dy * w_ref[kw - 1:kw, :]
    dws = []
    for i in range(kw - 1):
        dx = dx + _shift_up(dy, kw - 1 - i) * w_ref[i:i + 1, :]
        dws.append(jnp.sum(dy * shifted[i], axis=0, keepdims=True))
    dws.append(jnp.sum(dy * x, axis=0, keepdims=True))
    return dx, dws


def ffn_gate_fwd(up_g, up_v, cw_g, cw_v, B, S):
    T, Fd = up_g.shape
    tc = _tile(Fd, 256)

    def body(g_ref, v_ref, wg_ref, wv_ref, o_ref):
        ug = _conv_fwd(g_ref[...], wg_ref, FFN_CONV)
        uv = _conv_fwd(v_ref[...], wv_ref, FFN_CONV)
        o_ref[...] = (ug * _sigmoid(ug) * uv).astype(BF16)

    blk = pl.BlockSpec((S, tc), lambda b, j: (b, j))
    wblk = pl.BlockSpec((FFN_CONV, tc), lambda b, j: (0, j))
    return pl.pallas_call(
        body, name="ffn_gate_fwd", grid=(B, Fd // tc), in_specs=[blk, blk, wblk, wblk], out_specs=blk,
        out_shape=jax.ShapeDtypeStruct((T, Fd), BF16), compiler_params=_params("parallel", "parallel"),
    )(up_g, up_v, cw_g, cw_v)


def ffn_gate_bwd(up_g, up_v, cw_g, cw_v, d_act, B, S):
    T, Fd = up_g.shape
    tc = _tile(Fd, 256)

    def body(g_ref, v_ref, wg_ref, wv_ref, da_ref, dg_ref, dv_ref, dwg_ref, dwv_ref):
        @pl.when(pl.program_id(1) == 0)
        def _():
            dwg_ref[...] = jnp.zeros_like(dwg_ref)
            dwv_ref[...] = jnp.zeros_like(dwv_ref)

        xg, xv = g_ref[...], v_ref[...]
        ug, sh_g = _conv_fwd(xg, wg_ref, FFN_CONV, keep_shifted=True)
        uv, sh_v = _conv_fwd(xv, wv_ref, FFN_CONV, keep_shifted=True)
        da = da_ref[...]
        sg = _sigmoid(ug)
        d_ug = da * uv * (sg + ug * sg * (1.0 - sg))
        d_uv = da * ug * sg
        dxg, dwg = _conv_bwd(xg, sh_g, d_ug, wg_ref, FFN_CONV)
        dxv, dwv = _conv_bwd(xv, sh_v, d_uv, wv_ref, FFN_CONV)
        dg_ref[...] = dxg.astype(BF16)
        dv_ref[...] = dxv.astype(BF16)
        for i in range(FFN_CONV):
            dwg_ref[i:i + 1, :] += dwg[i]
            dwv_ref[i:i + 1, :] += dwv[i]

    blk = pl.BlockSpec((S, tc), lambda j, b: (b, j))
    wblk = pl.BlockSpec((FFN_CONV, tc), lambda j, b: (0, j))
    return pl.pallas_call(
        body, name="ffn_gate_bwd", grid=(Fd // tc, B), in_specs=[blk, blk, wblk, wblk, blk],
        out_specs=[blk, blk, wblk, wblk],
        out_shape=[jax.ShapeDtypeStruct((T, Fd), BF16), jax.ShapeDtypeStruct((T, Fd), BF16),
                   jax.ShapeDtypeStruct((FFN_CONV, Fd), F32), jax.ShapeDtypeStruct((FFN_CONV, Fd), F32)],
        compiler_params=_params("parallel", "arbitrary"),
    )(up_g, up_v, cw_g, cw_v, d_act)


def _post_merge(p_gates, bf_):
    D = bf_.shape[1]

    def fn(bg, row_ins, vec_ins, row_outs, acc_outs):
        gf_ref, gg_ref, bf_ref = row_ins
        row_outs[0][...] = (_sigmoid(gf_ref[...]) * bf_ref[...] + _sigmoid(gg_ref[...]) * bg).astype(BF16)

    return _Post(fn, row_ins=[(p_gates, D, 0), (p_gates, D, 1), bf_], row_outs=[(D, BF16)])


def _post_merge_bwd(p_gates, bf_, bg_):
    D = bf_.shape[1]

    def fn(d, row_ins, vec_ins, row_outs, acc_outs):
        gf_ref, gg_ref, bf_ref, bg_ref = row_ins
        sf, sg = _sigmoid(gf_ref[...]), _sigmoid(gg_ref[...])
        row_outs[0][...] = (d * sf).astype(BF16)
        row_outs[1][...] = (d * sg).astype(BF16)
        row_outs[2][:, 0:D] = (d * bf_ref[...] * sf * (1.0 - sf)).astype(BF16)
        row_outs[2][:, D:2 * D] = (d * bg_ref[...] * sg * (1.0 - sg)).astype(BF16)

    return _Post(fn, row_ins=[(p_gates, D, 0), (p_gates, D, 1), bf_, bg_],
                 row_outs=[(D, BF16), (D, BF16), (2 * D, BF16)], keep_main=False)


ATT_HEADS = 2
ATT_HEADS_FWD = 4


def fox_fwd(p_fox, c_col, c_row, B, S, H, rider=None):
    T = B * S
    t = _tile(S, ATT_TILE)
    nq = S // t
    scale = HEAD ** -0.5
    hp = next((n for n in (ATT_HEADS_FWD, ATT_HEADS) if H % n == 0), 1)
    hs = range(hp)

    def body(*refs):
        qkv_refs, (cq_ref, cr_ref, o_ref, o16_ref, lse_ref) = refs[:3 * hp], refs[3 * hp:]
        i = pl.program_id(2)
        w = t // 2
        halves = (0, 1)
        q = [[qkv_refs[3 * hh][e * w:(e + 1) * w, :] for e in halves] for hh in hs]
        row = lax.broadcasted_iota(jnp.int32, (w, t), 0)
        col = lax.broadcasted_iota(jnp.int32, (w, t), 1)

        def step(j, carry, diagonal):
            off = pl.multiple_of(j * t, t)
            k = [qkv_refs[3 * hh + 1][pl.ds(off, t), :] for hh in hs]
            v = [qkv_refs[3 * hh + 2][pl.ds(off, t), :] for hh in hs]
            out = [[None, None] for _ in hs]
            for e in halves:
                s = [lax.dot_general(q[hh][e], k[hh], _NT, preferred_element_type=F32) * scale
                     - cr_ref[hh, :, pl.ds(off, t)] for hh in hs]
                if diagonal:
                    s = [jnp.where(col <= row + e * w, s[hh], NEG) for hh in hs]
                m_new = [jnp.maximum(carry[hh][e][0], jnp.max(s[hh], axis=-1, keepdims=True)) for hh in hs]
                alpha = [jnp.exp(carry[hh][e][0] - m_new[hh]) for hh in hs]
                p = [jnp.exp(s[hh] - m_new[hh]) for hh in hs]
                l = [alpha[hh] * carry[hh][e][1] + jnp.sum(p[hh], axis=-1, keepdims=True) for hh in hs]
                acc = [alpha[hh] * carry[hh][e][2]
                       + lax.dot_general(p[hh].astype(BF16), v[hh], _NN, preferred_element_type=F32) for hh in hs]
                for hh in hs:
                    out[hh][e] = (m_new[hh], l[hh], acc[hh])
            return tuple(tuple(o) for o in out)

        start = (jnp.full((w, 1), NEG, F32), jnp.zeros((w, 1), F32), jnp.zeros((w, HEAD), F32))
        below = lax.fori_loop(0, i, functools.partial(step, diagonal=False), tuple((start, start) for _ in hs))
        done = step(i, below, diagonal=True)
        for hh in hs:
            cols = slice(hh * HEAD, (hh + 1) * HEAD)
            for e, (m, l, acc) in enumerate(done[hh]):
                at = slice(e * w, (e + 1) * w)
                o = acc / l
                o_ref[at, cols] = o
                o16_ref[at, cols] = o.astype(BF16)
                lse_ref[hh, at, :] = cq_ref[hh, at, :] + m + jnp.log(l)

    qkv = []
    for hh in hs:
        qkv.append(pl.BlockSpec((t, HEAD), lambda b, g, i, hh=hh: (b * nq + i, 3 * (hp * g + hh))))
        qkv.append(pl.BlockSpec((S, HEAD), lambda b, g, i, hh=hh: (b, 3 * (hp * g + hh) + 1)))
        qkv.append(pl.BlockSpec((S, HEAD), lambda b, g, i, hh=hh: (b, 3 * (hp * g + hh) + 2)))
    heads = pl.BlockSpec((t, hp * HEAD), lambda b, g, i: (b * nq + i, g))
    return _hosted_call(
        body, rider, name="fox_fwd", grid=(B, H // hp, nq),
        in_specs=qkv + [pl.BlockSpec((None, hp, t, 1), lambda b, g, i: (b, g, i, 0)),
                        pl.BlockSpec((None, hp, 1, S), lambda b, g, i: (b, g, 0, 0))],
        out_specs=[heads, heads, pl.BlockSpec((None, hp, t, 1), lambda b, g, i: (b, g, i, 0))],
        out_shape=[jax.ShapeDtypeStruct((T, H * HEAD), F32), jax.ShapeDtypeStruct((T, H * HEAD), BF16),
                   jax.ShapeDtypeStruct((B, H, S, 1), F32)],
        scratch_shapes=[], semantics=("parallel", "parallel", "arbitrary"),
    )(*([p_fox] * (3 * hp)), c_col, c_row)


def fox_bwd(p_fox, c_col, c_row, o, lse, do, B, S, H, rider=None):
    T = B * S
    t = _tile(S, ATT_TILE)
    n = S // t
    scale = HEAD ** -0.5
    hp = ATT_HEADS if H % ATT_HEADS == 0 else 1
    hs = range(hp)

    def body(*refs):
        qkv_refs = refs[:3 * hp]
        cq_ref, cr_ref, o_ref, lse_ref, do_ref, dqkv_ref, dcq_ref, dcr_ref, dq_acc, delta_s, lse_s = refs[3 * hp:]
        row = lax.broadcasted_iota(jnp.int32, (t, t), 0)
        col = lax.broadcasted_iota(jnp.int32, (t, t), 1)
        cols = [slice(hh * HEAD, (hh + 1) * HEAD) for hh in hs]

        def prep(i, c):
            rows = pl.ds(pl.multiple_of(i * t, t), t)
            for hh in hs:
                delta_s[hh, rows, :] = jnp.sum(do_ref[rows, cols[hh]] * o_ref[rows, cols[hh]], axis=-1, keepdims=True)
                lse_s[hh, rows, :] = lse_ref[hh, rows, :] - cq_ref[hh, rows, :]
                dq_acc[hh, rows, :] = jnp.zeros((t, HEAD), F32)
                dcq_ref[hh, rows, :] = jnp.zeros((t, 1), F32)
            return c

        lax.fori_loop(0, n, prep, 0)

        def kv_step(j, c):
            joff = pl.multiple_of(j * t, t)
            k = [qkv_refs[3 * hh + 1][pl.ds(joff, t), :] for hh in hs]
            v = [qkv_refs[3 * hh + 2][pl.ds(joff, t), :] for hh in hs]
            crj = [cr_ref[hh, :, pl.ds(joff, t)] for hh in hs]

            def q_step(i, carry, diagonal):
                rows = pl.ds(pl.multiple_of(i * t, t), t)
                q = [qkv_refs[3 * hh][rows, :] for hh in hs]
                dob = [do_ref[rows, cols[hh]].astype(BF16) for hh in hs]
                s = [lax.dot_general(q[hh], k[hh], _NT, preferred_element_type=F32) * scale - crj[hh] for hh in hs]
                dp = [lax.dot_general(dob[hh], v[hh], _NT, preferred_element_type=F32) for hh in hs]
                if diagonal:
                    s = [jnp.where(col <= row, s[hh], NEG) for hh in hs]
                p = [jnp.exp(s[hh] - lse_s[hh, rows, :]) for hh in hs]
                ds = [p[hh] * (dp[hh] - delta_s[hh, rows, :]) for hh in hs]
                dsb = [ds[hh].astype(BF16) for hh in hs]
                dv = [carry[hh][1] + lax.dot_general(p[hh].astype(BF16), dob[hh], _TN, preferred_element_type=F32)
                      for hh in hs]
                dk = [carry[hh][0] + lax.dot_general(dsb[hh], q[hh], _TN, preferred_element_type=F32) for hh in hs]
                for hh in hs:
                    dq_acc[hh, rows, :] += lax.dot_general(dsb[hh], k[hh], _NN, preferred_element_type=F32) * scale
                    dcq_ref[hh, rows, :] += jnp.sum(ds[hh], axis=-1, keepdims=True)
                dc = [carry[hh][2] + jnp.sum(ds[hh], axis=0, keepdims=True) for hh in hs]
                return tuple((dk[hh], dv[hh], dc[hh]) for hh in hs)

            z = jnp.zeros((t, HEAD), F32)
            zero = tuple((z, z, jnp.zeros((1, t), F32)) for _ in hs)
            on_diagonal = q_step(j, zero, diagonal=True)
            done = lax.fori_loop(j + 1, n, functools.partial(q_step, diagonal=False), on_diagonal)
            for hh, (dk, dv, dc) in enumerate(done):
                base = 3 * HEAD * hh
                dqkv_ref[pl.ds(joff, t), base + HEAD:base + 2 * HEAD] = (dk * scale).astype(BF16)
                dqkv_ref[pl.ds(joff, t), base + 2 * HEAD:base + 3 * HEAD] = dv.astype(BF16)
                dcr_ref[hh, :, pl.ds(joff, t)] = -dc
            return c

        lax.fori_loop(0, n, kv_step, 0)
        for hh in hs:
            dqkv_ref[:, 3 * HEAD * hh:3 * HEAD * hh + HEAD] = dq_acc[hh].astype(BF16)

    qkv = []
    for hh in hs:
        for part in range(3):
            qkv.append(pl.BlockSpec((S, HEAD), lambda b, g, hh=hh, part=part: (b, 3 * (hp * g + hh) + part)))
    col_spec = pl.BlockSpec((None, hp, S, 1), lambda b, g: (b, g, 0, 0))
    row_spec = pl.BlockSpec((None, hp, 1, S), lambda b, g: (b, g, 0, 0))
    heads = pl.BlockSpec((S, hp * HEAD), lambda b, g: (b, g))
    return _hosted_call(
        body, rider, name="fox_bwd", grid=(B, H // hp),
        in_specs=qkv + [col_spec, row_spec, heads, col_spec, heads],
        out_specs=[pl.BlockSpec((S, 3 * hp * HEAD), lambda b, g: (b, g)), col_spec, row_spec],
        out_shape=[jax.ShapeDtypeStruct((T, 3 * H * HEAD), BF16), jax.ShapeDtypeStruct((B, H, S, 1), F32),
                   jax.ShapeDtypeStruct((B, H, 1, S), F32)],
        scratch_shapes=[pltpu.VMEM((hp, S, HEAD), F32), pltpu.VMEM((hp, S, 1), F32), pltpu.VMEM((hp, S, 1), F32)],
        semantics=("parallel", "parallel"),
    )(*([p_fox] * (3 * hp)), c_col, c_row, o, lse, do)


def _small_fn(x, b0, b1, H):
    S = x.shape[0]
    lane = lax.broadcasted_iota(jnp.int32, x.shape, 1)
    z = x + b0
    tail = jnp.log1p(jnp.exp(-jnp.abs(z)))
    softplus = jnp.maximum(z, 0.0) + tail
    logsig = -(jnp.maximum(-z, 0.0) + tail)
    g = -jnp.exp(b1) * softplus
    pre = jnp.where(lane < H, logsig, jnp.where(lane < 2 * H, g, 0.0))
    bl = _tile(S, 256, CHUNK)
    r = lax.broadcasted_iota(jnp.int32, (bl, bl), 0)
    c = lax.broadcasted_iota(jnp.int32, (bl, bl), 1)
    tri = (r >= c).astype(F32)
    tri_chunk = jnp.where((r >= c) & (jnp.right_shift(r, 6) == jnp.right_shift(c, 6)), 1.0, 0.0)
    carry = jnp.zeros((1, x.shape[1]), F32)
    parts = []
    for i in range(S // bl):
        blk = pre[i * bl:(i + 1) * bl, :]
        full = lax.dot_general(tri, blk, _NN, precision=EXACT, preferred_element_type=F32) + carry
        chunked = lax.dot_general(tri_chunk, blk, _NN, precision=EXACT, preferred_element_type=F32)
        parts.append(jnp.where(lane[:bl] < H, full, chunked))
        carry = carry + jnp.sum(blk, axis=0, keepdims=True)
    cum = parts[0] if len(parts) == 1 else jnp.concatenate(parts, axis=0)
    return jnp.where(lane < 2 * H, cum, jnp.where(lane < 3 * H, _sigmoid(x), 0.0))


def small_fwd(p_small, prm, B, S, H):
    T = B * S

    def body(x_ref, p_ref, o_ref):
        o_ref[...] = _small_fn(x_ref[...], p_ref[0:1, :], p_ref[1:2, :], H)

    blk = pl.BlockSpec((S, 128), lambda b: (b, 0))
    return pl.pallas_call(
        body, name="small_fwd", grid=(B,), in_specs=[blk, pl.BlockSpec((8, 128), lambda b: (0, 0))], out_specs=blk,
        out_shape=jax.ShapeDtypeStruct((T, 128), F32), compiler_params=_params("parallel"),
    )(p_small, prm)


def small_bwd(p_small, prm, d_out, B, S, H):
    T = B * S

    def body(x_ref, p_ref, d_ref, dx_ref, dp_ref):
        @pl.when(pl.program_id(0) == 0)
        def _():
            dp_ref[...] = jnp.zeros_like(dp_ref)

        _, vjp = jax.vjp(functools.partial(_small_fn, H=H), x_ref[...], p_ref[0:1, :], p_ref[1:2, :])
        dx, db0, db1 = vjp(d_ref[...])
        dx_ref[...] = dx.astype(BF16)
        dp_ref[0:1, :] += db0
        dp_ref[1:2, :] += db1

    blk = pl.BlockSpec((S, 128), lambda b: (b, 0))
    pblk = pl.BlockSpec((8, 128), lambda b: (0, 0))
    return pl.pallas_call(
        body, name="small_bwd", grid=(B,), in_specs=[blk, pblk, blk], out_specs=[blk, pblk],
        out_shape=[jax.ShapeDtypeStruct((T, 128), BF16), jax.ShapeDtypeStruct((8, 128), F32)],
        compiler_params=_params("arbitrary"),
    )(p_small, prm, d_out)


def gdn_prep_fwd(p_gqkv, cw, B, S, H):
    T = B * S

    def body(x_ref, w_ref, o_ref):
        for part in range(3):
            cols = slice(part * HEAD, (part + 1) * HEAD)
            y = _conv_fwd(x_ref[:, cols], w_ref.at[:, cols], GDN_CONV)
            a = y * _sigmoid(y)
            if part < 2:
                a = a * lax.rsqrt(jnp.sum(a * a, axis=-1, keepdims=True) + EPS)
            o_ref[:, cols] = a

    blk = pl.BlockSpec((S, 3 * HEAD), lambda b, h: (b, h))
    wblk = pl.BlockSpec((GDN_CONV, 3 * HEAD), lambda b, h: (0, h))
    return pl.pallas_call(
        body, name="gdn_prep_fwd", grid=(B, H), in_specs=[blk, wblk], out_specs=blk,
        out_shape=jax.ShapeDtypeStruct((T, 3 * H * HEAD), F32), compiler_params=_params("parallel", "parallel"),
    )(p_gqkv, cw)


def gdn_prep_bwd(p_gqkv, cw, d_out, B, S, H):
    T = B * S

    def body(x_ref, w_ref, d_ref, dx_ref, dw_ref):
        @pl.when(pl.program_id(1) == 0)
        def _():
            dw_ref[...] = jnp.zeros_like(dw_ref)

        x = x_ref[...]
        y, shifted = _conv_fwd(x, w_ref, GDN_CONV, keep_shifted=True)
        sg = _sigmoid(y)
        a = y * sg
        rs = lax.rsqrt(jnp.sum(a * a, axis=-1, keepdims=True) + EPS)
        d = d_ref[...]
        out = a * rs
        da_qk = rs * (d - out * jnp.sum(d * out, axis=-1, keepdims=True))
        is_qk = (pl.program_id(0) % 3) < 2
        da = jnp.where(is_qk, da_qk, d)
        dy = da * (sg + y * sg * (1.0 - sg))
        dx, dws = _conv_bwd(x, shifted, dy, w_ref, GDN_CONV)
        dx_ref[...] = dx.astype(BF16)
        for i in range(GDN_CONV):
            dw_ref[i:i + 1, :] += dws[i]

    blk = pl.BlockSpec((S, HEAD), lambda n, b: (b, n))
    wblk = pl.BlockSpec((GDN_CONV, HEAD), lambda n, b: (0, n))
    return pl.pallas_call(
        body, name="gdn_prep_bwd", grid=(3 * H, B), in_specs=[blk, wblk, blk], out_specs=[blk, wblk],
        out_shape=[jax.ShapeDtypeStruct((T, 3 * H * HEAD), BF16), jax.ShapeDtypeStruct((GDN_CONV, 3 * H * HEAD), F32)],
        compiler_params=_params("parallel", "arbitrary"),
    )(p_gqkv, cw, d_out)


@jax.custom_vjp
def _given_inverse(a, t):
    return t


def _given_inverse_fwd(a, t):
    return t, t


def _given_inverse_bwd(t, g):
    x = _dg(t, g, _TN, True)
    return -_dg(x, t, _NT, True), jnp.zeros_like(t)


_given_inverse.defvjp(_given_inverse_fwd, _given_inverse_bwd)


def _to_col(row):
    n = row.shape[1]
    r = lax.broadcasted_iota(jnp.int32, (n, n), 0)
    c = lax.broadcasted_iota(jnp.int32, (n, n), 1)
    return jnp.sum(jnp.where(r == c, row, 0.0), axis=1, keepdims=True)


def _intra_fn(k, v, beta_r, gcr, ops, t_known=None):
    n = len(k)
    m = k[0].shape[0]
    r = lax.broadcasted_iota(jnp.int32, (m, m), 0)
    c = lax.broadcasted_iota(jnp.int32, (m, m), 1)
    below = (r > c) & (jnp.right_shift(r, 6) == jnp.right_shift(c, 6))
    beta = [_to_col(beta_r[i]) for i in range(n)]
    gcc = [_to_col(gcr[i]) for i in range(n)]
    decay = [jnp.exp(jnp.where(below, gcc[i] - gcr[i], NEG)) for i in range(n)]
    kb = [k[i] * beta[i] for i in range(n)]
    a = [ops.nt(kb[i], k[i]) * decay[i] for i in range(n)]
    if t_known is None:
        p = [-a[i] for i in range(n)]
        tm = [jnp.where(r == c, 1.0, 0.0) + p[i] for i in range(n)]
        for _ in range(5):
            p = [ops.nn(p[i], p[i], hi=True) for i in range(n)]
            tm = [tm[i] + ops.nn(tm[i], p[i], hi=True) for i in range(n)]
    else:
        tm = [_given_inverse(a[i], t_known[i]) for i in range(n)]
    both = [ops.nn(tm[i], jnp.concatenate([v[i] * beta[i], kb[i] * jnp.exp(gcc[i])], axis=1), hi=True) for i in range(n)]
    u_hat = [both[i][:, :HEAD] for i in range(n)]
    w = [both[i][:, HEAD:] for i in range(n)]
    return tuple(u_hat), tuple(w), tuple(tm)


INTRA_NB = 32
PAIR = 1


def gdn_intra_fwd(qkvn, betar5, gcr5, B, S, H, rider=None):
    T = B * S
    UNIT = PAIR * CHUNK
    N = S // UNIT
    nb = min(INTRA_NB // PAIR, N)
    rows = nb * UNIT
    ns = N // nb

    def body(k_ref, v_ref, b_ref, gr_ref, uh_ref, w_ref, t_ref):
        sls = [slice(ci * UNIT, (ci + 1) * UNIT) for ci in range(nb)]
        u_hat, w, tm = _intra_fn(tuple(k_ref[sl, :] for sl in sls), tuple(v_ref[sl, :] for sl in sls),
                                 tuple(b_ref[ci] for ci in range(nb)), tuple(gr_ref[ci] for ci in range(nb)), _RawOps)
        for ci, sl in enumerate(sls):
            uh_ref[sl, :] = u_hat[ci]
            w_ref[sl, :] = w[ci]
            t_ref[ci] = tm[ci]

    rowspec = pl.BlockSpec((None, None, nb, 1, UNIT), lambda b, h, i: (b, h, i, 0, 0))
    sqspec = pl.BlockSpec((None, None, nb, UNIT, UNIT), lambda b, h, i: (b, h, i, 0, 0))
    out = pl.BlockSpec((rows, HEAD), lambda b, h, i: (b * ns + i, h))
    return _hosted_call(
        body, rider, name="gdn_intra_fwd", grid=(B, H, ns),
        in_specs=[pl.BlockSpec((rows, HEAD), lambda b, h, i: (b * ns + i, 3 * h + 1)),
                  pl.BlockSpec((rows, HEAD), lambda b, h, i: (b * ns + i, 3 * h + 2)),
                  rowspec, rowspec],
        out_specs=[out, out, sqspec],
        out_shape=[jax.ShapeDtypeStruct((T, H * HEAD), F32), jax.ShapeDtypeStruct((T, H * HEAD), F32),
                   jax.ShapeDtypeStruct((B, H, N, UNIT, UNIT), F32)],
        scratch_shapes=[], semantics=("parallel", "parallel", "parallel"),
    )(qkvn, qkvn, betar5, gcr5)


def gdn_intra_bwd(qkvn, betar5, gcr5, t_inv, d_uh, d_w, dq_in, dk_in, B, S, H):
    T = B * S
    UNIT = PAIR * CHUNK
    N = S // UNIT
    nb = min(INTRA_NB // PAIR, N)
    rows = nb * UNIT
    ns = N // nb

    def body(k_ref, v_ref, b_ref, gr_ref, t_ref, duh_ref, dw_ref, dq_ref, dk_ref, o_ref, db_ref, dgr_ref):
        sls = [slice(ci * UNIT, (ci + 1) * UNIT) for ci in range(nb)]
        chunks = range(nb)
        _, vjp = jax.vjp(
            functools.partial(_intra_fn, ops=_DiffOps, t_known=tuple(t_ref[ci] for ci in chunks)),
            tuple(k_ref[sl, :] for sl in sls), tuple(v_ref[sl, :] for sl in sls), tuple(b_ref[ci] for ci in chunks),
            tuple(gr_ref[ci] for ci in chunks))
        zero = jnp.zeros((UNIT, UNIT), F32)
        dk, dv, db, dgr = vjp((tuple(duh_ref[sl, :] for sl in sls), tuple(dw_ref[sl, :] for sl in sls),
                               tuple(zero for _ in chunks)))
        for ci, sl in enumerate(sls):
            o_ref[sl, 0:HEAD] = dq_ref[sl, :]
            o_ref[sl, HEAD:2 * HEAD] = dk[ci] + dk_ref[sl, :]
            o_ref[sl, 2 * HEAD:3 * HEAD] = dv[ci]
            db_ref[ci] = db[ci]
            dgr_ref[ci] = dgr[ci]

    rowspec = pl.BlockSpec((None, None, nb, 1, UNIT), lambda b, h, i: (b, h, i, 0, 0))
    sqspec = pl.BlockSpec((None, None, nb, UNIT, UNIT), lambda b, h, i: (b, h, i, 0, 0))
    head = pl.BlockSpec((rows, HEAD), lambda b, h, i: (b * ns + i, h))
    return pl.pallas_call(
        body, name="gdn_intra_bwd", grid=(B, H, ns),
        in_specs=[pl.BlockSpec((rows, HEAD), lambda b, h, i: (b * ns + i, 3 * h + 1)),
                  pl.BlockSpec((rows, HEAD), lambda b, h, i: (b * ns + i, 3 * h + 2)),
                  rowspec, rowspec, sqspec, head, head, head, head],
        out_specs=[pl.BlockSpec((rows, 3 * HEAD), lambda b, h, i: (b * ns + i, h)), rowspec, rowspec],
        out_shape=[jax.ShapeDtypeStruct((T, 3 * H * HEAD), F32),
                   jax.ShapeDtypeStruct((B, H, N, 1, UNIT), F32), jax.ShapeDtypeStruct((B, H, N, 1, UNIT), F32)],
        compiler_params=_params("parallel", "parallel", "parallel"),
    )(qkvn, qkvn, betar5, gcr5, t_inv, d_uh, d_w, dq_in, dk_in)


def _inter_fn(q, k, u_hat, w, gcr, state, ops):
    n = len(q)
    r = lax.broadcasted_iota(jnp.int32, (CHUNK, CHUNK), 0)
    c = lax.broadcasted_iota(jnp.int32, (CHUNK, CHUNK), 1)
    last = lax.broadcasted_iota(jnp.int32, (1, CHUNK), 1) == CHUNK - 1
    gcc = [_to_col(gcr[i]) for i in range(n)]
    gl = [jnp.sum(jnp.where(last, gcr[i], 0.0), axis=1, keepdims=True) for i in range(n)]
    decay = [jnp.exp(jnp.where(r >= c, gcc[i] - gcr[i], NEG)) for i in range(n)]
    qs = [q[i] * (HEAD ** -0.5) for i in range(n)]
    ws = [ops.nn(w[i], state[i]) for i in range(n)]
    qst = [ops.nn(qs[i] * jnp.exp(gcc[i]), state[i]) for i in range(n)]
    attn = [ops.nt(qs[i], k[i]) * decay[i] for i in range(n)]
    u = [u_hat[i] - ws[i] for i in range(n)]
    o = [qst[i] + ops.nn(attn[i], u[i]) for i in range(n)]
    kdu = [ops.tn(k[i] * jnp.exp(gl[i] - gcc[i]), u[i]) for i in range(n)]
    new_state = [state[i] * jnp.exp(gl[i]) + kdu[i] for i in range(n)]
    return tuple(o), tuple(new_state)


INTER_HEADS = 8
INTER_ROWS = 512
INTER_ROWS_BWD = 256


def _inter_heads(H):
    return INTER_HEADS if H % INTER_HEADS == 0 else (4 if H % 4 == 0 else 1)


def _inter_specs(ts, ns, hp, backward):
    at = (lambda s: ns - 1 - s) if backward else (lambda s: s)
    nc = ts // CHUNK
    qk = []
    for hh in range(hp):
        qk.append(pl.BlockSpec((ts, HEAD), lambda b, g, s, hh=hh: (b * ns + at(s), 3 * (hp * g + hh))))
        qk.append(pl.BlockSpec((ts, HEAD), lambda b, g, s, hh=hh: (b * ns + at(s), 3 * (hp * g + hh) + 1)))
    heads = pl.BlockSpec((ts, hp * HEAD), lambda b, g, s: (b * ns + at(s), g))
    rowspec = pl.BlockSpec((None, hp, nc, 1, CHUNK), lambda b, g, s: (b, g, at(s), 0, 0))
    stspec = pl.BlockSpec((None, hp, nc, HEAD, HEAD), lambda b, g, s: (b, g, at(s), 0, 0))
    return qk, heads, rowspec, stspec


def gdn_inter_fwd(qkvn, u_hat, w, gcr5, p_gz, gnorm, B, S, H):
    T = B * S
    N = S // CHUNK
    hp = _inter_heads(H)
    hs = range(hp)
    ts = _tile(S, INTER_ROWS, CHUNK)
    ns, nc = S // ts, ts // CHUNK

    def body(*refs):
        qk_refs, (uh_ref, w_ref, gr_ref, z_ref, gn_ref, o_ref, st_ref, y_ref, s_scr) = refs[:2 * hp], refs[2 * hp:]

        @pl.when(pl.program_id(2) == 0)
        def _():
            s_scr[...] = jnp.zeros_like(s_scr)

        gn = gn_ref[...]

        def step(n, c):
            rows = pl.ds(pl.multiple_of(n * CHUNK, CHUNK), CHUNK)
            st = tuple(s_scr[hh] for hh in hs)
            for hh in hs:
                st_ref[hh, n] = st[hh]
            o, new = _inter_fn(tuple(qk_refs[2 * hh][rows, :] for hh in hs), tuple(qk_refs[2 * hh + 1][rows, :] for hh in hs),
                               tuple(uh_ref[rows, hh * HEAD:(hh + 1) * HEAD] for hh in hs),
                               tuple(w_ref[rows, hh * HEAD:(hh + 1) * HEAD] for hh in hs),
                               tuple(gr_ref[hh, n] for hh in hs), st, _RawOps)
            for hh in hs:
                cols = slice(hh * HEAD, (hh + 1) * HEAD)
                o_ref[rows, cols] = o[hh]
                s_scr[hh] = new[hh]
                z = z_ref[rows, cols]
                r = lax.rsqrt(jnp.mean(o[hh] * o[hh], axis=-1, keepdims=True) + EPS)
                y_ref[rows, cols] = (o[hh] * r * gn * z * _sigmoid(z)).astype(BF16)
            return c

        lax.fori_loop(0, nc, step, 0)

    qk, heads, rowspec, stspec = _inter_specs(ts, ns, hp, backward=False)
    return pl.pallas_call(
        body, name="gdn_inter_fwd", grid=(B, H // hp, ns),
        in_specs=qk + [heads, heads, rowspec, heads, pl.BlockSpec((1, HEAD), lambda b, g, s: (0, 0))],
        out_specs=[heads, stspec, heads],
        out_shape=[jax.ShapeDtypeStruct((T, H * HEAD), F32), jax.ShapeDtypeStruct((B, H, N, HEAD, HEAD), F32),
                   jax.ShapeDtypeStruct((T, H * HEAD), BF16)],
        scratch_shapes=[pltpu.VMEM((hp, HEAD, HEAD), F32)],
        compiler_params=_params("parallel", "parallel", "arbitrary"),
    )(*([qkvn] * (2 * hp)), u_hat, w, gcr5, p_gz, gnorm)


def gdn_inter_bwd(qkvn, u_hat, w, gcr5, states, o, p_gz, gnorm, d_y, B, S, H, rider=None):
    T = B * S
    N = S // CHUNK
    hp = _inter_heads(H)
    hs = range(hp)
    ts = _tile(S, INTER_ROWS_BWD, CHUNK)
    ns, nc = S // ts, ts // CHUNK

    def body(*refs):
        qk_refs = refs[:2 * hp]
        (uh_ref, w_ref, gr_ref, st_ref, o_ref, z_ref, gn_ref, dy_ref,
         dq_ref, dk_ref, duh_ref, dw_ref, dgr_ref, dz_ref, dgn_ref, ds_scr) = refs[2 * hp:]

        @pl.when(pl.program_id(2) == 0)
        def _():
            ds_scr[...] = jnp.zeros_like(ds_scr)
            dgn_ref[...] = jnp.zeros_like(dgn_ref)

        cols = [slice(hh * HEAD, (hh + 1) * HEAD) for hh in hs]
        gn = gn_ref[...]

        def through_norm(rows, hh):
            ov, z, d = o_ref[rows, cols[hh]], z_ref[rows, cols[hh]], dy_ref[rows, cols[hh]]
            r = lax.rsqrt(jnp.mean(ov * ov, axis=-1, keepdims=True) + EPS)
            xh = ov * r
            sg = _sigmoid(z)
            d_n = d * (z * sg)
            dz_ref[rows, cols[hh]] = (d * xh * gn * (sg + z * sg * (1.0 - sg))).astype(BF16)
            dgn_ref[0:1, :] += jnp.sum(d_n * xh, axis=0, keepdims=True)
            dxh = d_n * gn
            return r * (dxh - xh * jnp.mean(dxh * xh, axis=-1, keepdims=True))

        def step(i, c):
            n = nc - 1 - i
            rows = pl.ds(pl.multiple_of(n * CHUNK, CHUNK), CHUNK)
            _, vjp = jax.vjp(functools.partial(_inter_fn, ops=_DiffOps),
                             tuple(qk_refs[2 * hh][rows, :] for hh in hs), tuple(qk_refs[2 * hh + 1][rows, :] for hh in hs),
                             tuple(uh_ref[rows, cols[hh]] for hh in hs), tuple(w_ref[rows, cols[hh]] for hh in hs),
                             tuple(gr_ref[hh, n] for hh in hs), tuple(st_ref[hh, n] for hh in hs))
            dq, dk, duh, dw, dgr, ds = vjp((tuple(through_norm(rows, hh) for hh in hs), tuple(ds_scr[hh] for hh in hs)))
            for hh in hs:
                dq_ref[rows, cols[hh]] = dq[hh]
                dk_ref[rows, cols[hh]] = dk[hh]
                duh_ref[rows, cols[hh]] = duh[hh]
                dw_ref[rows, cols[hh]] = dw[hh]
                dgr_ref[hh, n] = dgr[hh]
                ds_scr[hh] = ds[hh]
            return c

        lax.fori_loop(0, nc, step, 0)

    qk, heads, rowspec, stspec = _inter_specs(ts, ns, hp, backward=True)
    hshape = jax.ShapeDtypeStruct((T, H * HEAD), F32)
    return _hosted_call(
        body, rider, name="gdn_inter_bwd", grid=(B, H // hp, ns),
        in_specs=qk + [heads, heads, rowspec, stspec, heads, heads, pl.BlockSpec((1, HEAD), lambda b, g, s: (0, 0)), heads],
        out_specs=[heads, heads, heads, heads, rowspec, heads,
                   pl.BlockSpec((None, None, 8, HEAD), lambda b, g, s: (b, g, 0, 0))],
        out_shape=[hshape, hshape, hshape, hshape, jax.ShapeDtypeStruct((B, H, N, 1, CHUNK), F32),
                   jax.ShapeDtypeStruct((T, H * HEAD), BF16), jax.ShapeDtypeStruct((B, H // hp, 8, HEAD), F32)],
        scratch_shapes=[pltpu.VMEM((hp, HEAD, HEAD), F32)], semantics=("parallel", "parallel", "arbitrary"),
    )(*([qkvn] * (2 * hp)), u_hat, w, gcr5, states, o, p_gz, gnorm, d_y)


def adamw(w, g, m, v, name):
    shape = w.shape
    lead = (None,) * (w.ndim - 2)
    zeros = (0,) * (w.ndim - 2)
    R, C = shape[-2:]
    g2 = g.reshape(R, C)
    tr, tc = _tile(R, 128, 8), C
    if tr % 8 and R > 8:
        tr, tc = R, _tile(C, 128)

    def body(w_ref, g_ref, m_ref, v_ref, d_ref, nm_ref, nv_ref):
        gv = g_ref[...]
        nm = ADAM_B1 * m_ref[...] + (1.0 - ADAM_B1) * gv
        nv = ADAM_B2 * v_ref[...] + (1.0 - ADAM_B2) * (gv * gv)
        m_hat = nm / (1.0 - ADAM_B1 ** ADAM_STEP)
        v_hat = nv / (1.0 - ADAM_B2 ** ADAM_STEP)
        d_ref[...] = -ADAM_LR * (m_hat / (jnp.sqrt(v_hat) + ADAM_EPS) + ADAM_WD * w_ref[...])
        nm_ref[...] = nm
        nv_ref[...] = nv

    blk = pl.BlockSpec(lead + (tr, tc), lambda i, j: zeros + (i, j))
    gblk = pl.BlockSpec((tr, tc), lambda i, j: (i, j))
    sh = jax.ShapeDtypeStruct(shape, F32)
    return pl.pallas_call(
        body, name=name, grid=(R // tr, C // tc), in_specs=[blk, gblk, blk, blk], out_specs=[blk] * 3, out_shape=[sh] * 3,
        compiler_params=_params("parallel", "parallel"),
    )(w, g2, m, v)


def _place():
    x, y, c = lax.axis_index("x"), lax.axis_index("y"), lax.axis_index("c")
    chips = [(1 - x, y), (x, 1 - y), (1 - x, 1 - y)]
    return x, y, c, chips


_HBM = pl.BlockSpec(memory_space=pltpu.HBM)


class _Rider:
    def __init__(self, inputs, out_shapes, n_sems, sends, recvs, aliases=None):
        self.inputs, self.out_shapes, self.n_sems = list(inputs), list(out_shapes), n_sems
        self.sends, self.recvs, self.aliases = sends, recvs, aliases or {}

    def start(self, *refs):
        for cp in self.sends(*refs):
            cp.start()

    def wait(self, *refs):
        for cp in self.recvs(*refs):
            cp.wait_recv()
        for cp in self.sends(*refs):
            cp.wait_send()


def _remote(src, dst, send_sems, recv_sems, k, to):
    return pltpu.make_async_remote_copy(src_ref=src, dst_ref=dst, send_sem=send_sems.at[k], recv_sem=recv_sems.at[k],
                                        device_id=to, device_id_type=MESH)


def _run_alone(rider, name):
    ri = len(rider.inputs)

    def body(*refs):
        ins, outs, (send_sems, recv_sems) = refs[:ri], refs[ri:-2], refs[-2:]
        rider.start(ins, outs, send_sems, recv_sems)
        rider.wait(ins, outs, send_sems, recv_sems)

    return pl.pallas_call(
        body, name=name, in_specs=[_HBM] * ri, out_specs=[_HBM] * len(rider.out_shapes), out_shape=rider.out_shapes,
        scratch_shapes=[pltpu.SemaphoreType.DMA((rider.n_sems,))] * 2, input_output_aliases=rider.aliases,
    )(*rider.inputs)


def _hosted_call(body, rider, *, name, grid, in_specs, out_specs, out_shape, scratch_shapes, semantics):
    if rider is None:
        return pl.pallas_call(body, name=name, grid=grid, in_specs=in_specs, out_specs=out_specs, out_shape=out_shape,
                              scratch_shapes=scratch_shapes, compiler_params=_params(*semantics))
    n_in, n_out, n_scr = len(in_specs), len(out_specs), len(scratch_shapes)
    ri, ro = len(rider.inputs), len(rider.out_shapes)

    def hosted(*refs):
        parts, p = [], 0
        for cnt in (n_in, ri, n_out, ro, n_scr, 2):
            parts.append(refs[p:p + cnt])
            p += cnt
        ins, rins, outs, routs, scr, (send_sems, recv_sems) = parts
        first = functools.reduce(jnp.logical_and, [pl.program_id(a) == 0 for a in range(len(grid))])
        last = functools.reduce(jnp.logical_and, [pl.program_id(a) == grid[a] - 1 for a in range(len(grid))])

        @pl.when(first)
        def _():
            rider.start(rins, routs, send_sems, recv_sems)

        body(*ins, *outs, *scr)

        @pl.when(last)
        def _():
            rider.wait(rins, routs, send_sems, recv_sems)

    call = pl.pallas_call(
        hosted, name=name, grid=grid, in_specs=list(in_specs) + [_HBM] * ri, out_specs=list(out_specs) + [_HBM] * ro,
        out_shape=list(out_shape) + rider.out_shapes,
        scratch_shapes=list(scratch_shapes) + [pltpu.SemaphoreType.DMA((rider.n_sems,))] * 2,
        input_output_aliases={n_in + i: n_out + o for i, o in rider.aliases.items()},
        compiler_params=_params(*(("arbitrary",) * len(grid))))

    def run(*args):
        res = call(*args, *rider.inputs)
        return res[:n_out], res[n_out:]

    return run


def _ride_gather_ici(packs):
    n = len(packs)

    def sends(ins, outs, send_sems, recv_sems):
        x, y, c, chips = _place()
        return [_remote(ins[a].at[c], outs[a].at[2 * x + y, c], send_sems, recv_sems, 3 * a + j, (*chip, c))
                for a in range(n) for j, chip in enumerate(chips)]

    def recvs(ins, outs, send_sems, recv_sems):
        x, y, c, chips = _place()
        return [_remote(ins[a].at[c], outs[a].at[2 * chip[0] + chip[1], c], send_sems, recv_sems, 3 * a + j, (x, y, c))
                for a in range(n) for j, chip in enumerate(chips)]

    return _Rider(packs, [jax.ShapeDtypeStruct((N_CHIP,) + p.shape, p.dtype) for p in packs], 3 * n, sends, recvs)


def _ride_gather_d2d(gathered):
    n = len(gathered)

    def copies(landing_half, to):
        def build(ins, outs, send_sems, recv_sems):
            x, y, c, chips = _place()
            return [_remote(ins[a].at[2 * chip[0] + chip[1], c], outs[a].at[2 * chip[0] + chip[1], landing_half(c)],
                            send_sems, recv_sems, 3 * a + j, to(x, y, c))
                    for a in range(n) for j, chip in enumerate(chips)]
        return build

    return _Rider(gathered, [jax.ShapeDtypeStruct(g.shape, g.dtype) for g in gathered], 3 * n,
                  copies(lambda c: c, lambda x, y, c: (x, y, 1 - c)), copies(lambda c: 1 - c, lambda x, y, c: (x, y, c)),
                  aliases={a: a for a in range(n)})


def _ride_exchange(gs):
    n = len(gs)
    shapes = [g.shape[1:] if g.ndim == 4 else g.shape[:2] + (g.shape[2] // 2,) for g in gs]

    def copies(ins, outs, send_sems, recv_sems):
        x, y, c, _ = _place()

        def theirs(a):
            if gs[a].ndim == 4:
                return ins[a].at[1 - c]
            cols = shapes[a][2]
            return ins[a].at[:, :, pl.ds((1 - c) * cols, cols)]

        return [_remote(theirs(a), outs[a], send_sems, recv_sems, a, (x, y, 1 - c)) for a in range(n)]

    return _Rider(gs, [jax.ShapeDtypeStruct(sh, g.dtype) for sh, g in zip(shapes, gs)], n, copies, copies)


def _ride_scatter(b16s, to=(0, 1, 2), landing=None):
    n = len(b16s)

    def sends(ins, outs, send_sems, recv_sems):
        x, y, c, chips = _place()
        return [_remote(ins[a].at[2 * chips[j][0] + chips[j][1]], outs[a].at[2 * x + y], send_sems, recv_sems, 3 * a + j,
                        (*chips[j], c)) for a in range(n) for j in to]

    def recvs(ins, outs, send_sems, recv_sems):
        x, y, c, chips = _place()
        return [_remote(ins[a].at[2 * x + y], outs[a].at[2 * chips[j][0] + chips[j][1]], send_sems, recv_sems, 3 * a + j,
                        (x, y, c)) for a in range(n) for j in to]

    return _Rider(list(b16s) + list(landing or []), [jax.ShapeDtypeStruct(b.shape, b.dtype) for b in b16s], 3 * n,
                  sends, recvs, aliases={n + a: a for a in range(n)} if landing else None)


def _slab_tile(r, cols):
    tr = _tile(r, 256, 16)
    if tr % 16 == 0:
        return tr, cols
    return r, _tile(cols, 128)


def add_halves(g, got, idx, name):
    ns, r, cols = got.shape
    tr, tc = _slab_tile(r, cols)
    if g.ndim == 4:
        mine = pl.BlockSpec((None, None, tr, tc), lambda s, i, j, idx_ref: (idx_ref[0], s, i, j))
    else:
        mine = pl.BlockSpec((None, tr, tc), lambda s, i, j, idx_ref: (s, i, idx_ref[0] * (cols // tc) + j))

    def body(idx_ref, a_ref, b_ref, o32_ref, o16_ref):
        s = a_ref[...] + b_ref[...]
        o32_ref[...] = s
        o16_ref[...] = s.astype(BF16)

    blk = pl.BlockSpec((None, tr, tc), lambda s, i, j, idx_ref: (s, i, j))
    return pl.pallas_call(
        body, name=name,
        grid_spec=pltpu.PrefetchScalarGridSpec(
            num_scalar_prefetch=1, grid=(ns, r // tr, cols // tc),
            in_specs=[mine, blk], out_specs=[blk, blk]),
        out_shape=[jax.ShapeDtypeStruct((ns, r, cols), F32), jax.ShapeDtypeStruct((ns, r, cols), BF16)],
        compiler_params=_params("parallel", "parallel", "parallel"),
    )(idx, g, got)


def add_chips(a32, got16, idx, name):
    ns, r, cols = a32.shape
    tr, tc = _slab_tile(r, cols)

    def body(idx_ref, a_ref, r1_ref, r2_ref, r3_ref, o_ref):
        o_ref[...] = ((a_ref[...] + r1_ref[...].astype(F32)) + r2_ref[...].astype(F32)) + r3_ref[...].astype(F32)

    def slab(k):
        return pl.BlockSpec((None, tr, tc), lambda i, j, idx_ref: ((idx_ref[1] + k) % ns, i, j))

    return pl.pallas_call(
        body, name=name,
        grid_spec=pltpu.PrefetchScalarGridSpec(
            num_scalar_prefetch=1, grid=(r // tr, cols // tc), in_specs=[slab(0), slab(1), slab(2), slab(3)],
            out_specs=pl.BlockSpec((tr, tc), lambda i, j, idx_ref: (i, j))),
        out_shape=jax.ShapeDtypeStruct((r, cols), F32),
        compiler_params=_params("parallel", "parallel"),
    )(idx, a32, got16, got16, got16)


def share_halves(halves):
    n = len(halves)

    def body(*refs):
        in_refs, out_refs, (send_sems, recv_sems) = refs[:n], refs[n:2 * n], refs[2 * n:]
        x, y, c, _ = _place()
        cps = [pltpu.make_async_remote_copy(src_ref=in_refs[a], dst_ref=out_refs[a], send_sem=send_sems.at[a],
                                            recv_sem=recv_sems.at[a], device_id=(x, y, 1 - c), device_id_type=MESH)
               for a in range(n)]
        for cp in cps:
            cp.start()
        for cp in cps:
            cp.wait()

    return pl.pallas_call(
        body, name="share_halves", in_specs=[_HBM] * n, out_specs=[_HBM] * n,
        out_shape=[jax.ShapeDtypeStruct(h.shape, F32) for h in halves],
        scratch_shapes=[pltpu.SemaphoreType.DMA((n,)), pltpu.SemaphoreType.DMA((n,))],
    )(*halves)


def allreduce_small(v):
    R, _ = v.shape

    def body(in_ref, out_ref, slots, send_sems, recv_sems):
        x, y, c, _ = _place()
        me = 4 * x + 2 * y + c
        slots[me] = in_ref[...]
        cps = []
        for k in range(1, N_DEV):
            to = (x ^ (k >> 2), y ^ ((k >> 1) & 1), c ^ (k & 1))
            cps.append(pltpu.make_async_remote_copy(src_ref=in_ref, dst_ref=slots.at[me], send_sem=send_sems.at[k - 1],
                                                    recv_sem=recv_sems.at[k - 1], device_id=to, device_id_type=MESH))
        for cp in cps:
            cp.start()
        for k in range(1, N_DEV):
            frm = 4 * (x ^ (k >> 2)) + 2 * (y ^ ((k >> 1) & 1)) + (c ^ (k & 1))
            pltpu.make_async_remote_copy(src_ref=in_ref, dst_ref=slots.at[frm], send_sem=send_sems.at[k - 1],
                                         recv_sem=recv_sems.at[k - 1], device_id=(x, y, c), device_id_type=MESH).wait_recv()
        for cp in cps:
            cp.wait_send()
        acc = slots[0]
        for d in range(1, N_DEV):
            acc = acc + slots[d]
        out_ref[...] = acc

    vm = pl.BlockSpec(memory_space=pltpu.VMEM)
    return pl.pallas_call(
        body, name="allreduce_small", in_specs=[vm], out_specs=vm, out_shape=jax.ShapeDtypeStruct((R, ROW), F32),
        scratch_shapes=[pltpu.VMEM((N_DEV, R, ROW), F32), pltpu.SemaphoreType.DMA((N_DEV - 1,)),
                        pltpu.SemaphoreType.DMA((N_DEV - 1,))],
    )(v)


def _rows_of(n, unit=16):
    return -(-n // (unit * ROW)) * unit


def _pack_rows(items, total_rows, dtype, unit=16):
    parts = []
    used = 0
    for a in items:
        flat = a.reshape(-1)
        r = _rows_of(flat.shape[0], unit)
        flat = jnp.pad(flat, (0, r * ROW - flat.shape[0]))
        parts.append(flat.reshape(r, ROW))
        used += r
    if total_rows > used:
        parts.append(jnp.zeros((total_rows - used, ROW), dtype))
    return jnp.concatenate(parts, axis=0)


def _unpack_rows(buf, shapes, unit=16):
    lead = buf.shape[:-2]
    out = []
    off = 0
    for shp in shapes:
        n = math.prod(shp)
        r = _rows_of(n, unit)
        piece = buf[..., off:off + r, :].reshape(*lead, r * ROW)[..., :n].reshape(*lead, *shp)
        out.append(piece)
        off += r
    return out


def _interleave_heads(w, H):
    lead = w.shape[:-1]
    return w.reshape(*lead, 3, H, HEAD).swapaxes(-3, -2).reshape(*lead, 3 * H * HEAD)


def _deinterleave_heads(w, H):
    lead = w.shape[:-1]
    return w.reshape(*lead, H, 3, HEAD).swapaxes(-3, -2).reshape(*lead, 3 * H * HEAD)


def _interleave_head_rows(w, H):
    return w.reshape(3, H, HEAD, w.shape[-1]).swapaxes(0, 1).reshape(3 * H * HEAD, w.shape[-1])


def _deinterleave_head_rows(w, H):
    return w.reshape(H, 3, HEAD, w.shape[-1]).swapaxes(0, 1).reshape(3 * H * HEAD, w.shape[-1])


def kernel(x, norm_mix, w_in, fox_f_bias, gdn_conv_w, gdn_a_log, gdn_dt_bias, gdn_norm, w_branch_fox, w_branch_gdn, w_out, norm_ffn, w_up, ffn_conv_w, w_down, norm_final, loss_target, m_norm_mix, m_w_in, m_fox_f_bias, m_gdn_conv_w, m_gdn_a_log, m_gdn_dt_bias, m_gdn_norm, m_w_branch_fox, m_w_branch_gdn, m_w_out, m_norm_ffn, m_w_up, m_ffn_conv_w, m_w_down, m_norm_final, v_norm_mix, v_w_in, v_fox_f_bias, v_gdn_conv_w, v_gdn_a_log, v_gdn_dt_bias, v_gdn_norm, v_w_branch_fox, v_w_branch_gdn, v_w_out, v_norm_ffn, v_w_up, v_ffn_conv_w, v_w_down, v_norm_final):
    B, S, D = x.shape
    T = B * S
    H = D // HEAD
    N = S // CHUNK
    FF = w_down.shape[1] * N_CHIP
    d_in = 9 * D + 3 * H
    assert w_in.shape[2] * N_CHIP == d_in and 3 * H <= 128

    cidx = lax.axis_index("c").astype(jnp.int32)
    sidx = (2 * lax.axis_index("x") + lax.axis_index("y")).astype(jnp.int32)
    idx = jnp.stack([cidx, sidx])

    rowed = [w_branch_fox[0], w_branch_gdn[0], w_out[0], w_down[0]]
    convs = [gdn_conv_w[0], ffn_conv_w[0]]
    rowed_shapes = [a.shape for a in rowed]
    conv_shapes = [a.shape + (2,) for a in convs]
    pad_rows = lambda shapes: -(-sum(_rows_of(math.prod(s)) for s in shapes) // 256) * 128
    Rh, Rc = pad_rows(rowed_shapes), pad_rows(conv_shapes)
    halves = lambda a: a.reshape(2, a.shape[0] // 2, a.shape[1])
    c_in = w_in.shape[2]
    packs_a = [w_in[0].T.astype(BF16).reshape(c_in, 2, D // 2).transpose(1, 0, 2),
               halves(_pack_rows([lax.bitcast_convert_type(a, BF16) for a in convs], 2 * Rc, BF16))]
    packs_b = [halves(w_up[0].astype(BF16)), halves(_pack_rows([a.astype(BF16) for a in rowed], 2 * Rh, BF16))]
    own = lambda gs, ps: [lax.dynamic_update_slice(g, p[None], (sidx, 0, 0, 0)) for g, p in zip(gs, ps)]
    by_cols = lambda g: g.transpose(1, 2, 0, 3).reshape(2 * g.shape[2], N_CHIP * g.shape[3])
    cat_cols = lambda p: jnp.concatenate([p[i] for i in range(N_CHIP)], axis=-1)
    cat_rows = lambda p: p.reshape(-1, p.shape[-1])
    x2 = x.reshape(T, D)
    hn1, landed = rmsnorm_fwd(x2, norm_mix, "rmsnorm_mix", rider=_ride_gather_ici(packs_a))
    g_in, g_conv = own(_run_alone(_ride_gather_d2d(landed), "gather_to_sibling"), packs_a)
    W_inT = g_in.transpose(0, 2, 1, 3).reshape(N_CHIP * c_in, D)
    conv_parts = _unpack_rows(g_conv.reshape(N_CHIP, 2 * Rc, ROW), conv_shapes)
    gconv = cat_cols(lax.bitcast_convert_type(conv_parts[0], F32))
    fconv = cat_cols(lax.bitcast_convert_type(conv_parts[1], F32))

    o1, o2 = 3 * D, 3 * D + H
    o3, o4, o5, o6 = o2 + 3 * D, o2 + 3 * D + H, o2 + 3 * D + 2 * H, o2 + 4 * D + 2 * H
    W_foxT = _interleave_head_rows(W_inT[:o1], H)
    W_gqkvT = _interleave_head_rows(W_inT[o2:o3], H)
    W_gzT = W_inT[o5:o6]
    W_gatesT = W_inT[o6:]
    W_smallT = jnp.concatenate([W_inT[o1:o2], W_inT[o3:o5], jnp.zeros((128 - 3 * H, D), BF16)], axis=0)
    gconv_i = _interleave_heads(gconv, H)
    fconv_g, fconv_v = fconv[:, :FF], fconv[:, FF:]
    prm = jnp.zeros((8, 128), F32)
    prm = prm.at[0, 0:H].set(fox_f_bias[0]).at[0, H:2 * H].set(gdn_dt_bias[0]).at[1, H:2 * H].set(gdn_a_log[0])

    tgt = loss_target.reshape(T, D)

    p_fox = matmul(hn1, W_foxT, "nt", "proj_fox", out_dtype=BF16)
    p_gqkv = matmul(hn1, W_gqkvT, "nt", "proj_gqkv")
    p_gz = matmul(hn1, W_gzT, "nt", "proj_gz")
    p_gates = matmul(hn1, W_gatesT, "nt", "proj_gates")
    p_small = matmul(hn1, W_smallT, "nt", "proj_small")

    sm = small_fwd(p_small, prm, B, S, H)
    heads = lambda a: a.reshape(B, S, H).transpose(0, 2, 1)
    c_bhs, gc_bhs, beta_bhs = heads(sm[:, 0:H]), heads(sm[:, H:2 * H]), heads(sm[:, 2 * H:3 * H])
    c_col, c_row = c_bhs[..., None], c_bhs[:, :, None, :]
    gcr5 = gc_bhs.reshape(B, H, N, 1, CHUNK)
    gcr_u = gc_bhs.reshape(B, H, N // PAIR, 1, PAIR * CHUNK)
    betar_u = beta_bhs.reshape(B, H, N // PAIR, 1, PAIR * CHUNK)

    (o_fox, o_fox16, lse), arriving = fox_fwd(p_fox, c_col, c_row, B, S, H, rider=_ride_gather_ici(packs_b))
    qkvn = gdn_prep_fwd(p_gqkv, gconv_i, B, S, H)
    (u_hat, w_t, t_inv), arrived = gdn_intra_fwd(qkvn, betar_u, gcr_u, B, S, H, rider=_ride_gather_d2d(arriving))
    g_up, g_rowed = own(arrived, packs_b)
    W_up = by_cols(g_up)
    W_up_g, W_up_v = W_up[:, :FF], W_up[:, FF:]
    W_bf, W_bg, W_out, W_down = (cat_rows(p) for p in _unpack_rows(g_rowed.reshape(N_CHIP, 2 * Rh, ROW), rowed_shapes))
    o_gdn, states, y_gdn = gdn_inter_fwd(qkvn, u_hat, w_t, gcr5, p_gz, gdn_norm, B, S, H)
    bf_ = matmul(o_fox16, W_bf, "nn", "branch_fox")
    bg_, y = matmul(y_gdn, W_bg, "nn", "branch_gdn", post=_post_merge(p_gates, bf_))
    h1, hn2 = matmul(y, W_out, "nn", "out_proj", add=x2, post=_post_rmsnorm(norm_ffn))
    up_g = matmul(hn2, W_up_g, "nn", "up_gate")
    up_v = matmul(hn2, W_up_v, "nn", "up_val")
    act = ffn_gate_fwd(up_g, up_v, fconv_g, fconv_v, B, S)
    dh2, dh2_16, loss_cols, d_norm_final = matmul(act, W_down, "nn", "down_proj", add=h1,
                                                  post=_post_loss(norm_final.reshape(1, D), tgt))
    loss_here = (0.5 / D) * jnp.sum(loss_cols)

    d_act = matmul(dh2_16, W_down, "nt", "d_act")
    dW_down = matmul(act, dh2_16, "tn", "dw_down")
    d_upg, d_upv, d_fconv_g, d_fconv_v = ffn_gate_bwd(up_g, up_v, fconv_g, fconv_v, d_act, B, S)
    d_hn2 = matmul(d_upg, W_up_g, "nt", "d_hn2_g")
    dh1, dh1_16, d_norm_ffn = matmul(d_upv, W_up_v, "nt", "d_hn2_v", add=d_hn2,
                                     post=_post_rmsnorm_bwd(h1, norm_ffn, dh2, True))
    dW_up_slabs = jnp.concatenate([matmul(hn2, d_upg, "tn", "dw_up_g", slab=(D // 2, 2 * FF // N_CHIP)),
                                   matmul(hn2, d_upv, "tn", "dw_up_v", slab=(D // 2, 2 * FF // N_CHIP))], axis=1)
    d_bf, d_bg, d_gates = matmul(dh1_16, W_out, "nt", "d_y", post=_post_merge_bwd(p_gates, bf_, bg_))
    dW_out = matmul(y, dh1_16, "tn", "dw_out")
    d_ofox = matmul(d_bf, W_bf, "nt", "d_ofox")
    dW_bf = matmul(o_fox16, d_bf, "tn", "dw_bf")
    d_ygdn = matmul(d_bg, W_bg, "nt", "d_ygdn")
    dW_bg = matmul(y_gdn, d_bg, "tn", "dw_bg")

    d_fconv = jnp.concatenate([d_fconv_g, d_fconv_v], axis=1)
    col_shard = lambda g, s: g[:, s * (g.shape[1] // N_CHIP):(s + 1) * (g.shape[1] // N_CHIP)]
    row_shard = lambda g, s: g[s * (g.shape[0] // N_CHIP):(s + 1) * (g.shape[0] // N_CHIP)]
    shard_items = lambda s: [row_shard(dW_bf, s), row_shard(dW_bg, s), row_shard(dW_out, s), row_shard(dW_down, s),
                             col_shard(d_fconv, s)]
    g_shapes = [a.shape for a in shard_items(0)]
    assert sum(_rows_of(math.prod(s)) for s in g_shapes) <= 2 * Rh
    to_slabs = lambda g: g.reshape(2, g.shape[0] // 2, N_CHIP, g.shape[1] // N_CHIP).transpose(0, 2, 1, 3)
    gpacks_b = [dW_up_slabs,
                jnp.stack([_pack_rows(shard_items(s), 2 * Rh, F32).reshape(2, Rh, ROW) for s in range(N_CHIP)], axis=1)]
    (d_pfox, d_ccol, d_crow), gots_b = fox_bwd(p_fox, c_col, c_row, o_fox, lse, d_ofox, B, S, H,
                                              rider=_ride_exchange(gpacks_b))
    sums_b = [add_halves(g, got, idx, "add_halves_b%d" % i) for i, (g, got) in enumerate(zip(gpacks_b, gots_b))]

    (dq_i, dk_i, d_uh, d_wt, dgcr_a, d_gz, d_gn_parts), got16_b = gdn_inter_bwd(
        qkvn, u_hat, w_t, gcr5, states, o_gdn, p_gz, gdn_norm, d_ygdn, B, S, H,
        rider=_ride_scatter([s16 for _, s16 in sums_b]))
    d_gdn_norm = jnp.sum(d_gn_parts[:, :, 0, :], axis=(0, 1))[None]
    mine_b = [add_chips(s32, g16, idx, "add_chips_b%d" % i) for i, ((s32, _), g16) in enumerate(zip(sums_b, got16_b))]
    d_qkvn, d_betar5, dgcr_b = gdn_intra_bwd(qkvn, betar_u, gcr_u, t_inv, d_uh, d_wt, dq_i, dk_i, B, S, H)
    d_pgqkv, d_gconv_i = gdn_prep_bwd(p_gqkv, gconv_i, d_qkvn, B, S, H)

    tokens = lambda a: a.reshape(B, H, S).transpose(0, 2, 1).reshape(T, H)
    d_gc = dgcr_a.reshape(B, H, S) + dgcr_b.reshape(B, H, S)
    d_sm = jnp.concatenate([tokens(d_ccol.reshape(B, H, S) + d_crow.reshape(B, H, S)), tokens(d_gc), tokens(d_betar5.reshape(B, H, S)),
                            jnp.zeros((T, 128 - 3 * H), F32)], axis=1)
    d_psmall, d_prm = small_bwd(p_small, prm, d_sm, B, S, H)

    dW_foxT = matmul(d_pfox, hn1, "tn", "dw_fox")
    dW_gqkvT = matmul(d_pgqkv, hn1, "tn", "dw_gqkv")
    dW_gzT = matmul(d_gz, hn1, "tn", "dw_gz")
    dW_gatesT = matmul(d_gates, hn1, "tn", "dw_gates")
    dW_smallT = matmul(d_psmall, hn1, "tn", "dw_small")
    dW_inT = jnp.concatenate([_deinterleave_head_rows(dW_foxT, H), dW_smallT[0:H], _deinterleave_head_rows(dW_gqkvT, H),
                              dW_smallT[H:3 * H], dW_gzT, dW_gatesT], axis=0)
    d_gconv = _deinterleave_heads(d_gconv_i, H)

    gpack_a = [dW_inT.reshape(N_CHIP, c_in, D)]
    d_hn1, gots_a = matmul(d_pfox, W_foxT, "nn", "d_hn1_fox", rider=_ride_exchange(gpack_a))
    sums_a = [add_halves(gpack_a[0], gots_a[0], idx, "add_halves_a")]
    d_hn1, landing_a = matmul(d_pgqkv, W_gqkvT, "nn", "d_hn1_gqkv", add=d_hn1,
                              rider=_ride_scatter([sums_a[0][1]], to=(0, 1)))
    d_hn1 = matmul(d_gz, W_gzT, "nn", "d_hn1_gz", add=d_hn1)
    d_hn1, got16_a = matmul(d_gates, W_gatesT, "nn", "d_hn1_gates", add=d_hn1,
                            rider=_ride_scatter([sums_a[0][1]], to=(2,), landing=landing_a))
    mine = [add_chips(sums_a[0][0], got16_a[0], idx, "add_chips_a")] + mine_b
    grad_x, d_norm_mix = matmul(d_psmall, W_smallT, "nn", "d_hn1_small", add=d_hn1,
                                post=_post_rmsnorm_bwd(x2, norm_mix, dh1, False))

    others = share_halves(mine)
    g_w_inT, g_up, g_rows = (jnp.concatenate([jnp.where(cidx == 0, h, o), jnp.where(cidx == 0, o, h)], axis=ax)
                             for h, o, ax in zip(mine, others, (1, 0, 0)))
    g_w_in = g_w_inT.T
    g_bf, g_bg, g_out, g_down, g_fconv = _unpack_rows(g_rows, g_shapes)

    small_items = [d_norm_mix, d_norm_ffn, d_norm_final, d_gdn_norm, d_prm, d_gconv, loss_here.reshape(1, 1)]
    small_shapes = [a.shape for a in small_items]
    sv = allreduce_small(_pack_rows(small_items, 0, F32, unit=8))
    g_norm_mix, g_norm_ffn, g_norm_final, g_gdn_norm, g_prm, g_gconv_all, loss = _unpack_rows(sv, small_shapes, unit=8)
    loss = loss[0, 0]
    g_norm_final = g_norm_final.reshape(D)
    g_fbias, g_dtb, g_alog = g_prm[0:1, 0:H], g_prm[0:1, H:2 * H], g_prm[1:2, H:2 * H]
    g_gconv = lax.dynamic_slice_in_dim(g_gconv_all, sidx * (3 * D // N_CHIP), 3 * D // N_CHIP, axis=1)

    names = ["norm_mix", "w_in", "fox_f_bias", "gdn_conv_w", "gdn_a_log", "gdn_dt_bias", "gdn_norm", "w_branch_fox",
             "w_branch_gdn", "w_out", "norm_ffn", "w_up", "ffn_conv_w", "w_down", "norm_final"]
    ws = [norm_mix, w_in, fox_f_bias, gdn_conv_w, gdn_a_log, gdn_dt_bias, gdn_norm, w_branch_fox, w_branch_gdn, w_out,
          norm_ffn, w_up, ffn_conv_w, w_down, norm_final]
    ms = [m_norm_mix, m_w_in, m_fox_f_bias, m_gdn_conv_w, m_gdn_a_log, m_gdn_dt_bias, m_gdn_norm, m_w_branch_fox,
          m_w_branch_gdn, m_w_out, m_norm_ffn, m_w_up, m_ffn_conv_w, m_w_down, m_norm_final]
    vs = [v_norm_mix, v_w_in, v_fox_f_bias, v_gdn_conv_w, v_gdn_a_log, v_gdn_dt_bias, v_gdn_norm, v_w_branch_fox,
          v_w_branch_gdn, v_w_out, v_norm_ffn, v_w_up, v_ffn_conv_w, v_w_down, v_norm_final]
    gs = [g_norm_mix, g_w_in, g_fbias, g_gconv, g_alog, g_dtb, g_gdn_norm, g_bf, g_bg, g_out, g_norm_ffn, g_up,
          g_fconv, g_down, g_norm_final]
    gs = [g.reshape(w.shape) for g, w in zip(gs, ws)]
    deltas, new_ms, new_vs = [], [], []
    for nm, w, g, m, v in zip(names, ws, gs, ms, vs):
        if w.ndim == 1:
            d, a, b = adamw(w.reshape(1, -1), g.reshape(1, -1), m.reshape(1, -1), v.reshape(1, -1), "adamw_" + nm)
            d, a, b = d.reshape(w.shape), a.reshape(w.shape), b.reshape(w.shape)
        elif nm == "w_in":
            d, a, b = (r.T[None] for r in adamw(w[0].T, g_w_inT, m[0].T, v[0].T, "adamw_" + nm))
        else:
            d, a, b = adamw(w, g, m, v, "adamw_" + nm)
        deltas.append(d)
        new_ms.append(a)
        new_vs.append(b)

    return (loss, grad_x.reshape(B, S, D), *gs, *deltas, *new_ms, *new_vs)
```
